```python
import math
import jax, jax.numpy as jnp
from jax import lax
import numpy as np

D_MODEL = 2048
BATCH = 8
SEQ = 4096
DEPTH = 2

HEAD_DIM = 128
WIDTH_A = D_MODEL // 2
WIDTH_B = D_MODEL // 2
N_HEADS_A = WIDTH_A // HEAD_DIM
N_HEADS_B = WIDTH_B // HEAD_DIM
CHUNK = 128
Q_BLOCK = 128
SSM_WIDTH = D_MODEL // 2
SSM_GROUP = 16
SSM_GROUPS = SSM_WIDTH // SSM_GROUP
SSM_STATE = 64
EPS = 1e-6
DT_MIN = 1e-3
DT_MAX = 1e-1

kernel_name = "hybrid_sgu_stickbreak_s5_adaln"


def rms_norm(x, g):
    xf = x.astype(jnp.float32)
    y = xf * lax.rsqrt(jnp.mean(xf * xf, axis=-1, keepdims=True) + EPS)
    return (y * g.astype(jnp.float32)).astype(x.dtype)


def spatial_gating(u, v, norm_g, w_s, b_s):
    bsz, l, _ = v.shape
    n_chunks = l // CHUNK
    vh = v.reshape(bsz, n_chunks, CHUNK, N_HEADS_A, HEAD_DIM)
    vh = rms_norm(vh, norm_g.reshape(N_HEADS_A, HEAD_DIM))
    causal = jnp.tril(jnp.ones((CHUNK, CHUNK), dtype=bool))
    w = jnp.where(causal[None], w_s, 0.0).astype(vh.dtype)
    s = jnp.einsum('hts,bnshd->bnthd', w, vh) + b_s.T.astype(vh.dtype)[None, None, :, :, None]
    return u * s.reshape(bsz, l, WIDTH_A)


def stick_breaking(q, k, v):
    bsz, l, h, dh = q.shape
    n_blocks = l // Q_BLOCK
    qb = q.reshape(bsz, n_blocks, Q_BLOCK, h, dh).transpose(1, 0, 3, 2, 4)
    kt = k.transpose(0, 2, 1, 3)
    vt = v.transpose(0, 2, 1, 3)
    scale = 1.0 / math.sqrt(dh)
    k_pos = jnp.arange(l)

    def block(args):
        q_blk, blk = args
        q_pos = blk * Q_BLOCK + jnp.arange(Q_BLOCK)
        mask = k_pos[None, :] < q_pos[:, None]
        z = jnp.einsum('bhqd,bhkd->bhqk', q_blk, kt).astype(jnp.float32) * scale
        log_beta = jax.nn.log_sigmoid(z)
        log_keep = jnp.where(mask, jax.nn.log_sigmoid(-z), 0.0)
        later = lax.cumsum(log_keep, axis=3, reverse=True) - log_keep
        w = jnp.where(mask, jnp.exp(log_beta + later), 0.0)
        return jnp.einsum('bhqk,bhkd->bhqd', w.astype(vt.dtype), vt)

    out = lax.map(block, (qb, jnp.arange(n_blocks)))
    return out.transpose(1, 0, 3, 2, 4).reshape(bsz, l, h * dh)


def s5_ssm(u, lam_re, lam_im, b_re, b_im, c_re, c_im, d_skip, log_dt):
    f32 = jnp.float32
    bsz, l, _ = u.shape
    uf = u.astype(f32).reshape(bsz, l, SSM_GROUPS, SSM_GROUP)
    dt = jnp.exp(log_dt.astype(f32))[:, None]
    lr = lam_re.astype(f32)
    li = lam_im.astype(f32)
    mag = jnp.exp(lr * dt)
    a_re = mag * jnp.cos(li * dt)
    a_im = mag * jnp.sin(li * dt)
    den = lr * lr + li * li
    nr = a_re - 1.0
    coef_re = (nr * lr + a_im * li) / den
    coef_im = (a_im * lr - nr * li) / den
    br = b_re.astype(f32)
    bi = b_im.astype(f32)
    bb_re = coef_re[..., None] * br - coef_im[..., None] * bi
    bb_im = coef_re[..., None] * bi + coef_im[..., None] * br
    bu_re = jnp.einsum('gpc,blgc->blgp', bb_re, uf)
    bu_im = jnp.einsum('gpc,blgc->blgp', bb_im, uf)
    a_re_t = jnp.broadcast_to(a_re, (1, l) + a_re.shape)
    a_im_t = jnp.broadcast_to(a_im, (1, l) + a_im.shape)

    def combine(e1, e2):
        a1r, a1i, b1r, b1i = e1
        a2r, a2i, b2r, b2i = e2
        return (a2r * a1r - a2i * a1i,
                a2r * a1i + a2i * a1r,
                a2r * b1r - a2i * b1i + b2r,
                a2r * b1i + a2i * b1r + b2i)

    _, _, h_re, h_im = lax.associative_scan(combine, (a_re_t, a_im_t, bu_re, bu_im), axis=1)
    y = (jnp.einsum('gcp,blgp->blgc', c_re.astype(f32), h_re)
         - jnp.einsum('gcp,blgp->blgc', c_im.astype(f32), h_im))
    y = y.reshape(bsz, l, SSM_WIDTH) + d_skip.astype(f32) * u.astype(f32)
    return y.astype(u.dtype)


def ab_mixer(h, w_in, w_out, sgu_norm_g, sgu_w, sgu_b):
    bsz, l, _ = h.shape
    proj = h @ w_in
    cuts = np.cumsum([WIDTH_A, WIDTH_A, WIDTH_A, WIDTH_B, WIDTH_B, WIDTH_B]).tolist()
    a_u, a_v, a_z, q, k, v, b_z = jnp.split(proj, cuts, axis=-1)
    out_a = spatial_gating(jax.nn.gelu(a_u), jax.nn.gelu(a_v), sgu_norm_g, sgu_w, sgu_b)
    out_a = out_a * jax.nn.silu(a_z)
    shp = (bsz, l, N_HEADS_B, HEAD_DIM)
    out_b = stick_breaking(q.reshape(shp), k.reshape(shp), v.reshape(shp)) * jax.nn.silu(b_z)
    return jnp.concatenate([out_a, out_b], axis=-1) @ w_out


def ssm_mixer(h, w_in, w_out, lam_re, lam_im, b_re, b_im, c_re, c_im, d_skip, log_dt, w_glu, b_glu):
    proj = h @ w_in
    u, z = jnp.split(proj, 2, axis=-1)
    y = s5_ssm(u, lam_re, lam_im, b_re, b_im, c_re, c_im, d_skip, log_dt)
    g = jax.nn.gelu(y)
    y = g * jax.nn.sigmoid(g @ w_glu + b_glu)
    return (y * jax.nn.silu(z)) @ w_out


def _fwd_setup_inputs(seed: int = 0) -> dict:
    key = jax.random.key(seed)
    ks = jax.random.split(key, 24)
    n_even = (DEPTH + 1) // 2
    n_odd = DEPTH // 2
    d = D_MODEL
    nrm = jax.random.normal
    w_in_ab_cols = 3 * WIDTH_A + 4 * WIDTH_B
    log_dt = jax.random.uniform(ks[20], (n_odd, SSM_GROUPS), minval=math.log(DT_MIN), maxval=math.log(DT_MAX))
    n_idx = jnp.arange(SSM_STATE, dtype=jnp.float32)
    return {
        "x": nrm(ks[0], (BATCH, SEQ, d)),
        "c": nrm(ks[1], (BATCH, d)),
        "ln_pre_g": 1.0 + 0.02 * nrm(ks[2], (DEPTH, d)),
        "ln_post_g": 1.0 + 0.02 * nrm(ks[3], (DEPTH, d)),
        "w_mod": nrm(ks[4], (DEPTH, d, 3 * d)) * d ** -0.5,
        "b_mod": 0.02 * nrm(ks[5], (DEPTH, 3 * d)),
        "w_in_ab": nrm(ks[6], (n_even, d, w_in_ab_cols)) * d ** -0.5,
        "w_out_ab": nrm(ks[7], (n_even, WIDTH_A + WIDTH_B, d)) * (WIDTH_A + WIDTH_B) ** -0.5,
        "sgu_norm_g": 1.0 + 0.02 * nrm(ks[8], (n_even, WIDTH_A)),
        "sgu_w": nrm(ks[9], (n_even, N_HEADS_A, CHUNK, CHUNK)) * CHUNK ** -0.5,
        "sgu_b": 1.0 + 0.02 * nrm(ks[10], (n_even, N_HEADS_A, CHUNK)),
        "w_in_ssm": nrm(ks[11], (n_odd, d, 2 * SSM_WIDTH)) * d ** -0.5,
        "w_out_ssm": nrm(ks[12], (n_odd, SSM_WIDTH, d)) * SSM_WIDTH ** -0.5,
        "lam_re": -0.5 + 0.01 * nrm(ks[13], (n_odd, SSM_GROUPS, SSM_STATE)),
        "lam_im": math.pi * n_idx + 0.01 * nrm(ks[14], (n_odd, SSM_GROUPS, SSM_STATE)),
        "b_re": nrm(ks[15], (n_odd, SSM_GROUPS, SSM_STATE, SSM_GROUP)) * (2 * SSM_GROUP) ** -0.5,
        "b_im": nrm(ks[16], (n_odd, SSM_GROUPS, SSM_STATE, SSM_GROUP)) * (2 * SSM_GROUP) ** -0.5,
        "c_re": nrm(ks[17], (n_odd, SSM_GROUPS, SSM_GROUP, SSM_STATE)) * (2 * SSM_STATE) ** -0.5,
        "c_im": nrm(ks[18], (n_odd, SSM_GROUPS, SSM_GROUP, SSM_STATE)) * (2 * SSM_STATE) ** -0.5,
        "d_skip": nrm(ks[19], (n_odd, SSM_WIDTH)),
        "log_dt": log_dt,
        "w_glu": nrm(ks[21], (n_odd, SSM_WIDTH, SSM_WIDTH)) * SSM_WIDTH ** -0.5,
        "b_glu": 0.02 * nrm(ks[22], (n_odd, SSM_WIDTH)),
    }


def _fwd_reference(x, c, ln_pre_g, ln_post_g, w_mod, b_mod, w_in_ab, w_out_ab, sgu_norm_g, sgu_w, sgu_b,
              w_in_ssm, w_out_ssm, lam_re, lam_im, b_re, b_im, c_re, c_im, d_skip, log_dt, w_glu, b_glu):
    cond = jax.nn.silu(c)
    for layer in range(DEPTH):
        mod = cond @ w_mod[layer] + b_mod[layer]
        shift, scale, gate = jnp.split(mod[:, None, :], 3, axis=-1)
        h = rms_norm(x, ln_pre_g[layer]) * (1.0 + scale) + shift
        i = layer // 2
        if layer % 2 == 0:
            y = ab_mixer(h, w_in_ab[i], w_out_ab[i], sgu_norm_g[i], sgu_w[i], sgu_b[i])
        else:
            y = ssm_mixer(h, w_in_ssm[i], w_out_ssm[i], lam_re[i], lam_im[i], b_re[i], b_im[i],
                          c_re[i], c_im[i], d_skip[i], log_dt[i], w_glu[i], b_glu[i])
        x = x + (gate * rms_norm(y, ln_post_g[layer])).astype(x.dtype)
    return x


import jax as _jax
import jax.numpy as _jnp

TWIN_FORMAT = 'train_step'
FWD_PARAMS = ['x', 'c', 'ln_pre_g', 'ln_post_g', 'w_mod', 'b_mod', 'w_in_ab', 'w_out_ab', 'sgu_norm_g', 'sgu_w', 'sgu_b', 'w_in_ssm', 'w_out_ssm', 'lam_re', 'lam_im', 'b_re', 'b_im', 'c_re', 'c_im', 'd_skip', 'log_dt', 'w_glu', 'b_glu']
TWIN_WEIGHTS = ['ln_pre_g', 'ln_post_g', 'w_mod', 'b_mod', 'w_in_ab', 'w_out_ab', 'sgu_norm_g', 'sgu_w', 'sgu_b', 'w_in_ssm', 'w_out_ssm', 'lam_re', 'lam_im', 'b_re', 'b_im', 'c_re', 'c_im', 'd_skip', 'log_dt', 'w_glu', 'b_glu']
TWIN_DIFF_INPUT = 'x'
TWIN_INPUTS = ['x', 'c', 'ln_pre_g', 'ln_post_g', 'w_mod', 'b_mod', 'w_in_ab', 'w_out_ab', 'sgu_norm_g', 'sgu_w', 'sgu_b', 'w_in_ssm', 'w_out_ssm', 'lam_re', 'lam_im', 'b_re', 'b_im', 'c_re', 'c_im', 'd_skip', 'log_dt', 'w_glu', 'b_glu', 'loss_target', 'm_ln_pre_g', 'm_ln_post_g', 'm_w_mod', 'm_b_mod', 'm_w_in_ab', 'm_w_out_ab', 'm_sgu_norm_g', 'm_sgu_w', 'm_sgu_b', 'm_w_in_ssm', 'm_w_out_ssm', 'm_lam_re', 'm_lam_im', 'm_b_re', 'm_b_im', 'm_c_re', 'm_c_im', 'm_d_skip', 'm_log_dt', 'm_w_glu', 'm_b_glu', 'v_ln_pre_g', 'v_ln_post_g', 'v_w_mod', 'v_b_mod', 'v_w_in_ab', 'v_w_out_ab', 'v_sgu_norm_g', 'v_sgu_w', 'v_sgu_b', 'v_w_in_ssm', 'v_w_out_ssm', 'v_lam_re', 'v_lam_im', 'v_b_re', 'v_b_im', 'v_c_re', 'v_c_im', 'v_d_skip', 'v_log_dt', 'v_w_glu', 'v_b_glu']
TWIN_OUTPUTS = ['loss', 'grad_x', 'grad_ln_pre_g', 'grad_ln_post_g', 'grad_w_mod', 'grad_b_mod', 'grad_w_in_ab', 'grad_w_out_ab', 'grad_sgu_norm_g', 'grad_sgu_w', 'grad_sgu_b', 'grad_w_in_ssm', 'grad_w_out_ssm', 'grad_lam_re', 'grad_lam_im', 'grad_b_re', 'grad_b_im', 'grad_c_re', 'grad_c_im', 'grad_d_skip', 'grad_log_dt', 'grad_w_glu', 'grad_b_glu', 'delta_ln_pre_g', 'delta_ln_post_g', 'delta_w_mod', 'delta_b_mod', 'delta_w_in_ab', 'delta_w_out_ab', 'delta_sgu_norm_g', 'delta_sgu_w', 'delta_sgu_b', 'delta_w_in_ssm', 'delta_w_out_ssm', 'delta_lam_re', 'delta_lam_im', 'delta_b_re', 'delta_b_im', 'delta_c_re', 'delta_c_im', 'delta_d_skip', 'delta_log_dt', 'delta_w_glu', 'delta_b_glu', 'new_m_ln_pre_g', 'new_m_ln_post_g', 'new_m_w_mod', 'new_m_b_mod', 'new_m_w_in_ab', 'new_m_w_out_ab', 'new_m_sgu_norm_g', 'new_m_sgu_w', 'new_m_sgu_b', 'new_m_w_in_ssm', 'new_m_w_out_ssm', 'new_m_lam_re', 'new_m_lam_im', 'new_m_b_re', 'new_m_b_im', 'new_m_c_re', 'new_m_c_im', 'new_m_d_skip', 'new_m_log_dt', 'new_m_w_glu', 'new_m_b_glu', 'new_v_ln_pre_g', 'new_v_ln_post_g', 'new_v_w_mod', 'new_v_b_mod', 'new_v_w_in_ab', 'new_v_w_out_ab', 'new_v_sgu_norm_g', 'new_v_sgu_w', 'new_v_sgu_b', 'new_v_w_in_ssm', 'new_v_w_out_ssm', 'new_v_lam_re', 'new_v_lam_im', 'new_v_b_re', 'new_v_b_im', 'new_v_c_re', 'new_v_c_im', 'new_v_d_skip', 'new_v_log_dt', 'new_v_w_glu', 'new_v_b_glu']
TWIN_LEAF_KINDS = {'loss': 'loss', 'grad_x': 'grad_x', 'grad_ln_pre_g': 'grad_w', 'grad_ln_post_g': 'grad_w', 'grad_w_mod': 'grad_w', 'grad_b_mod': 'grad_w', 'grad_w_in_ab': 'grad_w', 'grad_w_out_ab': 'grad_w', 'grad_sgu_norm_g': 'grad_w', 'grad_sgu_w': 'grad_w', 'grad_sgu_b': 'grad_w', 'grad_w_in_ssm': 'grad_w', 'grad_w_out_ssm': 'grad_w', 'grad_lam_re': 'grad_w', 'grad_lam_im': 'grad_w', 'grad_b_re': 'grad_w', 'grad_b_im': 'grad_w', 'grad_c_re': 'grad_w', 'grad_c_im': 'grad_w', 'grad_d_skip': 'grad_w', 'grad_log_dt': 'grad_w', 'grad_w_glu': 'grad_w', 'grad_b_glu': 'grad_w', 'delta_ln_pre_g': 'delta_w', 'delta_ln_post_g': 'delta_w', 'delta_w_mod': 'delta_w', 'delta_b_mod': 'delta_w', 'delta_w_in_ab': 'delta_w', 'delta_w_out_ab': 'delta_w', 'delta_sgu_norm_g': 'delta_w', 'delta_sgu_w': 'delta_w', 'delta_sgu_b': 'delta_w', 'delta_w_in_ssm': 'delta_w', 'delta_w_out_ssm': 'delta_w', 'delta_lam_re': 'delta_w', 'delta_lam_im': 'delta_w', 'delta_b_re': 'delta_w', 'delta_b_im': 'delta_w', 'delta_c_re': 'delta_w', 'delta_c_im': 'delta_w', 'delta_d_skip': 'delta_w', 'delta_log_dt': 'delta_w', 'delta_w_glu': 'delta_w', 'delta_b_glu': 'delta_w', 'new_m_ln_pre_g': 'new_m', 'new_m_ln_post_g': 'new_m', 'new_m_w_mod': 'new_m', 'new_m_b_mod': 'new_m', 'new_m_w_in_ab': 'new_m', 'new_m_w_out_ab': 'new_m', 'new_m_sgu_norm_g': 'new_m', 'new_m_sgu_w': 'new_m', 'new_m_sgu_b': 'new_m', 'new_m_w_in_ssm': 'new_m', 'new_m_w_out_ssm': 'new_m', 'new_m_lam_re': 'new_m', 'new_m_lam_im': 'new_m', 'new_m_b_re': 'new_m', 'new_m_b_im': 'new_m', 'new_m_c_re': 'new_m', 'new_m_c_im': 'new_m', 'new_m_d_skip': 'new_m', 'new_m_log_dt': 'new_m', 'new_m_w_glu': 'new_m', 'new_m_b_glu': 'new_m', 'new_v_ln_pre_g': 'new_v', 'new_v_ln_post_g': 'new_v', 'new_v_w_mod': 'new_v', 'new_v_b_mod': 'new_v', 'new_v_w_in_ab': 'new_v', 'new_v_w_out_ab': 'new_v', 'new_v_sgu_norm_g': 'new_v', 'new_v_sgu_w': 'new_v', 'new_v_sgu_b': 'new_v', 'new_v_w_in_ssm': 'new_v', 'new_v_w_out_ssm': 'new_v', 'new_v_lam_re': 'new_v', 'new_v_lam_im': 'new_v', 'new_v_b_re': 'new_v', 'new_v_b_im': 'new_v', 'new_v_c_re': 'new_v', 'new_v_c_im': 'new_v', 'new_v_d_skip': 'new_v', 'new_v_log_dt': 'new_v', 'new_v_w_glu': 'new_v', 'new_v_b_glu': 'new_v'}


def _forward(args):
    return _fwd_reference(*[args[k] for k in FWD_PARAMS])


def _output_shape():
    def fwd():
        inp = _fwd_setup_inputs(0)
        return _fwd_reference(*[inp[k] for k in FWD_PARAMS])
    out = _jax.eval_shape(fwd)
    return out.shape, out.dtype

N_MICROBATCH = 1
ADAM_LR = 0.001
ADAM_B1 = 0.9
ADAM_B2 = 0.999
ADAM_EPS = 1e-08
ADAM_WD = 0.01
ADAM_STEP = 10
PER_EXAMPLE_BATCH_AXIS = {'x': 0, 'c': 0, 'loss_target': 0}
SHARED_INPUTS = []
_WEIGHT_DTYPES = {'ln_pre_g': _jnp.float32, 'ln_post_g': _jnp.float32, 'w_mod': _jnp.float32, 'b_mod': _jnp.float32, 'w_in_ab': _jnp.float32, 'w_out_ab': _jnp.float32, 'sgu_norm_g': _jnp.float32, 'sgu_w': _jnp.float32, 'sgu_b': _jnp.float32, 'w_in_ssm': _jnp.float32, 'w_out_ssm': _jnp.float32, 'lam_re': _jnp.float32, 'lam_im': _jnp.float32, 'b_re': _jnp.float32, 'b_im': _jnp.float32, 'c_re': _jnp.float32, 'c_im': _jnp.float32, 'd_skip': _jnp.float32, 'log_dt': _jnp.float32, 'w_glu': _jnp.float32, 'b_glu': _jnp.float32}
MOMENT_SCALE = {'ln_pre_g': 1.725285e-01, 'ln_post_g': 6.634693e+00, 'w_mod': 1.255365e+00, 'b_mod': 2.645445e+00, 'w_in_ab': 1.947211e-01, 'w_out_ab': 3.281325e-01, 'sgu_norm_g': 7.779274e-02, 'sgu_w': 6.795647e-02, 'sgu_b': 9.801856e-02, 'w_in_ssm': 3.696513e-01, 'w_out_ssm': 4.200578e-01, 'lam_re': 3.661087e-02, 'lam_im': 4.000199e-02, 'b_re': 2.745007e-02, 'b_im': 3.105615e-02, 'c_re': 5.413131e-02, 'c_im': 5.886685e-02, 'd_skip': 6.265727e-01, 'log_dt': 4.227585e+00, 'w_glu': 1.397315e-01, 'b_glu': 2.654508e-01}


def _to_microbatches(a, axis):
    t = _jnp.moveaxis(a, axis, 0)
    t = t.reshape((N_MICROBATCH, t.shape[0] // N_MICROBATCH) + t.shape[1:])
    return _jnp.moveaxis(t, 1, axis + 1)


def setup_inputs(seed: int = 0) -> dict:
    inp = _fwd_setup_inputs(seed)
    key = _jax.random.fold_in(_jax.random.key(seed), 7919)
    shape, _ = _output_shape()
    out = dict(inp)
    out["loss_target"] = _jax.random.normal(_jax.random.fold_in(key, 0), shape, _jnp.float32)
    for i, name in enumerate(TWIN_WEIGHTS):
        w = inp[name].astype(_jnp.float32)
        if MOMENT_SCALE is None:
            s = _jnp.sqrt(_jnp.mean(_jnp.square(w)) + 1e-30)
        else:
            s = MOMENT_SCALE[name]
        km, kv = _jax.random.split(_jax.random.fold_in(key, i + 1))
        out[name] = w
        out["m_" + name] = s * _jax.random.normal(km, w.shape, _jnp.float32)
        out["v_" + name] = (s * s) * _jax.random.uniform(kv, w.shape, _jnp.float32, 0.5, 1.5)
    if N_MICROBATCH > 1:
        for name, axis in PER_EXAMPLE_BATCH_AXIS.items():
            out[name] = _to_microbatches(out[name], axis)
    return {'x': out['x'], 'c': out['c'], 'ln_pre_g': out['ln_pre_g'], 'ln_post_g': out['ln_post_g'], 'w_mod': out['w_mod'], 'b_mod': out['b_mod'], 'w_in_ab': out['w_in_ab'], 'w_out_ab': out['w_out_ab'], 'sgu_norm_g': out['sgu_norm_g'], 'sgu_w': out['sgu_w'], 'sgu_b': out['sgu_b'], 'w_in_ssm': out['w_in_ssm'], 'w_out_ssm': out['w_out_ssm'], 'lam_re': out['lam_re'], 'lam_im': out['lam_im'], 'b_re': out['b_re'], 'b_im': out['b_im'], 'c_re': out['c_re'], 'c_im': out['c_im'], 'd_skip': out['d_skip'], 'log_dt': out['log_dt'], 'w_glu': out['w_glu'], 'b_glu': out['b_glu'], 'loss_target': out['loss_target'], 'm_ln_pre_g': out['m_ln_pre_g'], 'm_ln_post_g': out['m_ln_post_g'], 'm_w_mod': out['m_w_mod'], 'm_b_mod': out['m_b_mod'], 'm_w_in_ab': out['m_w_in_ab'], 'm_w_out_ab': out['m_w_out_ab'], 'm_sgu_norm_g': out['m_sgu_norm_g'], 'm_sgu_w': out['m_sgu_w'], 'm_sgu_b': out['m_sgu_b'], 'm_w_in_ssm': out['m_w_in_ssm'], 'm_w_out_ssm': out['m_w_out_ssm'], 'm_lam_re': out['m_lam_re'], 'm_lam_im': out['m_lam_im'], 'm_b_re': out['m_b_re'], 'm_b_im': out['m_b_im'], 'm_c_re': out['m_c_re'], 'm_c_im': out['m_c_im'], 'm_d_skip': out['m_d_skip'], 'm_log_dt': out['m_log_dt'], 'm_w_glu': out['m_w_glu'], 'm_b_glu': out['m_b_glu'], 'v_ln_pre_g': out['v_ln_pre_g'], 'v_ln_post_g': out['v_ln_post_g'], 'v_w_mod': out['v_w_mod'], 'v_b_mod': out['v_b_mod'], 'v_w_in_ab': out['v_w_in_ab'], 'v_w_out_ab': out['v_w_out_ab'], 'v_sgu_norm_g': out['v_sgu_norm_g'], 'v_sgu_w': out['v_sgu_w'], 'v_sgu_b': out['v_sgu_b'], 'v_w_in_ssm': out['v_w_in_ssm'], 'v_w_out_ssm': out['v_w_out_ssm'], 'v_lam_re': out['v_lam_re'], 'v_lam_im': out['v_lam_im'], 'v_b_re': out['v_b_re'], 'v_b_im': out['v_b_im'], 'v_c_re': out['v_c_re'], 'v_c_im': out['v_c_im'], 'v_d_skip': out['v_d_skip'], 'v_log_dt': out['v_log_dt'], 'v_w_glu': out['v_w_glu'], 'v_b_glu': out['v_b_glu']}


def _loss(weights, diff, rest, loss_target):
    with _jax.named_scope("forward"):
        args = {**rest, TWIN_DIFF_INPUT: diff, **{k: w.astype(_WEIGHT_DTYPES[k]) for k, w in weights.items()}}
        y = _forward(args)
    with _jax.named_scope("loss_head"):
        err = _jnp.square(y.astype(_jnp.float32) - loss_target)
        return 0.5 * _jnp.sum(_jnp.mean(err, axis=-1)) if err.ndim else 0.5 * err


def _adamw(w, g, m, v):
    m = ADAM_B1 * m + (1.0 - ADAM_B1) * g
    v = ADAM_B2 * v + (1.0 - ADAM_B2) * _jnp.square(g)
    m_hat = m / (1.0 - ADAM_B1 ** ADAM_STEP)
    v_hat = v / (1.0 - ADAM_B2 ** ADAM_STEP)
    delta = -ADAM_LR * (m_hat / (_jnp.sqrt(v_hat) + ADAM_EPS) + ADAM_WD * w)
    return delta, m, v


def reference(x, c, ln_pre_g, ln_post_g, w_mod, b_mod, w_in_ab, w_out_ab, sgu_norm_g, sgu_w, sgu_b, w_in_ssm, w_out_ssm, lam_re, lam_im, b_re, b_im, c_re, c_im, d_skip, log_dt, w_glu, b_glu, loss_target, m_ln_pre_g, m_ln_post_g, m_w_mod, m_b_mod, m_w_in_ab, m_w_out_ab, m_sgu_norm_g, m_sgu_w, m_sgu_b, m_w_in_ssm, m_w_out_ssm, m_lam_re, m_lam_im, m_b_re, m_b_im, m_c_re, m_c_im, m_d_skip, m_log_dt, m_w_glu, m_b_glu, v_ln_pre_g, v_ln_post_g, v_w_mod, v_b_mod, v_w_in_ab, v_w_out_ab, v_sgu_norm_g, v_sgu_w, v_sgu_b, v_w_in_ssm, v_w_out_ssm, v_lam_re, v_lam_im, v_b_re, v_b_im, v_c_re, v_c_im, v_d_skip, v_log_dt, v_w_glu, v_b_glu):
    given = dict(x=x, c=c, ln_pre_g=ln_pre_g, ln_post_g=ln_post_g, w_mod=w_mod, b_mod=b_mod, w_in_ab=w_in_ab, w_out_ab=w_out_ab, sgu_norm_g=sgu_norm_g, sgu_w=sgu_w, sgu_b=sgu_b, w_in_ssm=w_in_ssm, w_out_ssm=w_out_ssm, lam_re=lam_re, lam_im=lam_im, b_re=b_re, b_im=b_im, c_re=c_re, c_im=c_im, d_skip=d_skip, log_dt=log_dt, w_glu=w_glu, b_glu=b_glu, loss_target=loss_target, m_ln_pre_g=m_ln_pre_g, m_ln_post_g=m_ln_post_g, m_w_mod=m_w_mod, m_b_mod=m_b_mod, m_w_in_ab=m_w_in_ab, m_w_out_ab=m_w_out_ab, m_sgu_norm_g=m_sgu_norm_g, m_sgu_w=m_sgu_w, m_sgu_b=m_sgu_b, m_w_in_ssm=m_w_in_ssm, m_w_out_ssm=m_w_out_ssm, m_lam_re=m_lam_re, m_lam_im=m_lam_im, m_b_re=m_b_re, m_b_im=m_b_im, m_c_re=m_c_re, m_c_im=m_c_im, m_d_skip=m_d_skip, m_log_dt=m_log_dt, m_w_glu=m_w_glu, m_b_glu=m_b_glu, v_ln_pre_g=v_ln_pre_g, v_ln_post_g=v_ln_post_g, v_w_mod=v_w_mod, v_b_mod=v_b_mod, v_w_in_ab=v_w_in_ab, v_w_out_ab=v_w_out_ab, v_sgu_norm_g=v_sgu_norm_g, v_sgu_w=v_sgu_w, v_sgu_b=v_sgu_b, v_w_in_ssm=v_w_in_ssm, v_w_out_ssm=v_w_out_ssm, v_lam_re=v_lam_re, v_lam_im=v_lam_im, v_b_re=v_b_re, v_b_im=v_b_im, v_c_re=v_c_re, v_c_im=v_c_im, v_d_skip=v_d_skip, v_log_dt=v_log_dt, v_w_glu=v_w_glu, v_b_glu=v_b_glu)
    weights = {n: given[n] for n in TWIN_WEIGHTS}
    shared = {n: given[n] for n in SHARED_INPUTS}
    per_example = {n: given[n] for n in ['x', 'c']}
    grad_fn = _jax.value_and_grad(_loss, argnums=(0, 1))

    def one_microbatch(ex, loss_target):
        ex = dict(ex)
        diff = ex.pop(TWIN_DIFF_INPUT)
        return grad_fn(weights, diff, {**shared, **ex}, loss_target)

    if N_MICROBATCH == 1:
        loss, (grad_w, grad_x) = one_microbatch(per_example, given["loss_target"])
    else:
        def body(carry, xs):
            loss_sum, grad_sum = carry
            l_k, (gw_k, gx_k) = one_microbatch(xs[0], xs[1])
            with _jax.named_scope("update"):
                return (loss_sum + l_k, _jax.tree.map(_jnp.add, grad_sum, gw_k)), gx_k

        init = (_jnp.zeros((), _jnp.float32), _jax.tree.map(_jnp.zeros_like, weights))
        (loss, grad_w), grad_x = _jax.lax.scan(body, init, (per_example, given["loss_target"]))
    with _jax.named_scope("update"):
        delta_w, new_m, new_v = {}, {}, {}
        for n in TWIN_WEIGHTS:
            delta_w[n], new_m[n], new_v[n] = _adamw(weights[n], grad_w[n], given["m_" + n], given["v_" + n])
    return (loss, grad_x, *[grad_w[n] for n in TWIN_WEIGHTS], *[delta_w[n] for n in TWIN_WEIGHTS],
            *[new_m[n] for n in TWIN_WEIGHTS], *[new_v[n] for n in TWIN_WEIGHTS])
```

```python
import functools
import math

import jax
import jax.numpy as jnp
from jax import lax
from jax.experimental import pallas as pl
from jax.experimental.pallas import tpu as pltpu

F32 = jnp.float32
BF16 = jnp.bfloat16
MESH = pl.DeviceIdType.MESH

EPS = 1e-6
HEAD = 128
SSM_T = 128
SSM_GB = 16
ADAM_LR, ADAM_B1, ADAM_B2, ADAM_EPS, ADAM_WD, ADAM_STEP = 0.001, 0.9, 0.999, 1e-08, 0.01, 10
VMEM_LIMIT = 56 * 1024 * 1024

NN = (((1,), (0,)), ((), ()))
NT = (((1,), (1,)), ((), ()))
TN = (((0,), (0,)), ((), ()))


def _params(sem=None):
    return pltpu.CompilerParams(dimension_semantics=sem, vmem_limit_bytes=VMEM_LIMIT)


def _dot(a, b, dims=NN):
    return lax.dot_general(a, b, dims, preferred_element_type=F32)


def _bf(x):
    return x.astype(BF16)


def _split_dot(x, m):
    hi = _bf(x)
    lo = _bf(x - hi.astype(F32))
    return _dot(hi, m) + _dot(lo, m)


def _gelu(x):
    k = math.sqrt(2.0 / math.pi)
    t = jnp.tanh(k * (x + 0.044715 * x * x * x))
    return 0.5 * x * (1.0 + t)


def _gelu_grad(x):
    k = math.sqrt(2.0 / math.pi)
    x2 = x * x
    t = jnp.tanh(k * (x + 0.044715 * x * x2))
    return 0.5 * (1.0 + t) + 0.5 * x * (1.0 - t * t) * k * (1.0 + 3.0 * 0.044715 * x2)


def _sigmoid(x):
    return 1.0 / (1.0 + jnp.exp(-x))


def _silu(x):
    return x * _sigmoid(x)


def _silu_grad(x):
    s = _sigmoid(x)
    return s * (1.0 + x * (1.0 - s))


def _tile(n, t, mult=128):
    if n <= t:
        return n
    for cand in range(t - t % mult, 0, -mult):
        if n % cand == 0:
            return cand
    raise ValueError((n, t, mult))


def _matmul(a, b, mode, out_dtype, name, tm=512, tn=512, tk=2048, n_split=1):
    if mode == "nn":
        (m, kk), (_, n) = a.shape, b.shape
    elif mode == "nt":
        (m, kk), (n, _) = a.shape, b.shape
    else:
        (kk, m), (_, n) = a.shape, b.shape
    tm, tk = _tile(m, tm), _tile(kk, tk)
    ns = n // n_split
    tn = _tile(ns, tn)
    nk = kk // tk
    dims = {"nn": NN, "nt": NT, "tn": TN}[mode]

    def body(a_ref, b_ref, o_ref, acc_ref):
        k = pl.program_id(2)
        part = _dot(_bf(a_ref[...]), _bf(b_ref[...]), dims)

        @pl.when(k == 0)
        def _():
            acc_ref[...] = part

        @pl.when(k > 0)
        def _():
            acc_ref[...] += part

        @pl.when(k == nk - 1)
        def _():
            o_ref[...] = acc_ref[...].astype(out_dtype).reshape(o_ref.shape)

    if mode == "nn":
        a_spec = pl.BlockSpec((tm, tk), lambda i, j, k: (i, k))
        b_spec = pl.BlockSpec((tk, tn), lambda i, j, k: (k, j))
    elif mode == "nt":
        a_spec = pl.BlockSpec((tm, tk), lambda i, j, k: (i, k))
        b_spec = pl.BlockSpec((tn, tk), lambda i, j, k: (j, k))
    else:
        a_spec = pl.BlockSpec((tk, tm), lambda i, j, k: (k, i))
        b_spec = pl.BlockSpec((tk, tn), lambda i, j, k: (k, j))
    if n_split == 1:
        out_shape = jax.ShapeDtypeStruct((m, n), out_dtype)
        o_spec = pl.BlockSpec((tm, tn), lambda i, j, k: (i, j))
    else:
        per = ns // tn
        out_shape = jax.ShapeDtypeStruct((n_split, m, ns), out_dtype)
        o_spec = pl.BlockSpec((1, tm, tn), lambda i, j, k: (j // per, i, j % per))
    return pl.pallas_call(
        body, name=name, grid=(m // tm, n // tn, nk),
        in_specs=[a_spec, b_spec], out_specs=o_spec, out_shape=out_shape,
        scratch_shapes=[pltpu.VMEM((tm, tn), F32)],
        compiler_params=_params(("parallel", "parallel", "arbitrary")),
    )(a, b)


def _row_spec(tm, d):
    return pl.BlockSpec((tm, d), lambda i: (i, 0))


def _vec_spec(d):
    return pl.BlockSpec((1, d), lambda i: (0, 0))


def _acc(ref, first, val):
    @pl.when(first)
    def _():
        ref[...] = val

    @pl.when(jnp.logical_not(first))
    def _():
        ref[...] += val


def _colsum(x):
    return jnp.sum(x, axis=0, keepdims=True)


def _rownorm(x):
    r = lax.rsqrt(jnp.mean(x * x, axis=-1, keepdims=True) + EPS)
    return x * r, r


def _pre_fwd(x, g, scale, shift, name):
    l, d = x.shape
    tm = _tile(l, 256)

    def body(x_ref, g_ref, sc_ref, sh_ref, h_ref):
        n, _ = _rownorm(x_ref[...])
        h_ref[...] = _bf(n * g_ref[...] * (1.0 + sc_ref[...]) + sh_ref[...])

    return pl.pallas_call(
        body, name=name, grid=(l // tm,),
        in_specs=[_row_spec(tm, d), _vec_spec(d), _vec_spec(d), _vec_spec(d)],
        out_specs=_row_spec(tm, d), out_shape=jax.ShapeDtypeStruct((l, d), BF16),
        compiler_params=_params(("parallel",)),
    )(x, g, scale, shift)


def _post_pre_fwd(x, y, gate, pg, g1, scale1, shift1, name):
    l, d = x.shape
    tm = _tile(l, 256)

    def body(x_ref, y_ref, gate_ref, pg_ref, g1_ref, sc_ref, sh_ref, x1_ref, h1_ref):
        ny, _ = _rownorm(y_ref[...])
        x1 = x_ref[...] + gate_ref[...] * (ny * pg_ref[...])
        x1_ref[...] = x1
        n1, _ = _rownorm(x1)
        h1_ref[...] = _bf(n1 * g1_ref[...] * (1.0 + sc_ref[...]) + sh_ref[...])

    v = _vec_spec(d)
    return pl.pallas_call(
        body, name=name, grid=(l // tm,),
        in_specs=[_row_spec(tm, d), _row_spec(tm, d), v, v, v, v, v],
        out_specs=[_row_spec(tm, d), _row_spec(tm, d)],
        out_shape=[jax.ShapeDtypeStruct((l, d), F32), jax.ShapeDtypeStruct((l, d), BF16)],
        compiler_params=_params(("parallel",)),
    )(x, y, gate, pg, g1, scale1, shift1)


def _post_loss(x1, y1, gate, pg, target, name):
    l, d = x1.shape
    tm = _tile(l, 256)

    def body(x_ref, y_ref, gate_ref, pg_ref, t_ref, loss_ref, dy_ref, dx_ref, dgate_ref, dpg_ref):
        first = pl.program_id(0) == 0
        y = y_ref[...]
        ny, ry = _rownorm(y)
        q = ny * pg_ref[...]
        x2 = x_ref[...] + gate_ref[...] * q
        e = x2 - t_ref[...]
        _acc(loss_ref, first, jnp.full((1, 128), 0.5 / d, F32) * jnp.sum(e * e))
        dx2 = e * (1.0 / d)
        dx_ref[...] = dx2
        _acc(dgate_ref, first, _colsum(dx2 * q))
        dq = dx2 * gate_ref[...]
        _acc(dpg_ref, first, _colsum(dq * ny))
        dny = dq * pg_ref[...]
        dy = ry * (dny - ny * jnp.mean(dny * ny, axis=-1, keepdims=True))
        dy_ref[...] = _bf(dy)

    v = _vec_spec(d)
    return pl.pallas_call(
        body, name=name, grid=(l // tm,),
        in_specs=[_row_spec(tm, d), _row_spec(tm, d), v, v, _row_spec(tm, d)],
        out_specs=[_vec_spec(128), _row_spec(tm, d), _row_spec(tm, d), v, v],
        out_shape=[jax.ShapeDtypeStruct((1, 128), F32), jax.ShapeDtypeStruct((l, d), BF16),
                   jax.ShapeDtypeStruct((l, d), F32), jax.ShapeDtypeStruct((1, d), F32),
                   jax.ShapeDtypeStruct((1, d), F32)],
        compiler_params=_params(("arbitrary",)),
    )(x1, y1, gate, pg, target)


def _pre_bwd(dh, dres, x, g, scale, name, post=None):
    l, d = x.shape
    tm = _tile(l, 256)
    with_post = post is not None

    def body(*refs):
        if with_post:
            (dh_ref, dres_ref, x_ref, g_ref, sc_ref, y_ref, gate_ref, pg_ref,
             dx_ref, dsc_ref, dsh_ref, dg_ref, dy_ref, dgate_ref, dpg_ref) = refs
        else:
            dh_ref, dres_ref, x_ref, g_ref, sc_ref, dx_ref, dsc_ref, dsh_ref, dg_ref = refs
        first = pl.program_id(0) == 0
        dh = dh_ref[...]
        n, r = _rownorm(x_ref[...])
        _acc(dsc_ref, first, _colsum(dh * (n * g_ref[...])))
        _acc(dsh_ref, first, _colsum(dh))
        dyn = dh * (1.0 + sc_ref[...])
        _acc(dg_ref, first, _colsum(dyn * n))
        dn = dyn * g_ref[...]
        dx = dres_ref[...] + r * (dn - n * jnp.mean(dn * n, axis=-1, keepdims=True))
        dx_ref[...] = dx
        if with_post:
            ny, ry = _rownorm(y_ref[...])
            _acc(dgate_ref, first, _colsum(dx * (ny * pg_ref[...])))
            dq = dx * gate_ref[...]
            _acc(dpg_ref, first, _colsum(dq * ny))
            dny = dq * pg_ref[...]
            dy_ref[...] = _bf(ry * (dny - ny * jnp.mean(dny * ny, axis=-1, keepdims=True)))

    v = _vec_spec(d)
    row = _row_spec(tm, d)
    vec_out = jax.ShapeDtypeStruct((1, d), F32)
    in_specs = [row, row, row, v, v]
    args = [dh, dres, x, g, scale]
    out_specs = [row, v, v, v]
    out_shape = [jax.ShapeDtypeStruct((l, d), F32), vec_out, vec_out, vec_out]
    if with_post:
        in_specs += [row, v, v]
        args += list(post)
        out_specs += [row, v, v]
        out_shape += [jax.ShapeDtypeStruct((l, d), BF16), vec_out, vec_out]
    return pl.pallas_call(
        body, name=name, grid=(l // tm,), in_specs=in_specs, out_specs=out_specs, out_shape=out_shape,
        compiler_params=_params(("arbitrary",)),
    )(*args)


def _softplus_parts(z):
    e = jnp.exp(-jnp.abs(z))
    den = 1.0 + e
    lb = jnp.minimum(z, 0.0) - jnp.log(den)
    sig = jnp.where(z >= 0.0, 1.0, e) / den
    return lb, lb - z, sig


def _tri(cmp):
    row = lax.broadcasted_iota(jnp.int32, (HEAD, HEAD), 0)
    col = lax.broadcasted_iota(jnp.int32, (HEAD, HEAD), 1)
    return cmp(row, col)


def _attn_fwd(proj, wa, wb, name):
    l = proj.shape[0]
    nh, nq = wb // HEAD, l // HEAD
    q0, k0, v0 = (3 * wa) // HEAD, (3 * wa + wb) // HEAD, (3 * wa + 2 * wb) // HEAD
    scale = 1.0 / math.sqrt(HEAD)

    def body(q_ref, k_ref, v_ref, o_ref, lk_ref):
        i = pl.program_id(1)
        qb = _bf(q_ref[...])
        valid = _tri(lambda r, c: c < r)
        m_gt = _bf(_tri(lambda r, c: r > c).astype(F32))

        def tile(j, acc, run, diag):
            rows = pl.ds(pl.multiple_of(j * HEAD, HEAD), HEAD)
            kb, vb = _bf(k_ref[rows, :]), _bf(v_ref[rows, :])
            z = _dot(qb, kb, NT) * scale
            lb, lk, _ = _softplus_parts(z)
            if diag:
                lk = jnp.where(valid, lk, 0.0)
            w = jnp.exp(lb + _split_dot(lk, m_gt) + run)
            if diag:
                w = jnp.where(valid, w, 0.0)
            return acc + _dot(_bf(w), vb), run + jnp.sum(lk, axis=1, keepdims=True)

        acc, run = tile(i, jnp.zeros((HEAD, HEAD), F32), jnp.zeros((HEAD, 1), F32), True)

        def step(t, carry):
            return tile(i - 1 - t, carry[0], carry[1], False)

        acc, run = lax.fori_loop(0, i, step, (acc, run))
        o_ref[...] = acc
        lk_ref[...] = jnp.broadcast_to(run, (HEAD, HEAD))

    blk = lambda off: pl.BlockSpec((HEAD, HEAD), lambda h, i: (i, off + h))
    full = lambda off: pl.BlockSpec((l, HEAD), lambda h, i: (0, off + h))
    out = pl.BlockSpec((HEAD, HEAD), lambda h, i: (i, h))
    return pl.pallas_call(
        body, name=name, grid=(nh, nq),
        in_specs=[blk(q0), full(k0), full(v0)], out_specs=[out, out],
        out_shape=[jax.ShapeDtypeStruct((l, wb), F32), jax.ShapeDtypeStruct((l, wb), F32)],
        compiler_params=_params(("parallel", "arbitrary")),
    )(proj, proj, proj)


def _attn_bwd(proj, dcat, lktot, wa, wb, name):
    l = proj.shape[0]
    nh, nq = wb // HEAD, l // HEAD
    q0, k0, v0 = (3 * wa) // HEAD, (3 * wa + wb) // HEAD, (3 * wa + 2 * wb) // HEAD
    bz0, dc0 = (3 * wa + 3 * wb) // HEAD, wa // HEAD
    scale = 1.0 / math.sqrt(HEAD)

    def body(q_ref, k_ref, v_ref, bz_ref, dc_ref, lt_ref, dq_ref, dk_ref, dv_ref):
        i = pl.program_id(1)

        @pl.when(i == 0)
        def _():
            dk_ref[...] = jnp.zeros_like(dk_ref)
            dv_ref[...] = jnp.zeros_like(dv_ref)

        qb = _bf(q_ref[...])
        dob = _bf(dc_ref[...] * _silu(bz_ref[...]))
        lktot = lt_ref[...][:, :1]
        valid = _tri(lambda r, c: c < r)
        m_le = _bf(_tri(lambda r, c: r <= c).astype(F32))
        m_lt = _bf(_tri(lambda r, c: r < c).astype(F32))

        def tile(j, dq, cpre, ppre, diag):
            rows = pl.ds(pl.multiple_of(j * HEAD, HEAD), HEAD)
            kb, vb = _bf(k_ref[rows, :]), _bf(v_ref[rows, :])
            z = _dot(qb, kb, NT) * scale
            lb, lk, sig = _softplus_parts(z)
            if diag:
                lk = jnp.where(valid, lk, 0.0)
            later = lktot - cpre - _split_dot(lk, m_le)
            w = jnp.exp(lb + later)
            if diag:
                w = jnp.where(valid, w, 0.0)
            da = _dot(dob, vb, NT) * w
            p = ppre + _split_dot(da, m_lt)
            dz = (da * (1.0 - sig) - sig * p) * scale
            if diag:
                dz = jnp.where(valid, dz, 0.0)
            dk_ref[rows, :] += _dot(_bf(dz.T), qb)
            dv_ref[rows, :] += _dot(_bf(w.T), dob)
            return (dq + _dot(_bf(dz), kb), cpre + jnp.sum(lk, axis=1, keepdims=True),
                    ppre + jnp.sum(da, axis=1, keepdims=True))

        def step(j, carry):
            return tile(j, *carry, False)

        zero = jnp.zeros((HEAD, 1), F32)
        carry = lax.fori_loop(0, i, step, (jnp.zeros((HEAD, HEAD), F32), zero, zero))
        dq, _, _ = tile(i, *carry, True)
        dq_ref[...] = dq

    blk = lambda off: pl.BlockSpec((HEAD, HEAD), lambda h, i: (i, off + h))
    full = lambda off: pl.BlockSpec((l, HEAD), lambda h, i: (0, off + h))
    acc = pl.BlockSpec((l, HEAD), lambda h, i: (0, h))
    shp = jax.ShapeDtypeStruct((l, wb), F32)
    return pl.pallas_call(
        body, name=name, grid=(nh, nq),
        in_specs=[blk(q0), full(k0), full(v0), blk(bz0), blk(dc0), blk(0)],
        out_specs=[blk(0), acc, acc], out_shape=[shp, shp, shp],
        compiler_params=_params(("parallel", "arbitrary")),
    )(proj, proj, proj, proj, dcat, lktot)


def _sgu_heads(v, g_ref, w_ref, bt_ref, nh):
    keep = _tri(lambda r, c: r >= c)
    out = []
    for h in range(nh):
        cols = slice(h * HEAD, (h + 1) * HEAD)
        nv, r = _rownorm(v[:, cols])
        wm = jnp.where(keep, w_ref[h], 0.0)
        s = _dot(_bf(wm), _bf(nv * g_ref[:, cols])) + bt_ref[:, h:h + 1]
        out.append((nv, r, wm, s))
    return out


def _sgu_fwd(proj, out_b, norm_g, sgu_w, sgu_bt, wa, wb, name):
    l, n = proj.shape
    nh = wa // HEAD

    def body(au_ref, av_ref, az_ref, bz_ref, ob_ref, g_ref, w_ref, bt_ref, cat_ref):
        u, v, sz = _gelu(au_ref[...]), _gelu(av_ref[...]), _silu(az_ref[...])
        for h, (_, _, _, s) in enumerate(_sgu_heads(v, g_ref, w_ref, bt_ref, nh)):
            cols = slice(h * HEAD, (h + 1) * HEAD)
            cat_ref[:, cols] = _bf(u[:, cols] * s * sz[:, cols])
        cat_ref[:, wa:] = _bf(ob_ref[...] * _silu(bz_ref[...]))

    a_blk = lambda j: pl.BlockSpec((HEAD, wa), lambda i: (i, j))
    bz_blk = pl.BlockSpec((HEAD, wb), lambda i: (i, (3 * wa + 3 * wb) // wb))
    return pl.pallas_call(
        body, name=name, grid=(l // HEAD,),
        in_specs=[a_blk(0), a_blk(1), a_blk(2), bz_blk, pl.BlockSpec((HEAD, wb), lambda i: (i, 0)),
                  _vec_spec(wa), pl.BlockSpec((nh, HEAD, HEAD), lambda i: (0, 0, 0)),
                  pl.BlockSpec((HEAD, nh), lambda i: (0, 0))],
        out_specs=pl.BlockSpec((HEAD, wa + wb), lambda i: (i, 0)),
        out_shape=jax.ShapeDtypeStruct((l, wa + wb), BF16),
        compiler_params=_params(("parallel",)),
    )(proj, proj, proj, proj, out_b, norm_g, sgu_w, sgu_bt)


def _sgu_bwd(proj, out_b, dcat, dq, dk, dv, norm_g, sgu_w, sgu_bt, wa, wb, name):
    l, n = proj.shape
    nh = wa // HEAD

    def body(au_ref, av_ref, az_ref, bz_ref, ob_ref, dc_ref, dq_ref, dk_ref, dv_ref, g_ref, w_ref, bt_ref,
             dp_ref, dw_ref, dbt_ref, dg_ref):
        first = pl.program_id(0) == 0
        keep = _tri(lambda r, c: r >= c)
        au, av, az = au_ref[...], av_ref[...], az_ref[...]
        u, v, sz = _gelu(au), _gelu(av), _silu(az)
        dgelu_u, dgelu_v, dsilu_z = _gelu_grad(au), _gelu_grad(av), _silu_grad(az)
        dg_parts = []
        for h, (nv, r, wm, s) in enumerate(_sgu_heads(v, g_ref, w_ref, bt_ref, nh)):
            cols = slice(h * HEAD, (h + 1) * HEAD)
            dca, uh, szh, gh = dc_ref[:, cols], u[:, cols], sz[:, cols], g_ref[:, cols]
            dp_ref[:, cols] = _bf(dca * s * szh * dgelu_u[:, cols])
            dp_ref[:, 2 * wa + h * HEAD:2 * wa + (h + 1) * HEAD] = _bf(dca * uh * s * dsilu_z[:, cols])
            ds = dca * uh * szh
            _acc(dw_ref.at[h], first, jnp.where(keep, _dot(_bf(ds), _bf(nv * gh), NT), 0.0))
            _acc(dbt_ref.at[:, h:h + 1], first, jnp.sum(ds, axis=1, keepdims=True))
            dvh = _dot(_bf(wm.T), _bf(ds))
            dg_parts.append(_colsum(dvh * nv))
            dnv = dvh * gh
            dvv = r * (dnv - nv * jnp.mean(dnv * nv, axis=-1, keepdims=True))
            dp_ref[:, wa + h * HEAD:wa + (h + 1) * HEAD] = _bf(dvv * dgelu_v[:, cols])
        _acc(dg_ref, first, jnp.concatenate(dg_parts, axis=1))
        base = 3 * wa
        dp_ref[:, base:base + wb] = _bf(dq_ref[...])
        dp_ref[:, base + wb:base + 2 * wb] = _bf(dk_ref[...])
        dp_ref[:, base + 2 * wb:base + 3 * wb] = _bf(dv_ref[...])
        dp_ref[:, base + 3 * wb:] = _bf(dc_ref[:, wa:] * ob_ref[...] * _silu_grad(bz_ref[...]))

    a_blk = lambda j: pl.BlockSpec((HEAD, wa), lambda i: (i, j))
    b_blk = pl.BlockSpec((HEAD, wb), lambda i: (i, 0))
    bz_blk = pl.BlockSpec((HEAD, wb), lambda i: (i, (3 * wa + 3 * wb) // wb))
    w_spec = pl.BlockSpec((nh, HEAD, HEAD), lambda i: (0, 0, 0))
    bt_spec = pl.BlockSpec((HEAD, nh), lambda i: (0, 0))
    return pl.pallas_call(
        body, name=name, grid=(l // HEAD,),
        in_specs=[a_blk(0), a_blk(1), a_blk(2), bz_blk, b_blk, pl.BlockSpec((HEAD, wa + wb), lambda i: (i, 0)),
                  b_blk, b_blk, b_blk, _vec_spec(wa), w_spec, bt_spec],
        out_specs=[pl.BlockSpec((HEAD, n), lambda i: (i, 0)), w_spec, bt_spec, _vec_spec(wa)],
        out_shape=[jax.ShapeDtypeStruct((l, n), BF16), jax.ShapeDtypeStruct((nh, HEAD, HEAD), F32),
                   jax.ShapeDtypeStruct((HEAD, nh), F32), jax.ShapeDtypeStruct((1, wa), F32)],
        compiler_params=_params(("arbitrary",)),
    )(proj, proj, proj, proj, out_b, dcat, dq, dk, dv, norm_g, sgu_w, sgu_bt)


def _ssm_discretise(lr, li, ldt, br, bi):
    dt = jnp.exp(ldt)
    mag = jnp.exp(lr * dt)
    a_re = mag * jnp.cos(li * dt)
    a_im = mag * jnp.sin(li * dt)
    den = lr * lr + li * li
    nr = a_re - 1.0
    coef_re = (nr * lr + a_im * li) / den
    coef_im = (a_im * lr - nr * li) / den
    return a_re, a_im, coef_re * br - coef_im * bi, coef_re * bi + coef_im * br


def _ssm_prep(lr, li, ldt, br, bi, lr_row, li_row, ldt_row, name):
    s, c = br.shape

    def body(lr_ref, li_ref, ldt_ref, br_ref, bi_ref, lrr_ref, lir_ref, ldtr_ref, bbr_ref, bbi_ref, pr_ref, pi_ref):
        _, _, bbr, bbi = _ssm_discretise(lr_ref[...], li_ref[...], ldt_ref[...], br_ref[...], bi_ref[...])
        bbr_ref[...] = bbr
        bbi_ref[...] = bbi
        n = jnp.left_shift(1, lax.broadcasted_iota(jnp.int32, (8, 1), 0)).astype(F32)
        dt = jnp.exp(ldtr_ref[...])
        mag = jnp.exp(n * (lrr_ref[...] * dt))
        ang = n * (lir_ref[...] * dt)
        pr_ref[...] = mag * jnp.cos(ang)
        pi_ref[...] = mag * jnp.sin(ang)

    col = jax.ShapeDtypeStruct((s, c), F32)
    row = jax.ShapeDtypeStruct((8, s), F32)
    return pl.pallas_call(body, name=name, out_shape=[col, col, row, row])(
        lr, li, ldt, br, bi, lr_row, li_row, ldt_row)


def _ssm_prep_bwd(lr, li, ldt, br, bi, da_re, da_im, dbb_re, dbb_im, p, name):
    s, c = br.shape

    def body(lr_ref, li_ref, ldt_ref, br_ref, bi_ref, dar_ref, dai_ref, dbr_ref, dbi_ref,
             dlr_ref, dli_ref, dldt_ref, dbre_ref, dbim_ref):
        args = (lr_ref[...], li_ref[...], ldt_ref[...], br_ref[...], bi_ref[...])
        _, vjp = jax.vjp(_ssm_discretise, *args)
        dlr, dli, dldt, dbr, dbi = vjp((dar_ref[...], dai_ref[...], dbr_ref[...], dbi_ref[...]))
        dlr_ref[...] = dlr
        dli_ref[...] = dli
        dbre_ref[...] = dbr
        dbim_ref[...] = dbi
        idx = lax.broadcasted_iota(jnp.int32, (s, s // p), 0)
        grp = lax.broadcasted_iota(jnp.int32, (s, s // p), 1)
        own = (idx >= grp * p) & (idx < (grp + 1) * p)
        dldt_ref[...] = _colsum(jnp.where(own, dldt, 0.0))

    col1 = jax.ShapeDtypeStruct((s, 1), F32)
    colc = jax.ShapeDtypeStruct((s, c), F32)
    return pl.pallas_call(
        body, name=name, out_shape=[col1, col1, jax.ShapeDtypeStruct((1, s // p), F32), colc, colc],
    )(lr, li, ldt, br, bi, da_re, da_im, dbb_re, dbb_im)


def _shift_rows(x, s, up):
    t = x.shape[0]
    row = lax.broadcasted_iota(jnp.int32, x.shape, 0)
    if up:
        return jnp.where(row < t - s, pltpu.roll(x, t - s, 0), 0.0)
    return jnp.where(row >= s, pltpu.roll(x, s, 0), 0.0)


def _scan_rows(xr, xi, pr_ref, pi_ref, reverse):
    t = xr.shape[0]
    for k in range(int(math.log2(t))):
        ar, ai = pr_ref[k:k + 1, :], pi_ref[k:k + 1, :]
        if reverse:
            ai = -ai
        sr, si = _shift_rows(xr, 1 << k, reverse), _shift_rows(xi, 1 << k, reverse)
        xr, xi = xr + ar * sr - ai * si, xi + ar * si + ai * sr
    return xr, xi


def _first_row(x, val):
    row = lax.broadcasted_iota(jnp.int32, x.shape, 0)
    return jnp.where(row == 0, val, x)


def _last_row(x, val):
    row = lax.broadcasted_iota(jnp.int32, x.shape, 0)
    return jnp.where(row == x.shape[0] - 1, val, x)


def _ssm_states(u_bf, bbd, pr_ref, pi_ref, hr0, hi0, ns):
    bu = _dot(u_bf, bbd)
    xr, xi = bu[:, :ns], bu[:, ns:]
    ar, ai = pr_ref[0:1, :], pi_ref[0:1, :]
    xr = _first_row(xr, xr[0:1, :] + ar * hr0 - ai * hi0)
    xi = _first_row(xi, xi[0:1, :] + ar * hi0 + ai * hr0)
    return _scan_rows(xr, xi, pr_ref, pi_ref, False)


def _ssm_fwd(proj, bbd, ccd, pw_re, pw_im, d_skip, w, name):
    l = proj.shape[0]
    nb, cw, ns2 = bbd.shape
    ns = ns2 // 2
    nc = l // SSM_T

    def body(u_ref, bbd_ref, ccd_ref, pr_ref, pi_ref, d_ref, y_ref, hsr_ref, hsi_ref, hr_s, hi_s):
        @pl.when(pl.program_id(1) == 0)
        def _():
            hr_s[...] = jnp.zeros_like(hr_s)
            hi_s[...] = jnp.zeros_like(hi_s)

        hsr_ref[...] = hr_s[...].reshape(hsr_ref.shape)
        hsi_ref[...] = hi_s[...].reshape(hsi_ref.shape)
        u = u_ref[...]
        hr, hi = _ssm_states(_bf(u), bbd_ref[0], pr_ref, pi_ref, hr_s[...], hi_s[...], ns)
        hr_s[...] = hr[SSM_T - 1:, :]
        hi_s[...] = hi[SSM_T - 1:, :]
        y_ref[...] = _dot(_bf(jnp.concatenate([hr, hi], axis=1)), ccd_ref[0]) + d_ref[...] * u

    return pl.pallas_call(
        body, name=name, grid=(nb, nc),
        in_specs=[pl.BlockSpec((SSM_T, cw), lambda b, k: (k, b)),
                  pl.BlockSpec((1, cw, ns2), lambda b, k: (b, 0, 0)),
                  pl.BlockSpec((1, ns2, cw), lambda b, k: (b, 0, 0)),
                  pl.BlockSpec((8, ns), lambda b, k: (0, b)), pl.BlockSpec((8, ns), lambda b, k: (0, b)),
                  pl.BlockSpec((1, cw), lambda b, k: (0, b))],
        out_specs=[pl.BlockSpec((SSM_T, cw), lambda b, k: (k, b)),
                   pl.BlockSpec((1, 1, ns), lambda b, k: (k, 0, b)), pl.BlockSpec((1, 1, ns), lambda b, k: (k, 0, b))],
        out_shape=[jax.ShapeDtypeStruct((l, w), F32), jax.ShapeDtypeStruct((nc, 1, nb * ns), F32),
                   jax.ShapeDtypeStruct((nc, 1, nb * ns), F32)],
        scratch_shapes=[pltpu.VMEM((1, ns), F32), pltpu.VMEM((1, ns), F32)],
        compiler_params=_params(("parallel", "arbitrary")),
    )(proj, bbd, ccd, pw_re, pw_im, d_skip)


def _ssm_bwd(proj, dy, hs_re, hs_im, bbd, ccd, pw_re, pw_im, d_skip, w, name):
    l = proj.shape[0]
    nb, cw, ns2 = bbd.shape
    ns = ns2 // 2
    nc = l // SSM_T

    def body(u_ref, dy_ref, hsr_ref, hsi_ref, bbd_ref, ccd_ref, pr_ref, pi_ref, d_ref,
             du_ref, dbbd_ref, dccd_ref, dar_ref, dai_ref, dd_ref, gr_s, gi_s):
        first = pl.program_id(1) == 0

        @pl.when(first)
        def _():
            gr_s[...] = jnp.zeros_like(gr_s)
            gi_s[...] = jnp.zeros_like(gi_s)

        u, dy = u_ref[...], dy_ref[...]
        u_bf, dy_bf = _bf(u), _bf(dy)
        hr0, hi0 = hsr_ref[0], hsi_ref[0]
        hr, hi = _ssm_states(u_bf, bbd_ref[0], pr_ref, pi_ref, hr0, hi0, ns)
        dh = _dot(dy_bf, ccd_ref[0], NT)
        ar, ai = pr_ref[0:1, :], pi_ref[0:1, :]
        gcr, gci = gr_s[...], gi_s[...]
        xr, xi = dh[:, :ns], dh[:, ns:]
        xr = _last_row(xr, xr[SSM_T - 1:, :] + ar * gcr + ai * gci)
        xi = _last_row(xi, xi[SSM_T - 1:, :] + ar * gci - ai * gcr)
        gr, gi = _scan_rows(xr, xi, pr_ref, pi_ref, True)
        gr_s[...] = gr[0:1, :]
        gi_s[...] = gi[0:1, :]
        pr_h = _first_row(_shift_rows(hr, 1, False), hr0)
        pi_h = _first_row(_shift_rows(hi, 1, False), hi0)
        _acc(dar_ref, first, _colsum(pr_h * gr + pi_h * gi))
        _acc(dai_ref, first, _colsum(pr_h * gi - pi_h * gr))
        g_bf = _bf(jnp.concatenate([gr, gi], axis=1))
        _acc(dbbd_ref.at[0], first, _dot(_bf(u.T), g_bf))
        _acc(dccd_ref.at[0], first, _dot(_bf(jnp.concatenate([hr, hi], axis=1).T), dy_bf))
        du_ref[...] = _bf(_dot(g_bf, bbd_ref[0], NT) + d_ref[...] * dy)
        _acc(dd_ref, first, _colsum(dy * u))

    rev = lambda b, k: (nc - 1 - k, b)
    return pl.pallas_call(
        body, name=name, grid=(nb, nc),
        in_specs=[pl.BlockSpec((SSM_T, cw), rev), pl.BlockSpec((SSM_T, cw), rev),
                  pl.BlockSpec((1, 1, ns), lambda b, k: (nc - 1 - k, 0, b)),
                  pl.BlockSpec((1, 1, ns), lambda b, k: (nc - 1 - k, 0, b)),
                  pl.BlockSpec((1, cw, ns2), lambda b, k: (b, 0, 0)),
                  pl.BlockSpec((1, ns2, cw), lambda b, k: (b, 0, 0)),
                  pl.BlockSpec((8, ns), lambda b, k: (0, b)), pl.BlockSpec((8, ns), lambda b, k: (0, b)),
                  pl.BlockSpec((1, cw), lambda b, k: (0, b))],
        out_specs=[pl.BlockSpec((SSM_T, cw), rev),
                   pl.BlockSpec((1, cw, ns2), lambda b, k: (b, 0, 0)),
                   pl.BlockSpec((1, ns2, cw), lambda b, k: (b, 0, 0)),
                   pl.BlockSpec((1, ns), lambda b, k: (0, b)), pl.BlockSpec((1, ns), lambda b, k: (0, b)),
                   pl.BlockSpec((1, cw), lambda b, k: (0, b))],
        out_shape=[jax.ShapeDtypeStruct((l, w), BF16), jax.ShapeDtypeStruct(bbd.shape, F32),
                   jax.ShapeDtypeStruct(ccd.shape, F32), jax.ShapeDtypeStruct((1, nb * ns), F32),
                   jax.ShapeDtypeStruct((1, nb * ns), F32), jax.ShapeDtypeStruct((1, w), F32)],
        scratch_shapes=[pltpu.VMEM((1, ns), F32), pltpu.VMEM((1, ns), F32)],
        compiler_params=_params(("parallel", "arbitrary")),
    )(proj, dy, hs_re, hs_im, bbd, ccd, pw_re, pw_im, d_skip)


def _block_diag_b(bb_re, bb_im, g, p, c):
    nb = g // SSM_GB
    eye = jnp.eye(SSM_GB, dtype=F32)

    def one(bb):
        t = bb.reshape(nb, SSM_GB, p, c).transpose(0, 1, 3, 2)
        return (t[:, :, :, None, :] * eye[None, :, None, :, None]).reshape(nb, SSM_GB * c, SSM_GB * p)

    return jnp.concatenate([one(bb_re), one(bb_im)], axis=2)


def _block_diag_c(c_re, c_im, g, p, c):
    nb = g // SSM_GB
    eye = jnp.eye(SSM_GB, dtype=F32)

    def one(cc):
        t = cc.reshape(nb, SSM_GB, c, p).transpose(0, 1, 3, 2)
        return (t[:, :, :, None, :] * eye[None, :, None, :, None]).reshape(nb, SSM_GB * p, SSM_GB * c)

    return jnp.concatenate([one(c_re), one(-c_im)], axis=1)


def _diag_of_b(dbbd, g, p, c):
    nb = g // SSM_GB
    t = dbbd.reshape(nb, SSM_GB, c, 2, SSM_GB, p)
    idx = jnp.arange(SSM_GB)
    d = t[:, idx, :, :, idx, :]
    d = d.transpose(1, 0, 3, 4, 2)
    return d[:, :, 0].reshape(g * p, c), d[:, :, 1].reshape(g * p, c)


def _diag_of_c(dccd, g, p, c):
    nb = g // SSM_GB
    t = dccd.reshape(nb, 2, SSM_GB, p, SSM_GB, c)
    idx = jnp.arange(SSM_GB)
    d = t[:, :, idx, :, idx, :]
    d = d.transpose(1, 0, 2, 4, 3)
    return d[:, :, 0].reshape(g, c, p), -d[:, :, 1].reshape(g, c, p)


def _glu_fwd(y, proj, w_glu, b_glu, name):
    l, w = y.shape
    tm = _tile(l, 256)

    def body(y_ref, z_ref, w_ref, b_ref, o_ref):
        g = _gelu(y_ref[...])
        t = _dot(_bf(g), w_ref[...]) + b_ref[...]
        o_ref[...] = _bf(g * _sigmoid(t) * _silu(z_ref[...]))

    return pl.pallas_call(
        body, name=name, grid=(l // tm,),
        in_specs=[_row_spec(tm, w), pl.BlockSpec((tm, w), lambda i: (i, 1)),
                  pl.BlockSpec((w, w), lambda i: (0, 0)), _vec_spec(w)],
        out_specs=_row_spec(tm, w), out_shape=jax.ShapeDtypeStruct((l, w), BF16),
        compiler_params=_params(("parallel",)),
    )(y, proj, w_glu, b_glu)


def _glu_bwd(do, y, proj, w_glu, b_glu, name):
    l, w = y.shape
    tm = _tile(l, 256)
    nsteps = l // tm

    def body(do_ref, y_ref, z_ref, w_ref, b_ref, dy_ref, dz_ref, dw_ref, db_ref, dw_acc):
        i = pl.program_id(0)
        first = i == 0
        yv, z, do = y_ref[...], z_ref[...], do_ref[...]
        g = _gelu(yv)
        g_bf = _bf(g)
        sg = _sigmoid(_dot(g_bf, w_ref[...]) + b_ref[...])
        dyy = do * _silu(z)
        dz_ref[...] = _bf(do * g * sg * _silu_grad(z))
        dt = dyy * g * sg * (1.0 - sg)
        dt_bf = _bf(dt)
        dg = dyy * sg + _dot(dt_bf, w_ref[...], NT)
        dy_ref[...] = dg * _gelu_grad(yv)
        _acc(dw_acc, first, _dot(_bf(g.T), dt_bf))
        _acc(db_ref, first, _colsum(dt))

        @pl.when(i == nsteps - 1)
        def _():
            dw_ref[...] = _bf(dw_acc[...])

    return pl.pallas_call(
        body, name=name, grid=(nsteps,),
        in_specs=[_row_spec(tm, w), _row_spec(tm, w), pl.BlockSpec((tm, w), lambda i: (i, 1)),
                  pl.BlockSpec((w, w), lambda i: (0, 0)), _vec_spec(w)],
        out_specs=[_row_spec(tm, w), _row_spec(tm, w), pl.BlockSpec((w, w), lambda i: (0, 0)), _vec_spec(w)],
        out_shape=[jax.ShapeDtypeStruct((l, w), F32), jax.ShapeDtypeStruct((l, w), BF16),
                   jax.ShapeDtypeStruct((w, w), BF16), jax.ShapeDtypeStruct((1, w), F32)],
        scratch_shapes=[pltpu.VMEM((w, w), F32)],
        compiler_params=_params(("arbitrary",)),
    )(do, y, proj, w_glu, b_glu)


MOD_ROWS = 128


def _mod_fwd(cond_pad, w_mod, b_shard, name):
    nl, d, ncol = w_mod.shape
    tn = _tile(ncol, 512)

    def body(c_ref, w_ref, b_ref, o_ref):
        o_ref[0] = _dot(_bf(c_ref[...]), _bf(w_ref[0])) + b_ref[0]

    return pl.pallas_call(
        body, name=name, grid=(nl, ncol // tn),
        in_specs=[pl.BlockSpec((MOD_ROWS, d), lambda a, j: (0, 0)),
                  pl.BlockSpec((1, d, tn), lambda a, j: (a, 0, j)),
                  pl.BlockSpec((1, 1, tn), lambda a, j: (a, 0, j))],
        out_specs=pl.BlockSpec((1, MOD_ROWS, tn), lambda a, j: (a, 0, j)),
        out_shape=jax.ShapeDtypeStruct((nl, MOD_ROWS, ncol), F32),
        compiler_params=_params(("parallel", "parallel")),
    )(cond_pad, w_mod, b_shard)


def _mod_bwd(cond_pad_t, dmod_pad, name):
    nl, _, ncol = dmod_pad.shape
    d = cond_pad_t.shape[0]
    tn = _tile(ncol, 512)

    def body(c_ref, dm_ref, o_ref):
        o_ref[0] = _dot(_bf(c_ref[...]), _bf(dm_ref[0]))

    return pl.pallas_call(
        body, name=name, grid=(nl, ncol // tn),
        in_specs=[pl.BlockSpec((d, MOD_ROWS), lambda a, j: (0, 0)),
                  pl.BlockSpec((1, MOD_ROWS, tn), lambda a, j: (a, 0, j))],
        out_specs=pl.BlockSpec((1, d, tn), lambda a, j: (a, 0, j)),
        out_shape=jax.ShapeDtypeStruct((nl, d, ncol), F32),
        compiler_params=_params(("parallel", "parallel")),
    )(cond_pad_t, dmod_pad)


def _silu_rows(c2d, name):
    def body(c_ref, o_ref):
        o_ref[...] = _silu(c_ref[...])

    return pl.pallas_call(body, name=name, out_shape=jax.ShapeDtypeStruct(c2d.shape, F32))(c2d)


def _sum_leading(x, name):
    n, r, c = x.shape
    tr = _tile(r, max(16, (1 << 20) // (4 * c)), 16 if r % 16 == 0 else 8)

    def body(x_ref, o_ref):
        acc = x_ref[0].astype(F32)
        for k in range(1, n):
            acc = acc + x_ref[k].astype(F32)
        o_ref[...] = acc

    return pl.pallas_call(
        body, name=name, grid=(r // tr,),
        in_specs=[pl.BlockSpec((n, tr, c), lambda i: (0, i, 0))], out_specs=pl.BlockSpec((tr, c), lambda i: (i, 0)),
        out_shape=jax.ShapeDtypeStruct((r, c), F32), compiler_params=_params(("parallel",)),
    )(x)


def _adamw(w, gs, m, v, name):
    r, c = w.shape
    tr = _tile(r, max(8, (1 << 19) // (4 * c)), 8)
    ng = len(gs)

    def body(*refs):
        w_ref, g_refs, m_ref, v_ref = refs[0], refs[1:1 + ng], refs[1 + ng], refs[2 + ng]
        g_ref, d_ref, nm_ref, nv_ref = refs[3 + ng:]
        g = g_refs[0][...]
        for extra in g_refs[1:]:
            g = g + extra[...]
        g_ref[...] = g
        nm = ADAM_B1 * m_ref[...] + (1.0 - ADAM_B1) * g
        nv = ADAM_B2 * v_ref[...] + (1.0 - ADAM_B2) * (g * g)
        nm_ref[...] = nm
        nv_ref[...] = nv
        m_hat = nm / (1.0 - ADAM_B1 ** ADAM_STEP)
        v_hat = nv / (1.0 - ADAM_B2 ** ADAM_STEP)
        d_ref[...] = -ADAM_LR * (m_hat / (jnp.sqrt(v_hat) + ADAM_EPS) + ADAM_WD * w_ref[...])

    spec = pl.BlockSpec((tr, c), lambda i: (i, 0))
    shp = jax.ShapeDtypeStruct((r, c), F32)
    return pl.pallas_call(
        body, name=name, grid=(r // tr,), in_specs=[spec] * (3 + ng), out_specs=[spec] * 4,
        out_shape=[shp] * 4, compiler_params=_params(("parallel",)),
    )(w, *gs, m, v)


ANY = pl.BlockSpec(memory_space=pl.ANY)


def _flip(v, bit):
    return 1 - v if bit else v


def _allgather8(x, name):
    def body(x_ref, o_ref, send_sems, recv_sems, local_sem):
        mx, my, mc = lax.axis_index("x"), lax.axis_index("y"), lax.axis_index("c")
        me = 4 * mx + 2 * my + mc
        mine = pltpu.make_async_copy(x_ref, o_ref.at[me], local_sem)
        mine.start()

        def copy(j):
            peer = (_flip(mx, j & 4), _flip(my, j & 2), _flip(mc, j & 1))
            return pltpu.make_async_remote_copy(
                src_ref=x_ref, dst_ref=o_ref.at[me], send_sem=send_sems.at[j - 1], recv_sem=recv_sems.at[j - 1],
                device_id=peer, device_id_type=MESH)

        def landing(j):
            peer = (_flip(mx, j & 4), _flip(my, j & 2), _flip(mc, j & 1))
            slot = 4 * peer[0] + 2 * peer[1] + peer[2]
            return pltpu.make_async_remote_copy(
                src_ref=x_ref, dst_ref=o_ref.at[slot], send_sem=send_sems.at[j - 1], recv_sem=recv_sems.at[j - 1],
                device_id=peer, device_id_type=MESH)

        for j in range(1, 8):
            copy(j).start()
        for j in range(1, 8):
            landing(j).wait()
        mine.wait()

    return pl.pallas_call(
        body, name=name, in_specs=[ANY], out_specs=ANY,
        out_shape=jax.ShapeDtypeStruct((8,) + x.shape, x.dtype),
        scratch_shapes=[pltpu.SemaphoreType.DMA((7,)), pltpu.SemaphoreType.DMA((7,)), pltpu.SemaphoreType.DMA],
    )(x)


def _chip_exchange(xs, gather, name):
    n = len(xs)

    def body(*refs):
        x_refs, o_refs = refs[:n], refs[n:2 * n]
        send_sems, recv_sems, local_sems = refs[2 * n:]
        mx, my, mc = lax.axis_index("x"), lax.axis_index("y"), lax.axis_index("c")
        k0 = 2 * mx + my
        local = []
        for a in range(n):
            src = x_refs[a] if gather else x_refs[a].at[k0]
            cp = pltpu.make_async_copy(src, o_refs[a].at[k0], local_sems.at[a])
            cp.start()
            local.append(cp)

        def copy(a, j, outgoing):
            px, py = _flip(mx, j & 2), _flip(my, j & 1)
            kp = 2 * px + py
            if outgoing:
                src = x_refs[a] if gather else x_refs[a].at[kp]
                dst = o_refs[a].at[k0]
            else:
                src = x_refs[a] if gather else x_refs[a].at[k0]
                dst = o_refs[a].at[kp]
            s = a * 3 + j - 1
            return pltpu.make_async_remote_copy(
                src_ref=src, dst_ref=dst, send_sem=send_sems.at[s], recv_sem=recv_sems.at[s],
                device_id=(px, py, mc), device_id_type=MESH)

        for a in range(n):
            for j in range(1, 4):
                copy(a, j, True).start()
        for a in range(n):
            for j in range(1, 4):
                copy(a, j, False).wait()
        for cp in local:
            cp.wait()

    out_shape = [jax.ShapeDtypeStruct(((4,) + x.shape) if gather else x.shape, x.dtype) for x in xs]
    return pl.pallas_call(
        body, name=name, in_specs=[ANY] * n, out_specs=[ANY] * n, out_shape=out_shape,
        scratch_shapes=[pltpu.SemaphoreType.DMA((3 * n,)), pltpu.SemaphoreType.DMA((3 * n,)),
                        pltpu.SemaphoreType.DMA((n,))],
    )(*xs)


def _sibling_exchange(xs, name):
    n = len(xs)

    def body(*refs):
        x_refs, o_refs = refs[:n], refs[n:2 * n]
        send_sems, recv_sems = refs[2 * n:]
        sib = (lax.axis_index("x"), lax.axis_index("y"), 1 - lax.axis_index("c"))
        copies = [pltpu.make_async_remote_copy(
            src_ref=x_refs[a], dst_ref=o_refs[a], send_sem=send_sems.at[a], recv_sem=recv_sems.at[a],
            device_id=sib, device_id_type=MESH) for a in range(n)]
        for cp in copies:
            cp.start()
        for cp in copies:
            cp.wait()

    return pl.pallas_call(
        body, name=name, in_specs=[ANY] * n, out_specs=[ANY] * n,
        out_shape=[jax.ShapeDtypeStruct(x.shape, x.dtype) for x in xs],
        scratch_shapes=[pltpu.SemaphoreType.DMA((n,)), pltpu.SemaphoreType.DMA((n,))],
    )(*xs)


PACK = 1024


def _pack(parts):
    flat = []
    for p in parts:
        v = p.reshape(-1).astype(F32)
        flat.append(jnp.pad(v, (0, (-v.shape[0]) % PACK)))
    return jnp.concatenate(flat).reshape(-1, 128)


def _unpack(packed, shapes):
    flat = packed.reshape(-1)
    out, off = [], 0
    for shp in shapes:
        n = math.prod(shp)
        out.append(flat[off:off + n].reshape(shp))
        off += n + (-n) % PACK
    return out


def kernel(x, c, ln_pre_g, ln_post_g, w_mod, b_mod, w_in_ab, w_out_ab, sgu_norm_g, sgu_w, sgu_b, w_in_ssm, w_out_ssm, lam_re, lam_im, b_re, b_im, c_re, c_im, d_skip, log_dt, w_glu, b_glu, loss_target, m_ln_pre_g, m_ln_post_g, m_w_mod, m_b_mod, m_w_in_ab, m_w_out_ab, m_sgu_norm_g, m_sgu_w, m_sgu_b, m_w_in_ssm, m_w_out_ssm, m_lam_re, m_lam_im, m_b_re, m_b_im, m_c_re, m_c_im, m_d_skip, m_log_dt, m_w_glu, m_b_glu, v_ln_pre_g, v_ln_post_g, v_w_mod, v_b_mod, v_w_in_ab, v_w_out_ab, v_sgu_norm_g, v_sgu_w, v_sgu_b, v_w_in_ssm, v_w_out_ssm, v_lam_re, v_lam_im, v_b_re, v_b_im, v_c_re, v_c_im, v_d_skip, v_log_dt, v_w_glu, v_b_glu):
    given = dict(locals())
    mx, my, mc = lax.axis_index("x"), lax.axis_index("y"), lax.axis_index("c")
    me = 4 * mx + 2 * my + mc
    chip = 2 * mx + my

    _, l, d = x.shape
    x2, tgt = x[0], loss_target[0]
    n_in = w_in_ab.shape[2] * 4
    wa = wb = n_in // 7
    w = w_out_ssm.shape[1]
    g, p, cch = b_re.shape[1:]
    nmod = w_mod.shape[2]

    shards = [_bf(w_in_ab[0]), _bf(w_out_ab[0]), _bf(w_in_ssm[0]), _bf(w_out_ssm[0]), _bf(w_glu[0]),
              d_skip, b_glu]
    gw_in_ab, gw_out_ab, gw_in_ssm, gw_out_ssm, gw_glu, g_dskip, g_bglu = _chip_exchange(
        shards, True, "gather_weights")
    win_ab = jnp.concatenate([gw_in_ab[k] for k in range(4)], axis=1)
    wout_ab = gw_out_ab.reshape(wa + wb, d)
    win_ssm = gw_in_ssm.reshape(d, 2 * w)
    wout_ssm = jnp.concatenate([gw_out_ssm[k] for k in range(4)], axis=1)
    wglu = gw_glu.reshape(w, w)
    dskip_full = g_dskip.reshape(1, w)
    bglu_full = g_bglu.reshape(1, w)

    cond = _silu_rows(c.reshape(d // 128, 128), "cond_silu")
    cond_all = _allgather8(cond, "gather_cond").reshape(8, d)
    b_shard = lax.dynamic_slice(b_mod, (0, chip * nmod), (2, nmod)).reshape(2, 1, nmod)
    cond_pad = jnp.pad(cond_all, ((0, MOD_ROWS - 8), (0, 0)))
    modp = _mod_fwd(cond_pad, w_mod, b_shard, "mod_fwd")[:, :8]
    modp_all = _allgather8(modp.reshape(16, nmod), "gather_mod").reshape(4, 2, 2, 8, nmod)
    mine = lax.dynamic_index_in_dim(lax.dynamic_index_in_dim(modp_all, mc, 1, False), me, 2, False)
    mod = mine.transpose(1, 0, 2).reshape(2, 3 * d)
    shift = [mod[a:a + 1, :d] for a in range(2)]
    scale = [mod[a:a + 1, d:2 * d] for a in range(2)]
    gate = [mod[a:a + 1, 2 * d:] for a in range(2)]
    pre_g = [ln_pre_g[a:a + 1] for a in range(2)]
    post_g = [ln_post_g[a:a + 1] for a in range(2)]

    sgu_w0, sgu_bt = sgu_w[0], sgu_b[0].T
    h0 = _pre_fwd(x2, pre_g[0], scale[0], shift[0], "pre0_fwd")
    proj0 = _matmul(h0, win_ab, "nn", F32, "proj0", tm=1024, tn=256)
    out_b, lktot = _attn_fwd(proj0, wa, wb, "attn_fwd")
    cat = _sgu_fwd(proj0, out_b, sgu_norm_g, sgu_w0, sgu_bt, wa, wb, "sgu_fwd")
    y0 = _matmul(cat, wout_ab, "nn", F32, "out0", tm=1024)
    x1, h1 = _post_pre_fwd(x2, y0, gate[0], post_g[0], pre_g[1], scale[1], shift[1], "post0_pre1_fwd")

    s = g * p
    lr_c, li_c = lam_re.reshape(s, 1), lam_im.reshape(s, 1)
    ldt_c = jnp.repeat(log_dt.reshape(g), p).reshape(s, 1)
    br_c, bi_c = b_re.reshape(s, cch), b_im.reshape(s, cch)
    bb_re, bb_im, pw_re, pw_im = _ssm_prep(lr_c, li_c, ldt_c, br_c, bi_c, lr_c.reshape(1, s), li_c.reshape(1, s),
                                           ldt_c.reshape(1, s), "ssm_prep")
    bbd = _bf(_block_diag_b(bb_re, bb_im, g, p, cch))
    ccd = _bf(_block_diag_c(c_re[0], c_im[0], g, p, cch))
    proj1 = _matmul(h1, win_ssm, "nn", F32, "proj1", tm=1024)
    y_ssm, hs_re, hs_im = _ssm_fwd(proj1, bbd, ccd, pw_re, pw_im, dskip_full, w, "ssm_fwd")
    o1 = _glu_fwd(y_ssm, proj1, wglu, bglu_full, "glu_fwd")
    y1 = _matmul(o1, wout_ssm, "nn", F32, "out1", tm=1024)
    loss_vec, dy1, dx2, dgate1, dpost1 = _post_loss(x1, y1, gate[1], post_g[1], tgt, "post1_loss")

    do1 = _matmul(dy1, wout_ssm, "nt", F32, "out1_dx", tm=1024)
    gr_wout_ssm = _matmul(o1, dy1, "tn", BF16, "out1_dw", tk=1024, n_split=4)
    dy_ssm, dz1, gr_wglu, gr_bglu = _glu_bwd(do1, y_ssm, proj1, wglu, bglu_full, "glu_bwd")
    du1, dbbd, dccd, da_re, da_im, gr_dskip = _ssm_bwd(proj1, dy_ssm, hs_re, hs_im, bbd, ccd, pw_re, pw_im,
                                                       dskip_full, w, "ssm_bwd")
    dproj1 = jnp.concatenate([du1, dz1], axis=1)
    dh1 = _matmul(dproj1, win_ssm, "nt", F32, "proj1_dx", tm=1024)
    gr_win_ssm = _matmul(h1, dproj1, "tn", BF16, "proj1_dw", tk=1024)
    dx1, dscale1, dshift1, dpre1, dy0, dgate0, dpost0 = _pre_bwd(
        dh1, dx2, x1, pre_g[1], scale[1], "pre1_post0_bwd", post=(y0, gate[0], post_g[0]))

    dcat = _matmul(dy0, wout_ab, "nt", F32, "out0_dx", tm=1024)
    gr_wout_ab = _matmul(cat, dy0, "tn", BF16, "out0_dw", tk=1024)
    dq, dk, dv = _attn_bwd(proj0, dcat, lktot, wa, wb, "attn_bwd")
    dproj0, gr_sgu_w, gr_sgu_bt, gr_sgu_g = _sgu_bwd(proj0, out_b, dcat, dq, dk, dv, sgu_norm_g, sgu_w0, sgu_bt,
                                                     wa, wb, "sgu_bwd")
    dh0 = _matmul(dproj0, win_ab, "nt", F32, "proj0_dx", tm=1024, tk=1792)
    gr_win_ab = _matmul(h0, dproj0, "tn", BF16, "proj0_dw", tk=1024, tn=256, n_split=4)
    grad_x, dscale0, dshift0, dpre0 = _pre_bwd(dh0, dx1, x2, pre_g[0], scale[0], "pre0_bwd")

    pieces = [gr_win_ab, gr_wout_ab.reshape(4, (wa + wb) // 4, d), gr_win_ssm.reshape(4, d // 4, 2 * w),
              gr_wout_ssm, gr_wglu.reshape(4, w // 4, w)]
    landed = _chip_exchange(pieces, False, "scatter_grads")
    big_names = ["w_in_ab", "w_out_ab", "w_in_ssm", "w_out_ssm", "w_glu"]
    sums = [_sum_leading(a, "sum_" + nm) for a, nm in zip(landed, big_names)]
    sib = _sibling_exchange(sums, "sibling_grads")
    results = {}
    for nm, s_mine, s_sib in zip(big_names, sums, sib):
        shp = given[nm].shape
        two_d = lambda a: a.reshape(-1, shp[-1])
        outs = _adamw(two_d(given[nm]), [s_mine, s_sib], two_d(given["m_" + nm]), two_d(given["v_" + nm]),
                      "adamw_" + nm)
        results[nm] = [o.reshape(shp) for o in outs]

    dbb_re, dbb_im = _diag_of_b(dbbd, g, p, cch)
    dc_re, dc_im = _diag_of_c(dccd, g, p, cch)
    dmod = jnp.concatenate([jnp.concatenate([dshift0, dscale0, dgate0], axis=1),
                            jnp.concatenate([dshift1, dscale1, dgate1], axis=1)], axis=0)
    partial = [loss_vec[:, :1], jnp.concatenate([dpre0, dpre1], 0), jnp.concatenate([dpost0, dpost1], 0), dmod,
               gr_sgu_g, gr_sgu_w, gr_sgu_bt.T, da_re, da_im, dbb_re, dbb_im, dc_re, dc_im, gr_dskip, gr_bglu]
    part_shapes = [a.shape for a in partial]
    packed = _pack(partial)
    gathered = _allgather8(packed, "gather_small")
    total = _sum_leading(gathered, "sum_small")
    (loss_s, g_pre, g_post, g_bmod, g_sgu_g, g_sgu_w, g_sgu_b, s_da_re, s_da_im, s_dbb_re, s_dbb_im, g_c_re, g_c_im,
     g_dskip_full, g_bglu_full) = _unpack(total, part_shapes)
    loss = loss_s.reshape(())

    g_lr, g_li, g_ldt, g_br, g_bi = _ssm_prep_bwd(lr_c, li_c, ldt_c, br_c, bi_c, s_da_re.reshape(s, 1),
                                                  s_da_im.reshape(s, 1), s_dbb_re, s_dbb_im, p, "ssm_prep_bwd")
    small = {
        "ln_pre_g": g_pre, "ln_post_g": g_post, "b_mod": g_bmod, "sgu_norm_g": g_sgu_g,
        "sgu_w": g_sgu_w.reshape(sgu_w.shape), "sgu_b": g_sgu_b.reshape(sgu_b.shape),
        "lam_re": g_lr.reshape(lam_re.shape), "lam_im": g_li.reshape(lam_im.shape),
        "b_re": g_br.reshape(b_re.shape), "b_im": g_bi.reshape(b_im.shape),
        "c_re": g_c_re.reshape(c_re.shape), "c_im": g_c_im.reshape(c_im.shape),
        "d_skip": lax.dynamic_slice(g_dskip_full, (0, chip * (w // 4)), (1, w // 4)),
        "log_dt": g_ldt.reshape(log_dt.shape),
        "b_glu": lax.dynamic_slice(g_bglu_full, (0, chip * (w // 4)), (1, w // 4)),
    }
    small_names = list(small)
    small_shapes = [small[nm].shape for nm in small_names]
    outs = _adamw(_pack([given[nm] for nm in small_names]), [_pack([small[nm] for nm in small_names])],
                  _pack([given["m_" + nm] for nm in small_names]), _pack([given["v_" + nm] for nm in small_names]),
                  "adamw_small")
    unpacked = [_unpack(o, small_shapes) for o in outs]
    for i, nm in enumerate(small_names):
        results[nm] = [small[nm]] + [unpacked[k][i] for k in range(1, 4)]

    off = sum(math.prod(sh) + (-math.prod(sh)) % PACK for sh in part_shapes[:3])
    dmod_rows = gathered.reshape(8, -1)[:, off:off + 6 * d].reshape(8, 2, 3 * d)
    dmod_shard = lax.dynamic_slice(dmod_rows, (0, 0, chip * nmod), (8, 2, nmod)).transpose(1, 0, 2)
    dmod_pad = jnp.pad(dmod_shard, ((0, 0), (0, MOD_ROWS - 8), (0, 0)))
    gr_wmod = _mod_bwd(cond_pad.T, dmod_pad, "mod_bwd")
    two_d = lambda a: a.reshape(-1, nmod)
    outs = _adamw(two_d(w_mod), [two_d(gr_wmod)], two_d(m_w_mod), two_d(v_w_mod), "adamw_w_mod")
    results["w_mod"] = [o.reshape(w_mod.shape) for o in outs]

    names = ["ln_pre_g", "ln_post_g", "w_mod", "b_mod", "w_in_ab", "w_out_ab", "sgu_norm_g", "sgu_w", "sgu_b",
             "w_in_ssm", "w_out_ssm", "lam_re", "lam_im", "b_re", "b_im", "c_re", "c_im", "d_skip", "log_dt",
             "w_glu", "b_glu"]
    return (loss, grad_x[None], *[results[nm][0] for nm in names], *[results[nm][1] for nm in names],
            *[results[nm][2] for nm in names], *[results[nm][3] for nm in names])
```

```python
import functools
import math

import jax
import jax.numpy as jnp
from jax import lax
from jax.experimental import pallas as pl
from jax.experimental.pallas import tpu as pltpu

F32 = jnp.float32
BF16 = jnp.bfloat16
MESH = pl.DeviceIdType.MESH

EPS = 1e-6
HEAD = 128
SSM_T = 128
SSM_GB = 16
ADAM_LR, ADAM_B1, ADAM_B2, ADAM_EPS, ADAM_WD, ADAM_STEP = 0.001, 0.9, 0.999, 1e-08, 0.01, 10
VMEM_LIMIT = 56 * 1024 * 1024

NN = (((1,), (0,)), ((), ()))
NT = (((1,), (1,)), ((), ()))
TN = (((0,), (0,)), ((), ()))


def _params(sem=None):
    return pltpu.CompilerParams(dimension_semantics=sem, vmem_limit_bytes=VMEM_LIMIT)


def _dot(a, b, dims=NN):
    return lax.dot_general(a, b, dims, preferred_element_type=F32)


def _bf(x):
    return x.astype(BF16)


def _gelu(x):
    k = math.sqrt(2.0 / math.pi)
    t = jnp.tanh(k * (x + 0.044715 * x * x * x))
    return 0.5 * x * (1.0 + t)


def _gelu_grad(x):
    k = math.sqrt(2.0 / math.pi)
    x2 = x * x
    t = jnp.tanh(k * (x + 0.044715 * x * x2))
    return 0.5 * (1.0 + t) + 0.5 * x * (1.0 - t * t) * k * (1.0 + 3.0 * 0.044715 * x2)


def _sigmoid(x):
    return 1.0 / (1.0 + jnp.exp(-x))


def _silu(x):
    return x * _sigmoid(x)


def _silu_grad(x):
    s = _sigmoid(x)
    return s * (1.0 + x * (1.0 - s))


def _tile(n, t, mult=128):
    if n <= t:
        return n
    for cand in range(t - t % mult, 0, -mult):
        if n % cand == 0:
            return cand
    raise ValueError((n, t, mult))


def _matmul(a, b, mode, out_dtype, name, tm=512, tn=512, tk=2048, n_split=1):
    if mode == "nn":
        (m, kk), (_, n) = a.shape, b.shape
    elif mode == "nt":
        (m, kk), (n, _) = a.shape, b.shape
    else:
        (kk, m), (_, n) = a.shape, b.shape
    tm, tk = _tile(m, tm), _tile(kk, tk)
    ns = n // n_split
    tn = _tile(ns, tn)
    nk = kk // tk
    dims = {"nn": NN, "nt": NT, "tn": TN}[mode]

    def body(a_ref, b_ref, o_ref, acc_ref):
        k = pl.program_id(2)
        part = _dot(_bf(a_ref[...]), _bf(b_ref[...]), dims)

        @pl.when(k == 0)
        def _():
            acc_ref[...] = part

        @pl.when(k > 0)
        def _():
            acc_ref[...] += part

        @pl.when(k == nk - 1)
        def _():
            o_ref[...] = acc_ref[...].astype(out_dtype).reshape(o_ref.shape)

    if mode == "nn":
        a_spec = pl.BlockSpec((tm, tk), lambda i, j, k: (i, k))
        b_spec = pl.BlockSpec((tk, tn), lambda i, j, k: (k, j))
    elif mode == "nt":
        a_spec = pl.BlockSpec((tm, tk), lambda i, j, k: (i, k))
        b_spec = pl.BlockSpec((tn, tk), lambda i, j, k: (j, k))
    else:
        a_spec = pl.BlockSpec((tk, tm), lambda i, j, k: (k, i))
        b_spec = pl.BlockSpec((tk, tn), lambda i, j, k: (k, j))
    if n_split == 1:
        out_shape = jax.ShapeDtypeStruct((m, n), out_dtype)
        o_spec = pl.BlockSpec((tm, tn), lambda i, j, k: (i, j))
    else:
        per = ns // tn
        out_shape = jax.ShapeDtypeStruct((n_split, m, ns), out_dtype)
        o_spec = pl.BlockSpec((1, tm, tn), lambda i, j, k: (j // per, i, j % per))
    return pl.pallas_call(
        body, name=name, grid=(m // tm, n // tn, nk),
        in_specs=[a_spec, b_spec], out_specs=o_spec, out_shape=out_shape,
        scratch_shapes=[pltpu.VMEM((tm, tn), F32)],
        compiler_params=_params(("parallel", "parallel", "arbitrary")),
    )(a, b)


def _row_spec(tm, d):
    return pl.BlockSpec((tm, d), lambda i: (i, 0))


def _vec_spec(d):
    return pl.BlockSpec((1, d), lambda i: (0, 0))


def _acc(ref, first, val):
    @pl.when(first)
    def _():
        ref[...] = val

    @pl.when(jnp.logical_not(first))
    def _():
        ref[...] += val


def _colsum(x):
    return jnp.sum(x, axis=0, keepdims=True)


def _rownorm(x):
    r = lax.rsqrt(jnp.mean(x * x, axis=-1, keepdims=True) + EPS)
    return x * r, r


def _pre_fwd(x, g, scale, shift, name):
    l, d = x.shape
    tm = _tile(l, 256)

    def body(x_ref, g_ref, sc_ref, sh_ref, h_ref):
        n, _ = _rownorm(x_ref[...])
        h_ref[...] = _bf(n * g_ref[...] * (1.0 + sc_ref[...]) + sh_ref[...])

    return pl.pallas_call(
        body, name=name, grid=(l // tm,),
        in_specs=[_row_spec(tm, d), _vec_spec(d), _vec_spec(d), _vec_spec(d)],
        out_specs=_row_spec(tm, d), out_shape=jax.ShapeDtypeStruct((l, d), BF16),
        compiler_params=_params(("parallel",)),
    )(x, g, scale, shift)


def _post_pre_fwd(x, y, gate, pg, g1, scale1, shift1, name):
    l, d = x.shape
    tm = _tile(l, 256)

    def body(x_ref, y_ref, gate_ref, pg_ref, g1_ref, sc_ref, sh_ref, x1_ref, h1_ref):
        ny, _ = _rownorm(y_ref[...])
        x1 = x_ref[...] + gate_ref[...] * (ny * pg_ref[...])
        x1_ref[...] = x1
        n1, _ = _rownorm(x1)
        h1_ref[...] = _bf(n1 * g1_ref[...] * (1.0 + sc_ref[...]) + sh_ref[...])

    v = _vec_spec(d)
    return pl.pallas_call(
        body, name=name, grid=(l // tm,),
        in_specs=[_row_spec(tm, d), _row_spec(tm, d), v, v, v, v, v],
        out_specs=[_row_spec(tm, d), _row_spec(tm, d)],
        out_shape=[jax.ShapeDtypeStruct((l, d), F32), jax.ShapeDtypeStruct((l, d), BF16)],
        compiler_params=_params(("parallel",)),
    )(x, y, gate, pg, g1, scale1, shift1)


def _post_loss(x1, y1, gate, pg, target, name):
    l, d = x1.shape
    tm = _tile(l, 256)

    def body(x_ref, y_ref, gate_ref, pg_ref, t_ref, loss_ref, dy_ref, dx_ref, dgate_ref, dpg_ref):
        first = pl.program_id(0) == 0
        y = y_ref[...]
        ny, ry = _rownorm(y)
        q = ny * pg_ref[...]
        x2 = x_ref[...] + gate_ref[...] * q
        e = x2 - t_ref[...]
        _acc(loss_ref, first, jnp.full((1, 128), 0.5 / d, F32) * jnp.sum(e * e))
        dx2 = e * (1.0 / d)
        dx_ref[...] = dx2
        _acc(dgate_ref, first, _colsum(dx2 * q))
        dq = dx2 * gate_ref[...]
        _acc(dpg_ref, first, _colsum(dq * ny))
        dny = dq * pg_ref[...]
        dy = ry * (dny - ny * jnp.mean(dny * ny, axis=-1, keepdims=True))
        dy_ref[...] = _bf(dy)

    v = _vec_spec(d)
    return pl.pallas_call(
        body, name=name, grid=(l // tm,),
        in_specs=[_row_spec(tm, d), _row_spec(tm, d), v, v, _row_spec(tm, d)],
        out_specs=[_vec_spec(128), _row_spec(tm, d), _row_spec(tm, d), v, v],
        out_shape=[jax.ShapeDtypeStruct((1, 128), F32), jax.ShapeDtypeStruct((l, d), BF16),
                   jax.ShapeDtypeStruct((l, d), F32), jax.ShapeDtypeStruct((1, d), F32),
                   jax.ShapeDtypeStruct((1, d), F32)],
        compiler_params=_params(("arbitrary",)),
    )(x1, y1, gate, pg, target)


def _pre_bwd(dh, dres, x, g, scale, name, post=None):
    l, d = x.shape
    tm = _tile(l, 256)
    with_post = post is not None

    def body(*refs):
        if with_post:
            (dh_ref, dres_ref, x_ref, g_ref, sc_ref, y_ref, gate_ref, pg_ref,
             dx_ref, dsc_ref, dsh_ref, dg_ref, dy_ref, dgate_ref, dpg_ref) = refs
        else:
            dh_ref, dres_ref, x_ref, g_ref, sc_ref, dx_ref, dsc_ref, dsh_ref, dg_ref = refs
        first = pl.program_id(0) == 0
        dh = dh_ref[...]
        n, r = _rownorm(x_ref[...])
        _acc(dsc_ref, first, _colsum(dh * (n * g_ref[...])))
        _acc(dsh_ref, first, _colsum(dh))
        dyn = dh * (1.0 + sc_ref[...])
        _acc(dg_ref, first, _colsum(dyn * n))
        dn = dyn * g_ref[...]
        dx = dres_ref[...] + r * (dn - n * jnp.mean(dn * n, axis=-1, keepdims=True))
        dx_ref[...] = dx
        if with_post:
            ny, ry = _rownorm(y_ref[...])
            _acc(dgate_ref, first, _colsum(dx * (ny * pg_ref[...])))
            dq = dx * gate_ref[...]
            _acc(dpg_ref, first, _colsum(dq * ny))
            dny = dq * pg_ref[...]
            dy_ref[...] = _bf(ry * (dny - ny * jnp.mean(dny * ny, axis=-1, keepdims=True)))

    v = _vec_spec(d)
    row = _row_spec(tm, d)
    vec_out = jax.ShapeDtypeStruct((1, d), F32)
    in_specs = [row, row, row, v, v]
    args = [dh, dres, x, g, scale]
    out_specs = [row, v, v, v]
    out_shape = [jax.ShapeDtypeStruct((l, d), F32), vec_out, vec_out, vec_out]
    if with_post:
        in_specs += [row, v, v]
        args += list(post)
        out_specs += [row, v, v]
        out_shape += [jax.ShapeDtypeStruct((l, d), BF16), vec_out, vec_out]
    return pl.pallas_call(
        body, name=name, grid=(l // tm,), in_specs=in_specs, out_specs=out_specs, out_shape=out_shape,
        compiler_params=_params(("arbitrary",)),
    )(*args)


def _softplus_parts(z):
    e = jnp.exp(-jnp.abs(z))
    den = 1.0 + e
    lb = jnp.minimum(z, 0.0) - jnp.log(den)
    sig = jnp.where(z >= 0.0, 1.0, e) / den
    return lb, lb - z, sig


def _tri(cmp, n=HEAD):
    row = lax.broadcasted_iota(jnp.int32, (n, n), 0)
    col = lax.broadcasted_iota(jnp.int32, (n, n), 1)
    return cmp(row, col)


ATT_T = 256


def _split(x):
    hi = _bf(x)
    return hi, _bf(x - hi.astype(F32))


def _two(hi, lo, m):
    return _dot(hi, m) + _dot(lo, m)


def _attn_fwd(qkv, wb, name, hp=4):
    l = qkv.shape[0]
    t = ATT_T
    nh, nq = wb // HEAD, l // t
    hp = min(hp, nh)
    ng, wg = nh // hp, hp * HEAD
    scale = 1.0 / math.sqrt(HEAD)

    def body(q_ref, k_ref, v_ref, o_ref, lk_ref):
        i = pl.program_id(1)
        valid = _tri(lambda r, c: c < r, t)
        m_gt = _bf(_tri(lambda r, c: r > c, t).astype(F32))

        def tile(j, carry, diag):
            rows = pl.ds(pl.multiple_of(j * t, t), t)
            out = []
            for hh, (acc, run) in enumerate(carry):
                cs = slice(hh * HEAD, (hh + 1) * HEAD)
                z = _dot(q_ref[:, cs], k_ref[rows, cs], NT) * scale
                lb, lk, _ = _softplus_parts(z)
                if diag:
                    lk = jnp.where(valid, lk, 0.0)
                hi, lo = _split(lk)
                later = _two(hi, lo, m_gt)
                w = jnp.exp(lb + later + run)
                if diag:
                    w = jnp.where(valid, w, 0.0)
                out.append((acc + _dot(_bf(w), v_ref[rows, cs]), run + later[:, :1] + lk[:, :1]))
            return tuple(out)

        zero = (jnp.zeros((t, HEAD), F32), jnp.zeros((t, 1), F32))
        carry = tile(i, (zero,) * hp, True)
        carry = lax.fori_loop(0, i, lambda s, c: tile(i - 1 - s, c, False), carry)
        for hh, (acc, run) in enumerate(carry):
            cs = slice(hh * HEAD, (hh + 1) * HEAD)
            o_ref[:, cs] = acc
            lk_ref[:, cs] = jnp.broadcast_to(run, (t, HEAD))

    blk = lambda off: pl.BlockSpec((t, wg), lambda h, i: (i, off + h))
    full = lambda off: pl.BlockSpec((l, wg), lambda h, i: (0, off + h))
    out = pl.BlockSpec((t, wg), lambda h, i: (i, h))
    return pl.pallas_call(
        body, name=name, grid=(ng, nq),
        in_specs=[blk(0), full(ng), full(2 * ng)], out_specs=[out, out],
        out_shape=[jax.ShapeDtypeStruct((l, wb), F32), jax.ShapeDtypeStruct((l, wb), F32)],
        compiler_params=_params(("parallel", "arbitrary")),
    )(qkv, qkv, qkv)


def _attn_bwd(qkv, proj, dcat, lktot, wa, wb, name, hp=2):
    l = qkv.shape[0]
    t = ATT_T
    nh, nq = wb // HEAD, l // t
    hp = min(hp, nh)
    ng, wg = nh // hp, hp * HEAD
    scale = 1.0 / math.sqrt(HEAD)

    def body(q_ref, k_ref, v_ref, bz_ref, dc_ref, lt_ref, dq_ref, dkt_ref, dvt_ref, do_s, qt_s, dot_s):
        i = pl.program_id(1)

        @pl.when(i == 0)
        def _():
            dkt_ref[...] = jnp.zeros_like(dkt_ref)
            dvt_ref[...] = jnp.zeros_like(dvt_ref)

        do = dc_ref[...] * _silu(bz_ref[...])
        do_s[...] = _bf(do)
        for hh in range(hp):
            cs = slice(hh * HEAD, (hh + 1) * HEAD)
            qt_s[hh] = _bf(q_ref[:, cs].astype(F32).T)
            dot_s[hh] = _bf(do[:, cs].T)
        valid = _tri(lambda r, c: c < r, t)
        m_le = _bf(_tri(lambda r, c: r <= c, t).astype(F32))
        m_lt = _bf(_tri(lambda r, c: r < c, t).astype(F32))

        def tile(j, carry, diag):
            rows = pl.ds(pl.multiple_of(j * t, t), t)
            out = []
            for hh, (dq, cpre, ppre) in enumerate(carry):
                cs = slice(hh * HEAD, (hh + 1) * HEAD)
                kb = k_ref[rows, cs]
                z = _dot(q_ref[:, cs], kb, NT) * scale
                lb, lk, sig = _softplus_parts(z)
                if diag:
                    lk = jnp.where(valid, lk, 0.0)
                hi, lo = _split(lk)
                pin = _two(hi, lo, m_le)
                w = jnp.exp(lb + (lt_ref[:, hh * HEAD:hh * HEAD + 1] - cpre) - pin)
                if diag:
                    w = jnp.where(valid, w, 0.0)
                da = _dot(do_s[:, cs], v_ref[rows, cs], NT) * w
                pex = _dot(_bf(da), m_lt)
                dz = (da - sig * (da + ppre + pex)) * scale
                if diag:
                    dz = jnp.where(valid, dz, 0.0)
                dz_bf = _bf(dz)
                dkt_ref[hh, j] += _dot(qt_s[hh], dz_bf)
                dvt_ref[hh, j] += _dot(dot_s[hh], _bf(w))
                out.append((dq + _dot(dz_bf, kb), cpre + pin[:, t - 1:], ppre + pex[:, t - 1:] + da[:, t - 1:]))
            return tuple(out)

        zero = (jnp.zeros((t, HEAD), F32), jnp.zeros((t, 1), F32), jnp.zeros((t, 1), F32))
        carry = lax.fori_loop(0, i, lambda j, c: tile(j, c, False), (zero,) * hp)
        carry = tile(i, carry, True)
        for hh in range(hp):
            dq_ref[:, hh * HEAD:(hh + 1) * HEAD] = carry[hh][0]

    blk = lambda off: pl.BlockSpec((t, wg), lambda h, i: (i, off + h))
    full = lambda off: pl.BlockSpec((l, wg), lambda h, i: (0, off + h))
    acc = pl.BlockSpec((hp, nq, HEAD, t), lambda h, i: (h, 0, 0, 0))
    acc_shape = jax.ShapeDtypeStruct((nh, nq, HEAD, t), F32)
    dq, dkt, dvt = pl.pallas_call(
        body, name=name, grid=(ng, nq),
        in_specs=[blk(0), full(ng), full(2 * ng), blk(3 * wa // wg), blk(wa // wg), blk(0)],
        out_specs=[blk(0), acc, acc], out_shape=[jax.ShapeDtypeStruct((l, wb), F32), acc_shape, acc_shape],
        scratch_shapes=[pltpu.VMEM((t, wg), BF16), pltpu.VMEM((hp, HEAD, t), BF16), pltpu.VMEM((hp, HEAD, t), BF16)],
        compiler_params=_params(("parallel", "arbitrary")),
    )(qkv, qkv, qkv, proj, dcat, lktot)
    untranspose = lambda a: a.transpose(1, 3, 0, 2).reshape(l, wb)
    return dq, untranspose(dkt), untranspose(dvt)


def _sgu_heads(v, g_ref, w_ref, bt_ref, nh):
    keep = _tri(lambda r, c: r >= c)
    out = []
    for h in range(nh):
        cols = slice(h * HEAD, (h + 1) * HEAD)
        nv, r = _rownorm(v[:, cols])
        wm = jnp.where(keep, w_ref[h], 0.0)
        s = _dot(_bf(wm), _bf(nv * g_ref[:, cols])) + bt_ref[:, h:h + 1]
        out.append((nv, r, wm, s))
    return out


def _sgu_fwd(proj, out_b, norm_g, sgu_w, sgu_bt, wa, wb, name):
    l, n = proj.shape
    nh = wa // HEAD

    def body(au_ref, av_ref, az_ref, bz_ref, ob_ref, g_ref, w_ref, bt_ref, cat_ref):
        u, v, sz = _gelu(au_ref[...]), _gelu(av_ref[...]), _silu(az_ref[...])
        for h, (_, _, _, s) in enumerate(_sgu_heads(v, g_ref, w_ref, bt_ref, nh)):
            cols = slice(h * HEAD, (h + 1) * HEAD)
            cat_ref[:, cols] = _bf(u[:, cols] * s * sz[:, cols])
        cat_ref[:, wa:] = _bf(ob_ref[...] * _silu(bz_ref[...]))

    a_blk = lambda j: pl.BlockSpec((HEAD, wa), lambda i: (i, j))
    return pl.pallas_call(
        body, name=name, grid=(l // HEAD,),
        in_specs=[a_blk(0), a_blk(1), a_blk(2), a_blk(3), pl.BlockSpec((HEAD, wb), lambda i: (i, 0)),
                  _vec_spec(wa), pl.BlockSpec((nh, HEAD, HEAD), lambda i: (0, 0, 0)),
                  pl.BlockSpec((HEAD, nh), lambda i: (0, 0))],
        out_specs=pl.BlockSpec((HEAD, wa + wb), lambda i: (i, 0)),
        out_shape=jax.ShapeDtypeStruct((l, wa + wb), BF16),
        compiler_params=_params(("parallel",)),
    )(proj, proj, proj, proj, out_b, norm_g, sgu_w, sgu_bt)


def _sgu_bwd(proj, out_b, dcat, dq, dk, dv, norm_g, sgu_w, sgu_bt, wa, wb, name):
    l = proj.shape[0]
    n = 3 * wa + 4 * wb
    nh = wa // HEAD

    def body(au_ref, av_ref, az_ref, bz_ref, ob_ref, dc_ref, dq_ref, dk_ref, dv_ref, g_ref, w_ref, bt_ref,
             dp_ref, dw_ref, dbt_ref, dg_ref):
        first = pl.program_id(0) == 0
        keep = _tri(lambda r, c: r >= c)
        au, av, az = au_ref[...], av_ref[...], az_ref[...]
        u, v, sz = _gelu(au), _gelu(av), _silu(az)
        dgelu_u, dgelu_v, dsilu_z = _gelu_grad(au), _gelu_grad(av), _silu_grad(az)
        dg_parts = []
        for h, (nv, r, wm, s) in enumerate(_sgu_heads(v, g_ref, w_ref, bt_ref, nh)):
            cols = slice(h * HEAD, (h + 1) * HEAD)
            dca, uh, szh, gh = dc_ref[:, cols], u[:, cols], sz[:, cols], g_ref[:, cols]
            dp_ref[:, cols] = _bf(dca * s * szh * dgelu_u[:, cols])
            dp_ref[:, 2 * wa + h * HEAD:2 * wa + (h + 1) * HEAD] = _bf(dca * uh * s * dsilu_z[:, cols])
            ds = dca * uh * szh
            _acc(dw_ref.at[h], first, jnp.where(keep, _dot(_bf(ds), _bf(nv * gh), NT), 0.0))
            _acc(dbt_ref.at[:, h:h + 1], first, jnp.sum(ds, axis=1, keepdims=True))
            dvh = _dot(_bf(wm.T), _bf(ds))
            dg_parts.append(_colsum(dvh * nv))
            dnv = dvh * gh
            dvv = r * (dnv - nv * jnp.mean(dnv * nv, axis=-1, keepdims=True))
            dp_ref[:, wa + h * HEAD:wa + (h + 1) * HEAD] = _bf(dvv * dgelu_v[:, cols])
        _acc(dg_ref, first, jnp.concatenate(dg_parts, axis=1))
        base = 3 * wa
        dp_ref[:, base:base + wb] = _bf(dq_ref[...])
        dp_ref[:, base + wb:base + 2 * wb] = _bf(dk_ref[...])
        dp_ref[:, base + 2 * wb:base + 3 * wb] = _bf(dv_ref[...])
        dp_ref[:, base + 3 * wb:] = _bf(dc_ref[:, wa:] * ob_ref[...] * _silu_grad(bz_ref[...]))

    a_blk = lambda j: pl.BlockSpec((HEAD, wa), lambda i: (i, j))
    b_blk = pl.BlockSpec((HEAD, wb), lambda i: (i, 0))
    w_spec = pl.BlockSpec((nh, HEAD, HEAD), lambda i: (0, 0, 0))
    bt_spec = pl.BlockSpec((HEAD, nh), lambda i: (0, 0))
    return pl.pallas_call(
        body, name=name, grid=(l // HEAD,),
        in_specs=[a_blk(0), a_blk(1), a_blk(2), a_blk(3), b_blk, pl.BlockSpec((HEAD, wa + wb), lambda i: (i, 0)),
                  b_blk, b_blk, b_blk, _vec_spec(wa), w_spec, bt_spec],
        out_specs=[pl.BlockSpec((HEAD, n), lambda i: (i, 0)), w_spec, bt_spec, _vec_spec(wa)],
        out_shape=[jax.ShapeDtypeStruct((l, n), BF16), jax.ShapeDtypeStruct((nh, HEAD, HEAD), F32),
                   jax.ShapeDtypeStruct((HEAD, nh), F32), jax.ShapeDtypeStruct((1, wa), F32)],
        compiler_params=_params(("arbitrary",)),
    )(proj, proj, proj, proj, out_b, dcat, dq, dk, dv, norm_g, sgu_w, sgu_bt)


def _ssm_discretise(lr, li, ldt, br, bi):
    dt = jnp.exp(ldt)
    mag = jnp.exp(lr * dt)
    a_re = mag * jnp.cos(li * dt)
    a_im = mag * jnp.sin(li * dt)
    den = lr * lr + li * li
    nr = a_re - 1.0
    coef_re = (nr * lr + a_im * li) / den
    coef_im = (a_im * lr - nr * li) / den
    return a_re, a_im, coef_re * br - coef_im * bi, coef_re * bi + coef_im * br


def _ssm_prep(lr, li, ldt, br, bi, lr_row, li_row, ldt_row, name):
    s, c = br.shape

    def body(lr_ref, li_ref, ldt_ref, br_ref, bi_ref, lrr_ref, lir_ref, ldtr_ref, bbr_ref, bbi_ref, pr_ref, pi_ref):
        _, _, bbr, bbi = _ssm_discretise(lr_ref[...], li_ref[...], ldt_ref[...], br_ref[...], bi_ref[...])
        bbr_ref[...] = bbr
        bbi_ref[...] = bbi
        n = jnp.left_shift(1, lax.broadcasted_iota(jnp.int32, (8, 1), 0)).astype(F32)
        dt = jnp.exp(ldtr_ref[...])
        mag = jnp.exp(n * (lrr_ref[...] * dt))
        ang = n * (lir_ref[...] * dt)
        pr_ref[...] = mag * jnp.cos(ang)
        pi_ref[...] = mag * jnp.sin(ang)

    col = jax.ShapeDtypeStruct((s, c), F32)
    row = jax.ShapeDtypeStruct((8, s), F32)
    return pl.pallas_call(body, name=name, out_shape=[col, col, row, row])(
        lr, li, ldt, br, bi, lr_row, li_row, ldt_row)


def _ssm_prep_bwd(lr, li, ldt, br, bi, da_re, da_im, dbb_re, dbb_im, p, name):
    s, c = br.shape

    def body(lr_ref, li_ref, ldt_ref, br_ref, bi_ref, dar_ref, dai_ref, dbr_ref, dbi_ref,
             dlr_ref, dli_ref, dldt_ref, dbre_ref, dbim_ref):
        args = (lr_ref[...], li_ref[...], ldt_ref[...], br_ref[...], bi_ref[...])
        _, vjp = jax.vjp(_ssm_discretise, *args)
        dlr, dli, dldt, dbr, dbi = vjp((dar_ref[...], dai_ref[...], dbr_ref[...], dbi_ref[...]))
        dlr_ref[...] = dlr
        dli_ref[...] = dli
        dbre_ref[...] = dbr
        dbim_ref[...] = dbi
        idx = lax.broadcasted_iota(jnp.int32, (s, s // p), 0)
        grp = lax.broadcasted_iota(jnp.int32, (s, s // p), 1)
        own = (idx >= grp * p) & (idx < (grp + 1) * p)
        dldt_ref[...] = _colsum(jnp.where(own, dldt, 0.0))

    col1 = jax.ShapeDtypeStruct((s, 1), F32)
    colc = jax.ShapeDtypeStruct((s, c), F32)
    return pl.pallas_call(
        body, name=name, out_shape=[col1, col1, jax.ShapeDtypeStruct((1, s // p), F32), colc, colc],
    )(lr, li, ldt, br, bi, da_re, da_im, dbb_re, dbb_im)


def _shift_rows(x, s, up):
    t = x.shape[0]
    row = lax.broadcasted_iota(jnp.int32, x.shape, 0)
    if up:
        return jnp.where(row < t - s, pltpu.roll(x, t - s, 0), 0.0)
    return jnp.where(row >= s, pltpu.roll(x, s, 0), 0.0)


def _scan_rows(xr, xi, pr_ref, pi_ref, reverse):
    t = xr.shape[0]
    for k in range(int(math.log2(t))):
        ar, ai = pr_ref[k:k + 1, :], pi_ref[k:k + 1, :]
        if reverse:
            ai = -ai
        sr, si = _shift_rows(xr, 1 << k, reverse), _shift_rows(xi, 1 << k, reverse)
        xr, xi = xr + ar * sr - ai * si, xi + ar * si + ai * sr
    return xr, xi


def _first_row(x, val):
    row = lax.broadcasted_iota(jnp.int32, x.shape, 0)
    return jnp.where(row == 0, val, x)


def _last_row(x, val):
    row = lax.broadcasted_iota(jnp.int32, x.shape, 0)
    return jnp.where(row == x.shape[0] - 1, val, x)


def _ssm_states(u_bf, bbd, pr_ref, pi_ref, hr0, hi0, ns):
    bu = _dot(u_bf, bbd)
    xr, xi = bu[:, :ns], bu[:, ns:]
    ar, ai = pr_ref[0:1, :], pi_ref[0:1, :]
    xr = _first_row(xr, xr[0:1, :] + ar * hr0 - ai * hi0)
    xi = _first_row(xi, xi[0:1, :] + ar * hi0 + ai * hr0)
    return _scan_rows(xr, xi, pr_ref, pi_ref, False)


def _ssm_fwd(proj, bbd, ccd, pw_re, pw_im, d_skip, w, name):
    l = proj.shape[0]
    nb, cw, ns2 = bbd.shape
    ns = ns2 // 2
    nc = l // SSM_T

    def body(u_ref, bbd_ref, ccd_ref, pr_ref, pi_ref, d_ref, y_ref, hsr_ref, hsi_ref, hr_s, hi_s):
        @pl.when(pl.program_id(1) == 0)
        def _():
            hr_s[...] = jnp.zeros_like(hr_s)
            hi_s[...] = jnp.zeros_like(hi_s)

        hsr_ref[...] = hr_s[...].reshape(hsr_ref.shape)
        hsi_ref[...] = hi_s[...].reshape(hsi_ref.shape)
        u = u_ref[...]
        hr, hi = _ssm_states(_bf(u), bbd_ref[0], pr_ref, pi_ref, hr_s[...], hi_s[...], ns)
        hr_s[...] = hr[SSM_T - 1:, :]
        hi_s[...] = hi[SSM_T - 1:, :]
        y_ref[...] = _dot(_bf(jnp.concatenate([hr, hi], axis=1)), ccd_ref[0]) + d_ref[...] * u

    return pl.pallas_call(
        body, name=name, grid=(nb, nc),
        in_specs=[pl.BlockSpec((SSM_T, cw), lambda b, k: (k, b)),
                  pl.BlockSpec((1, cw, ns2), lambda b, k: (b, 0, 0)),
                  pl.BlockSpec((1, ns2, cw), lambda b, k: (b, 0, 0)),
                  pl.BlockSpec((8, ns), lambda b, k: (0, b)), pl.BlockSpec((8, ns), lambda b, k: (0, b)),
                  pl.BlockSpec((1, cw), lambda b, k: (0, b))],
        out_specs=[pl.BlockSpec((SSM_T, cw), lambda b, k: (k, b)),
                   pl.BlockSpec((1, 1, ns), lambda b, k: (k, 0, b)), pl.BlockSpec((1, 1, ns), lambda b, k: (k, 0, b))],
        out_shape=[jax.ShapeDtypeStruct((l, w), F32), jax.ShapeDtypeStruct((nc, 1, nb * ns), F32),
                   jax.ShapeDtypeStruct((nc, 1, nb * ns), F32)],
        scratch_shapes=[pltpu.VMEM((1, ns), F32), pltpu.VMEM((1, ns), F32)],
        compiler_params=_params(("parallel", "arbitrary")),
    )(proj, bbd, ccd, pw_re, pw_im, d_skip)


def _ssm_bwd(proj, dy, hs_re, hs_im, bbd, ccd, pw_re, pw_im, d_skip, w, name):
    l = proj.shape[0]
    nb, cw, ns2 = bbd.shape
    ns = ns2 // 2
    nc = l // SSM_T

    def body(u_ref, dy_ref, hsr_ref, hsi_ref, bbd_ref, ccd_ref, pr_ref, pi_ref, d_ref,
             du_ref, dbbd_ref, dccd_ref, dar_ref, dai_ref, dd_ref, gr_s, gi_s):
        first = pl.program_id(1) == 0

        @pl.when(first)
        def _():
            gr_s[...] = jnp.zeros_like(gr_s)
            gi_s[...] = jnp.zeros_like(gi_s)

        u, dy = u_ref[...], dy_ref[...]
        u_bf, dy_bf = _bf(u), _bf(dy)
        hr0, hi0 = hsr_ref[0], hsi_ref[0]
        hr, hi = _ssm_states(u_bf, bbd_ref[0], pr_ref, pi_ref, hr0, hi0, ns)
        dh = _dot(dy_bf, ccd_ref[0], NT)
        ar, ai = pr_ref[0:1, :], pi_ref[0:1, :]
        gcr, gci = gr_s[...], gi_s[...]
        xr, xi = dh[:, :ns], dh[:, ns:]
        xr = _last_row(xr, xr[SSM_T - 1:, :] + ar * gcr + ai * gci)
        xi = _last_row(xi, xi[SSM_T - 1:, :] + ar * gci - ai * gcr)
        gr, gi = _scan_rows(xr, xi, pr_ref, pi_ref, True)
        gr_s[...] = gr[0:1, :]
        gi_s[...] = gi[0:1, :]
        pr_h = _first_row(_shift_rows(hr, 1, False), hr0)
        pi_h = _first_row(_shift_rows(hi, 1, False), hi0)
        _acc(dar_ref, first, _colsum(pr_h * gr + pi_h * gi))
        _acc(dai_ref, first, _colsum(pr_h * gi - pi_h * gr))
        g_bf = _bf(jnp.concatenate([gr, gi], axis=1))
        _acc(dbbd_ref.at[0], first, _dot(_bf(u.T), g_bf))
        _acc(dccd_ref.at[0], first, _dot(_bf(jnp.concatenate([hr, hi], axis=1).T), dy_bf))
        du_ref[...] = _bf(_dot(g_bf, bbd_ref[0], NT) + d_ref[...] * dy)
        _acc(dd_ref, first, _colsum(dy * u))

    rev = lambda b, k: (nc - 1 - k, b)
    return pl.pallas_call(
        body, name=name, grid=(nb, nc),
        in_specs=[pl.BlockSpec((SSM_T, cw), rev), pl.BlockSpec((SSM_T, cw), rev),
                  pl.BlockSpec((1, 1, ns), lambda b, k: (nc - 1 - k, 0, b)),
                  pl.BlockSpec((1, 1, ns), lambda b, k: (nc - 1 - k, 0, b)),
                  pl.BlockSpec((1, cw, ns2), lambda b, k: (b, 0, 0)),
                  pl.BlockSpec((1, ns2, cw), lambda b, k: (b, 0, 0)),
                  pl.BlockSpec((8, ns), lambda b, k: (0, b)), pl.BlockSpec((8, ns), lambda b, k: (0, b)),
                  pl.BlockSpec((1, cw), lambda b, k: (0, b))],
        out_specs=[pl.BlockSpec((SSM_T, cw), rev),
                   pl.BlockSpec((1, cw, ns2), lambda b, k: (b, 0, 0)),
                   pl.BlockSpec((1, ns2, cw), lambda b, k: (b, 0, 0)),
                   pl.BlockSpec((1, ns), lambda b, k: (0, b)), pl.BlockSpec((1, ns), lambda b, k: (0, b)),
                   pl.BlockSpec((1, cw), lambda b, k: (0, b))],
        out_shape=[jax.ShapeDtypeStruct((l, w), BF16), jax.ShapeDtypeStruct(bbd.shape, F32),
                   jax.ShapeDtypeStruct(ccd.shape, F32), jax.ShapeDtypeStruct((1, nb * ns), F32),
                   jax.ShapeDtypeStruct((1, nb * ns), F32), jax.ShapeDtypeStruct((1, w), F32)],
        scratch_shapes=[pltpu.VMEM((1, ns), F32), pltpu.VMEM((1, ns), F32)],
        compiler_params=_params(("parallel", "arbitrary")),
    )(proj, dy, hs_re, hs_im, bbd, ccd, pw_re, pw_im, d_skip)


def _block_diag_b(bb_re, bb_im, g, p, c):
    nb = g // SSM_GB
    eye = jnp.eye(SSM_GB, dtype=F32)

    def one(bb):
        t = bb.reshape(nb, SSM_GB, p, c).transpose(0, 1, 3, 2)
        return (t[:, :, :, None, :] * eye[None, :, None, :, None]).reshape(nb, SSM_GB * c, SSM_GB * p)

    return jnp.concatenate([one(bb_re), one(bb_im)], axis=2)


def _block_diag_c(c_re, c_im, g, p, c):
    nb = g // SSM_GB
    eye = jnp.eye(SSM_GB, dtype=F32)

    def one(cc):
        t = cc.reshape(nb, SSM_GB, c, p).transpose(0, 1, 3, 2)
        return (t[:, :, :, None, :] * eye[None, :, None, :, None]).reshape(nb, SSM_GB * p, SSM_GB * c)

    return jnp.concatenate([one(c_re), one(-c_im)], axis=1)


def _diag_of_b(dbbd, g, p, c):
    nb = g // SSM_GB
    t = dbbd.reshape(nb, SSM_GB, c, 2, SSM_GB, p)
    idx = jnp.arange(SSM_GB)
    d = t[:, idx, :, :, idx, :]
    d = d.transpose(1, 0, 3, 4, 2)
    return d[:, :, 0].reshape(g * p, c), d[:, :, 1].reshape(g * p, c)


def _diag_of_c(dccd, g, p, c):
    nb = g // SSM_GB
    t = dccd.reshape(nb, 2, SSM_GB, p, SSM_GB, c)
    idx = jnp.arange(SSM_GB)
    d = t[:, :, idx, :, idx, :]
    d = d.transpose(1, 0, 2, 4, 3)
    return d[:, :, 0].reshape(g, c, p), -d[:, :, 1].reshape(g, c, p)


def _glu_fwd(y, proj, w_glu, b_glu, name):
    l, w = y.shape
    tm = _tile(l, 256)

    def body(y_ref, z_ref, w_ref, b_ref, o_ref):
        g = _gelu(y_ref[...])
        t = _dot(_bf(g), w_ref[...]) + b_ref[...]
        o_ref[...] = _bf(g * _sigmoid(t) * _silu(z_ref[...]))

    return pl.pallas_call(
        body, name=name, grid=(l // tm,),
        in_specs=[_row_spec(tm, w), pl.BlockSpec((tm, w), lambda i: (i, 1)),
                  pl.BlockSpec((w, w), lambda i: (0, 0)), _vec_spec(w)],
        out_specs=_row_spec(tm, w), out_shape=jax.ShapeDtypeStruct((l, w), BF16),
        compiler_params=_params(("parallel",)),
    )(y, proj, w_glu, b_glu)


def _glu_bwd(do, y, proj, w_glu, b_glu, name):
    l, w = y.shape
    tm = _tile(l, 256)
    nsteps = l // tm

    def body(do_ref, y_ref, z_ref, w_ref, b_ref, dy_ref, dz_ref, dw_ref, db_ref, dw_acc):
        i = pl.program_id(0)
        first = i == 0
        yv, z, do = y_ref[...], z_ref[...], do_ref[...]
        g = _gelu(yv)
        g_bf = _bf(g)
        sg = _sigmoid(_dot(g_bf, w_ref[...]) + b_ref[...])
        dyy = do * _silu(z)
        dz_ref[...] = _bf(do * g * sg * _silu_grad(z))
        dt = dyy * g * sg * (1.0 - sg)
        dt_bf = _bf(dt)
        dg = dyy * sg + _dot(dt_bf, w_ref[...], NT)
        dy_ref[...] = dg * _gelu_grad(yv)
        _acc(dw_acc, first, _dot(_bf(g.T), dt_bf))
        _acc(db_ref, first, _colsum(dt))

        @pl.when(i == nsteps - 1)
        def _():
            dw_ref[...] = _bf(dw_acc[...])

    return pl.pallas_call(
        body, name=name, grid=(nsteps,),
        in_specs=[_row_spec(tm, w), _row_spec(tm, w), pl.BlockSpec((tm, w), lambda i: (i, 1)),
                  pl.BlockSpec((w, w), lambda i: (0, 0)), _vec_spec(w)],
        out_specs=[_row_spec(tm, w), _row_spec(tm, w), pl.BlockSpec((w, w), lambda i: (0, 0)), _vec_spec(w)],
        out_shape=[jax.ShapeDtypeStruct((l, w), F32), jax.ShapeDtypeStruct((l, w), BF16),
                   jax.ShapeDtypeStruct((w, w), BF16), jax.ShapeDtypeStruct((1, w), F32)],
        scratch_shapes=[pltpu.VMEM((w, w), F32)],
        compiler_params=_params(("arbitrary",)),
    )(do, y, proj, w_glu, b_glu)


MOD_ROWS = 128


def _mod_fwd(cond_pad, w_mod, b_shard, name):
    nl, d, ncol = w_mod.shape
    tn = _tile(ncol, 512)

    def body(c_ref, w_ref, b_ref, o_ref):
        o_ref[0] = _dot(_bf(c_ref[...]), _bf(w_ref[0])) + b_ref[0]

    return pl.pallas_call(
        body, name=name, grid=(nl, ncol // tn),
        in_specs=[pl.BlockSpec((MOD_ROWS, d), lambda a, j: (0, 0)),
                  pl.BlockSpec((1, d, tn), lambda a, j: (a, 0, j)),
                  pl.BlockSpec((1, 1, tn), lambda a, j: (a, 0, j))],
        out_specs=pl.BlockSpec((1, MOD_ROWS, tn), lambda a, j: (a, 0, j)),
        out_shape=jax.ShapeDtypeStruct((nl, MOD_ROWS, ncol), F32),
        compiler_params=_params(("parallel", "parallel")),
    )(cond_pad, w_mod, b_shard)


def _mod_bwd(cond_pad_t, dmod_pad, name):
    nl, _, ncol = dmod_pad.shape
    d = cond_pad_t.shape[0]
    tn = _tile(ncol, 512)

    def body(c_ref, dm_ref, o_ref):
        o_ref[0] = _dot(_bf(c_ref[...]), _bf(dm_ref[0]))

    return pl.pallas_call(
        body, name=name, grid=(nl, ncol // tn),
        in_specs=[pl.BlockSpec((d, MOD_ROWS), lambda a, j: (0, 0)),
                  pl.BlockSpec((1, MOD_ROWS, tn), lambda a, j: (a, 0, j))],
        out_specs=pl.BlockSpec((1, d, tn), lambda a, j: (a, 0, j)),
        out_shape=jax.ShapeDtypeStruct((nl, d, ncol), F32),
        compiler_params=_params(("parallel", "parallel")),
    )(cond_pad_t, dmod_pad)


def _silu_rows(c2d, name):
    def body(c_ref, o_ref):
        o_ref[...] = _silu(c_ref[...])

    return pl.pallas_call(body, name=name, out_shape=jax.ShapeDtypeStruct(c2d.shape, F32))(c2d)


def _sum_leading(x, name):
    n, r, c = x.shape
    tr = _tile(r, max(16, (1 << 20) // (4 * c)), 16 if r % 16 == 0 else 8)

    def body(x_ref, o_ref):
        acc = x_ref[0].astype(F32)
        for k in range(1, n):
            acc = acc + x_ref[k].astype(F32)
        o_ref[...] = acc

    return pl.pallas_call(
        body, name=name, grid=(r // tr,),
        in_specs=[pl.BlockSpec((n, tr, c), lambda i: (0, i, 0))], out_specs=pl.BlockSpec((tr, c), lambda i: (i, 0)),
        out_shape=jax.ShapeDtypeStruct((r, c), F32), compiler_params=_params(("parallel",)),
    )(x)


def _adamw(w, gs, m, v, name):
    r, c = w.shape
    tr = _tile(r, max(8, (1 << 19) // (4 * c)), 8)
    ng = len(gs)

    def body(*refs):
        w_ref, g_refs, m_ref, v_ref = refs[0], refs[1:1 + ng], refs[1 + ng], refs[2 + ng]
        g_ref, d_ref, nm_ref, nv_ref = refs[3 + ng:]
        g = g_refs[0][...]
        for extra in g_refs[1:]:
            g = g + extra[...]
        g_ref[...] = g
        nm = ADAM_B1 * m_ref[...] + (1.0 - ADAM_B1) * g
        nv = ADAM_B2 * v_ref[...] + (1.0 - ADAM_B2) * (g * g)
        nm_ref[...] = nm
        nv_ref[...] = nv
        m_hat = nm / (1.0 - ADAM_B1 ** ADAM_STEP)
        v_hat = nv / (1.0 - ADAM_B2 ** ADAM_STEP)
        d_ref[...] = -ADAM_LR * (m_hat / (jnp.sqrt(v_hat) + ADAM_EPS) + ADAM_WD * w_ref[...])

    spec = pl.BlockSpec((tr, c), lambda i: (i, 0))
    shp = jax.ShapeDtypeStruct((r, c), F32)
    return pl.pallas_call(
        body, name=name, grid=(r // tr,), in_specs=[spec] * (3 + ng), out_specs=[spec] * 4,
        out_shape=[shp] * 4, compiler_params=_params(("parallel",)),
    )(w, *gs, m, v)


ANY = pl.BlockSpec(memory_space=pl.ANY)


def _flip(v, bit):
    return 1 - v if bit else v


def _allgather8(x, name):
    def body(x_ref, o_ref, send_sems, recv_sems, local_sem):
        mx, my, mc = lax.axis_index("x"), lax.axis_index("y"), lax.axis_index("c")
        me = 4 * mx + 2 * my + mc
        mine = pltpu.make_async_copy(x_ref, o_ref.at[me], local_sem)
        mine.start()

        def copy(j):
            peer = (_flip(mx, j & 4), _flip(my, j & 2), _flip(mc, j & 1))
            return pltpu.make_async_remote_copy(
                src_ref=x_ref, dst_ref=o_ref.at[me], send_sem=send_sems.at[j - 1], recv_sem=recv_sems.at[j - 1],
                device_id=peer, device_id_type=MESH)

        def landing(j):
            peer = (_flip(mx, j & 4), _flip(my, j & 2), _flip(mc, j & 1))
            slot = 4 * peer[0] + 2 * peer[1] + peer[2]
            return pltpu.make_async_remote_copy(
                src_ref=x_ref, dst_ref=o_ref.at[slot], send_sem=send_sems.at[j - 1], recv_sem=recv_sems.at[j - 1],
                device_id=peer, device_id_type=MESH)

        for j in range(1, 8):
            copy(j).start()
        for j in range(1, 8):
            landing(j).wait()
        mine.wait()

    return pl.pallas_call(
        body, name=name, in_specs=[ANY], out_specs=ANY,
        out_shape=jax.ShapeDtypeStruct((8,) + x.shape, x.dtype),
        scratch_shapes=[pltpu.SemaphoreType.DMA((7,)), pltpu.SemaphoreType.DMA((7,)), pltpu.SemaphoreType.DMA],
    )(x)


def _chip_exchange(xs, gather, name):
    n = len(xs)

    def body(*refs):
        x_refs, o_refs = refs[:n], refs[n:2 * n]
        send_sems, recv_sems, local_sems = refs[2 * n:]
        mx, my, mc = lax.axis_index("x"), lax.axis_index("y"), lax.axis_index("c")
        k0 = 2 * mx + my
        local = []
        for a in range(n):
            src = x_refs[a] if gather else x_refs[a].at[k0]
            cp = pltpu.make_async_copy(src, o_refs[a].at[k0], local_sems.at[a])
            cp.start()
            local.append(cp)

        def copy(a, j, outgoing):
            px, py = _flip(mx, j & 2), _flip(my, j & 1)
            kp = 2 * px + py
            if outgoing:
                src = x_refs[a] if gather else x_refs[a].at[kp]
                dst = o_refs[a].at[k0]
            else:
                src = x_refs[a] if gather else x_refs[a].at[k0]
                dst = o_refs[a].at[kp]
            s = a * 3 + j - 1
            return pltpu.make_async_remote_copy(
                src_ref=src, dst_ref=dst, send_sem=send_sems.at[s], recv_sem=recv_sems.at[s],
                device_id=(px, py, mc), device_id_type=MESH)

        for a in range(n):
            for j in range(1, 4):
                copy(a, j, True).start()
        for a in range(n):
            for j in range(1, 4):
                copy(a, j, False).wait()
        for cp in local:
            cp.wait()

    out_shape = [jax.ShapeDtypeStruct(((4,) + x.shape) if gather else x.shape, x.dtype) for x in xs]
    return pl.pallas_call(
        body, name=name, in_specs=[ANY] * n, out_specs=[ANY] * n, out_shape=out_shape,
        scratch_shapes=[pltpu.SemaphoreType.DMA((3 * n,)), pltpu.SemaphoreType.DMA((3 * n,)),
                        pltpu.SemaphoreType.DMA((n,))],
    )(*xs)


def _sibling_exchange(xs, name):
    n = len(xs)

    def body(*refs):
        x_refs, o_refs = refs[:n], refs[n:2 * n]
        send_sems, recv_sems = refs[2 * n:]
        sib = (lax.axis_index("x"), lax.axis_index("y"), 1 - lax.axis_index("c"))
        copies = [pltpu.make_async_remote_copy(
            src_ref=x_refs[a], dst_ref=o_refs[a], send_sem=send_sems.at[a], recv_sem=recv_sems.at[a],
            device_id=sib, device_id_type=MESH) for a in range(n)]
        for cp in copies:
            cp.start()
        for cp in copies:
            cp.wait()

    return pl.pallas_call(
        body, name=name, in_specs=[ANY] * n, out_specs=[ANY] * n,
        out_shape=[jax.ShapeDtypeStruct(x.shape, x.dtype) for x in xs],
        scratch_shapes=[pltpu.SemaphoreType.DMA((n,)), pltpu.SemaphoreType.DMA((n,))],
    )(*xs)


PACK = 1024


def _pack(parts):
    flat = []
    for p in parts:
        v = p.reshape(-1).astype(F32)
        flat.append(jnp.pad(v, (0, (-v.shape[0]) % PACK)))
    return jnp.concatenate(flat).reshape(-1, 128)


def _unpack(packed, shapes):
    flat = packed.reshape(-1)
    out, off = [], 0
    for shp in shapes:
        n = math.prod(shp)
        out.append(flat[off:off + n].reshape(shp))
        off += n + (-n) % PACK
    return out


def kernel(x, c, ln_pre_g, ln_post_g, w_mod, b_mod, w_in_ab, w_out_ab, sgu_norm_g, sgu_w, sgu_b, w_in_ssm, w_out_ssm, lam_re, lam_im, b_re, b_im, c_re, c_im, d_skip, log_dt, w_glu, b_glu, loss_target, m_ln_pre_g, m_ln_post_g, m_w_mod, m_b_mod, m_w_in_ab, m_w_out_ab, m_sgu_norm_g, m_sgu_w, m_sgu_b, m_w_in_ssm, m_w_out_ssm, m_lam_re, m_lam_im, m_b_re, m_b_im, m_c_re, m_c_im, m_d_skip, m_log_dt, m_w_glu, m_b_glu, v_ln_pre_g, v_ln_post_g, v_w_mod, v_b_mod, v_w_in_ab, v_w_out_ab, v_sgu_norm_g, v_sgu_w, v_sgu_b, v_w_in_ssm, v_w_out_ssm, v_lam_re, v_lam_im, v_b_re, v_b_im, v_c_re, v_c_im, v_d_skip, v_log_dt, v_w_glu, v_b_glu):
    given = dict(locals())
    mx, my, mc = lax.axis_index("x"), lax.axis_index("y"), lax.axis_index("c")
    me = 4 * mx + 2 * my + mc
    chip = 2 * mx + my

    _, l, d = x.shape
    x2, tgt = x[0], loss_target[0]
    n_in = w_in_ab.shape[2] * 4
    wa = wb = n_in // 7
    w = w_out_ssm.shape[1]
    g, p, cch = b_re.shape[1:]
    nmod = w_mod.shape[2]

    shards = [_bf(w_in_ab[0]), _bf(w_out_ab[0]), _bf(w_in_ssm[0]), _bf(w_out_ssm[0]), _bf(w_glu[0]),
              d_skip, b_glu]
    gw_in_ab, gw_out_ab, gw_in_ssm, gw_out_ssm, gw_glu, g_dskip, g_bglu = _chip_exchange(
        shards, True, "gather_weights")
    win_ab = jnp.concatenate([gw_in_ab[k] for k in range(4)], axis=1)
    wout_ab = gw_out_ab.reshape(wa + wb, d)
    win_ssm = gw_in_ssm.reshape(d, 2 * w)
    wout_ssm = jnp.concatenate([gw_out_ssm[k] for k in range(4)], axis=1)
    wglu = gw_glu.reshape(w, w)
    dskip_full = g_dskip.reshape(1, w)
    bglu_full = g_bglu.reshape(1, w)

    cond = _silu_rows(c.reshape(d // 128, 128), "cond_silu")
    cond_all = _allgather8(cond, "gather_cond").reshape(8, d)
    b_shard = lax.dynamic_slice(b_mod, (0, chip * nmod), (2, nmod)).reshape(2, 1, nmod)
    cond_pad = jnp.pad(cond_all, ((0, MOD_ROWS - 8), (0, 0)))
    modp = _mod_fwd(cond_pad, w_mod, b_shard, "mod_fwd")[:, :8]
    modp_all = _allgather8(modp.reshape(16, nmod), "gather_mod").reshape(4, 2, 2, 8, nmod)
    mine = lax.dynamic_index_in_dim(lax.dynamic_index_in_dim(modp_all, mc, 1, False), me, 2, False)
    mod = mine.transpose(1, 0, 2).reshape(2, 3 * d)
    shift = [mod[a:a + 1, :d] for a in range(2)]
    scale = [mod[a:a + 1, d:2 * d] for a in range(2)]
    gate = [mod[a:a + 1, 2 * d:] for a in range(2)]
    pre_g = [ln_pre_g[a:a + 1] for a in range(2)]
    post_g = [ln_post_g[a:a + 1] for a in range(2)]

    sgu_w0, sgu_bt = sgu_w[0], sgu_b[0].T
    h0 = _pre_fwd(x2, pre_g[0], scale[0], shift[0], "pre0_fwd")
    w_gates = jnp.concatenate([win_ab[:, :3 * wa], win_ab[:, 3 * wa + 3 * wb:]], axis=1)
    proj0 = _matmul(h0, w_gates, "nn", F32, "proj0", tm=1024)
    qkv = _matmul(h0, win_ab[:, 3 * wa:3 * wa + 3 * wb], "nn", BF16, "proj0_qkv", tm=1024)
    out_b, lktot = _attn_fwd(qkv, wb, "attn_fwd")
    cat = _sgu_fwd(proj0, out_b, sgu_norm_g, sgu_w0, sgu_bt, wa, wb, "sgu_fwd")
    y0 = _matmul(cat, wout_ab, "nn", F32, "out0", tm=1024)
    x1, h1 = _post_pre_fwd(x2, y0, gate[0], post_g[0], pre_g[1], scale[1], shift[1], "post0_pre1_fwd")

    s = g * p
    lr_c, li_c = lam_re.reshape(s, 1), lam_im.reshape(s, 1)
    ldt_c = jnp.repeat(log_dt.reshape(g), p).reshape(s, 1)
    br_c, bi_c = b_re.reshape(s, cch), b_im.reshape(s, cch)
    bb_re, bb_im, pw_re, pw_im = _ssm_prep(lr_c, li_c, ldt_c, br_c, bi_c, lr_c.reshape(1, s), li_c.reshape(1, s),
                                           ldt_c.reshape(1, s), "ssm_prep")
    bbd = _bf(_block_diag_b(bb_re, bb_im, g, p, cch))
    ccd = _bf(_block_diag_c(c_re[0], c_im[0], g, p, cch))
    proj1 = _matmul(h1, win_ssm, "nn", F32, "proj1", tm=1024)
    y_ssm, hs_re, hs_im = _ssm_fwd(proj1, bbd, ccd, pw_re, pw_im, dskip_full, w, "ssm_fwd")
    o1 = _glu_fwd(y_ssm, proj1, wglu, bglu_full, "glu_fwd")
    y1 = _matmul(o1, wout_ssm, "nn", F32, "out1", tm=1024)
    loss_vec, dy1, dx2, dgate1, dpost1 = _post_loss(x1, y1, gate[1], post_g[1], tgt, "post1_loss")

    do1 = _matmul(dy1, wout_ssm, "nt", F32, "out1_dx", tm=1024)
    gr_wout_ssm = _matmul(o1, dy1, "tn", BF16, "out1_dw", tm=1024, tk=1024, n_split=4)
    dy_ssm, dz1, gr_wglu, gr_bglu = _glu_bwd(do1, y_ssm, proj1, wglu, bglu_full, "glu_bwd")
    du1, dbbd, dccd, da_re, da_im, gr_dskip = _ssm_bwd(proj1, dy_ssm, hs_re, hs_im, bbd, ccd, pw_re, pw_im,
                                                       dskip_full, w, "ssm_bwd")
    dproj1 = jnp.concatenate([du1, dz1], axis=1)
    dh1 = _matmul(dproj1, win_ssm, "nt", F32, "proj1_dx", tm=1024)
    gr_win_ssm = _matmul(h1, dproj1, "tn", BF16, "proj1_dw", tm=1024, tn=1024, tk=1024)
    dx1, dscale1, dshift1, dpre1, dy0, dgate0, dpost0 = _pre_bwd(
        dh1, dx2, x1, pre_g[1], scale[1], "pre1_post0_bwd", post=(y0, gate[0], post_g[0]))

    dcat = _matmul(dy0, wout_ab, "nt", F32, "out0_dx", tm=1024)
    gr_wout_ab = _matmul(cat, dy0, "tn", BF16, "out0_dw", tm=1024, tn=1024, tk=1024)
    dq, dk, dv = _attn_bwd(qkv, proj0, dcat, lktot, wa, wb, "attn_bwd")
    dproj0, gr_sgu_w, gr_sgu_bt, gr_sgu_g = _sgu_bwd(proj0, out_b, dcat, dq, dk, dv, sgu_norm_g, sgu_w0, sgu_bt,
                                                     wa, wb, "sgu_bwd")
    dh0 = _matmul(dproj0, win_ab, "nt", F32, "proj0_dx", tm=1024, tk=1792)
    gr_win_ab = _matmul(h0, dproj0, "tn", BF16, "proj0_dw", tm=1024, tk=1024, tn=896, n_split=4)
    grad_x, dscale0, dshift0, dpre0 = _pre_bwd(dh0, dx1, x2, pre_g[0], scale[0], "pre0_bwd")

    pieces = [gr_win_ab, gr_wout_ab.reshape(4, (wa + wb) // 4, d), gr_win_ssm.reshape(4, d // 4, 2 * w),
              gr_wout_ssm, gr_wglu.reshape(4, w // 4, w)]
    landed = _chip_exchange(pieces, False, "scatter_grads")
    big_names = ["w_in_ab", "w_out_ab", "w_in_ssm", "w_out_ssm", "w_glu"]
    sums = [_sum_leading(a, "sum_" + nm) for a, nm in zip(landed, big_names)]
    sib = _sibling_exchange(sums, "sibling_grads")
    results = {}
    for nm, s_mine, s_sib in zip(big_names, sums, sib):
        shp = given[nm].shape
        two_d = lambda a: a.reshape(-1, shp[-1])
        outs = _adamw(two_d(given[nm]), [s_mine, s_sib], two_d(given["m_" + nm]), two_d(given["v_" + nm]),
                      "adamw_" + nm)
        results[nm] = [o.reshape(shp) for o in outs]

    dbb_re, dbb_im = _diag_of_b(dbbd, g, p, cch)
    dc_re, dc_im = _diag_of_c(dccd, g, p, cch)
    dmod = jnp.concatenate([jnp.concatenate([dshift0, dscale0, dgate0], axis=1),
                            jnp.concatenate([dshift1, dscale1, dgate1], axis=1)], axis=0)
    partial = [loss_vec[:, :1], jnp.concatenate([dpre0, dpre1], 0), jnp.concatenate([dpost0, dpost1], 0), dmod,
               gr_sgu_g, gr_sgu_w, gr_sgu_bt.T, da_re, da_im, dbb_re, dbb_im, dc_re, dc_im, gr_dskip, gr_bglu]
    part_shapes = [a.shape for a in partial]
    packed = _pack(partial)
    gathered = _allgather8(packed, "gather_small")
    total = _sum_leading(gathered, "sum_small")
    (loss_s, g_pre, g_post, g_bmod, g_sgu_g, g_sgu_w, g_sgu_b, s_da_re, s_da_im, s_dbb_re, s_dbb_im, g_c_re, g_c_im,
     g_dskip_full, g_bglu_full) = _unpack(total, part_shapes)
    loss = loss_s.reshape(())

    g_lr, g_li, g_ldt, g_br, g_bi = _ssm_prep_bwd(lr_c, li_c, ldt_c, br_c, bi_c, s_da_re.reshape(s, 1),
                                                  s_da_im.reshape(s, 1), s_dbb_re, s_dbb_im, p, "ssm_prep_bwd")
    small = {
        "ln_pre_g": g_pre, "ln_post_g": g_post, "b_mod": g_bmod, "sgu_norm_g": g_sgu_g,
        "sgu_w": g_sgu_w.reshape(sgu_w.shape), "sgu_b": g_sgu_b.reshape(sgu_b.shape),
        "lam_re": g_lr.reshape(lam_re.shape), "lam_im": g_li.reshape(lam_im.shape),
        "b_re": g_br.reshape(b_re.shape), "b_im": g_bi.reshape(b_im.shape),
        "c_re": g_c_re.reshape(c_re.shape), "c_im": g_c_im.reshape(c_im.shape),
        "d_skip": lax.dynamic_slice(g_dskip_full, (0, chip * (w // 4)), (1, w // 4)),
        "log_dt": g_ldt.reshape(log_dt.shape),
        "b_glu": lax.dynamic_slice(g_bglu_full, (0, chip * (w // 4)), (1, w // 4)),
    }
    small_names = list(small)
    small_shapes = [small[nm].shape for nm in small_names]
    outs = _adamw(_pack([given[nm] for nm in small_names]), [_pack([small[nm] for nm in small_names])],
                  _pack([given["m_" + nm] for nm in small_names]), _pack([given["v_" + nm] for nm in small_names]),
                  "adamw_small")
    unpacked = [_unpack(o, small_shapes) for o in outs]
    for i, nm in enumerate(small_names):
        results[nm] = [small[nm]] + [unpacked[k][i] for k in range(1, 4)]

    off = sum(math.prod(sh) + (-math.prod(sh)) % PACK for sh in part_shapes[:3])
    dmod_rows = gathered.reshape(8, -1)[:, off:off + 6 * d].reshape(8, 2, 3 * d)
    dmod_shard = lax.dynamic_slice(dmod_rows, (0, 0, chip * nmod), (8, 2, nmod)).transpose(1, 0, 2)
    dmod_pad = jnp.pad(dmod_shard, ((0, 0), (0, MOD_ROWS - 8), (0, 0)))
    gr_wmod = _mod_bwd(cond_pad.T, dmod_pad, "mod_bwd")
    two_d = lambda a: a.reshape(-1, nmod)
    outs = _adamw(two_d(w_mod), [two_d(gr_wmod)], two_d(m_w_mod), two_d(v_w_mod), "adamw_w_mod")
    results["w_mod"] = [o.reshape(w_mod.shape) for o in outs]

    names = ["ln_pre_g", "ln_post_g", "w_mod", "b_mod", "w_in_ab", "w_out_ab", "sgu_norm_g", "sgu_w", "sgu_b",
             "w_in_ssm", "w_out_ssm", "lam_re", "lam_im", "b_re", "b_im", "c_re", "c_im", "d_skip", "log_dt",
             "w_glu", "b_glu"]
    return (loss, grad_x[None], *[results[nm][0] for nm in names], *[results[nm][1] for nm in names],
            *[results[nm][2] for nm in names], *[results[nm][3] for nm in names])
```

```python
import functools
import math

import jax
import jax.numpy as jnp
from jax import lax
from jax.experimental import pallas as pl
from jax.experimental.pallas import tpu as pltpu

F32 = jnp.float32
BF16 = jnp.bfloat16
MESH = pl.DeviceIdType.MESH

EPS = 1e-6
HEAD = 128
SSM_T = 128
SSM_GB = 16
ADAM_LR, ADAM_B1, ADAM_B2, ADAM_EPS, ADAM_WD, ADAM_STEP = 0.001, 0.9, 0.999, 1e-08, 0.01, 10
VMEM_LIMIT = 56 * 1024 * 1024

NN = (((1,), (0,)), ((), ()))
NT = (((1,), (1,)), ((), ()))
TN = (((0,), (0,)), ((), ()))


def _params(sem=None):
    return pltpu.CompilerParams(dimension_semantics=sem, vmem_limit_bytes=VMEM_LIMIT)


def _dot(a, b, dims=NN):
    return lax.dot_general(a, b, dims, preferred_element_type=F32)


def _bf(x):
    return x.astype(BF16)


def _gelu(x):
    k = math.sqrt(2.0 / math.pi)
    t = jnp.tanh(k * (x + 0.044715 * x * x * x))
    return 0.5 * x * (1.0 + t)


def _gelu_grad(x):
    k = math.sqrt(2.0 / math.pi)
    x2 = x * x
    t = jnp.tanh(k * (x + 0.044715 * x * x2))
    return 0.5 * (1.0 + t) + 0.5 * x * (1.0 - t * t) * k * (1.0 + 3.0 * 0.044715 * x2)


def _sigmoid(x):
    return 1.0 / (1.0 + jnp.exp(-x))


def _silu(x):
    return x * _sigmoid(x)


def _silu_grad(x):
    s = _sigmoid(x)
    return s * (1.0 + x * (1.0 - s))


def _tile(n, t, mult=128):
    if n <= t:
        return n
    for cand in range(t - t % mult, 0, -mult):
        if n % cand == 0:
            return cand
    raise ValueError((n, t, mult))


def _matmul(a, b, mode, out_dtype, name, tm=512, tn=512, tk=2048, n_split=1, ride=None):
    if mode == "nn":
        (m, kk), (_, n) = a.shape, b.shape
    elif mode == "nt":
        (m, kk), (n, _) = a.shape, b.shape
    else:
        (kk, m), (_, n) = a.shape, b.shape
    tm, tk = _tile(m, tm), _tile(kk, tk)
    ns = n // n_split
    tn = _tile(ns, tn)
    nk = kk // tk
    dims = {"nn": NN, "nt": NT, "tn": TN}[mode]

    def body(a_ref, b_ref, o_ref, acc_ref):
        k = pl.program_id(2)
        part = _dot(_bf(a_ref[...]), _bf(b_ref[...]), dims)

        @pl.when(k == 0)
        def _():
            acc_ref[...] = part

        @pl.when(k > 0)
        def _():
            acc_ref[...] += part

        @pl.when(k == nk - 1)
        def _():
            o_ref[...] = acc_ref[...].astype(out_dtype).reshape(o_ref.shape)

    if mode == "nn":
        a_spec = pl.BlockSpec((tm, tk), lambda i, j, k: (i, k))
        b_spec = pl.BlockSpec((tk, tn), lambda i, j, k: (k, j))
    elif mode == "nt":
        a_spec = pl.BlockSpec((tm, tk), lambda i, j, k: (i, k))
        b_spec = pl.BlockSpec((tn, tk), lambda i, j, k: (j, k))
    else:
        a_spec = pl.BlockSpec((tk, tm), lambda i, j, k: (k, i))
        b_spec = pl.BlockSpec((tk, tn), lambda i, j, k: (k, j))
    if n_split == 1:
        out_shape = jax.ShapeDtypeStruct((m, n), out_dtype)
        o_spec = pl.BlockSpec((tm, tn), lambda i, j, k: (i, j))
    else:
        per = ns // tn
        out_shape = jax.ShapeDtypeStruct((n_split, m, ns), out_dtype)
        o_spec = pl.BlockSpec((1, tm, tn), lambda i, j, k: (j // per, i, j % per))
    outs = _call(body, name=name, grid=(m // tm, n // tn, nk), in_specs=[a_spec, b_spec], out_specs=[o_spec],
                 out_shape=[out_shape], scratch_shapes=[pltpu.VMEM((tm, tn), F32)], args=(a, b),
                 sem=("parallel", "parallel", "arbitrary"), ride=ride)
    return outs[0] if ride is None else (outs[0], outs[1:])


def _row_spec(tm, d):
    return pl.BlockSpec((tm, d), lambda i: (i, 0))


def _vec_spec(d):
    return pl.BlockSpec((1, d), lambda i: (0, 0))


def _acc(ref, first, val):
    @pl.when(first)
    def _():
        ref[...] = val

    @pl.when(jnp.logical_not(first))
    def _():
        ref[...] += val


def _colsum(x):
    return jnp.sum(x, axis=0, keepdims=True)


def _rownorm(x):
    r = lax.rsqrt(jnp.mean(x * x, axis=-1, keepdims=True) + EPS)
    return x * r, r


def _pre_fwd(x, g, scale, shift, name):
    l, d = x.shape
    tm = _tile(l, 256)

    def body(x_ref, g_ref, sc_ref, sh_ref, h_ref):
        n, _ = _rownorm(x_ref[...])
        h_ref[...] = _bf(n * g_ref[...] * (1.0 + sc_ref[...]) + sh_ref[...])

    return pl.pallas_call(
        body, name=name, grid=(l // tm,),
        in_specs=[_row_spec(tm, d), _vec_spec(d), _vec_spec(d), _vec_spec(d)],
        out_specs=_row_spec(tm, d), out_shape=jax.ShapeDtypeStruct((l, d), BF16),
        compiler_params=_params(("parallel",)),
    )(x, g, scale, shift)


def _post_pre_fwd(x, y, gate, pg, g1, scale1, shift1, name):
    l, d = x.shape
    tm = _tile(l, 256)

    def body(x_ref, y_ref, gate_ref, pg_ref, g1_ref, sc_ref, sh_ref, x1_ref, h1_ref):
        ny, _ = _rownorm(y_ref[...])
        x1 = x_ref[...] + gate_ref[...] * (ny * pg_ref[...])
        x1_ref[...] = x1
        n1, _ = _rownorm(x1)
        h1_ref[...] = _bf(n1 * g1_ref[...] * (1.0 + sc_ref[...]) + sh_ref[...])

    v = _vec_spec(d)
    return pl.pallas_call(
        body, name=name, grid=(l // tm,),
        in_specs=[_row_spec(tm, d), _row_spec(tm, d), v, v, v, v, v],
        out_specs=[_row_spec(tm, d), _row_spec(tm, d)],
        out_shape=[jax.ShapeDtypeStruct((l, d), F32), jax.ShapeDtypeStruct((l, d), BF16)],
        compiler_params=_params(("parallel",)),
    )(x, y, gate, pg, g1, scale1, shift1)


def _post_loss(x1, y1, gate, pg, target, name):
    l, d = x1.shape
    tm = _tile(l, 256)

    def body(x_ref, y_ref, gate_ref, pg_ref, t_ref, loss_ref, dy_ref, dx_ref, dgate_ref, dpg_ref):
        first = pl.program_id(0) == 0
        y = y_ref[...]
        ny, ry = _rownorm(y)
        q = ny * pg_ref[...]
        x2 = x_ref[...] + gate_ref[...] * q
        e = x2 - t_ref[...]
        _acc(loss_ref, first, jnp.full((1, 128), 0.5 / d, F32) * jnp.sum(e * e))
        dx2 = e * (1.0 / d)
        dx_ref[...] = dx2
        _acc(dgate_ref, first, _colsum(dx2 * q))
        dq = dx2 * gate_ref[...]
        _acc(dpg_ref, first, _colsum(dq * ny))
        dny = dq * pg_ref[...]
        dy = ry * (dny - ny * jnp.mean(dny * ny, axis=-1, keepdims=True))
        dy_ref[...] = _bf(dy)

    v = _vec_spec(d)
    return pl.pallas_call(
        body, name=name, grid=(l // tm,),
        in_specs=[_row_spec(tm, d), _row_spec(tm, d), v, v, _row_spec(tm, d)],
        out_specs=[_vec_spec(128), _row_spec(tm, d), _row_spec(tm, d), v, v],
        out_shape=[jax.ShapeDtypeStruct((1, 128), F32), jax.ShapeDtypeStruct((l, d), BF16),
                   jax.ShapeDtypeStruct((l, d), F32), jax.ShapeDtypeStruct((1, d), F32),
                   jax.ShapeDtypeStruct((1, d), F32)],
        compiler_params=_params(("arbitrary",)),
    )(x1, y1, gate, pg, target)


def _pre_bwd(dh, dres, x, g, scale, name, post=None):
    l, d = x.shape
    tm = _tile(l, 256)
    with_post = post is not None

    def body(*refs):
        if with_post:
            (dh_ref, dres_ref, x_ref, g_ref, sc_ref, y_ref, gate_ref, pg_ref,
             dx_ref, dsc_ref, dsh_ref, dg_ref, dy_ref, dgate_ref, dpg_ref) = refs
        else:
            dh_ref, dres_ref, x_ref, g_ref, sc_ref, dx_ref, dsc_ref, dsh_ref, dg_ref = refs
        first = pl.program_id(0) == 0
        dh = dh_ref[...]
        n, r = _rownorm(x_ref[...])
        _acc(dsc_ref, first, _colsum(dh * (n * g_ref[...])))
        _acc(dsh_ref, first, _colsum(dh))
        dyn = dh * (1.0 + sc_ref[...])
        _acc(dg_ref, first, _colsum(dyn * n))
        dn = dyn * g_ref[...]
        dx = dres_ref[...] + r * (dn - n * jnp.mean(dn * n, axis=-1, keepdims=True))
        dx_ref[...] = dx
        if with_post:
            ny, ry = _rownorm(y_ref[...])
            _acc(dgate_ref, first, _colsum(dx * (ny * pg_ref[...])))
            dq = dx * gate_ref[...]
            _acc(dpg_ref, first, _colsum(dq * ny))
            dny = dq * pg_ref[...]
            dy_ref[...] = _bf(ry * (dny - ny * jnp.mean(dny * ny, axis=-1, keepdims=True)))

    v = _vec_spec(d)
    row = _row_spec(tm, d)
    vec_out = jax.ShapeDtypeStruct((1, d), F32)
    in_specs = [row, row, row, v, v]
    args = [dh, dres, x, g, scale]
    out_specs = [row, v, v, v]
    out_shape = [jax.ShapeDtypeStruct((l, d), F32), vec_out, vec_out, vec_out]
    if with_post:
        in_specs += [row, v, v]
        args += list(post)
        out_specs += [row, v, v]
        out_shape += [jax.ShapeDtypeStruct((l, d), BF16), vec_out, vec_out]
    return pl.pallas_call(
        body, name=name, grid=(l // tm,), in_specs=in_specs, out_specs=out_specs, out_shape=out_shape,
        compiler_params=_params(("arbitrary",)),
    )(*args)


def _softplus_parts(z):
    e = jnp.exp(-jnp.abs(z))
    den = 1.0 + e
    lb = jnp.minimum(z, 0.0) - jnp.log(den)
    sig = jnp.where(z >= 0.0, 1.0, e) / den
    return lb, lb - z, sig


def _tri(cmp, n=HEAD):
    row = lax.broadcasted_iota(jnp.int32, (n, n), 0)
    col = lax.broadcasted_iota(jnp.int32, (n, n), 1)
    return cmp(row, col)


ATT_T = 256


def _split(x):
    hi = _bf(x)
    return hi, _bf(x - hi.astype(F32))


def _two(hi, lo, m):
    return _dot(hi, m) + _dot(lo, m)


def _attn_fwd(qkv, wb, name, hp=4, ride=None):
    l = qkv.shape[0]
    t = ATT_T
    nh, nq = wb // HEAD, l // t
    hp = min(hp, nh)
    ng, wg = nh // hp, hp * HEAD
    scale = 1.0 / math.sqrt(HEAD)

    def body(q_ref, k_ref, v_ref, o_ref, lk_ref):
        i = pl.program_id(1)
        valid = _tri(lambda r, c: c < r, t)
        m_gt = _bf(_tri(lambda r, c: r > c, t).astype(F32))

        def tile(j, carry, diag):
            rows = pl.ds(pl.multiple_of(j * t, t), t)
            out = []
            for hh, (acc, run) in enumerate(carry):
                cs = slice(hh * HEAD, (hh + 1) * HEAD)
                z = _dot(q_ref[:, cs], k_ref[rows, cs], NT) * scale
                lb, lk, _ = _softplus_parts(z)
                if diag:
                    lk = jnp.where(valid, lk, 0.0)
                hi, lo = _split(lk)
                later = _two(hi, lo, m_gt)
                w = jnp.exp(lb + later + run)
                if diag:
                    w = jnp.where(valid, w, 0.0)
                out.append((acc + _dot(_bf(w), v_ref[rows, cs]), run + later[:, :1] + lk[:, :1]))
            return tuple(out)

        zero = (jnp.zeros((t, HEAD), F32), jnp.zeros((t, 1), F32))
        carry = tile(i, (zero,) * hp, True)
        carry = lax.fori_loop(0, i, lambda s, c: tile(i - 1 - s, c, False), carry)
        for hh, (acc, run) in enumerate(carry):
            cs = slice(hh * HEAD, (hh + 1) * HEAD)
            o_ref[:, cs] = acc
            lk_ref[:, cs] = jnp.broadcast_to(run, (t, HEAD))

    blk = lambda off: pl.BlockSpec((t, wg), lambda h, i: (i, off + h))
    full = lambda off: pl.BlockSpec((l, wg), lambda h, i: (0, off + h))
    out = pl.BlockSpec((t, wg), lambda h, i: (i, h))
    outs = _call(body, name=name, grid=(ng, nq), in_specs=[blk(0), full(ng), full(2 * ng)], out_specs=[out, out],
                 out_shape=[jax.ShapeDtypeStruct((l, wb), F32), jax.ShapeDtypeStruct((l, wb), F32)],
                 args=(qkv, qkv, qkv), sem=("parallel", "arbitrary"), ride=ride)
    return outs[0], outs[1], outs[2:]


def _attn_bwd(qkv, proj, dcat, lktot, wa, wb, name, hp=2, ride=None):
    l = qkv.shape[0]
    t = ATT_T
    nh, nq = wb // HEAD, l // t
    hp = min(hp, nh)
    ng, wg = nh // hp, hp * HEAD
    scale = 1.0 / math.sqrt(HEAD)

    def body(q_ref, k_ref, v_ref, bz_ref, dc_ref, lt_ref, dq_ref, dkt_ref, dvt_ref, do_s, qt_s, dot_s):
        i = pl.program_id(1)

        @pl.when(i == 0)
        def _():
            dkt_ref[...] = jnp.zeros_like(dkt_ref)
            dvt_ref[...] = jnp.zeros_like(dvt_ref)

        do = dc_ref[...] * _silu(bz_ref[...])
        do_s[...] = _bf(do)
        for hh in range(hp):
            cs = slice(hh * HEAD, (hh + 1) * HEAD)
            qt_s[hh] = _bf(q_ref[:, cs].astype(F32).T)
            dot_s[hh] = _bf(do[:, cs].T)
        valid = _tri(lambda r, c: c < r, t)
        m_le = _bf(_tri(lambda r, c: r <= c, t).astype(F32))
        m_lt = _bf(_tri(lambda r, c: r < c, t).astype(F32))

        def tile(j, carry, diag):
            rows = pl.ds(pl.multiple_of(j * t, t), t)
            out = []
            for hh, (dq, cpre, ppre) in enumerate(carry):
                cs = slice(hh * HEAD, (hh + 1) * HEAD)
                kb = k_ref[rows, cs]
                z = _dot(q_ref[:, cs], kb, NT) * scale
                lb, lk, sig = _softplus_parts(z)
                if diag:
                    lk = jnp.where(valid, lk, 0.0)
                hi, lo = _split(lk)
                pin = _two(hi, lo, m_le)
                w = jnp.exp(lb + (lt_ref[:, hh * HEAD:hh * HEAD + 1] - cpre) - pin)
                if diag:
                    w = jnp.where(valid, w, 0.0)
                da = _dot(do_s[:, cs], v_ref[rows, cs], NT) * w
                pex = _dot(_bf(da), m_lt)
                dz = (da - sig * (da + ppre + pex)) * scale
                if diag:
                    dz = jnp.where(valid, dz, 0.0)
                dz_bf = _bf(dz)
                dkt_ref[hh, j] += _dot(qt_s[hh], dz_bf)
                dvt_ref[hh, j] += _dot(dot_s[hh], _bf(w))
                out.append((dq + _dot(dz_bf, kb), cpre + pin[:, t - 1:], ppre + pex[:, t - 1:] + da[:, t - 1:]))
            return tuple(out)

        zero = (jnp.zeros((t, HEAD), F32), jnp.zeros((t, 1), F32), jnp.zeros((t, 1), F32))
        carry = lax.fori_loop(0, i, lambda j, c: tile(j, c, False), (zero,) * hp)
        carry = tile(i, carry, True)
        for hh in range(hp):
            dq_ref[:, hh * HEAD:(hh + 1) * HEAD] = carry[hh][0]

    blk = lambda off: pl.BlockSpec((t, wg), lambda h, i: (i, off + h))
    full = lambda off: pl.BlockSpec((l, wg), lambda h, i: (0, off + h))
    acc = pl.BlockSpec((hp, nq, HEAD, t), lambda h, i: (h, 0, 0, 0))
    acc_shape = jax.ShapeDtypeStruct((nh, nq, HEAD, t), F32)
    outs = _call(
        body, name=name, grid=(ng, nq),
        in_specs=[blk(0), full(ng), full(2 * ng), blk(3 * wa // wg), blk(wa // wg), blk(0)],
        out_specs=[blk(0), acc, acc], out_shape=[jax.ShapeDtypeStruct((l, wb), F32), acc_shape, acc_shape],
        scratch_shapes=[pltpu.VMEM((t, wg), BF16), pltpu.VMEM((hp, HEAD, t), BF16), pltpu.VMEM((hp, HEAD, t), BF16)],
        args=(qkv, qkv, qkv, proj, dcat, lktot), sem=("parallel", "arbitrary"), ride=ride)
    untranspose = lambda a: a.transpose(1, 3, 0, 2).reshape(l, wb)
    return outs[0], untranspose(outs[1]), untranspose(outs[2]), outs[3:]


def _sgu_heads(v, g_ref, w_ref, bt_ref, nh):
    keep = _tri(lambda r, c: r >= c)
    out = []
    for h in range(nh):
        cols = slice(h * HEAD, (h + 1) * HEAD)
        nv, r = _rownorm(v[:, cols])
        wm = jnp.where(keep, w_ref[h], 0.0)
        s = _dot(_bf(wm), _bf(nv * g_ref[:, cols])) + bt_ref[:, h:h + 1]
        out.append((nv, r, wm, s))
    return out


def _sgu_fwd(proj, out_b, norm_g, sgu_w, sgu_bt, wa, wb, name):
    l, n = proj.shape
    nh = wa // HEAD

    def body(au_ref, av_ref, az_ref, bz_ref, ob_ref, g_ref, w_ref, bt_ref, cat_ref):
        u, v, sz = _gelu(au_ref[...]), _gelu(av_ref[...]), _silu(az_ref[...])
        for h, (_, _, _, s) in enumerate(_sgu_heads(v, g_ref, w_ref, bt_ref, nh)):
            cols = slice(h * HEAD, (h + 1) * HEAD)
            cat_ref[:, cols] = _bf(u[:, cols] * s * sz[:, cols])
        cat_ref[:, wa:] = _bf(ob_ref[...] * _silu(bz_ref[...]))

    a_blk = lambda j: pl.BlockSpec((HEAD, wa), lambda i: (i, j))
    return pl.pallas_call(
        body, name=name, grid=(l // HEAD,),
        in_specs=[a_blk(0), a_blk(1), a_blk(2), a_blk(3), pl.BlockSpec((HEAD, wb), lambda i: (i, 0)),
                  _vec_spec(wa), pl.BlockSpec((nh, HEAD, HEAD), lambda i: (0, 0, 0)),
                  pl.BlockSpec((HEAD, nh), lambda i: (0, 0))],
        out_specs=pl.BlockSpec((HEAD, wa + wb), lambda i: (i, 0)),
        out_shape=jax.ShapeDtypeStruct((l, wa + wb), BF16),
        compiler_params=_params(("parallel",)),
    )(proj, proj, proj, proj, out_b, norm_g, sgu_w, sgu_bt)


def _sgu_bwd(proj, out_b, dcat, dq, dk, dv, norm_g, sgu_w, sgu_bt, wa, wb, name):
    l = proj.shape[0]
    n = 3 * wa + 4 * wb
    nh = wa // HEAD

    def body(au_ref, av_ref, az_ref, bz_ref, ob_ref, dc_ref, dq_ref, dk_ref, dv_ref, g_ref, w_ref, bt_ref,
             dp_ref, dw_ref, dbt_ref, dg_ref):
        first = pl.program_id(0) == 0
        keep = _tri(lambda r, c: r >= c)
        au, av, az = au_ref[...], av_ref[...], az_ref[...]
        u, v, sz = _gelu(au), _gelu(av), _silu(az)
        dgelu_u, dgelu_v, dsilu_z = _gelu_grad(au), _gelu_grad(av), _silu_grad(az)
        dg_parts = []
        for h, (nv, r, wm, s) in enumerate(_sgu_heads(v, g_ref, w_ref, bt_ref, nh)):
            cols = slice(h * HEAD, (h + 1) * HEAD)
            dca, uh, szh, gh = dc_ref[:, cols], u[:, cols], sz[:, cols], g_ref[:, cols]
            dp_ref[:, cols] = _bf(dca * s * szh * dgelu_u[:, cols])
            dp_ref[:, 2 * wa + h * HEAD:2 * wa + (h + 1) * HEAD] = _bf(dca * uh * s * dsilu_z[:, cols])
            ds = dca * uh * szh
            _acc(dw_ref.at[h], first, jnp.where(keep, _dot(_bf(ds), _bf(nv * gh), NT), 0.0))
            _acc(dbt_ref.at[:, h:h + 1], first, jnp.sum(ds, axis=1, keepdims=True))
            dvh = _dot(_bf(wm.T), _bf(ds))
            dg_parts.append(_colsum(dvh * nv))
            dnv = dvh * gh
            dvv = r * (dnv - nv * jnp.mean(dnv * nv, axis=-1, keepdims=True))
            dp_ref[:, wa + h * HEAD:wa + (h + 1) * HEAD] = _bf(dvv * dgelu_v[:, cols])
        _acc(dg_ref, first, jnp.concatenate(dg_parts, axis=1))
        base = 3 * wa
        dp_ref[:, base:base + wb] = _bf(dq_ref[...])
        dp_ref[:, base + wb:base + 2 * wb] = _bf(dk_ref[...])
        dp_ref[:, base + 2 * wb:base + 3 * wb] = _bf(dv_ref[...])
        dp_ref[:, base + 3 * wb:] = _bf(dc_ref[:, wa:] * ob_ref[...] * _silu_grad(bz_ref[...]))

    a_blk = lambda j: pl.BlockSpec((HEAD, wa), lambda i: (i, j))
    b_blk = pl.BlockSpec((HEAD, wb), lambda i: (i, 0))
    w_spec = pl.BlockSpec((nh, HEAD, HEAD), lambda i: (0, 0, 0))
    bt_spec = pl.BlockSpec((HEAD, nh), lambda i: (0, 0))
    return pl.pallas_call(
        body, name=name, grid=(l // HEAD,),
        in_specs=[a_blk(0), a_blk(1), a_blk(2), a_blk(3), b_blk, pl.BlockSpec((HEAD, wa + wb), lambda i: (i, 0)),
                  b_blk, b_blk, b_blk, _vec_spec(wa), w_spec, bt_spec],
        out_specs=[pl.BlockSpec((HEAD, n), lambda i: (i, 0)), w_spec, bt_spec, _vec_spec(wa)],
        out_shape=[jax.ShapeDtypeStruct((l, n), BF16), jax.ShapeDtypeStruct((nh, HEAD, HEAD), F32),
                   jax.ShapeDtypeStruct((HEAD, nh), F32), jax.ShapeDtypeStruct((1, wa), F32)],
        compiler_params=_params(("arbitrary",)),
    )(proj, proj, proj, proj, out_b, dcat, dq, dk, dv, norm_g, sgu_w, sgu_bt)


def _ssm_discretise(lr, li, ldt, br, bi):
    dt = jnp.exp(ldt)
    mag = jnp.exp(lr * dt)
    a_re = mag * jnp.cos(li * dt)
    a_im = mag * jnp.sin(li * dt)
    den = lr * lr + li * li
    nr = a_re - 1.0
    coef_re = (nr * lr + a_im * li) / den
    coef_im = (a_im * lr - nr * li) / den
    return a_re, a_im, coef_re * br - coef_im * bi, coef_re * bi + coef_im * br


def _ssm_prep(lr, li, ldt, br, bi, lr_row, li_row, ldt_row, name):
    s, c = br.shape

    def body(lr_ref, li_ref, ldt_ref, br_ref, bi_ref, lrr_ref, lir_ref, ldtr_ref, bbr_ref, bbi_ref, pr_ref, pi_ref):
        _, _, bbr, bbi = _ssm_discretise(lr_ref[...], li_ref[...], ldt_ref[...], br_ref[...], bi_ref[...])
        bbr_ref[...] = bbr
        bbi_ref[...] = bbi
        n = jnp.left_shift(1, lax.broadcasted_iota(jnp.int32, (8, 1), 0)).astype(F32)
        dt = jnp.exp(ldtr_ref[...])
        mag = jnp.exp(n * (lrr_ref[...] * dt))
        ang = n * (lir_ref[...] * dt)
        pr_ref[...] = mag * jnp.cos(ang)
        pi_ref[...] = mag * jnp.sin(ang)

    col = jax.ShapeDtypeStruct((s, c), F32)
    row = jax.ShapeDtypeStruct((8, s), F32)
    return pl.pallas_call(body, name=name, out_shape=[col, col, row, row])(
        lr, li, ldt, br, bi, lr_row, li_row, ldt_row)


def _ssm_prep_bwd(lr, li, ldt, br, bi, da_re, da_im, dbb_re, dbb_im, p, name):
    s, c = br.shape

    def body(lr_ref, li_ref, ldt_ref, br_ref, bi_ref, dar_ref, dai_ref, dbr_ref, dbi_ref,
             dlr_ref, dli_ref, dldt_ref, dbre_ref, dbim_ref):
        args = (lr_ref[...], li_ref[...], ldt_ref[...], br_ref[...], bi_ref[...])
        _, vjp = jax.vjp(_ssm_discretise, *args)
        dlr, dli, dldt, dbr, dbi = vjp((dar_ref[...], dai_ref[...], dbr_ref[...], dbi_ref[...]))
        dlr_ref[...] = dlr
        dli_ref[...] = dli
        dbre_ref[...] = dbr
        dbim_ref[...] = dbi
        idx = lax.broadcasted_iota(jnp.int32, (s, s // p), 0)
        grp = lax.broadcasted_iota(jnp.int32, (s, s // p), 1)
        own = (idx >= grp * p) & (idx < (grp + 1) * p)
        dldt_ref[...] = _colsum(jnp.where(own, dldt, 0.0))

    col1 = jax.ShapeDtypeStruct((s, 1), F32)
    colc = jax.ShapeDtypeStruct((s, c), F32)
    return pl.pallas_call(
        body, name=name, out_shape=[col1, col1, jax.ShapeDtypeStruct((1, s // p), F32), colc, colc],
    )(lr, li, ldt, br, bi, da_re, da_im, dbb_re, dbb_im)


def _shift_rows(x, s, up):
    t = x.shape[0]
    row = lax.broadcasted_iota(jnp.int32, x.shape, 0)
    if up:
        return jnp.where(row < t - s, pltpu.roll(x, t - s, 0), 0.0)
    return jnp.where(row >= s, pltpu.roll(x, s, 0), 0.0)


def _scan_rows(xr, xi, pr_ref, pi_ref, reverse):
    t = xr.shape[0]
    for k in range(int(math.log2(t))):
        ar, ai = pr_ref[k:k + 1, :], pi_ref[k:k + 1, :]
        if reverse:
            ai = -ai
        sr, si = _shift_rows(xr, 1 << k, reverse), _shift_rows(xi, 1 << k, reverse)
        xr, xi = xr + ar * sr - ai * si, xi + ar * si + ai * sr
    return xr, xi


def _first_row(x, val):
    row = lax.broadcasted_iota(jnp.int32, x.shape, 0)
    return jnp.where(row == 0, val, x)


def _last_row(x, val):
    row = lax.broadcasted_iota(jnp.int32, x.shape, 0)
    return jnp.where(row == x.shape[0] - 1, val, x)


def _ssm_states(u_bf, bbd, pr_ref, pi_ref, hr0, hi0, ns):
    bu = _dot(u_bf, bbd)
    xr, xi = bu[:, :ns], bu[:, ns:]
    ar, ai = pr_ref[0:1, :], pi_ref[0:1, :]
    xr = _first_row(xr, xr[0:1, :] + ar * hr0 - ai * hi0)
    xi = _first_row(xi, xi[0:1, :] + ar * hi0 + ai * hr0)
    return _scan_rows(xr, xi, pr_ref, pi_ref, False)


def _ssm_fwd(proj, bbd, ccd, pw_re, pw_im, d_skip, w, name):
    l = proj.shape[0]
    nb, cw, ns2 = bbd.shape
    ns = ns2 // 2
    nc = l // SSM_T

    def body(u_ref, bbd_ref, ccd_ref, pr_ref, pi_ref, d_ref, y_ref, hsr_ref, hsi_ref, hr_s, hi_s):
        @pl.when(pl.program_id(1) == 0)
        def _():
            hr_s[...] = jnp.zeros_like(hr_s)
            hi_s[...] = jnp.zeros_like(hi_s)

        hsr_ref[...] = hr_s[...].reshape(hsr_ref.shape)
        hsi_ref[...] = hi_s[...].reshape(hsi_ref.shape)
        u = u_ref[...]
        hr, hi = _ssm_states(_bf(u), bbd_ref[0], pr_ref, pi_ref, hr_s[...], hi_s[...], ns)
        hr_s[...] = hr[SSM_T - 1:, :]
        hi_s[...] = hi[SSM_T - 1:, :]
        y_ref[...] = _dot(_bf(jnp.concatenate([hr, hi], axis=1)), ccd_ref[0]) + d_ref[...] * u

    return pl.pallas_call(
        body, name=name, grid=(nb, nc),
        in_specs=[pl.BlockSpec((SSM_T, cw), lambda b, k: (k, b)),
                  pl.BlockSpec((1, cw, ns2), lambda b, k: (b, 0, 0)),
                  pl.BlockSpec((1, ns2, cw), lambda b, k: (b, 0, 0)),
                  pl.BlockSpec((8, ns), lambda b, k: (0, b)), pl.BlockSpec((8, ns), lambda b, k: (0, b)),
                  pl.BlockSpec((1, cw), lambda b, k: (0, b))],
        out_specs=[pl.BlockSpec((SSM_T, cw), lambda b, k: (k, b)),
                   pl.BlockSpec((1, 1, ns), lambda b, k: (k, 0, b)), pl.BlockSpec((1, 1, ns), lambda b, k: (k, 0, b))],
        out_shape=[jax.ShapeDtypeStruct((l, w), F32), jax.ShapeDtypeStruct((nc, 1, nb * ns), F32),
                   jax.ShapeDtypeStruct((nc, 1, nb * ns), F32)],
        scratch_shapes=[pltpu.VMEM((1, ns), F32), pltpu.VMEM((1, ns), F32)],
        compiler_params=_params(("parallel", "arbitrary")),
    )(proj, bbd, ccd, pw_re, pw_im, d_skip)


def _ssm_bwd(proj, dy, hs_re, hs_im, bbd, ccd, pw_re, pw_im, d_skip, w, name, ride=None):
    l = proj.shape[0]
    nb, cw, ns2 = bbd.shape
    ns = ns2 // 2
    nc = l // SSM_T

    def body(u_ref, dy_ref, hsr_ref, hsi_ref, bbd_ref, ccd_ref, pr_ref, pi_ref, d_ref,
             du_ref, dbbd_ref, dccd_ref, dar_ref, dai_ref, dd_ref, gr_s, gi_s):
        first = pl.program_id(1) == 0

        @pl.when(first)
        def _():
            gr_s[...] = jnp.zeros_like(gr_s)
            gi_s[...] = jnp.zeros_like(gi_s)

        u, dy = u_ref[...], dy_ref[...]
        u_bf, dy_bf = _bf(u), _bf(dy)
        hr0, hi0 = hsr_ref[0], hsi_ref[0]
        hr, hi = _ssm_states(u_bf, bbd_ref[0], pr_ref, pi_ref, hr0, hi0, ns)
        dh = _dot(dy_bf, ccd_ref[0], NT)
        ar, ai = pr_ref[0:1, :], pi_ref[0:1, :]
        gcr, gci = gr_s[...], gi_s[...]
        xr, xi = dh[:, :ns], dh[:, ns:]
        xr = _last_row(xr, xr[SSM_T - 1:, :] + ar * gcr + ai * gci)
        xi = _last_row(xi, xi[SSM_T - 1:, :] + ar * gci - ai * gcr)
        gr, gi = _scan_rows(xr, xi, pr_ref, pi_ref, True)
        gr_s[...] = gr[0:1, :]
        gi_s[...] = gi[0:1, :]
        pr_h = _first_row(_shift_rows(hr, 1, False), hr0)
        pi_h = _first_row(_shift_rows(hi, 1, False), hi0)
        _acc(dar_ref, first, _colsum(pr_h * gr + pi_h * gi))
        _acc(dai_ref, first, _colsum(pr_h * gi - pi_h * gr))
        g_bf = _bf(jnp.concatenate([gr, gi], axis=1))
        _acc(dbbd_ref.at[0], first, _dot(_bf(u.T), g_bf))
        _acc(dccd_ref.at[0], first, _dot(_bf(jnp.concatenate([hr, hi], axis=1).T), dy_bf))
        du_ref[...] = _bf(_dot(g_bf, bbd_ref[0], NT) + d_ref[...] * dy)
        _acc(dd_ref, first, _colsum(dy * u))

    rev = lambda b, k: (nc - 1 - k, b)
    outs = _call(
        body, name=name, grid=(nb, nc), ride=ride, sem=("parallel", "arbitrary"),
        args=(proj, dy, hs_re, hs_im, bbd, ccd, pw_re, pw_im, d_skip),
        in_specs=[pl.BlockSpec((SSM_T, cw), rev), pl.BlockSpec((SSM_T, cw), rev),
                  pl.BlockSpec((1, 1, ns), lambda b, k: (nc - 1 - k, 0, b)),
                  pl.BlockSpec((1, 1, ns), lambda b, k: (nc - 1 - k, 0, b)),
                  pl.BlockSpec((1, cw, ns2), lambda b, k: (b, 0, 0)),
                  pl.BlockSpec((1, ns2, cw), lambda b, k: (b, 0, 0)),
                  pl.BlockSpec((8, ns), lambda b, k: (0, b)), pl.BlockSpec((8, ns), lambda b, k: (0, b)),
                  pl.BlockSpec((1, cw), lambda b, k: (0, b))],
        out_specs=[pl.BlockSpec((SSM_T, cw), rev),
                   pl.BlockSpec((1, cw, ns2), lambda b, k: (b, 0, 0)),
                   pl.BlockSpec((1, ns2, cw), lambda b, k: (b, 0, 0)),
                   pl.BlockSpec((1, ns), lambda b, k: (0, b)), pl.BlockSpec((1, ns), lambda b, k: (0, b)),
                   pl.BlockSpec((1, cw), lambda b, k: (0, b))],
        out_shape=[jax.ShapeDtypeStruct((l, w), BF16), jax.ShapeDtypeStruct(bbd.shape, F32),
                   jax.ShapeDtypeStruct(ccd.shape, F32), jax.ShapeDtypeStruct((1, nb * ns), F32),
                   jax.ShapeDtypeStruct((1, nb * ns), F32), jax.ShapeDtypeStruct((1, w), F32)],
        scratch_shapes=[pltpu.VMEM((1, ns), F32), pltpu.VMEM((1, ns), F32)])
    return (*outs[:6], outs[6:])


def _block_diag_b(bb_re, bb_im, g, p, c):
    nb = g // SSM_GB
    eye = jnp.eye(SSM_GB, dtype=F32)

    def one(bb):
        t = bb.reshape(nb, SSM_GB, p, c).transpose(0, 1, 3, 2)
        return (t[:, :, :, None, :] * eye[None, :, None, :, None]).reshape(nb, SSM_GB * c, SSM_GB * p)

    return jnp.concatenate([one(bb_re), one(bb_im)], axis=2)


def _block_diag_c(c_re, c_im, g, p, c):
    nb = g // SSM_GB
    eye = jnp.eye(SSM_GB, dtype=F32)

    def one(cc):
        t = cc.reshape(nb, SSM_GB, c, p).transpose(0, 1, 3, 2)
        return (t[:, :, :, None, :] * eye[None, :, None, :, None]).reshape(nb, SSM_GB * p, SSM_GB * c)

    return jnp.concatenate([one(c_re), one(-c_im)], axis=1)


def _diag_of_b(dbbd, g, p, c):
    nb = g // SSM_GB
    t = dbbd.reshape(nb, SSM_GB, c, 2, SSM_GB, p)
    idx = jnp.arange(SSM_GB)
    d = t[:, idx, :, :, idx, :]
    d = d.transpose(1, 0, 3, 4, 2)
    return d[:, :, 0].reshape(g * p, c), d[:, :, 1].reshape(g * p, c)


def _diag_of_c(dccd, g, p, c):
    nb = g // SSM_GB
    t = dccd.reshape(nb, 2, SSM_GB, p, SSM_GB, c)
    idx = jnp.arange(SSM_GB)
    d = t[:, :, idx, :, idx, :]
    d = d.transpose(1, 0, 2, 4, 3)
    return d[:, :, 0].reshape(g, c, p), -d[:, :, 1].reshape(g, c, p)


def _glu_fwd(y, proj, w_glu, b_glu, name):
    l, w = y.shape
    tm = _tile(l, 256)

    def body(y_ref, z_ref, w_ref, b_ref, o_ref):
        g = _gelu(y_ref[...])
        t = _dot(_bf(g), w_ref[...]) + b_ref[...]
        o_ref[...] = _bf(g * _sigmoid(t) * _silu(z_ref[...]))

    return pl.pallas_call(
        body, name=name, grid=(l // tm,),
        in_specs=[_row_spec(tm, w), pl.BlockSpec((tm, w), lambda i: (i, 1)),
                  pl.BlockSpec((w, w), lambda i: (0, 0)), _vec_spec(w)],
        out_specs=_row_spec(tm, w), out_shape=jax.ShapeDtypeStruct((l, w), BF16),
        compiler_params=_params(("parallel",)),
    )(y, proj, w_glu, b_glu)


def _glu_bwd(do, y, proj, w_glu, b_glu, name):
    l, w = y.shape
    tm = _tile(l, 256)
    nsteps = l // tm

    def body(do_ref, y_ref, z_ref, w_ref, b_ref, dy_ref, dz_ref, dw_ref, db_ref, dw_acc):
        i = pl.program_id(0)
        first = i == 0
        yv, z, do = y_ref[...], z_ref[...], do_ref[...]
        g = _gelu(yv)
        g_bf = _bf(g)
        sg = _sigmoid(_dot(g_bf, w_ref[...]) + b_ref[...])
        dyy = do * _silu(z)
        dz_ref[...] = _bf(do * g * sg * _silu_grad(z))
        dt = dyy * g * sg * (1.0 - sg)
        dt_bf = _bf(dt)
        dg = dyy * sg + _dot(dt_bf, w_ref[...], NT)
        dy_ref[...] = dg * _gelu_grad(yv)
        _acc(dw_acc, first, _dot(_bf(g.T), dt_bf))
        _acc(db_ref, first, _colsum(dt))

        @pl.when(i == nsteps - 1)
        def _():
            dw_ref[...] = _bf(dw_acc[...])

    return pl.pallas_call(
        body, name=name, grid=(nsteps,),
        in_specs=[_row_spec(tm, w), _row_spec(tm, w), pl.BlockSpec((tm, w), lambda i: (i, 1)),
                  pl.BlockSpec((w, w), lambda i: (0, 0)), _vec_spec(w)],
        out_specs=[_row_spec(tm, w), _row_spec(tm, w), pl.BlockSpec((w, w), lambda i: (0, 0)), _vec_spec(w)],
        out_shape=[jax.ShapeDtypeStruct((l, w), F32), jax.ShapeDtypeStruct((l, w), BF16),
                   jax.ShapeDtypeStruct((w, w), BF16), jax.ShapeDtypeStruct((1, w), F32)],
        scratch_shapes=[pltpu.VMEM((w, w), F32)],
        compiler_params=_params(("arbitrary",)),
    )(do, y, proj, w_glu, b_glu)


MOD_ROWS = 128


def _mod_fwd(cond_pad, w_mod, b_shard, name):
    nl, d, ncol = w_mod.shape
    tn = _tile(ncol, 512)

    def body(c_ref, w_ref, b_ref, o_ref):
        o_ref[0] = _dot(_bf(c_ref[...]), _bf(w_ref[0])) + b_ref[0]

    return pl.pallas_call(
        body, name=name, grid=(nl, ncol // tn),
        in_specs=[pl.BlockSpec((MOD_ROWS, d), lambda a, j: (0, 0)),
                  pl.BlockSpec((1, d, tn), lambda a, j: (a, 0, j)),
                  pl.BlockSpec((1, 1, tn), lambda a, j: (a, 0, j))],
        out_specs=pl.BlockSpec((1, MOD_ROWS, tn), lambda a, j: (a, 0, j)),
        out_shape=jax.ShapeDtypeStruct((nl, MOD_ROWS, ncol), F32),
        compiler_params=_params(("parallel", "parallel")),
    )(cond_pad, w_mod, b_shard)


def _mod_bwd(cond_pad_t, dmod_pad, name):
    nl, _, ncol = dmod_pad.shape
    d = cond_pad_t.shape[0]
    tn = _tile(ncol, 512)

    def body(c_ref, dm_ref, o_ref):
        o_ref[0] = _dot(_bf(c_ref[...]), _bf(dm_ref[0]))

    return pl.pallas_call(
        body, name=name, grid=(nl, ncol // tn),
        in_specs=[pl.BlockSpec((d, MOD_ROWS), lambda a, j: (0, 0)),
                  pl.BlockSpec((1, MOD_ROWS, tn), lambda a, j: (a, 0, j))],
        out_specs=pl.BlockSpec((1, d, tn), lambda a, j: (a, 0, j)),
        out_shape=jax.ShapeDtypeStruct((nl, d, ncol), F32),
        compiler_params=_params(("parallel", "parallel")),
    )(cond_pad_t, dmod_pad)


def _silu_rows(c2d, name):
    def body(c_ref, o_ref):
        o_ref[...] = _silu(c_ref[...])

    return pl.pallas_call(body, name=name, out_shape=jax.ShapeDtypeStruct(c2d.shape, F32))(c2d)


def _sum_leading(x, name):
    n, r, c = x.shape
    tr = _tile(r, max(16, (1 << 20) // (4 * c)), 16 if r % 16 == 0 else 8)

    def body(x_ref, o_ref):
        acc = x_ref[0].astype(F32)
        for k in range(1, n):
            acc = acc + x_ref[k].astype(F32)
        o_ref[...] = acc

    return pl.pallas_call(
        body, name=name, grid=(r // tr,),
        in_specs=[pl.BlockSpec((n, tr, c), lambda i: (0, i, 0))], out_specs=pl.BlockSpec((tr, c), lambda i: (i, 0)),
        out_shape=jax.ShapeDtypeStruct((r, c), F32), compiler_params=_params(("parallel",)),
    )(x)


def _adamw(w, gs, m, v, name):
    r, c = w.shape
    tr = _tile(r, max(8, (1 << 19) // (4 * c)), 8)
    ng = len(gs)

    def body(*refs):
        w_ref, g_refs, m_ref, v_ref = refs[0], refs[1:1 + ng], refs[1 + ng], refs[2 + ng]
        g_ref, d_ref, nm_ref, nv_ref = refs[3 + ng:]
        g = g_refs[0][...]
        for extra in g_refs[1:]:
            g = g + extra[...]
        g_ref[...] = g
        nm = ADAM_B1 * m_ref[...] + (1.0 - ADAM_B1) * g
        nv = ADAM_B2 * v_ref[...] + (1.0 - ADAM_B2) * (g * g)
        nm_ref[...] = nm
        nv_ref[...] = nv
        m_hat = nm / (1.0 - ADAM_B1 ** ADAM_STEP)
        v_hat = nv / (1.0 - ADAM_B2 ** ADAM_STEP)
        d_ref[...] = -ADAM_LR * (m_hat / (jnp.sqrt(v_hat) + ADAM_EPS) + ADAM_WD * w_ref[...])

    spec = pl.BlockSpec((tr, c), lambda i: (i, 0))
    shp = jax.ShapeDtypeStruct((r, c), F32)
    return pl.pallas_call(
        body, name=name, grid=(r // tr,), in_specs=[spec] * (3 + ng), out_specs=[spec] * 4,
        out_shape=[shp] * 4, compiler_params=_params(("parallel",)),
    )(w, *gs, m, v)


ANY = pl.BlockSpec(memory_space=pl.ANY)


def _flip(v, bit):
    return 1 - v if bit else v


def _allgather8(x, name):
    def body(x_ref, o_ref, send_sems, recv_sems, local_sem):
        mx, my, mc = lax.axis_index("x"), lax.axis_index("y"), lax.axis_index("c")
        me = 4 * mx + 2 * my + mc
        mine = pltpu.make_async_copy(x_ref, o_ref.at[me], local_sem)
        mine.start()

        def copy(j):
            peer = (_flip(mx, j & 4), _flip(my, j & 2), _flip(mc, j & 1))
            return pltpu.make_async_remote_copy(
                src_ref=x_ref, dst_ref=o_ref.at[me], send_sem=send_sems.at[j - 1], recv_sem=recv_sems.at[j - 1],
                device_id=peer, device_id_type=MESH)

        def landing(j):
            peer = (_flip(mx, j & 4), _flip(my, j & 2), _flip(mc, j & 1))
            slot = 4 * peer[0] + 2 * peer[1] + peer[2]
            return pltpu.make_async_remote_copy(
                src_ref=x_ref, dst_ref=o_ref.at[slot], send_sem=send_sems.at[j - 1], recv_sem=recv_sems.at[j - 1],
                device_id=peer, device_id_type=MESH)

        for j in range(1, 8):
            copy(j).start()
        for j in range(1, 8):
            landing(j).wait()
        mine.wait()

    return pl.pallas_call(
        body, name=name, in_specs=[ANY], out_specs=ANY,
        out_shape=jax.ShapeDtypeStruct((8,) + x.shape, x.dtype),
        scratch_shapes=[pltpu.SemaphoreType.DMA((7,)), pltpu.SemaphoreType.DMA((7,)), pltpu.SemaphoreType.DMA],
    )(x)


def _chip_exchange(xs, gather, name):
    n = len(xs)

    def body(*refs):
        start, wait = _chip_exchange_ops(refs[:n], refs[n:2 * n], *refs[2 * n:], gather)
        start()
        wait()

    return pl.pallas_call(
        body, name=name, in_specs=[ANY] * n, out_specs=[ANY] * n, out_shape=_chip_exchange_shapes(xs, gather),
        scratch_shapes=_chip_exchange_sems(n),
    )(*xs)


def _chip_exchange_shapes(xs, gather):
    return [jax.ShapeDtypeStruct(((4,) + x.shape) if gather else x.shape, x.dtype) for x in xs]


def _chip_exchange_sems(n):
    return [pltpu.SemaphoreType.DMA((3 * n,)), pltpu.SemaphoreType.DMA((3 * n,)), pltpu.SemaphoreType.DMA((n,))]


def _chip_exchange_ops(x_refs, o_refs, send_sems, recv_sems, local_sems, gather):
    n = len(x_refs)
    mx, my, mc = lax.axis_index("x"), lax.axis_index("y"), lax.axis_index("c")
    k0 = 2 * mx + my

    def local(a):
        src = x_refs[a] if gather else x_refs[a].at[k0]
        return pltpu.make_async_copy(src, o_refs[a].at[k0], local_sems.at[a])

    def copy(a, j, outgoing):
        px, py = _flip(mx, j & 2), _flip(my, j & 1)
        kp = 2 * px + py
        if outgoing:
            src = x_refs[a] if gather else x_refs[a].at[kp]
            dst = o_refs[a].at[k0]
        else:
            src = x_refs[a] if gather else x_refs[a].at[k0]
            dst = o_refs[a].at[kp]
        s = a * 3 + j - 1
        return pltpu.make_async_remote_copy(
            src_ref=src, dst_ref=dst, send_sem=send_sems.at[s], recv_sem=recv_sems.at[s],
            device_id=(px, py, mc), device_id_type=MESH)

    def start():
        for a in range(n):
            local(a).start()
            for j in range(1, 4):
                copy(a, j, True).start()

    def wait():
        for a in range(n):
            for j in range(1, 4):
                copy(a, j, False).wait()
            local(a).wait()

    return start, wait


def _call(body, *, name, grid, in_specs, out_specs, out_shape, args, scratch_shapes=(), sem=None, ride=None):
    if ride is None:
        return pl.pallas_call(
            body, name=name, grid=grid, in_specs=list(in_specs), out_specs=list(out_specs), out_shape=list(out_shape),
            scratch_shapes=list(scratch_shapes), compiler_params=_params(sem))(*args)
    xs, gather = ride
    n_in, n_out, n_scr, nx = len(in_specs), len(out_specs), len(scratch_shapes), len(xs)

    def wrapped(*refs):
        ins, x_refs = refs[:n_in], refs[n_in:n_in + nx]
        outs = refs[n_in + nx:n_in + nx + n_out]
        lands = refs[n_in + nx + n_out:n_in + 2 * nx + n_out]
        rest = refs[n_in + 2 * nx + n_out:]
        scr, sems = rest[:n_scr], rest[n_scr:]
        start, wait = _chip_exchange_ops(x_refs, lands, *sems, gather)
        ids = [pl.program_id(a) for a in range(len(grid))]
        first = functools.reduce(jnp.logical_and, [i == 0 for i in ids])
        last = functools.reduce(jnp.logical_and, [i == g - 1 for i, g in zip(ids, grid)])
        pl.when(first)(start)
        body(*ins, *outs, *scr)
        pl.when(last)(wait)

    return pl.pallas_call(
        wrapped, name=name, grid=grid, in_specs=list(in_specs) + [ANY] * nx, out_specs=list(out_specs) + [ANY] * nx,
        out_shape=list(out_shape) + _chip_exchange_shapes(xs, gather),
        scratch_shapes=list(scratch_shapes) + _chip_exchange_sems(nx),
        compiler_params=_params(("arbitrary",) * len(grid)))(*args, *xs)


def _sibling_exchange(xs, name):
    n = len(xs)

    def body(*refs):
        x_refs, o_refs = refs[:n], refs[n:2 * n]
        send_sems, recv_sems = refs[2 * n:]
        sib = (lax.axis_index("x"), lax.axis_index("y"), 1 - lax.axis_index("c"))
        copies = [pltpu.make_async_remote_copy(
            src_ref=x_refs[a], dst_ref=o_refs[a], send_sem=send_sems.at[a], recv_sem=recv_sems.at[a],
            device_id=sib, device_id_type=MESH) for a in range(n)]
        for cp in copies:
            cp.start()
        for cp in copies:
            cp.wait()

    return pl.pallas_call(
        body, name=name, in_specs=[ANY] * n, out_specs=[ANY] * n,
        out_shape=[jax.ShapeDtypeStruct(x.shape, x.dtype) for x in xs],
        scratch_shapes=[pltpu.SemaphoreType.DMA((n,)), pltpu.SemaphoreType.DMA((n,))],
    )(*xs)


PACK = 1024


def _pack(parts):
    flat = []
    for p in parts:
        v = p.reshape(-1).astype(F32)
        flat.append(jnp.pad(v, (0, (-v.shape[0]) % PACK)))
    return jnp.concatenate(flat).reshape(-1, 128)


def _unpack(packed, shapes):
    flat = packed.reshape(-1)
    out, off = [], 0
    for shp in shapes:
        n = math.prod(shp)
        out.append(flat[off:off + n].reshape(shp))
        off += n + (-n) % PACK
    return out


def kernel(x, c, ln_pre_g, ln_post_g, w_mod, b_mod, w_in_ab, w_out_ab, sgu_norm_g, sgu_w, sgu_b, w_in_ssm, w_out_ssm, lam_re, lam_im, b_re, b_im, c_re, c_im, d_skip, log_dt, w_glu, b_glu, loss_target, m_ln_pre_g, m_ln_post_g, m_w_mod, m_b_mod, m_w_in_ab, m_w_out_ab, m_sgu_norm_g, m_sgu_w, m_sgu_b, m_w_in_ssm, m_w_out_ssm, m_lam_re, m_lam_im, m_b_re, m_b_im, m_c_re, m_c_im, m_d_skip, m_log_dt, m_w_glu, m_b_glu, v_ln_pre_g, v_ln_post_g, v_w_mod, v_b_mod, v_w_in_ab, v_w_out_ab, v_sgu_norm_g, v_sgu_w, v_sgu_b, v_w_in_ssm, v_w_out_ssm, v_lam_re, v_lam_im, v_b_re, v_b_im, v_c_re, v_c_im, v_d_skip, v_log_dt, v_w_glu, v_b_glu):
    given = dict(locals())
    mx, my, mc = lax.axis_index("x"), lax.axis_index("y"), lax.axis_index("c")
    me = 4 * mx + 2 * my + mc
    chip = 2 * mx + my

    _, l, d = x.shape
    x2, tgt = x[0], loss_target[0]
    n_in = w_in_ab.shape[2] * 4
    wa = wb = n_in // 7
    w = w_out_ssm.shape[1]
    g, p, cch = b_re.shape[1:]
    nmod = w_mod.shape[2]

    (gw_in_ab,) = _chip_exchange([_bf(w_in_ab[0])], True, "gather_w_in_ab")
    win_ab = jnp.concatenate([gw_in_ab[k] for k in range(4)], axis=1)
    later_shards = [_bf(w_out_ab[0]), _bf(w_in_ssm[0]), _bf(w_out_ssm[0]), _bf(w_glu[0]), d_skip, b_glu]

    cond = _silu_rows(c.reshape(d // 128, 128), "cond_silu")
    cond_all = _allgather8(cond, "gather_cond").reshape(8, d)
    b_shard = lax.dynamic_slice(b_mod, (0, chip * nmod), (2, nmod)).reshape(2, 1, nmod)
    cond_pad = jnp.pad(cond_all, ((0, MOD_ROWS - 8), (0, 0)))
    modp = _mod_fwd(cond_pad, w_mod, b_shard, "mod_fwd")[:, :8]
    modp_all = _allgather8(modp.reshape(16, nmod), "gather_mod").reshape(4, 2, 2, 8, nmod)
    mine = lax.dynamic_index_in_dim(lax.dynamic_index_in_dim(modp_all, mc, 1, False), me, 2, False)
    mod = mine.transpose(1, 0, 2).reshape(2, 3 * d)
    shift = [mod[a:a + 1, :d] for a in range(2)]
    scale = [mod[a:a + 1, d:2 * d] for a in range(2)]
    gate = [mod[a:a + 1, 2 * d:] for a in range(2)]
    pre_g = [ln_pre_g[a:a + 1] for a in range(2)]
    post_g = [ln_post_g[a:a + 1] for a in range(2)]

    sgu_w0, sgu_bt = sgu_w[0], sgu_b[0].T
    h0 = _pre_fwd(x2, pre_g[0], scale[0], shift[0], "pre0_fwd")
    w_gates = jnp.concatenate([win_ab[:, :3 * wa], win_ab[:, 3 * wa + 3 * wb:]], axis=1)
    proj0 = _matmul(h0, w_gates, "nn", F32, "proj0", tm=1024)
    qkv = _matmul(h0, win_ab[:, 3 * wa:3 * wa + 3 * wb], "nn", BF16, "proj0_qkv", tm=1024)
    out_b, lktot, (gw_out_ab, gw_in_ssm, gw_out_ssm, gw_glu, g_dskip, g_bglu) = _attn_fwd(
        qkv, wb, "attn_fwd", ride=(later_shards, True))
    wout_ab = gw_out_ab.reshape(wa + wb, d)
    win_ssm = gw_in_ssm.reshape(d, 2 * w)
    wout_ssm = jnp.concatenate([gw_out_ssm[k] for k in range(4)], axis=1)
    wglu = gw_glu.reshape(w, w)
    dskip_full = g_dskip.reshape(1, w)
    bglu_full = g_bglu.reshape(1, w)
    cat =_sgu_fwd(proj0, out_b, sgu_norm_g, sgu_w0, sgu_bt, wa, wb, "sgu_fwd")
    y0 = _matmul(cat, wout_ab, "nn", F32, "out0", tm=1024)
    x1, h1 = _post_pre_fwd(x2, y0, gate[0], post_g[0], pre_g[1], scale[1], shift[1], "post0_pre1_fwd")

    s = g * p
    lr_c, li_c = lam_re.reshape(s, 1), lam_im.reshape(s, 1)
    ldt_c = jnp.repeat(log_dt.reshape(g), p).reshape(s, 1)
    br_c, bi_c = b_re.reshape(s, cch), b_im.reshape(s, cch)
    bb_re, bb_im, pw_re, pw_im = _ssm_prep(lr_c, li_c, ldt_c, br_c, bi_c, lr_c.reshape(1, s), li_c.reshape(1, s),
                                           ldt_c.reshape(1, s), "ssm_prep")
    bbd = _bf(_block_diag_b(bb_re, bb_im, g, p, cch))
    ccd = _bf(_block_diag_c(c_re[0], c_im[0], g, p, cch))
    proj1 = _matmul(h1, win_ssm, "nn", F32, "proj1", tm=1024)
    y_ssm, hs_re, hs_im = _ssm_fwd(proj1, bbd, ccd, pw_re, pw_im, dskip_full, w, "ssm_fwd")
    o1 = _glu_fwd(y_ssm, proj1, wglu, bglu_full, "glu_fwd")
    y1 = _matmul(o1, wout_ssm, "nn", F32, "out1", tm=1024)
    loss_vec, dy1, dx2, dgate1, dpost1 = _post_loss(x1, y1, gate[1], post_g[1], tgt, "post1_loss")

    do1 = _matmul(dy1, wout_ssm, "nt", F32, "out1_dx", tm=1024)
    gr_wout_ssm = _matmul(o1, dy1, "tn", BF16, "out1_dw", tm=1024, tk=1024, n_split=4)
    dy_ssm, dz1, gr_wglu, gr_bglu = _glu_bwd(do1, y_ssm, proj1, wglu, bglu_full, "glu_bwd")
    du1, dbbd, dccd, da_re, da_im, gr_dskip, (ld_wout_ssm, ld_wglu) = _ssm_bwd(
        proj1, dy_ssm, hs_re, hs_im, bbd, ccd, pw_re, pw_im, dskip_full, w, "ssm_bwd",
        ride=([gr_wout_ssm, gr_wglu.reshape(4, w // 4, w)], False))
    dproj1 = jnp.concatenate([du1, dz1], axis=1)
    dh1 = _matmul(dproj1, win_ssm, "nt", F32, "proj1_dx", tm=1024)
    gr_win_ssm = _matmul(h1, dproj1, "tn", BF16, "proj1_dw", tm=1024, tn=1024, tk=1024)
    dx1, dscale1, dshift1, dpre1, dy0, dgate0, dpost0 = _pre_bwd(
        dh1, dx2, x1, pre_g[1], scale[1], "pre1_post0_bwd", post=(y0, gate[0], post_g[0]))

    dcat = _matmul(dy0, wout_ab, "nt", F32, "out0_dx", tm=1024)
    gr_wout_ab = _matmul(cat, dy0, "tn", BF16, "out0_dw", tm=1024, tn=1024, tk=1024)
    dq, dk, dv, (ld_win_ssm, ld_wout_ab) = _attn_bwd(
        qkv, proj0, dcat, lktot, wa, wb, "attn_bwd",
        ride=([gr_win_ssm.reshape(4, d // 4, 2 * w), gr_wout_ab.reshape(4, (wa + wb) // 4, d)], False))
    dproj0, gr_sgu_w, gr_sgu_bt, gr_sgu_g = _sgu_bwd(proj0, out_b, dcat, dq, dk, dv, sgu_norm_g, sgu_w0, sgu_bt,
                                                     wa, wb, "sgu_bwd")
    gr_win_ab = _matmul(h0, dproj0, "tn", BF16, "proj0_dw", tm=1024, tk=1024, tn=896, n_split=4)
    dh0, (ld_win_ab,) = _matmul(dproj0, win_ab, "nt", F32, "proj0_dx", tm=1024, tk=1792, ride=([gr_win_ab], False))
    grad_x, dscale0, dshift0, dpre0 = _pre_bwd(dh0, dx1, x2, pre_g[0], scale[0], "pre0_bwd")

    landed = [ld_win_ab, ld_wout_ab, ld_win_ssm, ld_wout_ssm, ld_wglu]
    big_names = ["w_in_ab", "w_out_ab", "w_in_ssm", "w_out_ssm", "w_glu"]
    sums = [_sum_leading(a, "sum_" + nm) for a, nm in zip(landed, big_names)]
    sib = _sibling_exchange(sums, "sibling_grads")
    results = {}
    for nm, s_mine, s_sib in zip(big_names, sums, sib):
        shp = given[nm].shape
        two_d = lambda a: a.reshape(-1, shp[-1])
        outs = _adamw(two_d(given[nm]), [s_mine, s_sib], two_d(given["m_" + nm]), two_d(given["v_" + nm]),
                      "adamw_" + nm)
        results[nm] = [o.reshape(shp) for o in outs]

    dbb_re, dbb_im = _diag_of_b(dbbd, g, p, cch)
    dc_re, dc_im = _diag_of_c(dccd, g, p, cch)
    dmod = jnp.concatenate([jnp.concatenate([dshift0, dscale0, dgate0], axis=1),
                            jnp.concatenate([dshift1, dscale1, dgate1], axis=1)], axis=0)
    partial = [loss_vec[:, :1], jnp.concatenate([dpre0, dpre1], 0), jnp.concatenate([dpost0, dpost1], 0), dmod,
               gr_sgu_g, gr_sgu_w, gr_sgu_bt.T, da_re, da_im, dbb_re, dbb_im, dc_re, dc_im, gr_dskip, gr_bglu]
    part_shapes = [a.shape for a in partial]
    packed = _pack(partial)
    gathered = _allgather8(packed, "gather_small")
    total = _sum_leading(gathered, "sum_small")
    (loss_s, g_pre, g_post, g_bmod, g_sgu_g, g_sgu_w, g_sgu_b, s_da_re, s_da_im, s_dbb_re, s_dbb_im, g_c_re, g_c_im,
     g_dskip_full, g_bglu_full) = _unpack(total, part_shapes)
    loss = loss_s.reshape(())

    g_lr, g_li, g_ldt, g_br, g_bi = _ssm_prep_bwd(lr_c, li_c, ldt_c, br_c, bi_c, s_da_re.reshape(s, 1),
                                                  s_da_im.reshape(s, 1), s_dbb_re, s_dbb_im, p, "ssm_prep_bwd")
    small = {
        "ln_pre_g": g_pre, "ln_post_g": g_post, "b_mod": g_bmod, "sgu_norm_g": g_sgu_g,
        "sgu_w": g_sgu_w.reshape(sgu_w.shape), "sgu_b": g_sgu_b.reshape(sgu_b.shape),
        "lam_re": g_lr.reshape(lam_re.shape), "lam_im": g_li.reshape(lam_im.shape),
        "b_re": g_br.reshape(b_re.shape), "b_im": g_bi.reshape(b_im.shape),
        "c_re": g_c_re.reshape(c_re.shape), "c_im": g_c_im.reshape(c_im.shape),
        "d_skip": lax.dynamic_slice(g_dskip_full, (0, chip * (w // 4)), (1, w // 4)),
        "log_dt": g_ldt.reshape(log_dt.shape),
        "b_glu": lax.dynamic_slice(g_bglu_full, (0, chip * (w // 4)), (1, w // 4)),
    }
    small_names = list(small)
    small_shapes = [small[nm].shape for nm in small_names]
    outs = _adamw(_pack([given[nm] for nm in small_names]), [_pack([small[nm] for nm in small_names])],
                  _pack([given["m_" + nm] for nm in small_names]), _pack([given["v_" + nm] for nm in small_names]),
                  "adamw_small")
    unpacked = [_unpack(o, small_shapes) for o in outs]
    for i, nm in enumerate(small_names):
        results[nm] = [small[nm]] + [unpacked[k][i] for k in range(1, 4)]

    off = sum(math.prod(sh) + (-math.prod(sh)) % PACK for sh in part_shapes[:3])
    dmod_rows = gathered.reshape(8, -1)[:, off:off + 6 * d].reshape(8, 2, 3 * d)
    dmod_shard = lax.dynamic_slice(dmod_rows, (0, 0, chip * nmod), (8, 2, nmod)).transpose(1, 0, 2)
    dmod_pad = jnp.pad(dmod_shard, ((0, 0), (0, MOD_ROWS - 8), (0, 0)))
    gr_wmod = _mod_bwd(cond_pad.T, dmod_pad, "mod_bwd")
    two_d = lambda a: a.reshape(-1, nmod)
    outs = _adamw(two_d(w_mod), [two_d(gr_wmod)], two_d(m_w_mod), two_d(v_w_mod), "adamw_w_mod")
    results["w_mod"] = [o.reshape(w_mod.shape) for o in outs]

    names = ["ln_pre_g", "ln_post_g", "w_mod", "b_mod", "w_in_ab", "w_out_ab", "sgu_norm_g", "sgu_w", "sgu_b",
             "w_in_ssm", "w_out_ssm", "lam_re", "lam_im", "b_re", "b_im", "c_re", "c_im", "d_skip", "log_dt",
             "w_glu", "b_glu"]
    return (loss, grad_x[None], *[results[nm][0] for nm in names], *[results[nm][1] for nm in names],
            *[results[nm][2] for nm in names], *[results[nm][3] for nm in names])
```

```python
import functools
import math

import jax
import jax.numpy as jnp
from jax import lax
from jax.experimental import pallas as pl
from jax.experimental.pallas import tpu as pltpu

F32 = jnp.float32
BF16 = jnp.bfloat16
MESH = pl.DeviceIdType.MESH

EPS = 1e-6
HEAD = 128
SSM_T = 512
SSM_GB = 16
ADAM_LR, ADAM_B1, ADAM_B2, ADAM_EPS, ADAM_WD, ADAM_STEP = 0.001, 0.9, 0.999, 1e-08, 0.01, 10
VMEM_LIMIT = 56 * 1024 * 1024

NN = (((1,), (0,)), ((), ()))
NT = (((1,), (1,)), ((), ()))
TN = (((0,), (0,)), ((), ()))


def _params(sem=None):
    return pltpu.CompilerParams(dimension_semantics=sem, vmem_limit_bytes=VMEM_LIMIT)


def _dot(a, b, dims=NN):
    return lax.dot_general(a, b, dims, preferred_element_type=F32)


def _bf(x):
    return x.astype(BF16)


def _gelu(x):
    k = math.sqrt(2.0 / math.pi)
    t = jnp.tanh(k * (x + 0.044715 * x * x * x))
    return 0.5 * x * (1.0 + t)


def _gelu_grad(x):
    k = math.sqrt(2.0 / math.pi)
    x2 = x * x
    t = jnp.tanh(k * (x + 0.044715 * x * x2))
    return 0.5 * (1.0 + t) + 0.5 * x * (1.0 - t * t) * k * (1.0 + 3.0 * 0.044715 * x2)


def _sigmoid(x):
    return 1.0 / (1.0 + jnp.exp(-x))


def _silu(x):
    return x * _sigmoid(x)


def _silu_grad(x):
    s = _sigmoid(x)
    return s * (1.0 + x * (1.0 - s))


def _tile(n, t, mult=128):
    if n <= t:
        return n
    for cand in range(t - t % mult, 0, -mult):
        if n % cand == 0:
            return cand
    raise ValueError((n, t, mult))


def _matmul(a, b, mode, out_dtype, name, tm=512, tn=512, tk=2048, n_split=1, ride=None):
    if mode == "nn":
        (m, kk), (_, n) = a.shape, b.shape
    elif mode == "nt":
        (m, kk), (n, _) = a.shape, b.shape
    else:
        (kk, m), (_, n) = a.shape, b.shape
    tm, tk = _tile(m, tm), _tile(kk, tk)
    ns = n // n_split
    tn = _tile(ns, tn)
    nk = kk // tk
    dims = {"nn": NN, "nt": NT, "tn": TN}[mode]

    def body(a_ref, b_ref, o_ref, acc_ref):
        k = pl.program_id(2)
        part = _dot(_bf(a_ref[...]), _bf(b_ref[...]), dims)

        @pl.when(k == 0)
        def _():
            acc_ref[...] = part

        @pl.when(k > 0)
        def _():
            acc_ref[...] += part

        @pl.when(k == nk - 1)
        def _():
            o_ref[...] = acc_ref[...].astype(out_dtype).reshape(o_ref.shape)

    if mode == "nn":
        a_spec = pl.BlockSpec((tm, tk), lambda i, j, k: (i, k))
        b_spec = pl.BlockSpec((tk, tn), lambda i, j, k: (k, j))
    elif mode == "nt":
        a_spec = pl.BlockSpec((tm, tk), lambda i, j, k: (i, k))
        b_spec = pl.BlockSpec((tn, tk), lambda i, j, k: (j, k))
    else:
        a_spec = pl.BlockSpec((tk, tm), lambda i, j, k: (k, i))
        b_spec = pl.BlockSpec((tk, tn), lambda i, j, k: (k, j))
    if n_split == 1:
        out_shape = jax.ShapeDtypeStruct((m, n), out_dtype)
        o_spec = pl.BlockSpec((tm, tn), lambda i, j, k: (i, j))
    else:
        per = ns // tn
        out_shape = jax.ShapeDtypeStruct((n_split, m, ns), out_dtype)
        o_spec = pl.BlockSpec((1, tm, tn), lambda i, j, k: (j // per, i, j % per))
    outs = _call(body, name=name, grid=(m // tm, n // tn, nk), in_specs=[a_spec, b_spec], out_specs=[o_spec],
                 out_shape=[out_shape], scratch_shapes=[pltpu.VMEM((tm, tn), F32)], args=(a, b),
                 sem=("parallel", "parallel", "arbitrary"), ride=ride)
    return outs[0] if ride is None else (outs[0], outs[1:])


def _row_spec(tm, d):
    return pl.BlockSpec((tm, d), lambda i: (i, 0))


def _vec_spec(d):
    return pl.BlockSpec((1, d), lambda i: (0, 0))


def _acc(ref, first, val):
    @pl.when(first)
    def _():
        ref[...] = val

    @pl.when(jnp.logical_not(first))
    def _():
        ref[...] += val


def _colsum(x):
    return jnp.sum(x, axis=0, keepdims=True)


def _rownorm(x):
    r = lax.rsqrt(jnp.mean(x * x, axis=-1, keepdims=True) + EPS)
    return x * r, r


def _pre_fwd(x, g, scale, shift, name):
    l, d = x.shape
    tm = _tile(l, 256)

    def body(x_ref, g_ref, sc_ref, sh_ref, h_ref):
        n, _ = _rownorm(x_ref[...])
        h_ref[...] = _bf(n * g_ref[...] * (1.0 + sc_ref[...]) + sh_ref[...])

    return pl.pallas_call(
        body, name=name, grid=(l // tm,),
        in_specs=[_row_spec(tm, d), _vec_spec(d), _vec_spec(d), _vec_spec(d)],
        out_specs=_row_spec(tm, d), out_shape=jax.ShapeDtypeStruct((l, d), BF16),
        compiler_params=_params(("parallel",)),
    )(x, g, scale, shift)


def _post_pre_fwd(x, y, gate, pg, g1, scale1, shift1, name):
    l, d = x.shape
    tm = _tile(l, 256)

    def body(x_ref, y_ref, gate_ref, pg_ref, g1_ref, sc_ref, sh_ref, x1_ref, h1_ref):
        ny, _ = _rownorm(y_ref[...])
        x1 = x_ref[...] + gate_ref[...] * (ny * pg_ref[...])
        x1_ref[...] = x1
        n1, _ = _rownorm(x1)
        h1_ref[...] = _bf(n1 * g1_ref[...] * (1.0 + sc_ref[...]) + sh_ref[...])

    v = _vec_spec(d)
    return pl.pallas_call(
        body, name=name, grid=(l // tm,),
        in_specs=[_row_spec(tm, d), _row_spec(tm, d), v, v, v, v, v],
        out_specs=[_row_spec(tm, d), _row_spec(tm, d)],
        out_shape=[jax.ShapeDtypeStruct((l, d), F32), jax.ShapeDtypeStruct((l, d), BF16)],
        compiler_params=_params(("parallel",)),
    )(x, y, gate, pg, g1, scale1, shift1)


def _post_loss(x1, y1, gate, pg, target, name):
    l, d = x1.shape
    tm = _tile(l, 256)

    def body(x_ref, y_ref, gate_ref, pg_ref, t_ref, loss_ref, dy_ref, dx_ref, dgate_ref, dpg_ref):
        first = pl.program_id(0) == 0
        y = y_ref[...]
        ny, ry = _rownorm(y)
        q = ny * pg_ref[...]
        x2 = x_ref[...] + gate_ref[...] * q
        e = x2 - t_ref[...]
        _acc(loss_ref, first, jnp.full((1, 128), 0.5 / d, F32) * jnp.sum(e * e))
        dx2 = e * (1.0 / d)
        dx_ref[...] = dx2
        _acc(dgate_ref, first, _colsum(dx2 * q))
        dq = dx2 * gate_ref[...]
        _acc(dpg_ref, first, _colsum(dq * ny))
        dny = dq * pg_ref[...]
        dy = ry * (dny - ny * jnp.mean(dny * ny, axis=-1, keepdims=True))
        dy_ref[...] = _bf(dy)

    v = _vec_spec(d)
    return pl.pallas_call(
        body, name=name, grid=(l // tm,),
        in_specs=[_row_spec(tm, d), _row_spec(tm, d), v, v, _row_spec(tm, d)],
        out_specs=[_vec_spec(128), _row_spec(tm, d), _row_spec(tm, d), v, v],
        out_shape=[jax.ShapeDtypeStruct((1, 128), F32), jax.ShapeDtypeStruct((l, d), BF16),
                   jax.ShapeDtypeStruct((l, d), F32), jax.ShapeDtypeStruct((1, d), F32),
                   jax.ShapeDtypeStruct((1, d), F32)],
        compiler_params=_params(("arbitrary",)),
    )(x1, y1, gate, pg, target)


def _pre_bwd(dh, dres, x, g, scale, name, post=None):
    l, d = x.shape
    tm = _tile(l, 256)
    with_post = post is not None

    def body(*refs):
        if with_post:
            (dh_ref, dres_ref, x_ref, g_ref, sc_ref, y_ref, gate_ref, pg_ref,
             dx_ref, dsc_ref, dsh_ref, dg_ref, dy_ref, dgate_ref, dpg_ref) = refs
        else:
            dh_ref, dres_ref, x_ref, g_ref, sc_ref, dx_ref, dsc_ref, dsh_ref, dg_ref = refs
        first = pl.program_id(0) == 0
        dh = dh_ref[...]
        n, r = _rownorm(x_ref[...])
        _acc(dsc_ref, first, _colsum(dh * (n * g_ref[...])))
        _acc(dsh_ref, first, _colsum(dh))
        dyn = dh * (1.0 + sc_ref[...])
        _acc(dg_ref, first, _colsum(dyn * n))
        dn = dyn * g_ref[...]
        dx = dres_ref[...] + r * (dn - n * jnp.mean(dn * n, axis=-1, keepdims=True))
        dx_ref[...] = dx
        if with_post:
            ny, ry = _rownorm(y_ref[...])
            _acc(dgate_ref, first, _colsum(dx * (ny * pg_ref[...])))
            dq = dx * gate_ref[...]
            _acc(dpg_ref, first, _colsum(dq * ny))
            dny = dq * pg_ref[...]
            dy_ref[...] = _bf(ry * (dny - ny * jnp.mean(dny * ny, axis=-1, keepdims=True)))

    v = _vec_spec(d)
    row = _row_spec(tm, d)
    vec_out = jax.ShapeDtypeStruct((1, d), F32)
    in_specs = [row, row, row, v, v]
    args = [dh, dres, x, g, scale]
    out_specs = [row, v, v, v]
    out_shape = [jax.ShapeDtypeStruct((l, d), F32), vec_out, vec_out, vec_out]
    if with_post:
        in_specs += [row, v, v]
        args += list(post)
        out_specs += [row, v, v]
        out_shape += [jax.ShapeDtypeStruct((l, d), BF16), vec_out, vec_out]
    return pl.pallas_call(
        body, name=name, grid=(l // tm,), in_specs=in_specs, out_specs=out_specs, out_shape=out_shape,
        compiler_params=_params(("arbitrary",)),
    )(*args)


def _softplus_parts(z):
    e = jnp.exp(-jnp.abs(z))
    den = 1.0 + e
    lb = jnp.minimum(z, 0.0) - jnp.log(den)
    sig = jnp.where(z >= 0.0, 1.0, e) / den
    return lb, lb - z, sig


def _tri(cmp, n=HEAD):
    row = lax.broadcasted_iota(jnp.int32, (n, n), 0)
    col = lax.broadcasted_iota(jnp.int32, (n, n), 1)
    return cmp(row, col)


ATT_T = 256


def _split(x):
    hi = _bf(x)
    return hi, _bf(x - hi.astype(F32))


def _two(hi, lo, m):
    return _dot(hi, m) + _dot(lo, m)


def _attn_fwd(qkv, wb, name, hp=4, ride=None):
    l = qkv.shape[0]
    t = ATT_T
    nh, nq = wb // HEAD, l // t
    hp = min(hp, nh)
    ng, wg = nh // hp, hp * HEAD
    scale = 1.0 / math.sqrt(HEAD)

    def body(q_ref, k_ref, v_ref, o_ref, lk_ref):
        i = pl.program_id(1)
        valid = _tri(lambda r, c: c < r, t)
        m_gt = _bf(_tri(lambda r, c: r > c, t).astype(F32))

        def tile(j, carry, diag):
            rows = pl.ds(pl.multiple_of(j * t, t), t)
            out = []
            for hh, (acc, run) in enumerate(carry):
                cs = slice(hh * HEAD, (hh + 1) * HEAD)
                z = _dot(q_ref[:, cs], k_ref[rows, cs], NT) * scale
                lb, lk, _ = _softplus_parts(z)
                if diag:
                    lk = jnp.where(valid, lk, 0.0)
                hi, lo = _split(lk)
                later = _two(hi, lo, m_gt)
                w = jnp.exp(lb + later + run)
                if diag:
                    w = jnp.where(valid, w, 0.0)
                out.append((acc + _dot(_bf(w), v_ref[rows, cs]), run + later[:, :1] + lk[:, :1]))
            return tuple(out)

        zero = (jnp.zeros((t, HEAD), F32), jnp.zeros((t, 1), F32))
        carry = tile(i, (zero,) * hp, True)
        carry = lax.fori_loop(0, i, lambda s, c: tile(i - 1 - s, c, False), carry)
        for hh, (acc, run) in enumerate(carry):
            cs = slice(hh * HEAD, (hh + 1) * HEAD)
            o_ref[:, cs] = acc
            lk_ref[:, cs] = jnp.broadcast_to(run, (t, HEAD))

    blk = lambda off: pl.BlockSpec((t, wg), lambda h, i: (i, off + h))
    full = lambda off: pl.BlockSpec((l, wg), lambda h, i: (0, off + h))
    out = pl.BlockSpec((t, wg), lambda h, i: (i, h))
    outs = _call(body, name=name, grid=(ng, nq), in_specs=[blk(0), full(ng), full(2 * ng)], out_specs=[out, out],
                 out_shape=[jax.ShapeDtypeStruct((l, wb), F32), jax.ShapeDtypeStruct((l, wb), F32)],
                 args=(qkv, qkv, qkv), sem=("parallel", "arbitrary"), ride=ride)
    return outs[0], outs[1], outs[2:]


def _attn_bwd(qkv, proj, dcat, lktot, wa, wb, name, hp=2, ride=None):
    l = qkv.shape[0]
    t = ATT_T
    nh, nq = wb // HEAD, l // t
    hp = min(hp, nh)
    ng, wg = nh // hp, hp * HEAD
    scale = 1.0 / math.sqrt(HEAD)

    def body(q_ref, k_ref, v_ref, bz_ref, dc_ref, lt_ref, dq_ref, dkt_ref, dvt_ref, do_s, qt_s, dot_s):
        i = pl.program_id(1)

        @pl.when(i == 0)
        def _():
            dkt_ref[...] = jnp.zeros_like(dkt_ref)
            dvt_ref[...] = jnp.zeros_like(dvt_ref)

        do = dc_ref[...] * _silu(bz_ref[...])
        do_s[...] = _bf(do)
        for hh in range(hp):
            cs = slice(hh * HEAD, (hh + 1) * HEAD)
            qt_s[hh] = _bf(q_ref[:, cs].astype(F32).T)
            dot_s[hh] = _bf(do[:, cs].T)
        valid = _tri(lambda r, c: c < r, t)
        m_le = _bf(_tri(lambda r, c: r <= c, t).astype(F32))
        m_lt = _bf(_tri(lambda r, c: r < c, t).astype(F32))

        def tile(j, carry, diag):
            rows = pl.ds(pl.multiple_of(j * t, t), t)
            out = []
            for hh, (dq, cpre, ppre) in enumerate(carry):
                cs = slice(hh * HEAD, (hh + 1) * HEAD)
                kb = k_ref[rows, cs]
                z = _dot(q_ref[:, cs], kb, NT) * scale
                lb, lk, sig = _softplus_parts(z)
                if diag:
                    lk = jnp.where(valid, lk, 0.0)
                hi, lo = _split(lk)
                pin = _two(hi, lo, m_le)
                w = jnp.exp(lb + (lt_ref[:, hh * HEAD:hh * HEAD + 1] - cpre) - pin)
                if diag:
                    w = jnp.where(valid, w, 0.0)
                da = _dot(do_s[:, cs], v_ref[rows, cs], NT) * w
                pex = _dot(_bf(da), m_lt)
                dz = (da - sig * (da + ppre + pex)) * scale
                if diag:
                    dz = jnp.where(valid, dz, 0.0)
                dz_bf = _bf(dz)
                dkt_ref[hh, j] += _dot(qt_s[hh], dz_bf)
                dvt_ref[hh, j] += _dot(dot_s[hh], _bf(w))
                out.append((dq + _dot(dz_bf, kb), cpre + pin[:, t - 1:], ppre + pex[:, t - 1:] + da[:, t - 1:]))
            return tuple(out)

        zero = (jnp.zeros((t, HEAD), F32), jnp.zeros((t, 1), F32), jnp.zeros((t, 1), F32))
        carry = lax.fori_loop(0, i, lambda j, c: tile(j, c, False), (zero,) * hp)
        carry = tile(i, carry, True)
        for hh in range(hp):
            dq_ref[:, hh * HEAD:(hh + 1) * HEAD] = carry[hh][0]

    blk = lambda off: pl.BlockSpec((t, wg), lambda h, i: (i, off + h))
    full = lambda off: pl.BlockSpec((l, wg), lambda h, i: (0, off + h))
    acc = pl.BlockSpec((hp, nq, HEAD, t), lambda h, i: (h, 0, 0, 0))
    acc_shape = jax.ShapeDtypeStruct((nh, nq, HEAD, t), F32)
    outs = _call(
        body, name=name, grid=(ng, nq),
        in_specs=[blk(0), full(ng), full(2 * ng), blk(3 * wa // wg), blk(wa // wg), blk(0)],
        out_specs=[blk(0), acc, acc], out_shape=[jax.ShapeDtypeStruct((l, wb), F32), acc_shape, acc_shape],
        scratch_shapes=[pltpu.VMEM((t, wg), BF16), pltpu.VMEM((hp, HEAD, t), BF16), pltpu.VMEM((hp, HEAD, t), BF16)],
        args=(qkv, qkv, qkv, proj, dcat, lktot), sem=("parallel", "arbitrary"), ride=ride)
    untranspose = lambda a: a.transpose(1, 3, 0, 2).reshape(l, wb)
    return outs[0], untranspose(outs[1]), untranspose(outs[2]), outs[3:]


def _sgu_heads(v, g_ref, w_ref, bt_ref, nh):
    keep = _tri(lambda r, c: r >= c)
    out = []
    for h in range(nh):
        cols = slice(h * HEAD, (h + 1) * HEAD)
        nv, r = _rownorm(v[:, cols])
        wm = jnp.where(keep, w_ref[h], 0.0)
        s = _dot(_bf(wm), _bf(nv * g_ref[:, cols])) + bt_ref[:, h:h + 1]
        out.append((nv, r, wm, s))
    return out


def _sgu_fwd(proj, out_b, norm_g, sgu_w, sgu_bt, wa, wb, name):
    l, n = proj.shape
    nh = wa // HEAD

    def body(au_ref, av_ref, az_ref, bz_ref, ob_ref, g_ref, w_ref, bt_ref, cat_ref):
        u, v, sz = _gelu(au_ref[...]), _gelu(av_ref[...]), _silu(az_ref[...])
        for h, (_, _, _, s) in enumerate(_sgu_heads(v, g_ref, w_ref, bt_ref, nh)):
            cols = slice(h * HEAD, (h + 1) * HEAD)
            cat_ref[:, cols] = _bf(u[:, cols] * s * sz[:, cols])
        cat_ref[:, wa:] = _bf(ob_ref[...] * _silu(bz_ref[...]))

    a_blk = lambda j: pl.BlockSpec((HEAD, wa), lambda i: (i, j))
    return pl.pallas_call(
        body, name=name, grid=(l // HEAD,),
        in_specs=[a_blk(0), a_blk(1), a_blk(2), a_blk(3), pl.BlockSpec((HEAD, wb), lambda i: (i, 0)),
                  _vec_spec(wa), pl.BlockSpec((nh, HEAD, HEAD), lambda i: (0, 0, 0)),
                  pl.BlockSpec((HEAD, nh), lambda i: (0, 0))],
        out_specs=pl.BlockSpec((HEAD, wa + wb), lambda i: (i, 0)),
        out_shape=jax.ShapeDtypeStruct((l, wa + wb), BF16),
        compiler_params=_params(("parallel",)),
    )(proj, proj, proj, proj, out_b, norm_g, sgu_w, sgu_bt)


def _sgu_bwd(proj, out_b, dcat, dq, dk, dv, norm_g, sgu_w, sgu_bt, wa, wb, name):
    l = proj.shape[0]
    n = 3 * wa + 4 * wb
    nh = wa // HEAD

    def body(au_ref, av_ref, az_ref, bz_ref, ob_ref, dc_ref, dq_ref, dk_ref, dv_ref, g_ref, w_ref, bt_ref,
             dp_ref, dw_ref, dbt_ref, dg_ref):
        first = pl.program_id(0) == 0
        keep = _tri(lambda r, c: r >= c)
        au, av, az = au_ref[...], av_ref[...], az_ref[...]
        u, v, sz = _gelu(au), _gelu(av), _silu(az)
        dgelu_u, dgelu_v, dsilu_z = _gelu_grad(au), _gelu_grad(av), _silu_grad(az)
        dg_parts = []
        for h, (nv, r, wm, s) in enumerate(_sgu_heads(v, g_ref, w_ref, bt_ref, nh)):
            cols = slice(h * HEAD, (h + 1) * HEAD)
            dca, uh, szh, gh = dc_ref[:, cols], u[:, cols], sz[:, cols], g_ref[:, cols]
            dp_ref[:, cols] = _bf(dca * s * szh * dgelu_u[:, cols])
            dp_ref[:, 2 * wa + h * HEAD:2 * wa + (h + 1) * HEAD] = _bf(dca * uh * s * dsilu_z[:, cols])
            ds = dca * uh * szh
            _acc(dw_ref.at[h], first, jnp.where(keep, _dot(_bf(ds), _bf(nv * gh), NT), 0.0))
            _acc(dbt_ref.at[:, h:h + 1], first, jnp.sum(ds, axis=1, keepdims=True))
            dvh = _dot(_bf(wm.T), _bf(ds))
            dg_parts.append(_colsum(dvh * nv))
            dnv = dvh * gh
            dvv = r * (dnv - nv * jnp.mean(dnv * nv, axis=-1, keepdims=True))
            dp_ref[:, wa + h * HEAD:wa + (h + 1) * HEAD] = _bf(dvv * dgelu_v[:, cols])
        _acc(dg_ref, first, jnp.concatenate(dg_parts, axis=1))
        base = 3 * wa
        dp_ref[:, base:base + wb] = _bf(dq_ref[...])
        dp_ref[:, base + wb:base + 2 * wb] = _bf(dk_ref[...])
        dp_ref[:, base + 2 * wb:base + 3 * wb] = _bf(dv_ref[...])
        dp_ref[:, base + 3 * wb:] = _bf(dc_ref[:, wa:] * ob_ref[...] * _silu_grad(bz_ref[...]))

    a_blk = lambda j: pl.BlockSpec((HEAD, wa), lambda i: (i, j))
    b_blk = pl.BlockSpec((HEAD, wb), lambda i: (i, 0))
    w_spec = pl.BlockSpec((nh, HEAD, HEAD), lambda i: (0, 0, 0))
    bt_spec = pl.BlockSpec((HEAD, nh), lambda i: (0, 0))
    return pl.pallas_call(
        body, name=name, grid=(l // HEAD,),
        in_specs=[a_blk(0), a_blk(1), a_blk(2), a_blk(3), b_blk, pl.BlockSpec((HEAD, wa + wb), lambda i: (i, 0)),
                  b_blk, b_blk, b_blk, _vec_spec(wa), w_spec, bt_spec],
        out_specs=[pl.BlockSpec((HEAD, n), lambda i: (i, 0)), w_spec, bt_spec, _vec_spec(wa)],
        out_shape=[jax.ShapeDtypeStruct((l, n), BF16), jax.ShapeDtypeStruct((nh, HEAD, HEAD), F32),
                   jax.ShapeDtypeStruct((HEAD, nh), F32), jax.ShapeDtypeStruct((1, wa), F32)],
        compiler_params=_params(("arbitrary",)),
    )(proj, proj, proj, proj, out_b, dcat, dq, dk, dv, norm_g, sgu_w, sgu_bt)


def _ssm_discretise(lr, li, ldt, br, bi):
    dt = jnp.exp(ldt)
    mag = jnp.exp(lr * dt)
    a_re = mag * jnp.cos(li * dt)
    a_im = mag * jnp.sin(li * dt)
    den = lr * lr + li * li
    nr = a_re - 1.0
    coef_re = (nr * lr + a_im * li) / den
    coef_im = (a_im * lr - nr * li) / den
    return a_re, a_im, coef_re * br - coef_im * bi, coef_re * bi + coef_im * br


def _ssm_prep(lr, li, ldt, br, bi, lr_row, li_row, ldt_row, name):
    s, c = br.shape

    def body(lr_ref, li_ref, ldt_ref, br_ref, bi_ref, lrr_ref, lir_ref, ldtr_ref, bbr_ref, bbi_ref, tr_ref, ti_ref):
        _, _, bbr, bbi = _ssm_discretise(lr_ref[...], li_ref[...], ldt_ref[...], br_ref[...], bi_ref[...])
        bbr_ref[...] = bbr
        bbi_ref[...] = bbi
        row = lax.broadcasted_iota(jnp.int32, (SCAN_ROWS, 1), 0)
        blk, r = jnp.right_shift(row, 3), jnp.bitwise_and(row, 7)
        kind, rev = jnp.bitwise_and(blk, 3), blk >= 4
        step = jnp.left_shift(1, kind)
        n = jnp.where(kind < 3, step, jnp.where(rev, 8 - r, r + 1)).astype(F32)
        keep = (kind == 3) | (rev & (r < 8 - step)) | (jnp.logical_not(rev) & (r >= step))
        dt = jnp.exp(ldtr_ref[...])
        mag = jnp.exp(n * (lrr_ref[...] * dt))
        ang = n * (lir_ref[...] * dt)
        tr_ref[...] = jnp.where(keep, mag * jnp.cos(ang), 0.0)
        ti_ref[...] = jnp.where(keep, jnp.where(rev, -1.0, 1.0) * mag * jnp.sin(ang), 0.0)

    col = jax.ShapeDtypeStruct((s, c), F32)
    row = jax.ShapeDtypeStruct((SCAN_ROWS, s), F32)
    return pl.pallas_call(body, name=name, out_shape=[col, col, row, row])(
        lr, li, ldt, br, bi, lr_row, li_row, ldt_row)


def _ssm_prep_bwd(lr, li, ldt, br, bi, da_re, da_im, dbb_re, dbb_im, p, name):
    s, c = br.shape

    def body(lr_ref, li_ref, ldt_ref, br_ref, bi_ref, dar_ref, dai_ref, dbr_ref, dbi_ref,
             dlr_ref, dli_ref, dldt_ref, dbre_ref, dbim_ref):
        args = (lr_ref[...], li_ref[...], ldt_ref[...], br_ref[...], bi_ref[...])
        _, vjp = jax.vjp(_ssm_discretise, *args)
        dlr, dli, dldt, dbr, dbi = vjp((dar_ref[...], dai_ref[...], dbr_ref[...], dbi_ref[...]))
        dlr_ref[...] = dlr
        dli_ref[...] = dli
        dbre_ref[...] = dbr
        dbim_ref[...] = dbi
        idx = lax.broadcasted_iota(jnp.int32, (s, s // p), 0)
        grp = lax.broadcasted_iota(jnp.int32, (s, s // p), 1)
        own = (idx >= grp * p) & (idx < (grp + 1) * p)
        dldt_ref[...] = _colsum(jnp.where(own, dldt, 0.0))

    col1 = jax.ShapeDtypeStruct((s, 1), F32)
    colc = jax.ShapeDtypeStruct((s, c), F32)
    return pl.pallas_call(
        body, name=name, out_shape=[col1, col1, jax.ShapeDtypeStruct((1, s // p), F32), colc, colc],
    )(lr, li, ldt, br, bi, da_re, da_im, dbb_re, dbb_im)


SCAN_ROWS = 64


def _scan_groups(xr, xi, tr_ref, ti_ref, cr, ci, reverse):
    ng = xr.shape[0] // 8
    base = SCAN_ROWS // 2 if reverse else 0
    pr, pi = tr_ref[base + 24:base + 32, :], ti_ref[base + 24:base + 32, :]
    edge = slice(0, 1) if reverse else slice(7, 8)
    out_r, out_i = [None] * ng, [None] * ng
    for g in (range(ng - 1, -1, -1) if reverse else range(ng)):
        sr, si = xr[8 * g:8 * g + 8, :], xi[8 * g:8 * g + 8, :]
        for k in range(3):
            ar, ai = tr_ref[base + 8 * k:base + 8 * k + 8, :], ti_ref[base + 8 * k:base + 8 * k + 8, :]
            shift = 8 - (1 << k) if reverse else 1 << k
            rr, ri = pltpu.roll(sr, shift, 0), pltpu.roll(si, shift, 0)
            sr, si = sr + ar * rr - ai * ri, si + ar * ri + ai * rr
        sr, si = sr + pr * cr - pi * ci, si + pr * ci + pi * cr
        cr, ci = sr[edge, :], si[edge, :]
        out_r[g], out_i[g] = sr, si
    return jnp.concatenate(out_r, axis=0), jnp.concatenate(out_i, axis=0), cr, ci


def _ssm_fwd(proj, bbd, ccd, pw_re, pw_im, d_skip, w, name):
    l = proj.shape[0]
    nb, cw, ns2 = bbd.shape
    ns = ns2 // 2
    nc = l // SSM_T

    def body(u_ref, bbd_ref, ccd_ref, pr_ref, pi_ref, d_ref, y_ref, hsr_ref, hsi_ref, hr_s, hi_s):
        @pl.when(pl.program_id(1) == 0)
        def _():
            hr_s[...] = jnp.zeros_like(hr_s)
            hi_s[...] = jnp.zeros_like(hi_s)

        hsr_ref[...] = hr_s[...].reshape(hsr_ref.shape)
        hsi_ref[...] = hi_s[...].reshape(hsi_ref.shape)
        u = u_ref[...]
        bu = _dot(_bf(u), bbd_ref[0])
        hr, hi, cr, ci = _scan_groups(bu[:, :ns], bu[:, ns:], pr_ref, pi_ref, hr_s[...], hi_s[...], False)
        hr_s[...] = cr
        hi_s[...] = ci
        y_ref[...] = _dot(_bf(jnp.concatenate([hr, hi], axis=1)), ccd_ref[0]) + d_ref[...] * u

    tab = pl.BlockSpec((SCAN_ROWS, ns), lambda b, k: (0, b))
    return pl.pallas_call(
        body, name=name, grid=(nb, nc),
        in_specs=[pl.BlockSpec((SSM_T, cw), lambda b, k: (k, b)),
                  pl.BlockSpec((1, cw, ns2), lambda b, k: (b, 0, 0)),
                  pl.BlockSpec((1, ns2, cw), lambda b, k: (b, 0, 0)),
                  tab, tab, pl.BlockSpec((1, cw), lambda b, k: (0, b))],
        out_specs=[pl.BlockSpec((SSM_T, cw), lambda b, k: (k, b)),
                   pl.BlockSpec((1, 1, ns), lambda b, k: (k, 0, b)), pl.BlockSpec((1, 1, ns), lambda b, k: (k, 0, b))],
        out_shape=[jax.ShapeDtypeStruct((l, w), F32), jax.ShapeDtypeStruct((nc, 1, nb * ns), F32),
                   jax.ShapeDtypeStruct((nc, 1, nb * ns), F32)],
        scratch_shapes=[pltpu.VMEM((1, ns), F32), pltpu.VMEM((1, ns), F32)],
        compiler_params=_params(("parallel", "arbitrary")),
    )(proj, bbd, ccd, pw_re, pw_im, d_skip)


def _ssm_bwd(proj, dy, hs_re, hs_im, bbd, ccd, pw_re, pw_im, d_skip, w, name, ride=None):
    l = proj.shape[0]
    nb, cw, ns2 = bbd.shape
    ns = ns2 // 2
    nc = l // SSM_T

    def body(u_ref, dy_ref, hsr_ref, hsi_ref, bbd_ref, ccd_ref, pr_ref, pi_ref, d_ref,
             du_ref, dbbd_ref, dccd_ref, dar_ref, dai_ref, dd_ref, gr_s, gi_s):
        first = pl.program_id(1) == 0

        @pl.when(first)
        def _():
            gr_s[...] = jnp.zeros_like(gr_s)
            gi_s[...] = jnp.zeros_like(gi_s)

        u, dy = u_ref[...], dy_ref[...]
        u_bf, dy_bf = _bf(u), _bf(dy)
        hr0, hi0 = hsr_ref[0], hsi_ref[0]
        bu = _dot(u_bf, bbd_ref[0])
        hr, hi, _, _ = _scan_groups(bu[:, :ns], bu[:, ns:], pr_ref, pi_ref, hr0, hi0, False)
        dh = _dot(dy_bf, ccd_ref[0], NT)
        gr, gi, gcr, gci = _scan_groups(dh[:, :ns], dh[:, ns:], pr_ref, pi_ref, gr_s[...], gi_s[...], True)
        gr_s[...] = gcr
        gi_s[...] = gci
        row0 = lax.broadcasted_iota(jnp.int32, hr.shape, 0) == 0
        pr_h = jnp.where(row0, hr0, pltpu.roll(hr, 1, 0))
        pi_h = jnp.where(row0, hi0, pltpu.roll(hi, 1, 0))
        _acc(dar_ref, first, _colsum(pr_h * gr + pi_h * gi))
        _acc(dai_ref, first, _colsum(pr_h * gi - pi_h * gr))
        g_bf = _bf(jnp.concatenate([gr, gi], axis=1))
        _acc(dbbd_ref.at[0], first, _dot(_bf(u.T), g_bf))
        _acc(dccd_ref.at[0], first, _dot(_bf(jnp.concatenate([hr, hi], axis=1).T), dy_bf))
        du_ref[...] = _bf(_dot(g_bf, bbd_ref[0], NT) + d_ref[...] * dy)
        _acc(dd_ref, first, _colsum(dy * u))

    rev = lambda b, k: (nc - 1 - k, b)
    outs = _call(
        body, name=name, grid=(nb, nc), ride=ride, sem=("parallel", "arbitrary"),
        args=(proj, dy, hs_re, hs_im, bbd, ccd, pw_re, pw_im, d_skip),
        in_specs=[pl.BlockSpec((SSM_T, cw), rev), pl.BlockSpec((SSM_T, cw), rev),
                  pl.BlockSpec((1, 1, ns), lambda b, k: (nc - 1 - k, 0, b)),
                  pl.BlockSpec((1, 1, ns), lambda b, k: (nc - 1 - k, 0, b)),
                  pl.BlockSpec((1, cw, ns2), lambda b, k: (b, 0, 0)),
                  pl.BlockSpec((1, ns2, cw), lambda b, k: (b, 0, 0)),
                  pl.BlockSpec((SCAN_ROWS, ns), lambda b, k: (0, b)), pl.BlockSpec((SCAN_ROWS, ns), lambda b, k: (0, b)),
                  pl.BlockSpec((1, cw), lambda b, k: (0, b))],
        out_specs=[pl.BlockSpec((SSM_T, cw), rev),
                   pl.BlockSpec((1, cw, ns2), lambda b, k: (b, 0, 0)),
                   pl.BlockSpec((1, ns2, cw), lambda b, k: (b, 0, 0)),
                   pl.BlockSpec((1, ns), lambda b, k: (0, b)), pl.BlockSpec((1, ns), lambda b, k: (0, b)),
                   pl.BlockSpec((1, cw), lambda b, k: (0, b))],
        out_shape=[jax.ShapeDtypeStruct((l, w), BF16), jax.ShapeDtypeStruct(bbd.shape, F32),
                   jax.ShapeDtypeStruct(ccd.shape, F32), jax.ShapeDtypeStruct((1, nb * ns), F32),
                   jax.ShapeDtypeStruct((1, nb * ns), F32), jax.ShapeDtypeStruct((1, w), F32)],
        scratch_shapes=[pltpu.VMEM((1, ns), F32), pltpu.VMEM((1, ns), F32)])
    return (*outs[:6], outs[6:])


def _block_diag_b(bb_re, bb_im, g, p, c):
    nb = g // SSM_GB
    eye = jnp.eye(SSM_GB, dtype=F32)

    def one(bb):
        t = bb.reshape(nb, SSM_GB, p, c).transpose(0, 1, 3, 2)
        return (t[:, :, :, None, :] * eye[None, :, None, :, None]).reshape(nb, SSM_GB * c, SSM_GB * p)

    return jnp.concatenate([one(bb_re), one(bb_im)], axis=2)


def _block_diag_c(c_re, c_im, g, p, c):
    nb = g // SSM_GB
    eye = jnp.eye(SSM_GB, dtype=F32)

    def one(cc):
        t = cc.reshape(nb, SSM_GB, c, p).transpose(0, 1, 3, 2)
        return (t[:, :, :, None, :] * eye[None, :, None, :, None]).reshape(nb, SSM_GB * p, SSM_GB * c)

    return jnp.concatenate([one(c_re), one(-c_im)], axis=1)


def _diag_of_b(dbbd, g, p, c):
    nb = g // SSM_GB
    t = dbbd.reshape(nb, SSM_GB, c, 2, SSM_GB, p)
    idx = jnp.arange(SSM_GB)
    d = t[:, idx, :, :, idx, :]
    d = d.transpose(1, 0, 3, 4, 2)
    return d[:, :, 0].reshape(g * p, c), d[:, :, 1].reshape(g * p, c)


def _diag_of_c(dccd, g, p, c):
    nb = g // SSM_GB
    t = dccd.reshape(nb, 2, SSM_GB, p, SSM_GB, c)
    idx = jnp.arange(SSM_GB)
    d = t[:, :, idx, :, idx, :]
    d = d.transpose(1, 0, 2, 4, 3)
    return d[:, :, 0].reshape(g, c, p), -d[:, :, 1].reshape(g, c, p)


def _glu_fwd(y, proj, w_glu, b_glu, name):
    l, w = y.shape
    tm = _tile(l, 256)

    def body(y_ref, z_ref, w_ref, b_ref, o_ref):
        g = _gelu(y_ref[...])
        t = _dot(_bf(g), w_ref[...]) + b_ref[...]
        o_ref[...] = _bf(g * _sigmoid(t) * _silu(z_ref[...]))

    return pl.pallas_call(
        body, name=name, grid=(l // tm,),
        in_specs=[_row_spec(tm, w), pl.BlockSpec((tm, w), lambda i: (i, 1)),
                  pl.BlockSpec((w, w), lambda i: (0, 0)), _vec_spec(w)],
        out_specs=_row_spec(tm, w), out_shape=jax.ShapeDtypeStruct((l, w), BF16),
        compiler_params=_params(("parallel",)),
    )(y, proj, w_glu, b_glu)


def _glu_bwd(do, y, proj, w_glu, b_glu, name):
    l, w = y.shape
    tm = _tile(l, 256)
    nsteps = l // tm

    def body(do_ref, y_ref, z_ref, w_ref, b_ref, dy_ref, dz_ref, dw_ref, db_ref, dw_acc):
        i = pl.program_id(0)
        first = i == 0
        yv, z, do = y_ref[...], z_ref[...], do_ref[...]
        g = _gelu(yv)
        g_bf = _bf(g)
        sg = _sigmoid(_dot(g_bf, w_ref[...]) + b_ref[...])
        dyy = do * _silu(z)
        dz_ref[...] = _bf(do * g * sg * _silu_grad(z))
        dt = dyy * g * sg * (1.0 - sg)
        dt_bf = _bf(dt)
        dg = dyy * sg + _dot(dt_bf, w_ref[...], NT)
        dy_ref[...] = dg * _gelu_grad(yv)
        _acc(dw_acc, first, _dot(_bf(g.T), dt_bf))
        _acc(db_ref, first, _colsum(dt))

        @pl.when(i == nsteps - 1)
        def _():
            dw_ref[...] = _bf(dw_acc[...])

    return pl.pallas_call(
        body, name=name, grid=(nsteps,),
        in_specs=[_row_spec(tm, w), _row_spec(tm, w), pl.BlockSpec((tm, w), lambda i: (i, 1)),
                  pl.BlockSpec((w, w), lambda i: (0, 0)), _vec_spec(w)],
        out_specs=[_row_spec(tm, w), _row_spec(tm, w), pl.BlockSpec((w, w), lambda i: (0, 0)), _vec_spec(w)],
        out_shape=[jax.ShapeDtypeStruct((l, w), F32), jax.ShapeDtypeStruct((l, w), BF16),
                   jax.ShapeDtypeStruct((w, w), BF16), jax.ShapeDtypeStruct((1, w), F32)],
        scratch_shapes=[pltpu.VMEM((w, w), F32)],
        compiler_params=_params(("arbitrary",)),
    )(do, y, proj, w_glu, b_glu)


MOD_ROWS = 128


def _mod_fwd(cond_pad, w_mod, b_shard, name):
    nl, d, ncol = w_mod.shape
    tn = _tile(ncol, 512)

    def body(c_ref, w_ref, b_ref, o_ref):
        o_ref[0] = _dot(_bf(c_ref[...]), _bf(w_ref[0])) + b_ref[0]

    return pl.pallas_call(
        body, name=name, grid=(nl, ncol // tn),
        in_specs=[pl.BlockSpec((MOD_ROWS, d), lambda a, j: (0, 0)),
                  pl.BlockSpec((1, d, tn), lambda a, j: (a, 0, j)),
                  pl.BlockSpec((1, 1, tn), lambda a, j: (a, 0, j))],
        out_specs=pl.BlockSpec((1, MOD_ROWS, tn), lambda a, j: (a, 0, j)),
        out_shape=jax.ShapeDtypeStruct((nl, MOD_ROWS, ncol), F32),
        compiler_params=_params(("parallel", "parallel")),
    )(cond_pad, w_mod, b_shard)


def _mod_bwd(cond_pad_t, dmod_pad, name):
    nl, _, ncol = dmod_pad.shape
    d = cond_pad_t.shape[0]
    tn = _tile(ncol, 512)

    def body(c_ref, dm_ref, o_ref):
        o_ref[0] = _dot(_bf(c_ref[...]), _bf(dm_ref[0]))

    return pl.pallas_call(
        body, name=name, grid=(nl, ncol // tn),
        in_specs=[pl.BlockSpec((d, MOD_ROWS), lambda a, j: (0, 0)),
                  pl.BlockSpec((1, MOD_ROWS, tn), lambda a, j: (a, 0, j))],
        out_specs=pl.BlockSpec((1, d, tn), lambda a, j: (a, 0, j)),
        out_shape=jax.ShapeDtypeStruct((nl, d, ncol), F32),
        compiler_params=_params(("parallel", "parallel")),
    )(cond_pad_t, dmod_pad)


def _silu_rows(c2d, name):
    def body(c_ref, o_ref):
        o_ref[...] = _silu(c_ref[...])

    return pl.pallas_call(body, name=name, out_shape=jax.ShapeDtypeStruct(c2d.shape, F32))(c2d)


def _sum_leading(x, name):
    n, r, c = x.shape
    tr = _tile(r, max(16, (1 << 20) // (4 * c)), 16 if r % 16 == 0 else 8)

    def body(x_ref, o_ref):
        acc = x_ref[0].astype(F32)
        for k in range(1, n):
            acc = acc + x_ref[k].astype(F32)
        o_ref[...] = acc

    return pl.pallas_call(
        body, name=name, grid=(r // tr,),
        in_specs=[pl.BlockSpec((n, tr, c), lambda i: (0, i, 0))], out_specs=pl.BlockSpec((tr, c), lambda i: (i, 0)),
        out_shape=jax.ShapeDtypeStruct((r, c), F32), compiler_params=_params(("parallel",)),
    )(x)


def _adamw(w, gs, m, v, name):
    r, c = w.shape
    tr = _tile(r, max(8, (1 << 19) // (4 * c)), 8)
    ng = len(gs)

    def body(*refs):
        w_ref, g_refs, m_ref, v_ref = refs[0], refs[1:1 + ng], refs[1 + ng], refs[2 + ng]
        g_ref, d_ref, nm_ref, nv_ref = refs[3 + ng:]
        g = g_refs[0][...]
        for extra in g_refs[1:]:
            g = g + extra[...]
        g_ref[...] = g
        nm = ADAM_B1 * m_ref[...] + (1.0 - ADAM_B1) * g
        nv = ADAM_B2 * v_ref[...] + (1.0 - ADAM_B2) * (g * g)
        nm_ref[...] = nm
        nv_ref[...] = nv
        m_hat = nm / (1.0 - ADAM_B1 ** ADAM_STEP)
        v_hat = nv / (1.0 - ADAM_B2 ** ADAM_STEP)
        d_ref[...] = -ADAM_LR * (m_hat / (jnp.sqrt(v_hat) + ADAM_EPS) + ADAM_WD * w_ref[...])

    spec = pl.BlockSpec((tr, c), lambda i: (i, 0))
    shp = jax.ShapeDtypeStruct((r, c), F32)
    return pl.pallas_call(
        body, name=name, grid=(r // tr,), in_specs=[spec] * (3 + ng), out_specs=[spec] * 4,
        out_shape=[shp] * 4, compiler_params=_params(("parallel",)),
    )(w, *gs, m, v)


ANY = pl.BlockSpec(memory_space=pl.ANY)


def _flip(v, bit):
    return 1 - v if bit else v


def _allgather8(x, name):
    def body(x_ref, o_ref, send_sems, recv_sems, local_sem):
        mx, my, mc = lax.axis_index("x"), lax.axis_index("y"), lax.axis_index("c")
        me = 4 * mx + 2 * my + mc
        mine = pltpu.make_async_copy(x_ref, o_ref.at[me], local_sem)
        mine.start()

        def copy(j):
            peer = (_flip(mx, j & 4), _flip(my, j & 2), _flip(mc, j & 1))
            return pltpu.make_async_remote_copy(
                src_ref=x_ref, dst_ref=o_ref.at[me], send_sem=send_sems.at[j - 1], recv_sem=recv_sems.at[j - 1],
                device_id=peer, device_id_type=MESH)

        def landing(j):
            peer = (_flip(mx, j & 4), _flip(my, j & 2), _flip(mc, j & 1))
            slot = 4 * peer[0] + 2 * peer[1] + peer[2]
            return pltpu.make_async_remote_copy(
                src_ref=x_ref, dst_ref=o_ref.at[slot], send_sem=send_sems.at[j - 1], recv_sem=recv_sems.at[j - 1],
                device_id=peer, device_id_type=MESH)

        for j in range(1, 8):
            copy(j).start()
        for j in range(1, 8):
            landing(j).wait()
        mine.wait()

    return pl.pallas_call(
        body, name=name, in_specs=[ANY], out_specs=ANY,
        out_shape=jax.ShapeDtypeStruct((8,) + x.shape, x.dtype),
        scratch_shapes=[pltpu.SemaphoreType.DMA((7,)), pltpu.SemaphoreType.DMA((7,)), pltpu.SemaphoreType.DMA],
    )(x)


def _chip_exchange(xs, gather, name):
    n = len(xs)

    def body(*refs):
        start, wait = _chip_exchange_ops(refs[:n], refs[n:2 * n], *refs[2 * n:], gather)
        start()
        wait()

    return pl.pallas_call(
        body, name=name, in_specs=[ANY] * n, out_specs=[ANY] * n, out_shape=_chip_exchange_shapes(xs, gather),
        scratch_shapes=_chip_exchange_sems(n),
    )(*xs)


def _chip_exchange_shapes(xs, gather):
    return [jax.ShapeDtypeStruct(((4,) + x.shape) if gather else x.shape, x.dtype) for x in xs]


def _chip_exchange_sems(n):
    return [pltpu.SemaphoreType.DMA((3 * n,)), pltpu.SemaphoreType.DMA((3 * n,)), pltpu.SemaphoreType.DMA((n,))]


def _chip_exchange_ops(x_refs, o_refs, send_sems, recv_sems, local_sems, gather):
    n = len(x_refs)
    mx, my, mc = lax.axis_index("x"), lax.axis_index("y"), lax.axis_index("c")
    k0 = 2 * mx + my

    def local(a):
        src = x_refs[a] if gather else x_refs[a].at[k0]
        return pltpu.make_async_copy(src, o_refs[a].at[k0], local_sems.at[a])

    def copy(a, j, outgoing):
        px, py = _flip(mx, j & 2), _flip(my, j & 1)
        kp = 2 * px + py
        if outgoing:
            src = x_refs[a] if gather else x_refs[a].at[kp]
            dst = o_refs[a].at[k0]
        else:
            src = x_refs[a] if gather else x_refs[a].at[k0]
            dst = o_refs[a].at[kp]
        s = a * 3 + j - 1
        return pltpu.make_async_remote_copy(
            src_ref=src, dst_ref=dst, send_sem=send_sems.at[s], recv_sem=recv_sems.at[s],
            device_id=(px, py, mc), device_id_type=MESH)

    def start():
        for a in range(n):
            local(a).start()
            for j in range(1, 4):
                copy(a, j, True).start()

    def wait():
        for a in range(n):
            for j in range(1, 4):
                copy(a, j, False).wait()
            local(a).wait()

    return start, wait


def _call(body, *, name, grid, in_specs, out_specs, out_shape, args, scratch_shapes=(), sem=None, ride=None):
    if ride is None:
        return pl.pallas_call(
            body, name=name, grid=grid, in_specs=list(in_specs), out_specs=list(out_specs), out_shape=list(out_shape),
            scratch_shapes=list(scratch_shapes), compiler_params=_params(sem))(*args)
    xs, gather = ride
    n_in, n_out, n_scr, nx = len(in_specs), len(out_specs), len(scratch_shapes), len(xs)

    def wrapped(*refs):
        ins, x_refs = refs[:n_in], refs[n_in:n_in + nx]
        outs = refs[n_in + nx:n_in + nx + n_out]
        lands = refs[n_in + nx + n_out:n_in + 2 * nx + n_out]
        rest = refs[n_in + 2 * nx + n_out:]
        scr, sems = rest[:n_scr], rest[n_scr:]
        start, wait = _chip_exchange_ops(x_refs, lands, *sems, gather)
        ids = [pl.program_id(a) for a in range(len(grid))]
        first = functools.reduce(jnp.logical_and, [i == 0 for i in ids])
        last = functools.reduce(jnp.logical_and, [i == g - 1 for i, g in zip(ids, grid)])
        pl.when(first)(start)
        body(*ins, *outs, *scr)
        pl.when(last)(wait)

    return pl.pallas_call(
        wrapped, name=name, grid=grid, in_specs=list(in_specs) + [ANY] * nx, out_specs=list(out_specs) + [ANY] * nx,
        out_shape=list(out_shape) + _chip_exchange_shapes(xs, gather),
        scratch_shapes=list(scratch_shapes) + _chip_exchange_sems(nx),
        compiler_params=_params(("arbitrary",) * len(grid)))(*args, *xs)


def _sibling_exchange(xs, name):
    n = len(xs)

    def body(*refs):
        x_refs, o_refs = refs[:n], refs[n:2 * n]
        send_sems, recv_sems = refs[2 * n:]
        sib = (lax.axis_index("x"), lax.axis_index("y"), 1 - lax.axis_index("c"))
        copies = [pltpu.make_async_remote_copy(
            src_ref=x_refs[a], dst_ref=o_refs[a], send_sem=send_sems.at[a], recv_sem=recv_sems.at[a],
            device_id=sib, device_id_type=MESH) for a in range(n)]
        for cp in copies:
            cp.start()
        for cp in copies:
            cp.wait()

    return pl.pallas_call(
        body, name=name, in_specs=[ANY] * n, out_specs=[ANY] * n,
        out_shape=[jax.ShapeDtypeStruct(x.shape, x.dtype) for x in xs],
        scratch_shapes=[pltpu.SemaphoreType.DMA((n,)), pltpu.SemaphoreType.DMA((n,))],
    )(*xs)


PACK = 1024


def _pack(parts):
    flat = []
    for p in parts:
        v = p.reshape(-1).astype(F32)
        flat.append(jnp.pad(v, (0, (-v.shape[0]) % PACK)))
    return jnp.concatenate(flat).reshape(-1, 128)


def _unpack(packed, shapes):
    flat = packed.reshape(-1)
    out, off = [], 0
    for shp in shapes:
        n = math.prod(shp)
        out.append(flat[off:off + n].reshape(shp))
        off += n + (-n) % PACK
    return out


def kernel(x, c, ln_pre_g, ln_post_g, w_mod, b_mod, w_in_ab, w_out_ab, sgu_norm_g, sgu_w, sgu_b, w_in_ssm, w_out_ssm, lam_re, lam_im, b_re, b_im, c_re, c_im, d_skip, log_dt, w_glu, b_glu, loss_target, m_ln_pre_g, m_ln_post_g, m_w_mod, m_b_mod, m_w_in_ab, m_w_out_ab, m_sgu_norm_g, m_sgu_w, m_sgu_b, m_w_in_ssm, m_w_out_ssm, m_lam_re, m_lam_im, m_b_re, m_b_im, m_c_re, m_c_im, m_d_skip, m_log_dt, m_w_glu, m_b_glu, v_ln_pre_g, v_ln_post_g, v_w_mod, v_b_mod, v_w_in_ab, v_w_out_ab, v_sgu_norm_g, v_sgu_w, v_sgu_b, v_w_in_ssm, v_w_out_ssm, v_lam_re, v_lam_im, v_b_re, v_b_im, v_c_re, v_c_im, v_d_skip, v_log_dt, v_w_glu, v_b_glu):
    given = dict(locals())
    mx, my, mc = lax.axis_index("x"), lax.axis_index("y"), lax.axis_index("c")
    me = 4 * mx + 2 * my + mc
    chip = 2 * mx + my

    _, l, d = x.shape
    x2, tgt = x[0], loss_target[0]
    n_in = w_in_ab.shape[2] * 4
    wa = wb = n_in // 7
    w = w_out_ssm.shape[1]
    g, p, cch = b_re.shape[1:]
    nmod = w_mod.shape[2]

    (gw_in_ab,) = _chip_exchange([_bf(w_in_ab[0])], True, "gather_w_in_ab")
    win_ab = jnp.concatenate([gw_in_ab[k] for k in range(4)], axis=1)
    later_shards = [_bf(w_out_ab[0]), _bf(w_in_ssm[0]), _bf(w_out_ssm[0]), _bf(w_glu[0]), d_skip, b_glu]

    cond = _silu_rows(c.reshape(d // 128, 128), "cond_silu")
    cond_all = _allgather8(cond, "gather_cond").reshape(8, d)
    b_shard = lax.dynamic_slice(b_mod, (0, chip * nmod), (2, nmod)).reshape(2, 1, nmod)
    cond_pad = jnp.pad(cond_all, ((0, MOD_ROWS - 8), (0, 0)))
    modp = _mod_fwd(cond_pad, w_mod, b_shard, "mod_fwd")[:, :8]
    modp_all = _allgather8(modp.reshape(16, nmod), "gather_mod").reshape(4, 2, 2, 8, nmod)
    mine = lax.dynamic_index_in_dim(lax.dynamic_index_in_dim(modp_all, mc, 1, False), me, 2, False)
    mod = mine.transpose(1, 0, 2).reshape(2, 3 * d)
    shift = [mod[a:a + 1, :d] for a in range(2)]
    scale = [mod[a:a + 1, d:2 * d] for a in range(2)]
    gate = [mod[a:a + 1, 2 * d:] for a in range(2)]
    pre_g = [ln_pre_g[a:a + 1] for a in range(2)]
    post_g = [ln_post_g[a:a + 1] for a in range(2)]

    sgu_w0, sgu_bt = sgu_w[0], sgu_b[0].T
    h0 = _pre_fwd(x2, pre_g[0], scale[0], shift[0], "pre0_fwd")
    w_gates = jnp.concatenate([win_ab[:, :3 * wa], win_ab[:, 3 * wa + 3 * wb:]], axis=1)
    proj0 = _matmul(h0, w_gates, "nn", F32, "proj0", tm=1024)
    qkv = _matmul(h0, win_ab[:, 3 * wa:3 * wa + 3 * wb], "nn", BF16, "proj0_qkv", tm=1024)
    out_b, lktot, (gw_out_ab, gw_in_ssm, gw_out_ssm, gw_glu, g_dskip, g_bglu) = _attn_fwd(
        qkv, wb, "attn_fwd", ride=(later_shards, True))
    wout_ab = gw_out_ab.reshape(wa + wb, d)
    win_ssm = gw_in_ssm.reshape(d, 2 * w)
    wout_ssm = jnp.concatenate([gw_out_ssm[k] for k in range(4)], axis=1)
    wglu = gw_glu.reshape(w, w)
    dskip_full = g_dskip.reshape(1, w)
    bglu_full = g_bglu.reshape(1, w)
    cat =_sgu_fwd(proj0, out_b, sgu_norm_g, sgu_w0, sgu_bt, wa, wb, "sgu_fwd")
    y0 = _matmul(cat, wout_ab, "nn", F32, "out0", tm=1024)
    x1, h1 = _post_pre_fwd(x2, y0, gate[0], post_g[0], pre_g[1], scale[1], shift[1], "post0_pre1_fwd")

    s = g * p
    lr_c, li_c = lam_re.reshape(s, 1), lam_im.reshape(s, 1)
    ldt_c = jnp.repeat(log_dt.reshape(g), p).reshape(s, 1)
    br_c, bi_c = b_re.reshape(s, cch), b_im.reshape(s, cch)
    bb_re, bb_im, pw_re, pw_im = _ssm_prep(lr_c, li_c, ldt_c, br_c, bi_c, lr_c.reshape(1, s), li_c.reshape(1, s),
                                           ldt_c.reshape(1, s), "ssm_prep")
    bbd = _bf(_block_diag_b(bb_re, bb_im, g, p, cch))
    ccd = _bf(_block_diag_c(c_re[0], c_im[0], g, p, cch))
    proj1 = _matmul(h1, win_ssm, "nn", F32, "proj1", tm=1024)
    y_ssm, hs_re, hs_im = _ssm_fwd(proj1, bbd, ccd, pw_re, pw_im, dskip_full, w, "ssm_fwd")
    o1 = _glu_fwd(y_ssm, proj1, wglu, bglu_full, "glu_fwd")
    y1 = _matmul(o1, wout_ssm, "nn", F32, "out1", tm=1024)
    loss_vec, dy1, dx2, dgate1, dpost1 = _post_loss(x1, y1, gate[1], post_g[1], tgt, "post1_loss")

    do1 = _matmul(dy1, wout_ssm, "nt", F32, "out1_dx", tm=1024)
    gr_wout_ssm = _matmul(o1, dy1, "tn", BF16, "out1_dw", tm=1024, tk=1024, n_split=4)
    dy_ssm, dz1, gr_wglu, gr_bglu = _glu_bwd(do1, y_ssm, proj1, wglu, bglu_full, "glu_bwd")
    du1, dbbd, dccd, da_re, da_im, gr_dskip, (ld_wout_ssm, ld_wglu) = _ssm_bwd(
        proj1, dy_ssm, hs_re, hs_im, bbd, ccd, pw_re, pw_im, dskip_full, w, "ssm_bwd",
        ride=([gr_wout_ssm, gr_wglu.reshape(4, w // 4, w)], False))
    dproj1 = jnp.concatenate([du1, dz1], axis=1)
    dh1 = _matmul(dproj1, win_ssm, "nt", F32, "proj1_dx", tm=1024)
    gr_win_ssm = _matmul(h1, dproj1, "tn", BF16, "proj1_dw", tm=1024, tn=1024, tk=1024)
    dx1, dscale1, dshift1, dpre1, dy0, dgate0, dpost0 = _pre_bwd(
        dh1, dx2, x1, pre_g[1], scale[1], "pre1_post0_bwd", post=(y0, gate[0], post_g[0]))

    dcat = _matmul(dy0, wout_ab, "nt", F32, "out0_dx", tm=1024)
    gr_wout_ab = _matmul(cat, dy0, "tn", BF16, "out0_dw", tm=1024, tn=1024, tk=1024)
    dq, dk, dv, (ld_win_ssm, ld_wout_ab) = _attn_bwd(
        qkv, proj0, dcat, lktot, wa, wb, "attn_bwd",
        ride=([gr_win_ssm.reshape(4, d // 4, 2 * w), gr_wout_ab.reshape(4, (wa + wb) // 4, d)], False))
    dproj0, gr_sgu_w, gr_sgu_bt, gr_sgu_g = _sgu_bwd(proj0, out_b, dcat, dq, dk, dv, sgu_norm_g, sgu_w0, sgu_bt,
                                                     wa, wb, "sgu_bwd")
    gr_win_ab = _matmul(h0, dproj0, "tn", BF16, "proj0_dw", tm=1024, tk=1024, tn=896, n_split=4)
    dh0, (ld_win_ab,) = _matmul(dproj0, win_ab, "nt", F32, "proj0_dx", tm=1024, tk=1792, ride=([gr_win_ab], False))
    grad_x, dscale0, dshift0, dpre0 = _pre_bwd(dh0, dx1, x2, pre_g[0], scale[0], "pre0_bwd")

    landed = [ld_win_ab, ld_wout_ab, ld_win_ssm, ld_wout_ssm, ld_wglu]
    big_names = ["w_in_ab", "w_out_ab", "w_in_ssm", "w_out_ssm", "w_glu"]
    sums = [_sum_leading(a, "sum_" + nm) for a, nm in zip(landed, big_names)]
    sib = _sibling_exchange(sums, "sibling_grads")
    results = {}
    for nm, s_mine, s_sib in zip(big_names, sums, sib):
        shp = given[nm].shape
        two_d = lambda a: a.reshape(-1, shp[-1])
        outs = _adamw(two_d(given[nm]), [s_mine, s_sib], two_d(given["m_" + nm]), two_d(given["v_" + nm]),
                      "adamw_" + nm)
        results[nm] = [o.reshape(shp) for o in outs]

    dbb_re, dbb_im = _diag_of_b(dbbd, g, p, cch)
    dc_re, dc_im = _diag_of_c(dccd, g, p, cch)
    dmod = jnp.concatenate([jnp.concatenate([dshift0, dscale0, dgate0], axis=1),
                            jnp.concatenate([dshift1, dscale1, dgate1], axis=1)], axis=0)
    partial = [loss_vec[:, :1], jnp.concatenate([dpre0, dpre1], 0), jnp.concatenate([dpost0, dpost1], 0), dmod,
               gr_sgu_g, gr_sgu_w, gr_sgu_bt.T, da_re, da_im, dbb_re, dbb_im, dc_re, dc_im, gr_dskip, gr_bglu]
    part_shapes = [a.shape for a in partial]
    packed = _pack(partial)
    gathered = _allgather8(packed, "gather_small")
    total = _sum_leading(gathered, "sum_small")
    (loss_s, g_pre, g_post, g_bmod, g_sgu_g, g_sgu_w, g_sgu_b, s_da_re, s_da_im, s_dbb_re, s_dbb_im, g_c_re, g_c_im,
     g_dskip_full, g_bglu_full) = _unpack(total, part_shapes)
    loss = loss_s.reshape(())

    g_lr, g_li, g_ldt, g_br, g_bi = _ssm_prep_bwd(lr_c, li_c, ldt_c, br_c, bi_c, s_da_re.reshape(s, 1),
                                                  s_da_im.reshape(s, 1), s_dbb_re, s_dbb_im, p, "ssm_prep_bwd")
    small = {
        "ln_pre_g": g_pre, "ln_post_g": g_post, "b_mod": g_bmod, "sgu_norm_g": g_sgu_g,
        "sgu_w": g_sgu_w.reshape(sgu_w.shape), "sgu_b": g_sgu_b.reshape(sgu_b.shape),
        "lam_re": g_lr.reshape(lam_re.shape), "lam_im": g_li.reshape(lam_im.shape),
        "b_re": g_br.reshape(b_re.shape), "b_im": g_bi.reshape(b_im.shape),
        "c_re": g_c_re.reshape(c_re.shape), "c_im": g_c_im.reshape(c_im.shape),
        "d_skip": lax.dynamic_slice(g_dskip_full, (0, chip * (w // 4)), (1, w // 4)),
        "log_dt": g_ldt.reshape(log_dt.shape),
        "b_glu": lax.dynamic_slice(g_bglu_full, (0, chip * (w // 4)), (1, w // 4)),
    }
    small_names = list(small)
    small_shapes = [small[nm].shape for nm in small_names]
    outs = _adamw(_pack([given[nm] for nm in small_names]), [_pack([small[nm] for nm in small_names])],
                  _pack([given["m_" + nm] for nm in small_names]), _pack([given["v_" + nm] for nm in small_names]),
                  "adamw_small")
    unpacked = [_unpack(o, small_shapes) for o in outs]
    for i, nm in enumerate(small_names):
        results[nm] = [small[nm]] + [unpacked[k][i] for k in range(1, 4)]

    off = sum(math.prod(sh) + (-math.prod(sh)) % PACK for sh in part_shapes[:3])
    dmod_rows = gathered.reshape(8, -1)[:, off:off + 6 * d].reshape(8, 2, 3 * d)
    dmod_shard = lax.dynamic_slice(dmod_rows, (0, 0, chip * nmod), (8, 2, nmod)).transpose(1, 0, 2)
    dmod_pad = jnp.pad(dmod_shard, ((0, 0), (0, MOD_ROWS - 8), (0, 0)))
    gr_wmod = _mod_bwd(cond_pad.T, dmod_pad, "mod_bwd")
    two_d = lambda a: a.reshape(-1, nmod)
    outs = _adamw(two_d(w_mod), [two_d(gr_wmod)], two_d(m_w_mod), two_d(v_w_mod), "adamw_w_mod")
    results["w_mod"] = [o.reshape(w_mod.shape) for o in outs]

    names = ["ln_pre_g", "ln_post_g", "w_mod", "b_mod", "w_in_ab", "w_out_ab", "sgu_norm_g", "sgu_w", "sgu_b",
             "w_in_ssm", "w_out_ssm", "lam_re", "lam_im", "b_re", "b_im", "c_re", "c_im", "d_skip", "log_dt",
             "w_glu", "b_glu"]
    return (loss, grad_x[None], *[results[nm][0] for nm in names], *[results[nm][1] for nm in names],
            *[results[nm][2] for nm in names], *[results[nm][3] for nm in names])
```

```python
import functools
import math

import jax
import jax.numpy as jnp
from jax import lax
from jax.experimental import pallas as pl
from jax.experimental.pallas import tpu as pltpu

F32 = jnp.float32
BF16 = jnp.bfloat16
MESH = pl.DeviceIdType.MESH

EPS = 1e-6
HEAD = 128
SSM_T = 512
SSM_GB = 16
ADAM_LR, ADAM_B1, ADAM_B2, ADAM_EPS, ADAM_WD, ADAM_STEP = 0.001, 0.9, 0.999, 1e-08, 0.01, 10
VMEM_LIMIT = 56 * 1024 * 1024

NN = (((1,), (0,)), ((), ()))
NT = (((1,), (1,)), ((), ()))
TN = (((0,), (0,)), ((), ()))


def _params(sem=None):
    return pltpu.CompilerParams(dimension_semantics=sem, vmem_limit_bytes=VMEM_LIMIT)


def _dot(a, b, dims=NN):
    return lax.dot_general(a, b, dims, preferred_element_type=F32)


def _bf(x):
    return x.astype(BF16)


def _gelu(x):
    k = math.sqrt(2.0 / math.pi)
    t = jnp.tanh(k * (x + 0.044715 * x * x * x))
    return 0.5 * x * (1.0 + t)


def _gelu_grad(x):
    k = math.sqrt(2.0 / math.pi)
    x2 = x * x
    t = jnp.tanh(k * (x + 0.044715 * x * x2))
    return 0.5 * (1.0 + t) + 0.5 * x * (1.0 - t * t) * k * (1.0 + 3.0 * 0.044715 * x2)


def _sigmoid(x):
    return 1.0 / (1.0 + jnp.exp(-x))


def _silu(x):
    return x * _sigmoid(x)


def _silu_grad(x):
    s = _sigmoid(x)
    return s * (1.0 + x * (1.0 - s))


def _tile(n, t, mult=128):
    if n <= t:
        return n
    for cand in range(t - t % mult, 0, -mult):
        if n % cand == 0:
            return cand
    raise ValueError((n, t, mult))


def _matmul(a, b, mode, out_dtype, name, tm=512, tn=512, tk=2048, n_split=1, ride=None):
    if mode == "nn":
        (m, kk), (_, n) = a.shape, b.shape
    elif mode == "nt":
        (m, kk), (n, _) = a.shape, b.shape
    else:
        (kk, m), (_, n) = a.shape, b.shape
    tm, tk = _tile(m, tm), _tile(kk, tk)
    ns = n // n_split
    tn = _tile(ns, tn)
    nk = kk // tk
    dims = {"nn": NN, "nt": NT, "tn": TN}[mode]

    def body(a_ref, b_ref, o_ref, acc_ref):
        k = pl.program_id(2)
        part = _dot(_bf(a_ref[...]), _bf(b_ref[...]), dims)

        @pl.when(k == 0)
        def _():
            acc_ref[...] = part

        @pl.when(k > 0)
        def _():
            acc_ref[...] += part

        @pl.when(k == nk - 1)
        def _():
            o_ref[...] = acc_ref[...].astype(out_dtype).reshape(o_ref.shape)

    if mode == "nn":
        a_spec = pl.BlockSpec((tm, tk), lambda i, j, k: (i, k))
        b_spec = pl.BlockSpec((tk, tn), lambda i, j, k: (k, j))
    elif mode == "nt":
        a_spec = pl.BlockSpec((tm, tk), lambda i, j, k: (i, k))
        b_spec = pl.BlockSpec((tn, tk), lambda i, j, k: (j, k))
    else:
        a_spec = pl.BlockSpec((tk, tm), lambda i, j, k: (k, i))
        b_spec = pl.BlockSpec((tk, tn), lambda i, j, k: (k, j))
    if n_split == 1:
        out_shape = jax.ShapeDtypeStruct((m, n), out_dtype)
        o_spec = pl.BlockSpec((tm, tn), lambda i, j, k: (i, j))
    else:
        per = ns // tn
        out_shape = jax.ShapeDtypeStruct((n_split, m, ns), out_dtype)
        o_spec = pl.BlockSpec((1, tm, tn), lambda i, j, k: (j // per, i, j % per))
    outs = _call(body, name=name, grid=(m // tm, n // tn, nk), in_specs=[a_spec, b_spec], out_specs=[o_spec],
                 out_shape=[out_shape], scratch_shapes=[pltpu.VMEM((tm, tn), F32)], args=(a, b),
                 sem=("parallel", "parallel", "arbitrary"), ride=ride)
    return outs[0] if ride is None else (outs[0], outs[1:])


def _row_spec(tm, d):
    return pl.BlockSpec((tm, d), lambda i: (i, 0))


def _vec_spec(d):
    return pl.BlockSpec((1, d), lambda i: (0, 0))


def _acc(ref, first, val):
    @pl.when(first)
    def _():
        ref[...] = val

    @pl.when(jnp.logical_not(first))
    def _():
        ref[...] += val


def _colsum(x):
    return jnp.sum(x, axis=0, keepdims=True)


def _rownorm(x):
    r = lax.rsqrt(jnp.mean(x * x, axis=-1, keepdims=True) + EPS)
    return x * r, r


def _pre_fwd(x, g, scale, shift, name):
    l, d = x.shape
    tm = _tile(l, 256)

    def body(x_ref, g_ref, sc_ref, sh_ref, h_ref):
        n, _ = _rownorm(x_ref[...])
        h_ref[...] = _bf(n * g_ref[...] * (1.0 + sc_ref[...]) + sh_ref[...])

    return pl.pallas_call(
        body, name=name, grid=(l // tm,),
        in_specs=[_row_spec(tm, d), _vec_spec(d), _vec_spec(d), _vec_spec(d)],
        out_specs=_row_spec(tm, d), out_shape=jax.ShapeDtypeStruct((l, d), BF16),
        compiler_params=_params(("parallel",)),
    )(x, g, scale, shift)


def _post_pre_fwd(x, y, gate, pg, g1, scale1, shift1, name):
    l, d = x.shape
    tm = _tile(l, 256)

    def body(x_ref, y_ref, gate_ref, pg_ref, g1_ref, sc_ref, sh_ref, x1_ref, h1_ref):
        ny, _ = _rownorm(y_ref[...])
        x1 = x_ref[...] + gate_ref[...] * (ny * pg_ref[...])
        x1_ref[...] = x1
        n1, _ = _rownorm(x1)
        h1_ref[...] = _bf(n1 * g1_ref[...] * (1.0 + sc_ref[...]) + sh_ref[...])

    v = _vec_spec(d)
    return pl.pallas_call(
        body, name=name, grid=(l // tm,),
        in_specs=[_row_spec(tm, d), _row_spec(tm, d), v, v, v, v, v],
        out_specs=[_row_spec(tm, d), _row_spec(tm, d)],
        out_shape=[jax.ShapeDtypeStruct((l, d), F32), jax.ShapeDtypeStruct((l, d), BF16)],
        compiler_params=_params(("parallel",)),
    )(x, y, gate, pg, g1, scale1, shift1)


def _post_loss(x1, y1, gate, pg, target, name):
    l, d = x1.shape
    tm = _tile(l, 256)

    def body(x_ref, y_ref, gate_ref, pg_ref, t_ref, loss_ref, dy_ref, dx_ref, dgate_ref, dpg_ref):
        first = pl.program_id(0) == 0
        y = y_ref[...]
        ny, ry = _rownorm(y)
        q = ny * pg_ref[...]
        x2 = x_ref[...] + gate_ref[...] * q
        e = x2 - t_ref[...]
        _acc(loss_ref, first, jnp.full((1, 128), 0.5 / d, F32) * jnp.sum(e * e))
        dx2 = e * (1.0 / d)
        dx_ref[...] = dx2
        _acc(dgate_ref, first, _colsum(dx2 * q))
        dq = dx2 * gate_ref[...]
        _acc(dpg_ref, first, _colsum(dq * ny))
        dny = dq * pg_ref[...]
        dy = ry * (dny - ny * jnp.mean(dny * ny, axis=-1, keepdims=True))
        dy_ref[...] = _bf(dy)

    v = _vec_spec(d)
    return pl.pallas_call(
        body, name=name, grid=(l // tm,),
        in_specs=[_row_spec(tm, d), _row_spec(tm, d), v, v, _row_spec(tm, d)],
        out_specs=[_vec_spec(128), _row_spec(tm, d), _row_spec(tm, d), v, v],
        out_shape=[jax.ShapeDtypeStruct((1, 128), F32), jax.ShapeDtypeStruct((l, d), BF16),
                   jax.ShapeDtypeStruct((l, d), F32), jax.ShapeDtypeStruct((1, d), F32),
                   jax.ShapeDtypeStruct((1, d), F32)],
        compiler_params=_params(("arbitrary",)),
    )(x1, y1, gate, pg, target)


def _pre_bwd(dh, dres, x, g, scale, name, post=None):
    l, d = x.shape
    tm = _tile(l, 256)
    with_post = post is not None

    def body(*refs):
        if with_post:
            (dh_ref, dres_ref, x_ref, g_ref, sc_ref, y_ref, gate_ref, pg_ref,
             dx_ref, dsc_ref, dsh_ref, dg_ref, dy_ref, dgate_ref, dpg_ref) = refs
        else:
            dh_ref, dres_ref, x_ref, g_ref, sc_ref, dx_ref, dsc_ref, dsh_ref, dg_ref = refs
        first = pl.program_id(0) == 0
        dh = dh_ref[...]
        n, r = _rownorm(x_ref[...])
        _acc(dsc_ref, first, _colsum(dh * (n * g_ref[...])))
        _acc(dsh_ref, first, _colsum(dh))
        dyn = dh * (1.0 + sc_ref[...])
        _acc(dg_ref, first, _colsum(dyn * n))
        dn = dyn * g_ref[...]
        dx = dres_ref[...] + r * (dn - n * jnp.mean(dn * n, axis=-1, keepdims=True))
        dx_ref[...] = dx
        if with_post:
            ny, ry = _rownorm(y_ref[...])
            _acc(dgate_ref, first, _colsum(dx * (ny * pg_ref[...])))
            dq = dx * gate_ref[...]
            _acc(dpg_ref, first, _colsum(dq * ny))
            dny = dq * pg_ref[...]
            dy_ref[...] = _bf(ry * (dny - ny * jnp.mean(dny * ny, axis=-1, keepdims=True)))

    v = _vec_spec(d)
    row = _row_spec(tm, d)
    vec_out = jax.ShapeDtypeStruct((1, d), F32)
    in_specs = [row, row, row, v, v]
    args = [dh, dres, x, g, scale]
    out_specs = [row, v, v, v]
    out_shape = [jax.ShapeDtypeStruct((l, d), F32), vec_out, vec_out, vec_out]
    if with_post:
        in_specs += [row, v, v]
        args += list(post)
        out_specs += [row, v, v]
        out_shape += [jax.ShapeDtypeStruct((l, d), BF16), vec_out, vec_out]
    return pl.pallas_call(
        body, name=name, grid=(l // tm,), in_specs=in_specs, out_specs=out_specs, out_shape=out_shape,
        compiler_params=_params(("arbitrary",)),
    )(*args)


def _softplus_parts(z):
    e = jnp.exp(-jnp.abs(z))
    den = 1.0 + e
    lb = jnp.minimum(z, 0.0) - jnp.log(den)
    sig = jnp.where(z >= 0.0, 1.0, e) / den
    return lb, lb - z, sig


def _tri(cmp, n=HEAD):
    row = lax.broadcasted_iota(jnp.int32, (n, n), 0)
    col = lax.broadcasted_iota(jnp.int32, (n, n), 1)
    return cmp(row, col)


ATT_T = 256


def _attn_fwd(qkv, wb, name, hp=4, ride=None):
    l = qkv.shape[0]
    t = ATT_T
    nh, nq = wb // HEAD, l // t
    hp = min(hp, nh)
    ng, wg = nh // hp, hp * HEAD
    scale = 1.0 / math.sqrt(HEAD)

    def body(q_ref, k_ref, v_ref, o_ref, lk_ref):
        i = pl.program_id(1)
        valid = _tri(lambda r, c: c < r, t)
        m_gt = _bf(_tri(lambda r, c: r > c, t).astype(F32))

        def tile(j, carry, diag):
            rows = pl.ds(pl.multiple_of(j * t, t), t)
            out = []
            for hh, (acc, run) in enumerate(carry):
                cs = slice(hh * HEAD, (hh + 1) * HEAD)
                z = _dot(q_ref[:, cs], k_ref[rows, cs], NT) * scale
                lb, lk, _ = _softplus_parts(z)
                if diag:
                    lk = jnp.where(valid, lk, 0.0)
                later = _dot(_bf(lk), m_gt)
                w = jnp.exp(lb + later + run)
                if diag:
                    w = jnp.where(valid, w, 0.0)
                out.append((acc + _dot(_bf(w), v_ref[rows, cs]), run + jnp.sum(lk, axis=1, keepdims=True)))
            return tuple(out)

        zero = (jnp.zeros((t, HEAD), F32), jnp.zeros((t, 1), F32))
        carry = tile(i, (zero,) * hp, True)
        carry = lax.fori_loop(0, i, lambda s, c: tile(i - 1 - s, c, False), carry)
        for hh, (acc, run) in enumerate(carry):
            cs = slice(hh * HEAD, (hh + 1) * HEAD)
            o_ref[:, cs] = acc
            lk_ref[:, cs] = jnp.broadcast_to(run, (t, HEAD))

    blk = lambda off: pl.BlockSpec((t, wg), lambda h, i: (i, off + h))
    full = lambda off: pl.BlockSpec((l, wg), lambda h, i: (0, off + h))
    out = pl.BlockSpec((t, wg), lambda h, i: (i, h))
    outs = _call(body, name=name, grid=(ng, nq), in_specs=[blk(0), full(ng), full(2 * ng)], out_specs=[out, out],
                 out_shape=[jax.ShapeDtypeStruct((l, wb), F32), jax.ShapeDtypeStruct((l, wb), F32)],
                 args=(qkv, qkv, qkv), sem=("parallel", "arbitrary"), ride=ride)
    return outs[0], outs[1], outs[2:]


def _attn_bwd(qkv, proj, dcat, lktot, wa, wb, name, hp=2, ride=None):
    l = qkv.shape[0]
    t = ATT_T
    nh, nq = wb // HEAD, l // t
    hp = min(hp, nh)
    ng, wg = nh // hp, hp * HEAD
    scale = 1.0 / math.sqrt(HEAD)

    def body(q_ref, k_ref, v_ref, bz_ref, dc_ref, lt_ref, dq_ref, dkt_ref, dvt_ref, do_s, qt_s, dot_s):
        i = pl.program_id(1)

        @pl.when(i == 0)
        def _():
            dkt_ref[...] = jnp.zeros_like(dkt_ref)
            dvt_ref[...] = jnp.zeros_like(dvt_ref)

        do = dc_ref[...] * _silu(bz_ref[...])
        do_s[...] = _bf(do)
        for hh in range(hp):
            cs = slice(hh * HEAD, (hh + 1) * HEAD)
            qt_s[hh] = _bf(q_ref[:, cs].astype(F32).T)
            dot_s[hh] = _bf(do[:, cs].T)
        valid = _tri(lambda r, c: c < r, t)
        m_le = _bf(_tri(lambda r, c: r <= c, t).astype(F32))
        m_lt = _bf(_tri(lambda r, c: r < c, t).astype(F32))

        def tile(j, carry, diag):
            rows = pl.ds(pl.multiple_of(j * t, t), t)
            out = []
            for hh, (dq, cpre, ppre) in enumerate(carry):
                cs = slice(hh * HEAD, (hh + 1) * HEAD)
                kb = k_ref[rows, cs]
                z = _dot(q_ref[:, cs], kb, NT) * scale
                lb, lk, sig = _softplus_parts(z)
                if diag:
                    lk = jnp.where(valid, lk, 0.0)
                pin = _dot(_bf(lk), m_le)
                w = jnp.exp(lb + (lt_ref[:, hh * HEAD:hh * HEAD + 1] - cpre) - pin)
                if diag:
                    w = jnp.where(valid, w, 0.0)
                da = _dot(do_s[:, cs], v_ref[rows, cs], NT) * w
                pex = _dot(_bf(da), m_lt)
                dz = (da - sig * (da + ppre + pex)) * scale
                if diag:
                    dz = jnp.where(valid, dz, 0.0)
                dz_bf = _bf(dz)
                dkt_ref[hh, j] += _dot(qt_s[hh], dz_bf)
                dvt_ref[hh, j] += _dot(dot_s[hh], _bf(w))
                out.append((dq + _dot(dz_bf, kb), cpre + jnp.sum(lk, axis=1, keepdims=True),
                            ppre + pex[:, t - 1:] + da[:, t - 1:]))
            return tuple(out)

        zero = (jnp.zeros((t, HEAD), F32), jnp.zeros((t, 1), F32), jnp.zeros((t, 1), F32))
        carry = lax.fori_loop(0, i, lambda j, c: tile(j, c, False), (zero,) * hp)
        carry = tile(i, carry, True)
        for hh in range(hp):
            dq_ref[:, hh * HEAD:(hh + 1) * HEAD] = carry[hh][0]

    blk = lambda off: pl.BlockSpec((t, wg), lambda h, i: (i, off + h))
    full = lambda off: pl.BlockSpec((l, wg), lambda h, i: (0, off + h))
    acc = pl.BlockSpec((hp, nq, HEAD, t), lambda h, i: (h, 0, 0, 0))
    acc_shape = jax.ShapeDtypeStruct((nh, nq, HEAD, t), F32)
    outs = _call(
        body, name=name, grid=(ng, nq),
        in_specs=[blk(0), full(ng), full(2 * ng), blk(3 * wa // wg), blk(wa // wg), blk(0)],
        out_specs=[blk(0), acc, acc], out_shape=[jax.ShapeDtypeStruct((l, wb), F32), acc_shape, acc_shape],
        scratch_shapes=[pltpu.VMEM((t, wg), BF16), pltpu.VMEM((hp, HEAD, t), BF16), pltpu.VMEM((hp, HEAD, t), BF16)],
        args=(qkv, qkv, qkv, proj, dcat, lktot), sem=("parallel", "arbitrary"), ride=ride)
    untranspose = lambda a: a.transpose(1, 3, 0, 2).reshape(l, wb)
    return outs[0], untranspose(outs[1]), untranspose(outs[2]), outs[3:]


def _sgu_heads(v, g_ref, w_ref, bt_ref, nh):
    keep = _tri(lambda r, c: r >= c)
    out = []
    for h in range(nh):
        cols = slice(h * HEAD, (h + 1) * HEAD)
        nv, r = _rownorm(v[:, cols])
        wm = jnp.where(keep, w_ref[h], 0.0)
        s = _dot(_bf(wm), _bf(nv * g_ref[:, cols])) + bt_ref[:, h:h + 1]
        out.append((nv, r, wm, s))
    return out


def _sgu_fwd(proj, out_b, norm_g, sgu_w, sgu_bt, wa, wb, name):
    l, n = proj.shape
    nh = wa // HEAD

    def body(au_ref, av_ref, az_ref, bz_ref, ob_ref, g_ref, w_ref, bt_ref, cat_ref):
        u, v, sz = _gelu(au_ref[...]), _gelu(av_ref[...]), _silu(az_ref[...])
        for h, (_, _, _, s) in enumerate(_sgu_heads(v, g_ref, w_ref, bt_ref, nh)):
            cols = slice(h * HEAD, (h + 1) * HEAD)
            cat_ref[:, cols] = _bf(u[:, cols] * s * sz[:, cols])
        cat_ref[:, wa:] = _bf(ob_ref[...] * _silu(bz_ref[...]))

    a_blk = lambda j: pl.BlockSpec((HEAD, wa), lambda i: (i, j))
    return pl.pallas_call(
        body, name=name, grid=(l // HEAD,),
        in_specs=[a_blk(0), a_blk(1), a_blk(2), a_blk(3), pl.BlockSpec((HEAD, wb), lambda i: (i, 0)),
                  _vec_spec(wa), pl.BlockSpec((nh, HEAD, HEAD), lambda i: (0, 0, 0)),
                  pl.BlockSpec((HEAD, nh), lambda i: (0, 0))],
        out_specs=pl.BlockSpec((HEAD, wa + wb), lambda i: (i, 0)),
        out_shape=jax.ShapeDtypeStruct((l, wa + wb), BF16),
        compiler_params=_params(("parallel",)),
    )(proj, proj, proj, proj, out_b, norm_g, sgu_w, sgu_bt)


def _sgu_bwd(proj, out_b, dcat, dq, dk, dv, norm_g, sgu_w, sgu_bt, wa, wb, name):
    l = proj.shape[0]
    n = 3 * wa + 4 * wb
    nh = wa // HEAD

    def body(au_ref, av_ref, az_ref, bz_ref, ob_ref, dc_ref, dq_ref, dk_ref, dv_ref, g_ref, w_ref, bt_ref,
             dp_ref, dw_ref, dbt_ref, dg_ref):
        first = pl.program_id(0) == 0
        keep = _tri(lambda r, c: r >= c)
        au, av, az = au_ref[...], av_ref[...], az_ref[...]
        u, v, sz = _gelu(au), _gelu(av), _silu(az)
        dgelu_u, dgelu_v, dsilu_z = _gelu_grad(au), _gelu_grad(av), _silu_grad(az)
        dg_parts = []
        for h, (nv, r, wm, s) in enumerate(_sgu_heads(v, g_ref, w_ref, bt_ref, nh)):
            cols = slice(h * HEAD, (h + 1) * HEAD)
            dca, uh, szh, gh = dc_ref[:, cols], u[:, cols], sz[:, cols], g_ref[:, cols]
            dp_ref[:, cols] = _bf(dca * s * szh * dgelu_u[:, cols])
            dp_ref[:, 2 * wa + h * HEAD:2 * wa + (h + 1) * HEAD] = _bf(dca * uh * s * dsilu_z[:, cols])
            ds = dca * uh * szh
            _acc(dw_ref.at[h], first, jnp.where(keep, _dot(_bf(ds), _bf(nv * gh), NT), 0.0))
            _acc(dbt_ref.at[:, h:h + 1], first, jnp.sum(ds, axis=1, keepdims=True))
            dvh = _dot(_bf(wm.T), _bf(ds))
            dg_parts.append(_colsum(dvh * nv))
            dnv = dvh * gh
            dvv = r * (dnv - nv * jnp.mean(dnv * nv, axis=-1, keepdims=True))
            dp_ref[:, wa + h * HEAD:wa + (h + 1) * HEAD] = _bf(dvv * dgelu_v[:, cols])
        _acc(dg_ref, first, jnp.concatenate(dg_parts, axis=1))
        base = 3 * wa
        dp_ref[:, base:base + wb] = _bf(dq_ref[...])
        dp_ref[:, base + wb:base + 2 * wb] = _bf(dk_ref[...])
        dp_ref[:, base + 2 * wb:base + 3 * wb] = _bf(dv_ref[...])
        dp_ref[:, base + 3 * wb:] = _bf(dc_ref[:, wa:] * ob_ref[...] * _silu_grad(bz_ref[...]))

    a_blk = lambda j: pl.BlockSpec((HEAD, wa), lambda i: (i, j))
    b_blk = pl.BlockSpec((HEAD, wb), lambda i: (i, 0))
    w_spec = pl.BlockSpec((nh, HEAD, HEAD), lambda i: (0, 0, 0))
    bt_spec = pl.BlockSpec((HEAD, nh), lambda i: (0, 0))
    return pl.pallas_call(
        body, name=name, grid=(l // HEAD,),
        in_specs=[a_blk(0), a_blk(1), a_blk(2), a_blk(3), b_blk, pl.BlockSpec((HEAD, wa + wb), lambda i: (i, 0)),
                  b_blk, b_blk, b_blk, _vec_spec(wa), w_spec, bt_spec],
        out_specs=[pl.BlockSpec((HEAD, n), lambda i: (i, 0)), w_spec, bt_spec, _vec_spec(wa)],
        out_shape=[jax.ShapeDtypeStruct((l, n), BF16), jax.ShapeDtypeStruct((nh, HEAD, HEAD), F32),
                   jax.ShapeDtypeStruct((HEAD, nh), F32), jax.ShapeDtypeStruct((1, wa), F32)],
        compiler_params=_params(("arbitrary",)),
    )(proj, proj, proj, proj, out_b, dcat, dq, dk, dv, norm_g, sgu_w, sgu_bt)


def _ssm_discretise(lr, li, ldt, br, bi):
    dt = jnp.exp(ldt)
    mag = jnp.exp(lr * dt)
    a_re = mag * jnp.cos(li * dt)
    a_im = mag * jnp.sin(li * dt)
    den = lr * lr + li * li
    nr = a_re - 1.0
    coef_re = (nr * lr + a_im * li) / den
    coef_im = (a_im * lr - nr * li) / den
    return a_re, a_im, coef_re * br - coef_im * bi, coef_re * bi + coef_im * br


def _ssm_prep(lr, li, ldt, br, bi, lr_row, li_row, ldt_row, name):
    s, c = br.shape

    def body(lr_ref, li_ref, ldt_ref, br_ref, bi_ref, lrr_ref, lir_ref, ldtr_ref, bbr_ref, bbi_ref, tr_ref, ti_ref):
        _, _, bbr, bbi = _ssm_discretise(lr_ref[...], li_ref[...], ldt_ref[...], br_ref[...], bi_ref[...])
        bbr_ref[...] = bbr
        bbi_ref[...] = bbi
        row = lax.broadcasted_iota(jnp.int32, (SCAN_ROWS, 1), 0)
        blk, r = jnp.right_shift(row, 3), jnp.bitwise_and(row, 7)
        kind, rev = jnp.bitwise_and(blk, 3), blk >= 4
        step = jnp.left_shift(1, kind)
        n = jnp.where(kind < 3, step, jnp.where(rev, 8 - r, r + 1)).astype(F32)
        keep = (kind == 3) | (rev & (r < 8 - step)) | (jnp.logical_not(rev) & (r >= step))
        dt = jnp.exp(ldtr_ref[...])
        mag = jnp.exp(n * (lrr_ref[...] * dt))
        ang = n * (lir_ref[...] * dt)
        tr_ref[...] = jnp.where(keep, mag * jnp.cos(ang), 0.0)
        ti_ref[...] = jnp.where(keep, jnp.where(rev, -1.0, 1.0) * mag * jnp.sin(ang), 0.0)

    col = jax.ShapeDtypeStruct((s, c), F32)
    row = jax.ShapeDtypeStruct((SCAN_ROWS, s), F32)
    return pl.pallas_call(body, name=name, out_shape=[col, col, row, row])(
        lr, li, ldt, br, bi, lr_row, li_row, ldt_row)


def _ssm_prep_bwd(lr, li, ldt, br, bi, da_re, da_im, dbb_re, dbb_im, p, name):
    s, c = br.shape

    def body(lr_ref, li_ref, ldt_ref, br_ref, bi_ref, dar_ref, dai_ref, dbr_ref, dbi_ref,
             dlr_ref, dli_ref, dldt_ref, dbre_ref, dbim_ref):
        args = (lr_ref[...], li_ref[...], ldt_ref[...], br_ref[...], bi_ref[...])
        _, vjp = jax.vjp(_ssm_discretise, *args)
        dlr, dli, dldt, dbr, dbi = vjp((dar_ref[...], dai_ref[...], dbr_ref[...], dbi_ref[...]))
        dlr_ref[...] = dlr
        dli_ref[...] = dli
        dbre_ref[...] = dbr
        dbim_ref[...] = dbi
        idx = lax.broadcasted_iota(jnp.int32, (s, s // p), 0)
        grp = lax.broadcasted_iota(jnp.int32, (s, s // p), 1)
        own = (idx >= grp * p) & (idx < (grp + 1) * p)
        dldt_ref[...] = _colsum(jnp.where(own, dldt, 0.0))

    col1 = jax.ShapeDtypeStruct((s, 1), F32)
    colc = jax.ShapeDtypeStruct((s, c), F32)
    return pl.pallas_call(
        body, name=name, out_shape=[col1, col1, jax.ShapeDtypeStruct((1, s // p), F32), colc, colc],
    )(lr, li, ldt, br, bi, da_re, da_im, dbb_re, dbb_im)


SCAN_ROWS = 64


def _scan_groups(xr, xi, tr_ref, ti_ref, cr, ci, reverse):
    ng = xr.shape[0] // 8
    base = SCAN_ROWS // 2 if reverse else 0
    pr, pi = tr_ref[base + 24:base + 32, :], ti_ref[base + 24:base + 32, :]
    edge = slice(0, 1) if reverse else slice(7, 8)
    out_r, out_i = [None] * ng, [None] * ng
    for g in (range(ng - 1, -1, -1) if reverse else range(ng)):
        sr, si = xr[8 * g:8 * g + 8, :], xi[8 * g:8 * g + 8, :]
        for k in range(3):
            ar, ai = tr_ref[base + 8 * k:base + 8 * k + 8, :], ti_ref[base + 8 * k:base + 8 * k + 8, :]
            shift = 8 - (1 << k) if reverse else 1 << k
            rr, ri = pltpu.roll(sr, shift, 0), pltpu.roll(si, shift, 0)
            sr, si = sr + ar * rr - ai * ri, si + ar * ri + ai * rr
        sr, si = sr + pr * cr - pi * ci, si + pr * ci + pi * cr
        cr, ci = sr[edge, :], si[edge, :]
        out_r[g], out_i[g] = sr, si
    return jnp.concatenate(out_r, axis=0), jnp.concatenate(out_i, axis=0), cr, ci


def _ssm_fwd(proj, bbd, ccd, pw_re, pw_im, d_skip, w, name):
    l = proj.shape[0]
    nb, cw, ns2 = bbd.shape
    ns = ns2 // 2
    nc = l // SSM_T

    def body(u_ref, bbd_ref, ccd_ref, pr_ref, pi_ref, d_ref, y_ref, hsr_ref, hsi_ref, hr_s, hi_s):
        @pl.when(pl.program_id(1) == 0)
        def _():
            hr_s[...] = jnp.zeros_like(hr_s)
            hi_s[...] = jnp.zeros_like(hi_s)

        hsr_ref[...] = hr_s[...].reshape(hsr_ref.shape)
        hsi_ref[...] = hi_s[...].reshape(hsi_ref.shape)
        u = u_ref[...]
        bu = _dot(_bf(u), bbd_ref[0])
        hr, hi, cr, ci = _scan_groups(bu[:, :ns], bu[:, ns:], pr_ref, pi_ref, hr_s[...], hi_s[...], False)
        hr_s[...] = cr
        hi_s[...] = ci
        y_ref[...] = _dot(_bf(jnp.concatenate([hr, hi], axis=1)), ccd_ref[0]) + d_ref[...] * u

    tab = pl.BlockSpec((SCAN_ROWS, ns), lambda b, k: (0, b))
    return pl.pallas_call(
        body, name=name, grid=(nb, nc),
        in_specs=[pl.BlockSpec((SSM_T, cw), lambda b, k: (k, b)),
                  pl.BlockSpec((1, cw, ns2), lambda b, k: (b, 0, 0)),
                  pl.BlockSpec((1, ns2, cw), lambda b, k: (b, 0, 0)),
                  tab, tab, pl.BlockSpec((1, cw), lambda b, k: (0, b))],
        out_specs=[pl.BlockSpec((SSM_T, cw), lambda b, k: (k, b)),
                   pl.BlockSpec((1, 1, ns), lambda b, k: (k, 0, b)), pl.BlockSpec((1, 1, ns), lambda b, k: (k, 0, b))],
        out_shape=[jax.ShapeDtypeStruct((l, w), F32), jax.ShapeDtypeStruct((nc, 1, nb * ns), F32),
                   jax.ShapeDtypeStruct((nc, 1, nb * ns), F32)],
        scratch_shapes=[pltpu.VMEM((1, ns), F32), pltpu.VMEM((1, ns), F32)],
        compiler_params=_params(("parallel", "arbitrary")),
    )(proj, bbd, ccd, pw_re, pw_im, d_skip)


def _ssm_bwd(proj, dy, hs_re, hs_im, bbd, ccd, pw_re, pw_im, d_skip, w, name, ride=None):
    l = proj.shape[0]
    nb, cw, ns2 = bbd.shape
    ns = ns2 // 2
    nc = l // SSM_T

    def body(u_ref, dy_ref, hsr_ref, hsi_ref, bbd_ref, ccd_ref, pr_ref, pi_ref, d_ref,
             du_ref, dbbd_ref, dccd_ref, dar_ref, dai_ref, dd_ref, gr_s, gi_s):
        first = pl.program_id(1) == 0

        @pl.when(first)
        def _():
            gr_s[...] = jnp.zeros_like(gr_s)
            gi_s[...] = jnp.zeros_like(gi_s)

        u, dy = u_ref[...], dy_ref[...]
        u_bf, dy_bf = _bf(u), _bf(dy)
        hr0, hi0 = hsr_ref[0], hsi_ref[0]
        bu = _dot(u_bf, bbd_ref[0])
        hr, hi, _, _ = _scan_groups(bu[:, :ns], bu[:, ns:], pr_ref, pi_ref, hr0, hi0, False)
        dh = _dot(dy_bf, ccd_ref[0], NT)
        gr, gi, gcr, gci = _scan_groups(dh[:, :ns], dh[:, ns:], pr_ref, pi_ref, gr_s[...], gi_s[...], True)
        gr_s[...] = gcr
        gi_s[...] = gci
        row0 = lax.broadcasted_iota(jnp.int32, hr.shape, 0) == 0
        pr_h = jnp.where(row0, hr0, pltpu.roll(hr, 1, 0))
        pi_h = jnp.where(row0, hi0, pltpu.roll(hi, 1, 0))
        _acc(dar_ref, first, _colsum(pr_h * gr + pi_h * gi))
        _acc(dai_ref, first, _colsum(pr_h * gi - pi_h * gr))
        g_bf = _bf(jnp.concatenate([gr, gi], axis=1))
        _acc(dbbd_ref.at[0], first, _dot(_bf(u.T), g_bf))
        _acc(dccd_ref.at[0], first, _dot(_bf(jnp.concatenate([hr, hi], axis=1).T), dy_bf))
        du_ref[...] = _bf(_dot(g_bf, bbd_ref[0], NT) + d_ref[...] * dy)
        _acc(dd_ref, first, _colsum(dy * u))

    rev = lambda b, k: (nc - 1 - k, b)
    outs = _call(
        body, name=name, grid=(nb, nc), ride=ride, sem=("parallel", "arbitrary"),
        args=(proj, dy, hs_re, hs_im, bbd, ccd, pw_re, pw_im, d_skip),
        in_specs=[pl.BlockSpec((SSM_T, cw), rev), pl.BlockSpec((SSM_T, cw), rev),
                  pl.BlockSpec((1, 1, ns), lambda b, k: (nc - 1 - k, 0, b)),
                  pl.BlockSpec((1, 1, ns), lambda b, k: (nc - 1 - k, 0, b)),
                  pl.BlockSpec((1, cw, ns2), lambda b, k: (b, 0, 0)),
                  pl.BlockSpec((1, ns2, cw), lambda b, k: (b, 0, 0)),
                  pl.BlockSpec((SCAN_ROWS, ns), lambda b, k: (0, b)), pl.BlockSpec((SCAN_ROWS, ns), lambda b, k: (0, b)),
                  pl.BlockSpec((1, cw), lambda b, k: (0, b))],
        out_specs=[pl.BlockSpec((SSM_T, cw), rev),
                   pl.BlockSpec((1, cw, ns2), lambda b, k: (b, 0, 0)),
                   pl.BlockSpec((1, ns2, cw), lambda b, k: (b, 0, 0)),
                   pl.BlockSpec((1, ns), lambda b, k: (0, b)), pl.BlockSpec((1, ns), lambda b, k: (0, b)),
                   pl.BlockSpec((1, cw), lambda b, k: (0, b))],
        out_shape=[jax.ShapeDtypeStruct((l, w), BF16), jax.ShapeDtypeStruct(bbd.shape, F32),
                   jax.ShapeDtypeStruct(ccd.shape, F32), jax.ShapeDtypeStruct((1, nb * ns), F32),
                   jax.ShapeDtypeStruct((1, nb * ns), F32), jax.ShapeDtypeStruct((1, w), F32)],
        scratch_shapes=[pltpu.VMEM((1, ns), F32), pltpu.VMEM((1, ns), F32)])
    return (*outs[:6], outs[6:])


def _block_diag_b(bb_re, bb_im, g, p, c):
    nb = g // SSM_GB
    eye = jnp.eye(SSM_GB, dtype=F32)

    def one(bb):
        t = bb.reshape(nb, SSM_GB, p, c).transpose(0, 1, 3, 2)
        return (t[:, :, :, None, :] * eye[None, :, None, :, None]).reshape(nb, SSM_GB * c, SSM_GB * p)

    return jnp.concatenate([one(bb_re), one(bb_im)], axis=2)


def _block_diag_c(c_re, c_im, g, p, c):
    nb = g // SSM_GB
    eye = jnp.eye(SSM_GB, dtype=F32)

    def one(cc):
        t = cc.reshape(nb, SSM_GB, c, p).transpose(0, 1, 3, 2)
        return (t[:, :, :, None, :] * eye[None, :, None, :, None]).reshape(nb, SSM_GB * p, SSM_GB * c)

    return jnp.concatenate([one(c_re), one(-c_im)], axis=1)


def _diag_of_b(dbbd, g, p, c):
    nb = g // SSM_GB
    t = dbbd.reshape(nb, SSM_GB, c, 2, SSM_GB, p)
    idx = jnp.arange(SSM_GB)
    d = t[:, idx, :, :, idx, :]
    d = d.transpose(1, 0, 3, 4, 2)
    return d[:, :, 0].reshape(g * p, c), d[:, :, 1].reshape(g * p, c)


def _diag_of_c(dccd, g, p, c):
    nb = g // SSM_GB
    t = dccd.reshape(nb, 2, SSM_GB, p, SSM_GB, c)
    idx = jnp.arange(SSM_GB)
    d = t[:, :, idx, :, idx, :]
    d = d.transpose(1, 0, 2, 4, 3)
    return d[:, :, 0].reshape(g, c, p), -d[:, :, 1].reshape(g, c, p)


def _glu_fwd(y, proj, w_glu, b_glu, name):
    l, w = y.shape
    tm = _tile(l, 256)

    def body(y_ref, z_ref, w_ref, b_ref, o_ref):
        g = _gelu(y_ref[...])
        t = _dot(_bf(g), w_ref[...]) + b_ref[...]
        o_ref[...] = _bf(g * _sigmoid(t) * _silu(z_ref[...]))

    return pl.pallas_call(
        body, name=name, grid=(l // tm,),
        in_specs=[_row_spec(tm, w), pl.BlockSpec((tm, w), lambda i: (i, 1)),
                  pl.BlockSpec((w, w), lambda i: (0, 0)), _vec_spec(w)],
        out_specs=_row_spec(tm, w), out_shape=jax.ShapeDtypeStruct((l, w), BF16),
        compiler_params=_params(("parallel",)),
    )(y, proj, w_glu, b_glu)


def _glu_bwd(do, y, proj, w_glu, b_glu, name):
    l, w = y.shape
    tm = _tile(l, 256)
    nsteps = l // tm

    def body(do_ref, y_ref, z_ref, w_ref, b_ref, dy_ref, dz_ref, dw_ref, db_ref, dw_acc):
        i = pl.program_id(0)
        first = i == 0
        yv, z, do = y_ref[...], z_ref[...], do_ref[...]
        g = _gelu(yv)
        g_bf = _bf(g)
        sg = _sigmoid(_dot(g_bf, w_ref[...]) + b_ref[...])
        dyy = do * _silu(z)
        dz_ref[...] = _bf(do * g * sg * _silu_grad(z))
        dt = dyy * g * sg * (1.0 - sg)
        dt_bf = _bf(dt)
        dg = dyy * sg + _dot(dt_bf, w_ref[...], NT)
        dy_ref[...] = dg * _gelu_grad(yv)
        _acc(dw_acc, first, _dot(_bf(g.T), dt_bf))
        _acc(db_ref, first, _colsum(dt))

        @pl.when(i == nsteps - 1)
        def _():
            dw_ref[...] = _bf(dw_acc[...])

    return pl.pallas_call(
        body, name=name, grid=(nsteps,),
        in_specs=[_row_spec(tm, w), _row_spec(tm, w), pl.BlockSpec((tm, w), lambda i: (i, 1)),
                  pl.BlockSpec((w, w), lambda i: (0, 0)), _vec_spec(w)],
        out_specs=[_row_spec(tm, w), _row_spec(tm, w), pl.BlockSpec((w, w), lambda i: (0, 0)), _vec_spec(w)],
        out_shape=[jax.ShapeDtypeStruct((l, w), F32), jax.ShapeDtypeStruct((l, w), BF16),
                   jax.ShapeDtypeStruct((w, w), BF16), jax.ShapeDtypeStruct((1, w), F32)],
        scratch_shapes=[pltpu.VMEM((w, w), F32)],
        compiler_params=_params(("arbitrary",)),
    )(do, y, proj, w_glu, b_glu)


MOD_ROWS = 128


def _mod_fwd(cond_pad, w_mod, b_shard, name):
    nl, d, ncol = w_mod.shape
    tn = _tile(ncol, 512)

    def body(c_ref, w_ref, b_ref, o_ref):
        o_ref[0] = _dot(_bf(c_ref[...]), _bf(w_ref[0])) + b_ref[0]

    return pl.pallas_call(
        body, name=name, grid=(nl, ncol // tn),
        in_specs=[pl.BlockSpec((MOD_ROWS, d), lambda a, j: (0, 0)),
                  pl.BlockSpec((1, d, tn), lambda a, j: (a, 0, j)),
                  pl.BlockSpec((1, 1, tn), lambda a, j: (a, 0, j))],
        out_specs=pl.BlockSpec((1, MOD_ROWS, tn), lambda a, j: (a, 0, j)),
        out_shape=jax.ShapeDtypeStruct((nl, MOD_ROWS, ncol), F32),
        compiler_params=_params(("parallel", "parallel")),
    )(cond_pad, w_mod, b_shard)


def _mod_bwd(cond_pad_t, dmod_pad, name):
    nl, _, ncol = dmod_pad.shape
    d = cond_pad_t.shape[0]
    tn = _tile(ncol, 512)

    def body(c_ref, dm_ref, o_ref):
        o_ref[0] = _dot(_bf(c_ref[...]), _bf(dm_ref[0]))

    return pl.pallas_call(
        body, name=name, grid=(nl, ncol // tn),
        in_specs=[pl.BlockSpec((d, MOD_ROWS), lambda a, j: (0, 0)),
                  pl.BlockSpec((1, MOD_ROWS, tn), lambda a, j: (a, 0, j))],
        out_specs=pl.BlockSpec((1, d, tn), lambda a, j: (a, 0, j)),
        out_shape=jax.ShapeDtypeStruct((nl, d, ncol), F32),
        compiler_params=_params(("parallel", "parallel")),
    )(cond_pad_t, dmod_pad)


def _silu_rows(c2d, name):
    def body(c_ref, o_ref):
        o_ref[...] = _silu(c_ref[...])

    return pl.pallas_call(body, name=name, out_shape=jax.ShapeDtypeStruct(c2d.shape, F32))(c2d)


def _sum_leading(x, name):
    n, r, c = x.shape
    tr = _tile(r, max(16, (1 << 20) // (4 * c)), 16 if r % 16 == 0 else 8)

    def body(x_ref, o_ref):
        acc = x_ref[0].astype(F32)
        for k in range(1, n):
            acc = acc + x_ref[k].astype(F32)
        o_ref[...] = acc

    return pl.pallas_call(
        body, name=name, grid=(r // tr,),
        in_specs=[pl.BlockSpec((n, tr, c), lambda i: (0, i, 0))], out_specs=pl.BlockSpec((tr, c), lambda i: (i, 0)),
        out_shape=jax.ShapeDtypeStruct((r, c), F32), compiler_params=_params(("parallel",)),
    )(x)


def _adamw(w, gs, m, v, name):
    r, c = w.shape
    tr = _tile(r, max(8, (1 << 19) // (4 * c)), 8)
    ng = len(gs)

    def body(*refs):
        w_ref, g_refs, m_ref, v_ref = refs[0], refs[1:1 + ng], refs[1 + ng], refs[2 + ng]
        g_ref, d_ref, nm_ref, nv_ref = refs[3 + ng:]
        g = g_refs[0][...]
        for extra in g_refs[1:]:
            g = g + extra[...]
        g_ref[...] = g
        nm = ADAM_B1 * m_ref[...] + (1.0 - ADAM_B1) * g
        nv = ADAM_B2 * v_ref[...] + (1.0 - ADAM_B2) * (g * g)
        nm_ref[...] = nm
        nv_ref[...] = nv
        m_hat = nm / (1.0 - ADAM_B1 ** ADAM_STEP)
        v_hat = nv / (1.0 - ADAM_B2 ** ADAM_STEP)
        d_ref[...] = -ADAM_LR * (m_hat / (jnp.sqrt(v_hat) + ADAM_EPS) + ADAM_WD * w_ref[...])

    spec = pl.BlockSpec((tr, c), lambda i: (i, 0))
    shp = jax.ShapeDtypeStruct((r, c), F32)
    return pl.pallas_call(
        body, name=name, grid=(r // tr,), in_specs=[spec] * (3 + ng), out_specs=[spec] * 4,
        out_shape=[shp] * 4, compiler_params=_params(("parallel",)),
    )(w, *gs, m, v)


ANY = pl.BlockSpec(memory_space=pl.ANY)


def _flip(v, bit):
    return 1 - v if bit else v


def _allgather8(x, name):
    def body(x_ref, o_ref, send_sems, recv_sems, local_sem):
        mx, my, mc = lax.axis_index("x"), lax.axis_index("y"), lax.axis_index("c")
        me = 4 * mx + 2 * my + mc
        mine = pltpu.make_async_copy(x_ref, o_ref.at[me], local_sem)
        mine.start()

        def copy(j):
            peer = (_flip(mx, j & 4), _flip(my, j & 2), _flip(mc, j & 1))
            return pltpu.make_async_remote_copy(
                src_ref=x_ref, dst_ref=o_ref.at[me], send_sem=send_sems.at[j - 1], recv_sem=recv_sems.at[j - 1],
                device_id=peer, device_id_type=MESH)

        def landing(j):
            peer = (_flip(mx, j & 4), _flip(my, j & 2), _flip(mc, j & 1))
            slot = 4 * peer[0] + 2 * peer[1] + peer[2]
            return pltpu.make_async_remote_copy(
                src_ref=x_ref, dst_ref=o_ref.at[slot], send_sem=send_sems.at[j - 1], recv_sem=recv_sems.at[j - 1],
                device_id=peer, device_id_type=MESH)

        for j in range(1, 8):
            copy(j).start()
        for j in range(1, 8):
            landing(j).wait()
        mine.wait()

    return pl.pallas_call(
        body, name=name, in_specs=[ANY], out_specs=ANY,
        out_shape=jax.ShapeDtypeStruct((8,) + x.shape, x.dtype),
        scratch_shapes=[pltpu.SemaphoreType.DMA((7,)), pltpu.SemaphoreType.DMA((7,)), pltpu.SemaphoreType.DMA],
    )(x)


def _chip_exchange(xs, gather, name):
    n = len(xs)

    def body(*refs):
        start, wait = _chip_exchange_ops(refs[:n], refs[n:2 * n], *refs[2 * n:], gather)
        start()
        wait()

    return pl.pallas_call(
        body, name=name, in_specs=[ANY] * n, out_specs=[ANY] * n, out_shape=_chip_exchange_shapes(xs, gather),
        scratch_shapes=_chip_exchange_sems(n),
    )(*xs)


def _chip_exchange_shapes(xs, gather):
    return [jax.ShapeDtypeStruct(((4,) + x.shape) if gather else x.shape, x.dtype) for x in xs]


def _chip_exchange_sems(n):
    return [pltpu.SemaphoreType.DMA((3 * n,)), pltpu.SemaphoreType.DMA((3 * n,)), pltpu.SemaphoreType.DMA((n,))]


def _chip_exchange_ops(x_refs, o_refs, send_sems, recv_sems, local_sems, gather):
    n = len(x_refs)
    mx, my, mc = lax.axis_index("x"), lax.axis_index("y"), lax.axis_index("c")
    k0 = 2 * mx + my

    def local(a):
        src = x_refs[a] if gather else x_refs[a].at[k0]
        return pltpu.make_async_copy(src, o_refs[a].at[k0], local_sems.at[a])

    def copy(a, j, outgoing):
        px, py = _flip(mx, j & 2), _flip(my, j & 1)
        kp = 2 * px + py
        if outgoing:
            src = x_refs[a] if gather else x_refs[a].at[kp]
            dst = o_refs[a].at[k0]
        else:
            src = x_refs[a] if gather else x_refs[a].at[k0]
            dst = o_refs[a].at[kp]
        s = a * 3 + j - 1
        return pltpu.make_async_remote_copy(
            src_ref=src, dst_ref=dst, send_sem=send_sems.at[s], recv_sem=recv_sems.at[s],
            device_id=(px, py, mc), device_id_type=MESH)

    def start():
        for a in range(n):
            local(a).start()
            for j in range(1, 4):
                copy(a, j, True).start()

    def wait():
        for a in range(n):
            for j in range(1, 4):
                copy(a, j, False).wait()
            local(a).wait()

    return start, wait


def _call(body, *, name, grid, in_specs, out_specs, out_shape, args, scratch_shapes=(), sem=None, ride=None):
    if ride is None:
        return pl.pallas_call(
            body, name=name, grid=grid, in_specs=list(in_specs), out_specs=list(out_specs), out_shape=list(out_shape),
            scratch_shapes=list(scratch_shapes), compiler_params=_params(sem))(*args)
    xs, gather = ride
    n_in, n_out, n_scr, nx = len(in_specs), len(out_specs), len(scratch_shapes), len(xs)

    def wrapped(*refs):
        ins, x_refs = refs[:n_in], refs[n_in:n_in + nx]
        outs = refs[n_in + nx:n_in + nx + n_out]
        lands = refs[n_in + nx + n_out:n_in + 2 * nx + n_out]
        rest = refs[n_in + 2 * nx + n_out:]
        scr, sems = rest[:n_scr], rest[n_scr:]
        start, wait = _chip_exchange_ops(x_refs, lands, *sems, gather)
        ids = [pl.program_id(a) for a in range(len(grid))]
        first = functools.reduce(jnp.logical_and, [i == 0 for i in ids])
        last = functools.reduce(jnp.logical_and, [i == g - 1 for i, g in zip(ids, grid)])
        pl.when(first)(start)
        body(*ins, *outs, *scr)
        pl.when(last)(wait)

    return pl.pallas_call(
        wrapped, name=name, grid=grid, in_specs=list(in_specs) + [ANY] * nx, out_specs=list(out_specs) + [ANY] * nx,
        out_shape=list(out_shape) + _chip_exchange_shapes(xs, gather),
        scratch_shapes=list(scratch_shapes) + _chip_exchange_sems(nx),
        compiler_params=_params(("arbitrary",) * len(grid)))(*args, *xs)


def _sibling_exchange(xs, name):
    n = len(xs)

    def body(*refs):
        x_refs, o_refs = refs[:n], refs[n:2 * n]
        send_sems, recv_sems = refs[2 * n:]
        sib = (lax.axis_index("x"), lax.axis_index("y"), 1 - lax.axis_index("c"))
        copies = [pltpu.make_async_remote_copy(
            src_ref=x_refs[a], dst_ref=o_refs[a], send_sem=send_sems.at[a], recv_sem=recv_sems.at[a],
            device_id=sib, device_id_type=MESH) for a in range(n)]
        for cp in copies:
            cp.start()
        for cp in copies:
            cp.wait()

    return pl.pallas_call(
        body, name=name, in_specs=[ANY] * n, out_specs=[ANY] * n,
        out_shape=[jax.ShapeDtypeStruct(x.shape, x.dtype) for x in xs],
        scratch_shapes=[pltpu.SemaphoreType.DMA((n,)), pltpu.SemaphoreType.DMA((n,))],
    )(*xs)


PACK = 1024
PACK_ROWS = 512


def _pack(parts):
    flat = []
    for p in parts:
        v = p.reshape(-1).astype(F32)
        flat.append(jnp.pad(v, (0, (-v.shape[0]) % PACK)))
    total = sum(v.shape[0] for v in flat)
    flat.append(jnp.zeros(((-total) % (PACK_ROWS * 128),), F32))
    return jnp.concatenate(flat).reshape(-1, 128)


def _unpack(packed, shapes):
    flat = packed.reshape(-1)
    out, off = [], 0
    for shp in shapes:
        n = math.prod(shp)
        out.append(flat[off:off + n].reshape(shp))
        off += n + (-n) % PACK
    return out


def kernel(x, c, ln_pre_g, ln_post_g, w_mod, b_mod, w_in_ab, w_out_ab, sgu_norm_g, sgu_w, sgu_b, w_in_ssm, w_out_ssm, lam_re, lam_im, b_re, b_im, c_re, c_im, d_skip, log_dt, w_glu, b_glu, loss_target, m_ln_pre_g, m_ln_post_g, m_w_mod, m_b_mod, m_w_in_ab, m_w_out_ab, m_sgu_norm_g, m_sgu_w, m_sgu_b, m_w_in_ssm, m_w_out_ssm, m_lam_re, m_lam_im, m_b_re, m_b_im, m_c_re, m_c_im, m_d_skip, m_log_dt, m_w_glu, m_b_glu, v_ln_pre_g, v_ln_post_g, v_w_mod, v_b_mod, v_w_in_ab, v_w_out_ab, v_sgu_norm_g, v_sgu_w, v_sgu_b, v_w_in_ssm, v_w_out_ssm, v_lam_re, v_lam_im, v_b_re, v_b_im, v_c_re, v_c_im, v_d_skip, v_log_dt, v_w_glu, v_b_glu):
    given = dict(locals())
    mx, my, mc = lax.axis_index("x"), lax.axis_index("y"), lax.axis_index("c")
    me = 4 * mx + 2 * my + mc
    chip = 2 * mx + my

    _, l, d = x.shape
    x2, tgt = x[0], loss_target[0]
    n_in = w_in_ab.shape[2] * 4
    wa = wb = n_in // 7
    w = w_out_ssm.shape[1]
    g, p, cch = b_re.shape[1:]
    nmod = w_mod.shape[2]

    (gw_in_ab,) = _chip_exchange([_bf(w_in_ab[0])], True, "gather_w_in_ab")
    win_ab = jnp.concatenate([gw_in_ab[k] for k in range(4)], axis=1)
    later_shards = [_bf(w_out_ab[0]), _bf(w_in_ssm[0]), _bf(w_out_ssm[0]), _bf(w_glu[0]), d_skip, b_glu]

    cond = _silu_rows(c.reshape(d // 128, 128), "cond_silu")
    cond_all = _allgather8(cond, "gather_cond").reshape(8, d)
    b_shard = lax.dynamic_slice(b_mod, (0, chip * nmod), (2, nmod)).reshape(2, 1, nmod)
    cond_pad = jnp.pad(cond_all, ((0, MOD_ROWS - 8), (0, 0)))
    modp = _mod_fwd(cond_pad, w_mod, b_shard, "mod_fwd")[:, :8]
    modp_all = _allgather8(modp.reshape(16, nmod), "gather_mod").reshape(4, 2, 2, 8, nmod)
    mine = lax.dynamic_index_in_dim(lax.dynamic_index_in_dim(modp_all, mc, 1, False), me, 2, False)
    mod = mine.transpose(1, 0, 2).reshape(2, 3 * d)
    shift = [mod[a:a + 1, :d] for a in range(2)]
    scale = [mod[a:a + 1, d:2 * d] for a in range(2)]
    gate = [mod[a:a + 1, 2 * d:] for a in range(2)]
    pre_g = [ln_pre_g[a:a + 1] for a in range(2)]
    post_g = [ln_post_g[a:a + 1] for a in range(2)]

    sgu_w0, sgu_bt = sgu_w[0], sgu_b[0].T
    h0 = _pre_fwd(x2, pre_g[0], scale[0], shift[0], "pre0_fwd")
    w_gates = jnp.concatenate([win_ab[:, :3 * wa], win_ab[:, 3 * wa + 3 * wb:]], axis=1)
    proj0 = _matmul(h0, w_gates, "nn", F32, "proj0", tm=1024)
    qkv = _matmul(h0, win_ab[:, 3 * wa:3 * wa + 3 * wb], "nn", BF16, "proj0_qkv", tm=1024)
    out_b, lktot, (gw_out_ab, gw_in_ssm, gw_out_ssm, gw_glu, g_dskip, g_bglu) = _attn_fwd(
        qkv, wb, "attn_fwd", hp=8, ride=(later_shards, True))
    wout_ab = gw_out_ab.reshape(wa + wb, d)
    win_ssm = gw_in_ssm.reshape(d, 2 * w)
    wout_ssm = jnp.concatenate([gw_out_ssm[k] for k in range(4)], axis=1)
    wglu = gw_glu.reshape(w, w)
    dskip_full = g_dskip.reshape(1, w)
    bglu_full = g_bglu.reshape(1, w)
    cat =_sgu_fwd(proj0, out_b, sgu_norm_g, sgu_w0, sgu_bt, wa, wb, "sgu_fwd")
    y0 = _matmul(cat, wout_ab, "nn", F32, "out0", tm=1024)
    x1, h1 = _post_pre_fwd(x2, y0, gate[0], post_g[0], pre_g[1], scale[1], shift[1], "post0_pre1_fwd")

    s = g * p
    lr_c, li_c = lam_re.reshape(s, 1), lam_im.reshape(s, 1)
    ldt_c = jnp.repeat(log_dt.reshape(g), p).reshape(s, 1)
    br_c, bi_c = b_re.reshape(s, cch), b_im.reshape(s, cch)
    bb_re, bb_im, pw_re, pw_im = _ssm_prep(lr_c, li_c, ldt_c, br_c, bi_c, lr_c.reshape(1, s), li_c.reshape(1, s),
                                           ldt_c.reshape(1, s), "ssm_prep")
    bbd = _bf(_block_diag_b(bb_re, bb_im, g, p, cch))
    ccd = _bf(_block_diag_c(c_re[0], c_im[0], g, p, cch))
    proj1 = _matmul(h1, win_ssm, "nn", F32, "proj1", tm=1024)
    y_ssm, hs_re, hs_im = _ssm_fwd(proj1, bbd, ccd, pw_re, pw_im, dskip_full, w, "ssm_fwd")
    o1 = _glu_fwd(y_ssm, proj1, wglu, bglu_full, "glu_fwd")
    y1 = _matmul(o1, wout_ssm, "nn", F32, "out1", tm=1024)
    loss_vec, dy1, dx2, dgate1, dpost1 = _post_loss(x1, y1, gate[1], post_g[1], tgt, "post1_loss")

    do1 = _matmul(dy1, wout_ssm, "nt", F32, "out1_dx", tm=1024)
    gr_wout_ssm = _matmul(o1, dy1, "tn", BF16, "out1_dw", tm=1024, tk=1024, n_split=4)
    dy_ssm, dz1, gr_wglu, gr_bglu = _glu_bwd(do1, y_ssm, proj1, wglu, bglu_full, "glu_bwd")
    du1, dbbd, dccd, da_re, da_im, gr_dskip, (ld_wout_ssm, ld_wglu) = _ssm_bwd(
        proj1, dy_ssm, hs_re, hs_im, bbd, ccd, pw_re, pw_im, dskip_full, w, "ssm_bwd",
        ride=([gr_wout_ssm, gr_wglu.reshape(4, w // 4, w)], False))
    dproj1 = jnp.concatenate([du1, dz1], axis=1)
    dh1 = _matmul(dproj1, win_ssm, "nt", F32, "proj1_dx", tm=1024)
    gr_win_ssm = _matmul(h1, dproj1, "tn", BF16, "proj1_dw", tm=1024, tn=1024, tk=1024)
    dx1, dscale1, dshift1, dpre1, dy0, dgate0, dpost0 = _pre_bwd(
        dh1, dx2, x1, pre_g[1], scale[1], "pre1_post0_bwd", post=(y0, gate[0], post_g[0]))

    dcat = _matmul(dy0, wout_ab, "nt", F32, "out0_dx", tm=1024)
    gr_wout_ab = _matmul(cat, dy0, "tn", BF16, "out0_dw", tm=1024, tn=1024, tk=1024)
    dq, dk, dv, (ld_win_ssm, ld_wout_ab) = _attn_bwd(
        qkv, proj0, dcat, lktot, wa, wb, "attn_bwd",
        ride=([gr_win_ssm.reshape(4, d // 4, 2 * w), gr_wout_ab.reshape(4, (wa + wb) // 4, d)], False))
    dproj0, gr_sgu_w, gr_sgu_bt, gr_sgu_g = _sgu_bwd(proj0, out_b, dcat, dq, dk, dv, sgu_norm_g, sgu_w0, sgu_bt,
                                                     wa, wb, "sgu_bwd")
    gr_win_ab = _matmul(h0, dproj0, "tn", BF16, "proj0_dw", tm=1024, tk=1024, tn=896, n_split=4)
    dh0, (ld_win_ab,) = _matmul(dproj0, win_ab, "nt", F32, "proj0_dx", tm=1024, tk=1792, ride=([gr_win_ab], False))
    grad_x, dscale0, dshift0, dpre0 = _pre_bwd(dh0, dx1, x2, pre_g[0], scale[0], "pre0_bwd")

    landed = [ld_win_ab, ld_wout_ab, ld_win_ssm, ld_wout_ssm, ld_wglu]
    big_names = ["w_in_ab", "w_out_ab", "w_in_ssm", "w_out_ssm", "w_glu"]
    sums = [_sum_leading(a, "sum_" + nm) for a, nm in zip(landed, big_names)]
    sib = _sibling_exchange(sums, "sibling_grads")
    results = {}
    for nm, s_mine, s_sib in zip(big_names, sums, sib):
        shp = given[nm].shape
        two_d = lambda a: a.reshape(-1, shp[-1])
        outs = _adamw(two_d(given[nm]), [s_mine, s_sib], two_d(given["m_" + nm]), two_d(given["v_" + nm]),
                      "adamw_" + nm)
        results[nm] = [o.reshape(shp) for o in outs]

    dbb_re, dbb_im = _diag_of_b(dbbd, g, p, cch)
    dc_re, dc_im = _diag_of_c(dccd, g, p, cch)
    dmod = jnp.concatenate([jnp.concatenate([dshift0, dscale0, dgate0], axis=1),
                            jnp.concatenate([dshift1, dscale1, dgate1], axis=1)], axis=0)
    partial = [loss_vec[:, :1], jnp.concatenate([dpre0, dpre1], 0), jnp.concatenate([dpost0, dpost1], 0), dmod,
               gr_sgu_g, gr_sgu_w, gr_sgu_bt.T, da_re, da_im, dbb_re, dbb_im, dc_re, dc_im, gr_dskip, gr_bglu]
    part_shapes = [a.shape for a in partial]
    packed = _pack(partial)
    gathered = _allgather8(packed, "gather_small")
    total = _sum_leading(gathered, "sum_small")
    (loss_s, g_pre, g_post, g_bmod, g_sgu_g, g_sgu_w, g_sgu_b, s_da_re, s_da_im, s_dbb_re, s_dbb_im, g_c_re, g_c_im,
     g_dskip_full, g_bglu_full) = _unpack(total, part_shapes)
    loss = loss_s.reshape(())

    g_lr, g_li, g_ldt, g_br, g_bi = _ssm_prep_bwd(lr_c, li_c, ldt_c, br_c, bi_c, s_da_re.reshape(s, 1),
                                                  s_da_im.reshape(s, 1), s_dbb_re, s_dbb_im, p, "ssm_prep_bwd")
    small = {
        "ln_pre_g": g_pre, "ln_post_g": g_post, "b_mod": g_bmod, "sgu_norm_g": g_sgu_g,
        "sgu_w": g_sgu_w.reshape(sgu_w.shape), "sgu_b": g_sgu_b.reshape(sgu_b.shape),
        "lam_re": g_lr.reshape(lam_re.shape), "lam_im": g_li.reshape(lam_im.shape),
        "b_re": g_br.reshape(b_re.shape), "b_im": g_bi.reshape(b_im.shape),
        "c_re": g_c_re.reshape(c_re.shape), "c_im": g_c_im.reshape(c_im.shape),
        "d_skip": lax.dynamic_slice(g_dskip_full, (0, chip * (w // 4)), (1, w // 4)),
        "log_dt": g_ldt.reshape(log_dt.shape),
        "b_glu": lax.dynamic_slice(g_bglu_full, (0, chip * (w // 4)), (1, w // 4)),
    }
    small_names = list(small)
    small_shapes = [small[nm].shape for nm in small_names]
    outs = _adamw(_pack([given[nm] for nm in small_names]), [_pack([small[nm] for nm in small_names])],
                  _pack([given["m_" + nm] for nm in small_names]), _pack([given["v_" + nm] for nm in small_names]),
                  "adamw_small")
    unpacked = [_unpack(o, small_shapes) for o in outs]
    for i, nm in enumerate(small_names):
        results[nm] = [small[nm]] + [unpacked[k][i] for k in range(1, 4)]

    off = sum(math.prod(sh) + (-math.prod(sh)) % PACK for sh in part_shapes[:3])
    dmod_rows = gathered.reshape(8, -1)[:, off:off + 6 * d].reshape(8, 2, 3 * d)
    dmod_shard = lax.dynamic_slice(dmod_rows, (0, 0, chip * nmod), (8, 2, nmod)).transpose(1, 0, 2)
    dmod_pad = jnp.pad(dmod_shard, ((0, 0), (0, MOD_ROWS - 8), (0, 0)))
    gr_wmod = _mod_bwd(cond_pad.T, dmod_pad, "mod_bwd")
    two_d = lambda a: a.reshape(-1, nmod)
    outs = _adamw(two_d(w_mod), [two_d(gr_wmod)], two_d(m_w_mod), two_d(v_w_mod), "adamw_w_mod")
    results["w_mod"] = [o.reshape(w_mod.shape) for o in outs]

    names = ["ln_pre_g", "ln_post_g", "w_mod", "b_mod", "w_in_ab", "w_out_ab", "sgu_norm_g", "sgu_w", "sgu_b",
             "w_in_ssm", "w_out_ssm", "lam_re", "lam_im", "b_re", "b_im", "c_re", "c_im", "d_skip", "log_dt",
             "w_glu", "b_glu"]
    return (loss, grad_x[None], *[results[nm][0] for nm in names], *[results[nm][1] for nm in names],
            *[results[nm][2] for nm in names], *[results[nm][3] for nm in names])
```

```python
import functools
import math

import jax
import jax.numpy as jnp
from jax import lax
from jax.experimental import pallas as pl
from jax.experimental.pallas import tpu as pltpu

F32 = jnp.float32
BF16 = jnp.bfloat16
MESH = pl.DeviceIdType.MESH

EPS = 1e-6
HEAD = 128
SSM_T = 512
SSM_GB = 16
ADAM_LR, ADAM_B1, ADAM_B2, ADAM_EPS, ADAM_WD, ADAM_STEP = 0.001, 0.9, 0.999, 1e-08, 0.01, 10
VMEM_LIMIT = 56 * 1024 * 1024

NN = (((1,), (0,)), ((), ()))
NT = (((1,), (1,)), ((), ()))
TN = (((0,), (0,)), ((), ()))


def _params(sem=None):
    return pltpu.CompilerParams(dimension_semantics=sem, vmem_limit_bytes=VMEM_LIMIT)


def _dot(a, b, dims=NN):
    return lax.dot_general(a, b, dims, preferred_element_type=F32)


def _bf(x):
    return x.astype(BF16)


def _gelu(x):
    k = math.sqrt(2.0 / math.pi)
    t = jnp.tanh(k * (x + 0.044715 * x * x * x))
    return 0.5 * x * (1.0 + t)


def _gelu_grad(x):
    k = math.sqrt(2.0 / math.pi)
    x2 = x * x
    t = jnp.tanh(k * (x + 0.044715 * x * x2))
    return 0.5 * (1.0 + t) + 0.5 * x * (1.0 - t * t) * k * (1.0 + 3.0 * 0.044715 * x2)


def _sigmoid(x):
    return 1.0 / (1.0 + jnp.exp(-x))


def _silu(x):
    return x * _sigmoid(x)


def _silu_grad(x):
    s = _sigmoid(x)
    return s * (1.0 + x * (1.0 - s))


def _tile(n, t, mult=128):
    if n <= t:
        return n
    for cand in range(t - t % mult, 0, -mult):
        if n % cand == 0:
            return cand
    raise ValueError((n, t, mult))


def _matmul(a, b, mode, out_dtype, name, tm=512, tn=512, tk=2048, n_split=1, ride=None):
    if mode == "nn":
        (m, kk), (_, n) = a.shape, b.shape
    elif mode == "nt":
        (m, kk), (n, _) = a.shape, b.shape
    else:
        (kk, m), (_, n) = a.shape, b.shape
    tm, tk = _tile(m, tm), _tile(kk, tk)
    ns = n // n_split
    tn = _tile(ns, tn)
    nk = kk // tk
    dims = {"nn": NN, "nt": NT, "tn": TN}[mode]

    def body(a_ref, b_ref, o_ref, acc_ref):
        k = pl.program_id(2)
        part = _dot(_bf(a_ref[...]), _bf(b_ref[...]), dims)

        @pl.when(k == 0)
        def _():
            acc_ref[...] = part

        @pl.when(k > 0)
        def _():
            acc_ref[...] += part

        @pl.when(k == nk - 1)
        def _():
            o_ref[...] = acc_ref[...].astype(out_dtype).reshape(o_ref.shape)

    if mode == "nn":
        a_spec = pl.BlockSpec((tm, tk), lambda i, j, k: (i, k))
        b_spec = pl.BlockSpec((tk, tn), lambda i, j, k: (k, j))
    elif mode == "nt":
        a_spec = pl.BlockSpec((tm, tk), lambda i, j, k: (i, k))
        b_spec = pl.BlockSpec((tn, tk), lambda i, j, k: (j, k))
    else:
        a_spec = pl.BlockSpec((tk, tm), lambda i, j, k: (k, i))
        b_spec = pl.BlockSpec((tk, tn), lambda i, j, k: (k, j))
    if n_split == 1:
        out_shape = jax.ShapeDtypeStruct((m, n), out_dtype)
        o_spec = pl.BlockSpec((tm, tn), lambda i, j, k: (i, j))
    else:
        per = ns // tn
        out_shape = jax.ShapeDtypeStruct((n_split, m, ns), out_dtype)
        o_spec = pl.BlockSpec((1, tm, tn), lambda i, j, k: (j // per, i, j % per))
    outs = _call(body, name=name, grid=(m // tm, n // tn, nk), in_specs=[a_spec, b_spec], out_specs=[o_spec],
                 out_shape=[out_shape], scratch_shapes=[pltpu.VMEM((tm, tn), F32)], args=(a, b),
                 sem=("parallel", "parallel", "arbitrary"), ride=ride)
    return outs[0] if ride is None else (outs[0], outs[1:])


def _row_spec(tm, d):
    return pl.BlockSpec((tm, d), lambda i: (i, 0))


def _vec_spec(d):
    return pl.BlockSpec((1, d), lambda i: (0, 0))


def _acc(ref, first, val):
    @pl.when(first)
    def _():
        ref[...] = val

    @pl.when(jnp.logical_not(first))
    def _():
        ref[...] += val


def _colsum(x):
    return jnp.sum(x, axis=0, keepdims=True)


def _rownorm(x):
    r = lax.rsqrt(jnp.mean(x * x, axis=-1, keepdims=True) + EPS)
    return x * r, r


def _pre_fwd(x, g, scale, shift, name):
    l, d = x.shape
    tm = _tile(l, 256)

    def body(x_ref, g_ref, sc_ref, sh_ref, h_ref):
        n, _ = _rownorm(x_ref[...])
        h_ref[...] = _bf(n * g_ref[...] * (1.0 + sc_ref[...]) + sh_ref[...])

    return pl.pallas_call(
        body, name=name, grid=(l // tm,),
        in_specs=[_row_spec(tm, d), _vec_spec(d), _vec_spec(d), _vec_spec(d)],
        out_specs=_row_spec(tm, d), out_shape=jax.ShapeDtypeStruct((l, d), BF16),
        compiler_params=_params(("parallel",)),
    )(x, g, scale, shift)


def _post_pre_fwd(x, y, gate, pg, g1, scale1, shift1, name):
    l, d = x.shape
    tm = _tile(l, 256)

    def body(x_ref, y_ref, gate_ref, pg_ref, g1_ref, sc_ref, sh_ref, x1_ref, h1_ref):
        ny, _ = _rownorm(y_ref[...])
        x1 = x_ref[...] + gate_ref[...] * (ny * pg_ref[...])
        x1_ref[...] = x1
        n1, _ = _rownorm(x1)
        h1_ref[...] = _bf(n1 * g1_ref[...] * (1.0 + sc_ref[...]) + sh_ref[...])

    v = _vec_spec(d)
    return pl.pallas_call(
        body, name=name, grid=(l // tm,),
        in_specs=[_row_spec(tm, d), _row_spec(tm, d), v, v, v, v, v],
        out_specs=[_row_spec(tm, d), _row_spec(tm, d)],
        out_shape=[jax.ShapeDtypeStruct((l, d), F32), jax.ShapeDtypeStruct((l, d), BF16)],
        compiler_params=_params(("parallel",)),
    )(x, y, gate, pg, g1, scale1, shift1)


def _post_loss(x1, y1, gate, pg, target, name):
    l, d = x1.shape
    tm = _tile(l, 256)

    def body(x_ref, y_ref, gate_ref, pg_ref, t_ref, loss_ref, dy_ref, dx_ref, dgate_ref, dpg_ref):
        first = pl.program_id(0) == 0
        y = y_ref[...]
        ny, ry = _rownorm(y)
        q = ny * pg_ref[...]
        x2 = x_ref[...] + gate_ref[...] * q
        e = x2 - t_ref[...]
        _acc(loss_ref, first, jnp.full((1, 128), 0.5 / d, F32) * jnp.sum(e * e))
        dx2 = e * (1.0 / d)
        dx_ref[...] = dx2
        _acc(dgate_ref, first, _colsum(dx2 * q))
        dq = dx2 * gate_ref[...]
        _acc(dpg_ref, first, _colsum(dq * ny))
        dny = dq * pg_ref[...]
        dy = ry * (dny - ny * jnp.mean(dny * ny, axis=-1, keepdims=True))
        dy_ref[...] = _bf(dy)

    v = _vec_spec(d)
    return pl.pallas_call(
        body, name=name, grid=(l // tm,),
        in_specs=[_row_spec(tm, d), _row_spec(tm, d), v, v, _row_spec(tm, d)],
        out_specs=[_vec_spec(128), _row_spec(tm, d), _row_spec(tm, d), v, v],
        out_shape=[jax.ShapeDtypeStruct((1, 128), F32), jax.ShapeDtypeStruct((l, d), BF16),
                   jax.ShapeDtypeStruct((l, d), F32), jax.ShapeDtypeStruct((1, d), F32),
                   jax.ShapeDtypeStruct((1, d), F32)],
        compiler_params=_params(("arbitrary",)),
    )(x1, y1, gate, pg, target)


def _pre_bwd(dh, dres, x, g, scale, name, post=None):
    l, d = x.shape
    tm = _tile(l, 256)
    with_post = post is not None

    def body(*refs):
        if with_post:
            (dh_ref, dres_ref, x_ref, g_ref, sc_ref, y_ref, gate_ref, pg_ref,
             dx_ref, dsc_ref, dsh_ref, dg_ref, dy_ref, dgate_ref, dpg_ref) = refs
        else:
            dh_ref, dres_ref, x_ref, g_ref, sc_ref, dx_ref, dsc_ref, dsh_ref, dg_ref = refs
        first = pl.program_id(0) == 0
        dh = dh_ref[...]
        n, r = _rownorm(x_ref[...])
        _acc(dsc_ref, first, _colsum(dh * (n * g_ref[...])))
        _acc(dsh_ref, first, _colsum(dh))
        dyn = dh * (1.0 + sc_ref[...])
        _acc(dg_ref, first, _colsum(dyn * n))
        dn = dyn * g_ref[...]
        dx = dres_ref[...] + r * (dn - n * jnp.mean(dn * n, axis=-1, keepdims=True))
        dx_ref[...] = dx
        if with_post:
            ny, ry = _rownorm(y_ref[...])
            _acc(dgate_ref, first, _colsum(dx * (ny * pg_ref[...])))
            dq = dx * gate_ref[...]
            _acc(dpg_ref, first, _colsum(dq * ny))
            dny = dq * pg_ref[...]
            dy_ref[...] = _bf(ry * (dny - ny * jnp.mean(dny * ny, axis=-1, keepdims=True)))

    v = _vec_spec(d)
    row = _row_spec(tm, d)
    vec_out = jax.ShapeDtypeStruct((1, d), F32)
    in_specs = [row, row, row, v, v]
    args = [dh, dres, x, g, scale]
    out_specs = [row, v, v, v]
    out_shape = [jax.ShapeDtypeStruct((l, d), F32), vec_out, vec_out, vec_out]
    if with_post:
        in_specs += [row, v, v]
        args += list(post)
        out_specs += [row, v, v]
        out_shape += [jax.ShapeDtypeStruct((l, d), BF16), vec_out, vec_out]
    return pl.pallas_call(
        body, name=name, grid=(l // tm,), in_specs=in_specs, out_specs=out_specs, out_shape=out_shape,
        compiler_params=_params(("arbitrary",)),
    )(*args)


def _softplus_parts(z):
    e = jnp.exp(-jnp.abs(z))
    den = 1.0 + e
    lb = jnp.minimum(z, 0.0) - jnp.log(den)
    sig = jnp.where(z >= 0.0, 1.0, e) * pl.reciprocal(den, approx=True)
    return lb, lb - z, sig


def _tri(cmp, n=HEAD):
    row = lax.broadcasted_iota(jnp.int32, (n, n), 0)
    col = lax.broadcasted_iota(jnp.int32, (n, n), 1)
    return cmp(row, col)


ATT_T = 256


def _attn_fwd(qkv, wb, name, hp=4, ride=None):
    l = qkv.shape[0]
    t = ATT_T
    nh, nq = wb // HEAD, l // t
    hp = min(hp, nh)
    ng, wg = nh // hp, hp * HEAD
    scale = 1.0 / math.sqrt(HEAD)

    def body(q_ref, k_ref, v_ref, o_ref, lk_ref):
        i = pl.program_id(1)
        valid = _tri(lambda r, c: c < r, t)
        m_gt = _bf(_tri(lambda r, c: r > c, t).astype(F32))

        def tile(j, carry, diag):
            rows = pl.ds(pl.multiple_of(j * t, t), t)
            out = []
            for hh, (acc, run) in enumerate(carry):
                cs = slice(hh * HEAD, (hh + 1) * HEAD)
                z = _dot(q_ref[:, cs], k_ref[rows, cs], NT) * scale
                lb, lk, _ = _softplus_parts(z)
                if diag:
                    lk = jnp.where(valid, lk, 0.0)
                later = _dot(_bf(lk), m_gt)
                w = jnp.exp(lb + later + run)
                if diag:
                    w = jnp.where(valid, w, 0.0)
                out.append((acc + _dot(_bf(w), v_ref[rows, cs]), run + jnp.sum(lk, axis=1, keepdims=True)))
            return tuple(out)

        zero = (jnp.zeros((t, HEAD), F32), jnp.zeros((t, 1), F32))
        carry = tile(i, (zero,) * hp, True)
        carry = lax.fori_loop(0, i, lambda s, c: tile(i - 1 - s, c, False), carry)
        for hh, (acc, run) in enumerate(carry):
            cs = slice(hh * HEAD, (hh + 1) * HEAD)
            o_ref[:, cs] = acc
            lk_ref[:, cs] = jnp.broadcast_to(run, (t, HEAD))

    blk = lambda off: pl.BlockSpec((t, wg), lambda h, i: (i, off + h))
    full = lambda off: pl.BlockSpec((l, wg), lambda h, i: (0, off + h))
    out = pl.BlockSpec((t, wg), lambda h, i: (i, h))
    outs = _call(body, name=name, grid=(ng, nq), in_specs=[blk(0), full(ng), full(2 * ng)], out_specs=[out, out],
                 out_shape=[jax.ShapeDtypeStruct((l, wb), F32), jax.ShapeDtypeStruct((l, wb), F32)],
                 args=(qkv, qkv, qkv), sem=("parallel", "arbitrary"), ride=ride)
    return outs[0], outs[1], outs[2:]


def _attn_bwd(qkv, proj, dcat, lktot, wa, wb, name, hp=2, ride=None):
    l = qkv.shape[0]
    t = ATT_T
    nh, nq = wb // HEAD, l // t
    hp = min(hp, nh)
    ng, wg = nh // hp, hp * HEAD
    scale = 1.0 / math.sqrt(HEAD)

    def body(q_ref, k_ref, v_ref, bz_ref, dc_ref, lt_ref, dq_ref, dkt_out, dvt_out, do_s, qt_s, dot_s,
             dkt_ref, dvt_ref, out_sems):
        i = pl.program_id(1)

        @pl.when(i == 0)
        def _():
            dkt_ref[...] = jnp.zeros_like(dkt_ref)
            dvt_ref[...] = jnp.zeros_like(dvt_ref)

        do = dc_ref[...] * _silu(bz_ref[...])
        do_s[...] = _bf(do)
        for hh in range(hp):
            cs = slice(hh * HEAD, (hh + 1) * HEAD)
            qt_s[hh] = _bf(q_ref[:, cs].astype(F32).T * scale)
            dot_s[hh] = _bf(do[:, cs].T)
        valid = _tri(lambda r, c: c < r, t)
        m_le = _bf(_tri(lambda r, c: r <= c, t).astype(F32))
        m_lt = _bf(_tri(lambda r, c: r < c, t).astype(F32))

        def tile(j, carry, diag):
            rows = pl.ds(pl.multiple_of(j * t, t), t)
            out = []
            for hh, (dq, cpre, ppre) in enumerate(carry):
                cs = slice(hh * HEAD, (hh + 1) * HEAD)
                kb = k_ref[rows, cs]
                z = _dot(q_ref[:, cs], kb, NT) * scale
                lb, lk, sig = _softplus_parts(z)
                if diag:
                    lk = jnp.where(valid, lk, 0.0)
                pin = _dot(_bf(lk), m_le)
                w = jnp.exp(lb + (lt_ref[:, hh * HEAD:hh * HEAD + 1] - cpre) - pin)
                if diag:
                    w = jnp.where(valid, w, 0.0)
                da = _dot(do_s[:, cs], v_ref[rows, cs], NT) * w
                pex = _dot(_bf(da), m_lt)
                dz = da - sig * (da + ppre + pex)
                if diag:
                    dz = jnp.where(valid, dz, 0.0)
                dz_bf = _bf(dz)
                dkt, dvt = _dot(qt_s[hh], dz_bf), _dot(dot_s[hh], _bf(w))
                for half in range(t // HEAD):
                    dkt_ref[hh, sub * j + half] += dkt[:, half * HEAD:(half + 1) * HEAD]
                    dvt_ref[hh, sub * j + half] += dvt[:, half * HEAD:(half + 1) * HEAD]
                out.append((dq + _dot(dz_bf, kb), cpre + jnp.sum(lk, axis=1, keepdims=True),
                            ppre + pex[:, t - 1:] + da[:, t - 1:]))
            return tuple(out)

        zero = (jnp.zeros((t, HEAD), F32), jnp.zeros((t, 1), F32), jnp.zeros((t, 1), F32))
        carry = lax.fori_loop(0, i, lambda j, c: tile(j, c, False), (zero,) * hp)
        carry = tile(i, carry, True)
        for hh in range(hp):
            dq_ref[:, hh * HEAD:(hh + 1) * HEAD] = carry[hh][0] * scale

        @pl.when(i == nq - 1)
        def _():
            heads = pl.ds(pl.program_id(0) * hp, hp)
            copies = [pltpu.make_async_copy(dkt_ref, dkt_out.at[heads], out_sems.at[0]),
                      pltpu.make_async_copy(dvt_ref, dvt_out.at[heads], out_sems.at[1])]
            for cp in copies:
                cp.start()
            for cp in copies:
                cp.wait()

    sub = t // HEAD
    blk = lambda off: pl.BlockSpec((t, wg), lambda h, i: (i, off + h))
    full = lambda off: pl.BlockSpec((l, wg), lambda h, i: (0, off + h))
    acc_shape = jax.ShapeDtypeStruct((nh, l // HEAD, HEAD, HEAD), F32)
    acc_scratch = pltpu.VMEM((hp, l // HEAD, HEAD, HEAD), F32)
    outs = _call(
        body, name=name, grid=(ng, nq),
        in_specs=[blk(0), full(ng), full(2 * ng), blk(3 * wa // wg), blk(wa // wg), blk(0)],
        out_specs=[blk(0), ANY, ANY], out_shape=[jax.ShapeDtypeStruct((l, wb), F32), acc_shape, acc_shape],
        scratch_shapes=[pltpu.VMEM((t, wg), BF16), pltpu.VMEM((hp, HEAD, t), BF16), pltpu.VMEM((hp, HEAD, t), BF16),
                        acc_scratch, acc_scratch, pltpu.SemaphoreType.DMA((2,))],
        args=(qkv, qkv, qkv, proj, dcat, lktot), sem=("parallel", "arbitrary"), ride=ride)
    return outs[0], outs[1], outs[2], outs[3:]


def _sgu_heads(v, g_ref, w_ref, bt_ref, nh):
    keep = _tri(lambda r, c: r >= c)
    out = []
    for h in range(nh):
        cols = slice(h * HEAD, (h + 1) * HEAD)
        nv, r = _rownorm(v[:, cols])
        wm = jnp.where(keep, w_ref[h], 0.0)
        s = _dot(_bf(wm), _bf(nv * g_ref[:, cols])) + bt_ref[:, h:h + 1]
        out.append((nv, r, wm, s))
    return out


def _sgu_fwd(proj, out_b, norm_g, sgu_w, sgu_bt, wa, wb, name):
    l, n = proj.shape
    nh = wa // HEAD

    def body(au_ref, av_ref, az_ref, bz_ref, ob_ref, g_ref, w_ref, bt_ref, cat_ref):
        u, v, sz = _gelu(au_ref[...]), _gelu(av_ref[...]), _silu(az_ref[...])
        for h, (_, _, _, s) in enumerate(_sgu_heads(v, g_ref, w_ref, bt_ref, nh)):
            cols = slice(h * HEAD, (h + 1) * HEAD)
            cat_ref[:, cols] = _bf(u[:, cols] * s * sz[:, cols])
        cat_ref[:, wa:] = _bf(ob_ref[...] * _silu(bz_ref[...]))

    a_blk = lambda j: pl.BlockSpec((HEAD, wa), lambda i: (i, j))
    return pl.pallas_call(
        body, name=name, grid=(l // HEAD,),
        in_specs=[a_blk(0), a_blk(1), a_blk(2), a_blk(3), pl.BlockSpec((HEAD, wb), lambda i: (i, 0)),
                  _vec_spec(wa), pl.BlockSpec((nh, HEAD, HEAD), lambda i: (0, 0, 0)),
                  pl.BlockSpec((HEAD, nh), lambda i: (0, 0))],
        out_specs=pl.BlockSpec((HEAD, wa + wb), lambda i: (i, 0)),
        out_shape=jax.ShapeDtypeStruct((l, wa + wb), BF16),
        compiler_params=_params(("parallel",)),
    )(proj, proj, proj, proj, out_b, norm_g, sgu_w, sgu_bt)


def _sgu_bwd(proj, out_b, dcat, dq, dk, dv, norm_g, sgu_w, sgu_bt, wa, wb, name):
    l = proj.shape[0]
    n = 3 * wa + 4 * wb
    nh = wa // HEAD

    def body(au_ref, av_ref, az_ref, bz_ref, ob_ref, dc_ref, dq_ref, dk_ref, dv_ref, g_ref, w_ref, bt_ref,
             dp_ref, dw_ref, dbt_ref, dg_ref):
        first = pl.program_id(0) == 0
        keep = _tri(lambda r, c: r >= c)
        au, av, az = au_ref[...], av_ref[...], az_ref[...]
        u, v, sz = _gelu(au), _gelu(av), _silu(az)
        dgelu_u, dgelu_v, dsilu_z = _gelu_grad(au), _gelu_grad(av), _silu_grad(az)
        dg_parts = []
        for h, (nv, r, wm, s) in enumerate(_sgu_heads(v, g_ref, w_ref, bt_ref, nh)):
            cols = slice(h * HEAD, (h + 1) * HEAD)
            dca, uh, szh, gh = dc_ref[:, cols], u[:, cols], sz[:, cols], g_ref[:, cols]
            dp_ref[:, cols] = _bf(dca * s * szh * dgelu_u[:, cols])
            dp_ref[:, 2 * wa + h * HEAD:2 * wa + (h + 1) * HEAD] = _bf(dca * uh * s * dsilu_z[:, cols])
            ds = dca * uh * szh
            _acc(dw_ref.at[h], first, jnp.where(keep, _dot(_bf(ds), _bf(nv * gh), NT), 0.0))
            _acc(dbt_ref.at[:, h:h + 1], first, jnp.sum(ds, axis=1, keepdims=True))
            dvh = _dot(_bf(wm.T), _bf(ds))
            dg_parts.append(_colsum(dvh * nv))
            dnv = dvh * gh
            dvv = r * (dnv - nv * jnp.mean(dnv * nv, axis=-1, keepdims=True))
            dp_ref[:, wa + h * HEAD:wa + (h + 1) * HEAD] = _bf(dvv * dgelu_v[:, cols])
        _acc(dg_ref, first, jnp.concatenate(dg_parts, axis=1))
        base = 3 * wa
        dp_ref[:, base:base + wb] = _bf(dq_ref[...])
        for h in range(wb // HEAD):
            dp_ref[:, base + wb + h * HEAD:base + wb + (h + 1) * HEAD] = _bf(dk_ref[h, 0].T)
            dp_ref[:, base + 2 * wb + h * HEAD:base + 2 * wb + (h + 1) * HEAD] = _bf(dv_ref[h, 0].T)
        dp_ref[:, base + 3 * wb:] = _bf(dc_ref[:, wa:] * ob_ref[...] * _silu_grad(bz_ref[...]))

    a_blk = lambda j: pl.BlockSpec((HEAD, wa), lambda i: (i, j))
    b_blk = pl.BlockSpec((HEAD, wb), lambda i: (i, 0))
    t_blk = pl.BlockSpec((wb // HEAD, 1, HEAD, HEAD), lambda i: (0, i, 0, 0))
    w_spec = pl.BlockSpec((nh, HEAD, HEAD), lambda i: (0, 0, 0))
    bt_spec = pl.BlockSpec((HEAD, nh), lambda i: (0, 0))
    return pl.pallas_call(
        body, name=name, grid=(l // HEAD,),
        in_specs=[a_blk(0), a_blk(1), a_blk(2), a_blk(3), b_blk, pl.BlockSpec((HEAD, wa + wb), lambda i: (i, 0)),
                  b_blk, t_blk, t_blk, _vec_spec(wa), w_spec, bt_spec],
        out_specs=[pl.BlockSpec((HEAD, n), lambda i: (i, 0)), w_spec, bt_spec, _vec_spec(wa)],
        out_shape=[jax.ShapeDtypeStruct((l, n), BF16), jax.ShapeDtypeStruct((nh, HEAD, HEAD), F32),
                   jax.ShapeDtypeStruct((HEAD, nh), F32), jax.ShapeDtypeStruct((1, wa), F32)],
        compiler_params=_params(("arbitrary",)),
    )(proj, proj, proj, proj, out_b, dcat, dq, dk, dv, norm_g, sgu_w, sgu_bt)


def _ssm_discretise(lr, li, ldt, br, bi):
    dt = jnp.exp(ldt)
    mag = jnp.exp(lr * dt)
    a_re = mag * jnp.cos(li * dt)
    a_im = mag * jnp.sin(li * dt)
    den = lr * lr + li * li
    nr = a_re - 1.0
    coef_re = (nr * lr + a_im * li) / den
    coef_im = (a_im * lr - nr * li) / den
    return a_re, a_im, coef_re * br - coef_im * bi, coef_re * bi + coef_im * br


def _ssm_prep(lr, li, ldt, br, bi, lr_row, li_row, ldt_row, name):
    s, c = br.shape

    def body(lr_ref, li_ref, ldt_ref, br_ref, bi_ref, lrr_ref, lir_ref, ldtr_ref, bbr_ref, bbi_ref, tr_ref, ti_ref):
        _, _, bbr, bbi = _ssm_discretise(lr_ref[...], li_ref[...], ldt_ref[...], br_ref[...], bi_ref[...])
        bbr_ref[...] = bbr
        bbi_ref[...] = bbi
        row = lax.broadcasted_iota(jnp.int32, (SCAN_ROWS, 1), 0)
        blk, r = jnp.right_shift(row, 3), jnp.bitwise_and(row, 7)
        kind, rev = jnp.bitwise_and(blk, 3), blk >= 4
        step = jnp.left_shift(1, kind)
        n = jnp.where(kind < 3, step, jnp.where(rev, 8 - r, r + 1)).astype(F32)
        keep = (kind == 3) | (rev & (r < 8 - step)) | (jnp.logical_not(rev) & (r >= step))
        dt = jnp.exp(ldtr_ref[...])
        mag = jnp.exp(n * (lrr_ref[...] * dt))
        ang = n * (lir_ref[...] * dt)
        tr_ref[...] = jnp.where(keep, mag * jnp.cos(ang), 0.0)
        ti_ref[...] = jnp.where(keep, jnp.where(rev, -1.0, 1.0) * mag * jnp.sin(ang), 0.0)

    col = jax.ShapeDtypeStruct((s, c), F32)
    row = jax.ShapeDtypeStruct((SCAN_ROWS, s), F32)
    return pl.pallas_call(body, name=name, out_shape=[col, col, row, row])(
        lr, li, ldt, br, bi, lr_row, li_row, ldt_row)


def _ssm_prep_bwd(lr, li, ldt, br, bi, da_re, da_im, dbb_re, dbb_im, p, name):
    s, c = br.shape

    def body(lr_ref, li_ref, ldt_ref, br_ref, bi_ref, dar_ref, dai_ref, dbr_ref, dbi_ref,
             dlr_ref, dli_ref, dldt_ref, dbre_ref, dbim_ref):
        args = (lr_ref[...], li_ref[...], ldt_ref[...], br_ref[...], bi_ref[...])
        _, vjp = jax.vjp(_ssm_discretise, *args)
        dlr, dli, dldt, dbr, dbi = vjp((dar_ref[...], dai_ref[...], dbr_ref[...], dbi_ref[...]))
        dlr_ref[...] = dlr
        dli_ref[...] = dli
        dbre_ref[...] = dbr
        dbim_ref[...] = dbi
        idx = lax.broadcasted_iota(jnp.int32, (s, s // p), 0)
        grp = lax.broadcasted_iota(jnp.int32, (s, s // p), 1)
        own = (idx >= grp * p) & (idx < (grp + 1) * p)
        dldt_ref[...] = _colsum(jnp.where(own, dldt, 0.0))

    col1 = jax.ShapeDtypeStruct((s, 1), F32)
    colc = jax.ShapeDtypeStruct((s, c), F32)
    return pl.pallas_call(
        body, name=name, out_shape=[col1, col1, jax.ShapeDtypeStruct((1, s // p), F32), colc, colc],
    )(lr, li, ldt, br, bi, da_re, da_im, dbb_re, dbb_im)


SCAN_ROWS = 64


def _scan_groups(xr, xi, tr_ref, ti_ref, cr, ci, reverse):
    ng = xr.shape[0] // 8
    base = SCAN_ROWS // 2 if reverse else 0
    pr, pi = tr_ref[base + 24:base + 32, :], ti_ref[base + 24:base + 32, :]
    edge = slice(0, 1) if reverse else slice(7, 8)
    out_r, out_i = [None] * ng, [None] * ng
    for g in (range(ng - 1, -1, -1) if reverse else range(ng)):
        sr, si = xr[8 * g:8 * g + 8, :], xi[8 * g:8 * g + 8, :]
        for k in range(3):
            ar, ai = tr_ref[base + 8 * k:base + 8 * k + 8, :], ti_ref[base + 8 * k:base + 8 * k + 8, :]
            shift = 8 - (1 << k) if reverse else 1 << k
            rr, ri = pltpu.roll(sr, shift, 0), pltpu.roll(si, shift, 0)
            sr, si = sr + ar * rr - ai * ri, si + ar * ri + ai * rr
        sr, si = sr + pr * cr - pi * ci, si + pr * ci + pi * cr
        cr, ci = sr[edge, :], si[edge, :]
        out_r[g], out_i[g] = sr, si
    return jnp.concatenate(out_r, axis=0), jnp.concatenate(out_i, axis=0), cr, ci


def _ssm_fwd(proj, bbd, ccd, pw_re, pw_im, d_skip, w, name):
    l = proj.shape[0]
    nb, cw, ns2 = bbd.shape
    ns = ns2 // 2
    nc = l // SSM_T

    def body(u_ref, bbd_ref, ccd_ref, pr_ref, pi_ref, d_ref, y_ref, hsr_ref, hsi_ref, hr_s, hi_s):
        @pl.when(pl.program_id(1) == 0)
        def _():
            hr_s[...] = jnp.zeros_like(hr_s)
            hi_s[...] = jnp.zeros_like(hi_s)

        hsr_ref[...] = hr_s[...].reshape(hsr_ref.shape)
        hsi_ref[...] = hi_s[...].reshape(hsi_ref.shape)
        u = u_ref[...]
        bu = _dot(_bf(u), bbd_ref[0])
        hr, hi, cr, ci = _scan_groups(bu[:, :ns], bu[:, ns:], pr_ref, pi_ref, hr_s[...], hi_s[...], False)
        hr_s[...] = cr
        hi_s[...] = ci
        y_ref[...] = _dot(_bf(jnp.concatenate([hr, hi], axis=1)), ccd_ref[0]) + d_ref[...] * u

    tab = pl.BlockSpec((SCAN_ROWS, ns), lambda b, k: (0, b))
    return pl.pallas_call(
        body, name=name, grid=(nb, nc),
        in_specs=[pl.BlockSpec((SSM_T, cw), lambda b, k: (k, b)),
                  pl.BlockSpec((1, cw, ns2), lambda b, k: (b, 0, 0)),
                  pl.BlockSpec((1, ns2, cw), lambda b, k: (b, 0, 0)),
                  tab, tab, pl.BlockSpec((1, cw), lambda b, k: (0, b))],
        out_specs=[pl.BlockSpec((SSM_T, cw), lambda b, k: (k, b)),
                   pl.BlockSpec((1, 1, ns), lambda b, k: (k, 0, b)), pl.BlockSpec((1, 1, ns), lambda b, k: (k, 0, b))],
        out_shape=[jax.ShapeDtypeStruct((l, w), F32), jax.ShapeDtypeStruct((nc, 1, nb * ns), F32),
                   jax.ShapeDtypeStruct((nc, 1, nb * ns), F32)],
        scratch_shapes=[pltpu.VMEM((1, ns), F32), pltpu.VMEM((1, ns), F32)],
        compiler_params=_params(("parallel", "arbitrary")),
    )(proj, bbd, ccd, pw_re, pw_im, d_skip)


def _ssm_bwd(proj, dy, hs_re, hs_im, bbd, ccd, pw_re, pw_im, d_skip, w, name, ride=None):
    l = proj.shape[0]
    nb, cw, ns2 = bbd.shape
    ns = ns2 // 2
    nc = l // SSM_T

    def body(u_ref, dy_ref, hsr_ref, hsi_ref, bbd_ref, ccd_ref, pr_ref, pi_ref, d_ref,
             du_ref, dbbd_ref, dccd_ref, dar_ref, dai_ref, dd_ref, gr_s, gi_s):
        first = pl.program_id(1) == 0

        @pl.when(first)
        def _():
            gr_s[...] = jnp.zeros_like(gr_s)
            gi_s[...] = jnp.zeros_like(gi_s)

        u, dy = u_ref[...], dy_ref[...]
        u_bf, dy_bf = _bf(u), _bf(dy)
        hr0, hi0 = hsr_ref[0], hsi_ref[0]
        bu = _dot(u_bf, bbd_ref[0])
        hr, hi, _, _ = _scan_groups(bu[:, :ns], bu[:, ns:], pr_ref, pi_ref, hr0, hi0, False)
        dh = _dot(dy_bf, ccd_ref[0], NT)
        gr, gi, gcr, gci = _scan_groups(dh[:, :ns], dh[:, ns:], pr_ref, pi_ref, gr_s[...], gi_s[...], True)
        gr_s[...] = gcr
        gi_s[...] = gci
        row0 = lax.broadcasted_iota(jnp.int32, hr.shape, 0) == 0
        pr_h = jnp.where(row0, hr0, pltpu.roll(hr, 1, 0))
        pi_h = jnp.where(row0, hi0, pltpu.roll(hi, 1, 0))
        _acc(dar_ref, first, _colsum(pr_h * gr + pi_h * gi))
        _acc(dai_ref, first, _colsum(pr_h * gi - pi_h * gr))
        g_bf = _bf(jnp.concatenate([gr, gi], axis=1))
        _acc(dbbd_ref.at[0], first, _dot(_bf(u.T), g_bf))
        _acc(dccd_ref.at[0], first, _dot(_bf(jnp.concatenate([hr, hi], axis=1).T), dy_bf))
        du_ref[...] = _bf(_dot(g_bf, bbd_ref[0], NT) + d_ref[...] * dy)
        _acc(dd_ref, first, _colsum(dy * u))

    rev = lambda b, k: (nc - 1 - k, b)
    outs = _call(
        body, name=name, grid=(nb, nc), ride=ride, sem=("parallel", "arbitrary"),
        args=(proj, dy, hs_re, hs_im, bbd, ccd, pw_re, pw_im, d_skip),
        in_specs=[pl.BlockSpec((SSM_T, cw), rev), pl.BlockSpec((SSM_T, cw), rev),
                  pl.BlockSpec((1, 1, ns), lambda b, k: (nc - 1 - k, 0, b)),
                  pl.BlockSpec((1, 1, ns), lambda b, k: (nc - 1 - k, 0, b)),
                  pl.BlockSpec((1, cw, ns2), lambda b, k: (b, 0, 0)),
                  pl.BlockSpec((1, ns2, cw), lambda b, k: (b, 0, 0)),
                  pl.BlockSpec((SCAN_ROWS, ns), lambda b, k: (0, b)), pl.BlockSpec((SCAN_ROWS, ns), lambda b, k: (0, b)),
                  pl.BlockSpec((1, cw), lambda b, k: (0, b))],
        out_specs=[pl.BlockSpec((SSM_T, cw), rev),
                   pl.BlockSpec((1, cw, ns2), lambda b, k: (b, 0, 0)),
                   pl.BlockSpec((1, ns2, cw), lambda b, k: (b, 0, 0)),
                   pl.BlockSpec((1, ns), lambda b, k: (0, b)), pl.BlockSpec((1, ns), lambda b, k: (0, b)),
                   pl.BlockSpec((1, cw), lambda b, k: (0, b))],
        out_shape=[jax.ShapeDtypeStruct((l, w), BF16), jax.ShapeDtypeStruct(bbd.shape, F32),
                   jax.ShapeDtypeStruct(ccd.shape, F32), jax.ShapeDtypeStruct((1, nb * ns), F32),
                   jax.ShapeDtypeStruct((1, nb * ns), F32), jax.ShapeDtypeStruct((1, w), F32)],
        scratch_shapes=[pltpu.VMEM((1, ns), F32), pltpu.VMEM((1, ns), F32)])
    return (*outs[:6], outs[6:])


def _block_diag_b(bb_re, bb_im, g, p, c):
    nb = g // SSM_GB
    eye = jnp.eye(SSM_GB, dtype=F32)

    def one(bb):
        t = bb.reshape(nb, SSM_GB, p, c).transpose(0, 1, 3, 2)
        return (t[:, :, :, None, :] * eye[None, :, None, :, None]).reshape(nb, SSM_GB * c, SSM_GB * p)

    return jnp.concatenate([one(bb_re), one(bb_im)], axis=2)


def _block_diag_c(c_re, c_im, g, p, c):
    nb = g // SSM_GB
    eye = jnp.eye(SSM_GB, dtype=F32)

    def one(cc):
        t = cc.reshape(nb, SSM_GB, c, p).transpose(0, 1, 3, 2)
        return (t[:, :, :, None, :] * eye[None, :, None, :, None]).reshape(nb, SSM_GB * p, SSM_GB * c)

    return jnp.concatenate([one(c_re), one(-c_im)], axis=1)


def _diag_of_b(dbbd, g, p, c):
    nb = g // SSM_GB
    t = dbbd.reshape(nb, SSM_GB, c, 2, SSM_GB, p)
    idx = jnp.arange(SSM_GB)
    d = t[:, idx, :, :, idx, :]
    d = d.transpose(1, 0, 3, 4, 2)
    return d[:, :, 0].reshape(g * p, c), d[:, :, 1].reshape(g * p, c)


def _diag_of_c(dccd, g, p, c):
    nb = g // SSM_GB
    t = dccd.reshape(nb, 2, SSM_GB, p, SSM_GB, c)
    idx = jnp.arange(SSM_GB)
    d = t[:, :, idx, :, idx, :]
    d = d.transpose(1, 0, 2, 4, 3)
    return d[:, :, 0].reshape(g, c, p), -d[:, :, 1].reshape(g, c, p)


def _glu_fwd(y, proj, w_glu, b_glu, name):
    l, w = y.shape
    tm = _tile(l, 256)

    def body(y_ref, z_ref, w_ref, b_ref, o_ref):
        g = _gelu(y_ref[...])
        t = _dot(_bf(g), w_ref[...]) + b_ref[...]
        o_ref[...] = _bf(g * _sigmoid(t) * _silu(z_ref[...]))

    return pl.pallas_call(
        body, name=name, grid=(l // tm,),
        in_specs=[_row_spec(tm, w), pl.BlockSpec((tm, w), lambda i: (i, 1)),
                  pl.BlockSpec((w, w), lambda i: (0, 0)), _vec_spec(w)],
        out_specs=_row_spec(tm, w), out_shape=jax.ShapeDtypeStruct((l, w), BF16),
        compiler_params=_params(("parallel",)),
    )(y, proj, w_glu, b_glu)


def _glu_bwd(do, y, proj, w_glu, b_glu, name):
    l, w = y.shape
    tm = _tile(l, 256)
    nsteps = l // tm

    def body(do_ref, y_ref, z_ref, w_ref, b_ref, dy_ref, dz_ref, dw_ref, db_ref, dw_acc):
        i = pl.program_id(0)
        first = i == 0
        yv, z, do = y_ref[...], z_ref[...], do_ref[...]
        g = _gelu(yv)
        g_bf = _bf(g)
        sg = _sigmoid(_dot(g_bf, w_ref[...]) + b_ref[...])
        dyy = do * _silu(z)
        dz_ref[...] = _bf(do * g * sg * _silu_grad(z))
        dt = dyy * g * sg * (1.0 - sg)
        dt_bf = _bf(dt)
        dg = dyy * sg + _dot(dt_bf, w_ref[...], NT)
        dy_ref[...] = dg * _gelu_grad(yv)
        _acc(dw_acc, first, _dot(_bf(g.T), dt_bf))
        _acc(db_ref, first, _colsum(dt))

        @pl.when(i == nsteps - 1)
        def _():
            dw_ref[...] = _bf(dw_acc[...])

    return pl.pallas_call(
        body, name=name, grid=(nsteps,),
        in_specs=[_row_spec(tm, w), _row_spec(tm, w), pl.BlockSpec((tm, w), lambda i: (i, 1)),
                  pl.BlockSpec((w, w), lambda i: (0, 0)), _vec_spec(w)],
        out_specs=[_row_spec(tm, w), _row_spec(tm, w), pl.BlockSpec((w, w), lambda i: (0, 0)), _vec_spec(w)],
        out_shape=[jax.ShapeDtypeStruct((l, w), F32), jax.ShapeDtypeStruct((l, w), BF16),
                   jax.ShapeDtypeStruct((w, w), BF16), jax.ShapeDtypeStruct((1, w), F32)],
        scratch_shapes=[pltpu.VMEM((w, w), F32)],
        compiler_params=_params(("arbitrary",)),
    )(do, y, proj, w_glu, b_glu)


MOD_ROWS = 128


def _mod_fwd(cond_pad, w_mod, b_shard, name):
    nl, d, ncol = w_mod.shape
    tn = _tile(ncol, 512)

    def body(c_ref, w_ref, b_ref, o_ref):
        o_ref[0] = _dot(_bf(c_ref[...]), _bf(w_ref[0])) + b_ref[0]

    return pl.pallas_call(
        body, name=name, grid=(nl, ncol // tn),
        in_specs=[pl.BlockSpec((MOD_ROWS, d), lambda a, j: (0, 0)),
                  pl.BlockSpec((1, d, tn), lambda a, j: (a, 0, j)),
                  pl.BlockSpec((1, 1, tn), lambda a, j: (a, 0, j))],
        out_specs=pl.BlockSpec((1, MOD_ROWS, tn), lambda a, j: (a, 0, j)),
        out_shape=jax.ShapeDtypeStruct((nl, MOD_ROWS, ncol), F32),
        compiler_params=_params(("parallel", "parallel")),
    )(cond_pad, w_mod, b_shard)


def _mod_bwd(cond_pad_t, dmod_pad, name):
    nl, _, ncol = dmod_pad.shape
    d = cond_pad_t.shape[0]
    tn = _tile(ncol, 512)

    def body(c_ref, dm_ref, o_ref):
        o_ref[0] = _dot(_bf(c_ref[...]), _bf(dm_ref[0]))

    return pl.pallas_call(
        body, name=name, grid=(nl, ncol // tn),
        in_specs=[pl.BlockSpec((d, MOD_ROWS), lambda a, j: (0, 0)),
                  pl.BlockSpec((1, MOD_ROWS, tn), lambda a, j: (a, 0, j))],
        out_specs=pl.BlockSpec((1, d, tn), lambda a, j: (a, 0, j)),
        out_shape=jax.ShapeDtypeStruct((nl, d, ncol), F32),
        compiler_params=_params(("parallel", "parallel")),
    )(cond_pad_t, dmod_pad)


def _silu_rows(c2d, name):
    def body(c_ref, o_ref):
        o_ref[...] = _silu(c_ref[...])

    return pl.pallas_call(body, name=name, out_shape=jax.ShapeDtypeStruct(c2d.shape, F32))(c2d)


def _sum_leading(x, name):
    n, r, c = x.shape
    tr = _tile(r, max(16, (1 << 20) // (4 * c)), 16 if r % 16 == 0 else 8)

    def body(x_ref, o_ref):
        acc = x_ref[0].astype(F32)
        for k in range(1, n):
            acc = acc + x_ref[k].astype(F32)
        o_ref[...] = acc

    return pl.pallas_call(
        body, name=name, grid=(r // tr,),
        in_specs=[pl.BlockSpec((n, tr, c), lambda i: (0, i, 0))], out_specs=pl.BlockSpec((tr, c), lambda i: (i, 0)),
        out_shape=jax.ShapeDtypeStruct((r, c), F32), compiler_params=_params(("parallel",)),
    )(x)


def _adamw(w, gs, m, v, name):
    r, c = w.shape
    tr = _tile(r, max(8, (1 << 19) // (4 * c)), 8)
    ng = len(gs)

    def body(*refs):
        w_ref, g_refs, m_ref, v_ref = refs[0], refs[1:1 + ng], refs[1 + ng], refs[2 + ng]
        g_ref, d_ref, nm_ref, nv_ref = refs[3 + ng:]
        g = g_refs[0][...]
        for extra in g_refs[1:]:
            g = g + extra[...]
        g_ref[...] = g
        nm = ADAM_B1 * m_ref[...] + (1.0 - ADAM_B1) * g
        nv = ADAM_B2 * v_ref[...] + (1.0 - ADAM_B2) * (g * g)
        nm_ref[...] = nm
        nv_ref[...] = nv
        m_hat = nm / (1.0 - ADAM_B1 ** ADAM_STEP)
        v_hat = nv / (1.0 - ADAM_B2 ** ADAM_STEP)
        d_ref[...] = -ADAM_LR * (m_hat / (jnp.sqrt(v_hat) + ADAM_EPS) + ADAM_WD * w_ref[...])

    spec = pl.BlockSpec((tr, c), lambda i: (i, 0))
    shp = jax.ShapeDtypeStruct((r, c), F32)
    return pl.pallas_call(
        body, name=name, grid=(r // tr,), in_specs=[spec] * (3 + ng), out_specs=[spec] * 4,
        out_shape=[shp] * 4, compiler_params=_params(("parallel",)),
    )(w, *gs, m, v)


ANY = pl.BlockSpec(memory_space=pl.ANY)


def _flip(v, bit):
    return 1 - v if bit else v


def _allgather8_ops(x_ref, o_ref, send_sems, recv_sems, local_sem):
    mx, my, mc = lax.axis_index("x"), lax.axis_index("y"), lax.axis_index("c")
    me = 4 * mx + 2 * my + mc

    def mine():
        return pltpu.make_async_copy(x_ref, o_ref.at[me], local_sem)

    def copy(j, outgoing):
        peer = (_flip(mx, j & 4), _flip(my, j & 2), _flip(mc, j & 1))
        slot = me if outgoing else 4 * peer[0] + 2 * peer[1] + peer[2]
        return pltpu.make_async_remote_copy(
            src_ref=x_ref, dst_ref=o_ref.at[slot], send_sem=send_sems.at[j - 1], recv_sem=recv_sems.at[j - 1],
            device_id=peer, device_id_type=MESH)

    def start():
        mine().start()
        for j in range(1, 8):
            copy(j, True).start()

    def wait():
        for j in range(1, 8):
            copy(j, False).wait()
        mine().wait()

    return start, wait


def _ride_all8(x):
    return dict(xs=[x], shapes=[jax.ShapeDtypeStruct((8,) + x.shape, x.dtype)],
                sems=[pltpu.SemaphoreType.DMA((7,)), pltpu.SemaphoreType.DMA((7,)), pltpu.SemaphoreType.DMA],
                ops=lambda x_refs, o_refs, sems: _allgather8_ops(x_refs[0], o_refs[0], *sems))


def _ride_chip(xs, gather):
    return dict(xs=list(xs), shapes=_chip_exchange_shapes(xs, gather), sems=_chip_exchange_sems(len(xs)),
                ops=lambda x_refs, o_refs, sems: _chip_exchange_ops(x_refs, o_refs, *sems, gather))


def _allgather8(x, name):
    def body(x_ref, o_ref, *sems):
        start, wait = _allgather8_ops(x_ref, o_ref, *sems)
        start()
        wait()

    ride = _ride_all8(x)
    return pl.pallas_call(body, name=name, in_specs=[ANY], out_specs=ANY, out_shape=ride["shapes"][0],
                          scratch_shapes=ride["sems"])(x)


def _gather_halves(x, name):
    r = x.shape[0]
    half = r // 2

    def body(x_ref, o_ref, ici_send, ici_recv, d2d_send, d2d_recv, local_sem):
        mx, my, mc = lax.axis_index("x"), lax.axis_index("y"), lax.axis_index("c")
        k0 = 2 * mx + my
        mine = pl.ds(pl.multiple_of(mc * half, 16), half)
        theirs = pl.ds(pl.multiple_of((1 - mc) * half, 16), half)
        local = pltpu.make_async_copy(x_ref, o_ref.at[k0], local_sem)
        local.start()

        def chips(j):
            px, py = _flip(mx, j & 2), _flip(my, j & 1)
            return px, py, 2 * px + py

        def over_ici(j, outgoing):
            px, py, kp = chips(j)
            dst = o_ref.at[k0, mine] if outgoing else o_ref.at[kp, mine]
            return pltpu.make_async_remote_copy(
                src_ref=x_ref.at[mine], dst_ref=dst, send_sem=ici_send.at[j - 1], recv_sem=ici_recv.at[j - 1],
                device_id=(px, py, mc), device_id_type=MESH)

        def over_d2d(j, outgoing):
            _, _, kp = chips(j)
            rows = mine if outgoing else theirs
            return pltpu.make_async_remote_copy(
                src_ref=o_ref.at[kp, rows], dst_ref=o_ref.at[kp, rows], send_sem=d2d_send.at[j - 1],
                recv_sem=d2d_recv.at[j - 1], device_id=(mx, my, 1 - mc), device_id_type=MESH)

        for j in range(1, 4):
            over_ici(j, True).start()
        for j in range(1, 4):
            over_ici(j, False).wait_recv()
            over_d2d(j, True).start()
        for j in range(1, 4):
            over_ici(j, True).wait_send()
            over_d2d(j, True).wait_send()
            over_d2d(j, False).wait_recv()
        local.wait()

    dma3 = pltpu.SemaphoreType.DMA((3,))
    return pl.pallas_call(
        body, name=name, in_specs=[ANY], out_specs=ANY, out_shape=jax.ShapeDtypeStruct((4,) + x.shape, x.dtype),
        scratch_shapes=[dma3, dma3, dma3, dma3, pltpu.SemaphoreType.DMA])(x)


def _chip_exchange(xs, gather, name):
    n = len(xs)

    def body(*refs):
        start, wait = _chip_exchange_ops(refs[:n], refs[n:2 * n], *refs[2 * n:], gather)
        start()
        wait()

    return pl.pallas_call(
        body, name=name, in_specs=[ANY] * n, out_specs=[ANY] * n, out_shape=_chip_exchange_shapes(xs, gather),
        scratch_shapes=_chip_exchange_sems(n),
    )(*xs)


def _chip_exchange_shapes(xs, gather):
    return [jax.ShapeDtypeStruct(((4,) + x.shape) if gather else x.shape, x.dtype) for x in xs]


def _chip_exchange_sems(n):
    return [pltpu.SemaphoreType.DMA((3 * n,)), pltpu.SemaphoreType.DMA((3 * n,)), pltpu.SemaphoreType.DMA((n,))]


def _chip_exchange_ops(x_refs, o_refs, send_sems, recv_sems, local_sems, gather):
    n = len(x_refs)
    mx, my, mc = lax.axis_index("x"), lax.axis_index("y"), lax.axis_index("c")
    k0 = 2 * mx + my

    def local(a):
        src = x_refs[a] if gather else x_refs[a].at[k0]
        return pltpu.make_async_copy(src, o_refs[a].at[k0], local_sems.at[a])

    def copy(a, j, outgoing):
        px, py = _flip(mx, j & 2), _flip(my, j & 1)
        kp = 2 * px + py
        if outgoing:
            src = x_refs[a] if gather else x_refs[a].at[kp]
            dst = o_refs[a].at[k0]
        else:
            src = x_refs[a] if gather else x_refs[a].at[k0]
            dst = o_refs[a].at[kp]
        s = a * 3 + j - 1
        return pltpu.make_async_remote_copy(
            src_ref=src, dst_ref=dst, send_sem=send_sems.at[s], recv_sem=recv_sems.at[s],
            device_id=(px, py, mc), device_id_type=MESH)

    def start():
        for a in range(n):
            local(a).start()
            for j in range(1, 4):
                copy(a, j, True).start()

    def wait():
        for a in range(n):
            for j in range(1, 4):
                copy(a, j, False).wait()
            local(a).wait()

    return start, wait


def _call(body, *, name, grid, in_specs, out_specs, out_shape, args, scratch_shapes=(), sem=None, ride=None):
    if not ride:
        return pl.pallas_call(
            body, name=name, grid=grid, in_specs=list(in_specs), out_specs=list(out_specs), out_shape=list(out_shape),
            scratch_shapes=list(scratch_shapes), compiler_params=_params(sem))(*args)
    xs = [x for r in ride for x in r["xs"]]
    shapes = [s for r in ride for s in r["shapes"]]
    sems = [s for r in ride for s in r["sems"]]
    n_in, n_out, n_scr, nx = len(in_specs), len(out_specs), len(scratch_shapes), len(xs)

    def wrapped(*refs):
        ins, x_refs = refs[:n_in], refs[n_in:n_in + nx]
        outs = refs[n_in + nx:n_in + nx + n_out]
        lands = refs[n_in + nx + n_out:n_in + 2 * nx + n_out]
        rest = refs[n_in + 2 * nx + n_out:]
        scr, sem_refs = rest[:n_scr], rest[n_scr:]
        ops, xo, so = [], 0, 0
        for r in ride:
            nr, ns = len(r["xs"]), len(r["sems"])
            ops.append(r["ops"](x_refs[xo:xo + nr], lands[xo:xo + nr], sem_refs[so:so + ns]))
            xo, so = xo + nr, so + ns
        ids = [pl.program_id(a) for a in range(len(grid))]
        first = functools.reduce(jnp.logical_and, [i == 0 for i in ids])
        last = functools.reduce(jnp.logical_and, [i == g - 1 for i, g in zip(ids, grid)])

        @pl.when(first)
        def _():
            for start, _ in ops:
                start()

        body(*ins, *outs, *scr)

        @pl.when(last)
        def _():
            for _, wait in ops:
                wait()

    return pl.pallas_call(
        wrapped, name=name, grid=grid, in_specs=list(in_specs) + [ANY] * nx, out_specs=list(out_specs) + [ANY] * nx,
        out_shape=list(out_shape) + shapes, scratch_shapes=list(scratch_shapes) + sems,
        compiler_params=_params(("arbitrary",) * len(grid)))(*args, *xs)


def _sibling_exchange(xs, name):
    n = len(xs)

    def body(*refs):
        x_refs, o_refs = refs[:n], refs[n:2 * n]
        send_sems, recv_sems = refs[2 * n:]
        sib = (lax.axis_index("x"), lax.axis_index("y"), 1 - lax.axis_index("c"))
        copies = [pltpu.make_async_remote_copy(
            src_ref=x_refs[a], dst_ref=o_refs[a], send_sem=send_sems.at[a], recv_sem=recv_sems.at[a],
            device_id=sib, device_id_type=MESH) for a in range(n)]
        for cp in copies:
            cp.start()
        for cp in copies:
            cp.wait()

    return pl.pallas_call(
        body, name=name, in_specs=[ANY] * n, out_specs=[ANY] * n,
        out_shape=[jax.ShapeDtypeStruct(x.shape, x.dtype) for x in xs],
        scratch_shapes=[pltpu.SemaphoreType.DMA((n,)), pltpu.SemaphoreType.DMA((n,))],
    )(*xs)


PACK = 1024
PACK_ROWS = 512


def _pack(parts):
    flat = []
    for p in parts:
        v = p.reshape(-1).astype(F32)
        flat.append(jnp.pad(v, (0, (-v.shape[0]) % PACK)))
    total = sum(v.shape[0] for v in flat)
    flat.append(jnp.zeros(((-total) % (PACK_ROWS * 128),), F32))
    return jnp.concatenate(flat).reshape(-1, 128)


def _unpack_rows(gathered, shapes):
    flat = gathered.reshape(gathered.shape[0], -1)
    out, off = [], 0
    for shp in shapes:
        n = math.prod(shp)
        out.append(flat[:, off:off + n].reshape((flat.shape[0],) + tuple(shp)))
        off += n + (-n) % PACK
    return out


def _unpack(packed, shapes):
    flat = packed.reshape(-1)
    out, off = [], 0
    for shp in shapes:
        n = math.prod(shp)
        out.append(flat[off:off + n].reshape(shp))
        off += n + (-n) % PACK
    return out


def kernel(x, c, ln_pre_g, ln_post_g, w_mod, b_mod, w_in_ab, w_out_ab, sgu_norm_g, sgu_w, sgu_b, w_in_ssm, w_out_ssm, lam_re, lam_im, b_re, b_im, c_re, c_im, d_skip, log_dt, w_glu, b_glu, loss_target, m_ln_pre_g, m_ln_post_g, m_w_mod, m_b_mod, m_w_in_ab, m_w_out_ab, m_sgu_norm_g, m_sgu_w, m_sgu_b, m_w_in_ssm, m_w_out_ssm, m_lam_re, m_lam_im, m_b_re, m_b_im, m_c_re, m_c_im, m_d_skip, m_log_dt, m_w_glu, m_b_glu, v_ln_pre_g, v_ln_post_g, v_w_mod, v_b_mod, v_w_in_ab, v_w_out_ab, v_sgu_norm_g, v_sgu_w, v_sgu_b, v_w_in_ssm, v_w_out_ssm, v_lam_re, v_lam_im, v_b_re, v_b_im, v_c_re, v_c_im, v_d_skip, v_log_dt, v_w_glu, v_b_glu):
    given = dict(locals())
    mx, my, mc = lax.axis_index("x"), lax.axis_index("y"), lax.axis_index("c")
    me = 4 * mx + 2 * my + mc
    chip = 2 * mx + my

    _, l, d = x.shape
    x2, tgt = x[0], loss_target[0]
    n_in = w_in_ab.shape[2] * 4
    wa = wb = n_in // 7
    w = w_out_ssm.shape[1]
    g, p, cch = b_re.shape[1:]
    nmod = w_mod.shape[2]

    gw_in_ab = _gather_halves(_bf(w_in_ab[0]), "gather_w_in_ab")
    win_ab = jnp.concatenate([gw_in_ab[k] for k in range(4)], axis=1)
    later_shards = [_bf(w_out_ab[0]), _bf(w_in_ssm[0]), _bf(w_out_ssm[0]), _bf(w_glu[0]), d_skip, b_glu]

    cond = _silu_rows(c.reshape(d // 128, 128), "cond_silu")
    cond_all = _allgather8(cond, "gather_cond").reshape(8, d)
    b_shard = lax.dynamic_slice(b_mod, (0, chip * nmod), (2, nmod)).reshape(2, 1, nmod)
    cond_pad = jnp.pad(cond_all, ((0, MOD_ROWS - 8), (0, 0)))
    modp = _mod_fwd(cond_pad, w_mod, b_shard, "mod_fwd")[:, :8]
    modp_all = _allgather8(modp.reshape(16, nmod), "gather_mod").reshape(4, 2, 2, 8, nmod)
    mine = lax.dynamic_index_in_dim(lax.dynamic_index_in_dim(modp_all, mc, 1, False), me, 2, False)
    mod = mine.transpose(1, 0, 2).reshape(2, 3 * d)
    shift = [mod[a:a + 1, :d] for a in range(2)]
    scale = [mod[a:a + 1, d:2 * d] for a in range(2)]
    gate = [mod[a:a + 1, 2 * d:] for a in range(2)]
    pre_g = [ln_pre_g[a:a + 1] for a in range(2)]
    post_g = [ln_post_g[a:a + 1] for a in range(2)]

    sgu_w0, sgu_bt = sgu_w[0], sgu_b[0].T
    h0 = _pre_fwd(x2, pre_g[0], scale[0], shift[0], "pre0_fwd")
    w_gates = jnp.concatenate([win_ab[:, :3 * wa], win_ab[:, 3 * wa + 3 * wb:]], axis=1)
    proj0 = _matmul(h0, w_gates, "nn", F32, "proj0", tm=1024)
    qkv = _matmul(h0, win_ab[:, 3 * wa:3 * wa + 3 * wb], "nn", BF16, "proj0_qkv", tm=1024)
    out_b, lktot, (gw_out_ab, gw_in_ssm, gw_out_ssm, gw_glu, g_dskip, g_bglu) = _attn_fwd(
        qkv, wb, "attn_fwd", hp=8, ride=[_ride_chip(later_shards, True)])
    wout_ab = gw_out_ab.reshape(wa + wb, d)
    win_ssm = gw_in_ssm.reshape(d, 2 * w)
    wout_ssm = jnp.concatenate([gw_out_ssm[k] for k in range(4)], axis=1)
    wglu = gw_glu.reshape(w, w)
    dskip_full = g_dskip.reshape(1, w)
    bglu_full = g_bglu.reshape(1, w)
    cat =_sgu_fwd(proj0, out_b, sgu_norm_g, sgu_w0, sgu_bt, wa, wb, "sgu_fwd")
    y0 = _matmul(cat, wout_ab, "nn", F32, "out0", tm=1024)
    x1, h1 = _post_pre_fwd(x2, y0, gate[0], post_g[0], pre_g[1], scale[1], shift[1], "post0_pre1_fwd")

    s = g * p
    lr_c, li_c = lam_re.reshape(s, 1), lam_im.reshape(s, 1)
    ldt_c = jnp.repeat(log_dt.reshape(g), p).reshape(s, 1)
    br_c, bi_c = b_re.reshape(s, cch), b_im.reshape(s, cch)
    bb_re, bb_im, pw_re, pw_im = _ssm_prep(lr_c, li_c, ldt_c, br_c, bi_c, lr_c.reshape(1, s), li_c.reshape(1, s),
                                           ldt_c.reshape(1, s), "ssm_prep")
    bbd = _bf(_block_diag_b(bb_re, bb_im, g, p, cch))
    ccd = _bf(_block_diag_c(c_re[0], c_im[0], g, p, cch))
    proj1 = _matmul(h1, win_ssm, "nn", F32, "proj1", tm=1024)
    y_ssm, hs_re, hs_im = _ssm_fwd(proj1, bbd, ccd, pw_re, pw_im, dskip_full, w, "ssm_fwd")
    o1 = _glu_fwd(y_ssm, proj1, wglu, bglu_full, "glu_fwd")
    y1 = _matmul(o1, wout_ssm, "nn", F32, "out1", tm=1024)
    loss_vec, dy1, dx2, dgate1, dpost1 = _post_loss(x1, y1, gate[1], post_g[1], tgt, "post1_loss")

    do1 = _matmul(dy1, wout_ssm, "nt", F32, "out1_dx", tm=1024)
    gr_wout_ssm = _matmul(o1, dy1, "tn", BF16, "out1_dw", tm=1024, tk=1024, n_split=4)
    dy_ssm, dz1, gr_wglu, gr_bglu = _glu_bwd(do1, y_ssm, proj1, wglu, bglu_full, "glu_bwd")
    du1, dbbd, dccd, da_re, da_im, gr_dskip, (ld_wout_ssm, ld_wglu) = _ssm_bwd(
        proj1, dy_ssm, hs_re, hs_im, bbd, ccd, pw_re, pw_im, dskip_full, w, "ssm_bwd",
        ride=[_ride_chip([gr_wout_ssm, gr_wglu.reshape(4, w // 4, w)], False)])
    dproj1 = jnp.concatenate([du1, dz1], axis=1)
    dh1 = _matmul(dproj1, win_ssm, "nt", F32, "proj1_dx", tm=1024)
    gr_win_ssm = _matmul(h1, dproj1, "tn", BF16, "proj1_dw", tm=1024, tn=1024, tk=1024)
    dx1, dscale1, dshift1, dpre1, dy0, dgate0, dpost0 = _pre_bwd(
        dh1, dx2, x1, pre_g[1], scale[1], "pre1_post0_bwd", post=(y0, gate[0], post_g[0]))

    dcat = _matmul(dy0, wout_ab, "nt", F32, "out0_dx", tm=1024)
    gr_wout_ab = _matmul(cat, dy0, "tn", BF16, "out0_dw", tm=1024, tn=1024, tk=1024)
    dbb_re, dbb_im = _diag_of_b(dbbd, g, p, cch)
    dc_re, dc_im = _diag_of_c(dccd, g, p, cch)
    part_a = [loss_vec[:, :1], dpre1, dpost0, dpost1, dgate0, dshift1, dscale1, dgate1, da_re, da_im,
              dbb_re, dbb_im, dc_re, dc_im, gr_dskip, gr_bglu]
    shapes_a = [a.shape for a in part_a]
    dq, dk, dv, (ld_win_ssm, ld_wout_ab, gath_a) = _attn_bwd(
        qkv, proj0, dcat, lktot, wa, wb, "attn_bwd", hp=4,
        ride=[_ride_chip([gr_win_ssm.reshape(4, d // 4, 2 * w), gr_wout_ab.reshape(4, (wa + wb) // 4, d)], False),
              _ride_all8(_pack(part_a))])
    dproj0, gr_sgu_w, gr_sgu_bt, gr_sgu_g = _sgu_bwd(proj0, out_b, dcat, dq, dk, dv, sgu_norm_g, sgu_w0, sgu_bt,
                                                     wa, wb, "sgu_bwd")
    part_b = [gr_sgu_g, gr_sgu_w, gr_sgu_bt.T]
    shapes_b = [a.shape for a in part_b]
    gr_win_ab = _matmul(h0, dproj0, "tn", BF16, "proj0_dw", tm=1024, tk=1024, tn=896, n_split=4)
    dh0, (ld_win_ab, gath_b) = _matmul(dproj0, win_ab, "nt", F32, "proj0_dx", tm=1024, tk=1792,
                                       ride=[_ride_chip([gr_win_ab], False), _ride_all8(_pack(part_b))])
    grad_x, dscale0, dshift0, dpre0 = _pre_bwd(dh0, dx1, x2, pre_g[0], scale[0], "pre0_bwd")
    part_c = [dpre0, dshift0, dscale0]
    shapes_c = [a.shape for a in part_c]
    gath_c = _allgather8(_pack(part_c), "gather_small_tail")

    landed = [ld_win_ab, ld_wout_ab, ld_win_ssm, ld_wout_ssm, ld_wglu]
    big_names = ["w_in_ab", "w_out_ab", "w_in_ssm", "w_out_ssm", "w_glu"]
    sums = [_sum_leading(a, "sum_" + nm) for a, nm in zip(landed, big_names)]
    sib = _sibling_exchange(sums, "sibling_grads")
    results = {}
    for nm, s_mine, s_sib in zip(big_names, sums, sib):
        shp = given[nm].shape
        two_d = lambda a: a.reshape(-1, shp[-1])
        outs = _adamw(two_d(given[nm]), [s_mine, s_sib], two_d(given["m_" + nm]), two_d(given["v_" + nm]),
                      "adamw_" + nm)
        results[nm] = [o.reshape(shp) for o in outs]

    (loss_s, g_pre1, g_post0, g_post1, g_gate0, g_shift1, g_scale1, g_gate1, s_da_re, s_da_im, s_dbb_re, s_dbb_im,
     g_c_re, g_c_im, g_dskip_full, g_bglu_full) = _unpack(_sum_leading(gath_a, "sum_small_a"), shapes_a)
    g_sgu_g, g_sgu_w, g_sgu_b = _unpack(_sum_leading(gath_b, "sum_small_b"), shapes_b)
    g_pre0, g_shift0, g_scale0 = _unpack(_sum_leading(gath_c, "sum_small_c"), shapes_c)
    loss = loss_s.reshape(())
    g_pre = jnp.concatenate([g_pre0, g_pre1], axis=0)
    g_post = jnp.concatenate([g_post0, g_post1], axis=0)
    g_bmod = jnp.concatenate([jnp.concatenate([g_shift0, g_scale0, g_gate0], axis=1),
                              jnp.concatenate([g_shift1, g_scale1, g_gate1], axis=1)], axis=0)

    g_lr, g_li, g_ldt, g_br, g_bi = _ssm_prep_bwd(lr_c, li_c, ldt_c, br_c, bi_c, s_da_re.reshape(s, 1),
                                                  s_da_im.reshape(s, 1), s_dbb_re, s_dbb_im, p, "ssm_prep_bwd")
    small = {
        "ln_pre_g": g_pre, "ln_post_g": g_post, "b_mod": g_bmod, "sgu_norm_g": g_sgu_g,
        "sgu_w": g_sgu_w.reshape(sgu_w.shape), "sgu_b": g_sgu_b.reshape(sgu_b.shape),
        "lam_re": g_lr.reshape(lam_re.shape), "lam_im": g_li.reshape(lam_im.shape),
        "b_re": g_br.reshape(b_re.shape), "b_im": g_bi.reshape(b_im.shape),
        "c_re": g_c_re.reshape(c_re.shape), "c_im": g_c_im.reshape(c_im.shape),
        "d_skip": lax.dynamic_slice(g_dskip_full, (0, chip * (w // 4)), (1, w // 4)),
        "log_dt": g_ldt.reshape(log_dt.shape),
        "b_glu": lax.dynamic_slice(g_bglu_full, (0, chip * (w // 4)), (1, w // 4)),
    }
    small_names = list(small)
    small_shapes = [small[nm].shape for nm in small_names]
    outs = _adamw(_pack([given[nm] for nm in small_names]), [_pack([small[nm] for nm in small_names])],
                  _pack([given["m_" + nm] for nm in small_names]), _pack([given["v_" + nm] for nm in small_names]),
                  "adamw_small")
    unpacked = [_unpack(o, small_shapes) for o in outs]
    for i, nm in enumerate(small_names):
        results[nm] = [small[nm]] + [unpacked[k][i] for k in range(1, 4)]

    rows_a = _unpack_rows(gath_a, shapes_a)
    rows_c = _unpack_rows(gath_c, shapes_c)
    dmod_rows = jnp.concatenate([rows_c[1], rows_c[2], rows_a[4], rows_a[5], rows_a[6], rows_a[7]],
                                axis=2).reshape(8, 2, 3 * d)
    dmod_shard = lax.dynamic_slice(dmod_rows, (0, 0, chip * nmod), (8, 2, nmod)).transpose(1, 0, 2)
    dmod_pad = jnp.pad(dmod_shard, ((0, 0), (0, MOD_ROWS - 8), (0, 0)))
    gr_wmod = _mod_bwd(cond_pad.T, dmod_pad, "mod_bwd")
    two_d = lambda a: a.reshape(-1, nmod)
    outs = _adamw(two_d(w_mod), [two_d(gr_wmod)], two_d(m_w_mod), two_d(v_w_mod), "adamw_w_mod")
    results["w_mod"] = [o.reshape(w_mod.shape) for o in outs]

    names = ["ln_pre_g", "ln_post_g", "w_mod", "b_mod", "w_in_ab", "w_out_ab", "sgu_norm_g", "sgu_w", "sgu_b",
             "w_in_ssm", "w_out_ssm", "lam_re", "lam_im", "b_re", "b_im", "c_re", "c_im", "d_skip", "log_dt",
             "w_glu", "b_glu"]
    return (loss, grad_x[None], *[results[nm][0] for nm in names], *[results[nm][1] for nm in names],
            *[results[nm][2] for nm in names], *[results[nm][3] for nm in names])
```

```python
import functools
import math

import jax
import jax.numpy as jnp
from jax import lax
from jax.experimental import pallas as pl
from jax.experimental.pallas import tpu as pltpu

F32 = jnp.float32
BF16 = jnp.bfloat16
MESH = pl.DeviceIdType.MESH

EPS = 1e-6
HEAD = 128
SSM_T = 512
SSM_GB = 16
ADAM_LR, ADAM_B1, ADAM_B2, ADAM_EPS, ADAM_WD, ADAM_STEP = 0.001, 0.9, 0.999, 1e-08, 0.01, 10
VMEM_LIMIT = 56 * 1024 * 1024

NN = (((1,), (0,)), ((), ()))
NT = (((1,), (1,)), ((), ()))
TN = (((0,), (0,)), ((), ()))


def _params(sem=None):
    return pltpu.CompilerParams(dimension_semantics=sem, vmem_limit_bytes=VMEM_LIMIT)


def _dot(a, b, dims=NN):
    return lax.dot_general(a, b, dims, preferred_element_type=F32)


def _bf(x):
    return x.astype(BF16)


def _gelu(x):
    k = math.sqrt(2.0 / math.pi)
    t = jnp.tanh(k * (x + 0.044715 * x * x * x))
    return 0.5 * x * (1.0 + t)


def _gelu_grad(x):
    k = math.sqrt(2.0 / math.pi)
    x2 = x * x
    t = jnp.tanh(k * (x + 0.044715 * x * x2))
    return 0.5 * (1.0 + t) + 0.5 * x * (1.0 - t * t) * k * (1.0 + 3.0 * 0.044715 * x2)


def _sigmoid(x):
    return 1.0 / (1.0 + jnp.exp(-x))


def _silu(x):
    return x * _sigmoid(x)


def _silu_grad(x):
    s = _sigmoid(x)
    return s * (1.0 + x * (1.0 - s))


def _tile(n, t, mult=128):
    if n <= t:
        return n
    for cand in range(t - t % mult, 0, -mult):
        if n % cand == 0:
            return cand
    raise ValueError((n, t, mult))


def _matmul(a, b, mode, out_dtype, name, tm=512, tn=512, tk=2048, n_split=1, ride=None, m_part=None):
    if mode == "nn":
        (m, kk), (_, n) = a.shape, b.shape
    elif mode == "nt":
        (m, kk), (n, _) = a.shape, b.shape
    else:
        (kk, m), (_, n) = a.shape, b.shape
    m_off = 0
    if m_part is not None:
        assert mode == "tn"
        m = m // m_part[1]
        m_off = m_part[0] * (m // _tile(m, tm))
    tm, tk = _tile(m, tm), _tile(kk, tk)
    ns = n // n_split
    tn = _tile(ns, tn)
    nk = kk // tk
    dims = {"nn": NN, "nt": NT, "tn": TN}[mode]

    def body(a_ref, b_ref, o_ref, acc_ref):
        k = pl.program_id(2)
        part = _dot(_bf(a_ref[...]), _bf(b_ref[...]), dims)

        @pl.when(k == 0)
        def _():
            acc_ref[...] = part

        @pl.when(k > 0)
        def _():
            acc_ref[...] += part

        @pl.when(k == nk - 1)
        def _():
            o_ref[...] = acc_ref[...].astype(out_dtype).reshape(o_ref.shape)

    if mode == "nn":
        a_spec = pl.BlockSpec((tm, tk), lambda i, j, k: (i, k))
        b_spec = pl.BlockSpec((tk, tn), lambda i, j, k: (k, j))
    elif mode == "nt":
        a_spec = pl.BlockSpec((tm, tk), lambda i, j, k: (i, k))
        b_spec = pl.BlockSpec((tn, tk), lambda i, j, k: (j, k))
    else:
        a_spec = pl.BlockSpec((tk, tm), lambda i, j, k: (k, i + m_off))
        b_spec = pl.BlockSpec((tk, tn), lambda i, j, k: (k, j))
    if n_split == 1:
        out_shape = jax.ShapeDtypeStruct((m, n), out_dtype)
        o_spec = pl.BlockSpec((tm, tn), lambda i, j, k: (i, j))
    else:
        per = ns // tn
        out_shape = jax.ShapeDtypeStruct((n_split, m, ns), out_dtype)
        o_spec = pl.BlockSpec((1, tm, tn), lambda i, j, k: (j // per, i, j % per))
    outs = _call(body, name=name, grid=(m // tm, n // tn, nk), in_specs=[a_spec, b_spec], out_specs=[o_spec],
                 out_shape=[out_shape], scratch_shapes=[pltpu.VMEM((tm, tn), F32)], args=(a, b),
                 sem=("parallel", "parallel", "arbitrary"), ride=ride)
    return outs[0] if ride is None else (outs[0], outs[1:])


def _row_spec(tm, d):
    return pl.BlockSpec((tm, d), lambda i: (i, 0))


def _vec_spec(d):
    return pl.BlockSpec((1, d), lambda i: (0, 0))


def _acc(ref, first, val):
    @pl.when(first)
    def _():
        ref[...] = val

    @pl.when(jnp.logical_not(first))
    def _():
        ref[...] += val


def _colsum(x):
    return jnp.sum(x, axis=0, keepdims=True)


def _rownorm(x):
    r = lax.rsqrt(jnp.mean(x * x, axis=-1, keepdims=True) + EPS)
    return x * r, r


def _pre_fwd(x, g, scale, shift, name):
    l, d = x.shape
    tm = _tile(l, 256)

    def body(x_ref, g_ref, sc_ref, sh_ref, h_ref):
        n, _ = _rownorm(x_ref[...])
        h_ref[...] = _bf(n * g_ref[...] * (1.0 + sc_ref[...]) + sh_ref[...])

    return pl.pallas_call(
        body, name=name, grid=(l // tm,),
        in_specs=[_row_spec(tm, d), _vec_spec(d), _vec_spec(d), _vec_spec(d)],
        out_specs=_row_spec(tm, d), out_shape=jax.ShapeDtypeStruct((l, d), BF16),
        compiler_params=_params(("parallel",)),
    )(x, g, scale, shift)


def _post_pre_fwd(x, y, gate, pg, g1, scale1, shift1, name):
    l, d = x.shape
    tm = _tile(l, 256)

    def body(x_ref, y_ref, gate_ref, pg_ref, g1_ref, sc_ref, sh_ref, x1_ref, h1_ref):
        ny, _ = _rownorm(y_ref[...])
        x1 = x_ref[...] + gate_ref[...] * (ny * pg_ref[...])
        x1_ref[...] = x1
        n1, _ = _rownorm(x1)
        h1_ref[...] = _bf(n1 * g1_ref[...] * (1.0 + sc_ref[...]) + sh_ref[...])

    v = _vec_spec(d)
    return pl.pallas_call(
        body, name=name, grid=(l // tm,),
        in_specs=[_row_spec(tm, d), _row_spec(tm, d), v, v, v, v, v],
        out_specs=[_row_spec(tm, d), _row_spec(tm, d)],
        out_shape=[jax.ShapeDtypeStruct((l, d), F32), jax.ShapeDtypeStruct((l, d), BF16)],
        compiler_params=_params(("parallel",)),
    )(x, y, gate, pg, g1, scale1, shift1)


def _post_loss(x1, y1, gate, pg, target, name):
    l, d = x1.shape
    tm = _tile(l, 256)

    def body(x_ref, y_ref, gate_ref, pg_ref, t_ref, loss_ref, dy_ref, dx_ref, dgate_ref, dpg_ref):
        first = pl.program_id(0) == 0
        y = y_ref[...]
        ny, ry = _rownorm(y)
        q = ny * pg_ref[...]
        x2 = x_ref[...] + gate_ref[...] * q
        e = x2 - t_ref[...]
        _acc(loss_ref, first, jnp.full((1, 128), 0.5 / d, F32) * jnp.sum(e * e))
        dx2 = e * (1.0 / d)
        dx_ref[...] = dx2
        _acc(dgate_ref, first, _colsum(dx2 * q))
        dq = dx2 * gate_ref[...]
        _acc(dpg_ref, first, _colsum(dq * ny))
        dny = dq * pg_ref[...]
        dy = ry * (dny - ny * jnp.mean(dny * ny, axis=-1, keepdims=True))
        dy_ref[...] = _bf(dy)

    v = _vec_spec(d)
    return pl.pallas_call(
        body, name=name, grid=(l // tm,),
        in_specs=[_row_spec(tm, d), _row_spec(tm, d), v, v, _row_spec(tm, d)],
        out_specs=[_vec_spec(128), _row_spec(tm, d), _row_spec(tm, d), v, v],
        out_shape=[jax.ShapeDtypeStruct((1, 128), F32), jax.ShapeDtypeStruct((l, d), BF16),
                   jax.ShapeDtypeStruct((l, d), F32), jax.ShapeDtypeStruct((1, d), F32),
                   jax.ShapeDtypeStruct((1, d), F32)],
        compiler_params=_params(("arbitrary",)),
    )(x1, y1, gate, pg, target)


def _pre_bwd(dh, dres, x, g, scale, name, post=None):
    l, d = x.shape
    tm = _tile(l, 256)
    with_post = post is not None

    def body(*refs):
        if with_post:
            (dh_ref, dres_ref, x_ref, g_ref, sc_ref, y_ref, gate_ref, pg_ref,
             dx_ref, dsc_ref, dsh_ref, dg_ref, dy_ref, dgate_ref, dpg_ref) = refs
        else:
            dh_ref, dres_ref, x_ref, g_ref, sc_ref, dx_ref, dsc_ref, dsh_ref, dg_ref = refs
        first = pl.program_id(0) == 0
        dh = dh_ref[...]
        n, r = _rownorm(x_ref[...])
        _acc(dsc_ref, first, _colsum(dh * (n * g_ref[...])))
        _acc(dsh_ref, first, _colsum(dh))
        dyn = dh * (1.0 + sc_ref[...])
        _acc(dg_ref, first, _colsum(dyn * n))
        dn = dyn * g_ref[...]
        dx = dres_ref[...] + r * (dn - n * jnp.mean(dn * n, axis=-1, keepdims=True))
        dx_ref[...] = dx
        if with_post:
            ny, ry = _rownorm(y_ref[...])
            _acc(dgate_ref, first, _colsum(dx * (ny * pg_ref[...])))
            dq = dx * gate_ref[...]
            _acc(dpg_ref, first, _colsum(dq * ny))
            dny = dq * pg_ref[...]
            dy_ref[...] = _bf(ry * (dny - ny * jnp.mean(dny * ny, axis=-1, keepdims=True)))

    v = _vec_spec(d)
    row = _row_spec(tm, d)
    vec_out = jax.ShapeDtypeStruct((1, d), F32)
    in_specs = [row, row, row, v, v]
    args = [dh, dres, x, g, scale]
    out_specs = [row, v, v, v]
    out_shape = [jax.ShapeDtypeStruct((l, d), F32), vec_out, vec_out, vec_out]
    if with_post:
        in_specs += [row, v, v]
        args += list(post)
        out_specs += [row, v, v]
        out_shape += [jax.ShapeDtypeStruct((l, d), BF16), vec_out, vec_out]
    return pl.pallas_call(
        body, name=name, grid=(l // tm,), in_specs=in_specs, out_specs=out_specs, out_shape=out_shape,
        compiler_params=_params(("arbitrary",)),
    )(*args)


def _softplus_parts(z):
    e = jnp.exp(-jnp.abs(z))
    den = 1.0 + e
    lb = jnp.minimum(z, 0.0) - jnp.log(den)
    sig = jnp.where(z >= 0.0, 1.0, e) * pl.reciprocal(den, approx=True)
    return lb, lb - z, sig


def _tri(cmp, n=HEAD):
    row = lax.broadcasted_iota(jnp.int32, (n, n), 0)
    col = lax.broadcasted_iota(jnp.int32, (n, n), 1)
    return cmp(row, col)


ATT_T = 256


def _attn_fwd(qkv, wb, name, hp=4, ride=None):
    l = qkv.shape[0]
    t = ATT_T
    nh, nq = wb // HEAD, l // t
    hp = min(hp, nh)
    ng, wg = nh // hp, hp * HEAD
    scale = 1.0 / math.sqrt(HEAD)

    def body(q_ref, k_ref, v_ref, o_ref, lk_ref):
        i = pl.program_id(1)
        valid = _tri(lambda r, c: c < r, t)
        m_gt = _bf(_tri(lambda r, c: r > c, t).astype(F32))

        def tile(j, carry, diag):
            rows = pl.ds(pl.multiple_of(j * t, t), t)
            cols = [slice(hh * HEAD, (hh + 1) * HEAD) for hh in range(hp)]
            zs = [_dot(q_ref[:, cs], k_ref[rows, cs], NT) * scale for cs in cols]
            lbs, lks = [], []
            for z in zs:
                lb, lk, _ = _softplus_parts(z)
                lbs.append(lb)
                lks.append(jnp.where(valid, lk, 0.0) if diag else lk)
            laters = [_dot(_bf(lk), m_gt) for lk in lks]
            ws = [jnp.exp(lb + later + run) for lb, later, (_, run) in zip(lbs, laters, carry)]
            if diag:
                ws = [jnp.where(valid, w, 0.0) for w in ws]
            return tuple((acc + _dot(_bf(w), v_ref[rows, cs]), run + jnp.sum(lk, axis=1, keepdims=True))
                         for w, lk, cs, (acc, run) in zip(ws, lks, cols, carry))

        zero = (jnp.zeros((t, HEAD), F32), jnp.zeros((t, 1), F32))
        carry = tile(i, (zero,) * hp, True)
        carry = lax.fori_loop(0, i, lambda s, c: tile(i - 1 - s, c, False), carry)
        for hh, (acc, run) in enumerate(carry):
            cs = slice(hh * HEAD, (hh + 1) * HEAD)
            o_ref[:, cs] = acc
            lk_ref[:, cs] = jnp.broadcast_to(run, (t, HEAD))

    blk = lambda off: pl.BlockSpec((t, wg), lambda h, i: (i, off + h))
    full = lambda off: pl.BlockSpec((l, wg), lambda h, i: (0, off + h))
    out = pl.BlockSpec((t, wg), lambda h, i: (i, h))
    outs = _call(body, name=name, grid=(ng, nq), in_specs=[blk(0), full(ng), full(2 * ng)], out_specs=[out, out],
                 out_shape=[jax.ShapeDtypeStruct((l, wb), F32), jax.ShapeDtypeStruct((l, wb), F32)],
                 args=(qkv, qkv, qkv), sem=("parallel", "arbitrary"), ride=ride)
    return outs[0], outs[1], outs[2:]


def _attn_bwd(qkv, proj, dcat, lktot, wa, wb, name, hp=2, ride=None):
    l = qkv.shape[0]
    t = ATT_T
    nh, nq = wb // HEAD, l // t
    hp = min(hp, nh)
    ng, wg = nh // hp, hp * HEAD
    scale = 1.0 / math.sqrt(HEAD)

    def body(q_ref, k_ref, v_ref, bz_ref, dc_ref, lt_ref, dq_ref, dkt_out, dvt_out, do_s, qt_s, dot_s,
             dkt_ref, dvt_ref, out_sems):
        i = pl.program_id(1)

        @pl.when(i == 0)
        def _():
            dkt_ref[...] = jnp.zeros_like(dkt_ref)
            dvt_ref[...] = jnp.zeros_like(dvt_ref)

        do = dc_ref[...] * _silu(bz_ref[...])
        do_s[...] = _bf(do)
        for hh in range(hp):
            cs = slice(hh * HEAD, (hh + 1) * HEAD)
            qt_s[hh] = _bf(q_ref[:, cs].astype(F32).T * scale)
            dot_s[hh] = _bf(do[:, cs].T)
        valid = _tri(lambda r, c: c < r, t)
        m_le = _bf(_tri(lambda r, c: r <= c, t).astype(F32))
        m_lt = _bf(_tri(lambda r, c: r < c, t).astype(F32))

        def tile(j, carry, diag):
            rows = pl.ds(pl.multiple_of(j * t, t), t)
            heads = range(hp)
            cols = [slice(hh * HEAD, (hh + 1) * HEAD) for hh in heads]
            zs = [_dot(q_ref[:, cs], k_ref[rows, cs], NT) * scale for cs in cols]
            dws = [_dot(do_s[:, cs], v_ref[rows, cs], NT) for cs in cols]
            lbs, lks, sigs = [], [], []
            for z in zs:
                lb, lk, sig = _softplus_parts(z)
                lbs.append(lb)
                lks.append(jnp.where(valid, lk, 0.0) if diag else lk)
                sigs.append(sig)
            pins = [_dot(_bf(lk), m_le) for lk in lks]
            ws = [jnp.exp(lbs[hh] + (lt_ref[:, hh * HEAD:hh * HEAD + 1] - carry[hh][1]) - pins[hh]) for hh in heads]
            if diag:
                ws = [jnp.where(valid, w, 0.0) for w in ws]
            das = [dw * w for dw, w in zip(dws, ws)]
            pexs = [_dot(_bf(da), m_lt) for da in das]
            dzs = [das[hh] - sigs[hh] * (das[hh] + carry[hh][2] + pexs[hh]) for hh in heads]
            if diag:
                dzs = [jnp.where(valid, dz, 0.0) for dz in dzs]
            dzs = [_bf(dz) for dz in dzs]
            out = []
            for hh in heads:
                dkt, dvt = _dot(qt_s[hh], dzs[hh]), _dot(dot_s[hh], _bf(ws[hh]))
                for half in range(t // HEAD):
                    dkt_ref[hh, sub * j + half] += dkt[:, half * HEAD:(half + 1) * HEAD]
                    dvt_ref[hh, sub * j + half] += dvt[:, half * HEAD:(half + 1) * HEAD]
                dq, cpre, ppre = carry[hh]
                out.append((dq + _dot(dzs[hh], k_ref[rows, cols[hh]]), cpre + jnp.sum(lks[hh], axis=1, keepdims=True),
                            ppre + pexs[hh][:, t - 1:] + das[hh][:, t - 1:]))
            return tuple(out)

        zero = (jnp.zeros((t, HEAD), F32), jnp.zeros((t, 1), F32), jnp.zeros((t, 1), F32))
        carry = lax.fori_loop(0, i, lambda j, c: tile(j, c, False), (zero,) * hp)
        carry = tile(i, carry, True)
        for hh in range(hp):
            dq_ref[:, hh * HEAD:(hh + 1) * HEAD] = carry[hh][0] * scale

        @pl.when(i == nq - 1)
        def _():
            heads = pl.ds(pl.program_id(0) * hp, hp)
            copies = [pltpu.make_async_copy(dkt_ref, dkt_out.at[heads], out_sems.at[0]),
                      pltpu.make_async_copy(dvt_ref, dvt_out.at[heads], out_sems.at[1])]
            for cp in copies:
                cp.start()
            for cp in copies:
                cp.wait()

    sub = t // HEAD
    blk = lambda off: pl.BlockSpec((t, wg), lambda h, i: (i, off + h))
    full = lambda off: pl.BlockSpec((l, wg), lambda h, i: (0, off + h))
    acc_shape = jax.ShapeDtypeStruct((nh, l // HEAD, HEAD, HEAD), F32)
    acc_scratch = pltpu.VMEM((hp, l // HEAD, HEAD, HEAD), F32)
    outs = _call(
        body, name=name, grid=(ng, nq),
        in_specs=[blk(0), full(ng), full(2 * ng), blk(3 * wa // wg), blk(wa // wg), blk(0)],
        out_specs=[blk(0), ANY, ANY], out_shape=[jax.ShapeDtypeStruct((l, wb), F32), acc_shape, acc_shape],
        scratch_shapes=[pltpu.VMEM((t, wg), BF16), pltpu.VMEM((hp, HEAD, t), BF16), pltpu.VMEM((hp, HEAD, t), BF16),
                        acc_scratch, acc_scratch, pltpu.SemaphoreType.DMA((2,))],
        args=(qkv, qkv, qkv, proj, dcat, lktot), sem=("parallel", "arbitrary"), ride=ride)
    return outs[0], outs[1], outs[2], outs[3:]


def _sgu_heads(v, g_ref, w_ref, bt_ref, nh):
    keep = _tri(lambda r, c: r >= c)
    out = []
    for h in range(nh):
        cols = slice(h * HEAD, (h + 1) * HEAD)
        nv, r = _rownorm(v[:, cols])
        wm = jnp.where(keep, w_ref[h], 0.0)
        s = _dot(_bf(wm), _bf(nv * g_ref[:, cols])) + bt_ref[:, h:h + 1]
        out.append((nv, r, wm, s))
    return out


def _sgu_fwd(proj, out_b, norm_g, sgu_w, sgu_bt, wa, wb, name):
    l, n = proj.shape
    nh = wa // HEAD

    def body(au_ref, av_ref, az_ref, bz_ref, ob_ref, g_ref, w_ref, bt_ref, cat_ref):
        u, v, sz = _gelu(au_ref[...]), _gelu(av_ref[...]), _silu(az_ref[...])
        for h, (_, _, _, s) in enumerate(_sgu_heads(v, g_ref, w_ref, bt_ref, nh)):
            cols = slice(h * HEAD, (h + 1) * HEAD)
            cat_ref[:, cols] = _bf(u[:, cols] * s * sz[:, cols])
        cat_ref[:, wa:] = _bf(ob_ref[...] * _silu(bz_ref[...]))

    a_blk = lambda j: pl.BlockSpec((HEAD, wa), lambda i: (i, j))
    return pl.pallas_call(
        body, name=name, grid=(l // HEAD,),
        in_specs=[a_blk(0), a_blk(1), a_blk(2), a_blk(3), pl.BlockSpec((HEAD, wb), lambda i: (i, 0)),
                  _vec_spec(wa), pl.BlockSpec((nh, HEAD, HEAD), lambda i: (0, 0, 0)),
                  pl.BlockSpec((HEAD, nh), lambda i: (0, 0))],
        out_specs=pl.BlockSpec((HEAD, wa + wb), lambda i: (i, 0)),
        out_shape=jax.ShapeDtypeStruct((l, wa + wb), BF16),
        compiler_params=_params(("parallel",)),
    )(proj, proj, proj, proj, out_b, norm_g, sgu_w, sgu_bt)


def _sgu_bwd(proj, out_b, dcat, dq, dk, dv, norm_g, sgu_w, sgu_bt, wa, wb, name):
    l = proj.shape[0]
    n = 3 * wa + 4 * wb
    nh = wa // HEAD

    def body(au_ref, av_ref, az_ref, bz_ref, ob_ref, dc_ref, dq_ref, dk_ref, dv_ref, g_ref, w_ref, bt_ref,
             dp_ref, dw_ref, dbt_ref, dg_ref):
        first = pl.program_id(0) == 0
        keep = _tri(lambda r, c: r >= c)
        au, av, az = au_ref[...], av_ref[...], az_ref[...]
        u, v, sz = _gelu(au), _gelu(av), _silu(az)
        dgelu_u, dgelu_v, dsilu_z = _gelu_grad(au), _gelu_grad(av), _silu_grad(az)
        dg_parts = []
        for h, (nv, r, wm, s) in enumerate(_sgu_heads(v, g_ref, w_ref, bt_ref, nh)):
            cols = slice(h * HEAD, (h + 1) * HEAD)
            dca, uh, szh, gh = dc_ref[:, cols], u[:, cols], sz[:, cols], g_ref[:, cols]
            dp_ref[:, cols] = _bf(dca * s * szh * dgelu_u[:, cols])
            dp_ref[:, 2 * wa + h * HEAD:2 * wa + (h + 1) * HEAD] = _bf(dca * uh * s * dsilu_z[:, cols])
            ds = dca * uh * szh
            _acc(dw_ref.at[h], first, jnp.where(keep, _dot(_bf(ds), _bf(nv * gh), NT), 0.0))
            _acc(dbt_ref.at[:, h:h + 1], first, jnp.sum(ds, axis=1, keepdims=True))
            dvh = _dot(_bf(wm.T), _bf(ds))
            dg_parts.append(_colsum(dvh * nv))
            dnv = dvh * gh
            dvv = r * (dnv - nv * jnp.mean(dnv * nv, axis=-1, keepdims=True))
            dp_ref[:, wa + h * HEAD:wa + (h + 1) * HEAD] = _bf(dvv * dgelu_v[:, cols])
        _acc(dg_ref, first, jnp.concatenate(dg_parts, axis=1))
        base = 3 * wa
        dp_ref[:, base:base + wb] = _bf(dq_ref[...])
        for h in range(wb // HEAD):
            dp_ref[:, base + wb + h * HEAD:base + wb + (h + 1) * HEAD] = _bf(dk_ref[h, 0].T)
            dp_ref[:, base + 2 * wb + h * HEAD:base + 2 * wb + (h + 1) * HEAD] = _bf(dv_ref[h, 0].T)
        dp_ref[:, base + 3 * wb:] = _bf(dc_ref[:, wa:] * ob_ref[...] * _silu_grad(bz_ref[...]))

    a_blk = lambda j: pl.BlockSpec((HEAD, wa), lambda i: (i, j))
    b_blk = pl.BlockSpec((HEAD, wb), lambda i: (i, 0))
    t_blk = pl.BlockSpec((wb // HEAD, 1, HEAD, HEAD), lambda i: (0, i, 0, 0))
    w_spec = pl.BlockSpec((nh, HEAD, HEAD), lambda i: (0, 0, 0))
    bt_spec = pl.BlockSpec((HEAD, nh), lambda i: (0, 0))
    return pl.pallas_call(
        body, name=name, grid=(l // HEAD,),
        in_specs=[a_blk(0), a_blk(1), a_blk(2), a_blk(3), b_blk, pl.BlockSpec((HEAD, wa + wb), lambda i: (i, 0)),
                  b_blk, t_blk, t_blk, _vec_spec(wa), w_spec, bt_spec],
        out_specs=[pl.BlockSpec((HEAD, n), lambda i: (i, 0)), w_spec, bt_spec, _vec_spec(wa)],
        out_shape=[jax.ShapeDtypeStruct((l, n), BF16), jax.ShapeDtypeStruct((nh, HEAD, HEAD), F32),
                   jax.ShapeDtypeStruct((HEAD, nh), F32), jax.ShapeDtypeStruct((1, wa), F32)],
        compiler_params=_params(("arbitrary",)),
    )(proj, proj, proj, proj, out_b, dcat, dq, dk, dv, norm_g, sgu_w, sgu_bt)


def _ssm_discretise(lr, li, ldt, br, bi):
    dt = jnp.exp(ldt)
    mag = jnp.exp(lr * dt)
    a_re = mag * jnp.cos(li * dt)
    a_im = mag * jnp.sin(li * dt)
    den = lr * lr + li * li
    nr = a_re - 1.0
    coef_re = (nr * lr + a_im * li) / den
    coef_im = (a_im * lr - nr * li) / den
    return a_re, a_im, coef_re * br - coef_im * bi, coef_re * bi + coef_im * br


def _ssm_prep(lr, li, ldt, br, bi, lr_row, li_row, ldt_row, name):
    s, c = br.shape

    def body(lr_ref, li_ref, ldt_ref, br_ref, bi_ref, lrr_ref, lir_ref, ldtr_ref, bbr_ref, bbi_ref, tr_ref, ti_ref):
        _, _, bbr, bbi = _ssm_discretise(lr_ref[...], li_ref[...], ldt_ref[...], br_ref[...], bi_ref[...])
        bbr_ref[...] = bbr
        bbi_ref[...] = bbi
        row = lax.broadcasted_iota(jnp.int32, (SCAN_ROWS, 1), 0)
        blk, r = jnp.right_shift(row, 3), jnp.bitwise_and(row, 7)
        kind, rev = jnp.bitwise_and(blk, 3), blk >= 4
        step = jnp.left_shift(1, kind)
        n = jnp.where(kind < 3, step, jnp.where(rev, 8 - r, r + 1)).astype(F32)
        keep = (kind == 3) | (rev & (r < 8 - step)) | (jnp.logical_not(rev) & (r >= step))
        dt = jnp.exp(ldtr_ref[...])
        mag = jnp.exp(n * (lrr_ref[...] * dt))
        ang = n * (lir_ref[...] * dt)
        tr_ref[...] = jnp.where(keep, mag * jnp.cos(ang), 0.0)
        ti_ref[...] = jnp.where(keep, jnp.where(rev, -1.0, 1.0) * mag * jnp.sin(ang), 0.0)

    col = jax.ShapeDtypeStruct((s, c), F32)
    row = jax.ShapeDtypeStruct((SCAN_ROWS, s), F32)
    return pl.pallas_call(body, name=name, out_shape=[col, col, row, row])(
        lr, li, ldt, br, bi, lr_row, li_row, ldt_row)


def _ssm_prep_bwd(lr, li, ldt, br, bi, da_re, da_im, dbb_re, dbb_im, p, name):
    s, c = br.shape

    def body(lr_ref, li_ref, ldt_ref, br_ref, bi_ref, dar_ref, dai_ref, dbr_ref, dbi_ref,
             dlr_ref, dli_ref, dldt_ref, dbre_ref, dbim_ref):
        args = (lr_ref[...], li_ref[...], ldt_ref[...], br_ref[...], bi_ref[...])
        _, vjp = jax.vjp(_ssm_discretise, *args)
        dlr, dli, dldt, dbr, dbi = vjp((dar_ref[...], dai_ref[...], dbr_ref[...], dbi_ref[...]))
        dlr_ref[...] = dlr
        dli_ref[...] = dli
        dbre_ref[...] = dbr
        dbim_ref[...] = dbi
        idx = lax.broadcasted_iota(jnp.int32, (s, s // p), 0)
        grp = lax.broadcasted_iota(jnp.int32, (s, s // p), 1)
        own = (idx >= grp * p) & (idx < (grp + 1) * p)
        dldt_ref[...] = _colsum(jnp.where(own, dldt, 0.0))

    col1 = jax.ShapeDtypeStruct((s, 1), F32)
    colc = jax.ShapeDtypeStruct((s, c), F32)
    return pl.pallas_call(
        body, name=name, out_shape=[col1, col1, jax.ShapeDtypeStruct((1, s // p), F32), colc, colc],
    )(lr, li, ldt, br, bi, da_re, da_im, dbb_re, dbb_im)


SCAN_ROWS = 64


def _scan_groups(xr, xi, tr_ref, ti_ref, cr, ci, reverse):
    ng = xr.shape[0] // 8
    base = SCAN_ROWS // 2 if reverse else 0
    pr, pi = tr_ref[base + 24:base + 32, :], ti_ref[base + 24:base + 32, :]
    edge = slice(0, 1) if reverse else slice(7, 8)
    out_r, out_i = [None] * ng, [None] * ng
    for g in (range(ng - 1, -1, -1) if reverse else range(ng)):
        sr, si = xr[8 * g:8 * g + 8, :], xi[8 * g:8 * g + 8, :]
        for k in range(3):
            ar, ai = tr_ref[base + 8 * k:base + 8 * k + 8, :], ti_ref[base + 8 * k:base + 8 * k + 8, :]
            shift = 8 - (1 << k) if reverse else 1 << k
            rr, ri = pltpu.roll(sr, shift, 0), pltpu.roll(si, shift, 0)
            sr, si = sr + ar * rr - ai * ri, si + ar * ri + ai * rr
        sr, si = sr + pr * cr - pi * ci, si + pr * ci + pi * cr
        cr, ci = sr[edge, :], si[edge, :]
        out_r[g], out_i[g] = sr, si
    return jnp.concatenate(out_r, axis=0), jnp.concatenate(out_i, axis=0), cr, ci


def _ssm_fwd(proj, bbd, ccd, pw_re, pw_im, d_skip, w, name):
    l = proj.shape[0]
    nb, cw, ns2 = bbd.shape
    ns = ns2 // 2
    nc = l // SSM_T

    def body(u_ref, bbd_ref, ccd_ref, pr_ref, pi_ref, d_ref, y_ref, hsr_ref, hsi_ref, hr_s, hi_s):
        @pl.when(pl.program_id(1) == 0)
        def _():
            hr_s[...] = jnp.zeros_like(hr_s)
            hi_s[...] = jnp.zeros_like(hi_s)

        hsr_ref[...] = hr_s[...].reshape(hsr_ref.shape)
        hsi_ref[...] = hi_s[...].reshape(hsi_ref.shape)
        u = u_ref[...]
        bu = _dot(_bf(u), bbd_ref[0])
        hr, hi, cr, ci = _scan_groups(bu[:, :ns], bu[:, ns:], pr_ref, pi_ref, hr_s[...], hi_s[...], False)
        hr_s[...] = cr
        hi_s[...] = ci
        y_ref[...] = _dot(_bf(jnp.concatenate([hr, hi], axis=1)), ccd_ref[0]) + d_ref[...] * u

    tab = pl.BlockSpec((SCAN_ROWS, ns), lambda b, k: (0, b))
    return pl.pallas_call(
        body, name=name, grid=(nb, nc),
        in_specs=[pl.BlockSpec((SSM_T, cw), lambda b, k: (k, b)),
                  pl.BlockSpec((1, cw, ns2), lambda b, k: (b, 0, 0)),
                  pl.BlockSpec((1, ns2, cw), lambda b, k: (b, 0, 0)),
                  tab, tab, pl.BlockSpec((1, cw), lambda b, k: (0, b))],
        out_specs=[pl.BlockSpec((SSM_T, cw), lambda b, k: (k, b)),
                   pl.BlockSpec((1, 1, ns), lambda b, k: (k, 0, b)), pl.BlockSpec((1, 1, ns), lambda b, k: (k, 0, b))],
        out_shape=[jax.ShapeDtypeStruct((l, w), F32), jax.ShapeDtypeStruct((nc, 1, nb * ns), F32),
                   jax.ShapeDtypeStruct((nc, 1, nb * ns), F32)],
        scratch_shapes=[pltpu.VMEM((1, ns), F32), pltpu.VMEM((1, ns), F32)],
        compiler_params=_params(("parallel", "arbitrary")),
    )(proj, bbd, ccd, pw_re, pw_im, d_skip)


def _ssm_bwd(proj, dy, hs_re, hs_im, bbd, ccd, pw_re, pw_im, d_skip, w, name, ride=None):
    l = proj.shape[0]
    nb, cw, ns2 = bbd.shape
    ns = ns2 // 2
    nc = l // SSM_T

    def body(u_ref, dy_ref, hsr_ref, hsi_ref, bbd_ref, ccd_ref, pr_ref, pi_ref, d_ref,
             du_ref, dbbd_ref, dccd_ref, dar_ref, dai_ref, dd_ref, gr_s, gi_s):
        first = pl.program_id(1) == 0

        @pl.when(first)
        def _():
            gr_s[...] = jnp.zeros_like(gr_s)
            gi_s[...] = jnp.zeros_like(gi_s)

        u, dy = u_ref[...], dy_ref[...]
        u_bf, dy_bf = _bf(u), _bf(dy)
        hr0, hi0 = hsr_ref[0], hsi_ref[0]
        bu = _dot(u_bf, bbd_ref[0])
        hr, hi, _, _ = _scan_groups(bu[:, :ns], bu[:, ns:], pr_ref, pi_ref, hr0, hi0, False)
        dh = _dot(dy_bf, ccd_ref[0], NT)
        gr, gi, gcr, gci = _scan_groups(dh[:, :ns], dh[:, ns:], pr_ref, pi_ref, gr_s[...], gi_s[...], True)
        gr_s[...] = gcr
        gi_s[...] = gci
        row0 = lax.broadcasted_iota(jnp.int32, hr.shape, 0) == 0
        pr_h = jnp.where(row0, hr0, pltpu.roll(hr, 1, 0))
        pi_h = jnp.where(row0, hi0, pltpu.roll(hi, 1, 0))
        _acc(dar_ref, first, _colsum(pr_h * gr + pi_h * gi))
        _acc(dai_ref, first, _colsum(pr_h * gi - pi_h * gr))
        g_bf = _bf(jnp.concatenate([gr, gi], axis=1))
        _acc(dbbd_ref.at[0], first, _dot(_bf(u.T), g_bf))
        _acc(dccd_ref.at[0], first, _dot(_bf(jnp.concatenate([hr, hi], axis=1).T), dy_bf))
        du_ref[...] = _bf(_dot(g_bf, bbd_ref[0], NT) + d_ref[...] * dy)
        _acc(dd_ref, first, _colsum(dy * u))

    rev = lambda b, k: (nc - 1 - k, b)
    outs = _call(
        body, name=name, grid=(nb, nc), ride=ride, sem=("parallel", "arbitrary"),
        args=(proj, dy, hs_re, hs_im, bbd, ccd, pw_re, pw_im, d_skip),
        in_specs=[pl.BlockSpec((SSM_T, cw), rev), pl.BlockSpec((SSM_T, cw), rev),
                  pl.BlockSpec((1, 1, ns), lambda b, k: (nc - 1 - k, 0, b)),
                  pl.BlockSpec((1, 1, ns), lambda b, k: (nc - 1 - k, 0, b)),
                  pl.BlockSpec((1, cw, ns2), lambda b, k: (b, 0, 0)),
                  pl.BlockSpec((1, ns2, cw), lambda b, k: (b, 0, 0)),
                  pl.BlockSpec((SCAN_ROWS, ns), lambda b, k: (0, b)), pl.BlockSpec((SCAN_ROWS, ns), lambda b, k: (0, b)),
                  pl.BlockSpec((1, cw), lambda b, k: (0, b))],
        out_specs=[pl.BlockSpec((SSM_T, cw), rev),
                   pl.BlockSpec((1, cw, ns2), lambda b, k: (b, 0, 0)),
                   pl.BlockSpec((1, ns2, cw), lambda b, k: (b, 0, 0)),
                   pl.BlockSpec((1, ns), lambda b, k: (0, b)), pl.BlockSpec((1, ns), lambda b, k: (0, b)),
                   pl.BlockSpec((1, cw), lambda b, k: (0, b))],
        out_shape=[jax.ShapeDtypeStruct((l, w), BF16), jax.ShapeDtypeStruct(bbd.shape, F32),
                   jax.ShapeDtypeStruct(ccd.shape, F32), jax.ShapeDtypeStruct((1, nb * ns), F32),
                   jax.ShapeDtypeStruct((1, nb * ns), F32), jax.ShapeDtypeStruct((1, w), F32)],
        scratch_shapes=[pltpu.VMEM((1, ns), F32), pltpu.VMEM((1, ns), F32)])
    return (*outs[:6], outs[6:])


def _block_diag_b(bb_re, bb_im, g, p, c):
    nb = g // SSM_GB
    eye = jnp.eye(SSM_GB, dtype=F32)

    def one(bb):
        t = bb.reshape(nb, SSM_GB, p, c).transpose(0, 1, 3, 2)
        return (t[:, :, :, None, :] * eye[None, :, None, :, None]).reshape(nb, SSM_GB * c, SSM_GB * p)

    return jnp.concatenate([one(bb_re), one(bb_im)], axis=2)


def _block_diag_c(c_re, c_im, g, p, c):
    nb = g // SSM_GB
    eye = jnp.eye(SSM_GB, dtype=F32)

    def one(cc):
        t = cc.reshape(nb, SSM_GB, c, p).transpose(0, 1, 3, 2)
        return (t[:, :, :, None, :] * eye[None, :, None, :, None]).reshape(nb, SSM_GB * p, SSM_GB * c)

    return jnp.concatenate([one(c_re), one(-c_im)], axis=1)


def _diag_of_b(dbbd, g, p, c):
    nb = g // SSM_GB
    t = dbbd.reshape(nb, SSM_GB, c, 2, SSM_GB, p)
    idx = jnp.arange(SSM_GB)
    d = t[:, idx, :, :, idx, :]
    d = d.transpose(1, 0, 3, 4, 2)
    return d[:, :, 0].reshape(g * p, c), d[:, :, 1].reshape(g * p, c)


def _diag_of_c(dccd, g, p, c):
    nb = g // SSM_GB
    t = dccd.reshape(nb, 2, SSM_GB, p, SSM_GB, c)
    idx = jnp.arange(SSM_GB)
    d = t[:, :, idx, :, idx, :]
    d = d.transpose(1, 0, 2, 4, 3)
    return d[:, :, 0].reshape(g, c, p), -d[:, :, 1].reshape(g, c, p)


def _glu_fwd(y, proj, w_glu, b_glu, name):
    l, w = y.shape
    tm = _tile(l, 256)

    def body(y_ref, z_ref, w_ref, b_ref, o_ref):
        g = _gelu(y_ref[...])
        t = _dot(_bf(g), w_ref[...]) + b_ref[...]
        o_ref[...] = _bf(g * _sigmoid(t) * _silu(z_ref[...]))

    return pl.pallas_call(
        body, name=name, grid=(l // tm,),
        in_specs=[_row_spec(tm, w), pl.BlockSpec((tm, w), lambda i: (i, 1)),
                  pl.BlockSpec((w, w), lambda i: (0, 0)), _vec_spec(w)],
        out_specs=_row_spec(tm, w), out_shape=jax.ShapeDtypeStruct((l, w), BF16),
        compiler_params=_params(("parallel",)),
    )(y, proj, w_glu, b_glu)


def _glu_bwd(do, y, proj, w_glu, b_glu, name):
    l, w = y.shape
    tm = _tile(l, 256)
    nsteps = l // tm

    def body(do_ref, y_ref, z_ref, w_ref, b_ref, dy_ref, dz_ref, dw_ref, db_ref, dw_acc):
        i = pl.program_id(0)
        first = i == 0
        yv, z, do = y_ref[...], z_ref[...], do_ref[...]
        g = _gelu(yv)
        g_bf = _bf(g)
        sg = _sigmoid(_dot(g_bf, w_ref[...]) + b_ref[...])
        dyy = do * _silu(z)
        dz_ref[...] = _bf(do * g * sg * _silu_grad(z))
        dt = dyy * g * sg * (1.0 - sg)
        dt_bf = _bf(dt)
        dg = dyy * sg + _dot(dt_bf, w_ref[...], NT)
        dy_ref[...] = dg * _gelu_grad(yv)
        _acc(dw_acc, first, _dot(_bf(g.T), dt_bf))
        _acc(db_ref, first, _colsum(dt))

        @pl.when(i == nsteps - 1)
        def _():
            dw_ref[...] = _bf(dw_acc[...])

    return pl.pallas_call(
        body, name=name, grid=(nsteps,),
        in_specs=[_row_spec(tm, w), _row_spec(tm, w), pl.BlockSpec((tm, w), lambda i: (i, 1)),
                  pl.BlockSpec((w, w), lambda i: (0, 0)), _vec_spec(w)],
        out_specs=[_row_spec(tm, w), _row_spec(tm, w), pl.BlockSpec((w, w), lambda i: (0, 0)), _vec_spec(w)],
        out_shape=[jax.ShapeDtypeStruct((l, w), F32), jax.ShapeDtypeStruct((l, w), BF16),
                   jax.ShapeDtypeStruct((w, w), BF16), jax.ShapeDtypeStruct((1, w), F32)],
        scratch_shapes=[pltpu.VMEM((w, w), F32)],
        compiler_params=_params(("arbitrary",)),
    )(do, y, proj, w_glu, b_glu)


MOD_ROWS = 128


def _mod_fwd(cond_pad, w_mod, b_shard, name):
    nl, d, ncol = w_mod.shape
    tn = _tile(ncol, 512)

    def body(c_ref, w_ref, b_ref, o_ref):
        o_ref[0] = _dot(_bf(c_ref[...]), _bf(w_ref[0])) + b_ref[0]

    return pl.pallas_call(
        body, name=name, grid=(nl, ncol // tn),
        in_specs=[pl.BlockSpec((MOD_ROWS, d), lambda a, j: (0, 0)),
                  pl.BlockSpec((1, d, tn), lambda a, j: (a, 0, j)),
                  pl.BlockSpec((1, 1, tn), lambda a, j: (a, 0, j))],
        out_specs=pl.BlockSpec((1, MOD_ROWS, tn), lambda a, j: (a, 0, j)),
        out_shape=jax.ShapeDtypeStruct((nl, MOD_ROWS, ncol), F32),
        compiler_params=_params(("parallel", "parallel")),
    )(cond_pad, w_mod, b_shard)


def _mod_bwd(cond_pad_t, dmod_pad, name):
    nl, _, ncol = dmod_pad.shape
    d = cond_pad_t.shape[0]
    tn = _tile(ncol, 512)

    def body(c_ref, dm_ref, o_ref):
        o_ref[0] = _dot(_bf(c_ref[...]), _bf(dm_ref[0]))

    return pl.pallas_call(
        body, name=name, grid=(nl, ncol // tn),
        in_specs=[pl.BlockSpec((d, MOD_ROWS), lambda a, j: (0, 0)),
                  pl.BlockSpec((1, MOD_ROWS, tn), lambda a, j: (a, 0, j))],
        out_specs=pl.BlockSpec((1, d, tn), lambda a, j: (a, 0, j)),
        out_shape=jax.ShapeDtypeStruct((nl, d, ncol), F32),
        compiler_params=_params(("parallel", "parallel")),
    )(cond_pad_t, dmod_pad)


def _silu_rows(c2d, name):
    def body(c_ref, o_ref):
        o_ref[...] = _silu(c_ref[...])

    return pl.pallas_call(body, name=name, out_shape=jax.ShapeDtypeStruct(c2d.shape, F32))(c2d)


def _sum_leading(x, name):
    n, r, c = x.shape
    tr = _tile(r, max(16, (1 << 20) // (4 * c)), 16 if r % 16 == 0 else 8)

    def body(x_ref, o_ref):
        acc = x_ref[0].astype(F32)
        for k in range(1, n):
            acc = acc + x_ref[k].astype(F32)
        o_ref[...] = acc

    return pl.pallas_call(
        body, name=name, grid=(r // tr,),
        in_specs=[pl.BlockSpec((n, tr, c), lambda i: (0, i, 0))], out_specs=pl.BlockSpec((tr, c), lambda i: (i, 0)),
        out_shape=jax.ShapeDtypeStruct((r, c), F32), compiler_params=_params(("parallel",)),
    )(x)


def _adamw(w, gs, m, v, name):
    r, c = w.shape
    tr = _tile(r, max(8, (1 << 19) // (4 * c)), 8)
    ng = len(gs)

    def body(*refs):
        w_ref, g_refs, m_ref, v_ref = refs[0], refs[1:1 + ng], refs[1 + ng], refs[2 + ng]
        g_ref, d_ref, nm_ref, nv_ref = refs[3 + ng:]
        g = g_refs[0][...]
        for extra in g_refs[1:]:
            g = g + extra[...]
        g_ref[...] = g
        nm = ADAM_B1 * m_ref[...] + (1.0 - ADAM_B1) * g
        nv = ADAM_B2 * v_ref[...] + (1.0 - ADAM_B2) * (g * g)
        nm_ref[...] = nm
        nv_ref[...] = nv
        m_hat = nm / (1.0 - ADAM_B1 ** ADAM_STEP)
        v_hat = nv / (1.0 - ADAM_B2 ** ADAM_STEP)
        d_ref[...] = -ADAM_LR * (m_hat / (jnp.sqrt(v_hat) + ADAM_EPS) + ADAM_WD * w_ref[...])

    spec = pl.BlockSpec((tr, c), lambda i: (i, 0))
    shp = jax.ShapeDtypeStruct((r, c), F32)
    return pl.pallas_call(
        body, name=name, grid=(r // tr,), in_specs=[spec] * (3 + ng), out_specs=[spec] * 4,
        out_shape=[shp] * 4, compiler_params=_params(("parallel",)),
    )(w, *gs, m, v)


ANY = pl.BlockSpec(memory_space=pl.ANY)


def _flip(v, bit):
    return 1 - v if bit else v


def _allgather8_ops(x_ref, o_ref, send_sems, recv_sems, local_sem):
    mx, my, mc = lax.axis_index("x"), lax.axis_index("y"), lax.axis_index("c")
    me = 4 * mx + 2 * my + mc

    def mine():
        return pltpu.make_async_copy(x_ref, o_ref.at[me], local_sem)

    def copy(j, outgoing):
        peer = (_flip(mx, j & 4), _flip(my, j & 2), _flip(mc, j & 1))
        slot = me if outgoing else 4 * peer[0] + 2 * peer[1] + peer[2]
        return pltpu.make_async_remote_copy(
            src_ref=x_ref, dst_ref=o_ref.at[slot], send_sem=send_sems.at[j - 1], recv_sem=recv_sems.at[j - 1],
            device_id=peer, device_id_type=MESH)

    def start():
        mine().start()
        for j in range(1, 8):
            copy(j, True).start()

    def wait():
        for j in range(1, 8):
            copy(j, False).wait()
        mine().wait()

    return start, wait


def _ride_all8(x):
    return dict(xs=[x], shapes=[jax.ShapeDtypeStruct((8,) + x.shape, x.dtype)],
                sems=[pltpu.SemaphoreType.DMA((7,)), pltpu.SemaphoreType.DMA((7,)), pltpu.SemaphoreType.DMA],
                ops=lambda x_refs, o_refs, sems: _allgather8_ops(x_refs[0], o_refs[0], *sems))


def _ride_chip(xs, gather):
    return dict(xs=list(xs), shapes=_chip_exchange_shapes(xs, gather), sems=_chip_exchange_sems(len(xs)),
                ops=lambda x_refs, o_refs, sems: _chip_exchange_ops(x_refs, o_refs, *sems, gather))


def _allgather8(x, name):
    def body(x_ref, o_ref, *sems):
        start, wait = _allgather8_ops(x_ref, o_ref, *sems)
        start()
        wait()

    ride = _ride_all8(x)
    return pl.pallas_call(body, name=name, in_specs=[ANY], out_specs=ANY, out_shape=ride["shapes"][0],
                          scratch_shapes=ride["sems"])(x)


def _gather_halves(x, name):
    r = x.shape[0]
    half = r // 2

    def body(x_ref, o_ref, ici_send, ici_recv, d2d_send, d2d_recv, local_sem):
        mx, my, mc = lax.axis_index("x"), lax.axis_index("y"), lax.axis_index("c")
        k0 = 2 * mx + my
        mine = pl.ds(pl.multiple_of(mc * half, 16), half)
        theirs = pl.ds(pl.multiple_of((1 - mc) * half, 16), half)
        local = pltpu.make_async_copy(x_ref, o_ref.at[k0], local_sem)
        local.start()

        def chips(j):
            px, py = _flip(mx, j & 2), _flip(my, j & 1)
            return px, py, 2 * px + py

        def over_ici(j, outgoing):
            px, py, kp = chips(j)
            dst = o_ref.at[k0, mine] if outgoing else o_ref.at[kp, mine]
            return pltpu.make_async_remote_copy(
                src_ref=x_ref.at[mine], dst_ref=dst, send_sem=ici_send.at[j - 1], recv_sem=ici_recv.at[j - 1],
                device_id=(px, py, mc), device_id_type=MESH)

        def over_d2d(j, outgoing):
            _, _, kp = chips(j)
            rows = mine if outgoing else theirs
            return pltpu.make_async_remote_copy(
                src_ref=o_ref.at[kp, rows], dst_ref=o_ref.at[kp, rows], send_sem=d2d_send.at[j - 1],
                recv_sem=d2d_recv.at[j - 1], device_id=(mx, my, 1 - mc), device_id_type=MESH)

        for j in range(1, 4):
            over_ici(j, True).start()
        for j in range(1, 4):
            over_ici(j, False).wait_recv()
            over_d2d(j, True).start()
        for j in range(1, 4):
            over_ici(j, True).wait_send()
            over_d2d(j, True).wait_send()
            over_d2d(j, False).wait_recv()
        local.wait()

    dma3 = pltpu.SemaphoreType.DMA((3,))
    return pl.pallas_call(
        body, name=name, in_specs=[ANY], out_specs=ANY, out_shape=jax.ShapeDtypeStruct((4,) + x.shape, x.dtype),
        scratch_shapes=[dma3, dma3, dma3, dma3, pltpu.SemaphoreType.DMA])(x)


def _chip_exchange(xs, gather, name):
    n = len(xs)

    def body(*refs):
        start, wait = _chip_exchange_ops(refs[:n], refs[n:2 * n], *refs[2 * n:], gather)
        start()
        wait()

    return pl.pallas_call(
        body, name=name, in_specs=[ANY] * n, out_specs=[ANY] * n, out_shape=_chip_exchange_shapes(xs, gather),
        scratch_shapes=_chip_exchange_sems(n),
    )(*xs)


def _chip_exchange_shapes(xs, gather):
    return [jax.ShapeDtypeStruct(((4,) + x.shape) if gather else x.shape, x.dtype) for x in xs]


def _chip_exchange_sems(n):
    return [pltpu.SemaphoreType.DMA((3 * n,)), pltpu.SemaphoreType.DMA((3 * n,)), pltpu.SemaphoreType.DMA((n,))]


def _chip_exchange_ops(x_refs, o_refs, send_sems, recv_sems, local_sems, gather):
    n = len(x_refs)
    mx, my, mc = lax.axis_index("x"), lax.axis_index("y"), lax.axis_index("c")
    k0 = 2 * mx + my

    def local(a):
        src = x_refs[a] if gather else x_refs[a].at[k0]
        return pltpu.make_async_copy(src, o_refs[a].at[k0], local_sems.at[a])

    def copy(a, j, outgoing):
        px, py = _flip(mx, j & 2), _flip(my, j & 1)
        kp = 2 * px + py
        if outgoing:
            src = x_refs[a] if gather else x_refs[a].at[kp]
            dst = o_refs[a].at[k0]
        else:
            src = x_refs[a] if gather else x_refs[a].at[k0]
            dst = o_refs[a].at[kp]
        s = a * 3 + j - 1
        return pltpu.make_async_remote_copy(
            src_ref=src, dst_ref=dst, send_sem=send_sems.at[s], recv_sem=recv_sems.at[s],
            device_id=(px, py, mc), device_id_type=MESH)

    def start():
        for a in range(n):
            local(a).start()
            for j in range(1, 4):
                copy(a, j, True).start()

    def wait():
        for a in range(n):
            for j in range(1, 4):
                copy(a, j, False).wait()
            local(a).wait()

    return start, wait


def _call(body, *, name, grid, in_specs, out_specs, out_shape, args, scratch_shapes=(), sem=None, ride=None):
    if not ride:
        return pl.pallas_call(
            body, name=name, grid=grid, in_specs=list(in_specs), out_specs=list(out_specs), out_shape=list(out_shape),
            scratch_shapes=list(scratch_shapes), compiler_params=_params(sem))(*args)
    xs = [x for r in ride for x in r["xs"]]
    shapes = [s for r in ride for s in r["shapes"]]
    sems = [s for r in ride for s in r["sems"]]
    n_in, n_out, n_scr, nx = len(in_specs), len(out_specs), len(scratch_shapes), len(xs)

    def wrapped(*refs):
        ins, x_refs = refs[:n_in], refs[n_in:n_in + nx]
        outs = refs[n_in + nx:n_in + nx + n_out]
        lands = refs[n_in + nx + n_out:n_in + 2 * nx + n_out]
        rest = refs[n_in + 2 * nx + n_out:]
        scr, sem_refs = rest[:n_scr], rest[n_scr:]
        ops, xo, so = [], 0, 0
        for r in ride:
            nr, ns = len(r["xs"]), len(r["sems"])
            ops.append(r["ops"](x_refs[xo:xo + nr], lands[xo:xo + nr], sem_refs[so:so + ns]))
            xo, so = xo + nr, so + ns
        ids = [pl.program_id(a) for a in range(len(grid))]
        first = functools.reduce(jnp.logical_and, [i == 0 for i in ids])
        last = functools.reduce(jnp.logical_and, [i == g - 1 for i, g in zip(ids, grid)])

        @pl.when(first)
        def _():
            for start, _ in ops:
                start()

        body(*ins, *outs, *scr)

        @pl.when(last)
        def _():
            for _, wait in ops:
                wait()

    return pl.pallas_call(
        wrapped, name=name, grid=grid, in_specs=list(in_specs) + [ANY] * nx, out_specs=list(out_specs) + [ANY] * nx,
        out_shape=list(out_shape) + shapes, scratch_shapes=list(scratch_shapes) + sems,
        compiler_params=_params(("arbitrary",) * len(grid)))(*args, *xs)


def _sibling_exchange(xs, name):
    n = len(xs)

    def body(*refs):
        x_refs, o_refs = refs[:n], refs[n:2 * n]
        send_sems, recv_sems = refs[2 * n:]
        sib = (lax.axis_index("x"), lax.axis_index("y"), 1 - lax.axis_index("c"))
        copies = [pltpu.make_async_remote_copy(
            src_ref=x_refs[a], dst_ref=o_refs[a], send_sem=send_sems.at[a], recv_sem=recv_sems.at[a],
            device_id=sib, device_id_type=MESH) for a in range(n)]
        for cp in copies:
            cp.start()
        for cp in copies:
            cp.wait()

    return pl.pallas_call(
        body, name=name, in_specs=[ANY] * n, out_specs=[ANY] * n,
        out_shape=[jax.ShapeDtypeStruct(x.shape, x.dtype) for x in xs],
        scratch_shapes=[pltpu.SemaphoreType.DMA((n,)), pltpu.SemaphoreType.DMA((n,))],
    )(*xs)


PACK = 1024
PACK_ROWS = 512


def _pack(parts):
    flat = []
    for p in parts:
        v = p.reshape(-1).astype(F32)
        flat.append(jnp.pad(v, (0, (-v.shape[0]) % PACK)))
    total = sum(v.shape[0] for v in flat)
    flat.append(jnp.zeros(((-total) % (PACK_ROWS * 128),), F32))
    return jnp.concatenate(flat).reshape(-1, 128)


def _unpack_rows(gathered, shapes):
    flat = gathered.reshape(gathered.shape[0], -1)
    out, off = [], 0
    for shp in shapes:
        n = math.prod(shp)
        out.append(flat[:, off:off + n].reshape((flat.shape[0],) + tuple(shp)))
        off += n + (-n) % PACK
    return out


def _unpack(packed, shapes):
    flat = packed.reshape(-1)
    out, off = [], 0
    for shp in shapes:
        n = math.prod(shp)
        out.append(flat[off:off + n].reshape(shp))
        off += n + (-n) % PACK
    return out


def kernel(x, c, ln_pre_g, ln_post_g, w_mod, b_mod, w_in_ab, w_out_ab, sgu_norm_g, sgu_w, sgu_b, w_in_ssm, w_out_ssm, lam_re, lam_im, b_re, b_im, c_re, c_im, d_skip, log_dt, w_glu, b_glu, loss_target, m_ln_pre_g, m_ln_post_g, m_w_mod, m_b_mod, m_w_in_ab, m_w_out_ab, m_sgu_norm_g, m_sgu_w, m_sgu_b, m_w_in_ssm, m_w_out_ssm, m_lam_re, m_lam_im, m_b_re, m_b_im, m_c_re, m_c_im, m_d_skip, m_log_dt, m_w_glu, m_b_glu, v_ln_pre_g, v_ln_post_g, v_w_mod, v_b_mod, v_w_in_ab, v_w_out_ab, v_sgu_norm_g, v_sgu_w, v_sgu_b, v_w_in_ssm, v_w_out_ssm, v_lam_re, v_lam_im, v_b_re, v_b_im, v_c_re, v_c_im, v_d_skip, v_log_dt, v_w_glu, v_b_glu):
    given = dict(locals())
    mx, my, mc = lax.axis_index("x"), lax.axis_index("y"), lax.axis_index("c")
    me = 4 * mx + 2 * my + mc
    chip = 2 * mx + my

    _, l, d = x.shape
    x2, tgt = x[0], loss_target[0]
    n_in = w_in_ab.shape[2] * 4
    wa = wb = n_in // 7
    w = w_out_ssm.shape[1]
    g, p, cch = b_re.shape[1:]
    nmod = w_mod.shape[2]

    gw_in_ab = _gather_halves(_bf(w_in_ab[0]), "gather_w_in_ab")
    win_ab = jnp.concatenate([gw_in_ab[k] for k in range(4)], axis=1)
    later_shards = [_bf(w_out_ab[0]), _bf(w_in_ssm[0]), _bf(w_out_ssm[0]), _bf(w_glu[0]), d_skip, b_glu]

    cond = _silu_rows(c.reshape(d // 128, 128), "cond_silu")
    cond_all = _allgather8(cond, "gather_cond").reshape(8, d)
    b_shard = lax.dynamic_slice(b_mod, (0, chip * nmod), (2, nmod)).reshape(2, 1, nmod)
    cond_pad = jnp.pad(cond_all, ((0, MOD_ROWS - 8), (0, 0)))
    modp = _mod_fwd(cond_pad, w_mod, b_shard, "mod_fwd")[:, :8]
    modp_all = _allgather8(modp.reshape(16, nmod), "gather_mod").reshape(4, 2, 2, 8, nmod)
    mine = lax.dynamic_index_in_dim(lax.dynamic_index_in_dim(modp_all, mc, 1, False), me, 2, False)
    mod = mine.transpose(1, 0, 2).reshape(2, 3 * d)
    shift = [mod[a:a + 1, :d] for a in range(2)]
    scale = [mod[a:a + 1, d:2 * d] for a in range(2)]
    gate = [mod[a:a + 1, 2 * d:] for a in range(2)]
    pre_g = [ln_pre_g[a:a + 1] for a in range(2)]
    post_g = [ln_post_g[a:a + 1] for a in range(2)]

    sgu_w0, sgu_bt = sgu_w[0], sgu_b[0].T
    h0 = _pre_fwd(x2, pre_g[0], scale[0], shift[0], "pre0_fwd")
    w_gates = jnp.concatenate([win_ab[:, :3 * wa], win_ab[:, 3 * wa + 3 * wb:]], axis=1)
    proj0 = _matmul(h0, w_gates, "nn", F32, "proj0", tm=1024)
    qkv = _matmul(h0, win_ab[:, 3 * wa:3 * wa + 3 * wb], "nn", BF16, "proj0_qkv", tm=1024)
    out_b, lktot, (gw_out_ab, gw_in_ssm, gw_out_ssm, gw_glu, g_dskip, g_bglu) = _attn_fwd(
        qkv, wb, "attn_fwd", hp=8, ride=[_ride_chip(later_shards, True)])
    wout_ab = gw_out_ab.reshape(wa + wb, d)
    win_ssm = gw_in_ssm.reshape(d, 2 * w)
    wout_ssm = jnp.concatenate([gw_out_ssm[k] for k in range(4)], axis=1)
    wglu = gw_glu.reshape(w, w)
    dskip_full = g_dskip.reshape(1, w)
    bglu_full = g_bglu.reshape(1, w)
    cat =_sgu_fwd(proj0, out_b, sgu_norm_g, sgu_w0, sgu_bt, wa, wb, "sgu_fwd")
    y0 = _matmul(cat, wout_ab, "nn", F32, "out0", tm=1024)
    x1, h1 = _post_pre_fwd(x2, y0, gate[0], post_g[0], pre_g[1], scale[1], shift[1], "post0_pre1_fwd")

    s = g * p
    lr_c, li_c = lam_re.reshape(s, 1), lam_im.reshape(s, 1)
    ldt_c = jnp.repeat(log_dt.reshape(g), p).reshape(s, 1)
    br_c, bi_c = b_re.reshape(s, cch), b_im.reshape(s, cch)
    bb_re, bb_im, pw_re, pw_im = _ssm_prep(lr_c, li_c, ldt_c, br_c, bi_c, lr_c.reshape(1, s), li_c.reshape(1, s),
                                           ldt_c.reshape(1, s), "ssm_prep")
    bbd = _bf(_block_diag_b(bb_re, bb_im, g, p, cch))
    ccd = _bf(_block_diag_c(c_re[0], c_im[0], g, p, cch))
    proj1 = _matmul(h1, win_ssm, "nn", F32, "proj1", tm=1024)
    y_ssm, hs_re, hs_im = _ssm_fwd(proj1, bbd, ccd, pw_re, pw_im, dskip_full, w, "ssm_fwd")
    o1 = _glu_fwd(y_ssm, proj1, wglu, bglu_full, "glu_fwd")
    y1 = _matmul(o1, wout_ssm, "nn", F32, "out1", tm=1024)
    loss_vec, dy1, dx2, dgate1, dpost1 = _post_loss(x1, y1, gate[1], post_g[1], tgt, "post1_loss")

    do1 = _matmul(dy1, wout_ssm, "nt", F32, "out1_dx", tm=1024)
    gr_wout_ssm = _matmul(o1, dy1, "tn", BF16, "out1_dw", tm=1024, tk=1024, n_split=4)
    dy_ssm, dz1, gr_wglu, gr_bglu = _glu_bwd(do1, y_ssm, proj1, wglu, bglu_full, "glu_bwd")
    du1, dbbd, dccd, da_re, da_im, gr_dskip, (ld_wout_ssm, ld_wglu) = _ssm_bwd(
        proj1, dy_ssm, hs_re, hs_im, bbd, ccd, pw_re, pw_im, dskip_full, w, "ssm_bwd",
        ride=[_ride_chip([gr_wout_ssm, gr_wglu.reshape(4, w // 4, w)], False)])
    dproj1 = jnp.concatenate([du1, dz1], axis=1)
    dh1 = _matmul(dproj1, win_ssm, "nt", F32, "proj1_dx", tm=1024)
    gr_win_ssm = _matmul(h1, dproj1, "tn", BF16, "proj1_dw", tm=1024, tn=1024, tk=1024)
    dx1, dscale1, dshift1, dpre1, dy0, dgate0, dpost0 = _pre_bwd(
        dh1, dx2, x1, pre_g[1], scale[1], "pre1_post0_bwd", post=(y0, gate[0], post_g[0]))

    dcat = _matmul(dy0, wout_ab, "nt", F32, "out0_dx", tm=1024)
    gr_wout_ab = _matmul(cat, dy0, "tn", BF16, "out0_dw", tm=1024, tn=1024, tk=1024)
    dbb_re, dbb_im = _diag_of_b(dbbd, g, p, cch)
    dc_re, dc_im = _diag_of_c(dccd, g, p, cch)
    part_a = [loss_vec[:, :1], dpre1, dpost0, dpost1, dgate0, dshift1, dscale1, dgate1, da_re, da_im,
              dbb_re, dbb_im, dc_re, dc_im, gr_dskip, gr_bglu]
    shapes_a = [a.shape for a in part_a]
    dq, dk, dv, (ld_win_ssm, ld_wout_ab, gath_a) = _attn_bwd(
        qkv, proj0, dcat, lktot, wa, wb, "attn_bwd", hp=4,
        ride=[_ride_chip([gr_win_ssm.reshape(4, d // 4, 2 * w), gr_wout_ab.reshape(4, (wa + wb) // 4, d)], False),
              _ride_all8(_pack(part_a))])
    dproj0, gr_sgu_w, gr_sgu_bt, gr_sgu_g = _sgu_bwd(proj0, out_b, dcat, dq, dk, dv, sgu_norm_g, sgu_w0, sgu_bt,
                                                     wa, wb, "sgu_bwd")
    part_b = [gr_sgu_g, gr_sgu_w, gr_sgu_bt.T]
    shapes_b = [a.shape for a in part_b]
    gr_win_ab_lo = _matmul(h0, dproj0, "tn", BF16, "proj0_dw_lo", tm=1024, tk=1024, tn=896, n_split=4, m_part=(0, 2))
    gr_win_ab_hi, (ld_win_ab_lo, gath_b) = _matmul(
        h0, dproj0, "tn", BF16, "proj0_dw_hi", tm=1024, tk=1024, tn=896, n_split=4, m_part=(1, 2),
        ride=[_ride_chip([gr_win_ab_lo], False), _ride_all8(_pack(part_b))])
    dh0, (ld_win_ab_hi,) = _matmul(dproj0, win_ab, "nt", F32, "proj0_dx", tm=1024, tk=1792,
                                   ride=[_ride_chip([gr_win_ab_hi], False)])
    grad_x, dscale0, dshift0, dpre0 = _pre_bwd(dh0, dx1, x2, pre_g[0], scale[0], "pre0_bwd")
    part_c = [dpre0, dshift0, dscale0]
    shapes_c = [a.shape for a in part_c]
    gath_c = _allgather8(_pack(part_c), "gather_small_tail")

    landed = [ld_wout_ab, ld_win_ssm, ld_wout_ssm, ld_wglu]
    big_names = ["w_in_ab", "w_out_ab", "w_in_ssm", "w_out_ssm", "w_glu"]
    sums = [jnp.concatenate([_sum_leading(ld_win_ab_lo, "sum_w_in_ab_lo"), _sum_leading(ld_win_ab_hi, "sum_w_in_ab_hi")],
                            axis=0)]
    sums += [_sum_leading(a, "sum_" + nm) for a, nm in zip(landed, big_names[1:])]
    sib = _sibling_exchange(sums, "sibling_grads")
    results = {}
    for nm, s_mine, s_sib in zip(big_names, sums, sib):
        shp = given[nm].shape
        two_d = lambda a: a.reshape(-1, shp[-1])
        outs = _adamw(two_d(given[nm]), [s_mine, s_sib], two_d(given["m_" + nm]), two_d(given["v_" + nm]),
                      "adamw_" + nm)
        results[nm] = [o.reshape(shp) for o in outs]

    (loss_s, g_pre1, g_post0, g_post1, g_gate0, g_shift1, g_scale1, g_gate1, s_da_re, s_da_im, s_dbb_re, s_dbb_im,
     g_c_re, g_c_im, g_dskip_full, g_bglu_full) = _unpack(_sum_leading(gath_a, "sum_small_a"), shapes_a)
    g_sgu_g, g_sgu_w, g_sgu_b = _unpack(_sum_leading(gath_b, "sum_small_b"), shapes_b)
    g_pre0, g_shift0, g_scale0 = _unpack(_sum_leading(gath_c, "sum_small_c"), shapes_c)
    loss = loss_s.reshape(())
    g_pre = jnp.concatenate([g_pre0, g_pre1], axis=0)
    g_post = jnp.concatenate([g_post0, g_post1], axis=0)
    g_bmod = jnp.concatenate([jnp.concatenate([g_shift0, g_scale0, g_gate0], axis=1),
                              jnp.concatenate([g_shift1, g_scale1, g_gate1], axis=1)], axis=0)

    g_lr, g_li, g_ldt, g_br, g_bi = _ssm_prep_bwd(lr_c, li_c, ldt_c, br_c, bi_c, s_da_re.reshape(s, 1),
                                                  s_da_im.reshape(s, 1), s_dbb_re, s_dbb_im, p, "ssm_prep_bwd")
    small = {
        "ln_pre_g": g_pre, "ln_post_g": g_post, "b_mod": g_bmod, "sgu_norm_g": g_sgu_g,
        "sgu_w": g_sgu_w.reshape(sgu_w.shape), "sgu_b": g_sgu_b.reshape(sgu_b.shape),
        "lam_re": g_lr.reshape(lam_re.shape), "lam_im": g_li.reshape(lam_im.shape),
        "b_re": g_br.reshape(b_re.shape), "b_im": g_bi.reshape(b_im.shape),
        "c_re": g_c_re.reshape(c_re.shape), "c_im": g_c_im.reshape(c_im.shape),
        "d_skip": lax.dynamic_slice(g_dskip_full, (0, chip * (w // 4)), (1, w // 4)),
        "log_dt": g_ldt.reshape(log_dt.shape),
        "b_glu": lax.dynamic_slice(g_bglu_full, (0, chip * (w // 4)), (1, w // 4)),
    }
    small_names = list(small)
    small_shapes = [small[nm].shape for nm in small_names]
    outs = _adamw(_pack([given[nm] for nm in small_names]), [_pack([small[nm] for nm in small_names])],
                  _pack([given["m_" + nm] for nm in small_names]), _pack([given["v_" + nm] for nm in small_names]),
                  "adamw_small")
    unpacked = [_unpack(o, small_shapes) for o in outs]
    for i, nm in enumerate(small_names):
        results[nm] = [small[nm]] + [unpacked[k][i] for k in range(1, 4)]

    rows_a = _unpack_rows(gath_a, shapes_a)
    rows_c = _unpack_rows(gath_c, shapes_c)
    dmod_rows = jnp.concatenate([rows_c[1], rows_c[2], rows_a[4], rows_a[5], rows_a[6], rows_a[7]],
                                axis=2).reshape(8, 2, 3 * d)
    dmod_shard = lax.dynamic_slice(dmod_rows, (0, 0, chip * nmod), (8, 2, nmod)).transpose(1, 0, 2)
    dmod_pad = jnp.pad(dmod_shard, ((0, 0), (0, MOD_ROWS - 8), (0, 0)))
    gr_wmod = _mod_bwd(cond_pad.T, dmod_pad, "mod_bwd")
    two_d = lambda a: a.reshape(-1, nmod)
    outs = _adamw(two_d(w_mod), [two_d(gr_wmod)], two_d(m_w_mod), two_d(v_w_mod), "adamw_w_mod")
    results["w_mod"] = [o.reshape(w_mod.shape) for o in outs]

    names = ["ln_pre_g", "ln_post_g", "w_mod", "b_mod", "w_in_ab", "w_out_ab", "sgu_norm_g", "sgu_w", "sgu_b",
             "w_in_ssm", "w_out_ssm", "lam_re", "lam_im", "b_re", "b_im", "c_re", "c_im", "d_skip", "log_dt",
             "w_glu", "b_glu"]
    return (loss, grad_x[None], *[results[nm][0] for nm in names], *[results[nm][1] for nm in names],
            *[results[nm][2] for nm in names], *[results[nm][3] for nm in names])
```

```python
import functools
import math

import jax
import jax.numpy as jnp
from jax import lax
from jax.experimental import pallas as pl
from jax.experimental.pallas import tpu as pltpu

F32 = jnp.float32
BF16 = jnp.bfloat16
MESH = pl.DeviceIdType.MESH

EPS = 1e-6
HEAD = 128
SSM_T = 512
SSM_GB = 16
ADAM_LR, ADAM_B1, ADAM_B2, ADAM_EPS, ADAM_WD, ADAM_STEP = 0.001, 0.9, 0.999, 1e-08, 0.01, 10
VMEM_LIMIT = 56 * 1024 * 1024

NN = (((1,), (0,)), ((), ()))
NT = (((1,), (1,)), ((), ()))
TN = (((0,), (0,)), ((), ()))


def _params(sem=None):
    return pltpu.CompilerParams(dimension_semantics=sem, vmem_limit_bytes=VMEM_LIMIT)


def _dot(a, b, dims=NN):
    return lax.dot_general(a, b, dims, preferred_element_type=F32)


def _bf(x):
    return x.astype(BF16)


def _gelu(x):
    k = math.sqrt(2.0 / math.pi)
    t = jnp.tanh(k * (x + 0.044715 * x * x * x))
    return 0.5 * x * (1.0 + t)


def _gelu_grad(x):
    k = math.sqrt(2.0 / math.pi)
    x2 = x * x
    t = jnp.tanh(k * (x + 0.044715 * x * x2))
    return 0.5 * (1.0 + t) + 0.5 * x * (1.0 - t * t) * k * (1.0 + 3.0 * 0.044715 * x2)


def _sigmoid(x):
    return 1.0 / (1.0 + jnp.exp(-x))


def _silu(x):
    return x * _sigmoid(x)


def _silu_grad(x):
    s = _sigmoid(x)
    return s * (1.0 + x * (1.0 - s))


def _tile(n, t, mult=128):
    if n <= t:
        return n
    for cand in range(t - t % mult, 0, -mult):
        if n % cand == 0:
            return cand
    raise ValueError((n, t, mult))


def _matmul(a, b, mode, out_dtype, name, tm=512, tn=512, tk=2048, n_split=1, ride=None, m_part=None):
    if mode == "nn":
        (m, kk), (_, n) = a.shape, b.shape
    elif mode == "nt":
        (m, kk), (n, _) = a.shape, b.shape
    else:
        (kk, m), (_, n) = a.shape, b.shape
    m_off = 0
    if m_part is not None:
        assert mode == "tn"
        first, count, parts = m_part
        tm = _tile(m // parts, tm)
        m_off = first * (m // parts) // tm
        m = count * (m // parts)
    tm, tk = _tile(m, tm), _tile(kk, tk)
    ns = n // n_split
    tn = _tile(ns, tn)
    nk = kk // tk
    dims = {"nn": NN, "nt": NT, "tn": TN}[mode]

    def body(a_ref, b_ref, o_ref, acc_ref):
        k = pl.program_id(2)
        part = _dot(_bf(a_ref[...]), _bf(b_ref[...]), dims)

        @pl.when(k == 0)
        def _():
            acc_ref[...] = part

        @pl.when(k > 0)
        def _():
            acc_ref[...] += part

        @pl.when(k == nk - 1)
        def _():
            o_ref[...] = acc_ref[...].astype(out_dtype).reshape(o_ref.shape)

    if mode == "nn":
        a_spec = pl.BlockSpec((tm, tk), lambda i, j, k: (i, k))
        b_spec = pl.BlockSpec((tk, tn), lambda i, j, k: (k, j))
    elif mode == "nt":
        a_spec = pl.BlockSpec((tm, tk), lambda i, j, k: (i, k))
        b_spec = pl.BlockSpec((tn, tk), lambda i, j, k: (j, k))
    else:
        a_spec = pl.BlockSpec((tk, tm), lambda i, j, k: (k, i + m_off))
        b_spec = pl.BlockSpec((tk, tn), lambda i, j, k: (k, j))
    if n_split == 1:
        out_shape = jax.ShapeDtypeStruct((m, n), out_dtype)
        o_spec = pl.BlockSpec((tm, tn), lambda i, j, k: (i, j))
    else:
        per = ns // tn
        out_shape = jax.ShapeDtypeStruct((n_split, m, ns), out_dtype)
        o_spec = pl.BlockSpec((1, tm, tn), lambda i, j, k: (j // per, i, j % per))
    outs = _call(body, name=name, grid=(m // tm, n // tn, nk), in_specs=[a_spec, b_spec], out_specs=[o_spec],
                 out_shape=[out_shape], scratch_shapes=[pltpu.VMEM((tm, tn), F32)], args=(a, b),
                 sem=("parallel", "parallel", "arbitrary"), ride=ride)
    return outs[0] if ride is None else (outs[0], outs[1:])


def _row_spec(tm, d):
    return pl.BlockSpec((tm, d), lambda i: (i, 0))


def _vec_spec(d):
    return pl.BlockSpec((1, d), lambda i: (0, 0))


def _acc(ref, first, val):
    @pl.when(first)
    def _():
        ref[...] = val

    @pl.when(jnp.logical_not(first))
    def _():
        ref[...] += val


def _colsum(x):
    return jnp.sum(x, axis=0, keepdims=True)


def _rownorm(x):
    r = lax.rsqrt(jnp.mean(x * x, axis=-1, keepdims=True) + EPS)
    return x * r, r


def _pre_fwd(x, g, scale, shift, name):
    l, d = x.shape
    tm = _tile(l, 256)

    def body(x_ref, g_ref, sc_ref, sh_ref, h_ref):
        n, _ = _rownorm(x_ref[...])
        h_ref[...] = _bf(n * g_ref[...] * (1.0 + sc_ref[...]) + sh_ref[...])

    return pl.pallas_call(
        body, name=name, grid=(l // tm,),
        in_specs=[_row_spec(tm, d), _vec_spec(d), _vec_spec(d), _vec_spec(d)],
        out_specs=_row_spec(tm, d), out_shape=jax.ShapeDtypeStruct((l, d), BF16),
        compiler_params=_params(("parallel",)),
    )(x, g, scale, shift)


def _post_pre_fwd(x, y, gate, pg, g1, scale1, shift1, name):
    l, d = x.shape
    tm = _tile(l, 256)

    def body(x_ref, y_ref, gate_ref, pg_ref, g1_ref, sc_ref, sh_ref, x1_ref, h1_ref):
        ny, _ = _rownorm(y_ref[...])
        x1 = x_ref[...] + gate_ref[...] * (ny * pg_ref[...])
        x1_ref[...] = x1
        n1, _ = _rownorm(x1)
        h1_ref[...] = _bf(n1 * g1_ref[...] * (1.0 + sc_ref[...]) + sh_ref[...])

    v = _vec_spec(d)
    return pl.pallas_call(
        body, name=name, grid=(l // tm,),
        in_specs=[_row_spec(tm, d), _row_spec(tm, d), v, v, v, v, v],
        out_specs=[_row_spec(tm, d), _row_spec(tm, d)],
        out_shape=[jax.ShapeDtypeStruct((l, d), F32), jax.ShapeDtypeStruct((l, d), BF16)],
        compiler_params=_params(("parallel",)),
    )(x, y, gate, pg, g1, scale1, shift1)


def _post_loss(x1, y1, gate, pg, target, name):
    l, d = x1.shape
    tm = _tile(l, 256)

    def body(x_ref, y_ref, gate_ref, pg_ref, t_ref, loss_ref, dy_ref, dx_ref, dgate_ref, dpg_ref):
        first = pl.program_id(0) == 0
        y = y_ref[...]
        ny, ry = _rownorm(y)
        q = ny * pg_ref[...]
        x2 = x_ref[...] + gate_ref[...] * q
        e = x2 - t_ref[...]
        _acc(loss_ref, first, jnp.full((1, 128), 0.5 / d, F32) * jnp.sum(e * e))
        dx2 = e * (1.0 / d)
        dx_ref[...] = dx2
        _acc(dgate_ref, first, _colsum(dx2 * q))
        dq = dx2 * gate_ref[...]
        _acc(dpg_ref, first, _colsum(dq * ny))
        dny = dq * pg_ref[...]
        dy = ry * (dny - ny * jnp.mean(dny * ny, axis=-1, keepdims=True))
        dy_ref[...] = _bf(dy)

    v = _vec_spec(d)
    return pl.pallas_call(
        body, name=name, grid=(l // tm,),
        in_specs=[_row_spec(tm, d), _row_spec(tm, d), v, v, _row_spec(tm, d)],
        out_specs=[_vec_spec(128), _row_spec(tm, d), _row_spec(tm, d), v, v],
        out_shape=[jax.ShapeDtypeStruct((1, 128), F32), jax.ShapeDtypeStruct((l, d), BF16),
                   jax.ShapeDtypeStruct((l, d), F32), jax.ShapeDtypeStruct((1, d), F32),
                   jax.ShapeDtypeStruct((1, d), F32)],
        compiler_params=_params(("arbitrary",)),
    )(x1, y1, gate, pg, target)


def _pre_bwd(dh, dres, x, g, scale, name, post=None):
    l, d = x.shape
    tm = _tile(l, 256)
    with_post = post is not None

    def body(*refs):
        if with_post:
            (dh_ref, dres_ref, x_ref, g_ref, sc_ref, y_ref, gate_ref, pg_ref,
             dx_ref, dsc_ref, dsh_ref, dg_ref, dy_ref, dgate_ref, dpg_ref) = refs
        else:
            dh_ref, dres_ref, x_ref, g_ref, sc_ref, dx_ref, dsc_ref, dsh_ref, dg_ref = refs
        first = pl.program_id(0) == 0
        dh = dh_ref[...]
        n, r = _rownorm(x_ref[...])
        _acc(dsc_ref, first, _colsum(dh * (n * g_ref[...])))
        _acc(dsh_ref, first, _colsum(dh))
        dyn = dh * (1.0 + sc_ref[...])
        _acc(dg_ref, first, _colsum(dyn * n))
        dn = dyn * g_ref[...]
        dx = dres_ref[...] + r * (dn - n * jnp.mean(dn * n, axis=-1, keepdims=True))
        dx_ref[...] = dx
        if with_post:
            ny, ry = _rownorm(y_ref[...])
            _acc(dgate_ref, first, _colsum(dx * (ny * pg_ref[...])))
            dq = dx * gate_ref[...]
            _acc(dpg_ref, first, _colsum(dq * ny))
            dny = dq * pg_ref[...]
            dy_ref[...] = _bf(ry * (dny - ny * jnp.mean(dny * ny, axis=-1, keepdims=True)))

    v = _vec_spec(d)
    row = _row_spec(tm, d)
    vec_out = jax.ShapeDtypeStruct((1, d), F32)
    in_specs = [row, row, row, v, v]
    args = [dh, dres, x, g, scale]
    out_specs = [row, v, v, v]
    out_shape = [jax.ShapeDtypeStruct((l, d), F32), vec_out, vec_out, vec_out]
    if with_post:
        in_specs += [row, v, v]
        args += list(post)
        out_specs += [row, v, v]
        out_shape += [jax.ShapeDtypeStruct((l, d), BF16), vec_out, vec_out]
    return pl.pallas_call(
        body, name=name, grid=(l // tm,), in_specs=in_specs, out_specs=out_specs, out_shape=out_shape,
        compiler_params=_params(("arbitrary",)),
    )(*args)


def _softplus_parts(z):
    e = jnp.exp(-jnp.abs(z))
    den = 1.0 + e
    lb = jnp.minimum(z, 0.0) - jnp.log(den)
    sig = jnp.where(z >= 0.0, 1.0, e) * pl.reciprocal(den, approx=True)
    return lb, lb - z, sig


def _tri(cmp, n=HEAD):
    row = lax.broadcasted_iota(jnp.int32, (n, n), 0)
    col = lax.broadcasted_iota(jnp.int32, (n, n), 1)
    return cmp(row, col)


ATT_T = 256


def _attn_fwd(qkv, wb, name, hp=4, ride=None):
    l = qkv.shape[0]
    t = ATT_T
    nh, nq = wb // HEAD, l // t
    hp = min(hp, nh)
    ng, wg = nh // hp, hp * HEAD
    scale = 1.0 / math.sqrt(HEAD)

    def body(q_ref, k_ref, v_ref, o_ref, lk_ref):
        i = pl.program_id(1)
        valid = _tri(lambda r, c: c < r, t)
        m_gt = _bf(_tri(lambda r, c: r > c, t).astype(F32))

        def tile(j, carry, diag):
            rows = pl.ds(pl.multiple_of(j * t, t), t)
            cols = [slice(hh * HEAD, (hh + 1) * HEAD) for hh in range(hp)]
            zs = [_dot(q_ref[:, cs], k_ref[rows, cs], NT) * scale for cs in cols]
            lbs, lks = [], []
            for z in zs:
                lb, lk, _ = _softplus_parts(z)
                lbs.append(lb)
                lks.append(jnp.where(valid, lk, 0.0) if diag else lk)
            laters = [_dot(_bf(lk), m_gt) for lk in lks]
            ws = [jnp.exp(lb + later + run) for lb, later, (_, run) in zip(lbs, laters, carry)]
            if diag:
                ws = [jnp.where(valid, w, 0.0) for w in ws]
            return tuple((acc + _dot(_bf(w), v_ref[rows, cs]), run + jnp.sum(lk, axis=1, keepdims=True))
                         for w, lk, cs, (acc, run) in zip(ws, lks, cols, carry))

        zero = (jnp.zeros((t, HEAD), F32), jnp.zeros((t, 1), F32))
        carry = tile(i, (zero,) * hp, True)
        carry = lax.fori_loop(0, i, lambda s, c: tile(i - 1 - s, c, False), carry)
        for hh, (acc, run) in enumerate(carry):
            cs = slice(hh * HEAD, (hh + 1) * HEAD)
            o_ref[:, cs] = acc
            lk_ref[:, cs] = jnp.broadcast_to(run, (t, HEAD))

    blk = lambda off: pl.BlockSpec((t, wg), lambda h, i: (i, off + h))
    full = lambda off: pl.BlockSpec((l, wg), lambda h, i: (0, off + h))
    out = pl.BlockSpec((t, wg), lambda h, i: (i, h))
    outs = _call(body, name=name, grid=(ng, nq), in_specs=[blk(0), full(ng), full(2 * ng)], out_specs=[out, out],
                 out_shape=[jax.ShapeDtypeStruct((l, wb), F32), jax.ShapeDtypeStruct((l, wb), F32)],
                 args=(qkv, qkv, qkv), sem=("parallel", "arbitrary"), ride=ride)
    return outs[0], outs[1], outs[2:]


def _attn_bwd(qkv, proj, dcat, lktot, wa, wb, name, hp=2, ride=None):
    l = qkv.shape[0]
    t = ATT_T
    nh, nq = wb // HEAD, l // t
    hp = min(hp, nh)
    ng, wg = nh // hp, hp * HEAD
    scale = 1.0 / math.sqrt(HEAD)

    def body(q_ref, k_ref, v_ref, bz_ref, dc_ref, lt_ref, dq_ref, dkt_out, dvt_out, do_s, qt_s, dot_s,
             dkt_ref, dvt_ref, out_sems):
        i = pl.program_id(1)

        @pl.when(i == 0)
        def _():
            dkt_ref[...] = jnp.zeros_like(dkt_ref)
            dvt_ref[...] = jnp.zeros_like(dvt_ref)

        do = dc_ref[...] * _silu(bz_ref[...])
        do_s[...] = _bf(do)
        for hh in range(hp):
            cs = slice(hh * HEAD, (hh + 1) * HEAD)
            qt_s[hh] = _bf(q_ref[:, cs].astype(F32).T * scale)
            dot_s[hh] = _bf(do[:, cs].T)
        valid = _tri(lambda r, c: c < r, t)
        m_le = _bf(_tri(lambda r, c: r <= c, t).astype(F32))
        m_lt = _bf(_tri(lambda r, c: r < c, t).astype(F32))

        def tile(j, carry, diag):
            rows = pl.ds(pl.multiple_of(j * t, t), t)
            heads = range(hp)
            cols = [slice(hh * HEAD, (hh + 1) * HEAD) for hh in heads]
            zs = [_dot(q_ref[:, cs], k_ref[rows, cs], NT) * scale for cs in cols]
            dws = [_dot(do_s[:, cs], v_ref[rows, cs], NT) for cs in cols]
            lbs, lks, sigs = [], [], []
            for z in zs:
                lb, lk, sig = _softplus_parts(z)
                lbs.append(lb)
                lks.append(jnp.where(valid, lk, 0.0) if diag else lk)
                sigs.append(sig)
            pins = [_dot(_bf(lk), m_le) for lk in lks]
            ws = [jnp.exp(lbs[hh] + (lt_ref[:, hh * HEAD:hh * HEAD + 1] - carry[hh][1]) - pins[hh]) for hh in heads]
            if diag:
                ws = [jnp.where(valid, w, 0.0) for w in ws]
            das = [dw * w for dw, w in zip(dws, ws)]
            pexs = [_dot(_bf(da), m_lt) for da in das]
            dzs = [das[hh] - sigs[hh] * (das[hh] + carry[hh][2] + pexs[hh]) for hh in heads]
            if diag:
                dzs = [jnp.where(valid, dz, 0.0) for dz in dzs]
            dzs = [_bf(dz) for dz in dzs]
            out = []
            for hh in heads:
                dkt, dvt = _dot(qt_s[hh], dzs[hh]), _dot(dot_s[hh], _bf(ws[hh]))
                for half in range(t // HEAD):
                    dkt_ref[hh, sub * j + half] += dkt[:, half * HEAD:(half + 1) * HEAD]
                    dvt_ref[hh, sub * j + half] += dvt[:, half * HEAD:(half + 1) * HEAD]
                dq, cpre, ppre = carry[hh]
                out.append((dq + _dot(dzs[hh], k_ref[rows, cols[hh]]), cpre + jnp.sum(lks[hh], axis=1, keepdims=True),
                            ppre + pexs[hh][:, t - 1:] + das[hh][:, t - 1:]))
            return tuple(out)

        zero = (jnp.zeros((t, HEAD), F32), jnp.zeros((t, 1), F32), jnp.zeros((t, 1), F32))
        carry = lax.fori_loop(0, i, lambda j, c: tile(j, c, False), (zero,) * hp)
        carry = tile(i, carry, True)
        for hh in range(hp):
            dq_ref[:, hh * HEAD:(hh + 1) * HEAD] = carry[hh][0] * scale

        @pl.when(i == nq - 1)
        def _():
            heads = pl.ds(pl.program_id(0) * hp, hp)
            copies = [pltpu.make_async_copy(dkt_ref, dkt_out.at[heads], out_sems.at[0]),
                      pltpu.make_async_copy(dvt_ref, dvt_out.at[heads], out_sems.at[1])]
            for cp in copies:
                cp.start()
            for cp in copies:
                cp.wait()

    sub = t // HEAD
    blk = lambda off: pl.BlockSpec((t, wg), lambda h, i: (i, off + h))
    full = lambda off: pl.BlockSpec((l, wg), lambda h, i: (0, off + h))
    acc_shape = jax.ShapeDtypeStruct((nh, l // HEAD, HEAD, HEAD), F32)
    acc_scratch = pltpu.VMEM((hp, l // HEAD, HEAD, HEAD), F32)
    outs = _call(
        body, name=name, grid=(ng, nq),
        in_specs=[blk(0), full(ng), full(2 * ng), blk(3 * wa // wg), blk(wa // wg), blk(0)],
        out_specs=[blk(0), ANY, ANY], out_shape=[jax.ShapeDtypeStruct((l, wb), F32), acc_shape, acc_shape],
        scratch_shapes=[pltpu.VMEM((t, wg), BF16), pltpu.VMEM((hp, HEAD, t), BF16), pltpu.VMEM((hp, HEAD, t), BF16),
                        acc_scratch, acc_scratch, pltpu.SemaphoreType.DMA((2,))],
        args=(qkv, qkv, qkv, proj, dcat, lktot), sem=("parallel", "arbitrary"), ride=ride)
    return outs[0], outs[1], outs[2], outs[3:]


def _sgu_heads(v, g_ref, w_ref, bt_ref, nh):
    keep = _tri(lambda r, c: r >= c)
    out = []
    for h in range(nh):
        cols = slice(h * HEAD, (h + 1) * HEAD)
        nv, r = _rownorm(v[:, cols])
        wm = jnp.where(keep, w_ref[h], 0.0)
        s = _dot(_bf(wm), _bf(nv * g_ref[:, cols])) + bt_ref[:, h:h + 1]
        out.append((nv, r, wm, s))
    return out


def _sgu_fwd(proj, out_b, norm_g, sgu_w, sgu_bt, wa, wb, name):
    l, n = proj.shape
    nh = wa // HEAD

    def body(au_ref, av_ref, az_ref, bz_ref, ob_ref, g_ref, w_ref, bt_ref, cat_ref):
        u, v, sz = _gelu(au_ref[...]), _gelu(av_ref[...]), _silu(az_ref[...])
        for h, (_, _, _, s) in enumerate(_sgu_heads(v, g_ref, w_ref, bt_ref, nh)):
            cols = slice(h * HEAD, (h + 1) * HEAD)
            cat_ref[:, cols] = _bf(u[:, cols] * s * sz[:, cols])
        cat_ref[:, wa:] = _bf(ob_ref[...] * _silu(bz_ref[...]))

    a_blk = lambda j: pl.BlockSpec((HEAD, wa), lambda i: (i, j))
    return pl.pallas_call(
        body, name=name, grid=(l // HEAD,),
        in_specs=[a_blk(0), a_blk(1), a_blk(2), a_blk(3), pl.BlockSpec((HEAD, wb), lambda i: (i, 0)),
                  _vec_spec(wa), pl.BlockSpec((nh, HEAD, HEAD), lambda i: (0, 0, 0)),
                  pl.BlockSpec((HEAD, nh), lambda i: (0, 0))],
        out_specs=pl.BlockSpec((HEAD, wa + wb), lambda i: (i, 0)),
        out_shape=jax.ShapeDtypeStruct((l, wa + wb), BF16),
        compiler_params=_params(("parallel",)),
    )(proj, proj, proj, proj, out_b, norm_g, sgu_w, sgu_bt)


def _sgu_bwd(proj, out_b, dcat, dq, dk, dv, norm_g, sgu_w, sgu_bt, wa, wb, name):
    l = proj.shape[0]
    n = 3 * wa + 4 * wb
    nh = wa // HEAD

    def body(au_ref, av_ref, az_ref, bz_ref, ob_ref, dc_ref, dq_ref, dk_ref, dv_ref, g_ref, w_ref, bt_ref,
             dp_ref, dw_ref, dbt_ref, dg_ref):
        first = pl.program_id(0) == 0
        keep = _tri(lambda r, c: r >= c)
        au, av, az = au_ref[...], av_ref[...], az_ref[...]
        u, v, sz = _gelu(au), _gelu(av), _silu(az)
        dgelu_u, dgelu_v, dsilu_z = _gelu_grad(au), _gelu_grad(av), _silu_grad(az)
        dg_parts = []
        for h, (nv, r, wm, s) in enumerate(_sgu_heads(v, g_ref, w_ref, bt_ref, nh)):
            cols = slice(h * HEAD, (h + 1) * HEAD)
            dca, uh, szh, gh = dc_ref[:, cols], u[:, cols], sz[:, cols], g_ref[:, cols]
            dp_ref[:, cols] = _bf(dca * s * szh * dgelu_u[:, cols])
            dp_ref[:, 2 * wa + h * HEAD:2 * wa + (h + 1) * HEAD] = _bf(dca * uh * s * dsilu_z[:, cols])
            ds = dca * uh * szh
            _acc(dw_ref.at[h], first, jnp.where(keep, _dot(_bf(ds), _bf(nv * gh), NT), 0.0))
            _acc(dbt_ref.at[:, h:h + 1], first, jnp.sum(ds, axis=1, keepdims=True))
            dvh = _dot(_bf(wm.T), _bf(ds))
            dg_parts.append(_colsum(dvh * nv))
            dnv = dvh * gh
            dvv = r * (dnv - nv * jnp.mean(dnv * nv, axis=-1, keepdims=True))
            dp_ref[:, wa + h * HEAD:wa + (h + 1) * HEAD] = _bf(dvv * dgelu_v[:, cols])
        _acc(dg_ref, first, jnp.concatenate(dg_parts, axis=1))
        base = 3 * wa
        dp_ref[:, base:base + wb] = _bf(dq_ref[...])
        for h in range(wb // HEAD):
            dp_ref[:, base + wb + h * HEAD:base + wb + (h + 1) * HEAD] = _bf(dk_ref[h, 0].T)
            dp_ref[:, base + 2 * wb + h * HEAD:base + 2 * wb + (h + 1) * HEAD] = _bf(dv_ref[h, 0].T)
        dp_ref[:, base + 3 * wb:] = _bf(dc_ref[:, wa:] * ob_ref[...] * _silu_grad(bz_ref[...]))

    a_blk = lambda j: pl.BlockSpec((HEAD, wa), lambda i: (i, j))
    b_blk = pl.BlockSpec((HEAD, wb), lambda i: (i, 0))
    t_blk = pl.BlockSpec((wb // HEAD, 1, HEAD, HEAD), lambda i: (0, i, 0, 0))
    w_spec = pl.BlockSpec((nh, HEAD, HEAD), lambda i: (0, 0, 0))
    bt_spec = pl.BlockSpec((HEAD, nh), lambda i: (0, 0))
    return pl.pallas_call(
        body, name=name, grid=(l // HEAD,),
        in_specs=[a_blk(0), a_blk(1), a_blk(2), a_blk(3), b_blk, pl.BlockSpec((HEAD, wa + wb), lambda i: (i, 0)),
                  b_blk, t_blk, t_blk, _vec_spec(wa), w_spec, bt_spec],
        out_specs=[pl.BlockSpec((HEAD, n), lambda i: (i, 0)), w_spec, bt_spec, _vec_spec(wa)],
        out_shape=[jax.ShapeDtypeStruct((l, n), BF16), jax.ShapeDtypeStruct((nh, HEAD, HEAD), F32),
                   jax.ShapeDtypeStruct((HEAD, nh), F32), jax.ShapeDtypeStruct((1, wa), F32)],
        compiler_params=_params(("arbitrary",)),
    )(proj, proj, proj, proj, out_b, dcat, dq, dk, dv, norm_g, sgu_w, sgu_bt)


def _ssm_discretise(lr, li, ldt, br, bi):
    dt = jnp.exp(ldt)
    mag = jnp.exp(lr * dt)
    a_re = mag * jnp.cos(li * dt)
    a_im = mag * jnp.sin(li * dt)
    den = lr * lr + li * li
    nr = a_re - 1.0
    coef_re = (nr * lr + a_im * li) / den
    coef_im = (a_im * lr - nr * li) / den
    return a_re, a_im, coef_re * br - coef_im * bi, coef_re * bi + coef_im * br


def _ssm_prep(lr, li, ldt, br, bi, lr_row, li_row, ldt_row, name):
    s, c = br.shape

    def body(lr_ref, li_ref, ldt_ref, br_ref, bi_ref, lrr_ref, lir_ref, ldtr_ref, bbr_ref, bbi_ref, tr_ref, ti_ref):
        _, _, bbr, bbi = _ssm_discretise(lr_ref[...], li_ref[...], ldt_ref[...], br_ref[...], bi_ref[...])
        bbr_ref[...] = bbr
        bbi_ref[...] = bbi
        row = lax.broadcasted_iota(jnp.int32, (SCAN_ROWS, 1), 0)
        blk, r = jnp.right_shift(row, 3), jnp.bitwise_and(row, 7)
        kind, rev = jnp.bitwise_and(blk, 3), blk >= 4
        step = jnp.left_shift(1, kind)
        n = jnp.where(kind < 3, step, jnp.where(rev, 8 - r, r + 1)).astype(F32)
        keep = (kind == 3) | (rev & (r < 8 - step)) | (jnp.logical_not(rev) & (r >= step))
        dt = jnp.exp(ldtr_ref[...])
        mag = jnp.exp(n * (lrr_ref[...] * dt))
        ang = n * (lir_ref[...] * dt)
        tr_ref[...] = jnp.where(keep, mag * jnp.cos(ang), 0.0)
        ti_ref[...] = jnp.where(keep, jnp.where(rev, -1.0, 1.0) * mag * jnp.sin(ang), 0.0)

    col = jax.ShapeDtypeStruct((s, c), F32)
    row = jax.ShapeDtypeStruct((SCAN_ROWS, s), F32)
    return pl.pallas_call(body, name=name, out_shape=[col, col, row, row])(
        lr, li, ldt, br, bi, lr_row, li_row, ldt_row)


def _ssm_prep_bwd(lr, li, ldt, br, bi, da_re, da_im, dbb_re, dbb_im, p, name):
    s, c = br.shape

    def body(lr_ref, li_ref, ldt_ref, br_ref, bi_ref, dar_ref, dai_ref, dbr_ref, dbi_ref,
             dlr_ref, dli_ref, dldt_ref, dbre_ref, dbim_ref):
        args = (lr_ref[...], li_ref[...], ldt_ref[...], br_ref[...], bi_ref[...])
        _, vjp = jax.vjp(_ssm_discretise, *args)
        dlr, dli, dldt, dbr, dbi = vjp((dar_ref[...], dai_ref[...], dbr_ref[...], dbi_ref[...]))
        dlr_ref[...] = dlr
        dli_ref[...] = dli
        dbre_ref[...] = dbr
        dbim_ref[...] = dbi
        idx = lax.broadcasted_iota(jnp.int32, (s, s // p), 0)
        grp = lax.broadcasted_iota(jnp.int32, (s, s // p), 1)
        own = (idx >= grp * p) & (idx < (grp + 1) * p)
        dldt_ref[...] = _colsum(jnp.where(own, dldt, 0.0))

    col1 = jax.ShapeDtypeStruct((s, 1), F32)
    colc = jax.ShapeDtypeStruct((s, c), F32)
    return pl.pallas_call(
        body, name=name, out_shape=[col1, col1, jax.ShapeDtypeStruct((1, s // p), F32), colc, colc],
    )(lr, li, ldt, br, bi, da_re, da_im, dbb_re, dbb_im)


SCAN_ROWS = 64


def _scan_groups(xr, xi, tr_ref, ti_ref, cr, ci, reverse):
    ng = xr.shape[0] // 8
    base = SCAN_ROWS // 2 if reverse else 0
    pr, pi = tr_ref[base + 24:base + 32, :], ti_ref[base + 24:base + 32, :]
    edge = slice(0, 1) if reverse else slice(7, 8)
    out_r, out_i = [None] * ng, [None] * ng
    for g in (range(ng - 1, -1, -1) if reverse else range(ng)):
        sr, si = xr[8 * g:8 * g + 8, :], xi[8 * g:8 * g + 8, :]
        for k in range(3):
            ar, ai = tr_ref[base + 8 * k:base + 8 * k + 8, :], ti_ref[base + 8 * k:base + 8 * k + 8, :]
            shift = 8 - (1 << k) if reverse else 1 << k
            rr, ri = pltpu.roll(sr, shift, 0), pltpu.roll(si, shift, 0)
            sr, si = sr + ar * rr - ai * ri, si + ar * ri + ai * rr
        sr, si = sr + pr * cr - pi * ci, si + pr * ci + pi * cr
        cr, ci = sr[edge, :], si[edge, :]
        out_r[g], out_i[g] = sr, si
    return jnp.concatenate(out_r, axis=0), jnp.concatenate(out_i, axis=0), cr, ci


def _ssm_fwd(proj, bbd, ccd, pw_re, pw_im, d_skip, w, name):
    l = proj.shape[0]
    nb, cw, ns2 = bbd.shape
    ns = ns2 // 2
    nc = l // SSM_T

    def body(u_ref, bbd_ref, ccd_ref, pr_ref, pi_ref, d_ref, y_ref, hsr_ref, hsi_ref, hr_s, hi_s):
        @pl.when(pl.program_id(1) == 0)
        def _():
            hr_s[...] = jnp.zeros_like(hr_s)
            hi_s[...] = jnp.zeros_like(hi_s)

        hsr_ref[...] = hr_s[...].reshape(hsr_ref.shape)
        hsi_ref[...] = hi_s[...].reshape(hsi_ref.shape)
        u = u_ref[...]
        bu = _dot(_bf(u), bbd_ref[0])
        hr, hi, cr, ci = _scan_groups(bu[:, :ns], bu[:, ns:], pr_ref, pi_ref, hr_s[...], hi_s[...], False)
        hr_s[...] = cr
        hi_s[...] = ci
        y_ref[...] = _dot(_bf(jnp.concatenate([hr, hi], axis=1)), ccd_ref[0]) + d_ref[...] * u

    tab = pl.BlockSpec((SCAN_ROWS, ns), lambda b, k: (0, b))
    return pl.pallas_call(
        body, name=name, grid=(nb, nc),
        in_specs=[pl.BlockSpec((SSM_T, cw), lambda b, k: (k, b)),
                  pl.BlockSpec((1, cw, ns2), lambda b, k: (b, 0, 0)),
                  pl.BlockSpec((1, ns2, cw), lambda b, k: (b, 0, 0)),
                  tab, tab, pl.BlockSpec((1, cw), lambda b, k: (0, b))],
        out_specs=[pl.BlockSpec((SSM_T, cw), lambda b, k: (k, b)),
                   pl.BlockSpec((1, 1, ns), lambda b, k: (k, 0, b)), pl.BlockSpec((1, 1, ns), lambda b, k: (k, 0, b))],
        out_shape=[jax.ShapeDtypeStruct((l, w), F32), jax.ShapeDtypeStruct((nc, 1, nb * ns), F32),
                   jax.ShapeDtypeStruct((nc, 1, nb * ns), F32)],
        scratch_shapes=[pltpu.VMEM((1, ns), F32), pltpu.VMEM((1, ns), F32)],
        compiler_params=_params(("parallel", "arbitrary")),
    )(proj, bbd, ccd, pw_re, pw_im, d_skip)


def _ssm_bwd(proj, dy, hs_re, hs_im, bbd, ccd, pw_re, pw_im, d_skip, w, name, ride=None):
    l = proj.shape[0]
    nb, cw, ns2 = bbd.shape
    ns = ns2 // 2
    nc = l // SSM_T

    def body(u_ref, dy_ref, hsr_ref, hsi_ref, bbd_ref, ccd_ref, pr_ref, pi_ref, d_ref,
             du_ref, dbbd_ref, dccd_ref, dar_ref, dai_ref, dd_ref, gr_s, gi_s):
        first = pl.program_id(1) == 0

        @pl.when(first)
        def _():
            gr_s[...] = jnp.zeros_like(gr_s)
            gi_s[...] = jnp.zeros_like(gi_s)

        u, dy = u_ref[...], dy_ref[...]
        u_bf, dy_bf = _bf(u), _bf(dy)
        hr0, hi0 = hsr_ref[0], hsi_ref[0]
        bu = _dot(u_bf, bbd_ref[0])
        hr, hi, _, _ = _scan_groups(bu[:, :ns], bu[:, ns:], pr_ref, pi_ref, hr0, hi0, False)
        dh = _dot(dy_bf, ccd_ref[0], NT)
        gr, gi, gcr, gci = _scan_groups(dh[:, :ns], dh[:, ns:], pr_ref, pi_ref, gr_s[...], gi_s[...], True)
        gr_s[...] = gcr
        gi_s[...] = gci
        row0 = lax.broadcasted_iota(jnp.int32, hr.shape, 0) == 0
        pr_h = jnp.where(row0, hr0, pltpu.roll(hr, 1, 0))
        pi_h = jnp.where(row0, hi0, pltpu.roll(hi, 1, 0))
        _acc(dar_ref, first, _colsum(pr_h * gr + pi_h * gi))
        _acc(dai_ref, first, _colsum(pr_h * gi - pi_h * gr))
        g_bf = _bf(jnp.concatenate([gr, gi], axis=1))
        _acc(dbbd_ref.at[0], first, _dot(_bf(u.T), g_bf))
        _acc(dccd_ref.at[0], first, _dot(_bf(jnp.concatenate([hr, hi], axis=1).T), dy_bf))
        du_ref[...] = _bf(_dot(g_bf, bbd_ref[0], NT) + d_ref[...] * dy)
        _acc(dd_ref, first, _colsum(dy * u))

    rev = lambda b, k: (nc - 1 - k, b)
    outs = _call(
        body, name=name, grid=(nb, nc), ride=ride, sem=("parallel", "arbitrary"),
        args=(proj, dy, hs_re, hs_im, bbd, ccd, pw_re, pw_im, d_skip),
        in_specs=[pl.BlockSpec((SSM_T, cw), rev), pl.BlockSpec((SSM_T, cw), rev),
                  pl.BlockSpec((1, 1, ns), lambda b, k: (nc - 1 - k, 0, b)),
                  pl.BlockSpec((1, 1, ns), lambda b, k: (nc - 1 - k, 0, b)),
                  pl.BlockSpec((1, cw, ns2), lambda b, k: (b, 0, 0)),
                  pl.BlockSpec((1, ns2, cw), lambda b, k: (b, 0, 0)),
                  pl.BlockSpec((SCAN_ROWS, ns), lambda b, k: (0, b)), pl.BlockSpec((SCAN_ROWS, ns), lambda b, k: (0, b)),
                  pl.BlockSpec((1, cw), lambda b, k: (0, b))],
        out_specs=[pl.BlockSpec((SSM_T, cw), rev),
                   pl.BlockSpec((1, cw, ns2), lambda b, k: (b, 0, 0)),
                   pl.BlockSpec((1, ns2, cw), lambda b, k: (b, 0, 0)),
                   pl.BlockSpec((1, ns), lambda b, k: (0, b)), pl.BlockSpec((1, ns), lambda b, k: (0, b)),
                   pl.BlockSpec((1, cw), lambda b, k: (0, b))],
        out_shape=[jax.ShapeDtypeStruct((l, w), BF16), jax.ShapeDtypeStruct(bbd.shape, F32),
                   jax.ShapeDtypeStruct(ccd.shape, F32), jax.ShapeDtypeStruct((1, nb * ns), F32),
                   jax.ShapeDtypeStruct((1, nb * ns), F32), jax.ShapeDtypeStruct((1, w), F32)],
        scratch_shapes=[pltpu.VMEM((1, ns), F32), pltpu.VMEM((1, ns), F32)])
    return (*outs[:6], outs[6:])


def _block_diag_b(bb_re, bb_im, g, p, c):
    nb = g // SSM_GB
    keep = _same_group(SSM_GB * c, c, SSM_GB * p, p)

    def one(bb):
        t = bb.reshape(nb, SSM_GB, p, c).transpose(0, 1, 3, 2).reshape(nb, SSM_GB * c, p)
        return jnp.where(keep, jnp.tile(t, (1, 1, SSM_GB)), 0.0)

    return jnp.concatenate([one(bb_re), one(bb_im)], axis=2)


def _same_group(rows, per_row, cols, per_col):
    r = lax.broadcasted_iota(jnp.int32, (rows, cols), 0) // per_row
    q = lax.broadcasted_iota(jnp.int32, (rows, cols), 1) // per_col
    return r == q


def _block_diag_c(c_re, c_im, g, p, c):
    nb = g // SSM_GB
    keep = _same_group(SSM_GB * p, p, SSM_GB * c, c)

    def one(cc):
        t = cc.reshape(nb, SSM_GB, c, p).transpose(0, 1, 3, 2).reshape(nb, SSM_GB * p, c)
        return jnp.where(keep, jnp.tile(t, (1, 1, SSM_GB)), 0.0)

    return jnp.concatenate([one(c_re), one(-c_im)], axis=1)


def _diag_of_b(dbbd, g, p, c):
    nb = g // SSM_GB
    keep = _same_group(SSM_GB * c, c, SSM_GB * p, p)

    def one(blk):
        d = jnp.where(keep, blk, 0.0).reshape(nb, SSM_GB * c, SSM_GB, p).sum(axis=2)
        return d.reshape(nb, SSM_GB, c, p).transpose(0, 1, 3, 2).reshape(g * p, c)

    half = SSM_GB * p
    return one(dbbd[:, :, :half]), one(dbbd[:, :, half:])


def _diag_of_c(dccd, g, p, c):
    nb = g // SSM_GB
    keep = _same_group(SSM_GB * p, p, SSM_GB * c, c)

    def one(blk):
        d = jnp.where(keep, blk, 0.0).reshape(nb, SSM_GB * p, SSM_GB, c).sum(axis=2)
        return d.reshape(nb, SSM_GB, p, c).transpose(0, 1, 3, 2).reshape(g, c, p)

    half = SSM_GB * p
    return one(dccd[:, :half]), -one(dccd[:, half:])


def _glu_fwd(y, proj, w_glu, b_glu, name):
    l, w = y.shape
    tm = _tile(l, 256)

    def body(y_ref, z_ref, w_ref, b_ref, o_ref):
        g = _gelu(y_ref[...])
        t = _dot(_bf(g), w_ref[...]) + b_ref[...]
        o_ref[...] = _bf(g * _sigmoid(t) * _silu(z_ref[...]))

    return pl.pallas_call(
        body, name=name, grid=(l // tm,),
        in_specs=[_row_spec(tm, w), pl.BlockSpec((tm, w), lambda i: (i, 1)),
                  pl.BlockSpec((w, w), lambda i: (0, 0)), _vec_spec(w)],
        out_specs=_row_spec(tm, w), out_shape=jax.ShapeDtypeStruct((l, w), BF16),
        compiler_params=_params(("parallel",)),
    )(y, proj, w_glu, b_glu)


def _glu_bwd(do, y, proj, w_glu, b_glu, name):
    l, w = y.shape
    tm = _tile(l, 256)
    nsteps = l // tm

    def body(do_ref, y_ref, z_ref, w_ref, b_ref, dy_ref, dz_ref, dw_ref, db_ref, dw_acc):
        i = pl.program_id(0)
        first = i == 0
        yv, z, do = y_ref[...], z_ref[...], do_ref[...]
        g = _gelu(yv)
        g_bf = _bf(g)
        sg = _sigmoid(_dot(g_bf, w_ref[...]) + b_ref[...])
        dyy = do * _silu(z)
        dz_ref[...] = _bf(do * g * sg * _silu_grad(z))
        dt = dyy * g * sg * (1.0 - sg)
        dt_bf = _bf(dt)
        dg = dyy * sg + _dot(dt_bf, w_ref[...], NT)
        dy_ref[...] = dg * _gelu_grad(yv)
        _acc(dw_acc, first, _dot(_bf(g.T), dt_bf))
        _acc(db_ref, first, _colsum(dt))

        @pl.when(i == nsteps - 1)
        def _():
            dw_ref[...] = _bf(dw_acc[...])

    return pl.pallas_call(
        body, name=name, grid=(nsteps,),
        in_specs=[_row_spec(tm, w), _row_spec(tm, w), pl.BlockSpec((tm, w), lambda i: (i, 1)),
                  pl.BlockSpec((w, w), lambda i: (0, 0)), _vec_spec(w)],
        out_specs=[_row_spec(tm, w), _row_spec(tm, w), pl.BlockSpec((w, w), lambda i: (0, 0)), _vec_spec(w)],
        out_shape=[jax.ShapeDtypeStruct((l, w), F32), jax.ShapeDtypeStruct((l, w), BF16),
                   jax.ShapeDtypeStruct((w, w), BF16), jax.ShapeDtypeStruct((1, w), F32)],
        scratch_shapes=[pltpu.VMEM((w, w), F32)],
        compiler_params=_params(("arbitrary",)),
    )(do, y, proj, w_glu, b_glu)


MOD_ROWS = 128


def _mod_fwd(cond_pad, w_mod, b_shard, name):
    nl, d, ncol = w_mod.shape
    tn = _tile(ncol, 512)

    def body(c_ref, w_ref, b_ref, o_ref):
        o_ref[0] = _dot(_bf(c_ref[...]), _bf(w_ref[0])) + b_ref[0]

    return pl.pallas_call(
        body, name=name, grid=(nl, ncol // tn),
        in_specs=[pl.BlockSpec((MOD_ROWS, d), lambda a, j: (0, 0)),
                  pl.BlockSpec((1, d, tn), lambda a, j: (a, 0, j)),
                  pl.BlockSpec((1, 1, tn), lambda a, j: (a, 0, j))],
        out_specs=pl.BlockSpec((1, MOD_ROWS, tn), lambda a, j: (a, 0, j)),
        out_shape=jax.ShapeDtypeStruct((nl, MOD_ROWS, ncol), F32),
        compiler_params=_params(("parallel", "parallel")),
    )(cond_pad, w_mod, b_shard)


def _mod_bwd(cond_pad_t, dmod_pad, name):
    nl, _, ncol = dmod_pad.shape
    d = cond_pad_t.shape[0]
    tn = _tile(ncol, 512)

    def body(c_ref, dm_ref, o_ref):
        o_ref[0] = _dot(_bf(c_ref[...]), _bf(dm_ref[0]))

    return pl.pallas_call(
        body, name=name, grid=(nl, ncol // tn),
        in_specs=[pl.BlockSpec((d, MOD_ROWS), lambda a, j: (0, 0)),
                  pl.BlockSpec((1, MOD_ROWS, tn), lambda a, j: (a, 0, j))],
        out_specs=pl.BlockSpec((1, d, tn), lambda a, j: (a, 0, j)),
        out_shape=jax.ShapeDtypeStruct((nl, d, ncol), F32),
        compiler_params=_params(("parallel", "parallel")),
    )(cond_pad_t, dmod_pad)


def _silu_rows(c2d, name):
    def body(c_ref, o_ref):
        o_ref[...] = _silu(c_ref[...])

    return pl.pallas_call(body, name=name, out_shape=jax.ShapeDtypeStruct(c2d.shape, F32))(c2d)


def _sum_leading(x, name):
    n, r, c = x.shape
    tr = _tile(r, max(16, (1 << 20) // (4 * c)), 16 if r % 16 == 0 else 8)

    def body(x_ref, o_ref):
        acc = x_ref[0].astype(F32)
        for k in range(1, n):
            acc = acc + x_ref[k].astype(F32)
        o_ref[...] = acc

    return pl.pallas_call(
        body, name=name, grid=(r // tr,),
        in_specs=[pl.BlockSpec((n, tr, c), lambda i: (0, i, 0))], out_specs=pl.BlockSpec((tr, c), lambda i: (i, 0)),
        out_shape=jax.ShapeDtypeStruct((r, c), F32), compiler_params=_params(("parallel",)),
    )(x)


def _adamw(w, gs, m, v, name):
    r, c = w.shape
    tr = _tile(r, max(8, (1 << 19) // (4 * c)), 8)
    ng = len(gs)

    def body(*refs):
        w_ref, g_refs, m_ref, v_ref = refs[0], refs[1:1 + ng], refs[1 + ng], refs[2 + ng]
        g_ref, d_ref, nm_ref, nv_ref = refs[3 + ng:]
        g = g_refs[0][...]
        for extra in g_refs[1:]:
            g = g + extra[...]
        g_ref[...] = g
        d_ref[...], nm_ref[...], nv_ref[...] = _adamw_math(w_ref[...], g, m_ref[...], v_ref[...])

    spec = pl.BlockSpec((tr, c), lambda i: (i, 0))
    shp = jax.ShapeDtypeStruct((r, c), F32)
    return pl.pallas_call(
        body, name=name, grid=(r // tr,), in_specs=[spec] * (3 + ng), out_specs=[spec] * 4,
        out_shape=[shp] * 4, compiler_params=_params(("parallel",)),
    )(w, *gs, m, v)


def _adamw_math(w, g, m, v):
    nm = ADAM_B1 * m + (1.0 - ADAM_B1) * g
    nv = ADAM_B2 * v + (1.0 - ADAM_B2) * (g * g)
    m_hat = nm / (1.0 - ADAM_B1 ** ADAM_STEP)
    v_hat = nv / (1.0 - ADAM_B2 ** ADAM_STEP)
    return -ADAM_LR * (m_hat / (jnp.sqrt(v_hat) + ADAM_EPS) + ADAM_WD * w), nm, nv


def _adamw_many(ws, gs, ms, vs, name):
    n = len(ws)

    def body(*refs):
        w_refs, g_refs, m_refs, v_refs = (refs[k * n:(k + 1) * n] for k in range(4))
        outs = refs[4 * n:]
        for i in range(n):
            outs[3 * i][...], outs[3 * i + 1][...], outs[3 * i + 2][...] = _adamw_math(
                w_refs[i][...], g_refs[i][...], m_refs[i][...], v_refs[i][...])

    out_shape = [jax.ShapeDtypeStruct(w.shape, F32) for w in ws for _ in range(3)]
    outs = pl.pallas_call(body, name=name, out_shape=out_shape, compiler_params=_params())(*ws, *gs, *ms, *vs)
    return [tuple(outs[3 * i:3 * i + 3]) for i in range(n)]


ANY = pl.BlockSpec(memory_space=pl.ANY)


def _flip(v, bit):
    return 1 - v if bit else v


def _allgather8_ops(x_ref, o_ref, send_sems, recv_sems, local_sem):
    mx, my, mc = lax.axis_index("x"), lax.axis_index("y"), lax.axis_index("c")
    me = 4 * mx + 2 * my + mc

    def mine():
        return pltpu.make_async_copy(x_ref, o_ref.at[me], local_sem)

    def copy(j, outgoing):
        peer = (_flip(mx, j & 4), _flip(my, j & 2), _flip(mc, j & 1))
        slot = me if outgoing else 4 * peer[0] + 2 * peer[1] + peer[2]
        return pltpu.make_async_remote_copy(
            src_ref=x_ref, dst_ref=o_ref.at[slot], send_sem=send_sems.at[j - 1], recv_sem=recv_sems.at[j - 1],
            device_id=peer, device_id_type=MESH)

    def start():
        mine().start()
        for j in range(1, 8):
            copy(j, True).start()

    def wait():
        for j in range(1, 8):
            copy(j, False).wait()
        mine().wait()

    return start, wait


def _ride_all8(x):
    return dict(xs=[x], shapes=[jax.ShapeDtypeStruct((8,) + x.shape, x.dtype)],
                sems=[pltpu.SemaphoreType.DMA((7,)), pltpu.SemaphoreType.DMA((7,)), pltpu.SemaphoreType.DMA],
                ops=lambda x_refs, o_refs, sems: _allgather8_ops(x_refs[0], o_refs[0], *sems))


def _ride_chip(xs, gather):
    return dict(xs=list(xs), shapes=_chip_exchange_shapes(xs, gather), sems=_chip_exchange_sems(len(xs)),
                ops=lambda x_refs, o_refs, sems: _chip_exchange_ops(x_refs, o_refs, *sems, gather))


def _allgather8(x, name):
    def body(x_ref, o_ref, *sems):
        start, wait = _allgather8_ops(x_ref, o_ref, *sems)
        start()
        wait()

    ride = _ride_all8(x)
    return pl.pallas_call(body, name=name, in_specs=[ANY], out_specs=ANY, out_shape=ride["shapes"][0],
                          scratch_shapes=ride["sems"])(x)


def _gather_halves(x, name):
    r = x.shape[0]
    half = r // 2

    def body(x_ref, o_ref, ici_send, ici_recv, d2d_send, d2d_recv, local_sem):
        mx, my, mc = lax.axis_index("x"), lax.axis_index("y"), lax.axis_index("c")
        k0 = 2 * mx + my
        mine = pl.ds(pl.multiple_of(mc * half, 16), half)
        theirs = pl.ds(pl.multiple_of((1 - mc) * half, 16), half)
        local = pltpu.make_async_copy(x_ref, o_ref.at[k0], local_sem)
        local.start()

        def chips(j):
            px, py = _flip(mx, j & 2), _flip(my, j & 1)
            return px, py, 2 * px + py

        def over_ici(j, outgoing):
            px, py, kp = chips(j)
            dst = o_ref.at[k0, mine] if outgoing else o_ref.at[kp, mine]
            return pltpu.make_async_remote_copy(
                src_ref=x_ref.at[mine], dst_ref=dst, send_sem=ici_send.at[j - 1], recv_sem=ici_recv.at[j - 1],
                device_id=(px, py, mc), device_id_type=MESH)

        def over_d2d(j, outgoing):
            _, _, kp = chips(j)
            rows = mine if outgoing else theirs
            return pltpu.make_async_remote_copy(
                src_ref=o_ref.at[kp, rows], dst_ref=o_ref.at[kp, rows], send_sem=d2d_send.at[j - 1],
                recv_sem=d2d_recv.at[j - 1], device_id=(mx, my, 1 - mc), device_id_type=MESH)

        for j in range(1, 4):
            over_ici(j, True).start()
        for j in range(1, 4):
            over_ici(j, False).wait_recv()
            over_d2d(j, True).start()
        for j in range(1, 4):
            over_ici(j, True).wait_send()
            over_d2d(j, True).wait_send()
            over_d2d(j, False).wait_recv()
        local.wait()

    dma3 = pltpu.SemaphoreType.DMA((3,))
    return pl.pallas_call(
        body, name=name, in_specs=[ANY], out_specs=ANY, out_shape=jax.ShapeDtypeStruct((4,) + x.shape, x.dtype),
        scratch_shapes=[dma3, dma3, dma3, dma3, pltpu.SemaphoreType.DMA])(x)


def _chip_exchange(xs, gather, name):
    n = len(xs)

    def body(*refs):
        start, wait = _chip_exchange_ops(refs[:n], refs[n:2 * n], *refs[2 * n:], gather)
        start()
        wait()

    return pl.pallas_call(
        body, name=name, in_specs=[ANY] * n, out_specs=[ANY] * n, out_shape=_chip_exchange_shapes(xs, gather),
        scratch_shapes=_chip_exchange_sems(n),
    )(*xs)


def _chip_exchange_shapes(xs, gather):
    return [jax.ShapeDtypeStruct(((4,) + x.shape) if gather else x.shape, x.dtype) for x in xs]


def _chip_exchange_sems(n):
    return [pltpu.SemaphoreType.DMA((3 * n,)), pltpu.SemaphoreType.DMA((3 * n,)), pltpu.SemaphoreType.DMA((n,))]


def _chip_exchange_ops(x_refs, o_refs, send_sems, recv_sems, local_sems, gather):
    n = len(x_refs)
    mx, my, mc = lax.axis_index("x"), lax.axis_index("y"), lax.axis_index("c")
    k0 = 2 * mx + my

    def local(a):
        src = x_refs[a] if gather else x_refs[a].at[k0]
        return pltpu.make_async_copy(src, o_refs[a].at[k0], local_sems.at[a])

    def copy(a, j, outgoing):
        px, py = _flip(mx, j & 2), _flip(my, j & 1)
        kp = 2 * px + py
        if outgoing:
            src = x_refs[a] if gather else x_refs[a].at[kp]
            dst = o_refs[a].at[k0]
        else:
            src = x_refs[a] if gather else x_refs[a].at[k0]
            dst = o_refs[a].at[kp]
        s = a * 3 + j - 1
        return pltpu.make_async_remote_copy(
            src_ref=src, dst_ref=dst, send_sem=send_sems.at[s], recv_sem=recv_sems.at[s],
            device_id=(px, py, mc), device_id_type=MESH)

    def start():
        for a in range(n):
            local(a).start()
            for j in range(1, 4):
                copy(a, j, True).start()

    def wait():
        for a in range(n):
            for j in range(1, 4):
                copy(a, j, False).wait()
            local(a).wait()

    return start, wait


def _call(body, *, name, grid, in_specs, out_specs, out_shape, args, scratch_shapes=(), sem=None, ride=None):
    if not ride:
        return pl.pallas_call(
            body, name=name, grid=grid, in_specs=list(in_specs), out_specs=list(out_specs), out_shape=list(out_shape),
            scratch_shapes=list(scratch_shapes), compiler_params=_params(sem))(*args)
    xs = [x for r in ride for x in r["xs"]]
    shapes = [s for r in ride for s in r["shapes"]]
    sems = [s for r in ride for s in r["sems"]]
    n_in, n_out, n_scr, nx = len(in_specs), len(out_specs), len(scratch_shapes), len(xs)

    def wrapped(*refs):
        ins, x_refs = refs[:n_in], refs[n_in:n_in + nx]
        outs = refs[n_in + nx:n_in + nx + n_out]
        lands = refs[n_in + nx + n_out:n_in + 2 * nx + n_out]
        rest = refs[n_in + 2 * nx + n_out:]
        scr, sem_refs = rest[:n_scr], rest[n_scr:]
        ops, xo, so = [], 0, 0
        for r in ride:
            nr, ns = len(r["xs"]), len(r["sems"])
            ops.append(r["ops"](x_refs[xo:xo + nr], lands[xo:xo + nr], sem_refs[so:so + ns]))
            xo, so = xo + nr, so + ns
        ids = [pl.program_id(a) for a in range(len(grid))]
        first = functools.reduce(jnp.logical_and, [i == 0 for i in ids])
        last = functools.reduce(jnp.logical_and, [i == g - 1 for i, g in zip(ids, grid)])

        @pl.when(first)
        def _():
            for start, _ in ops:
                start()

        body(*ins, *outs, *scr)

        @pl.when(last)
        def _():
            for _, wait in ops:
                wait()

    return pl.pallas_call(
        wrapped, name=name, grid=grid, in_specs=list(in_specs) + [ANY] * nx, out_specs=list(out_specs) + [ANY] * nx,
        out_shape=list(out_shape) + shapes, scratch_shapes=list(scratch_shapes) + sems,
        compiler_params=_params(("arbitrary",) * len(grid)))(*args, *xs)


def _sibling_exchange(xs, name):
    n = len(xs)

    def body(*refs):
        x_refs, o_refs = refs[:n], refs[n:2 * n]
        send_sems, recv_sems = refs[2 * n:]
        sib = (lax.axis_index("x"), lax.axis_index("y"), 1 - lax.axis_index("c"))
        copies = [pltpu.make_async_remote_copy(
            src_ref=x_refs[a], dst_ref=o_refs[a], send_sem=send_sems.at[a], recv_sem=recv_sems.at[a],
            device_id=sib, device_id_type=MESH) for a in range(n)]
        for cp in copies:
            cp.start()
        for cp in copies:
            cp.wait()

    return pl.pallas_call(
        body, name=name, in_specs=[ANY] * n, out_specs=[ANY] * n,
        out_shape=[jax.ShapeDtypeStruct(x.shape, x.dtype) for x in xs],
        scratch_shapes=[pltpu.SemaphoreType.DMA((n,)), pltpu.SemaphoreType.DMA((n,))],
    )(*xs)


PACK = 1024
PACK_ROWS = 512


def _pack(parts):
    flat = []
    for p in parts:
        v = p.reshape(-1).astype(F32)
        flat.append(jnp.pad(v, (0, (-v.shape[0]) % PACK)))
    total = sum(v.shape[0] for v in flat)
    flat.append(jnp.zeros(((-total) % (PACK_ROWS * 128),), F32))
    return jnp.concatenate(flat).reshape(-1, 128)


def _unpack_rows(gathered, shapes):
    flat = gathered.reshape(gathered.shape[0], -1)
    out, off = [], 0
    for shp in shapes:
        n = math.prod(shp)
        out.append(flat[:, off:off + n].reshape((flat.shape[0],) + tuple(shp)))
        off += n + (-n) % PACK
    return out


def _unpack(packed, shapes):
    flat = packed.reshape(-1)
    out, off = [], 0
    for shp in shapes:
        n = math.prod(shp)
        out.append(flat[off:off + n].reshape(shp))
        off += n + (-n) % PACK
    return out


def kernel(x, c, ln_pre_g, ln_post_g, w_mod, b_mod, w_in_ab, w_out_ab, sgu_norm_g, sgu_w, sgu_b, w_in_ssm, w_out_ssm, lam_re, lam_im, b_re, b_im, c_re, c_im, d_skip, log_dt, w_glu, b_glu, loss_target, m_ln_pre_g, m_ln_post_g, m_w_mod, m_b_mod, m_w_in_ab, m_w_out_ab, m_sgu_norm_g, m_sgu_w, m_sgu_b, m_w_in_ssm, m_w_out_ssm, m_lam_re, m_lam_im, m_b_re, m_b_im, m_c_re, m_c_im, m_d_skip, m_log_dt, m_w_glu, m_b_glu, v_ln_pre_g, v_ln_post_g, v_w_mod, v_b_mod, v_w_in_ab, v_w_out_ab, v_sgu_norm_g, v_sgu_w, v_sgu_b, v_w_in_ssm, v_w_out_ssm, v_lam_re, v_lam_im, v_b_re, v_b_im, v_c_re, v_c_im, v_d_skip, v_log_dt, v_w_glu, v_b_glu):
    given = dict(locals())
    mx, my, mc = lax.axis_index("x"), lax.axis_index("y"), lax.axis_index("c")
    me = 4 * mx + 2 * my + mc
    chip = 2 * mx + my

    _, l, d = x.shape
    x2, tgt = x[0], loss_target[0]
    n_in = w_in_ab.shape[2] * 4
    wa = wb = n_in // 7
    w = w_out_ssm.shape[1]
    g, p, cch = b_re.shape[1:]
    nmod = w_mod.shape[2]

    gw_in_ab = _gather_halves(_bf(w_in_ab[0]), "gather_w_in_ab")
    win_ab = jnp.concatenate([gw_in_ab[k] for k in range(4)], axis=1)
    later_shards = [_bf(w_out_ab[0]), _bf(w_in_ssm[0]), _bf(w_out_ssm[0]), _bf(w_glu[0]), d_skip, b_glu]

    cond = _silu_rows(c.reshape(d // 128, 128), "cond_silu")
    cond_all = _allgather8(cond, "gather_cond").reshape(8, d)
    b_shard = lax.dynamic_slice(b_mod, (0, chip * nmod), (2, nmod)).reshape(2, 1, nmod)
    cond_pad = jnp.pad(cond_all, ((0, MOD_ROWS - 8), (0, 0)))
    modp = _mod_fwd(cond_pad, w_mod, b_shard, "mod_fwd")[:, :8]
    modp_all = _allgather8(modp.reshape(16, nmod), "gather_mod").reshape(4, 2, 2, 8, nmod)
    mine = lax.dynamic_index_in_dim(lax.dynamic_index_in_dim(modp_all, mc, 1, False), me, 2, False)
    mod = mine.transpose(1, 0, 2).reshape(2, 3 * d)
    shift = [mod[a:a + 1, :d] for a in range(2)]
    scale = [mod[a:a + 1, d:2 * d] for a in range(2)]
    gate = [mod[a:a + 1, 2 * d:] for a in range(2)]
    pre_g = [ln_pre_g[a:a + 1] for a in range(2)]
    post_g = [ln_post_g[a:a + 1] for a in range(2)]

    sgu_w0, sgu_bt = sgu_w[0], sgu_b[0].T
    h0 = _pre_fwd(x2, pre_g[0], scale[0], shift[0], "pre0_fwd")
    w_gates = jnp.concatenate([win_ab[:, :3 * wa], win_ab[:, 3 * wa + 3 * wb:]], axis=1)
    proj0 = _matmul(h0, w_gates, "nn", F32, "proj0", tm=1024)
    qkv = _matmul(h0, win_ab[:, 3 * wa:3 * wa + 3 * wb], "nn", BF16, "proj0_qkv", tm=1024)
    out_b, lktot, (gw_out_ab, gw_in_ssm, gw_out_ssm, gw_glu, g_dskip, g_bglu) = _attn_fwd(
        qkv, wb, "attn_fwd", hp=8, ride=[_ride_chip(later_shards, True)])
    wout_ab = gw_out_ab.reshape(wa + wb, d)
    win_ssm = gw_in_ssm.reshape(d, 2 * w)
    wout_ssm = jnp.concatenate([gw_out_ssm[k] for k in range(4)], axis=1)
    wglu = gw_glu.reshape(w, w)
    dskip_full = g_dskip.reshape(1, w)
    bglu_full = g_bglu.reshape(1, w)
    cat =_sgu_fwd(proj0, out_b, sgu_norm_g, sgu_w0, sgu_bt, wa, wb, "sgu_fwd")
    y0 = _matmul(cat, wout_ab, "nn", F32, "out0", tm=1024)
    x1, h1 = _post_pre_fwd(x2, y0, gate[0], post_g[0], pre_g[1], scale[1], shift[1], "post0_pre1_fwd")

    s = g * p
    lr_c, li_c = lam_re.reshape(s, 1), lam_im.reshape(s, 1)
    ldt_c = jnp.repeat(log_dt.reshape(g), p).reshape(s, 1)
    br_c, bi_c = b_re.reshape(s, cch), b_im.reshape(s, cch)
    bb_re, bb_im, pw_re, pw_im = _ssm_prep(lr_c, li_c, ldt_c, br_c, bi_c, lr_c.reshape(1, s), li_c.reshape(1, s),
                                           ldt_c.reshape(1, s), "ssm_prep")
    bbd = _bf(_block_diag_b(bb_re, bb_im, g, p, cch))
    ccd = _bf(_block_diag_c(c_re[0], c_im[0], g, p, cch))
    proj1 = _matmul(h1, win_ssm, "nn", F32, "proj1", tm=1024)
    y_ssm, hs_re, hs_im = _ssm_fwd(proj1, bbd, ccd, pw_re, pw_im, dskip_full, w, "ssm_fwd")
    o1 = _glu_fwd(y_ssm, proj1, wglu, bglu_full, "glu_fwd")
    y1 = _matmul(o1, wout_ssm, "nn", F32, "out1", tm=1024)
    loss_vec, dy1, dx2, dgate1, dpost1 = _post_loss(x1, y1, gate[1], post_g[1], tgt, "post1_loss")

    do1 = _matmul(dy1, wout_ssm, "nt", F32, "out1_dx", tm=1024)
    gr_wout_ssm = _matmul(o1, dy1, "tn", BF16, "out1_dw", tm=1024, tk=1024, n_split=4)
    dy_ssm, dz1, gr_wglu, gr_bglu = _glu_bwd(do1, y_ssm, proj1, wglu, bglu_full, "glu_bwd")
    du1, dbbd, dccd, da_re, da_im, gr_dskip, (ld_wout_ssm, ld_wglu) = _ssm_bwd(
        proj1, dy_ssm, hs_re, hs_im, bbd, ccd, pw_re, pw_im, dskip_full, w, "ssm_bwd",
        ride=[_ride_chip([gr_wout_ssm, gr_wglu.reshape(4, w // 4, w)], False)])
    dproj1 = jnp.concatenate([du1, dz1], axis=1)
    dh1 = _matmul(dproj1, win_ssm, "nt", F32, "proj1_dx", tm=1024)
    gr_win_ssm = _matmul(h1, dproj1, "tn", BF16, "proj1_dw", tm=1024, tn=1024, tk=1024)
    dx1, dscale1, dshift1, dpre1, dy0, dgate0, dpost0 = _pre_bwd(
        dh1, dx2, x1, pre_g[1], scale[1], "pre1_post0_bwd", post=(y0, gate[0], post_g[0]))

    dcat = _matmul(dy0, wout_ab, "nt", F32, "out0_dx", tm=1024)
    gr_wout_ab = _matmul(cat, dy0, "tn", BF16, "out0_dw", tm=1024, tn=1024, tk=1024)
    dbb_re, dbb_im = _diag_of_b(dbbd, g, p, cch)
    dc_re, dc_im = _diag_of_c(dccd, g, p, cch)
    part_a = [loss_vec[:, :1], dpre1, dpost0, dpost1, dgate0, dshift1, dscale1, dgate1, da_re, da_im,
              dbb_re, dbb_im, dc_re, dc_im, gr_dskip, gr_bglu]
    shapes_a = [a.shape for a in part_a]
    dq, dk, dv, (ld_win_ssm, ld_wout_ab, gath_a) = _attn_bwd(
        qkv, proj0, dcat, lktot, wa, wb, "attn_bwd", hp=4,
        ride=[_ride_chip([gr_win_ssm.reshape(4, d // 4, 2 * w), gr_wout_ab.reshape(4, (wa + wb) // 4, d)], False),
              _ride_all8(_pack(part_a))])
    dproj0, gr_sgu_w, gr_sgu_bt, gr_sgu_g = _sgu_bwd(proj0, out_b, dcat, dq, dk, dv, sgu_norm_g, sgu_w0, sgu_bt,
                                                     wa, wb, "sgu_bwd")
    part_b = [gr_sgu_g, gr_sgu_w, gr_sgu_bt.T]
    shapes_b = [a.shape for a in part_b]
    gr_win_ab_lo, (gath_b,) = _matmul(h0, dproj0, "tn", BF16, "proj0_dw_lo", tm=1024, tk=1024, tn=896, n_split=4,
                                      m_part=(0, 1, 4), ride=[_ride_all8(_pack(part_b))])
    gr_win_ab_hi, (ld_win_ab_lo,) = _matmul(
        h0, dproj0, "tn", BF16, "proj0_dw_hi", tm=1024, tk=1024, tn=896, n_split=4, m_part=(1, 3, 4),
        ride=[_ride_chip([gr_win_ab_lo], False)])
    dh0, (ld_win_ab_hi,) = _matmul(dproj0, win_ab, "nt", F32, "proj0_dx", tm=1024, tk=1792,
                                   ride=[_ride_chip([gr_win_ab_hi], False)])
    grad_x, dscale0, dshift0, dpre0 = _pre_bwd(dh0, dx1, x2, pre_g[0], scale[0], "pre0_bwd")
    part_c = [dpre0, dshift0, dscale0]
    shapes_c = [a.shape for a in part_c]
    gath_c = _allgather8(_pack(part_c), "gather_small_tail")

    landed = [ld_wout_ab, ld_win_ssm, ld_wout_ssm, ld_wglu]
    big_names = ["w_in_ab", "w_out_ab", "w_in_ssm", "w_out_ssm", "w_glu"]
    sums = [jnp.concatenate([_sum_leading(ld_win_ab_lo, "sum_w_in_ab_lo"), _sum_leading(ld_win_ab_hi, "sum_w_in_ab_hi")],
                            axis=0)]
    sums += [_sum_leading(a, "sum_" + nm) for a, nm in zip(landed, big_names[1:])]
    sib = _sibling_exchange(sums, "sibling_grads")
    results = {}
    for nm, s_mine, s_sib in zip(big_names, sums, sib):
        shp = given[nm].shape
        two_d = lambda a: a.reshape(-1, shp[-1])
        outs = _adamw(two_d(given[nm]), [s_mine, s_sib], two_d(given["m_" + nm]), two_d(given["v_" + nm]),
                      "adamw_" + nm)
        results[nm] = [o.reshape(shp) for o in outs]

    (loss_s, g_pre1, g_post0, g_post1, g_gate0, g_shift1, g_scale1, g_gate1, s_da_re, s_da_im, s_dbb_re, s_dbb_im,
     g_c_re, g_c_im, g_dskip_full, g_bglu_full) = _unpack(_sum_leading(gath_a, "sum_small_a"), shapes_a)
    g_sgu_g, g_sgu_w, g_sgu_b = _unpack(_sum_leading(gath_b, "sum_small_b"), shapes_b)
    g_pre0, g_shift0, g_scale0 = _unpack(_sum_leading(gath_c, "sum_small_c"), shapes_c)
    loss = loss_s.reshape(())
    g_pre = jnp.concatenate([g_pre0, g_pre1], axis=0)
    g_post = jnp.concatenate([g_post0, g_post1], axis=0)
    g_bmod = jnp.concatenate([jnp.concatenate([g_shift0, g_scale0, g_gate0], axis=1),
                              jnp.concatenate([g_shift1, g_scale1, g_gate1], axis=1)], axis=0)

    g_lr, g_li, g_ldt, g_br, g_bi = _ssm_prep_bwd(lr_c, li_c, ldt_c, br_c, bi_c, s_da_re.reshape(s, 1),
                                                  s_da_im.reshape(s, 1), s_dbb_re, s_dbb_im, p, "ssm_prep_bwd")
    small = {
        "ln_pre_g": g_pre, "ln_post_g": g_post, "b_mod": g_bmod, "sgu_norm_g": g_sgu_g,
        "sgu_w": g_sgu_w.reshape(sgu_w.shape), "sgu_b": g_sgu_b.reshape(sgu_b.shape),
        "lam_re": g_lr.reshape(lam_re.shape), "lam_im": g_li.reshape(lam_im.shape),
        "b_re": g_br.reshape(b_re.shape), "b_im": g_bi.reshape(b_im.shape),
        "c_re": g_c_re.reshape(c_re.shape), "c_im": g_c_im.reshape(c_im.shape),
        "d_skip": lax.dynamic_slice(g_dskip_full, (0, chip * (w // 4)), (1, w // 4)),
        "log_dt": g_ldt.reshape(log_dt.shape),
        "b_glu": lax.dynamic_slice(g_bglu_full, (0, chip * (w // 4)), (1, w // 4)),
    }
    flat2 = lambda a: a.reshape(-1, a.shape[-1])
    wide = ("b_re", "b_im")
    for tag, group in (("adamw_small", [nm for nm in small if nm not in wide]), ("adamw_small_b", list(wide))):
        outs = _adamw_many([flat2(given[nm]) for nm in group], [flat2(small[nm]) for nm in group],
                           [flat2(given["m_" + nm]) for nm in group], [flat2(given["v_" + nm]) for nm in group], tag)
        for nm, trio in zip(group, outs):
            results[nm] = [small[nm]] + [o.reshape(given[nm].shape) for o in trio]

    rows_a = _unpack_rows(gath_a, shapes_a)
    rows_c = _unpack_rows(gath_c, shapes_c)
    dmod_rows = jnp.concatenate([rows_c[1], rows_c[2], rows_a[4], rows_a[5], rows_a[6], rows_a[7]],
                                axis=2).reshape(8, 2, 3 * d)
    dmod_shard = lax.dynamic_slice(dmod_rows, (0, 0, chip * nmod), (8, 2, nmod)).transpose(1, 0, 2)
    dmod_pad = jnp.pad(dmod_shard, ((0, 0), (0, MOD_ROWS - 8), (0, 0)))
    gr_wmod = _mod_bwd(cond_pad.T, dmod_pad, "mod_bwd")
    two_d = lambda a: a.reshape(-1, nmod)
    outs = _adamw(two_d(w_mod), [two_d(gr_wmod)], two_d(m_w_mod), two_d(v_w_mod), "adamw_w_mod")
    results["w_mod"] = [o.reshape(w_mod.shape) for o in outs]

    names = ["ln_pre_g", "ln_post_g", "w_mod", "b_mod", "w_in_ab", "w_out_ab", "sgu_norm_g", "sgu_w", "sgu_b",
             "w_in_ssm", "w_out_ssm", "lam_re", "lam_im", "b_re", "b_im", "c_re", "c_im", "d_skip", "log_dt",
             "w_glu", "b_glu"]
    return (loss, grad_x[None], *[results[nm][0] for nm in names], *[results[nm][1] for nm in names],
            *[results[nm][2] for nm in names], *[results[nm][3] for nm in names])
```

```python
import functools
import math

import jax
import jax.numpy as jnp
from jax import lax
from jax.experimental import pallas as pl
from jax.experimental.pallas import tpu as pltpu

F32 = jnp.float32
BF16 = jnp.bfloat16
MESH = pl.DeviceIdType.MESH

EPS = 1e-6
HEAD = 128
SSM_T = 512
SSM_GB = 16
ADAM_LR, ADAM_B1, ADAM_B2, ADAM_EPS, ADAM_WD, ADAM_STEP = 0.001, 0.9, 0.999, 1e-08, 0.01, 10
VMEM_LIMIT = 56 * 1024 * 1024

NN = (((1,), (0,)), ((), ()))
NT = (((1,), (1,)), ((), ()))
TN = (((0,), (0,)), ((), ()))


def _params(sem=None):
    return pltpu.CompilerParams(dimension_semantics=sem, vmem_limit_bytes=VMEM_LIMIT)


def _dot(a, b, dims=NN):
    return lax.dot_general(a, b, dims, preferred_element_type=F32)


def _bf(x):
    return x.astype(BF16)


def _gelu(x):
    k = math.sqrt(2.0 / math.pi)
    t = jnp.tanh(k * (x + 0.044715 * x * x * x))
    return 0.5 * x * (1.0 + t)


def _gelu_grad(x):
    k = math.sqrt(2.0 / math.pi)
    x2 = x * x
    t = jnp.tanh(k * (x + 0.044715 * x * x2))
    return 0.5 * (1.0 + t) + 0.5 * x * (1.0 - t * t) * k * (1.0 + 3.0 * 0.044715 * x2)


def _sigmoid(x):
    return 1.0 / (1.0 + jnp.exp(-x))


def _silu(x):
    return x * _sigmoid(x)


def _silu_grad(x):
    s = _sigmoid(x)
    return s * (1.0 + x * (1.0 - s))


def _tile(n, t, mult=128):
    if n <= t:
        return n
    for cand in range(t - t % mult, 0, -mult):
        if n % cand == 0:
            return cand
    raise ValueError((n, t, mult))


def _matmul(a, b, mode, out_dtype, name, tm=512, tn=512, tk=2048, n_split=1, ride=None, m_part=None):
    if mode == "nn":
        (m, kk), (_, n) = a.shape, b.shape
    elif mode == "nt":
        (m, kk), (n, _) = a.shape, b.shape
    else:
        (kk, m), (_, n) = a.shape, b.shape
    m_off = 0
    if m_part is not None:
        assert mode == "tn"
        first, count, parts = m_part
        tm = _tile(m // parts, tm)
        m_off = first * (m // parts) // tm
        m = count * (m // parts)
    tm, tk = _tile(m, tm), _tile(kk, tk)
    ns = n // n_split
    tn = _tile(ns, tn)
    nk = kk // tk
    dims = {"nn": NN, "nt": NT, "tn": TN}[mode]

    def body(a_ref, b_ref, o_ref, acc_ref):
        k = pl.program_id(2)
        part = _dot(_bf(a_ref[...]), _bf(b_ref[...]), dims)

        @pl.when(k == 0)
        def _():
            acc_ref[...] = part

        @pl.when(k > 0)
        def _():
            acc_ref[...] += part

        @pl.when(k == nk - 1)
        def _():
            o_ref[...] = acc_ref[...].astype(out_dtype).reshape(o_ref.shape)

    if mode == "nn":
        a_spec = pl.BlockSpec((tm, tk), lambda i, j, k: (i, k))
        b_spec = pl.BlockSpec((tk, tn), lambda i, j, k: (k, j))
    elif mode == "nt":
        a_spec = pl.BlockSpec((tm, tk), lambda i, j, k: (i, k))
        b_spec = pl.BlockSpec((tn, tk), lambda i, j, k: (j, k))
    else:
        a_spec = pl.BlockSpec((tk, tm), lambda i, j, k: (k, i + m_off))
        b_spec = pl.BlockSpec((tk, tn), lambda i, j, k: (k, j))
    if n_split == 1:
        out_shape = jax.ShapeDtypeStruct((m, n), out_dtype)
        o_spec = pl.BlockSpec((tm, tn), lambda i, j, k: (i, j))
    else:
        per = ns // tn
        out_shape = jax.ShapeDtypeStruct((n_split, m, ns), out_dtype)
        o_spec = pl.BlockSpec((1, tm, tn), lambda i, j, k: (j // per, i, j % per))
    outs = _call(body, name=name, grid=(m // tm, n // tn, nk), in_specs=[a_spec, b_spec], out_specs=[o_spec],
                 out_shape=[out_shape], scratch_shapes=[pltpu.VMEM((tm, tn), F32)], args=(a, b),
                 sem=("parallel", "parallel", "arbitrary"), ride=ride)
    return outs[0] if ride is None else (outs[0], outs[1:])


def _row_spec(tm, d):
    return pl.BlockSpec((tm, d), lambda i: (i, 0))


def _vec_spec(d):
    return pl.BlockSpec((1, d), lambda i: (0, 0))


def _acc(ref, first, val):
    @pl.when(first)
    def _():
        ref[...] = val

    @pl.when(jnp.logical_not(first))
    def _():
        ref[...] += val


def _colsum(x):
    return jnp.sum(x, axis=0, keepdims=True)


def _rownorm(x):
    r = lax.rsqrt(jnp.mean(x * x, axis=-1, keepdims=True) + EPS)
    return x * r, r


def _pre_fwd(x, g, scale, shift, name):
    l, d = x.shape
    tm = _tile(l, 256)

    def body(x_ref, g_ref, sc_ref, sh_ref, h_ref):
        n, _ = _rownorm(x_ref[...])
        h_ref[...] = _bf(n * g_ref[...] * (1.0 + sc_ref[...]) + sh_ref[...])

    return pl.pallas_call(
        body, name=name, grid=(l // tm,),
        in_specs=[_row_spec(tm, d), _vec_spec(d), _vec_spec(d), _vec_spec(d)],
        out_specs=_row_spec(tm, d), out_shape=jax.ShapeDtypeStruct((l, d), BF16),
        compiler_params=_params(("parallel",)),
    )(x, g, scale, shift)


def _post_pre_fwd(x, y, gate, pg, g1, scale1, shift1, name):
    l, d = x.shape
    tm = _tile(l, 256)

    def body(x_ref, y_ref, gate_ref, pg_ref, g1_ref, sc_ref, sh_ref, x1_ref, h1_ref):
        ny, _ = _rownorm(y_ref[...])
        x1 = x_ref[...] + gate_ref[...] * (ny * pg_ref[...])
        x1_ref[...] = x1
        n1, _ = _rownorm(x1)
        h1_ref[...] = _bf(n1 * g1_ref[...] * (1.0 + sc_ref[...]) + sh_ref[...])

    v = _vec_spec(d)
    return pl.pallas_call(
        body, name=name, grid=(l // tm,),
        in_specs=[_row_spec(tm, d), _row_spec(tm, d), v, v, v, v, v],
        out_specs=[_row_spec(tm, d), _row_spec(tm, d)],
        out_shape=[jax.ShapeDtypeStruct((l, d), F32), jax.ShapeDtypeStruct((l, d), BF16)],
        compiler_params=_params(("parallel",)),
    )(x, y, gate, pg, g1, scale1, shift1)


def _post_loss(x1, y1, gate, pg, target, name):
    l, d = x1.shape
    tm = _tile(l, 256)

    def body(x_ref, y_ref, gate_ref, pg_ref, t_ref, loss_ref, dy_ref, dx_ref, dgate_ref, dpg_ref):
        first = pl.program_id(0) == 0
        y = y_ref[...]
        ny, ry = _rownorm(y)
        q = ny * pg_ref[...]
        x2 = x_ref[...] + gate_ref[...] * q
        e = x2 - t_ref[...]
        _acc(loss_ref, first, jnp.full((1, 128), 0.5 / d, F32) * jnp.sum(e * e))
        dx2 = e * (1.0 / d)
        dx_ref[...] = dx2
        _acc(dgate_ref, first, _colsum(dx2 * q))
        dq = dx2 * gate_ref[...]
        _acc(dpg_ref, first, _colsum(dq * ny))
        dny = dq * pg_ref[...]
        dy = ry * (dny - ny * jnp.mean(dny * ny, axis=-1, keepdims=True))
        dy_ref[...] = _bf(dy)

    v = _vec_spec(d)
    return pl.pallas_call(
        body, name=name, grid=(l // tm,),
        in_specs=[_row_spec(tm, d), _row_spec(tm, d), v, v, _row_spec(tm, d)],
        out_specs=[_vec_spec(128), _row_spec(tm, d), _row_spec(tm, d), v, v],
        out_shape=[jax.ShapeDtypeStruct((1, 128), F32), jax.ShapeDtypeStruct((l, d), BF16),
                   jax.ShapeDtypeStruct((l, d), F32), jax.ShapeDtypeStruct((1, d), F32),
                   jax.ShapeDtypeStruct((1, d), F32)],
        compiler_params=_params(("arbitrary",)),
    )(x1, y1, gate, pg, target)


def _pre_bwd(dh, dres, x, g, scale, name, post=None):
    l, d = x.shape
    tm = _tile(l, 256)
    with_post = post is not None

    def body(*refs):
        if with_post:
            (dh_ref, dres_ref, x_ref, g_ref, sc_ref, y_ref, gate_ref, pg_ref,
             dx_ref, dsc_ref, dsh_ref, dg_ref, dy_ref, dgate_ref, dpg_ref) = refs
        else:
            dh_ref, dres_ref, x_ref, g_ref, sc_ref, dx_ref, dsc_ref, dsh_ref, dg_ref = refs
        first = pl.program_id(0) == 0
        dh = dh_ref[...]
        n, r = _rownorm(x_ref[...])
        _acc(dsc_ref, first, _colsum(dh * (n * g_ref[...])))
        _acc(dsh_ref, first, _colsum(dh))
        dyn = dh * (1.0 + sc_ref[...])
        _acc(dg_ref, first, _colsum(dyn * n))
        dn = dyn * g_ref[...]
        dx = dres_ref[...] + r * (dn - n * jnp.mean(dn * n, axis=-1, keepdims=True))
        dx_ref[...] = dx
        if with_post:
            ny, ry = _rownorm(y_ref[...])
            _acc(dgate_ref, first, _colsum(dx * (ny * pg_ref[...])))
            dq = dx * gate_ref[...]
            _acc(dpg_ref, first, _colsum(dq * ny))
            dny = dq * pg_ref[...]
            dy_ref[...] = _bf(ry * (dny - ny * jnp.mean(dny * ny, axis=-1, keepdims=True)))

    v = _vec_spec(d)
    row = _row_spec(tm, d)
    vec_out = jax.ShapeDtypeStruct((1, d), F32)
    in_specs = [row, row, row, v, v]
    args = [dh, dres, x, g, scale]
    out_specs = [row, v, v, v]
    out_shape = [jax.ShapeDtypeStruct((l, d), F32), vec_out, vec_out, vec_out]
    if with_post:
        in_specs += [row, v, v]
        args += list(post)
        out_specs += [row, v, v]
        out_shape += [jax.ShapeDtypeStruct((l, d), BF16), vec_out, vec_out]
    return pl.pallas_call(
        body, name=name, grid=(l // tm,), in_specs=in_specs, out_specs=out_specs, out_shape=out_shape,
        compiler_params=_params(("arbitrary",)),
    )(*args)


def _softplus_parts(z):
    e = jnp.exp(-jnp.abs(z))
    den = 1.0 + e
    lb = jnp.minimum(z, 0.0) - jnp.log(den)
    sig = jnp.where(z >= 0.0, 1.0, e) * pl.reciprocal(den, approx=True)
    return lb, lb - z, sig


def _tri(cmp, n=HEAD):
    row = lax.broadcasted_iota(jnp.int32, (n, n), 0)
    col = lax.broadcasted_iota(jnp.int32, (n, n), 1)
    return cmp(row, col)


ATT_T = 256


def _attn_fwd(qkv, wb, name, hp=4, ride=None):
    l = qkv.shape[0]
    t = ATT_T
    nh, nq = wb // HEAD, l // t
    hp = min(hp, nh)
    ng, wg = nh // hp, hp * HEAD
    scale = 1.0 / math.sqrt(HEAD)

    def body(q_ref, k_ref, v_ref, o_ref, lk_ref):
        i = pl.program_id(1)
        valid = _tri(lambda r, c: c < r, t)
        m_gt = _bf(_tri(lambda r, c: r > c, t).astype(F32))

        def tile(j, carry, diag):
            rows = pl.ds(pl.multiple_of(j * t, t), t)
            cols = [slice(hh * HEAD, (hh + 1) * HEAD) for hh in range(hp)]
            zs = [_dot(q_ref[:, cs], k_ref[rows, cs], NT) * scale for cs in cols]
            lbs, lks = [], []
            for z in zs:
                lb, lk, _ = _softplus_parts(z)
                lbs.append(lb)
                lks.append(jnp.where(valid, lk, 0.0) if diag else lk)
            laters = [_dot(_bf(lk), m_gt) for lk in lks]
            ws = [jnp.exp(lb + later + run) for lb, later, (_, run) in zip(lbs, laters, carry)]
            if diag:
                ws = [jnp.where(valid, w, 0.0) for w in ws]
            return tuple((acc + _dot(_bf(w), v_ref[rows, cs]), run + jnp.sum(lk, axis=1, keepdims=True))
                         for w, lk, cs, (acc, run) in zip(ws, lks, cols, carry))

        zero = (jnp.zeros((t, HEAD), F32), jnp.zeros((t, 1), F32))
        carry = tile(i, (zero,) * hp, True)
        carry = lax.fori_loop(0, i, lambda s, c: tile(i - 1 - s, c, False), carry)
        for hh, (acc, run) in enumerate(carry):
            cs = slice(hh * HEAD, (hh + 1) * HEAD)
            o_ref[:, cs] = acc
            lk_ref[:, cs] = jnp.broadcast_to(run, (t, HEAD))

    blk = lambda off: pl.BlockSpec((t, wg), lambda h, i: (i, off + h))
    full = lambda off: pl.BlockSpec((l, wg), lambda h, i: (0, off + h))
    out = pl.BlockSpec((t, wg), lambda h, i: (i, h))
    outs = _call(body, name=name, grid=(ng, nq), in_specs=[blk(0), full(ng), full(2 * ng)], out_specs=[out, out],
                 out_shape=[jax.ShapeDtypeStruct((l, wb), F32), jax.ShapeDtypeStruct((l, wb), F32)],
                 args=(qkv, qkv, qkv), sem=("parallel", "arbitrary"), ride=ride)
    return outs[0], outs[1], outs[2:]


def _attn_bwd(qkv, proj, dcat, lktot, wa, wb, name, hp=2, ride=None):
    l = qkv.shape[0]
    t = ATT_T
    nh, nq = wb // HEAD, l // t
    hp = min(hp, nh)
    ng, wg = nh // hp, hp * HEAD
    scale = 1.0 / math.sqrt(HEAD)

    def body(q_ref, k_ref, v_ref, bz_ref, dc_ref, lt_ref, dq_ref, dkt_out, dvt_out, do_s, qt_s, dot_s,
             dkt_ref, dvt_ref, out_sems):
        i = pl.program_id(1)

        @pl.when(i == 0)
        def _():
            dkt_ref[...] = jnp.zeros_like(dkt_ref)
            dvt_ref[...] = jnp.zeros_like(dvt_ref)

        do = dc_ref[...] * _silu(bz_ref[...])
        do_s[...] = _bf(do)
        for hh in range(hp):
            cs = slice(hh * HEAD, (hh + 1) * HEAD)
            qt_s[hh] = _bf(q_ref[:, cs].astype(F32).T * scale)
            dot_s[hh] = _bf(do[:, cs].T)
        valid = _tri(lambda r, c: c < r, t)
        m_le = _bf(_tri(lambda r, c: r <= c, t).astype(F32))
        m_lt = _bf(_tri(lambda r, c: r < c, t).astype(F32))

        def tile(j, carry, diag):
            rows = pl.ds(pl.multiple_of(j * t, t), t)
            heads = range(hp)
            cols = [slice(hh * HEAD, (hh + 1) * HEAD) for hh in heads]
            zs = [_dot(q_ref[:, cs], k_ref[rows, cs], NT) * scale for cs in cols]
            dws = [_dot(do_s[:, cs], v_ref[rows, cs], NT) for cs in cols]
            lbs, lks, sigs = [], [], []
            for z in zs:
                lb, lk, sig = _softplus_parts(z)
                lbs.append(lb)
                lks.append(jnp.where(valid, lk, 0.0) if diag else lk)
                sigs.append(sig)
            pins = [_dot(_bf(lk), m_le) for lk in lks]
            ws = [jnp.exp(lbs[hh] + (lt_ref[:, hh * HEAD:hh * HEAD + 1] - carry[hh][1]) - pins[hh]) for hh in heads]
            if diag:
                ws = [jnp.where(valid, w, 0.0) for w in ws]
            das = [dw * w for dw, w in zip(dws, ws)]
            pexs = [_dot(_bf(da), m_lt) for da in das]
            dzs = [das[hh] - sigs[hh] * (das[hh] + carry[hh][2] + pexs[hh]) for hh in heads]
            if diag:
                dzs = [jnp.where(valid, dz, 0.0) for dz in dzs]
            dzs = [_bf(dz) for dz in dzs]
            out = []
            for hh in heads:
                dkt, dvt = _dot(qt_s[hh], dzs[hh]), _dot(dot_s[hh], _bf(ws[hh]))
                for half in range(t // HEAD):
                    dkt_ref[hh, sub * j + half] += dkt[:, half * HEAD:(half + 1) * HEAD]
                    dvt_ref[hh, sub * j + half] += dvt[:, half * HEAD:(half + 1) * HEAD]
                dq, cpre, ppre = carry[hh]
                out.append((dq + _dot(dzs[hh], k_ref[rows, cols[hh]]), cpre + jnp.sum(lks[hh], axis=1, keepdims=True),
                            ppre + pexs[hh][:, t - 1:] + das[hh][:, t - 1:]))
            return tuple(out)

        zero = (jnp.zeros((t, HEAD), F32), jnp.zeros((t, 1), F32), jnp.zeros((t, 1), F32))
        carry = lax.fori_loop(0, i, lambda j, c: tile(j, c, False), (zero,) * hp)
        carry = tile(i, carry, True)
        for hh in range(hp):
            dq_ref[:, hh * HEAD:(hh + 1) * HEAD] = carry[hh][0] * scale

        @pl.when(i == nq - 1)
        def _():
            heads = pl.ds(pl.program_id(0) * hp, hp)
            copies = [pltpu.make_async_copy(dkt_ref, dkt_out.at[heads], out_sems.at[0]),
                      pltpu.make_async_copy(dvt_ref, dvt_out.at[heads], out_sems.at[1])]
            for cp in copies:
                cp.start()
            for cp in copies:
                cp.wait()

    sub = t // HEAD
    blk = lambda off: pl.BlockSpec((t, wg), lambda h, i: (i, off + h))
    full = lambda off: pl.BlockSpec((l, wg), lambda h, i: (0, off + h))
    acc_shape = jax.ShapeDtypeStruct((nh, l // HEAD, HEAD, HEAD), F32)
    acc_scratch = pltpu.VMEM((hp, l // HEAD, HEAD, HEAD), F32)
    outs = _call(
        body, name=name, grid=(ng, nq),
        in_specs=[blk(0), full(ng), full(2 * ng), blk(3 * wa // wg), blk(wa // wg), blk(0)],
        out_specs=[blk(0), ANY, ANY], out_shape=[jax.ShapeDtypeStruct((l, wb), F32), acc_shape, acc_shape],
        scratch_shapes=[pltpu.VMEM((t, wg), BF16), pltpu.VMEM((hp, HEAD, t), BF16), pltpu.VMEM((hp, HEAD, t), BF16),
                        acc_scratch, acc_scratch, pltpu.SemaphoreType.DMA((2,))],
        args=(qkv, qkv, qkv, proj, dcat, lktot), sem=("parallel", "arbitrary"), ride=ride)
    return outs[0], outs[1], outs[2], outs[3:]


def _sgu_heads(v, g_ref, w_ref, bt_ref, nh):
    keep = _tri(lambda r, c: r >= c)
    out = []
    for h in range(nh):
        cols = slice(h * HEAD, (h + 1) * HEAD)
        nv, r = _rownorm(v[:, cols])
        wm = jnp.where(keep, w_ref[h], 0.0)
        s = _dot(_bf(wm), _bf(nv * g_ref[:, cols])) + bt_ref[:, h:h + 1]
        out.append((nv, r, wm, s))
    return out


def _sgu_fwd(proj, out_b, norm_g, sgu_w, sgu_bt, wa, wb, name):
    l, n = proj.shape
    nh = wa // HEAD

    def body(au_ref, av_ref, az_ref, bz_ref, ob_ref, g_ref, w_ref, bt_ref, cat_ref):
        u, v, sz = _gelu(au_ref[...]), _gelu(av_ref[...]), _silu(az_ref[...])
        for h, (_, _, _, s) in enumerate(_sgu_heads(v, g_ref, w_ref, bt_ref, nh)):
            cols = slice(h * HEAD, (h + 1) * HEAD)
            cat_ref[:, cols] = _bf(u[:, cols] * s * sz[:, cols])
        cat_ref[:, wa:] = _bf(ob_ref[...] * _silu(bz_ref[...]))

    a_blk = lambda j: pl.BlockSpec((HEAD, wa), lambda i: (i, j))
    return pl.pallas_call(
        body, name=name, grid=(l // HEAD,),
        in_specs=[a_blk(0), a_blk(1), a_blk(2), a_blk(3), pl.BlockSpec((HEAD, wb), lambda i: (i, 0)),
                  _vec_spec(wa), pl.BlockSpec((nh, HEAD, HEAD), lambda i: (0, 0, 0)),
                  pl.BlockSpec((HEAD, nh), lambda i: (0, 0))],
        out_specs=pl.BlockSpec((HEAD, wa + wb), lambda i: (i, 0)),
        out_shape=jax.ShapeDtypeStruct((l, wa + wb), BF16),
        compiler_params=_params(("parallel",)),
    )(proj, proj, proj, proj, out_b, norm_g, sgu_w, sgu_bt)


def _sgu_bwd(proj, out_b, dcat, dq, dk, dv, norm_g, sgu_w, sgu_bt, wa, wb, name):
    l = proj.shape[0]
    n = 3 * wa + 4 * wb
    nh = wa // HEAD

    def body(au_ref, av_ref, az_ref, bz_ref, ob_ref, dc_ref, dq_ref, dk_ref, dv_ref, g_ref, w_ref, bt_ref,
             dp_ref, dw_ref, dbt_ref, dg_ref):
        first = pl.program_id(0) == 0
        keep = _tri(lambda r, c: r >= c)
        au, av, az = au_ref[...], av_ref[...], az_ref[...]
        u, v, sz = _gelu(au), _gelu(av), _silu(az)
        dgelu_u, dgelu_v, dsilu_z = _gelu_grad(au), _gelu_grad(av), _silu_grad(az)
        dg_parts = []
        for h, (nv, r, wm, s) in enumerate(_sgu_heads(v, g_ref, w_ref, bt_ref, nh)):
            cols = slice(h * HEAD, (h + 1) * HEAD)
            dca, uh, szh, gh = dc_ref[:, cols], u[:, cols], sz[:, cols], g_ref[:, cols]
            dp_ref[:, cols] = _bf(dca * s * szh * dgelu_u[:, cols])
            dp_ref[:, 2 * wa + h * HEAD:2 * wa + (h + 1) * HEAD] = _bf(dca * uh * s * dsilu_z[:, cols])
            ds = dca * uh * szh
            _acc(dw_ref.at[h], first, jnp.where(keep, _dot(_bf(ds), _bf(nv * gh), NT), 0.0))
            _acc(dbt_ref.at[:, h:h + 1], first, jnp.sum(ds, axis=1, keepdims=True))
            dvh = _dot(_bf(wm.T), _bf(ds))
            dg_parts.append(_colsum(dvh * nv))
            dnv = dvh * gh
            dvv = r * (dnv - nv * jnp.mean(dnv * nv, axis=-1, keepdims=True))
            dp_ref[:, wa + h * HEAD:wa + (h + 1) * HEAD] = _bf(dvv * dgelu_v[:, cols])
        _acc(dg_ref, first, jnp.concatenate(dg_parts, axis=1))
        base = 3 * wa
        dp_ref[:, base:base + wb] = _bf(dq_ref[...])
        for h in range(wb // HEAD):
            dp_ref[:, base + wb + h * HEAD:base + wb + (h + 1) * HEAD] = _bf(dk_ref[h, 0].T)
            dp_ref[:, base + 2 * wb + h * HEAD:base + 2 * wb + (h + 1) * HEAD] = _bf(dv_ref[h, 0].T)
        dp_ref[:, base + 3 * wb:] = _bf(dc_ref[:, wa:] * ob_ref[...] * _silu_grad(bz_ref[...]))

    a_blk = lambda j: pl.BlockSpec((HEAD, wa), lambda i: (i, j))
    b_blk = pl.BlockSpec((HEAD, wb), lambda i: (i, 0))
    t_blk = pl.BlockSpec((wb // HEAD, 1, HEAD, HEAD), lambda i: (0, i, 0, 0))
    w_spec = pl.BlockSpec((nh, HEAD, HEAD), lambda i: (0, 0, 0))
    bt_spec = pl.BlockSpec((HEAD, nh), lambda i: (0, 0))
    return pl.pallas_call(
        body, name=name, grid=(l // HEAD,),
        in_specs=[a_blk(0), a_blk(1), a_blk(2), a_blk(3), b_blk, pl.BlockSpec((HEAD, wa + wb), lambda i: (i, 0)),
                  b_blk, t_blk, t_blk, _vec_spec(wa), w_spec, bt_spec],
        out_specs=[pl.BlockSpec((HEAD, n), lambda i: (i, 0)), w_spec, bt_spec, _vec_spec(wa)],
        out_shape=[jax.ShapeDtypeStruct((l, n), BF16), jax.ShapeDtypeStruct((nh, HEAD, HEAD), F32),
                   jax.ShapeDtypeStruct((HEAD, nh), F32), jax.ShapeDtypeStruct((1, wa), F32)],
        compiler_params=_params(("arbitrary",)),
    )(proj, proj, proj, proj, out_b, dcat, dq, dk, dv, norm_g, sgu_w, sgu_bt)


def _ssm_discretise(lr, li, ldt, br, bi):
    dt = jnp.exp(ldt)
    mag = jnp.exp(lr * dt)
    a_re = mag * jnp.cos(li * dt)
    a_im = mag * jnp.sin(li * dt)
    den = lr * lr + li * li
    nr = a_re - 1.0
    coef_re = (nr * lr + a_im * li) / den
    coef_im = (a_im * lr - nr * li) / den
    return a_re, a_im, coef_re * br - coef_im * bi, coef_re * bi + coef_im * br


def _ssm_prep(lr, li, ldt, br, bi, lr_row, li_row, ldt_row, name):
    s, c = br.shape

    def body(lr_ref, li_ref, ldt_ref, br_ref, bi_ref, lrr_ref, lir_ref, ldtr_ref, bbr_ref, bbi_ref, tr_ref, ti_ref):
        _, _, bbr, bbi = _ssm_discretise(lr_ref[...], li_ref[...], ldt_ref[...], br_ref[...], bi_ref[...])
        bbr_ref[...] = bbr
        bbi_ref[...] = bbi
        row = lax.broadcasted_iota(jnp.int32, (SCAN_ROWS, 1), 0)
        blk, r = jnp.right_shift(row, 3), jnp.bitwise_and(row, 7)
        kind, rev = jnp.bitwise_and(blk, 3), blk >= 4
        step = jnp.left_shift(1, kind)
        n = jnp.where(kind < 3, step, jnp.where(rev, 8 - r, r + 1)).astype(F32)
        keep = (kind == 3) | (rev & (r < 8 - step)) | (jnp.logical_not(rev) & (r >= step))
        dt = jnp.exp(ldtr_ref[...])
        mag = jnp.exp(n * (lrr_ref[...] * dt))
        ang = n * (lir_ref[...] * dt)
        tr_ref[...] = jnp.where(keep, mag * jnp.cos(ang), 0.0)
        ti_ref[...] = jnp.where(keep, jnp.where(rev, -1.0, 1.0) * mag * jnp.sin(ang), 0.0)

    col = jax.ShapeDtypeStruct((s, c), F32)
    row = jax.ShapeDtypeStruct((SCAN_ROWS, s), F32)
    return pl.pallas_call(body, name=name, out_shape=[col, col, row, row])(
        lr, li, ldt, br, bi, lr_row, li_row, ldt_row)


def _ssm_prep_bwd(lr, li, ldt, br, bi, da_re, da_im, dbb_re, dbb_im, p, name):
    s, c = br.shape

    def body(lr_ref, li_ref, ldt_ref, br_ref, bi_ref, dar_ref, dai_ref, dbr_ref, dbi_ref,
             dlr_ref, dli_ref, dldt_ref, dbre_ref, dbim_ref):
        args = (lr_ref[...], li_ref[...], ldt_ref[...], br_ref[...], bi_ref[...])
        _, vjp = jax.vjp(_ssm_discretise, *args)
        dlr, dli, dldt, dbr, dbi = vjp((dar_ref[...], dai_ref[...], dbr_ref[...], dbi_ref[...]))
        dlr_ref[...] = dlr
        dli_ref[...] = dli
        dbre_ref[...] = dbr
        dbim_ref[...] = dbi
        idx = lax.broadcasted_iota(jnp.int32, (s, s // p), 0)
        grp = lax.broadcasted_iota(jnp.int32, (s, s // p), 1)
        own = (idx >= grp * p) & (idx < (grp + 1) * p)
        dldt_ref[...] = _colsum(jnp.where(own, dldt, 0.0))

    col1 = jax.ShapeDtypeStruct((s, 1), F32)
    colc = jax.ShapeDtypeStruct((s, c), F32)
    return pl.pallas_call(
        body, name=name, out_shape=[col1, col1, jax.ShapeDtypeStruct((1, s // p), F32), colc, colc],
    )(lr, li, ldt, br, bi, da_re, da_im, dbb_re, dbb_im)


SCAN_ROWS = 64


def _scan_groups(xr, xi, tr_ref, ti_ref, cr, ci, reverse):
    ng = xr.shape[0] // 8
    base = SCAN_ROWS // 2 if reverse else 0
    pr, pi = tr_ref[base + 24:base + 32, :], ti_ref[base + 24:base + 32, :]
    edge = slice(0, 1) if reverse else slice(7, 8)
    out_r, out_i = [None] * ng, [None] * ng
    for g in (range(ng - 1, -1, -1) if reverse else range(ng)):
        sr, si = xr[8 * g:8 * g + 8, :], xi[8 * g:8 * g + 8, :]
        for k in range(3):
            ar, ai = tr_ref[base + 8 * k:base + 8 * k + 8, :], ti_ref[base + 8 * k:base + 8 * k + 8, :]
            shift = 8 - (1 << k) if reverse else 1 << k
            rr, ri = pltpu.roll(sr, shift, 0), pltpu.roll(si, shift, 0)
            sr, si = sr + ar * rr - ai * ri, si + ar * ri + ai * rr
        sr, si = sr + pr * cr - pi * ci, si + pr * ci + pi * cr
        cr, ci = sr[edge, :], si[edge, :]
        out_r[g], out_i[g] = sr, si
    return jnp.concatenate(out_r, axis=0), jnp.concatenate(out_i, axis=0), cr, ci


def _ssm_fwd(proj, bbd, ccd, pw_re, pw_im, d_skip, w, name):
    l = proj.shape[0]
    nb, cw, ns2 = bbd.shape
    ns = ns2 // 2
    nc = l // SSM_T

    def body(u_ref, bbd_ref, ccd_ref, pr_ref, pi_ref, d_ref, y_ref, hsr_ref, hsi_ref, h_ref, hr_s, hi_s):
        @pl.when(pl.program_id(1) == 0)
        def _():
            hr_s[...] = jnp.zeros_like(hr_s)
            hi_s[...] = jnp.zeros_like(hi_s)

        hsr_ref[...] = hr_s[...].reshape(hsr_ref.shape)
        hsi_ref[...] = hi_s[...].reshape(hsi_ref.shape)
        u = u_ref[...]
        bu = _dot(_bf(u), bbd_ref[0])
        hr, hi, cr, ci = _scan_groups(bu[:, :ns], bu[:, ns:], pr_ref, pi_ref, hr_s[...], hi_s[...], False)
        hr_s[...] = cr
        hi_s[...] = ci
        h_bf = _bf(jnp.concatenate([hr, hi], axis=1))
        h_ref[...] = h_bf
        y_ref[...] = _dot(h_bf, ccd_ref[0]) + d_ref[...] * u

    tab = pl.BlockSpec((SCAN_ROWS, ns), lambda b, k: (0, b))
    return pl.pallas_call(
        body, name=name, grid=(nb, nc),
        in_specs=[pl.BlockSpec((SSM_T, cw), lambda b, k: (k, b)),
                  pl.BlockSpec((1, cw, ns2), lambda b, k: (b, 0, 0)),
                  pl.BlockSpec((1, ns2, cw), lambda b, k: (b, 0, 0)),
                  tab, tab, pl.BlockSpec((1, cw), lambda b, k: (0, b))],
        out_specs=[pl.BlockSpec((SSM_T, cw), lambda b, k: (k, b)),
                   pl.BlockSpec((1, 1, ns), lambda b, k: (k, 0, b)), pl.BlockSpec((1, 1, ns), lambda b, k: (k, 0, b)),
                   pl.BlockSpec((SSM_T, ns2), lambda b, k: (k, b))],
        out_shape=[jax.ShapeDtypeStruct((l, w), F32), jax.ShapeDtypeStruct((nc, 1, nb * ns), F32),
                   jax.ShapeDtypeStruct((nc, 1, nb * ns), F32), jax.ShapeDtypeStruct((l, nb * ns2), BF16)],
        scratch_shapes=[pltpu.VMEM((1, ns), F32), pltpu.VMEM((1, ns), F32)],
        compiler_params=_params(("parallel", "arbitrary")),
    )(proj, bbd, ccd, pw_re, pw_im, d_skip)


def _ssm_bwd(proj, dy, hs_re, hs_im, h_all, bbd, ccd, pw_re, pw_im, d_skip, w, name, ride=None):
    l = proj.shape[0]
    nb, cw, ns2 = bbd.shape
    ns = ns2 // 2
    nc = l // SSM_T

    def body(u_ref, dy_ref, hsr_ref, hsi_ref, h_ref, bbd_ref, ccd_ref, pr_ref, pi_ref, d_ref,
             du_ref, dbbd_ref, dccd_ref, dar_ref, dai_ref, dd_ref, gr_s, gi_s):
        first = pl.program_id(1) == 0

        @pl.when(first)
        def _():
            gr_s[...] = jnp.zeros_like(gr_s)
            gi_s[...] = jnp.zeros_like(gi_s)

        u, dy = u_ref[...], dy_ref[...]
        dy_bf = _bf(dy)
        hr0, hi0 = hsr_ref[0], hsi_ref[0]
        h = h_ref[...].astype(F32)
        hr, hi = h[:, :ns], h[:, ns:]
        dh = _dot(dy_bf, ccd_ref[0], NT)
        gr, gi, gcr, gci = _scan_groups(dh[:, :ns], dh[:, ns:], pr_ref, pi_ref, gr_s[...], gi_s[...], True)
        gr_s[...] = gcr
        gi_s[...] = gci
        row0 = lax.broadcasted_iota(jnp.int32, hr.shape, 0) == 0
        pr_h = jnp.where(row0, hr0, pltpu.roll(hr, 1, 0))
        pi_h = jnp.where(row0, hi0, pltpu.roll(hi, 1, 0))
        _acc(dar_ref, first, _colsum(pr_h * gr + pi_h * gi))
        _acc(dai_ref, first, _colsum(pr_h * gi - pi_h * gr))
        g_bf = _bf(jnp.concatenate([gr, gi], axis=1))
        _acc(dbbd_ref.at[0], first, _dot(_bf(u.T), g_bf))
        _acc(dccd_ref.at[0], first, _dot(_bf(h.T), dy_bf))
        du_ref[...] = _bf(_dot(g_bf, bbd_ref[0], NT) + d_ref[...] * dy)
        _acc(dd_ref, first, _colsum(dy * u))

    rev = lambda b, k: (nc - 1 - k, b)
    outs = _call(
        body, name=name, grid=(nb, nc), ride=ride, sem=("parallel", "arbitrary"),
        args=(proj, dy, hs_re, hs_im, h_all, bbd, ccd, pw_re, pw_im, d_skip),
        in_specs=[pl.BlockSpec((SSM_T, cw), rev), pl.BlockSpec((SSM_T, cw), rev),
                  pl.BlockSpec((1, 1, ns), lambda b, k: (nc - 1 - k, 0, b)),
                  pl.BlockSpec((1, 1, ns), lambda b, k: (nc - 1 - k, 0, b)),
                  pl.BlockSpec((SSM_T, ns2), rev),
                  pl.BlockSpec((1, cw, ns2), lambda b, k: (b, 0, 0)),
                  pl.BlockSpec((1, ns2, cw), lambda b, k: (b, 0, 0)),
                  pl.BlockSpec((SCAN_ROWS, ns), lambda b, k: (0, b)), pl.BlockSpec((SCAN_ROWS, ns), lambda b, k: (0, b)),
                  pl.BlockSpec((1, cw), lambda b, k: (0, b))],
        out_specs=[pl.BlockSpec((SSM_T, cw), rev),
                   pl.BlockSpec((1, cw, ns2), lambda b, k: (b, 0, 0)),
                   pl.BlockSpec((1, ns2, cw), lambda b, k: (b, 0, 0)),
                   pl.BlockSpec((1, ns), lambda b, k: (0, b)), pl.BlockSpec((1, ns), lambda b, k: (0, b)),
                   pl.BlockSpec((1, cw), lambda b, k: (0, b))],
        out_shape=[jax.ShapeDtypeStruct((l, w), BF16), jax.ShapeDtypeStruct(bbd.shape, F32),
                   jax.ShapeDtypeStruct(ccd.shape, F32), jax.ShapeDtypeStruct((1, nb * ns), F32),
                   jax.ShapeDtypeStruct((1, nb * ns), F32), jax.ShapeDtypeStruct((1, w), F32)],
        scratch_shapes=[pltpu.VMEM((1, ns), F32), pltpu.VMEM((1, ns), F32)])
    return (*outs[:6], outs[6:])


def _block_diag_b(bb_re, bb_im, g, p, c):
    nb = g // SSM_GB
    keep = _same_group(SSM_GB * c, c, SSM_GB * p, p)

    def one(bb):
        t = bb.reshape(nb, SSM_GB, p, c).transpose(0, 1, 3, 2).reshape(nb, SSM_GB * c, p)
        return jnp.where(keep, jnp.tile(t, (1, 1, SSM_GB)), 0.0)

    return jnp.concatenate([one(bb_re), one(bb_im)], axis=2)


def _same_group(rows, per_row, cols, per_col):
    r = lax.broadcasted_iota(jnp.int32, (rows, cols), 0) // per_row
    q = lax.broadcasted_iota(jnp.int32, (rows, cols), 1) // per_col
    return r == q


def _block_diag_c(c_re, c_im, g, p, c):
    nb = g // SSM_GB
    keep = _same_group(SSM_GB * p, p, SSM_GB * c, c)

    def one(cc):
        t = cc.reshape(nb, SSM_GB, c, p).transpose(0, 1, 3, 2).reshape(nb, SSM_GB * p, c)
        return jnp.where(keep, jnp.tile(t, (1, 1, SSM_GB)), 0.0)

    return jnp.concatenate([one(c_re), one(-c_im)], axis=1)


def _diag_of_b(dbbd, g, p, c):
    nb = g // SSM_GB
    keep = _same_group(SSM_GB * c, c, SSM_GB * p, p)

    def one(blk):
        d = jnp.where(keep, blk, 0.0).reshape(nb, SSM_GB * c, SSM_GB, p).sum(axis=2)
        return d.reshape(nb, SSM_GB, c, p).transpose(0, 1, 3, 2).reshape(g * p, c)

    half = SSM_GB * p
    return one(dbbd[:, :, :half]), one(dbbd[:, :, half:])


def _diag_of_c(dccd, g, p, c):
    nb = g // SSM_GB
    keep = _same_group(SSM_GB * p, p, SSM_GB * c, c)

    def one(blk):
        d = jnp.where(keep, blk, 0.0).reshape(nb, SSM_GB * p, SSM_GB, c).sum(axis=2)
        return d.reshape(nb, SSM_GB, p, c).transpose(0, 1, 3, 2).reshape(g, c, p)

    half = SSM_GB * p
    return one(dccd[:, :half]), -one(dccd[:, half:])


def _glu_fwd(y, proj, w_glu, b_glu, name):
    l, w = y.shape
    tm = _tile(l, 256)

    def body(y_ref, z_ref, w_ref, b_ref, o_ref):
        g = _gelu(y_ref[...])
        t = _dot(_bf(g), w_ref[...]) + b_ref[...]
        o_ref[...] = _bf(g * _sigmoid(t) * _silu(z_ref[...]))

    return pl.pallas_call(
        body, name=name, grid=(l // tm,),
        in_specs=[_row_spec(tm, w), pl.BlockSpec((tm, w), lambda i: (i, 1)),
                  pl.BlockSpec((w, w), lambda i: (0, 0)), _vec_spec(w)],
        out_specs=_row_spec(tm, w), out_shape=jax.ShapeDtypeStruct((l, w), BF16),
        compiler_params=_params(("parallel",)),
    )(y, proj, w_glu, b_glu)


def _glu_bwd(do, y, proj, w_glu, b_glu, name):
    l, w = y.shape
    tm = _tile(l, 256)
    nsteps = l // tm

    def body(do_ref, y_ref, z_ref, w_ref, b_ref, dy_ref, dz_ref, dw_ref, db_ref, dw_acc):
        i = pl.program_id(0)
        first = i == 0
        yv, z, do = y_ref[...], z_ref[...], do_ref[...]
        g = _gelu(yv)
        g_bf = _bf(g)
        sg = _sigmoid(_dot(g_bf, w_ref[...]) + b_ref[...])
        dyy = do * _silu(z)
        dz_ref[...] = _bf(do * g * sg * _silu_grad(z))
        dt = dyy * g * sg * (1.0 - sg)
        dt_bf = _bf(dt)
        dg = dyy * sg + _dot(dt_bf, w_ref[...], NT)
        dy_ref[...] = dg * _gelu_grad(yv)
        _acc(dw_acc, first, _dot(_bf(g.T), dt_bf))
        _acc(db_ref, first, _colsum(dt))

        @pl.when(i == nsteps - 1)
        def _():
            dw_ref[...] = _bf(dw_acc[...])

    return pl.pallas_call(
        body, name=name, grid=(nsteps,),
        in_specs=[_row_spec(tm, w), _row_spec(tm, w), pl.BlockSpec((tm, w), lambda i: (i, 1)),
                  pl.BlockSpec((w, w), lambda i: (0, 0)), _vec_spec(w)],
        out_specs=[_row_spec(tm, w), _row_spec(tm, w), pl.BlockSpec((w, w), lambda i: (0, 0)), _vec_spec(w)],
        out_shape=[jax.ShapeDtypeStruct((l, w), F32), jax.ShapeDtypeStruct((l, w), BF16),
                   jax.ShapeDtypeStruct((w, w), BF16), jax.ShapeDtypeStruct((1, w), F32)],
        scratch_shapes=[pltpu.VMEM((w, w), F32)],
        compiler_params=_params(("arbitrary",)),
    )(do, y, proj, w_glu, b_glu)


MOD_ROWS = 128


def _mod_fwd(cond_pad, w_mod, b_shard, name):
    nl, d, ncol = w_mod.shape
    tn = _tile(ncol, 512)

    def body(c_ref, w_ref, b_ref, o_ref):
        o_ref[0] = _dot(_bf(c_ref[...]), _bf(w_ref[0])) + b_ref[0]

    return pl.pallas_call(
        body, name=name, grid=(nl, ncol // tn),
        in_specs=[pl.BlockSpec((MOD_ROWS, d), lambda a, j: (0, 0)),
                  pl.BlockSpec((1, d, tn), lambda a, j: (a, 0, j)),
                  pl.BlockSpec((1, 1, tn), lambda a, j: (a, 0, j))],
        out_specs=pl.BlockSpec((1, MOD_ROWS, tn), lambda a, j: (a, 0, j)),
        out_shape=jax.ShapeDtypeStruct((nl, MOD_ROWS, ncol), F32),
        compiler_params=_params(("parallel", "parallel")),
    )(cond_pad, w_mod, b_shard)


def _mod_bwd(cond_pad_t, dmod_pad, name):
    nl, _, ncol = dmod_pad.shape
    d = cond_pad_t.shape[0]
    tn = _tile(ncol, 512)

    def body(c_ref, dm_ref, o_ref):
        o_ref[0] = _dot(_bf(c_ref[...]), _bf(dm_ref[0]))

    return pl.pallas_call(
        body, name=name, grid=(nl, ncol // tn),
        in_specs=[pl.BlockSpec((d, MOD_ROWS), lambda a, j: (0, 0)),
                  pl.BlockSpec((1, MOD_ROWS, tn), lambda a, j: (a, 0, j))],
        out_specs=pl.BlockSpec((1, d, tn), lambda a, j: (a, 0, j)),
        out_shape=jax.ShapeDtypeStruct((nl, d, ncol), F32),
        compiler_params=_params(("parallel", "parallel")),
    )(cond_pad_t, dmod_pad)


def _silu_rows(c2d, name):
    def body(c_ref, o_ref):
        o_ref[...] = _silu(c_ref[...])

    return pl.pallas_call(body, name=name, out_shape=jax.ShapeDtypeStruct(c2d.shape, F32))(c2d)


def _sum_leading(x, name):
    n, r, c = x.shape
    tr = _tile(r, max(16, (1 << 20) // (4 * c)), 16 if r % 16 == 0 else 8)

    def body(x_ref, o_ref):
        acc = x_ref[0].astype(F32)
        for k in range(1, n):
            acc = acc + x_ref[k].astype(F32)
        o_ref[...] = acc

    return pl.pallas_call(
        body, name=name, grid=(r // tr,),
        in_specs=[pl.BlockSpec((n, tr, c), lambda i: (0, i, 0))], out_specs=pl.BlockSpec((tr, c), lambda i: (i, 0)),
        out_shape=jax.ShapeDtypeStruct((r, c), F32), compiler_params=_params(("parallel",)),
    )(x)


def _adamw(w, gs, m, v, name):
    r, c = w.shape
    tr = _tile(r, max(8, (3 << 19) // (4 * c)), 8)
    ng = len(gs)

    def body(*refs):
        w_ref, g_refs, m_ref, v_ref = refs[0], refs[1:1 + ng], refs[1 + ng], refs[2 + ng]
        g_ref, d_ref, nm_ref, nv_ref = refs[3 + ng:]
        g = g_refs[0][...]
        for extra in g_refs[1:]:
            g = g + extra[...]
        g_ref[...] = g
        d_ref[...], nm_ref[...], nv_ref[...] = _adamw_math(w_ref[...], g, m_ref[...], v_ref[...])

    spec = pl.BlockSpec((tr, c), lambda i: (i, 0))
    shp = jax.ShapeDtypeStruct((r, c), F32)
    return pl.pallas_call(
        body, name=name, grid=(r // tr,), in_specs=[spec] * (3 + ng), out_specs=[spec] * 4,
        out_shape=[shp] * 4, compiler_params=_params(("parallel",)),
    )(w, *gs, m, v)


def _adamw_math(w, g, m, v):
    nm = ADAM_B1 * m + (1.0 - ADAM_B1) * g
    nv = ADAM_B2 * v + (1.0 - ADAM_B2) * (g * g)
    m_hat = nm / (1.0 - ADAM_B1 ** ADAM_STEP)
    v_hat = nv / (1.0 - ADAM_B2 ** ADAM_STEP)
    return -ADAM_LR * (m_hat / (jnp.sqrt(v_hat) + ADAM_EPS) + ADAM_WD * w), nm, nv


def _adamw_many(ws, gs, ms, vs, name):
    n = len(ws)

    def body(*refs):
        w_refs, g_refs, m_refs, v_refs = (refs[k * n:(k + 1) * n] for k in range(4))
        outs = refs[4 * n:]
        for i in range(n):
            outs[3 * i][...], outs[3 * i + 1][...], outs[3 * i + 2][...] = _adamw_math(
                w_refs[i][...], g_refs[i][...], m_refs[i][...], v_refs[i][...])

    out_shape = [jax.ShapeDtypeStruct(w.shape, F32) for w in ws for _ in range(3)]
    outs = pl.pallas_call(body, name=name, out_shape=out_shape, compiler_params=_params())(*ws, *gs, *ms, *vs)
    return [tuple(outs[3 * i:3 * i + 3]) for i in range(n)]


ANY = pl.BlockSpec(memory_space=pl.ANY)


def _flip(v, bit):
    return 1 - v if bit else v


def _allgather8_ops(x_ref, o_ref, send_sems, recv_sems, local_sem):
    mx, my, mc = lax.axis_index("x"), lax.axis_index("y"), lax.axis_index("c")
    me = 4 * mx + 2 * my + mc

    def mine():
        return pltpu.make_async_copy(x_ref, o_ref.at[me], local_sem)

    def copy(j, outgoing):
        peer = (_flip(mx, j & 4), _flip(my, j & 2), _flip(mc, j & 1))
        slot = me if outgoing else 4 * peer[0] + 2 * peer[1] + peer[2]
        return pltpu.make_async_remote_copy(
            src_ref=x_ref, dst_ref=o_ref.at[slot], send_sem=send_sems.at[j - 1], recv_sem=recv_sems.at[j - 1],
            device_id=peer, device_id_type=MESH)

    def start():
        mine().start()
        for j in range(1, 8):
            copy(j, True).start()

    def wait():
        for j in range(1, 8):
            copy(j, False).wait()
        mine().wait()

    return start, wait


def _ride_all8(x):
    return dict(xs=[x], shapes=[jax.ShapeDtypeStruct((8,) + x.shape, x.dtype)],
                sems=[pltpu.SemaphoreType.DMA((7,)), pltpu.SemaphoreType.DMA((7,)), pltpu.SemaphoreType.DMA],
                ops=lambda x_refs, o_refs, sems: _allgather8_ops(x_refs[0], o_refs[0], *sems))


def _ride_chip(xs, gather):
    return dict(xs=list(xs), shapes=_chip_exchange_shapes(xs, gather), sems=_chip_exchange_sems(len(xs)),
                ops=lambda x_refs, o_refs, sems: _chip_exchange_ops(x_refs, o_refs, *sems, gather))


def _allgather8(x, name):
    def body(x_ref, o_ref, *sems):
        start, wait = _allgather8_ops(x_ref, o_ref, *sems)
        start()
        wait()

    ride = _ride_all8(x)
    return pl.pallas_call(body, name=name, in_specs=[ANY], out_specs=ANY, out_shape=ride["shapes"][0],
                          scratch_shapes=ride["sems"])(x)


def _gather_halves(x, name):
    r = x.shape[0]
    half = r // 2

    def body(x_ref, o_ref, ici_send, ici_recv, d2d_send, d2d_recv, local_sem):
        mx, my, mc = lax.axis_index("x"), lax.axis_index("y"), lax.axis_index("c")
        k0 = 2 * mx + my
        mine = pl.ds(pl.multiple_of(mc * half, 16), half)
        theirs = pl.ds(pl.multiple_of((1 - mc) * half, 16), half)
        local = pltpu.make_async_copy(x_ref, o_ref.at[k0], local_sem)
        local.start()

        def chips(j):
            px, py = _flip(mx, j & 2), _flip(my, j & 1)
            return px, py, 2 * px + py

        def over_ici(j, outgoing):
            px, py, kp = chips(j)
            dst = o_ref.at[k0, mine] if outgoing else o_ref.at[kp, mine]
            return pltpu.make_async_remote_copy(
                src_ref=x_ref.at[mine], dst_ref=dst, send_sem=ici_send.at[j - 1], recv_sem=ici_recv.at[j - 1],
                device_id=(px, py, mc), device_id_type=MESH)

        def over_d2d(j, outgoing):
            _, _, kp = chips(j)
            rows = mine if outgoing else theirs
            return pltpu.make_async_remote_copy(
                src_ref=o_ref.at[kp, rows], dst_ref=o_ref.at[kp, rows], send_sem=d2d_send.at[j - 1],
                recv_sem=d2d_recv.at[j - 1], device_id=(mx, my, 1 - mc), device_id_type=MESH)

        for j in range(1, 4):
            over_ici(j, True).start()
        for j in range(1, 4):
            over_ici(j, False).wait_recv()
            over_d2d(j, True).start()
        for j in range(1, 4):
            over_ici(j, True).wait_send()
            over_d2d(j, True).wait_send()
            over_d2d(j, False).wait_recv()
        local.wait()

    dma3 = pltpu.SemaphoreType.DMA((3,))
    return pl.pallas_call(
        body, name=name, in_specs=[ANY], out_specs=ANY, out_shape=jax.ShapeDtypeStruct((4,) + x.shape, x.dtype),
        scratch_shapes=[dma3, dma3, dma3, dma3, pltpu.SemaphoreType.DMA])(x)


def _chip_exchange(xs, gather, name):
    n = len(xs)

    def body(*refs):
        start, wait = _chip_exchange_ops(refs[:n], refs[n:2 * n], *refs[2 * n:], gather)
        start()
        wait()

    return pl.pallas_call(
        body, name=name, in_specs=[ANY] * n, out_specs=[ANY] * n, out_shape=_chip_exchange_shapes(xs, gather),
        scratch_shapes=_chip_exchange_sems(n),
    )(*xs)


def _chip_exchange_shapes(xs, gather):
    return [jax.ShapeDtypeStruct(((4,) + x.shape) if gather else x.shape, x.dtype) for x in xs]


def _chip_exchange_sems(n):
    return [pltpu.SemaphoreType.DMA((3 * n,)), pltpu.SemaphoreType.DMA((3 * n,)), pltpu.SemaphoreType.DMA((n,))]


def _chip_exchange_ops(x_refs, o_refs, send_sems, recv_sems, local_sems, gather):
    n = len(x_refs)
    mx, my, mc = lax.axis_index("x"), lax.axis_index("y"), lax.axis_index("c")
    k0 = 2 * mx + my

    def local(a):
        src = x_refs[a] if gather else x_refs[a].at[k0]
        return pltpu.make_async_copy(src, o_refs[a].at[k0], local_sems.at[a])

    def copy(a, j, outgoing):
        px, py = _flip(mx, j & 2), _flip(my, j & 1)
        kp = 2 * px + py
        if outgoing:
            src = x_refs[a] if gather else x_refs[a].at[kp]
            dst = o_refs[a].at[k0]
        else:
            src = x_refs[a] if gather else x_refs[a].at[k0]
            dst = o_refs[a].at[kp]
        s = a * 3 + j - 1
        return pltpu.make_async_remote_copy(
            src_ref=src, dst_ref=dst, send_sem=send_sems.at[s], recv_sem=recv_sems.at[s],
            device_id=(px, py, mc), device_id_type=MESH)

    def start():
        for a in range(n):
            local(a).start()
            for j in range(1, 4):
                copy(a, j, True).start()

    def wait():
        for a in range(n):
            for j in range(1, 4):
                copy(a, j, False).wait()
            local(a).wait()

    return start, wait


def _call(body, *, name, grid, in_specs, out_specs, out_shape, args, scratch_shapes=(), sem=None, ride=None):
    if not ride:
        return pl.pallas_call(
            body, name=name, grid=grid, in_specs=list(in_specs), out_specs=list(out_specs), out_shape=list(out_shape),
            scratch_shapes=list(scratch_shapes), compiler_params=_params(sem))(*args)
    xs = [x for r in ride for x in r["xs"]]
    shapes = [s for r in ride for s in r["shapes"]]
    sems = [s for r in ride for s in r["sems"]]
    n_in, n_out, n_scr, nx = len(in_specs), len(out_specs), len(scratch_shapes), len(xs)

    def wrapped(*refs):
        ins, x_refs = refs[:n_in], refs[n_in:n_in + nx]
        outs = refs[n_in + nx:n_in + nx + n_out]
        lands = refs[n_in + nx + n_out:n_in + 2 * nx + n_out]
        rest = refs[n_in + 2 * nx + n_out:]
        scr, sem_refs = rest[:n_scr], rest[n_scr:]
        ops, xo, so = [], 0, 0
        for r in ride:
            nr, ns = len(r["xs"]), len(r["sems"])
            ops.append(r["ops"](x_refs[xo:xo + nr], lands[xo:xo + nr], sem_refs[so:so + ns]))
            xo, so = xo + nr, so + ns
        ids = [pl.program_id(a) for a in range(len(grid))]
        first = functools.reduce(jnp.logical_and, [i == 0 for i in ids])
        last = functools.reduce(jnp.logical_and, [i == g - 1 for i, g in zip(ids, grid)])

        @pl.when(first)
        def _():
            for start, _ in ops:
                start()

        body(*ins, *outs, *scr)

        @pl.when(last)
        def _():
            for _, wait in ops:
                wait()

    return pl.pallas_call(
        wrapped, name=name, grid=grid, in_specs=list(in_specs) + [ANY] * nx, out_specs=list(out_specs) + [ANY] * nx,
        out_shape=list(out_shape) + shapes, scratch_shapes=list(scratch_shapes) + sems,
        compiler_params=_params(("arbitrary",) * len(grid)))(*args, *xs)


def _sibling_exchange(xs, name):
    n = len(xs)

    def body(*refs):
        x_refs, o_refs = refs[:n], refs[n:2 * n]
        send_sems, recv_sems = refs[2 * n:]
        sib = (lax.axis_index("x"), lax.axis_index("y"), 1 - lax.axis_index("c"))
        copies = [pltpu.make_async_remote_copy(
            src_ref=x_refs[a], dst_ref=o_refs[a], send_sem=send_sems.at[a], recv_sem=recv_sems.at[a],
            device_id=sib, device_id_type=MESH) for a in range(n)]
        for cp in copies:
            cp.start()
        for cp in copies:
            cp.wait()

    return pl.pallas_call(
        body, name=name, in_specs=[ANY] * n, out_specs=[ANY] * n,
        out_shape=[jax.ShapeDtypeStruct(x.shape, x.dtype) for x in xs],
        scratch_shapes=[pltpu.SemaphoreType.DMA((n,)), pltpu.SemaphoreType.DMA((n,))],
    )(*xs)


PACK = 1024
PACK_ROWS = 512


def _pack(parts):
    flat = []
    for p in parts:
        v = p.reshape(-1).astype(F32)
        flat.append(jnp.pad(v, (0, (-v.shape[0]) % PACK)))
    total = sum(v.shape[0] for v in flat)
    flat.append(jnp.zeros(((-total) % (PACK_ROWS * 128),), F32))
    return jnp.concatenate(flat).reshape(-1, 128)


def _unpack_rows(gathered, shapes):
    flat = gathered.reshape(gathered.shape[0], -1)
    out, off = [], 0
    for shp in shapes:
        n = math.prod(shp)
        out.append(flat[:, off:off + n].reshape((flat.shape[0],) + tuple(shp)))
        off += n + (-n) % PACK
    return out


def _unpack(packed, shapes):
    flat = packed.reshape(-1)
    out, off = [], 0
    for shp in shapes:
        n = math.prod(shp)
        out.append(flat[off:off + n].reshape(shp))
        off += n + (-n) % PACK
    return out


def kernel(x, c, ln_pre_g, ln_post_g, w_mod, b_mod, w_in_ab, w_out_ab, sgu_norm_g, sgu_w, sgu_b, w_in_ssm, w_out_ssm, lam_re, lam_im, b_re, b_im, c_re, c_im, d_skip, log_dt, w_glu, b_glu, loss_target, m_ln_pre_g, m_ln_post_g, m_w_mod, m_b_mod, m_w_in_ab, m_w_out_ab, m_sgu_norm_g, m_sgu_w, m_sgu_b, m_w_in_ssm, m_w_out_ssm, m_lam_re, m_lam_im, m_b_re, m_b_im, m_c_re, m_c_im, m_d_skip, m_log_dt, m_w_glu, m_b_glu, v_ln_pre_g, v_ln_post_g, v_w_mod, v_b_mod, v_w_in_ab, v_w_out_ab, v_sgu_norm_g, v_sgu_w, v_sgu_b, v_w_in_ssm, v_w_out_ssm, v_lam_re, v_lam_im, v_b_re, v_b_im, v_c_re, v_c_im, v_d_skip, v_log_dt, v_w_glu, v_b_glu):
    given = dict(locals())
    mx, my, mc = lax.axis_index("x"), lax.axis_index("y"), lax.axis_index("c")
    me = 4 * mx + 2 * my + mc
    chip = 2 * mx + my

    _, l, d = x.shape
    x2, tgt = x[0], loss_target[0]
    n_in = w_in_ab.shape[2] * 4
    wa = wb = n_in // 7
    w = w_out_ssm.shape[1]
    g, p, cch = b_re.shape[1:]
    nmod = w_mod.shape[2]

    gw_in_ab = _gather_halves(_bf(w_in_ab[0]), "gather_w_in_ab")
    win_ab = jnp.concatenate([gw_in_ab[k] for k in range(4)], axis=1)
    later_shards = [_bf(w_out_ab[0]), _bf(w_in_ssm[0]), _bf(w_out_ssm[0]), _bf(w_glu[0]), d_skip, b_glu]

    cond = _silu_rows(c.reshape(d // 128, 128), "cond_silu")
    cond_all = _allgather8(cond, "gather_cond").reshape(8, d)
    b_shard = lax.dynamic_slice(b_mod, (0, chip * nmod), (2, nmod)).reshape(2, 1, nmod)
    cond_pad = jnp.pad(cond_all, ((0, MOD_ROWS - 8), (0, 0)))
    modp = _mod_fwd(cond_pad, w_mod, b_shard, "mod_fwd")[:, :8]
    modp_all = _allgather8(modp.reshape(16, nmod), "gather_mod").reshape(4, 2, 2, 8, nmod)
    mine = lax.dynamic_index_in_dim(lax.dynamic_index_in_dim(modp_all, mc, 1, False), me, 2, False)
    mod = mine.transpose(1, 0, 2).reshape(2, 3 * d)
    shift = [mod[a:a + 1, :d] for a in range(2)]
    scale = [mod[a:a + 1, d:2 * d] for a in range(2)]
    gate = [mod[a:a + 1, 2 * d:] for a in range(2)]
    pre_g = [ln_pre_g[a:a + 1] for a in range(2)]
    post_g = [ln_post_g[a:a + 1] for a in range(2)]

    sgu_w0, sgu_bt = sgu_w[0], sgu_b[0].T
    h0 = _pre_fwd(x2, pre_g[0], scale[0], shift[0], "pre0_fwd")
    w_gates = jnp.concatenate([win_ab[:, :3 * wa], win_ab[:, 3 * wa + 3 * wb:]], axis=1)
    proj0 = _matmul(h0, w_gates, "nn", F32, "proj0", tm=1024)
    qkv = _matmul(h0, win_ab[:, 3 * wa:3 * wa + 3 * wb], "nn", BF16, "proj0_qkv", tm=1024)
    out_b, lktot, (gw_out_ab, gw_in_ssm, gw_out_ssm, gw_glu, g_dskip, g_bglu) = _attn_fwd(
        qkv, wb, "attn_fwd", hp=8, ride=[_ride_chip(later_shards, True)])
    wout_ab = gw_out_ab.reshape(wa + wb, d)
    win_ssm = gw_in_ssm.reshape(d, 2 * w)
    wout_ssm = jnp.concatenate([gw_out_ssm[k] for k in range(4)], axis=1)
    wglu = gw_glu.reshape(w, w)
    dskip_full = g_dskip.reshape(1, w)
    bglu_full = g_bglu.reshape(1, w)
    cat =_sgu_fwd(proj0, out_b, sgu_norm_g, sgu_w0, sgu_bt, wa, wb, "sgu_fwd")
    y0 = _matmul(cat, wout_ab, "nn", F32, "out0", tm=1024)
    x1, h1 = _post_pre_fwd(x2, y0, gate[0], post_g[0], pre_g[1], scale[1], shift[1], "post0_pre1_fwd")

    s = g * p
    lr_c, li_c = lam_re.reshape(s, 1), lam_im.reshape(s, 1)
    ldt_c = jnp.repeat(log_dt.reshape(g), p).reshape(s, 1)
    br_c, bi_c = b_re.reshape(s, cch), b_im.reshape(s, cch)
    bb_re, bb_im, pw_re, pw_im = _ssm_prep(lr_c, li_c, ldt_c, br_c, bi_c, lr_c.reshape(1, s), li_c.reshape(1, s),
                                           ldt_c.reshape(1, s), "ssm_prep")
    bbd = _bf(_block_diag_b(bb_re, bb_im, g, p, cch))
    ccd = _bf(_block_diag_c(c_re[0], c_im[0], g, p, cch))
    proj1 = _matmul(h1, win_ssm, "nn", F32, "proj1", tm=1024)
    y_ssm, hs_re, hs_im, h_all = _ssm_fwd(proj1, bbd, ccd, pw_re, pw_im, dskip_full, w, "ssm_fwd")
    o1 = _glu_fwd(y_ssm, proj1, wglu, bglu_full, "glu_fwd")
    y1 = _matmul(o1, wout_ssm, "nn", F32, "out1", tm=1024)
    loss_vec, dy1, dx2, dgate1, dpost1 = _post_loss(x1, y1, gate[1], post_g[1], tgt, "post1_loss")

    do1 = _matmul(dy1, wout_ssm, "nt", F32, "out1_dx", tm=1024)
    gr_wout_ssm = _matmul(o1, dy1, "tn", BF16, "out1_dw", tm=1024, tk=1024, n_split=4)
    dy_ssm, dz1, gr_wglu, gr_bglu = _glu_bwd(do1, y_ssm, proj1, wglu, bglu_full, "glu_bwd")
    du1, dbbd, dccd, da_re, da_im, gr_dskip, (ld_wout_ssm, ld_wglu) = _ssm_bwd(
        proj1, dy_ssm, hs_re, hs_im, h_all, bbd, ccd, pw_re, pw_im, dskip_full, w, "ssm_bwd",
        ride=[_ride_chip([gr_wout_ssm, gr_wglu.reshape(4, w // 4, w)], False)])
    dproj1 = jnp.concatenate([du1, dz1], axis=1)
    dh1 = _matmul(dproj1, win_ssm, "nt", F32, "proj1_dx", tm=1024)
    gr_win_ssm = _matmul(h1, dproj1, "tn", BF16, "proj1_dw", tm=1024, tn=1024, tk=1024)
    dx1, dscale1, dshift1, dpre1, dy0, dgate0, dpost0 = _pre_bwd(
        dh1, dx2, x1, pre_g[1], scale[1], "pre1_post0_bwd", post=(y0, gate[0], post_g[0]))

    dcat = _matmul(dy0, wout_ab, "nt", F32, "out0_dx", tm=1024)
    gr_wout_ab = _matmul(cat, dy0, "tn", BF16, "out0_dw", tm=1024, tn=1024, tk=1024)
    dbb_re, dbb_im = _diag_of_b(dbbd, g, p, cch)
    dc_re, dc_im = _diag_of_c(dccd, g, p, cch)
    part_a = [loss_vec[:, :1], dpre1, dpost0, dpost1, dgate0, dshift1, dscale1, dgate1, da_re, da_im,
              dbb_re, dbb_im, dc_re, dc_im, gr_dskip, gr_bglu]
    shapes_a = [a.shape for a in part_a]
    dq, dk, dv, (ld_win_ssm, ld_wout_ab, gath_a) = _attn_bwd(
        qkv, proj0, dcat, lktot, wa, wb, "attn_bwd", hp=4,
        ride=[_ride_chip([gr_win_ssm.reshape(4, d // 4, 2 * w), gr_wout_ab.reshape(4, (wa + wb) // 4, d)], False),
              _ride_all8(_pack(part_a))])
    dproj0, gr_sgu_w, gr_sgu_bt, gr_sgu_g = _sgu_bwd(proj0, out_b, dcat, dq, dk, dv, sgu_norm_g, sgu_w0, sgu_bt,
                                                     wa, wb, "sgu_bwd")
    part_b = [gr_sgu_g, gr_sgu_w, gr_sgu_bt.T]
    shapes_b = [a.shape for a in part_b]
    gr_win_ab_lo, (gath_b,) = _matmul(h0, dproj0, "tn", BF16, "proj0_dw_lo", tm=1024, tk=1024, tn=896, n_split=4,
                                      m_part=(0, 1, 4), ride=[_ride_all8(_pack(part_b))])
    gr_win_ab_hi, (ld_win_ab_lo,) = _matmul(
        h0, dproj0, "tn", BF16, "proj0_dw_hi", tm=1024, tk=1024, tn=896, n_split=4, m_part=(1, 3, 4),
        ride=[_ride_chip([gr_win_ab_lo], False)])
    dh0, (ld_win_ab_hi,) = _matmul(dproj0, win_ab, "nt", F32, "proj0_dx", tm=1024, tk=1792,
                                   ride=[_ride_chip([gr_win_ab_hi], False)])
    grad_x, dscale0, dshift0, dpre0 = _pre_bwd(dh0, dx1, x2, pre_g[0], scale[0], "pre0_bwd")
    part_c = [dpre0, dshift0, dscale0]
    shapes_c = [a.shape for a in part_c]
    gath_c = _allgather8(_pack(part_c), "gather_small_tail")

    landed = [ld_wout_ab, ld_win_ssm, ld_wout_ssm, ld_wglu]
    big_names = ["w_in_ab", "w_out_ab", "w_in_ssm", "w_out_ssm", "w_glu"]
    sums = [jnp.concatenate([_sum_leading(ld_win_ab_lo, "sum_w_in_ab_lo"), _sum_leading(ld_win_ab_hi, "sum_w_in_ab_hi")],
                            axis=0)]
    sums += [_sum_leading(a, "sum_" + nm) for a, nm in zip(landed, big_names[1:])]
    sib = _sibling_exchange(sums, "sibling_grads")
    results = {}
    for nm, s_mine, s_sib in zip(big_names, sums, sib):
        shp = given[nm].shape
        two_d = lambda a: a.reshape(-1, shp[-1])
        outs = _adamw(two_d(given[nm]), [s_mine, s_sib], two_d(given["m_" + nm]), two_d(given["v_" + nm]),
                      "adamw_" + nm)
        results[nm] = [o.reshape(shp) for o in outs]

    (loss_s, g_pre1, g_post0, g_post1, g_gate0, g_shift1, g_scale1, g_gate1, s_da_re, s_da_im, s_dbb_re, s_dbb_im,
     g_c_re, g_c_im, g_dskip_full, g_bglu_full) = _unpack(_sum_leading(gath_a, "sum_small_a"), shapes_a)
    g_sgu_g, g_sgu_w, g_sgu_b = _unpack(_sum_leading(gath_b, "sum_small_b"), shapes_b)
    g_pre0, g_shift0, g_scale0 = _unpack(_sum_leading(gath_c, "sum_small_c"), shapes_c)
    loss = loss_s.reshape(())
    g_pre = jnp.concatenate([g_pre0, g_pre1], axis=0)
    g_post = jnp.concatenate([g_post0, g_post1], axis=0)
    g_bmod = jnp.concatenate([jnp.concatenate([g_shift0, g_scale0, g_gate0], axis=1),
                              jnp.concatenate([g_shift1, g_scale1, g_gate1], axis=1)], axis=0)

    g_lr, g_li, g_ldt, g_br, g_bi = _ssm_prep_bwd(lr_c, li_c, ldt_c, br_c, bi_c, s_da_re.reshape(s, 1),
                                                  s_da_im.reshape(s, 1), s_dbb_re, s_dbb_im, p, "ssm_prep_bwd")
    small = {
        "ln_pre_g": g_pre, "ln_post_g": g_post, "b_mod": g_bmod, "sgu_norm_g": g_sgu_g,
        "sgu_w": g_sgu_w.reshape(sgu_w.shape), "sgu_b": g_sgu_b.reshape(sgu_b.shape),
        "lam_re": g_lr.reshape(lam_re.shape), "lam_im": g_li.reshape(lam_im.shape),
        "b_re": g_br.reshape(b_re.shape), "b_im": g_bi.reshape(b_im.shape),
        "c_re": g_c_re.reshape(c_re.shape), "c_im": g_c_im.reshape(c_im.shape),
        "d_skip": lax.dynamic_slice(g_dskip_full, (0, chip * (w // 4)), (1, w // 4)),
        "log_dt": g_ldt.reshape(log_dt.shape),
        "b_glu": lax.dynamic_slice(g_bglu_full, (0, chip * (w // 4)), (1, w // 4)),
    }
    flat2 = lambda a: a.reshape(-1, a.shape[-1])
    wide = ("b_re", "b_im")
    for tag, group in (("adamw_small", [nm for nm in small if nm not in wide]), ("adamw_small_b", list(wide))):
        outs = _adamw_many([flat2(given[nm]) for nm in group], [flat2(small[nm]) for nm in group],
                           [flat2(given["m_" + nm]) for nm in group], [flat2(given["v_" + nm]) for nm in group], tag)
        for nm, trio in zip(group, outs):
            results[nm] = [small[nm]] + [o.reshape(given[nm].shape) for o in trio]

    rows_a = _unpack_rows(gath_a, shapes_a)
    rows_c = _unpack_rows(gath_c, shapes_c)
    dmod_rows = jnp.concatenate([rows_c[1], rows_c[2], rows_a[4], rows_a[5], rows_a[6], rows_a[7]],
                                axis=2).reshape(8, 2, 3 * d)
    dmod_shard = lax.dynamic_slice(dmod_rows, (0, 0, chip * nmod), (8, 2, nmod)).transpose(1, 0, 2)
    dmod_pad = jnp.pad(dmod_shard, ((0, 0), (0, MOD_ROWS - 8), (0, 0)))
    gr_wmod = _mod_bwd(cond_pad.T, dmod_pad, "mod_bwd")
    two_d = lambda a: a.reshape(-1, nmod)
    outs = _adamw(two_d(w_mod), [two_d(gr_wmod)], two_d(m_w_mod), two_d(v_w_mod), "adamw_w_mod")
    results["w_mod"] = [o.reshape(w_mod.shape) for o in outs]

    names = ["ln_pre_g", "ln_post_g", "w_mod", "b_mod", "w_in_ab", "w_out_ab", "sgu_norm_g", "sgu_w", "sgu_b",
             "w_in_ssm", "w_out_ssm", "lam_re", "lam_im", "b_re", "b_im", "c_re", "c_im", "d_skip", "log_dt",
             "w_glu", "b_glu"]
    return (loss, grad_x[None], *[results[nm][0] for nm in names], *[results[nm][1] for nm in names],
            *[results[nm][2] for nm in names], *[results[nm][3] for nm in names])
```

```python
import functools
import math

import jax
import jax.numpy as jnp
from jax import lax
from jax.experimental import pallas as pl
from jax.experimental.pallas import tpu as pltpu

F32 = jnp.float32
BF16 = jnp.bfloat16
MESH = pl.DeviceIdType.MESH

EPS = 1e-6
HEAD = 128
SSM_T = 512
SSM_GB = 16
ADAM_LR, ADAM_B1, ADAM_B2, ADAM_EPS, ADAM_WD, ADAM_STEP = 0.001, 0.9, 0.999, 1e-08, 0.01, 10
VMEM_LIMIT = 56 * 1024 * 1024

NN = (((1,), (0,)), ((), ()))
NT = (((1,), (1,)), ((), ()))
TN = (((0,), (0,)), ((), ()))


def _params(sem=None):
    return pltpu.CompilerParams(dimension_semantics=sem, vmem_limit_bytes=VMEM_LIMIT)


def _dot(a, b, dims=NN):
    return lax.dot_general(a, b, dims, preferred_element_type=F32)


def _bf(x):
    return x.astype(BF16)


def _gelu(x):
    k = math.sqrt(2.0 / math.pi)
    t = jnp.tanh(k * (x + 0.044715 * x * x * x))
    return 0.5 * x * (1.0 + t)


def _gelu_grad(x):
    k = math.sqrt(2.0 / math.pi)
    x2 = x * x
    t = jnp.tanh(k * (x + 0.044715 * x * x2))
    return 0.5 * (1.0 + t) + 0.5 * x * (1.0 - t * t) * k * (1.0 + 3.0 * 0.044715 * x2)


def _sigmoid(x):
    return 1.0 / (1.0 + jnp.exp(-x))


def _silu(x):
    return x * _sigmoid(x)


def _silu_grad(x):
    s = _sigmoid(x)
    return s * (1.0 + x * (1.0 - s))


def _tile(n, t, mult=128):
    if n <= t:
        return n
    for cand in range(t - t % mult, 0, -mult):
        if n % cand == 0:
            return cand
    raise ValueError((n, t, mult))


def _matmul(a, b, mode, out_dtype, name, tm=512, tn=512, tk=2048, n_split=1, ride=None, m_part=None):
    if mode == "nn":
        (m, kk), (_, n) = a.shape, b.shape
    elif mode == "nt":
        (m, kk), (n, _) = a.shape, b.shape
    else:
        (kk, m), (_, n) = a.shape, b.shape
    m_off = 0
    if m_part is not None:
        assert mode == "tn"
        first, count, parts = m_part
        tm = _tile(m // parts, tm)
        m_off = first * (m // parts) // tm
        m = count * (m // parts)
    tm, tk = _tile(m, tm), _tile(kk, tk)
    ns = n // n_split
    tn = _tile(ns, tn)
    nk = kk // tk
    dims = {"nn": NN, "nt": NT, "tn": TN}[mode]

    def body(a_ref, b_ref, o_ref, acc_ref):
        k = pl.program_id(2)
        part = _dot(_bf(a_ref[...]), _bf(b_ref[...]), dims)

        @pl.when(k == 0)
        def _():
            acc_ref[...] = part

        @pl.when(k > 0)
        def _():
            acc_ref[...] += part

        @pl.when(k == nk - 1)
        def _():
            o_ref[...] = acc_ref[...].astype(out_dtype).reshape(o_ref.shape)

    if mode == "nn":
        a_spec = pl.BlockSpec((tm, tk), lambda i, j, k: (i, k))
        b_spec = pl.BlockSpec((tk, tn), lambda i, j, k: (k, j))
    elif mode == "nt":
        a_spec = pl.BlockSpec((tm, tk), lambda i, j, k: (i, k))
        b_spec = pl.BlockSpec((tn, tk), lambda i, j, k: (j, k))
    else:
        a_spec = pl.BlockSpec((tk, tm), lambda i, j, k: (k, i + m_off))
        b_spec = pl.BlockSpec((tk, tn), lambda i, j, k: (k, j))
    if n_split == 1:
        out_shape = jax.ShapeDtypeStruct((m, n), out_dtype)
        o_spec = pl.BlockSpec((tm, tn), lambda i, j, k: (i, j))
    else:
        per = ns // tn
        out_shape = jax.ShapeDtypeStruct((n_split, m, ns), out_dtype)
        o_spec = pl.BlockSpec((1, tm, tn), lambda i, j, k: (j // per, i, j % per))
    outs = _call(body, name=name, grid=(m // tm, n // tn, nk), in_specs=[a_spec, b_spec], out_specs=[o_spec],
                 out_shape=[out_shape], scratch_shapes=[pltpu.VMEM((tm, tn), F32)], args=(a, b),
                 sem=("parallel", "parallel", "arbitrary"), ride=ride)
    return outs[0] if ride is None else (outs[0], outs[1:])


def _row_spec(tm, d):
    return pl.BlockSpec((tm, d), lambda i: (i, 0))


def _vec_spec(d):
    return pl.BlockSpec((1, d), lambda i: (0, 0))


def _acc(ref, first, val):
    @pl.when(first)
    def _():
        ref[...] = val

    @pl.when(jnp.logical_not(first))
    def _():
        ref[...] += val


def _colsum(x):
    return jnp.sum(x, axis=0, keepdims=True)


def _rownorm(x):
    r = lax.rsqrt(jnp.mean(x * x, axis=-1, keepdims=True) + EPS)
    return x * r, r


def _pre_fwd(x, g, scale, shift, name):
    l, d = x.shape
    tm = _tile(l, 256)

    def body(x_ref, g_ref, sc_ref, sh_ref, h_ref):
        n, _ = _rownorm(x_ref[...])
        h_ref[...] = _bf(n * g_ref[...] * (1.0 + sc_ref[...]) + sh_ref[...])

    return pl.pallas_call(
        body, name=name, grid=(l // tm,),
        in_specs=[_row_spec(tm, d), _vec_spec(d), _vec_spec(d), _vec_spec(d)],
        out_specs=_row_spec(tm, d), out_shape=jax.ShapeDtypeStruct((l, d), BF16),
        compiler_params=_params(("parallel",)),
    )(x, g, scale, shift)


def _post_pre_fwd(x, y, gate, pg, g1, scale1, shift1, name):
    l, d = x.shape
    tm = _tile(l, 256)

    def body(x_ref, y_ref, gate_ref, pg_ref, g1_ref, sc_ref, sh_ref, x1_ref, h1_ref):
        ny, _ = _rownorm(y_ref[...])
        x1 = x_ref[...] + gate_ref[...] * (ny * pg_ref[...])
        x1_ref[...] = x1
        n1, _ = _rownorm(x1)
        h1_ref[...] = _bf(n1 * g1_ref[...] * (1.0 + sc_ref[...]) + sh_ref[...])

    v = _vec_spec(d)
    return pl.pallas_call(
        body, name=name, grid=(l // tm,),
        in_specs=[_row_spec(tm, d), _row_spec(tm, d), v, v, v, v, v],
        out_specs=[_row_spec(tm, d), _row_spec(tm, d)],
        out_shape=[jax.ShapeDtypeStruct((l, d), F32), jax.ShapeDtypeStruct((l, d), BF16)],
        compiler_params=_params(("parallel",)),
    )(x, y, gate, pg, g1, scale1, shift1)


def _post_loss(x1, y1, gate, pg, target, name):
    l, d = x1.shape
    tm = _tile(l, 256)

    def body(x_ref, y_ref, gate_ref, pg_ref, t_ref, loss_ref, dy_ref, dx_ref, dgate_ref, dpg_ref):
        first = pl.program_id(0) == 0
        y = y_ref[...]
        ny, ry = _rownorm(y)
        q = ny * pg_ref[...]
        x2 = x_ref[...] + gate_ref[...] * q
        e = x2 - t_ref[...]
        _acc(loss_ref, first, jnp.full((1, 128), 0.5 / d, F32) * jnp.sum(e * e))
        dx2 = e * (1.0 / d)
        dx_ref[...] = dx2
        _acc(dgate_ref, first, _colsum(dx2 * q))
        dq = dx2 * gate_ref[...]
        _acc(dpg_ref, first, _colsum(dq * ny))
        dny = dq * pg_ref[...]
        dy = ry * (dny - ny * jnp.mean(dny * ny, axis=-1, keepdims=True))
        dy_ref[...] = _bf(dy)

    v = _vec_spec(d)
    return pl.pallas_call(
        body, name=name, grid=(l // tm,),
        in_specs=[_row_spec(tm, d), _row_spec(tm, d), v, v, _row_spec(tm, d)],
        out_specs=[_vec_spec(128), _row_spec(tm, d), _row_spec(tm, d), v, v],
        out_shape=[jax.ShapeDtypeStruct((1, 128), F32), jax.ShapeDtypeStruct((l, d), BF16),
                   jax.ShapeDtypeStruct((l, d), F32), jax.ShapeDtypeStruct((1, d), F32),
                   jax.ShapeDtypeStruct((1, d), F32)],
        compiler_params=_params(("arbitrary",)),
    )(x1, y1, gate, pg, target)


def _pre_bwd(dh, dres, x, g, scale, name, post=None):
    l, d = x.shape
    tm = _tile(l, 256)
    with_post = post is not None

    def body(*refs):
        if with_post:
            (dh_ref, dres_ref, x_ref, g_ref, sc_ref, y_ref, gate_ref, pg_ref,
             dx_ref, dsc_ref, dsh_ref, dg_ref, dy_ref, dgate_ref, dpg_ref) = refs
        else:
            dh_ref, dres_ref, x_ref, g_ref, sc_ref, dx_ref, dsc_ref, dsh_ref, dg_ref = refs
        first = pl.program_id(0) == 0
        dh = dh_ref[...]
        n, r = _rownorm(x_ref[...])
        _acc(dsc_ref, first, _colsum(dh * (n * g_ref[...])))
        _acc(dsh_ref, first, _colsum(dh))
        dyn = dh * (1.0 + sc_ref[...])
        _acc(dg_ref, first, _colsum(dyn * n))
        dn = dyn * g_ref[...]
        dx = dres_ref[...] + r * (dn - n * jnp.mean(dn * n, axis=-1, keepdims=True))
        dx_ref[...] = dx
        if with_post:
            ny, ry = _rownorm(y_ref[...])
            _acc(dgate_ref, first, _colsum(dx * (ny * pg_ref[...])))
            dq = dx * gate_ref[...]
            _acc(dpg_ref, first, _colsum(dq * ny))
            dny = dq * pg_ref[...]
            dy_ref[...] = _bf(ry * (dny - ny * jnp.mean(dny * ny, axis=-1, keepdims=True)))

    v = _vec_spec(d)
    row = _row_spec(tm, d)
    vec_out = jax.ShapeDtypeStruct((1, d), F32)
    in_specs = [row, row, row, v, v]
    args = [dh, dres, x, g, scale]
    out_specs = [row, v, v, v]
    out_shape = [jax.ShapeDtypeStruct((l, d), F32), vec_out, vec_out, vec_out]
    if with_post:
        in_specs += [row, v, v]
        args += list(post)
        out_specs += [row, v, v]
        out_shape += [jax.ShapeDtypeStruct((l, d), BF16), vec_out, vec_out]
    return pl.pallas_call(
        body, name=name, grid=(l // tm,), in_specs=in_specs, out_specs=out_specs, out_shape=out_shape,
        compiler_params=_params(("arbitrary",)),
    )(*args)


def _softplus_parts(z):
    e = jnp.exp(-jnp.abs(z))
    den = 1.0 + e
    lb = jnp.minimum(z, 0.0) - jnp.log(den)
    sig = jnp.where(z >= 0.0, 1.0, e) * pl.reciprocal(den, approx=True)
    return lb, lb - z, sig


def _tri(cmp, n=HEAD):
    row = lax.broadcasted_iota(jnp.int32, (n, n), 0)
    col = lax.broadcasted_iota(jnp.int32, (n, n), 1)
    return cmp(row, col)


ATT_T = 256


def _attn_fwd(qkv, wb, name, hp=4, ride=None):
    l = qkv.shape[0]
    t = ATT_T
    nh, nq = wb // HEAD, l // t
    hp = min(hp, nh)
    ng, wg = nh // hp, hp * HEAD
    scale = 1.0 / math.sqrt(HEAD)

    def body(q_ref, k_ref, v_ref, o_ref, lk_ref):
        i = pl.program_id(1)
        valid = _tri(lambda r, c: c < r, t)
        m_gt = _bf(_tri(lambda r, c: r > c, t).astype(F32))

        def tile(j, carry, diag):
            rows = pl.ds(pl.multiple_of(j * t, t), t)
            cols = [slice(hh * HEAD, (hh + 1) * HEAD) for hh in range(hp)]
            zs = [_dot(q_ref[:, cs], k_ref[rows, cs], NT) * scale for cs in cols]
            lbs, lks = [], []
            for z in zs:
                lb, lk, _ = _softplus_parts(z)
                lbs.append(lb)
                lks.append(jnp.where(valid, lk, 0.0) if diag else lk)
            laters = [_dot(_bf(lk), m_gt) for lk in lks]
            ws = [jnp.exp(lb + later + run) for lb, later, (_, run) in zip(lbs, laters, carry)]
            if diag:
                ws = [jnp.where(valid, w, 0.0) for w in ws]
            return tuple((acc + _dot(_bf(w), v_ref[rows, cs]), run + jnp.sum(lk, axis=1, keepdims=True))
                         for w, lk, cs, (acc, run) in zip(ws, lks, cols, carry))

        zero = (jnp.zeros((t, HEAD), F32), jnp.zeros((t, 1), F32))
        carry = tile(i, (zero,) * hp, True)
        carry = lax.fori_loop(0, i, lambda s, c: tile(i - 1 - s, c, False), carry)
        for hh, (acc, run) in enumerate(carry):
            cs = slice(hh * HEAD, (hh + 1) * HEAD)
            o_ref[:, cs] = acc
            lk_ref[:, cs] = jnp.broadcast_to(run, (t, HEAD))

    blk = lambda off: pl.BlockSpec((t, wg), lambda h, i: (i, off + h))
    full = lambda off: pl.BlockSpec((l, wg), lambda h, i: (0, off + h))
    out = pl.BlockSpec((t, wg), lambda h, i: (i, h))
    outs = _call(body, name=name, grid=(ng, nq), in_specs=[blk(0), full(ng), full(2 * ng)], out_specs=[out, out],
                 out_shape=[jax.ShapeDtypeStruct((l, wb), F32), jax.ShapeDtypeStruct((l, wb), F32)],
                 args=(qkv, qkv, qkv), sem=("parallel", "arbitrary"), ride=ride)
    return outs[0], outs[1], outs[2:]


def _attn_bwd(qkv, proj, dcat, lktot, wa, wb, name, hp=2, ride=None):
    l = qkv.shape[0]
    t = ATT_T
    nh, nq = wb // HEAD, l // t
    hp = min(hp, nh)
    ng, wg = nh // hp, hp * HEAD
    scale = 1.0 / math.sqrt(HEAD)

    def body(q_ref, k_ref, v_ref, bz_ref, dc_ref, lt_ref, dq_ref, dkt_out, dvt_out, do_s, qt_s, dot_s,
             dkt_ref, dvt_ref, out_sems):
        i = pl.program_id(1)

        @pl.when(i == 0)
        def _():
            dkt_ref[...] = jnp.zeros_like(dkt_ref)
            dvt_ref[...] = jnp.zeros_like(dvt_ref)

        do = dc_ref[...] * _silu(bz_ref[...])
        do_s[...] = _bf(do)
        for hh in range(hp):
            cs = slice(hh * HEAD, (hh + 1) * HEAD)
            qt_s[hh] = _bf(q_ref[:, cs].astype(F32).T * scale)
            dot_s[hh] = _bf(do[:, cs].T)
        valid = _tri(lambda r, c: c < r, t)
        m_le = _bf(_tri(lambda r, c: r <= c, t).astype(F32))
        m_lt = _bf(_tri(lambda r, c: r < c, t).astype(F32))

        def tile(j, carry, diag):
            rows = pl.ds(pl.multiple_of(j * t, t), t)
            heads = range(hp)
            cols = [slice(hh * HEAD, (hh + 1) * HEAD) for hh in heads]
            zs = [_dot(q_ref[:, cs], k_ref[rows, cs], NT) * scale for cs in cols]
            dws = [_dot(do_s[:, cs], v_ref[rows, cs], NT) for cs in cols]
            lbs, lks, sigs = [], [], []
            for z in zs:
                lb, lk, sig = _softplus_parts(z)
                lbs.append(lb)
                lks.append(jnp.where(valid, lk, 0.0) if diag else lk)
                sigs.append(sig)
            pins = [_dot(_bf(lk), m_le) for lk in lks]
            ws = [jnp.exp(lbs[hh] + (lt_ref[:, hh * HEAD:hh * HEAD + 1] - carry[hh][1]) - pins[hh]) for hh in heads]
            if diag:
                ws = [jnp.where(valid, w, 0.0) for w in ws]
            das = [dw * w for dw, w in zip(dws, ws)]
            pexs = [_dot(_bf(da), m_lt) for da in das]
            dzs = [das[hh] - sigs[hh] * (das[hh] + carry[hh][2] + pexs[hh]) for hh in heads]
            if diag:
                dzs = [jnp.where(valid, dz, 0.0) for dz in dzs]
            dzs = [_bf(dz) for dz in dzs]
            out = []
            for hh in heads:
                dkt, dvt = _dot(qt_s[hh], dzs[hh]), _dot(dot_s[hh], _bf(ws[hh]))
                for half in range(t // HEAD):
                    dkt_ref[hh, sub * j + half] += dkt[:, half * HEAD:(half + 1) * HEAD]
                    dvt_ref[hh, sub * j + half] += dvt[:, half * HEAD:(half + 1) * HEAD]
                dq, cpre, ppre = carry[hh]
                out.append((dq + _dot(dzs[hh], k_ref[rows, cols[hh]]), cpre + jnp.sum(lks[hh], axis=1, keepdims=True),
                            ppre + pexs[hh][:, t - 1:] + das[hh][:, t - 1:]))
            return tuple(out)

        zero = (jnp.zeros((t, HEAD), F32), jnp.zeros((t, 1), F32), jnp.zeros((t, 1), F32))
        carry = lax.fori_loop(0, i, lambda j, c: tile(j, c, False), (zero,) * hp)
        carry = tile(i, carry, True)
        for hh in range(hp):
            dq_ref[:, hh * HEAD:(hh + 1) * HEAD] = carry[hh][0] * scale

        @pl.when(i == nq - 1)
        def _():
            heads = pl.ds(pl.program_id(0) * hp, hp)
            copies = [pltpu.make_async_copy(dkt_ref, dkt_out.at[heads], out_sems.at[0]),
                      pltpu.make_async_copy(dvt_ref, dvt_out.at[heads], out_sems.at[1])]
            for cp in copies:
                cp.start()
            for cp in copies:
                cp.wait()

    sub = t // HEAD
    blk = lambda off: pl.BlockSpec((t, wg), lambda h, i: (i, off + h))
    full = lambda off: pl.BlockSpec((l, wg), lambda h, i: (0, off + h))
    acc_shape = jax.ShapeDtypeStruct((nh, l // HEAD, HEAD, HEAD), F32)
    acc_scratch = pltpu.VMEM((hp, l // HEAD, HEAD, HEAD), F32)
    outs = _call(
        body, name=name, grid=(ng, nq),
        in_specs=[blk(0), full(ng), full(2 * ng), blk(3 * wa // wg), blk(wa // wg), blk(0)],
        out_specs=[blk(0), ANY, ANY], out_shape=[jax.ShapeDtypeStruct((l, wb), F32), acc_shape, acc_shape],
        scratch_shapes=[pltpu.VMEM((t, wg), BF16), pltpu.VMEM((hp, HEAD, t), BF16), pltpu.VMEM((hp, HEAD, t), BF16),
                        acc_scratch, acc_scratch, pltpu.SemaphoreType.DMA((2,))],
        args=(qkv, qkv, qkv, proj, dcat, lktot), sem=("parallel", "arbitrary"), ride=ride)
    return outs[0], outs[1], outs[2], outs[3:]


def _sgu_heads(v, g_ref, w_ref, bt_ref, nh):
    keep = _tri(lambda r, c: r >= c)
    out = []
    for h in range(nh):
        cols = slice(h * HEAD, (h + 1) * HEAD)
        nv, r = _rownorm(v[:, cols])
        wm = jnp.where(keep, w_ref[h], 0.0)
        s = _dot(_bf(wm), _bf(nv * g_ref[:, cols])) + bt_ref[:, h:h + 1]
        out.append((nv, r, wm, s))
    return out


def _sgu_fwd(proj, out_b, norm_g, sgu_w, sgu_bt, wa, wb, name):
    l, n = proj.shape
    nh = wa // HEAD

    def body(au_ref, av_ref, az_ref, bz_ref, ob_ref, g_ref, w_ref, bt_ref, cat_ref):
        u, v, sz = _gelu(au_ref[...]), _gelu(av_ref[...]), _silu(az_ref[...])
        for h, (_, _, _, s) in enumerate(_sgu_heads(v, g_ref, w_ref, bt_ref, nh)):
            cols = slice(h * HEAD, (h + 1) * HEAD)
            cat_ref[:, cols] = _bf(u[:, cols] * s * sz[:, cols])
        cat_ref[:, wa:] = _bf(ob_ref[...] * _silu(bz_ref[...]))

    a_blk = lambda j: pl.BlockSpec((HEAD, wa), lambda i: (i, j))
    return pl.pallas_call(
        body, name=name, grid=(l // HEAD,),
        in_specs=[a_blk(0), a_blk(1), a_blk(2), a_blk(3), pl.BlockSpec((HEAD, wb), lambda i: (i, 0)),
                  _vec_spec(wa), pl.BlockSpec((nh, HEAD, HEAD), lambda i: (0, 0, 0)),
                  pl.BlockSpec((HEAD, nh), lambda i: (0, 0))],
        out_specs=pl.BlockSpec((HEAD, wa + wb), lambda i: (i, 0)),
        out_shape=jax.ShapeDtypeStruct((l, wa + wb), BF16),
        compiler_params=_params(("parallel",)),
    )(proj, proj, proj, proj, out_b, norm_g, sgu_w, sgu_bt)


def _sgu_bwd(proj, out_b, dcat, dq, dk, dv, norm_g, sgu_w, sgu_bt, wa, wb, name):
    l = proj.shape[0]
    n = 3 * wa + 4 * wb
    nh = wa // HEAD

    def body(au_ref, av_ref, az_ref, bz_ref, ob_ref, dc_ref, dq_ref, dk_ref, dv_ref, g_ref, w_ref, wt_ref, bt_ref,
             dp_ref, dw_ref, dbt_ref, dg_ref):
        first = pl.program_id(0) == 0
        keep = _tri(lambda r, c: r >= c)
        au, av, az = au_ref[...], av_ref[...], az_ref[...]
        u, v, sz = _gelu(au), _gelu(av), _silu(az)
        dgelu_u, dgelu_v, dsilu_z = _gelu_grad(au), _gelu_grad(av), _silu_grad(az)
        heads = _sgu_heads(v, g_ref, w_ref, bt_ref, nh)
        cols = [slice(h * HEAD, (h + 1) * HEAD) for h in range(nh)]
        dss = []
        for h, (nv, r, wm, s) in enumerate(heads):
            dca, uh, szh = dc_ref[:, cols[h]], u[:, cols[h]], sz[:, cols[h]]
            dp_ref[:, cols[h]] = _bf(dca * s * szh * dgelu_u[:, cols[h]])
            dp_ref[:, 2 * wa + h * HEAD:2 * wa + (h + 1) * HEAD] = _bf(dca * uh * s * dsilu_z[:, cols[h]])
            dss.append(dca * uh * szh)
        dws = [_dot(_bf(ds), _bf(nv * g_ref[:, cs]), NT) for ds, cs, (nv, _, _, _) in zip(dss, cols, heads)]
        keep_t = _tri(lambda r, c: r <= c)
        dvhs = [_dot(_bf(jnp.where(keep_t, wt_ref[h], 0.0)), _bf(dss[h])) for h in range(nh)]
        dg_parts = []
        for h, (nv, r, wm, s) in enumerate(heads):
            _acc(dw_ref.at[h], first, jnp.where(keep, dws[h], 0.0))
            _acc(dbt_ref.at[:, h:h + 1], first, jnp.sum(dss[h], axis=1, keepdims=True))
            dg_parts.append(_colsum(dvhs[h] * nv))
            dnv = dvhs[h] * g_ref[:, cols[h]]
            dvv = r * (dnv - nv * jnp.mean(dnv * nv, axis=-1, keepdims=True))
            dp_ref[:, wa + h * HEAD:wa + (h + 1) * HEAD] = _bf(dvv * dgelu_v[:, cols[h]])
        _acc(dg_ref, first, jnp.concatenate(dg_parts, axis=1))
        base = 3 * wa
        dp_ref[:, base:base + wb] = _bf(dq_ref[...])
        for h in range(wb // HEAD):
            dp_ref[:, base + wb + h * HEAD:base + wb + (h + 1) * HEAD] = _bf(dk_ref[h, 0].T)
            dp_ref[:, base + 2 * wb + h * HEAD:base + 2 * wb + (h + 1) * HEAD] = _bf(dv_ref[h, 0].T)
        dp_ref[:, base + 3 * wb:] = _bf(dc_ref[:, wa:] * ob_ref[...] * _silu_grad(bz_ref[...]))

    a_blk = lambda j: pl.BlockSpec((HEAD, wa), lambda i: (i, j))
    b_blk = pl.BlockSpec((HEAD, wb), lambda i: (i, 0))
    t_blk = pl.BlockSpec((wb // HEAD, 1, HEAD, HEAD), lambda i: (0, i, 0, 0))
    w_spec = pl.BlockSpec((nh, HEAD, HEAD), lambda i: (0, 0, 0))
    bt_spec = pl.BlockSpec((HEAD, nh), lambda i: (0, 0))
    return pl.pallas_call(
        body, name=name, grid=(l // HEAD,),
        in_specs=[a_blk(0), a_blk(1), a_blk(2), a_blk(3), b_blk, pl.BlockSpec((HEAD, wa + wb), lambda i: (i, 0)),
                  b_blk, t_blk, t_blk, _vec_spec(wa), w_spec, w_spec, bt_spec],
        out_specs=[pl.BlockSpec((HEAD, n), lambda i: (i, 0)), w_spec, bt_spec, _vec_spec(wa)],
        out_shape=[jax.ShapeDtypeStruct((l, n), BF16), jax.ShapeDtypeStruct((nh, HEAD, HEAD), F32),
                   jax.ShapeDtypeStruct((HEAD, nh), F32), jax.ShapeDtypeStruct((1, wa), F32)],
        compiler_params=_params(("arbitrary",)),
    )(proj, proj, proj, proj, out_b, dcat, dq, dk, dv, norm_g, sgu_w, sgu_w.transpose(0, 2, 1), sgu_bt)


def _ssm_discretise(lr, li, ldt, br, bi):
    dt = jnp.exp(ldt)
    mag = jnp.exp(lr * dt)
    a_re = mag * jnp.cos(li * dt)
    a_im = mag * jnp.sin(li * dt)
    den = lr * lr + li * li
    nr = a_re - 1.0
    coef_re = (nr * lr + a_im * li) / den
    coef_im = (a_im * lr - nr * li) / den
    return a_re, a_im, coef_re * br - coef_im * bi, coef_re * bi + coef_im * br


def _ssm_prep(lr, li, ldt, br, bi, lr_row, li_row, ldt_row, name):
    s, c = br.shape

    def body(lr_ref, li_ref, ldt_ref, br_ref, bi_ref, lrr_ref, lir_ref, ldtr_ref, bbr_ref, bbi_ref, tr_ref, ti_ref):
        _, _, bbr, bbi = _ssm_discretise(lr_ref[...], li_ref[...], ldt_ref[...], br_ref[...], bi_ref[...])
        bbr_ref[...] = bbr
        bbi_ref[...] = bbi
        row = lax.broadcasted_iota(jnp.int32, (SCAN_ROWS, 1), 0)
        blk, r = jnp.right_shift(row, 3), jnp.bitwise_and(row, 7)
        kind, rev = jnp.bitwise_and(blk, 3), blk >= 4
        step = jnp.left_shift(1, kind)
        n = jnp.where(kind < 3, step, jnp.where(rev, 8 - r, r + 1)).astype(F32)
        keep = (kind == 3) | (rev & (r < 8 - step)) | (jnp.logical_not(rev) & (r >= step))
        dt = jnp.exp(ldtr_ref[...])
        mag = jnp.exp(n * (lrr_ref[...] * dt))
        ang = n * (lir_ref[...] * dt)
        tr_ref[...] = jnp.where(keep, mag * jnp.cos(ang), 0.0)
        ti_ref[...] = jnp.where(keep, jnp.where(rev, -1.0, 1.0) * mag * jnp.sin(ang), 0.0)

    col = jax.ShapeDtypeStruct((s, c), F32)
    row = jax.ShapeDtypeStruct((SCAN_ROWS, s), F32)
    return pl.pallas_call(body, name=name, out_shape=[col, col, row, row])(
        lr, li, ldt, br, bi, lr_row, li_row, ldt_row)


def _ssm_prep_bwd(lr, li, ldt, br, bi, da_re, da_im, dbb_re, dbb_im, p, name):
    s, c = br.shape

    def body(lr_ref, li_ref, ldt_ref, br_ref, bi_ref, dar_ref, dai_ref, dbr_ref, dbi_ref,
             dlr_ref, dli_ref, dldt_ref, dbre_ref, dbim_ref):
        args = (lr_ref[...], li_ref[...], ldt_ref[...], br_ref[...], bi_ref[...])
        _, vjp = jax.vjp(_ssm_discretise, *args)
        dlr, dli, dldt, dbr, dbi = vjp((dar_ref[...], dai_ref[...], dbr_ref[...], dbi_ref[...]))
        dlr_ref[...] = dlr
        dli_ref[...] = dli
        dbre_ref[...] = dbr
        dbim_ref[...] = dbi
        idx = lax.broadcasted_iota(jnp.int32, (s, s // p), 0)
        grp = lax.broadcasted_iota(jnp.int32, (s, s // p), 1)
        own = (idx >= grp * p) & (idx < (grp + 1) * p)
        dldt_ref[...] = _colsum(jnp.where(own, dldt, 0.0))

    col1 = jax.ShapeDtypeStruct((s, 1), F32)
    colc = jax.ShapeDtypeStruct((s, c), F32)
    return pl.pallas_call(
        body, name=name, out_shape=[col1, col1, jax.ShapeDtypeStruct((1, s // p), F32), colc, colc],
    )(lr, li, ldt, br, bi, da_re, da_im, dbb_re, dbb_im)


SCAN_ROWS = 64


def _scan_groups(xr, xi, tr_ref, ti_ref, cr, ci, reverse):
    ng = xr.shape[0] // 8
    base = SCAN_ROWS // 2 if reverse else 0
    pr, pi = tr_ref[base + 24:base + 32, :], ti_ref[base + 24:base + 32, :]
    edge = slice(0, 1) if reverse else slice(7, 8)
    out_r, out_i = [None] * ng, [None] * ng
    for g in (range(ng - 1, -1, -1) if reverse else range(ng)):
        sr, si = xr[8 * g:8 * g + 8, :], xi[8 * g:8 * g + 8, :]
        for k in range(3):
            ar, ai = tr_ref[base + 8 * k:base + 8 * k + 8, :], ti_ref[base + 8 * k:base + 8 * k + 8, :]
            shift = 8 - (1 << k) if reverse else 1 << k
            rr, ri = pltpu.roll(sr, shift, 0), pltpu.roll(si, shift, 0)
            sr, si = sr + ar * rr - ai * ri, si + ar * ri + ai * rr
        sr, si = sr + pr * cr - pi * ci, si + pr * ci + pi * cr
        cr, ci = sr[edge, :], si[edge, :]
        out_r[g], out_i[g] = sr, si
    return jnp.concatenate(out_r, axis=0), jnp.concatenate(out_i, axis=0), cr, ci


def _ssm_fwd(proj, bbd, ccd, pw_re, pw_im, d_skip, w, name):
    l = proj.shape[0]
    nb, cw, ns2 = bbd.shape
    ns = ns2 // 2
    nc = l // SSM_T

    def body(u_ref, bbd_ref, ccd_ref, pr_ref, pi_ref, d_ref, y_ref, hsr_ref, hsi_ref, h_ref, hr_s, hi_s):
        @pl.when(pl.program_id(1) == 0)
        def _():
            hr_s[...] = jnp.zeros_like(hr_s)
            hi_s[...] = jnp.zeros_like(hi_s)

        hsr_ref[...] = hr_s[...].reshape(hsr_ref.shape)
        hsi_ref[...] = hi_s[...].reshape(hsi_ref.shape)
        u = u_ref[...]
        bu = _dot(_bf(u), bbd_ref[0])
        hr, hi, cr, ci = _scan_groups(bu[:, :ns], bu[:, ns:], pr_ref, pi_ref, hr_s[...], hi_s[...], False)
        hr_s[...] = cr
        hi_s[...] = ci
        h_bf = _bf(jnp.concatenate([hr, hi], axis=1))
        h_ref[...] = h_bf
        y_ref[...] = _dot(h_bf, ccd_ref[0]) + d_ref[...] * u

    tab = pl.BlockSpec((SCAN_ROWS, ns), lambda b, k: (0, b))
    return pl.pallas_call(
        body, name=name, grid=(nb, nc),
        in_specs=[pl.BlockSpec((SSM_T, cw), lambda b, k: (k, b)),
                  pl.BlockSpec((1, cw, ns2), lambda b, k: (b, 0, 0)),
                  pl.BlockSpec((1, ns2, cw), lambda b, k: (b, 0, 0)),
                  tab, tab, pl.BlockSpec((1, cw), lambda b, k: (0, b))],
        out_specs=[pl.BlockSpec((SSM_T, cw), lambda b, k: (k, b)),
                   pl.BlockSpec((1, 1, ns), lambda b, k: (k, 0, b)), pl.BlockSpec((1, 1, ns), lambda b, k: (k, 0, b)),
                   pl.BlockSpec((SSM_T, ns2), lambda b, k: (k, b))],
        out_shape=[jax.ShapeDtypeStruct((l, w), F32), jax.ShapeDtypeStruct((nc, 1, nb * ns), F32),
                   jax.ShapeDtypeStruct((nc, 1, nb * ns), F32), jax.ShapeDtypeStruct((l, nb * ns2), BF16)],
        scratch_shapes=[pltpu.VMEM((1, ns), F32), pltpu.VMEM((1, ns), F32)],
        compiler_params=_params(("parallel", "arbitrary")),
    )(proj, bbd, ccd, pw_re, pw_im, d_skip)


def _ssm_bwd(proj, dy, hs_re, hs_im, h_all, bbd, ccd, pw_re, pw_im, d_skip, w, name, ride=None):
    l = proj.shape[0]
    nb, cw, ns2 = bbd.shape
    ns = ns2 // 2
    nc = l // SSM_T

    def body(u_ref, dy_ref, hsr_ref, hsi_ref, h_ref, bbd_ref, ccd_ref, pr_ref, pi_ref, d_ref,
             du_ref, dbbd_ref, dccd_ref, dar_ref, dai_ref, dd_ref, gr_s, gi_s):
        first = pl.program_id(1) == 0

        @pl.when(first)
        def _():
            gr_s[...] = jnp.zeros_like(gr_s)
            gi_s[...] = jnp.zeros_like(gi_s)

        u, dy = u_ref[...], dy_ref[...]
        dy_bf = _bf(dy)
        hr0, hi0 = hsr_ref[0], hsi_ref[0]
        h = h_ref[...].astype(F32)
        hr, hi = h[:, :ns], h[:, ns:]
        dh = _dot(dy_bf, ccd_ref[0], NT)
        gr, gi, gcr, gci = _scan_groups(dh[:, :ns], dh[:, ns:], pr_ref, pi_ref, gr_s[...], gi_s[...], True)
        gr_s[...] = gcr
        gi_s[...] = gci
        row0 = lax.broadcasted_iota(jnp.int32, hr.shape, 0) == 0
        pr_h = jnp.where(row0, hr0, pltpu.roll(hr, 1, 0))
        pi_h = jnp.where(row0, hi0, pltpu.roll(hi, 1, 0))
        _acc(dar_ref, first, _colsum(pr_h * gr + pi_h * gi))
        _acc(dai_ref, first, _colsum(pr_h * gi - pi_h * gr))
        g_bf = _bf(jnp.concatenate([gr, gi], axis=1))
        _acc(dbbd_ref.at[0], first, _dot(_bf(u.T), g_bf))
        _acc(dccd_ref.at[0], first, _dot(_bf(h.T), dy_bf))
        du_ref[...] = _bf(_dot(g_bf, bbd_ref[0], NT) + d_ref[...] * dy)
        _acc(dd_ref, first, _colsum(dy * u))

    rev = lambda b, k: (nc - 1 - k, b)
    outs = _call(
        body, name=name, grid=(nb, nc), ride=ride, sem=("parallel", "arbitrary"),
        args=(proj, dy, hs_re, hs_im, h_all, bbd, ccd, pw_re, pw_im, d_skip),
        in_specs=[pl.BlockSpec((SSM_T, cw), rev), pl.BlockSpec((SSM_T, cw), rev),
                  pl.BlockSpec((1, 1, ns), lambda b, k: (nc - 1 - k, 0, b)),
                  pl.BlockSpec((1, 1, ns), lambda b, k: (nc - 1 - k, 0, b)),
                  pl.BlockSpec((SSM_T, ns2), rev),
                  pl.BlockSpec((1, cw, ns2), lambda b, k: (b, 0, 0)),
                  pl.BlockSpec((1, ns2, cw), lambda b, k: (b, 0, 0)),
                  pl.BlockSpec((SCAN_ROWS, ns), lambda b, k: (0, b)), pl.BlockSpec((SCAN_ROWS, ns), lambda b, k: (0, b)),
                  pl.BlockSpec((1, cw), lambda b, k: (0, b))],
        out_specs=[pl.BlockSpec((SSM_T, cw), rev),
                   pl.BlockSpec((1, cw, ns2), lambda b, k: (b, 0, 0)),
                   pl.BlockSpec((1, ns2, cw), lambda b, k: (b, 0, 0)),
                   pl.BlockSpec((1, ns), lambda b, k: (0, b)), pl.BlockSpec((1, ns), lambda b, k: (0, b)),
                   pl.BlockSpec((1, cw), lambda b, k: (0, b))],
        out_shape=[jax.ShapeDtypeStruct((l, w), BF16), jax.ShapeDtypeStruct(bbd.shape, F32),
                   jax.ShapeDtypeStruct(ccd.shape, F32), jax.ShapeDtypeStruct((1, nb * ns), F32),
                   jax.ShapeDtypeStruct((1, nb * ns), F32), jax.ShapeDtypeStruct((1, w), F32)],
        scratch_shapes=[pltpu.VMEM((1, ns), F32), pltpu.VMEM((1, ns), F32)])
    return (*outs[:6], outs[6:])


def _block_diag_b(bb_re, bb_im, g, p, c):
    nb = g // SSM_GB
    keep = _same_group(SSM_GB * c, c, SSM_GB * p, p)

    def one(bb):
        t = bb.reshape(nb, SSM_GB, p, c).transpose(0, 1, 3, 2).reshape(nb, SSM_GB * c, p)
        return jnp.where(keep, jnp.tile(t, (1, 1, SSM_GB)), 0.0)

    return jnp.concatenate([one(bb_re), one(bb_im)], axis=2)


def _same_group(rows, per_row, cols, per_col):
    r = lax.broadcasted_iota(jnp.int32, (rows, cols), 0) // per_row
    q = lax.broadcasted_iota(jnp.int32, (rows, cols), 1) // per_col
    return r == q


def _block_diag_c(c_re, c_im, g, p, c):
    nb = g // SSM_GB
    keep = _same_group(SSM_GB * p, p, SSM_GB * c, c)

    def one(cc):
        t = cc.reshape(nb, SSM_GB, c, p).transpose(0, 1, 3, 2).reshape(nb, SSM_GB * p, c)
        return jnp.where(keep, jnp.tile(t, (1, 1, SSM_GB)), 0.0)

    return jnp.concatenate([one(c_re), one(-c_im)], axis=1)


def _diag_of_b(dbbd, g, p, c):
    nb = g // SSM_GB
    keep = _same_group(SSM_GB * c, c, SSM_GB * p, p)

    def one(blk):
        d = jnp.where(keep, blk, 0.0).reshape(nb, SSM_GB * c, SSM_GB, p).sum(axis=2)
        return d.reshape(nb, SSM_GB, c, p).transpose(0, 1, 3, 2).reshape(g * p, c)

    half = SSM_GB * p
    return one(dbbd[:, :, :half]), one(dbbd[:, :, half:])


def _diag_of_c(dccd, g, p, c):
    nb = g // SSM_GB
    keep = _same_group(SSM_GB * p, p, SSM_GB * c, c)

    def one(blk):
        d = jnp.where(keep, blk, 0.0).reshape(nb, SSM_GB * p, SSM_GB, c).sum(axis=2)
        return d.reshape(nb, SSM_GB, p, c).transpose(0, 1, 3, 2).reshape(g, c, p)

    half = SSM_GB * p
    return one(dccd[:, :half]), -one(dccd[:, half:])


def _glu_fwd(y, proj, w_glu, b_glu, name):
    l, w = y.shape
    tm = _tile(l, 256)

    def body(y_ref, z_ref, w_ref, b_ref, o_ref):
        g = _gelu(y_ref[...])
        t = _dot(_bf(g), w_ref[...]) + b_ref[...]
        o_ref[...] = _bf(g * _sigmoid(t) * _silu(z_ref[...]))

    return pl.pallas_call(
        body, name=name, grid=(l // tm,),
        in_specs=[_row_spec(tm, w), pl.BlockSpec((tm, w), lambda i: (i, 1)),
                  pl.BlockSpec((w, w), lambda i: (0, 0)), _vec_spec(w)],
        out_specs=_row_spec(tm, w), out_shape=jax.ShapeDtypeStruct((l, w), BF16),
        compiler_params=_params(("parallel",)),
    )(y, proj, w_glu, b_glu)


def _glu_bwd(do, y, proj, w_glu, b_glu, name):
    l, w = y.shape
    tm = _tile(l, 256)
    nsteps = l // tm

    def body(do_ref, y_ref, z_ref, w_ref, b_ref, dy_ref, dz_ref, dw_ref, db_ref, dw_acc):
        i = pl.program_id(0)
        first = i == 0
        yv, z, do = y_ref[...], z_ref[...], do_ref[...]
        g = _gelu(yv)
        g_bf = _bf(g)
        sg = _sigmoid(_dot(g_bf, w_ref[...]) + b_ref[...])
        dyy = do * _silu(z)
        dz_ref[...] = _bf(do * g * sg * _silu_grad(z))
        dt = dyy * g * sg * (1.0 - sg)
        dt_bf = _bf(dt)
        dg = dyy * sg + _dot(dt_bf, w_ref[...], NT)
        dy_ref[...] = dg * _gelu_grad(yv)
        _acc(dw_acc, first, _dot(_bf(g.T), dt_bf))
        _acc(db_ref, first, _colsum(dt))

        @pl.when(i == nsteps - 1)
        def _():
            dw_ref[...] = _bf(dw_acc[...])

    return pl.pallas_call(
        body, name=name, grid=(nsteps,),
        in_specs=[_row_spec(tm, w), _row_spec(tm, w), pl.BlockSpec((tm, w), lambda i: (i, 1)),
                  pl.BlockSpec((w, w), lambda i: (0, 0)), _vec_spec(w)],
        out_specs=[_row_spec(tm, w), _row_spec(tm, w), pl.BlockSpec((w, w), lambda i: (0, 0)), _vec_spec(w)],
        out_shape=[jax.ShapeDtypeStruct((l, w), F32), jax.ShapeDtypeStruct((l, w), BF16),
                   jax.ShapeDtypeStruct((w, w), BF16), jax.ShapeDtypeStruct((1, w), F32)],
        scratch_shapes=[pltpu.VMEM((w, w), F32)],
        compiler_params=_params(("arbitrary",)),
    )(do, y, proj, w_glu, b_glu)


MOD_ROWS = 128


def _mod_fwd(cond_pad, w_mod, b_shard, name):
    nl, d, ncol = w_mod.shape
    tn = _tile(ncol, 512)

    def body(c_ref, w_ref, b_ref, o_ref):
        o_ref[0] = _dot(_bf(c_ref[...]), _bf(w_ref[0])) + b_ref[0]

    return pl.pallas_call(
        body, name=name, grid=(nl, ncol // tn),
        in_specs=[pl.BlockSpec((MOD_ROWS, d), lambda a, j: (0, 0)),
                  pl.BlockSpec((1, d, tn), lambda a, j: (a, 0, j)),
                  pl.BlockSpec((1, 1, tn), lambda a, j: (a, 0, j))],
        out_specs=pl.BlockSpec((1, MOD_ROWS, tn), lambda a, j: (a, 0, j)),
        out_shape=jax.ShapeDtypeStruct((nl, MOD_ROWS, ncol), F32),
        compiler_params=_params(("parallel", "parallel")),
    )(cond_pad, w_mod, b_shard)


def _mod_bwd(cond_pad_t, dmod_pad, name):
    nl, _, ncol = dmod_pad.shape
    d = cond_pad_t.shape[0]
    tn = _tile(ncol, 512)

    def body(c_ref, dm_ref, o_ref):
        o_ref[0] = _dot(_bf(c_ref[...]), _bf(dm_ref[0]))

    return pl.pallas_call(
        body, name=name, grid=(nl, ncol // tn),
        in_specs=[pl.BlockSpec((d, MOD_ROWS), lambda a, j: (0, 0)),
                  pl.BlockSpec((1, MOD_ROWS, tn), lambda a, j: (a, 0, j))],
        out_specs=pl.BlockSpec((1, d, tn), lambda a, j: (a, 0, j)),
        out_shape=jax.ShapeDtypeStruct((nl, d, ncol), F32),
        compiler_params=_params(("parallel", "parallel")),
    )(cond_pad_t, dmod_pad)


def _silu_rows(c2d, name):
    def body(c_ref, o_ref):
        o_ref[...] = _silu(c_ref[...])

    return pl.pallas_call(body, name=name, out_shape=jax.ShapeDtypeStruct(c2d.shape, F32))(c2d)


def _sum_leading(x, name):
    n, r, c = x.shape
    tr = _tile(r, max(16, (1 << 20) // (4 * c)), 16 if r % 16 == 0 else 8)

    def body(x_ref, o_ref):
        acc = x_ref[0].astype(F32)
        for k in range(1, n):
            acc = acc + x_ref[k].astype(F32)
        o_ref[...] = acc

    return pl.pallas_call(
        body, name=name, grid=(r // tr,),
        in_specs=[pl.BlockSpec((n, tr, c), lambda i: (0, i, 0))], out_specs=pl.BlockSpec((tr, c), lambda i: (i, 0)),
        out_shape=jax.ShapeDtypeStruct((r, c), F32), compiler_params=_params(("parallel",)),
    )(x)


def _adamw(w, gs, m, v, name):
    r, c = w.shape
    tr = _tile(r, max(8, (3 << 19) // (4 * c)), 8)
    ng = len(gs)

    def body(*refs):
        w_ref, g_refs, m_ref, v_ref = refs[0], refs[1:1 + ng], refs[1 + ng], refs[2 + ng]
        g_ref, d_ref, nm_ref, nv_ref = refs[3 + ng:]
        g = g_refs[0][...]
        for extra in g_refs[1:]:
            g = g + extra[...]
        g_ref[...] = g
        d_ref[...], nm_ref[...], nv_ref[...] = _adamw_math(w_ref[...], g, m_ref[...], v_ref[...])

    spec = pl.BlockSpec((tr, c), lambda i: (i, 0))
    shp = jax.ShapeDtypeStruct((r, c), F32)
    return pl.pallas_call(
        body, name=name, grid=(r // tr,), in_specs=[spec] * (3 + ng), out_specs=[spec] * 4,
        out_shape=[shp] * 4, compiler_params=_params(("parallel",)),
    )(w, *gs, m, v)


def _adamw_math(w, g, m, v):
    nm = ADAM_B1 * m + (1.0 - ADAM_B1) * g
    nv = ADAM_B2 * v + (1.0 - ADAM_B2) * (g * g)
    m_hat = nm / (1.0 - ADAM_B1 ** ADAM_STEP)
    v_hat = nv / (1.0 - ADAM_B2 ** ADAM_STEP)
    return -ADAM_LR * (m_hat / (jnp.sqrt(v_hat) + ADAM_EPS) + ADAM_WD * w), nm, nv


def _adamw_many(ws, gs, ms, vs, name):
    n = len(ws)

    def body(*refs):
        w_refs, g_refs, m_refs, v_refs = (refs[k * n:(k + 1) * n] for k in range(4))
        outs = refs[4 * n:]
        for i in range(n):
            outs[3 * i][...], outs[3 * i + 1][...], outs[3 * i + 2][...] = _adamw_math(
                w_refs[i][...], g_refs[i][...], m_refs[i][...], v_refs[i][...])

    out_shape = [jax.ShapeDtypeStruct(w.shape, F32) for w in ws for _ in range(3)]
    outs = pl.pallas_call(body, name=name, out_shape=out_shape, compiler_params=_params())(*ws, *gs, *ms, *vs)
    return [tuple(outs[3 * i:3 * i + 3]) for i in range(n)]


ANY = pl.BlockSpec(memory_space=pl.ANY)


def _flip(v, bit):
    return 1 - v if bit else v


def _allgather8_ops(x_ref, o_ref, send_sems, recv_sems, local_sem):
    mx, my, mc = lax.axis_index("x"), lax.axis_index("y"), lax.axis_index("c")
    me = 4 * mx + 2 * my + mc

    def mine():
        return pltpu.make_async_copy(x_ref, o_ref.at[me], local_sem)

    def copy(j, outgoing):
        peer = (_flip(mx, j & 4), _flip(my, j & 2), _flip(mc, j & 1))
        slot = me if outgoing else 4 * peer[0] + 2 * peer[1] + peer[2]
        return pltpu.make_async_remote_copy(
            src_ref=x_ref, dst_ref=o_ref.at[slot], send_sem=send_sems.at[j - 1], recv_sem=recv_sems.at[j - 1],
            device_id=peer, device_id_type=MESH)

    def start():
        mine().start()
        for j in range(1, 8):
            copy(j, True).start()

    def wait():
        for j in range(1, 8):
            copy(j, False).wait()
        mine().wait()

    return start, wait


def _ride_all8(x):
    return dict(xs=[x], shapes=[jax.ShapeDtypeStruct((8,) + x.shape, x.dtype)],
                sems=[pltpu.SemaphoreType.DMA((7,)), pltpu.SemaphoreType.DMA((7,)), pltpu.SemaphoreType.DMA],
                ops=lambda x_refs, o_refs, sems: _allgather8_ops(x_refs[0], o_refs[0], *sems))


def _ride_chip(xs, gather):
    return dict(xs=list(xs), shapes=_chip_exchange_shapes(xs, gather), sems=_chip_exchange_sems(len(xs)),
                ops=lambda x_refs, o_refs, sems: _chip_exchange_ops(x_refs, o_refs, *sems, gather))


def _allgather8(x, name):
    def body(x_ref, o_ref, *sems):
        start, wait = _allgather8_ops(x_ref, o_ref, *sems)
        start()
        wait()

    ride = _ride_all8(x)
    return pl.pallas_call(body, name=name, in_specs=[ANY], out_specs=ANY, out_shape=ride["shapes"][0],
                          scratch_shapes=ride["sems"])(x)


def _gather_halves(x, name):
    r = x.shape[0]
    half = r // 2

    def body(x_ref, o_ref, ici_send, ici_recv, d2d_send, d2d_recv, local_sem):
        mx, my, mc = lax.axis_index("x"), lax.axis_index("y"), lax.axis_index("c")
        k0 = 2 * mx + my
        mine = pl.ds(pl.multiple_of(mc * half, 16), half)
        theirs = pl.ds(pl.multiple_of((1 - mc) * half, 16), half)
        local = pltpu.make_async_copy(x_ref, o_ref.at[k0], local_sem)
        local.start()

        def chips(j):
            px, py = _flip(mx, j & 2), _flip(my, j & 1)
            return px, py, 2 * px + py

        def over_ici(j, outgoing):
            px, py, kp = chips(j)
            dst = o_ref.at[k0, mine] if outgoing else o_ref.at[kp, mine]
            return pltpu.make_async_remote_copy(
                src_ref=x_ref.at[mine], dst_ref=dst, send_sem=ici_send.at[j - 1], recv_sem=ici_recv.at[j - 1],
                device_id=(px, py, mc), device_id_type=MESH)

        def over_d2d(j, outgoing):
            _, _, kp = chips(j)
            rows = mine if outgoing else theirs
            return pltpu.make_async_remote_copy(
                src_ref=o_ref.at[kp, rows], dst_ref=o_ref.at[kp, rows], send_sem=d2d_send.at[j - 1],
                recv_sem=d2d_recv.at[j - 1], device_id=(mx, my, 1 - mc), device_id_type=MESH)

        for j in range(1, 4):
            over_ici(j, True).start()
        for j in range(1, 4):
            over_ici(j, False).wait_recv()
            over_d2d(j, True).start()
        for j in range(1, 4):
            over_ici(j, True).wait_send()
            over_d2d(j, True).wait_send()
            over_d2d(j, False).wait_recv()
        local.wait()

    dma3 = pltpu.SemaphoreType.DMA((3,))
    return pl.pallas_call(
        body, name=name, in_specs=[ANY], out_specs=ANY, out_shape=jax.ShapeDtypeStruct((4,) + x.shape, x.dtype),
        scratch_shapes=[dma3, dma3, dma3, dma3, pltpu.SemaphoreType.DMA])(x)


def _chip_exchange(xs, gather, name):
    n = len(xs)

    def body(*refs):
        start, wait = _chip_exchange_ops(refs[:n], refs[n:2 * n], *refs[2 * n:], gather)
        start()
        wait()

    return pl.pallas_call(
        body, name=name, in_specs=[ANY] * n, out_specs=[ANY] * n, out_shape=_chip_exchange_shapes(xs, gather),
        scratch_shapes=_chip_exchange_sems(n),
    )(*xs)


def _chip_exchange_shapes(xs, gather):
    return [jax.ShapeDtypeStruct(((4,) + x.shape) if gather else x.shape, x.dtype) for x in xs]


def _chip_exchange_sems(n):
    return [pltpu.SemaphoreType.DMA((3 * n,)), pltpu.SemaphoreType.DMA((3 * n,)), pltpu.SemaphoreType.DMA((n,))]


def _chip_exchange_ops(x_refs, o_refs, send_sems, recv_sems, local_sems, gather):
    n = len(x_refs)
    mx, my, mc = lax.axis_index("x"), lax.axis_index("y"), lax.axis_index("c")
    k0 = 2 * mx + my

    def local(a):
        src = x_refs[a] if gather else x_refs[a].at[k0]
        return pltpu.make_async_copy(src, o_refs[a].at[k0], local_sems.at[a])

    def copy(a, j, outgoing):
        px, py = _flip(mx, j & 2), _flip(my, j & 1)
        kp = 2 * px + py
        if outgoing:
            src = x_refs[a] if gather else x_refs[a].at[kp]
            dst = o_refs[a].at[k0]
        else:
            src = x_refs[a] if gather else x_refs[a].at[k0]
            dst = o_refs[a].at[kp]
        s = a * 3 + j - 1
        return pltpu.make_async_remote_copy(
            src_ref=src, dst_ref=dst, send_sem=send_sems.at[s], recv_sem=recv_sems.at[s],
            device_id=(px, py, mc), device_id_type=MESH)

    def start():
        for a in range(n):
            local(a).start()
            for j in range(1, 4):
                copy(a, j, True).start()

    def wait():
        for a in range(n):
            for j in range(1, 4):
                copy(a, j, False).wait()
            local(a).wait()

    return start, wait


def _call(body, *, name, grid, in_specs, out_specs, out_shape, args, scratch_shapes=(), sem=None, ride=None):
    if not ride:
        return pl.pallas_call(
            body, name=name, grid=grid, in_specs=list(in_specs), out_specs=list(out_specs), out_shape=list(out_shape),
            scratch_shapes=list(scratch_shapes), compiler_params=_params(sem))(*args)
    xs = [x for r in ride for x in r["xs"]]
    shapes = [s for r in ride for s in r["shapes"]]
    sems = [s for r in ride for s in r["sems"]]
    n_in, n_out, n_scr, nx = len(in_specs), len(out_specs), len(scratch_shapes), len(xs)

    def wrapped(*refs):
        ins, x_refs = refs[:n_in], refs[n_in:n_in + nx]
        outs = refs[n_in + nx:n_in + nx + n_out]
        lands = refs[n_in + nx + n_out:n_in + 2 * nx + n_out]
        rest = refs[n_in + 2 * nx + n_out:]
        scr, sem_refs = rest[:n_scr], rest[n_scr:]
        ops, xo, so = [], 0, 0
        for r in ride:
            nr, ns = len(r["xs"]), len(r["sems"])
            ops.append(r["ops"](x_refs[xo:xo + nr], lands[xo:xo + nr], sem_refs[so:so + ns]))
            xo, so = xo + nr, so + ns
        ids = [pl.program_id(a) for a in range(len(grid))]
        first = functools.reduce(jnp.logical_and, [i == 0 for i in ids])
        last = functools.reduce(jnp.logical_and, [i == g - 1 for i, g in zip(ids, grid)])

        @pl.when(first)
        def _():
            for start, _ in ops:
                start()

        body(*ins, *outs, *scr)

        @pl.when(last)
        def _():
            for _, wait in ops:
                wait()

    return pl.pallas_call(
        wrapped, name=name, grid=grid, in_specs=list(in_specs) + [ANY] * nx, out_specs=list(out_specs) + [ANY] * nx,
        out_shape=list(out_shape) + shapes, scratch_shapes=list(scratch_shapes) + sems,
        compiler_params=_params(("arbitrary",) * len(grid)))(*args, *xs)


def _sibling_exchange(xs, name):
    n = len(xs)

    def body(*refs):
        x_refs, o_refs = refs[:n], refs[n:2 * n]
        send_sems, recv_sems = refs[2 * n:]
        sib = (lax.axis_index("x"), lax.axis_index("y"), 1 - lax.axis_index("c"))
        copies = [pltpu.make_async_remote_copy(
            src_ref=x_refs[a], dst_ref=o_refs[a], send_sem=send_sems.at[a], recv_sem=recv_sems.at[a],
            device_id=sib, device_id_type=MESH) for a in range(n)]
        for cp in copies:
            cp.start()
        for cp in copies:
            cp.wait()

    return pl.pallas_call(
        body, name=name, in_specs=[ANY] * n, out_specs=[ANY] * n,
        out_shape=[jax.ShapeDtypeStruct(x.shape, x.dtype) for x in xs],
        scratch_shapes=[pltpu.SemaphoreType.DMA((n,)), pltpu.SemaphoreType.DMA((n,))],
    )(*xs)


PACK = 1024
PACK_ROWS = 512


def _pack(parts):
    flat = []
    for p in parts:
        v = p.reshape(-1).astype(F32)
        flat.append(jnp.pad(v, (0, (-v.shape[0]) % PACK)))
    total = sum(v.shape[0] for v in flat)
    flat.append(jnp.zeros(((-total) % (PACK_ROWS * 128),), F32))
    return jnp.concatenate(flat).reshape(-1, 128)


def _unpack_rows(gathered, shapes):
    flat = gathered.reshape(gathered.shape[0], -1)
    out, off = [], 0
    for shp in shapes:
        n = math.prod(shp)
        out.append(flat[:, off:off + n].reshape((flat.shape[0],) + tuple(shp)))
        off += n + (-n) % PACK
    return out


def _unpack(packed, shapes):
    flat = packed.reshape(-1)
    out, off = [], 0
    for shp in shapes:
        n = math.prod(shp)
        out.append(flat[off:off + n].reshape(shp))
        off += n + (-n) % PACK
    return out


def kernel(x, c, ln_pre_g, ln_post_g, w_mod, b_mod, w_in_ab, w_out_ab, sgu_norm_g, sgu_w, sgu_b, w_in_ssm, w_out_ssm, lam_re, lam_im, b_re, b_im, c_re, c_im, d_skip, log_dt, w_glu, b_glu, loss_target, m_ln_pre_g, m_ln_post_g, m_w_mod, m_b_mod, m_w_in_ab, m_w_out_ab, m_sgu_norm_g, m_sgu_w, m_sgu_b, m_w_in_ssm, m_w_out_ssm, m_lam_re, m_lam_im, m_b_re, m_b_im, m_c_re, m_c_im, m_d_skip, m_log_dt, m_w_glu, m_b_glu, v_ln_pre_g, v_ln_post_g, v_w_mod, v_b_mod, v_w_in_ab, v_w_out_ab, v_sgu_norm_g, v_sgu_w, v_sgu_b, v_w_in_ssm, v_w_out_ssm, v_lam_re, v_lam_im, v_b_re, v_b_im, v_c_re, v_c_im, v_d_skip, v_log_dt, v_w_glu, v_b_glu):
    given = dict(locals())
    mx, my, mc = lax.axis_index("x"), lax.axis_index("y"), lax.axis_index("c")
    me = 4 * mx + 2 * my + mc
    chip = 2 * mx + my

    _, l, d = x.shape
    x2, tgt = x[0], loss_target[0]
    n_in = w_in_ab.shape[2] * 4
    wa = wb = n_in // 7
    w = w_out_ssm.shape[1]
    g, p, cch = b_re.shape[1:]
    nmod = w_mod.shape[2]

    gw_in_ab = _gather_halves(_bf(w_in_ab[0]), "gather_w_in_ab")
    win_ab = jnp.concatenate([gw_in_ab[k] for k in range(4)], axis=1)
    later_shards = [_bf(w_out_ab[0]), _bf(w_in_ssm[0]), _bf(w_out_ssm[0]), _bf(w_glu[0]), d_skip, b_glu]

    cond = _silu_rows(c.reshape(d // 128, 128), "cond_silu")
    cond_all = _allgather8(cond, "gather_cond").reshape(8, d)
    b_shard = lax.dynamic_slice(b_mod, (0, chip * nmod), (2, nmod)).reshape(2, 1, nmod)
    cond_pad = jnp.pad(cond_all, ((0, MOD_ROWS - 8), (0, 0)))
    modp = _mod_fwd(cond_pad, w_mod, b_shard, "mod_fwd")[:, :8]
    modp_all = _allgather8(modp.reshape(16, nmod), "gather_mod").reshape(4, 2, 2, 8, nmod)
    mine = lax.dynamic_index_in_dim(lax.dynamic_index_in_dim(modp_all, mc, 1, False), me, 2, False)
    mod = mine.transpose(1, 0, 2).reshape(2, 3 * d)
    shift = [mod[a:a + 1, :d] for a in range(2)]
    scale = [mod[a:a + 1, d:2 * d] for a in range(2)]
    gate = [mod[a:a + 1, 2 * d:] for a in range(2)]
    pre_g = [ln_pre_g[a:a + 1] for a in range(2)]
    post_g = [ln_post_g[a:a + 1] for a in range(2)]

    sgu_w0, sgu_bt = sgu_w[0], sgu_b[0].T
    h0 = _pre_fwd(x2, pre_g[0], scale[0], shift[0], "pre0_fwd")
    w_gates = jnp.concatenate([win_ab[:, :3 * wa], win_ab[:, 3 * wa + 3 * wb:]], axis=1)
    proj0 = _matmul(h0, w_gates, "nn", F32, "proj0", tm=1024)
    qkv = _matmul(h0, win_ab[:, 3 * wa:3 * wa + 3 * wb], "nn", BF16, "proj0_qkv", tm=1024)
    out_b, lktot, (gw_out_ab, gw_in_ssm, gw_out_ssm, gw_glu, g_dskip, g_bglu) = _attn_fwd(
        qkv, wb, "attn_fwd", hp=4, ride=[_ride_chip(later_shards, True)])
    wout_ab = gw_out_ab.reshape(wa + wb, d)
    win_ssm = gw_in_ssm.reshape(d, 2 * w)
    wout_ssm = jnp.concatenate([gw_out_ssm[k] for k in range(4)], axis=1)
    wglu = gw_glu.reshape(w, w)
    dskip_full = g_dskip.reshape(1, w)
    bglu_full = g_bglu.reshape(1, w)
    cat =_sgu_fwd(proj0, out_b, sgu_norm_g, sgu_w0, sgu_bt, wa, wb, "sgu_fwd")
    y0 = _matmul(cat, wout_ab, "nn", F32, "out0", tm=1024)
    x1, h1 = _post_pre_fwd(x2, y0, gate[0], post_g[0], pre_g[1], scale[1], shift[1], "post0_pre1_fwd")

    s = g * p
    lr_c, li_c = lam_re.reshape(s, 1), lam_im.reshape(s, 1)
    ldt_c = jnp.repeat(log_dt.reshape(g), p).reshape(s, 1)
    br_c, bi_c = b_re.reshape(s, cch), b_im.reshape(s, cch)
    bb_re, bb_im, pw_re, pw_im = _ssm_prep(lr_c, li_c, ldt_c, br_c, bi_c, lr_c.reshape(1, s), li_c.reshape(1, s),
                                           ldt_c.reshape(1, s), "ssm_prep")
    bbd = _bf(_block_diag_b(bb_re, bb_im, g, p, cch))
    ccd = _bf(_block_diag_c(c_re[0], c_im[0], g, p, cch))
    proj1 = _matmul(h1, win_ssm, "nn", F32, "proj1", tm=1024)
    y_ssm, hs_re, hs_im, h_all = _ssm_fwd(proj1, bbd, ccd, pw_re, pw_im, dskip_full, w, "ssm_fwd")
    o1 = _glu_fwd(y_ssm, proj1, wglu, bglu_full, "glu_fwd")
    y1 = _matmul(o1, wout_ssm, "nn", F32, "out1", tm=1024)
    loss_vec, dy1, dx2, dgate1, dpost1 = _post_loss(x1, y1, gate[1], post_g[1], tgt, "post1_loss")

    do1 = _matmul(dy1, wout_ssm, "nt", F32, "out1_dx", tm=1024)
    gr_wout_ssm = _matmul(o1, dy1, "tn", BF16, "out1_dw", tm=1024, tk=1024, n_split=4)
    dy_ssm, dz1, gr_wglu, gr_bglu = _glu_bwd(do1, y_ssm, proj1, wglu, bglu_full, "glu_bwd")
    du1, dbbd, dccd, da_re, da_im, gr_dskip, (ld_wout_ssm, ld_wglu) = _ssm_bwd(
        proj1, dy_ssm, hs_re, hs_im, h_all, bbd, ccd, pw_re, pw_im, dskip_full, w, "ssm_bwd",
        ride=[_ride_chip([gr_wout_ssm, gr_wglu.reshape(4, w // 4, w)], False)])
    dproj1 = jnp.concatenate([du1, dz1], axis=1)
    dh1 = _matmul(dproj1, win_ssm, "nt", F32, "proj1_dx", tm=1024)
    gr_win_ssm = _matmul(h1, dproj1, "tn", BF16, "proj1_dw", tm=1024, tn=1024, tk=1024)
    dx1, dscale1, dshift1, dpre1, dy0, dgate0, dpost0 = _pre_bwd(
        dh1, dx2, x1, pre_g[1], scale[1], "pre1_post0_bwd", post=(y0, gate[0], post_g[0]))

    dcat = _matmul(dy0, wout_ab, "nt", F32, "out0_dx", tm=1024)
    gr_wout_ab = _matmul(cat, dy0, "tn", BF16, "out0_dw", tm=1024, tn=1024, tk=1024)
    dbb_re, dbb_im = _diag_of_b(dbbd, g, p, cch)
    dc_re, dc_im = _diag_of_c(dccd, g, p, cch)
    part_a = [loss_vec[:, :1], dpre1, dpost0, dpost1, dgate0, dshift1, dscale1, dgate1, da_re, da_im,
              dbb_re, dbb_im, dc_re, dc_im, gr_dskip, gr_bglu]
    shapes_a = [a.shape for a in part_a]
    dq, dk, dv, (ld_win_ssm, ld_wout_ab, gath_a) = _attn_bwd(
        qkv, proj0, dcat, lktot, wa, wb, "attn_bwd", hp=4,
        ride=[_ride_chip([gr_win_ssm.reshape(4, d // 4, 2 * w), gr_wout_ab.reshape(4, (wa + wb) // 4, d)], False),
              _ride_all8(_pack(part_a))])
    dproj0, gr_sgu_w, gr_sgu_bt, gr_sgu_g = _sgu_bwd(proj0, out_b, dcat, dq, dk, dv, sgu_norm_g, sgu_w0, sgu_bt,
                                                     wa, wb, "sgu_bwd")
    part_b = [gr_sgu_g, gr_sgu_w, gr_sgu_bt.T]
    shapes_b = [a.shape for a in part_b]
    gr_win_ab_lo, (gath_b,) = _matmul(h0, dproj0, "tn", BF16, "proj0_dw_lo", tm=1024, tk=1024, tn=896, n_split=4,
                                      m_part=(0, 3, 8), ride=[_ride_all8(_pack(part_b))])
    gr_win_ab_hi, (ld_win_ab_lo,) = _matmul(
        h0, dproj0, "tn", BF16, "proj0_dw_hi", tm=1024, tk=1024, tn=896, n_split=4, m_part=(3, 5, 8),
        ride=[_ride_chip([gr_win_ab_lo], False)])
    dh0, (ld_win_ab_hi,) = _matmul(dproj0, win_ab, "nt", F32, "proj0_dx", tm=1024, tk=1792,
                                   ride=[_ride_chip([gr_win_ab_hi], False)])
    grad_x, dscale0, dshift0, dpre0 = _pre_bwd(dh0, dx1, x2, pre_g[0], scale[0], "pre0_bwd")
    part_c = [dpre0, dshift0, dscale0]
    shapes_c = [a.shape for a in part_c]
    gath_c = _allgather8(_pack(part_c), "gather_small_tail")

    landed = [ld_wout_ab, ld_win_ssm, ld_wout_ssm, ld_wglu]
    big_names = ["w_in_ab", "w_out_ab", "w_in_ssm", "w_out_ssm", "w_glu"]
    sums = [jnp.concatenate([_sum_leading(ld_win_ab_lo, "sum_w_in_ab_lo"), _sum_leading(ld_win_ab_hi, "sum_w_in_ab_hi")],
                            axis=0)]
    sums += [_sum_leading(a, "sum_" + nm) for a, nm in zip(landed, big_names[1:])]
    sib = _sibling_exchange(sums, "sibling_grads")
    results = {}
    for nm, s_mine, s_sib in zip(big_names, sums, sib):
        shp = given[nm].shape
        two_d = lambda a: a.reshape(-1, shp[-1])
        outs = _adamw(two_d(given[nm]), [s_mine, s_sib], two_d(given["m_" + nm]), two_d(given["v_" + nm]),
                      "adamw_" + nm)
        results[nm] = [o.reshape(shp) for o in outs]

    (loss_s, g_pre1, g_post0, g_post1, g_gate0, g_shift1, g_scale1, g_gate1, s_da_re, s_da_im, s_dbb_re, s_dbb_im,
     g_c_re, g_c_im, g_dskip_full, g_bglu_full) = _unpack(_sum_leading(gath_a, "sum_small_a"), shapes_a)
    g_sgu_g, g_sgu_w, g_sgu_b = _unpack(_sum_leading(gath_b, "sum_small_b"), shapes_b)
    g_pre0, g_shift0, g_scale0 = _unpack(_sum_leading(gath_c, "sum_small_c"), shapes_c)
    loss = loss_s.reshape(())
    g_pre = jnp.concatenate([g_pre0, g_pre1], axis=0)
    g_post = jnp.concatenate([g_post0, g_post1], axis=0)
    g_bmod = jnp.concatenate([jnp.concatenate([g_shift0, g_scale0, g_gate0], axis=1),
                              jnp.concatenate([g_shift1, g_scale1, g_gate1], axis=1)], axis=0)

    g_lr, g_li, g_ldt, g_br, g_bi = _ssm_prep_bwd(lr_c, li_c, ldt_c, br_c, bi_c, s_da_re.reshape(s, 1),
                                                  s_da_im.reshape(s, 1), s_dbb_re, s_dbb_im, p, "ssm_prep_bwd")
    small = {
        "ln_pre_g": g_pre, "ln_post_g": g_post, "b_mod": g_bmod, "sgu_norm_g": g_sgu_g,
        "sgu_w": g_sgu_w.reshape(sgu_w.shape), "sgu_b": g_sgu_b.reshape(sgu_b.shape),
        "lam_re": g_lr.reshape(lam_re.shape), "lam_im": g_li.reshape(lam_im.shape),
        "b_re": g_br.reshape(b_re.shape), "b_im": g_bi.reshape(b_im.shape),
        "c_re": g_c_re.reshape(c_re.shape), "c_im": g_c_im.reshape(c_im.shape),
        "d_skip": lax.dynamic_slice(g_dskip_full, (0, chip * (w // 4)), (1, w // 4)),
        "log_dt": g_ldt.reshape(log_dt.shape),
        "b_glu": lax.dynamic_slice(g_bglu_full, (0, chip * (w // 4)), (1, w // 4)),
    }
    flat2 = lambda a: a.reshape(-1, a.shape[-1])
    wide = ("b_re", "b_im")
    for tag, group in (("adamw_small", [nm for nm in small if nm not in wide]), ("adamw_small_b", list(wide))):
        outs = _adamw_many([flat2(given[nm]) for nm in group], [flat2(small[nm]) for nm in group],
                           [flat2(given["m_" + nm]) for nm in group], [flat2(given["v_" + nm]) for nm in group], tag)
        for nm, trio in zip(group, outs):
            results[nm] = [small[nm]] + [o.reshape(given[nm].shape) for o in trio]

    rows_a = _unpack_rows(gath_a, shapes_a)
    rows_c = _unpack_rows(gath_c, shapes_c)
    dmod_rows = jnp.concatenate([rows_c[1], rows_c[2], rows_a[4], rows_a[5], rows_a[6], rows_a[7]],
                                axis=2).reshape(8, 2, 3 * d)
    dmod_shard = lax.dynamic_slice(dmod_rows, (0, 0, chip * nmod), (8, 2, nmod)).transpose(1, 0, 2)
    dmod_pad = jnp.pad(dmod_shard, ((0, 0), (0, MOD_ROWS - 8), (0, 0)))
    gr_wmod = _mod_bwd(cond_pad.T, dmod_pad, "mod_bwd")
    two_d = lambda a: a.reshape(-1, nmod)
    outs = _adamw(two_d(w_mod), [two_d(gr_wmod)], two_d(m_w_mod), two_d(v_w_mod), "adamw_w_mod")
    results["w_mod"] = [o.reshape(w_mod.shape) for o in outs]

    names = ["ln_pre_g", "ln_post_g", "w_mod", "b_mod", "w_in_ab", "w_out_ab", "sgu_norm_g", "sgu_w", "sgu_b",
             "w_in_ssm", "w_out_ssm", "lam_re", "lam_im", "b_re", "b_im", "c_re", "c_im", "d_skip", "log_dt",
             "w_glu", "b_glu"]
    return (loss, grad_x[None], *[results[nm][0] for nm in names], *[results[nm][1] for nm in names],
            *[results[nm][2] for nm in names], *[results[nm][3] for nm in names])
```

```python
import functools
import math

import jax
import jax.numpy as jnp
from jax import lax
from jax.experimental import pallas as pl
from jax.experimental.pallas import tpu as pltpu

F32 = jnp.float32
BF16 = jnp.bfloat16
MESH = pl.DeviceIdType.MESH

EPS = 1e-6
HEAD = 128
SSM_T = 512
SSM_GB = 16
ADAM_LR, ADAM_B1, ADAM_B2, ADAM_EPS, ADAM_WD, ADAM_STEP = 0.001, 0.9, 0.999, 1e-08, 0.01, 10
VMEM_LIMIT = 56 * 1024 * 1024

NN = (((1,), (0,)), ((), ()))
NT = (((1,), (1,)), ((), ()))
TN = (((0,), (0,)), ((), ()))


def _params(sem=None):
    return pltpu.CompilerParams(dimension_semantics=sem, vmem_limit_bytes=VMEM_LIMIT)


def _dot(a, b, dims=NN):
    return lax.dot_general(a, b, dims, preferred_element_type=F32)


def _bf(x):
    return x.astype(BF16)


def _gelu(x):
    k = math.sqrt(2.0 / math.pi)
    t = jnp.tanh(k * (x + 0.044715 * x * x * x))
    return 0.5 * x * (1.0 + t)


def _gelu_grad(x):
    k = math.sqrt(2.0 / math.pi)
    x2 = x * x
    t = jnp.tanh(k * (x + 0.044715 * x * x2))
    return 0.5 * (1.0 + t) + 0.5 * x * (1.0 - t * t) * k * (1.0 + 3.0 * 0.044715 * x2)


def _sigmoid(x):
    return 1.0 / (1.0 + jnp.exp(-x))


def _silu(x):
    return x * _sigmoid(x)


def _silu_grad(x):
    s = _sigmoid(x)
    return s * (1.0 + x * (1.0 - s))


def _tile(n, t, mult=128):
    if n <= t:
        return n
    for cand in range(t - t % mult, 0, -mult):
        if n % cand == 0:
            return cand
    raise ValueError((n, t, mult))


def _matmul(a, b, mode, out_dtype, name, tm=512, tn=512, tk=2048, n_split=1, ride=None, m_part=None):
    b_sharded = b.ndim == 3
    if mode == "nn":
        (m, kk), (_, n) = a.shape, b.shape
    elif b_sharded:
        assert mode == "nt"
        (m, kk), n, tk = a.shape, b.shape[1], b.shape[2]
    elif mode == "nt":
        (m, kk), (n, _) = a.shape, b.shape
    else:
        (kk, m), (_, n) = a.shape, b.shape
    m_off = 0
    if m_part is not None:
        assert mode == "tn"
        first, count, parts = m_part
        tm = _tile(m // parts, tm)
        m_off = first * (m // parts) // tm
        m = count * (m // parts)
    tm, tk = _tile(m, tm), _tile(kk, tk)
    ns = n // n_split
    tn = _tile(ns, tn)
    nk = kk // tk
    dims = {"nn": NN, "nt": NT, "tn": TN}[mode]

    def body(a_ref, b_ref, o_ref, acc_ref):
        k = pl.program_id(2)
        part = _dot(_bf(a_ref[...]), _bf(b_ref[0] if b_sharded else b_ref[...]), dims)

        @pl.when(k == 0)
        def _():
            acc_ref[...] = part

        @pl.when(k > 0)
        def _():
            acc_ref[...] += part

        @pl.when(k == nk - 1)
        def _():
            o_ref[...] = acc_ref[...].astype(out_dtype).reshape(o_ref.shape)

    if mode == "nn":
        a_spec = pl.BlockSpec((tm, tk), lambda i, j, k: (i, k))
        b_spec = pl.BlockSpec((tk, tn), lambda i, j, k: (k, j))
    elif mode == "nt":
        a_spec = pl.BlockSpec((tm, tk), lambda i, j, k: (i, k))
        b_spec = (pl.BlockSpec((1, tn, tk), lambda i, j, k: (k, j, 0)) if b_sharded
                  else pl.BlockSpec((tn, tk), lambda i, j, k: (j, k)))
    else:
        a_spec = pl.BlockSpec((tk, tm), lambda i, j, k: (k, i + m_off))
        b_spec = pl.BlockSpec((tk, tn), lambda i, j, k: (k, j))
    if n_split == 1:
        out_shape = jax.ShapeDtypeStruct((m, n), out_dtype)
        o_spec = pl.BlockSpec((tm, tn), lambda i, j, k: (i, j))
    else:
        per = ns // tn
        out_shape = jax.ShapeDtypeStruct((n_split, m, ns), out_dtype)
        o_spec = pl.BlockSpec((1, tm, tn), lambda i, j, k: (j // per, i, j % per))
    outs = _call(body, name=name, grid=(m // tm, n // tn, nk), in_specs=[a_spec, b_spec], out_specs=[o_spec],
                 out_shape=[out_shape], scratch_shapes=[pltpu.VMEM((tm, tn), F32)], args=(a, b),
                 sem=("parallel", "parallel", "arbitrary"), ride=ride)
    return outs[0] if ride is None else (outs[0], outs[1:])


def _row_spec(tm, d):
    return pl.BlockSpec((tm, d), lambda i: (i, 0))


def _vec_spec(d):
    return pl.BlockSpec((1, d), lambda i: (0, 0))


def _acc(ref, first, val):
    @pl.when(first)
    def _():
        ref[...] = val

    @pl.when(jnp.logical_not(first))
    def _():
        ref[...] += val


def _colsum(x):
    return jnp.sum(x, axis=0, keepdims=True)


def _rownorm(x):
    r = lax.rsqrt(jnp.mean(x * x, axis=-1, keepdims=True) + EPS)
    return x * r, r


def _pre_fwd(x, g, scale, shift, name):
    l, d = x.shape
    tm = _tile(l, 256)

    def body(x_ref, g_ref, sc_ref, sh_ref, h_ref):
        n, _ = _rownorm(x_ref[...])
        h_ref[...] = _bf(n * g_ref[...] * (1.0 + sc_ref[...]) + sh_ref[...])

    return pl.pallas_call(
        body, name=name, grid=(l // tm,),
        in_specs=[_row_spec(tm, d), _vec_spec(d), _vec_spec(d), _vec_spec(d)],
        out_specs=_row_spec(tm, d), out_shape=jax.ShapeDtypeStruct((l, d), BF16),
        compiler_params=_params(("parallel",)),
    )(x, g, scale, shift)


def _post_pre_fwd(x, y, gate, pg, g1, scale1, shift1, name):
    l, d = x.shape
    tm = _tile(l, 256)

    def body(x_ref, y_ref, gate_ref, pg_ref, g1_ref, sc_ref, sh_ref, x1_ref, h1_ref):
        ny, _ = _rownorm(y_ref[...])
        x1 = x_ref[...] + gate_ref[...] * (ny * pg_ref[...])
        x1_ref[...] = x1
        n1, _ = _rownorm(x1)
        h1_ref[...] = _bf(n1 * g1_ref[...] * (1.0 + sc_ref[...]) + sh_ref[...])

    v = _vec_spec(d)
    return pl.pallas_call(
        body, name=name, grid=(l // tm,),
        in_specs=[_row_spec(tm, d), _row_spec(tm, d), v, v, v, v, v],
        out_specs=[_row_spec(tm, d), _row_spec(tm, d)],
        out_shape=[jax.ShapeDtypeStruct((l, d), F32), jax.ShapeDtypeStruct((l, d), BF16)],
        compiler_params=_params(("parallel",)),
    )(x, y, gate, pg, g1, scale1, shift1)


def _post_loss(x1, y1, gate, pg, target, name):
    l, d = x1.shape
    tm = _tile(l, 256)

    def body(x_ref, y_ref, gate_ref, pg_ref, t_ref, loss_ref, dy_ref, dx_ref, dgate_ref, dpg_ref):
        first = pl.program_id(0) == 0
        y = y_ref[...]
        ny, ry = _rownorm(y)
        q = ny * pg_ref[...]
        x2 = x_ref[...] + gate_ref[...] * q
        e = x2 - t_ref[...]
        _acc(loss_ref, first, jnp.full((1, 128), 0.5 / d, F32) * jnp.sum(e * e))
        dx2 = e * (1.0 / d)
        dx_ref[...] = dx2
        _acc(dgate_ref, first, _colsum(dx2 * q))
        dq = dx2 * gate_ref[...]
        _acc(dpg_ref, first, _colsum(dq * ny))
        dny = dq * pg_ref[...]
        dy = ry * (dny - ny * jnp.mean(dny * ny, axis=-1, keepdims=True))
        dy_ref[...] = _bf(dy)

    v = _vec_spec(d)
    return pl.pallas_call(
        body, name=name, grid=(l // tm,),
        in_specs=[_row_spec(tm, d), _row_spec(tm, d), v, v, _row_spec(tm, d)],
        out_specs=[_vec_spec(128), _row_spec(tm, d), _row_spec(tm, d), v, v],
        out_shape=[jax.ShapeDtypeStruct((1, 128), F32), jax.ShapeDtypeStruct((l, d), BF16),
                   jax.ShapeDtypeStruct((l, d), F32), jax.ShapeDtypeStruct((1, d), F32),
                   jax.ShapeDtypeStruct((1, d), F32)],
        compiler_params=_params(("arbitrary",)),
    )(x1, y1, gate, pg, target)


def _pre_bwd(dh, dres, x, g, scale, name, post=None):
    l, d = x.shape
    tm = _tile(l, 256)
    with_post = post is not None

    def body(*refs):
        if with_post:
            (dh_ref, dres_ref, x_ref, g_ref, sc_ref, y_ref, gate_ref, pg_ref,
             dx_ref, dsc_ref, dsh_ref, dg_ref, dy_ref, dgate_ref, dpg_ref) = refs
        else:
            dh_ref, dres_ref, x_ref, g_ref, sc_ref, dx_ref, dsc_ref, dsh_ref, dg_ref = refs
        first = pl.program_id(0) == 0
        dh = dh_ref[...]
        n, r = _rownorm(x_ref[...])
        _acc(dsc_ref, first, _colsum(dh * (n * g_ref[...])))
        _acc(dsh_ref, first, _colsum(dh))
        dyn = dh * (1.0 + sc_ref[...])
        _acc(dg_ref, first, _colsum(dyn * n))
        dn = dyn * g_ref[...]
        dx = dres_ref[...] + r * (dn - n * jnp.mean(dn * n, axis=-1, keepdims=True))
        dx_ref[...] = dx
        if with_post:
            ny, ry = _rownorm(y_ref[...])
            _acc(dgate_ref, first, _colsum(dx * (ny * pg_ref[...])))
            dq = dx * gate_ref[...]
            _acc(dpg_ref, first, _colsum(dq * ny))
            dny = dq * pg_ref[...]
            dy_ref[...] = _bf(ry * (dny - ny * jnp.mean(dny * ny, axis=-1, keepdims=True)))

    v = _vec_spec(d)
    row = _row_spec(tm, d)
    vec_out = jax.ShapeDtypeStruct((1, d), F32)
    in_specs = [row, row, row, v, v]
    args = [dh, dres, x, g, scale]
    out_specs = [row, v, v, v]
    out_shape = [jax.ShapeDtypeStruct((l, d), F32), vec_out, vec_out, vec_out]
    if with_post:
        in_specs += [row, v, v]
        args += list(post)
        out_specs += [row, v, v]
        out_shape += [jax.ShapeDtypeStruct((l, d), BF16), vec_out, vec_out]
    return pl.pallas_call(
        body, name=name, grid=(l // tm,), in_specs=in_specs, out_specs=out_specs, out_shape=out_shape,
        compiler_params=_params(("arbitrary",)),
    )(*args)


def _softplus_parts(z):
    e = jnp.exp(-jnp.abs(z))
    den = 1.0 + e
    lb = jnp.minimum(z, 0.0) - jnp.log(den)
    sig = jnp.where(z >= 0.0, 1.0, e) * pl.reciprocal(den, approx=True)
    return lb, lb - z, sig


def _tri(cmp, n=HEAD):
    row = lax.broadcasted_iota(jnp.int32, (n, n), 0)
    col = lax.broadcasted_iota(jnp.int32, (n, n), 1)
    return cmp(row, col)


ATT_T = 256


def _attn_fwd(qkv, wb, name, hp=4, ride=None):
    l = qkv.shape[0]
    t = ATT_T
    nh, nq = wb // HEAD, l // t
    hp = min(hp, nh)
    ng, wg = nh // hp, hp * HEAD
    scale = 1.0 / math.sqrt(HEAD)

    def body(q_ref, k_ref, v_ref, o_ref, lk_ref):
        i = pl.program_id(1)
        valid = _tri(lambda r, c: c < r, t)
        m_gt = _bf(_tri(lambda r, c: r > c, t).astype(F32))

        def tile(j, carry, diag):
            rows = pl.ds(pl.multiple_of(j * t, t), t)
            cols = [slice(hh * HEAD, (hh + 1) * HEAD) for hh in range(hp)]
            zs = [_dot(q_ref[:, cs], k_ref[rows, cs], NT) * scale for cs in cols]
            lbs, lks = [], []
            for z in zs:
                lb, lk, _ = _softplus_parts(z)
                lbs.append(lb)
                lks.append(jnp.where(valid, lk, 0.0) if diag else lk)
            laters = [_dot(_bf(lk), m_gt) for lk in lks]
            ws = [jnp.exp(lb + later + run) for lb, later, (_, run) in zip(lbs, laters, carry)]
            if diag:
                ws = [jnp.where(valid, w, 0.0) for w in ws]
            return tuple((acc + _dot(_bf(w), v_ref[rows, cs]), run + jnp.sum(lk, axis=1, keepdims=True))
                         for w, lk, cs, (acc, run) in zip(ws, lks, cols, carry))

        zero = (jnp.zeros((t, HEAD), F32), jnp.zeros((t, 1), F32))
        carry = tile(i, (zero,) * hp, True)
        carry = lax.fori_loop(0, i, lambda s, c: tile(i - 1 - s, c, False), carry)
        for hh, (acc, run) in enumerate(carry):
            cs = slice(hh * HEAD, (hh + 1) * HEAD)
            o_ref[:, cs] = acc
            lk_ref[:, cs] = jnp.broadcast_to(run, (t, HEAD))

    blk = lambda off: pl.BlockSpec((t, wg), lambda h, i: (i, off + h))
    full = lambda off: pl.BlockSpec((l, wg), lambda h, i: (0, off + h))
    out = pl.BlockSpec((t, wg), lambda h, i: (i, h))
    outs = _call(body, name=name, grid=(ng, nq), in_specs=[blk(0), full(ng), full(2 * ng)], out_specs=[out, out],
                 out_shape=[jax.ShapeDtypeStruct((l, wb), F32), jax.ShapeDtypeStruct((l, wb), F32)],
                 args=(qkv, qkv, qkv), sem=("parallel", "arbitrary"), ride=ride)
    return outs[0], outs[1], outs[2:]


def _attn_bwd(qkv, proj, dcat, lktot, wa, wb, name, hp=2, ride=None):
    l = qkv.shape[0]
    t = ATT_T
    nh, nq = wb // HEAD, l // t
    hp = min(hp, nh)
    ng, wg = nh // hp, hp * HEAD
    scale = 1.0 / math.sqrt(HEAD)

    def body(q_ref, k_ref, v_ref, bz_ref, dc_ref, lt_ref, dq_ref, dkt_out, dvt_out, do_s, qt_s, dot_s,
             dkt_ref, dvt_ref, out_sems):
        i = pl.program_id(1)

        @pl.when(i == 0)
        def _():
            dkt_ref[...] = jnp.zeros_like(dkt_ref)
            dvt_ref[...] = jnp.zeros_like(dvt_ref)

        do = dc_ref[...] * _silu(bz_ref[...])
        do_s[...] = _bf(do)
        for hh in range(hp):
            cs = slice(hh * HEAD, (hh + 1) * HEAD)
            qt_s[hh] = _bf(q_ref[:, cs].astype(F32).T * scale)
            dot_s[hh] = _bf(do[:, cs].T)
        valid = _tri(lambda r, c: c < r, t)
        m_le = _bf(_tri(lambda r, c: r <= c, t).astype(F32))
        m_lt = _bf(_tri(lambda r, c: r < c, t).astype(F32))

        def tile(j, carry, diag):
            rows = pl.ds(pl.multiple_of(j * t, t), t)
            heads = range(hp)
            cols = [slice(hh * HEAD, (hh + 1) * HEAD) for hh in heads]
            zs = [_dot(q_ref[:, cs], k_ref[rows, cs], NT) * scale for cs in cols]
            dws = [_dot(do_s[:, cs], v_ref[rows, cs], NT) for cs in cols]
            lbs, lks, sigs = [], [], []
            for z in zs:
                lb, lk, sig = _softplus_parts(z)
                lbs.append(lb)
                lks.append(jnp.where(valid, lk, 0.0) if diag else lk)
                sigs.append(sig)
            pins = [_dot(_bf(lk), m_le) for lk in lks]
            ws = [jnp.exp(lbs[hh] + (lt_ref[:, hh * HEAD:hh * HEAD + 1] - carry[hh][1]) - pins[hh]) for hh in heads]
            if diag:
                ws = [jnp.where(valid, w, 0.0) for w in ws]
            das = [dw * w for dw, w in zip(dws, ws)]
            pexs = [_dot(_bf(da), m_lt) for da in das]
            dzs = [das[hh] - sigs[hh] * (das[hh] + carry[hh][2] + pexs[hh]) for hh in heads]
            if diag:
                dzs = [jnp.where(valid, dz, 0.0) for dz in dzs]
            dzs = [_bf(dz) for dz in dzs]
            out = []
            for hh in heads:
                dkt, dvt = _dot(qt_s[hh], dzs[hh]), _dot(dot_s[hh], _bf(ws[hh]))
                for half in range(t // HEAD):
                    dkt_ref[hh, sub * j + half] += dkt[:, half * HEAD:(half + 1) * HEAD]
                    dvt_ref[hh, sub * j + half] += dvt[:, half * HEAD:(half + 1) * HEAD]
                dq, cpre, ppre = carry[hh]
                out.append((dq + _dot(dzs[hh], k_ref[rows, cols[hh]]), cpre + jnp.sum(lks[hh], axis=1, keepdims=True),
                            ppre + pexs[hh][:, t - 1:] + das[hh][:, t - 1:]))
            return tuple(out)

        zero = (jnp.zeros((t, HEAD), F32), jnp.zeros((t, 1), F32), jnp.zeros((t, 1), F32))
        carry = lax.fori_loop(0, i, lambda j, c: tile(j, c, False), (zero,) * hp)
        carry = tile(i, carry, True)
        for hh in range(hp):
            dq_ref[:, hh * HEAD:(hh + 1) * HEAD] = carry[hh][0] * scale

        @pl.when(i == nq - 1)
        def _():
            heads = pl.ds(pl.program_id(0) * hp, hp)
            copies = [pltpu.make_async_copy(dkt_ref, dkt_out.at[heads], out_sems.at[0]),
                      pltpu.make_async_copy(dvt_ref, dvt_out.at[heads], out_sems.at[1])]
            for cp in copies:
                cp.start()
            for cp in copies:
                cp.wait()

    sub = t // HEAD
    blk = lambda off: pl.BlockSpec((t, wg), lambda h, i: (i, off + h))
    full = lambda off: pl.BlockSpec((l, wg), lambda h, i: (0, off + h))
    acc_shape = jax.ShapeDtypeStruct((nh, l // HEAD, HEAD, HEAD), F32)
    acc_scratch = pltpu.VMEM((hp, l // HEAD, HEAD, HEAD), F32)
    outs = _call(
        body, name=name, grid=(ng, nq),
        in_specs=[blk(0), full(ng), full(2 * ng), blk(3 * wa // wg), blk(wa // wg), blk(0)],
        out_specs=[blk(0), ANY, ANY], out_shape=[jax.ShapeDtypeStruct((l, wb), F32), acc_shape, acc_shape],
        scratch_shapes=[pltpu.VMEM((t, wg), BF16), pltpu.VMEM((hp, HEAD, t), BF16), pltpu.VMEM((hp, HEAD, t), BF16),
                        acc_scratch, acc_scratch, pltpu.SemaphoreType.DMA((2,))],
        args=(qkv, qkv, qkv, proj, dcat, lktot), sem=("parallel", "arbitrary"), ride=ride)
    return outs[0], outs[1], outs[2], outs[3:]


def _sgu_heads(v, g_ref, w_ref, bt_ref, nh):
    keep = _tri(lambda r, c: r >= c)
    out = []
    for h in range(nh):
        cols = slice(h * HEAD, (h + 1) * HEAD)
        nv, r = _rownorm(v[:, cols])
        wm = jnp.where(keep, w_ref[h], 0.0)
        s = _dot(_bf(wm), _bf(nv * g_ref[:, cols])) + bt_ref[:, h:h + 1]
        out.append((nv, r, wm, s))
    return out


def _sgu_fwd(proj, out_b, norm_g, sgu_w, sgu_bt, wa, wb, name):
    l, n = proj.shape
    nh = wa // HEAD

    def body(au_ref, av_ref, az_ref, bz_ref, ob_ref, g_ref, w_ref, bt_ref, cat_ref):
        u, v, sz = _gelu(au_ref[...]), _gelu(av_ref[...]), _silu(az_ref[...])
        for h, (_, _, _, s) in enumerate(_sgu_heads(v, g_ref, w_ref, bt_ref, nh)):
            cols = slice(h * HEAD, (h + 1) * HEAD)
            cat_ref[:, cols] = _bf(u[:, cols] * s * sz[:, cols])
        cat_ref[:, wa:] = _bf(ob_ref[...] * _silu(bz_ref[...]))

    a_blk = lambda j: pl.BlockSpec((HEAD, wa), lambda i: (i, j))
    return pl.pallas_call(
        body, name=name, grid=(l // HEAD,),
        in_specs=[a_blk(0), a_blk(1), a_blk(2), a_blk(3), pl.BlockSpec((HEAD, wb), lambda i: (i, 0)),
                  _vec_spec(wa), pl.BlockSpec((nh, HEAD, HEAD), lambda i: (0, 0, 0)),
                  pl.BlockSpec((HEAD, nh), lambda i: (0, 0))],
        out_specs=pl.BlockSpec((HEAD, wa + wb), lambda i: (i, 0)),
        out_shape=jax.ShapeDtypeStruct((l, wa + wb), BF16),
        compiler_params=_params(("parallel",)),
    )(proj, proj, proj, proj, out_b, norm_g, sgu_w, sgu_bt)


def _sgu_bwd(proj, out_b, dcat, dq, dk, dv, norm_g, sgu_w, sgu_bt, wa, wb, name):
    l = proj.shape[0]
    n = 3 * wa + 4 * wb
    nh = wa // HEAD

    def body(au_ref, av_ref, az_ref, bz_ref, ob_ref, dc_ref, dq_ref, dk_ref, dv_ref, g_ref, w_ref, wt_ref, bt_ref,
             dp_ref, dw_ref, dbt_ref, dg_ref):
        first = pl.program_id(0) == 0
        keep = _tri(lambda r, c: r >= c)
        au, av, az = au_ref[...], av_ref[...], az_ref[...]
        u, v, sz = _gelu(au), _gelu(av), _silu(az)
        dgelu_u, dgelu_v, dsilu_z = _gelu_grad(au), _gelu_grad(av), _silu_grad(az)
        heads = _sgu_heads(v, g_ref, w_ref, bt_ref, nh)
        cols = [slice(h * HEAD, (h + 1) * HEAD) for h in range(nh)]
        dss = []
        for h, (nv, r, wm, s) in enumerate(heads):
            dca, uh, szh = dc_ref[:, cols[h]], u[:, cols[h]], sz[:, cols[h]]
            dp_ref[:, cols[h]] = _bf(dca * s * szh * dgelu_u[:, cols[h]])
            dp_ref[:, 2 * wa + h * HEAD:2 * wa + (h + 1) * HEAD] = _bf(dca * uh * s * dsilu_z[:, cols[h]])
            dss.append(dca * uh * szh)
        dws = [_dot(_bf(ds), _bf(nv * g_ref[:, cs]), NT) for ds, cs, (nv, _, _, _) in zip(dss, cols, heads)]
        keep_t = _tri(lambda r, c: r <= c)
        dvhs = [_dot(_bf(jnp.where(keep_t, wt_ref[h], 0.0)), _bf(dss[h])) for h in range(nh)]
        dg_parts = []
        for h, (nv, r, wm, s) in enumerate(heads):
            _acc(dw_ref.at[h], first, jnp.where(keep, dws[h], 0.0))
            _acc(dbt_ref.at[:, h:h + 1], first, jnp.sum(dss[h], axis=1, keepdims=True))
            dg_parts.append(_colsum(dvhs[h] * nv))
            dnv = dvhs[h] * g_ref[:, cols[h]]
            dvv = r * (dnv - nv * jnp.mean(dnv * nv, axis=-1, keepdims=True))
            dp_ref[:, wa + h * HEAD:wa + (h + 1) * HEAD] = _bf(dvv * dgelu_v[:, cols[h]])
        _acc(dg_ref, first, jnp.concatenate(dg_parts, axis=1))
        base = 3 * wa
        dp_ref[:, base:base + wb] = _bf(dq_ref[...])
        for h in range(wb // HEAD):
            dp_ref[:, base + wb + h * HEAD:base + wb + (h + 1) * HEAD] = _bf(dk_ref[h, 0].T)
            dp_ref[:, base + 2 * wb + h * HEAD:base + 2 * wb + (h + 1) * HEAD] = _bf(dv_ref[h, 0].T)
        dp_ref[:, base + 3 * wb:] = _bf(dc_ref[:, wa:] * ob_ref[...] * _silu_grad(bz_ref[...]))

    a_blk = lambda j: pl.BlockSpec((HEAD, wa), lambda i: (i, j))
    b_blk = pl.BlockSpec((HEAD, wb), lambda i: (i, 0))
    t_blk = pl.BlockSpec((wb // HEAD, 1, HEAD, HEAD), lambda i: (0, i, 0, 0))
    w_spec = pl.BlockSpec((nh, HEAD, HEAD), lambda i: (0, 0, 0))
    bt_spec = pl.BlockSpec((HEAD, nh), lambda i: (0, 0))
    return pl.pallas_call(
        body, name=name, grid=(l // HEAD,),
        in_specs=[a_blk(0), a_blk(1), a_blk(2), a_blk(3), b_blk, pl.BlockSpec((HEAD, wa + wb), lambda i: (i, 0)),
                  b_blk, t_blk, t_blk, _vec_spec(wa), w_spec, w_spec, bt_spec],
        out_specs=[pl.BlockSpec((HEAD, n), lambda i: (i, 0)), w_spec, bt_spec, _vec_spec(wa)],
        out_shape=[jax.ShapeDtypeStruct((l, n), BF16), jax.ShapeDtypeStruct((nh, HEAD, HEAD), F32),
                   jax.ShapeDtypeStruct((HEAD, nh), F32), jax.ShapeDtypeStruct((1, wa), F32)],
        compiler_params=_params(("arbitrary",)),
    )(proj, proj, proj, proj, out_b, dcat, dq, dk, dv, norm_g, sgu_w, sgu_w.transpose(0, 2, 1), sgu_bt)


def _ssm_discretise(lr, li, ldt, br, bi):
    dt = jnp.exp(ldt)
    mag = jnp.exp(lr * dt)
    a_re = mag * jnp.cos(li * dt)
    a_im = mag * jnp.sin(li * dt)
    den = lr * lr + li * li
    nr = a_re - 1.0
    coef_re = (nr * lr + a_im * li) / den
    coef_im = (a_im * lr - nr * li) / den
    return a_re, a_im, coef_re * br - coef_im * bi, coef_re * bi + coef_im * br


def _ssm_prep(lr, li, ldt, br, bi, lr_row, li_row, ldt_row, name):
    s, c = br.shape

    def body(lr_ref, li_ref, ldt_ref, br_ref, bi_ref, lrr_ref, lir_ref, ldtr_ref, bbr_ref, bbi_ref, tr_ref, ti_ref):
        _, _, bbr, bbi = _ssm_discretise(lr_ref[...], li_ref[...], ldt_ref[...], br_ref[...], bi_ref[...])
        bbr_ref[...] = bbr
        bbi_ref[...] = bbi
        row = lax.broadcasted_iota(jnp.int32, (SCAN_ROWS, 1), 0)
        blk, r = jnp.right_shift(row, 3), jnp.bitwise_and(row, 7)
        kind, rev = jnp.bitwise_and(blk, 3), blk >= 4
        step = jnp.left_shift(1, kind)
        n = jnp.where(kind < 3, step, jnp.where(rev, 8 - r, r + 1)).astype(F32)
        keep = (kind == 3) | (rev & (r < 8 - step)) | (jnp.logical_not(rev) & (r >= step))
        dt = jnp.exp(ldtr_ref[...])
        mag = jnp.exp(n * (lrr_ref[...] * dt))
        ang = n * (lir_ref[...] * dt)
        tr_ref[...] = jnp.where(keep, mag * jnp.cos(ang), 0.0)
        ti_ref[...] = jnp.where(keep, jnp.where(rev, -1.0, 1.0) * mag * jnp.sin(ang), 0.0)

    col = jax.ShapeDtypeStruct((s, c), F32)
    row = jax.ShapeDtypeStruct((SCAN_ROWS, s), F32)
    return pl.pallas_call(body, name=name, out_shape=[col, col, row, row])(
        lr, li, ldt, br, bi, lr_row, li_row, ldt_row)


def _ssm_prep_bwd(lr, li, ldt, br, bi, da_re, da_im, dbb_re, dbb_im, p, name):
    s, c = br.shape

    def body(lr_ref, li_ref, ldt_ref, br_ref, bi_ref, dar_ref, dai_ref, dbr_ref, dbi_ref,
             dlr_ref, dli_ref, dldt_ref, dbre_ref, dbim_ref):
        args = (lr_ref[...], li_ref[...], ldt_ref[...], br_ref[...], bi_ref[...])
        _, vjp = jax.vjp(_ssm_discretise, *args)
        dlr, dli, dldt, dbr, dbi = vjp((dar_ref[...], dai_ref[...], dbr_ref[...], dbi_ref[...]))
        dlr_ref[...] = dlr
        dli_ref[...] = dli
        dbre_ref[...] = dbr
        dbim_ref[...] = dbi
        idx = lax.broadcasted_iota(jnp.int32, (s, s // p), 0)
        grp = lax.broadcasted_iota(jnp.int32, (s, s // p), 1)
        own = (idx >= grp * p) & (idx < (grp + 1) * p)
        dldt_ref[...] = _colsum(jnp.where(own, dldt, 0.0))

    col1 = jax.ShapeDtypeStruct((s, 1), F32)
    colc = jax.ShapeDtypeStruct((s, c), F32)
    return pl.pallas_call(
        body, name=name, out_shape=[col1, col1, jax.ShapeDtypeStruct((1, s // p), F32), colc, colc],
    )(lr, li, ldt, br, bi, da_re, da_im, dbb_re, dbb_im)


SCAN_ROWS = 64


def _scan_groups(xr, xi, tr_ref, ti_ref, cr, ci, reverse):
    ng = xr.shape[0] // 8
    base = SCAN_ROWS // 2 if reverse else 0
    pr, pi = tr_ref[base + 24:base + 32, :], ti_ref[base + 24:base + 32, :]
    edge = slice(0, 1) if reverse else slice(7, 8)
    out_r, out_i = [None] * ng, [None] * ng
    for g in (range(ng - 1, -1, -1) if reverse else range(ng)):
        sr, si = xr[8 * g:8 * g + 8, :], xi[8 * g:8 * g + 8, :]
        for k in range(3):
            ar, ai = tr_ref[base + 8 * k:base + 8 * k + 8, :], ti_ref[base + 8 * k:base + 8 * k + 8, :]
            shift = 8 - (1 << k) if reverse else 1 << k
            rr, ri = pltpu.roll(sr, shift, 0), pltpu.roll(si, shift, 0)
            sr, si = sr + ar * rr - ai * ri, si + ar * ri + ai * rr
        sr, si = sr + pr * cr - pi * ci, si + pr * ci + pi * cr
        cr, ci = sr[edge, :], si[edge, :]
        out_r[g], out_i[g] = sr, si
    return jnp.concatenate(out_r, axis=0), jnp.concatenate(out_i, axis=0), cr, ci


def _ssm_fwd(proj, bbd, ccd, pw_re, pw_im, d_skip, w, name):
    l = proj.shape[0]
    nb, cw, ns2 = bbd.shape
    ns = ns2 // 2
    nc = l // SSM_T

    def body(u_ref, bbd_ref, ccd_ref, pr_ref, pi_ref, d_ref, y_ref, hsr_ref, hsi_ref, h_ref, hr_s, hi_s):
        @pl.when(pl.program_id(1) == 0)
        def _():
            hr_s[...] = jnp.zeros_like(hr_s)
            hi_s[...] = jnp.zeros_like(hi_s)

        hsr_ref[...] = hr_s[...].reshape(hsr_ref.shape)
        hsi_ref[...] = hi_s[...].reshape(hsi_ref.shape)
        u = u_ref[...]
        bu = _dot(_bf(u), bbd_ref[0])
        hr, hi, cr, ci = _scan_groups(bu[:, :ns], bu[:, ns:], pr_ref, pi_ref, hr_s[...], hi_s[...], False)
        hr_s[...] = cr
        hi_s[...] = ci
        h_bf = _bf(jnp.concatenate([hr, hi], axis=1))
        h_ref[...] = h_bf
        y_ref[...] = _dot(h_bf, ccd_ref[0]) + d_ref[...] * u

    tab = pl.BlockSpec((SCAN_ROWS, ns), lambda b, k: (0, b))
    return pl.pallas_call(
        body, name=name, grid=(nb, nc),
        in_specs=[pl.BlockSpec((SSM_T, cw), lambda b, k: (k, b)),
                  pl.BlockSpec((1, cw, ns2), lambda b, k: (b, 0, 0)),
                  pl.BlockSpec((1, ns2, cw), lambda b, k: (b, 0, 0)),
                  tab, tab, pl.BlockSpec((1, cw), lambda b, k: (0, b))],
        out_specs=[pl.BlockSpec((SSM_T, cw), lambda b, k: (k, b)),
                   pl.BlockSpec((1, 1, ns), lambda b, k: (k, 0, b)), pl.BlockSpec((1, 1, ns), lambda b, k: (k, 0, b)),
                   pl.BlockSpec((SSM_T, ns2), lambda b, k: (k, b))],
        out_shape=[jax.ShapeDtypeStruct((l, w), F32), jax.ShapeDtypeStruct((nc, 1, nb * ns), F32),
                   jax.ShapeDtypeStruct((nc, 1, nb * ns), F32), jax.ShapeDtypeStruct((l, nb * ns2), BF16)],
        scratch_shapes=[pltpu.VMEM((1, ns), F32), pltpu.VMEM((1, ns), F32)],
        compiler_params=_params(("parallel", "arbitrary")),
    )(proj, bbd, ccd, pw_re, pw_im, d_skip)


def _ssm_bwd(proj, dy, hs_re, hs_im, h_all, bbd, ccd, pw_re, pw_im, d_skip, w, name, ride=None):
    l = proj.shape[0]
    nb, cw, ns2 = bbd.shape
    ns = ns2 // 2
    nc = l // SSM_T

    def body(u_ref, dy_ref, hsr_ref, hsi_ref, h_ref, bbd_ref, ccd_ref, pr_ref, pi_ref, d_ref,
             du_ref, dbbd_ref, dccd_ref, dar_ref, dai_ref, dd_ref, gr_s, gi_s):
        first = pl.program_id(1) == 0

        @pl.when(first)
        def _():
            gr_s[...] = jnp.zeros_like(gr_s)
            gi_s[...] = jnp.zeros_like(gi_s)

        u, dy = u_ref[...], dy_ref[...]
        dy_bf = _bf(dy)
        hr0, hi0 = hsr_ref[0], hsi_ref[0]
        h = h_ref[...].astype(F32)
        hr, hi = h[:, :ns], h[:, ns:]
        dh = _dot(dy_bf, ccd_ref[0], NT)
        gr, gi, gcr, gci = _scan_groups(dh[:, :ns], dh[:, ns:], pr_ref, pi_ref, gr_s[...], gi_s[...], True)
        gr_s[...] = gcr
        gi_s[...] = gci
        row0 = lax.broadcasted_iota(jnp.int32, hr.shape, 0) == 0
        pr_h = jnp.where(row0, hr0, pltpu.roll(hr, 1, 0))
        pi_h = jnp.where(row0, hi0, pltpu.roll(hi, 1, 0))
        _acc(dar_ref, first, _colsum(pr_h * gr + pi_h * gi))
        _acc(dai_ref, first, _colsum(pr_h * gi - pi_h * gr))
        g_bf = _bf(jnp.concatenate([gr, gi], axis=1))
        _acc(dbbd_ref.at[0], first, _dot(_bf(u.T), g_bf))
        _acc(dccd_ref.at[0], first, _dot(_bf(h.T), dy_bf))
        du_ref[...] = _bf(_dot(g_bf, bbd_ref[0], NT) + d_ref[...] * dy)
        _acc(dd_ref, first, _colsum(dy * u))

    rev = lambda b, k: (nc - 1 - k, b)
    outs = _call(
        body, name=name, grid=(nb, nc), ride=ride, sem=("parallel", "arbitrary"),
        args=(proj, dy, hs_re, hs_im, h_all, bbd, ccd, pw_re, pw_im, d_skip),
        in_specs=[pl.BlockSpec((SSM_T, cw), rev), pl.BlockSpec((SSM_T, cw), rev),
                  pl.BlockSpec((1, 1, ns), lambda b, k: (nc - 1 - k, 0, b)),
                  pl.BlockSpec((1, 1, ns), lambda b, k: (nc - 1 - k, 0, b)),
                  pl.BlockSpec((SSM_T, ns2), rev),
                  pl.BlockSpec((1, cw, ns2), lambda b, k: (b, 0, 0)),
                  pl.BlockSpec((1, ns2, cw), lambda b, k: (b, 0, 0)),
                  pl.BlockSpec((SCAN_ROWS, ns), lambda b, k: (0, b)), pl.BlockSpec((SCAN_ROWS, ns), lambda b, k: (0, b)),
                  pl.BlockSpec((1, cw), lambda b, k: (0, b))],
        out_specs=[pl.BlockSpec((SSM_T, cw), rev),
                   pl.BlockSpec((1, cw, ns2), lambda b, k: (b, 0, 0)),
                   pl.BlockSpec((1, ns2, cw), lambda b, k: (b, 0, 0)),
                   pl.BlockSpec((1, ns), lambda b, k: (0, b)), pl.BlockSpec((1, ns), lambda b, k: (0, b)),
                   pl.BlockSpec((1, cw), lambda b, k: (0, b))],
        out_shape=[jax.ShapeDtypeStruct((l, w), BF16), jax.ShapeDtypeStruct(bbd.shape, F32),
                   jax.ShapeDtypeStruct(ccd.shape, F32), jax.ShapeDtypeStruct((1, nb * ns), F32),
                   jax.ShapeDtypeStruct((1, nb * ns), F32), jax.ShapeDtypeStruct((1, w), F32)],
        scratch_shapes=[pltpu.VMEM((1, ns), F32), pltpu.VMEM((1, ns), F32)])
    return (*outs[:6], outs[6:])


def _block_diag_b(bb_re, bb_im, g, p, c):
    nb = g // SSM_GB
    keep = _same_group(SSM_GB * c, c, SSM_GB * p, p)

    def one(bb):
        t = bb.reshape(nb, SSM_GB, p, c).transpose(0, 1, 3, 2).reshape(nb, SSM_GB * c, p)
        return jnp.where(keep, jnp.tile(t, (1, 1, SSM_GB)), 0.0)

    return jnp.concatenate([one(bb_re), one(bb_im)], axis=2)


def _same_group(rows, per_row, cols, per_col):
    r = lax.broadcasted_iota(jnp.int32, (rows, cols), 0) // per_row
    q = lax.broadcasted_iota(jnp.int32, (rows, cols), 1) // per_col
    return r == q


def _block_diag_c(c_re, c_im, g, p, c):
    nb = g // SSM_GB
    keep = _same_group(SSM_GB * p, p, SSM_GB * c, c)

    def one(cc):
        t = cc.reshape(nb, SSM_GB, c, p).transpose(0, 1, 3, 2).reshape(nb, SSM_GB * p, c)
        return jnp.where(keep, jnp.tile(t, (1, 1, SSM_GB)), 0.0)

    return jnp.concatenate([one(c_re), one(-c_im)], axis=1)


def _diag_of_b(dbbd, g, p, c):
    nb = g // SSM_GB
    keep = _same_group(SSM_GB * c, c, SSM_GB * p, p)

    def one(blk):
        d = jnp.where(keep, blk, 0.0).reshape(nb, SSM_GB * c, SSM_GB, p).sum(axis=2)
        return d.reshape(nb, SSM_GB, c, p).transpose(0, 1, 3, 2).reshape(g * p, c)

    half = SSM_GB * p
    return one(dbbd[:, :, :half]), one(dbbd[:, :, half:])


def _diag_of_c(dccd, g, p, c):
    nb = g // SSM_GB
    keep = _same_group(SSM_GB * p, p, SSM_GB * c, c)

    def one(blk):
        d = jnp.where(keep, blk, 0.0).reshape(nb, SSM_GB * p, SSM_GB, c).sum(axis=2)
        return d.reshape(nb, SSM_GB, p, c).transpose(0, 1, 3, 2).reshape(g, c, p)

    half = SSM_GB * p
    return one(dccd[:, :half]), -one(dccd[:, half:])


def _glu_fwd(y, proj, w_glu, b_glu, name):
    l, w = y.shape
    tm = _tile(l, 256)

    def body(y_ref, z_ref, w_ref, b_ref, o_ref):
        g = _gelu(y_ref[...])
        t = _dot(_bf(g), w_ref[...]) + b_ref[...]
        o_ref[...] = _bf(g * _sigmoid(t) * _silu(z_ref[...]))

    return pl.pallas_call(
        body, name=name, grid=(l // tm,),
        in_specs=[_row_spec(tm, w), pl.BlockSpec((tm, w), lambda i: (i, 1)),
                  pl.BlockSpec((w, w), lambda i: (0, 0)), _vec_spec(w)],
        out_specs=_row_spec(tm, w), out_shape=jax.ShapeDtypeStruct((l, w), BF16),
        compiler_params=_params(("parallel",)),
    )(y, proj, w_glu, b_glu)


def _glu_bwd(do, y, proj, w_glu, b_glu, name):
    l, w = y.shape
    tm = _tile(l, 512)
    nsteps = l // tm

    def body(do_ref, y_ref, z_ref, w_ref, b_ref, dy_ref, dz_ref, dw_ref, db_ref, dw_acc):
        i = pl.program_id(0)
        first = i == 0
        yv, z, do = y_ref[...], z_ref[...], do_ref[...]
        g = _gelu(yv)
        g_bf = _bf(g)
        sg = _sigmoid(_dot(g_bf, w_ref[...]) + b_ref[...])
        dyy = do * _silu(z)
        dz_ref[...] = _bf(do * g * sg * _silu_grad(z))
        dt = dyy * g * sg * (1.0 - sg)
        dt_bf = _bf(dt)
        dg = dyy * sg + _dot(dt_bf, w_ref[...], NT)
        dy_ref[...] = dg * _gelu_grad(yv)
        _acc(dw_acc, first, _dot(_bf(g.T), dt_bf))
        _acc(db_ref, first, _colsum(dt))

        @pl.when(i == nsteps - 1)
        def _():
            dw_ref[...] = _bf(dw_acc[...])

    return pl.pallas_call(
        body, name=name, grid=(nsteps,),
        in_specs=[_row_spec(tm, w), _row_spec(tm, w), pl.BlockSpec((tm, w), lambda i: (i, 1)),
                  pl.BlockSpec((w, w), lambda i: (0, 0)), _vec_spec(w)],
        out_specs=[_row_spec(tm, w), _row_spec(tm, w), pl.BlockSpec((w, w), lambda i: (0, 0)), _vec_spec(w)],
        out_shape=[jax.ShapeDtypeStruct((l, w), F32), jax.ShapeDtypeStruct((l, w), BF16),
                   jax.ShapeDtypeStruct((w, w), BF16), jax.ShapeDtypeStruct((1, w), F32)],
        scratch_shapes=[pltpu.VMEM((w, w), F32)],
        compiler_params=_params(("arbitrary",)),
    )(do, y, proj, w_glu, b_glu)


MOD_ROWS = 128


def _mod_fwd(cond_pad, w_mod, b_shard, name):
    nl, d, ncol = w_mod.shape
    tn = _tile(ncol, 512)

    def body(c_ref, w_ref, b_ref, o_ref):
        o_ref[0] = _dot(_bf(c_ref[...]), _bf(w_ref[0])) + b_ref[0]

    return pl.pallas_call(
        body, name=name, grid=(nl, ncol // tn),
        in_specs=[pl.BlockSpec((MOD_ROWS, d), lambda a, j: (0, 0)),
                  pl.BlockSpec((1, d, tn), lambda a, j: (a, 0, j)),
                  pl.BlockSpec((1, 1, tn), lambda a, j: (a, 0, j))],
        out_specs=pl.BlockSpec((1, MOD_ROWS, tn), lambda a, j: (a, 0, j)),
        out_shape=jax.ShapeDtypeStruct((nl, MOD_ROWS, ncol), F32),
        compiler_params=_params(("parallel", "parallel")),
    )(cond_pad, w_mod, b_shard)


def _mod_bwd(cond_pad_t, dmod_pad, name):
    nl, _, ncol = dmod_pad.shape
    d = cond_pad_t.shape[0]
    tn = _tile(ncol, 512)

    def body(c_ref, dm_ref, o_ref):
        o_ref[0] = _dot(_bf(c_ref[...]), _bf(dm_ref[0]))

    return pl.pallas_call(
        body, name=name, grid=(nl, ncol // tn),
        in_specs=[pl.BlockSpec((d, MOD_ROWS), lambda a, j: (0, 0)),
                  pl.BlockSpec((1, MOD_ROWS, tn), lambda a, j: (a, 0, j))],
        out_specs=pl.BlockSpec((1, d, tn), lambda a, j: (a, 0, j)),
        out_shape=jax.ShapeDtypeStruct((nl, d, ncol), F32),
        compiler_params=_params(("parallel", "parallel")),
    )(cond_pad_t, dmod_pad)


def _silu_rows(c2d, name):
    def body(c_ref, o_ref):
        o_ref[...] = _silu(c_ref[...])

    return pl.pallas_call(body, name=name, out_shape=jax.ShapeDtypeStruct(c2d.shape, F32))(c2d)


def _sum_leading(x, name):
    n, r, c = x.shape
    tr = _tile(r, max(16, (1 << 20) // (4 * c)), 16 if r % 16 == 0 else 8)

    def body(x_ref, o_ref):
        acc = x_ref[0].astype(F32)
        for k in range(1, n):
            acc = acc + x_ref[k].astype(F32)
        o_ref[...] = acc

    return pl.pallas_call(
        body, name=name, grid=(r // tr,),
        in_specs=[pl.BlockSpec((n, tr, c), lambda i: (0, i, 0))], out_specs=pl.BlockSpec((tr, c), lambda i: (i, 0)),
        out_shape=jax.ShapeDtypeStruct((r, c), F32), compiler_params=_params(("parallel",)),
    )(x)


def _adamw(w, gs, m, v, name):
    r, c = w.shape
    tr = _tile(r, max(8, (3 << 19) // (4 * c)), 8)
    ng = len(gs)

    def body(*refs):
        w_ref, g_refs, m_ref, v_ref = refs[0], refs[1:1 + ng], refs[1 + ng], refs[2 + ng]
        g_ref, d_ref, nm_ref, nv_ref = refs[3 + ng:]
        g = g_refs[0][...]
        for extra in g_refs[1:]:
            g = g + extra[...]
        g_ref[...] = g
        d_ref[...], nm_ref[...], nv_ref[...] = _adamw_math(w_ref[...], g, m_ref[...], v_ref[...])

    spec = pl.BlockSpec((tr, c), lambda i: (i, 0))
    shp = jax.ShapeDtypeStruct((r, c), F32)
    return pl.pallas_call(
        body, name=name, grid=(r // tr,), in_specs=[spec] * (3 + ng), out_specs=[spec] * 4,
        out_shape=[shp] * 4, compiler_params=_params(("parallel",)),
    )(w, *gs, m, v)


def _adamw_math(w, g, m, v):
    nm = ADAM_B1 * m + (1.0 - ADAM_B1) * g
    nv = ADAM_B2 * v + (1.0 - ADAM_B2) * (g * g)
    m_hat = nm / (1.0 - ADAM_B1 ** ADAM_STEP)
    v_hat = nv / (1.0 - ADAM_B2 ** ADAM_STEP)
    return -ADAM_LR * (m_hat / (jnp.sqrt(v_hat) + ADAM_EPS) + ADAM_WD * w), nm, nv


def _adamw_many(ws, gs, ms, vs, name):
    n = len(ws)

    def body(*refs):
        w_refs, g_refs, m_refs, v_refs = (refs[k * n:(k + 1) * n] for k in range(4))
        outs = refs[4 * n:]
        for i in range(n):
            outs[3 * i][...], outs[3 * i + 1][...], outs[3 * i + 2][...] = _adamw_math(
                w_refs[i][...], g_refs[i][...], m_refs[i][...], v_refs[i][...])

    out_shape = [jax.ShapeDtypeStruct(w.shape, F32) for w in ws for _ in range(3)]
    outs = pl.pallas_call(body, name=name, out_shape=out_shape, compiler_params=_params())(*ws, *gs, *ms, *vs)
    return [tuple(outs[3 * i:3 * i + 3]) for i in range(n)]


ANY = pl.BlockSpec(memory_space=pl.ANY)


def _flip(v, bit):
    return 1 - v if bit else v


def _allgather8_ops(x_ref, o_ref, send_sems, recv_sems, local_sem):
    mx, my, mc = lax.axis_index("x"), lax.axis_index("y"), lax.axis_index("c")
    me = 4 * mx + 2 * my + mc

    def mine():
        return pltpu.make_async_copy(x_ref, o_ref.at[me], local_sem)

    def copy(j, outgoing):
        peer = (_flip(mx, j & 4), _flip(my, j & 2), _flip(mc, j & 1))
        slot = me if outgoing else 4 * peer[0] + 2 * peer[1] + peer[2]
        return pltpu.make_async_remote_copy(
            src_ref=x_ref, dst_ref=o_ref.at[slot], send_sem=send_sems.at[j - 1], recv_sem=recv_sems.at[j - 1],
            device_id=peer, device_id_type=MESH)

    def start():
        mine().start()
        for j in range(1, 8):
            copy(j, True).start()

    def wait():
        for j in range(1, 8):
            copy(j, False).wait()
        mine().wait()

    return start, wait


def _ride_all8(x):
    return dict(xs=[x], shapes=[jax.ShapeDtypeStruct((8,) + x.shape, x.dtype)],
                sems=[pltpu.SemaphoreType.DMA((7,)), pltpu.SemaphoreType.DMA((7,)), pltpu.SemaphoreType.DMA],
                ops=lambda x_refs, o_refs, sems: _allgather8_ops(x_refs[0], o_refs[0], *sems))


def _ride_chip(xs, gather):
    return dict(xs=list(xs), shapes=_chip_exchange_shapes(xs, gather), sems=_chip_exchange_sems(len(xs)),
                ops=lambda x_refs, o_refs, sems: _chip_exchange_ops(x_refs, o_refs, *sems, gather))


def _allgather8(x, name):
    def body(x_ref, o_ref, *sems):
        start, wait = _allgather8_ops(x_ref, o_ref, *sems)
        start()
        wait()

    ride = _ride_all8(x)
    return pl.pallas_call(body, name=name, in_specs=[ANY], out_specs=ANY, out_shape=ride["shapes"][0],
                          scratch_shapes=ride["sems"])(x)


def _gather_halves(x, name):
    r = x.shape[0]
    half = r // 2

    def body(x_ref, o_ref, ici_send, ici_recv, d2d_send, d2d_recv, local_sem):
        mx, my, mc = lax.axis_index("x"), lax.axis_index("y"), lax.axis_index("c")
        k0 = 2 * mx + my
        mine = pl.ds(pl.multiple_of(mc * half, 16), half)
        theirs = pl.ds(pl.multiple_of((1 - mc) * half, 16), half)
        local = pltpu.make_async_copy(x_ref, o_ref.at[k0], local_sem)
        local.start()

        def chips(j):
            px, py = _flip(mx, j & 2), _flip(my, j & 1)
            return px, py, 2 * px + py

        def over_ici(j, outgoing):
            px, py, kp = chips(j)
            dst = o_ref.at[k0, mine] if outgoing else o_ref.at[kp, mine]
            return pltpu.make_async_remote_copy(
                src_ref=x_ref.at[mine], dst_ref=dst, send_sem=ici_send.at[j - 1], recv_sem=ici_recv.at[j - 1],
                device_id=(px, py, mc), device_id_type=MESH)

        def over_d2d(j, outgoing):
            _, _, kp = chips(j)
            rows = mine if outgoing else theirs
            return pltpu.make_async_remote_copy(
                src_ref=o_ref.at[kp, rows], dst_ref=o_ref.at[kp, rows], send_sem=d2d_send.at[j - 1],
                recv_sem=d2d_recv.at[j - 1], device_id=(mx, my, 1 - mc), device_id_type=MESH)

        for j in range(1, 4):
            over_ici(j, True).start()
        for j in range(1, 4):
            over_ici(j, False).wait_recv()
            over_d2d(j, True).start()
        for j in range(1, 4):
            over_ici(j, True).wait_send()
            over_d2d(j, True).wait_send()
            over_d2d(j, False).wait_recv()
        local.wait()

    dma3 = pltpu.SemaphoreType.DMA((3,))
    return pl.pallas_call(
        body, name=name, in_specs=[ANY], out_specs=ANY, out_shape=jax.ShapeDtypeStruct((4,) + x.shape, x.dtype),
        scratch_shapes=[dma3, dma3, dma3, dma3, pltpu.SemaphoreType.DMA])(x)


def _chip_exchange(xs, gather, name):
    n = len(xs)

    def body(*refs):
        start, wait = _chip_exchange_ops(refs[:n], refs[n:2 * n], *refs[2 * n:], gather)
        start()
        wait()

    return pl.pallas_call(
        body, name=name, in_specs=[ANY] * n, out_specs=[ANY] * n, out_shape=_chip_exchange_shapes(xs, gather),
        scratch_shapes=_chip_exchange_sems(n),
    )(*xs)


def _chip_exchange_shapes(xs, gather):
    return [jax.ShapeDtypeStruct(((4,) + x.shape) if gather else x.shape, x.dtype) for x in xs]


def _chip_exchange_sems(n):
    return [pltpu.SemaphoreType.DMA((3 * n,)), pltpu.SemaphoreType.DMA((3 * n,)), pltpu.SemaphoreType.DMA((n,))]


def _chip_exchange_ops(x_refs, o_refs, send_sems, recv_sems, local_sems, gather):
    n = len(x_refs)
    mx, my, mc = lax.axis_index("x"), lax.axis_index("y"), lax.axis_index("c")
    k0 = 2 * mx + my

    def local(a):
        src = x_refs[a] if gather else x_refs[a].at[k0]
        return pltpu.make_async_copy(src, o_refs[a].at[k0], local_sems.at[a])

    def copy(a, j, outgoing):
        px, py = _flip(mx, j & 2), _flip(my, j & 1)
        kp = 2 * px + py
        if outgoing:
            src = x_refs[a] if gather else x_refs[a].at[kp]
            dst = o_refs[a].at[k0]
        else:
            src = x_refs[a] if gather else x_refs[a].at[k0]
            dst = o_refs[a].at[kp]
        s = a * 3 + j - 1
        return pltpu.make_async_remote_copy(
            src_ref=src, dst_ref=dst, send_sem=send_sems.at[s], recv_sem=recv_sems.at[s],
            device_id=(px, py, mc), device_id_type=MESH)

    def start():
        for a in range(n):
            local(a).start()
            for j in range(1, 4):
                copy(a, j, True).start()

    def wait():
        for a in range(n):
            for j in range(1, 4):
                copy(a, j, False).wait()
            local(a).wait()

    return start, wait


def _call(body, *, name, grid, in_specs, out_specs, out_shape, args, scratch_shapes=(), sem=None, ride=None):
    if not ride:
        return pl.pallas_call(
            body, name=name, grid=grid, in_specs=list(in_specs), out_specs=list(out_specs), out_shape=list(out_shape),
            scratch_shapes=list(scratch_shapes), compiler_params=_params(sem))(*args)
    xs = [x for r in ride for x in r["xs"]]
    shapes = [s for r in ride for s in r["shapes"]]
    sems = [s for r in ride for s in r["sems"]]
    n_in, n_out, n_scr, nx = len(in_specs), len(out_specs), len(scratch_shapes), len(xs)

    def wrapped(*refs):
        ins, x_refs = refs[:n_in], refs[n_in:n_in + nx]
        outs = refs[n_in + nx:n_in + nx + n_out]
        lands = refs[n_in + nx + n_out:n_in + 2 * nx + n_out]
        rest = refs[n_in + 2 * nx + n_out:]
        scr, sem_refs = rest[:n_scr], rest[n_scr:]
        ops, xo, so = [], 0, 0
        for r in ride:
            nr, ns = len(r["xs"]), len(r["sems"])
            ops.append(r["ops"](x_refs[xo:xo + nr], lands[xo:xo + nr], sem_refs[so:so + ns]))
            xo, so = xo + nr, so + ns
        ids = [pl.program_id(a) for a in range(len(grid))]
        first = functools.reduce(jnp.logical_and, [i == 0 for i in ids])
        last = functools.reduce(jnp.logical_and, [i == g - 1 for i, g in zip(ids, grid)])

        @pl.when(first)
        def _():
            for start, _ in ops:
                start()

        body(*ins, *outs, *scr)

        @pl.when(last)
        def _():
            for _, wait in ops:
                wait()

    return pl.pallas_call(
        wrapped, name=name, grid=grid, in_specs=list(in_specs) + [ANY] * nx, out_specs=list(out_specs) + [ANY] * nx,
        out_shape=list(out_shape) + shapes, scratch_shapes=list(scratch_shapes) + sems,
        compiler_params=_params(("arbitrary",) * len(grid)))(*args, *xs)


def _sibling_exchange(xs, name):
    n = len(xs)

    def body(*refs):
        x_refs, o_refs = refs[:n], refs[n:2 * n]
        send_sems, recv_sems = refs[2 * n:]
        sib = (lax.axis_index("x"), lax.axis_index("y"), 1 - lax.axis_index("c"))
        copies = [pltpu.make_async_remote_copy(
            src_ref=x_refs[a], dst_ref=o_refs[a], send_sem=send_sems.at[a], recv_sem=recv_sems.at[a],
            device_id=sib, device_id_type=MESH) for a in range(n)]
        for cp in copies:
            cp.start()
        for cp in copies:
            cp.wait()

    return pl.pallas_call(
        body, name=name, in_specs=[ANY] * n, out_specs=[ANY] * n,
        out_shape=[jax.ShapeDtypeStruct(x.shape, x.dtype) for x in xs],
        scratch_shapes=[pltpu.SemaphoreType.DMA((n,)), pltpu.SemaphoreType.DMA((n,))],
    )(*xs)


PACK = 1024
PACK_ROWS = 512


def _pack(parts):
    flat = []
    for p in parts:
        v = p.reshape(-1).astype(F32)
        flat.append(jnp.pad(v, (0, (-v.shape[0]) % PACK)))
    total = sum(v.shape[0] for v in flat)
    flat.append(jnp.zeros(((-total) % (PACK_ROWS * 128),), F32))
    return jnp.concatenate(flat).reshape(-1, 128)


def _shard_columns(shards, lo, hi):
    width = shards.shape[2]
    out = []
    for k in range(shards.shape[0]):
        a, b = max(lo, k * width), min(hi, (k + 1) * width)
        if a < b:
            out.append(shards[k, :, a - k * width:b - k * width])
    return out


def _unpack_rows(gathered, shapes):
    flat = gathered.reshape(gathered.shape[0], -1)
    out, off = [], 0
    for shp in shapes:
        n = math.prod(shp)
        out.append(flat[:, off:off + n].reshape((flat.shape[0],) + tuple(shp)))
        off += n + (-n) % PACK
    return out


def _unpack(packed, shapes):
    flat = packed.reshape(-1)
    out, off = [], 0
    for shp in shapes:
        n = math.prod(shp)
        out.append(flat[off:off + n].reshape(shp))
        off += n + (-n) % PACK
    return out


def kernel(x, c, ln_pre_g, ln_post_g, w_mod, b_mod, w_in_ab, w_out_ab, sgu_norm_g, sgu_w, sgu_b, w_in_ssm, w_out_ssm, lam_re, lam_im, b_re, b_im, c_re, c_im, d_skip, log_dt, w_glu, b_glu, loss_target, m_ln_pre_g, m_ln_post_g, m_w_mod, m_b_mod, m_w_in_ab, m_w_out_ab, m_sgu_norm_g, m_sgu_w, m_sgu_b, m_w_in_ssm, m_w_out_ssm, m_lam_re, m_lam_im, m_b_re, m_b_im, m_c_re, m_c_im, m_d_skip, m_log_dt, m_w_glu, m_b_glu, v_ln_pre_g, v_ln_post_g, v_w_mod, v_b_mod, v_w_in_ab, v_w_out_ab, v_sgu_norm_g, v_sgu_w, v_sgu_b, v_w_in_ssm, v_w_out_ssm, v_lam_re, v_lam_im, v_b_re, v_b_im, v_c_re, v_c_im, v_d_skip, v_log_dt, v_w_glu, v_b_glu):
    given = dict(locals())
    mx, my, mc = lax.axis_index("x"), lax.axis_index("y"), lax.axis_index("c")
    me = 4 * mx + 2 * my + mc
    chip = 2 * mx + my

    _, l, d = x.shape
    x2, tgt = x[0], loss_target[0]
    n_in = w_in_ab.shape[2] * 4
    wa = wb = n_in // 7
    w = w_out_ssm.shape[1]
    g, p, cch = b_re.shape[1:]
    nmod = w_mod.shape[2]

    gw_in_ab = _gather_halves(_bf(w_in_ab[0]), "gather_w_in_ab")
    later_shards = [_bf(w_out_ab[0]), _bf(w_in_ssm[0]), _bf(w_out_ssm[0]), _bf(w_glu[0]), d_skip, b_glu]

    cond = _silu_rows(c.reshape(d // 128, 128), "cond_silu")
    cond_all = _allgather8(cond, "gather_cond").reshape(8, d)
    b_shard = lax.dynamic_slice(b_mod, (0, chip * nmod), (2, nmod)).reshape(2, 1, nmod)
    cond_pad = jnp.pad(cond_all, ((0, MOD_ROWS - 8), (0, 0)))
    modp = _mod_fwd(cond_pad, w_mod, b_shard, "mod_fwd")[:, :8]
    modp_all = _allgather8(modp.reshape(16, nmod), "gather_mod").reshape(4, 2, 2, 8, nmod)
    mine = lax.dynamic_index_in_dim(lax.dynamic_index_in_dim(modp_all, mc, 1, False), me, 2, False)
    mod = mine.transpose(1, 0, 2).reshape(2, 3 * d)
    shift = [mod[a:a + 1, :d] for a in range(2)]
    scale = [mod[a:a + 1, d:2 * d] for a in range(2)]
    gate = [mod[a:a + 1, 2 * d:] for a in range(2)]
    pre_g = [ln_pre_g[a:a + 1] for a in range(2)]
    post_g = [ln_post_g[a:a + 1] for a in range(2)]

    sgu_w0, sgu_bt = sgu_w[0], sgu_b[0].T
    h0 = _pre_fwd(x2, pre_g[0], scale[0], shift[0], "pre0_fwd")
    w_gates = jnp.concatenate(_shard_columns(gw_in_ab, 0, 3 * wa) + _shard_columns(gw_in_ab, 3 * wa + 3 * wb, n_in),
                              axis=1)
    w_qkv = jnp.concatenate(_shard_columns(gw_in_ab, 3 * wa, 3 * wa + 3 * wb), axis=1)
    proj0 = _matmul(h0, w_gates, "nn", F32, "proj0", tm=1024)
    qkv = _matmul(h0, w_qkv, "nn", BF16, "proj0_qkv", tm=1024)
    out_b, lktot, (gw_out_ab, gw_in_ssm, gw_out_ssm, gw_glu, g_dskip, g_bglu) = _attn_fwd(
        qkv, wb, "attn_fwd", hp=8, ride=[_ride_chip(later_shards, True)])
    wout_ab = gw_out_ab.reshape(wa + wb, d)
    win_ssm = gw_in_ssm.reshape(d, 2 * w)
    wout_ssm = jnp.concatenate([gw_out_ssm[k] for k in range(4)], axis=1)
    wglu = gw_glu.reshape(w, w)
    dskip_full = g_dskip.reshape(1, w)
    bglu_full = g_bglu.reshape(1, w)
    cat =_sgu_fwd(proj0, out_b, sgu_norm_g, sgu_w0, sgu_bt, wa, wb, "sgu_fwd")
    y0 = _matmul(cat, wout_ab, "nn", F32, "out0", tm=1024)
    x1, h1 = _post_pre_fwd(x2, y0, gate[0], post_g[0], pre_g[1], scale[1], shift[1], "post0_pre1_fwd")

    s = g * p
    lr_c, li_c = lam_re.reshape(s, 1), lam_im.reshape(s, 1)
    ldt_c = jnp.repeat(log_dt.reshape(g), p).reshape(s, 1)
    br_c, bi_c = b_re.reshape(s, cch), b_im.reshape(s, cch)
    bb_re, bb_im, pw_re, pw_im = _ssm_prep(lr_c, li_c, ldt_c, br_c, bi_c, lr_c.reshape(1, s), li_c.reshape(1, s),
                                           ldt_c.reshape(1, s), "ssm_prep")
    bbd = _bf(_block_diag_b(bb_re, bb_im, g, p, cch))
    ccd = _bf(_block_diag_c(c_re[0], c_im[0], g, p, cch))
    proj1 = _matmul(h1, win_ssm, "nn", F32, "proj1", tm=1024)
    y_ssm, hs_re, hs_im, h_all = _ssm_fwd(proj1, bbd, ccd, pw_re, pw_im, dskip_full, w, "ssm_fwd")
    o1 = _glu_fwd(y_ssm, proj1, wglu, bglu_full, "glu_fwd")
    y1 = _matmul(o1, wout_ssm, "nn", F32, "out1", tm=1024)
    loss_vec, dy1, dx2, dgate1, dpost1 = _post_loss(x1, y1, gate[1], post_g[1], tgt, "post1_loss")

    do1 = _matmul(dy1, wout_ssm, "nt", F32, "out1_dx", tm=1024)
    gr_wout_ssm = _matmul(o1, dy1, "tn", BF16, "out1_dw", tm=1024, tk=1024, n_split=4)
    dy_ssm, dz1, gr_wglu, gr_bglu = _glu_bwd(do1, y_ssm, proj1, wglu, bglu_full, "glu_bwd")
    du1, dbbd, dccd, da_re, da_im, gr_dskip, (ld_wout_ssm, ld_wglu) = _ssm_bwd(
        proj1, dy_ssm, hs_re, hs_im, h_all, bbd, ccd, pw_re, pw_im, dskip_full, w, "ssm_bwd",
        ride=[_ride_chip([gr_wout_ssm, gr_wglu.reshape(4, w // 4, w)], False)])
    dproj1 = jnp.concatenate([du1, dz1], axis=1)
    dh1 = _matmul(dproj1, win_ssm, "nt", F32, "proj1_dx", tm=1024)
    gr_win_ssm = _matmul(h1, dproj1, "tn", BF16, "proj1_dw", tm=1024, tn=1024, tk=1024)
    dx1, dscale1, dshift1, dpre1, dy0, dgate0, dpost0 = _pre_bwd(
        dh1, dx2, x1, pre_g[1], scale[1], "pre1_post0_bwd", post=(y0, gate[0], post_g[0]))

    dcat = _matmul(dy0, wout_ab, "nt", F32, "out0_dx", tm=1024)
    gr_wout_ab = _matmul(cat, dy0, "tn", BF16, "out0_dw", tm=1024, tn=1024, tk=1024)
    dbb_re, dbb_im = _diag_of_b(dbbd, g, p, cch)
    dc_re, dc_im = _diag_of_c(dccd, g, p, cch)
    part_a = [loss_vec[:, :1], dpre1, dpost0, dpost1, dgate0, dshift1, dscale1, dgate1, da_re, da_im,
              dbb_re, dbb_im, dc_re, dc_im, gr_dskip, gr_bglu]
    shapes_a = [a.shape for a in part_a]
    dq, dk, dv, (ld_win_ssm, ld_wout_ab, gath_a) = _attn_bwd(
        qkv, proj0, dcat, lktot, wa, wb, "attn_bwd", hp=4,
        ride=[_ride_chip([gr_win_ssm.reshape(4, d // 4, 2 * w), gr_wout_ab.reshape(4, (wa + wb) // 4, d)], False),
              _ride_all8(_pack(part_a))])
    dproj0, gr_sgu_w, gr_sgu_bt, gr_sgu_g = _sgu_bwd(proj0, out_b, dcat, dq, dk, dv, sgu_norm_g, sgu_w0, sgu_bt,
                                                     wa, wb, "sgu_bwd")
    part_b = [gr_sgu_g, gr_sgu_w, gr_sgu_bt.T]
    shapes_b = [a.shape for a in part_b]
    gr_win_ab_lo, (gath_b,) = _matmul(h0, dproj0, "tn", BF16, "proj0_dw_lo", tm=1024, tk=1024, tn=896, n_split=4,
                                      m_part=(0, 1, 2), ride=[_ride_all8(_pack(part_b))])
    gr_win_ab_hi, (ld_win_ab_lo,) = _matmul(
        h0, dproj0, "tn", BF16, "proj0_dw_hi", tm=1024, tk=1024, tn=896, n_split=4, m_part=(1, 1, 2),
        ride=[_ride_chip([gr_win_ab_lo], False)])
    dh0, (ld_win_ab_hi,) = _matmul(dproj0, gw_in_ab, "nt", F32, "proj0_dx", tm=1024,
                                   ride=[_ride_chip([gr_win_ab_hi], False)])
    grad_x, dscale0, dshift0, dpre0 = _pre_bwd(dh0, dx1, x2, pre_g[0], scale[0], "pre0_bwd")
    part_c = [dpre0, dshift0, dscale0]
    shapes_c = [a.shape for a in part_c]
    gath_c = _allgather8(_pack(part_c), "gather_small_tail")

    landed = [ld_wout_ab, ld_win_ssm, ld_wout_ssm, ld_wglu]
    big_names = ["w_in_ab", "w_out_ab", "w_in_ssm", "w_out_ssm", "w_glu"]
    sums = [jnp.concatenate([_sum_leading(ld_win_ab_lo, "sum_w_in_ab_lo"), _sum_leading(ld_win_ab_hi, "sum_w_in_ab_hi")],
                            axis=0)]
    sums += [_sum_leading(a, "sum_" + nm) for a, nm in zip(landed, big_names[1:])]
    sib = _sibling_exchange(sums, "sibling_grads")
    results = {}
    for nm, s_mine, s_sib in zip(big_names, sums, sib):
        shp = given[nm].shape
        two_d = lambda a: a.reshape(-1, shp[-1])
        outs = _adamw(two_d(given[nm]), [s_mine, s_sib], two_d(given["m_" + nm]), two_d(given["v_" + nm]),
                      "adamw_" + nm)
        results[nm] = [o.reshape(shp) for o in outs]

    (loss_s, g_pre1, g_post0, g_post1, g_gate0, g_shift1, g_scale1, g_gate1, s_da_re, s_da_im, s_dbb_re, s_dbb_im,
     g_c_re, g_c_im, g_dskip_full, g_bglu_full) = _unpack(_sum_leading(gath_a, "sum_small_a"), shapes_a)
    g_sgu_g, g_sgu_w, g_sgu_b = _unpack(_sum_leading(gath_b, "sum_small_b"), shapes_b)
    g_pre0, g_shift0, g_scale0 = _unpack(_sum_leading(gath_c, "sum_small_c"), shapes_c)
    loss = loss_s.reshape(())
    g_pre = jnp.concatenate([g_pre0, g_pre1], axis=0)
    g_post = jnp.concatenate([g_post0, g_post1], axis=0)
    g_bmod = jnp.concatenate([jnp.concatenate([g_shift0, g_scale0, g_gate0], axis=1),
                              jnp.concatenate([g_shift1, g_scale1, g_gate1], axis=1)], axis=0)

    g_lr, g_li, g_ldt, g_br, g_bi = _ssm_prep_bwd(lr_c, li_c, ldt_c, br_c, bi_c, s_da_re.reshape(s, 1),
                                                  s_da_im.reshape(s, 1), s_dbb_re, s_dbb_im, p, "ssm_prep_bwd")
    small = {
        "ln_pre_g": g_pre, "ln_post_g": g_post, "b_mod": g_bmod, "sgu_norm_g": g_sgu_g,
        "sgu_w": g_sgu_w.reshape(sgu_w.shape), "sgu_b": g_sgu_b.reshape(sgu_b.shape),
        "lam_re": g_lr.reshape(lam_re.shape), "lam_im": g_li.reshape(lam_im.shape),
        "b_re": g_br.reshape(b_re.shape), "b_im": g_bi.reshape(b_im.shape),
        "c_re": g_c_re.reshape(c_re.shape), "c_im": g_c_im.reshape(c_im.shape),
        "d_skip": lax.dynamic_slice(g_dskip_full, (0, chip * (w // 4)), (1, w // 4)),
        "log_dt": g_ldt.reshape(log_dt.shape),
        "b_glu": lax.dynamic_slice(g_bglu_full, (0, chip * (w // 4)), (1, w // 4)),
    }
    flat2 = lambda a: a.reshape(-1, a.shape[-1])
    wide = ("b_re", "b_im")
    for tag, group in (("adamw_small", [nm for nm in small if nm not in wide]), ("adamw_small_b", list(wide))):
        outs = _adamw_many([flat2(given[nm]) for nm in group], [flat2(small[nm]) for nm in group],
                           [flat2(given["m_" + nm]) for nm in group], [flat2(given["v_" + nm]) for nm in group], tag)
        for nm, trio in zip(group, outs):
            results[nm] = [small[nm]] + [o.reshape(given[nm].shape) for o in trio]

    rows_a = _unpack_rows(gath_a, shapes_a)
    rows_c = _unpack_rows(gath_c, shapes_c)
    dmod_rows = jnp.concatenate([rows_c[1], rows_c[2], rows_a[4], rows_a[5], rows_a[6], rows_a[7]],
                                axis=2).reshape(8, 2, 3 * d)
    dmod_shard = lax.dynamic_slice(dmod_rows, (0, 0, chip * nmod), (8, 2, nmod)).transpose(1, 0, 2)
    dmod_pad = jnp.pad(dmod_shard, ((0, 0), (0, MOD_ROWS - 8), (0, 0)))
    gr_wmod = _mod_bwd(cond_pad.T, dmod_pad, "mod_bwd")
    two_d = lambda a: a.reshape(-1, nmod)
    outs = _adamw(two_d(w_mod), [two_d(gr_wmod)], two_d(m_w_mod), two_d(v_w_mod), "adamw_w_mod")
    results["w_mod"] = [o.reshape(w_mod.shape) for o in outs]

    names = ["ln_pre_g", "ln_post_g", "w_mod", "b_mod", "w_in_ab", "w_out_ab", "sgu_norm_g", "sgu_w", "sgu_b",
             "w_in_ssm", "w_out_ssm", "lam_re", "lam_im", "b_re", "b_im", "c_re", "c_im", "d_skip", "log_dt",
             "w_glu", "b_glu"]
    return (loss, grad_x[None], *[results[nm][0] for nm in names], *[results[nm][1] for nm in names],
            *[results[nm][2] for nm in names], *[results[nm][3] for nm in names])
```

```python
import functools
import math

import jax
import jax.numpy as jnp
from jax import lax
from jax.experimental import pallas as pl
from jax.experimental.pallas import tpu as pltpu

F32 = jnp.float32
BF16 = jnp.bfloat16
MESH = pl.DeviceIdType.MESH

EPS = 1e-6
HEAD = 128
SSM_T = 512
SSM_GB = 16
ADAM_LR, ADAM_B1, ADAM_B2, ADAM_EPS, ADAM_WD, ADAM_STEP = 0.001, 0.9, 0.999, 1e-08, 0.01, 10
VMEM_LIMIT = 56 * 1024 * 1024

NN = (((1,), (0,)), ((), ()))
NT = (((1,), (1,)), ((), ()))
TN = (((0,), (0,)), ((), ()))


def _params(sem=None):
    return pltpu.CompilerParams(dimension_semantics=sem, vmem_limit_bytes=VMEM_LIMIT)


def _dot(a, b, dims=NN):
    return lax.dot_general(a, b, dims, preferred_element_type=F32)


def _bf(x):
    return x.astype(BF16)


def _gelu(x):
    k = math.sqrt(2.0 / math.pi)
    t = jnp.tanh(k * (x + 0.044715 * x * x * x))
    return 0.5 * x * (1.0 + t)


def _gelu_grad(x):
    k = math.sqrt(2.0 / math.pi)
    x2 = x * x
    t = jnp.tanh(k * (x + 0.044715 * x * x2))
    return 0.5 * (1.0 + t) + 0.5 * x * (1.0 - t * t) * k * (1.0 + 3.0 * 0.044715 * x2)


def _sigmoid(x):
    return 1.0 / (1.0 + jnp.exp(-x))


def _silu(x):
    return x * _sigmoid(x)


def _silu_grad(x):
    s = _sigmoid(x)
    return s * (1.0 + x * (1.0 - s))


def _tile(n, t, mult=128):
    if n <= t:
        return n
    for cand in range(t - t % mult, 0, -mult):
        if n % cand == 0:
            return cand
    raise ValueError((n, t, mult))


def _matmul(a, b, mode, out_dtype, name, tm=512, tn=512, tk=2048, n_split=1, ride=None, m_part=None):
    b_sharded = b.ndim == 3
    if mode == "nn":
        (m, kk), (_, n) = a.shape, b.shape
    elif b_sharded:
        assert mode == "nt"
        (m, kk), n, tk = a.shape, b.shape[1], b.shape[2]
    elif mode == "nt":
        (m, kk), (n, _) = a.shape, b.shape
    else:
        (kk, m), (_, n) = a.shape, b.shape
    m_off = 0
    if m_part is not None:
        assert mode == "tn"
        first, count, parts = m_part
        tm = _tile(m // parts, tm)
        m_off = first * (m // parts) // tm
        m = count * (m // parts)
    tm, tk = _tile(m, tm), _tile(kk, tk)
    ns = n // n_split
    tn = _tile(ns, tn)
    nk = kk // tk
    dims = {"nn": NN, "nt": NT, "tn": TN}[mode]

    def body(a_ref, b_ref, o_ref, acc_ref):
        k = pl.program_id(2)
        part = _dot(_bf(a_ref[...]), _bf(b_ref[0] if b_sharded else b_ref[...]), dims)

        @pl.when(k == 0)
        def _():
            acc_ref[...] = part

        @pl.when(k > 0)
        def _():
            acc_ref[...] += part

        @pl.when(k == nk - 1)
        def _():
            o_ref[...] = acc_ref[...].astype(out_dtype).reshape(o_ref.shape)

    if mode == "nn":
        a_spec = pl.BlockSpec((tm, tk), lambda i, j, k: (i, k))
        b_spec = pl.BlockSpec((tk, tn), lambda i, j, k: (k, j))
    elif mode == "nt":
        a_spec = pl.BlockSpec((tm, tk), lambda i, j, k: (i, k))
        b_spec = (pl.BlockSpec((1, tn, tk), lambda i, j, k: (k, j, 0)) if b_sharded
                  else pl.BlockSpec((tn, tk), lambda i, j, k: (j, k)))
    else:
        a_spec = pl.BlockSpec((tk, tm), lambda i, j, k: (k, i + m_off))
        b_spec = pl.BlockSpec((tk, tn), lambda i, j, k: (k, j))
    if n_split == 1:
        out_shape = jax.ShapeDtypeStruct((m, n), out_dtype)
        o_spec = pl.BlockSpec((tm, tn), lambda i, j, k: (i, j))
    else:
        per = ns // tn
        out_shape = jax.ShapeDtypeStruct((n_split, m, ns), out_dtype)
        o_spec = pl.BlockSpec((1, tm, tn), lambda i, j, k: (j // per, i, j % per))
    outs = _call(body, name=name, grid=(m // tm, n // tn, nk), in_specs=[a_spec, b_spec], out_specs=[o_spec],
                 out_shape=[out_shape], scratch_shapes=[pltpu.VMEM((tm, tn), F32)], args=(a, b),
                 sem=("parallel", "parallel", "arbitrary"), ride=ride)
    return outs[0] if ride is None else (outs[0], outs[1:])


def _row_spec(tm, d):
    return pl.BlockSpec((tm, d), lambda i: (i, 0))


def _vec_spec(d):
    return pl.BlockSpec((1, d), lambda i: (0, 0))


def _acc(ref, first, val):
    @pl.when(first)
    def _():
        ref[...] = val

    @pl.when(jnp.logical_not(first))
    def _():
        ref[...] += val


def _colsum(x):
    return jnp.sum(x, axis=0, keepdims=True)


def _rownorm(x):
    r = lax.rsqrt(jnp.mean(x * x, axis=-1, keepdims=True) + EPS)
    return x * r, r


def _pre_fwd(x, g, scale, shift, name, ride=None):
    l, d = x.shape
    tm = _tile(l, 256)

    def body(x_ref, g_ref, sc_ref, sh_ref, h_ref):
        n, _ = _rownorm(x_ref[...])
        h_ref[...] = _bf(n * g_ref[...] * (1.0 + sc_ref[...]) + sh_ref[...])

    outs = _call(body, name=name, grid=(l // tm,), in_specs=[_row_spec(tm, d), _vec_spec(d), _vec_spec(d), _vec_spec(d)],
                 out_specs=[_row_spec(tm, d)], out_shape=[jax.ShapeDtypeStruct((l, d), BF16)],
                 args=(x, g, scale, shift), sem=("parallel",), ride=ride)
    return outs[0], outs[1:]


def _post_pre_fwd(x, y, gate, pg, g1, scale1, shift1, name):
    l, d = x.shape
    tm = _tile(l, 256)

    def body(x_ref, y_ref, gate_ref, pg_ref, g1_ref, sc_ref, sh_ref, x1_ref, h1_ref):
        ny, _ = _rownorm(y_ref[...].astype(F32))
        x1 = x_ref[...] + gate_ref[...] * (ny * pg_ref[...])
        x1_ref[...] = x1
        n1, _ = _rownorm(x1)
        h1_ref[...] = _bf(n1 * g1_ref[...] * (1.0 + sc_ref[...]) + sh_ref[...])

    v = _vec_spec(d)
    return pl.pallas_call(
        body, name=name, grid=(l // tm,),
        in_specs=[_row_spec(tm, d), _row_spec(tm, d), v, v, v, v, v],
        out_specs=[_row_spec(tm, d), _row_spec(tm, d)],
        out_shape=[jax.ShapeDtypeStruct((l, d), F32), jax.ShapeDtypeStruct((l, d), BF16)],
        compiler_params=_params(("parallel",)),
    )(x, y, gate, pg, g1, scale1, shift1)


def _post_loss(x1, y1, gate, pg, target, name):
    l, d = x1.shape
    tm = _tile(l, 256)

    def body(x_ref, y_ref, gate_ref, pg_ref, t_ref, loss_ref, dy_ref, dx_ref, dgate_ref, dpg_ref):
        first = pl.program_id(0) == 0
        ny, ry = _rownorm(y_ref[...].astype(F32))
        q = ny * pg_ref[...]
        x2 = x_ref[...] + gate_ref[...] * q
        e = x2 - t_ref[...]
        _acc(loss_ref, first, jnp.full((1, 128), 0.5 / d, F32) * jnp.sum(e * e))
        dx2 = e * (1.0 / d)
        dx_ref[...] = dx2
        _acc(dgate_ref, first, _colsum(dx2 * q))
        dq = dx2 * gate_ref[...]
        _acc(dpg_ref, first, _colsum(dq * ny))
        dny = dq * pg_ref[...]
        dy = ry * (dny - ny * jnp.mean(dny * ny, axis=-1, keepdims=True))
        dy_ref[...] = _bf(dy)

    v = _vec_spec(d)
    return pl.pallas_call(
        body, name=name, grid=(l // tm,),
        in_specs=[_row_spec(tm, d), _row_spec(tm, d), v, v, _row_spec(tm, d)],
        out_specs=[_vec_spec(128), _row_spec(tm, d), _row_spec(tm, d), v, v],
        out_shape=[jax.ShapeDtypeStruct((1, 128), F32), jax.ShapeDtypeStruct((l, d), BF16),
                   jax.ShapeDtypeStruct((l, d), F32), jax.ShapeDtypeStruct((1, d), F32),
                   jax.ShapeDtypeStruct((1, d), F32)],
        compiler_params=_params(("arbitrary",)),
    )(x1, y1, gate, pg, target)


def _pre_bwd(dh, dres, x, g, scale, name, post=None):
    l, d = x.shape
    tm = _tile(l, 256)
    with_post = post is not None

    def body(*refs):
        if with_post:
            (dh_ref, dres_ref, x_ref, g_ref, sc_ref, y_ref, gate_ref, pg_ref,
             dx_ref, dsc_ref, dsh_ref, dg_ref, dy_ref, dgate_ref, dpg_ref) = refs
        else:
            dh_ref, dres_ref, x_ref, g_ref, sc_ref, dx_ref, dsc_ref, dsh_ref, dg_ref = refs
        first = pl.program_id(0) == 0
        dh = dh_ref[...].astype(F32)
        n, r = _rownorm(x_ref[...])
        _acc(dsc_ref, first, _colsum(dh * (n * g_ref[...])))
        _acc(dsh_ref, first, _colsum(dh))
        dyn = dh * (1.0 + sc_ref[...])
        _acc(dg_ref, first, _colsum(dyn * n))
        dn = dyn * g_ref[...]
        dx = dres_ref[...] + r * (dn - n * jnp.mean(dn * n, axis=-1, keepdims=True))
        dx_ref[...] = dx
        if with_post:
            ny, ry = _rownorm(y_ref[...].astype(F32))
            _acc(dgate_ref, first, _colsum(dx * (ny * pg_ref[...])))
            dq = dx * gate_ref[...]
            _acc(dpg_ref, first, _colsum(dq * ny))
            dny = dq * pg_ref[...]
            dy_ref[...] = _bf(ry * (dny - ny * jnp.mean(dny * ny, axis=-1, keepdims=True)))

    v = _vec_spec(d)
    row = _row_spec(tm, d)
    vec_out = jax.ShapeDtypeStruct((1, d), F32)
    in_specs = [row, row, row, v, v]
    args = [dh, dres, x, g, scale]
    out_specs = [row, v, v, v]
    out_shape = [jax.ShapeDtypeStruct((l, d), F32), vec_out, vec_out, vec_out]
    if with_post:
        in_specs += [row, v, v]
        args += list(post)
        out_specs += [row, v, v]
        out_shape += [jax.ShapeDtypeStruct((l, d), BF16), vec_out, vec_out]
    return pl.pallas_call(
        body, name=name, grid=(l // tm,), in_specs=in_specs, out_specs=out_specs, out_shape=out_shape,
        compiler_params=_params(("arbitrary",)),
    )(*args)


def _softplus_parts(z):
    e = jnp.exp(-jnp.abs(z))
    den = 1.0 + e
    lb = jnp.minimum(z, 0.0) - jnp.log(den)
    return lb, lb - z, jnp.exp(lb)


def _tri(cmp, n=HEAD):
    row = lax.broadcasted_iota(jnp.int32, (n, n), 0)
    col = lax.broadcasted_iota(jnp.int32, (n, n), 1)
    return cmp(row, col)


ATT_T = 256


def _attn_fwd(qkv, wb, name, hp=4, ride=None):
    l = qkv.shape[0]
    t = ATT_T
    nh, nq = wb // HEAD, l // t
    hp = min(hp, nh)
    ng, wg = nh // hp, hp * HEAD
    scale = 1.0 / math.sqrt(HEAD)

    def body(q_ref, k_ref, v_ref, o_ref, lk_ref):
        i = pl.program_id(1)
        valid = _tri(lambda r, c: c < r, t)
        m_gt = _bf(_tri(lambda r, c: r > c, t).astype(F32))

        def tile(j, carry, diag):
            rows = pl.ds(pl.multiple_of(j * t, t), t)
            cols = [slice(hh * HEAD, (hh + 1) * HEAD) for hh in range(hp)]
            zs = [_dot(q_ref[:, cs], k_ref[rows, cs], NT) * scale for cs in cols]
            lbs, lks = [], []
            for z in zs:
                lb, lk, _ = _softplus_parts(z)
                lbs.append(lb)
                lks.append(jnp.where(valid, lk, 0.0) if diag else lk)
            laters = [_dot(_bf(lk), m_gt) for lk in lks]
            ws = [jnp.exp(lb + later + run) for lb, later, (_, run) in zip(lbs, laters, carry)]
            if diag:
                ws = [jnp.where(valid, w, 0.0) for w in ws]
            return tuple((acc + _dot(_bf(w), v_ref[rows, cs]), run + jnp.sum(lk, axis=1, keepdims=True))
                         for w, lk, cs, (acc, run) in zip(ws, lks, cols, carry))

        zero = (jnp.zeros((t, HEAD), F32), jnp.zeros((t, 1), F32))
        carry = tile(i, (zero,) * hp, True)
        carry = lax.fori_loop(0, i, lambda s, c: tile(i - 1 - s, c, False), carry)
        for hh, (acc, run) in enumerate(carry):
            cs = slice(hh * HEAD, (hh + 1) * HEAD)
            o_ref[:, cs] = acc
            lk_ref[:, cs] = jnp.broadcast_to(run, (t, HEAD))

    blk = lambda off: pl.BlockSpec((t, wg), lambda h, i: (i, off + h))
    full = lambda off: pl.BlockSpec((l, wg), lambda h, i: (0, off + h))
    out = pl.BlockSpec((t, wg), lambda h, i: (i, h))
    outs = _call(body, name=name, grid=(ng, nq), in_specs=[blk(0), full(ng), full(2 * ng)], out_specs=[out, out],
                 out_shape=[jax.ShapeDtypeStruct((l, wb), F32), jax.ShapeDtypeStruct((l, wb), F32)],
                 args=(qkv, qkv, qkv), sem=("parallel", "arbitrary"), ride=ride)
    return outs[0], outs[1], outs[2:]


def _attn_bwd(qkv, proj, dcat, lktot, wa, wb, name, hp=2, ride=None):
    l = qkv.shape[0]
    t = ATT_T
    nh, nq = wb // HEAD, l // t
    hp = min(hp, nh)
    ng, wg = nh // hp, hp * HEAD
    scale = 1.0 / math.sqrt(HEAD)

    def body(q_ref, k_ref, v_ref, bz_ref, dc_ref, lt_ref, dq_ref, dkt_out, dvt_out, do_s, qt_s, dot_s,
             dkt_ref, dvt_ref, out_sems):
        i = pl.program_id(1)

        @pl.when(i == 0)
        def _():
            dkt_ref[...] = jnp.zeros_like(dkt_ref)
            dvt_ref[...] = jnp.zeros_like(dvt_ref)

        do = dc_ref[...].astype(F32) * _silu(bz_ref[...])
        do_s[...] = _bf(do)
        for hh in range(hp):
            cs = slice(hh * HEAD, (hh + 1) * HEAD)
            qt_s[hh] = _bf(q_ref[:, cs].astype(F32).T * scale)
            dot_s[hh] = _bf(do[:, cs].T)
        valid = _tri(lambda r, c: c < r, t)
        m_le = _bf(_tri(lambda r, c: r <= c, t).astype(F32))
        m_lt = _bf(_tri(lambda r, c: r < c, t).astype(F32))

        def tile(j, carry, diag):
            rows = pl.ds(pl.multiple_of(j * t, t), t)
            heads = range(hp)
            cols = [slice(hh * HEAD, (hh + 1) * HEAD) for hh in heads]
            zs = [_dot(q_ref[:, cs], k_ref[rows, cs], NT) * scale for cs in cols]
            dws = [_dot(do_s[:, cs], v_ref[rows, cs], NT) for cs in cols]
            lbs, lks, sigs = [], [], []
            for z in zs:
                lb, lk, sig = _softplus_parts(z)
                lbs.append(lb)
                lks.append(jnp.where(valid, lk, 0.0) if diag else lk)
                sigs.append(sig)
            pins = [_dot(_bf(lk), m_le) for lk in lks]
            ws = [jnp.exp(lbs[hh] + (lt_ref[:, hh * HEAD:hh * HEAD + 1] - carry[hh][1]) - pins[hh]) for hh in heads]
            if diag:
                ws = [jnp.where(valid, w, 0.0) for w in ws]
            das = [dw * w for dw, w in zip(dws, ws)]
            pexs = [_dot(_bf(da), m_lt) for da in das]
            dzs = [das[hh] - sigs[hh] * (das[hh] + carry[hh][2] + pexs[hh]) for hh in heads]
            if diag:
                dzs = [jnp.where(valid, dz, 0.0) for dz in dzs]
            dzs = [_bf(dz) for dz in dzs]
            out = []
            for hh in heads:
                dkt, dvt = _dot(qt_s[hh], dzs[hh]), _dot(dot_s[hh], _bf(ws[hh]))
                for half in range(t // HEAD):
                    dkt_ref[hh, sub * j + half] += dkt[:, half * HEAD:(half + 1) * HEAD]
                    dvt_ref[hh, sub * j + half] += dvt[:, half * HEAD:(half + 1) * HEAD]
                dq, cpre, ppre = carry[hh]
                out.append((dq + _dot(dzs[hh], k_ref[rows, cols[hh]]), cpre + jnp.sum(lks[hh], axis=1, keepdims=True),
                            ppre + pexs[hh][:, t - 1:] + das[hh][:, t - 1:]))
            return tuple(out)

        zero = (jnp.zeros((t, HEAD), F32), jnp.zeros((t, 1), F32), jnp.zeros((t, 1), F32))
        carry = lax.fori_loop(0, i, lambda j, c: tile(j, c, False), (zero,) * hp)
        carry = tile(i, carry, True)
        for hh in range(hp):
            dq_ref[:, hh * HEAD:(hh + 1) * HEAD] = carry[hh][0] * scale

        @pl.when(i == nq - 1)
        def _():
            heads = pl.ds(pl.program_id(0) * hp, hp)
            copies = [pltpu.make_async_copy(dkt_ref, dkt_out.at[heads], out_sems.at[0]),
                      pltpu.make_async_copy(dvt_ref, dvt_out.at[heads], out_sems.at[1])]
            for cp in copies:
                cp.start()
            for cp in copies:
                cp.wait()

    sub = t // HEAD
    blk = lambda off: pl.BlockSpec((t, wg), lambda h, i: (i, off + h))
    full = lambda off: pl.BlockSpec((l, wg), lambda h, i: (0, off + h))
    acc_shape = jax.ShapeDtypeStruct((nh, l // HEAD, HEAD, HEAD), F32)
    acc_scratch = pltpu.VMEM((hp, l // HEAD, HEAD, HEAD), F32)
    outs = _call(
        body, name=name, grid=(ng, nq),
        in_specs=[blk(0), full(ng), full(2 * ng), blk(3 * wa // wg), blk(wa // wg), blk(0)],
        out_specs=[blk(0), ANY, ANY], out_shape=[jax.ShapeDtypeStruct((l, wb), F32), acc_shape, acc_shape],
        scratch_shapes=[pltpu.VMEM((t, wg), BF16), pltpu.VMEM((hp, HEAD, t), BF16), pltpu.VMEM((hp, HEAD, t), BF16),
                        acc_scratch, acc_scratch, pltpu.SemaphoreType.DMA((2,))],
        args=(qkv, qkv, qkv, proj, dcat, lktot), sem=("parallel", "arbitrary"), ride=ride)
    return outs[0], outs[1], outs[2], outs[3:]


def _sgu_heads(v, g_ref, w_ref, bt_ref, nh):
    keep = _tri(lambda r, c: r >= c)
    out = []
    for h in range(nh):
        cols = slice(h * HEAD, (h + 1) * HEAD)
        nv, r = _rownorm(v[:, cols])
        wm = jnp.where(keep, w_ref[h], 0.0)
        s = _dot(_bf(wm), _bf(nv * g_ref[:, cols])) + bt_ref[:, h:h + 1]
        out.append((nv, r, wm, s))
    return out


def _sgu_fwd(proj, out_b, norm_g, sgu_w, sgu_bt, wa, wb, name):
    l, n = proj.shape
    nh = wa // HEAD

    def body(au_ref, av_ref, az_ref, bz_ref, ob_ref, g_ref, w_ref, bt_ref, cat_ref):
        u, v, sz = _gelu(au_ref[...]), _gelu(av_ref[...]), _silu(az_ref[...])
        for h, (_, _, _, s) in enumerate(_sgu_heads(v, g_ref, w_ref, bt_ref, nh)):
            cols = slice(h * HEAD, (h + 1) * HEAD)
            cat_ref[:, cols] = _bf(u[:, cols] * s * sz[:, cols])
        cat_ref[:, wa:] = _bf(ob_ref[...] * _silu(bz_ref[...]))

    a_blk = lambda j: pl.BlockSpec((HEAD, wa), lambda i: (i, j))
    return pl.pallas_call(
        body, name=name, grid=(l // HEAD,),
        in_specs=[a_blk(0), a_blk(1), a_blk(2), a_blk(3), pl.BlockSpec((HEAD, wb), lambda i: (i, 0)),
                  _vec_spec(wa), pl.BlockSpec((nh, HEAD, HEAD), lambda i: (0, 0, 0)),
                  pl.BlockSpec((HEAD, nh), lambda i: (0, 0))],
        out_specs=pl.BlockSpec((HEAD, wa + wb), lambda i: (i, 0)),
        out_shape=jax.ShapeDtypeStruct((l, wa + wb), BF16),
        compiler_params=_params(("parallel",)),
    )(proj, proj, proj, proj, out_b, norm_g, sgu_w, sgu_bt)


def _sgu_bwd(proj, out_b, dcat, dq, dk, dv, norm_g, sgu_w, sgu_bt, wa, wb, name):
    l = proj.shape[0]
    n = 3 * wa + 4 * wb
    nh = wa // HEAD

    def body(au_ref, av_ref, az_ref, bz_ref, ob_ref, dc_ref, dq_ref, dk_ref, dv_ref, g_ref, w_ref, wt_ref, bt_ref,
             dp_ref, dw_ref, dbt_ref, dg_ref):
        first = pl.program_id(0) == 0
        keep = _tri(lambda r, c: r >= c)
        au, av, az = au_ref[...], av_ref[...], az_ref[...]
        u, v, sz = _gelu(au), _gelu(av), _silu(az)
        dgelu_u, dgelu_v, dsilu_z = _gelu_grad(au), _gelu_grad(av), _silu_grad(az)
        heads = _sgu_heads(v, g_ref, w_ref, bt_ref, nh)
        cols = [slice(h * HEAD, (h + 1) * HEAD) for h in range(nh)]
        dss = []
        for h, (nv, r, wm, s) in enumerate(heads):
            dca, uh, szh = dc_ref[:, cols[h]].astype(F32), u[:, cols[h]], sz[:, cols[h]]
            dp_ref[:, cols[h]] = _bf(dca * s * szh * dgelu_u[:, cols[h]])
            dp_ref[:, 2 * wa + h * HEAD:2 * wa + (h + 1) * HEAD] = _bf(dca * uh * s * dsilu_z[:, cols[h]])
            dss.append(dca * uh * szh)
        dws = [_dot(_bf(ds), _bf(nv * g_ref[:, cs]), NT) for ds, cs, (nv, _, _, _) in zip(dss, cols, heads)]
        keep_t = _tri(lambda r, c: r <= c)
        dvhs = [_dot(_bf(jnp.where(keep_t, wt_ref[h], 0.0)), _bf(dss[h])) for h in range(nh)]
        dg_parts = []
        for h, (nv, r, wm, s) in enumerate(heads):
            _acc(dw_ref.at[h], first, jnp.where(keep, dws[h], 0.0))
            _acc(dbt_ref.at[:, h:h + 1], first, jnp.sum(dss[h], axis=1, keepdims=True))
            dg_parts.append(_colsum(dvhs[h] * nv))
            dnv = dvhs[h] * g_ref[:, cols[h]]
            dvv = r * (dnv - nv * jnp.mean(dnv * nv, axis=-1, keepdims=True))
            dp_ref[:, wa + h * HEAD:wa + (h + 1) * HEAD] = _bf(dvv * dgelu_v[:, cols[h]])
        _acc(dg_ref, first, jnp.concatenate(dg_parts, axis=1))
        base = 3 * wa
        dp_ref[:, base:base + wb] = _bf(dq_ref[...])
        for h in range(wb // HEAD):
            dp_ref[:, base + wb + h * HEAD:base + wb + (h + 1) * HEAD] = _bf(dk_ref[h, 0].T)
            dp_ref[:, base + 2 * wb + h * HEAD:base + 2 * wb + (h + 1) * HEAD] = _bf(dv_ref[h, 0].T)
        dp_ref[:, base + 3 * wb:] = _bf(dc_ref[:, wa:].astype(F32) * ob_ref[...] * _silu_grad(bz_ref[...]))

    a_blk = lambda j: pl.BlockSpec((HEAD, wa), lambda i: (i, j))
    b_blk = pl.BlockSpec((HEAD, wb), lambda i: (i, 0))
    t_blk = pl.BlockSpec((wb // HEAD, 1, HEAD, HEAD), lambda i: (0, i, 0, 0))
    w_spec = pl.BlockSpec((nh, HEAD, HEAD), lambda i: (0, 0, 0))
    bt_spec = pl.BlockSpec((HEAD, nh), lambda i: (0, 0))
    return pl.pallas_call(
        body, name=name, grid=(l // HEAD,),
        in_specs=[a_blk(0), a_blk(1), a_blk(2), a_blk(3), b_blk, pl.BlockSpec((HEAD, wa + wb), lambda i: (i, 0)),
                  b_blk, t_blk, t_blk, _vec_spec(wa), w_spec, w_spec, bt_spec],
        out_specs=[pl.BlockSpec((HEAD, n), lambda i: (i, 0)), w_spec, bt_spec, _vec_spec(wa)],
        out_shape=[jax.ShapeDtypeStruct((l, n), BF16), jax.ShapeDtypeStruct((nh, HEAD, HEAD), F32),
                   jax.ShapeDtypeStruct((HEAD, nh), F32), jax.ShapeDtypeStruct((1, wa), F32)],
        compiler_params=_params(("arbitrary",)),
    )(proj, proj, proj, proj, out_b, dcat, dq, dk, dv, norm_g, sgu_w, sgu_w.transpose(0, 2, 1), sgu_bt)


def _ssm_discretise(lr, li, ldt, br, bi):
    dt = jnp.exp(ldt)
    mag = jnp.exp(lr * dt)
    a_re = mag * jnp.cos(li * dt)
    a_im = mag * jnp.sin(li * dt)
    den = lr * lr + li * li
    nr = a_re - 1.0
    coef_re = (nr * lr + a_im * li) / den
    coef_im = (a_im * lr - nr * li) / den
    return a_re, a_im, coef_re * br - coef_im * bi, coef_re * bi + coef_im * br


def _ssm_prep(lr, li, ldt, br, bi, lr_row, li_row, ldt_row, name):
    s, c = br.shape

    def body(lr_ref, li_ref, ldt_ref, br_ref, bi_ref, lrr_ref, lir_ref, ldtr_ref, bbr_ref, bbi_ref, tr_ref, ti_ref):
        _, _, bbr, bbi = _ssm_discretise(lr_ref[...], li_ref[...], ldt_ref[...], br_ref[...], bi_ref[...])
        bbr_ref[...] = bbr
        bbi_ref[...] = bbi
        row = lax.broadcasted_iota(jnp.int32, (SCAN_ROWS, 1), 0)
        blk, r = jnp.right_shift(row, 3), jnp.bitwise_and(row, 7)
        kind, rev = jnp.bitwise_and(blk, 3), blk >= 4
        step = jnp.left_shift(1, kind)
        n = jnp.where(kind < 3, step, jnp.where(rev, 8 - r, r + 1)).astype(F32)
        keep = (kind == 3) | (rev & (r < 8 - step)) | (jnp.logical_not(rev) & (r >= step))
        dt = jnp.exp(ldtr_ref[...])
        mag = jnp.exp(n * (lrr_ref[...] * dt))
        ang = n * (lir_ref[...] * dt)
        tr_ref[...] = jnp.where(keep, mag * jnp.cos(ang), 0.0)
        ti_ref[...] = jnp.where(keep, jnp.where(rev, -1.0, 1.0) * mag * jnp.sin(ang), 0.0)

    col = jax.ShapeDtypeStruct((s, c), F32)
    row = jax.ShapeDtypeStruct((SCAN_ROWS, s), F32)
    return pl.pallas_call(body, name=name, out_shape=[col, col, row, row])(
        lr, li, ldt, br, bi, lr_row, li_row, ldt_row)


def _ssm_prep_bwd(lr, li, ldt, br, bi, da_re, da_im, dbb_re, dbb_im, p, name):
    s, c = br.shape

    def body(lr_ref, li_ref, ldt_ref, br_ref, bi_ref, dar_ref, dai_ref, dbr_ref, dbi_ref,
             dlr_ref, dli_ref, dldt_ref, dbre_ref, dbim_ref):
        args = (lr_ref[...], li_ref[...], ldt_ref[...], br_ref[...], bi_ref[...])
        _, vjp = jax.vjp(_ssm_discretise, *args)
        dlr, dli, dldt, dbr, dbi = vjp((dar_ref[...], dai_ref[...], dbr_ref[...], dbi_ref[...]))
        dlr_ref[...] = dlr
        dli_ref[...] = dli
        dbre_ref[...] = dbr
        dbim_ref[...] = dbi
        idx = lax.broadcasted_iota(jnp.int32, (s, s // p), 0)
        grp = lax.broadcasted_iota(jnp.int32, (s, s // p), 1)
        own = (idx >= grp * p) & (idx < (grp + 1) * p)
        dldt_ref[...] = _colsum(jnp.where(own, dldt, 0.0))

    col1 = jax.ShapeDtypeStruct((s, 1), F32)
    colc = jax.ShapeDtypeStruct((s, c), F32)
    return pl.pallas_call(
        body, name=name, out_shape=[col1, col1, jax.ShapeDtypeStruct((1, s // p), F32), colc, colc],
    )(lr, li, ldt, br, bi, da_re, da_im, dbb_re, dbb_im)


SCAN_ROWS = 64


def _scan_groups(xr, xi, tr_ref, ti_ref, cr, ci, reverse):
    ng = xr.shape[0] // 8
    base = SCAN_ROWS // 2 if reverse else 0
    pr, pi = tr_ref[base + 24:base + 32, :], ti_ref[base + 24:base + 32, :]
    edge = slice(0, 1) if reverse else slice(7, 8)
    out_r, out_i = [None] * ng, [None] * ng
    for g in (range(ng - 1, -1, -1) if reverse else range(ng)):
        sr, si = xr[8 * g:8 * g + 8, :], xi[8 * g:8 * g + 8, :]
        for k in range(3):
            ar, ai = tr_ref[base + 8 * k:base + 8 * k + 8, :], ti_ref[base + 8 * k:base + 8 * k + 8, :]
            shift = 8 - (1 << k) if reverse else 1 << k
            rr, ri = pltpu.roll(sr, shift, 0), pltpu.roll(si, shift, 0)
            sr, si = sr + ar * rr - ai * ri, si + ar * ri + ai * rr
        sr, si = sr + pr * cr - pi * ci, si + pr * ci + pi * cr
        cr, ci = sr[edge, :], si[edge, :]
        out_r[g], out_i[g] = sr, si
    return jnp.concatenate(out_r, axis=0), jnp.concatenate(out_i, axis=0), cr, ci


def _ssm_fwd(proj, bbd, ccd, pw_re, pw_im, d_skip, w, name):
    l = proj.shape[0]
    nb, cw, ns2 = bbd.shape
    ns = ns2 // 2
    nc = l // SSM_T

    def body(u_ref, bbd_ref, ccd_ref, pr_ref, pi_ref, d_ref, y_ref, hsr_ref, hsi_ref, h_ref, hr_s, hi_s):
        @pl.when(pl.program_id(1) == 0)
        def _():
            hr_s[...] = jnp.zeros_like(hr_s)
            hi_s[...] = jnp.zeros_like(hi_s)

        hsr_ref[...] = hr_s[...].reshape(hsr_ref.shape)
        hsi_ref[...] = hi_s[...].reshape(hsi_ref.shape)
        u = u_ref[...]
        bu = _dot(_bf(u), bbd_ref[0])
        hr, hi, cr, ci = _scan_groups(bu[:, :ns], bu[:, ns:], pr_ref, pi_ref, hr_s[...], hi_s[...], False)
        hr_s[...] = cr
        hi_s[...] = ci
        h_bf = _bf(jnp.concatenate([hr, hi], axis=1))
        h_ref[...] = h_bf
        y_ref[...] = _dot(h_bf, ccd_ref[0]) + d_ref[...] * u

    tab = pl.BlockSpec((SCAN_ROWS, ns), lambda b, k: (0, b))
    return pl.pallas_call(
        body, name=name, grid=(nb, nc),
        in_specs=[pl.BlockSpec((SSM_T, cw), lambda b, k: (k, b)),
                  pl.BlockSpec((1, cw, ns2), lambda b, k: (b, 0, 0)),
                  pl.BlockSpec((1, ns2, cw), lambda b, k: (b, 0, 0)),
                  tab, tab, pl.BlockSpec((1, cw), lambda b, k: (0, b))],
        out_specs=[pl.BlockSpec((SSM_T, cw), lambda b, k: (k, b)),
                   pl.BlockSpec((1, 1, ns), lambda b, k: (k, 0, b)), pl.BlockSpec((1, 1, ns), lambda b, k: (k, 0, b)),
                   pl.BlockSpec((SSM_T, ns2), lambda b, k: (k, b))],
        out_shape=[jax.ShapeDtypeStruct((l, w), F32), jax.ShapeDtypeStruct((nc, 1, nb * ns), F32),
                   jax.ShapeDtypeStruct((nc, 1, nb * ns), F32), jax.ShapeDtypeStruct((l, nb * ns2), BF16)],
        scratch_shapes=[pltpu.VMEM((1, ns), F32), pltpu.VMEM((1, ns), F32)],
        compiler_params=_params(("parallel", "arbitrary")),
    )(proj, bbd, ccd, pw_re, pw_im, d_skip)


def _ssm_bwd(proj, dy, hs_re, hs_im, h_all, bbd, ccd, pw_re, pw_im, d_skip, w, name, ride=None):
    l = proj.shape[0]
    nb, cw, ns2 = bbd.shape
    ns = ns2 // 2
    nc = l // SSM_T

    def body(u_ref, dy_ref, hsr_ref, hsi_ref, h_ref, bbd_ref, ccd_ref, pr_ref, pi_ref, d_ref,
             du_ref, dbbd_ref, dccd_ref, dar_ref, dai_ref, dd_ref, gr_s, gi_s):
        first = pl.program_id(1) == 0

        @pl.when(first)
        def _():
            gr_s[...] = jnp.zeros_like(gr_s)
            gi_s[...] = jnp.zeros_like(gi_s)

        u, dy = u_ref[...], dy_ref[...]
        dy_bf = _bf(dy)
        hr0, hi0 = hsr_ref[0], hsi_ref[0]
        h = h_ref[...].astype(F32)
        hr, hi = h[:, :ns], h[:, ns:]
        dh = _dot(dy_bf, ccd_ref[0], NT)
        gr, gi, gcr, gci = _scan_groups(dh[:, :ns], dh[:, ns:], pr_ref, pi_ref, gr_s[...], gi_s[...], True)
        gr_s[...] = gcr
        gi_s[...] = gci
        row0 = lax.broadcasted_iota(jnp.int32, hr.shape, 0) == 0
        pr_h = jnp.where(row0, hr0, pltpu.roll(hr, 1, 0))
        pi_h = jnp.where(row0, hi0, pltpu.roll(hi, 1, 0))
        _acc(dar_ref, first, _colsum(pr_h * gr + pi_h * gi))
        _acc(dai_ref, first, _colsum(pr_h * gi - pi_h * gr))
        g_bf = _bf(jnp.concatenate([gr, gi], axis=1))
        _acc(dbbd_ref.at[0], first, _dot(_bf(u.T), g_bf))
        _acc(dccd_ref.at[0], first, _dot(_bf(h.T), dy_bf))
        du_ref[...] = _bf(_dot(g_bf, bbd_ref[0], NT) + d_ref[...] * dy)
        _acc(dd_ref, first, _colsum(dy * u))

    rev = lambda b, k: (nc - 1 - k, b)
    outs = _call(
        body, name=name, grid=(nb, nc), ride=ride, sem=("parallel", "arbitrary"),
        args=(proj, dy, hs_re, hs_im, h_all, bbd, ccd, pw_re, pw_im, d_skip),
        in_specs=[pl.BlockSpec((SSM_T, cw), rev), pl.BlockSpec((SSM_T, cw), rev),
                  pl.BlockSpec((1, 1, ns), lambda b, k: (nc - 1 - k, 0, b)),
                  pl.BlockSpec((1, 1, ns), lambda b, k: (nc - 1 - k, 0, b)),
                  pl.BlockSpec((SSM_T, ns2), rev),
                  pl.BlockSpec((1, cw, ns2), lambda b, k: (b, 0, 0)),
                  pl.BlockSpec((1, ns2, cw), lambda b, k: (b, 0, 0)),
                  pl.BlockSpec((SCAN_ROWS, ns), lambda b, k: (0, b)), pl.BlockSpec((SCAN_ROWS, ns), lambda b, k: (0, b)),
                  pl.BlockSpec((1, cw), lambda b, k: (0, b))],
        out_specs=[pl.BlockSpec((SSM_T, cw), rev),
                   pl.BlockSpec((1, cw, ns2), lambda b, k: (b, 0, 0)),
                   pl.BlockSpec((1, ns2, cw), lambda b, k: (b, 0, 0)),
                   pl.BlockSpec((1, ns), lambda b, k: (0, b)), pl.BlockSpec((1, ns), lambda b, k: (0, b)),
                   pl.BlockSpec((1, cw), lambda b, k: (0, b))],
        out_shape=[jax.ShapeDtypeStruct((l, w), BF16), jax.ShapeDtypeStruct(bbd.shape, F32),
                   jax.ShapeDtypeStruct(ccd.shape, F32), jax.ShapeDtypeStruct((1, nb * ns), F32),
                   jax.ShapeDtypeStruct((1, nb * ns), F32), jax.ShapeDtypeStruct((1, w), F32)],
        scratch_shapes=[pltpu.VMEM((1, ns), F32), pltpu.VMEM((1, ns), F32)])
    return (*outs[:6], outs[6:])


def _block_diag_b(bb_re, bb_im, g, p, c):
    nb = g // SSM_GB
    keep = _same_group(SSM_GB * c, c, SSM_GB * p, p)

    def one(bb):
        t = bb.reshape(nb, SSM_GB, p, c).transpose(0, 1, 3, 2).reshape(nb, SSM_GB * c, p)
        return jnp.where(keep, jnp.tile(t, (1, 1, SSM_GB)), 0.0)

    return jnp.concatenate([one(bb_re), one(bb_im)], axis=2)


def _same_group(rows, per_row, cols, per_col):
    r = lax.broadcasted_iota(jnp.int32, (rows, cols), 0) // per_row
    q = lax.broadcasted_iota(jnp.int32, (rows, cols), 1) // per_col
    return r == q


def _block_diag_c(c_re, c_im, g, p, c):
    nb = g // SSM_GB
    keep = _same_group(SSM_GB * p, p, SSM_GB * c, c)

    def one(cc):
        t = cc.reshape(nb, SSM_GB, c, p).transpose(0, 1, 3, 2).reshape(nb, SSM_GB * p, c)
        return jnp.where(keep, jnp.tile(t, (1, 1, SSM_GB)), 0.0)

    return jnp.concatenate([one(c_re), one(-c_im)], axis=1)


def _diag_of_b(dbbd, g, p, c):
    nb = g // SSM_GB
    keep = _same_group(SSM_GB * c, c, SSM_GB * p, p)

    def one(blk):
        d = jnp.where(keep, blk, 0.0).reshape(nb, SSM_GB * c, SSM_GB, p).sum(axis=2)
        return d.reshape(nb, SSM_GB, c, p).transpose(0, 1, 3, 2).reshape(g * p, c)

    half = SSM_GB * p
    return one(dbbd[:, :, :half]), one(dbbd[:, :, half:])


def _diag_of_c(dccd, g, p, c):
    nb = g // SSM_GB
    keep = _same_group(SSM_GB * p, p, SSM_GB * c, c)

    def one(blk):
        d = jnp.where(keep, blk, 0.0).reshape(nb, SSM_GB * p, SSM_GB, c).sum(axis=2)
        return d.reshape(nb, SSM_GB, p, c).transpose(0, 1, 3, 2).reshape(g, c, p)

    half = SSM_GB * p
    return one(dccd[:, :half]), -one(dccd[:, half:])


def _glu_fwd(y, proj, w_glu, b_glu, name):
    l, w = y.shape
    tm = _tile(l, 256)

    def body(y_ref, z_ref, w_ref, b_ref, o_ref):
        g = _gelu(y_ref[...])
        t = _dot(_bf(g), w_ref[...]) + b_ref[...]
        o_ref[...] = _bf(g * _sigmoid(t) * _silu(z_ref[...]))

    return pl.pallas_call(
        body, name=name, grid=(l // tm,),
        in_specs=[_row_spec(tm, w), pl.BlockSpec((tm, w), lambda i: (i, 1)),
                  pl.BlockSpec((w, w), lambda i: (0, 0)), _vec_spec(w)],
        out_specs=_row_spec(tm, w), out_shape=jax.ShapeDtypeStruct((l, w), BF16),
        compiler_params=_params(("parallel",)),
    )(y, proj, w_glu, b_glu)


def _glu_bwd(do, y, proj, w_glu, b_glu, name):
    l, w = y.shape
    tm = _tile(l, 512)
    nsteps = l // tm

    def body(do_ref, y_ref, z_ref, w_ref, b_ref, dy_ref, dz_ref, dw_ref, db_ref, dw_acc):
        i = pl.program_id(0)
        first = i == 0
        yv, z, do = y_ref[...], z_ref[...], do_ref[...].astype(F32)
        g = _gelu(yv)
        g_bf = _bf(g)
        sg = _sigmoid(_dot(g_bf, w_ref[...]) + b_ref[...])
        dyy = do * _silu(z)
        dz_ref[...] = _bf(do * g * sg * _silu_grad(z))
        dt = dyy * g * sg * (1.0 - sg)
        dt_bf = _bf(dt)
        dg = dyy * sg + _dot(dt_bf, w_ref[...], NT)
        dy_ref[...] = dg * _gelu_grad(yv)
        _acc(dw_acc, first, _dot(_bf(g.T), dt_bf))
        _acc(db_ref, first, _colsum(dt))

        @pl.when(i == nsteps - 1)
        def _():
            dw_ref[...] = _bf(dw_acc[...])

    return pl.pallas_call(
        body, name=name, grid=(nsteps,),
        in_specs=[_row_spec(tm, w), _row_spec(tm, w), pl.BlockSpec((tm, w), lambda i: (i, 1)),
                  pl.BlockSpec((w, w), lambda i: (0, 0)), _vec_spec(w)],
        out_specs=[_row_spec(tm, w), _row_spec(tm, w), pl.BlockSpec((w, w), lambda i: (0, 0)), _vec_spec(w)],
        out_shape=[jax.ShapeDtypeStruct((l, w), F32), jax.ShapeDtypeStruct((l, w), BF16),
                   jax.ShapeDtypeStruct((w, w), BF16), jax.ShapeDtypeStruct((1, w), F32)],
        scratch_shapes=[pltpu.VMEM((w, w), F32)],
        compiler_params=_params(("arbitrary",)),
    )(do, y, proj, w_glu, b_glu)


MOD_ROWS = 128


def _mod_fwd(cond_pad, w_mod, b_shard, name):
    nl, d, ncol = w_mod.shape
    tn = _tile(ncol, 512)

    def body(c_ref, w_ref, b_ref, o_ref):
        o_ref[0] = _dot(_bf(c_ref[...]), _bf(w_ref[0])) + b_ref[0]

    return pl.pallas_call(
        body, name=name, grid=(nl, ncol // tn),
        in_specs=[pl.BlockSpec((MOD_ROWS, d), lambda a, j: (0, 0)),
                  pl.BlockSpec((1, d, tn), lambda a, j: (a, 0, j)),
                  pl.BlockSpec((1, 1, tn), lambda a, j: (a, 0, j))],
        out_specs=pl.BlockSpec((1, MOD_ROWS, tn), lambda a, j: (a, 0, j)),
        out_shape=jax.ShapeDtypeStruct((nl, MOD_ROWS, ncol), F32),
        compiler_params=_params(("parallel", "parallel")),
    )(cond_pad, w_mod, b_shard)


def _mod_bwd(cond_pad_t, dmod_pad, name):
    nl, _, ncol = dmod_pad.shape
    d = cond_pad_t.shape[0]
    tn = _tile(ncol, 512)

    def body(c_ref, dm_ref, o_ref):
        o_ref[0] = _dot(_bf(c_ref[...]), _bf(dm_ref[0]))

    return pl.pallas_call(
        body, name=name, grid=(nl, ncol // tn),
        in_specs=[pl.BlockSpec((d, MOD_ROWS), lambda a, j: (0, 0)),
                  pl.BlockSpec((1, MOD_ROWS, tn), lambda a, j: (a, 0, j))],
        out_specs=pl.BlockSpec((1, d, tn), lambda a, j: (a, 0, j)),
        out_shape=jax.ShapeDtypeStruct((nl, d, ncol), F32),
        compiler_params=_params(("parallel", "parallel")),
    )(cond_pad_t, dmod_pad)


def _silu_rows(c2d, name):
    def body(c_ref, o_ref):
        o_ref[...] = _silu(c_ref[...])

    return pl.pallas_call(body, name=name, out_shape=jax.ShapeDtypeStruct(c2d.shape, F32))(c2d)


def _sum_leading(x, name):
    n, r, c = x.shape
    tr = _tile(r, max(16, (1 << 20) // (4 * c)), 16 if r % 16 == 0 else 8)

    def body(x_ref, o_ref):
        acc = x_ref[0].astype(F32)
        for k in range(1, n):
            acc = acc + x_ref[k].astype(F32)
        o_ref[...] = acc

    return pl.pallas_call(
        body, name=name, grid=(r // tr,),
        in_specs=[pl.BlockSpec((n, tr, c), lambda i: (0, i, 0))], out_specs=pl.BlockSpec((tr, c), lambda i: (i, 0)),
        out_shape=jax.ShapeDtypeStruct((r, c), F32), compiler_params=_params(("parallel",)),
    )(x)


def _adamw(w, gs, m, v, name):
    r, c = w.shape
    tr = _tile(r, max(8, (3 << 19) // (4 * c)), 8)
    ng = len(gs)

    def body(*refs):
        w_ref, g_refs, m_ref, v_ref = refs[0], refs[1:1 + ng], refs[1 + ng], refs[2 + ng]
        g_ref, d_ref, nm_ref, nv_ref = refs[3 + ng:]
        g = g_refs[0][...]
        for extra in g_refs[1:]:
            g = g + extra[...]
        g_ref[...] = g
        d_ref[...], nm_ref[...], nv_ref[...] = _adamw_math(w_ref[...], g, m_ref[...], v_ref[...])

    spec = pl.BlockSpec((tr, c), lambda i: (i, 0))
    shp = jax.ShapeDtypeStruct((r, c), F32)
    return pl.pallas_call(
        body, name=name, grid=(r // tr,), in_specs=[spec] * (3 + ng), out_specs=[spec] * 4,
        out_shape=[shp] * 4, compiler_params=_params(("parallel",)),
    )(w, *gs, m, v)


def _adamw_math(w, g, m, v):
    nm = ADAM_B1 * m + (1.0 - ADAM_B1) * g
    nv = ADAM_B2 * v + (1.0 - ADAM_B2) * (g * g)
    m_hat = nm / (1.0 - ADAM_B1 ** ADAM_STEP)
    v_hat = nv / (1.0 - ADAM_B2 ** ADAM_STEP)
    return -ADAM_LR * (m_hat / (jnp.sqrt(v_hat) + ADAM_EPS) + ADAM_WD * w), nm, nv


def _adamw_many(ws, gs, ms, vs, name):
    n = len(ws)

    def body(*refs):
        w_refs, g_refs, m_refs, v_refs = (refs[k * n:(k + 1) * n] for k in range(4))
        outs = refs[4 * n:]
        for i in range(n):
            outs[3 * i][...], outs[3 * i + 1][...], outs[3 * i + 2][...] = _adamw_math(
                w_refs[i][...], g_refs[i][...], m_refs[i][...], v_refs[i][...])

    out_shape = [jax.ShapeDtypeStruct(w.shape, F32) for w in ws for _ in range(3)]
    outs = pl.pallas_call(body, name=name, out_shape=out_shape, compiler_params=_params())(*ws, *gs, *ms, *vs)
    return [tuple(outs[3 * i:3 * i + 3]) for i in range(n)]


ANY = pl.BlockSpec(memory_space=pl.ANY)


def _flip(v, bit):
    return 1 - v if bit else v


def _allgather8_ops(x_ref, o_ref, send_sems, recv_sems, local_sem):
    mx, my, mc = lax.axis_index("x"), lax.axis_index("y"), lax.axis_index("c")
    me = 4 * mx + 2 * my + mc

    def mine():
        return pltpu.make_async_copy(x_ref, o_ref.at[me], local_sem)

    def copy(j, outgoing):
        peer = (_flip(mx, j & 4), _flip(my, j & 2), _flip(mc, j & 1))
        slot = me if outgoing else 4 * peer[0] + 2 * peer[1] + peer[2]
        return pltpu.make_async_remote_copy(
            src_ref=x_ref, dst_ref=o_ref.at[slot], send_sem=send_sems.at[j - 1], recv_sem=recv_sems.at[j - 1],
            device_id=peer, device_id_type=MESH)

    def start():
        mine().start()
        for j in range(1, 8):
            copy(j, True).start()

    def wait():
        for j in range(1, 8):
            copy(j, False).wait()
        mine().wait()

    return start, wait


def _ride_all8(x):
    return dict(xs=[x], shapes=[jax.ShapeDtypeStruct((8,) + x.shape, x.dtype)],
                sems=[pltpu.SemaphoreType.DMA((7,)), pltpu.SemaphoreType.DMA((7,)), pltpu.SemaphoreType.DMA],
                ops=lambda x_refs, o_refs, sems: _allgather8_ops(x_refs[0], o_refs[0], *sems))


def _ride_chip(xs, gather):
    return dict(xs=list(xs), shapes=_chip_exchange_shapes(xs, gather), sems=_chip_exchange_sems(len(xs)),
                ops=lambda x_refs, o_refs, sems: _chip_exchange_ops(x_refs, o_refs, *sems, gather))


def _allgather8(x, name):
    def body(x_ref, o_ref, *sems):
        start, wait = _allgather8_ops(x_ref, o_ref, *sems)
        start()
        wait()

    ride = _ride_all8(x)
    return pl.pallas_call(body, name=name, in_specs=[ANY], out_specs=ANY, out_shape=ride["shapes"][0],
                          scratch_shapes=ride["sems"])(x)


def _gather_halves_ops(x_ref, o_ref, ici_send, ici_recv, d2d_send, d2d_recv, local_sem):
    half = x_ref.shape[0] // 2
    mx, my, mc = lax.axis_index("x"), lax.axis_index("y"), lax.axis_index("c")
    k0 = 2 * mx + my
    mine = pl.ds(pl.multiple_of(mc * half, 16), half)
    theirs = pl.ds(pl.multiple_of((1 - mc) * half, 16), half)

    def local():
        return pltpu.make_async_copy(x_ref, o_ref.at[k0], local_sem)

    def chips(j):
        px, py = _flip(mx, j & 2), _flip(my, j & 1)
        return px, py, 2 * px + py

    def over_ici(j, outgoing):
        px, py, kp = chips(j)
        dst = o_ref.at[k0, mine] if outgoing else o_ref.at[kp, mine]
        return pltpu.make_async_remote_copy(
            src_ref=x_ref.at[mine], dst_ref=dst, send_sem=ici_send.at[j - 1], recv_sem=ici_recv.at[j - 1],
            device_id=(px, py, mc), device_id_type=MESH)

    def over_d2d(j, outgoing):
        _, _, kp = chips(j)
        rows = mine if outgoing else theirs
        return pltpu.make_async_remote_copy(
            src_ref=o_ref.at[kp, rows], dst_ref=o_ref.at[kp, rows], send_sem=d2d_send.at[j - 1],
            recv_sem=d2d_recv.at[j - 1], device_id=(mx, my, 1 - mc), device_id_type=MESH)

    def start():
        local().start()
        for j in range(1, 4):
            over_ici(j, True).start()

    def wait():
        for j in range(1, 4):
            over_ici(j, False).wait_recv()
            over_d2d(j, True).start()
        for j in range(1, 4):
            over_ici(j, True).wait_send()
            over_d2d(j, True).wait_send()
            over_d2d(j, False).wait_recv()
        local().wait()

    return start, wait


def _ride_halves(x):
    dma3 = pltpu.SemaphoreType.DMA((3,))
    return dict(xs=[x], shapes=[jax.ShapeDtypeStruct((4,) + x.shape, x.dtype)],
                sems=[dma3, dma3, dma3, dma3, pltpu.SemaphoreType.DMA],
                ops=lambda x_refs, o_refs, sems: _gather_halves_ops(x_refs[0], o_refs[0], *sems))


def _chip_exchange(xs, gather, name):
    n = len(xs)

    def body(*refs):
        start, wait = _chip_exchange_ops(refs[:n], refs[n:2 * n], *refs[2 * n:], gather)
        start()
        wait()

    return pl.pallas_call(
        body, name=name, in_specs=[ANY] * n, out_specs=[ANY] * n, out_shape=_chip_exchange_shapes(xs, gather),
        scratch_shapes=_chip_exchange_sems(n),
    )(*xs)


def _chip_exchange_shapes(xs, gather):
    return [jax.ShapeDtypeStruct(((4,) + x.shape) if gather else x.shape, x.dtype) for x in xs]


def _chip_exchange_sems(n):
    return [pltpu.SemaphoreType.DMA((3 * n,)), pltpu.SemaphoreType.DMA((3 * n,)), pltpu.SemaphoreType.DMA((n,))]


def _chip_exchange_ops(x_refs, o_refs, send_sems, recv_sems, local_sems, gather):
    n = len(x_refs)
    mx, my, mc = lax.axis_index("x"), lax.axis_index("y"), lax.axis_index("c")
    k0 = 2 * mx + my

    def local(a):
        src = x_refs[a] if gather else x_refs[a].at[k0]
        return pltpu.make_async_copy(src, o_refs[a].at[k0], local_sems.at[a])

    def copy(a, j, outgoing):
        px, py = _flip(mx, j & 2), _flip(my, j & 1)
        kp = 2 * px + py
        if outgoing:
            src = x_refs[a] if gather else x_refs[a].at[kp]
            dst = o_refs[a].at[k0]
        else:
            src = x_refs[a] if gather else x_refs[a].at[k0]
            dst = o_refs[a].at[kp]
        s = a * 3 + j - 1
        return pltpu.make_async_remote_copy(
            src_ref=src, dst_ref=dst, send_sem=send_sems.at[s], recv_sem=recv_sems.at[s],
            device_id=(px, py, mc), device_id_type=MESH)

    def start():
        for a in range(n):
            local(a).start()
            for j in range(1, 4):
                copy(a, j, True).start()

    def wait():
        for a in range(n):
            for j in range(1, 4):
                copy(a, j, False).wait()
            local(a).wait()

    return start, wait


def _call(body, *, name, grid, in_specs, out_specs, out_shape, args, scratch_shapes=(), sem=None, ride=None):
    if not ride:
        return pl.pallas_call(
            body, name=name, grid=grid, in_specs=list(in_specs), out_specs=list(out_specs), out_shape=list(out_shape),
            scratch_shapes=list(scratch_shapes), compiler_params=_params(sem))(*args)
    xs = [x for r in ride for x in r["xs"]]
    shapes = [s for r in ride for s in r["shapes"]]
    sems = [s for r in ride for s in r["sems"]]
    n_in, n_out, n_scr, nx = len(in_specs), len(out_specs), len(scratch_shapes), len(xs)

    def wrapped(*refs):
        ins, x_refs = refs[:n_in], refs[n_in:n_in + nx]
        outs = refs[n_in + nx:n_in + nx + n_out]
        lands = refs[n_in + nx + n_out:n_in + 2 * nx + n_out]
        rest = refs[n_in + 2 * nx + n_out:]
        scr, sem_refs = rest[:n_scr], rest[n_scr:]
        ops, xo, so = [], 0, 0
        for r in ride:
            nr, ns = len(r["xs"]), len(r["sems"])
            ops.append(r["ops"](x_refs[xo:xo + nr], lands[xo:xo + nr], sem_refs[so:so + ns]))
            xo, so = xo + nr, so + ns
        ids = [pl.program_id(a) for a in range(len(grid))]
        first = functools.reduce(jnp.logical_and, [i == 0 for i in ids])
        last = functools.reduce(jnp.logical_and, [i == g - 1 for i, g in zip(ids, grid)])

        @pl.when(first)
        def _():
            for start, _ in ops:
                start()

        body(*ins, *outs, *scr)

        @pl.when(last)
        def _():
            for _, wait in ops:
                wait()

    return pl.pallas_call(
        wrapped, name=name, grid=grid, in_specs=list(in_specs) + [ANY] * nx, out_specs=list(out_specs) + [ANY] * nx,
        out_shape=list(out_shape) + shapes, scratch_shapes=list(scratch_shapes) + sems,
        compiler_params=_params(("arbitrary",) * len(grid)))(*args, *xs)


def _sibling_exchange(xs, name):
    n = len(xs)

    def body(*refs):
        x_refs, o_refs = refs[:n], refs[n:2 * n]
        send_sems, recv_sems = refs[2 * n:]
        sib = (lax.axis_index("x"), lax.axis_index("y"), 1 - lax.axis_index("c"))
        copies = [pltpu.make_async_remote_copy(
            src_ref=x_refs[a], dst_ref=o_refs[a], send_sem=send_sems.at[a], recv_sem=recv_sems.at[a],
            device_id=sib, device_id_type=MESH) for a in range(n)]
        for cp in copies:
            cp.start()
        for cp in copies:
            cp.wait()

    return pl.pallas_call(
        body, name=name, in_specs=[ANY] * n, out_specs=[ANY] * n,
        out_shape=[jax.ShapeDtypeStruct(x.shape, x.dtype) for x in xs],
        scratch_shapes=[pltpu.SemaphoreType.DMA((n,)), pltpu.SemaphoreType.DMA((n,))],
    )(*xs)


PACK = 1024
PACK_ROWS = 512


def _pack(parts):
    flat = []
    for p in parts:
        v = p.reshape(-1).astype(F32)
        flat.append(jnp.pad(v, (0, (-v.shape[0]) % PACK)))
    total = sum(v.shape[0] for v in flat)
    flat.append(jnp.zeros(((-total) % (PACK_ROWS * 128),), F32))
    return jnp.concatenate(flat).reshape(-1, 128)


def _shard_columns(shards, lo, hi):
    width = shards.shape[2]
    out = []
    for k in range(shards.shape[0]):
        a, b = max(lo, k * width), min(hi, (k + 1) * width)
        if a < b:
            out.append(shards[k, :, a - k * width:b - k * width])
    return out


def _unpack_rows(gathered, shapes):
    flat = gathered.reshape(gathered.shape[0], -1)
    out, off = [], 0
    for shp in shapes:
        n = math.prod(shp)
        out.append(flat[:, off:off + n].reshape((flat.shape[0],) + tuple(shp)))
        off += n + (-n) % PACK
    return out


def _unpack(packed, shapes):
    flat = packed.reshape(-1)
    out, off = [], 0
    for shp in shapes:
        n = math.prod(shp)
        out.append(flat[off:off + n].reshape(shp))
        off += n + (-n) % PACK
    return out


def kernel(x, c, ln_pre_g, ln_post_g, w_mod, b_mod, w_in_ab, w_out_ab, sgu_norm_g, sgu_w, sgu_b, w_in_ssm, w_out_ssm, lam_re, lam_im, b_re, b_im, c_re, c_im, d_skip, log_dt, w_glu, b_glu, loss_target, m_ln_pre_g, m_ln_post_g, m_w_mod, m_b_mod, m_w_in_ab, m_w_out_ab, m_sgu_norm_g, m_sgu_w, m_sgu_b, m_w_in_ssm, m_w_out_ssm, m_lam_re, m_lam_im, m_b_re, m_b_im, m_c_re, m_c_im, m_d_skip, m_log_dt, m_w_glu, m_b_glu, v_ln_pre_g, v_ln_post_g, v_w_mod, v_b_mod, v_w_in_ab, v_w_out_ab, v_sgu_norm_g, v_sgu_w, v_sgu_b, v_w_in_ssm, v_w_out_ssm, v_lam_re, v_lam_im, v_b_re, v_b_im, v_c_re, v_c_im, v_d_skip, v_log_dt, v_w_glu, v_b_glu):
    given = dict(locals())
    mx, my, mc = lax.axis_index("x"), lax.axis_index("y"), lax.axis_index("c")
    me = 4 * mx + 2 * my + mc
    chip = 2 * mx + my

    _, l, d = x.shape
    x2, tgt = x[0], loss_target[0]
    n_in = w_in_ab.shape[2] * 4
    wa = wb = n_in // 7
    w = w_out_ssm.shape[1]
    g, p, cch = b_re.shape[1:]
    nmod = w_mod.shape[2]

    later_shards = [_bf(w_out_ab[0]), _bf(w_in_ssm[0]), _bf(w_out_ssm[0]), _bf(w_glu[0]), d_skip, b_glu]

    cond = _silu_rows(c.reshape(d // 128, 128), "cond_silu")
    cond_all = _allgather8(cond, "gather_cond").reshape(8, d)
    b_shard = lax.dynamic_slice(b_mod, (0, chip * nmod), (2, nmod)).reshape(2, 1, nmod)
    cond_pad = jnp.pad(cond_all, ((0, MOD_ROWS - 8), (0, 0)))
    modp = _mod_fwd(cond_pad, w_mod, b_shard, "mod_fwd")[:, :8]
    modp_all = _allgather8(modp.reshape(16, nmod), "gather_mod").reshape(4, 2, 2, 8, nmod)
    mine = lax.dynamic_index_in_dim(lax.dynamic_index_in_dim(modp_all, mc, 1, False), me, 2, False)
    mod = mine.transpose(1, 0, 2).reshape(2, 3 * d)
    shift = [mod[a:a + 1, :d] for a in range(2)]
    scale = [mod[a:a + 1, d:2 * d] for a in range(2)]
    gate = [mod[a:a + 1, 2 * d:] for a in range(2)]
    pre_g = [ln_pre_g[a:a + 1] for a in range(2)]
    post_g = [ln_post_g[a:a + 1] for a in range(2)]

    sgu_w0, sgu_bt = sgu_w[0], sgu_b[0].T
    h0, (gw_in_ab,) = _pre_fwd(x2, pre_g[0], scale[0], shift[0], "pre0_fwd", ride=[_ride_halves(_bf(w_in_ab[0]))])
    w_gates = jnp.concatenate(_shard_columns(gw_in_ab, 0, 3 * wa) + _shard_columns(gw_in_ab, 3 * wa + 3 * wb, n_in),
                              axis=1)
    w_qkv = jnp.concatenate(_shard_columns(gw_in_ab, 3 * wa, 3 * wa + 3 * wb), axis=1)
    proj0 = _matmul(h0, w_gates, "nn", F32, "proj0", tm=1024)
    qkv = _matmul(h0, w_qkv, "nn", BF16, "proj0_qkv", tm=1024)
    out_b, lktot, (gw_out_ab, gw_in_ssm, gw_out_ssm, gw_glu, g_dskip, g_bglu) = _attn_fwd(
        qkv, wb, "attn_fwd", hp=8, ride=[_ride_chip(later_shards, True)])
    wout_ab = gw_out_ab.reshape(wa + wb, d)
    win_ssm = gw_in_ssm.reshape(d, 2 * w)
    wout_ssm = jnp.concatenate([gw_out_ssm[k] for k in range(4)], axis=1)
    wglu = gw_glu.reshape(w, w)
    dskip_full = g_dskip.reshape(1, w)
    bglu_full = g_bglu.reshape(1, w)
    cat =_sgu_fwd(proj0, out_b, sgu_norm_g, sgu_w0, sgu_bt, wa, wb, "sgu_fwd")
    y0 = _matmul(cat, wout_ab, "nn", BF16, "out0", tm=1024)
    x1, h1 = _post_pre_fwd(x2, y0, gate[0], post_g[0], pre_g[1], scale[1], shift[1], "post0_pre1_fwd")

    s = g * p
    lr_c, li_c = lam_re.reshape(s, 1), lam_im.reshape(s, 1)
    ldt_c = jnp.repeat(log_dt.reshape(g), p).reshape(s, 1)
    br_c, bi_c = b_re.reshape(s, cch), b_im.reshape(s, cch)
    bb_re, bb_im, pw_re, pw_im = _ssm_prep(lr_c, li_c, ldt_c, br_c, bi_c, lr_c.reshape(1, s), li_c.reshape(1, s),
                                           ldt_c.reshape(1, s), "ssm_prep")
    bbd = _bf(_block_diag_b(bb_re, bb_im, g, p, cch))
    ccd = _bf(_block_diag_c(c_re[0], c_im[0], g, p, cch))
    proj1 = _matmul(h1, win_ssm, "nn", F32, "proj1", tm=1024)
    y_ssm, hs_re, hs_im, h_all = _ssm_fwd(proj1, bbd, ccd, pw_re, pw_im, dskip_full, w, "ssm_fwd")
    o1 = _glu_fwd(y_ssm, proj1, wglu, bglu_full, "glu_fwd")
    y1 = _matmul(o1, wout_ssm, "nn", BF16, "out1", tm=1024)
    loss_vec, dy1, dx2, dgate1, dpost1 = _post_loss(x1, y1, gate[1], post_g[1], tgt, "post1_loss")

    do1 = _matmul(dy1, wout_ssm, "nt", BF16, "out1_dx", tm=1024)
    gr_wout_ssm = _matmul(o1, dy1, "tn", BF16, "out1_dw", tm=1024, tk=1024, n_split=4)
    dy_ssm, dz1, gr_wglu, gr_bglu = _glu_bwd(do1, y_ssm, proj1, wglu, bglu_full, "glu_bwd")
    du1, dbbd, dccd, da_re, da_im, gr_dskip, (ld_wout_ssm, ld_wglu) = _ssm_bwd(
        proj1, dy_ssm, hs_re, hs_im, h_all, bbd, ccd, pw_re, pw_im, dskip_full, w, "ssm_bwd",
        ride=[_ride_chip([gr_wout_ssm, gr_wglu.reshape(4, w // 4, w)], False)])
    dproj1 = jnp.concatenate([du1, dz1], axis=1)
    dh1 = _matmul(dproj1, win_ssm, "nt", BF16, "proj1_dx", tm=1024)
    gr_win_ssm = _matmul(h1, dproj1, "tn", BF16, "proj1_dw", tm=1024, tn=1024, tk=1024)
    dx1, dscale1, dshift1, dpre1, dy0, dgate0, dpost0 = _pre_bwd(
        dh1, dx2, x1, pre_g[1], scale[1], "pre1_post0_bwd", post=(y0, gate[0], post_g[0]))

    dcat = _matmul(dy0, wout_ab, "nt", BF16, "out0_dx", tm=1024)
    gr_wout_ab = _matmul(cat, dy0, "tn", BF16, "out0_dw", tm=1024, tn=1024, tk=1024)
    dbb_re, dbb_im = _diag_of_b(dbbd, g, p, cch)
    dc_re, dc_im = _diag_of_c(dccd, g, p, cch)
    part_a = [loss_vec[:, :1], dpre1, dpost0, dpost1, dgate0, dshift1, dscale1, dgate1, da_re, da_im,
              dbb_re, dbb_im, dc_re, dc_im, gr_dskip, gr_bglu]
    shapes_a = [a.shape for a in part_a]
    dq, dk, dv, (ld_win_ssm, ld_wout_ab, gath_a) = _attn_bwd(
        qkv, proj0, dcat, lktot, wa, wb, "attn_bwd", hp=4,
        ride=[_ride_chip([gr_win_ssm.reshape(4, d // 4, 2 * w), gr_wout_ab.reshape(4, (wa + wb) // 4, d)], False),
              _ride_all8(_pack(part_a))])
    dproj0, gr_sgu_w, gr_sgu_bt, gr_sgu_g = _sgu_bwd(proj0, out_b, dcat, dq, dk, dv, sgu_norm_g, sgu_w0, sgu_bt,
                                                     wa, wb, "sgu_bwd")
    part_b = [gr_sgu_g, gr_sgu_w, gr_sgu_bt.T]
    shapes_b = [a.shape for a in part_b]
    gr_win_ab_lo, (gath_b,) = _matmul(h0, dproj0, "tn", BF16, "proj0_dw_lo", tm=1024, tk=1024, tn=896, n_split=4,
                                      m_part=(0, 1, 2), ride=[_ride_all8(_pack(part_b))])
    gr_win_ab_hi, (ld_win_ab_lo,) = _matmul(
        h0, dproj0, "tn", BF16, "proj0_dw_hi", tm=1024, tk=1024, tn=896, n_split=4, m_part=(1, 1, 2),
        ride=[_ride_chip([gr_win_ab_lo], False)])
    dh0, (ld_win_ab_hi,) = _matmul(dproj0, gw_in_ab, "nt", BF16, "proj0_dx", tm=1024, tn=1024,
                                   ride=[_ride_chip([gr_win_ab_hi], False)])
    grad_x, dscale0, dshift0, dpre0 = _pre_bwd(dh0, dx1, x2, pre_g[0], scale[0], "pre0_bwd")
    part_c = [dpre0, dshift0, dscale0]
    shapes_c = [a.shape for a in part_c]
    gath_c = _allgather8(_pack(part_c), "gather_small_tail")

    landed = [ld_wout_ab, ld_win_ssm, ld_wout_ssm, ld_wglu]
    big_names = ["w_in_ab", "w_out_ab", "w_in_ssm", "w_out_ssm", "w_glu"]
    sums = [jnp.concatenate([_sum_leading(ld_win_ab_lo, "sum_w_in_ab_lo"), _sum_leading(ld_win_ab_hi, "sum_w_in_ab_hi")],
                            axis=0)]
    sums += [_sum_leading(a, "sum_" + nm) for a, nm in zip(landed, big_names[1:])]
    sib = _sibling_exchange(sums, "sibling_grads")
    results = {}
    for nm, s_mine, s_sib in zip(big_names, sums, sib):
        shp = given[nm].shape
        two_d = lambda a: a.reshape(-1, shp[-1])
        outs = _adamw(two_d(given[nm]), [s_mine, s_sib], two_d(given["m_" + nm]), two_d(given["v_" + nm]),
                      "adamw_" + nm)
        results[nm] = [o.reshape(shp) for o in outs]

    (loss_s, g_pre1, g_post0, g_post1, g_gate0, g_shift1, g_scale1, g_gate1, s_da_re, s_da_im, s_dbb_re, s_dbb_im,
     g_c_re, g_c_im, g_dskip_full, g_bglu_full) = _unpack(_sum_leading(gath_a, "sum_small_a"), shapes_a)
    g_sgu_g, g_sgu_w, g_sgu_b = _unpack(_sum_leading(gath_b, "sum_small_b"), shapes_b)
    g_pre0, g_shift0, g_scale0 = _unpack(_sum_leading(gath_c, "sum_small_c"), shapes_c)
    loss = loss_s.reshape(())
    g_pre = jnp.concatenate([g_pre0, g_pre1], axis=0)
    g_post = jnp.concatenate([g_post0, g_post1], axis=0)
    g_bmod = jnp.concatenate([jnp.concatenate([g_shift0, g_scale0, g_gate0], axis=1),
                              jnp.concatenate([g_shift1, g_scale1, g_gate1], axis=1)], axis=0)

    g_lr, g_li, g_ldt, g_br, g_bi = _ssm_prep_bwd(lr_c, li_c, ldt_c, br_c, bi_c, s_da_re.reshape(s, 1),
                                                  s_da_im.reshape(s, 1), s_dbb_re, s_dbb_im, p, "ssm_prep_bwd")
    small = {
        "ln_pre_g": g_pre, "ln_post_g": g_post, "b_mod": g_bmod, "sgu_norm_g": g_sgu_g,
        "sgu_w": g_sgu_w.reshape(sgu_w.shape), "sgu_b": g_sgu_b.reshape(sgu_b.shape),
        "lam_re": g_lr.reshape(lam_re.shape), "lam_im": g_li.reshape(lam_im.shape),
        "b_re": g_br.reshape(b_re.shape), "b_im": g_bi.reshape(b_im.shape),
        "c_re": g_c_re.reshape(c_re.shape), "c_im": g_c_im.reshape(c_im.shape),
        "d_skip": lax.dynamic_slice(g_dskip_full, (0, chip * (w // 4)), (1, w // 4)),
        "log_dt": g_ldt.reshape(log_dt.shape),
        "b_glu": lax.dynamic_slice(g_bglu_full, (0, chip * (w // 4)), (1, w // 4)),
    }
    flat2 = lambda a: a.reshape(-1, a.shape[-1])
    wide = ("b_re", "b_im")
    for tag, group in (("adamw_small", [nm for nm in small if nm not in wide]), ("adamw_small_b", list(wide))):
        outs = _adamw_many([flat2(given[nm]) for nm in group], [flat2(small[nm]) for nm in group],
                           [flat2(given["m_" + nm]) for nm in group], [flat2(given["v_" + nm]) for nm in group], tag)
        for nm, trio in zip(group, outs):
            results[nm] = [small[nm]] + [o.reshape(given[nm].shape) for o in trio]

    rows_a = _unpack_rows(gath_a, shapes_a)
    rows_c = _unpack_rows(gath_c, shapes_c)
    dmod_rows = jnp.concatenate([rows_c[1], rows_c[2], rows_a[4], rows_a[5], rows_a[6], rows_a[7]],
                                axis=2).reshape(8, 2, 3 * d)
    dmod_shard = lax.dynamic_slice(dmod_rows, (0, 0, chip * nmod), (8, 2, nmod)).transpose(1, 0, 2)
    dmod_pad = jnp.pad(dmod_shard, ((0, 0), (0, MOD_ROWS - 8), (0, 0)))
    gr_wmod = _mod_bwd(cond_pad.T, dmod_pad, "mod_bwd")
    two_d = lambda a: a.reshape(-1, nmod)
    outs = _adamw(two_d(w_mod), [two_d(gr_wmod)], two_d(m_w_mod), two_d(v_w_mod), "adamw_w_mod")
    results["w_mod"] = [o.reshape(w_mod.shape) for o in outs]

    names = ["ln_pre_g", "ln_post_g", "w_mod", "b_mod", "w_in_ab", "w_out_ab", "sgu_norm_g", "sgu_w", "sgu_b",
             "w_in_ssm", "w_out_ssm", "lam_re", "lam_im", "b_re", "b_im", "c_re", "c_im", "d_skip", "log_dt",
             "w_glu", "b_glu"]
    return (loss, grad_x[None], *[results[nm][0] for nm in names], *[results[nm][1] for nm in names],
            *[results[nm][2] for nm in names], *[results[nm][3] for nm in names])
```

```python
import functools
import math

import jax
import jax.numpy as jnp
from jax import lax
from jax.experimental import pallas as pl
from jax.experimental.pallas import tpu as pltpu

F32 = jnp.float32
BF16 = jnp.bfloat16
MESH = pl.DeviceIdType.MESH

EPS = 1e-6
HEAD = 128
SSM_T = 512
SSM_GB = 16
ADAM_LR, ADAM_B1, ADAM_B2, ADAM_EPS, ADAM_WD, ADAM_STEP = 0.001, 0.9, 0.999, 1e-08, 0.01, 10
VMEM_LIMIT = 56 * 1024 * 1024

NN = (((1,), (0,)), ((), ()))
NT = (((1,), (1,)), ((), ()))
TN = (((0,), (0,)), ((), ()))


def _params(sem=None):
    return pltpu.CompilerParams(dimension_semantics=sem, vmem_limit_bytes=VMEM_LIMIT)


def _dot(a, b, dims=NN):
    return lax.dot_general(a, b, dims, preferred_element_type=F32)


def _bf(x):
    return x.astype(BF16)


def _gelu(x):
    k = math.sqrt(2.0 / math.pi)
    t = jnp.tanh(k * (x + 0.044715 * x * x * x))
    return 0.5 * x * (1.0 + t)


def _gelu_grad(x):
    k = math.sqrt(2.0 / math.pi)
    x2 = x * x
    t = jnp.tanh(k * (x + 0.044715 * x * x2))
    return 0.5 * (1.0 + t) + 0.5 * x * (1.0 - t * t) * k * (1.0 + 3.0 * 0.044715 * x2)


def _sigmoid(x):
    return 1.0 / (1.0 + jnp.exp(-x))


def _silu(x):
    return x * _sigmoid(x)


def _silu_grad(x):
    s = _sigmoid(x)
    return s * (1.0 + x * (1.0 - s))


def _tile(n, t, mult=128):
    if n <= t:
        return n
    for cand in range(t - t % mult, 0, -mult):
        if n % cand == 0:
            return cand
    raise ValueError((n, t, mult))


def _matmul(a, b, mode, out_dtype, name, tm=512, tn=512, tk=2048, n_split=1, ride=None, m_part=None):
    b_sharded = b.ndim == 3
    if mode == "nn":
        (m, kk), (_, n) = a.shape, b.shape
    elif b_sharded:
        assert mode == "nt"
        (m, kk), n, tk = a.shape, b.shape[1], b.shape[2]
    elif mode == "nt":
        (m, kk), (n, _) = a.shape, b.shape
    else:
        (kk, m), (_, n) = a.shape, b.shape
    m_off = 0
    if m_part is not None:
        assert mode == "tn"
        first, count, parts = m_part
        tm = _tile(m // parts, tm)
        m_off = first * (m // parts) // tm
        m = count * (m // parts)
    tm, tk = _tile(m, tm), _tile(kk, tk)
    ns = n // n_split
    tn = _tile(ns, tn)
    nk = kk // tk
    dims = {"nn": NN, "nt": NT, "tn": TN}[mode]

    def body(a_ref, b_ref, o_ref, acc_ref):
        k = pl.program_id(2)
        part = _dot(_bf(a_ref[...]), _bf(b_ref[0] if b_sharded else b_ref[...]), dims)

        @pl.when(k == 0)
        def _():
            acc_ref[...] = part

        @pl.when(k > 0)
        def _():
            acc_ref[...] += part

        @pl.when(k == nk - 1)
        def _():
            o_ref[...] = acc_ref[...].astype(out_dtype).reshape(o_ref.shape)

    if mode == "nn":
        a_spec = pl.BlockSpec((tm, tk), lambda i, j, k: (i, k))
        b_spec = pl.BlockSpec((tk, tn), lambda i, j, k: (k, j))
    elif mode == "nt":
        a_spec = pl.BlockSpec((tm, tk), lambda i, j, k: (i, k))
        b_spec = (pl.BlockSpec((1, tn, tk), lambda i, j, k: (k, j, 0)) if b_sharded
                  else pl.BlockSpec((tn, tk), lambda i, j, k: (j, k)))
    else:
        a_spec = pl.BlockSpec((tk, tm), lambda i, j, k: (k, i + m_off))
        b_spec = pl.BlockSpec((tk, tn), lambda i, j, k: (k, j))
    if n_split == 1:
        out_shape = jax.ShapeDtypeStruct((m, n), out_dtype)
        o_spec = pl.BlockSpec((tm, tn), lambda i, j, k: (i, j))
    else:
        per = ns // tn
        out_shape = jax.ShapeDtypeStruct((n_split, m, ns), out_dtype)
        o_spec = pl.BlockSpec((1, tm, tn), lambda i, j, k: (j // per, i, j % per))
    outs = _call(body, name=name, grid=(m // tm, n // tn, nk), in_specs=[a_spec, b_spec], out_specs=[o_spec],
                 out_shape=[out_shape], scratch_shapes=[pltpu.VMEM((tm, tn), F32)], args=(a, b),
                 sem=("parallel", "parallel", "arbitrary"), ride=ride)
    return outs[0] if ride is None else (outs[0], outs[1:])


def _row_spec(tm, d):
    return pl.BlockSpec((tm, d), lambda i: (i, 0))


def _vec_spec(d):
    return pl.BlockSpec((1, d), lambda i: (0, 0))


def _acc(ref, first, val):
    @pl.when(first)
    def _():
        ref[...] = val

    @pl.when(jnp.logical_not(first))
    def _():
        ref[...] += val


def _colsum(x):
    return jnp.sum(x, axis=0, keepdims=True)


def _rownorm(x):
    r = lax.rsqrt(jnp.mean(x * x, axis=-1, keepdims=True) + EPS)
    return x * r, r


def _pre_fwd(x, g, scale, shift, name, ride=None):
    l, d = x.shape
    tm = _tile(l, 256)

    def body(x_ref, g_ref, sc_ref, sh_ref, h_ref):
        n, _ = _rownorm(x_ref[...])
        h_ref[...] = _bf(n * g_ref[...] * (1.0 + sc_ref[...]) + sh_ref[...])

    outs = _call(body, name=name, grid=(l // tm,), in_specs=[_row_spec(tm, d), _vec_spec(d), _vec_spec(d), _vec_spec(d)],
                 out_specs=[_row_spec(tm, d)], out_shape=[jax.ShapeDtypeStruct((l, d), BF16)],
                 args=(x, g, scale, shift), sem=("parallel",), ride=ride)
    return outs[0], outs[1:]


def _post_pre_fwd(x, y, gate, pg, g1, scale1, shift1, name):
    l, d = x.shape
    tm = _tile(l, 256)

    def body(x_ref, y_ref, gate_ref, pg_ref, g1_ref, sc_ref, sh_ref, x1_ref, h1_ref):
        ny, _ = _rownorm(y_ref[...].astype(F32))
        x1 = x_ref[...] + gate_ref[...] * (ny * pg_ref[...])
        x1_ref[...] = x1
        n1, _ = _rownorm(x1)
        h1_ref[...] = _bf(n1 * g1_ref[...] * (1.0 + sc_ref[...]) + sh_ref[...])

    v = _vec_spec(d)
    return pl.pallas_call(
        body, name=name, grid=(l // tm,),
        in_specs=[_row_spec(tm, d), _row_spec(tm, d), v, v, v, v, v],
        out_specs=[_row_spec(tm, d), _row_spec(tm, d)],
        out_shape=[jax.ShapeDtypeStruct((l, d), F32), jax.ShapeDtypeStruct((l, d), BF16)],
        compiler_params=_params(("parallel",)),
    )(x, y, gate, pg, g1, scale1, shift1)


def _post_loss(x1, y1, gate, pg, target, name):
    l, d = x1.shape
    tm = _tile(l, 256)

    def body(x_ref, y_ref, gate_ref, pg_ref, t_ref, loss_ref, dy_ref, dx_ref, dgate_ref, dpg_ref):
        first = pl.program_id(0) == 0
        ny, ry = _rownorm(y_ref[...].astype(F32))
        q = ny * pg_ref[...]
        x2 = x_ref[...] + gate_ref[...] * q
        e = x2 - t_ref[...]
        _acc(loss_ref, first, jnp.full((1, 128), 0.5 / d, F32) * jnp.sum(e * e))
        dx2 = e * (1.0 / d)
        dx_ref[...] = dx2
        _acc(dgate_ref, first, _colsum(dx2 * q))
        dq = dx2 * gate_ref[...]
        _acc(dpg_ref, first, _colsum(dq * ny))
        dny = dq * pg_ref[...]
        dy = ry * (dny - ny * jnp.mean(dny * ny, axis=-1, keepdims=True))
        dy_ref[...] = _bf(dy)

    v = _vec_spec(d)
    return pl.pallas_call(
        body, name=name, grid=(l // tm,),
        in_specs=[_row_spec(tm, d), _row_spec(tm, d), v, v, _row_spec(tm, d)],
        out_specs=[_vec_spec(128), _row_spec(tm, d), _row_spec(tm, d), v, v],
        out_shape=[jax.ShapeDtypeStruct((1, 128), F32), jax.ShapeDtypeStruct((l, d), BF16),
                   jax.ShapeDtypeStruct((l, d), F32), jax.ShapeDtypeStruct((1, d), F32),
                   jax.ShapeDtypeStruct((1, d), F32)],
        compiler_params=_params(("arbitrary",)),
    )(x1, y1, gate, pg, target)


def _pre_bwd(dh, dres, x, g, scale, name, post=None):
    l, d = x.shape
    tm = _tile(l, 256)
    with_post = post is not None

    def body(*refs):
        if with_post:
            (dh_ref, dres_ref, x_ref, g_ref, sc_ref, y_ref, gate_ref, pg_ref,
             dx_ref, dsc_ref, dsh_ref, dg_ref, dy_ref, dgate_ref, dpg_ref) = refs
        else:
            dh_ref, dres_ref, x_ref, g_ref, sc_ref, dx_ref, dsc_ref, dsh_ref, dg_ref = refs
        first = pl.program_id(0) == 0
        dh = dh_ref[...].astype(F32)
        n, r = _rownorm(x_ref[...])
        _acc(dsc_ref, first, _colsum(dh * (n * g_ref[...])))
        _acc(dsh_ref, first, _colsum(dh))
        dyn = dh * (1.0 + sc_ref[...])
        _acc(dg_ref, first, _colsum(dyn * n))
        dn = dyn * g_ref[...]
        dx = dres_ref[...] + r * (dn - n * jnp.mean(dn * n, axis=-1, keepdims=True))
        dx_ref[...] = dx
        if with_post:
            ny, ry = _rownorm(y_ref[...].astype(F32))
            _acc(dgate_ref, first, _colsum(dx * (ny * pg_ref[...])))
            dq = dx * gate_ref[...]
            _acc(dpg_ref, first, _colsum(dq * ny))
            dny = dq * pg_ref[...]
            dy_ref[...] = _bf(ry * (dny - ny * jnp.mean(dny * ny, axis=-1, keepdims=True)))

    v = _vec_spec(d)
    row = _row_spec(tm, d)
    vec_out = jax.ShapeDtypeStruct((1, d), F32)
    in_specs = [row, row, row, v, v]
    args = [dh, dres, x, g, scale]
    out_specs = [row, v, v, v]
    out_shape = [jax.ShapeDtypeStruct((l, d), F32), vec_out, vec_out, vec_out]
    if with_post:
        in_specs += [row, v, v]
        args += list(post)
        out_specs += [row, v, v]
        out_shape += [jax.ShapeDtypeStruct((l, d), BF16), vec_out, vec_out]
    return pl.pallas_call(
        body, name=name, grid=(l // tm,), in_specs=in_specs, out_specs=out_specs, out_shape=out_shape,
        compiler_params=_params(("arbitrary",)),
    )(*args)


def _softplus_parts(z):
    e = jnp.exp(-jnp.abs(z))
    den = 1.0 + e
    lb = jnp.minimum(z, 0.0) - jnp.log(den)
    return lb, lb - z, jnp.exp(lb)


def _tri(cmp, n=HEAD):
    row = lax.broadcasted_iota(jnp.int32, (n, n), 0)
    col = lax.broadcasted_iota(jnp.int32, (n, n), 1)
    return cmp(row, col)


ATT_T = 256
ATT_DEAD = 104.0


def _any_alive(runs):
    return functools.reduce(jnp.maximum, [jnp.max(r) for r in runs]) > -ATT_DEAD


def _attn_fwd(qkv, wb, name, hp=4, ride=None):
    l = qkv.shape[0]
    t = ATT_T
    nh, nq = wb // HEAD, l // t
    hp = min(hp, nh)
    ng, wg = nh // hp, hp * HEAD
    scale = 1.0 / math.sqrt(HEAD)

    def body(q_ref, k_ref, v_ref, o_ref):
        i = pl.program_id(1)
        valid = _tri(lambda r, c: c < r, t)
        m_gt = _bf(_tri(lambda r, c: r > c, t).astype(F32))

        def tile(j, carry, diag):
            rows = pl.ds(pl.multiple_of(j * t, t), t)
            cols = [slice(hh * HEAD, (hh + 1) * HEAD) for hh in range(hp)]
            zs = [_dot(q_ref[:, cs], k_ref[rows, cs], NT) * scale for cs in cols]
            lbs, lks = [], []
            for z in zs:
                lb, lk, _ = _softplus_parts(z)
                lbs.append(lb)
                lks.append(jnp.where(valid, lk, 0.0) if diag else lk)
            laters = [_dot(_bf(lk), m_gt) for lk in lks]
            ws = [jnp.exp(lb + later + run) for lb, later, (_, run) in zip(lbs, laters, carry)]
            if diag:
                ws = [jnp.where(valid, w, 0.0) for w in ws]
            return tuple((acc + _dot(_bf(w), v_ref[rows, cs]), run + jnp.sum(lk, axis=1, keepdims=True))
                         for w, lk, cs, (acc, run) in zip(ws, lks, cols, carry))

        zero = (jnp.zeros((t, HEAD), F32), jnp.zeros((t, 1), F32))
        carry = tile(i, (zero,) * hp, True)
        _, carry = lax.while_loop(lambda c: (c[0] < i) & _any_alive([run for _, run in c[1]]),
                                  lambda c: (c[0] + 1, tile(i - 1 - c[0], c[1], False)), (jnp.int32(0), carry))
        for hh, (acc, _) in enumerate(carry):
            o_ref[:, hh * HEAD:(hh + 1) * HEAD] = acc

    blk = lambda off: pl.BlockSpec((t, wg), lambda h, i: (i, off + h))
    full = lambda off: pl.BlockSpec((l, wg), lambda h, i: (0, off + h))
    out = pl.BlockSpec((t, wg), lambda h, i: (i, h))
    outs = _call(body, name=name, grid=(ng, nq), in_specs=[blk(0), full(ng), full(2 * ng)], out_specs=[out],
                 out_shape=[jax.ShapeDtypeStruct((l, wb), F32)],
                 args=(qkv, qkv, qkv), sem=("parallel", "arbitrary"), ride=ride)
    return outs[0], outs[1:]


def _attn_bwd(qkv, proj, dcat, wa, wb, name, hp=2, ride=None):
    l = qkv.shape[0]
    t = ATT_T
    nh, nq = wb // HEAD, l // t
    hp = min(hp, nh)
    ng, wg = nh // hp, hp * HEAD
    scale = 1.0 / math.sqrt(HEAD)

    def body(q_ref, k_ref, v_ref, bz_ref, dc_ref, dq_ref, dkt_out, dvt_out, do_s, qt_s, dot_s,
             dkt_ref, dvt_ref, out_sems):
        i = pl.program_id(1)

        @pl.when(i == 0)
        def _():
            dkt_ref[...] = jnp.zeros_like(dkt_ref)
            dvt_ref[...] = jnp.zeros_like(dvt_ref)

        do = dc_ref[...].astype(F32) * _silu(bz_ref[...])
        do_s[...] = _bf(do)
        for hh in range(hp):
            cs = slice(hh * HEAD, (hh + 1) * HEAD)
            qt_s[hh] = _bf(q_ref[:, cs].astype(F32).T * scale)
            dot_s[hh] = _bf(do[:, cs].T)
        valid = _tri(lambda r, c: c < r, t)
        m_le = _bf(_tri(lambda r, c: r <= c, t).astype(F32))
        m_lt = _bf(_tri(lambda r, c: r < c, t).astype(F32))

        heads = range(hp)
        cols = [slice(hh * HEAD, (hh + 1) * HEAD) for hh in heads]

        def row_sums(j, runs, diag):
            rows = pl.ds(pl.multiple_of(j * t, t), t)
            out = []
            for cs, run in zip(cols, runs):
                _, lk, _ = _softplus_parts(_dot(q_ref[:, cs], k_ref[rows, cs], NT) * scale)
                if diag:
                    lk = jnp.where(valid, lk, 0.0)
                out.append(run + jnp.sum(lk, axis=1, keepdims=True))
            return tuple(out)

        runs = row_sums(i, (jnp.zeros((t, 1), F32),) * hp, True)
        below, lktot = lax.while_loop(lambda c: (c[0] < i) & _any_alive(c[1]),
                                      lambda c: (c[0] + 1, row_sums(i - 1 - c[0], c[1], False)), (jnp.int32(0), runs))

        def tile(j, carry, diag):
            rows = pl.ds(pl.multiple_of(j * t, t), t)
            zs = [_dot(q_ref[:, cs], k_ref[rows, cs], NT) * scale for cs in cols]
            dws = [_dot(do_s[:, cs], v_ref[rows, cs], NT) for cs in cols]
            lbs, lks, sigs = [], [], []
            for z in zs:
                lb, lk, sig = _softplus_parts(z)
                lbs.append(lb)
                lks.append(jnp.where(valid, lk, 0.0) if diag else lk)
                sigs.append(sig)
            pins = [_dot(_bf(lk), m_le) for lk in lks]
            ws = [jnp.exp(lbs[hh] + (lktot[hh] - carry[hh][1]) - pins[hh]) for hh in heads]
            if diag:
                ws = [jnp.where(valid, w, 0.0) for w in ws]
            das = [dw * w for dw, w in zip(dws, ws)]
            pexs = [_dot(_bf(da), m_lt) for da in das]
            dzs = [das[hh] - sigs[hh] * (das[hh] + carry[hh][2] + pexs[hh]) for hh in heads]
            if diag:
                dzs = [jnp.where(valid, dz, 0.0) for dz in dzs]
            dzs = [_bf(dz) for dz in dzs]
            out = []
            for hh in heads:
                dkt, dvt = _dot(qt_s[hh], dzs[hh]), _dot(dot_s[hh], _bf(ws[hh]))
                for half in range(t // HEAD):
                    dkt_ref[hh, sub * j + half] += dkt[:, half * HEAD:(half + 1) * HEAD]
                    dvt_ref[hh, sub * j + half] += dvt[:, half * HEAD:(half + 1) * HEAD]
                dq, cpre, ppre = carry[hh]
                out.append((dq + _dot(dzs[hh], k_ref[rows, cols[hh]]), cpre + jnp.sum(lks[hh], axis=1, keepdims=True),
                            ppre + pexs[hh][:, t - 1:] + das[hh][:, t - 1:]))
            return tuple(out)

        zero = (jnp.zeros((t, HEAD), F32), jnp.zeros((t, 1), F32), jnp.zeros((t, 1), F32))
        carry = lax.fori_loop(i - below, i, lambda j, c: tile(j, c, False), (zero,) * hp)
        carry = tile(i, carry, True)
        for hh in range(hp):
            dq_ref[:, hh * HEAD:(hh + 1) * HEAD] = carry[hh][0] * scale

        @pl.when(i == nq - 1)
        def _():
            heads = pl.ds(pl.program_id(0) * hp, hp)
            copies = [pltpu.make_async_copy(dkt_ref, dkt_out.at[heads], out_sems.at[0]),
                      pltpu.make_async_copy(dvt_ref, dvt_out.at[heads], out_sems.at[1])]
            for cp in copies:
                cp.start()
            for cp in copies:
                cp.wait()

    sub = t // HEAD
    blk = lambda off: pl.BlockSpec((t, wg), lambda h, i: (i, off + h))
    full = lambda off: pl.BlockSpec((l, wg), lambda h, i: (0, off + h))
    acc_shape = jax.ShapeDtypeStruct((nh, l // HEAD, HEAD, HEAD), F32)
    acc_scratch = pltpu.VMEM((hp, l // HEAD, HEAD, HEAD), F32)
    outs = _call(
        body, name=name, grid=(ng, nq),
        in_specs=[blk(0), full(ng), full(2 * ng), blk(3 * wa // wg), blk(wa // wg)],
        out_specs=[blk(0), ANY, ANY], out_shape=[jax.ShapeDtypeStruct((l, wb), F32), acc_shape, acc_shape],
        scratch_shapes=[pltpu.VMEM((t, wg), BF16), pltpu.VMEM((hp, HEAD, t), BF16), pltpu.VMEM((hp, HEAD, t), BF16),
                        acc_scratch, acc_scratch, pltpu.SemaphoreType.DMA((2,))],
        args=(qkv, qkv, qkv, proj, dcat), sem=("parallel", "arbitrary"), ride=ride)
    return outs[0], outs[1], outs[2], outs[3:]


def _sgu_heads(v, g_ref, w_ref, bt_ref, nh):
    keep = _tri(lambda r, c: r >= c)
    out = []
    for h in range(nh):
        cols = slice(h * HEAD, (h + 1) * HEAD)
        nv, r = _rownorm(v[:, cols])
        wm = jnp.where(keep, w_ref[h], 0.0)
        s = _dot(_bf(wm), _bf(nv * g_ref[:, cols])) + bt_ref[:, h:h + 1]
        out.append((nv, r, wm, s))
    return out


def _sgu_fwd(proj, out_b, norm_g, sgu_w, sgu_bt, wa, wb, name):
    l, n = proj.shape
    nh = wa // HEAD

    def body(au_ref, av_ref, az_ref, bz_ref, ob_ref, g_ref, w_ref, bt_ref, cat_ref):
        u, v, sz = _gelu(au_ref[...]), _gelu(av_ref[...]), _silu(az_ref[...])
        for h, (_, _, _, s) in enumerate(_sgu_heads(v, g_ref, w_ref, bt_ref, nh)):
            cols = slice(h * HEAD, (h + 1) * HEAD)
            cat_ref[:, cols] = _bf(u[:, cols] * s * sz[:, cols])
        cat_ref[:, wa:] = _bf(ob_ref[...] * _silu(bz_ref[...]))

    a_blk = lambda j: pl.BlockSpec((HEAD, wa), lambda i: (i, j))
    return pl.pallas_call(
        body, name=name, grid=(l // HEAD,),
        in_specs=[a_blk(0), a_blk(1), a_blk(2), a_blk(3), pl.BlockSpec((HEAD, wb), lambda i: (i, 0)),
                  _vec_spec(wa), pl.BlockSpec((nh, HEAD, HEAD), lambda i: (0, 0, 0)),
                  pl.BlockSpec((HEAD, nh), lambda i: (0, 0))],
        out_specs=pl.BlockSpec((HEAD, wa + wb), lambda i: (i, 0)),
        out_shape=jax.ShapeDtypeStruct((l, wa + wb), BF16),
        compiler_params=_params(("parallel",)),
    )(proj, proj, proj, proj, out_b, norm_g, sgu_w, sgu_bt)


def _sgu_bwd(proj, out_b, dcat, dq, dk, dv, norm_g, sgu_w, sgu_bt, wa, wb, name):
    l = proj.shape[0]
    n = 3 * wa + 4 * wb
    nh = wa // HEAD

    def body(au_ref, av_ref, az_ref, bz_ref, ob_ref, dc_ref, dq_ref, dk_ref, dv_ref, g_ref, w_ref, wt_ref, bt_ref,
             dp_ref, dw_ref, dbt_ref, dg_ref):
        first = pl.program_id(0) == 0
        keep = _tri(lambda r, c: r >= c)
        au, av, az = au_ref[...], av_ref[...], az_ref[...]
        u, v, sz = _gelu(au), _gelu(av), _silu(az)
        dgelu_u, dgelu_v, dsilu_z = _gelu_grad(au), _gelu_grad(av), _silu_grad(az)
        heads = _sgu_heads(v, g_ref, w_ref, bt_ref, nh)
        cols = [slice(h * HEAD, (h + 1) * HEAD) for h in range(nh)]
        dss = []
        for h, (nv, r, wm, s) in enumerate(heads):
            dca, uh, szh = dc_ref[:, cols[h]].astype(F32), u[:, cols[h]], sz[:, cols[h]]
            dp_ref[:, cols[h]] = _bf(dca * s * szh * dgelu_u[:, cols[h]])
            dp_ref[:, 2 * wa + h * HEAD:2 * wa + (h + 1) * HEAD] = _bf(dca * uh * s * dsilu_z[:, cols[h]])
            dss.append(dca * uh * szh)
        dws = [_dot(_bf(ds), _bf(nv * g_ref[:, cs]), NT) for ds, cs, (nv, _, _, _) in zip(dss, cols, heads)]
        keep_t = _tri(lambda r, c: r <= c)
        dvhs = [_dot(_bf(jnp.where(keep_t, wt_ref[h], 0.0)), _bf(dss[h])) for h in range(nh)]
        dg_parts = []
        for h, (nv, r, wm, s) in enumerate(heads):
            _acc(dw_ref.at[h], first, jnp.where(keep, dws[h], 0.0))
            _acc(dbt_ref.at[:, h:h + 1], first, jnp.sum(dss[h], axis=1, keepdims=True))
            dg_parts.append(_colsum(dvhs[h] * nv))
            dnv = dvhs[h] * g_ref[:, cols[h]]
            dvv = r * (dnv - nv * jnp.mean(dnv * nv, axis=-1, keepdims=True))
            dp_ref[:, wa + h * HEAD:wa + (h + 1) * HEAD] = _bf(dvv * dgelu_v[:, cols[h]])
        _acc(dg_ref, first, jnp.concatenate(dg_parts, axis=1))
        base = 3 * wa
        dp_ref[:, base:base + wb] = _bf(dq_ref[...])
        for h in range(wb // HEAD):
            dp_ref[:, base + wb + h * HEAD:base + wb + (h + 1) * HEAD] = _bf(dk_ref[h, 0].T)
            dp_ref[:, base + 2 * wb + h * HEAD:base + 2 * wb + (h + 1) * HEAD] = _bf(dv_ref[h, 0].T)
        dp_ref[:, base + 3 * wb:] = _bf(dc_ref[:, wa:].astype(F32) * ob_ref[...] * _silu_grad(bz_ref[...]))

    a_blk = lambda j: pl.BlockSpec((HEAD, wa), lambda i: (i, j))
    b_blk = pl.BlockSpec((HEAD, wb), lambda i: (i, 0))
    t_blk = pl.BlockSpec((wb // HEAD, 1, HEAD, HEAD), lambda i: (0, i, 0, 0))
    w_spec = pl.BlockSpec((nh, HEAD, HEAD), lambda i: (0, 0, 0))
    bt_spec = pl.BlockSpec((HEAD, nh), lambda i: (0, 0))
    return pl.pallas_call(
        body, name=name, grid=(l // HEAD,),
        in_specs=[a_blk(0), a_blk(1), a_blk(2), a_blk(3), b_blk, pl.BlockSpec((HEAD, wa + wb), lambda i: (i, 0)),
                  b_blk, t_blk, t_blk, _vec_spec(wa), w_spec, w_spec, bt_spec],
        out_specs=[pl.BlockSpec((HEAD, n), lambda i: (i, 0)), w_spec, bt_spec, _vec_spec(wa)],
        out_shape=[jax.ShapeDtypeStruct((l, n), BF16), jax.ShapeDtypeStruct((nh, HEAD, HEAD), F32),
                   jax.ShapeDtypeStruct((HEAD, nh), F32), jax.ShapeDtypeStruct((1, wa), F32)],
        compiler_params=_params(("arbitrary",)),
    )(proj, proj, proj, proj, out_b, dcat, dq, dk, dv, norm_g, sgu_w, sgu_w.transpose(0, 2, 1), sgu_bt)


def _ssm_discretise(lr, li, ldt, br, bi):
    dt = jnp.exp(ldt)
    mag = jnp.exp(lr * dt)
    a_re = mag * jnp.cos(li * dt)
    a_im = mag * jnp.sin(li * dt)
    den = lr * lr + li * li
    nr = a_re - 1.0
    coef_re = (nr * lr + a_im * li) / den
    coef_im = (a_im * lr - nr * li) / den
    return a_re, a_im, coef_re * br - coef_im * bi, coef_re * bi + coef_im * br


def _ssm_prep(lr, li, ldt, br, bi, lr_row, li_row, ldt_row, name):
    s, c = br.shape

    def body(lr_ref, li_ref, ldt_ref, br_ref, bi_ref, lrr_ref, lir_ref, ldtr_ref, bbr_ref, bbi_ref, tr_ref, ti_ref):
        _, _, bbr, bbi = _ssm_discretise(lr_ref[...], li_ref[...], ldt_ref[...], br_ref[...], bi_ref[...])
        bbr_ref[...] = bbr
        bbi_ref[...] = bbi
        row = lax.broadcasted_iota(jnp.int32, (SCAN_ROWS, 1), 0)
        blk, r = jnp.right_shift(row, 3), jnp.bitwise_and(row, 7)
        kind, rev = jnp.bitwise_and(blk, 3), blk >= 4
        step = jnp.left_shift(1, kind)
        n = jnp.where(kind < 3, step, jnp.where(rev, 8 - r, r + 1)).astype(F32)
        keep = (kind == 3) | (rev & (r < 8 - step)) | (jnp.logical_not(rev) & (r >= step))
        dt = jnp.exp(ldtr_ref[...])
        mag = jnp.exp(n * (lrr_ref[...] * dt))
        ang = n * (lir_ref[...] * dt)
        tr_ref[...] = jnp.where(keep, mag * jnp.cos(ang), 0.0)
        ti_ref[...] = jnp.where(keep, jnp.where(rev, -1.0, 1.0) * mag * jnp.sin(ang), 0.0)

    col = jax.ShapeDtypeStruct((s, c), F32)
    row = jax.ShapeDtypeStruct((SCAN_ROWS, s), F32)
    return pl.pallas_call(body, name=name, out_shape=[col, col, row, row])(
        lr, li, ldt, br, bi, lr_row, li_row, ldt_row)


def _ssm_prep_bwd(lr, li, ldt, br, bi, da_re, da_im, dbb_re, dbb_im, p, name):
    s, c = br.shape

    def body(lr_ref, li_ref, ldt_ref, br_ref, bi_ref, dar_ref, dai_ref, dbr_ref, dbi_ref,
             dlr_ref, dli_ref, dldt_ref, dbre_ref, dbim_ref):
        args = (lr_ref[...], li_ref[...], ldt_ref[...], br_ref[...], bi_ref[...])
        _, vjp = jax.vjp(_ssm_discretise, *args)
        dlr, dli, dldt, dbr, dbi = vjp((dar_ref[...], dai_ref[...], dbr_ref[...], dbi_ref[...]))
        dlr_ref[...] = dlr
        dli_ref[...] = dli
        dbre_ref[...] = dbr
        dbim_ref[...] = dbi
        idx = lax.broadcasted_iota(jnp.int32, (s, s // p), 0)
        grp = lax.broadcasted_iota(jnp.int32, (s, s // p), 1)
        own = (idx >= grp * p) & (idx < (grp + 1) * p)
        dldt_ref[...] = _colsum(jnp.where(own, dldt, 0.0))

    col1 = jax.ShapeDtypeStruct((s, 1), F32)
    colc = jax.ShapeDtypeStruct((s, c), F32)
    return pl.pallas_call(
        body, name=name, out_shape=[col1, col1, jax.ShapeDtypeStruct((1, s // p), F32), colc, colc],
    )(lr, li, ldt, br, bi, da_re, da_im, dbb_re, dbb_im)


SCAN_ROWS = 64


def _scan_groups(xr, xi, tr_ref, ti_ref, cr, ci, reverse):
    ng = xr.shape[0] // 8
    base = SCAN_ROWS // 2 if reverse else 0
    pr, pi = tr_ref[base + 24:base + 32, :], ti_ref[base + 24:base + 32, :]
    edge = slice(0, 1) if reverse else slice(7, 8)
    out_r, out_i = [None] * ng, [None] * ng
    for g in (range(ng - 1, -1, -1) if reverse else range(ng)):
        sr, si = xr[8 * g:8 * g + 8, :], xi[8 * g:8 * g + 8, :]
        for k in range(3):
            ar, ai = tr_ref[base + 8 * k:base + 8 * k + 8, :], ti_ref[base + 8 * k:base + 8 * k + 8, :]
            shift = 8 - (1 << k) if reverse else 1 << k
            rr, ri = pltpu.roll(sr, shift, 0), pltpu.roll(si, shift, 0)
            sr, si = sr + ar * rr - ai * ri, si + ar * ri + ai * rr
        sr, si = sr + pr * cr - pi * ci, si + pr * ci + pi * cr
        cr, ci = sr[edge, :], si[edge, :]
        out_r[g], out_i[g] = sr, si
    return jnp.concatenate(out_r, axis=0), jnp.concatenate(out_i, axis=0), cr, ci


def _ssm_fwd(proj, bbd, ccd, pw_re, pw_im, d_skip, w, name):
    l = proj.shape[0]
    nb, cw, ns2 = bbd.shape
    ns = ns2 // 2
    nc = l // SSM_T

    def body(u_ref, bbd_ref, ccd_ref, pr_ref, pi_ref, d_ref, y_ref, hsr_ref, hsi_ref, h_ref, hr_s, hi_s):
        @pl.when(pl.program_id(1) == 0)
        def _():
            hr_s[...] = jnp.zeros_like(hr_s)
            hi_s[...] = jnp.zeros_like(hi_s)

        hsr_ref[...] = hr_s[...].reshape(hsr_ref.shape)
        hsi_ref[...] = hi_s[...].reshape(hsi_ref.shape)
        u = u_ref[...]
        bu = _dot(_bf(u), bbd_ref[0])
        hr, hi, cr, ci = _scan_groups(bu[:, :ns], bu[:, ns:], pr_ref, pi_ref, hr_s[...], hi_s[...], False)
        hr_s[...] = cr
        hi_s[...] = ci
        h_bf = _bf(jnp.concatenate([hr, hi], axis=1))
        h_ref[...] = h_bf
        y_ref[...] = _dot(h_bf, ccd_ref[0]) + d_ref[...] * u

    tab = pl.BlockSpec((SCAN_ROWS, ns), lambda b, k: (0, b))
    return pl.pallas_call(
        body, name=name, grid=(nb, nc),
        in_specs=[pl.BlockSpec((SSM_T, cw), lambda b, k: (k, b)),
                  pl.BlockSpec((1, cw, ns2), lambda b, k: (b, 0, 0)),
                  pl.BlockSpec((1, ns2, cw), lambda b, k: (b, 0, 0)),
                  tab, tab, pl.BlockSpec((1, cw), lambda b, k: (0, b))],
        out_specs=[pl.BlockSpec((SSM_T, cw), lambda b, k: (k, b)),
                   pl.BlockSpec((1, 1, ns), lambda b, k: (k, 0, b)), pl.BlockSpec((1, 1, ns), lambda b, k: (k, 0, b)),
                   pl.BlockSpec((SSM_T, ns2), lambda b, k: (k, b))],
        out_shape=[jax.ShapeDtypeStruct((l, w), F32), jax.ShapeDtypeStruct((nc, 1, nb * ns), F32),
                   jax.ShapeDtypeStruct((nc, 1, nb * ns), F32), jax.ShapeDtypeStruct((l, nb * ns2), BF16)],
        scratch_shapes=[pltpu.VMEM((1, ns), F32), pltpu.VMEM((1, ns), F32)],
        compiler_params=_params(("parallel", "arbitrary")),
    )(proj, bbd, ccd, pw_re, pw_im, d_skip)


def _ssm_bwd(proj, dy, hs_re, hs_im, h_all, bbd, ccd, pw_re, pw_im, d_skip, w, name, ride=None):
    l = proj.shape[0]
    nb, cw, ns2 = bbd.shape
    ns = ns2 // 2
    nc = l // SSM_T

    def body(u_ref, dy_ref, hsr_ref, hsi_ref, h_ref, bbd_ref, ccd_ref, pr_ref, pi_ref, d_ref,
             du_ref, dbbd_ref, dccd_ref, dar_ref, dai_ref, dd_ref, gr_s, gi_s):
        first = pl.program_id(1) == 0

        @pl.when(first)
        def _():
            gr_s[...] = jnp.zeros_like(gr_s)
            gi_s[...] = jnp.zeros_like(gi_s)

        u, dy = u_ref[...], dy_ref[...]
        dy_bf = _bf(dy)
        hr0, hi0 = hsr_ref[0], hsi_ref[0]
        h = h_ref[...].astype(F32)
        hr, hi = h[:, :ns], h[:, ns:]
        dh = _dot(dy_bf, ccd_ref[0], NT)
        gr, gi, gcr, gci = _scan_groups(dh[:, :ns], dh[:, ns:], pr_ref, pi_ref, gr_s[...], gi_s[...], True)
        gr_s[...] = gcr
        gi_s[...] = gci
        row0 = lax.broadcasted_iota(jnp.int32, hr.shape, 0) == 0
        pr_h = jnp.where(row0, hr0, pltpu.roll(hr, 1, 0))
        pi_h = jnp.where(row0, hi0, pltpu.roll(hi, 1, 0))
        _acc(dar_ref, first, _colsum(pr_h * gr + pi_h * gi))
        _acc(dai_ref, first, _colsum(pr_h * gi - pi_h * gr))
        g_bf = _bf(jnp.concatenate([gr, gi], axis=1))
        _acc(dbbd_ref.at[0], first, _dot(_bf(u.T), g_bf))
        _acc(dccd_ref.at[0], first, _dot(_bf(h.T), dy_bf))
        du_ref[...] = _bf(_dot(g_bf, bbd_ref[0], NT) + d_ref[...] * dy)
        _acc(dd_ref, first, _colsum(dy * u))

    rev = lambda b, k: (nc - 1 - k, b)
    outs = _call(
        body, name=name, grid=(nb, nc), ride=ride, sem=("parallel", "arbitrary"),
        args=(proj, dy, hs_re, hs_im, h_all, bbd, ccd, pw_re, pw_im, d_skip),
        in_specs=[pl.BlockSpec((SSM_T, cw), rev), pl.BlockSpec((SSM_T, cw), rev),
                  pl.BlockSpec((1, 1, ns), lambda b, k: (nc - 1 - k, 0, b)),
                  pl.BlockSpec((1, 1, ns), lambda b, k: (nc - 1 - k, 0, b)),
                  pl.BlockSpec((SSM_T, ns2), rev),
                  pl.BlockSpec((1, cw, ns2), lambda b, k: (b, 0, 0)),
                  pl.BlockSpec((1, ns2, cw), lambda b, k: (b, 0, 0)),
                  pl.BlockSpec((SCAN_ROWS, ns), lambda b, k: (0, b)), pl.BlockSpec((SCAN_ROWS, ns), lambda b, k: (0, b)),
                  pl.BlockSpec((1, cw), lambda b, k: (0, b))],
        out_specs=[pl.BlockSpec((SSM_T, cw), rev),
                   pl.BlockSpec((1, cw, ns2), lambda b, k: (b, 0, 0)),
                   pl.BlockSpec((1, ns2, cw), lambda b, k: (b, 0, 0)),
                   pl.BlockSpec((1, ns), lambda b, k: (0, b)), pl.BlockSpec((1, ns), lambda b, k: (0, b)),
                   pl.BlockSpec((1, cw), lambda b, k: (0, b))],
        out_shape=[jax.ShapeDtypeStruct((l, w), BF16), jax.ShapeDtypeStruct(bbd.shape, F32),
                   jax.ShapeDtypeStruct(ccd.shape, F32), jax.ShapeDtypeStruct((1, nb * ns), F32),
                   jax.ShapeDtypeStruct((1, nb * ns), F32), jax.ShapeDtypeStruct((1, w), F32)],
        scratch_shapes=[pltpu.VMEM((1, ns), F32), pltpu.VMEM((1, ns), F32)])
    return (*outs[:6], outs[6:])


def _block_diag_b(bb_re, bb_im, g, p, c):
    nb = g // SSM_GB
    keep = _same_group(SSM_GB * c, c, SSM_GB * p, p)

    def one(bb):
        t = bb.reshape(nb, SSM_GB, p, c).transpose(0, 1, 3, 2).reshape(nb, SSM_GB * c, p)
        return jnp.where(keep, jnp.tile(t, (1, 1, SSM_GB)), 0.0)

    return jnp.concatenate([one(bb_re), one(bb_im)], axis=2)


def _same_group(rows, per_row, cols, per_col):
    r = lax.broadcasted_iota(jnp.int32, (rows, cols), 0) // per_row
    q = lax.broadcasted_iota(jnp.int32, (rows, cols), 1) // per_col
    return r == q


def _block_diag_c(c_re, c_im, g, p, c):
    nb = g // SSM_GB
    keep = _same_group(SSM_GB * p, p, SSM_GB * c, c)

    def one(cc):
        t = cc.reshape(nb, SSM_GB, c, p).transpose(0, 1, 3, 2).reshape(nb, SSM_GB * p, c)
        return jnp.where(keep, jnp.tile(t, (1, 1, SSM_GB)), 0.0)

    return jnp.concatenate([one(c_re), one(-c_im)], axis=1)


def _diag_of_b(dbbd, g, p, c):
    nb = g // SSM_GB
    keep = _same_group(SSM_GB * c, c, SSM_GB * p, p)

    def one(blk):
        d = jnp.where(keep, blk, 0.0).reshape(nb, SSM_GB * c, SSM_GB, p).sum(axis=2)
        return d.reshape(nb, SSM_GB, c, p).transpose(0, 1, 3, 2).reshape(g * p, c)

    half = SSM_GB * p
    return one(dbbd[:, :, :half]), one(dbbd[:, :, half:])


def _diag_of_c(dccd, g, p, c):
    nb = g // SSM_GB
    keep = _same_group(SSM_GB * p, p, SSM_GB * c, c)

    def one(blk):
        d = jnp.where(keep, blk, 0.0).reshape(nb, SSM_GB * p, SSM_GB, c).sum(axis=2)
        return d.reshape(nb, SSM_GB, p, c).transpose(0, 1, 3, 2).reshape(g, c, p)

    half = SSM_GB * p
    return one(dccd[:, :half]), -one(dccd[:, half:])


def _glu_fwd(y, proj, w_glu, b_glu, name):
    l, w = y.shape
    tm = _tile(l, 256)

    def body(y_ref, z_ref, w_ref, b_ref, o_ref):
        g = _gelu(y_ref[...])
        t = _dot(_bf(g), w_ref[...]) + b_ref[...]
        o_ref[...] = _bf(g * _sigmoid(t) * _silu(z_ref[...]))

    return pl.pallas_call(
        body, name=name, grid=(l // tm,),
        in_specs=[_row_spec(tm, w), pl.BlockSpec((tm, w), lambda i: (i, 1)),
                  pl.BlockSpec((w, w), lambda i: (0, 0)), _vec_spec(w)],
        out_specs=_row_spec(tm, w), out_shape=jax.ShapeDtypeStruct((l, w), BF16),
        compiler_params=_params(("parallel",)),
    )(y, proj, w_glu, b_glu)


def _glu_bwd(do, y, proj, w_glu, b_glu, name):
    l, w = y.shape
    tm = _tile(l, 512)
    nsteps = l // tm

    def body(do_ref, y_ref, z_ref, w_ref, b_ref, dy_ref, dz_ref, dw_ref, db_ref, dw_acc):
        i = pl.program_id(0)
        first = i == 0
        yv, z, do = y_ref[...], z_ref[...], do_ref[...].astype(F32)
        g = _gelu(yv)
        g_bf = _bf(g)
        sg = _sigmoid(_dot(g_bf, w_ref[...]) + b_ref[...])
        dyy = do * _silu(z)
        dz_ref[...] = _bf(do * g * sg * _silu_grad(z))
        dt = dyy * g * sg * (1.0 - sg)
        dt_bf = _bf(dt)
        dg = dyy * sg + _dot(dt_bf, w_ref[...], NT)
        dy_ref[...] = dg * _gelu_grad(yv)
        _acc(dw_acc, first, _dot(_bf(g.T), dt_bf))
        _acc(db_ref, first, _colsum(dt))

        @pl.when(i == nsteps - 1)
        def _():
            dw_ref[...] = _bf(dw_acc[...])

    return pl.pallas_call(
        body, name=name, grid=(nsteps,),
        in_specs=[_row_spec(tm, w), _row_spec(tm, w), pl.BlockSpec((tm, w), lambda i: (i, 1)),
                  pl.BlockSpec((w, w), lambda i: (0, 0)), _vec_spec(w)],
        out_specs=[_row_spec(tm, w), _row_spec(tm, w), pl.BlockSpec((w, w), lambda i: (0, 0)), _vec_spec(w)],
        out_shape=[jax.ShapeDtypeStruct((l, w), F32), jax.ShapeDtypeStruct((l, w), BF16),
                   jax.ShapeDtypeStruct((w, w), BF16), jax.ShapeDtypeStruct((1, w), F32)],
        scratch_shapes=[pltpu.VMEM((w, w), F32)],
        compiler_params=_params(("arbitrary",)),
    )(do, y, proj, w_glu, b_glu)


MOD_ROWS = 128


def _mod_fwd(cond_pad, w_mod, b_shard, name):
    nl, d, ncol = w_mod.shape
    tn = _tile(ncol, 512)

    def body(c_ref, w_ref, b_ref, o_ref):
        o_ref[0] = _dot(_bf(c_ref[...]), _bf(w_ref[0])) + b_ref[0]

    return pl.pallas_call(
        body, name=name, grid=(nl, ncol // tn),
        in_specs=[pl.BlockSpec((MOD_ROWS, d), lambda a, j: (0, 0)),
                  pl.BlockSpec((1, d, tn), lambda a, j: (a, 0, j)),
                  pl.BlockSpec((1, 1, tn), lambda a, j: (a, 0, j))],
        out_specs=pl.BlockSpec((1, MOD_ROWS, tn), lambda a, j: (a, 0, j)),
        out_shape=jax.ShapeDtypeStruct((nl, MOD_ROWS, ncol), F32),
        compiler_params=_params(("parallel", "parallel")),
    )(cond_pad, w_mod, b_shard)


def _mod_bwd(cond_pad_t, dmod_pad, name):
    nl, _, ncol = dmod_pad.shape
    d = cond_pad_t.shape[0]
    tn = _tile(ncol, 512)

    def body(c_ref, dm_ref, o_ref):
        o_ref[0] = _dot(_bf(c_ref[...]), _bf(dm_ref[0]))

    return pl.pallas_call(
        body, name=name, grid=(nl, ncol // tn),
        in_specs=[pl.BlockSpec((d, MOD_ROWS), lambda a, j: (0, 0)),
                  pl.BlockSpec((1, MOD_ROWS, tn), lambda a, j: (a, 0, j))],
        out_specs=pl.BlockSpec((1, d, tn), lambda a, j: (a, 0, j)),
        out_shape=jax.ShapeDtypeStruct((nl, d, ncol), F32),
        compiler_params=_params(("parallel", "parallel")),
    )(cond_pad_t, dmod_pad)


def _silu_rows(c2d, name):
    def body(c_ref, o_ref):
        o_ref[...] = _silu(c_ref[...])

    return pl.pallas_call(body, name=name, out_shape=jax.ShapeDtypeStruct(c2d.shape, F32))(c2d)


def _sum_leading(x, name):
    n, r, c = x.shape
    tr = _tile(r, max(16, (1 << 20) // (4 * c)), 16 if r % 16 == 0 else 8)

    def body(x_ref, o_ref):
        acc = x_ref[0].astype(F32)
        for k in range(1, n):
            acc = acc + x_ref[k].astype(F32)
        o_ref[...] = acc

    return pl.pallas_call(
        body, name=name, grid=(r // tr,),
        in_specs=[pl.BlockSpec((n, tr, c), lambda i: (0, i, 0))], out_specs=pl.BlockSpec((tr, c), lambda i: (i, 0)),
        out_shape=jax.ShapeDtypeStruct((r, c), F32), compiler_params=_params(("parallel",)),
    )(x)


def _adamw(w, gs, m, v, name):
    r, c = w.shape
    tr = _tile(r, max(8, (3 << 19) // (4 * c)), 8)
    ng = len(gs)

    def body(*refs):
        w_ref, g_refs, m_ref, v_ref = refs[0], refs[1:1 + ng], refs[1 + ng], refs[2 + ng]
        g_ref, d_ref, nm_ref, nv_ref = refs[3 + ng:]
        g = g_refs[0][...]
        for extra in g_refs[1:]:
            g = g + extra[...]
        g_ref[...] = g
        d_ref[...], nm_ref[...], nv_ref[...] = _adamw_math(w_ref[...], g, m_ref[...], v_ref[...])

    spec = pl.BlockSpec((tr, c), lambda i: (i, 0))
    shp = jax.ShapeDtypeStruct((r, c), F32)
    return pl.pallas_call(
        body, name=name, grid=(r // tr,), in_specs=[spec] * (3 + ng), out_specs=[spec] * 4,
        out_shape=[shp] * 4, compiler_params=_params(("parallel",)),
    )(w, *gs, m, v)


def _adamw_math(w, g, m, v):
    nm = ADAM_B1 * m + (1.0 - ADAM_B1) * g
    nv = ADAM_B2 * v + (1.0 - ADAM_B2) * (g * g)
    m_hat = nm / (1.0 - ADAM_B1 ** ADAM_STEP)
    v_hat = nv / (1.0 - ADAM_B2 ** ADAM_STEP)
    return -ADAM_LR * (m_hat / (jnp.sqrt(v_hat) + ADAM_EPS) + ADAM_WD * w), nm, nv


def _adamw_many(ws, gs, ms, vs, name):
    n = len(ws)

    def body(*refs):
        w_refs, g_refs, m_refs, v_refs = (refs[k * n:(k + 1) * n] for k in range(4))
        outs = refs[4 * n:]
        for i in range(n):
            outs[3 * i][...], outs[3 * i + 1][...], outs[3 * i + 2][...] = _adamw_math(
                w_refs[i][...], g_refs[i][...], m_refs[i][...], v_refs[i][...])

    out_shape = [jax.ShapeDtypeStruct(w.shape, F32) for w in ws for _ in range(3)]
    outs = pl.pallas_call(body, name=name, out_shape=out_shape, compiler_params=_params())(*ws, *gs, *ms, *vs)
    return [tuple(outs[3 * i:3 * i + 3]) for i in range(n)]


ANY = pl.BlockSpec(memory_space=pl.ANY)


def _flip(v, bit):
    return 1 - v if bit else v


def _allgather8_ops(x_ref, o_ref, send_sems, recv_sems, local_sem):
    mx, my, mc = lax.axis_index("x"), lax.axis_index("y"), lax.axis_index("c")
    me = 4 * mx + 2 * my + mc

    def mine():
        return pltpu.make_async_copy(x_ref, o_ref.at[me], local_sem)

    def copy(j, outgoing):
        peer = (_flip(mx, j & 4), _flip(my, j & 2), _flip(mc, j & 1))
        slot = me if outgoing else 4 * peer[0] + 2 * peer[1] + peer[2]
        return pltpu.make_async_remote_copy(
            src_ref=x_ref, dst_ref=o_ref.at[slot], send_sem=send_sems.at[j - 1], recv_sem=recv_sems.at[j - 1],
            device_id=peer, device_id_type=MESH)

    def start():
        mine().start()
        for j in range(1, 8):
            copy(j, True).start()

    def wait():
        for j in range(1, 8):
            copy(j, False).wait()
        mine().wait()

    return start, wait


def _ride_all8(x):
    return dict(xs=[x], shapes=[jax.ShapeDtypeStruct((8,) + x.shape, x.dtype)],
                sems=[pltpu.SemaphoreType.DMA((7,)), pltpu.SemaphoreType.DMA((7,)), pltpu.SemaphoreType.DMA],
                ops=lambda x_refs, o_refs, sems: _allgather8_ops(x_refs[0], o_refs[0], *sems))


def _ride_chip(xs, gather):
    return dict(xs=list(xs), shapes=_chip_exchange_shapes(xs, gather), sems=_chip_exchange_sems(len(xs)),
                ops=lambda x_refs, o_refs, sems: _chip_exchange_ops(x_refs, o_refs, *sems, gather))


def _allgather8(x, name):
    def body(x_ref, o_ref, *sems):
        start, wait = _allgather8_ops(x_ref, o_ref, *sems)
        start()
        wait()

    ride = _ride_all8(x)
    return pl.pallas_call(body, name=name, in_specs=[ANY], out_specs=ANY, out_shape=ride["shapes"][0],
                          scratch_shapes=ride["sems"])(x)


def _gather_halves_ops(x_ref, o_ref, ici_send, ici_recv, d2d_send, d2d_recv, local_sem):
    half = x_ref.shape[0] // 2
    mx, my, mc = lax.axis_index("x"), lax.axis_index("y"), lax.axis_index("c")
    k0 = 2 * mx + my
    mine = pl.ds(pl.multiple_of(mc * half, 16), half)
    theirs = pl.ds(pl.multiple_of((1 - mc) * half, 16), half)

    def local():
        return pltpu.make_async_copy(x_ref, o_ref.at[k0], local_sem)

    def chips(j):
        px, py = _flip(mx, j & 2), _flip(my, j & 1)
        return px, py, 2 * px + py

    def over_ici(j, outgoing):
        px, py, kp = chips(j)
        dst = o_ref.at[k0, mine] if outgoing else o_ref.at[kp, mine]
        return pltpu.make_async_remote_copy(
            src_ref=x_ref.at[mine], dst_ref=dst, send_sem=ici_send.at[j - 1], recv_sem=ici_recv.at[j - 1],
            device_id=(px, py, mc), device_id_type=MESH)

    def over_d2d(j, outgoing):
        _, _, kp = chips(j)
        rows = mine if outgoing else theirs
        return pltpu.make_async_remote_copy(
            src_ref=o_ref.at[kp, rows], dst_ref=o_ref.at[kp, rows], send_sem=d2d_send.at[j - 1],
            recv_sem=d2d_recv.at[j - 1], device_id=(mx, my, 1 - mc), device_id_type=MESH)

    def start():
        local().start()
        for j in range(1, 4):
            over_ici(j, True).start()

    def wait():
        for j in range(1, 4):
            over_ici(j, False).wait_recv()
            over_d2d(j, True).start()
        for j in range(1, 4):
            over_ici(j, True).wait_send()
            over_d2d(j, True).wait_send()
            over_d2d(j, False).wait_recv()
        local().wait()

    return start, wait


def _ride_halves(x):
    dma3 = pltpu.SemaphoreType.DMA((3,))
    return dict(xs=[x], shapes=[jax.ShapeDtypeStruct((4,) + x.shape, x.dtype)],
                sems=[dma3, dma3, dma3, dma3, pltpu.SemaphoreType.DMA],
                ops=lambda x_refs, o_refs, sems: _gather_halves_ops(x_refs[0], o_refs[0], *sems))


def _chip_exchange(xs, gather, name):
    n = len(xs)

    def body(*refs):
        start, wait = _chip_exchange_ops(refs[:n], refs[n:2 * n], *refs[2 * n:], gather)
        start()
        wait()

    return pl.pallas_call(
        body, name=name, in_specs=[ANY] * n, out_specs=[ANY] * n, out_shape=_chip_exchange_shapes(xs, gather),
        scratch_shapes=_chip_exchange_sems(n),
    )(*xs)


def _chip_exchange_shapes(xs, gather):
    return [jax.ShapeDtypeStruct(((4,) + x.shape) if gather else x.shape, x.dtype) for x in xs]


def _chip_exchange_sems(n):
    return [pltpu.SemaphoreType.DMA((3 * n,)), pltpu.SemaphoreType.DMA((3 * n,)), pltpu.SemaphoreType.DMA((n,))]


def _chip_exchange_ops(x_refs, o_refs, send_sems, recv_sems, local_sems, gather):
    n = len(x_refs)
    mx, my, mc = lax.axis_index("x"), lax.axis_index("y"), lax.axis_index("c")
    k0 = 2 * mx + my

    def local(a):
        src = x_refs[a] if gather else x_refs[a].at[k0]
        return pltpu.make_async_copy(src, o_refs[a].at[k0], local_sems.at[a])

    def copy(a, j, outgoing):
        px, py = _flip(mx, j & 2), _flip(my, j & 1)
        kp = 2 * px + py
        if outgoing:
            src = x_refs[a] if gather else x_refs[a].at[kp]
            dst = o_refs[a].at[k0]
        else:
            src = x_refs[a] if gather else x_refs[a].at[k0]
            dst = o_refs[a].at[kp]
        s = a * 3 + j - 1
        return pltpu.make_async_remote_copy(
            src_ref=src, dst_ref=dst, send_sem=send_sems.at[s], recv_sem=recv_sems.at[s],
            device_id=(px, py, mc), device_id_type=MESH)

    def start():
        for a in range(n):
            local(a).start()
            for j in range(1, 4):
                copy(a, j, True).start()

    def wait():
        for a in range(n):
            for j in range(1, 4):
                copy(a, j, False).wait()
            local(a).wait()

    return start, wait


def _call(body, *, name, grid, in_specs, out_specs, out_shape, args, scratch_shapes=(), sem=None, ride=None):
    if not ride:
        return pl.pallas_call(
            body, name=name, grid=grid, in_specs=list(in_specs), out_specs=list(out_specs), out_shape=list(out_shape),
            scratch_shapes=list(scratch_shapes), compiler_params=_params(sem))(*args)
    xs = [x for r in ride for x in r["xs"]]
    shapes = [s for r in ride for s in r["shapes"]]
    sems = [s for r in ride for s in r["sems"]]
    n_in, n_out, n_scr, nx = len(in_specs), len(out_specs), len(scratch_shapes), len(xs)

    def wrapped(*refs):
        ins, x_refs = refs[:n_in], refs[n_in:n_in + nx]
        outs = refs[n_in + nx:n_in + nx + n_out]
        lands = refs[n_in + nx + n_out:n_in + 2 * nx + n_out]
        rest = refs[n_in + 2 * nx + n_out:]
        scr, sem_refs = rest[:n_scr], rest[n_scr:]
        ops, xo, so = [], 0, 0
        for r in ride:
            nr, ns = len(r["xs"]), len(r["sems"])
            ops.append(r["ops"](x_refs[xo:xo + nr], lands[xo:xo + nr], sem_refs[so:so + ns]))
            xo, so = xo + nr, so + ns
        ids = [pl.program_id(a) for a in range(len(grid))]
        first = functools.reduce(jnp.logical_and, [i == 0 for i in ids])
        last = functools.reduce(jnp.logical_and, [i == g - 1 for i, g in zip(ids, grid)])

        @pl.when(first)
        def _():
            for start, _ in ops:
                start()

        body(*ins, *outs, *scr)

        @pl.when(last)
        def _():
            for _, wait in ops:
                wait()

    return pl.pallas_call(
        wrapped, name=name, grid=grid, in_specs=list(in_specs) + [ANY] * nx, out_specs=list(out_specs) + [ANY] * nx,
        out_shape=list(out_shape) + shapes, scratch_shapes=list(scratch_shapes) + sems,
        compiler_params=_params(("arbitrary",) * len(grid)))(*args, *xs)


def _sibling_exchange(xs, name):
    n = len(xs)

    def body(*refs):
        x_refs, o_refs = refs[:n], refs[n:2 * n]
        send_sems, recv_sems = refs[2 * n:]
        sib = (lax.axis_index("x"), lax.axis_index("y"), 1 - lax.axis_index("c"))
        copies = [pltpu.make_async_remote_copy(
            src_ref=x_refs[a], dst_ref=o_refs[a], send_sem=send_sems.at[a], recv_sem=recv_sems.at[a],
            device_id=sib, device_id_type=MESH) for a in range(n)]
        for cp in copies:
            cp.start()
        for cp in copies:
            cp.wait()

    return pl.pallas_call(
        body, name=name, in_specs=[ANY] * n, out_specs=[ANY] * n,
        out_shape=[jax.ShapeDtypeStruct(x.shape, x.dtype) for x in xs],
        scratch_shapes=[pltpu.SemaphoreType.DMA((n,)), pltpu.SemaphoreType.DMA((n,))],
    )(*xs)


PACK = 1024
PACK_ROWS = 512


def _pack(parts):
    flat = []
    for p in parts:
        v = p.reshape(-1).astype(F32)
        flat.append(jnp.pad(v, (0, (-v.shape[0]) % PACK)))
    total = sum(v.shape[0] for v in flat)
    flat.append(jnp.zeros(((-total) % (PACK_ROWS * 128),), F32))
    return jnp.concatenate(flat).reshape(-1, 128)


def _shard_columns(shards, lo, hi):
    width = shards.shape[2]
    out = []
    for k in range(shards.shape[0]):
        a, b = max(lo, k * width), min(hi, (k + 1) * width)
        if a < b:
            out.append(shards[k, :, a - k * width:b - k * width])
    return out


def _unpack_rows(gathered, shapes):
    flat = gathered.reshape(gathered.shape[0], -1)
    out, off = [], 0
    for shp in shapes:
        n = math.prod(shp)
        out.append(flat[:, off:off + n].reshape((flat.shape[0],) + tuple(shp)))
        off += n + (-n) % PACK
    return out


def _unpack(packed, shapes):
    flat = packed.reshape(-1)
    out, off = [], 0
    for shp in shapes:
        n = math.prod(shp)
        out.append(flat[off:off + n].reshape(shp))
        off += n + (-n) % PACK
    return out


def kernel(x, c, ln_pre_g, ln_post_g, w_mod, b_mod, w_in_ab, w_out_ab, sgu_norm_g, sgu_w, sgu_b, w_in_ssm, w_out_ssm, lam_re, lam_im, b_re, b_im, c_re, c_im, d_skip, log_dt, w_glu, b_glu, loss_target, m_ln_pre_g, m_ln_post_g, m_w_mod, m_b_mod, m_w_in_ab, m_w_out_ab, m_sgu_norm_g, m_sgu_w, m_sgu_b, m_w_in_ssm, m_w_out_ssm, m_lam_re, m_lam_im, m_b_re, m_b_im, m_c_re, m_c_im, m_d_skip, m_log_dt, m_w_glu, m_b_glu, v_ln_pre_g, v_ln_post_g, v_w_mod, v_b_mod, v_w_in_ab, v_w_out_ab, v_sgu_norm_g, v_sgu_w, v_sgu_b, v_w_in_ssm, v_w_out_ssm, v_lam_re, v_lam_im, v_b_re, v_b_im, v_c_re, v_c_im, v_d_skip, v_log_dt, v_w_glu, v_b_glu):
    given = dict(locals())
    mx, my, mc = lax.axis_index("x"), lax.axis_index("y"), lax.axis_index("c")
    me = 4 * mx + 2 * my + mc
    chip = 2 * mx + my

    _, l, d = x.shape
    x2, tgt = x[0], loss_target[0]
    n_in = w_in_ab.shape[2] * 4
    wa = wb = n_in // 7
    w = w_out_ssm.shape[1]
    g, p, cch = b_re.shape[1:]
    nmod = w_mod.shape[2]

    later_shards = [_bf(w_out_ab[0]), _bf(w_in_ssm[0]), _bf(w_out_ssm[0]), _bf(w_glu[0]), d_skip, b_glu]

    cond = _silu_rows(c.reshape(d // 128, 128), "cond_silu")
    cond_all = _allgather8(cond, "gather_cond").reshape(8, d)
    b_shard = lax.dynamic_slice(b_mod, (0, chip * nmod), (2, nmod)).reshape(2, 1, nmod)
    cond_pad = jnp.pad(cond_all, ((0, MOD_ROWS - 8), (0, 0)))
    modp = _mod_fwd(cond_pad, w_mod, b_shard, "mod_fwd")[:, :8]
    modp_all = _allgather8(modp.reshape(16, nmod), "gather_mod").reshape(4, 2, 2, 8, nmod)
    mine = lax.dynamic_index_in_dim(lax.dynamic_index_in_dim(modp_all, mc, 1, False), me, 2, False)
    mod = mine.transpose(1, 0, 2).reshape(2, 3 * d)
    shift = [mod[a:a + 1, :d] for a in range(2)]
    scale = [mod[a:a + 1, d:2 * d] for a in range(2)]
    gate = [mod[a:a + 1, 2 * d:] for a in range(2)]
    pre_g = [ln_pre_g[a:a + 1] for a in range(2)]
    post_g = [ln_post_g[a:a + 1] for a in range(2)]

    sgu_w0, sgu_bt = sgu_w[0], sgu_b[0].T
    h0, (gw_in_ab,) = _pre_fwd(x2, pre_g[0], scale[0], shift[0], "pre0_fwd", ride=[_ride_halves(_bf(w_in_ab[0]))])
    w_gates = jnp.concatenate(_shard_columns(gw_in_ab, 0, 3 * wa) + _shard_columns(gw_in_ab, 3 * wa + 3 * wb, n_in),
                              axis=1)
    w_qkv = jnp.concatenate(_shard_columns(gw_in_ab, 3 * wa, 3 * wa + 3 * wb), axis=1)
    proj0 = _matmul(h0, w_gates, "nn", F32, "proj0", tm=1024)
    qkv = _matmul(h0, w_qkv, "nn", BF16, "proj0_qkv", tm=1024)
    out_b, (gw_out_ab, gw_in_ssm, gw_out_ssm, gw_glu, g_dskip, g_bglu) = _attn_fwd(
        qkv, wb, "attn_fwd", hp=8, ride=[_ride_chip(later_shards, True)])
    wout_ab = gw_out_ab.reshape(wa + wb, d)
    win_ssm = gw_in_ssm.reshape(d, 2 * w)
    wout_ssm = jnp.concatenate([gw_out_ssm[k] for k in range(4)], axis=1)
    wglu = gw_glu.reshape(w, w)
    dskip_full = g_dskip.reshape(1, w)
    bglu_full = g_bglu.reshape(1, w)
    cat =_sgu_fwd(proj0, out_b, sgu_norm_g, sgu_w0, sgu_bt, wa, wb, "sgu_fwd")
    y0 = _matmul(cat, wout_ab, "nn", BF16, "out0", tm=1024)
    x1, h1 = _post_pre_fwd(x2, y0, gate[0], post_g[0], pre_g[1], scale[1], shift[1], "post0_pre1_fwd")

    s = g * p
    lr_c, li_c = lam_re.reshape(s, 1), lam_im.reshape(s, 1)
    ldt_c = jnp.repeat(log_dt.reshape(g), p).reshape(s, 1)
    br_c, bi_c = b_re.reshape(s, cch), b_im.reshape(s, cch)
    bb_re, bb_im, pw_re, pw_im = _ssm_prep(lr_c, li_c, ldt_c, br_c, bi_c, lr_c.reshape(1, s), li_c.reshape(1, s),
                                           ldt_c.reshape(1, s), "ssm_prep")
    bbd = _bf(_block_diag_b(bb_re, bb_im, g, p, cch))
    ccd = _bf(_block_diag_c(c_re[0], c_im[0], g, p, cch))
    proj1 = _matmul(h1, win_ssm, "nn", F32, "proj1", tm=1024)
    y_ssm, hs_re, hs_im, h_all = _ssm_fwd(proj1, bbd, ccd, pw_re, pw_im, dskip_full, w, "ssm_fwd")
    o1 = _glu_fwd(y_ssm, proj1, wglu, bglu_full, "glu_fwd")
    y1 = _matmul(o1, wout_ssm, "nn", BF16, "out1", tm=1024)
    loss_vec, dy1, dx2, dgate1, dpost1 = _post_loss(x1, y1, gate[1], post_g[1], tgt, "post1_loss")

    do1 = _matmul(dy1, wout_ssm, "nt", BF16, "out1_dx", tm=1024)
    gr_wout_ssm = _matmul(o1, dy1, "tn", BF16, "out1_dw", tm=1024, tk=1024, n_split=4)
    dy_ssm, dz1, gr_wglu, gr_bglu = _glu_bwd(do1, y_ssm, proj1, wglu, bglu_full, "glu_bwd")
    du1, dbbd, dccd, da_re, da_im, gr_dskip, (ld_wout_ssm, ld_wglu) = _ssm_bwd(
        proj1, dy_ssm, hs_re, hs_im, h_all, bbd, ccd, pw_re, pw_im, dskip_full, w, "ssm_bwd",
        ride=[_ride_chip([gr_wout_ssm, gr_wglu.reshape(4, w // 4, w)], False)])
    dproj1 = jnp.concatenate([du1, dz1], axis=1)
    dh1 = _matmul(dproj1, win_ssm, "nt", BF16, "proj1_dx", tm=1024)
    gr_win_ssm = _matmul(h1, dproj1, "tn", BF16, "proj1_dw", tm=1024, tn=1024, tk=1024)
    dx1, dscale1, dshift1, dpre1, dy0, dgate0, dpost0 = _pre_bwd(
        dh1, dx2, x1, pre_g[1], scale[1], "pre1_post0_bwd", post=(y0, gate[0], post_g[0]))

    dcat = _matmul(dy0, wout_ab, "nt", BF16, "out0_dx", tm=1024)
    gr_wout_ab = _matmul(cat, dy0, "tn", BF16, "out0_dw", tm=1024, tn=1024, tk=1024)
    dbb_re, dbb_im = _diag_of_b(dbbd, g, p, cch)
    dc_re, dc_im = _diag_of_c(dccd, g, p, cch)
    part_a = [loss_vec[:, :1], dpre1, dpost0, dpost1, dgate0, dshift1, dscale1, dgate1, da_re, da_im,
              dbb_re, dbb_im, dc_re, dc_im, gr_dskip, gr_bglu]
    shapes_a = [a.shape for a in part_a]
    dq, dk, dv, (ld_win_ssm, ld_wout_ab, gath_a) = _attn_bwd(
        qkv, proj0, dcat, wa, wb, "attn_bwd", hp=4,
        ride=[_ride_chip([gr_win_ssm.reshape(4, d // 4, 2 * w), gr_wout_ab.reshape(4, (wa + wb) // 4, d)], False),
              _ride_all8(_pack(part_a))])
    dproj0, gr_sgu_w, gr_sgu_bt, gr_sgu_g = _sgu_bwd(proj0, out_b, dcat, dq, dk, dv, sgu_norm_g, sgu_w0, sgu_bt,
                                                     wa, wb, "sgu_bwd")
    part_b = [gr_sgu_g, gr_sgu_w, gr_sgu_bt.T]
    shapes_b = [a.shape for a in part_b]
    gr_win_ab_lo, (gath_b,) = _matmul(h0, dproj0, "tn", BF16, "proj0_dw_lo", tm=1024, tk=1024, tn=896, n_split=4,
                                      m_part=(0, 1, 2), ride=[_ride_all8(_pack(part_b))])
    gr_win_ab_hi, (ld_win_ab_lo,) = _matmul(
        h0, dproj0, "tn", BF16, "proj0_dw_hi", tm=1024, tk=1024, tn=896, n_split=4, m_part=(1, 1, 2),
        ride=[_ride_chip([gr_win_ab_lo], False)])
    dh0, (ld_win_ab_hi,) = _matmul(dproj0, gw_in_ab, "nt", BF16, "proj0_dx", tm=1024, tn=1024,
                                   ride=[_ride_chip([gr_win_ab_hi], False)])
    grad_x, dscale0, dshift0, dpre0 = _pre_bwd(dh0, dx1, x2, pre_g[0], scale[0], "pre0_bwd")
    part_c = [dpre0, dshift0, dscale0]
    shapes_c = [a.shape for a in part_c]
    gath_c = _allgather8(_pack(part_c), "gather_small_tail")

    landed = [ld_wout_ab, ld_win_ssm, ld_wout_ssm, ld_wglu]
    big_names = ["w_in_ab", "w_out_ab", "w_in_ssm", "w_out_ssm", "w_glu"]
    sums = [jnp.concatenate([_sum_leading(ld_win_ab_lo, "sum_w_in_ab_lo"), _sum_leading(ld_win_ab_hi, "sum_w_in_ab_hi")],
                            axis=0)]
    sums += [_sum_leading(a, "sum_" + nm) for a, nm in zip(landed, big_names[1:])]
    sib = _sibling_exchange(sums, "sibling_grads")
    results = {}
    for nm, s_mine, s_sib in zip(big_names, sums, sib):
        shp = given[nm].shape
        two_d = lambda a: a.reshape(-1, shp[-1])
        outs = _adamw(two_d(given[nm]), [s_mine, s_sib], two_d(given["m_" + nm]), two_d(given["v_" + nm]),
                      "adamw_" + nm)
        results[nm] = [o.reshape(shp) for o in outs]

    (loss_s, g_pre1, g_post0, g_post1, g_gate0, g_shift1, g_scale1, g_gate1, s_da_re, s_da_im, s_dbb_re, s_dbb_im,
     g_c_re, g_c_im, g_dskip_full, g_bglu_full) = _unpack(_sum_leading(gath_a, "sum_small_a"), shapes_a)
    g_sgu_g, g_sgu_w, g_sgu_b = _unpack(_sum_leading(gath_b, "sum_small_b"), shapes_b)
    g_pre0, g_shift0, g_scale0 = _unpack(_sum_leading(gath_c, "sum_small_c"), shapes_c)
    loss = loss_s.reshape(())
    g_pre = jnp.concatenate([g_pre0, g_pre1], axis=0)
    g_post = jnp.concatenate([g_post0, g_post1], axis=0)
    g_bmod = jnp.concatenate([jnp.concatenate([g_shift0, g_scale0, g_gate0], axis=1),
                              jnp.concatenate([g_shift1, g_scale1, g_gate1], axis=1)], axis=0)

    g_lr, g_li, g_ldt, g_br, g_bi = _ssm_prep_bwd(lr_c, li_c, ldt_c, br_c, bi_c, s_da_re.reshape(s, 1),
                                                  s_da_im.reshape(s, 1), s_dbb_re, s_dbb_im, p, "ssm_prep_bwd")
    small = {
        "ln_pre_g": g_pre, "ln_post_g": g_post, "b_mod": g_bmod, "sgu_norm_g": g_sgu_g,
        "sgu_w": g_sgu_w.reshape(sgu_w.shape), "sgu_b": g_sgu_b.reshape(sgu_b.shape),
        "lam_re": g_lr.reshape(lam_re.shape), "lam_im": g_li.reshape(lam_im.shape),
        "b_re": g_br.reshape(b_re.shape), "b_im": g_bi.reshape(b_im.shape),
        "c_re": g_c_re.reshape(c_re.shape), "c_im": g_c_im.reshape(c_im.shape),
        "d_skip": lax.dynamic_slice(g_dskip_full, (0, chip * (w // 4)), (1, w // 4)),
        "log_dt": g_ldt.reshape(log_dt.shape),
        "b_glu": lax.dynamic_slice(g_bglu_full, (0, chip * (w // 4)), (1, w // 4)),
    }
    flat2 = lambda a: a.reshape(-1, a.shape[-1])
    wide = ("b_re", "b_im")
    for tag, group in (("adamw_small", [nm for nm in small if nm not in wide]), ("adamw_small_b", list(wide))):
        outs = _adamw_many([flat2(given[nm]) for nm in group], [flat2(small[nm]) for nm in group],
                           [flat2(given["m_" + nm]) for nm in group], [flat2(given["v_" + nm]) for nm in group], tag)
        for nm, trio in zip(group, outs):
            results[nm] = [small[nm]] + [o.reshape(given[nm].shape) for o in trio]

    rows_a = _unpack_rows(gath_a, shapes_a)
    rows_c = _unpack_rows(gath_c, shapes_c)
    dmod_rows = jnp.concatenate([rows_c[1], rows_c[2], rows_a[4], rows_a[5], rows_a[6], rows_a[7]],
                                axis=2).reshape(8, 2, 3 * d)
    dmod_shard = lax.dynamic_slice(dmod_rows, (0, 0, chip * nmod), (8, 2, nmod)).transpose(1, 0, 2)
    dmod_pad = jnp.pad(dmod_shard, ((0, 0), (0, MOD_ROWS - 8), (0, 0)))
    gr_wmod = _mod_bwd(cond_pad.T, dmod_pad, "mod_bwd")
    two_d = lambda a: a.reshape(-1, nmod)
    outs = _adamw(two_d(w_mod), [two_d(gr_wmod)], two_d(m_w_mod), two_d(v_w_mod), "adamw_w_mod")
    results["w_mod"] = [o.reshape(w_mod.shape) for o in outs]

    names = ["ln_pre_g", "ln_post_g", "w_mod", "b_mod", "w_in_ab", "w_out_ab", "sgu_norm_g", "sgu_w", "sgu_b",
             "w_in_ssm", "w_out_ssm", "lam_re", "lam_im", "b_re", "b_im", "c_re", "c_im", "d_skip", "log_dt",
             "w_glu", "b_glu"]
    return (loss, grad_x[None], *[results[nm][0] for nm in names], *[results[nm][1] for nm in names],
            *[results[nm][2] for nm in names], *[results[nm][3] for nm in names])
```

```python
import functools
import math

import jax
import jax.numpy as jnp
from jax import lax
from jax.experimental import pallas as pl
from jax.experimental.pallas import tpu as pltpu

F32 = jnp.float32
BF16 = jnp.bfloat16
MESH = pl.DeviceIdType.MESH

EPS = 1e-6
HEAD = 128
SSM_T = 512
SSM_GB = 16
ADAM_LR, ADAM_B1, ADAM_B2, ADAM_EPS, ADAM_WD, ADAM_STEP = 0.001, 0.9, 0.999, 1e-08, 0.01, 10
VMEM_LIMIT = 56 * 1024 * 1024

NN = (((1,), (0,)), ((), ()))
NT = (((1,), (1,)), ((), ()))
TN = (((0,), (0,)), ((), ()))


def _params(sem=None):
    return pltpu.CompilerParams(dimension_semantics=sem, vmem_limit_bytes=VMEM_LIMIT)


def _dot(a, b, dims=NN):
    return lax.dot_general(a, b, dims, preferred_element_type=F32)


def _bf(x):
    return x.astype(BF16)


def _gelu(x):
    k = math.sqrt(2.0 / math.pi)
    t = jnp.tanh(k * (x + 0.044715 * x * x * x))
    return 0.5 * x * (1.0 + t)


def _gelu_grad(x):
    k = math.sqrt(2.0 / math.pi)
    x2 = x * x
    t = jnp.tanh(k * (x + 0.044715 * x * x2))
    return 0.5 * (1.0 + t) + 0.5 * x * (1.0 - t * t) * k * (1.0 + 3.0 * 0.044715 * x2)


def _sigmoid(x):
    return 1.0 / (1.0 + jnp.exp(-x))


def _silu(x):
    return x * _sigmoid(x)


def _silu_grad(x):
    s = _sigmoid(x)
    return s * (1.0 + x * (1.0 - s))


def _tile(n, t, mult=128):
    if n <= t:
        return n
    for cand in range(t - t % mult, 0, -mult):
        if n % cand == 0:
            return cand
    raise ValueError((n, t, mult))


def _matmul(a, b, mode, out_dtype, name, tm=512, tn=512, tk=2048, n_split=1, ride=None, m_part=None):
    b_sharded = b.ndim == 3
    if mode == "nn":
        (m, kk), (_, n) = a.shape, b.shape
    elif b_sharded:
        assert mode == "nt"
        (m, kk), n, tk = a.shape, b.shape[1], b.shape[2]
    elif mode == "nt":
        (m, kk), (n, _) = a.shape, b.shape
    else:
        (kk, m), (_, n) = a.shape, b.shape
    m_off = 0
    if m_part is not None:
        assert mode == "tn"
        first, count, parts = m_part
        tm = _tile(m // parts, tm)
        m_off = first * (m // parts) // tm
        m = count * (m // parts)
    tm, tk = _tile(m, tm), _tile(kk, tk)
    ns = n // n_split
    tn = _tile(ns, tn)
    nk = kk // tk
    dims = {"nn": NN, "nt": NT, "tn": TN}[mode]

    def body(a_ref, b_ref, o_ref, acc_ref):
        k = pl.program_id(2)
        part = _dot(_bf(a_ref[...]), _bf(b_ref[0] if b_sharded else b_ref[...]), dims)

        @pl.when(k == 0)
        def _():
            acc_ref[...] = part

        @pl.when(k > 0)
        def _():
            acc_ref[...] += part

        @pl.when(k == nk - 1)
        def _():
            o_ref[...] = acc_ref[...].astype(out_dtype).reshape(o_ref.shape)

    if mode == "nn":
        a_spec = pl.BlockSpec((tm, tk), lambda i, j, k: (i, k))
        b_spec = pl.BlockSpec((tk, tn), lambda i, j, k: (k, j))
    elif mode == "nt":
        a_spec = pl.BlockSpec((tm, tk), lambda i, j, k: (i, k))
        b_spec = (pl.BlockSpec((1, tn, tk), lambda i, j, k: (k, j, 0)) if b_sharded
                  else pl.BlockSpec((tn, tk), lambda i, j, k: (j, k)))
    else:
        a_spec = pl.BlockSpec((tk, tm), lambda i, j, k: (k, i + m_off))
        b_spec = pl.BlockSpec((tk, tn), lambda i, j, k: (k, j))
    if n_split == 1:
        out_shape = jax.ShapeDtypeStruct((m, n), out_dtype)
        o_spec = pl.BlockSpec((tm, tn), lambda i, j, k: (i, j))
    else:
        per = ns // tn
        out_shape = jax.ShapeDtypeStruct((n_split, m, ns), out_dtype)
        o_spec = pl.BlockSpec((1, tm, tn), lambda i, j, k: (j // per, i, j % per))
    outs = _call(body, name=name, grid=(m // tm, n // tn, nk), in_specs=[a_spec, b_spec], out_specs=[o_spec],
                 out_shape=[out_shape], scratch_shapes=[pltpu.VMEM((tm, tn), F32)], args=(a, b),
                 sem=("parallel", "parallel", "arbitrary"), ride=ride)
    return outs[0] if ride is None else (outs[0], outs[1:])


def _row_spec(tm, d):
    return pl.BlockSpec((tm, d), lambda i: (i, 0))


def _vec_spec(d):
    return pl.BlockSpec((1, d), lambda i: (0, 0))


def _acc(ref, first, val):
    @pl.when(first)
    def _():
        ref[...] = val

    @pl.when(jnp.logical_not(first))
    def _():
        ref[...] += val


def _colsum(x):
    return jnp.sum(x, axis=0, keepdims=True)


def _rownorm(x):
    r = lax.rsqrt(jnp.mean(x * x, axis=-1, keepdims=True) + EPS)
    return x * r, r


def _pre_fwd(x, g, scale, shift, name, ride=None):
    l, d = x.shape
    tm = _tile(l, 256)

    def body(x_ref, g_ref, sc_ref, sh_ref, h_ref):
        n, _ = _rownorm(x_ref[...])
        h_ref[...] = _bf(n * g_ref[...] * (1.0 + sc_ref[...]) + sh_ref[...])

    outs = _call(body, name=name, grid=(l // tm,), in_specs=[_row_spec(tm, d), _vec_spec(d), _vec_spec(d), _vec_spec(d)],
                 out_specs=[_row_spec(tm, d)], out_shape=[jax.ShapeDtypeStruct((l, d), BF16)],
                 args=(x, g, scale, shift), sem=("parallel",), ride=ride)
    return outs[0], outs[1:]


def _post_pre_fwd(x, y, gate, pg, g1, scale1, shift1, name):
    l, d = x.shape
    tm = _tile(l, 256)

    def body(x_ref, y_ref, gate_ref, pg_ref, g1_ref, sc_ref, sh_ref, x1_ref, h1_ref):
        ny, _ = _rownorm(y_ref[...].astype(F32))
        x1 = x_ref[...] + gate_ref[...] * (ny * pg_ref[...])
        x1_ref[...] = x1
        n1, _ = _rownorm(x1)
        h1_ref[...] = _bf(n1 * g1_ref[...] * (1.0 + sc_ref[...]) + sh_ref[...])

    v = _vec_spec(d)
    return pl.pallas_call(
        body, name=name, grid=(l // tm,),
        in_specs=[_row_spec(tm, d), _row_spec(tm, d), v, v, v, v, v],
        out_specs=[_row_spec(tm, d), _row_spec(tm, d)],
        out_shape=[jax.ShapeDtypeStruct((l, d), F32), jax.ShapeDtypeStruct((l, d), BF16)],
        compiler_params=_params(("parallel",)),
    )(x, y, gate, pg, g1, scale1, shift1)


def _post_loss(x1, y1, gate, pg, target, name):
    l, d = x1.shape
    tm = _tile(l, 256)

    def body(x_ref, y_ref, gate_ref, pg_ref, t_ref, loss_ref, dy_ref, dx_ref, dgate_ref, dpg_ref):
        first = pl.program_id(0) == 0
        ny, ry = _rownorm(y_ref[...].astype(F32))
        q = ny * pg_ref[...]
        x2 = x_ref[...] + gate_ref[...] * q
        e = x2 - t_ref[...]
        _acc(loss_ref, first, jnp.full((1, 128), 0.5 / d, F32) * jnp.sum(e * e))
        dx2 = e * (1.0 / d)
        dx_ref[...] = dx2
        _acc(dgate_ref, first, _colsum(dx2 * q))
        dq = dx2 * gate_ref[...]
        _acc(dpg_ref, first, _colsum(dq * ny))
        dny = dq * pg_ref[...]
        dy = ry * (dny - ny * jnp.mean(dny * ny, axis=-1, keepdims=True))
        dy_ref[...] = _bf(dy)

    v = _vec_spec(d)
    return pl.pallas_call(
        body, name=name, grid=(l // tm,),
        in_specs=[_row_spec(tm, d), _row_spec(tm, d), v, v, _row_spec(tm, d)],
        out_specs=[_vec_spec(128), _row_spec(tm, d), _row_spec(tm, d), v, v],
        out_shape=[jax.ShapeDtypeStruct((1, 128), F32), jax.ShapeDtypeStruct((l, d), BF16),
                   jax.ShapeDtypeStruct((l, d), F32), jax.ShapeDtypeStruct((1, d), F32),
                   jax.ShapeDtypeStruct((1, d), F32)],
        compiler_params=_params(("arbitrary",)),
    )(x1, y1, gate, pg, target)


def _pre_bwd(dh, dres, x, g, scale, name, post=None):
    l, d = x.shape
    tm = _tile(l, 256)
    with_post = post is not None

    def body(*refs):
        if with_post:
            (dh_ref, dres_ref, x_ref, g_ref, sc_ref, y_ref, gate_ref, pg_ref,
             dx_ref, dsc_ref, dsh_ref, dg_ref, dy_ref, dgate_ref, dpg_ref) = refs
        else:
            dh_ref, dres_ref, x_ref, g_ref, sc_ref, dx_ref, dsc_ref, dsh_ref, dg_ref = refs
        first = pl.program_id(0) == 0
        dh = dh_ref[...].astype(F32)
        n, r = _rownorm(x_ref[...])
        _acc(dsc_ref, first, _colsum(dh * (n * g_ref[...])))
        _acc(dsh_ref, first, _colsum(dh))
        dyn = dh * (1.0 + sc_ref[...])
        _acc(dg_ref, first, _colsum(dyn * n))
        dn = dyn * g_ref[...]
        dx = dres_ref[...] + r * (dn - n * jnp.mean(dn * n, axis=-1, keepdims=True))
        dx_ref[...] = dx
        if with_post:
            ny, ry = _rownorm(y_ref[...].astype(F32))
            _acc(dgate_ref, first, _colsum(dx * (ny * pg_ref[...])))
            dq = dx * gate_ref[...]
            _acc(dpg_ref, first, _colsum(dq * ny))
            dny = dq * pg_ref[...]
            dy_ref[...] = _bf(ry * (dny - ny * jnp.mean(dny * ny, axis=-1, keepdims=True)))

    v = _vec_spec(d)
    row = _row_spec(tm, d)
    vec_out = jax.ShapeDtypeStruct((1, d), F32)
    in_specs = [row, row, row, v, v]
    args = [dh, dres, x, g, scale]
    out_specs = [row, v, v, v]
    out_shape = [jax.ShapeDtypeStruct((l, d), F32), vec_out, vec_out, vec_out]
    if with_post:
        in_specs += [row, v, v]
        args += list(post)
        out_specs += [row, v, v]
        out_shape += [jax.ShapeDtypeStruct((l, d), BF16), vec_out, vec_out]
    return pl.pallas_call(
        body, name=name, grid=(l // tm,), in_specs=in_specs, out_specs=out_specs, out_shape=out_shape,
        compiler_params=_params(("arbitrary",)),
    )(*args)


def _softplus_parts(z):
    e = jnp.exp(-jnp.abs(z))
    den = 1.0 + e
    lb = jnp.minimum(z, 0.0) - jnp.log(den)
    return lb, lb - z, jnp.exp(lb)


def _tri(cmp, n=HEAD):
    row = lax.broadcasted_iota(jnp.int32, (n, n), 0)
    col = lax.broadcasted_iota(jnp.int32, (n, n), 1)
    return cmp(row, col)


ATT_T = 256
ATT_DEAD = 104.0


def _any_alive(runs):
    return functools.reduce(jnp.maximum, [jnp.max(r) for r in runs]) > -ATT_DEAD


def _attn_fwd(qkv, wb, name, hp=4, ride=None):
    l = qkv.shape[0]
    t = ATT_T
    nh, nq = wb // HEAD, l // t
    hp = min(hp, nh)
    ng, wg = nh // hp, hp * HEAD
    scale = 1.0 / math.sqrt(HEAD)

    def body(q_ref, k_ref, v_ref, o_ref):
        i = pl.program_id(1)
        valid = _tri(lambda r, c: c < r, t)
        m_gt = _bf(_tri(lambda r, c: r > c, t).astype(F32))

        def tile(j, carry, diag):
            rows = pl.ds(pl.multiple_of(j * t, t), t)
            cols = [slice(hh * HEAD, (hh + 1) * HEAD) for hh in range(hp)]
            zs = [_dot(q_ref[:, cs], k_ref[rows, cs], NT) * scale for cs in cols]
            lbs, lks = [], []
            for z in zs:
                lb, lk, _ = _softplus_parts(z)
                lbs.append(lb)
                lks.append(jnp.where(valid, lk, 0.0) if diag else lk)
            laters = [_dot(_bf(lk), m_gt) for lk in lks]
            ws = [jnp.exp(lb + later + run) for lb, later, (_, run) in zip(lbs, laters, carry)]
            if diag:
                ws = [jnp.where(valid, w, 0.0) for w in ws]
            return tuple((acc + _dot(_bf(w), v_ref[rows, cs]), run + jnp.sum(lk, axis=1, keepdims=True))
                         for w, lk, cs, (acc, run) in zip(ws, lks, cols, carry))

        zero = (jnp.zeros((t, HEAD), F32), jnp.zeros((t, 1), F32))
        carry = tile(i, (zero,) * hp, True)
        _, carry = lax.while_loop(lambda c: (c[0] < i) & _any_alive([run for _, run in c[1]]),
                                  lambda c: (c[0] + 1, tile(i - 1 - c[0], c[1], False)), (jnp.int32(0), carry))
        for hh, (acc, _) in enumerate(carry):
            o_ref[:, hh * HEAD:(hh + 1) * HEAD] = acc

    blk = lambda off: pl.BlockSpec((t, wg), lambda h, i: (i, off + h))
    full = lambda off: pl.BlockSpec((l, wg), lambda h, i: (0, off + h))
    out = pl.BlockSpec((t, wg), lambda h, i: (i, h))
    outs = _call(body, name=name, grid=(ng, nq), in_specs=[blk(0), full(ng), full(2 * ng)], out_specs=[out],
                 out_shape=[jax.ShapeDtypeStruct((l, wb), F32)],
                 args=(qkv, qkv, qkv), sem=("parallel", "arbitrary"), ride=ride)
    return outs[0], outs[1:]


def _attn_bwd(qkv, proj, dcat, wa, wb, name, hp=2, ride=None):
    l = qkv.shape[0]
    t = ATT_T
    nh, nq = wb // HEAD, l // t
    hp = min(hp, nh)
    ng, wg = nh // hp, hp * HEAD
    scale = 1.0 / math.sqrt(HEAD)

    def body(q_ref, k_ref, v_ref, bz_ref, dc_ref, dq_ref, dkt_out, dvt_out, do_s, qt_s, dot_s,
             dkt_ref, dvt_ref, out_sems):
        i = pl.program_id(1)

        @pl.when(i == 0)
        def _():
            dkt_ref[...] = jnp.zeros_like(dkt_ref)
            dvt_ref[...] = jnp.zeros_like(dvt_ref)

        do = dc_ref[...].astype(F32) * _silu(bz_ref[...])
        do_s[...] = _bf(do)
        for hh in range(hp):
            cs = slice(hh * HEAD, (hh + 1) * HEAD)
            qt_s[hh] = _bf(q_ref[:, cs].astype(F32).T * scale)
            dot_s[hh] = _bf(do[:, cs].T)
        valid = _tri(lambda r, c: c < r, t)
        m_le = _bf(_tri(lambda r, c: r <= c, t).astype(F32))
        m_lt = _bf(_tri(lambda r, c: r < c, t).astype(F32))

        heads = range(hp)
        cols = [slice(hh * HEAD, (hh + 1) * HEAD) for hh in heads]

        def row_sums(j, runs, diag):
            rows = pl.ds(pl.multiple_of(j * t, t), t)
            out = []
            for cs, run in zip(cols, runs):
                _, lk, _ = _softplus_parts(_dot(q_ref[:, cs], k_ref[rows, cs], NT) * scale)
                if diag:
                    lk = jnp.where(valid, lk, 0.0)
                out.append(run + jnp.sum(lk, axis=1, keepdims=True))
            return tuple(out)

        runs = row_sums(i, (jnp.zeros((t, 1), F32),) * hp, True)
        below, lktot = lax.while_loop(lambda c: (c[0] < i) & _any_alive(c[1]),
                                      lambda c: (c[0] + 1, row_sums(i - 1 - c[0], c[1], False)), (jnp.int32(0), runs))

        def tile(j, carry, diag):
            rows = pl.ds(pl.multiple_of(j * t, t), t)
            zs = [_dot(q_ref[:, cs], k_ref[rows, cs], NT) * scale for cs in cols]
            dws = [_dot(do_s[:, cs], v_ref[rows, cs], NT) for cs in cols]
            lbs, lks, sigs = [], [], []
            for z in zs:
                lb, lk, sig = _softplus_parts(z)
                lbs.append(lb)
                lks.append(jnp.where(valid, lk, 0.0) if diag else lk)
                sigs.append(sig)
            pins = [_dot(_bf(lk), m_le) for lk in lks]
            ws = [jnp.exp(lbs[hh] + (lktot[hh] - carry[hh][1]) - pins[hh]) for hh in heads]
            if diag:
                ws = [jnp.where(valid, w, 0.0) for w in ws]
            das = [dw * w for dw, w in zip(dws, ws)]
            pexs = [_dot(_bf(da), m_lt) for da in das]
            dzs = [das[hh] - sigs[hh] * (das[hh] + carry[hh][2] + pexs[hh]) for hh in heads]
            if diag:
                dzs = [jnp.where(valid, dz, 0.0) for dz in dzs]
            dzs = [_bf(dz) for dz in dzs]
            out = []
            for hh in heads:
                dkt, dvt = _dot(qt_s[hh], dzs[hh]), _dot(dot_s[hh], _bf(ws[hh]))
                for half in range(t // HEAD):
                    dkt_ref[hh, sub * j + half] += dkt[:, half * HEAD:(half + 1) * HEAD]
                    dvt_ref[hh, sub * j + half] += dvt[:, half * HEAD:(half + 1) * HEAD]
                dq, cpre, ppre = carry[hh]
                out.append((dq + _dot(dzs[hh], k_ref[rows, cols[hh]]), cpre + jnp.sum(lks[hh], axis=1, keepdims=True),
                            ppre + pexs[hh][:, t - 1:] + das[hh][:, t - 1:]))
            return tuple(out)

        zero = (jnp.zeros((t, HEAD), F32), jnp.zeros((t, 1), F32), jnp.zeros((t, 1), F32))
        carry = lax.fori_loop(i - below, i, lambda j, c: tile(j, c, False), (zero,) * hp)
        carry = tile(i, carry, True)
        for hh in range(hp):
            dq_ref[:, hh * HEAD:(hh + 1) * HEAD] = carry[hh][0] * scale

        @pl.when(i == nq - 1)
        def _():
            heads = pl.ds(pl.program_id(0) * hp, hp)
            copies = [pltpu.make_async_copy(dkt_ref, dkt_out.at[heads], out_sems.at[0]),
                      pltpu.make_async_copy(dvt_ref, dvt_out.at[heads], out_sems.at[1])]
            for cp in copies:
                cp.start()
            for cp in copies:
                cp.wait()

    sub = t // HEAD
    blk = lambda off: pl.BlockSpec((t, wg), lambda h, i: (i, off + h))
    full = lambda off: pl.BlockSpec((l, wg), lambda h, i: (0, off + h))
    acc_shape = jax.ShapeDtypeStruct((nh, l // HEAD, HEAD, HEAD), F32)
    acc_scratch = pltpu.VMEM((hp, l // HEAD, HEAD, HEAD), F32)
    outs = _call(
        body, name=name, grid=(ng, nq),
        in_specs=[blk(0), full(ng), full(2 * ng), blk(3 * wa // wg), blk(wa // wg)],
        out_specs=[blk(0), ANY, ANY], out_shape=[jax.ShapeDtypeStruct((l, wb), F32), acc_shape, acc_shape],
        scratch_shapes=[pltpu.VMEM((t, wg), BF16), pltpu.VMEM((hp, HEAD, t), BF16), pltpu.VMEM((hp, HEAD, t), BF16),
                        acc_scratch, acc_scratch, pltpu.SemaphoreType.DMA((2,))],
        args=(qkv, qkv, qkv, proj, dcat), sem=("parallel", "arbitrary"), ride=ride)
    return outs[0], outs[1], outs[2], outs[3:]


def _sgu_heads(v, g_ref, w_ref, bt_ref, nh):
    keep = _tri(lambda r, c: r >= c)
    out = []
    for h in range(nh):
        cols = slice(h * HEAD, (h + 1) * HEAD)
        nv, r = _rownorm(v[:, cols])
        wm = jnp.where(keep, w_ref[h], 0.0)
        s = _dot(_bf(wm), _bf(nv * g_ref[:, cols])) + bt_ref[:, h:h + 1]
        out.append((nv, r, wm, s))
    return out


def _sgu_fwd(proj, out_b, norm_g, sgu_w, sgu_bt, wa, wb, name):
    l, n = proj.shape
    nh = wa // HEAD

    def body(au_ref, av_ref, az_ref, bz_ref, ob_ref, g_ref, w_ref, bt_ref, cat_ref):
        u, v, sz = _gelu(au_ref[...]), _gelu(av_ref[...]), _silu(az_ref[...])
        for h, (_, _, _, s) in enumerate(_sgu_heads(v, g_ref, w_ref, bt_ref, nh)):
            cols = slice(h * HEAD, (h + 1) * HEAD)
            cat_ref[:, cols] = _bf(u[:, cols] * s * sz[:, cols])
        cat_ref[:, wa:] = _bf(ob_ref[...] * _silu(bz_ref[...]))

    a_blk = lambda j: pl.BlockSpec((HEAD, wa), lambda i: (i, j))
    return pl.pallas_call(
        body, name=name, grid=(l // HEAD,),
        in_specs=[a_blk(0), a_blk(1), a_blk(2), a_blk(3), pl.BlockSpec((HEAD, wb), lambda i: (i, 0)),
                  _vec_spec(wa), pl.BlockSpec((nh, HEAD, HEAD), lambda i: (0, 0, 0)),
                  pl.BlockSpec((HEAD, nh), lambda i: (0, 0))],
        out_specs=pl.BlockSpec((HEAD, wa + wb), lambda i: (i, 0)),
        out_shape=jax.ShapeDtypeStruct((l, wa + wb), BF16),
        compiler_params=_params(("parallel",)),
    )(proj, proj, proj, proj, out_b, norm_g, sgu_w, sgu_bt)


def _sgu_bwd(proj, out_b, dcat, dq, dk, dv, norm_g, sgu_w, sgu_bt, wa, wb, name, ride=None):
    l = proj.shape[0]
    n = 3 * wa + 4 * wb
    nh = wa // HEAD

    def body(au_ref, av_ref, az_ref, bz_ref, ob_ref, dc_ref, dq_ref, dk_ref, dv_ref, g_ref, w_ref, wt_ref, bt_ref,
             dp_ref, dw_ref, dbt_ref, dg_ref):
        first = pl.program_id(0) == 0
        keep = _tri(lambda r, c: r >= c)
        au, av, az = au_ref[...], av_ref[...], az_ref[...]
        u, v, sz = _gelu(au), _gelu(av), _silu(az)
        dgelu_u, dgelu_v, dsilu_z = _gelu_grad(au), _gelu_grad(av), _silu_grad(az)
        heads = _sgu_heads(v, g_ref, w_ref, bt_ref, nh)
        cols = [slice(h * HEAD, (h + 1) * HEAD) for h in range(nh)]
        dss = []
        for h, (nv, r, wm, s) in enumerate(heads):
            dca, uh, szh = dc_ref[:, cols[h]].astype(F32), u[:, cols[h]], sz[:, cols[h]]
            dp_ref[:, cols[h]] = _bf(dca * s * szh * dgelu_u[:, cols[h]])
            dp_ref[:, 2 * wa + h * HEAD:2 * wa + (h + 1) * HEAD] = _bf(dca * uh * s * dsilu_z[:, cols[h]])
            dss.append(dca * uh * szh)
        dws = [_dot(_bf(ds), _bf(nv * g_ref[:, cs]), NT) for ds, cs, (nv, _, _, _) in zip(dss, cols, heads)]
        keep_t = _tri(lambda r, c: r <= c)
        dvhs = [_dot(_bf(jnp.where(keep_t, wt_ref[h], 0.0)), _bf(dss[h])) for h in range(nh)]
        dg_parts = []
        for h, (nv, r, wm, s) in enumerate(heads):
            _acc(dw_ref.at[h], first, jnp.where(keep, dws[h], 0.0))
            _acc(dbt_ref.at[:, h:h + 1], first, jnp.sum(dss[h], axis=1, keepdims=True))
            dg_parts.append(_colsum(dvhs[h] * nv))
            dnv = dvhs[h] * g_ref[:, cols[h]]
            dvv = r * (dnv - nv * jnp.mean(dnv * nv, axis=-1, keepdims=True))
            dp_ref[:, wa + h * HEAD:wa + (h + 1) * HEAD] = _bf(dvv * dgelu_v[:, cols[h]])
        _acc(dg_ref, first, jnp.concatenate(dg_parts, axis=1))
        base = 3 * wa
        dp_ref[:, base:base + wb] = _bf(dq_ref[...])
        for h in range(wb // HEAD):
            dp_ref[:, base + wb + h * HEAD:base + wb + (h + 1) * HEAD] = _bf(dk_ref[h, 0].T)
            dp_ref[:, base + 2 * wb + h * HEAD:base + 2 * wb + (h + 1) * HEAD] = _bf(dv_ref[h, 0].T)
        dp_ref[:, base + 3 * wb:] = _bf(dc_ref[:, wa:].astype(F32) * ob_ref[...] * _silu_grad(bz_ref[...]))

    a_blk = lambda j: pl.BlockSpec((HEAD, wa), lambda i: (i, j))
    b_blk = pl.BlockSpec((HEAD, wb), lambda i: (i, 0))
    t_blk = pl.BlockSpec((wb // HEAD, 1, HEAD, HEAD), lambda i: (0, i, 0, 0))
    w_spec = pl.BlockSpec((nh, HEAD, HEAD), lambda i: (0, 0, 0))
    bt_spec = pl.BlockSpec((HEAD, nh), lambda i: (0, 0))
    outs = _call(
        body, name=name, grid=(l // HEAD,), ride=ride, sem=("arbitrary",),
        in_specs=[a_blk(0), a_blk(1), a_blk(2), a_blk(3), b_blk, pl.BlockSpec((HEAD, wa + wb), lambda i: (i, 0)),
                  b_blk, t_blk, t_blk, _vec_spec(wa), w_spec, w_spec, bt_spec],
        out_specs=[pl.BlockSpec((HEAD, n), lambda i: (i, 0)), w_spec, bt_spec, _vec_spec(wa)],
        out_shape=[jax.ShapeDtypeStruct((l, n), BF16), jax.ShapeDtypeStruct((nh, HEAD, HEAD), F32),
                   jax.ShapeDtypeStruct((HEAD, nh), F32), jax.ShapeDtypeStruct((1, wa), F32)],
        args=(proj, proj, proj, proj, out_b, dcat, dq, dk, dv, norm_g, sgu_w, sgu_w.transpose(0, 2, 1), sgu_bt))
    return (*outs[:4], outs[4:])


def _ssm_discretise(lr, li, ldt, br, bi):
    dt = jnp.exp(ldt)
    mag = jnp.exp(lr * dt)
    a_re = mag * jnp.cos(li * dt)
    a_im = mag * jnp.sin(li * dt)
    den = lr * lr + li * li
    nr = a_re - 1.0
    coef_re = (nr * lr + a_im * li) / den
    coef_im = (a_im * lr - nr * li) / den
    return a_re, a_im, coef_re * br - coef_im * bi, coef_re * bi + coef_im * br


def _ssm_prep(lr, li, ldt, br, bi, lr_row, li_row, ldt_row, name):
    s, c = br.shape

    def body(lr_ref, li_ref, ldt_ref, br_ref, bi_ref, lrr_ref, lir_ref, ldtr_ref, bbr_ref, bbi_ref, tr_ref, ti_ref):
        _, _, bbr, bbi = _ssm_discretise(lr_ref[...], li_ref[...], ldt_ref[...], br_ref[...], bi_ref[...])
        bbr_ref[...] = bbr
        bbi_ref[...] = bbi
        row = lax.broadcasted_iota(jnp.int32, (SCAN_ROWS, 1), 0)
        blk, r = jnp.right_shift(row, 3), jnp.bitwise_and(row, 7)
        kind, rev = jnp.bitwise_and(blk, 3), blk >= 4
        step = jnp.left_shift(1, kind)
        n = jnp.where(kind < 3, step, jnp.where(rev, 8 - r, r + 1)).astype(F32)
        keep = (kind == 3) | (rev & (r < 8 - step)) | (jnp.logical_not(rev) & (r >= step))
        dt = jnp.exp(ldtr_ref[...])
        mag = jnp.exp(n * (lrr_ref[...] * dt))
        ang = n * (lir_ref[...] * dt)
        tr_ref[...] = jnp.where(keep, mag * jnp.cos(ang), 0.0)
        ti_ref[...] = jnp.where(keep, jnp.where(rev, -1.0, 1.0) * mag * jnp.sin(ang), 0.0)

    col = jax.ShapeDtypeStruct((s, c), F32)
    row = jax.ShapeDtypeStruct((SCAN_ROWS, s), F32)
    return pl.pallas_call(body, name=name, out_shape=[col, col, row, row])(
        lr, li, ldt, br, bi, lr_row, li_row, ldt_row)


def _ssm_prep_bwd(lr, li, ldt, br, bi, da_re, da_im, dbb_re, dbb_im, p, name):
    s, c = br.shape

    def body(lr_ref, li_ref, ldt_ref, br_ref, bi_ref, dar_ref, dai_ref, dbr_ref, dbi_ref,
             dlr_ref, dli_ref, dldt_ref, dbre_ref, dbim_ref):
        args = (lr_ref[...], li_ref[...], ldt_ref[...], br_ref[...], bi_ref[...])
        _, vjp = jax.vjp(_ssm_discretise, *args)
        dlr, dli, dldt, dbr, dbi = vjp((dar_ref[...], dai_ref[...], dbr_ref[...], dbi_ref[...]))
        dlr_ref[...] = dlr
        dli_ref[...] = dli
        dbre_ref[...] = dbr
        dbim_ref[...] = dbi
        idx = lax.broadcasted_iota(jnp.int32, (s, s // p), 0)
        grp = lax.broadcasted_iota(jnp.int32, (s, s // p), 1)
        own = (idx >= grp * p) & (idx < (grp + 1) * p)
        dldt_ref[...] = _colsum(jnp.where(own, dldt, 0.0))

    col1 = jax.ShapeDtypeStruct((s, 1), F32)
    colc = jax.ShapeDtypeStruct((s, c), F32)
    return pl.pallas_call(
        body, name=name, out_shape=[col1, col1, jax.ShapeDtypeStruct((1, s // p), F32), colc, colc],
    )(lr, li, ldt, br, bi, da_re, da_im, dbb_re, dbb_im)


SCAN_ROWS = 64


def _scan_groups(xr, xi, tr_ref, ti_ref, cr, ci, reverse):
    ng = xr.shape[0] // 8
    base = SCAN_ROWS // 2 if reverse else 0
    pr, pi = tr_ref[base + 24:base + 32, :], ti_ref[base + 24:base + 32, :]
    edge = slice(0, 1) if reverse else slice(7, 8)
    out_r, out_i = [None] * ng, [None] * ng
    for g in (range(ng - 1, -1, -1) if reverse else range(ng)):
        sr, si = xr[8 * g:8 * g + 8, :], xi[8 * g:8 * g + 8, :]
        for k in range(3):
            ar, ai = tr_ref[base + 8 * k:base + 8 * k + 8, :], ti_ref[base + 8 * k:base + 8 * k + 8, :]
            shift = 8 - (1 << k) if reverse else 1 << k
            rr, ri = pltpu.roll(sr, shift, 0), pltpu.roll(si, shift, 0)
            sr, si = sr + ar * rr - ai * ri, si + ar * ri + ai * rr
        sr, si = sr + pr * cr - pi * ci, si + pr * ci + pi * cr
        cr, ci = sr[edge, :], si[edge, :]
        out_r[g], out_i[g] = sr, si
    return jnp.concatenate(out_r, axis=0), jnp.concatenate(out_i, axis=0), cr, ci


def _ssm_fwd(proj, bbd, ccd, pw_re, pw_im, d_skip, w, name):
    l = proj.shape[0]
    nb, cw, ns2 = bbd.shape
    ns = ns2 // 2
    nc = l // SSM_T

    def body(u_ref, bbd_ref, ccd_ref, pr_ref, pi_ref, d_ref, y_ref, hsr_ref, hsi_ref, h_ref, hr_s, hi_s):
        @pl.when(pl.program_id(1) == 0)
        def _():
            hr_s[...] = jnp.zeros_like(hr_s)
            hi_s[...] = jnp.zeros_like(hi_s)

        hsr_ref[...] = hr_s[...].reshape(hsr_ref.shape)
        hsi_ref[...] = hi_s[...].reshape(hsi_ref.shape)
        u = u_ref[...]
        bu = _dot(_bf(u), bbd_ref[0])
        hr, hi, cr, ci = _scan_groups(bu[:, :ns], bu[:, ns:], pr_ref, pi_ref, hr_s[...], hi_s[...], False)
        hr_s[...] = cr
        hi_s[...] = ci
        h_bf = _bf(jnp.concatenate([hr, hi], axis=1))
        h_ref[...] = h_bf
        y_ref[...] = _dot(h_bf, ccd_ref[0]) + d_ref[...] * u

    tab = pl.BlockSpec((SCAN_ROWS, ns), lambda b, k: (0, b))
    return pl.pallas_call(
        body, name=name, grid=(nb, nc),
        in_specs=[pl.BlockSpec((SSM_T, cw), lambda b, k: (k, b)),
                  pl.BlockSpec((1, cw, ns2), lambda b, k: (b, 0, 0)),
                  pl.BlockSpec((1, ns2, cw), lambda b, k: (b, 0, 0)),
                  tab, tab, pl.BlockSpec((1, cw), lambda b, k: (0, b))],
        out_specs=[pl.BlockSpec((SSM_T, cw), lambda b, k: (k, b)),
                   pl.BlockSpec((1, 1, ns), lambda b, k: (k, 0, b)), pl.BlockSpec((1, 1, ns), lambda b, k: (k, 0, b)),
                   pl.BlockSpec((SSM_T, ns2), lambda b, k: (k, b))],
        out_shape=[jax.ShapeDtypeStruct((l, w), F32), jax.ShapeDtypeStruct((nc, 1, nb * ns), F32),
                   jax.ShapeDtypeStruct((nc, 1, nb * ns), F32), jax.ShapeDtypeStruct((l, nb * ns2), BF16)],
        scratch_shapes=[pltpu.VMEM((1, ns), F32), pltpu.VMEM((1, ns), F32)],
        compiler_params=_params(("parallel", "arbitrary")),
    )(proj, bbd, ccd, pw_re, pw_im, d_skip)


def _ssm_bwd(proj, dy, hs_re, hs_im, h_all, bbd, ccd, pw_re, pw_im, d_skip, w, name, ride=None):
    l = proj.shape[0]
    nb, cw, ns2 = bbd.shape
    ns = ns2 // 2
    nc = l // SSM_T

    def body(u_ref, dy_ref, hsr_ref, hsi_ref, h_ref, bbd_ref, ccd_ref, pr_ref, pi_ref, d_ref,
             du_ref, dbbd_ref, dccd_ref, dar_ref, dai_ref, dd_ref, gr_s, gi_s):
        first = pl.program_id(1) == 0

        @pl.when(first)
        def _():
            gr_s[...] = jnp.zeros_like(gr_s)
            gi_s[...] = jnp.zeros_like(gi_s)

        u, dy = u_ref[...], dy_ref[...]
        dy_bf = _bf(dy)
        hr0, hi0 = hsr_ref[0], hsi_ref[0]
        h = h_ref[...].astype(F32)
        hr, hi = h[:, :ns], h[:, ns:]
        dh = _dot(dy_bf, ccd_ref[0], NT)
        gr, gi, gcr, gci = _scan_groups(dh[:, :ns], dh[:, ns:], pr_ref, pi_ref, gr_s[...], gi_s[...], True)
        gr_s[...] = gcr
        gi_s[...] = gci
        row0 = lax.broadcasted_iota(jnp.int32, hr.shape, 0) == 0
        pr_h = jnp.where(row0, hr0, pltpu.roll(hr, 1, 0))
        pi_h = jnp.where(row0, hi0, pltpu.roll(hi, 1, 0))
        _acc(dar_ref, first, _colsum(pr_h * gr + pi_h * gi))
        _acc(dai_ref, first, _colsum(pr_h * gi - pi_h * gr))
        g_bf = _bf(jnp.concatenate([gr, gi], axis=1))
        _acc(dbbd_ref.at[0], first, _dot(_bf(u.T), g_bf))
        _acc(dccd_ref.at[0], first, _dot(_bf(h.T), dy_bf))
        du_ref[...] = _bf(_dot(g_bf, bbd_ref[0], NT) + d_ref[...] * dy)
        _acc(dd_ref, first, _colsum(dy * u))

    rev = lambda b, k: (nc - 1 - k, b)
    outs = _call(
        body, name=name, grid=(nb, nc), ride=ride, sem=("parallel", "arbitrary"),
        args=(proj, dy, hs_re, hs_im, h_all, bbd, ccd, pw_re, pw_im, d_skip),
        in_specs=[pl.BlockSpec((SSM_T, cw), rev), pl.BlockSpec((SSM_T, cw), rev),
                  pl.BlockSpec((1, 1, ns), lambda b, k: (nc - 1 - k, 0, b)),
                  pl.BlockSpec((1, 1, ns), lambda b, k: (nc - 1 - k, 0, b)),
                  pl.BlockSpec((SSM_T, ns2), rev),
                  pl.BlockSpec((1, cw, ns2), lambda b, k: (b, 0, 0)),
                  pl.BlockSpec((1, ns2, cw), lambda b, k: (b, 0, 0)),
                  pl.BlockSpec((SCAN_ROWS, ns), lambda b, k: (0, b)), pl.BlockSpec((SCAN_ROWS, ns), lambda b, k: (0, b)),
                  pl.BlockSpec((1, cw), lambda b, k: (0, b))],
        out_specs=[pl.BlockSpec((SSM_T, cw), rev),
                   pl.BlockSpec((1, cw, ns2), lambda b, k: (b, 0, 0)),
                   pl.BlockSpec((1, ns2, cw), lambda b, k: (b, 0, 0)),
                   pl.BlockSpec((1, ns), lambda b, k: (0, b)), pl.BlockSpec((1, ns), lambda b, k: (0, b)),
                   pl.BlockSpec((1, cw), lambda b, k: (0, b))],
        out_shape=[jax.ShapeDtypeStruct((l, w), BF16), jax.ShapeDtypeStruct(bbd.shape, F32),
                   jax.ShapeDtypeStruct(ccd.shape, F32), jax.ShapeDtypeStruct((1, nb * ns), F32),
                   jax.ShapeDtypeStruct((1, nb * ns), F32), jax.ShapeDtypeStruct((1, w), F32)],
        scratch_shapes=[pltpu.VMEM((1, ns), F32), pltpu.VMEM((1, ns), F32)])
    return (*outs[:6], outs[6:])


def _block_diag_b(bb_re, bb_im, g, p, c):
    nb = g // SSM_GB
    keep = _same_group(SSM_GB * c, c, SSM_GB * p, p)

    def one(bb):
        t = bb.reshape(nb, SSM_GB, p, c).transpose(0, 1, 3, 2).reshape(nb, SSM_GB * c, p)
        return jnp.where(keep, jnp.tile(t, (1, 1, SSM_GB)), 0.0)

    return jnp.concatenate([one(bb_re), one(bb_im)], axis=2)


def _same_group(rows, per_row, cols, per_col):
    r = lax.broadcasted_iota(jnp.int32, (rows, cols), 0) // per_row
    q = lax.broadcasted_iota(jnp.int32, (rows, cols), 1) // per_col
    return r == q


def _block_diag_c(c_re, c_im, g, p, c):
    nb = g // SSM_GB
    keep = _same_group(SSM_GB * p, p, SSM_GB * c, c)

    def one(cc):
        t = cc.reshape(nb, SSM_GB, c, p).transpose(0, 1, 3, 2).reshape(nb, SSM_GB * p, c)
        return jnp.where(keep, jnp.tile(t, (1, 1, SSM_GB)), 0.0)

    return jnp.concatenate([one(c_re), one(-c_im)], axis=1)


def _diag_of_b(dbbd, g, p, c):
    nb = g // SSM_GB
    keep = _same_group(SSM_GB * c, c, SSM_GB * p, p)

    def one(blk):
        d = jnp.where(keep, blk, 0.0).reshape(nb, SSM_GB * c, SSM_GB, p).sum(axis=2)
        return d.reshape(nb, SSM_GB, c, p).transpose(0, 1, 3, 2).reshape(g * p, c)

    half = SSM_GB * p
    return one(dbbd[:, :, :half]), one(dbbd[:, :, half:])


def _diag_of_c(dccd, g, p, c):
    nb = g // SSM_GB
    keep = _same_group(SSM_GB * p, p, SSM_GB * c, c)

    def one(blk):
        d = jnp.where(keep, blk, 0.0).reshape(nb, SSM_GB * p, SSM_GB, c).sum(axis=2)
        return d.reshape(nb, SSM_GB, p, c).transpose(0, 1, 3, 2).reshape(g, c, p)

    half = SSM_GB * p
    return one(dccd[:, :half]), -one(dccd[:, half:])


def _glu_fwd(y, proj, w_glu, b_glu, name):
    l, w = y.shape
    tm = _tile(l, 256)

    def body(y_ref, z_ref, w_ref, b_ref, o_ref):
        g = _gelu(y_ref[...])
        t = _dot(_bf(g), w_ref[...]) + b_ref[...]
        o_ref[...] = _bf(g * _sigmoid(t) * _silu(z_ref[...]))

    return pl.pallas_call(
        body, name=name, grid=(l // tm,),
        in_specs=[_row_spec(tm, w), pl.BlockSpec((tm, w), lambda i: (i, 1)),
                  pl.BlockSpec((w, w), lambda i: (0, 0)), _vec_spec(w)],
        out_specs=_row_spec(tm, w), out_shape=jax.ShapeDtypeStruct((l, w), BF16),
        compiler_params=_params(("parallel",)),
    )(y, proj, w_glu, b_glu)


def _glu_bwd(do, y, proj, w_glu, b_glu, name):
    l, w = y.shape
    tm = _tile(l, 512)
    nsteps = l // tm

    def body(do_ref, y_ref, z_ref, w_ref, b_ref, dy_ref, dz_ref, dw_ref, db_ref, dw_acc):
        i = pl.program_id(0)
        first = i == 0
        yv, z, do = y_ref[...], z_ref[...], do_ref[...].astype(F32)
        g = _gelu(yv)
        g_bf = _bf(g)
        sg = _sigmoid(_dot(g_bf, w_ref[...]) + b_ref[...])
        dyy = do * _silu(z)
        dz_ref[...] = _bf(do * g * sg * _silu_grad(z))
        dt = dyy * g * sg * (1.0 - sg)
        dt_bf = _bf(dt)
        dg = dyy * sg + _dot(dt_bf, w_ref[...], NT)
        dy_ref[...] = dg * _gelu_grad(yv)
        _acc(dw_acc, first, _dot(_bf(g.T), dt_bf))
        _acc(db_ref, first, _colsum(dt))

        @pl.when(i == nsteps - 1)
        def _():
            dw_ref[...] = _bf(dw_acc[...])

    return pl.pallas_call(
        body, name=name, grid=(nsteps,),
        in_specs=[_row_spec(tm, w), _row_spec(tm, w), pl.BlockSpec((tm, w), lambda i: (i, 1)),
                  pl.BlockSpec((w, w), lambda i: (0, 0)), _vec_spec(w)],
        out_specs=[_row_spec(tm, w), _row_spec(tm, w), pl.BlockSpec((w, w), lambda i: (0, 0)), _vec_spec(w)],
        out_shape=[jax.ShapeDtypeStruct((l, w), F32), jax.ShapeDtypeStruct((l, w), BF16),
                   jax.ShapeDtypeStruct((w, w), BF16), jax.ShapeDtypeStruct((1, w), F32)],
        scratch_shapes=[pltpu.VMEM((w, w), F32)],
        compiler_params=_params(("arbitrary",)),
    )(do, y, proj, w_glu, b_glu)


MOD_ROWS = 128


def _mod_fwd(cond_pad, w_mod, b_shard, name):
    nl, d, ncol = w_mod.shape
    tn = _tile(ncol, 512)

    def body(c_ref, w_ref, b_ref, o_ref):
        o_ref[0] = _dot(_bf(c_ref[...]), _bf(w_ref[0])) + b_ref[0]

    return pl.pallas_call(
        body, name=name, grid=(nl, ncol // tn),
        in_specs=[pl.BlockSpec((MOD_ROWS, d), lambda a, j: (0, 0)),
                  pl.BlockSpec((1, d, tn), lambda a, j: (a, 0, j)),
                  pl.BlockSpec((1, 1, tn), lambda a, j: (a, 0, j))],
        out_specs=pl.BlockSpec((1, MOD_ROWS, tn), lambda a, j: (a, 0, j)),
        out_shape=jax.ShapeDtypeStruct((nl, MOD_ROWS, ncol), F32),
        compiler_params=_params(("parallel", "parallel")),
    )(cond_pad, w_mod, b_shard)


def _mod_bwd(cond_pad_t, dmod_pad, name):
    nl, _, ncol = dmod_pad.shape
    d = cond_pad_t.shape[0]
    tn = _tile(ncol, 512)

    def body(c_ref, dm_ref, o_ref):
        o_ref[0] = _dot(_bf(c_ref[...]), _bf(dm_ref[0]))

    return pl.pallas_call(
        body, name=name, grid=(nl, ncol // tn),
        in_specs=[pl.BlockSpec((d, MOD_ROWS), lambda a, j: (0, 0)),
                  pl.BlockSpec((1, MOD_ROWS, tn), lambda a, j: (a, 0, j))],
        out_specs=pl.BlockSpec((1, d, tn), lambda a, j: (a, 0, j)),
        out_shape=jax.ShapeDtypeStruct((nl, d, ncol), F32),
        compiler_params=_params(("parallel", "parallel")),
    )(cond_pad_t, dmod_pad)


def _silu_rows(c2d, name):
    def body(c_ref, o_ref):
        o_ref[...] = _silu(c_ref[...])

    return pl.pallas_call(body, name=name, out_shape=jax.ShapeDtypeStruct(c2d.shape, F32))(c2d)


def _sum_leading(x, name):
    n, r, c = x.shape
    tr = _tile(r, max(16, (1 << 20) // (4 * c)), 16 if r % 16 == 0 else 8)

    def body(x_ref, o_ref):
        acc = x_ref[0].astype(F32)
        for k in range(1, n):
            acc = acc + x_ref[k].astype(F32)
        o_ref[...] = acc

    return pl.pallas_call(
        body, name=name, grid=(r // tr,),
        in_specs=[pl.BlockSpec((n, tr, c), lambda i: (0, i, 0))], out_specs=pl.BlockSpec((tr, c), lambda i: (i, 0)),
        out_shape=jax.ShapeDtypeStruct((r, c), F32), compiler_params=_params(("parallel",)),
    )(x)


def _adamw(w, gs, m, v, name):
    r, c = w.shape
    tr = _tile(r, max(8, (3 << 19) // (4 * c)), 8)
    ng = len(gs)

    def body(*refs):
        w_ref, g_refs, m_ref, v_ref = refs[0], refs[1:1 + ng], refs[1 + ng], refs[2 + ng]
        g_ref, d_ref, nm_ref, nv_ref = refs[3 + ng:]
        g = g_refs[0][...]
        for extra in g_refs[1:]:
            g = g + extra[...]
        g_ref[...] = g
        d_ref[...], nm_ref[...], nv_ref[...] = _adamw_math(w_ref[...], g, m_ref[...], v_ref[...])

    spec = pl.BlockSpec((tr, c), lambda i: (i, 0))
    shp = jax.ShapeDtypeStruct((r, c), F32)
    return pl.pallas_call(
        body, name=name, grid=(r // tr,), in_specs=[spec] * (3 + ng), out_specs=[spec] * 4,
        out_shape=[shp] * 4, compiler_params=_params(("parallel",)),
    )(w, *gs, m, v)


def _adamw_math(w, g, m, v):
    nm = ADAM_B1 * m + (1.0 - ADAM_B1) * g
    nv = ADAM_B2 * v + (1.0 - ADAM_B2) * (g * g)
    m_hat = nm / (1.0 - ADAM_B1 ** ADAM_STEP)
    v_hat = nv / (1.0 - ADAM_B2 ** ADAM_STEP)
    return -ADAM_LR * (m_hat / (jnp.sqrt(v_hat) + ADAM_EPS) + ADAM_WD * w), nm, nv


def _adamw_many(ws, gs, ms, vs, name):
    n = len(ws)

    def body(*refs):
        w_refs, g_refs, m_refs, v_refs = (refs[k * n:(k + 1) * n] for k in range(4))
        outs = refs[4 * n:]
        for i in range(n):
            outs[3 * i][...], outs[3 * i + 1][...], outs[3 * i + 2][...] = _adamw_math(
                w_refs[i][...], g_refs[i][...], m_refs[i][...], v_refs[i][...])

    out_shape = [jax.ShapeDtypeStruct(w.shape, F32) for w in ws for _ in range(3)]
    outs = pl.pallas_call(body, name=name, out_shape=out_shape, compiler_params=_params())(*ws, *gs, *ms, *vs)
    return [tuple(outs[3 * i:3 * i + 3]) for i in range(n)]


ANY = pl.BlockSpec(memory_space=pl.ANY)


def _flip(v, bit):
    return 1 - v if bit else v


def _allgather8_ops(x_ref, o_ref, send_sems, recv_sems, local_sem):
    mx, my, mc = lax.axis_index("x"), lax.axis_index("y"), lax.axis_index("c")
    me = 4 * mx + 2 * my + mc

    def mine():
        return pltpu.make_async_copy(x_ref, o_ref.at[me], local_sem)

    def copy(j, outgoing):
        peer = (_flip(mx, j & 4), _flip(my, j & 2), _flip(mc, j & 1))
        slot = me if outgoing else 4 * peer[0] + 2 * peer[1] + peer[2]
        return pltpu.make_async_remote_copy(
            src_ref=x_ref, dst_ref=o_ref.at[slot], send_sem=send_sems.at[j - 1], recv_sem=recv_sems.at[j - 1],
            device_id=peer, device_id_type=MESH)

    def start():
        mine().start()
        for j in range(1, 8):
            copy(j, True).start()

    def wait():
        for j in range(1, 8):
            copy(j, False).wait()
        mine().wait()

    return start, wait


def _ride_all8(x):
    return dict(xs=[x], shapes=[jax.ShapeDtypeStruct((8,) + x.shape, x.dtype)],
                sems=[pltpu.SemaphoreType.DMA((7,)), pltpu.SemaphoreType.DMA((7,)), pltpu.SemaphoreType.DMA],
                ops=lambda x_refs, o_refs, sems: _allgather8_ops(x_refs[0], o_refs[0], *sems))


def _ride_chip(xs, gather):
    return dict(xs=list(xs), shapes=_chip_exchange_shapes(xs, gather), sems=_chip_exchange_sems(len(xs)),
                ops=lambda x_refs, o_refs, sems: _chip_exchange_ops(x_refs, o_refs, *sems, gather))


def _allgather8(x, name):
    def body(x_ref, o_ref, *sems):
        start, wait = _allgather8_ops(x_ref, o_ref, *sems)
        start()
        wait()

    ride = _ride_all8(x)
    return pl.pallas_call(body, name=name, in_specs=[ANY], out_specs=ANY, out_shape=ride["shapes"][0],
                          scratch_shapes=ride["sems"])(x)


def _gather_halves_ops(x_ref, o_ref, ici_send, ici_recv, d2d_send, d2d_recv, local_sem):
    half = x_ref.shape[0] // 2
    mx, my, mc = lax.axis_index("x"), lax.axis_index("y"), lax.axis_index("c")
    k0 = 2 * mx + my
    mine = pl.ds(pl.multiple_of(mc * half, 16), half)
    theirs = pl.ds(pl.multiple_of((1 - mc) * half, 16), half)

    def local():
        return pltpu.make_async_copy(x_ref, o_ref.at[k0], local_sem)

    def chips(j):
        px, py = _flip(mx, j & 2), _flip(my, j & 1)
        return px, py, 2 * px + py

    def over_ici(j, outgoing):
        px, py, kp = chips(j)
        dst = o_ref.at[k0, mine] if outgoing else o_ref.at[kp, mine]
        return pltpu.make_async_remote_copy(
            src_ref=x_ref.at[mine], dst_ref=dst, send_sem=ici_send.at[j - 1], recv_sem=ici_recv.at[j - 1],
            device_id=(px, py, mc), device_id_type=MESH)

    def over_d2d(j, outgoing):
        _, _, kp = chips(j)
        rows = mine if outgoing else theirs
        return pltpu.make_async_remote_copy(
            src_ref=o_ref.at[kp, rows], dst_ref=o_ref.at[kp, rows], send_sem=d2d_send.at[j - 1],
            recv_sem=d2d_recv.at[j - 1], device_id=(mx, my, 1 - mc), device_id_type=MESH)

    def start():
        local().start()
        for j in range(1, 4):
            over_ici(j, True).start()

    def wait():
        for j in range(1, 4):
            over_ici(j, False).wait_recv()
            over_d2d(j, True).start()
        for j in range(1, 4):
            over_ici(j, True).wait_send()
            over_d2d(j, True).wait_send()
            over_d2d(j, False).wait_recv()
        local().wait()

    return start, wait


def _ride_halves(x):
    dma3 = pltpu.SemaphoreType.DMA((3,))
    return dict(xs=[x], shapes=[jax.ShapeDtypeStruct((4,) + x.shape, x.dtype)],
                sems=[dma3, dma3, dma3, dma3, pltpu.SemaphoreType.DMA],
                ops=lambda x_refs, o_refs, sems: _gather_halves_ops(x_refs[0], o_refs[0], *sems))


def _chip_exchange(xs, gather, name):
    n = len(xs)

    def body(*refs):
        start, wait = _chip_exchange_ops(refs[:n], refs[n:2 * n], *refs[2 * n:], gather)
        start()
        wait()

    return pl.pallas_call(
        body, name=name, in_specs=[ANY] * n, out_specs=[ANY] * n, out_shape=_chip_exchange_shapes(xs, gather),
        scratch_shapes=_chip_exchange_sems(n),
    )(*xs)


def _chip_exchange_shapes(xs, gather):
    return [jax.ShapeDtypeStruct(((4,) + x.shape) if gather else x.shape, x.dtype) for x in xs]


def _chip_exchange_sems(n):
    return [pltpu.SemaphoreType.DMA((3 * n,)), pltpu.SemaphoreType.DMA((3 * n,)), pltpu.SemaphoreType.DMA((n,))]


def _chip_exchange_ops(x_refs, o_refs, send_sems, recv_sems, local_sems, gather):
    n = len(x_refs)
    mx, my, mc = lax.axis_index("x"), lax.axis_index("y"), lax.axis_index("c")
    k0 = 2 * mx + my

    def local(a):
        src = x_refs[a] if gather else x_refs[a].at[k0]
        return pltpu.make_async_copy(src, o_refs[a].at[k0], local_sems.at[a])

    def copy(a, j, outgoing):
        px, py = _flip(mx, j & 2), _flip(my, j & 1)
        kp = 2 * px + py
        if outgoing:
            src = x_refs[a] if gather else x_refs[a].at[kp]
            dst = o_refs[a].at[k0]
        else:
            src = x_refs[a] if gather else x_refs[a].at[k0]
            dst = o_refs[a].at[kp]
        s = a * 3 + j - 1
        return pltpu.make_async_remote_copy(
            src_ref=src, dst_ref=dst, send_sem=send_sems.at[s], recv_sem=recv_sems.at[s],
            device_id=(px, py, mc), device_id_type=MESH)

    def start():
        for a in range(n):
            local(a).start()
            for j in range(1, 4):
                copy(a, j, True).start()

    def wait():
        for a in range(n):
            for j in range(1, 4):
                copy(a, j, False).wait()
            local(a).wait()

    return start, wait


def _call(body, *, name, grid, in_specs, out_specs, out_shape, args, scratch_shapes=(), sem=None, ride=None):
    if not ride:
        return pl.pallas_call(
            body, name=name, grid=grid, in_specs=list(in_specs), out_specs=list(out_specs), out_shape=list(out_shape),
            scratch_shapes=list(scratch_shapes), compiler_params=_params(sem))(*args)
    xs = [x for r in ride for x in r["xs"]]
    shapes = [s for r in ride for s in r["shapes"]]
    sems = [s for r in ride for s in r["sems"]]
    n_in, n_out, n_scr, nx = len(in_specs), len(out_specs), len(scratch_shapes), len(xs)

    def wrapped(*refs):
        ins, x_refs = refs[:n_in], refs[n_in:n_in + nx]
        outs = refs[n_in + nx:n_in + nx + n_out]
        lands = refs[n_in + nx + n_out:n_in + 2 * nx + n_out]
        rest = refs[n_in + 2 * nx + n_out:]
        scr, sem_refs = rest[:n_scr], rest[n_scr:]
        ops, xo, so = [], 0, 0
        for r in ride:
            nr, ns = len(r["xs"]), len(r["sems"])
            ops.append(r["ops"](x_refs[xo:xo + nr], lands[xo:xo + nr], sem_refs[so:so + ns]))
            xo, so = xo + nr, so + ns
        ids = [pl.program_id(a) for a in range(len(grid))]
        first = functools.reduce(jnp.logical_and, [i == 0 for i in ids])
        last = functools.reduce(jnp.logical_and, [i == g - 1 for i, g in zip(ids, grid)])

        @pl.when(first)
        def _():
            for start, _ in ops:
                start()

        body(*ins, *outs, *scr)

        @pl.when(last)
        def _():
            for _, wait in ops:
                wait()

    return pl.pallas_call(
        wrapped, name=name, grid=grid, in_specs=list(in_specs) + [ANY] * nx, out_specs=list(out_specs) + [ANY] * nx,
        out_shape=list(out_shape) + shapes, scratch_shapes=list(scratch_shapes) + sems,
        compiler_params=_params(("arbitrary",) * len(grid)))(*args, *xs)


def _sibling_exchange(xs, name):
    n = len(xs)

    def body(*refs):
        x_refs, o_refs = refs[:n], refs[n:2 * n]
        send_sems, recv_sems = refs[2 * n:]
        sib = (lax.axis_index("x"), lax.axis_index("y"), 1 - lax.axis_index("c"))
        copies = [pltpu.make_async_remote_copy(
            src_ref=x_refs[a], dst_ref=o_refs[a], send_sem=send_sems.at[a], recv_sem=recv_sems.at[a],
            device_id=sib, device_id_type=MESH) for a in range(n)]
        for cp in copies:
            cp.start()
        for cp in copies:
            cp.wait()

    return pl.pallas_call(
        body, name=name, in_specs=[ANY] * n, out_specs=[ANY] * n,
        out_shape=[jax.ShapeDtypeStruct(x.shape, x.dtype) for x in xs],
        scratch_shapes=[pltpu.SemaphoreType.DMA((n,)), pltpu.SemaphoreType.DMA((n,))],
    )(*xs)


PACK = 1024
PACK_ROWS = 512


def _pack(parts):
    flat = []
    for p in parts:
        v = p.reshape(-1).astype(F32)
        flat.append(jnp.pad(v, (0, (-v.shape[0]) % PACK)))
    total = sum(v.shape[0] for v in flat)
    flat.append(jnp.zeros(((-total) % (PACK_ROWS * 128),), F32))
    return jnp.concatenate(flat).reshape(-1, 128)


def _shard_columns(shards, lo, hi):
    width = shards.shape[2]
    out = []
    for k in range(shards.shape[0]):
        a, b = max(lo, k * width), min(hi, (k + 1) * width)
        if a < b:
            out.append(shards[k, :, a - k * width:b - k * width])
    return out


def _unpack_rows(gathered, shapes):
    flat = gathered.reshape(gathered.shape[0], -1)
    out, off = [], 0
    for shp in shapes:
        n = math.prod(shp)
        out.append(flat[:, off:off + n].reshape((flat.shape[0],) + tuple(shp)))
        off += n + (-n) % PACK
    return out


def _unpack(packed, shapes):
    flat = packed.reshape(-1)
    out, off = [], 0
    for shp in shapes:
        n = math.prod(shp)
        out.append(flat[off:off + n].reshape(shp))
        off += n + (-n) % PACK
    return out


def kernel(x, c, ln_pre_g, ln_post_g, w_mod, b_mod, w_in_ab, w_out_ab, sgu_norm_g, sgu_w, sgu_b, w_in_ssm, w_out_ssm, lam_re, lam_im, b_re, b_im, c_re, c_im, d_skip, log_dt, w_glu, b_glu, loss_target, m_ln_pre_g, m_ln_post_g, m_w_mod, m_b_mod, m_w_in_ab, m_w_out_ab, m_sgu_norm_g, m_sgu_w, m_sgu_b, m_w_in_ssm, m_w_out_ssm, m_lam_re, m_lam_im, m_b_re, m_b_im, m_c_re, m_c_im, m_d_skip, m_log_dt, m_w_glu, m_b_glu, v_ln_pre_g, v_ln_post_g, v_w_mod, v_b_mod, v_w_in_ab, v_w_out_ab, v_sgu_norm_g, v_sgu_w, v_sgu_b, v_w_in_ssm, v_w_out_ssm, v_lam_re, v_lam_im, v_b_re, v_b_im, v_c_re, v_c_im, v_d_skip, v_log_dt, v_w_glu, v_b_glu):
    given = dict(locals())
    mx, my, mc = lax.axis_index("x"), lax.axis_index("y"), lax.axis_index("c")
    me = 4 * mx + 2 * my + mc
    chip = 2 * mx + my

    _, l, d = x.shape
    x2, tgt = x[0], loss_target[0]
    n_in = w_in_ab.shape[2] * 4
    wa = wb = n_in // 7
    w = w_out_ssm.shape[1]
    g, p, cch = b_re.shape[1:]
    nmod = w_mod.shape[2]


    cond = _silu_rows(c.reshape(d // 128, 128), "cond_silu")
    cond_all = _allgather8(cond, "gather_cond").reshape(8, d)
    b_shard = lax.dynamic_slice(b_mod, (0, chip * nmod), (2, nmod)).reshape(2, 1, nmod)
    cond_pad = jnp.pad(cond_all, ((0, MOD_ROWS - 8), (0, 0)))
    modp = _mod_fwd(cond_pad, w_mod, b_shard, "mod_fwd")[:, :8]
    modp_all = _allgather8(modp.reshape(16, nmod), "gather_mod").reshape(4, 2, 2, 8, nmod)
    mine = lax.dynamic_index_in_dim(lax.dynamic_index_in_dim(modp_all, mc, 1, False), me, 2, False)
    mod = mine.transpose(1, 0, 2).reshape(2, 3 * d)
    shift = [mod[a:a + 1, :d] for a in range(2)]
    scale = [mod[a:a + 1, d:2 * d] for a in range(2)]
    gate = [mod[a:a + 1, 2 * d:] for a in range(2)]
    pre_g = [ln_pre_g[a:a + 1] for a in range(2)]
    post_g = [ln_post_g[a:a + 1] for a in range(2)]

    sgu_w0, sgu_bt = sgu_w[0], sgu_b[0].T
    h0, (gw_in_ab,) = _pre_fwd(x2, pre_g[0], scale[0], shift[0], "pre0_fwd", ride=[_ride_halves(_bf(w_in_ab[0]))])
    w_gates = jnp.concatenate(_shard_columns(gw_in_ab, 0, 3 * wa) + _shard_columns(gw_in_ab, 3 * wa + 3 * wb, n_in),
                              axis=1)
    w_qkv = jnp.concatenate(_shard_columns(gw_in_ab, 3 * wa, 3 * wa + 3 * wb), axis=1)
    proj0, (gw_in_ssm,) = _matmul(h0, w_gates, "nn", F32, "proj0", tm=1024, ride=[_ride_chip([_bf(w_in_ssm[0])], True)])
    qkv, (gw_out_ssm, gw_glu, g_dskip, g_bglu) = _matmul(
        h0, w_qkv, "nn", BF16, "proj0_qkv", tm=1024,
        ride=[_ride_chip([_bf(w_out_ssm[0]), _bf(w_glu[0]), d_skip, b_glu], True)])
    out_b, (gw_out_ab,) = _attn_fwd(qkv, wb, "attn_fwd", hp=8, ride=[_ride_chip([_bf(w_out_ab[0])], True)])
    wout_ab = gw_out_ab.reshape(wa + wb, d)
    win_ssm = gw_in_ssm.reshape(d, 2 * w)
    wout_ssm = jnp.concatenate([gw_out_ssm[k] for k in range(4)], axis=1)
    wglu = gw_glu.reshape(w, w)
    dskip_full = g_dskip.reshape(1, w)
    bglu_full = g_bglu.reshape(1, w)
    cat =_sgu_fwd(proj0, out_b, sgu_norm_g, sgu_w0, sgu_bt, wa, wb, "sgu_fwd")
    y0 = _matmul(cat, wout_ab, "nn", BF16, "out0", tm=1024)
    x1, h1 = _post_pre_fwd(x2, y0, gate[0], post_g[0], pre_g[1], scale[1], shift[1], "post0_pre1_fwd")

    s = g * p
    lr_c, li_c = lam_re.reshape(s, 1), lam_im.reshape(s, 1)
    ldt_c = jnp.repeat(log_dt.reshape(g), p).reshape(s, 1)
    br_c, bi_c = b_re.reshape(s, cch), b_im.reshape(s, cch)
    bb_re, bb_im, pw_re, pw_im = _ssm_prep(lr_c, li_c, ldt_c, br_c, bi_c, lr_c.reshape(1, s), li_c.reshape(1, s),
                                           ldt_c.reshape(1, s), "ssm_prep")
    bbd = _bf(_block_diag_b(bb_re, bb_im, g, p, cch))
    ccd = _bf(_block_diag_c(c_re[0], c_im[0], g, p, cch))
    proj1 = _matmul(h1, win_ssm, "nn", F32, "proj1", tm=1024)
    y_ssm, hs_re, hs_im, h_all = _ssm_fwd(proj1, bbd, ccd, pw_re, pw_im, dskip_full, w, "ssm_fwd")
    o1 = _glu_fwd(y_ssm, proj1, wglu, bglu_full, "glu_fwd")
    y1 = _matmul(o1, wout_ssm, "nn", BF16, "out1", tm=1024)
    loss_vec, dy1, dx2, dgate1, dpost1 = _post_loss(x1, y1, gate[1], post_g[1], tgt, "post1_loss")

    do1 = _matmul(dy1, wout_ssm, "nt", BF16, "out1_dx", tm=1024)
    gr_wout_ssm = _matmul(o1, dy1, "tn", BF16, "out1_dw", tm=1024, tk=1024, n_split=4)
    dy_ssm, dz1, gr_wglu, gr_bglu = _glu_bwd(do1, y_ssm, proj1, wglu, bglu_full, "glu_bwd")
    du1, dbbd, dccd, da_re, da_im, gr_dskip, (ld_wout_ssm, ld_wglu) = _ssm_bwd(
        proj1, dy_ssm, hs_re, hs_im, h_all, bbd, ccd, pw_re, pw_im, dskip_full, w, "ssm_bwd",
        ride=[_ride_chip([gr_wout_ssm, gr_wglu.reshape(4, w // 4, w)], False)])
    dproj1 = jnp.concatenate([du1, dz1], axis=1)
    dh1 = _matmul(dproj1, win_ssm, "nt", BF16, "proj1_dx", tm=1024)
    gr_win_ssm = _matmul(h1, dproj1, "tn", BF16, "proj1_dw", tm=1024, tn=1024, tk=1024)
    dx1, dscale1, dshift1, dpre1, dy0, dgate0, dpost0 = _pre_bwd(
        dh1, dx2, x1, pre_g[1], scale[1], "pre1_post0_bwd", post=(y0, gate[0], post_g[0]))

    dcat = _matmul(dy0, wout_ab, "nt", BF16, "out0_dx", tm=1024)
    gr_wout_ab = _matmul(cat, dy0, "tn", BF16, "out0_dw", tm=1024, tn=1024, tk=1024)
    dbb_re, dbb_im = _diag_of_b(dbbd, g, p, cch)
    dc_re, dc_im = _diag_of_c(dccd, g, p, cch)
    part_a = [loss_vec[:, :1], dpre1, dpost0, dpost1, dgate0, dshift1, dscale1, dgate1, da_re, da_im,
              dbb_re, dbb_im, dc_re, dc_im, gr_dskip, gr_bglu]
    shapes_a = [a.shape for a in part_a]
    dq, dk, dv, (ld_win_ssm, gath_a) = _attn_bwd(
        qkv, proj0, dcat, wa, wb, "attn_bwd", hp=4,
        ride=[_ride_chip([gr_win_ssm.reshape(4, d // 4, 2 * w)], False), _ride_all8(_pack(part_a))])
    dproj0, gr_sgu_w, gr_sgu_bt, gr_sgu_g, (ld_wout_ab,) = _sgu_bwd(
        proj0, out_b, dcat, dq, dk, dv, sgu_norm_g, sgu_w0, sgu_bt, wa, wb, "sgu_bwd",
        ride=[_ride_chip([gr_wout_ab.reshape(4, (wa + wb) // 4, d)], False)])
    part_b = [gr_sgu_g, gr_sgu_w, gr_sgu_bt.T]
    shapes_b = [a.shape for a in part_b]
    gr_win_ab_lo, (gath_b,) = _matmul(h0, dproj0, "tn", BF16, "proj0_dw_lo", tm=1024, tk=1024, tn=896, n_split=4,
                                      m_part=(0, 1, 2), ride=[_ride_all8(_pack(part_b))])
    gr_win_ab_hi, (ld_win_ab_lo,) = _matmul(
        h0, dproj0, "tn", BF16, "proj0_dw_hi", tm=1024, tk=1024, tn=896, n_split=4, m_part=(1, 1, 2),
        ride=[_ride_chip([gr_win_ab_lo], False)])
    dh0, (ld_win_ab_hi,) = _matmul(dproj0, gw_in_ab, "nt", BF16, "proj0_dx", tm=1024, tn=1024,
                                   ride=[_ride_chip([gr_win_ab_hi], False)])
    grad_x, dscale0, dshift0, dpre0 = _pre_bwd(dh0, dx1, x2, pre_g[0], scale[0], "pre0_bwd")
    part_c = [dpre0, dshift0, dscale0]
    shapes_c = [a.shape for a in part_c]
    gath_c = _allgather8(_pack(part_c), "gather_small_tail")

    landed = [ld_wout_ab, ld_win_ssm, ld_wout_ssm, ld_wglu]
    big_names = ["w_in_ab", "w_out_ab", "w_in_ssm", "w_out_ssm", "w_glu"]
    sums = [jnp.concatenate([_sum_leading(ld_win_ab_lo, "sum_w_in_ab_lo"), _sum_leading(ld_win_ab_hi, "sum_w_in_ab_hi")],
                            axis=0)]
    sums += [_sum_leading(a, "sum_" + nm) for a, nm in zip(landed, big_names[1:])]
    sib = _sibling_exchange(sums, "sibling_grads")
    results = {}
    for nm, s_mine, s_sib in zip(big_names, sums, sib):
        shp = given[nm].shape
        two_d = lambda a: a.reshape(-1, shp[-1])
        outs = _adamw(two_d(given[nm]), [s_mine, s_sib], two_d(given["m_" + nm]), two_d(given["v_" + nm]),
                      "adamw_" + nm)
        results[nm] = [o.reshape(shp) for o in outs]

    (loss_s, g_pre1, g_post0, g_post1, g_gate0, g_shift1, g_scale1, g_gate1, s_da_re, s_da_im, s_dbb_re, s_dbb_im,
     g_c_re, g_c_im, g_dskip_full, g_bglu_full) = _unpack(_sum_leading(gath_a, "sum_small_a"), shapes_a)
    g_sgu_g, g_sgu_w, g_sgu_b = _unpack(_sum_leading(gath_b, "sum_small_b"), shapes_b)
    g_pre0, g_shift0, g_scale0 = _unpack(_sum_leading(gath_c, "sum_small_c"), shapes_c)
    loss = loss_s.reshape(())
    g_pre = jnp.concatenate([g_pre0, g_pre1], axis=0)
    g_post = jnp.concatenate([g_post0, g_post1], axis=0)
    g_bmod = jnp.concatenate([jnp.concatenate([g_shift0, g_scale0, g_gate0], axis=1),
                              jnp.concatenate([g_shift1, g_scale1, g_gate1], axis=1)], axis=0)

    g_lr, g_li, g_ldt, g_br, g_bi = _ssm_prep_bwd(lr_c, li_c, ldt_c, br_c, bi_c, s_da_re.reshape(s, 1),
                                                  s_da_im.reshape(s, 1), s_dbb_re, s_dbb_im, p, "ssm_prep_bwd")
    small = {
        "ln_pre_g": g_pre, "ln_post_g": g_post, "b_mod": g_bmod, "sgu_norm_g": g_sgu_g,
        "sgu_w": g_sgu_w.reshape(sgu_w.shape), "sgu_b": g_sgu_b.reshape(sgu_b.shape),
        "lam_re": g_lr.reshape(lam_re.shape), "lam_im": g_li.reshape(lam_im.shape),
        "b_re": g_br.reshape(b_re.shape), "b_im": g_bi.reshape(b_im.shape),
        "c_re": g_c_re.reshape(c_re.shape), "c_im": g_c_im.reshape(c_im.shape),
        "d_skip": lax.dynamic_slice(g_dskip_full, (0, chip * (w // 4)), (1, w // 4)),
        "log_dt": g_ldt.reshape(log_dt.shape),
        "b_glu": lax.dynamic_slice(g_bglu_full, (0, chip * (w // 4)), (1, w // 4)),
    }
    flat2 = lambda a: a.reshape(-1, a.shape[-1])
    wide = ("b_re", "b_im")
    for tag, group in (("adamw_small", [nm for nm in small if nm not in wide]), ("adamw_small_b", list(wide))):
        outs = _adamw_many([flat2(given[nm]) for nm in group], [flat2(small[nm]) for nm in group],
                           [flat2(given["m_" + nm]) for nm in group], [flat2(given["v_" + nm]) for nm in group], tag)
        for nm, trio in zip(group, outs):
            results[nm] = [small[nm]] + [o.reshape(given[nm].shape) for o in trio]

    rows_a = _unpack_rows(gath_a, shapes_a)
    rows_c = _unpack_rows(gath_c, shapes_c)
    dmod_rows = jnp.concatenate([rows_c[1], rows_c[2], rows_a[4], rows_a[5], rows_a[6], rows_a[7]],
                                axis=2).reshape(8, 2, 3 * d)
    dmod_shard = lax.dynamic_slice(dmod_rows, (0, 0, chip * nmod), (8, 2, nmod)).transpose(1, 0, 2)
    dmod_pad = jnp.pad(dmod_shard, ((0, 0), (0, MOD_ROWS - 8), (0, 0)))
    gr_wmod = _mod_bwd(cond_pad.T, dmod_pad, "mod_bwd")
    two_d = lambda a: a.reshape(-1, nmod)
    outs = _adamw(two_d(w_mod), [two_d(gr_wmod)], two_d(m_w_mod), two_d(v_w_mod), "adamw_w_mod")
    results["w_mod"] = [o.reshape(w_mod.shape) for o in outs]

    names = ["ln_pre_g", "ln_post_g", "w_mod", "b_mod", "w_in_ab", "w_out_ab", "sgu_norm_g", "sgu_w", "sgu_b",
             "w_in_ssm", "w_out_ssm", "lam_re", "lam_im", "b_re", "b_im", "c_re", "c_im", "d_skip", "log_dt",
             "w_glu", "b_glu"]
    return (loss, grad_x[None], *[results[nm][0] for nm in names], *[results[nm][1] for nm in names],
            *[results[nm][2] for nm in names], *[results[nm][3] for nm in names])
```

```python
import functools
import math

import jax
import jax.numpy as jnp
from jax import lax
from jax.experimental import pallas as pl
from jax.experimental.pallas import tpu as pltpu

F32 = jnp.float32
BF16 = jnp.bfloat16
MESH = pl.DeviceIdType.MESH

EPS = 1e-6
HEAD = 128
SSM_T = 512
SSM_GB = 16
ADAM_LR, ADAM_B1, ADAM_B2, ADAM_EPS, ADAM_WD, ADAM_STEP = 0.001, 0.9, 0.999, 1e-08, 0.01, 10
VMEM_LIMIT = 56 * 1024 * 1024

NN = (((1,), (0,)), ((), ()))
NT = (((1,), (1,)), ((), ()))
TN = (((0,), (0,)), ((), ()))


def _params(sem=None):
    return pltpu.CompilerParams(dimension_semantics=sem, vmem_limit_bytes=VMEM_LIMIT)


def _dot(a, b, dims=NN):
    return lax.dot_general(a, b, dims, preferred_element_type=F32)


def _bf(x):
    return x.astype(BF16)


def _gelu(x):
    k = math.sqrt(2.0 / math.pi)
    t = jnp.tanh(k * (x + 0.044715 * x * x * x))
    return 0.5 * x * (1.0 + t)


def _gelu_grad(x):
    k = math.sqrt(2.0 / math.pi)
    x2 = x * x
    t = jnp.tanh(k * (x + 0.044715 * x * x2))
    return 0.5 * (1.0 + t) + 0.5 * x * (1.0 - t * t) * k * (1.0 + 3.0 * 0.044715 * x2)


def _sigmoid(x):
    return 1.0 / (1.0 + jnp.exp(-x))


def _silu(x):
    return x * _sigmoid(x)


def _silu_grad(x):
    s = _sigmoid(x)
    return s * (1.0 + x * (1.0 - s))


def _tile(n, t, mult=128):
    if n <= t:
        return n
    for cand in range(t - t % mult, 0, -mult):
        if n % cand == 0:
            return cand
    raise ValueError((n, t, mult))


def _matmul(a, b, mode, out_dtype, name, tm=512, tn=512, tk=2048, n_split=1, ride=None, m_part=None):
    b_sharded = b.ndim == 3
    if mode == "nn":
        (m, kk), (_, n) = a.shape, b.shape
    elif b_sharded:
        assert mode == "nt"
        (m, kk), n, tk = a.shape, b.shape[1], b.shape[2]
    elif mode == "nt":
        (m, kk), (n, _) = a.shape, b.shape
    else:
        (kk, m), (_, n) = a.shape, b.shape
    m_off = 0
    if m_part is not None:
        assert mode == "tn"
        first, count, parts = m_part
        tm = _tile(m // parts, tm)
        m_off = first * (m // parts) // tm
        m = count * (m // parts)
    tm, tk = _tile(m, tm), _tile(kk, tk)
    ns = n // n_split
    tn = _tile(ns, tn)
    nk = kk // tk
    dims = {"nn": NN, "nt": NT, "tn": TN}[mode]

    def body(a_ref, b_ref, o_ref, acc_ref):
        k = pl.program_id(2)
        part = _dot(_bf(a_ref[...]), _bf(b_ref[0] if b_sharded else b_ref[...]), dims)

        @pl.when(k == 0)
        def _():
            acc_ref[...] = part

        @pl.when(k > 0)
        def _():
            acc_ref[...] += part

        @pl.when(k == nk - 1)
        def _():
            o_ref[...] = acc_ref[...].astype(out_dtype).reshape(o_ref.shape)

    if mode == "nn":
        a_spec = pl.BlockSpec((tm, tk), lambda i, j, k: (i, k))
        b_spec = pl.BlockSpec((tk, tn), lambda i, j, k: (k, j))
    elif mode == "nt":
        a_spec = pl.BlockSpec((tm, tk), lambda i, j, k: (i, k))
        b_spec = (pl.BlockSpec((1, tn, tk), lambda i, j, k: (k, j, 0)) if b_sharded
                  else pl.BlockSpec((tn, tk), lambda i, j, k: (j, k)))
    else:
        a_spec = pl.BlockSpec((tk, tm), lambda i, j, k: (k, i + m_off))
        b_spec = pl.BlockSpec((tk, tn), lambda i, j, k: (k, j))
    if n_split == 1:
        out_shape = jax.ShapeDtypeStruct((m, n), out_dtype)
        o_spec = pl.BlockSpec((tm, tn), lambda i, j, k: (i, j))
    else:
        per = ns // tn
        out_shape = jax.ShapeDtypeStruct((n_split, m, ns), out_dtype)
        o_spec = pl.BlockSpec((1, tm, tn), lambda i, j, k: (j // per, i, j % per))
    outs = _call(body, name=name, grid=(m // tm, n // tn, nk), in_specs=[a_spec, b_spec], out_specs=[o_spec],
                 out_shape=[out_shape], scratch_shapes=[pltpu.VMEM((tm, tn), F32)], args=(a, b),
                 sem=("parallel", "parallel", "arbitrary"), ride=ride)
    return outs[0] if ride is None else (outs[0], outs[1:])


def _row_spec(tm, d):
    return pl.BlockSpec((tm, d), lambda i: (i, 0))


def _vec_spec(d):
    return pl.BlockSpec((1, d), lambda i: (0, 0))


def _acc(ref, first, val):
    @pl.when(first)
    def _():
        ref[...] = val

    @pl.when(jnp.logical_not(first))
    def _():
        ref[...] += val


def _colsum(x):
    return jnp.sum(x, axis=0, keepdims=True)


def _rownorm(x):
    r = lax.rsqrt(jnp.mean(x * x, axis=-1, keepdims=True) + EPS)
    return x * r, r


def _pre_fwd(x, g, scale, shift, name, ride=None):
    l, d = x.shape
    tm = _tile(l, 256)

    def body(x_ref, g_ref, sc_ref, sh_ref, h_ref):
        n, _ = _rownorm(x_ref[...])
        h_ref[...] = _bf(n * g_ref[...] * (1.0 + sc_ref[...]) + sh_ref[...])

    outs = _call(body, name=name, grid=(l // tm,), in_specs=[_row_spec(tm, d), _vec_spec(d), _vec_spec(d), _vec_spec(d)],
                 out_specs=[_row_spec(tm, d)], out_shape=[jax.ShapeDtypeStruct((l, d), BF16)],
                 args=(x, g, scale, shift), sem=("parallel",), ride=ride)
    return outs[0], outs[1:]


def _post_pre_fwd(x, y, gate, pg, g1, scale1, shift1, name):
    l, d = x.shape
    tm = _tile(l, 256)

    def body(x_ref, y_ref, gate_ref, pg_ref, g1_ref, sc_ref, sh_ref, x1_ref, h1_ref):
        ny, _ = _rownorm(y_ref[...].astype(F32))
        x1 = x_ref[...] + gate_ref[...] * (ny * pg_ref[...])
        x1_ref[...] = x1
        n1, _ = _rownorm(x1)
        h1_ref[...] = _bf(n1 * g1_ref[...] * (1.0 + sc_ref[...]) + sh_ref[...])

    v = _vec_spec(d)
    return pl.pallas_call(
        body, name=name, grid=(l // tm,),
        in_specs=[_row_spec(tm, d), _row_spec(tm, d), v, v, v, v, v],
        out_specs=[_row_spec(tm, d), _row_spec(tm, d)],
        out_shape=[jax.ShapeDtypeStruct((l, d), F32), jax.ShapeDtypeStruct((l, d), BF16)],
        compiler_params=_params(("parallel",)),
    )(x, y, gate, pg, g1, scale1, shift1)


def _post_loss(x1, y1, gate, pg, target, name):
    l, d = x1.shape
    tm = _tile(l, 256)

    def body(x_ref, y_ref, gate_ref, pg_ref, t_ref, loss_ref, dy_ref, dx_ref, dgate_ref, dpg_ref):
        first = pl.program_id(0) == 0
        ny, ry = _rownorm(y_ref[...].astype(F32))
        q = ny * pg_ref[...]
        x2 = x_ref[...] + gate_ref[...] * q
        e = x2 - t_ref[...]
        _acc(loss_ref, first, jnp.full((1, 128), 0.5 / d, F32) * jnp.sum(e * e))
        dx2 = e * (1.0 / d)
        dx_ref[...] = dx2
        _acc(dgate_ref, first, _colsum(dx2 * q))
        dq = dx2 * gate_ref[...]
        _acc(dpg_ref, first, _colsum(dq * ny))
        dny = dq * pg_ref[...]
        dy = ry * (dny - ny * jnp.mean(dny * ny, axis=-1, keepdims=True))
        dy_ref[...] = _bf(dy)

    v = _vec_spec(d)
    return pl.pallas_call(
        body, name=name, grid=(l // tm,),
        in_specs=[_row_spec(tm, d), _row_spec(tm, d), v, v, _row_spec(tm, d)],
        out_specs=[_vec_spec(128), _row_spec(tm, d), _row_spec(tm, d), v, v],
        out_shape=[jax.ShapeDtypeStruct((1, 128), F32), jax.ShapeDtypeStruct((l, d), BF16),
                   jax.ShapeDtypeStruct((l, d), F32), jax.ShapeDtypeStruct((1, d), F32),
                   jax.ShapeDtypeStruct((1, d), F32)],
        compiler_params=_params(("arbitrary",)),
    )(x1, y1, gate, pg, target)


def _pre_bwd(dh, dres, x, g, scale, name, post=None):
    l, d = x.shape
    tm = _tile(l, 256)
    with_post = post is not None

    def body(*refs):
        if with_post:
            (dh_ref, dres_ref, x_ref, g_ref, sc_ref, y_ref, gate_ref, pg_ref,
             dx_ref, dsc_ref, dsh_ref, dg_ref, dy_ref, dgate_ref, dpg_ref) = refs
        else:
            dh_ref, dres_ref, x_ref, g_ref, sc_ref, dx_ref, dsc_ref, dsh_ref, dg_ref = refs
        first = pl.program_id(0) == 0
        dh = dh_ref[...].astype(F32)
        n, r = _rownorm(x_ref[...])
        _acc(dsc_ref, first, _colsum(dh * (n * g_ref[...])))
        _acc(dsh_ref, first, _colsum(dh))
        dyn = dh * (1.0 + sc_ref[...])
        _acc(dg_ref, first, _colsum(dyn * n))
        dn = dyn * g_ref[...]
        dx = dres_ref[...] + r * (dn - n * jnp.mean(dn * n, axis=-1, keepdims=True))
        dx_ref[...] = dx
        if with_post:
            ny, ry = _rownorm(y_ref[...].astype(F32))
            _acc(dgate_ref, first, _colsum(dx * (ny * pg_ref[...])))
            dq = dx * gate_ref[...]
            _acc(dpg_ref, first, _colsum(dq * ny))
            dny = dq * pg_ref[...]
            dy_ref[...] = _bf(ry * (dny - ny * jnp.mean(dny * ny, axis=-1, keepdims=True)))

    v = _vec_spec(d)
    row = _row_spec(tm, d)
    vec_out = jax.ShapeDtypeStruct((1, d), F32)
    in_specs = [row, row, row, v, v]
    args = [dh, dres, x, g, scale]
    out_specs = [row, v, v, v]
    out_shape = [jax.ShapeDtypeStruct((l, d), F32), vec_out, vec_out, vec_out]
    if with_post:
        in_specs += [row, v, v]
        args += list(post)
        out_specs += [row, v, v]
        out_shape += [jax.ShapeDtypeStruct((l, d), BF16), vec_out, vec_out]
    return pl.pallas_call(
        body, name=name, grid=(l // tm,), in_specs=in_specs, out_specs=out_specs, out_shape=out_shape,
        compiler_params=_params(("arbitrary",)),
    )(*args)


def _softplus_parts(z):
    e = jnp.exp(-jnp.abs(z))
    den = 1.0 + e
    lb = jnp.minimum(z, 0.0) - jnp.log(den)
    return lb, lb - z, jnp.exp(lb)


def _tri(cmp, n=HEAD):
    row = lax.broadcasted_iota(jnp.int32, (n, n), 0)
    col = lax.broadcasted_iota(jnp.int32, (n, n), 1)
    return cmp(row, col)


ATT_T = 256
ATT_DEAD = 104.0


def _any_alive(runs):
    return functools.reduce(jnp.maximum, [jnp.max(r) for r in runs]) > -ATT_DEAD


def _attn_fwd(qkv, wb, name, hp=4, ride=None):
    l = qkv.shape[0]
    t = ATT_T
    nh, nq = wb // HEAD, l // t
    hp = min(hp, nh)
    ng, wg = nh // hp, hp * HEAD
    scale = 1.0 / math.sqrt(HEAD)

    def body(q_ref, k_ref, v_ref, o_ref):
        i = pl.program_id(1)
        valid = _tri(lambda r, c: c < r, t)
        m_gt = _bf(_tri(lambda r, c: r > c, t).astype(F32))

        def tile(j, carry, diag):
            rows = pl.ds(pl.multiple_of(j * t, t), t)
            cols = [slice(hh * HEAD, (hh + 1) * HEAD) for hh in range(hp)]
            zs = [_dot(q_ref[:, cs], k_ref[rows, cs], NT) * scale for cs in cols]
            lbs, lks = [], []
            for z in zs:
                lb, lk, _ = _softplus_parts(z)
                lbs.append(lb)
                lks.append(jnp.where(valid, lk, 0.0) if diag else lk)
            laters = [_dot(_bf(lk), m_gt) for lk in lks]
            ws = [jnp.exp(lb + later + run) for lb, later, (_, run) in zip(lbs, laters, carry)]
            if diag:
                ws = [jnp.where(valid, w, 0.0) for w in ws]
            return tuple((acc + _dot(_bf(w), v_ref[rows, cs]), run + jnp.sum(lk, axis=1, keepdims=True))
                         for w, lk, cs, (acc, run) in zip(ws, lks, cols, carry))

        zero = (jnp.zeros((t, HEAD), F32), jnp.zeros((t, 1), F32))
        carry = tile(i, (zero,) * hp, True)
        _, carry = lax.while_loop(lambda c: (c[0] < i) & _any_alive([run for _, run in c[1]]),
                                  lambda c: (c[0] + 1, tile(i - 1 - c[0], c[1], False)), (jnp.int32(0), carry))
        for hh, (acc, _) in enumerate(carry):
            o_ref[:, hh * HEAD:(hh + 1) * HEAD] = acc

    blk = lambda off: pl.BlockSpec((t, wg), lambda h, i: (i, off + h))
    full = lambda off: pl.BlockSpec((l, wg), lambda h, i: (0, off + h))
    out = pl.BlockSpec((t, wg), lambda h, i: (i, h))
    outs = _call(body, name=name, grid=(ng, nq), in_specs=[blk(0), full(ng), full(2 * ng)], out_specs=[out],
                 out_shape=[jax.ShapeDtypeStruct((l, wb), F32)],
                 args=(qkv, qkv, qkv), sem=("parallel", "arbitrary"), ride=ride)
    return outs[0], outs[1:]


def _attn_bwd(qkv, proj, dcat, wa, wb, name, hp=2, ride=None):
    l = qkv.shape[0]
    t = ATT_T
    nh, nq = wb // HEAD, l // t
    hp = min(hp, nh)
    ng, wg = nh // hp, hp * HEAD
    scale = 1.0 / math.sqrt(HEAD)

    def body(q_ref, k_ref, v_ref, bz_ref, dc_ref, dq_ref, dkt_out, dvt_out, do_s, qt_s, dot_s,
             dkt_ref, dvt_ref, out_sems):
        i = pl.program_id(1)

        @pl.when(i == 0)
        def _():
            dkt_ref[...] = jnp.zeros_like(dkt_ref)
            dvt_ref[...] = jnp.zeros_like(dvt_ref)

        do = dc_ref[...].astype(F32) * _silu(bz_ref[...])
        do_s[...] = _bf(do)
        for hh in range(hp):
            cs = slice(hh * HEAD, (hh + 1) * HEAD)
            qt_s[hh] = _bf(q_ref[:, cs].astype(F32).T * scale)
            dot_s[hh] = _bf(do[:, cs].T)
        valid = _tri(lambda r, c: c < r, t)
        m_le = _bf(_tri(lambda r, c: r <= c, t).astype(F32))
        m_lt = _bf(_tri(lambda r, c: r < c, t).astype(F32))

        heads = range(hp)
        cols = [slice(hh * HEAD, (hh + 1) * HEAD) for hh in heads]

        def row_sums(j, runs, diag):
            rows = pl.ds(pl.multiple_of(j * t, t), t)
            out = []
            for cs, run in zip(cols, runs):
                _, lk, _ = _softplus_parts(_dot(q_ref[:, cs], k_ref[rows, cs], NT) * scale)
                if diag:
                    lk = jnp.where(valid, lk, 0.0)
                out.append(run + jnp.sum(lk, axis=1, keepdims=True))
            return tuple(out)

        runs = row_sums(i, (jnp.zeros((t, 1), F32),) * hp, True)
        below, lktot = lax.while_loop(lambda c: (c[0] < i) & _any_alive(c[1]),
                                      lambda c: (c[0] + 1, row_sums(i - 1 - c[0], c[1], False)), (jnp.int32(0), runs))

        def tile(j, carry, diag):
            rows = pl.ds(pl.multiple_of(j * t, t), t)
            zs = [_dot(q_ref[:, cs], k_ref[rows, cs], NT) * scale for cs in cols]
            dws = [_dot(do_s[:, cs], v_ref[rows, cs], NT) for cs in cols]
            lbs, lks, sigs = [], [], []
            for z in zs:
                lb, lk, sig = _softplus_parts(z)
                lbs.append(lb)
                lks.append(jnp.where(valid, lk, 0.0) if diag else lk)
                sigs.append(sig)
            pins = [_dot(_bf(lk), m_le) for lk in lks]
            ws = [jnp.exp(lbs[hh] + (lktot[hh] - carry[hh][1]) - pins[hh]) for hh in heads]
            if diag:
                ws = [jnp.where(valid, w, 0.0) for w in ws]
            das = [dw * w for dw, w in zip(dws, ws)]
            pexs = [_dot(_bf(da), m_lt) for da in das]
            dzs = [das[hh] - sigs[hh] * (das[hh] + carry[hh][2] + pexs[hh]) for hh in heads]
            if diag:
                dzs = [jnp.where(valid, dz, 0.0) for dz in dzs]
            dzs = [_bf(dz) for dz in dzs]
            out = []
            for hh in heads:
                dkt, dvt = _dot(qt_s[hh], dzs[hh]), _dot(dot_s[hh], _bf(ws[hh]))
                for half in range(t // HEAD):
                    dkt_ref[hh, sub * j + half] += dkt[:, half * HEAD:(half + 1) * HEAD]
                    dvt_ref[hh, sub * j + half] += dvt[:, half * HEAD:(half + 1) * HEAD]
                dq, cpre, ppre = carry[hh]
                out.append((dq + _dot(dzs[hh], k_ref[rows, cols[hh]]), cpre + jnp.sum(lks[hh], axis=1, keepdims=True),
                            ppre + pexs[hh][:, t - 1:] + das[hh][:, t - 1:]))
            return tuple(out)

        zero = (jnp.zeros((t, HEAD), F32), jnp.zeros((t, 1), F32), jnp.zeros((t, 1), F32))
        carry = lax.fori_loop(i - below, i, lambda j, c: tile(j, c, False), (zero,) * hp)
        carry = tile(i, carry, True)
        for hh in range(hp):
            dq_ref[:, hh * HEAD:(hh + 1) * HEAD] = carry[hh][0] * scale

        @pl.when(i == nq - 1)
        def _():
            heads = pl.ds(pl.program_id(0) * hp, hp)
            copies = [pltpu.make_async_copy(dkt_ref, dkt_out.at[heads], out_sems.at[0]),
                      pltpu.make_async_copy(dvt_ref, dvt_out.at[heads], out_sems.at[1])]
            for cp in copies:
                cp.start()
            for cp in copies:
                cp.wait()

    sub = t // HEAD
    blk = lambda off: pl.BlockSpec((t, wg), lambda h, i: (i, off + h))
    full = lambda off: pl.BlockSpec((l, wg), lambda h, i: (0, off + h))
    acc_shape = jax.ShapeDtypeStruct((nh, l // HEAD, HEAD, HEAD), F32)
    acc_scratch = pltpu.VMEM((hp, l // HEAD, HEAD, HEAD), F32)
    outs = _call(
        body, name=name, grid=(ng, nq),
        in_specs=[blk(0), full(ng), full(2 * ng), blk(3 * wa // wg), blk(wa // wg)],
        out_specs=[blk(0), ANY, ANY], out_shape=[jax.ShapeDtypeStruct((l, wb), F32), acc_shape, acc_shape],
        scratch_shapes=[pltpu.VMEM((t, wg), BF16), pltpu.VMEM((hp, HEAD, t), BF16), pltpu.VMEM((hp, HEAD, t), BF16),
                        acc_scratch, acc_scratch, pltpu.SemaphoreType.DMA((2,))],
        args=(qkv, qkv, qkv, proj, dcat), sem=("parallel", "arbitrary"), ride=ride)
    return outs[0], outs[1], outs[2], outs[3:]


def _sgu_heads(v, g_ref, w_ref, bt_ref, nh):
    keep = _tri(lambda r, c: r >= c)
    out = []
    for h in range(nh):
        cols = slice(h * HEAD, (h + 1) * HEAD)
        nv, r = _rownorm(v[:, cols])
        wm = jnp.where(keep, w_ref[h], 0.0)
        s = _dot(_bf(wm), _bf(nv * g_ref[:, cols])) + bt_ref[:, h:h + 1]
        out.append((nv, r, wm, s))
    return out


def _sgu_fwd(proj, out_b, norm_g, sgu_w, sgu_bt, wa, wb, name):
    l, n = proj.shape
    nh = wa // HEAD

    def body(au_ref, av_ref, az_ref, bz_ref, ob_ref, g_ref, w_ref, bt_ref, cat_ref):
        u, v, sz = _gelu(au_ref[...]), _gelu(av_ref[...]), _silu(az_ref[...])
        for h, (_, _, _, s) in enumerate(_sgu_heads(v, g_ref, w_ref, bt_ref, nh)):
            cols = slice(h * HEAD, (h + 1) * HEAD)
            cat_ref[:, cols] = _bf(u[:, cols] * s * sz[:, cols])
        cat_ref[:, wa:] = _bf(ob_ref[...] * _silu(bz_ref[...]))

    a_blk = lambda j: pl.BlockSpec((HEAD, wa), lambda i: (i, j))
    return pl.pallas_call(
        body, name=name, grid=(l // HEAD,),
        in_specs=[a_blk(0), a_blk(1), a_blk(2), a_blk(3), pl.BlockSpec((HEAD, wb), lambda i: (i, 0)),
                  _vec_spec(wa), pl.BlockSpec((nh, HEAD, HEAD), lambda i: (0, 0, 0)),
                  pl.BlockSpec((HEAD, nh), lambda i: (0, 0))],
        out_specs=pl.BlockSpec((HEAD, wa + wb), lambda i: (i, 0)),
        out_shape=jax.ShapeDtypeStruct((l, wa + wb), BF16),
        compiler_params=_params(("parallel",)),
    )(proj, proj, proj, proj, out_b, norm_g, sgu_w, sgu_bt)


def _sgu_bwd(proj, out_b, dcat, dq, dk, dv, norm_g, sgu_w, sgu_bt, wa, wb, name, ride=None):
    l = proj.shape[0]
    n = 3 * wa + 4 * wb
    nh = wa // HEAD

    def body(au_ref, av_ref, az_ref, bz_ref, ob_ref, dc_ref, dq_ref, dk_ref, dv_ref, g_ref, w_ref, wt_ref, bt_ref,
             dp_ref, dw_ref, dbt_ref, dg_ref):
        first = pl.program_id(0) == 0
        keep = _tri(lambda r, c: r >= c)
        au, av, az = au_ref[...], av_ref[...], az_ref[...]
        u, v, sz = _gelu(au), _gelu(av), _silu(az)
        dgelu_u, dgelu_v, dsilu_z = _gelu_grad(au), _gelu_grad(av), _silu_grad(az)
        heads = _sgu_heads(v, g_ref, w_ref, bt_ref, nh)
        cols = [slice(h * HEAD, (h + 1) * HEAD) for h in range(nh)]
        dss = []
        for h, (nv, r, wm, s) in enumerate(heads):
            dca, uh, szh = dc_ref[:, cols[h]].astype(F32), u[:, cols[h]], sz[:, cols[h]]
            dp_ref[:, cols[h]] = _bf(dca * s * szh * dgelu_u[:, cols[h]])
            dp_ref[:, 2 * wa + h * HEAD:2 * wa + (h + 1) * HEAD] = _bf(dca * uh * s * dsilu_z[:, cols[h]])
            dss.append(dca * uh * szh)
        dws = [_dot(_bf(ds), _bf(nv * g_ref[:, cs]), NT) for ds, cs, (nv, _, _, _) in zip(dss, cols, heads)]
        keep_t = _tri(lambda r, c: r <= c)
        dvhs = [_dot(_bf(jnp.where(keep_t, wt_ref[h], 0.0)), _bf(dss[h])) for h in range(nh)]
        dg_parts = []
        for h, (nv, r, wm, s) in enumerate(heads):
            _acc(dw_ref.at[h], first, jnp.where(keep, dws[h], 0.0))
            _acc(dbt_ref.at[:, h:h + 1], first, jnp.sum(dss[h], axis=1, keepdims=True))
            dg_parts.append(_colsum(dvhs[h] * nv))
            dnv = dvhs[h] * g_ref[:, cols[h]]
            dvv = r * (dnv - nv * jnp.mean(dnv * nv, axis=-1, keepdims=True))
            dp_ref[:, wa + h * HEAD:wa + (h + 1) * HEAD] = _bf(dvv * dgelu_v[:, cols[h]])
        _acc(dg_ref, first, jnp.concatenate(dg_parts, axis=1))
        base = 3 * wa
        dp_ref[:, base:base + wb] = _bf(dq_ref[...])
        for h in range(wb // HEAD):
            dp_ref[:, base + wb + h * HEAD:base + wb + (h + 1) * HEAD] = _bf(dk_ref[h, 0].T)
            dp_ref[:, base + 2 * wb + h * HEAD:base + 2 * wb + (h + 1) * HEAD] = _bf(dv_ref[h, 0].T)
        dp_ref[:, base + 3 * wb:] = _bf(dc_ref[:, wa:].astype(F32) * ob_ref[...] * _silu_grad(bz_ref[...]))

    a_blk = lambda j: pl.BlockSpec((HEAD, wa), lambda i: (i, j))
    b_blk = pl.BlockSpec((HEAD, wb), lambda i: (i, 0))
    t_blk = pl.BlockSpec((wb // HEAD, 1, HEAD, HEAD), lambda i: (0, i, 0, 0))
    w_spec = pl.BlockSpec((nh, HEAD, HEAD), lambda i: (0, 0, 0))
    bt_spec = pl.BlockSpec((HEAD, nh), lambda i: (0, 0))
    outs = _call(
        body, name=name, grid=(l // HEAD,), ride=ride, sem=("arbitrary",),
        in_specs=[a_blk(0), a_blk(1), a_blk(2), a_blk(3), b_blk, pl.BlockSpec((HEAD, wa + wb), lambda i: (i, 0)),
                  b_blk, t_blk, t_blk, _vec_spec(wa), w_spec, w_spec, bt_spec],
        out_specs=[pl.BlockSpec((HEAD, n), lambda i: (i, 0)), w_spec, bt_spec, _vec_spec(wa)],
        out_shape=[jax.ShapeDtypeStruct((l, n), BF16), jax.ShapeDtypeStruct((nh, HEAD, HEAD), F32),
                   jax.ShapeDtypeStruct((HEAD, nh), F32), jax.ShapeDtypeStruct((1, wa), F32)],
        args=(proj, proj, proj, proj, out_b, dcat, dq, dk, dv, norm_g, sgu_w, sgu_w.transpose(0, 2, 1), sgu_bt))
    return (*outs[:4], outs[4:])


def _ssm_discretise(lr, li, ldt, br, bi):
    dt = jnp.exp(ldt)
    mag = jnp.exp(lr * dt)
    a_re = mag * jnp.cos(li * dt)
    a_im = mag * jnp.sin(li * dt)
    den = lr * lr + li * li
    nr = a_re - 1.0
    coef_re = (nr * lr + a_im * li) / den
    coef_im = (a_im * lr - nr * li) / den
    return a_re, a_im, coef_re * br - coef_im * bi, coef_re * bi + coef_im * br


def _ssm_prep(lr, li, ldt, br, bi, lr_row, li_row, ldt_row, name):
    s, c = br.shape

    def body(lr_ref, li_ref, ldt_ref, br_ref, bi_ref, lrr_ref, lir_ref, ldtr_ref, bbr_ref, bbi_ref, tr_ref, ti_ref):
        _, _, bbr, bbi = _ssm_discretise(lr_ref[...], li_ref[...], ldt_ref[...], br_ref[...], bi_ref[...])
        bbr_ref[...] = bbr
        bbi_ref[...] = bbi
        row = lax.broadcasted_iota(jnp.int32, (SCAN_ROWS, 1), 0)
        blk, r = jnp.right_shift(row, 3), jnp.bitwise_and(row, 7)
        kind, rev = jnp.bitwise_and(blk, 3), blk >= 4
        step = jnp.left_shift(1, kind)
        n = jnp.where(kind < 3, step, jnp.where(rev, 8 - r, r + 1)).astype(F32)
        keep = (kind == 3) | (rev & (r < 8 - step)) | (jnp.logical_not(rev) & (r >= step))
        dt = jnp.exp(ldtr_ref[...])
        mag = jnp.exp(n * (lrr_ref[...] * dt))
        ang = n * (lir_ref[...] * dt)
        tr_ref[...] = jnp.where(keep, mag * jnp.cos(ang), 0.0)
        ti_ref[...] = jnp.where(keep, jnp.where(rev, -1.0, 1.0) * mag * jnp.sin(ang), 0.0)

    col = jax.ShapeDtypeStruct((s, c), F32)
    row = jax.ShapeDtypeStruct((SCAN_ROWS, s), F32)
    return pl.pallas_call(body, name=name, out_shape=[col, col, row, row])(
        lr, li, ldt, br, bi, lr_row, li_row, ldt_row)


def _ssm_prep_bwd(lr, li, ldt, br, bi, da_re, da_im, dbb_re, dbb_im, p, name):
    s, c = br.shape

    def body(lr_ref, li_ref, ldt_ref, br_ref, bi_ref, dar_ref, dai_ref, dbr_ref, dbi_ref,
             dlr_ref, dli_ref, dldt_ref, dbre_ref, dbim_ref):
        args = (lr_ref[...], li_ref[...], ldt_ref[...], br_ref[...], bi_ref[...])
        _, vjp = jax.vjp(_ssm_discretise, *args)
        dlr, dli, dldt, dbr, dbi = vjp((dar_ref[...], dai_ref[...], dbr_ref[...], dbi_ref[...]))
        dlr_ref[...] = dlr
        dli_ref[...] = dli
        dbre_ref[...] = dbr
        dbim_ref[...] = dbi
        idx = lax.broadcasted_iota(jnp.int32, (s, s // p), 0)
        grp = lax.broadcasted_iota(jnp.int32, (s, s // p), 1)
        own = (idx >= grp * p) & (idx < (grp + 1) * p)
        dldt_ref[...] = _colsum(jnp.where(own, dldt, 0.0))

    col1 = jax.ShapeDtypeStruct((s, 1), F32)
    colc = jax.ShapeDtypeStruct((s, c), F32)
    return pl.pallas_call(
        body, name=name, out_shape=[col1, col1, jax.ShapeDtypeStruct((1, s // p), F32), colc, colc],
    )(lr, li, ldt, br, bi, da_re, da_im, dbb_re, dbb_im)


SCAN_ROWS = 64


def _scan_groups(xr, xi, tr_ref, ti_ref, cr, ci, reverse):
    ng = xr.shape[0] // 8
    base = SCAN_ROWS // 2 if reverse else 0
    pr, pi = tr_ref[base + 24:base + 32, :], ti_ref[base + 24:base + 32, :]
    edge = slice(0, 1) if reverse else slice(7, 8)
    out_r, out_i = [None] * ng, [None] * ng
    for g in (range(ng - 1, -1, -1) if reverse else range(ng)):
        sr, si = xr[8 * g:8 * g + 8, :], xi[8 * g:8 * g + 8, :]
        for k in range(3):
            ar, ai = tr_ref[base + 8 * k:base + 8 * k + 8, :], ti_ref[base + 8 * k:base + 8 * k + 8, :]
            shift = 8 - (1 << k) if reverse else 1 << k
            rr, ri = pltpu.roll(sr, shift, 0), pltpu.roll(si, shift, 0)
            sr, si = sr + ar * rr - ai * ri, si + ar * ri + ai * rr
        sr, si = sr + pr * cr - pi * ci, si + pr * ci + pi * cr
        cr, ci = sr[edge, :], si[edge, :]
        out_r[g], out_i[g] = sr, si
    return jnp.concatenate(out_r, axis=0), jnp.concatenate(out_i, axis=0), cr, ci


def _ssm_fwd(proj, bbd, ccd, pw_re, pw_im, d_skip, w, name):
    l = proj.shape[0]
    nb, cw, ns2 = bbd.shape
    ns = ns2 // 2
    nc = l // SSM_T

    def body(u_ref, bbd_ref, ccd_ref, pr_ref, pi_ref, d_ref, y_ref, hsr_ref, hsi_ref, h_ref, hr_s, hi_s):
        @pl.when(pl.program_id(1) == 0)
        def _():
            hr_s[...] = jnp.zeros_like(hr_s)
            hi_s[...] = jnp.zeros_like(hi_s)

        hsr_ref[...] = hr_s[...].reshape(hsr_ref.shape)
        hsi_ref[...] = hi_s[...].reshape(hsi_ref.shape)
        u = u_ref[...]
        bu = _dot(_bf(u), bbd_ref[0])
        hr, hi, cr, ci = _scan_groups(bu[:, :ns], bu[:, ns:], pr_ref, pi_ref, hr_s[...], hi_s[...], False)
        hr_s[...] = cr
        hi_s[...] = ci
        h_bf = _bf(jnp.concatenate([hr, hi], axis=1))
        h_ref[...] = h_bf
        y_ref[...] = _dot(h_bf, ccd_ref[0]) + d_ref[...] * u

    tab = pl.BlockSpec((SCAN_ROWS, ns), lambda b, k: (0, b))
    return pl.pallas_call(
        body, name=name, grid=(nb, nc),
        in_specs=[pl.BlockSpec((SSM_T, cw), lambda b, k: (k, b)),
                  pl.BlockSpec((1, cw, ns2), lambda b, k: (b, 0, 0)),
                  pl.BlockSpec((1, ns2, cw), lambda b, k: (b, 0, 0)),
                  tab, tab, pl.BlockSpec((1, cw), lambda b, k: (0, b))],
        out_specs=[pl.BlockSpec((SSM_T, cw), lambda b, k: (k, b)),
                   pl.BlockSpec((1, 1, ns), lambda b, k: (k, 0, b)), pl.BlockSpec((1, 1, ns), lambda b, k: (k, 0, b)),
                   pl.BlockSpec((SSM_T, ns2), lambda b, k: (k, b))],
        out_shape=[jax.ShapeDtypeStruct((l, w), F32), jax.ShapeDtypeStruct((nc, 1, nb * ns), F32),
                   jax.ShapeDtypeStruct((nc, 1, nb * ns), F32), jax.ShapeDtypeStruct((l, nb * ns2), BF16)],
        scratch_shapes=[pltpu.VMEM((1, ns), F32), pltpu.VMEM((1, ns), F32)],
        compiler_params=_params(("parallel", "arbitrary")),
    )(proj, bbd, ccd, pw_re, pw_im, d_skip)


def _ssm_bwd(proj, dy, hs_re, hs_im, h_all, bbd, ccd, pw_re, pw_im, d_skip, w, name, ride=None):
    l = proj.shape[0]
    nb, cw, ns2 = bbd.shape
    ns = ns2 // 2
    nc = l // SSM_T

    def body(u_ref, dy_ref, hsr_ref, hsi_ref, h_ref, bbd_ref, ccd_ref, pr_ref, pi_ref, d_ref,
             du_ref, dbbd_ref, dccd_ref, dar_ref, dai_ref, dd_ref, gr_s, gi_s):
        first = pl.program_id(1) == 0

        @pl.when(first)
        def _():
            gr_s[...] = jnp.zeros_like(gr_s)
            gi_s[...] = jnp.zeros_like(gi_s)

        u, dy = u_ref[...], dy_ref[...]
        dy_bf = _bf(dy)
        hr0, hi0 = hsr_ref[0], hsi_ref[0]
        h = h_ref[...].astype(F32)
        hr, hi = h[:, :ns], h[:, ns:]
        dh = _dot(dy_bf, ccd_ref[0], NT)
        gr, gi, gcr, gci = _scan_groups(dh[:, :ns], dh[:, ns:], pr_ref, pi_ref, gr_s[...], gi_s[...], True)
        gr_s[...] = gcr
        gi_s[...] = gci
        row0 = lax.broadcasted_iota(jnp.int32, hr.shape, 0) == 0
        pr_h = jnp.where(row0, hr0, pltpu.roll(hr, 1, 0))
        pi_h = jnp.where(row0, hi0, pltpu.roll(hi, 1, 0))
        _acc(dar_ref, first, _colsum(pr_h * gr + pi_h * gi))
        _acc(dai_ref, first, _colsum(pr_h * gi - pi_h * gr))
        g_bf = _bf(jnp.concatenate([gr, gi], axis=1))
        _acc(dbbd_ref.at[0], first, _dot(_bf(u.T), g_bf))
        _acc(dccd_ref.at[0], first, _dot(_bf(h.T), dy_bf))
        du_ref[...] = _bf(_dot(g_bf, bbd_ref[0], NT) + d_ref[...] * dy)
        _acc(dd_ref, first, _colsum(dy * u))

    rev = lambda b, k: (nc - 1 - k, b)
    outs = _call(
        body, name=name, grid=(nb, nc), ride=ride, sem=("parallel", "arbitrary"),
        args=(proj, dy, hs_re, hs_im, h_all, bbd, ccd, pw_re, pw_im, d_skip),
        in_specs=[pl.BlockSpec((SSM_T, cw), rev), pl.BlockSpec((SSM_T, cw), rev),
                  pl.BlockSpec((1, 1, ns), lambda b, k: (nc - 1 - k, 0, b)),
                  pl.BlockSpec((1, 1, ns), lambda b, k: (nc - 1 - k, 0, b)),
                  pl.BlockSpec((SSM_T, ns2), rev),
                  pl.BlockSpec((1, cw, ns2), lambda b, k: (b, 0, 0)),
                  pl.BlockSpec((1, ns2, cw), lambda b, k: (b, 0, 0)),
                  pl.BlockSpec((SCAN_ROWS, ns), lambda b, k: (0, b)), pl.BlockSpec((SCAN_ROWS, ns), lambda b, k: (0, b)),
                  pl.BlockSpec((1, cw), lambda b, k: (0, b))],
        out_specs=[pl.BlockSpec((SSM_T, cw), rev),
                   pl.BlockSpec((1, cw, ns2), lambda b, k: (b, 0, 0)),
                   pl.BlockSpec((1, ns2, cw), lambda b, k: (b, 0, 0)),
                   pl.BlockSpec((1, ns), lambda b, k: (0, b)), pl.BlockSpec((1, ns), lambda b, k: (0, b)),
                   pl.BlockSpec((1, cw), lambda b, k: (0, b))],
        out_shape=[jax.ShapeDtypeStruct((l, w), BF16), jax.ShapeDtypeStruct(bbd.shape, F32),
                   jax.ShapeDtypeStruct(ccd.shape, F32), jax.ShapeDtypeStruct((1, nb * ns), F32),
                   jax.ShapeDtypeStruct((1, nb * ns), F32), jax.ShapeDtypeStruct((1, w), F32)],
        scratch_shapes=[pltpu.VMEM((1, ns), F32), pltpu.VMEM((1, ns), F32)])
    return (*outs[:6], outs[6:])


def _block_diag_b(bb_re, bb_im, g, p, c):
    nb = g // SSM_GB
    keep = _same_group(SSM_GB * c, c, SSM_GB * p, p)

    def one(bb):
        t = bb.reshape(nb, SSM_GB, p, c).transpose(0, 1, 3, 2).reshape(nb, SSM_GB * c, p)
        return jnp.where(keep, jnp.tile(t, (1, 1, SSM_GB)), 0.0)

    return jnp.concatenate([one(bb_re), one(bb_im)], axis=2)


def _same_group(rows, per_row, cols, per_col):
    r = lax.broadcasted_iota(jnp.int32, (rows, cols), 0) // per_row
    q = lax.broadcasted_iota(jnp.int32, (rows, cols), 1) // per_col
    return r == q


def _block_diag_c(c_re, c_im, g, p, c):
    nb = g // SSM_GB
    keep = _same_group(SSM_GB * p, p, SSM_GB * c, c)

    def one(cc):
        t = cc.reshape(nb, SSM_GB, c, p).transpose(0, 1, 3, 2).reshape(nb, SSM_GB * p, c)
        return jnp.where(keep, jnp.tile(t, (1, 1, SSM_GB)), 0.0)

    return jnp.concatenate([one(c_re), one(-c_im)], axis=1)


def _diag_of_b(dbbd, g, p, c):
    nb = g // SSM_GB
    keep = _same_group(SSM_GB * c, c, SSM_GB * p, p)

    def one(blk):
        d = jnp.where(keep, blk, 0.0).reshape(nb, SSM_GB * c, SSM_GB, p).sum(axis=2)
        return d.reshape(nb, SSM_GB, c, p).transpose(0, 1, 3, 2).reshape(g * p, c)

    half = SSM_GB * p
    return one(dbbd[:, :, :half]), one(dbbd[:, :, half:])


def _diag_of_c(dccd, g, p, c):
    nb = g // SSM_GB
    keep = _same_group(SSM_GB * p, p, SSM_GB * c, c)

    def one(blk):
        d = jnp.where(keep, blk, 0.0).reshape(nb, SSM_GB * p, SSM_GB, c).sum(axis=2)
        return d.reshape(nb, SSM_GB, p, c).transpose(0, 1, 3, 2).reshape(g, c, p)

    half = SSM_GB * p
    return one(dccd[:, :half]), -one(dccd[:, half:])


def _glu_fwd(y, proj, w_glu, b_glu, name):
    l, w = y.shape
    tm = _tile(l, 256)

    def body(y_ref, z_ref, w_ref, b_ref, o_ref):
        g = _gelu(y_ref[...])
        t = _dot(_bf(g), w_ref[...]) + b_ref[...]
        o_ref[...] = _bf(g * _sigmoid(t) * _silu(z_ref[...]))

    return pl.pallas_call(
        body, name=name, grid=(l // tm,),
        in_specs=[_row_spec(tm, w), pl.BlockSpec((tm, w), lambda i: (i, 1)),
                  pl.BlockSpec((w, w), lambda i: (0, 0)), _vec_spec(w)],
        out_specs=_row_spec(tm, w), out_shape=jax.ShapeDtypeStruct((l, w), BF16),
        compiler_params=_params(("parallel",)),
    )(y, proj, w_glu, b_glu)


def _glu_bwd(do, y, proj, w_glu, b_glu, name):
    l, w = y.shape
    tm = _tile(l, 512)
    nsteps = l // tm

    def body(do_ref, y_ref, z_ref, w_ref, b_ref, dy_ref, dz_ref, dw_ref, db_ref, dw_acc):
        i = pl.program_id(0)
        first = i == 0
        yv, z, do = y_ref[...], z_ref[...], do_ref[...].astype(F32)
        g = _gelu(yv)
        g_bf = _bf(g)
        sg = _sigmoid(_dot(g_bf, w_ref[...]) + b_ref[...])
        dyy = do * _silu(z)
        dz_ref[...] = _bf(do * g * sg * _silu_grad(z))
        dt = dyy * g * sg * (1.0 - sg)
        dt_bf = _bf(dt)
        dg = dyy * sg + _dot(dt_bf, w_ref[...], NT)
        dy_ref[...] = dg * _gelu_grad(yv)
        _acc(dw_acc, first, _dot(_bf(g.T), dt_bf))
        _acc(db_ref, first, _colsum(dt))

        @pl.when(i == nsteps - 1)
        def _():
            dw_ref[...] = _bf(dw_acc[...])

    return pl.pallas_call(
        body, name=name, grid=(nsteps,),
        in_specs=[_row_spec(tm, w), _row_spec(tm, w), pl.BlockSpec((tm, w), lambda i: (i, 1)),
                  pl.BlockSpec((w, w), lambda i: (0, 0)), _vec_spec(w)],
        out_specs=[_row_spec(tm, w), _row_spec(tm, w), pl.BlockSpec((w, w), lambda i: (0, 0)), _vec_spec(w)],
        out_shape=[jax.ShapeDtypeStruct((l, w), F32), jax.ShapeDtypeStruct((l, w), BF16),
                   jax.ShapeDtypeStruct((w, w), BF16), jax.ShapeDtypeStruct((1, w), F32)],
        scratch_shapes=[pltpu.VMEM((w, w), F32)],
        compiler_params=_params(("arbitrary",)),
    )(do, y, proj, w_glu, b_glu)


MOD_ROWS = 128


def _mod_fwd(cond_pad, w_mod, b_shard, name):
    nl, d, ncol = w_mod.shape
    tn = _tile(ncol, 512)

    def body(c_ref, w_ref, b_ref, o_ref):
        o_ref[0] = _dot(_bf(c_ref[...]), _bf(w_ref[0])) + b_ref[0]

    return pl.pallas_call(
        body, name=name, grid=(nl, ncol // tn),
        in_specs=[pl.BlockSpec((MOD_ROWS, d), lambda a, j: (0, 0)),
                  pl.BlockSpec((1, d, tn), lambda a, j: (a, 0, j)),
                  pl.BlockSpec((1, 1, tn), lambda a, j: (a, 0, j))],
        out_specs=pl.BlockSpec((1, MOD_ROWS, tn), lambda a, j: (a, 0, j)),
        out_shape=jax.ShapeDtypeStruct((nl, MOD_ROWS, ncol), F32),
        compiler_params=_params(("parallel", "parallel")),
    )(cond_pad, w_mod, b_shard)


def _mod_bwd(cond_pad_t, dmod_pad, name):
    nl, _, ncol = dmod_pad.shape
    d = cond_pad_t.shape[0]
    tn = _tile(ncol, 512)

    def body(c_ref, dm_ref, o_ref):
        o_ref[0] = _dot(_bf(c_ref[...]), _bf(dm_ref[0]))

    return pl.pallas_call(
        body, name=name, grid=(nl, ncol // tn),
        in_specs=[pl.BlockSpec((d, MOD_ROWS), lambda a, j: (0, 0)),
                  pl.BlockSpec((1, MOD_ROWS, tn), lambda a, j: (a, 0, j))],
        out_specs=pl.BlockSpec((1, d, tn), lambda a, j: (a, 0, j)),
        out_shape=jax.ShapeDtypeStruct((nl, d, ncol), F32),
        compiler_params=_params(("parallel", "parallel")),
    )(cond_pad_t, dmod_pad)


def _silu_rows(c2d, name):
    def body(c_ref, o_ref):
        o_ref[...] = _silu(c_ref[...])

    return pl.pallas_call(body, name=name, out_shape=jax.ShapeDtypeStruct(c2d.shape, F32))(c2d)


def _sum_leading(x, name):
    n, r, c = x.shape
    tr = _tile(r, max(16, (1 << 20) // (4 * c)), 16 if r % 16 == 0 else 8)

    def body(x_ref, o_ref):
        acc = x_ref[0].astype(F32)
        for k in range(1, n):
            acc = acc + x_ref[k].astype(F32)
        o_ref[...] = acc

    return pl.pallas_call(
        body, name=name, grid=(r // tr,),
        in_specs=[pl.BlockSpec((n, tr, c), lambda i: (0, i, 0))], out_specs=pl.BlockSpec((tr, c), lambda i: (i, 0)),
        out_shape=jax.ShapeDtypeStruct((r, c), F32), compiler_params=_params(("parallel",)),
    )(x)


def _adamw(w, gs, m, v, name, ride=None):
    r, c = w.shape
    tr = _tile(r, max(8, (3 << 19) // (4 * c)), 8)
    ng = len(gs)

    def body(*refs):
        w_ref, g_refs, m_ref, v_ref = refs[0], refs[1:1 + ng], refs[1 + ng], refs[2 + ng]
        g_ref, d_ref, nm_ref, nv_ref = refs[3 + ng:]
        g = g_refs[0][...]
        for extra in g_refs[1:]:
            g = g + extra[...]
        g_ref[...] = g
        d_ref[...], nm_ref[...], nv_ref[...] = _adamw_math(w_ref[...], g, m_ref[...], v_ref[...])

    spec = pl.BlockSpec((tr, c), lambda i: (i, 0))
    shp = jax.ShapeDtypeStruct((r, c), F32)
    outs = _call(body, name=name, grid=(r // tr,), in_specs=[spec] * (3 + ng), out_specs=[spec] * 4,
                 out_shape=[shp] * 4, args=(w, *gs, m, v), sem=("parallel",), ride=ride)
    return outs if ride is None else (outs[:4], outs[4:])


def _adamw_math(w, g, m, v):
    nm = ADAM_B1 * m + (1.0 - ADAM_B1) * g
    nv = ADAM_B2 * v + (1.0 - ADAM_B2) * (g * g)
    m_hat = nm / (1.0 - ADAM_B1 ** ADAM_STEP)
    v_hat = nv / (1.0 - ADAM_B2 ** ADAM_STEP)
    return -ADAM_LR * (m_hat / (jnp.sqrt(v_hat) + ADAM_EPS) + ADAM_WD * w), nm, nv


def _adamw_many(ws, gs, ms, vs, name):
    n = len(ws)

    def body(*refs):
        w_refs, g_refs, m_refs, v_refs = (refs[k * n:(k + 1) * n] for k in range(4))
        outs = refs[4 * n:]
        for i in range(n):
            outs[3 * i][...], outs[3 * i + 1][...], outs[3 * i + 2][...] = _adamw_math(
                w_refs[i][...], g_refs[i][...], m_refs[i][...], v_refs[i][...])

    out_shape = [jax.ShapeDtypeStruct(w.shape, F32) for w in ws for _ in range(3)]
    outs = pl.pallas_call(body, name=name, out_shape=out_shape, compiler_params=_params())(*ws, *gs, *ms, *vs)
    return [tuple(outs[3 * i:3 * i + 3]) for i in range(n)]


ANY = pl.BlockSpec(memory_space=pl.ANY)


def _flip(v, bit):
    return 1 - v if bit else v


def _allgather8_ops(x_ref, o_ref, send_sems, recv_sems, local_sem):
    mx, my, mc = lax.axis_index("x"), lax.axis_index("y"), lax.axis_index("c")
    me = 4 * mx + 2 * my + mc

    def mine():
        return pltpu.make_async_copy(x_ref, o_ref.at[me], local_sem)

    def copy(j, outgoing):
        peer = (_flip(mx, j & 4), _flip(my, j & 2), _flip(mc, j & 1))
        slot = me if outgoing else 4 * peer[0] + 2 * peer[1] + peer[2]
        return pltpu.make_async_remote_copy(
            src_ref=x_ref, dst_ref=o_ref.at[slot], send_sem=send_sems.at[j - 1], recv_sem=recv_sems.at[j - 1],
            device_id=peer, device_id_type=MESH)

    def start():
        mine().start()
        for j in range(1, 8):
            copy(j, True).start()

    def wait():
        for j in range(1, 8):
            copy(j, False).wait()
        mine().wait()

    return start, wait


def _ride_all8(x):
    return dict(xs=[x], shapes=[jax.ShapeDtypeStruct((8,) + x.shape, x.dtype)],
                sems=[pltpu.SemaphoreType.DMA((7,)), pltpu.SemaphoreType.DMA((7,)), pltpu.SemaphoreType.DMA],
                ops=lambda x_refs, o_refs, sems: _allgather8_ops(x_refs[0], o_refs[0], *sems))


def _ride_chip(xs, gather):
    return dict(xs=list(xs), shapes=_chip_exchange_shapes(xs, gather), sems=_chip_exchange_sems(len(xs)),
                ops=lambda x_refs, o_refs, sems: _chip_exchange_ops(x_refs, o_refs, *sems, gather))


def _allgather8(x, name):
    def body(x_ref, o_ref, *sems):
        start, wait = _allgather8_ops(x_ref, o_ref, *sems)
        start()
        wait()

    ride = _ride_all8(x)
    return pl.pallas_call(body, name=name, in_specs=[ANY], out_specs=ANY, out_shape=ride["shapes"][0],
                          scratch_shapes=ride["sems"])(x)


def _gather_halves_ops(x_ref, o_ref, ici_send, ici_recv, d2d_send, d2d_recv, local_sem):
    half = x_ref.shape[0] // 2
    mx, my, mc = lax.axis_index("x"), lax.axis_index("y"), lax.axis_index("c")
    k0 = 2 * mx + my
    mine = pl.ds(pl.multiple_of(mc * half, 16), half)
    theirs = pl.ds(pl.multiple_of((1 - mc) * half, 16), half)

    def local():
        return pltpu.make_async_copy(x_ref, o_ref.at[k0], local_sem)

    def chips(j):
        px, py = _flip(mx, j & 2), _flip(my, j & 1)
        return px, py, 2 * px + py

    def over_ici(j, outgoing):
        px, py, kp = chips(j)
        dst = o_ref.at[k0, mine] if outgoing else o_ref.at[kp, mine]
        return pltpu.make_async_remote_copy(
            src_ref=x_ref.at[mine], dst_ref=dst, send_sem=ici_send.at[j - 1], recv_sem=ici_recv.at[j - 1],
            device_id=(px, py, mc), device_id_type=MESH)

    def over_d2d(j, outgoing):
        _, _, kp = chips(j)
        rows = mine if outgoing else theirs
        return pltpu.make_async_remote_copy(
            src_ref=o_ref.at[kp, rows], dst_ref=o_ref.at[kp, rows], send_sem=d2d_send.at[j - 1],
            recv_sem=d2d_recv.at[j - 1], device_id=(mx, my, 1 - mc), device_id_type=MESH)

    def start():
        local().start()
        for j in range(1, 4):
            over_ici(j, True).start()

    def wait():
        for j in range(1, 4):
            over_ici(j, False).wait_recv()
            over_d2d(j, True).start()
        for j in range(1, 4):
            over_ici(j, True).wait_send()
            over_d2d(j, True).wait_send()
            over_d2d(j, False).wait_recv()
        local().wait()

    return start, wait


def _ride_halves(x):
    dma3 = pltpu.SemaphoreType.DMA((3,))
    return dict(xs=[x], shapes=[jax.ShapeDtypeStruct((4,) + x.shape, x.dtype)],
                sems=[dma3, dma3, dma3, dma3, pltpu.SemaphoreType.DMA],
                ops=lambda x_refs, o_refs, sems: _gather_halves_ops(x_refs[0], o_refs[0], *sems))


def _chip_exchange(xs, gather, name):
    n = len(xs)

    def body(*refs):
        start, wait = _chip_exchange_ops(refs[:n], refs[n:2 * n], *refs[2 * n:], gather)
        start()
        wait()

    return pl.pallas_call(
        body, name=name, in_specs=[ANY] * n, out_specs=[ANY] * n, out_shape=_chip_exchange_shapes(xs, gather),
        scratch_shapes=_chip_exchange_sems(n),
    )(*xs)


def _chip_exchange_shapes(xs, gather):
    return [jax.ShapeDtypeStruct(((4,) + x.shape) if gather else x.shape, x.dtype) for x in xs]


def _chip_exchange_sems(n):
    return [pltpu.SemaphoreType.DMA((3 * n,)), pltpu.SemaphoreType.DMA((3 * n,)), pltpu.SemaphoreType.DMA((n,))]


def _chip_exchange_ops(x_refs, o_refs, send_sems, recv_sems, local_sems, gather):
    n = len(x_refs)
    mx, my, mc = lax.axis_index("x"), lax.axis_index("y"), lax.axis_index("c")
    k0 = 2 * mx + my

    def local(a):
        src = x_refs[a] if gather else x_refs[a].at[k0]
        return pltpu.make_async_copy(src, o_refs[a].at[k0], local_sems.at[a])

    def copy(a, j, outgoing):
        px, py = _flip(mx, j & 2), _flip(my, j & 1)
        kp = 2 * px + py
        if outgoing:
            src = x_refs[a] if gather else x_refs[a].at[kp]
            dst = o_refs[a].at[k0]
        else:
            src = x_refs[a] if gather else x_refs[a].at[k0]
            dst = o_refs[a].at[kp]
        s = a * 3 + j - 1
        return pltpu.make_async_remote_copy(
            src_ref=src, dst_ref=dst, send_sem=send_sems.at[s], recv_sem=recv_sems.at[s],
            device_id=(px, py, mc), device_id_type=MESH)

    def start():
        for a in range(n):
            local(a).start()
            for j in range(1, 4):
                copy(a, j, True).start()

    def wait():
        for a in range(n):
            for j in range(1, 4):
                copy(a, j, False).wait()
            local(a).wait()

    return start, wait


def _call(body, *, name, grid, in_specs, out_specs, out_shape, args, scratch_shapes=(), sem=None, ride=None):
    if not ride:
        return pl.pallas_call(
            body, name=name, grid=grid, in_specs=list(in_specs), out_specs=list(out_specs), out_shape=list(out_shape),
            scratch_shapes=list(scratch_shapes), compiler_params=_params(sem))(*args)
    xs = [x for r in ride for x in r["xs"]]
    shapes = [s for r in ride for s in r["shapes"]]
    sems = [s for r in ride for s in r["sems"]]
    n_in, n_out, n_scr, nx = len(in_specs), len(out_specs), len(scratch_shapes), len(xs)

    def wrapped(*refs):
        ins, x_refs = refs[:n_in], refs[n_in:n_in + nx]
        outs = refs[n_in + nx:n_in + nx + n_out]
        lands = refs[n_in + nx + n_out:n_in + 2 * nx + n_out]
        rest = refs[n_in + 2 * nx + n_out:]
        scr, sem_refs = rest[:n_scr], rest[n_scr:]
        ops, xo, so = [], 0, 0
        for r in ride:
            nr, ns = len(r["xs"]), len(r["sems"])
            ops.append(r["ops"](x_refs[xo:xo + nr], lands[xo:xo + nr], sem_refs[so:so + ns]))
            xo, so = xo + nr, so + ns
        ids = [pl.program_id(a) for a in range(len(grid))]
        first = functools.reduce(jnp.logical_and, [i == 0 for i in ids])
        last = functools.reduce(jnp.logical_and, [i == g - 1 for i, g in zip(ids, grid)])

        @pl.when(first)
        def _():
            for start, _ in ops:
                start()

        body(*ins, *outs, *scr)

        @pl.when(last)
        def _():
            for _, wait in ops:
                wait()

    return pl.pallas_call(
        wrapped, name=name, grid=grid, in_specs=list(in_specs) + [ANY] * nx, out_specs=list(out_specs) + [ANY] * nx,
        out_shape=list(out_shape) + shapes, scratch_shapes=list(scratch_shapes) + sems,
        compiler_params=_params(("arbitrary",) * len(grid)))(*args, *xs)


def _ride_sibling(xs):
    n = len(xs)

    def ops(x_refs, o_refs, sems):
        send_sems, recv_sems = sems
        sib = (lax.axis_index("x"), lax.axis_index("y"), 1 - lax.axis_index("c"))

        def copies():
            return [pltpu.make_async_remote_copy(
                src_ref=x_refs[a], dst_ref=o_refs[a], send_sem=send_sems.at[a], recv_sem=recv_sems.at[a],
                device_id=sib, device_id_type=MESH) for a in range(n)]

        def start():
            for cp in copies():
                cp.start()

        def wait():
            for cp in copies():
                cp.wait()

        return start, wait

    return dict(xs=list(xs), shapes=[jax.ShapeDtypeStruct(x.shape, x.dtype) for x in xs],
                sems=[pltpu.SemaphoreType.DMA((n,)), pltpu.SemaphoreType.DMA((n,))], ops=ops)


PACK = 1024
PACK_ROWS = 512


def _pack(parts):
    flat = []
    for p in parts:
        v = p.reshape(-1).astype(F32)
        flat.append(jnp.pad(v, (0, (-v.shape[0]) % PACK)))
    total = sum(v.shape[0] for v in flat)
    flat.append(jnp.zeros(((-total) % (PACK_ROWS * 128),), F32))
    return jnp.concatenate(flat).reshape(-1, 128)


def _shard_columns(shards, lo, hi):
    width = shards.shape[2]
    out = []
    for k in range(shards.shape[0]):
        a, b = max(lo, k * width), min(hi, (k + 1) * width)
        if a < b:
            out.append(shards[k, :, a - k * width:b - k * width])
    return out


def _unpack_rows(gathered, shapes):
    flat = gathered.reshape(gathered.shape[0], -1)
    out, off = [], 0
    for shp in shapes:
        n = math.prod(shp)
        out.append(flat[:, off:off + n].reshape((flat.shape[0],) + tuple(shp)))
        off += n + (-n) % PACK
    return out


def _unpack(packed, shapes):
    flat = packed.reshape(-1)
    out, off = [], 0
    for shp in shapes:
        n = math.prod(shp)
        out.append(flat[off:off + n].reshape(shp))
        off += n + (-n) % PACK
    return out


def kernel(x, c, ln_pre_g, ln_post_g, w_mod, b_mod, w_in_ab, w_out_ab, sgu_norm_g, sgu_w, sgu_b, w_in_ssm, w_out_ssm, lam_re, lam_im, b_re, b_im, c_re, c_im, d_skip, log_dt, w_glu, b_glu, loss_target, m_ln_pre_g, m_ln_post_g, m_w_mod, m_b_mod, m_w_in_ab, m_w_out_ab, m_sgu_norm_g, m_sgu_w, m_sgu_b, m_w_in_ssm, m_w_out_ssm, m_lam_re, m_lam_im, m_b_re, m_b_im, m_c_re, m_c_im, m_d_skip, m_log_dt, m_w_glu, m_b_glu, v_ln_pre_g, v_ln_post_g, v_w_mod, v_b_mod, v_w_in_ab, v_w_out_ab, v_sgu_norm_g, v_sgu_w, v_sgu_b, v_w_in_ssm, v_w_out_ssm, v_lam_re, v_lam_im, v_b_re, v_b_im, v_c_re, v_c_im, v_d_skip, v_log_dt, v_w_glu, v_b_glu):
    given = dict(locals())
    mx, my, mc = lax.axis_index("x"), lax.axis_index("y"), lax.axis_index("c")
    me = 4 * mx + 2 * my + mc
    chip = 2 * mx + my

    _, l, d = x.shape
    x2, tgt = x[0], loss_target[0]
    n_in = w_in_ab.shape[2] * 4
    wa = wb = n_in // 7
    w = w_out_ssm.shape[1]
    g, p, cch = b_re.shape[1:]
    nmod = w_mod.shape[2]


    cond = _silu_rows(c.reshape(d // 128, 128), "cond_silu")
    cond_all = _allgather8(cond, "gather_cond").reshape(8, d)
    b_shard = lax.dynamic_slice(b_mod, (0, chip * nmod), (2, nmod)).reshape(2, 1, nmod)
    cond_pad = jnp.pad(cond_all, ((0, MOD_ROWS - 8), (0, 0)))
    modp = _mod_fwd(cond_pad, w_mod, b_shard, "mod_fwd")[:, :8]
    modp_all = _allgather8(modp.reshape(16, nmod), "gather_mod").reshape(4, 2, 2, 8, nmod)
    mine = lax.dynamic_index_in_dim(lax.dynamic_index_in_dim(modp_all, mc, 1, False), me, 2, False)
    mod = mine.transpose(1, 0, 2).reshape(2, 3 * d)
    shift = [mod[a:a + 1, :d] for a in range(2)]
    scale = [mod[a:a + 1, d:2 * d] for a in range(2)]
    gate = [mod[a:a + 1, 2 * d:] for a in range(2)]
    pre_g = [ln_pre_g[a:a + 1] for a in range(2)]
    post_g = [ln_post_g[a:a + 1] for a in range(2)]

    sgu_w0, sgu_bt = sgu_w[0], sgu_b[0].T
    h0, (gw_in_ab,) = _pre_fwd(x2, pre_g[0], scale[0], shift[0], "pre0_fwd", ride=[_ride_halves(_bf(w_in_ab[0]))])
    w_gates = jnp.concatenate(_shard_columns(gw_in_ab, 0, 3 * wa) + _shard_columns(gw_in_ab, 3 * wa + 3 * wb, n_in),
                              axis=1)
    w_qkv = jnp.concatenate(_shard_columns(gw_in_ab, 3 * wa, 3 * wa + 3 * wb), axis=1)
    proj0, (gw_in_ssm,) = _matmul(h0, w_gates, "nn", F32, "proj0", tm=1024, ride=[_ride_chip([_bf(w_in_ssm[0])], True)])
    qkv, (gw_out_ssm, gw_glu, g_dskip, g_bglu) = _matmul(
        h0, w_qkv, "nn", BF16, "proj0_qkv", tm=1024,
        ride=[_ride_chip([_bf(w_out_ssm[0]), _bf(w_glu[0]), d_skip, b_glu], True)])
    out_b, (gw_out_ab,) = _attn_fwd(qkv, wb, "attn_fwd", hp=8, ride=[_ride_chip([_bf(w_out_ab[0])], True)])
    wout_ab = gw_out_ab.reshape(wa + wb, d)
    win_ssm = gw_in_ssm.reshape(d, 2 * w)
    wout_ssm = jnp.concatenate([gw_out_ssm[k] for k in range(4)], axis=1)
    wglu = gw_glu.reshape(w, w)
    dskip_full = g_dskip.reshape(1, w)
    bglu_full = g_bglu.reshape(1, w)
    cat =_sgu_fwd(proj0, out_b, sgu_norm_g, sgu_w0, sgu_bt, wa, wb, "sgu_fwd")
    y0 = _matmul(cat, wout_ab, "nn", BF16, "out0", tm=1024)
    x1, h1 = _post_pre_fwd(x2, y0, gate[0], post_g[0], pre_g[1], scale[1], shift[1], "post0_pre1_fwd")

    s = g * p
    lr_c, li_c = lam_re.reshape(s, 1), lam_im.reshape(s, 1)
    ldt_c = jnp.repeat(log_dt.reshape(g), p).reshape(s, 1)
    br_c, bi_c = b_re.reshape(s, cch), b_im.reshape(s, cch)
    bb_re, bb_im, pw_re, pw_im = _ssm_prep(lr_c, li_c, ldt_c, br_c, bi_c, lr_c.reshape(1, s), li_c.reshape(1, s),
                                           ldt_c.reshape(1, s), "ssm_prep")
    bbd = _bf(_block_diag_b(bb_re, bb_im, g, p, cch))
    ccd = _bf(_block_diag_c(c_re[0], c_im[0], g, p, cch))
    proj1 = _matmul(h1, win_ssm, "nn", F32, "proj1", tm=1024)
    y_ssm, hs_re, hs_im, h_all = _ssm_fwd(proj1, bbd, ccd, pw_re, pw_im, dskip_full, w, "ssm_fwd")
    o1 = _glu_fwd(y_ssm, proj1, wglu, bglu_full, "glu_fwd")
    y1 = _matmul(o1, wout_ssm, "nn", BF16, "out1", tm=1024)
    loss_vec, dy1, dx2, dgate1, dpost1 = _post_loss(x1, y1, gate[1], post_g[1], tgt, "post1_loss")

    do1 = _matmul(dy1, wout_ssm, "nt", BF16, "out1_dx", tm=1024)
    gr_wout_ssm = _matmul(o1, dy1, "tn", BF16, "out1_dw", tm=1024, tk=1024, n_split=4)
    dy_ssm, dz1, gr_wglu, gr_bglu = _glu_bwd(do1, y_ssm, proj1, wglu, bglu_full, "glu_bwd")
    du1, dbbd, dccd, da_re, da_im, gr_dskip, (ld_wout_ssm, ld_wglu) = _ssm_bwd(
        proj1, dy_ssm, hs_re, hs_im, h_all, bbd, ccd, pw_re, pw_im, dskip_full, w, "ssm_bwd",
        ride=[_ride_chip([gr_wout_ssm, gr_wglu.reshape(4, w // 4, w)], False)])
    dproj1 = jnp.concatenate([du1, dz1], axis=1)
    dh1 = _matmul(dproj1, win_ssm, "nt", BF16, "proj1_dx", tm=1024)
    gr_win_ssm = _matmul(h1, dproj1, "tn", BF16, "proj1_dw", tm=1024, tn=1024, tk=1024)
    dx1, dscale1, dshift1, dpre1, dy0, dgate0, dpost0 = _pre_bwd(
        dh1, dx2, x1, pre_g[1], scale[1], "pre1_post0_bwd", post=(y0, gate[0], post_g[0]))

    dcat = _matmul(dy0, wout_ab, "nt", BF16, "out0_dx", tm=1024)
    gr_wout_ab = _matmul(cat, dy0, "tn", BF16, "out0_dw", tm=1024, tn=1024, tk=1024)
    dbb_re, dbb_im = _diag_of_b(dbbd, g, p, cch)
    dc_re, dc_im = _diag_of_c(dccd, g, p, cch)
    part_a = [loss_vec[:, :1], dpre1, dpost0, dpost1, dgate0, dshift1, dscale1, dgate1, da_re, da_im,
              dbb_re, dbb_im, dc_re, dc_im, gr_dskip, gr_bglu]
    shapes_a = [a.shape for a in part_a]
    dq, dk, dv, (ld_win_ssm, ld_wout_ab, gath_a) = _attn_bwd(
        qkv, proj0, dcat, wa, wb, "attn_bwd", hp=4,
        ride=[_ride_chip([gr_win_ssm.reshape(4, d // 4, 2 * w), gr_wout_ab.reshape(4, (wa + wb) // 4, d)], False),
              _ride_all8(_pack(part_a))])
    dproj0, gr_sgu_w, gr_sgu_bt, gr_sgu_g, _ = _sgu_bwd(
        proj0, out_b, dcat, dq, dk, dv, sgu_norm_g, sgu_w0, sgu_bt, wa, wb, "sgu_bwd")
    part_b = [gr_sgu_g, gr_sgu_w, gr_sgu_bt.T]
    shapes_b = [a.shape for a in part_b]
    gr_win_ab_lo, (gath_b,) = _matmul(h0, dproj0, "tn", BF16, "proj0_dw_lo", tm=1024, tk=1024, tn=896, n_split=4,
                                      m_part=(0, 1, 2), ride=[_ride_all8(_pack(part_b))])
    gr_win_ab_hi, (ld_win_ab_lo,) = _matmul(
        h0, dproj0, "tn", BF16, "proj0_dw_hi", tm=1024, tk=1024, tn=896, n_split=4, m_part=(1, 1, 2),
        ride=[_ride_chip([gr_win_ab_lo], False)])
    dh0, (ld_win_ab_hi,) = _matmul(dproj0, gw_in_ab, "nt", BF16, "proj0_dx", tm=1024, tn=1024,
                                   ride=[_ride_chip([gr_win_ab_hi], False)])
    grad_x, dscale0, dshift0, dpre0 = _pre_bwd(dh0, dx1, x2, pre_g[0], scale[0], "pre0_bwd")
    part_c = [dpre0, dshift0, dscale0]
    shapes_c = [a.shape for a in part_c]
    gath_c = _allgather8(_pack(part_c), "gather_small_tail")

    landed = [ld_wout_ab, ld_win_ssm, ld_wout_ssm, ld_wglu]
    big_names = ["w_in_ab", "w_out_ab", "w_in_ssm", "w_out_ssm", "w_glu"]
    sums = [jnp.concatenate([_sum_leading(ld_win_ab_lo, "sum_w_in_ab_lo"), _sum_leading(ld_win_ab_hi, "sum_w_in_ab_hi")],
                            axis=0)]
    sums += [_sum_leading(a, "sum_" + nm) for a, nm in zip(landed, big_names[1:])]
    results = {}

    (loss_s, g_pre1, g_post0, g_post1, g_gate0, g_shift1, g_scale1, g_gate1, s_da_re, s_da_im, s_dbb_re, s_dbb_im,
     g_c_re, g_c_im, g_dskip_full, g_bglu_full) = _unpack(_sum_leading(gath_a, "sum_small_a"), shapes_a)
    g_sgu_g, g_sgu_w, g_sgu_b = _unpack(_sum_leading(gath_b, "sum_small_b"), shapes_b)
    g_pre0, g_shift0, g_scale0 = _unpack(_sum_leading(gath_c, "sum_small_c"), shapes_c)
    loss = loss_s.reshape(())
    g_pre = jnp.concatenate([g_pre0, g_pre1], axis=0)
    g_post = jnp.concatenate([g_post0, g_post1], axis=0)
    g_bmod = jnp.concatenate([jnp.concatenate([g_shift0, g_scale0, g_gate0], axis=1),
                              jnp.concatenate([g_shift1, g_scale1, g_gate1], axis=1)], axis=0)

    g_lr, g_li, g_ldt, g_br, g_bi = _ssm_prep_bwd(lr_c, li_c, ldt_c, br_c, bi_c, s_da_re.reshape(s, 1),
                                                  s_da_im.reshape(s, 1), s_dbb_re, s_dbb_im, p, "ssm_prep_bwd")
    small = {
        "ln_pre_g": g_pre, "ln_post_g": g_post, "b_mod": g_bmod, "sgu_norm_g": g_sgu_g,
        "sgu_w": g_sgu_w.reshape(sgu_w.shape), "sgu_b": g_sgu_b.reshape(sgu_b.shape),
        "lam_re": g_lr.reshape(lam_re.shape), "lam_im": g_li.reshape(lam_im.shape),
        "b_re": g_br.reshape(b_re.shape), "b_im": g_bi.reshape(b_im.shape),
        "c_re": g_c_re.reshape(c_re.shape), "c_im": g_c_im.reshape(c_im.shape),
        "d_skip": lax.dynamic_slice(g_dskip_full, (0, chip * (w // 4)), (1, w // 4)),
        "log_dt": g_ldt.reshape(log_dt.shape),
        "b_glu": lax.dynamic_slice(g_bglu_full, (0, chip * (w // 4)), (1, w // 4)),
    }
    flat2 = lambda a: a.reshape(-1, a.shape[-1])
    wide = ("b_re", "b_im")
    for tag, group in (("adamw_small", [nm for nm in small if nm not in wide]), ("adamw_small_b", list(wide))):
        outs = _adamw_many([flat2(given[nm]) for nm in group], [flat2(small[nm]) for nm in group],
                           [flat2(given["m_" + nm]) for nm in group], [flat2(given["v_" + nm]) for nm in group], tag)
        for nm, trio in zip(group, outs):
            results[nm] = [small[nm]] + [o.reshape(given[nm].shape) for o in trio]

    rows_a = _unpack_rows(gath_a, shapes_a)
    rows_c = _unpack_rows(gath_c, shapes_c)
    dmod_rows = jnp.concatenate([rows_c[1], rows_c[2], rows_a[4], rows_a[5], rows_a[6], rows_a[7]],
                                axis=2).reshape(8, 2, 3 * d)
    dmod_shard = lax.dynamic_slice(dmod_rows, (0, 0, chip * nmod), (8, 2, nmod)).transpose(1, 0, 2)
    dmod_pad = jnp.pad(dmod_shard, ((0, 0), (0, MOD_ROWS - 8), (0, 0)))
    gr_wmod = _mod_bwd(cond_pad.T, dmod_pad, "mod_bwd")
    two_d = lambda a: a.reshape(-1, nmod)
    outs, sib = _adamw(two_d(w_mod), [two_d(gr_wmod)], two_d(m_w_mod), two_d(v_w_mod), "adamw_w_mod",
                       ride=[_ride_sibling(sums)])
    results["w_mod"] = [o.reshape(w_mod.shape) for o in outs]
    for nm, s_mine, s_sib in zip(big_names, sums, sib):
        shp = given[nm].shape
        two_d = lambda a: a.reshape(-1, shp[-1])
        outs = _adamw(two_d(given[nm]), [s_mine, s_sib], two_d(given["m_" + nm]), two_d(given["v_" + nm]),
                      "adamw_" + nm)
        results[nm] = [o.reshape(shp) for o in outs]

    names = ["ln_pre_g", "ln_post_g", "w_mod", "b_mod", "w_in_ab", "w_out_ab", "sgu_norm_g", "sgu_w", "sgu_b",
             "w_in_ssm", "w_out_ssm", "lam_re", "lam_im", "b_re", "b_im", "c_re", "c_im", "d_skip", "log_dt",
             "w_glu", "b_glu"]
    return (loss, grad_x[None], *[results[nm][0] for nm in names], *[results[nm][1] for nm in names],
            *[results[nm][2] for nm in names], *[results[nm][3] for nm in names])
```

```python
import functools
import math

import jax
import jax.numpy as jnp
from jax import lax
from jax.experimental import pallas as pl
from jax.experimental.pallas import tpu as pltpu

F32 = jnp.float32
BF16 = jnp.bfloat16
MESH = pl.DeviceIdType.MESH

EPS = 1e-6
HEAD = 128
SSM_T = 512
SSM_GB = 16
ADAM_LR, ADAM_B1, ADAM_B2, ADAM_EPS, ADAM_WD, ADAM_STEP = 0.001, 0.9, 0.999, 1e-08, 0.01, 10
VMEM_LIMIT = 56 * 1024 * 1024

NN = (((1,), (0,)), ((), ()))
NT = (((1,), (1,)), ((), ()))
TN = (((0,), (0,)), ((), ()))


def _params(sem=None):
    return pltpu.CompilerParams(dimension_semantics=sem, vmem_limit_bytes=VMEM_LIMIT)


def _dot(a, b, dims=NN):
    return lax.dot_general(a, b, dims, preferred_element_type=F32)


def _bf(x):
    return x.astype(BF16)


def _gelu(x):
    k = math.sqrt(2.0 / math.pi)
    t = jnp.tanh(k * (x + 0.044715 * x * x * x))
    return 0.5 * x * (1.0 + t)


def _gelu_grad(x):
    k = math.sqrt(2.0 / math.pi)
    x2 = x * x
    t = jnp.tanh(k * (x + 0.044715 * x * x2))
    return 0.5 * (1.0 + t) + 0.5 * x * (1.0 - t * t) * k * (1.0 + 3.0 * 0.044715 * x2)


def _sigmoid(x):
    return 1.0 / (1.0 + jnp.exp(-x))


def _silu(x):
    return x * _sigmoid(x)


def _silu_grad(x):
    s = _sigmoid(x)
    return s * (1.0 + x * (1.0 - s))


def _tile(n, t, mult=128):
    if n <= t:
        return n
    for cand in range(t - t % mult, 0, -mult):
        if n % cand == 0:
            return cand
    raise ValueError((n, t, mult))


def _matmul(a, b, mode, out_dtype, name, tm=512, tn=512, tk=2048, n_split=1, ride=None, m_part=None):
    b_sharded = b.ndim == 3
    if mode == "nn":
        (m, kk), (_, n) = a.shape, b.shape
    elif b_sharded:
        assert mode == "nt"
        (m, kk), n, tk = a.shape, b.shape[1], b.shape[2]
    elif mode == "nt":
        (m, kk), (n, _) = a.shape, b.shape
    else:
        (kk, m), (_, n) = a.shape, b.shape
    m_off = 0
    if m_part is not None:
        assert mode == "tn"
        first, count, parts = m_part
        tm = _tile(m // parts, tm)
        m_off = first * (m // parts) // tm
        m = count * (m // parts)
    tm, tk = _tile(m, tm), _tile(kk, tk)
    ns = n // n_split
    tn = _tile(ns, tn)
    nk = kk // tk
    dims = {"nn": NN, "nt": NT, "tn": TN}[mode]

    def body(a_ref, b_ref, o_ref, acc_ref):
        k = pl.program_id(2)
        part = _dot(_bf(a_ref[...]), _bf(b_ref[0] if b_sharded else b_ref[...]), dims)

        @pl.when(k == 0)
        def _():
            acc_ref[...] = part

        @pl.when(k > 0)
        def _():
            acc_ref[...] += part

        @pl.when(k == nk - 1)
        def _():
            o_ref[...] = acc_ref[...].astype(out_dtype).reshape(o_ref.shape)

    if mode == "nn":
        a_spec = pl.BlockSpec((tm, tk), lambda i, j, k: (i, k))
        b_spec = pl.BlockSpec((tk, tn), lambda i, j, k: (k, j))
    elif mode == "nt":
        a_spec = pl.BlockSpec((tm, tk), lambda i, j, k: (i, k))
        b_spec = (pl.BlockSpec((1, tn, tk), lambda i, j, k: (k, j, 0)) if b_sharded
                  else pl.BlockSpec((tn, tk), lambda i, j, k: (j, k)))
    else:
        a_spec = pl.BlockSpec((tk, tm), lambda i, j, k: (k, i + m_off))
        b_spec = pl.BlockSpec((tk, tn), lambda i, j, k: (k, j))
    if n_split == 1:
        out_shape = jax.ShapeDtypeStruct((m, n), out_dtype)
        o_spec = pl.BlockSpec((tm, tn), lambda i, j, k: (i, j))
    else:
        per = ns // tn
        out_shape = jax.ShapeDtypeStruct((n_split, m, ns), out_dtype)
        o_spec = pl.BlockSpec((1, tm, tn), lambda i, j, k: (j // per, i, j % per))
    outs = _call(body, name=name, grid=(m // tm, n // tn, nk), in_specs=[a_spec, b_spec], out_specs=[o_spec],
                 out_shape=[out_shape], scratch_shapes=[pltpu.VMEM((tm, tn), F32)], args=(a, b),
                 sem=("parallel", "parallel", "arbitrary"), ride=ride)
    return outs[0] if ride is None else (outs[0], outs[1:])


def _row_spec(tm, d):
    return pl.BlockSpec((tm, d), lambda i: (i, 0))


def _vec_spec(d):
    return pl.BlockSpec((1, d), lambda i: (0, 0))


def _acc(ref, first, val):
    @pl.when(first)
    def _():
        ref[...] = val

    @pl.when(jnp.logical_not(first))
    def _():
        ref[...] += val


def _colsum(x):
    return jnp.sum(x, axis=0, keepdims=True)


def _rownorm(x):
    r = lax.rsqrt(jnp.mean(x * x, axis=-1, keepdims=True) + EPS)
    return x * r, r


STRIP = 64


def _fold8(x):
    return functools.reduce(lambda a, b: a + b, [x[8 * k:8 * k + 8] for k in range(x.shape[0] // 8)])


def _pre_fwd(x, g, scale, shift, name, ride=None):
    l, d = x.shape
    tm = _tile(l, 256)

    def body(x_ref, g_ref, sc_ref, sh_ref, h_ref):
        n, _ = _rownorm(x_ref[...])
        h_ref[...] = _bf(n * g_ref[...] * (1.0 + sc_ref[...]) + sh_ref[...])

    outs = _call(body, name=name, grid=(l // tm,), in_specs=[_row_spec(tm, d), _vec_spec(d), _vec_spec(d), _vec_spec(d)],
                 out_specs=[_row_spec(tm, d)], out_shape=[jax.ShapeDtypeStruct((l, d), BF16)],
                 args=(x, g, scale, shift), sem=("parallel",), ride=ride)
    return outs[0], outs[1:]


def _post_pre_fwd(x, y, gate, pg, g1, scale1, shift1, name):
    l, d = x.shape
    tm = _tile(l, 256)

    def body(x_ref, y_ref, gate_ref, pg_ref, g1_ref, sc_ref, sh_ref, x1_ref, h1_ref):
        @pl.loop(0, tm // STRIP)
        def _(s):
            rows = pl.ds(pl.multiple_of(s * STRIP, STRIP), STRIP)
            ny, _ = _rownorm(y_ref[rows, :].astype(F32))
            x1 = x_ref[rows, :] + gate_ref[...] * (ny * pg_ref[...])
            x1_ref[rows, :] = x1
            n1, _ = _rownorm(x1)
            h1_ref[rows, :] = _bf(n1 * g1_ref[...] * (1.0 + sc_ref[...]) + sh_ref[...])

    v = _vec_spec(d)
    return pl.pallas_call(
        body, name=name, grid=(l // tm,),
        in_specs=[_row_spec(tm, d), _row_spec(tm, d), v, v, v, v, v],
        out_specs=[_row_spec(tm, d), _row_spec(tm, d)],
        out_shape=[jax.ShapeDtypeStruct((l, d), F32), jax.ShapeDtypeStruct((l, d), BF16)],
        compiler_params=_params(("parallel",)),
    )(x, y, gate, pg, g1, scale1, shift1)


def _post_loss(x1, y1, gate, pg, target, name):
    l, d = x1.shape
    tm = _tile(l, 256)

    def body(x_ref, y_ref, gate_ref, pg_ref, t_ref, loss_ref, dy_ref, dx_ref, dgate_ref, dpg_ref):
        first = pl.program_id(0) == 0

        def strip(s, sums):
            rows = pl.ds(pl.multiple_of(s * STRIP, STRIP), STRIP)
            ny, ry = _rownorm(y_ref[rows, :].astype(F32))
            q = ny * pg_ref[...]
            e = x_ref[rows, :] + gate_ref[...] * q - t_ref[rows, :]
            dx2 = e * (1.0 / d)
            dx_ref[rows, :] = dx2
            dq = dx2 * gate_ref[...]
            dny = dq * pg_ref[...]
            dy_ref[rows, :] = _bf(ry * (dny - ny * jnp.mean(dny * ny, axis=-1, keepdims=True)))
            return sums[0] + _fold8(e * e), sums[1] + _fold8(dx2 * q), sums[2] + _fold8(dq * ny)

        zero = jnp.zeros((8, d), F32)
        sq, dgate, dpg = lax.fori_loop(0, tm // STRIP, strip, (zero, zero, zero))
        _acc(loss_ref, first, jnp.full((1, 128), 0.5 / d, F32) * jnp.sum(sq))
        _acc(dgate_ref, first, _colsum(dgate))
        _acc(dpg_ref, first, _colsum(dpg))

    v = _vec_spec(d)
    return pl.pallas_call(
        body, name=name, grid=(l // tm,),
        in_specs=[_row_spec(tm, d), _row_spec(tm, d), v, v, _row_spec(tm, d)],
        out_specs=[_vec_spec(128), _row_spec(tm, d), _row_spec(tm, d), v, v],
        out_shape=[jax.ShapeDtypeStruct((1, 128), F32), jax.ShapeDtypeStruct((l, d), BF16),
                   jax.ShapeDtypeStruct((l, d), F32), jax.ShapeDtypeStruct((1, d), F32),
                   jax.ShapeDtypeStruct((1, d), F32)],
        compiler_params=_params(("arbitrary",)),
    )(x1, y1, gate, pg, target)


def _pre_bwd(dh, dres, x, g, scale, name, post=None):
    l, d = x.shape
    tm = _tile(l, 256)
    with_post = post is not None

    def body(*refs):
        if with_post:
            (dh_ref, dres_ref, x_ref, g_ref, sc_ref, y_ref, gate_ref, pg_ref,
             dx_ref, dsc_ref, dsh_ref, dg_ref, dy_ref, dgate_ref, dpg_ref) = refs
        else:
            dh_ref, dres_ref, x_ref, g_ref, sc_ref, dx_ref, dsc_ref, dsh_ref, dg_ref = refs
        first = pl.program_id(0) == 0

        def strip(s, sums):
            rows = pl.ds(pl.multiple_of(s * STRIP, STRIP), STRIP)
            dh = dh_ref[rows, :].astype(F32)
            n, r = _rownorm(x_ref[rows, :])
            dyn = dh * (1.0 + sc_ref[...])
            dn = dyn * g_ref[...]
            dx = dres_ref[rows, :] + r * (dn - n * jnp.mean(dn * n, axis=-1, keepdims=True))
            dx_ref[rows, :] = dx
            new = [sums[0] + _fold8(dh * (n * g_ref[...])), sums[1] + _fold8(dh), sums[2] + _fold8(dyn * n)]
            if with_post:
                ny, ry = _rownorm(y_ref[rows, :].astype(F32))
                dq = dx * gate_ref[...]
                dny = dq * pg_ref[...]
                dy_ref[rows, :] = _bf(ry * (dny - ny * jnp.mean(dny * ny, axis=-1, keepdims=True)))
                new += [sums[3] + _fold8(dx * (ny * pg_ref[...])), sums[4] + _fold8(dq * ny)]
            return tuple(new)

        zero = jnp.zeros((8, d), F32)
        sums = lax.fori_loop(0, tm // STRIP, strip, (zero,) * (5 if with_post else 3))
        outs = [dsc_ref, dsh_ref, dg_ref] + ([dgate_ref, dpg_ref] if with_post else [])
        for ref, acc in zip(outs, sums):
            _acc(ref, first, _colsum(acc))

    v = _vec_spec(d)
    row = _row_spec(tm, d)
    vec_out = jax.ShapeDtypeStruct((1, d), F32)
    in_specs = [row, row, row, v, v]
    args = [dh, dres, x, g, scale]
    out_specs = [row, v, v, v]
    out_shape = [jax.ShapeDtypeStruct((l, d), F32), vec_out, vec_out, vec_out]
    if with_post:
        in_specs += [row, v, v]
        args += list(post)
        out_specs += [row, v, v]
        out_shape += [jax.ShapeDtypeStruct((l, d), BF16), vec_out, vec_out]
    return pl.pallas_call(
        body, name=name, grid=(l // tm,), in_specs=in_specs, out_specs=out_specs, out_shape=out_shape,
        compiler_params=_params(("arbitrary",)),
    )(*args)


def _softplus_parts(z):
    e = jnp.exp(-jnp.abs(z))
    den = 1.0 + e
    lb = jnp.minimum(z, 0.0) - jnp.log(den)
    return lb, lb - z, jnp.exp(lb)


def _tri(cmp, n=HEAD):
    row = lax.broadcasted_iota(jnp.int32, (n, n), 0)
    col = lax.broadcasted_iota(jnp.int32, (n, n), 1)
    return cmp(row, col)


ATT_T = 256
ATT_DEAD = 104.0


def _any_alive(runs):
    return functools.reduce(jnp.maximum, [jnp.max(r) for r in runs]) > -ATT_DEAD


def _attn_fwd(qkv, wb, name, hp=4, ride=None):
    l = qkv.shape[0]
    t = ATT_T
    nh, nq = wb // HEAD, l // t
    hp = min(hp, nh)
    ng, wg = nh // hp, hp * HEAD
    scale = 1.0 / math.sqrt(HEAD)

    def body(q_ref, k_ref, v_ref, o_ref):
        i = pl.program_id(1)
        valid = _tri(lambda r, c: c < r, t)
        m_gt = _bf(_tri(lambda r, c: r > c, t).astype(F32))

        def tile(j, carry, diag):
            rows = pl.ds(pl.multiple_of(j * t, t), t)
            cols = [slice(hh * HEAD, (hh + 1) * HEAD) for hh in range(hp)]
            zs = [_dot(q_ref[:, cs], k_ref[rows, cs], NT) * scale for cs in cols]
            lbs, lks = [], []
            for z in zs:
                lb, lk, _ = _softplus_parts(z)
                lbs.append(lb)
                lks.append(jnp.where(valid, lk, 0.0) if diag else lk)
            laters = [_dot(_bf(lk), m_gt) for lk in lks]
            ws = [jnp.exp(lb + later + run) for lb, later, (_, run) in zip(lbs, laters, carry)]
            if diag:
                ws = [jnp.where(valid, w, 0.0) for w in ws]
            return tuple((acc + _dot(_bf(w), v_ref[rows, cs]), run + jnp.sum(lk, axis=1, keepdims=True))
                         for w, lk, cs, (acc, run) in zip(ws, lks, cols, carry))

        zero = (jnp.zeros((t, HEAD), F32), jnp.zeros((t, 1), F32))
        carry = tile(i, (zero,) * hp, True)
        _, carry = lax.while_loop(lambda c: (c[0] < i) & _any_alive([run for _, run in c[1]]),
                                  lambda c: (c[0] + 1, tile(i - 1 - c[0], c[1], False)), (jnp.int32(0), carry))
        for hh, (acc, _) in enumerate(carry):
            o_ref[:, hh * HEAD:(hh + 1) * HEAD] = acc

    blk = lambda off: pl.BlockSpec((t, wg), lambda h, i: (i, off + h))
    full = lambda off: pl.BlockSpec((l, wg), lambda h, i: (0, off + h))
    out = pl.BlockSpec((t, wg), lambda h, i: (i, h))
    outs = _call(body, name=name, grid=(ng, nq), in_specs=[blk(0), full(ng), full(2 * ng)], out_specs=[out],
                 out_shape=[jax.ShapeDtypeStruct((l, wb), F32)],
                 args=(qkv, qkv, qkv), sem=("parallel", "arbitrary"), ride=ride)
    return outs[0], outs[1:]


def _attn_bwd(qkv, proj, dcat, wa, wb, name, hp=2, ride=None):
    l = qkv.shape[0]
    t = ATT_T
    nh, nq = wb // HEAD, l // t
    hp = min(hp, nh)
    ng, wg = nh // hp, hp * HEAD
    scale = 1.0 / math.sqrt(HEAD)

    def body(q_ref, k_ref, v_ref, bz_ref, dc_ref, dq_ref, dkt_out, dvt_out, do_s, qt_s, dot_s,
             dkt_ref, dvt_ref, out_sems):
        i = pl.program_id(1)

        @pl.when(i == 0)
        def _():
            dkt_ref[...] = jnp.zeros_like(dkt_ref)
            dvt_ref[...] = jnp.zeros_like(dvt_ref)

        do = dc_ref[...].astype(F32) * _silu(bz_ref[...])
        do_s[...] = _bf(do)
        for hh in range(hp):
            cs = slice(hh * HEAD, (hh + 1) * HEAD)
            qt_s[hh] = _bf(q_ref[:, cs].astype(F32).T * scale)
            dot_s[hh] = _bf(do[:, cs].T)
        valid = _tri(lambda r, c: c < r, t)
        m_le = _bf(_tri(lambda r, c: r <= c, t).astype(F32))
        m_lt = _bf(_tri(lambda r, c: r < c, t).astype(F32))

        heads = range(hp)
        cols = [slice(hh * HEAD, (hh + 1) * HEAD) for hh in heads]

        def row_sums(j, runs, diag):
            rows = pl.ds(pl.multiple_of(j * t, t), t)
            out = []
            for cs, run in zip(cols, runs):
                _, lk, _ = _softplus_parts(_dot(q_ref[:, cs], k_ref[rows, cs], NT) * scale)
                if diag:
                    lk = jnp.where(valid, lk, 0.0)
                out.append(run + jnp.sum(lk, axis=1, keepdims=True))
            return tuple(out)

        runs = row_sums(i, (jnp.zeros((t, 1), F32),) * hp, True)
        below, lktot = lax.while_loop(lambda c: (c[0] < i) & _any_alive(c[1]),
                                      lambda c: (c[0] + 1, row_sums(i - 1 - c[0], c[1], False)), (jnp.int32(0), runs))

        def tile(j, carry, diag):
            rows = pl.ds(pl.multiple_of(j * t, t), t)
            zs = [_dot(q_ref[:, cs], k_ref[rows, cs], NT) * scale for cs in cols]
            dws = [_dot(do_s[:, cs], v_ref[rows, cs], NT) for cs in cols]
            lbs, lks, sigs = [], [], []
            for z in zs:
                lb, lk, sig = _softplus_parts(z)
                lbs.append(lb)
                lks.append(jnp.where(valid, lk, 0.0) if diag else lk)
                sigs.append(sig)
            pins = [_dot(_bf(lk), m_le) for lk in lks]
            ws = [jnp.exp(lbs[hh] + (lktot[hh] - carry[hh][1]) - pins[hh]) for hh in heads]
            if diag:
                ws = [jnp.where(valid, w, 0.0) for w in ws]
            das = [dw * w for dw, w in zip(dws, ws)]
            pexs = [_dot(_bf(da), m_lt) for da in das]
            dzs = [das[hh] - sigs[hh] * (das[hh] + carry[hh][2] + pexs[hh]) for hh in heads]
            if diag:
                dzs = [jnp.where(valid, dz, 0.0) for dz in dzs]
            dzs = [_bf(dz) for dz in dzs]
            out = []
            for hh in heads:
                dkt, dvt = _dot(qt_s[hh], dzs[hh]), _dot(dot_s[hh], _bf(ws[hh]))
                for half in range(t // HEAD):
                    dkt_ref[hh, sub * j + half] += dkt[:, half * HEAD:(half + 1) * HEAD]
                    dvt_ref[hh, sub * j + half] += dvt[:, half * HEAD:(half + 1) * HEAD]
                dq, cpre, ppre = carry[hh]
                out.append((dq + _dot(dzs[hh], k_ref[rows, cols[hh]]), cpre + jnp.sum(lks[hh], axis=1, keepdims=True),
                            ppre + pexs[hh][:, t - 1:] + das[hh][:, t - 1:]))
            return tuple(out)

        zero = (jnp.zeros((t, HEAD), F32), jnp.zeros((t, 1), F32), jnp.zeros((t, 1), F32))
        carry = lax.fori_loop(i - below, i, lambda j, c: tile(j, c, False), (zero,) * hp)
        carry = tile(i, carry, True)
        for hh in range(hp):
            dq_ref[:, hh * HEAD:(hh + 1) * HEAD] = carry[hh][0] * scale

        @pl.when(i == nq - 1)
        def _():
            heads = pl.ds(pl.program_id(0) * hp, hp)
            copies = [pltpu.make_async_copy(dkt_ref, dkt_out.at[heads], out_sems.at[0]),
                      pltpu.make_async_copy(dvt_ref, dvt_out.at[heads], out_sems.at[1])]
            for cp in copies:
                cp.start()
            for cp in copies:
                cp.wait()

    sub = t // HEAD
    blk = lambda off: pl.BlockSpec((t, wg), lambda h, i: (i, off + h))
    full = lambda off: pl.BlockSpec((l, wg), lambda h, i: (0, off + h))
    acc_shape = jax.ShapeDtypeStruct((nh, l // HEAD, HEAD, HEAD), F32)
    acc_scratch = pltpu.VMEM((hp, l // HEAD, HEAD, HEAD), F32)
    outs = _call(
        body, name=name, grid=(ng, nq),
        in_specs=[blk(0), full(ng), full(2 * ng), blk(3 * wa // wg), blk(wa // wg)],
        out_specs=[blk(0), ANY, ANY], out_shape=[jax.ShapeDtypeStruct((l, wb), F32), acc_shape, acc_shape],
        scratch_shapes=[pltpu.VMEM((t, wg), BF16), pltpu.VMEM((hp, HEAD, t), BF16), pltpu.VMEM((hp, HEAD, t), BF16),
                        acc_scratch, acc_scratch, pltpu.SemaphoreType.DMA((2,))],
        args=(qkv, qkv, qkv, proj, dcat), sem=("parallel", "arbitrary"), ride=ride)
    return outs[0], outs[1], outs[2], outs[3:]


def _sgu_heads(v, g_ref, w_ref, bt_ref, nh):
    keep = _tri(lambda r, c: r >= c)
    out = []
    for h in range(nh):
        cols = slice(h * HEAD, (h + 1) * HEAD)
        nv, r = _rownorm(v[:, cols])
        wm = jnp.where(keep, w_ref[h], 0.0)
        s = _dot(_bf(wm), _bf(nv * g_ref[:, cols])) + bt_ref[:, h:h + 1]
        out.append((nv, r, wm, s))
    return out


def _sgu_fwd(proj, out_b, norm_g, sgu_w, sgu_bt, wa, wb, name):
    l, n = proj.shape
    nh = wa // HEAD

    def body(au_ref, av_ref, az_ref, bz_ref, ob_ref, g_ref, w_ref, bt_ref, cat_ref):
        u, v, sz = _gelu(au_ref[...]), _gelu(av_ref[...]), _silu(az_ref[...])
        for h, (_, _, _, s) in enumerate(_sgu_heads(v, g_ref, w_ref, bt_ref, nh)):
            cols = slice(h * HEAD, (h + 1) * HEAD)
            cat_ref[:, cols] = _bf(u[:, cols] * s * sz[:, cols])
        cat_ref[:, wa:] = _bf(ob_ref[...] * _silu(bz_ref[...]))

    a_blk = lambda j: pl.BlockSpec((HEAD, wa), lambda i: (i, j))
    return pl.pallas_call(
        body, name=name, grid=(l // HEAD,),
        in_specs=[a_blk(0), a_blk(1), a_blk(2), a_blk(3), pl.BlockSpec((HEAD, wb), lambda i: (i, 0)),
                  _vec_spec(wa), pl.BlockSpec((nh, HEAD, HEAD), lambda i: (0, 0, 0)),
                  pl.BlockSpec((HEAD, nh), lambda i: (0, 0))],
        out_specs=pl.BlockSpec((HEAD, wa + wb), lambda i: (i, 0)),
        out_shape=jax.ShapeDtypeStruct((l, wa + wb), BF16),
        compiler_params=_params(("parallel",)),
    )(proj, proj, proj, proj, out_b, norm_g, sgu_w, sgu_bt)


def _sgu_bwd(proj, out_b, dcat, dq, dk, dv, norm_g, sgu_w, sgu_bt, wa, wb, name, ride=None):
    l = proj.shape[0]
    n = 3 * wa + 4 * wb
    nh = wa // HEAD

    def body(au_ref, av_ref, az_ref, bz_ref, ob_ref, dc_ref, dq_ref, dk_ref, dv_ref, g_ref, w_ref, wt_ref, bt_ref,
             dp_ref, dw_ref, dbt_ref, dg_ref):
        first = pl.program_id(0) == 0
        keep = _tri(lambda r, c: r >= c)
        au, av, az = au_ref[...], av_ref[...], az_ref[...]
        u, v, sz = _gelu(au), _gelu(av), _silu(az)
        dgelu_u, dgelu_v, dsilu_z = _gelu_grad(au), _gelu_grad(av), _silu_grad(az)
        heads = _sgu_heads(v, g_ref, w_ref, bt_ref, nh)
        cols = [slice(h * HEAD, (h + 1) * HEAD) for h in range(nh)]
        dss = []
        for h, (nv, r, wm, s) in enumerate(heads):
            dca, uh, szh = dc_ref[:, cols[h]].astype(F32), u[:, cols[h]], sz[:, cols[h]]
            dp_ref[:, cols[h]] = _bf(dca * s * szh * dgelu_u[:, cols[h]])
            dp_ref[:, 2 * wa + h * HEAD:2 * wa + (h + 1) * HEAD] = _bf(dca * uh * s * dsilu_z[:, cols[h]])
            dss.append(dca * uh * szh)
        dws = [_dot(_bf(ds), _bf(nv * g_ref[:, cs]), NT) for ds, cs, (nv, _, _, _) in zip(dss, cols, heads)]
        keep_t = _tri(lambda r, c: r <= c)
        dvhs = [_dot(_bf(jnp.where(keep_t, wt_ref[h], 0.0)), _bf(dss[h])) for h in range(nh)]
        dg_parts = []
        for h, (nv, r, wm, s) in enumerate(heads):
            _acc(dw_ref.at[h], first, jnp.where(keep, dws[h], 0.0))
            _acc(dbt_ref.at[:, h:h + 1], first, jnp.sum(dss[h], axis=1, keepdims=True))
            dg_parts.append(_colsum(dvhs[h] * nv))
            dnv = dvhs[h] * g_ref[:, cols[h]]
            dvv = r * (dnv - nv * jnp.mean(dnv * nv, axis=-1, keepdims=True))
            dp_ref[:, wa + h * HEAD:wa + (h + 1) * HEAD] = _bf(dvv * dgelu_v[:, cols[h]])
        _acc(dg_ref, first, jnp.concatenate(dg_parts, axis=1))
        base = 3 * wa
        dp_ref[:, base:base + wb] = _bf(dq_ref[...])
        for h in range(wb // HEAD):
            dp_ref[:, base + wb + h * HEAD:base + wb + (h + 1) * HEAD] = _bf(dk_ref[h, 0].T)
            dp_ref[:, base + 2 * wb + h * HEAD:base + 2 * wb + (h + 1) * HEAD] = _bf(dv_ref[h, 0].T)
        dp_ref[:, base + 3 * wb:] = _bf(dc_ref[:, wa:].astype(F32) * ob_ref[...] * _silu_grad(bz_ref[...]))

    a_blk = lambda j: pl.BlockSpec((HEAD, wa), lambda i: (i, j))
    b_blk = pl.BlockSpec((HEAD, wb), lambda i: (i, 0))
    t_blk = pl.BlockSpec((wb // HEAD, 1, HEAD, HEAD), lambda i: (0, i, 0, 0))
    w_spec = pl.BlockSpec((nh, HEAD, HEAD), lambda i: (0, 0, 0))
    bt_spec = pl.BlockSpec((HEAD, nh), lambda i: (0, 0))
    outs = _call(
        body, name=name, grid=(l // HEAD,), ride=ride, sem=("arbitrary",),
        in_specs=[a_blk(0), a_blk(1), a_blk(2), a_blk(3), b_blk, pl.BlockSpec((HEAD, wa + wb), lambda i: (i, 0)),
                  b_blk, t_blk, t_blk, _vec_spec(wa), w_spec, w_spec, bt_spec],
        out_specs=[pl.BlockSpec((HEAD, n), lambda i: (i, 0)), w_spec, bt_spec, _vec_spec(wa)],
        out_shape=[jax.ShapeDtypeStruct((l, n), BF16), jax.ShapeDtypeStruct((nh, HEAD, HEAD), F32),
                   jax.ShapeDtypeStruct((HEAD, nh), F32), jax.ShapeDtypeStruct((1, wa), F32)],
        args=(proj, proj, proj, proj, out_b, dcat, dq, dk, dv, norm_g, sgu_w, sgu_w.transpose(0, 2, 1), sgu_bt))
    return (*outs[:4], outs[4:])


def _ssm_discretise(lr, li, ldt, br, bi):
    dt = jnp.exp(ldt)
    mag = jnp.exp(lr * dt)
    a_re = mag * jnp.cos(li * dt)
    a_im = mag * jnp.sin(li * dt)
    den = lr * lr + li * li
    nr = a_re - 1.0
    coef_re = (nr * lr + a_im * li) / den
    coef_im = (a_im * lr - nr * li) / den
    return a_re, a_im, coef_re * br - coef_im * bi, coef_re * bi + coef_im * br


def _ssm_prep(lr, li, ldt, br, bi, lr_row, li_row, ldt_row, name):
    s, c = br.shape

    def body(lr_ref, li_ref, ldt_ref, br_ref, bi_ref, lrr_ref, lir_ref, ldtr_ref, bbr_ref, bbi_ref, tr_ref, ti_ref):
        _, _, bbr, bbi = _ssm_discretise(lr_ref[...], li_ref[...], ldt_ref[...], br_ref[...], bi_ref[...])
        bbr_ref[...] = bbr
        bbi_ref[...] = bbi
        row = lax.broadcasted_iota(jnp.int32, (SCAN_ROWS, 1), 0)
        blk, r = jnp.right_shift(row, 3), jnp.bitwise_and(row, 7)
        kind, rev = jnp.bitwise_and(blk, 3), blk >= 4
        step = jnp.left_shift(1, kind)
        n = jnp.where(kind < 3, step, jnp.where(rev, 8 - r, r + 1)).astype(F32)
        keep = (kind == 3) | (rev & (r < 8 - step)) | (jnp.logical_not(rev) & (r >= step))
        dt = jnp.exp(ldtr_ref[...])
        mag = jnp.exp(n * (lrr_ref[...] * dt))
        ang = n * (lir_ref[...] * dt)
        tr_ref[...] = jnp.where(keep, mag * jnp.cos(ang), 0.0)
        ti_ref[...] = jnp.where(keep, jnp.where(rev, -1.0, 1.0) * mag * jnp.sin(ang), 0.0)

    col = jax.ShapeDtypeStruct((s, c), F32)
    row = jax.ShapeDtypeStruct((SCAN_ROWS, s), F32)
    return pl.pallas_call(body, name=name, out_shape=[col, col, row, row])(
        lr, li, ldt, br, bi, lr_row, li_row, ldt_row)


def _ssm_prep_bwd(lr, li, ldt, br, bi, da_re, da_im, dbb_re, dbb_im, p, name):
    s, c = br.shape

    def body(lr_ref, li_ref, ldt_ref, br_ref, bi_ref, dar_ref, dai_ref, dbr_ref, dbi_ref,
             dlr_ref, dli_ref, dldt_ref, dbre_ref, dbim_ref):
        args = (lr_ref[...], li_ref[...], ldt_ref[...], br_ref[...], bi_ref[...])
        _, vjp = jax.vjp(_ssm_discretise, *args)
        dlr, dli, dldt, dbr, dbi = vjp((dar_ref[...], dai_ref[...], dbr_ref[...], dbi_ref[...]))
        dlr_ref[...] = dlr
        dli_ref[...] = dli
        dbre_ref[...] = dbr
        dbim_ref[...] = dbi
        idx = lax.broadcasted_iota(jnp.int32, (s, s // p), 0)
        grp = lax.broadcasted_iota(jnp.int32, (s, s // p), 1)
        own = (idx >= grp * p) & (idx < (grp + 1) * p)
        dldt_ref[...] = _colsum(jnp.where(own, dldt, 0.0))

    col1 = jax.ShapeDtypeStruct((s, 1), F32)
    colc = jax.ShapeDtypeStruct((s, c), F32)
    return pl.pallas_call(
        body, name=name, out_shape=[col1, col1, jax.ShapeDtypeStruct((1, s // p), F32), colc, colc],
    )(lr, li, ldt, br, bi, da_re, da_im, dbb_re, dbb_im)


SCAN_ROWS = 64


def _scan_groups(xr, xi, tr_ref, ti_ref, cr, ci, reverse):
    ng = xr.shape[0] // 8
    base = SCAN_ROWS // 2 if reverse else 0
    pr, pi = tr_ref[base + 24:base + 32, :], ti_ref[base + 24:base + 32, :]
    edge = slice(0, 1) if reverse else slice(7, 8)
    out_r, out_i = [None] * ng, [None] * ng
    for g in (range(ng - 1, -1, -1) if reverse else range(ng)):
        sr, si = xr[8 * g:8 * g + 8, :], xi[8 * g:8 * g + 8, :]
        for k in range(3):
            ar, ai = tr_ref[base + 8 * k:base + 8 * k + 8, :], ti_ref[base + 8 * k:base + 8 * k + 8, :]
            shift = 8 - (1 << k) if reverse else 1 << k
            rr, ri = pltpu.roll(sr, shift, 0), pltpu.roll(si, shift, 0)
            sr, si = sr + ar * rr - ai * ri, si + ar * ri + ai * rr
        sr, si = sr + pr * cr - pi * ci, si + pr * ci + pi * cr
        cr, ci = sr[edge, :], si[edge, :]
        out_r[g], out_i[g] = sr, si
    return jnp.concatenate(out_r, axis=0), jnp.concatenate(out_i, axis=0), cr, ci


def _ssm_fwd(proj, bbd, ccd, pw_re, pw_im, d_skip, w, name):
    l = proj.shape[0]
    nb, cw, ns2 = bbd.shape
    ns = ns2 // 2
    nc = l // SSM_T

    def body(u_ref, bbd_ref, ccd_ref, pr_ref, pi_ref, d_ref, y_ref, hsr_ref, hsi_ref, h_ref, hr_s, hi_s):
        @pl.when(pl.program_id(1) == 0)
        def _():
            hr_s[...] = jnp.zeros_like(hr_s)
            hi_s[...] = jnp.zeros_like(hi_s)

        hsr_ref[...] = hr_s[...].reshape(hsr_ref.shape)
        hsi_ref[...] = hi_s[...].reshape(hsi_ref.shape)
        u = u_ref[...]
        bu = _dot(_bf(u), bbd_ref[0])
        hr, hi, cr, ci = _scan_groups(bu[:, :ns], bu[:, ns:], pr_ref, pi_ref, hr_s[...], hi_s[...], False)
        hr_s[...] = cr
        hi_s[...] = ci
        h_bf = _bf(jnp.concatenate([hr, hi], axis=1))
        h_ref[...] = h_bf
        y_ref[...] = _dot(h_bf, ccd_ref[0]) + d_ref[...] * u

    tab = pl.BlockSpec((SCAN_ROWS, ns), lambda b, k: (0, b))
    return pl.pallas_call(
        body, name=name, grid=(nb, nc),
        in_specs=[pl.BlockSpec((SSM_T, cw), lambda b, k: (k, b)),
                  pl.BlockSpec((1, cw, ns2), lambda b, k: (b, 0, 0)),
                  pl.BlockSpec((1, ns2, cw), lambda b, k: (b, 0, 0)),
                  tab, tab, pl.BlockSpec((1, cw), lambda b, k: (0, b))],
        out_specs=[pl.BlockSpec((SSM_T, cw), lambda b, k: (k, b)),
                   pl.BlockSpec((1, 1, ns), lambda b, k: (k, 0, b)), pl.BlockSpec((1, 1, ns), lambda b, k: (k, 0, b)),
                   pl.BlockSpec((SSM_T, ns2), lambda b, k: (k, b))],
        out_shape=[jax.ShapeDtypeStruct((l, w), F32), jax.ShapeDtypeStruct((nc, 1, nb * ns), F32),
                   jax.ShapeDtypeStruct((nc, 1, nb * ns), F32), jax.ShapeDtypeStruct((l, nb * ns2), BF16)],
        scratch_shapes=[pltpu.VMEM((1, ns), F32), pltpu.VMEM((1, ns), F32)],
        compiler_params=_params(("parallel", "arbitrary")),
    )(proj, bbd, ccd, pw_re, pw_im, d_skip)


def _ssm_bwd(proj, dy, hs_re, hs_im, h_all, bbd, ccd, pw_re, pw_im, d_skip, w, name, ride=None):
    l = proj.shape[0]
    nb, cw, ns2 = bbd.shape
    ns = ns2 // 2
    nc = l // SSM_T

    def body(u_ref, dy_ref, hsr_ref, hsi_ref, h_ref, bbd_ref, ccd_ref, pr_ref, pi_ref, d_ref,
             du_ref, dbbd_ref, dccd_ref, dar_ref, dai_ref, dd_ref, gr_s, gi_s):
        first = pl.program_id(1) == 0

        @pl.when(first)
        def _():
            gr_s[...] = jnp.zeros_like(gr_s)
            gi_s[...] = jnp.zeros_like(gi_s)

        u, dy = u_ref[...], dy_ref[...]
        dy_bf = _bf(dy)
        hr0, hi0 = hsr_ref[0], hsi_ref[0]
        h = h_ref[...].astype(F32)
        hr, hi = h[:, :ns], h[:, ns:]
        dh = _dot(dy_bf, ccd_ref[0], NT)
        gr, gi, gcr, gci = _scan_groups(dh[:, :ns], dh[:, ns:], pr_ref, pi_ref, gr_s[...], gi_s[...], True)
        gr_s[...] = gcr
        gi_s[...] = gci
        row0 = lax.broadcasted_iota(jnp.int32, hr.shape, 0) == 0
        pr_h = jnp.where(row0, hr0, pltpu.roll(hr, 1, 0))
        pi_h = jnp.where(row0, hi0, pltpu.roll(hi, 1, 0))
        _acc(dar_ref, first, _colsum(pr_h * gr + pi_h * gi))
        _acc(dai_ref, first, _colsum(pr_h * gi - pi_h * gr))
        g_bf = _bf(jnp.concatenate([gr, gi], axis=1))
        _acc(dbbd_ref.at[0], first, _dot(_bf(u.T), g_bf))
        _acc(dccd_ref.at[0], first, _dot(_bf(h.T), dy_bf))
        du_ref[...] = _bf(_dot(g_bf, bbd_ref[0], NT) + d_ref[...] * dy)
        _acc(dd_ref, first, _colsum(dy * u))

    rev = lambda b, k: (nc - 1 - k, b)
    outs = _call(
        body, name=name, grid=(nb, nc), ride=ride, sem=("parallel", "arbitrary"),
        args=(proj, dy, hs_re, hs_im, h_all, bbd, ccd, pw_re, pw_im, d_skip),
        in_specs=[pl.BlockSpec((SSM_T, cw), rev), pl.BlockSpec((SSM_T, cw), rev),
                  pl.BlockSpec((1, 1, ns), lambda b, k: (nc - 1 - k, 0, b)),
                  pl.BlockSpec((1, 1, ns), lambda b, k: (nc - 1 - k, 0, b)),
                  pl.BlockSpec((SSM_T, ns2), rev),
                  pl.BlockSpec((1, cw, ns2), lambda b, k: (b, 0, 0)),
                  pl.BlockSpec((1, ns2, cw), lambda b, k: (b, 0, 0)),
                  pl.BlockSpec((SCAN_ROWS, ns), lambda b, k: (0, b)), pl.BlockSpec((SCAN_ROWS, ns), lambda b, k: (0, b)),
                  pl.BlockSpec((1, cw), lambda b, k: (0, b))],
        out_specs=[pl.BlockSpec((SSM_T, cw), rev),
                   pl.BlockSpec((1, cw, ns2), lambda b, k: (b, 0, 0)),
                   pl.BlockSpec((1, ns2, cw), lambda b, k: (b, 0, 0)),
                   pl.BlockSpec((1, ns), lambda b, k: (0, b)), pl.BlockSpec((1, ns), lambda b, k: (0, b)),
                   pl.BlockSpec((1, cw), lambda b, k: (0, b))],
        out_shape=[jax.ShapeDtypeStruct((l, w), BF16), jax.ShapeDtypeStruct(bbd.shape, F32),
                   jax.ShapeDtypeStruct(ccd.shape, F32), jax.ShapeDtypeStruct((1, nb * ns), F32),
                   jax.ShapeDtypeStruct((1, nb * ns), F32), jax.ShapeDtypeStruct((1, w), F32)],
        scratch_shapes=[pltpu.VMEM((1, ns), F32), pltpu.VMEM((1, ns), F32)])
    return (*outs[:6], outs[6:])


def _block_diag_b(bb_re, bb_im, g, p, c):
    nb = g // SSM_GB
    keep = _same_group(SSM_GB * c, c, SSM_GB * p, p)

    def one(bb):
        t = bb.reshape(nb, SSM_GB, p, c).transpose(0, 1, 3, 2).reshape(nb, SSM_GB * c, p)
        return jnp.where(keep, jnp.tile(t, (1, 1, SSM_GB)), 0.0)

    return jnp.concatenate([one(bb_re), one(bb_im)], axis=2)


def _same_group(rows, per_row, cols, per_col):
    r = lax.broadcasted_iota(jnp.int32, (rows, cols), 0) // per_row
    q = lax.broadcasted_iota(jnp.int32, (rows, cols), 1) // per_col
    return r == q


def _block_diag_c(c_re, c_im, g, p, c):
    nb = g // SSM_GB
    keep = _same_group(SSM_GB * p, p, SSM_GB * c, c)

    def one(cc):
        t = cc.reshape(nb, SSM_GB, c, p).transpose(0, 1, 3, 2).reshape(nb, SSM_GB * p, c)
        return jnp.where(keep, jnp.tile(t, (1, 1, SSM_GB)), 0.0)

    return jnp.concatenate([one(c_re), one(-c_im)], axis=1)


def _diag_of_b(dbbd, g, p, c):
    nb = g // SSM_GB
    keep = _same_group(SSM_GB * c, c, SSM_GB * p, p)

    def one(blk):
        d = jnp.where(keep, blk, 0.0).reshape(nb, SSM_GB * c, SSM_GB, p).sum(axis=2)
        return d.reshape(nb, SSM_GB, c, p).transpose(0, 1, 3, 2).reshape(g * p, c)

    half = SSM_GB * p
    return one(dbbd[:, :, :half]), one(dbbd[:, :, half:])


def _diag_of_c(dccd, g, p, c):
    nb = g // SSM_GB
    keep = _same_group(SSM_GB * p, p, SSM_GB * c, c)

    def one(blk):
        d = jnp.where(keep, blk, 0.0).reshape(nb, SSM_GB * p, SSM_GB, c).sum(axis=2)
        return d.reshape(nb, SSM_GB, p, c).transpose(0, 1, 3, 2).reshape(g, c, p)

    half = SSM_GB * p
    return one(dccd[:, :half]), -one(dccd[:, half:])


def _glu_fwd(y, proj, w_glu, b_glu, name):
    l, w = y.shape
    tm = _tile(l, 256)

    def body(y_ref, z_ref, w_ref, b_ref, o_ref):
        g = _gelu(y_ref[...])
        t = _dot(_bf(g), w_ref[...]) + b_ref[...]
        o_ref[...] = _bf(g * _sigmoid(t) * _silu(z_ref[...]))

    return pl.pallas_call(
        body, name=name, grid=(l // tm,),
        in_specs=[_row_spec(tm, w), pl.BlockSpec((tm, w), lambda i: (i, 1)),
                  pl.BlockSpec((w, w), lambda i: (0, 0)), _vec_spec(w)],
        out_specs=_row_spec(tm, w), out_shape=jax.ShapeDtypeStruct((l, w), BF16),
        compiler_params=_params(("parallel",)),
    )(y, proj, w_glu, b_glu)


def _glu_bwd(do, y, proj, w_glu, b_glu, name):
    l, w = y.shape
    tm = _tile(l, 512)
    nsteps = l // tm

    def body(do_ref, y_ref, z_ref, w_ref, b_ref, dy_ref, dz_ref, dw_ref, db_ref, dw_acc):
        i = pl.program_id(0)
        first = i == 0
        yv, z, do = y_ref[...], z_ref[...], do_ref[...].astype(F32)
        g = _gelu(yv)
        g_bf = _bf(g)
        sg = _sigmoid(_dot(g_bf, w_ref[...]) + b_ref[...])
        dyy = do * _silu(z)
        dz_ref[...] = _bf(do * g * sg * _silu_grad(z))
        dt = dyy * g * sg * (1.0 - sg)
        dt_bf = _bf(dt)
        dg = dyy * sg + _dot(dt_bf, w_ref[...], NT)
        dy_ref[...] = dg * _gelu_grad(yv)
        _acc(dw_acc, first, _dot(_bf(g.T), dt_bf))
        _acc(db_ref, first, _colsum(dt))

        @pl.when(i == nsteps - 1)
        def _():
            dw_ref[...] = _bf(dw_acc[...])

    return pl.pallas_call(
        body, name=name, grid=(nsteps,),
        in_specs=[_row_spec(tm, w), _row_spec(tm, w), pl.BlockSpec((tm, w), lambda i: (i, 1)),
                  pl.BlockSpec((w, w), lambda i: (0, 0)), _vec_spec(w)],
        out_specs=[_row_spec(tm, w), _row_spec(tm, w), pl.BlockSpec((w, w), lambda i: (0, 0)), _vec_spec(w)],
        out_shape=[jax.ShapeDtypeStruct((l, w), F32), jax.ShapeDtypeStruct((l, w), BF16),
                   jax.ShapeDtypeStruct((w, w), BF16), jax.ShapeDtypeStruct((1, w), F32)],
        scratch_shapes=[pltpu.VMEM((w, w), F32)],
        compiler_params=_params(("arbitrary",)),
    )(do, y, proj, w_glu, b_glu)


MOD_ROWS = 128


def _mod_fwd(cond_pad, w_mod, b_shard, name):
    nl, d, ncol = w_mod.shape
    tn = _tile(ncol, 512)

    def body(c_ref, w_ref, b_ref, o_ref):
        o_ref[0] = _dot(_bf(c_ref[...]), _bf(w_ref[0])) + b_ref[0]

    return pl.pallas_call(
        body, name=name, grid=(nl, ncol // tn),
        in_specs=[pl.BlockSpec((MOD_ROWS, d), lambda a, j: (0, 0)),
                  pl.BlockSpec((1, d, tn), lambda a, j: (a, 0, j)),
                  pl.BlockSpec((1, 1, tn), lambda a, j: (a, 0, j))],
        out_specs=pl.BlockSpec((1, MOD_ROWS, tn), lambda a, j: (a, 0, j)),
        out_shape=jax.ShapeDtypeStruct((nl, MOD_ROWS, ncol), F32),
        compiler_params=_params(("parallel", "parallel")),
    )(cond_pad, w_mod, b_shard)


def _mod_bwd(cond_pad_t, dmod_pad, name):
    nl, _, ncol = dmod_pad.shape
    d = cond_pad_t.shape[0]
    tn = _tile(ncol, 512)

    def body(c_ref, dm_ref, o_ref):
        o_ref[0] = _dot(_bf(c_ref[...]), _bf(dm_ref[0]))

    return pl.pallas_call(
        body, name=name, grid=(nl, ncol // tn),
        in_specs=[pl.BlockSpec((d, MOD_ROWS), lambda a, j: (0, 0)),
                  pl.BlockSpec((1, MOD_ROWS, tn), lambda a, j: (a, 0, j))],
        out_specs=pl.BlockSpec((1, d, tn), lambda a, j: (a, 0, j)),
        out_shape=jax.ShapeDtypeStruct((nl, d, ncol), F32),
        compiler_params=_params(("parallel", "parallel")),
    )(cond_pad_t, dmod_pad)


def _silu_rows(c2d, name):
    def body(c_ref, o_ref):
        o_ref[...] = _silu(c_ref[...])

    return pl.pallas_call(body, name=name, out_shape=jax.ShapeDtypeStruct(c2d.shape, F32))(c2d)


def _sum_leading(x, name):
    n, r, c = x.shape
    tr = _tile(r, max(16, (1 << 20) // (4 * c)), 16 if r % 16 == 0 else 8)

    def body(x_ref, o_ref):
        acc = x_ref[0].astype(F32)
        for k in range(1, n):
            acc = acc + x_ref[k].astype(F32)
        o_ref[...] = acc

    return pl.pallas_call(
        body, name=name, grid=(r // tr,),
        in_specs=[pl.BlockSpec((n, tr, c), lambda i: (0, i, 0))], out_specs=pl.BlockSpec((tr, c), lambda i: (i, 0)),
        out_shape=jax.ShapeDtypeStruct((r, c), F32), compiler_params=_params(("parallel",)),
    )(x)


def _adamw(w, gs, m, v, name, ride=None):
    r, c = w.shape
    tr = _tile(r, max(8, (3 << 19) // (4 * c)), 8)
    ng = len(gs)

    def body(*refs):
        w_ref, g_refs, m_ref, v_ref = refs[0], refs[1:1 + ng], refs[1 + ng], refs[2 + ng]
        g_ref, d_ref, nm_ref, nv_ref = refs[3 + ng:]
        g = g_refs[0][...]
        for extra in g_refs[1:]:
            g = g + extra[...]
        g_ref[...] = g
        d_ref[...], nm_ref[...], nv_ref[...] = _adamw_math(w_ref[...], g, m_ref[...], v_ref[...])

    spec = pl.BlockSpec((tr, c), lambda i: (i, 0))
    shp = jax.ShapeDtypeStruct((r, c), F32)
    outs = _call(body, name=name, grid=(r // tr,), in_specs=[spec] * (3 + ng), out_specs=[spec] * 4,
                 out_shape=[shp] * 4, args=(w, *gs, m, v), sem=("parallel",), ride=ride)
    return outs if ride is None else (outs[:4], outs[4:])


def _adamw_math(w, g, m, v):
    nm = ADAM_B1 * m + (1.0 - ADAM_B1) * g
    nv = ADAM_B2 * v + (1.0 - ADAM_B2) * (g * g)
    m_hat = nm / (1.0 - ADAM_B1 ** ADAM_STEP)
    v_hat = nv / (1.0 - ADAM_B2 ** ADAM_STEP)
    return -ADAM_LR * (m_hat / (jnp.sqrt(v_hat) + ADAM_EPS) + ADAM_WD * w), nm, nv


def _adamw_many(ws, gs, ms, vs, name):
    n = len(ws)

    def body(*refs):
        w_refs, g_refs, m_refs, v_refs = (refs[k * n:(k + 1) * n] for k in range(4))
        outs = refs[4 * n:]
        for i in range(n):
            outs[3 * i][...], outs[3 * i + 1][...], outs[3 * i + 2][...] = _adamw_math(
                w_refs[i][...], g_refs[i][...], m_refs[i][...], v_refs[i][...])

    out_shape = [jax.ShapeDtypeStruct(w.shape, F32) for w in ws for _ in range(3)]
    outs = pl.pallas_call(body, name=name, out_shape=out_shape, compiler_params=_params())(*ws, *gs, *ms, *vs)
    return [tuple(outs[3 * i:3 * i + 3]) for i in range(n)]


ANY = pl.BlockSpec(memory_space=pl.ANY)


def _flip(v, bit):
    return 1 - v if bit else v


def _allgather8_ops(x_ref, o_ref, send_sems, recv_sems, local_sem):
    mx, my, mc = lax.axis_index("x"), lax.axis_index("y"), lax.axis_index("c")
    me = 4 * mx + 2 * my + mc

    def mine():
        return pltpu.make_async_copy(x_ref, o_ref.at[me], local_sem)

    def copy(j, outgoing):
        peer = (_flip(mx, j & 4), _flip(my, j & 2), _flip(mc, j & 1))
        slot = me if outgoing else 4 * peer[0] + 2 * peer[1] + peer[2]
        return pltpu.make_async_remote_copy(
            src_ref=x_ref, dst_ref=o_ref.at[slot], send_sem=send_sems.at[j - 1], recv_sem=recv_sems.at[j - 1],
            device_id=peer, device_id_type=MESH)

    def start():
        mine().start()
        for j in range(1, 8):
            copy(j, True).start()

    def wait():
        for j in range(1, 8):
            copy(j, False).wait()
        mine().wait()

    return start, wait


def _ride_all8(x):
    return dict(xs=[x], shapes=[jax.ShapeDtypeStruct((8,) + x.shape, x.dtype)],
                sems=[pltpu.SemaphoreType.DMA((7,)), pltpu.SemaphoreType.DMA((7,)), pltpu.SemaphoreType.DMA],
                ops=lambda x_refs, o_refs, sems: _allgather8_ops(x_refs[0], o_refs[0], *sems))


def _ride_chip(xs, gather):
    return dict(xs=list(xs), shapes=_chip_exchange_shapes(xs, gather), sems=_chip_exchange_sems(len(xs)),
                ops=lambda x_refs, o_refs, sems: _chip_exchange_ops(x_refs, o_refs, *sems, gather))


def _allgather8(x, name):
    def body(x_ref, o_ref, *sems):
        start, wait = _allgather8_ops(x_ref, o_ref, *sems)
        start()
        wait()

    ride = _ride_all8(x)
    return pl.pallas_call(body, name=name, in_specs=[ANY], out_specs=ANY, out_shape=ride["shapes"][0],
                          scratch_shapes=ride["sems"])(x)


def _gather_halves_ops(x_ref, o_ref, ici_send, ici_recv, d2d_send, d2d_recv, local_sem):
    half = x_ref.shape[0] // 2
    mx, my, mc = lax.axis_index("x"), lax.axis_index("y"), lax.axis_index("c")
    k0 = 2 * mx + my
    mine = pl.ds(pl.multiple_of(mc * half, 16), half)
    theirs = pl.ds(pl.multiple_of((1 - mc) * half, 16), half)

    def local():
        return pltpu.make_async_copy(x_ref, o_ref.at[k0], local_sem)

    def chips(j):
        px, py = _flip(mx, j & 2), _flip(my, j & 1)
        return px, py, 2 * px + py

    def over_ici(j, outgoing):
        px, py, kp = chips(j)
        dst = o_ref.at[k0, mine] if outgoing else o_ref.at[kp, mine]
        return pltpu.make_async_remote_copy(
            src_ref=x_ref.at[mine], dst_ref=dst, send_sem=ici_send.at[j - 1], recv_sem=ici_recv.at[j - 1],
            device_id=(px, py, mc), device_id_type=MESH)

    def over_d2d(j, outgoing):
        _, _, kp = chips(j)
        rows = mine if outgoing else theirs
        return pltpu.make_async_remote_copy(
            src_ref=o_ref.at[kp, rows], dst_ref=o_ref.at[kp, rows], send_sem=d2d_send.at[j - 1],
            recv_sem=d2d_recv.at[j - 1], device_id=(mx, my, 1 - mc), device_id_type=MESH)

    def start():
        local().start()
        for j in range(1, 4):
            over_ici(j, True).start()

    def wait():
        for j in range(1, 4):
            over_ici(j, False).wait_recv()
            over_d2d(j, True).start()
        for j in range(1, 4):
            over_ici(j, True).wait_send()
            over_d2d(j, True).wait_send()
            over_d2d(j, False).wait_recv()
        local().wait()

    return start, wait


def _ride_halves(x):
    dma3 = pltpu.SemaphoreType.DMA((3,))
    return dict(xs=[x], shapes=[jax.ShapeDtypeStruct((4,) + x.shape, x.dtype)],
                sems=[dma3, dma3, dma3, dma3, pltpu.SemaphoreType.DMA],
                ops=lambda x_refs, o_refs, sems: _gather_halves_ops(x_refs[0], o_refs[0], *sems))


def _chip_exchange(xs, gather, name):
    n = len(xs)

    def body(*refs):
        start, wait = _chip_exchange_ops(refs[:n], refs[n:2 * n], *refs[2 * n:], gather)
        start()
        wait()

    return pl.pallas_call(
        body, name=name, in_specs=[ANY] * n, out_specs=[ANY] * n, out_shape=_chip_exchange_shapes(xs, gather),
        scratch_shapes=_chip_exchange_sems(n),
    )(*xs)


def _chip_exchange_shapes(xs, gather):
    return [jax.ShapeDtypeStruct(((4,) + x.shape) if gather else x.shape, x.dtype) for x in xs]


def _chip_exchange_sems(n):
    return [pltpu.SemaphoreType.DMA((3 * n,)), pltpu.SemaphoreType.DMA((3 * n,)), pltpu.SemaphoreType.DMA((n,))]


def _chip_exchange_ops(x_refs, o_refs, send_sems, recv_sems, local_sems, gather):
    n = len(x_refs)
    mx, my, mc = lax.axis_index("x"), lax.axis_index("y"), lax.axis_index("c")
    k0 = 2 * mx + my

    def local(a):
        src = x_refs[a] if gather else x_refs[a].at[k0]
        return pltpu.make_async_copy(src, o_refs[a].at[k0], local_sems.at[a])

    def copy(a, j, outgoing):
        px, py = _flip(mx, j & 2), _flip(my, j & 1)
        kp = 2 * px + py
        if outgoing:
            src = x_refs[a] if gather else x_refs[a].at[kp]
            dst = o_refs[a].at[k0]
        else:
            src = x_refs[a] if gather else x_refs[a].at[k0]
            dst = o_refs[a].at[kp]
        s = a * 3 + j - 1
        return pltpu.make_async_remote_copy(
            src_ref=src, dst_ref=dst, send_sem=send_sems.at[s], recv_sem=recv_sems.at[s],
            device_id=(px, py, mc), device_id_type=MESH)

    def start():
        for a in range(n):
            local(a).start()
            for j in range(1, 4):
                copy(a, j, True).start()

    def wait():
        for a in range(n):
            for j in range(1, 4):
                copy(a, j, False).wait()
            local(a).wait()

    return start, wait


def _call(body, *, name, grid, in_specs, out_specs, out_shape, args, scratch_shapes=(), sem=None, ride=None):
    if not ride:
        return pl.pallas_call(
            body, name=name, grid=grid, in_specs=list(in_specs), out_specs=list(out_specs), out_shape=list(out_shape),
            scratch_shapes=list(scratch_shapes), compiler_params=_params(sem))(*args)
    xs = [x for r in ride for x in r["xs"]]
    shapes = [s for r in ride for s in r["shapes"]]
    sems = [s for r in ride for s in r["sems"]]
    n_in, n_out, n_scr, nx = len(in_specs), len(out_specs), len(scratch_shapes), len(xs)

    def wrapped(*refs):
        ins, x_refs = refs[:n_in], refs[n_in:n_in + nx]
        outs = refs[n_in + nx:n_in + nx + n_out]
        lands = refs[n_in + nx + n_out:n_in + 2 * nx + n_out]
        rest = refs[n_in + 2 * nx + n_out:]
        scr, sem_refs = rest[:n_scr], rest[n_scr:]
        ops, xo, so = [], 0, 0
        for r in ride:
            nr, ns = len(r["xs"]), len(r["sems"])
            ops.append(r["ops"](x_refs[xo:xo + nr], lands[xo:xo + nr], sem_refs[so:so + ns]))
            xo, so = xo + nr, so + ns
        ids = [pl.program_id(a) for a in range(len(grid))]
        first = functools.reduce(jnp.logical_and, [i == 0 for i in ids])
        last = functools.reduce(jnp.logical_and, [i == g - 1 for i, g in zip(ids, grid)])

        @pl.when(first)
        def _():
            for start, _ in ops:
                start()

        body(*ins, *outs, *scr)

        @pl.when(last)
        def _():
            for _, wait in ops:
                wait()

    return pl.pallas_call(
        wrapped, name=name, grid=grid, in_specs=list(in_specs) + [ANY] * nx, out_specs=list(out_specs) + [ANY] * nx,
        out_shape=list(out_shape) + shapes, scratch_shapes=list(scratch_shapes) + sems,
        compiler_params=_params(("arbitrary",) * len(grid)))(*args, *xs)


def _sibling_exchange(xs, name):
    ride = _ride_sibling(xs)
    n = len(xs)

    def body(*refs):
        start, wait = ride["ops"](refs[:n], refs[n:2 * n], refs[2 * n:])
        start()
        wait()

    return pl.pallas_call(body, name=name, in_specs=[ANY] * n, out_specs=[ANY] * n, out_shape=ride["shapes"],
                          scratch_shapes=ride["sems"])(*xs)


def _ride_sibling(xs):
    n = len(xs)

    def ops(x_refs, o_refs, sems):
        send_sems, recv_sems = sems
        sib = (lax.axis_index("x"), lax.axis_index("y"), 1 - lax.axis_index("c"))

        def copies():
            return [pltpu.make_async_remote_copy(
                src_ref=x_refs[a], dst_ref=o_refs[a], send_sem=send_sems.at[a], recv_sem=recv_sems.at[a],
                device_id=sib, device_id_type=MESH) for a in range(n)]

        def start():
            for cp in copies():
                cp.start()

        def wait():
            for cp in copies():
                cp.wait()

        return start, wait

    return dict(xs=list(xs), shapes=[jax.ShapeDtypeStruct(x.shape, x.dtype) for x in xs],
                sems=[pltpu.SemaphoreType.DMA((n,)), pltpu.SemaphoreType.DMA((n,))], ops=ops)


PACK = 1024
PACK_ROWS = 512


def _pack(parts):
    flat = []
    for p in parts:
        v = p.reshape(-1).astype(F32)
        flat.append(jnp.pad(v, (0, (-v.shape[0]) % PACK)))
    total = sum(v.shape[0] for v in flat)
    flat.append(jnp.zeros(((-total) % (PACK_ROWS * 128),), F32))
    return jnp.concatenate(flat).reshape(-1, 128)


def _shard_columns(shards, lo, hi):
    width = shards.shape[2]
    out = []
    for k in range(shards.shape[0]):
        a, b = max(lo, k * width), min(hi, (k + 1) * width)
        if a < b:
            out.append(shards[k, :, a - k * width:b - k * width])
    return out


def _unpack_rows(gathered, shapes):
    flat = gathered.reshape(gathered.shape[0], -1)
    out, off = [], 0
    for shp in shapes:
        n = math.prod(shp)
        out.append(flat[:, off:off + n].reshape((flat.shape[0],) + tuple(shp)))
        off += n + (-n) % PACK
    return out


def _unpack(packed, shapes):
    flat = packed.reshape(-1)
    out, off = [], 0
    for shp in shapes:
        n = math.prod(shp)
        out.append(flat[off:off + n].reshape(shp))
        off += n + (-n) % PACK
    return out


def kernel(x, c, ln_pre_g, ln_post_g, w_mod, b_mod, w_in_ab, w_out_ab, sgu_norm_g, sgu_w, sgu_b, w_in_ssm, w_out_ssm, lam_re, lam_im, b_re, b_im, c_re, c_im, d_skip, log_dt, w_glu, b_glu, loss_target, m_ln_pre_g, m_ln_post_g, m_w_mod, m_b_mod, m_w_in_ab, m_w_out_ab, m_sgu_norm_g, m_sgu_w, m_sgu_b, m_w_in_ssm, m_w_out_ssm, m_lam_re, m_lam_im, m_b_re, m_b_im, m_c_re, m_c_im, m_d_skip, m_log_dt, m_w_glu, m_b_glu, v_ln_pre_g, v_ln_post_g, v_w_mod, v_b_mod, v_w_in_ab, v_w_out_ab, v_sgu_norm_g, v_sgu_w, v_sgu_b, v_w_in_ssm, v_w_out_ssm, v_lam_re, v_lam_im, v_b_re, v_b_im, v_c_re, v_c_im, v_d_skip, v_log_dt, v_w_glu, v_b_glu):
    given = dict(locals())
    mx, my, mc = lax.axis_index("x"), lax.axis_index("y"), lax.axis_index("c")
    me = 4 * mx + 2 * my + mc
    chip = 2 * mx + my

    _, l, d = x.shape
    x2, tgt = x[0], loss_target[0]
    n_in = w_in_ab.shape[2] * 4
    wa = wb = n_in // 7
    w = w_out_ssm.shape[1]
    g, p, cch = b_re.shape[1:]
    nmod = w_mod.shape[2]


    cond = _silu_rows(c.reshape(d // 128, 128), "cond_silu")
    cond_all = _allgather8(cond, "gather_cond").reshape(8, d)
    b_shard = lax.dynamic_slice(b_mod, (0, chip * nmod), (2, nmod)).reshape(2, 1, nmod)
    cond_pad = jnp.pad(cond_all, ((0, MOD_ROWS - 8), (0, 0)))
    modp = _mod_fwd(cond_pad, w_mod, b_shard, "mod_fwd")[:, :8]
    modp_all = _allgather8(modp.reshape(16, nmod), "gather_mod").reshape(4, 2, 2, 8, nmod)
    mine = lax.dynamic_index_in_dim(lax.dynamic_index_in_dim(modp_all, mc, 1, False), me, 2, False)
    mod = mine.transpose(1, 0, 2).reshape(2, 3 * d)
    shift = [mod[a:a + 1, :d] for a in range(2)]
    scale = [mod[a:a + 1, d:2 * d] for a in range(2)]
    gate = [mod[a:a + 1, 2 * d:] for a in range(2)]
    pre_g = [ln_pre_g[a:a + 1] for a in range(2)]
    post_g = [ln_post_g[a:a + 1] for a in range(2)]

    sgu_w0, sgu_bt = sgu_w[0], sgu_b[0].T
    h0, (gw_in_ab,) = _pre_fwd(x2, pre_g[0], scale[0], shift[0], "pre0_fwd", ride=[_ride_halves(_bf(w_in_ab[0]))])
    w_gates = jnp.concatenate(_shard_columns(gw_in_ab, 0, 3 * wa) + _shard_columns(gw_in_ab, 3 * wa + 3 * wb, n_in),
                              axis=1)
    w_qkv = jnp.concatenate(_shard_columns(gw_in_ab, 3 * wa, 3 * wa + 3 * wb), axis=1)
    proj0, (gw_in_ssm,) = _matmul(h0, w_gates, "nn", F32, "proj0", tm=1024, ride=[_ride_chip([_bf(w_in_ssm[0])], True)])
    qkv, (gw_out_ssm, gw_glu, g_dskip, g_bglu) = _matmul(
        h0, w_qkv, "nn", BF16, "proj0_qkv", tm=1024,
        ride=[_ride_chip([_bf(w_out_ssm[0]), _bf(w_glu[0]), d_skip, b_glu], True)])
    out_b, (gw_out_ab,) = _attn_fwd(qkv, wb, "attn_fwd", hp=8, ride=[_ride_chip([_bf(w_out_ab[0])], True)])
    wout_ab = gw_out_ab.reshape(wa + wb, d)
    win_ssm = gw_in_ssm.reshape(d, 2 * w)
    wout_ssm = jnp.concatenate([gw_out_ssm[k] for k in range(4)], axis=1)
    wglu = gw_glu.reshape(w, w)
    dskip_full = g_dskip.reshape(1, w)
    bglu_full = g_bglu.reshape(1, w)
    cat =_sgu_fwd(proj0, out_b, sgu_norm_g, sgu_w0, sgu_bt, wa, wb, "sgu_fwd")
    y0 = _matmul(cat, wout_ab, "nn", BF16, "out0", tm=1024)
    x1, h1 = _post_pre_fwd(x2, y0, gate[0], post_g[0], pre_g[1], scale[1], shift[1], "post0_pre1_fwd")

    s = g * p
    lr_c, li_c = lam_re.reshape(s, 1), lam_im.reshape(s, 1)
    ldt_c = jnp.repeat(log_dt.reshape(g), p).reshape(s, 1)
    br_c, bi_c = b_re.reshape(s, cch), b_im.reshape(s, cch)
    bb_re, bb_im, pw_re, pw_im = _ssm_prep(lr_c, li_c, ldt_c, br_c, bi_c, lr_c.reshape(1, s), li_c.reshape(1, s),
                                           ldt_c.reshape(1, s), "ssm_prep")
    bbd = _bf(_block_diag_b(bb_re, bb_im, g, p, cch))
    ccd = _bf(_block_diag_c(c_re[0], c_im[0], g, p, cch))
    proj1 = _matmul(h1, win_ssm, "nn", F32, "proj1", tm=1024)
    y_ssm, hs_re, hs_im, h_all = _ssm_fwd(proj1, bbd, ccd, pw_re, pw_im, dskip_full, w, "ssm_fwd")
    o1 = _glu_fwd(y_ssm, proj1, wglu, bglu_full, "glu_fwd")
    y1 = _matmul(o1, wout_ssm, "nn", BF16, "out1", tm=1024)
    loss_vec, dy1, dx2, dgate1, dpost1 = _post_loss(x1, y1, gate[1], post_g[1], tgt, "post1_loss")

    do1 = _matmul(dy1, wout_ssm, "nt", BF16, "out1_dx", tm=1024)
    gr_wout_ssm = _matmul(o1, dy1, "tn", BF16, "out1_dw", tm=1024, tk=1024, n_split=4)
    dy_ssm, dz1, gr_wglu, gr_bglu = _glu_bwd(do1, y_ssm, proj1, wglu, bglu_full, "glu_bwd")
    du1, dbbd, dccd, da_re, da_im, gr_dskip, (ld_wout_ssm, ld_wglu) = _ssm_bwd(
        proj1, dy_ssm, hs_re, hs_im, h_all, bbd, ccd, pw_re, pw_im, dskip_full, w, "ssm_bwd",
        ride=[_ride_chip([gr_wout_ssm, gr_wglu.reshape(4, w // 4, w)], False)])
    dproj1 = jnp.concatenate([du1, dz1], axis=1)
    dh1 = _matmul(dproj1, win_ssm, "nt", BF16, "proj1_dx", tm=1024)
    gr_win_ssm = _matmul(h1, dproj1, "tn", BF16, "proj1_dw", tm=1024, tn=1024, tk=1024)
    dx1, dscale1, dshift1, dpre1, dy0, dgate0, dpost0 = _pre_bwd(
        dh1, dx2, x1, pre_g[1], scale[1], "pre1_post0_bwd", post=(y0, gate[0], post_g[0]))

    dcat = _matmul(dy0, wout_ab, "nt", BF16, "out0_dx", tm=1024)
    gr_wout_ab = _matmul(cat, dy0, "tn", BF16, "out0_dw", tm=1024, tn=1024, tk=1024)
    dbb_re, dbb_im = _diag_of_b(dbbd, g, p, cch)
    dc_re, dc_im = _diag_of_c(dccd, g, p, cch)
    part_a = [loss_vec[:, :1], dpre1, dpost0, dpost1, dgate0, dshift1, dscale1, dgate1, da_re, da_im,
              dbb_re, dbb_im, dc_re, dc_im, gr_dskip, gr_bglu]
    shapes_a = [a.shape for a in part_a]
    dq, dk, dv, (ld_win_ssm, ld_wout_ab, gath_a) = _attn_bwd(
        qkv, proj0, dcat, wa, wb, "attn_bwd", hp=4,
        ride=[_ride_chip([gr_win_ssm.reshape(4, d // 4, 2 * w), gr_wout_ab.reshape(4, (wa + wb) // 4, d)], False),
              _ride_all8(_pack(part_a))])
    early_names = ["w_out_ab", "w_in_ssm", "w_out_ssm", "w_glu"]
    early_sums = [_sum_leading(a, "sum_" + nm) for a, nm in zip([ld_wout_ab, ld_win_ssm, ld_wout_ssm, ld_wglu], early_names)]
    dproj0, gr_sgu_w, gr_sgu_bt, gr_sgu_g, early_sib = _sgu_bwd(
        proj0, out_b, dcat, dq, dk, dv, sgu_norm_g, sgu_w0, sgu_bt, wa, wb, "sgu_bwd", ride=[_ride_sibling(early_sums)])
    part_b = [gr_sgu_g, gr_sgu_w, gr_sgu_bt.T]
    shapes_b = [a.shape for a in part_b]
    gr_win_ab_lo, (gath_b,) = _matmul(h0, dproj0, "tn", BF16, "proj0_dw_lo", tm=1024, tk=1024, tn=896, n_split=4,
                                      m_part=(0, 1, 2), ride=[_ride_all8(_pack(part_b))])
    gr_win_ab_hi, (ld_win_ab_lo,) = _matmul(
        h0, dproj0, "tn", BF16, "proj0_dw_hi", tm=1024, tk=1024, tn=896, n_split=4, m_part=(1, 1, 2),
        ride=[_ride_chip([gr_win_ab_lo], False)])
    dh0, (ld_win_ab_hi,) = _matmul(dproj0, gw_in_ab, "nt", BF16, "proj0_dx", tm=1024, tn=1024,
                                   ride=[_ride_chip([gr_win_ab_hi], False)])
    grad_x, dscale0, dshift0, dpre0 = _pre_bwd(dh0, dx1, x2, pre_g[0], scale[0], "pre0_bwd")
    part_c = [dpre0, dshift0, dscale0]
    shapes_c = [a.shape for a in part_c]
    gath_c = _allgather8(_pack(part_c), "gather_small_tail")

    big_names = ["w_in_ab"] + early_names
    sum_win_ab = jnp.concatenate([_sum_leading(ld_win_ab_lo, "sum_w_in_ab_lo"),
                                  _sum_leading(ld_win_ab_hi, "sum_w_in_ab_hi")], axis=0)
    sums = [sum_win_ab] + early_sums
    sib = list(_sibling_exchange([sum_win_ab], "sibling_w_in_ab")) + list(early_sib)
    results = {}
    for nm, s_mine, s_sib in zip(big_names, sums, sib):
        shp = given[nm].shape
        two_d = lambda a: a.reshape(-1, shp[-1])
        outs = _adamw(two_d(given[nm]), [s_mine, s_sib], two_d(given["m_" + nm]), two_d(given["v_" + nm]),
                      "adamw_" + nm)
        results[nm] = [o.reshape(shp) for o in outs]

    (loss_s, g_pre1, g_post0, g_post1, g_gate0, g_shift1, g_scale1, g_gate1, s_da_re, s_da_im, s_dbb_re, s_dbb_im,
     g_c_re, g_c_im, g_dskip_full, g_bglu_full) = _unpack(_sum_leading(gath_a, "sum_small_a"), shapes_a)
    g_sgu_g, g_sgu_w, g_sgu_b = _unpack(_sum_leading(gath_b, "sum_small_b"), shapes_b)
    g_pre0, g_shift0, g_scale0 = _unpack(_sum_leading(gath_c, "sum_small_c"), shapes_c)
    loss = loss_s.reshape(())
    g_pre = jnp.concatenate([g_pre0, g_pre1], axis=0)
    g_post = jnp.concatenate([g_post0, g_post1], axis=0)
    g_bmod = jnp.concatenate([jnp.concatenate([g_shift0, g_scale0, g_gate0], axis=1),
                              jnp.concatenate([g_shift1, g_scale1, g_gate1], axis=1)], axis=0)

    g_lr, g_li, g_ldt, g_br, g_bi = _ssm_prep_bwd(lr_c, li_c, ldt_c, br_c, bi_c, s_da_re.reshape(s, 1),
                                                  s_da_im.reshape(s, 1), s_dbb_re, s_dbb_im, p, "ssm_prep_bwd")
    small = {
        "ln_pre_g": g_pre, "ln_post_g": g_post, "b_mod": g_bmod, "sgu_norm_g": g_sgu_g,
        "sgu_w": g_sgu_w.reshape(sgu_w.shape), "sgu_b": g_sgu_b.reshape(sgu_b.shape),
        "lam_re": g_lr.reshape(lam_re.shape), "lam_im": g_li.reshape(lam_im.shape),
        "b_re": g_br.reshape(b_re.shape), "b_im": g_bi.reshape(b_im.shape),
        "c_re": g_c_re.reshape(c_re.shape), "c_im": g_c_im.reshape(c_im.shape),
        "d_skip": lax.dynamic_slice(g_dskip_full, (0, chip * (w // 4)), (1, w // 4)),
        "log_dt": g_ldt.reshape(log_dt.shape),
        "b_glu": lax.dynamic_slice(g_bglu_full, (0, chip * (w // 4)), (1, w // 4)),
    }
    flat2 = lambda a: a.reshape(-1, a.shape[-1])
    wide = ("b_re", "b_im")
    for tag, group in (("adamw_small", [nm for nm in small if nm not in wide]), ("adamw_small_b", list(wide))):
        outs = _adamw_many([flat2(given[nm]) for nm in group], [flat2(small[nm]) for nm in group],
                           [flat2(given["m_" + nm]) for nm in group], [flat2(given["v_" + nm]) for nm in group], tag)
        for nm, trio in zip(group, outs):
            results[nm] = [small[nm]] + [o.reshape(given[nm].shape) for o in trio]

    rows_a = _unpack_rows(gath_a, shapes_a)
    rows_c = _unpack_rows(gath_c, shapes_c)
    dmod_rows = jnp.concatenate([rows_c[1], rows_c[2], rows_a[4], rows_a[5], rows_a[6], rows_a[7]],
                                axis=2).reshape(8, 2, 3 * d)
    dmod_shard = lax.dynamic_slice(dmod_rows, (0, 0, chip * nmod), (8, 2, nmod)).transpose(1, 0, 2)
    dmod_pad = jnp.pad(dmod_shard, ((0, 0), (0, MOD_ROWS - 8), (0, 0)))
    gr_wmod = _mod_bwd(cond_pad.T, dmod_pad, "mod_bwd")
    two_d = lambda a: a.reshape(-1, nmod)
    outs = _adamw(two_d(w_mod), [two_d(gr_wmod)], two_d(m_w_mod), two_d(v_w_mod), "adamw_w_mod")
    results["w_mod"] = [o.reshape(w_mod.shape) for o in outs]

    names = ["ln_pre_g", "ln_post_g", "w_mod", "b_mod", "w_in_ab", "w_out_ab", "sgu_norm_g", "sgu_w", "sgu_b",
             "w_in_ssm", "w_out_ssm", "lam_re", "lam_im", "b_re", "b_im", "c_re", "c_im", "d_skip", "log_dt",
             "w_glu", "b_glu"]
    return (loss, grad_x[None], *[results[nm][0] for nm in names], *[results[nm][1] for nm in names],
            *[results[nm][2] for nm in names], *[results[nm][3] for nm in names])
```

```python
import functools
import math

import jax
import jax.numpy as jnp
from jax import lax
from jax.experimental import pallas as pl
from jax.experimental.pallas import tpu as pltpu

F32 = jnp.float32
BF16 = jnp.bfloat16
MESH = pl.DeviceIdType.MESH

EPS = 1e-6
HEAD = 128
SSM_T = 512
SSM_GB = 16
ADAM_LR, ADAM_B1, ADAM_B2, ADAM_EPS, ADAM_WD, ADAM_STEP = 0.001, 0.9, 0.999, 1e-08, 0.01, 10
VMEM_LIMIT = 56 * 1024 * 1024

NN = (((1,), (0,)), ((), ()))
NT = (((1,), (1,)), ((), ()))
TN = (((0,), (0,)), ((), ()))


def _params(sem=None):
    return pltpu.CompilerParams(dimension_semantics=sem, vmem_limit_bytes=VMEM_LIMIT)


def _dot(a, b, dims=NN):
    return lax.dot_general(a, b, dims, preferred_element_type=F32)


def _bf(x):
    return x.astype(BF16)


def _gelu(x):
    k = math.sqrt(2.0 / math.pi)
    t = jnp.tanh(k * (x + 0.044715 * x * x * x))
    return 0.5 * x * (1.0 + t)


def _gelu_grad(x):
    k = math.sqrt(2.0 / math.pi)
    x2 = x * x
    t = jnp.tanh(k * (x + 0.044715 * x * x2))
    return 0.5 * (1.0 + t) + 0.5 * x * (1.0 - t * t) * k * (1.0 + 3.0 * 0.044715 * x2)


def _sigmoid(x):
    return 1.0 / (1.0 + jnp.exp(-x))


def _silu(x):
    return x * _sigmoid(x)


def _silu_grad(x):
    s = _sigmoid(x)
    return s * (1.0 + x * (1.0 - s))


def _tile(n, t, mult=128):
    if n <= t:
        return n
    for cand in range(t - t % mult, 0, -mult):
        if n % cand == 0:
            return cand
    raise ValueError((n, t, mult))


def _matmul(a, b, mode, out_dtype, name, tm=512, tn=512, tk=2048, n_split=1, ride=None, m_part=None):
    b_sharded = b.ndim == 3
    if mode == "nn":
        (m, kk), (_, n) = a.shape, b.shape
    elif b_sharded:
        assert mode == "nt"
        (m, kk), n, tk = a.shape, b.shape[1], b.shape[2]
    elif mode == "nt":
        (m, kk), (n, _) = a.shape, b.shape
    else:
        (kk, m), (_, n) = a.shape, b.shape
    m_off = 0
    if m_part is not None:
        assert mode == "tn"
        first, count, parts = m_part
        tm = _tile(m // parts, tm)
        m_off = first * (m // parts) // tm
        m = count * (m // parts)
    tm, tk = _tile(m, tm), _tile(kk, tk)
    ns = n // n_split
    tn = _tile(ns, tn)
    nk = kk // tk
    dims = {"nn": NN, "nt": NT, "tn": TN}[mode]

    def body(a_ref, b_ref, o_ref, acc_ref):
        k = pl.program_id(2)
        part = _dot(_bf(a_ref[...]), _bf(b_ref[0] if b_sharded else b_ref[...]), dims)

        @pl.when(k == 0)
        def _():
            acc_ref[...] = part

        @pl.when(k > 0)
        def _():
            acc_ref[...] += part

        @pl.when(k == nk - 1)
        def _():
            o_ref[...] = acc_ref[...].astype(out_dtype).reshape(o_ref.shape)

    if mode == "nn":
        a_spec = pl.BlockSpec((tm, tk), lambda i, j, k: (i, k))
        b_spec = pl.BlockSpec((tk, tn), lambda i, j, k: (k, j))
    elif mode == "nt":
        a_spec = pl.BlockSpec((tm, tk), lambda i, j, k: (i, k))
        b_spec = (pl.BlockSpec((1, tn, tk), lambda i, j, k: (k, j, 0)) if b_sharded
                  else pl.BlockSpec((tn, tk), lambda i, j, k: (j, k)))
    else:
        a_spec = pl.BlockSpec((tk, tm), lambda i, j, k: (k, i + m_off))
        b_spec = pl.BlockSpec((tk, tn), lambda i, j, k: (k, j))
    if n_split == 1:
        out_shape = jax.ShapeDtypeStruct((m, n), out_dtype)
        o_spec = pl.BlockSpec((tm, tn), lambda i, j, k: (i, j))
    else:
        per = ns // tn
        out_shape = jax.ShapeDtypeStruct((n_split, m, ns), out_dtype)
        o_spec = pl.BlockSpec((1, tm, tn), lambda i, j, k: (j // per, i, j % per))
    outs = _call(body, name=name, grid=(m // tm, n // tn, nk), in_specs=[a_spec, b_spec], out_specs=[o_spec],
                 out_shape=[out_shape], scratch_shapes=[pltpu.VMEM((tm, tn), F32)], args=(a, b),
                 sem=("parallel", "parallel", "arbitrary"), ride=ride)
    return outs[0] if ride is None else (outs[0], outs[1:])


def _matmul_at(a, b, out_dtype, name, tm=1024, tn=512, n_split=1, ride=None, m_part=None):
    (kk, m), (_, n) = a.shape, b.shape
    m_off = 0
    if m_part is not None:
        first, count, parts = m_part
        tm = _tile(m // parts, tm)
        m_off = first * (m // parts) // tm
        m = count * (m // parts)
    tm = _tile(m, tm)
    ns = n // n_split
    tn = _tile(ns, tn)
    kc = _tile(kk, 512)

    def body(a_ref, b_ref, o_ref, at_ref):
        @pl.when(pl.program_id(1) == 0)
        def _():
            for c in range(kk // kc):
                at_ref[:, c * kc:(c + 1) * kc] = _bf(a_ref[c * kc:(c + 1) * kc, :].astype(F32).T)

        o_ref[...] = _dot(at_ref[...], _bf(b_ref[...])).astype(out_dtype).reshape(o_ref.shape)

    if n_split == 1:
        out_shape = jax.ShapeDtypeStruct((m, n), out_dtype)
        o_spec = pl.BlockSpec((tm, tn), lambda i, j: (i, j))
    else:
        per = ns // tn
        out_shape = jax.ShapeDtypeStruct((n_split, m, ns), out_dtype)
        o_spec = pl.BlockSpec((1, tm, tn), lambda i, j: (j // per, i, j % per))
    outs = _call(body, name=name, grid=(m // tm, n // tn),
                 in_specs=[pl.BlockSpec((kk, tm), lambda i, j: (0, i + m_off)), pl.BlockSpec((kk, tn), lambda i, j: (0, j))],
                 out_specs=[o_spec], out_shape=[out_shape], scratch_shapes=[pltpu.VMEM((tm, kk), BF16)], args=(a, b),
                 sem=("arbitrary", "arbitrary"), ride=ride)
    return outs[0] if ride is None else (outs[0], outs[1:])


def _row_spec(tm, d):
    return pl.BlockSpec((tm, d), lambda i: (i, 0))


def _vec_spec(d):
    return pl.BlockSpec((1, d), lambda i: (0, 0))


def _acc(ref, first, val):
    @pl.when(first)
    def _():
        ref[...] = val

    @pl.when(jnp.logical_not(first))
    def _():
        ref[...] += val


def _colsum(x):
    return jnp.sum(x, axis=0, keepdims=True)


def _rownorm(x):
    r = lax.rsqrt(jnp.mean(x * x, axis=-1, keepdims=True) + EPS)
    return x * r, r


STRIP = 64


def _fold8(x):
    return functools.reduce(lambda a, b: a + b, [x[8 * k:8 * k + 8] for k in range(x.shape[0] // 8)])


def _pre_fwd(x, g, scale, shift, name, ride=None):
    l, d = x.shape
    tm = _tile(l, 256)

    def body(x_ref, g_ref, sc_ref, sh_ref, h_ref):
        n, _ = _rownorm(x_ref[...])
        h_ref[...] = _bf(n * g_ref[...] * (1.0 + sc_ref[...]) + sh_ref[...])

    outs = _call(body, name=name, grid=(l // tm,), in_specs=[_row_spec(tm, d), _vec_spec(d), _vec_spec(d), _vec_spec(d)],
                 out_specs=[_row_spec(tm, d)], out_shape=[jax.ShapeDtypeStruct((l, d), BF16)],
                 args=(x, g, scale, shift), sem=("parallel",), ride=ride)
    return outs[0], outs[1:]


def _post_pre_fwd(x, y, gate, pg, g1, scale1, shift1, name):
    l, d = x.shape
    tm = _tile(l, 256)

    def body(x_ref, y_ref, gate_ref, pg_ref, g1_ref, sc_ref, sh_ref, x1_ref, h1_ref):
        @pl.loop(0, tm // STRIP)
        def _(s):
            rows = pl.ds(pl.multiple_of(s * STRIP, STRIP), STRIP)
            ny, _ = _rownorm(y_ref[rows, :].astype(F32))
            x1 = x_ref[rows, :] + gate_ref[...] * (ny * pg_ref[...])
            x1_ref[rows, :] = x1
            n1, _ = _rownorm(x1)
            h1_ref[rows, :] = _bf(n1 * g1_ref[...] * (1.0 + sc_ref[...]) + sh_ref[...])

    v = _vec_spec(d)
    return pl.pallas_call(
        body, name=name, grid=(l // tm,),
        in_specs=[_row_spec(tm, d), _row_spec(tm, d), v, v, v, v, v],
        out_specs=[_row_spec(tm, d), _row_spec(tm, d)],
        out_shape=[jax.ShapeDtypeStruct((l, d), F32), jax.ShapeDtypeStruct((l, d), BF16)],
        compiler_params=_params(("parallel",)),
    )(x, y, gate, pg, g1, scale1, shift1)


def _post_loss(x1, y1, gate, pg, target, name):
    l, d = x1.shape
    tm = _tile(l, 256)

    def body(x_ref, y_ref, gate_ref, pg_ref, t_ref, loss_ref, dy_ref, dx_ref, dgate_ref, dpg_ref):
        first = pl.program_id(0) == 0

        def strip(s, sums):
            rows = pl.ds(pl.multiple_of(s * STRIP, STRIP), STRIP)
            ny, ry = _rownorm(y_ref[rows, :].astype(F32))
            q = ny * pg_ref[...]
            e = x_ref[rows, :] + gate_ref[...] * q - t_ref[rows, :]
            dx2 = e * (1.0 / d)
            dx_ref[rows, :] = dx2
            dq = dx2 * gate_ref[...]
            dny = dq * pg_ref[...]
            dy_ref[rows, :] = _bf(ry * (dny - ny * jnp.mean(dny * ny, axis=-1, keepdims=True)))
            return sums[0] + _fold8(e * e), sums[1] + _fold8(dx2 * q), sums[2] + _fold8(dq * ny)

        zero = jnp.zeros((8, d), F32)
        sq, dgate, dpg = lax.fori_loop(0, tm // STRIP, strip, (zero, zero, zero))
        _acc(loss_ref, first, jnp.full((1, 128), 0.5 / d, F32) * jnp.sum(sq))
        _acc(dgate_ref, first, _colsum(dgate))
        _acc(dpg_ref, first, _colsum(dpg))

    v = _vec_spec(d)
    return pl.pallas_call(
        body, name=name, grid=(l // tm,),
        in_specs=[_row_spec(tm, d), _row_spec(tm, d), v, v, _row_spec(tm, d)],
        out_specs=[_vec_spec(128), _row_spec(tm, d), _row_spec(tm, d), v, v],
        out_shape=[jax.ShapeDtypeStruct((1, 128), F32), jax.ShapeDtypeStruct((l, d), BF16),
                   jax.ShapeDtypeStruct((l, d), F32), jax.ShapeDtypeStruct((1, d), F32),
                   jax.ShapeDtypeStruct((1, d), F32)],
        compiler_params=_params(("arbitrary",)),
    )(x1, y1, gate, pg, target)


def _pre_bwd(dh, dres, x, g, scale, name, post=None):
    l, d = x.shape
    tm = _tile(l, 256)
    with_post = post is not None

    def body(*refs):
        if with_post:
            (dh_ref, dres_ref, x_ref, g_ref, sc_ref, y_ref, gate_ref, pg_ref,
             dx_ref, dsc_ref, dsh_ref, dg_ref, dy_ref, dgate_ref, dpg_ref) = refs
        else:
            dh_ref, dres_ref, x_ref, g_ref, sc_ref, dx_ref, dsc_ref, dsh_ref, dg_ref = refs
        first = pl.program_id(0) == 0

        def strip(s, sums):
            rows = pl.ds(pl.multiple_of(s * STRIP, STRIP), STRIP)
            dh = dh_ref[rows, :].astype(F32)
            n, r = _rownorm(x_ref[rows, :])
            dyn = dh * (1.0 + sc_ref[...])
            dn = dyn * g_ref[...]
            dx = dres_ref[rows, :] + r * (dn - n * jnp.mean(dn * n, axis=-1, keepdims=True))
            dx_ref[rows, :] = dx
            new = [sums[0] + _fold8(dh * (n * g_ref[...])), sums[1] + _fold8(dh), sums[2] + _fold8(dyn * n)]
            if with_post:
                ny, ry = _rownorm(y_ref[rows, :].astype(F32))
                dq = dx * gate_ref[...]
                dny = dq * pg_ref[...]
                dy_ref[rows, :] = _bf(ry * (dny - ny * jnp.mean(dny * ny, axis=-1, keepdims=True)))
                new += [sums[3] + _fold8(dx * (ny * pg_ref[...])), sums[4] + _fold8(dq * ny)]
            return tuple(new)

        zero = jnp.zeros((8, d), F32)
        sums = lax.fori_loop(0, tm // STRIP, strip, (zero,) * (5 if with_post else 3))
        outs = [dsc_ref, dsh_ref, dg_ref] + ([dgate_ref, dpg_ref] if with_post else [])
        for ref, acc in zip(outs, sums):
            _acc(ref, first, _colsum(acc))

    v = _vec_spec(d)
    row = _row_spec(tm, d)
    vec_out = jax.ShapeDtypeStruct((1, d), F32)
    in_specs = [row, row, row, v, v]
    args = [dh, dres, x, g, scale]
    out_specs = [row, v, v, v]
    out_shape = [jax.ShapeDtypeStruct((l, d), F32), vec_out, vec_out, vec_out]
    if with_post:
        in_specs += [row, v, v]
        args += list(post)
        out_specs += [row, v, v]
        out_shape += [jax.ShapeDtypeStruct((l, d), BF16), vec_out, vec_out]
    return pl.pallas_call(
        body, name=name, grid=(l // tm,), in_specs=in_specs, out_specs=out_specs, out_shape=out_shape,
        compiler_params=_params(("arbitrary",)),
    )(*args)


def _softplus_parts(z):
    e = jnp.exp(-jnp.abs(z))
    den = 1.0 + e
    lb = jnp.minimum(z, 0.0) - jnp.log(den)
    return lb, lb - z, jnp.exp(lb)


def _tri(cmp, n=HEAD):
    row = lax.broadcasted_iota(jnp.int32, (n, n), 0)
    col = lax.broadcasted_iota(jnp.int32, (n, n), 1)
    return cmp(row, col)


ATT_T = 256
ATT_DEAD = 104.0


def _any_alive(runs):
    return functools.reduce(jnp.maximum, [jnp.max(r) for r in runs]) > -ATT_DEAD


def _attn_fwd(qkv, wb, name, hp=4, ride=None):
    l = qkv.shape[0]
    t = ATT_T
    nh, nq = wb // HEAD, l // t
    hp = min(hp, nh)
    ng, wg = nh // hp, hp * HEAD
    scale = 1.0 / math.sqrt(HEAD)

    def body(q_ref, k_ref, v_ref, o_ref):
        i = pl.program_id(1)
        valid = _tri(lambda r, c: c < r, t)
        m_gt = _bf(_tri(lambda r, c: r > c, t).astype(F32))

        def tile(j, carry, diag):
            rows = pl.ds(pl.multiple_of(j * t, t), t)
            cols = [slice(hh * HEAD, (hh + 1) * HEAD) for hh in range(hp)]
            zs = [_dot(q_ref[:, cs], k_ref[rows, cs], NT) * scale for cs in cols]
            lbs, lks = [], []
            for z in zs:
                lb, lk, _ = _softplus_parts(z)
                lbs.append(lb)
                lks.append(jnp.where(valid, lk, 0.0) if diag else lk)
            laters = [_dot(_bf(lk), m_gt) for lk in lks]
            ws = [jnp.exp(lb + later + run) for lb, later, (_, run) in zip(lbs, laters, carry)]
            if diag:
                ws = [jnp.where(valid, w, 0.0) for w in ws]
            return tuple((acc + _dot(_bf(w), v_ref[rows, cs]), run + jnp.sum(lk, axis=1, keepdims=True))
                         for w, lk, cs, (acc, run) in zip(ws, lks, cols, carry))

        zero = (jnp.zeros((t, HEAD), F32), jnp.zeros((t, 1), F32))
        carry = tile(i, (zero,) * hp, True)
        _, carry = lax.while_loop(lambda c: (c[0] < i) & _any_alive([run for _, run in c[1]]),
                                  lambda c: (c[0] + 1, tile(i - 1 - c[0], c[1], False)), (jnp.int32(0), carry))
        for hh, (acc, _) in enumerate(carry):
            o_ref[:, hh * HEAD:(hh + 1) * HEAD] = acc

    blk = lambda off: pl.BlockSpec((t, wg), lambda h, i: (i, off + h))
    full = lambda off: pl.BlockSpec((l, wg), lambda h, i: (0, off + h))
    out = pl.BlockSpec((t, wg), lambda h, i: (i, h))
    outs = _call(body, name=name, grid=(ng, nq), in_specs=[blk(0), full(ng), full(2 * ng)], out_specs=[out],
                 out_shape=[jax.ShapeDtypeStruct((l, wb), F32)],
                 args=(qkv, qkv, qkv), sem=("parallel", "arbitrary"), ride=ride)
    return outs[0], outs[1:]


def _attn_bwd(qkv, proj, dcat, wa, wb, name, hp=2, ride=None):
    l = qkv.shape[0]
    t = ATT_T
    nh, nq = wb // HEAD, l // t
    hp = min(hp, nh)
    ng, wg = nh // hp, hp * HEAD
    scale = 1.0 / math.sqrt(HEAD)

    def body(q_ref, k_ref, v_ref, bz_ref, dc_ref, dq_ref, dkt_out, dvt_out, do_s, qt_s, dot_s,
             dkt_ref, dvt_ref, out_sems):
        i = pl.program_id(1)

        @pl.when(i == 0)
        def _():
            dkt_ref[...] = jnp.zeros_like(dkt_ref)
            dvt_ref[...] = jnp.zeros_like(dvt_ref)

        do = dc_ref[...].astype(F32) * _silu(bz_ref[...])
        do_s[...] = _bf(do)
        for hh in range(hp):
            cs = slice(hh * HEAD, (hh + 1) * HEAD)
            qt_s[hh] = _bf(q_ref[:, cs].astype(F32).T * scale)
            dot_s[hh] = _bf(do[:, cs].T)
        valid = _tri(lambda r, c: c < r, t)
        m_le = _bf(_tri(lambda r, c: r <= c, t).astype(F32))
        m_lt = _bf(_tri(lambda r, c: r < c, t).astype(F32))

        heads = range(hp)
        cols = [slice(hh * HEAD, (hh + 1) * HEAD) for hh in heads]

        def row_sums(j, runs, diag):
            rows = pl.ds(pl.multiple_of(j * t, t), t)
            out = []
            for cs, run in zip(cols, runs):
                _, lk, _ = _softplus_parts(_dot(q_ref[:, cs], k_ref[rows, cs], NT) * scale)
                if diag:
                    lk = jnp.where(valid, lk, 0.0)
                out.append(run + jnp.sum(lk, axis=1, keepdims=True))
            return tuple(out)

        runs = row_sums(i, (jnp.zeros((t, 1), F32),) * hp, True)
        below, lktot = lax.while_loop(lambda c: (c[0] < i) & _any_alive(c[1]),
                                      lambda c: (c[0] + 1, row_sums(i - 1 - c[0], c[1], False)), (jnp.int32(0), runs))

        def tile(j, carry, diag):
            rows = pl.ds(pl.multiple_of(j * t, t), t)
            zs = [_dot(q_ref[:, cs], k_ref[rows, cs], NT) * scale for cs in cols]
            dws = [_dot(do_s[:, cs], v_ref[rows, cs], NT) for cs in cols]
            lbs, lks, sigs = [], [], []
            for z in zs:
                lb, lk, sig = _softplus_parts(z)
                lbs.append(lb)
                lks.append(jnp.where(valid, lk, 0.0) if diag else lk)
                sigs.append(sig)
            pins = [_dot(_bf(lk), m_le) for lk in lks]
            ws = [jnp.exp(lbs[hh] + (lktot[hh] - carry[hh][1]) - pins[hh]) for hh in heads]
            if diag:
                ws = [jnp.where(valid, w, 0.0) for w in ws]
            das = [dw * w for dw, w in zip(dws, ws)]
            pexs = [_dot(_bf(da), m_lt) for da in das]
            dzs = [das[hh] - sigs[hh] * (das[hh] + carry[hh][2] + pexs[hh]) for hh in heads]
            if diag:
                dzs = [jnp.where(valid, dz, 0.0) for dz in dzs]
            dzs = [_bf(dz) for dz in dzs]
            out = []
            for hh in heads:
                dkt, dvt = _dot(qt_s[hh], dzs[hh]), _dot(dot_s[hh], _bf(ws[hh]))
                for half in range(t // HEAD):
                    dkt_ref[hh, sub * j + half] += dkt[:, half * HEAD:(half + 1) * HEAD]
                    dvt_ref[hh, sub * j + half] += dvt[:, half * HEAD:(half + 1) * HEAD]
                dq, cpre, ppre = carry[hh]
                out.append((dq + _dot(dzs[hh], k_ref[rows, cols[hh]]), cpre + jnp.sum(lks[hh], axis=1, keepdims=True),
                            ppre + pexs[hh][:, t - 1:] + das[hh][:, t - 1:]))
            return tuple(out)

        zero = (jnp.zeros((t, HEAD), F32), jnp.zeros((t, 1), F32), jnp.zeros((t, 1), F32))
        carry = lax.fori_loop(i - below, i, lambda j, c: tile(j, c, False), (zero,) * hp)
        carry = tile(i, carry, True)
        for hh in range(hp):
            dq_ref[:, hh * HEAD:(hh + 1) * HEAD] = carry[hh][0] * scale

        @pl.when(i == nq - 1)
        def _():
            heads = pl.ds(pl.program_id(0) * hp, hp)
            copies = [pltpu.make_async_copy(dkt_ref, dkt_out.at[heads], out_sems.at[0]),
                      pltpu.make_async_copy(dvt_ref, dvt_out.at[heads], out_sems.at[1])]
            for cp in copies:
                cp.start()
            for cp in copies:
                cp.wait()

    sub = t // HEAD
    blk = lambda off: pl.BlockSpec((t, wg), lambda h, i: (i, off + h))
    full = lambda off: pl.BlockSpec((l, wg), lambda h, i: (0, off + h))
    acc_shape = jax.ShapeDtypeStruct((nh, l // HEAD, HEAD, HEAD), F32)
    acc_scratch = pltpu.VMEM((hp, l // HEAD, HEAD, HEAD), F32)
    outs = _call(
        body, name=name, grid=(ng, nq),
        in_specs=[blk(0), full(ng), full(2 * ng), blk(3 * wa // wg), blk(wa // wg)],
        out_specs=[blk(0), ANY, ANY], out_shape=[jax.ShapeDtypeStruct((l, wb), F32), acc_shape, acc_shape],
        scratch_shapes=[pltpu.VMEM((t, wg), BF16), pltpu.VMEM((hp, HEAD, t), BF16), pltpu.VMEM((hp, HEAD, t), BF16),
                        acc_scratch, acc_scratch, pltpu.SemaphoreType.DMA((2,))],
        args=(qkv, qkv, qkv, proj, dcat), sem=("parallel", "arbitrary"), ride=ride)
    return outs[0], outs[1], outs[2], outs[3:]


def _sgu_heads(v, g_ref, w_ref, bt_ref, nh):
    keep = _tri(lambda r, c: r >= c)
    out = []
    for h in range(nh):
        cols = slice(h * HEAD, (h + 1) * HEAD)
        nv, r = _rownorm(v[:, cols])
        wm = jnp.where(keep, w_ref[h], 0.0)
        s = _dot(_bf(wm), _bf(nv * g_ref[:, cols])) + bt_ref[:, h:h + 1]
        out.append((nv, r, wm, s))
    return out


def _sgu_fwd(proj, out_b, norm_g, sgu_w, sgu_bt, wa, wb, name):
    l, n = proj.shape
    nh = wa // HEAD

    def body(au_ref, av_ref, az_ref, bz_ref, ob_ref, g_ref, w_ref, bt_ref, cat_ref):
        u, v, sz = _gelu(au_ref[...]), _gelu(av_ref[...]), _silu(az_ref[...])
        for h, (_, _, _, s) in enumerate(_sgu_heads(v, g_ref, w_ref, bt_ref, nh)):
            cols = slice(h * HEAD, (h + 1) * HEAD)
            cat_ref[:, cols] = _bf(u[:, cols] * s * sz[:, cols])
        cat_ref[:, wa:] = _bf(ob_ref[...] * _silu(bz_ref[...]))

    a_blk = lambda j: pl.BlockSpec((HEAD, wa), lambda i: (i, j))
    return pl.pallas_call(
        body, name=name, grid=(l // HEAD,),
        in_specs=[a_blk(0), a_blk(1), a_blk(2), a_blk(3), pl.BlockSpec((HEAD, wb), lambda i: (i, 0)),
                  _vec_spec(wa), pl.BlockSpec((nh, HEAD, HEAD), lambda i: (0, 0, 0)),
                  pl.BlockSpec((HEAD, nh), lambda i: (0, 0))],
        out_specs=pl.BlockSpec((HEAD, wa + wb), lambda i: (i, 0)),
        out_shape=jax.ShapeDtypeStruct((l, wa + wb), BF16),
        compiler_params=_params(("parallel",)),
    )(proj, proj, proj, proj, out_b, norm_g, sgu_w, sgu_bt)


def _sgu_bwd(proj, out_b, dcat, dq, dk, dv, norm_g, sgu_w, sgu_bt, wa, wb, name, ride=None):
    l = proj.shape[0]
    n = 3 * wa + 4 * wb
    nh = wa // HEAD

    def body(au_ref, av_ref, az_ref, bz_ref, ob_ref, dc_ref, dq_ref, dk_ref, dv_ref, g_ref, w_ref, wt_ref, bt_ref,
             dp_ref, dw_ref, dbt_ref, dg_ref):
        first = pl.program_id(0) == 0
        keep = _tri(lambda r, c: r >= c)
        au, av, az = au_ref[...], av_ref[...], az_ref[...]
        u, v, sz = _gelu(au), _gelu(av), _silu(az)
        dgelu_u, dgelu_v, dsilu_z = _gelu_grad(au), _gelu_grad(av), _silu_grad(az)
        heads = _sgu_heads(v, g_ref, w_ref, bt_ref, nh)
        cols = [slice(h * HEAD, (h + 1) * HEAD) for h in range(nh)]
        dss = []
        for h, (nv, r, wm, s) in enumerate(heads):
            dca, uh, szh = dc_ref[:, cols[h]].astype(F32), u[:, cols[h]], sz[:, cols[h]]
            dp_ref[:, cols[h]] = _bf(dca * s * szh * dgelu_u[:, cols[h]])
            dp_ref[:, 2 * wa + h * HEAD:2 * wa + (h + 1) * HEAD] = _bf(dca * uh * s * dsilu_z[:, cols[h]])
            dss.append(dca * uh * szh)
        dws = [_dot(_bf(ds), _bf(nv * g_ref[:, cs]), NT) for ds, cs, (nv, _, _, _) in zip(dss, cols, heads)]
        keep_t = _tri(lambda r, c: r <= c)
        dvhs = [_dot(_bf(jnp.where(keep_t, wt_ref[h], 0.0)), _bf(dss[h])) for h in range(nh)]
        dg_parts = []
        for h, (nv, r, wm, s) in enumerate(heads):
            _acc(dw_ref.at[h], first, jnp.where(keep, dws[h], 0.0))
            _acc(dbt_ref.at[:, h:h + 1], first, jnp.sum(dss[h], axis=1, keepdims=True))
            dg_parts.append(_colsum(dvhs[h] * nv))
            dnv = dvhs[h] * g_ref[:, cols[h]]
            dvv = r * (dnv - nv * jnp.mean(dnv * nv, axis=-1, keepdims=True))
            dp_ref[:, wa + h * HEAD:wa + (h + 1) * HEAD] = _bf(dvv * dgelu_v[:, cols[h]])
        _acc(dg_ref, first, jnp.concatenate(dg_parts, axis=1))
        base = 3 * wa
        dp_ref[:, base:base + wb] = _bf(dq_ref[...])
        for h in range(wb // HEAD):
            dp_ref[:, base + wb + h * HEAD:base + wb + (h + 1) * HEAD] = _bf(dk_ref[h, 0].T)
            dp_ref[:, base + 2 * wb + h * HEAD:base + 2 * wb + (h + 1) * HEAD] = _bf(dv_ref[h, 0].T)
        dp_ref[:, base + 3 * wb:] = _bf(dc_ref[:, wa:].astype(F32) * ob_ref[...] * _silu_grad(bz_ref[...]))

    a_blk = lambda j: pl.BlockSpec((HEAD, wa), lambda i: (i, j))
    b_blk = pl.BlockSpec((HEAD, wb), lambda i: (i, 0))
    t_blk = pl.BlockSpec((wb // HEAD, 1, HEAD, HEAD), lambda i: (0, i, 0, 0))
    w_spec = pl.BlockSpec((nh, HEAD, HEAD), lambda i: (0, 0, 0))
    bt_spec = pl.BlockSpec((HEAD, nh), lambda i: (0, 0))
    outs = _call(
        body, name=name, grid=(l // HEAD,), ride=ride, sem=("arbitrary",),
        in_specs=[a_blk(0), a_blk(1), a_blk(2), a_blk(3), b_blk, pl.BlockSpec((HEAD, wa + wb), lambda i: (i, 0)),
                  b_blk, t_blk, t_blk, _vec_spec(wa), w_spec, w_spec, bt_spec],
        out_specs=[pl.BlockSpec((HEAD, n), lambda i: (i, 0)), w_spec, bt_spec, _vec_spec(wa)],
        out_shape=[jax.ShapeDtypeStruct((l, n), BF16), jax.ShapeDtypeStruct((nh, HEAD, HEAD), F32),
                   jax.ShapeDtypeStruct((HEAD, nh), F32), jax.ShapeDtypeStruct((1, wa), F32)],
        args=(proj, proj, proj, proj, out_b, dcat, dq, dk, dv, norm_g, sgu_w, sgu_w.transpose(0, 2, 1), sgu_bt))
    return (*outs[:4], outs[4:])


def _ssm_discretise(lr, li, ldt, br, bi):
    dt = jnp.exp(ldt)
    mag = jnp.exp(lr * dt)
    a_re = mag * jnp.cos(li * dt)
    a_im = mag * jnp.sin(li * dt)
    den = lr * lr + li * li
    nr = a_re - 1.0
    coef_re = (nr * lr + a_im * li) / den
    coef_im = (a_im * lr - nr * li) / den
    return a_re, a_im, coef_re * br - coef_im * bi, coef_re * bi + coef_im * br


def _ssm_prep(lr, li, ldt, br, bi, lr_row, li_row, ldt_row, name):
    s, c = br.shape

    def body(lr_ref, li_ref, ldt_ref, br_ref, bi_ref, lrr_ref, lir_ref, ldtr_ref, bbr_ref, bbi_ref, tr_ref, ti_ref):
        _, _, bbr, bbi = _ssm_discretise(lr_ref[...], li_ref[...], ldt_ref[...], br_ref[...], bi_ref[...])
        bbr_ref[...] = bbr
        bbi_ref[...] = bbi
        row = lax.broadcasted_iota(jnp.int32, (SCAN_ROWS, 1), 0)
        blk, r = jnp.right_shift(row, 3), jnp.bitwise_and(row, 7)
        kind, rev = jnp.bitwise_and(blk, 3), blk >= 4
        step = jnp.left_shift(1, kind)
        n = jnp.where(kind < 3, step, jnp.where(rev, 8 - r, r + 1)).astype(F32)
        keep = (kind == 3) | (rev & (r < 8 - step)) | (jnp.logical_not(rev) & (r >= step))
        dt = jnp.exp(ldtr_ref[...])
        mag = jnp.exp(n * (lrr_ref[...] * dt))
        ang = n * (lir_ref[...] * dt)
        tr_ref[...] = jnp.where(keep, mag * jnp.cos(ang), 0.0)
        ti_ref[...] = jnp.where(keep, jnp.where(rev, -1.0, 1.0) * mag * jnp.sin(ang), 0.0)

    col = jax.ShapeDtypeStruct((s, c), F32)
    row = jax.ShapeDtypeStruct((SCAN_ROWS, s), F32)
    return pl.pallas_call(body, name=name, out_shape=[col, col, row, row])(
        lr, li, ldt, br, bi, lr_row, li_row, ldt_row)


def _ssm_prep_bwd(lr, li, ldt, br, bi, da_re, da_im, dbb_re, dbb_im, p, name):
    s, c = br.shape

    def body(lr_ref, li_ref, ldt_ref, br_ref, bi_ref, dar_ref, dai_ref, dbr_ref, dbi_ref,
             dlr_ref, dli_ref, dldt_ref, dbre_ref, dbim_ref):
        args = (lr_ref[...], li_ref[...], ldt_ref[...], br_ref[...], bi_ref[...])
        _, vjp = jax.vjp(_ssm_discretise, *args)
        dlr, dli, dldt, dbr, dbi = vjp((dar_ref[...], dai_ref[...], dbr_ref[...], dbi_ref[...]))
        dlr_ref[...] = dlr
        dli_ref[...] = dli
        dbre_ref[...] = dbr
        dbim_ref[...] = dbi
        idx = lax.broadcasted_iota(jnp.int32, (s, s // p), 0)
        grp = lax.broadcasted_iota(jnp.int32, (s, s // p), 1)
        own = (idx >= grp * p) & (idx < (grp + 1) * p)
        dldt_ref[...] = _colsum(jnp.where(own, dldt, 0.0))

    col1 = jax.ShapeDtypeStruct((s, 1), F32)
    colc = jax.ShapeDtypeStruct((s, c), F32)
    return pl.pallas_call(
        body, name=name, out_shape=[col1, col1, jax.ShapeDtypeStruct((1, s // p), F32), colc, colc],
    )(lr, li, ldt, br, bi, da_re, da_im, dbb_re, dbb_im)


SCAN_ROWS = 64


def _scan_groups(xr, xi, tr_ref, ti_ref, cr, ci, reverse):
    ng = xr.shape[0] // 8
    base = SCAN_ROWS // 2 if reverse else 0
    pr, pi = tr_ref[base + 24:base + 32, :], ti_ref[base + 24:base + 32, :]
    edge = slice(0, 1) if reverse else slice(7, 8)
    out_r, out_i = [None] * ng, [None] * ng
    for g in (range(ng - 1, -1, -1) if reverse else range(ng)):
        sr, si = xr[8 * g:8 * g + 8, :], xi[8 * g:8 * g + 8, :]
        for k in range(3):
            ar, ai = tr_ref[base + 8 * k:base + 8 * k + 8, :], ti_ref[base + 8 * k:base + 8 * k + 8, :]
            shift = 8 - (1 << k) if reverse else 1 << k
            rr, ri = pltpu.roll(sr, shift, 0), pltpu.roll(si, shift, 0)
            sr, si = sr + ar * rr - ai * ri, si + ar * ri + ai * rr
        sr, si = sr + pr * cr - pi * ci, si + pr * ci + pi * cr
        cr, ci = sr[edge, :], si[edge, :]
        out_r[g], out_i[g] = sr, si
    return jnp.concatenate(out_r, axis=0), jnp.concatenate(out_i, axis=0), cr, ci


def _ssm_fwd(proj, bbd, ccd, pw_re, pw_im, d_skip, w, name):
    l = proj.shape[0]
    nb, cw, ns2 = bbd.shape
    ns = ns2 // 2
    nc = l // SSM_T

    def body(u_ref, bbd_ref, ccd_ref, pr_ref, pi_ref, d_ref, y_ref, hsr_ref, hsi_ref, h_ref, hr_s, hi_s):
        @pl.when(pl.program_id(1) == 0)
        def _():
            hr_s[...] = jnp.zeros_like(hr_s)
            hi_s[...] = jnp.zeros_like(hi_s)

        hsr_ref[...] = hr_s[...].reshape(hsr_ref.shape)
        hsi_ref[...] = hi_s[...].reshape(hsi_ref.shape)
        u = u_ref[...]
        bu = _dot(_bf(u), bbd_ref[0])
        hr, hi, cr, ci = _scan_groups(bu[:, :ns], bu[:, ns:], pr_ref, pi_ref, hr_s[...], hi_s[...], False)
        hr_s[...] = cr
        hi_s[...] = ci
        h_bf = _bf(jnp.concatenate([hr, hi], axis=1))
        h_ref[...] = h_bf
        y_ref[...] = _dot(h_bf, ccd_ref[0]) + d_ref[...] * u

    tab = pl.BlockSpec((SCAN_ROWS, ns), lambda b, k: (0, b))
    return pl.pallas_call(
        body, name=name, grid=(nb, nc),
        in_specs=[pl.BlockSpec((SSM_T, cw), lambda b, k: (k, b)),
                  pl.BlockSpec((1, cw, ns2), lambda b, k: (b, 0, 0)),
                  pl.BlockSpec((1, ns2, cw), lambda b, k: (b, 0, 0)),
                  tab, tab, pl.BlockSpec((1, cw), lambda b, k: (0, b))],
        out_specs=[pl.BlockSpec((SSM_T, cw), lambda b, k: (k, b)),
                   pl.BlockSpec((1, 1, ns), lambda b, k: (k, 0, b)), pl.BlockSpec((1, 1, ns), lambda b, k: (k, 0, b)),
                   pl.BlockSpec((SSM_T, ns2), lambda b, k: (k, b))],
        out_shape=[jax.ShapeDtypeStruct((l, w), F32), jax.ShapeDtypeStruct((nc, 1, nb * ns), F32),
                   jax.ShapeDtypeStruct((nc, 1, nb * ns), F32), jax.ShapeDtypeStruct((l, nb * ns2), BF16)],
        scratch_shapes=[pltpu.VMEM((1, ns), F32), pltpu.VMEM((1, ns), F32)],
        compiler_params=_params(("parallel", "arbitrary")),
    )(proj, bbd, ccd, pw_re, pw_im, d_skip)


def _ssm_bwd(proj, dy, hs_re, hs_im, h_all, bbd, ccd, pw_re, pw_im, d_skip, w, name, ride=None):
    l = proj.shape[0]
    nb, cw, ns2 = bbd.shape
    ns = ns2 // 2
    nc = l // SSM_T

    def body(u_ref, dy_ref, hsr_ref, hsi_ref, h_ref, bbd_ref, ccd_ref, pr_ref, pi_ref, d_ref,
             du_ref, dbbd_ref, dccd_ref, dar_ref, dai_ref, dd_ref, gr_s, gi_s):
        first = pl.program_id(1) == 0

        @pl.when(first)
        def _():
            gr_s[...] = jnp.zeros_like(gr_s)
            gi_s[...] = jnp.zeros_like(gi_s)

        u, dy = u_ref[...], dy_ref[...]
        dy_bf = _bf(dy)
        hr0, hi0 = hsr_ref[0], hsi_ref[0]
        h = h_ref[...].astype(F32)
        hr, hi = h[:, :ns], h[:, ns:]
        dh = _dot(dy_bf, ccd_ref[0], NT)
        gr, gi, gcr, gci = _scan_groups(dh[:, :ns], dh[:, ns:], pr_ref, pi_ref, gr_s[...], gi_s[...], True)
        gr_s[...] = gcr
        gi_s[...] = gci
        row0 = lax.broadcasted_iota(jnp.int32, hr.shape, 0) == 0
        pr_h = jnp.where(row0, hr0, pltpu.roll(hr, 1, 0))
        pi_h = jnp.where(row0, hi0, pltpu.roll(hi, 1, 0))
        _acc(dar_ref, first, _colsum(pr_h * gr + pi_h * gi))
        _acc(dai_ref, first, _colsum(pr_h * gi - pi_h * gr))
        g_bf = _bf(jnp.concatenate([gr, gi], axis=1))
        _acc(dbbd_ref.at[0], first, _dot(_bf(u.T), g_bf))
        _acc(dccd_ref.at[0], first, _dot(_bf(h.T), dy_bf))
        du_ref[...] = _bf(_dot(g_bf, bbd_ref[0], NT) + d_ref[...] * dy)
        _acc(dd_ref, first, _colsum(dy * u))

    rev = lambda b, k: (nc - 1 - k, b)
    outs = _call(
        body, name=name, grid=(nb, nc), ride=ride, sem=("parallel", "arbitrary"),
        args=(proj, dy, hs_re, hs_im, h_all, bbd, ccd, pw_re, pw_im, d_skip),
        in_specs=[pl.BlockSpec((SSM_T, cw), rev), pl.BlockSpec((SSM_T, cw), rev),
                  pl.BlockSpec((1, 1, ns), lambda b, k: (nc - 1 - k, 0, b)),
                  pl.BlockSpec((1, 1, ns), lambda b, k: (nc - 1 - k, 0, b)),
                  pl.BlockSpec((SSM_T, ns2), rev),
                  pl.BlockSpec((1, cw, ns2), lambda b, k: (b, 0, 0)),
                  pl.BlockSpec((1, ns2, cw), lambda b, k: (b, 0, 0)),
                  pl.BlockSpec((SCAN_ROWS, ns), lambda b, k: (0, b)), pl.BlockSpec((SCAN_ROWS, ns), lambda b, k: (0, b)),
                  pl.BlockSpec((1, cw), lambda b, k: (0, b))],
        out_specs=[pl.BlockSpec((SSM_T, cw), rev),
                   pl.BlockSpec((1, cw, ns2), lambda b, k: (b, 0, 0)),
                   pl.BlockSpec((1, ns2, cw), lambda b, k: (b, 0, 0)),
                   pl.BlockSpec((1, ns), lambda b, k: (0, b)), pl.BlockSpec((1, ns), lambda b, k: (0, b)),
                   pl.BlockSpec((1, cw), lambda b, k: (0, b))],
        out_shape=[jax.ShapeDtypeStruct((l, w), BF16), jax.ShapeDtypeStruct(bbd.shape, F32),
                   jax.ShapeDtypeStruct(ccd.shape, F32), jax.ShapeDtypeStruct((1, nb * ns), F32),
                   jax.ShapeDtypeStruct((1, nb * ns), F32), jax.ShapeDtypeStruct((1, w), F32)],
        scratch_shapes=[pltpu.VMEM((1, ns), F32), pltpu.VMEM((1, ns), F32)])
    return (*outs[:6], outs[6:])


def _block_diag_b(bb_re, bb_im, g, p, c):
    nb = g // SSM_GB
    keep = _same_group(SSM_GB * c, c, SSM_GB * p, p)

    def one(bb):
        t = bb.reshape(nb, SSM_GB, p, c).transpose(0, 1, 3, 2).reshape(nb, SSM_GB * c, p)
        return jnp.where(keep, jnp.tile(t, (1, 1, SSM_GB)), 0.0)

    return jnp.concatenate([one(bb_re), one(bb_im)], axis=2)


def _same_group(rows, per_row, cols, per_col):
    r = lax.broadcasted_iota(jnp.int32, (rows, cols), 0) // per_row
    q = lax.broadcasted_iota(jnp.int32, (rows, cols), 1) // per_col
    return r == q


def _block_diag_c(c_re, c_im, g, p, c):
    nb = g // SSM_GB
    keep = _same_group(SSM_GB * p, p, SSM_GB * c, c)

    def one(cc):
        t = cc.reshape(nb, SSM_GB, c, p).transpose(0, 1, 3, 2).reshape(nb, SSM_GB * p, c)
        return jnp.where(keep, jnp.tile(t, (1, 1, SSM_GB)), 0.0)

    return jnp.concatenate([one(c_re), one(-c_im)], axis=1)


def _diag_of_b(dbbd, g, p, c):
    nb = g // SSM_GB
    keep = _same_group(SSM_GB * c, c, SSM_GB * p, p)

    def one(blk):
        d = jnp.where(keep, blk, 0.0).reshape(nb, SSM_GB * c, SSM_GB, p).sum(axis=2)
        return d.reshape(nb, SSM_GB, c, p).transpose(0, 1, 3, 2).reshape(g * p, c)

    half = SSM_GB * p
    return one(dbbd[:, :, :half]), one(dbbd[:, :, half:])


def _diag_of_c(dccd, g, p, c):
    nb = g // SSM_GB
    keep = _same_group(SSM_GB * p, p, SSM_GB * c, c)

    def one(blk):
        d = jnp.where(keep, blk, 0.0).reshape(nb, SSM_GB * p, SSM_GB, c).sum(axis=2)
        return d.reshape(nb, SSM_GB, p, c).transpose(0, 1, 3, 2).reshape(g, c, p)

    half = SSM_GB * p
    return one(dccd[:, :half]), -one(dccd[:, half:])


def _glu_fwd(y, proj, w_glu, b_glu, name):
    l, w = y.shape
    tm = _tile(l, 256)

    def body(y_ref, z_ref, w_ref, b_ref, o_ref):
        g = _gelu(y_ref[...])
        t = _dot(_bf(g), w_ref[...]) + b_ref[...]
        o_ref[...] = _bf(g * _sigmoid(t) * _silu(z_ref[...]))

    return pl.pallas_call(
        body, name=name, grid=(l // tm,),
        in_specs=[_row_spec(tm, w), pl.BlockSpec((tm, w), lambda i: (i, 1)),
                  pl.BlockSpec((w, w), lambda i: (0, 0)), _vec_spec(w)],
        out_specs=_row_spec(tm, w), out_shape=jax.ShapeDtypeStruct((l, w), BF16),
        compiler_params=_params(("parallel",)),
    )(y, proj, w_glu, b_glu)


def _glu_bwd(do, y, proj, w_glu, b_glu, name):
    l, w = y.shape
    tm = _tile(l, 512)
    nsteps = l // tm

    def body(do_ref, y_ref, z_ref, w_ref, b_ref, dy_ref, dz_ref, dw_ref, db_ref, dw_acc):
        i = pl.program_id(0)
        first = i == 0
        yv, z, do = y_ref[...], z_ref[...], do_ref[...].astype(F32)
        g = _gelu(yv)
        g_bf = _bf(g)
        sg = _sigmoid(_dot(g_bf, w_ref[...]) + b_ref[...])
        dyy = do * _silu(z)
        dz_ref[...] = _bf(do * g * sg * _silu_grad(z))
        dt = dyy * g * sg * (1.0 - sg)
        dt_bf = _bf(dt)
        dg = dyy * sg + _dot(dt_bf, w_ref[...], NT)
        dy_ref[...] = dg * _gelu_grad(yv)
        _acc(dw_acc, first, _dot(_bf(g.T), dt_bf))
        _acc(db_ref, first, _colsum(dt))

        @pl.when(i == nsteps - 1)
        def _():
            dw_ref[...] = _bf(dw_acc[...])

    return pl.pallas_call(
        body, name=name, grid=(nsteps,),
        in_specs=[_row_spec(tm, w), _row_spec(tm, w), pl.BlockSpec((tm, w), lambda i: (i, 1)),
                  pl.BlockSpec((w, w), lambda i: (0, 0)), _vec_spec(w)],
        out_specs=[_row_spec(tm, w), _row_spec(tm, w), pl.BlockSpec((w, w), lambda i: (0, 0)), _vec_spec(w)],
        out_shape=[jax.ShapeDtypeStruct((l, w), F32), jax.ShapeDtypeStruct((l, w), BF16),
                   jax.ShapeDtypeStruct((w, w), BF16), jax.ShapeDtypeStruct((1, w), F32)],
        scratch_shapes=[pltpu.VMEM((w, w), F32)],
        compiler_params=_params(("arbitrary",)),
    )(do, y, proj, w_glu, b_glu)


MOD_ROWS = 128


def _mod_fwd(cond_pad, w_mod, b_shard, name):
    nl, d, ncol = w_mod.shape
    tn = _tile(ncol, 512)

    def body(c_ref, w_ref, b_ref, o_ref):
        o_ref[0] = _dot(_bf(c_ref[...]), _bf(w_ref[0])) + b_ref[0]

    return pl.pallas_call(
        body, name=name, grid=(nl, ncol // tn),
        in_specs=[pl.BlockSpec((MOD_ROWS, d), lambda a, j: (0, 0)),
                  pl.BlockSpec((1, d, tn), lambda a, j: (a, 0, j)),
                  pl.BlockSpec((1, 1, tn), lambda a, j: (a, 0, j))],
        out_specs=pl.BlockSpec((1, MOD_ROWS, tn), lambda a, j: (a, 0, j)),
        out_shape=jax.ShapeDtypeStruct((nl, MOD_ROWS, ncol), F32),
        compiler_params=_params(("parallel", "parallel")),
    )(cond_pad, w_mod, b_shard)


def _mod_bwd(cond_pad_t, dmod_pad, name):
    nl, _, ncol = dmod_pad.shape
    d = cond_pad_t.shape[0]
    tn = _tile(ncol, 512)

    def body(c_ref, dm_ref, o_ref):
        o_ref[0] = _dot(_bf(c_ref[...]), _bf(dm_ref[0]))

    return pl.pallas_call(
        body, name=name, grid=(nl, ncol // tn),
        in_specs=[pl.BlockSpec((d, MOD_ROWS), lambda a, j: (0, 0)),
                  pl.BlockSpec((1, MOD_ROWS, tn), lambda a, j: (a, 0, j))],
        out_specs=pl.BlockSpec((1, d, tn), lambda a, j: (a, 0, j)),
        out_shape=jax.ShapeDtypeStruct((nl, d, ncol), F32),
        compiler_params=_params(("parallel", "parallel")),
    )(cond_pad_t, dmod_pad)


def _silu_rows(c2d, name):
    def body(c_ref, o_ref):
        o_ref[...] = _silu(c_ref[...])

    return pl.pallas_call(body, name=name, out_shape=jax.ShapeDtypeStruct(c2d.shape, F32))(c2d)


def _sum_leading(x, name):
    n, r, c = x.shape
    tr = _tile(r, max(16, (1 << 20) // (4 * c)), 16 if r % 16 == 0 else 8)

    def body(x_ref, o_ref):
        acc = x_ref[0].astype(F32)
        for k in range(1, n):
            acc = acc + x_ref[k].astype(F32)
        o_ref[...] = acc

    return pl.pallas_call(
        body, name=name, grid=(r // tr,),
        in_specs=[pl.BlockSpec((n, tr, c), lambda i: (0, i, 0))], out_specs=pl.BlockSpec((tr, c), lambda i: (i, 0)),
        out_shape=jax.ShapeDtypeStruct((r, c), F32), compiler_params=_params(("parallel",)),
    )(x)


def _adamw(w, gs, m, v, name, ride=None):
    r, c = w.shape
    tr = _tile(r, max(8, (3 << 19) // (4 * c)), 8)
    ng = len(gs)

    def body(*refs):
        w_ref, g_refs, m_ref, v_ref = refs[0], refs[1:1 + ng], refs[1 + ng], refs[2 + ng]
        g_ref, d_ref, nm_ref, nv_ref = refs[3 + ng:]
        g = g_refs[0][...]
        for extra in g_refs[1:]:
            g = g + extra[...]
        g_ref[...] = g
        d_ref[...], nm_ref[...], nv_ref[...] = _adamw_math(w_ref[...], g, m_ref[...], v_ref[...])

    spec = pl.BlockSpec((tr, c), lambda i: (i, 0))
    shp = jax.ShapeDtypeStruct((r, c), F32)
    outs = _call(body, name=name, grid=(r // tr,), in_specs=[spec] * (3 + ng), out_specs=[spec] * 4,
                 out_shape=[shp] * 4, args=(w, *gs, m, v), sem=("parallel",), ride=ride)
    return outs if ride is None else (outs[:4], outs[4:])


def _adamw_math(w, g, m, v):
    nm = ADAM_B1 * m + (1.0 - ADAM_B1) * g
    nv = ADAM_B2 * v + (1.0 - ADAM_B2) * (g * g)
    m_hat = nm / (1.0 - ADAM_B1 ** ADAM_STEP)
    v_hat = nv / (1.0 - ADAM_B2 ** ADAM_STEP)
    return -ADAM_LR * (m_hat / (jnp.sqrt(v_hat) + ADAM_EPS) + ADAM_WD * w), nm, nv


def _adamw_many(ws, gs, ms, vs, name):
    n = len(ws)

    def body(*refs):
        w_refs, g_refs, m_refs, v_refs = (refs[k * n:(k + 1) * n] for k in range(4))
        outs = refs[4 * n:]
        for i in range(n):
            outs[3 * i][...], outs[3 * i + 1][...], outs[3 * i + 2][...] = _adamw_math(
                w_refs[i][...], g_refs[i][...], m_refs[i][...], v_refs[i][...])

    out_shape = [jax.ShapeDtypeStruct(w.shape, F32) for w in ws for _ in range(3)]
    outs = pl.pallas_call(body, name=name, out_shape=out_shape, compiler_params=_params())(*ws, *gs, *ms, *vs)
    return [tuple(outs[3 * i:3 * i + 3]) for i in range(n)]


ANY = pl.BlockSpec(memory_space=pl.ANY)


def _flip(v, bit):
    return 1 - v if bit else v


def _allgather8_ops(x_ref, o_ref, send_sems, recv_sems, local_sem):
    mx, my, mc = lax.axis_index("x"), lax.axis_index("y"), lax.axis_index("c")
    me = 4 * mx + 2 * my + mc

    def mine():
        return pltpu.make_async_copy(x_ref, o_ref.at[me], local_sem)

    def copy(j, outgoing):
        peer = (_flip(mx, j & 4), _flip(my, j & 2), _flip(mc, j & 1))
        slot = me if outgoing else 4 * peer[0] + 2 * peer[1] + peer[2]
        return pltpu.make_async_remote_copy(
            src_ref=x_ref, dst_ref=o_ref.at[slot], send_sem=send_sems.at[j - 1], recv_sem=recv_sems.at[j - 1],
            device_id=peer, device_id_type=MESH)

    def start():
        mine().start()
        for j in range(1, 8):
            copy(j, True).start()

    def wait():
        for j in range(1, 8):
            copy(j, False).wait()
        mine().wait()

    return start, wait


def _ride_all8(x):
    return dict(xs=[x], shapes=[jax.ShapeDtypeStruct((8,) + x.shape, x.dtype)],
                sems=[pltpu.SemaphoreType.DMA((7,)), pltpu.SemaphoreType.DMA((7,)), pltpu.SemaphoreType.DMA],
                ops=lambda x_refs, o_refs, sems: _allgather8_ops(x_refs[0], o_refs[0], *sems))


def _ride_chip(xs, gather):
    return dict(xs=list(xs), shapes=_chip_exchange_shapes(xs, gather), sems=_chip_exchange_sems(len(xs)),
                ops=lambda x_refs, o_refs, sems: _chip_exchange_ops(x_refs, o_refs, *sems, gather))


def _allgather8(x, name):
    def body(x_ref, o_ref, *sems):
        start, wait = _allgather8_ops(x_ref, o_ref, *sems)
        start()
        wait()

    ride = _ride_all8(x)
    return pl.pallas_call(body, name=name, in_specs=[ANY], out_specs=ANY, out_shape=ride["shapes"][0],
                          scratch_shapes=ride["sems"])(x)


def _gather_halves_ops(x_ref, o_ref, ici_send, ici_recv, d2d_send, d2d_recv, local_sem):
    half = x_ref.shape[0] // 2
    mx, my, mc = lax.axis_index("x"), lax.axis_index("y"), lax.axis_index("c")
    k0 = 2 * mx + my
    mine = pl.ds(pl.multiple_of(mc * half, 16), half)
    theirs = pl.ds(pl.multiple_of((1 - mc) * half, 16), half)

    def local():
        return pltpu.make_async_copy(x_ref, o_ref.at[k0], local_sem)

    def chips(j):
        px, py = _flip(mx, j & 2), _flip(my, j & 1)
        return px, py, 2 * px + py

    def over_ici(j, outgoing):
        px, py, kp = chips(j)
        dst = o_ref.at[k0, mine] if outgoing else o_ref.at[kp, mine]
        return pltpu.make_async_remote_copy(
            src_ref=x_ref.at[mine], dst_ref=dst, send_sem=ici_send.at[j - 1], recv_sem=ici_recv.at[j - 1],
            device_id=(px, py, mc), device_id_type=MESH)

    def over_d2d(j, outgoing):
        _, _, kp = chips(j)
        rows = mine if outgoing else theirs
        return pltpu.make_async_remote_copy(
            src_ref=o_ref.at[kp, rows], dst_ref=o_ref.at[kp, rows], send_sem=d2d_send.at[j - 1],
            recv_sem=d2d_recv.at[j - 1], device_id=(mx, my, 1 - mc), device_id_type=MESH)

    def start():
        local().start()
        for j in range(1, 4):
            over_ici(j, True).start()

    def wait():
        for j in range(1, 4):
            over_ici(j, False).wait_recv()
            over_d2d(j, True).start()
        for j in range(1, 4):
            over_ici(j, True).wait_send()
            over_d2d(j, True).wait_send()
            over_d2d(j, False).wait_recv()
        local().wait()

    return start, wait


def _ride_halves(x):
    dma3 = pltpu.SemaphoreType.DMA((3,))
    return dict(xs=[x], shapes=[jax.ShapeDtypeStruct((4,) + x.shape, x.dtype)],
                sems=[dma3, dma3, dma3, dma3, pltpu.SemaphoreType.DMA],
                ops=lambda x_refs, o_refs, sems: _gather_halves_ops(x_refs[0], o_refs[0], *sems))


def _chip_exchange(xs, gather, name):
    n = len(xs)

    def body(*refs):
        start, wait = _chip_exchange_ops(refs[:n], refs[n:2 * n], *refs[2 * n:], gather)
        start()
        wait()

    return pl.pallas_call(
        body, name=name, in_specs=[ANY] * n, out_specs=[ANY] * n, out_shape=_chip_exchange_shapes(xs, gather),
        scratch_shapes=_chip_exchange_sems(n),
    )(*xs)


def _chip_exchange_shapes(xs, gather):
    return [jax.ShapeDtypeStruct(((4,) + x.shape) if gather else x.shape, x.dtype) for x in xs]


def _chip_exchange_sems(n):
    return [pltpu.SemaphoreType.DMA((3 * n,)), pltpu.SemaphoreType.DMA((3 * n,)), pltpu.SemaphoreType.DMA((n,))]


def _chip_exchange_ops(x_refs, o_refs, send_sems, recv_sems, local_sems, gather):
    n = len(x_refs)
    mx, my, mc = lax.axis_index("x"), lax.axis_index("y"), lax.axis_index("c")
    k0 = 2 * mx + my

    def local(a):
        src = x_refs[a] if gather else x_refs[a].at[k0]
        return pltpu.make_async_copy(src, o_refs[a].at[k0], local_sems.at[a])

    def copy(a, j, outgoing):
        px, py = _flip(mx, j & 2), _flip(my, j & 1)
        kp = 2 * px + py
        if outgoing:
            src = x_refs[a] if gather else x_refs[a].at[kp]
            dst = o_refs[a].at[k0]
        else:
            src = x_refs[a] if gather else x_refs[a].at[k0]
            dst = o_refs[a].at[kp]
        s = a * 3 + j - 1
        return pltpu.make_async_remote_copy(
            src_ref=src, dst_ref=dst, send_sem=send_sems.at[s], recv_sem=recv_sems.at[s],
            device_id=(px, py, mc), device_id_type=MESH)

    def start():
        for a in range(n):
            local(a).start()
            for j in range(1, 4):
                copy(a, j, True).start()

    def wait():
        for a in range(n):
            for j in range(1, 4):
                copy(a, j, False).wait()
            local(a).wait()

    return start, wait


def _call(body, *, name, grid, in_specs, out_specs, out_shape, args, scratch_shapes=(), sem=None, ride=None):
    if not ride:
        return pl.pallas_call(
            body, name=name, grid=grid, in_specs=list(in_specs), out_specs=list(out_specs), out_shape=list(out_shape),
            scratch_shapes=list(scratch_shapes), compiler_params=_params(sem))(*args)
    xs = [x for r in ride for x in r["xs"]]
    shapes = [s for r in ride for s in r["shapes"]]
    sems = [s for r in ride for s in r["sems"]]
    n_in, n_out, n_scr, nx = len(in_specs), len(out_specs), len(scratch_shapes), len(xs)

    def wrapped(*refs):
        ins, x_refs = refs[:n_in], refs[n_in:n_in + nx]
        outs = refs[n_in + nx:n_in + nx + n_out]
        lands = refs[n_in + nx + n_out:n_in + 2 * nx + n_out]
        rest = refs[n_in + 2 * nx + n_out:]
        scr, sem_refs = rest[:n_scr], rest[n_scr:]
        ops, xo, so = [], 0, 0
        for r in ride:
            nr, ns = len(r["xs"]), len(r["sems"])
            ops.append(r["ops"](x_refs[xo:xo + nr], lands[xo:xo + nr], sem_refs[so:so + ns]))
            xo, so = xo + nr, so + ns
        ids = [pl.program_id(a) for a in range(len(grid))]
        first = functools.reduce(jnp.logical_and, [i == 0 for i in ids])
        last = functools.reduce(jnp.logical_and, [i == g - 1 for i, g in zip(ids, grid)])

        @pl.when(first)
        def _():
            for start, _ in ops:
                start()

        body(*ins, *outs, *scr)

        @pl.when(last)
        def _():
            for _, wait in ops:
                wait()

    return pl.pallas_call(
        wrapped, name=name, grid=grid, in_specs=list(in_specs) + [ANY] * nx, out_specs=list(out_specs) + [ANY] * nx,
        out_shape=list(out_shape) + shapes, scratch_shapes=list(scratch_shapes) + sems,
        compiler_params=_params(("arbitrary",) * len(grid)))(*args, *xs)


def _sibling_exchange(xs, name):
    ride = _ride_sibling(xs)
    n = len(xs)

    def body(*refs):
        start, wait = ride["ops"](refs[:n], refs[n:2 * n], refs[2 * n:])
        start()
        wait()

    return pl.pallas_call(body, name=name, in_specs=[ANY] * n, out_specs=[ANY] * n, out_shape=ride["shapes"],
                          scratch_shapes=ride["sems"])(*xs)


def _ride_sibling(xs):
    n = len(xs)

    def ops(x_refs, o_refs, sems):
        send_sems, recv_sems = sems
        sib = (lax.axis_index("x"), lax.axis_index("y"), 1 - lax.axis_index("c"))

        def copies():
            return [pltpu.make_async_remote_copy(
                src_ref=x_refs[a], dst_ref=o_refs[a], send_sem=send_sems.at[a], recv_sem=recv_sems.at[a],
                device_id=sib, device_id_type=MESH) for a in range(n)]

        def start():
            for cp in copies():
                cp.start()

        def wait():
            for cp in copies():
                cp.wait()

        return start, wait

    return dict(xs=list(xs), shapes=[jax.ShapeDtypeStruct(x.shape, x.dtype) for x in xs],
                sems=[pltpu.SemaphoreType.DMA((n,)), pltpu.SemaphoreType.DMA((n,))], ops=ops)


PACK = 1024
PACK_ROWS = 512


def _pack(parts):
    flat = []
    for p in parts:
        v = p.reshape(-1).astype(F32)
        flat.append(jnp.pad(v, (0, (-v.shape[0]) % PACK)))
    total = sum(v.shape[0] for v in flat)
    flat.append(jnp.zeros(((-total) % (PACK_ROWS * 128),), F32))
    return jnp.concatenate(flat).reshape(-1, 128)


def _shard_columns(shards, lo, hi):
    width = shards.shape[2]
    out = []
    for k in range(shards.shape[0]):
        a, b = max(lo, k * width), min(hi, (k + 1) * width)
        if a < b:
            out.append(shards[k, :, a - k * width:b - k * width])
    return out


def _unpack_rows(gathered, shapes):
    flat = gathered.reshape(gathered.shape[0], -1)
    out, off = [], 0
    for shp in shapes:
        n = math.prod(shp)
        out.append(flat[:, off:off + n].reshape((flat.shape[0],) + tuple(shp)))
        off += n + (-n) % PACK
    return out


def _unpack(packed, shapes):
    flat = packed.reshape(-1)
    out, off = [], 0
    for shp in shapes:
        n = math.prod(shp)
        out.append(flat[off:off + n].reshape(shp))
        off += n + (-n) % PACK
    return out


def kernel(x, c, ln_pre_g, ln_post_g, w_mod, b_mod, w_in_ab, w_out_ab, sgu_norm_g, sgu_w, sgu_b, w_in_ssm, w_out_ssm, lam_re, lam_im, b_re, b_im, c_re, c_im, d_skip, log_dt, w_glu, b_glu, loss_target, m_ln_pre_g, m_ln_post_g, m_w_mod, m_b_mod, m_w_in_ab, m_w_out_ab, m_sgu_norm_g, m_sgu_w, m_sgu_b, m_w_in_ssm, m_w_out_ssm, m_lam_re, m_lam_im, m_b_re, m_b_im, m_c_re, m_c_im, m_d_skip, m_log_dt, m_w_glu, m_b_glu, v_ln_pre_g, v_ln_post_g, v_w_mod, v_b_mod, v_w_in_ab, v_w_out_ab, v_sgu_norm_g, v_sgu_w, v_sgu_b, v_w_in_ssm, v_w_out_ssm, v_lam_re, v_lam_im, v_b_re, v_b_im, v_c_re, v_c_im, v_d_skip, v_log_dt, v_w_glu, v_b_glu):
    given = dict(locals())
    mx, my, mc = lax.axis_index("x"), lax.axis_index("y"), lax.axis_index("c")
    me = 4 * mx + 2 * my + mc
    chip = 2 * mx + my

    _, l, d = x.shape
    x2, tgt = x[0], loss_target[0]
    n_in = w_in_ab.shape[2] * 4
    wa = wb = n_in // 7
    w = w_out_ssm.shape[1]
    g, p, cch = b_re.shape[1:]
    nmod = w_mod.shape[2]


    cond = _silu_rows(c.reshape(d // 128, 128), "cond_silu")
    cond_all = _allgather8(cond, "gather_cond").reshape(8, d)
    b_shard = lax.dynamic_slice(b_mod, (0, chip * nmod), (2, nmod)).reshape(2, 1, nmod)
    cond_pad = jnp.pad(cond_all, ((0, MOD_ROWS - 8), (0, 0)))
    modp = _mod_fwd(cond_pad, w_mod, b_shard, "mod_fwd")[:, :8]
    modp_all = _allgather8(modp.reshape(16, nmod), "gather_mod").reshape(4, 2, 2, 8, nmod)
    mine = lax.dynamic_index_in_dim(lax.dynamic_index_in_dim(modp_all, mc, 1, False), me, 2, False)
    mod = mine.transpose(1, 0, 2).reshape(2, 3 * d)
    shift = [mod[a:a + 1, :d] for a in range(2)]
    scale = [mod[a:a + 1, d:2 * d] for a in range(2)]
    gate = [mod[a:a + 1, 2 * d:] for a in range(2)]
    pre_g = [ln_pre_g[a:a + 1] for a in range(2)]
    post_g = [ln_post_g[a:a + 1] for a in range(2)]

    sgu_w0, sgu_bt = sgu_w[0], sgu_b[0].T
    h0, (gw_in_ab,) = _pre_fwd(x2, pre_g[0], scale[0], shift[0], "pre0_fwd", ride=[_ride_halves(_bf(w_in_ab[0]))])
    w_gates = jnp.concatenate(_shard_columns(gw_in_ab, 0, 3 * wa) + _shard_columns(gw_in_ab, 3 * wa + 3 * wb, n_in),
                              axis=1)
    w_qkv = jnp.concatenate(_shard_columns(gw_in_ab, 3 * wa, 3 * wa + 3 * wb), axis=1)
    proj0, (gw_in_ssm,) = _matmul(h0, w_gates, "nn", F32, "proj0", tm=1024, ride=[_ride_chip([_bf(w_in_ssm[0])], True)])
    qkv, (gw_out_ssm, gw_glu, g_dskip, g_bglu) = _matmul(
        h0, w_qkv, "nn", BF16, "proj0_qkv", tm=1024,
        ride=[_ride_chip([_bf(w_out_ssm[0]), _bf(w_glu[0]), d_skip, b_glu], True)])
    out_b, (gw_out_ab,) = _attn_fwd(qkv, wb, "attn_fwd", hp=8, ride=[_ride_chip([_bf(w_out_ab[0])], True)])
    wout_ab = gw_out_ab.reshape(wa + wb, d)
    win_ssm = gw_in_ssm.reshape(d, 2 * w)
    wout_ssm = jnp.concatenate([gw_out_ssm[k] for k in range(4)], axis=1)
    wglu = gw_glu.reshape(w, w)
    dskip_full = g_dskip.reshape(1, w)
    bglu_full = g_bglu.reshape(1, w)
    cat =_sgu_fwd(proj0, out_b, sgu_norm_g, sgu_w0, sgu_bt, wa, wb, "sgu_fwd")
    y0 = _matmul(cat, wout_ab, "nn", BF16, "out0", tm=1024)
    x1, h1 = _post_pre_fwd(x2, y0, gate[0], post_g[0], pre_g[1], scale[1], shift[1], "post0_pre1_fwd")

    s = g * p
    lr_c, li_c = lam_re.reshape(s, 1), lam_im.reshape(s, 1)
    ldt_c = jnp.repeat(log_dt.reshape(g), p).reshape(s, 1)
    br_c, bi_c = b_re.reshape(s, cch), b_im.reshape(s, cch)
    bb_re, bb_im, pw_re, pw_im = _ssm_prep(lr_c, li_c, ldt_c, br_c, bi_c, lr_c.reshape(1, s), li_c.reshape(1, s),
                                           ldt_c.reshape(1, s), "ssm_prep")
    bbd = _bf(_block_diag_b(bb_re, bb_im, g, p, cch))
    ccd = _bf(_block_diag_c(c_re[0], c_im[0], g, p, cch))
    proj1 = _matmul(h1, win_ssm, "nn", F32, "proj1", tm=1024)
    y_ssm, hs_re, hs_im, h_all = _ssm_fwd(proj1, bbd, ccd, pw_re, pw_im, dskip_full, w, "ssm_fwd")
    o1 = _glu_fwd(y_ssm, proj1, wglu, bglu_full, "glu_fwd")
    y1 = _matmul(o1, wout_ssm, "nn", BF16, "out1", tm=1024)
    loss_vec, dy1, dx2, dgate1, dpost1 = _post_loss(x1, y1, gate[1], post_g[1], tgt, "post1_loss")

    do1 = _matmul(dy1, wout_ssm, "nt", BF16, "out1_dx", tm=1024)
    gr_wout_ssm = _matmul_at(o1, dy1, BF16, "out1_dw", n_split=4)
    dy_ssm, dz1, gr_wglu, gr_bglu = _glu_bwd(do1, y_ssm, proj1, wglu, bglu_full, "glu_bwd")
    du1, dbbd, dccd, da_re, da_im, gr_dskip, (ld_wout_ssm, ld_wglu) = _ssm_bwd(
        proj1, dy_ssm, hs_re, hs_im, h_all, bbd, ccd, pw_re, pw_im, dskip_full, w, "ssm_bwd",
        ride=[_ride_chip([gr_wout_ssm, gr_wglu.reshape(4, w // 4, w)], False)])
    dproj1 = jnp.concatenate([du1, dz1], axis=1)
    dh1 = _matmul(dproj1, win_ssm, "nt", BF16, "proj1_dx", tm=1024)
    gr_win_ssm = _matmul_at(h1, dproj1, BF16, "proj1_dw", tn=1024)
    dx1, dscale1, dshift1, dpre1, dy0, dgate0, dpost0 = _pre_bwd(
        dh1, dx2, x1, pre_g[1], scale[1], "pre1_post0_bwd", post=(y0, gate[0], post_g[0]))

    dcat = _matmul(dy0, wout_ab, "nt", BF16, "out0_dx", tm=1024)
    gr_wout_ab = _matmul_at(cat, dy0, BF16, "out0_dw", tn=1024)
    dbb_re, dbb_im = _diag_of_b(dbbd, g, p, cch)
    dc_re, dc_im = _diag_of_c(dccd, g, p, cch)
    part_a = [loss_vec[:, :1], dpre1, dpost0, dpost1, dgate0, dshift1, dscale1, dgate1, da_re, da_im,
              dbb_re, dbb_im, dc_re, dc_im, gr_dskip, gr_bglu]
    shapes_a = [a.shape for a in part_a]
    dq, dk, dv, (ld_win_ssm, ld_wout_ab, gath_a) = _attn_bwd(
        qkv, proj0, dcat, wa, wb, "attn_bwd", hp=4,
        ride=[_ride_chip([gr_win_ssm.reshape(4, d // 4, 2 * w), gr_wout_ab.reshape(4, (wa + wb) // 4, d)], False),
              _ride_all8(_pack(part_a))])
    early_names = ["w_out_ab", "w_in_ssm", "w_out_ssm", "w_glu"]
    early_sums = [_sum_leading(a, "sum_" + nm) for a, nm in zip([ld_wout_ab, ld_win_ssm, ld_wout_ssm, ld_wglu], early_names)]
    dproj0, gr_sgu_w, gr_sgu_bt, gr_sgu_g, early_sib = _sgu_bwd(
        proj0, out_b, dcat, dq, dk, dv, sgu_norm_g, sgu_w0, sgu_bt, wa, wb, "sgu_bwd", ride=[_ride_sibling(early_sums)])
    part_b = [gr_sgu_g, gr_sgu_w, gr_sgu_bt.T]
    shapes_b = [a.shape for a in part_b]
    gr_win_ab_lo, (gath_b,) = _matmul_at(h0, dproj0, BF16, "proj0_dw_lo", tm=512, tn=n_in // 4, n_split=4,
                                         m_part=(0, 1, 2), ride=[_ride_all8(_pack(part_b))])
    gr_win_ab_hi, (ld_win_ab_lo,) = _matmul_at(
        h0, dproj0, BF16, "proj0_dw_hi", tm=512, tn=n_in // 4, n_split=4, m_part=(1, 1, 2),
        ride=[_ride_chip([gr_win_ab_lo], False)])
    dh0, (ld_win_ab_hi,) = _matmul(dproj0, gw_in_ab, "nt", BF16, "proj0_dx", tm=1024, tn=1024,
                                   ride=[_ride_chip([gr_win_ab_hi], False)])
    grad_x, dscale0, dshift0, dpre0 = _pre_bwd(dh0, dx1, x2, pre_g[0], scale[0], "pre0_bwd")
    part_c = [dpre0, dshift0, dscale0]
    shapes_c = [a.shape for a in part_c]
    gath_c = _allgather8(_pack(part_c), "gather_small_tail")

    big_names = ["w_in_ab"] + early_names
    sum_win_ab = jnp.concatenate([_sum_leading(ld_win_ab_lo, "sum_w_in_ab_lo"),
                                  _sum_leading(ld_win_ab_hi, "sum_w_in_ab_hi")], axis=0)
    sums = [sum_win_ab] + early_sums
    sib = list(_sibling_exchange([sum_win_ab], "sibling_w_in_ab")) + list(early_sib)
    results = {}
    for nm, s_mine, s_sib in zip(big_names, sums, sib):
        shp = given[nm].shape
        two_d = lambda a: a.reshape(-1, shp[-1])
        outs = _adamw(two_d(given[nm]), [s_mine, s_sib], two_d(given["m_" + nm]), two_d(given["v_" + nm]),
                      "adamw_" + nm)
        results[nm] = [o.reshape(shp) for o in outs]

    (loss_s, g_pre1, g_post0, g_post1, g_gate0, g_shift1, g_scale1, g_gate1, s_da_re, s_da_im, s_dbb_re, s_dbb_im,
     g_c_re, g_c_im, g_dskip_full, g_bglu_full) = _unpack(_sum_leading(gath_a, "sum_small_a"), shapes_a)
    g_sgu_g, g_sgu_w, g_sgu_b = _unpack(_sum_leading(gath_b, "sum_small_b"), shapes_b)
    g_pre0, g_shift0, g_scale0 = _unpack(_sum_leading(gath_c, "sum_small_c"), shapes_c)
    loss = loss_s.reshape(())
    g_pre = jnp.concatenate([g_pre0, g_pre1], axis=0)
    g_post = jnp.concatenate([g_post0, g_post1], axis=0)
    g_bmod = jnp.concatenate([jnp.concatenate([g_shift0, g_scale0, g_gate0], axis=1),
                              jnp.concatenate([g_shift1, g_scale1, g_gate1], axis=1)], axis=0)

    g_lr, g_li, g_ldt, g_br, g_bi = _ssm_prep_bwd(lr_c, li_c, ldt_c, br_c, bi_c, s_da_re.reshape(s, 1),
                                                  s_da_im.reshape(s, 1), s_dbb_re, s_dbb_im, p, "ssm_prep_bwd")
    small = {
        "ln_pre_g": g_pre, "ln_post_g": g_post, "b_mod": g_bmod, "sgu_norm_g": g_sgu_g,
        "sgu_w": g_sgu_w.reshape(sgu_w.shape), "sgu_b": g_sgu_b.reshape(sgu_b.shape),
        "lam_re": g_lr.reshape(lam_re.shape), "lam_im": g_li.reshape(lam_im.shape),
        "b_re": g_br.reshape(b_re.shape), "b_im": g_bi.reshape(b_im.shape),
        "c_re": g_c_re.reshape(c_re.shape), "c_im": g_c_im.reshape(c_im.shape),
        "d_skip": lax.dynamic_slice(g_dskip_full, (0, chip * (w // 4)), (1, w // 4)),
        "log_dt": g_ldt.reshape(log_dt.shape),
        "b_glu": lax.dynamic_slice(g_bglu_full, (0, chip * (w // 4)), (1, w // 4)),
    }
    flat2 = lambda a: a.reshape(-1, a.shape[-1])
    wide = ("b_re", "b_im")
    for tag, group in (("adamw_small", [nm for nm in small if nm not in wide]), ("adamw_small_b", list(wide))):
        outs = _adamw_many([flat2(given[nm]) for nm in group], [flat2(small[nm]) for nm in group],
                           [flat2(given["m_" + nm]) for nm in group], [flat2(given["v_" + nm]) for nm in group], tag)
        for nm, trio in zip(group, outs):
            results[nm] = [small[nm]] + [o.reshape(given[nm].shape) for o in trio]

    rows_a = _unpack_rows(gath_a, shapes_a)
    rows_c = _unpack_rows(gath_c, shapes_c)
    dmod_rows = jnp.concatenate([rows_c[1], rows_c[2], rows_a[4], rows_a[5], rows_a[6], rows_a[7]],
                                axis=2).reshape(8, 2, 3 * d)
    dmod_shard = lax.dynamic_slice(dmod_rows, (0, 0, chip * nmod), (8, 2, nmod)).transpose(1, 0, 2)
    dmod_pad = jnp.pad(dmod_shard, ((0, 0), (0, MOD_ROWS - 8), (0, 0)))
    gr_wmod = _mod_bwd(cond_pad.T, dmod_pad, "mod_bwd")
    two_d = lambda a: a.reshape(-1, nmod)
    outs = _adamw(two_d(w_mod), [two_d(gr_wmod)], two_d(m_w_mod), two_d(v_w_mod), "adamw_w_mod")
    results["w_mod"] = [o.reshape(w_mod.shape) for o in outs]

    names = ["ln_pre_g", "ln_post_g", "w_mod", "b_mod", "w_in_ab", "w_out_ab", "sgu_norm_g", "sgu_w", "sgu_b",
             "w_in_ssm", "w_out_ssm", "lam_re", "lam_im", "b_re", "b_im", "c_re", "c_im", "d_skip", "log_dt",
             "w_glu", "b_glu"]
    return (loss, grad_x[None], *[results[nm][0] for nm in names], *[results[nm][1] for nm in names],
            *[results[nm][2] for nm in names], *[results[nm][3] for nm in names])
```

```python
import functools
import math

import jax
import jax.numpy as jnp
from jax import lax
from jax.experimental import pallas as pl
from jax.experimental.pallas import tpu as pltpu

F32 = jnp.float32
BF16 = jnp.bfloat16
MESH = pl.DeviceIdType.MESH

EPS = 1e-6
HEAD = 128
SSM_T = 512
SSM_GB = 16
ADAM_LR, ADAM_B1, ADAM_B2, ADAM_EPS, ADAM_WD, ADAM_STEP = 0.001, 0.9, 0.999, 1e-08, 0.01, 10
VMEM_LIMIT = 56 * 1024 * 1024

NN = (((1,), (0,)), ((), ()))
NT = (((1,), (1,)), ((), ()))
TN = (((0,), (0,)), ((), ()))


def _params(sem=None):
    return pltpu.CompilerParams(dimension_semantics=sem, vmem_limit_bytes=VMEM_LIMIT)


def _dot(a, b, dims=NN):
    return lax.dot_general(a, b, dims, preferred_element_type=F32)


def _bf(x):
    return x.astype(BF16)


def _gelu(x):
    k = math.sqrt(2.0 / math.pi)
    t = jnp.tanh(k * (x + 0.044715 * x * x * x))
    return 0.5 * x * (1.0 + t)


def _gelu_grad(x):
    k = math.sqrt(2.0 / math.pi)
    x2 = x * x
    t = jnp.tanh(k * (x + 0.044715 * x * x2))
    return 0.5 * (1.0 + t) + 0.5 * x * (1.0 - t * t) * k * (1.0 + 3.0 * 0.044715 * x2)


def _sigmoid(x):
    return 1.0 / (1.0 + jnp.exp(-x))


def _silu(x):
    return x * _sigmoid(x)


def _silu_grad(x):
    s = _sigmoid(x)
    return s * (1.0 + x * (1.0 - s))


def _tile(n, t, mult=128):
    if n <= t:
        return n
    for cand in range(t - t % mult, 0, -mult):
        if n % cand == 0:
            return cand
    raise ValueError((n, t, mult))


def _matmul(a, b, mode, out_dtype, name, tm=512, tn=512, tk=2048, n_split=1, ride=None, m_part=None):
    b_sharded = b.ndim == 3
    if mode == "nn":
        (m, kk), (_, n) = a.shape, b.shape
    elif b_sharded:
        assert mode == "nt"
        (m, kk), n, tk = a.shape, b.shape[1], b.shape[2]
    elif mode == "nt":
        (m, kk), (n, _) = a.shape, b.shape
    else:
        (kk, m), (_, n) = a.shape, b.shape
    m_off = 0
    if m_part is not None:
        assert mode == "tn"
        first, count, parts = m_part
        tm = _tile(m // parts, tm)
        m_off = first * (m // parts) // tm
        m = count * (m // parts)
    tm, tk = _tile(m, tm), _tile(kk, tk)
    ns = n // n_split
    tn = _tile(ns, tn)
    nk = kk // tk
    dims = {"nn": NN, "nt": NT, "tn": TN}[mode]

    def body(a_ref, b_ref, o_ref, acc_ref):
        k = pl.program_id(2)
        part = _dot(_bf(a_ref[...]), _bf(b_ref[0] if b_sharded else b_ref[...]), dims)

        @pl.when(k == 0)
        def _():
            acc_ref[...] = part

        @pl.when(k > 0)
        def _():
            acc_ref[...] += part

        @pl.when(k == nk - 1)
        def _():
            o_ref[...] = acc_ref[...].astype(out_dtype).reshape(o_ref.shape)

    if mode == "nn":
        a_spec = pl.BlockSpec((tm, tk), lambda i, j, k: (i, k))
        b_spec = pl.BlockSpec((tk, tn), lambda i, j, k: (k, j))
    elif mode == "nt":
        a_spec = pl.BlockSpec((tm, tk), lambda i, j, k: (i, k))
        b_spec = (pl.BlockSpec((1, tn, tk), lambda i, j, k: (k, j, 0)) if b_sharded
                  else pl.BlockSpec((tn, tk), lambda i, j, k: (j, k)))
    else:
        a_spec = pl.BlockSpec((tk, tm), lambda i, j, k: (k, i + m_off))
        b_spec = pl.BlockSpec((tk, tn), lambda i, j, k: (k, j))
    if n_split == 1:
        out_shape = jax.ShapeDtypeStruct((m, n), out_dtype)
        o_spec = pl.BlockSpec((tm, tn), lambda i, j, k: (i, j))
    else:
        per = ns // tn
        out_shape = jax.ShapeDtypeStruct((n_split, m, ns), out_dtype)
        o_spec = pl.BlockSpec((1, tm, tn), lambda i, j, k: (j // per, i, j % per))
    outs = _call(body, name=name, grid=(m // tm, n // tn, nk), in_specs=[a_spec, b_spec], out_specs=[o_spec],
                 out_shape=[out_shape], scratch_shapes=[pltpu.VMEM((tm, tn), F32)], args=(a, b),
                 sem=("parallel", "parallel", "arbitrary"), ride=ride)
    return outs[0] if ride is None else (outs[0], outs[1:])


def _matmul_at(a, b, out_dtype, name, tm=1024, tn=512, n_split=1, ride=None, m_part=None):
    (kk, m), (_, n) = a.shape, b.shape
    m_off = 0
    if m_part is not None:
        first, count, parts = m_part
        tm = _tile(m // parts, tm)
        m_off = first * (m // parts) // tm
        m = count * (m // parts)
    tm = _tile(m, tm)
    ns = n // n_split
    tn = _tile(ns, tn)
    kc = _tile(kk, 512)

    def body(a_ref, b_ref, o_ref, at_ref):
        @pl.when(pl.program_id(1) == 0)
        def _():
            for c in range(kk // kc):
                at_ref[:, c * kc:(c + 1) * kc] = _bf(a_ref[c * kc:(c + 1) * kc, :].astype(F32).T)

        o_ref[...] = _dot(at_ref[...], _bf(b_ref[...])).astype(out_dtype).reshape(o_ref.shape)

    if n_split == 1:
        out_shape = jax.ShapeDtypeStruct((m, n), out_dtype)
        o_spec = pl.BlockSpec((tm, tn), lambda i, j: (i, j))
    else:
        per = ns // tn
        out_shape = jax.ShapeDtypeStruct((n_split, m, ns), out_dtype)
        o_spec = pl.BlockSpec((1, tm, tn), lambda i, j: (j // per, i, j % per))
    outs = _call(body, name=name, grid=(m // tm, n // tn),
                 in_specs=[pl.BlockSpec((kk, tm), lambda i, j: (0, i + m_off)), pl.BlockSpec((kk, tn), lambda i, j: (0, j))],
                 out_specs=[o_spec], out_shape=[out_shape], scratch_shapes=[pltpu.VMEM((tm, kk), BF16)], args=(a, b),
                 sem=("arbitrary", "arbitrary"), ride=ride)
    return outs[0] if ride is None else (outs[0], outs[1:])


def _row_spec(tm, d):
    return pl.BlockSpec((tm, d), lambda i: (i, 0))


def _vec_spec(d):
    return pl.BlockSpec((1, d), lambda i: (0, 0))


def _acc(ref, first, val):
    @pl.when(first)
    def _():
        ref[...] = val

    @pl.when(jnp.logical_not(first))
    def _():
        ref[...] += val


def _colsum(x):
    return jnp.sum(x, axis=0, keepdims=True)


def _rownorm(x):
    r = lax.rsqrt(jnp.mean(x * x, axis=-1, keepdims=True) + EPS)
    return x * r, r


STRIP = 64


def _fold8(x):
    return functools.reduce(lambda a, b: a + b, [x[8 * k:8 * k + 8] for k in range(x.shape[0] // 8)])


def _pre_fwd(x, g, scale, shift, name, ride=None):
    l, d = x.shape
    tm = _tile(l, 256)

    def body(x_ref, g_ref, sc_ref, sh_ref, h_ref):
        n, _ = _rownorm(x_ref[...])
        h_ref[...] = _bf(n * g_ref[...] * (1.0 + sc_ref[...]) + sh_ref[...])

    outs = _call(body, name=name, grid=(l // tm,), in_specs=[_row_spec(tm, d), _vec_spec(d), _vec_spec(d), _vec_spec(d)],
                 out_specs=[_row_spec(tm, d)], out_shape=[jax.ShapeDtypeStruct((l, d), BF16)],
                 args=(x, g, scale, shift), sem=("parallel",), ride=ride)
    return outs[0], outs[1:]


def _post_pre_fwd(x, y, gate, pg, g1, scale1, shift1, name):
    l, d = x.shape
    tm = _tile(l, 256)

    def body(x_ref, y_ref, gate_ref, pg_ref, g1_ref, sc_ref, sh_ref, x1_ref, h1_ref):
        @pl.loop(0, tm // STRIP)
        def _(s):
            rows = pl.ds(pl.multiple_of(s * STRIP, STRIP), STRIP)
            ny, _ = _rownorm(y_ref[rows, :].astype(F32))
            x1 = x_ref[rows, :] + gate_ref[...] * (ny * pg_ref[...])
            x1_ref[rows, :] = x1
            n1, _ = _rownorm(x1)
            h1_ref[rows, :] = _bf(n1 * g1_ref[...] * (1.0 + sc_ref[...]) + sh_ref[...])

    v = _vec_spec(d)
    return pl.pallas_call(
        body, name=name, grid=(l // tm,),
        in_specs=[_row_spec(tm, d), _row_spec(tm, d), v, v, v, v, v],
        out_specs=[_row_spec(tm, d), _row_spec(tm, d)],
        out_shape=[jax.ShapeDtypeStruct((l, d), F32), jax.ShapeDtypeStruct((l, d), BF16)],
        compiler_params=_params(("parallel",)),
    )(x, y, gate, pg, g1, scale1, shift1)


def _post_loss(x1, y1, gate, pg, target, name):
    l, d = x1.shape
    tm = _tile(l, 256)

    def body(x_ref, y_ref, gate_ref, pg_ref, t_ref, loss_ref, dy_ref, dx_ref, dgate_ref, dpg_ref):
        first = pl.program_id(0) == 0

        def strip(s, sums):
            rows = pl.ds(pl.multiple_of(s * STRIP, STRIP), STRIP)
            ny, ry = _rownorm(y_ref[rows, :].astype(F32))
            q = ny * pg_ref[...]
            e = x_ref[rows, :] + gate_ref[...] * q - t_ref[rows, :]
            dx2 = e * (1.0 / d)
            dx_ref[rows, :] = dx2
            dq = dx2 * gate_ref[...]
            dny = dq * pg_ref[...]
            dy_ref[rows, :] = _bf(ry * (dny - ny * jnp.mean(dny * ny, axis=-1, keepdims=True)))
            return sums[0] + _fold8(e * e), sums[1] + _fold8(dx2 * q), sums[2] + _fold8(dq * ny)

        zero = jnp.zeros((8, d), F32)
        sq, dgate, dpg = lax.fori_loop(0, tm // STRIP, strip, (zero, zero, zero))
        _acc(loss_ref, first, jnp.full((1, 128), 0.5 / d, F32) * jnp.sum(sq))
        _acc(dgate_ref, first, _colsum(dgate))
        _acc(dpg_ref, first, _colsum(dpg))

    v = _vec_spec(d)
    return pl.pallas_call(
        body, name=name, grid=(l // tm,),
        in_specs=[_row_spec(tm, d), _row_spec(tm, d), v, v, _row_spec(tm, d)],
        out_specs=[_vec_spec(128), _row_spec(tm, d), _row_spec(tm, d), v, v],
        out_shape=[jax.ShapeDtypeStruct((1, 128), F32), jax.ShapeDtypeStruct((l, d), BF16),
                   jax.ShapeDtypeStruct((l, d), F32), jax.ShapeDtypeStruct((1, d), F32),
                   jax.ShapeDtypeStruct((1, d), F32)],
        compiler_params=_params(("arbitrary",)),
    )(x1, y1, gate, pg, target)


def _pre_bwd(dh, dres, x, g, scale, name, post=None):
    l, d = x.shape
    tm = _tile(l, 256)
    with_post = post is not None

    def body(*refs):
        if with_post:
            (dh_ref, dres_ref, x_ref, g_ref, sc_ref, y_ref, gate_ref, pg_ref,
             dx_ref, dsc_ref, dsh_ref, dg_ref, dy_ref, dgate_ref, dpg_ref) = refs
        else:
            dh_ref, dres_ref, x_ref, g_ref, sc_ref, dx_ref, dsc_ref, dsh_ref, dg_ref = refs
        first = pl.program_id(0) == 0

        def strip(s, sums):
            rows = pl.ds(pl.multiple_of(s * STRIP, STRIP), STRIP)
            dh = dh_ref[rows, :].astype(F32)
            n, r = _rownorm(x_ref[rows, :])
            dyn = dh * (1.0 + sc_ref[...])
            dn = dyn * g_ref[...]
            dx = dres_ref[rows, :] + r * (dn - n * jnp.mean(dn * n, axis=-1, keepdims=True))
            dx_ref[rows, :] = dx
            new = [sums[0] + _fold8(dh * (n * g_ref[...])), sums[1] + _fold8(dh), sums[2] + _fold8(dyn * n)]
            if with_post:
                ny, ry = _rownorm(y_ref[rows, :].astype(F32))
                dq = dx * gate_ref[...]
                dny = dq * pg_ref[...]
                dy_ref[rows, :] = _bf(ry * (dny - ny * jnp.mean(dny * ny, axis=-1, keepdims=True)))
                new += [sums[3] + _fold8(dx * (ny * pg_ref[...])), sums[4] + _fold8(dq * ny)]
            return tuple(new)

        zero = jnp.zeros((8, d), F32)
        sums = lax.fori_loop(0, tm // STRIP, strip, (zero,) * (5 if with_post else 3))
        outs = [dsc_ref, dsh_ref, dg_ref] + ([dgate_ref, dpg_ref] if with_post else [])
        for ref, acc in zip(outs, sums):
            _acc(ref, first, _colsum(acc))

    v = _vec_spec(d)
    row = _row_spec(tm, d)
    vec_out = jax.ShapeDtypeStruct((1, d), F32)
    in_specs = [row, row, row, v, v]
    args = [dh, dres, x, g, scale]
    out_specs = [row, v, v, v]
    out_shape = [jax.ShapeDtypeStruct((l, d), F32), vec_out, vec_out, vec_out]
    if with_post:
        in_specs += [row, v, v]
        args += list(post)
        out_specs += [row, v, v]
        out_shape += [jax.ShapeDtypeStruct((l, d), BF16), vec_out, vec_out]
    return pl.pallas_call(
        body, name=name, grid=(l // tm,), in_specs=in_specs, out_specs=out_specs, out_shape=out_shape,
        compiler_params=_params(("arbitrary",)),
    )(*args)


def _softplus_parts(z):
    e = jnp.exp(-jnp.abs(z))
    den = 1.0 + e
    lb = jnp.minimum(z, 0.0) - jnp.log(den)
    return lb, lb - z, jnp.exp(lb)


def _tri(cmp, n=HEAD):
    row = lax.broadcasted_iota(jnp.int32, (n, n), 0)
    col = lax.broadcasted_iota(jnp.int32, (n, n), 1)
    return cmp(row, col)


ATT_T = 256
ATT_DEAD = 104.0


def _any_alive(runs):
    return functools.reduce(jnp.maximum, [jnp.max(r) for r in runs]) > -ATT_DEAD


def _attn_fwd(qkv, wb, name, hp=4, ride=None):
    l = qkv.shape[0]
    t = ATT_T
    nh, nq = wb // HEAD, l // t
    hp = min(hp, nh)
    ng, wg = nh // hp, hp * HEAD
    scale = 1.0 / math.sqrt(HEAD)

    def body(q_ref, k_ref, v_ref, o_ref):
        i = pl.program_id(1)
        valid = _tri(lambda r, c: c < r, t)
        m_gt = _bf(_tri(lambda r, c: r > c, t).astype(F32))

        def tile(j, carry, diag):
            rows = pl.ds(pl.multiple_of(j * t, t), t)
            cols = [slice(hh * HEAD, (hh + 1) * HEAD) for hh in range(hp)]
            zs = [_dot(q_ref[:, cs], k_ref[rows, cs], NT) * scale for cs in cols]
            lbs, lks = [], []
            for z in zs:
                lb, lk, _ = _softplus_parts(z)
                lbs.append(lb)
                lks.append(jnp.where(valid, lk, 0.0) if diag else lk)
            laters = [_dot(_bf(lk), m_gt) for lk in lks]
            ws = [jnp.exp(lb + later + run) for lb, later, (_, run) in zip(lbs, laters, carry)]
            if diag:
                ws = [jnp.where(valid, w, 0.0) for w in ws]
            return tuple((acc + _dot(_bf(w), v_ref[rows, cs]), run + jnp.sum(lk, axis=1, keepdims=True))
                         for w, lk, cs, (acc, run) in zip(ws, lks, cols, carry))

        zero = (jnp.zeros((t, HEAD), F32), jnp.zeros((t, 1), F32))
        carry = tile(i, (zero,) * hp, True)
        _, carry = lax.while_loop(lambda c: (c[0] < i) & _any_alive([run for _, run in c[1]]),
                                  lambda c: (c[0] + 1, tile(i - 1 - c[0], c[1], False)), (jnp.int32(0), carry))
        for hh, (acc, _) in enumerate(carry):
            o_ref[:, hh * HEAD:(hh + 1) * HEAD] = acc

    blk = lambda off: pl.BlockSpec((t, wg), lambda h, i: (i, off + h))
    full = lambda off: pl.BlockSpec((l, wg), lambda h, i: (0, off + h))
    out = pl.BlockSpec((t, wg), lambda h, i: (i, h))
    outs = _call(body, name=name, grid=(ng, nq), in_specs=[blk(0), full(ng), full(2 * ng)], out_specs=[out],
                 out_shape=[jax.ShapeDtypeStruct((l, wb), F32)],
                 args=(qkv, qkv, qkv), sem=("parallel", "arbitrary"), ride=ride)
    return outs[0], outs[1:]


def _attn_bwd(qkv, proj, dcat, wa, wb, name, hp=2, ride=None):
    l = qkv.shape[0]
    t = ATT_T
    nh, nq = wb // HEAD, l // t
    hp = min(hp, nh)
    ng, wg = nh // hp, hp * HEAD
    scale = 1.0 / math.sqrt(HEAD)

    def body(q_ref, k_ref, v_ref, bz_ref, dc_ref, dq_ref, dkt_out, dvt_out, do_s, qt_s, dot_s,
             dkt_ref, dvt_ref, out_sems):
        i = pl.program_id(1)

        @pl.when(i == 0)
        def _():
            dkt_ref[...] = jnp.zeros_like(dkt_ref)
            dvt_ref[...] = jnp.zeros_like(dvt_ref)

        do = dc_ref[...].astype(F32) * _silu(bz_ref[...])
        do_s[...] = _bf(do)
        for hh in range(hp):
            cs = slice(hh * HEAD, (hh + 1) * HEAD)
            qt_s[hh] = _bf(q_ref[:, cs].astype(F32).T * scale)
            dot_s[hh] = _bf(do[:, cs].T)
        valid = _tri(lambda r, c: c < r, t)
        m_le = _bf(_tri(lambda r, c: r <= c, t).astype(F32))
        m_lt = _bf(_tri(lambda r, c: r < c, t).astype(F32))

        heads = range(hp)
        cols = [slice(hh * HEAD, (hh + 1) * HEAD) for hh in heads]

        def row_sums(j, runs, diag):
            rows = pl.ds(pl.multiple_of(j * t, t), t)
            out = []
            for cs, run in zip(cols, runs):
                _, lk, _ = _softplus_parts(_dot(q_ref[:, cs], k_ref[rows, cs], NT) * scale)
                if diag:
                    lk = jnp.where(valid, lk, 0.0)
                out.append(run + jnp.sum(lk, axis=1, keepdims=True))
            return tuple(out)

        runs = row_sums(i, (jnp.zeros((t, 1), F32),) * hp, True)
        below, lktot = lax.while_loop(lambda c: (c[0] < i) & _any_alive(c[1]),
                                      lambda c: (c[0] + 1, row_sums(i - 1 - c[0], c[1], False)), (jnp.int32(0), runs))

        def tile(j, carry, diag):
            rows = pl.ds(pl.multiple_of(j * t, t), t)
            zs = [_dot(q_ref[:, cs], k_ref[rows, cs], NT) * scale for cs in cols]
            dws = [_dot(do_s[:, cs], v_ref[rows, cs], NT) for cs in cols]
            lbs, lks, sigs = [], [], []
            for z in zs:
                lb, lk, sig = _softplus_parts(z)
                lbs.append(lb)
                lks.append(jnp.where(valid, lk, 0.0) if diag else lk)
                sigs.append(sig)
            pins = [_dot(_bf(lk), m_le) for lk in lks]
            ws = [jnp.exp(lbs[hh] + (lktot[hh] - carry[hh][1]) - pins[hh]) for hh in heads]
            if diag:
                ws = [jnp.where(valid, w, 0.0) for w in ws]
            das = [dw * w for dw, w in zip(dws, ws)]
            pexs = [_dot(_bf(da), m_lt) for da in das]
            dzs = [das[hh] - sigs[hh] * (das[hh] + carry[hh][2] + pexs[hh]) for hh in heads]
            if diag:
                dzs = [jnp.where(valid, dz, 0.0) for dz in dzs]
            dzs = [_bf(dz) for dz in dzs]
            out = []
            for hh in heads:
                dkt, dvt = _dot(qt_s[hh], dzs[hh]), _dot(dot_s[hh], _bf(ws[hh]))
                for half in range(t // HEAD):
                    dkt_ref[hh, sub * j + half] += dkt[:, half * HEAD:(half + 1) * HEAD]
                    dvt_ref[hh, sub * j + half] += dvt[:, half * HEAD:(half + 1) * HEAD]
                dq, cpre, ppre = carry[hh]
                out.append((dq + _dot(dzs[hh], k_ref[rows, cols[hh]]), cpre + jnp.sum(lks[hh], axis=1, keepdims=True),
                            ppre + pexs[hh][:, t - 1:] + das[hh][:, t - 1:]))
            return tuple(out)

        zero = (jnp.zeros((t, HEAD), F32), jnp.zeros((t, 1), F32), jnp.zeros((t, 1), F32))
        carry = lax.fori_loop(i - below, i, lambda j, c: tile(j, c, False), (zero,) * hp)
        carry = tile(i, carry, True)
        for hh in range(hp):
            dq_ref[:, hh * HEAD:(hh + 1) * HEAD] = carry[hh][0] * scale

        @pl.when(i == nq - 1)
        def _():
            heads = pl.ds(pl.program_id(0) * hp, hp)
            copies = [pltpu.make_async_copy(dkt_ref, dkt_out.at[heads], out_sems.at[0]),
                      pltpu.make_async_copy(dvt_ref, dvt_out.at[heads], out_sems.at[1])]
            for cp in copies:
                cp.start()
            for cp in copies:
                cp.wait()

    sub = t // HEAD
    blk = lambda off: pl.BlockSpec((t, wg), lambda h, i: (i, off + h))
    full = lambda off: pl.BlockSpec((l, wg), lambda h, i: (0, off + h))
    acc_shape = jax.ShapeDtypeStruct((nh, l // HEAD, HEAD, HEAD), F32)
    acc_scratch = pltpu.VMEM((hp, l // HEAD, HEAD, HEAD), F32)
    outs = _call(
        body, name=name, grid=(ng, nq),
        in_specs=[blk(0), full(ng), full(2 * ng), blk(3 * wa // wg), blk(wa // wg)],
        out_specs=[blk(0), ANY, ANY], out_shape=[jax.ShapeDtypeStruct((l, wb), F32), acc_shape, acc_shape],
        scratch_shapes=[pltpu.VMEM((t, wg), BF16), pltpu.VMEM((hp, HEAD, t), BF16), pltpu.VMEM((hp, HEAD, t), BF16),
                        acc_scratch, acc_scratch, pltpu.SemaphoreType.DMA((2,))],
        args=(qkv, qkv, qkv, proj, dcat), sem=("parallel", "arbitrary"), ride=ride)
    return outs[0], outs[1], outs[2], outs[3:]


def _sgu_heads(v, g_ref, w_ref, bt_ref, nh):
    keep = _tri(lambda r, c: r >= c)
    out = []
    for h in range(nh):
        cols = slice(h * HEAD, (h + 1) * HEAD)
        nv, r = _rownorm(v[:, cols])
        wm = jnp.where(keep, w_ref[h], 0.0)
        s = _dot(_bf(wm), _bf(nv * g_ref[:, cols])) + bt_ref[:, h:h + 1]
        out.append((nv, r, wm, s))
    return out


def _sgu_fwd(proj, out_b, norm_g, sgu_w, sgu_bt, wa, wb, name):
    l, n = proj.shape
    nh = wa // HEAD

    def body(au_ref, av_ref, az_ref, bz_ref, ob_ref, g_ref, w_ref, bt_ref, cat_ref):
        u, v, sz = _gelu(au_ref[...]), _gelu(av_ref[...]), _silu(az_ref[...])
        for h, (_, _, _, s) in enumerate(_sgu_heads(v, g_ref, w_ref, bt_ref, nh)):
            cols = slice(h * HEAD, (h + 1) * HEAD)
            cat_ref[:, cols] = _bf(u[:, cols] * s * sz[:, cols])
        cat_ref[:, wa:] = _bf(ob_ref[...] * _silu(bz_ref[...]))

    a_blk = lambda j: pl.BlockSpec((HEAD, wa), lambda i: (i, j))
    return pl.pallas_call(
        body, name=name, grid=(l // HEAD,),
        in_specs=[a_blk(0), a_blk(1), a_blk(2), a_blk(3), pl.BlockSpec((HEAD, wb), lambda i: (i, 0)),
                  _vec_spec(wa), pl.BlockSpec((nh, HEAD, HEAD), lambda i: (0, 0, 0)),
                  pl.BlockSpec((HEAD, nh), lambda i: (0, 0))],
        out_specs=pl.BlockSpec((HEAD, wa + wb), lambda i: (i, 0)),
        out_shape=jax.ShapeDtypeStruct((l, wa + wb), BF16),
        compiler_params=_params(("parallel",)),
    )(proj, proj, proj, proj, out_b, norm_g, sgu_w, sgu_bt)


def _sgu_bwd(proj, out_b, dcat, dq, dk, dv, norm_g, sgu_w, sgu_bt, wa, wb, name, ride=None):
    l = proj.shape[0]
    n = 3 * wa + 4 * wb
    nh = wa // HEAD

    def body(au_ref, av_ref, az_ref, bz_ref, ob_ref, dc_ref, dq_ref, dk_ref, dv_ref, g_ref, w_ref, wt_ref, bt_ref,
             dp_ref, dw_ref, dbt_ref, dg_ref):
        first = pl.program_id(0) == 0
        keep = _tri(lambda r, c: r >= c)
        au, av, az = au_ref[...], av_ref[...], az_ref[...]
        u, v, sz = _gelu(au), _gelu(av), _silu(az)
        dgelu_u, dgelu_v, dsilu_z = _gelu_grad(au), _gelu_grad(av), _silu_grad(az)
        heads = _sgu_heads(v, g_ref, w_ref, bt_ref, nh)
        cols = [slice(h * HEAD, (h + 1) * HEAD) for h in range(nh)]
        dss = []
        for h, (nv, r, wm, s) in enumerate(heads):
            dca, uh, szh = dc_ref[:, cols[h]].astype(F32), u[:, cols[h]], sz[:, cols[h]]
            dp_ref[:, cols[h]] = _bf(dca * s * szh * dgelu_u[:, cols[h]])
            dp_ref[:, 2 * wa + h * HEAD:2 * wa + (h + 1) * HEAD] = _bf(dca * uh * s * dsilu_z[:, cols[h]])
            dss.append(dca * uh * szh)
        dws = [_dot(_bf(ds), _bf(nv * g_ref[:, cs]), NT) for ds, cs, (nv, _, _, _) in zip(dss, cols, heads)]
        keep_t = _tri(lambda r, c: r <= c)
        dvhs = [_dot(_bf(jnp.where(keep_t, wt_ref[h], 0.0)), _bf(dss[h])) for h in range(nh)]
        dg_parts = []
        for h, (nv, r, wm, s) in enumerate(heads):
            _acc(dw_ref.at[h], first, jnp.where(keep, dws[h], 0.0))
            _acc(dbt_ref.at[:, h:h + 1], first, jnp.sum(dss[h], axis=1, keepdims=True))
            dg_parts.append(_colsum(dvhs[h] * nv))
            dnv = dvhs[h] * g_ref[:, cols[h]]
            dvv = r * (dnv - nv * jnp.mean(dnv * nv, axis=-1, keepdims=True))
            dp_ref[:, wa + h * HEAD:wa + (h + 1) * HEAD] = _bf(dvv * dgelu_v[:, cols[h]])
        _acc(dg_ref, first, jnp.concatenate(dg_parts, axis=1))
        base = 3 * wa
        dp_ref[:, base:base + wb] = _bf(dq_ref[...])
        for h in range(wb // HEAD):
            dp_ref[:, base + wb + h * HEAD:base + wb + (h + 1) * HEAD] = _bf(dk_ref[h, 0].T)
            dp_ref[:, base + 2 * wb + h * HEAD:base + 2 * wb + (h + 1) * HEAD] = _bf(dv_ref[h, 0].T)
        dp_ref[:, base + 3 * wb:] = _bf(dc_ref[:, wa:].astype(F32) * ob_ref[...] * _silu_grad(bz_ref[...]))

    a_blk = lambda j: pl.BlockSpec((HEAD, wa), lambda i: (i, j))
    b_blk = pl.BlockSpec((HEAD, wb), lambda i: (i, 0))
    t_blk = pl.BlockSpec((wb // HEAD, 1, HEAD, HEAD), lambda i: (0, i, 0, 0))
    w_spec = pl.BlockSpec((nh, HEAD, HEAD), lambda i: (0, 0, 0))
    bt_spec = pl.BlockSpec((HEAD, nh), lambda i: (0, 0))
    outs = _call(
        body, name=name, grid=(l // HEAD,), ride=ride, sem=("arbitrary",),
        in_specs=[a_blk(0), a_blk(1), a_blk(2), a_blk(3), b_blk, pl.BlockSpec((HEAD, wa + wb), lambda i: (i, 0)),
                  b_blk, t_blk, t_blk, _vec_spec(wa), w_spec, w_spec, bt_spec],
        out_specs=[pl.BlockSpec((HEAD, n), lambda i: (i, 0)), w_spec, bt_spec, _vec_spec(wa)],
        out_shape=[jax.ShapeDtypeStruct((l, n), BF16), jax.ShapeDtypeStruct((nh, HEAD, HEAD), F32),
                   jax.ShapeDtypeStruct((HEAD, nh), F32), jax.ShapeDtypeStruct((1, wa), F32)],
        args=(proj, proj, proj, proj, out_b, dcat, dq, dk, dv, norm_g, sgu_w, sgu_w.transpose(0, 2, 1), sgu_bt))
    return (*outs[:4], outs[4:])


def _ssm_discretise(lr, li, ldt, br, bi):
    dt = jnp.exp(ldt)
    mag = jnp.exp(lr * dt)
    a_re = mag * jnp.cos(li * dt)
    a_im = mag * jnp.sin(li * dt)
    den = lr * lr + li * li
    nr = a_re - 1.0
    coef_re = (nr * lr + a_im * li) / den
    coef_im = (a_im * lr - nr * li) / den
    return a_re, a_im, coef_re * br - coef_im * bi, coef_re * bi + coef_im * br


def _ssm_prep(lr, li, ldt, br, bi, lr_row, li_row, ldt_row, name):
    s, c = br.shape

    def body(lr_ref, li_ref, ldt_ref, br_ref, bi_ref, lrr_ref, lir_ref, ldtr_ref, bbr_ref, bbi_ref, tr_ref, ti_ref):
        _, _, bbr, bbi = _ssm_discretise(lr_ref[...], li_ref[...], ldt_ref[...], br_ref[...], bi_ref[...])
        bbr_ref[...] = bbr
        bbi_ref[...] = bbi
        row = lax.broadcasted_iota(jnp.int32, (SCAN_ROWS, 1), 0)
        blk, r = jnp.right_shift(row, 3), jnp.bitwise_and(row, 7)
        kind, rev = jnp.bitwise_and(blk, 3), blk >= 4
        step = jnp.left_shift(1, kind)
        n = jnp.where(kind < 3, step, jnp.where(rev, 8 - r, r + 1)).astype(F32)
        keep = (kind == 3) | (rev & (r < 8 - step)) | (jnp.logical_not(rev) & (r >= step))
        dt = jnp.exp(ldtr_ref[...])
        mag = jnp.exp(n * (lrr_ref[...] * dt))
        ang = n * (lir_ref[...] * dt)
        tr_ref[...] = jnp.where(keep, mag * jnp.cos(ang), 0.0)
        ti_ref[...] = jnp.where(keep, jnp.where(rev, -1.0, 1.0) * mag * jnp.sin(ang), 0.0)

    col = jax.ShapeDtypeStruct((s, c), F32)
    row = jax.ShapeDtypeStruct((SCAN_ROWS, s), F32)
    return pl.pallas_call(body, name=name, out_shape=[col, col, row, row])(
        lr, li, ldt, br, bi, lr_row, li_row, ldt_row)


def _ssm_prep_bwd(lr, li, ldt, br, bi, da_re, da_im, dbb_re, dbb_im, p, name):
    s, c = br.shape

    def body(lr_ref, li_ref, ldt_ref, br_ref, bi_ref, dar_ref, dai_ref, dbr_ref, dbi_ref,
             dlr_ref, dli_ref, dldt_ref, dbre_ref, dbim_ref):
        args = (lr_ref[...], li_ref[...], ldt_ref[...], br_ref[...], bi_ref[...])
        _, vjp = jax.vjp(_ssm_discretise, *args)
        dlr, dli, dldt, dbr, dbi = vjp((dar_ref[...], dai_ref[...], dbr_ref[...], dbi_ref[...]))
        dlr_ref[...] = dlr
        dli_ref[...] = dli
        dbre_ref[...] = dbr
        dbim_ref[...] = dbi
        idx = lax.broadcasted_iota(jnp.int32, (s, s // p), 0)
        grp = lax.broadcasted_iota(jnp.int32, (s, s // p), 1)
        own = (idx >= grp * p) & (idx < (grp + 1) * p)
        dldt_ref[...] = _colsum(jnp.where(own, dldt, 0.0))

    col1 = jax.ShapeDtypeStruct((s, 1), F32)
    colc = jax.ShapeDtypeStruct((s, c), F32)
    return pl.pallas_call(
        body, name=name, out_shape=[col1, col1, jax.ShapeDtypeStruct((1, s // p), F32), colc, colc],
    )(lr, li, ldt, br, bi, da_re, da_im, dbb_re, dbb_im)


SCAN_ROWS = 64


def _scan_groups(xr, xi, tr_ref, ti_ref, cr, ci, reverse):
    ng = xr.shape[0] // 8
    base = SCAN_ROWS // 2 if reverse else 0
    pr, pi = tr_ref[base + 24:base + 32, :], ti_ref[base + 24:base + 32, :]
    edge = slice(0, 1) if reverse else slice(7, 8)
    out_r, out_i = [None] * ng, [None] * ng
    for g in (range(ng - 1, -1, -1) if reverse else range(ng)):
        sr, si = xr[8 * g:8 * g + 8, :], xi[8 * g:8 * g + 8, :]
        for k in range(3):
            ar, ai = tr_ref[base + 8 * k:base + 8 * k + 8, :], ti_ref[base + 8 * k:base + 8 * k + 8, :]
            shift = 8 - (1 << k) if reverse else 1 << k
            rr, ri = pltpu.roll(sr, shift, 0), pltpu.roll(si, shift, 0)
            sr, si = sr + ar * rr - ai * ri, si + ar * ri + ai * rr
        sr, si = sr + pr * cr - pi * ci, si + pr * ci + pi * cr
        cr, ci = sr[edge, :], si[edge, :]
        out_r[g], out_i[g] = sr, si
    return jnp.concatenate(out_r, axis=0), jnp.concatenate(out_i, axis=0), cr, ci


def _ssm_fwd(proj, bbd, ccd, pw_re, pw_im, d_skip, w, name):
    l = proj.shape[0]
    nb, cw, ns2 = bbd.shape
    ns = ns2 // 2
    nc = l // SSM_T

    def body(u_ref, bbd_ref, ccd_ref, pr_ref, pi_ref, d_ref, y_ref, hsr_ref, hsi_ref, h_ref, hr_s, hi_s):
        @pl.when(pl.program_id(1) == 0)
        def _():
            hr_s[...] = jnp.zeros_like(hr_s)
            hi_s[...] = jnp.zeros_like(hi_s)

        hsr_ref[...] = hr_s[...].reshape(hsr_ref.shape)
        hsi_ref[...] = hi_s[...].reshape(hsi_ref.shape)
        u = u_ref[...]
        bu = _dot(_bf(u), bbd_ref[0])
        hr, hi, cr, ci = _scan_groups(bu[:, :ns], bu[:, ns:], pr_ref, pi_ref, hr_s[...], hi_s[...], False)
        hr_s[...] = cr
        hi_s[...] = ci
        h_bf = _bf(jnp.concatenate([hr, hi], axis=1))
        h_ref[...] = h_bf
        y_ref[...] = _dot(h_bf, ccd_ref[0]) + d_ref[...] * u

    tab = pl.BlockSpec((SCAN_ROWS, ns), lambda b, k: (0, b))
    return pl.pallas_call(
        body, name=name, grid=(nb, nc),
        in_specs=[pl.BlockSpec((SSM_T, cw), lambda b, k: (k, b)),
                  pl.BlockSpec((1, cw, ns2), lambda b, k: (b, 0, 0)),
                  pl.BlockSpec((1, ns2, cw), lambda b, k: (b, 0, 0)),
                  tab, tab, pl.BlockSpec((1, cw), lambda b, k: (0, b))],
        out_specs=[pl.BlockSpec((SSM_T, cw), lambda b, k: (k, b)),
                   pl.BlockSpec((1, 1, ns), lambda b, k: (k, 0, b)), pl.BlockSpec((1, 1, ns), lambda b, k: (k, 0, b)),
                   pl.BlockSpec((SSM_T, ns2), lambda b, k: (k, b))],
        out_shape=[jax.ShapeDtypeStruct((l, w), F32), jax.ShapeDtypeStruct((nc, 1, nb * ns), F32),
                   jax.ShapeDtypeStruct((nc, 1, nb * ns), F32), jax.ShapeDtypeStruct((l, nb * ns2), BF16)],
        scratch_shapes=[pltpu.VMEM((1, ns), F32), pltpu.VMEM((1, ns), F32)],
        compiler_params=_params(("parallel", "arbitrary")),
    )(proj, bbd, ccd, pw_re, pw_im, d_skip)


def _ssm_bwd(proj, dy, hs_re, hs_im, h_all, bbd, ccd, pw_re, pw_im, d_skip, w, name, ride=None):
    l = proj.shape[0]
    nb, cw, ns2 = bbd.shape
    ns = ns2 // 2
    nc = l // SSM_T

    def body(u_ref, dy_ref, hsr_ref, hsi_ref, h_ref, bbd_ref, ccd_ref, pr_ref, pi_ref, d_ref,
             du_ref, dbbd_ref, dccd_ref, dar_ref, dai_ref, dd_ref, gr_s, gi_s):
        first = pl.program_id(1) == 0

        @pl.when(first)
        def _():
            gr_s[...] = jnp.zeros_like(gr_s)
            gi_s[...] = jnp.zeros_like(gi_s)

        u, dy = u_ref[...], dy_ref[...]
        dy_bf = _bf(dy)
        hr0, hi0 = hsr_ref[0], hsi_ref[0]
        h = h_ref[...].astype(F32)
        hr, hi = h[:, :ns], h[:, ns:]
        dh = _dot(dy_bf, ccd_ref[0], NT)
        gr, gi, gcr, gci = _scan_groups(dh[:, :ns], dh[:, ns:], pr_ref, pi_ref, gr_s[...], gi_s[...], True)
        gr_s[...] = gcr
        gi_s[...] = gci
        row0 = lax.broadcasted_iota(jnp.int32, hr.shape, 0) == 0
        pr_h = jnp.where(row0, hr0, pltpu.roll(hr, 1, 0))
        pi_h = jnp.where(row0, hi0, pltpu.roll(hi, 1, 0))
        _acc(dar_ref, first, _colsum(pr_h * gr + pi_h * gi))
        _acc(dai_ref, first, _colsum(pr_h * gi - pi_h * gr))
        g_bf = _bf(jnp.concatenate([gr, gi], axis=1))
        _acc(dbbd_ref.at[0], first, _dot(_bf(u.T), g_bf))
        _acc(dccd_ref.at[0], first, _dot(_bf(h.T), dy_bf))
        du_ref[...] = _bf(_dot(g_bf, bbd_ref[0], NT) + d_ref[...] * dy)
        _acc(dd_ref, first, _colsum(dy * u))

    rev = lambda b, k: (nc - 1 - k, b)
    outs = _call(
        body, name=name, grid=(nb, nc), ride=ride, sem=("parallel", "arbitrary"),
        args=(proj, dy, hs_re, hs_im, h_all, bbd, ccd, pw_re, pw_im, d_skip),
        in_specs=[pl.BlockSpec((SSM_T, cw), rev), pl.BlockSpec((SSM_T, cw), rev),
                  pl.BlockSpec((1, 1, ns), lambda b, k: (nc - 1 - k, 0, b)),
                  pl.BlockSpec((1, 1, ns), lambda b, k: (nc - 1 - k, 0, b)),
                  pl.BlockSpec((SSM_T, ns2), rev),
                  pl.BlockSpec((1, cw, ns2), lambda b, k: (b, 0, 0)),
                  pl.BlockSpec((1, ns2, cw), lambda b, k: (b, 0, 0)),
                  pl.BlockSpec((SCAN_ROWS, ns), lambda b, k: (0, b)), pl.BlockSpec((SCAN_ROWS, ns), lambda b, k: (0, b)),
                  pl.BlockSpec((1, cw), lambda b, k: (0, b))],
        out_specs=[pl.BlockSpec((SSM_T, cw), rev),
                   pl.BlockSpec((1, cw, ns2), lambda b, k: (b, 0, 0)),
                   pl.BlockSpec((1, ns2, cw), lambda b, k: (b, 0, 0)),
                   pl.BlockSpec((1, ns), lambda b, k: (0, b)), pl.BlockSpec((1, ns), lambda b, k: (0, b)),
                   pl.BlockSpec((1, cw), lambda b, k: (0, b))],
        out_shape=[jax.ShapeDtypeStruct((l, w), BF16), jax.ShapeDtypeStruct(bbd.shape, F32),
                   jax.ShapeDtypeStruct(ccd.shape, F32), jax.ShapeDtypeStruct((1, nb * ns), F32),
                   jax.ShapeDtypeStruct((1, nb * ns), F32), jax.ShapeDtypeStruct((1, w), F32)],
        scratch_shapes=[pltpu.VMEM((1, ns), F32), pltpu.VMEM((1, ns), F32)])
    return (*outs[:6], outs[6:])


def _block_diag_b(bb_re, bb_im, g, p, c):
    nb = g // SSM_GB
    keep = _same_group(SSM_GB * c, c, SSM_GB * p, p)

    def one(bb):
        t = bb.reshape(nb, SSM_GB, p, c).transpose(0, 1, 3, 2).reshape(nb, SSM_GB * c, p)
        return jnp.where(keep, jnp.tile(t, (1, 1, SSM_GB)), 0.0)

    return jnp.concatenate([one(bb_re), one(bb_im)], axis=2)


def _same_group(rows, per_row, cols, per_col):
    r = lax.broadcasted_iota(jnp.int32, (rows, cols), 0) // per_row
    q = lax.broadcasted_iota(jnp.int32, (rows, cols), 1) // per_col
    return r == q


def _block_diag_c(c_re, c_im, g, p, c):
    nb = g // SSM_GB
    keep = _same_group(SSM_GB * p, p, SSM_GB * c, c)

    def one(cc):
        t = cc.reshape(nb, SSM_GB, c, p).transpose(0, 1, 3, 2).reshape(nb, SSM_GB * p, c)
        return jnp.where(keep, jnp.tile(t, (1, 1, SSM_GB)), 0.0)

    return jnp.concatenate([one(c_re), one(-c_im)], axis=1)


def _diag_of_b(dbbd, g, p, c):
    nb = g // SSM_GB
    keep = _same_group(SSM_GB * c, c, SSM_GB * p, p)

    def one(blk):
        d = jnp.where(keep, blk, 0.0).reshape(nb, SSM_GB * c, SSM_GB, p).sum(axis=2)
        return d.reshape(nb, SSM_GB, c, p).transpose(0, 1, 3, 2).reshape(g * p, c)

    half = SSM_GB * p
    return one(dbbd[:, :, :half]), one(dbbd[:, :, half:])


def _diag_of_c(dccd, g, p, c):
    nb = g // SSM_GB
    keep = _same_group(SSM_GB * p, p, SSM_GB * c, c)

    def one(blk):
        d = jnp.where(keep, blk, 0.0).reshape(nb, SSM_GB * p, SSM_GB, c).sum(axis=2)
        return d.reshape(nb, SSM_GB, p, c).transpose(0, 1, 3, 2).reshape(g, c, p)

    half = SSM_GB * p
    return one(dccd[:, :half]), -one(dccd[:, half:])


def _glu_fwd(y, proj, w_glu, b_glu, name):
    l, w = y.shape
    tm = _tile(l, 256)

    def body(y_ref, z_ref, w_ref, b_ref, o_ref):
        g = _gelu(y_ref[...])
        t = _dot(_bf(g), w_ref[...]) + b_ref[...]
        o_ref[...] = _bf(g * _sigmoid(t) * _silu(z_ref[...]))

    return pl.pallas_call(
        body, name=name, grid=(l // tm,),
        in_specs=[_row_spec(tm, w), pl.BlockSpec((tm, w), lambda i: (i, 1)),
                  pl.BlockSpec((w, w), lambda i: (0, 0)), _vec_spec(w)],
        out_specs=_row_spec(tm, w), out_shape=jax.ShapeDtypeStruct((l, w), BF16),
        compiler_params=_params(("parallel",)),
    )(y, proj, w_glu, b_glu)


def _glu_bwd(do, y, proj, w_glu, b_glu, name):
    l, w = y.shape
    tm = _tile(l, 512)
    nsteps = l // tm

    def body(do_ref, y_ref, z_ref, w_ref, b_ref, dy_ref, dz_ref, dw_ref, db_ref, dw_acc):
        i = pl.program_id(0)
        first = i == 0
        yv, z, do = y_ref[...], z_ref[...], do_ref[...].astype(F32)
        g = _gelu(yv)
        g_bf = _bf(g)
        sg = _sigmoid(_dot(g_bf, w_ref[...]) + b_ref[...])
        dyy = do * _silu(z)
        dz_ref[...] = _bf(do * g * sg * _silu_grad(z))
        dt = dyy * g * sg * (1.0 - sg)
        dt_bf = _bf(dt)
        dg = dyy * sg + _dot(dt_bf, w_ref[...], NT)
        dy_ref[...] = dg * _gelu_grad(yv)
        _acc(dw_acc, first, _dot(_bf(g.T), dt_bf))
        _acc(db_ref, first, _colsum(dt))

        @pl.when(i == nsteps - 1)
        def _():
            dw_ref[...] = _bf(dw_acc[...])

    return pl.pallas_call(
        body, name=name, grid=(nsteps,),
        in_specs=[_row_spec(tm, w), _row_spec(tm, w), pl.BlockSpec((tm, w), lambda i: (i, 1)),
                  pl.BlockSpec((w, w), lambda i: (0, 0)), _vec_spec(w)],
        out_specs=[_row_spec(tm, w), _row_spec(tm, w), pl.BlockSpec((w, w), lambda i: (0, 0)), _vec_spec(w)],
        out_shape=[jax.ShapeDtypeStruct((l, w), F32), jax.ShapeDtypeStruct((l, w), BF16),
                   jax.ShapeDtypeStruct((w, w), BF16), jax.ShapeDtypeStruct((1, w), F32)],
        scratch_shapes=[pltpu.VMEM((w, w), F32)],
        compiler_params=_params(("arbitrary",)),
    )(do, y, proj, w_glu, b_glu)


MOD_ROWS = 128


def _mod_fwd(cond_pad, w_mod, b_shard, name):
    nl, d, ncol = w_mod.shape
    tn = _tile(ncol, 512)

    def body(c_ref, w_ref, b_ref, o_ref):
        o_ref[0] = _dot(_bf(c_ref[...]), _bf(w_ref[0])) + b_ref[0]

    return pl.pallas_call(
        body, name=name, grid=(nl, ncol // tn),
        in_specs=[pl.BlockSpec((MOD_ROWS, d), lambda a, j: (0, 0)),
                  pl.BlockSpec((1, d, tn), lambda a, j: (a, 0, j)),
                  pl.BlockSpec((1, 1, tn), lambda a, j: (a, 0, j))],
        out_specs=pl.BlockSpec((1, MOD_ROWS, tn), lambda a, j: (a, 0, j)),
        out_shape=jax.ShapeDtypeStruct((nl, MOD_ROWS, ncol), F32),
        compiler_params=_params(("parallel", "parallel")),
    )(cond_pad, w_mod, b_shard)


def _mod_bwd(cond_pad_t, dmod_pad, name):
    nl, _, ncol = dmod_pad.shape
    d = cond_pad_t.shape[0]
    tn = _tile(ncol, 512)

    def body(c_ref, dm_ref, o_ref):
        o_ref[0] = _dot(_bf(c_ref[...]), _bf(dm_ref[0]))

    return pl.pallas_call(
        body, name=name, grid=(nl, ncol // tn),
        in_specs=[pl.BlockSpec((d, MOD_ROWS), lambda a, j: (0, 0)),
                  pl.BlockSpec((1, MOD_ROWS, tn), lambda a, j: (a, 0, j))],
        out_specs=pl.BlockSpec((1, d, tn), lambda a, j: (a, 0, j)),
        out_shape=jax.ShapeDtypeStruct((nl, d, ncol), F32),
        compiler_params=_params(("parallel", "parallel")),
    )(cond_pad_t, dmod_pad)


def _silu_rows(c2d, name):
    def body(c_ref, o_ref):
        o_ref[...] = _silu(c_ref[...])

    return pl.pallas_call(body, name=name, out_shape=jax.ShapeDtypeStruct(c2d.shape, F32))(c2d)


def _sum_leading(x, name):
    n, r, c = x.shape
    tr = _tile(r, max(16, (1 << 20) // (4 * c)), 16 if r % 16 == 0 else 8)

    def body(x_ref, o_ref):
        acc = x_ref[0].astype(F32)
        for k in range(1, n):
            acc = acc + x_ref[k].astype(F32)
        o_ref[...] = acc

    return pl.pallas_call(
        body, name=name, grid=(r // tr,),
        in_specs=[pl.BlockSpec((n, tr, c), lambda i: (0, i, 0))], out_specs=pl.BlockSpec((tr, c), lambda i: (i, 0)),
        out_shape=jax.ShapeDtypeStruct((r, c), F32), compiler_params=_params(("parallel",)),
    )(x)


def _adamw(w, gs, m, v, name, ride=None):
    r, c = w.shape
    tr = _tile(r, max(8, (3 << 19) // (4 * c)), 8)
    ng = len(gs)

    def body(*refs):
        w_ref, g_refs, m_ref, v_ref = refs[0], refs[1:1 + ng], refs[1 + ng], refs[2 + ng]
        g_ref, d_ref, nm_ref, nv_ref = refs[3 + ng:]
        g = g_refs[0][...]
        for extra in g_refs[1:]:
            g = g + extra[...]
        g_ref[...] = g
        d_ref[...], nm_ref[...], nv_ref[...] = _adamw_math(w_ref[...], g, m_ref[...], v_ref[...])

    spec = pl.BlockSpec((tr, c), lambda i: (i, 0))
    shp = jax.ShapeDtypeStruct((r, c), F32)
    outs = _call(body, name=name, grid=(r // tr,), in_specs=[spec] * (3 + ng), out_specs=[spec] * 4,
                 out_shape=[shp] * 4, args=(w, *gs, m, v), sem=("parallel",), ride=ride)
    return outs if ride is None else (outs[:4], outs[4:])


def _adamw_math(w, g, m, v):
    nm = ADAM_B1 * m + (1.0 - ADAM_B1) * g
    nv = ADAM_B2 * v + (1.0 - ADAM_B2) * (g * g)
    m_hat = nm / (1.0 - ADAM_B1 ** ADAM_STEP)
    v_hat = nv / (1.0 - ADAM_B2 ** ADAM_STEP)
    return -ADAM_LR * (m_hat / (jnp.sqrt(v_hat) + ADAM_EPS) + ADAM_WD * w), nm, nv


def _adamw_many(ws, gs, ms, vs, name):
    n = len(ws)

    def body(*refs):
        w_refs, g_refs, m_refs, v_refs = (refs[k * n:(k + 1) * n] for k in range(4))
        outs = refs[4 * n:]
        for i in range(n):
            outs[3 * i][...], outs[3 * i + 1][...], outs[3 * i + 2][...] = _adamw_math(
                w_refs[i][...], g_refs[i][...], m_refs[i][...], v_refs[i][...])

    out_shape = [jax.ShapeDtypeStruct(w.shape, F32) for w in ws for _ in range(3)]
    outs = pl.pallas_call(body, name=name, out_shape=out_shape, compiler_params=_params())(*ws, *gs, *ms, *vs)
    return [tuple(outs[3 * i:3 * i + 3]) for i in range(n)]


ANY = pl.BlockSpec(memory_space=pl.ANY)


def _flip(v, bit):
    return 1 - v if bit else v


def _allgather8_ops(x_ref, o_ref, send_sems, recv_sems, local_sem):
    mx, my, mc = lax.axis_index("x"), lax.axis_index("y"), lax.axis_index("c")
    me = 4 * mx + 2 * my + mc

    def mine():
        return pltpu.make_async_copy(x_ref, o_ref.at[me], local_sem)

    def copy(j, outgoing):
        peer = (_flip(mx, j & 4), _flip(my, j & 2), _flip(mc, j & 1))
        slot = me if outgoing else 4 * peer[0] + 2 * peer[1] + peer[2]
        return pltpu.make_async_remote_copy(
            src_ref=x_ref, dst_ref=o_ref.at[slot], send_sem=send_sems.at[j - 1], recv_sem=recv_sems.at[j - 1],
            device_id=peer, device_id_type=MESH)

    def start():
        mine().start()
        for j in range(1, 8):
            copy(j, True).start()

    def wait():
        for j in range(1, 8):
            copy(j, False).wait()
        mine().wait()

    return start, wait


def _ride_all8(x):
    return dict(xs=[x], shapes=[jax.ShapeDtypeStruct((8,) + x.shape, x.dtype)],
                sems=[pltpu.SemaphoreType.DMA((7,)), pltpu.SemaphoreType.DMA((7,)), pltpu.SemaphoreType.DMA],
                ops=lambda x_refs, o_refs, sems: _allgather8_ops(x_refs[0], o_refs[0], *sems))


def _ride_chip(xs, gather):
    return dict(xs=list(xs), shapes=_chip_exchange_shapes(xs, gather), sems=_chip_exchange_sems(len(xs)),
                ops=lambda x_refs, o_refs, sems: _chip_exchange_ops(x_refs, o_refs, *sems, gather))


def _allgather8(x, name):
    def body(x_ref, o_ref, *sems):
        start, wait = _allgather8_ops(x_ref, o_ref, *sems)
        start()
        wait()

    ride = _ride_all8(x)
    return pl.pallas_call(body, name=name, in_specs=[ANY], out_specs=ANY, out_shape=ride["shapes"][0],
                          scratch_shapes=ride["sems"])(x)


def _gather_halves_ops(x_ref, o_ref, ici_send, ici_recv, d2d_send, d2d_recv, local_sem):
    half = x_ref.shape[0] // 2
    quarter = half // 2
    mx, my, mc = lax.axis_index("x"), lax.axis_index("y"), lax.axis_index("c")
    k0, kx, ky, kd = 2 * mx + my, 2 * (1 - mx) + my, 2 * mx + (1 - my), 2 * (1 - mx) + (1 - my)
    x_nbr, y_nbr, sib = (1 - mx, my, mc), (mx, 1 - my, mc), (mx, my, 1 - mc)

    def rows(core, part):
        if part is None:
            return pl.ds(pl.multiple_of(core * half, 16), half)
        return pl.ds(pl.multiple_of(core * half + part * quarter, 16), quarter)

    def local():
        return pltpu.make_async_copy(x_ref, o_ref.at[k0], local_sem)

    def ici(n, outgoing):
        to = x_nbr if n in (0, 2) else y_nbr
        if n < 2:
            src = x_ref.at[rows(mc, None)]
            dst = o_ref.at[k0 if outgoing else (kx if n == 0 else ky), rows(mc, None)]
        else:
            part = n - 2
            src = o_ref.at[ky if n == 2 else kx, rows(mc, part)]
            dst = o_ref.at[(ky if n == 2 else kx) if outgoing else kd, rows(mc, part)]
        return pltpu.make_async_remote_copy(src_ref=src, dst_ref=dst, send_sem=ici_send.at[n], recv_sem=ici_recv.at[n],
                                            device_id=to, device_id_type=MESH)

    def d2d(n, outgoing):
        slot, part = ((kx, None), (ky, None), (kd, 0), (kd, 1))[n]
        piece = o_ref.at[slot, rows(mc if outgoing else 1 - mc, part)]
        return pltpu.make_async_remote_copy(src_ref=piece, dst_ref=piece, send_sem=d2d_send.at[n],
                                            recv_sem=d2d_recv.at[n], device_id=sib, device_id_type=MESH)

    def start():
        local().start()
        ici(0, True).start()
        ici(1, True).start()

    def wait():
        ici(1, False).wait_recv()
        ici(2, True).start()
        d2d(1, True).start()
        ici(0, False).wait_recv()
        ici(3, True).start()
        d2d(0, True).start()
        ici(2, False).wait_recv()
        d2d(2, True).start()
        ici(3, False).wait_recv()
        d2d(3, True).start()
        for n in range(4):
            ici(n, True).wait_send()
            d2d(n, True).wait_send()
            d2d(n, False).wait_recv()
        local().wait()

    return start, wait


def _ride_halves(x):
    dma4 = pltpu.SemaphoreType.DMA((4,))
    return dict(xs=[x], shapes=[jax.ShapeDtypeStruct((4,) + x.shape, x.dtype)],
                sems=[dma4, dma4, dma4, dma4, pltpu.SemaphoreType.DMA],
                ops=lambda x_refs, o_refs, sems: _gather_halves_ops(x_refs[0], o_refs[0], *sems))


def _chip_exchange(xs, gather, name):
    n = len(xs)

    def body(*refs):
        start, wait = _chip_exchange_ops(refs[:n], refs[n:2 * n], *refs[2 * n:], gather)
        start()
        wait()

    return pl.pallas_call(
        body, name=name, in_specs=[ANY] * n, out_specs=[ANY] * n, out_shape=_chip_exchange_shapes(xs, gather),
        scratch_shapes=_chip_exchange_sems(n),
    )(*xs)


def _chip_exchange_shapes(xs, gather):
    return [jax.ShapeDtypeStruct(((4,) + x.shape) if gather else x.shape, x.dtype) for x in xs]


def _chip_exchange_sems(n):
    return [pltpu.SemaphoreType.DMA((3 * n,)), pltpu.SemaphoreType.DMA((3 * n,)), pltpu.SemaphoreType.DMA((n,))]


def _chip_exchange_ops(x_refs, o_refs, send_sems, recv_sems, local_sems, gather):
    n = len(x_refs)
    mx, my, mc = lax.axis_index("x"), lax.axis_index("y"), lax.axis_index("c")
    k0 = 2 * mx + my

    def local(a):
        src = x_refs[a] if gather else x_refs[a].at[k0]
        return pltpu.make_async_copy(src, o_refs[a].at[k0], local_sems.at[a])

    def copy(a, j, outgoing):
        px, py = _flip(mx, j & 2), _flip(my, j & 1)
        kp = 2 * px + py
        if outgoing:
            src = x_refs[a] if gather else x_refs[a].at[kp]
            dst = o_refs[a].at[k0]
        else:
            src = x_refs[a] if gather else x_refs[a].at[k0]
            dst = o_refs[a].at[kp]
        s = a * 3 + j - 1
        return pltpu.make_async_remote_copy(
            src_ref=src, dst_ref=dst, send_sem=send_sems.at[s], recv_sem=recv_sems.at[s],
            device_id=(px, py, mc), device_id_type=MESH)

    def start():
        for a in range(n):
            local(a).start()
            for j in range(1, 4):
                copy(a, j, True).start()

    def wait():
        for a in range(n):
            for j in range(1, 4):
                copy(a, j, False).wait()
            local(a).wait()

    return start, wait


def _call(body, *, name, grid, in_specs, out_specs, out_shape, args, scratch_shapes=(), sem=None, ride=None):
    if not ride:
        return pl.pallas_call(
            body, name=name, grid=grid, in_specs=list(in_specs), out_specs=list(out_specs), out_shape=list(out_shape),
            scratch_shapes=list(scratch_shapes), compiler_params=_params(sem))(*args)
    xs = [x for r in ride for x in r["xs"]]
    shapes = [s for r in ride for s in r["shapes"]]
    sems = [s for r in ride for s in r["sems"]]
    n_in, n_out, n_scr, nx = len(in_specs), len(out_specs), len(scratch_shapes), len(xs)

    def wrapped(*refs):
        ins, x_refs = refs[:n_in], refs[n_in:n_in + nx]
        outs = refs[n_in + nx:n_in + nx + n_out]
        lands = refs[n_in + nx + n_out:n_in + 2 * nx + n_out]
        rest = refs[n_in + 2 * nx + n_out:]
        scr, sem_refs = rest[:n_scr], rest[n_scr:]
        ops, xo, so = [], 0, 0
        for r in ride:
            nr, ns = len(r["xs"]), len(r["sems"])
            ops.append(r["ops"](x_refs[xo:xo + nr], lands[xo:xo + nr], sem_refs[so:so + ns]))
            xo, so = xo + nr, so + ns
        ids = [pl.program_id(a) for a in range(len(grid))]
        first = functools.reduce(jnp.logical_and, [i == 0 for i in ids])
        last = functools.reduce(jnp.logical_and, [i == g - 1 for i, g in zip(ids, grid)])

        @pl.when(first)
        def _():
            for start, _ in ops:
                start()

        body(*ins, *outs, *scr)

        @pl.when(last)
        def _():
            for _, wait in ops:
                wait()

    return pl.pallas_call(
        wrapped, name=name, grid=grid, in_specs=list(in_specs) + [ANY] * nx, out_specs=list(out_specs) + [ANY] * nx,
        out_shape=list(out_shape) + shapes, scratch_shapes=list(scratch_shapes) + sems,
        compiler_params=_params(("arbitrary",) * len(grid)))(*args, *xs)


def _sibling_exchange(xs, name):
    ride = _ride_sibling(xs)
    n = len(xs)

    def body(*refs):
        start, wait = ride["ops"](refs[:n], refs[n:2 * n], refs[2 * n:])
        start()
        wait()

    return pl.pallas_call(body, name=name, in_specs=[ANY] * n, out_specs=[ANY] * n, out_shape=ride["shapes"],
                          scratch_shapes=ride["sems"])(*xs)


def _ride_sibling(xs):
    n = len(xs)

    def ops(x_refs, o_refs, sems):
        send_sems, recv_sems = sems
        sib = (lax.axis_index("x"), lax.axis_index("y"), 1 - lax.axis_index("c"))

        def copies():
            return [pltpu.make_async_remote_copy(
                src_ref=x_refs[a], dst_ref=o_refs[a], send_sem=send_sems.at[a], recv_sem=recv_sems.at[a],
                device_id=sib, device_id_type=MESH) for a in range(n)]

        def start():
            for cp in copies():
                cp.start()

        def wait():
            for cp in copies():
                cp.wait()

        return start, wait

    return dict(xs=list(xs), shapes=[jax.ShapeDtypeStruct(x.shape, x.dtype) for x in xs],
                sems=[pltpu.SemaphoreType.DMA((n,)), pltpu.SemaphoreType.DMA((n,))], ops=ops)


PACK = 1024
PACK_ROWS = 512


def _pack(parts):
    flat = []
    for p in parts:
        v = p.reshape(-1).astype(F32)
        flat.append(jnp.pad(v, (0, (-v.shape[0]) % PACK)))
    total = sum(v.shape[0] for v in flat)
    flat.append(jnp.zeros(((-total) % (PACK_ROWS * 128),), F32))
    return jnp.concatenate(flat).reshape(-1, 128)


def _shard_columns(shards, lo, hi):
    width = shards.shape[2]
    out = []
    for k in range(shards.shape[0]):
        a, b = max(lo, k * width), min(hi, (k + 1) * width)
        if a < b:
            out.append(shards[k, :, a - k * width:b - k * width])
    return out


def _unpack_rows(gathered, shapes):
    flat = gathered.reshape(gathered.shape[0], -1)
    out, off = [], 0
    for shp in shapes:
        n = math.prod(shp)
        out.append(flat[:, off:off + n].reshape((flat.shape[0],) + tuple(shp)))
        off += n + (-n) % PACK
    return out


def _unpack(packed, shapes):
    flat = packed.reshape(-1)
    out, off = [], 0
    for shp in shapes:
        n = math.prod(shp)
        out.append(flat[off:off + n].reshape(shp))
        off += n + (-n) % PACK
    return out


def kernel(x, c, ln_pre_g, ln_post_g, w_mod, b_mod, w_in_ab, w_out_ab, sgu_norm_g, sgu_w, sgu_b, w_in_ssm, w_out_ssm, lam_re, lam_im, b_re, b_im, c_re, c_im, d_skip, log_dt, w_glu, b_glu, loss_target, m_ln_pre_g, m_ln_post_g, m_w_mod, m_b_mod, m_w_in_ab, m_w_out_ab, m_sgu_norm_g, m_sgu_w, m_sgu_b, m_w_in_ssm, m_w_out_ssm, m_lam_re, m_lam_im, m_b_re, m_b_im, m_c_re, m_c_im, m_d_skip, m_log_dt, m_w_glu, m_b_glu, v_ln_pre_g, v_ln_post_g, v_w_mod, v_b_mod, v_w_in_ab, v_w_out_ab, v_sgu_norm_g, v_sgu_w, v_sgu_b, v_w_in_ssm, v_w_out_ssm, v_lam_re, v_lam_im, v_b_re, v_b_im, v_c_re, v_c_im, v_d_skip, v_log_dt, v_w_glu, v_b_glu):
    given = dict(locals())
    mx, my, mc = lax.axis_index("x"), lax.axis_index("y"), lax.axis_index("c")
    me = 4 * mx + 2 * my + mc
    chip = 2 * mx + my

    _, l, d = x.shape
    x2, tgt = x[0], loss_target[0]
    n_in = w_in_ab.shape[2] * 4
    wa = wb = n_in // 7
    w = w_out_ssm.shape[1]
    g, p, cch = b_re.shape[1:]
    nmod = w_mod.shape[2]


    cond = _silu_rows(c.reshape(d // 128, 128), "cond_silu")
    cond_all = _allgather8(cond, "gather_cond").reshape(8, d)
    b_shard = lax.dynamic_slice(b_mod, (0, chip * nmod), (2, nmod)).reshape(2, 1, nmod)
    cond_pad = jnp.pad(cond_all, ((0, MOD_ROWS - 8), (0, 0)))
    modp = _mod_fwd(cond_pad, w_mod, b_shard, "mod_fwd")[:, :8]
    modp_all = _allgather8(modp.reshape(16, nmod), "gather_mod").reshape(4, 2, 2, 8, nmod)
    mine = lax.dynamic_index_in_dim(lax.dynamic_index_in_dim(modp_all, mc, 1, False), me, 2, False)
    mod = mine.transpose(1, 0, 2).reshape(2, 3 * d)
    shift = [mod[a:a + 1, :d] for a in range(2)]
    scale = [mod[a:a + 1, d:2 * d] for a in range(2)]
    gate = [mod[a:a + 1, 2 * d:] for a in range(2)]
    pre_g = [ln_pre_g[a:a + 1] for a in range(2)]
    post_g = [ln_post_g[a:a + 1] for a in range(2)]

    sgu_w0, sgu_bt = sgu_w[0], sgu_b[0].T
    h0, (gw_in_ab,) = _pre_fwd(x2, pre_g[0], scale[0], shift[0], "pre0_fwd", ride=[_ride_halves(_bf(w_in_ab[0]))])
    w_gates = jnp.concatenate(_shard_columns(gw_in_ab, 0, 3 * wa) + _shard_columns(gw_in_ab, 3 * wa + 3 * wb, n_in),
                              axis=1)
    w_qkv = jnp.concatenate(_shard_columns(gw_in_ab, 3 * wa, 3 * wa + 3 * wb), axis=1)
    proj0, (gw_in_ssm,) = _matmul(h0, w_gates, "nn", F32, "proj0", tm=1024, ride=[_ride_chip([_bf(w_in_ssm[0])], True)])
    qkv, (gw_out_ssm, gw_glu, g_dskip, g_bglu) = _matmul(
        h0, w_qkv, "nn", BF16, "proj0_qkv", tm=1024,
        ride=[_ride_chip([_bf(w_out_ssm[0]), _bf(w_glu[0]), d_skip, b_glu], True)])
    out_b, (gw_out_ab,) = _attn_fwd(qkv, wb, "attn_fwd", hp=8, ride=[_ride_chip([_bf(w_out_ab[0])], True)])
    wout_ab = gw_out_ab.reshape(wa + wb, d)
    win_ssm = gw_in_ssm.reshape(d, 2 * w)
    wout_ssm = jnp.concatenate([gw_out_ssm[k] for k in range(4)], axis=1)
    wglu = gw_glu.reshape(w, w)
    dskip_full = g_dskip.reshape(1, w)
    bglu_full = g_bglu.reshape(1, w)
    cat =_sgu_fwd(proj0, out_b, sgu_norm_g, sgu_w0, sgu_bt, wa, wb, "sgu_fwd")
    y0 = _matmul(cat, wout_ab, "nn", BF16, "out0", tm=1024)
    x1, h1 = _post_pre_fwd(x2, y0, gate[0], post_g[0], pre_g[1], scale[1], shift[1], "post0_pre1_fwd")

    s = g * p
    lr_c, li_c = lam_re.reshape(s, 1), lam_im.reshape(s, 1)
    ldt_c = jnp.repeat(log_dt.reshape(g), p).reshape(s, 1)
    br_c, bi_c = b_re.reshape(s, cch), b_im.reshape(s, cch)
    bb_re, bb_im, pw_re, pw_im = _ssm_prep(lr_c, li_c, ldt_c, br_c, bi_c, lr_c.reshape(1, s), li_c.reshape(1, s),
                                           ldt_c.reshape(1, s), "ssm_prep")
    bbd = _bf(_block_diag_b(bb_re, bb_im, g, p, cch))
    ccd = _bf(_block_diag_c(c_re[0], c_im[0], g, p, cch))
    proj1 = _matmul(h1, win_ssm, "nn", F32, "proj1", tm=1024)
    y_ssm, hs_re, hs_im, h_all = _ssm_fwd(proj1, bbd, ccd, pw_re, pw_im, dskip_full, w, "ssm_fwd")
    o1 = _glu_fwd(y_ssm, proj1, wglu, bglu_full, "glu_fwd")
    y1 = _matmul(o1, wout_ssm, "nn", BF16, "out1", tm=1024)
    loss_vec, dy1, dx2, dgate1, dpost1 = _post_loss(x1, y1, gate[1], post_g[1], tgt, "post1_loss")

    do1 = _matmul(dy1, wout_ssm, "nt", BF16, "out1_dx", tm=1024)
    gr_wout_ssm = _matmul_at(o1, dy1, BF16, "out1_dw", n_split=4)
    dy_ssm, dz1, gr_wglu, gr_bglu = _glu_bwd(do1, y_ssm, proj1, wglu, bglu_full, "glu_bwd")
    du1, dbbd, dccd, da_re, da_im, gr_dskip, (ld_wout_ssm, ld_wglu) = _ssm_bwd(
        proj1, dy_ssm, hs_re, hs_im, h_all, bbd, ccd, pw_re, pw_im, dskip_full, w, "ssm_bwd",
        ride=[_ride_chip([gr_wout_ssm, gr_wglu.reshape(4, w // 4, w)], False)])
    dproj1 = jnp.concatenate([du1, dz1], axis=1)
    dh1 = _matmul(dproj1, win_ssm, "nt", BF16, "proj1_dx", tm=1024)
    gr_win_ssm = _matmul_at(h1, dproj1, BF16, "proj1_dw", tn=1024)
    dx1, dscale1, dshift1, dpre1, dy0, dgate0, dpost0 = _pre_bwd(
        dh1, dx2, x1, pre_g[1], scale[1], "pre1_post0_bwd", post=(y0, gate[0], post_g[0]))

    dcat = _matmul(dy0, wout_ab, "nt", BF16, "out0_dx", tm=1024)
    gr_wout_ab = _matmul_at(cat, dy0, BF16, "out0_dw", tn=1024)
    dbb_re, dbb_im = _diag_of_b(dbbd, g, p, cch)
    dc_re, dc_im = _diag_of_c(dccd, g, p, cch)
    part_a = [loss_vec[:, :1], dpre1, dpost0, dpost1, dgate0, dshift1, dscale1, dgate1, da_re, da_im,
              dbb_re, dbb_im, dc_re, dc_im, gr_dskip, gr_bglu]
    shapes_a = [a.shape for a in part_a]
    dq, dk, dv, (ld_win_ssm, ld_wout_ab, gath_a) = _attn_bwd(
        qkv, proj0, dcat, wa, wb, "attn_bwd", hp=4,
        ride=[_ride_chip([gr_win_ssm.reshape(4, d // 4, 2 * w), gr_wout_ab.reshape(4, (wa + wb) // 4, d)], False),
              _ride_all8(_pack(part_a))])
    early_names = ["w_out_ab", "w_in_ssm", "w_out_ssm", "w_glu"]
    early_sums = [_sum_leading(a, "sum_" + nm) for a, nm in zip([ld_wout_ab, ld_win_ssm, ld_wout_ssm, ld_wglu], early_names)]
    dproj0, gr_sgu_w, gr_sgu_bt, gr_sgu_g, early_sib = _sgu_bwd(
        proj0, out_b, dcat, dq, dk, dv, sgu_norm_g, sgu_w0, sgu_bt, wa, wb, "sgu_bwd", ride=[_ride_sibling(early_sums)])
    part_b = [gr_sgu_g, gr_sgu_w, gr_sgu_bt.T]
    shapes_b = [a.shape for a in part_b]
    gr_win_ab_lo, (gath_b,) = _matmul_at(h0, dproj0, BF16, "proj0_dw_lo", tm=512, tn=n_in // 4, n_split=4,
                                         m_part=(0, 1, 2), ride=[_ride_all8(_pack(part_b))])
    gr_win_ab_hi, (ld_win_ab_lo,) = _matmul_at(
        h0, dproj0, BF16, "proj0_dw_hi", tm=512, tn=n_in // 4, n_split=4, m_part=(1, 1, 2),
        ride=[_ride_chip([gr_win_ab_lo], False)])
    dh0, (ld_win_ab_hi,) = _matmul(dproj0, gw_in_ab, "nt", BF16, "proj0_dx", tm=1024, tn=1024,
                                   ride=[_ride_chip([gr_win_ab_hi], False)])
    grad_x, dscale0, dshift0, dpre0 = _pre_bwd(dh0, dx1, x2, pre_g[0], scale[0], "pre0_bwd")
    part_c = [dpre0, dshift0, dscale0]
    shapes_c = [a.shape for a in part_c]
    gath_c = _allgather8(_pack(part_c), "gather_small_tail")

    big_names = ["w_in_ab"] + early_names
    sum_win_ab = jnp.concatenate([_sum_leading(ld_win_ab_lo, "sum_w_in_ab_lo"),
                                  _sum_leading(ld_win_ab_hi, "sum_w_in_ab_hi")], axis=0)
    sums = [sum_win_ab] + early_sums
    sib = list(_sibling_exchange([sum_win_ab], "sibling_w_in_ab")) + list(early_sib)
    results = {}
    for nm, s_mine, s_sib in zip(big_names, sums, sib):
        shp = given[nm].shape
        two_d = lambda a: a.reshape(-1, shp[-1])
        outs = _adamw(two_d(given[nm]), [s_mine, s_sib], two_d(given["m_" + nm]), two_d(given["v_" + nm]),
                      "adamw_" + nm)
        results[nm] = [o.reshape(shp) for o in outs]

    (loss_s, g_pre1, g_post0, g_post1, g_gate0, g_shift1, g_scale1, g_gate1, s_da_re, s_da_im, s_dbb_re, s_dbb_im,
     g_c_re, g_c_im, g_dskip_full, g_bglu_full) = _unpack(_sum_leading(gath_a, "sum_small_a"), shapes_a)
    g_sgu_g, g_sgu_w, g_sgu_b = _unpack(_sum_leading(gath_b, "sum_small_b"), shapes_b)
    g_pre0, g_shift0, g_scale0 = _unpack(_sum_leading(gath_c, "sum_small_c"), shapes_c)
    loss = loss_s.reshape(())
    g_pre = jnp.concatenate([g_pre0, g_pre1], axis=0)
    g_post = jnp.concatenate([g_post0, g_post1], axis=0)
    g_bmod = jnp.concatenate([jnp.concatenate([g_shift0, g_scale0, g_gate0], axis=1),
                              jnp.concatenate([g_shift1, g_scale1, g_gate1], axis=1)], axis=0)

    g_lr, g_li, g_ldt, g_br, g_bi = _ssm_prep_bwd(lr_c, li_c, ldt_c, br_c, bi_c, s_da_re.reshape(s, 1),
                                                  s_da_im.reshape(s, 1), s_dbb_re, s_dbb_im, p, "ssm_prep_bwd")
    small = {
        "ln_pre_g": g_pre, "ln_post_g": g_post, "b_mod": g_bmod, "sgu_norm_g": g_sgu_g,
        "sgu_w": g_sgu_w.reshape(sgu_w.shape), "sgu_b": g_sgu_b.reshape(sgu_b.shape),
        "lam_re": g_lr.reshape(lam_re.shape), "lam_im": g_li.reshape(lam_im.shape),
        "b_re": g_br.reshape(b_re.shape), "b_im": g_bi.reshape(b_im.shape),
        "c_re": g_c_re.reshape(c_re.shape), "c_im": g_c_im.reshape(c_im.shape),
        "d_skip": lax.dynamic_slice(g_dskip_full, (0, chip * (w // 4)), (1, w // 4)),
        "log_dt": g_ldt.reshape(log_dt.shape),
        "b_glu": lax.dynamic_slice(g_bglu_full, (0, chip * (w // 4)), (1, w // 4)),
    }
    flat2 = lambda a: a.reshape(-1, a.shape[-1])
    wide = ("b_re", "b_im")
    for tag, group in (("adamw_small", [nm for nm in small if nm not in wide]), ("adamw_small_b", list(wide))):
        outs = _adamw_many([flat2(given[nm]) for nm in group], [flat2(small[nm]) for nm in group],
                           [flat2(given["m_" + nm]) for nm in group], [flat2(given["v_" + nm]) for nm in group], tag)
        for nm, trio in zip(group, outs):
            results[nm] = [small[nm]] + [o.reshape(given[nm].shape) for o in trio]

    rows_a = _unpack_rows(gath_a, shapes_a)
    rows_c = _unpack_rows(gath_c, shapes_c)
    dmod_rows = jnp.concatenate([rows_c[1], rows_c[2], rows_a[4], rows_a[5], rows_a[6], rows_a[7]],
                                axis=2).reshape(8, 2, 3 * d)
    dmod_shard = lax.dynamic_slice(dmod_rows, (0, 0, chip * nmod), (8, 2, nmod)).transpose(1, 0, 2)
    dmod_pad = jnp.pad(dmod_shard, ((0, 0), (0, MOD_ROWS - 8), (0, 0)))
    gr_wmod = _mod_bwd(cond_pad.T, dmod_pad, "mod_bwd")
    two_d = lambda a: a.reshape(-1, nmod)
    outs = _adamw(two_d(w_mod), [two_d(gr_wmod)], two_d(m_w_mod), two_d(v_w_mod), "adamw_w_mod")
    results["w_mod"] = [o.reshape(w_mod.shape) for o in outs]

    names = ["ln_pre_g", "ln_post_g", "w_mod", "b_mod", "w_in_ab", "w_out_ab", "sgu_norm_g", "sgu_w", "sgu_b",
             "w_in_ssm", "w_out_ssm", "lam_re", "lam_im", "b_re", "b_im", "c_re", "c_im", "d_skip", "log_dt",
             "w_glu", "b_glu"]
    return (loss, grad_x[None], *[results[nm][0] for nm in names], *[results[nm][1] for nm in names],
            *[results[nm][2] for nm in names], *[results[nm][3] for nm in names])
```

```python
import functools
import math

import jax
import jax.numpy as jnp
from jax import lax
from jax.experimental import pallas as pl
from jax.experimental.pallas import tpu as pltpu

F32 = jnp.float32
BF16 = jnp.bfloat16
MESH = pl.DeviceIdType.MESH

EPS = 1e-6
HEAD = 128
SSM_T = 512
SSM_GB = 16
ADAM_LR, ADAM_B1, ADAM_B2, ADAM_EPS, ADAM_WD, ADAM_STEP = 0.001, 0.9, 0.999, 1e-08, 0.01, 10
VMEM_LIMIT = 56 * 1024 * 1024

NN = (((1,), (0,)), ((), ()))
NT = (((1,), (1,)), ((), ()))
TN = (((0,), (0,)), ((), ()))


def _params(sem=None):
    return pltpu.CompilerParams(dimension_semantics=sem, vmem_limit_bytes=VMEM_LIMIT)


def _dot(a, b, dims=NN):
    return lax.dot_general(a, b, dims, preferred_element_type=F32)


def _bf(x):
    return x.astype(BF16)


def _gelu(x):
    k = math.sqrt(2.0 / math.pi)
    t = jnp.tanh(k * (x + 0.044715 * x * x * x))
    return 0.5 * x * (1.0 + t)


def _gelu_grad(x):
    k = math.sqrt(2.0 / math.pi)
    x2 = x * x
    t = jnp.tanh(k * (x + 0.044715 * x * x2))
    return 0.5 * (1.0 + t) + 0.5 * x * (1.0 - t * t) * k * (1.0 + 3.0 * 0.044715 * x2)


def _sigmoid(x):
    return 1.0 / (1.0 + jnp.exp(-x))


def _silu(x):
    return x * _sigmoid(x)


def _silu_grad(x):
    s = _sigmoid(x)
    return s * (1.0 + x * (1.0 - s))


def _tile(n, t, mult=128):
    if n <= t:
        return n
    for cand in range(t - t % mult, 0, -mult):
        if n % cand == 0:
            return cand
    raise ValueError((n, t, mult))


def _matmul(a, b, mode, out_dtype, name, tm=512, tn=512, tk=2048, n_split=1, ride=None, m_part=None):
    b_sharded = b.ndim == 3
    if mode == "nn":
        (m, kk), (_, n) = a.shape, b.shape
    elif b_sharded:
        assert mode == "nt"
        (m, kk), n, tk = a.shape, b.shape[1], b.shape[2]
    elif mode == "nt":
        (m, kk), (n, _) = a.shape, b.shape
    else:
        (kk, m), (_, n) = a.shape, b.shape
    m_off = 0
    if m_part is not None:
        assert mode == "tn"
        first, count, parts = m_part
        tm = _tile(m // parts, tm)
        m_off = first * (m // parts) // tm
        m = count * (m // parts)
    tm, tk = _tile(m, tm), _tile(kk, tk)
    ns = n // n_split
    tn = _tile(ns, tn)
    nk = kk // tk
    dims = {"nn": NN, "nt": NT, "tn": TN}[mode]

    def body(a_ref, b_ref, o_ref, acc_ref):
        k = pl.program_id(2)
        part = _dot(_bf(a_ref[...]), _bf(b_ref[0] if b_sharded else b_ref[...]), dims)

        @pl.when(k == 0)
        def _():
            acc_ref[...] = part

        @pl.when(k > 0)
        def _():
            acc_ref[...] += part

        @pl.when(k == nk - 1)
        def _():
            o_ref[...] = acc_ref[...].astype(out_dtype).reshape(o_ref.shape)

    if mode == "nn":
        a_spec = pl.BlockSpec((tm, tk), lambda i, j, k: (i, k))
        b_spec = pl.BlockSpec((tk, tn), lambda i, j, k: (k, j))
    elif mode == "nt":
        a_spec = pl.BlockSpec((tm, tk), lambda i, j, k: (i, k))
        b_spec = (pl.BlockSpec((1, tn, tk), lambda i, j, k: (k, j, 0)) if b_sharded
                  else pl.BlockSpec((tn, tk), lambda i, j, k: (j, k)))
    else:
        a_spec = pl.BlockSpec((tk, tm), lambda i, j, k: (k, i + m_off))
        b_spec = pl.BlockSpec((tk, tn), lambda i, j, k: (k, j))
    if n_split == 1:
        out_shape = jax.ShapeDtypeStruct((m, n), out_dtype)
        o_spec = pl.BlockSpec((tm, tn), lambda i, j, k: (i, j))
    else:
        per = ns // tn
        out_shape = jax.ShapeDtypeStruct((n_split, m, ns), out_dtype)
        o_spec = pl.BlockSpec((1, tm, tn), lambda i, j, k: (j // per, i, j % per))
    outs = _call(body, name=name, grid=(m // tm, n // tn, nk), in_specs=[a_spec, b_spec], out_specs=[o_spec],
                 out_shape=[out_shape], scratch_shapes=[pltpu.VMEM((tm, tn), F32)], args=(a, b),
                 sem=("parallel", "parallel", "arbitrary"), ride=ride)
    return outs[0] if ride is None else (outs[0], outs[1:])


def _matmul_at(a, b, out_dtype, name, tm=1024, tn=512, n_split=1, ride=None, m_part=None):
    (kk, m), (_, n) = a.shape, b.shape
    m_off = 0
    if m_part is not None:
        first, count, parts = m_part
        tm = _tile(m // parts, tm)
        m_off = first * (m // parts) // tm
        m = count * (m // parts)
    tm = _tile(m, tm)
    ns = n // n_split
    tn = _tile(ns, tn)
    kc = _tile(kk, 512)

    def body(a_ref, b_ref, o_ref, at_ref):
        @pl.when(pl.program_id(1) == 0)
        def _():
            for c in range(kk // kc):
                at_ref[:, c * kc:(c + 1) * kc] = _bf(a_ref[c * kc:(c + 1) * kc, :].astype(F32).T)

        o_ref[...] = _dot(at_ref[...], _bf(b_ref[...])).astype(out_dtype).reshape(o_ref.shape)

    if n_split == 1:
        out_shape = jax.ShapeDtypeStruct((m, n), out_dtype)
        o_spec = pl.BlockSpec((tm, tn), lambda i, j: (i, j))
    else:
        per = ns // tn
        out_shape = jax.ShapeDtypeStruct((n_split, m, ns), out_dtype)
        o_spec = pl.BlockSpec((1, tm, tn), lambda i, j: (j // per, i, j % per))
    outs = _call(body, name=name, grid=(m // tm, n // tn),
                 in_specs=[pl.BlockSpec((kk, tm), lambda i, j: (0, i + m_off)), pl.BlockSpec((kk, tn), lambda i, j: (0, j))],
                 out_specs=[o_spec], out_shape=[out_shape], scratch_shapes=[pltpu.VMEM((tm, kk), BF16)], args=(a, b),
                 sem=("arbitrary", "arbitrary"), ride=ride)
    return outs[0] if ride is None else (outs[0], outs[1:])


def _row_spec(tm, d):
    return pl.BlockSpec((tm, d), lambda i: (i, 0))


def _vec_spec(d):
    return pl.BlockSpec((1, d), lambda i: (0, 0))


def _acc(ref, first, val):
    @pl.when(first)
    def _():
        ref[...] = val

    @pl.when(jnp.logical_not(first))
    def _():
        ref[...] += val


def _colsum(x):
    return jnp.sum(x, axis=0, keepdims=True)


def _rownorm(x):
    r = lax.rsqrt(jnp.mean(x * x, axis=-1, keepdims=True) + EPS)
    return x * r, r


STRIP = 64


def _fold8(x):
    return functools.reduce(lambda a, b: a + b, [x[8 * k:8 * k + 8] for k in range(x.shape[0] // 8)])


def _pre_fwd(x, g, scale, shift, name, ride=None):
    l, d = x.shape
    tm = _tile(l, 256)

    def body(x_ref, g_ref, sc_ref, sh_ref, h_ref):
        n, _ = _rownorm(x_ref[...])
        h_ref[...] = _bf(n * g_ref[...] * (1.0 + sc_ref[...]) + sh_ref[...])

    outs = _call(body, name=name, grid=(l // tm,), in_specs=[_row_spec(tm, d), _vec_spec(d), _vec_spec(d), _vec_spec(d)],
                 out_specs=[_row_spec(tm, d)], out_shape=[jax.ShapeDtypeStruct((l, d), BF16)],
                 args=(x, g, scale, shift), sem=("parallel",), ride=ride)
    return outs[0], outs[1:]


def _post_pre_fwd(x, y, gate, pg, g1, scale1, shift1, name):
    l, d = x.shape
    tm = _tile(l, 256)

    def body(x_ref, y_ref, gate_ref, pg_ref, g1_ref, sc_ref, sh_ref, x1_ref, h1_ref):
        @pl.loop(0, tm // STRIP)
        def _(s):
            rows = pl.ds(pl.multiple_of(s * STRIP, STRIP), STRIP)
            ny, _ = _rownorm(y_ref[rows, :].astype(F32))
            x1 = x_ref[rows, :] + gate_ref[...] * (ny * pg_ref[...])
            x1_ref[rows, :] = x1
            n1, _ = _rownorm(x1)
            h1_ref[rows, :] = _bf(n1 * g1_ref[...] * (1.0 + sc_ref[...]) + sh_ref[...])

    v = _vec_spec(d)
    return pl.pallas_call(
        body, name=name, grid=(l // tm,),
        in_specs=[_row_spec(tm, d), _row_spec(tm, d), v, v, v, v, v],
        out_specs=[_row_spec(tm, d), _row_spec(tm, d)],
        out_shape=[jax.ShapeDtypeStruct((l, d), F32), jax.ShapeDtypeStruct((l, d), BF16)],
        compiler_params=_params(("parallel",)),
    )(x, y, gate, pg, g1, scale1, shift1)


def _post_loss(x1, y1, gate, pg, target, name):
    l, d = x1.shape
    tm = _tile(l, 256)

    def body(x_ref, y_ref, gate_ref, pg_ref, t_ref, loss_ref, dy_ref, dx_ref, dgate_ref, dpg_ref):
        first = pl.program_id(0) == 0

        def strip(s, sums):
            rows = pl.ds(pl.multiple_of(s * STRIP, STRIP), STRIP)
            ny, ry = _rownorm(y_ref[rows, :].astype(F32))
            q = ny * pg_ref[...]
            e = x_ref[rows, :] + gate_ref[...] * q - t_ref[rows, :]
            dx2 = e * (1.0 / d)
            dx_ref[rows, :] = dx2
            dq = dx2 * gate_ref[...]
            dny = dq * pg_ref[...]
            dy_ref[rows, :] = _bf(ry * (dny - ny * jnp.mean(dny * ny, axis=-1, keepdims=True)))
            return sums[0] + _fold8(e * e), sums[1] + _fold8(dx2 * q), sums[2] + _fold8(dq * ny)

        zero = jnp.zeros((8, d), F32)
        sq, dgate, dpg = lax.fori_loop(0, tm // STRIP, strip, (zero, zero, zero))
        _acc(loss_ref, first, jnp.full((1, 128), 0.5 / d, F32) * jnp.sum(sq))
        _acc(dgate_ref, first, _colsum(dgate))
        _acc(dpg_ref, first, _colsum(dpg))

    v = _vec_spec(d)
    return pl.pallas_call(
        body, name=name, grid=(l // tm,),
        in_specs=[_row_spec(tm, d), _row_spec(tm, d), v, v, _row_spec(tm, d)],
        out_specs=[_vec_spec(128), _row_spec(tm, d), _row_spec(tm, d), v, v],
        out_shape=[jax.ShapeDtypeStruct((1, 128), F32), jax.ShapeDtypeStruct((l, d), BF16),
                   jax.ShapeDtypeStruct((l, d), F32), jax.ShapeDtypeStruct((1, d), F32),
                   jax.ShapeDtypeStruct((1, d), F32)],
        compiler_params=_params(("arbitrary",)),
    )(x1, y1, gate, pg, target)


def _pre_bwd(dh, dres, x, g, scale, name, post=None):
    l, d = x.shape
    tm = _tile(l, 256)
    with_post = post is not None

    def body(*refs):
        if with_post:
            (dh_ref, dres_ref, x_ref, g_ref, sc_ref, y_ref, gate_ref, pg_ref,
             dx_ref, dsc_ref, dsh_ref, dg_ref, dy_ref, dgate_ref, dpg_ref) = refs
        else:
            dh_ref, dres_ref, x_ref, g_ref, sc_ref, dx_ref, dsc_ref, dsh_ref, dg_ref = refs
        first = pl.program_id(0) == 0

        def strip(s, sums):
            rows = pl.ds(pl.multiple_of(s * STRIP, STRIP), STRIP)
            dh = dh_ref[rows, :].astype(F32)
            n, r = _rownorm(x_ref[rows, :])
            dyn = dh * (1.0 + sc_ref[...])
            dn = dyn * g_ref[...]
            dx = dres_ref[rows, :] + r * (dn - n * jnp.mean(dn * n, axis=-1, keepdims=True))
            dx_ref[rows, :] = dx
            new = [sums[0] + _fold8(dh * (n * g_ref[...])), sums[1] + _fold8(dh), sums[2] + _fold8(dyn * n)]
            if with_post:
                ny, ry = _rownorm(y_ref[rows, :].astype(F32))
                dq = dx * gate_ref[...]
                dny = dq * pg_ref[...]
                dy_ref[rows, :] = _bf(ry * (dny - ny * jnp.mean(dny * ny, axis=-1, keepdims=True)))
                new += [sums[3] + _fold8(dx * (ny * pg_ref[...])), sums[4] + _fold8(dq * ny)]
            return tuple(new)

        zero = jnp.zeros((8, d), F32)
        sums = lax.fori_loop(0, tm // STRIP, strip, (zero,) * (5 if with_post else 3))
        outs = [dsc_ref, dsh_ref, dg_ref] + ([dgate_ref, dpg_ref] if with_post else [])
        for ref, acc in zip(outs, sums):
            _acc(ref, first, _colsum(acc))

    v = _vec_spec(d)
    row = _row_spec(tm, d)
    vec_out = jax.ShapeDtypeStruct((1, d), F32)
    in_specs = [row, row, row, v, v]
    args = [dh, dres, x, g, scale]
    out_specs = [row, v, v, v]
    out_shape = [jax.ShapeDtypeStruct((l, d), F32), vec_out, vec_out, vec_out]
    if with_post:
        in_specs += [row, v, v]
        args += list(post)
        out_specs += [row, v, v]
        out_shape += [jax.ShapeDtypeStruct((l, d), BF16), vec_out, vec_out]
    return pl.pallas_call(
        body, name=name, grid=(l // tm,), in_specs=in_specs, out_specs=out_specs, out_shape=out_shape,
        compiler_params=_params(("arbitrary",)),
    )(*args)


def _softplus_parts(z):
    e = jnp.exp(-jnp.abs(z))
    den = 1.0 + e
    lb = jnp.minimum(z, 0.0) - jnp.log(den)
    return lb, lb - z, jnp.exp(lb)


def _tri(cmp, n=HEAD):
    row = lax.broadcasted_iota(jnp.int32, (n, n), 0)
    col = lax.broadcasted_iota(jnp.int32, (n, n), 1)
    return cmp(row, col)


ATT_T = 256
ATT_DEAD = 104.0


def _any_alive(runs):
    return functools.reduce(jnp.maximum, [jnp.max(r) for r in runs]) > -ATT_DEAD


def _attn_fwd(qkv, wb, name, hp=4, ride=None):
    l = qkv.shape[0]
    t = ATT_T
    nh, nq = wb // HEAD, l // t
    hp = min(hp, nh)
    ng, wg = nh // hp, hp * HEAD
    scale = 1.0 / math.sqrt(HEAD)

    def body(q_ref, k_ref, v_ref, o_ref):
        i = pl.program_id(1)
        valid = _tri(lambda r, c: c < r, t)
        m_gt = _bf(_tri(lambda r, c: r > c, t).astype(F32))

        def tile(j, carry, diag):
            rows = pl.ds(pl.multiple_of(j * t, t), t)
            cols = [slice(hh * HEAD, (hh + 1) * HEAD) for hh in range(hp)]
            zs = [_dot(q_ref[:, cs], k_ref[rows, cs], NT) * scale for cs in cols]
            lbs, lks = [], []
            for z in zs:
                lb, lk, _ = _softplus_parts(z)
                lbs.append(lb)
                lks.append(jnp.where(valid, lk, 0.0) if diag else lk)
            laters = [_dot(_bf(lk), m_gt) for lk in lks]
            ws = [jnp.exp(lb + later + run) for lb, later, (_, run) in zip(lbs, laters, carry)]
            if diag:
                ws = [jnp.where(valid, w, 0.0) for w in ws]
            return tuple((acc + _dot(_bf(w), v_ref[rows, cs]), run + jnp.sum(lk, axis=1, keepdims=True))
                         for w, lk, cs, (acc, run) in zip(ws, lks, cols, carry))

        zero = (jnp.zeros((t, HEAD), F32), jnp.zeros((t, 1), F32))
        carry = tile(i, (zero,) * hp, True)
        _, carry = lax.while_loop(lambda c: (c[0] < i) & _any_alive([run for _, run in c[1]]),
                                  lambda c: (c[0] + 1, tile(i - 1 - c[0], c[1], False)), (jnp.int32(0), carry))
        for hh, (acc, _) in enumerate(carry):
            o_ref[:, hh * HEAD:(hh + 1) * HEAD] = acc

    blk = lambda off: pl.BlockSpec((t, wg), lambda h, i: (i, off + h))
    full = lambda off: pl.BlockSpec((l, wg), lambda h, i: (0, off + h))
    out = pl.BlockSpec((t, wg), lambda h, i: (i, h))
    outs = _call(body, name=name, grid=(ng, nq), in_specs=[blk(0), full(ng), full(2 * ng)], out_specs=[out],
                 out_shape=[jax.ShapeDtypeStruct((l, wb), F32)],
                 args=(qkv, qkv, qkv), sem=("parallel", "arbitrary"), ride=ride)
    return outs[0], outs[1:]


def _attn_bwd(qkv, proj, dcat, wa, wb, name, hp=2, ride=None):
    l = qkv.shape[0]
    t = ATT_T
    nh, nq = wb // HEAD, l // t
    hp = min(hp, nh)
    ng, wg = nh // hp, hp * HEAD
    scale = 1.0 / math.sqrt(HEAD)

    def body(q_ref, k_ref, v_ref, bz_ref, dc_ref, dq_ref, dkt_out, dvt_out, do_s, qt_s, dot_s,
             dkt_ref, dvt_ref, out_sems):
        i = pl.program_id(1)

        @pl.when(i == 0)
        def _():
            dkt_ref[...] = jnp.zeros_like(dkt_ref)
            dvt_ref[...] = jnp.zeros_like(dvt_ref)

        do = dc_ref[...].astype(F32) * _silu(bz_ref[...].astype(F32))
        do_s[...] = _bf(do)
        for hh in range(hp):
            cs = slice(hh * HEAD, (hh + 1) * HEAD)
            qt_s[hh] = _bf(q_ref[:, cs].astype(F32).T * scale)
            dot_s[hh] = _bf(do[:, cs].T)
        valid = _tri(lambda r, c: c < r, t)
        m_le = _bf(_tri(lambda r, c: r <= c, t).astype(F32))
        m_lt = _bf(_tri(lambda r, c: r < c, t).astype(F32))

        heads = range(hp)
        cols = [slice(hh * HEAD, (hh + 1) * HEAD) for hh in heads]

        def row_sums(j, runs, diag):
            rows = pl.ds(pl.multiple_of(j * t, t), t)
            out = []
            for cs, run in zip(cols, runs):
                _, lk, _ = _softplus_parts(_dot(q_ref[:, cs], k_ref[rows, cs], NT) * scale)
                if diag:
                    lk = jnp.where(valid, lk, 0.0)
                out.append(run + jnp.sum(lk, axis=1, keepdims=True))
            return tuple(out)

        runs = row_sums(i, (jnp.zeros((t, 1), F32),) * hp, True)
        below, lktot = lax.while_loop(lambda c: (c[0] < i) & _any_alive(c[1]),
                                      lambda c: (c[0] + 1, row_sums(i - 1 - c[0], c[1], False)), (jnp.int32(0), runs))

        def tile(j, carry, diag):
            rows = pl.ds(pl.multiple_of(j * t, t), t)
            zs = [_dot(q_ref[:, cs], k_ref[rows, cs], NT) * scale for cs in cols]
            dws = [_dot(do_s[:, cs], v_ref[rows, cs], NT) for cs in cols]
            lbs, lks, sigs = [], [], []
            for z in zs:
                lb, lk, sig = _softplus_parts(z)
                lbs.append(lb)
                lks.append(jnp.where(valid, lk, 0.0) if diag else lk)
                sigs.append(sig)
            pins = [_dot(_bf(lk), m_le) for lk in lks]
            ws = [jnp.exp(lbs[hh] + (lktot[hh] - carry[hh][1]) - pins[hh]) for hh in heads]
            if diag:
                ws = [jnp.where(valid, w, 0.0) for w in ws]
            das = [dw * w for dw, w in zip(dws, ws)]
            pexs = [_dot(_bf(da), m_lt) for da in das]
            dzs = [das[hh] - sigs[hh] * (das[hh] + carry[hh][2] + pexs[hh]) for hh in heads]
            if diag:
                dzs = [jnp.where(valid, dz, 0.0) for dz in dzs]
            dzs = [_bf(dz) for dz in dzs]
            out = []
            for hh in heads:
                dkt, dvt = _dot(qt_s[hh], dzs[hh]), _dot(dot_s[hh], _bf(ws[hh]))
                for half in range(t // HEAD):
                    dkt_ref[hh, sub * j + half] += dkt[:, half * HEAD:(half + 1) * HEAD]
                    dvt_ref[hh, sub * j + half] += dvt[:, half * HEAD:(half + 1) * HEAD]
                dq, cpre, ppre = carry[hh]
                out.append((dq + _dot(dzs[hh], k_ref[rows, cols[hh]]), cpre + jnp.sum(lks[hh], axis=1, keepdims=True),
                            ppre + pexs[hh][:, t - 1:] + das[hh][:, t - 1:]))
            return tuple(out)

        zero = (jnp.zeros((t, HEAD), F32), jnp.zeros((t, 1), F32), jnp.zeros((t, 1), F32))
        carry = lax.fori_loop(i - below, i, lambda j, c: tile(j, c, False), (zero,) * hp)
        carry = tile(i, carry, True)
        for hh in range(hp):
            dq_ref[:, hh * HEAD:(hh + 1) * HEAD] = carry[hh][0] * scale

        @pl.when(i == nq - 1)
        def _():
            heads = pl.ds(pl.program_id(0) * hp, hp)
            copies = [pltpu.make_async_copy(dkt_ref, dkt_out.at[heads], out_sems.at[0]),
                      pltpu.make_async_copy(dvt_ref, dvt_out.at[heads], out_sems.at[1])]
            for cp in copies:
                cp.start()
            for cp in copies:
                cp.wait()

    sub = t // HEAD
    blk = lambda off: pl.BlockSpec((t, wg), lambda h, i: (i, off + h))
    full = lambda off: pl.BlockSpec((l, wg), lambda h, i: (0, off + h))
    acc_shape = jax.ShapeDtypeStruct((nh, l // HEAD, HEAD, HEAD), F32)
    acc_scratch = pltpu.VMEM((hp, l // HEAD, HEAD, HEAD), F32)
    outs = _call(
        body, name=name, grid=(ng, nq),
        in_specs=[blk(0), full(ng), full(2 * ng), blk(3 * wa // wg), blk(wa // wg)],
        out_specs=[blk(0), ANY, ANY], out_shape=[jax.ShapeDtypeStruct((l, wb), F32), acc_shape, acc_shape],
        scratch_shapes=[pltpu.VMEM((t, wg), BF16), pltpu.VMEM((hp, HEAD, t), BF16), pltpu.VMEM((hp, HEAD, t), BF16),
                        acc_scratch, acc_scratch, pltpu.SemaphoreType.DMA((2,))],
        args=(qkv, qkv, qkv, proj, dcat), sem=("parallel", "arbitrary"), ride=ride)
    return outs[0], outs[1], outs[2], outs[3:]


def _sgu_heads(v, g_ref, w_ref, bt_ref, nh):
    keep = _tri(lambda r, c: r >= c)
    out = []
    for h in range(nh):
        cols = slice(h * HEAD, (h + 1) * HEAD)
        nv, r = _rownorm(v[:, cols])
        wm = jnp.where(keep, w_ref[h], 0.0)
        s = _dot(_bf(wm), _bf(nv * g_ref[:, cols])) + bt_ref[:, h:h + 1]
        out.append((nv, r, wm, s))
    return out


def _sgu_fwd(proj, out_b, norm_g, sgu_w, sgu_bt, wa, wb, name):
    l, n = proj.shape
    nh = wa // HEAD

    def body(au_ref, av_ref, az_ref, bz_ref, ob_ref, g_ref, w_ref, bt_ref, cat_ref):
        u, v, sz = _gelu(au_ref[...].astype(F32)), _gelu(av_ref[...].astype(F32)), _silu(az_ref[...].astype(F32))
        for h, (_, _, _, s) in enumerate(_sgu_heads(v, g_ref, w_ref, bt_ref, nh)):
            cols = slice(h * HEAD, (h + 1) * HEAD)
            cat_ref[:, cols] = _bf(u[:, cols] * s * sz[:, cols])
        cat_ref[:, wa:] = _bf(ob_ref[...] * _silu(bz_ref[...].astype(F32)))

    a_blk = lambda j: pl.BlockSpec((HEAD, wa), lambda i: (i, j))
    return pl.pallas_call(
        body, name=name, grid=(l // HEAD,),
        in_specs=[a_blk(0), a_blk(1), a_blk(2), a_blk(3), pl.BlockSpec((HEAD, wb), lambda i: (i, 0)),
                  _vec_spec(wa), pl.BlockSpec((nh, HEAD, HEAD), lambda i: (0, 0, 0)),
                  pl.BlockSpec((HEAD, nh), lambda i: (0, 0))],
        out_specs=pl.BlockSpec((HEAD, wa + wb), lambda i: (i, 0)),
        out_shape=jax.ShapeDtypeStruct((l, wa + wb), BF16),
        compiler_params=_params(("parallel",)),
    )(proj, proj, proj, proj, out_b, norm_g, sgu_w, sgu_bt)


def _sgu_bwd(proj, out_b, dcat, dq, dk, dv, norm_g, sgu_w, sgu_bt, wa, wb, name, ride=None):
    l = proj.shape[0]
    n = 3 * wa + 4 * wb
    nh = wa // HEAD

    def body(au_ref, av_ref, az_ref, bz_ref, ob_ref, dc_ref, dq_ref, dk_ref, dv_ref, g_ref, w_ref, wt_ref, bt_ref,
             dp_ref, dw_ref, dbt_ref, dg_ref):
        first = pl.program_id(0) == 0
        keep = _tri(lambda r, c: r >= c)
        au, av, az = au_ref[...].astype(F32), av_ref[...].astype(F32), az_ref[...].astype(F32)
        u, v, sz = _gelu(au), _gelu(av), _silu(az)
        dgelu_u, dgelu_v, dsilu_z = _gelu_grad(au), _gelu_grad(av), _silu_grad(az)
        heads = _sgu_heads(v, g_ref, w_ref, bt_ref, nh)
        cols = [slice(h * HEAD, (h + 1) * HEAD) for h in range(nh)]
        dss = []
        for h, (nv, r, wm, s) in enumerate(heads):
            dca, uh, szh = dc_ref[:, cols[h]].astype(F32), u[:, cols[h]], sz[:, cols[h]]
            dp_ref[:, cols[h]] = _bf(dca * s * szh * dgelu_u[:, cols[h]])
            dp_ref[:, 2 * wa + h * HEAD:2 * wa + (h + 1) * HEAD] = _bf(dca * uh * s * dsilu_z[:, cols[h]])
            dss.append(dca * uh * szh)
        dws = [_dot(_bf(ds), _bf(nv * g_ref[:, cs]), NT) for ds, cs, (nv, _, _, _) in zip(dss, cols, heads)]
        keep_t = _tri(lambda r, c: r <= c)
        dvhs = [_dot(_bf(jnp.where(keep_t, wt_ref[h], 0.0)), _bf(dss[h])) for h in range(nh)]
        dg_parts = []
        for h, (nv, r, wm, s) in enumerate(heads):
            _acc(dw_ref.at[h], first, jnp.where(keep, dws[h], 0.0))
            _acc(dbt_ref.at[:, h:h + 1], first, jnp.sum(dss[h], axis=1, keepdims=True))
            dg_parts.append(_colsum(dvhs[h] * nv))
            dnv = dvhs[h] * g_ref[:, cols[h]]
            dvv = r * (dnv - nv * jnp.mean(dnv * nv, axis=-1, keepdims=True))
            dp_ref[:, wa + h * HEAD:wa + (h + 1) * HEAD] = _bf(dvv * dgelu_v[:, cols[h]])
        _acc(dg_ref, first, jnp.concatenate(dg_parts, axis=1))
        base = 3 * wa
        dp_ref[:, base:base + wb] = _bf(dq_ref[...])
        for h in range(wb // HEAD):
            dp_ref[:, base + wb + h * HEAD:base + wb + (h + 1) * HEAD] = _bf(dk_ref[h, 0].T)
            dp_ref[:, base + 2 * wb + h * HEAD:base + 2 * wb + (h + 1) * HEAD] = _bf(dv_ref[h, 0].T)
        dp_ref[:, base + 3 * wb:] = _bf(dc_ref[:, wa:].astype(F32) * ob_ref[...]
                                        * _silu_grad(bz_ref[...].astype(F32)))

    a_blk = lambda j: pl.BlockSpec((HEAD, wa), lambda i: (i, j))
    b_blk = pl.BlockSpec((HEAD, wb), lambda i: (i, 0))
    t_blk = pl.BlockSpec((wb // HEAD, 1, HEAD, HEAD), lambda i: (0, i, 0, 0))
    w_spec = pl.BlockSpec((nh, HEAD, HEAD), lambda i: (0, 0, 0))
    bt_spec = pl.BlockSpec((HEAD, nh), lambda i: (0, 0))
    outs = _call(
        body, name=name, grid=(l // HEAD,), ride=ride, sem=("arbitrary",),
        in_specs=[a_blk(0), a_blk(1), a_blk(2), a_blk(3), b_blk, pl.BlockSpec((HEAD, wa + wb), lambda i: (i, 0)),
                  b_blk, t_blk, t_blk, _vec_spec(wa), w_spec, w_spec, bt_spec],
        out_specs=[pl.BlockSpec((HEAD, n), lambda i: (i, 0)), w_spec, bt_spec, _vec_spec(wa)],
        out_shape=[jax.ShapeDtypeStruct((l, n), BF16), jax.ShapeDtypeStruct((nh, HEAD, HEAD), F32),
                   jax.ShapeDtypeStruct((HEAD, nh), F32), jax.ShapeDtypeStruct((1, wa), F32)],
        args=(proj, proj, proj, proj, out_b, dcat, dq, dk, dv, norm_g, sgu_w, sgu_w.transpose(0, 2, 1), sgu_bt))
    return (*outs[:4], outs[4:])


def _ssm_discretise(lr, li, ldt, br, bi):
    dt = jnp.exp(ldt)
    mag = jnp.exp(lr * dt)
    a_re = mag * jnp.cos(li * dt)
    a_im = mag * jnp.sin(li * dt)
    den = lr * lr + li * li
    nr = a_re - 1.0
    coef_re = (nr * lr + a_im * li) / den
    coef_im = (a_im * lr - nr * li) / den
    return a_re, a_im, coef_re * br - coef_im * bi, coef_re * bi + coef_im * br


def _ssm_prep(lr, li, ldt, br, bi, lr_row, li_row, ldt_row, name):
    s, c = br.shape

    def body(lr_ref, li_ref, ldt_ref, br_ref, bi_ref, lrr_ref, lir_ref, ldtr_ref, bbr_ref, bbi_ref, tr_ref, ti_ref):
        _, _, bbr, bbi = _ssm_discretise(lr_ref[...], li_ref[...], ldt_ref[...], br_ref[...], bi_ref[...])
        bbr_ref[...] = bbr
        bbi_ref[...] = bbi
        row = lax.broadcasted_iota(jnp.int32, (SCAN_ROWS, 1), 0)
        blk, r = jnp.right_shift(row, 3), jnp.bitwise_and(row, 7)
        kind, rev = jnp.bitwise_and(blk, 3), blk >= 4
        step = jnp.left_shift(1, kind)
        n = jnp.where(kind < 3, step, jnp.where(rev, 8 - r, r + 1)).astype(F32)
        keep = (kind == 3) | (rev & (r < 8 - step)) | (jnp.logical_not(rev) & (r >= step))
        dt = jnp.exp(ldtr_ref[...])
        mag = jnp.exp(n * (lrr_ref[...] * dt))
        ang = n * (lir_ref[...] * dt)
        tr_ref[...] = jnp.where(keep, mag * jnp.cos(ang), 0.0)
        ti_ref[...] = jnp.where(keep, jnp.where(rev, -1.0, 1.0) * mag * jnp.sin(ang), 0.0)

    col = jax.ShapeDtypeStruct((s, c), F32)
    row = jax.ShapeDtypeStruct((SCAN_ROWS, s), F32)
    return pl.pallas_call(body, name=name, out_shape=[col, col, row, row])(
        lr, li, ldt, br, bi, lr_row, li_row, ldt_row)


def _ssm_prep_bwd(lr, li, ldt, br, bi, da_re, da_im, dbb_re, dbb_im, p, name):
    s, c = br.shape

    def body(lr_ref, li_ref, ldt_ref, br_ref, bi_ref, dar_ref, dai_ref, dbr_ref, dbi_ref,
             dlr_ref, dli_ref, dldt_ref, dbre_ref, dbim_ref):
        args = (lr_ref[...], li_ref[...], ldt_ref[...], br_ref[...], bi_ref[...])
        _, vjp = jax.vjp(_ssm_discretise, *args)
        dlr, dli, dldt, dbr, dbi = vjp((dar_ref[...], dai_ref[...], dbr_ref[...], dbi_ref[...]))
        dlr_ref[...] = dlr
        dli_ref[...] = dli
        dbre_ref[...] = dbr
        dbim_ref[...] = dbi
        idx = lax.broadcasted_iota(jnp.int32, (s, s // p), 0)
        grp = lax.broadcasted_iota(jnp.int32, (s, s // p), 1)
        own = (idx >= grp * p) & (idx < (grp + 1) * p)
        dldt_ref[...] = _colsum(jnp.where(own, dldt, 0.0))

    col1 = jax.ShapeDtypeStruct((s, 1), F32)
    colc = jax.ShapeDtypeStruct((s, c), F32)
    return pl.pallas_call(
        body, name=name, out_shape=[col1, col1, jax.ShapeDtypeStruct((1, s // p), F32), colc, colc],
    )(lr, li, ldt, br, bi, da_re, da_im, dbb_re, dbb_im)


SCAN_ROWS = 64


def _scan_groups(xr, xi, tr_ref, ti_ref, cr, ci, reverse):
    ng = xr.shape[0] // 8
    base = SCAN_ROWS // 2 if reverse else 0
    pr, pi = tr_ref[base + 24:base + 32, :], ti_ref[base + 24:base + 32, :]
    edge = slice(0, 1) if reverse else slice(7, 8)
    out_r, out_i = [None] * ng, [None] * ng
    for g in (range(ng - 1, -1, -1) if reverse else range(ng)):
        sr, si = xr[8 * g:8 * g + 8, :], xi[8 * g:8 * g + 8, :]
        for k in range(3):
            ar, ai = tr_ref[base + 8 * k:base + 8 * k + 8, :], ti_ref[base + 8 * k:base + 8 * k + 8, :]
            shift = 8 - (1 << k) if reverse else 1 << k
            rr, ri = pltpu.roll(sr, shift, 0), pltpu.roll(si, shift, 0)
            sr, si = sr + ar * rr - ai * ri, si + ar * ri + ai * rr
        sr, si = sr + pr * cr - pi * ci, si + pr * ci + pi * cr
        cr, ci = sr[edge, :], si[edge, :]
        out_r[g], out_i[g] = sr, si
    return jnp.concatenate(out_r, axis=0), jnp.concatenate(out_i, axis=0), cr, ci


def _ssm_fwd(proj, bbd, ccd, pw_re, pw_im, d_skip, w, name):
    l = proj.shape[0]
    nb, cw, ns2 = bbd.shape
    ns = ns2 // 2
    nc = l // SSM_T

    def body(u_ref, bbd_ref, ccd_ref, pr_ref, pi_ref, d_ref, y_ref, hsr_ref, hsi_ref, h_ref, hr_s, hi_s):
        @pl.when(pl.program_id(1) == 0)
        def _():
            hr_s[...] = jnp.zeros_like(hr_s)
            hi_s[...] = jnp.zeros_like(hi_s)

        hsr_ref[...] = hr_s[...].reshape(hsr_ref.shape)
        hsi_ref[...] = hi_s[...].reshape(hsi_ref.shape)
        u = u_ref[...]
        bu = _dot(_bf(u), bbd_ref[0])
        hr, hi, cr, ci = _scan_groups(bu[:, :ns], bu[:, ns:], pr_ref, pi_ref, hr_s[...], hi_s[...], False)
        hr_s[...] = cr
        hi_s[...] = ci
        h_bf = _bf(jnp.concatenate([hr, hi], axis=1))
        h_ref[...] = h_bf
        y_ref[...] = _dot(h_bf, ccd_ref[0]) + d_ref[...] * u

    tab = pl.BlockSpec((SCAN_ROWS, ns), lambda b, k: (0, b))
    return pl.pallas_call(
        body, name=name, grid=(nb, nc),
        in_specs=[pl.BlockSpec((SSM_T, cw), lambda b, k: (k, b)),
                  pl.BlockSpec((1, cw, ns2), lambda b, k: (b, 0, 0)),
                  pl.BlockSpec((1, ns2, cw), lambda b, k: (b, 0, 0)),
                  tab, tab, pl.BlockSpec((1, cw), lambda b, k: (0, b))],
        out_specs=[pl.BlockSpec((SSM_T, cw), lambda b, k: (k, b)),
                   pl.BlockSpec((1, 1, ns), lambda b, k: (k, 0, b)), pl.BlockSpec((1, 1, ns), lambda b, k: (k, 0, b)),
                   pl.BlockSpec((SSM_T, ns2), lambda b, k: (k, b))],
        out_shape=[jax.ShapeDtypeStruct((l, w), F32), jax.ShapeDtypeStruct((nc, 1, nb * ns), F32),
                   jax.ShapeDtypeStruct((nc, 1, nb * ns), F32), jax.ShapeDtypeStruct((l, nb * ns2), BF16)],
        scratch_shapes=[pltpu.VMEM((1, ns), F32), pltpu.VMEM((1, ns), F32)],
        compiler_params=_params(("parallel", "arbitrary")),
    )(proj, bbd, ccd, pw_re, pw_im, d_skip)


def _ssm_bwd(proj, dy, hs_re, hs_im, h_all, bbd, ccd, pw_re, pw_im, d_skip, w, name, ride=None):
    l = proj.shape[0]
    nb, cw, ns2 = bbd.shape
    ns = ns2 // 2
    nc = l // SSM_T

    def body(u_ref, dy_ref, hsr_ref, hsi_ref, h_ref, bbd_ref, ccd_ref, pr_ref, pi_ref, d_ref,
             du_ref, dbbd_ref, dccd_ref, dar_ref, dai_ref, dd_ref, gr_s, gi_s):
        first = pl.program_id(1) == 0

        @pl.when(first)
        def _():
            gr_s[...] = jnp.zeros_like(gr_s)
            gi_s[...] = jnp.zeros_like(gi_s)

        u, dy = u_ref[...], dy_ref[...]
        dy_bf = _bf(dy)
        hr0, hi0 = hsr_ref[0], hsi_ref[0]
        h = h_ref[...].astype(F32)
        hr, hi = h[:, :ns], h[:, ns:]
        dh = _dot(dy_bf, ccd_ref[0], NT)
        gr, gi, gcr, gci = _scan_groups(dh[:, :ns], dh[:, ns:], pr_ref, pi_ref, gr_s[...], gi_s[...], True)
        gr_s[...] = gcr
        gi_s[...] = gci
        row0 = lax.broadcasted_iota(jnp.int32, hr.shape, 0) == 0
        pr_h = jnp.where(row0, hr0, pltpu.roll(hr, 1, 0))
        pi_h = jnp.where(row0, hi0, pltpu.roll(hi, 1, 0))
        _acc(dar_ref, first, _colsum(pr_h * gr + pi_h * gi))
        _acc(dai_ref, first, _colsum(pr_h * gi - pi_h * gr))
        g_bf = _bf(jnp.concatenate([gr, gi], axis=1))
        _acc(dbbd_ref.at[0], first, _dot(_bf(u.T), g_bf))
        _acc(dccd_ref.at[0], first, _dot(_bf(h.T), dy_bf))
        du_ref[...] = _bf(_dot(g_bf, bbd_ref[0], NT) + d_ref[...] * dy)
        _acc(dd_ref, first, _colsum(dy * u))

    rev = lambda b, k: (nc - 1 - k, b)
    outs = _call(
        body, name=name, grid=(nb, nc), ride=ride, sem=("parallel", "arbitrary"),
        args=(proj, dy, hs_re, hs_im, h_all, bbd, ccd, pw_re, pw_im, d_skip),
        in_specs=[pl.BlockSpec((SSM_T, cw), rev), pl.BlockSpec((SSM_T, cw), rev),
                  pl.BlockSpec((1, 1, ns), lambda b, k: (nc - 1 - k, 0, b)),
                  pl.BlockSpec((1, 1, ns), lambda b, k: (nc - 1 - k, 0, b)),
                  pl.BlockSpec((SSM_T, ns2), rev),
                  pl.BlockSpec((1, cw, ns2), lambda b, k: (b, 0, 0)),
                  pl.BlockSpec((1, ns2, cw), lambda b, k: (b, 0, 0)),
                  pl.BlockSpec((SCAN_ROWS, ns), lambda b, k: (0, b)), pl.BlockSpec((SCAN_ROWS, ns), lambda b, k: (0, b)),
                  pl.BlockSpec((1, cw), lambda b, k: (0, b))],
        out_specs=[pl.BlockSpec((SSM_T, cw), rev),
                   pl.BlockSpec((1, cw, ns2), lambda b, k: (b, 0, 0)),
                   pl.BlockSpec((1, ns2, cw), lambda b, k: (b, 0, 0)),
                   pl.BlockSpec((1, ns), lambda b, k: (0, b)), pl.BlockSpec((1, ns), lambda b, k: (0, b)),
                   pl.BlockSpec((1, cw), lambda b, k: (0, b))],
        out_shape=[jax.ShapeDtypeStruct((l, w), BF16), jax.ShapeDtypeStruct(bbd.shape, F32),
                   jax.ShapeDtypeStruct(ccd.shape, F32), jax.ShapeDtypeStruct((1, nb * ns), F32),
                   jax.ShapeDtypeStruct((1, nb * ns), F32), jax.ShapeDtypeStruct((1, w), F32)],
        scratch_shapes=[pltpu.VMEM((1, ns), F32), pltpu.VMEM((1, ns), F32)])
    return (*outs[:6], outs[6:])


def _block_diag_b(bb_re, bb_im, g, p, c):
    nb = g // SSM_GB
    keep = _same_group(SSM_GB * c, c, SSM_GB * p, p)

    def one(bb):
        t = bb.reshape(nb, SSM_GB, p, c).transpose(0, 1, 3, 2).reshape(nb, SSM_GB * c, p)
        return jnp.where(keep, jnp.tile(t, (1, 1, SSM_GB)), 0.0)

    return jnp.concatenate([one(bb_re), one(bb_im)], axis=2)


def _same_group(rows, per_row, cols, per_col):
    r = lax.broadcasted_iota(jnp.int32, (rows, cols), 0) // per_row
    q = lax.broadcasted_iota(jnp.int32, (rows, cols), 1) // per_col
    return r == q


def _block_diag_c(c_re, c_im, g, p, c):
    nb = g // SSM_GB
    keep = _same_group(SSM_GB * p, p, SSM_GB * c, c)

    def one(cc):
        t = cc.reshape(nb, SSM_GB, c, p).transpose(0, 1, 3, 2).reshape(nb, SSM_GB * p, c)
        return jnp.where(keep, jnp.tile(t, (1, 1, SSM_GB)), 0.0)

    return jnp.concatenate([one(c_re), one(-c_im)], axis=1)


def _diag_of_b(dbbd, g, p, c):
    nb = g // SSM_GB
    keep = _same_group(SSM_GB * c, c, SSM_GB * p, p)

    def one(blk):
        d = jnp.where(keep, blk, 0.0).reshape(nb, SSM_GB * c, SSM_GB, p).sum(axis=2)
        return d.reshape(nb, SSM_GB, c, p).transpose(0, 1, 3, 2).reshape(g * p, c)

    half = SSM_GB * p
    return one(dbbd[:, :, :half]), one(dbbd[:, :, half:])


def _diag_of_c(dccd, g, p, c):
    nb = g // SSM_GB
    keep = _same_group(SSM_GB * p, p, SSM_GB * c, c)

    def one(blk):
        d = jnp.where(keep, blk, 0.0).reshape(nb, SSM_GB * p, SSM_GB, c).sum(axis=2)
        return d.reshape(nb, SSM_GB, p, c).transpose(0, 1, 3, 2).reshape(g, c, p)

    half = SSM_GB * p
    return one(dccd[:, :half]), -one(dccd[:, half:])


def _glu_fwd(y, proj, w_glu, b_glu, name):
    l, w = y.shape
    tm = _tile(l, 256)

    def body(y_ref, z_ref, w_ref, b_ref, o_ref):
        g = _gelu(y_ref[...])
        t = _dot(_bf(g), w_ref[...]) + b_ref[...]
        o_ref[...] = _bf(g * _sigmoid(t) * _silu(z_ref[...]))

    return pl.pallas_call(
        body, name=name, grid=(l // tm,),
        in_specs=[_row_spec(tm, w), pl.BlockSpec((tm, w), lambda i: (i, 1)),
                  pl.BlockSpec((w, w), lambda i: (0, 0)), _vec_spec(w)],
        out_specs=_row_spec(tm, w), out_shape=jax.ShapeDtypeStruct((l, w), BF16),
        compiler_params=_params(("parallel",)),
    )(y, proj, w_glu, b_glu)


def _glu_bwd(do, y, proj, w_glu, b_glu, name):
    l, w = y.shape
    tm = _tile(l, 512)
    nsteps = l // tm

    def body(do_ref, y_ref, z_ref, w_ref, b_ref, dy_ref, dz_ref, dw_ref, db_ref, dw_acc):
        i = pl.program_id(0)
        first = i == 0
        yv, z, do = y_ref[...], z_ref[...], do_ref[...].astype(F32)
        g = _gelu(yv)
        g_bf = _bf(g)
        sg = _sigmoid(_dot(g_bf, w_ref[...]) + b_ref[...])
        dyy = do * _silu(z)
        dz_ref[...] = _bf(do * g * sg * _silu_grad(z))
        dt = dyy * g * sg * (1.0 - sg)
        dt_bf = _bf(dt)
        dg = dyy * sg + _dot(dt_bf, w_ref[...], NT)
        dy_ref[...] = dg * _gelu_grad(yv)
        _acc(dw_acc, first, _dot(_bf(g.T), dt_bf))
        _acc(db_ref, first, _colsum(dt))

        @pl.when(i == nsteps - 1)
        def _():
            dw_ref[...] = _bf(dw_acc[...])

    return pl.pallas_call(
        body, name=name, grid=(nsteps,),
        in_specs=[_row_spec(tm, w), _row_spec(tm, w), pl.BlockSpec((tm, w), lambda i: (i, 1)),
                  pl.BlockSpec((w, w), lambda i: (0, 0)), _vec_spec(w)],
        out_specs=[_row_spec(tm, w), _row_spec(tm, w), pl.BlockSpec((w, w), lambda i: (0, 0)), _vec_spec(w)],
        out_shape=[jax.ShapeDtypeStruct((l, w), F32), jax.ShapeDtypeStruct((l, w), BF16),
                   jax.ShapeDtypeStruct((w, w), BF16), jax.ShapeDtypeStruct((1, w), F32)],
        scratch_shapes=[pltpu.VMEM((w, w), F32)],
        compiler_params=_params(("arbitrary",)),
    )(do, y, proj, w_glu, b_glu)


MOD_ROWS = 128


def _mod_fwd(cond_pad, w_mod, b_shard, name):
    nl, d, ncol = w_mod.shape
    tn = _tile(ncol, 512)

    def body(c_ref, w_ref, b_ref, o_ref):
        o_ref[0] = _dot(_bf(c_ref[...]), _bf(w_ref[0])) + b_ref[0]

    return pl.pallas_call(
        body, name=name, grid=(nl, ncol // tn),
        in_specs=[pl.BlockSpec((MOD_ROWS, d), lambda a, j: (0, 0)),
                  pl.BlockSpec((1, d, tn), lambda a, j: (a, 0, j)),
                  pl.BlockSpec((1, 1, tn), lambda a, j: (a, 0, j))],
        out_specs=pl.BlockSpec((1, MOD_ROWS, tn), lambda a, j: (a, 0, j)),
        out_shape=jax.ShapeDtypeStruct((nl, MOD_ROWS, ncol), F32),
        compiler_params=_params(("parallel", "parallel")),
    )(cond_pad, w_mod, b_shard)


def _mod_bwd(cond_pad_t, dmod_pad, name):
    nl, _, ncol = dmod_pad.shape
    d = cond_pad_t.shape[0]
    tn = _tile(ncol, 512)

    def body(c_ref, dm_ref, o_ref):
        o_ref[0] = _dot(_bf(c_ref[...]), _bf(dm_ref[0]))

    return pl.pallas_call(
        body, name=name, grid=(nl, ncol // tn),
        in_specs=[pl.BlockSpec((d, MOD_ROWS), lambda a, j: (0, 0)),
                  pl.BlockSpec((1, MOD_ROWS, tn), lambda a, j: (a, 0, j))],
        out_specs=pl.BlockSpec((1, d, tn), lambda a, j: (a, 0, j)),
        out_shape=jax.ShapeDtypeStruct((nl, d, ncol), F32),
        compiler_params=_params(("parallel", "parallel")),
    )(cond_pad_t, dmod_pad)


def _silu_rows(c2d, name):
    def body(c_ref, o_ref):
        o_ref[...] = _silu(c_ref[...])

    return pl.pallas_call(body, name=name, out_shape=jax.ShapeDtypeStruct(c2d.shape, F32))(c2d)


def _sum_leading(x, name):
    n, r, c = x.shape
    tr = _tile(r, max(16, (1 << 20) // (4 * c)), 16 if r % 16 == 0 else 8)

    def body(x_ref, o_ref):
        acc = x_ref[0].astype(F32)
        for k in range(1, n):
            acc = acc + x_ref[k].astype(F32)
        o_ref[...] = acc

    return pl.pallas_call(
        body, name=name, grid=(r // tr,),
        in_specs=[pl.BlockSpec((n, tr, c), lambda i: (0, i, 0))], out_specs=pl.BlockSpec((tr, c), lambda i: (i, 0)),
        out_shape=jax.ShapeDtypeStruct((r, c), F32), compiler_params=_params(("parallel",)),
    )(x)


def _adamw(w, gs, m, v, name, ride=None):
    r, c = w.shape
    tr = _tile(r, max(8, (3 << 19) // (4 * c)), 8)
    ng = len(gs)

    def body(*refs):
        w_ref, g_refs, m_ref, v_ref = refs[0], refs[1:1 + ng], refs[1 + ng], refs[2 + ng]
        g_ref, d_ref, nm_ref, nv_ref = refs[3 + ng:]
        g = g_refs[0][...]
        for extra in g_refs[1:]:
            g = g + extra[...]
        g_ref[...] = g
        d_ref[...], nm_ref[...], nv_ref[...] = _adamw_math(w_ref[...], g, m_ref[...], v_ref[...])

    spec = pl.BlockSpec((tr, c), lambda i: (i, 0))
    shp = jax.ShapeDtypeStruct((r, c), F32)
    outs = _call(body, name=name, grid=(r // tr,), in_specs=[spec] * (3 + ng), out_specs=[spec] * 4,
                 out_shape=[shp] * 4, args=(w, *gs, m, v), sem=("parallel",), ride=ride)
    return outs if ride is None else (outs[:4], outs[4:])


def _adamw_math(w, g, m, v):
    nm = ADAM_B1 * m + (1.0 - ADAM_B1) * g
    nv = ADAM_B2 * v + (1.0 - ADAM_B2) * (g * g)
    m_hat = nm / (1.0 - ADAM_B1 ** ADAM_STEP)
    v_hat = nv / (1.0 - ADAM_B2 ** ADAM_STEP)
    return -ADAM_LR * (m_hat / (jnp.sqrt(v_hat) + ADAM_EPS) + ADAM_WD * w), nm, nv


def _adamw_many(ws, gs, ms, vs, name):
    n = len(ws)

    def body(*refs):
        w_refs, g_refs, m_refs, v_refs = (refs[k * n:(k + 1) * n] for k in range(4))
        outs = refs[4 * n:]
        for i in range(n):
            outs[3 * i][...], outs[3 * i + 1][...], outs[3 * i + 2][...] = _adamw_math(
                w_refs[i][...], g_refs[i][...], m_refs[i][...], v_refs[i][...])

    out_shape = [jax.ShapeDtypeStruct(w.shape, F32) for w in ws for _ in range(3)]
    outs = pl.pallas_call(body, name=name, out_shape=out_shape, compiler_params=_params())(*ws, *gs, *ms, *vs)
    return [tuple(outs[3 * i:3 * i + 3]) for i in range(n)]


ANY = pl.BlockSpec(memory_space=pl.ANY)


def _flip(v, bit):
    return 1 - v if bit else v


def _allgather8_ops(x_ref, o_ref, send_sems, recv_sems, local_sem):
    mx, my, mc = lax.axis_index("x"), lax.axis_index("y"), lax.axis_index("c")
    me = 4 * mx + 2 * my + mc

    def mine():
        return pltpu.make_async_copy(x_ref, o_ref.at[me], local_sem)

    def copy(j, outgoing):
        peer = (_flip(mx, j & 4), _flip(my, j & 2), _flip(mc, j & 1))
        slot = me if outgoing else 4 * peer[0] + 2 * peer[1] + peer[2]
        return pltpu.make_async_remote_copy(
            src_ref=x_ref, dst_ref=o_ref.at[slot], send_sem=send_sems.at[j - 1], recv_sem=recv_sems.at[j - 1],
            device_id=peer, device_id_type=MESH)

    def start():
        mine().start()
        for j in range(1, 8):
            copy(j, True).start()

    def wait():
        for j in range(1, 8):
            copy(j, False).wait()
        mine().wait()

    return start, wait


def _ride_all8(x):
    return dict(xs=[x], shapes=[jax.ShapeDtypeStruct((8,) + x.shape, x.dtype)],
                sems=[pltpu.SemaphoreType.DMA((7,)), pltpu.SemaphoreType.DMA((7,)), pltpu.SemaphoreType.DMA],
                ops=lambda x_refs, o_refs, sems: _allgather8_ops(x_refs[0], o_refs[0], *sems))


def _ride_chip(xs, gather):
    return dict(xs=list(xs), shapes=_chip_exchange_shapes(xs, gather), sems=_chip_exchange_sems(len(xs)),
                ops=lambda x_refs, o_refs, sems: _chip_exchange_ops(x_refs, o_refs, *sems, gather))


def _allgather8(x, name):
    def body(x_ref, o_ref, *sems):
        start, wait = _allgather8_ops(x_ref, o_ref, *sems)
        start()
        wait()

    ride = _ride_all8(x)
    return pl.pallas_call(body, name=name, in_specs=[ANY], out_specs=ANY, out_shape=ride["shapes"][0],
                          scratch_shapes=ride["sems"])(x)


def _gather_halves_ops(x_ref, o_ref, ici_send, ici_recv, d2d_send, d2d_recv, local_sem):
    half = x_ref.shape[0] // 2
    quarter = half // 2
    mx, my, mc = lax.axis_index("x"), lax.axis_index("y"), lax.axis_index("c")
    k0, kx, ky, kd = 2 * mx + my, 2 * (1 - mx) + my, 2 * mx + (1 - my), 2 * (1 - mx) + (1 - my)
    x_nbr, y_nbr, sib = (1 - mx, my, mc), (mx, 1 - my, mc), (mx, my, 1 - mc)

    def rows(core, part):
        if part is None:
            return pl.ds(pl.multiple_of(core * half, 16), half)
        return pl.ds(pl.multiple_of(core * half + part * quarter, 16), quarter)

    def local():
        return pltpu.make_async_copy(x_ref, o_ref.at[k0], local_sem)

    def ici(n, outgoing):
        to = x_nbr if n in (0, 2) else y_nbr
        if n < 2:
            src = x_ref.at[rows(mc, None)]
            dst = o_ref.at[k0 if outgoing else (kx if n == 0 else ky), rows(mc, None)]
        else:
            part = n - 2
            src = o_ref.at[ky if n == 2 else kx, rows(mc, part)]
            dst = o_ref.at[(ky if n == 2 else kx) if outgoing else kd, rows(mc, part)]
        return pltpu.make_async_remote_copy(src_ref=src, dst_ref=dst, send_sem=ici_send.at[n], recv_sem=ici_recv.at[n],
                                            device_id=to, device_id_type=MESH)

    def d2d(n, outgoing):
        slot, part = ((kx, None), (ky, None), (kd, 0), (kd, 1))[n]
        piece = o_ref.at[slot, rows(mc if outgoing else 1 - mc, part)]
        return pltpu.make_async_remote_copy(src_ref=piece, dst_ref=piece, send_sem=d2d_send.at[n],
                                            recv_sem=d2d_recv.at[n], device_id=sib, device_id_type=MESH)

    def start():
        local().start()
        ici(0, True).start()
        ici(1, True).start()

    def wait():
        ici(1, False).wait_recv()
        ici(2, True).start()
        d2d(1, True).start()
        ici(0, False).wait_recv()
        ici(3, True).start()
        d2d(0, True).start()
        ici(2, False).wait_recv()
        d2d(2, True).start()
        ici(3, False).wait_recv()
        d2d(3, True).start()
        for n in range(4):
            ici(n, True).wait_send()
            d2d(n, True).wait_send()
            d2d(n, False).wait_recv()
        local().wait()

    return start, wait


def _ride_halves(x):
    dma4 = pltpu.SemaphoreType.DMA((4,))
    return dict(xs=[x], shapes=[jax.ShapeDtypeStruct((4,) + x.shape, x.dtype)],
                sems=[dma4, dma4, dma4, dma4, pltpu.SemaphoreType.DMA],
                ops=lambda x_refs, o_refs, sems: _gather_halves_ops(x_refs[0], o_refs[0], *sems))


def _chip_exchange(xs, gather, name):
    n = len(xs)

    def body(*refs):
        start, wait = _chip_exchange_ops(refs[:n], refs[n:2 * n], *refs[2 * n:], gather)
        start()
        wait()

    return pl.pallas_call(
        body, name=name, in_specs=[ANY] * n, out_specs=[ANY] * n, out_shape=_chip_exchange_shapes(xs, gather),
        scratch_shapes=_chip_exchange_sems(n),
    )(*xs)


def _chip_exchange_shapes(xs, gather):
    return [jax.ShapeDtypeStruct(((4,) + x.shape) if gather else x.shape, x.dtype) for x in xs]


def _chip_exchange_sems(n):
    return [pltpu.SemaphoreType.DMA((3 * n,)), pltpu.SemaphoreType.DMA((3 * n,)), pltpu.SemaphoreType.DMA((n,))]


def _chip_exchange_ops(x_refs, o_refs, send_sems, recv_sems, local_sems, gather):
    n = len(x_refs)
    mx, my, mc = lax.axis_index("x"), lax.axis_index("y"), lax.axis_index("c")
    k0 = 2 * mx + my

    def local(a):
        src = x_refs[a] if gather else x_refs[a].at[k0]
        return pltpu.make_async_copy(src, o_refs[a].at[k0], local_sems.at[a])

    def copy(a, j, outgoing):
        px, py = _flip(mx, j & 2), _flip(my, j & 1)
        kp = 2 * px + py
        if outgoing:
            src = x_refs[a] if gather else x_refs[a].at[kp]
            dst = o_refs[a].at[k0]
        else:
            src = x_refs[a] if gather else x_refs[a].at[k0]
            dst = o_refs[a].at[kp]
        s = a * 3 + j - 1
        return pltpu.make_async_remote_copy(
            src_ref=src, dst_ref=dst, send_sem=send_sems.at[s], recv_sem=recv_sems.at[s],
            device_id=(px, py, mc), device_id_type=MESH)

    def start():
        for a in range(n):
            local(a).start()
            for j in range(1, 4):
                copy(a, j, True).start()

    def wait():
        for a in range(n):
            for j in range(1, 4):
                copy(a, j, False).wait()
            local(a).wait()

    return start, wait


def _call(body, *, name, grid, in_specs, out_specs, out_shape, args, scratch_shapes=(), sem=None, ride=None):
    if not ride:
        return pl.pallas_call(
            body, name=name, grid=grid, in_specs=list(in_specs), out_specs=list(out_specs), out_shape=list(out_shape),
            scratch_shapes=list(scratch_shapes), compiler_params=_params(sem))(*args)
    xs = [x for r in ride for x in r["xs"]]
    shapes = [s for r in ride for s in r["shapes"]]
    sems = [s for r in ride for s in r["sems"]]
    n_in, n_out, n_scr, nx = len(in_specs), len(out_specs), len(scratch_shapes), len(xs)

    def wrapped(*refs):
        ins, x_refs = refs[:n_in], refs[n_in:n_in + nx]
        outs = refs[n_in + nx:n_in + nx + n_out]
        lands = refs[n_in + nx + n_out:n_in + 2 * nx + n_out]
        rest = refs[n_in + 2 * nx + n_out:]
        scr, sem_refs = rest[:n_scr], rest[n_scr:]
        ops, xo, so = [], 0, 0
        for r in ride:
            nr, ns = len(r["xs"]), len(r["sems"])
            ops.append(r["ops"](x_refs[xo:xo + nr], lands[xo:xo + nr], sem_refs[so:so + ns]))
            xo, so = xo + nr, so + ns
        ids = [pl.program_id(a) for a in range(len(grid))]
        first = functools.reduce(jnp.logical_and, [i == 0 for i in ids])
        last = functools.reduce(jnp.logical_and, [i == g - 1 for i, g in zip(ids, grid)])

        @pl.when(first)
        def _():
            for start, _ in ops:
                start()

        body(*ins, *outs, *scr)

        @pl.when(last)
        def _():
            for _, wait in ops:
                wait()

    return pl.pallas_call(
        wrapped, name=name, grid=grid, in_specs=list(in_specs) + [ANY] * nx, out_specs=list(out_specs) + [ANY] * nx,
        out_shape=list(out_shape) + shapes, scratch_shapes=list(scratch_shapes) + sems,
        compiler_params=_params(("arbitrary",) * len(grid)))(*args, *xs)


def _sibling_exchange(xs, name):
    ride = _ride_sibling(xs)
    n = len(xs)

    def body(*refs):
        start, wait = ride["ops"](refs[:n], refs[n:2 * n], refs[2 * n:])
        start()
        wait()

    return pl.pallas_call(body, name=name, in_specs=[ANY] * n, out_specs=[ANY] * n, out_shape=ride["shapes"],
                          scratch_shapes=ride["sems"])(*xs)


def _ride_sibling(xs):
    n = len(xs)

    def ops(x_refs, o_refs, sems):
        send_sems, recv_sems = sems
        sib = (lax.axis_index("x"), lax.axis_index("y"), 1 - lax.axis_index("c"))

        def copies():
            return [pltpu.make_async_remote_copy(
                src_ref=x_refs[a], dst_ref=o_refs[a], send_sem=send_sems.at[a], recv_sem=recv_sems.at[a],
                device_id=sib, device_id_type=MESH) for a in range(n)]

        def start():
            for cp in copies():
                cp.start()

        def wait():
            for cp in copies():
                cp.wait()

        return start, wait

    return dict(xs=list(xs), shapes=[jax.ShapeDtypeStruct(x.shape, x.dtype) for x in xs],
                sems=[pltpu.SemaphoreType.DMA((n,)), pltpu.SemaphoreType.DMA((n,))], ops=ops)


PACK = 1024
PACK_ROWS = 512


def _pack(parts, pad_rows=PACK_ROWS):
    flat = []
    for p in parts:
        v = p.reshape(-1).astype(F32)
        flat.append(jnp.pad(v, (0, (-v.shape[0]) % PACK)))
    total = sum(v.shape[0] for v in flat)
    flat.append(jnp.zeros(((-total) % (pad_rows * 128),), F32))
    return jnp.concatenate(flat).reshape(-1, 128)


def _shard_columns(shards, lo, hi):
    width = shards.shape[2]
    out = []
    for k in range(shards.shape[0]):
        a, b = max(lo, k * width), min(hi, (k + 1) * width)
        if a < b:
            out.append(shards[k, :, a - k * width:b - k * width])
    return out


def _unpack_rows(gathered, shapes):
    flat = gathered.reshape(gathered.shape[0], -1)
    out, off = [], 0
    for shp in shapes:
        n = math.prod(shp)
        out.append(flat[:, off:off + n].reshape((flat.shape[0],) + tuple(shp)))
        off += n + (-n) % PACK
    return out


def _unpack(packed, shapes):
    flat = packed.reshape(-1)
    out, off = [], 0
    for shp in shapes:
        n = math.prod(shp)
        out.append(flat[off:off + n].reshape(shp))
        off += n + (-n) % PACK
    return out


def kernel(x, c, ln_pre_g, ln_post_g, w_mod, b_mod, w_in_ab, w_out_ab, sgu_norm_g, sgu_w, sgu_b, w_in_ssm, w_out_ssm, lam_re, lam_im, b_re, b_im, c_re, c_im, d_skip, log_dt, w_glu, b_glu, loss_target, m_ln_pre_g, m_ln_post_g, m_w_mod, m_b_mod, m_w_in_ab, m_w_out_ab, m_sgu_norm_g, m_sgu_w, m_sgu_b, m_w_in_ssm, m_w_out_ssm, m_lam_re, m_lam_im, m_b_re, m_b_im, m_c_re, m_c_im, m_d_skip, m_log_dt, m_w_glu, m_b_glu, v_ln_pre_g, v_ln_post_g, v_w_mod, v_b_mod, v_w_in_ab, v_w_out_ab, v_sgu_norm_g, v_sgu_w, v_sgu_b, v_w_in_ssm, v_w_out_ssm, v_lam_re, v_lam_im, v_b_re, v_b_im, v_c_re, v_c_im, v_d_skip, v_log_dt, v_w_glu, v_b_glu):
    given = dict(locals())
    mx, my, mc = lax.axis_index("x"), lax.axis_index("y"), lax.axis_index("c")
    me = 4 * mx + 2 * my + mc
    chip = 2 * mx + my

    _, l, d = x.shape
    x2, tgt = x[0], loss_target[0]
    n_in = w_in_ab.shape[2] * 4
    wa = wb = n_in // 7
    w = w_out_ssm.shape[1]
    g, p, cch = b_re.shape[1:]
    nmod = w_mod.shape[2]


    cond = _silu_rows(c.reshape(d // 128, 128), "cond_silu")
    cond_all = _allgather8(cond, "gather_cond").reshape(8, d)
    b_shard = lax.dynamic_slice(b_mod, (0, chip * nmod), (2, nmod)).reshape(2, 1, nmod)
    cond_pad = jnp.pad(cond_all, ((0, MOD_ROWS - 8), (0, 0)))
    modp = _mod_fwd(cond_pad, w_mod, b_shard, "mod_fwd")[:, :8]
    modp_all = _allgather8(modp.reshape(16, nmod), "gather_mod").reshape(4, 2, 2, 8, nmod)
    mine = lax.dynamic_index_in_dim(lax.dynamic_index_in_dim(modp_all, mc, 1, False), me, 2, False)
    mod = mine.transpose(1, 0, 2).reshape(2, 3 * d)
    shift = [mod[a:a + 1, :d] for a in range(2)]
    scale = [mod[a:a + 1, d:2 * d] for a in range(2)]
    gate = [mod[a:a + 1, 2 * d:] for a in range(2)]
    pre_g = [ln_pre_g[a:a + 1] for a in range(2)]
    post_g = [ln_post_g[a:a + 1] for a in range(2)]

    sgu_w0, sgu_bt = sgu_w[0], sgu_b[0].T
    h0, (gw_in_ab,) = _pre_fwd(x2, pre_g[0], scale[0], shift[0], "pre0_fwd", ride=[_ride_halves(_bf(w_in_ab[0]))])
    w_gates = jnp.concatenate(_shard_columns(gw_in_ab, 0, 3 * wa) + _shard_columns(gw_in_ab, 3 * wa + 3 * wb, n_in),
                              axis=1)
    w_qkv = jnp.concatenate(_shard_columns(gw_in_ab, 3 * wa, 3 * wa + 3 * wb), axis=1)
    proj0, (gw_in_ssm,) = _matmul(h0, w_gates, "nn", BF16, "proj0", tm=1024, ride=[_ride_chip([_bf(w_in_ssm[0])], True)])
    qkv, (gw_out_ssm, gw_glu, g_dskip, g_bglu) = _matmul(
        h0, w_qkv, "nn", BF16, "proj0_qkv", tm=1024,
        ride=[_ride_chip([_bf(w_out_ssm[0]), _bf(w_glu[0]), d_skip, b_glu], True)])
    out_b, (gw_out_ab,) = _attn_fwd(qkv, wb, "attn_fwd", hp=8, ride=[_ride_chip([_bf(w_out_ab[0])], True)])
    wout_ab = gw_out_ab.reshape(wa + wb, d)
    win_ssm = gw_in_ssm.reshape(d, 2 * w)
    wout_ssm = jnp.concatenate([gw_out_ssm[k] for k in range(4)], axis=1)
    wglu = gw_glu.reshape(w, w)
    dskip_full = g_dskip.reshape(1, w)
    bglu_full = g_bglu.reshape(1, w)
    cat =_sgu_fwd(proj0, out_b, sgu_norm_g, sgu_w0, sgu_bt, wa, wb, "sgu_fwd")
    y0 = _matmul(cat, wout_ab, "nn", BF16, "out0", tm=1024)
    x1, h1 = _post_pre_fwd(x2, y0, gate[0], post_g[0], pre_g[1], scale[1], shift[1], "post0_pre1_fwd")

    s = g * p
    lr_c, li_c = lam_re.reshape(s, 1), lam_im.reshape(s, 1)
    ldt_c = jnp.repeat(log_dt.reshape(g), p).reshape(s, 1)
    br_c, bi_c = b_re.reshape(s, cch), b_im.reshape(s, cch)
    bb_re, bb_im, pw_re, pw_im = _ssm_prep(lr_c, li_c, ldt_c, br_c, bi_c, lr_c.reshape(1, s), li_c.reshape(1, s),
                                           ldt_c.reshape(1, s), "ssm_prep")
    bbd = _bf(_block_diag_b(bb_re, bb_im, g, p, cch))
    ccd = _bf(_block_diag_c(c_re[0], c_im[0], g, p, cch))
    proj1 = _matmul(h1, win_ssm, "nn", F32, "proj1", tm=1024)
    y_ssm, hs_re, hs_im, h_all = _ssm_fwd(proj1, bbd, ccd, pw_re, pw_im, dskip_full, w, "ssm_fwd")
    o1 = _glu_fwd(y_ssm, proj1, wglu, bglu_full, "glu_fwd")
    y1 = _matmul(o1, wout_ssm, "nn", BF16, "out1", tm=1024)
    loss_vec, dy1, dx2, dgate1, dpost1 = _post_loss(x1, y1, gate[1], post_g[1], tgt, "post1_loss")

    do1 = _matmul(dy1, wout_ssm, "nt", BF16, "out1_dx", tm=1024)
    gr_wout_ssm = _matmul_at(o1, dy1, BF16, "out1_dw", n_split=4)
    dy_ssm, dz1, gr_wglu, gr_bglu = _glu_bwd(do1, y_ssm, proj1, wglu, bglu_full, "glu_bwd")
    du1, dbbd, dccd, da_re, da_im, gr_dskip, (ld_wout_ssm, ld_wglu) = _ssm_bwd(
        proj1, dy_ssm, hs_re, hs_im, h_all, bbd, ccd, pw_re, pw_im, dskip_full, w, "ssm_bwd",
        ride=[_ride_chip([gr_wout_ssm, gr_wglu.reshape(4, w // 4, w)], False)])
    dproj1 = jnp.concatenate([du1, dz1], axis=1)
    dh1 = _matmul(dproj1, win_ssm, "nt", BF16, "proj1_dx", tm=1024)
    gr_win_ssm = _matmul_at(h1, dproj1, BF16, "proj1_dw", tn=1024)
    dx1, dscale1, dshift1, dpre1, dy0, dgate0, dpost0 = _pre_bwd(
        dh1, dx2, x1, pre_g[1], scale[1], "pre1_post0_bwd", post=(y0, gate[0], post_g[0]))

    dcat = _matmul(dy0, wout_ab, "nt", BF16, "out0_dx", tm=1024)
    gr_wout_ab = _matmul_at(cat, dy0, BF16, "out0_dw", tn=1024)
    dbb_re, dbb_im = _diag_of_b(dbbd, g, p, cch)
    dc_re, dc_im = _diag_of_c(dccd, g, p, cch)
    part_a = [loss_vec[:, :1], dpre1, dpost0, dpost1, dgate0, dshift1, dscale1, dgate1, da_re, da_im,
              dbb_re, dbb_im, dc_re, dc_im, gr_dskip, gr_bglu]
    shapes_a = [a.shape for a in part_a]
    dq, dk, dv, (ld_win_ssm, ld_wout_ab, gath_a) = _attn_bwd(
        qkv, proj0, dcat, wa, wb, "attn_bwd", hp=4,
        ride=[_ride_chip([gr_win_ssm.reshape(4, d // 4, 2 * w), gr_wout_ab.reshape(4, (wa + wb) // 4, d)], False),
              _ride_all8(_pack(part_a))])
    early_names = ["w_out_ab", "w_in_ssm", "w_out_ssm", "w_glu"]
    early_sums = [_sum_leading(a, "sum_" + nm) for a, nm in zip([ld_wout_ab, ld_win_ssm, ld_wout_ssm, ld_wglu], early_names)]
    dproj0, gr_sgu_w, gr_sgu_bt, gr_sgu_g, early_sib = _sgu_bwd(
        proj0, out_b, dcat, dq, dk, dv, sgu_norm_g, sgu_w0, sgu_bt, wa, wb, "sgu_bwd", ride=[_ride_sibling(early_sums)])
    part_b = [gr_sgu_g, gr_sgu_w, gr_sgu_bt.T]
    shapes_b = [a.shape for a in part_b]
    gr_win_ab_lo, (gath_b,) = _matmul_at(h0, dproj0, BF16, "proj0_dw_lo", tm=512, tn=n_in // 4, n_split=4,
                                         m_part=(0, 1, 2), ride=[_ride_all8(_pack(part_b))])
    gr_win_ab_hi, (ld_win_ab_lo,) = _matmul_at(
        h0, dproj0, BF16, "proj0_dw_hi", tm=512, tn=n_in // 4, n_split=4, m_part=(1, 1, 2),
        ride=[_ride_chip([gr_win_ab_lo], False)])
    dh0, (ld_win_ab_hi,) = _matmul(dproj0, gw_in_ab, "nt", BF16, "proj0_dx", tm=1024, tn=1024,
                                   ride=[_ride_chip([gr_win_ab_hi], False)])
    grad_x, dscale0, dshift0, dpre0 = _pre_bwd(dh0, dx1, x2, pre_g[0], scale[0], "pre0_bwd")
    part_c = [dpre0, dshift0, dscale0]
    shapes_c = [a.shape for a in part_c]
    gath_c = _allgather8(_pack(part_c, pad_rows=8), "gather_small_tail")

    big_names = ["w_in_ab"] + early_names
    sum_win_ab = jnp.concatenate([_sum_leading(ld_win_ab_lo, "sum_w_in_ab_lo"),
                                  _sum_leading(ld_win_ab_hi, "sum_w_in_ab_hi")], axis=0)
    sums = [sum_win_ab] + early_sums
    sib = list(_sibling_exchange([sum_win_ab], "sibling_w_in_ab")) + list(early_sib)
    results = {}
    for nm, s_mine, s_sib in zip(big_names, sums, sib):
        shp = given[nm].shape
        two_d = lambda a: a.reshape(-1, shp[-1])
        outs = _adamw(two_d(given[nm]), [s_mine, s_sib], two_d(given["m_" + nm]), two_d(given["v_" + nm]),
                      "adamw_" + nm)
        results[nm] = [o.reshape(shp) for o in outs]

    (loss_s, g_pre1, g_post0, g_post1, g_gate0, g_shift1, g_scale1, g_gate1, s_da_re, s_da_im, s_dbb_re, s_dbb_im,
     g_c_re, g_c_im, g_dskip_full, g_bglu_full) = _unpack(_sum_leading(gath_a, "sum_small_a"), shapes_a)
    g_sgu_g, g_sgu_w, g_sgu_b = _unpack(_sum_leading(gath_b, "sum_small_b"), shapes_b)
    g_pre0, g_shift0, g_scale0 = _unpack(_sum_leading(gath_c, "sum_small_c"), shapes_c)
    loss = loss_s.reshape(())
    g_pre = jnp.concatenate([g_pre0, g_pre1], axis=0)
    g_post = jnp.concatenate([g_post0, g_post1], axis=0)
    g_bmod = jnp.concatenate([jnp.concatenate([g_shift0, g_scale0, g_gate0], axis=1),
                              jnp.concatenate([g_shift1, g_scale1, g_gate1], axis=1)], axis=0)

    g_lr, g_li, g_ldt, g_br, g_bi = _ssm_prep_bwd(lr_c, li_c, ldt_c, br_c, bi_c, s_da_re.reshape(s, 1),
                                                  s_da_im.reshape(s, 1), s_dbb_re, s_dbb_im, p, "ssm_prep_bwd")
    small = {
        "ln_pre_g": g_pre, "ln_post_g": g_post, "b_mod": g_bmod, "sgu_norm_g": g_sgu_g,
        "sgu_w": g_sgu_w.reshape(sgu_w.shape), "sgu_b": g_sgu_b.reshape(sgu_b.shape),
        "lam_re": g_lr.reshape(lam_re.shape), "lam_im": g_li.reshape(lam_im.shape),
        "b_re": g_br.reshape(b_re.shape), "b_im": g_bi.reshape(b_im.shape),
        "c_re": g_c_re.reshape(c_re.shape), "c_im": g_c_im.reshape(c_im.shape),
        "d_skip": lax.dynamic_slice(g_dskip_full, (0, chip * (w // 4)), (1, w // 4)),
        "log_dt": g_ldt.reshape(log_dt.shape),
        "b_glu": lax.dynamic_slice(g_bglu_full, (0, chip * (w // 4)), (1, w // 4)),
    }
    flat2 = lambda a: a.reshape(-1, a.shape[-1])
    wide = ("b_re", "b_im")
    for tag, group in (("adamw_small", [nm for nm in small if nm not in wide]), ("adamw_small_b", list(wide))):
        outs = _adamw_many([flat2(given[nm]) for nm in group], [flat2(small[nm]) for nm in group],
                           [flat2(given["m_" + nm]) for nm in group], [flat2(given["v_" + nm]) for nm in group], tag)
        for nm, trio in zip(group, outs):
            results[nm] = [small[nm]] + [o.reshape(given[nm].shape) for o in trio]

    rows_a = _unpack_rows(gath_a, shapes_a)
    rows_c = _unpack_rows(gath_c, shapes_c)
    dmod_rows = jnp.concatenate([rows_c[1], rows_c[2], rows_a[4], rows_a[5], rows_a[6], rows_a[7]],
                                axis=2).reshape(8, 2, 3 * d)
    dmod_shard = lax.dynamic_slice(dmod_rows, (0, 0, chip * nmod), (8, 2, nmod)).transpose(1, 0, 2)
    dmod_pad = jnp.pad(dmod_shard, ((0, 0), (0, MOD_ROWS - 8), (0, 0)))
    gr_wmod = _mod_bwd(cond_pad.T, dmod_pad, "mod_bwd")
    two_d = lambda a: a.reshape(-1, nmod)
    outs = _adamw(two_d(w_mod), [two_d(gr_wmod)], two_d(m_w_mod), two_d(v_w_mod), "adamw_w_mod")
    results["w_mod"] = [o.reshape(w_mod.shape) for o in outs]

    names = ["ln_pre_g", "ln_post_g", "w_mod", "b_mod", "w_in_ab", "w_out_ab", "sgu_norm_g", "sgu_w", "sgu_b",
             "w_in_ssm", "w_out_ssm", "lam_re", "lam_im", "b_re", "b_im", "c_re", "c_im", "d_skip", "log_dt",
             "w_glu", "b_glu"]
    return (loss, grad_x[None], *[results[nm][0] for nm in names], *[results[nm][1] for nm in names],
            *[results[nm][2] for nm in names], *[results[nm][3] for nm in names])
```

```python
import functools
import math

import jax
import jax.numpy as jnp
from jax import lax
from jax.experimental import pallas as pl
from jax.experimental.pallas import tpu as pltpu

F32 = jnp.float32
BF16 = jnp.bfloat16
MESH = pl.DeviceIdType.MESH

EPS = 1e-6
HEAD = 128
SSM_T = 512
SSM_GB = 16
ADAM_LR, ADAM_B1, ADAM_B2, ADAM_EPS, ADAM_WD, ADAM_STEP = 0.001, 0.9, 0.999, 1e-08, 0.01, 10
VMEM_LIMIT = 56 * 1024 * 1024

NN = (((1,), (0,)), ((), ()))
NT = (((1,), (1,)), ((), ()))
TN = (((0,), (0,)), ((), ()))


def _params(sem=None):
    return pltpu.CompilerParams(dimension_semantics=sem, vmem_limit_bytes=VMEM_LIMIT)


def _dot(a, b, dims=NN):
    return lax.dot_general(a, b, dims, preferred_element_type=F32)


def _bf(x):
    return x.astype(BF16)


def _gelu(x):
    k = math.sqrt(2.0 / math.pi)
    t = jnp.tanh(k * (x + 0.044715 * x * x * x))
    return 0.5 * x * (1.0 + t)


def _gelu_grad(x):
    k = math.sqrt(2.0 / math.pi)
    x2 = x * x
    t = jnp.tanh(k * (x + 0.044715 * x * x2))
    return 0.5 * (1.0 + t) + 0.5 * x * (1.0 - t * t) * k * (1.0 + 3.0 * 0.044715 * x2)


def _sigmoid(x):
    return 1.0 / (1.0 + jnp.exp(-x))


def _silu(x):
    return x * _sigmoid(x)


def _silu_grad(x):
    s = _sigmoid(x)
    return s * (1.0 + x * (1.0 - s))


def _tile(n, t, mult=128):
    if n <= t:
        return n
    for cand in range(t - t % mult, 0, -mult):
        if n % cand == 0:
            return cand
    raise ValueError((n, t, mult))


def _matmul(a, b, mode, out_dtype, name, tm=512, tn=512, tk=2048, n_split=1, ride=None, m_part=None):
    b_sharded = b.ndim == 3
    if mode == "nn":
        (m, kk), (_, n) = a.shape, b.shape
    elif b_sharded:
        assert mode == "nt"
        (m, kk), n, tk = a.shape, b.shape[1], b.shape[2]
    elif mode == "nt":
        (m, kk), (n, _) = a.shape, b.shape
    else:
        (kk, m), (_, n) = a.shape, b.shape
    m_off = 0
    if m_part is not None:
        assert mode == "tn"
        first, count, parts = m_part
        tm = _tile(m // parts, tm)
        m_off = first * (m // parts) // tm
        m = count * (m // parts)
    tm, tk = _tile(m, tm), _tile(kk, tk)
    ns = n // n_split
    tn = _tile(ns, tn)
    nk = kk // tk
    dims = {"nn": NN, "nt": NT, "tn": TN}[mode]

    def body(a_ref, b_ref, o_ref, acc_ref):
        k = pl.program_id(2)
        part = _dot(_bf(a_ref[...]), _bf(b_ref[0] if b_sharded else b_ref[...]), dims)

        @pl.when(k == 0)
        def _():
            acc_ref[...] = part

        @pl.when(k > 0)
        def _():
            acc_ref[...] += part

        @pl.when(k == nk - 1)
        def _():
            o_ref[...] = acc_ref[...].astype(out_dtype).reshape(o_ref.shape)

    if mode == "nn":
        a_spec = pl.BlockSpec((tm, tk), lambda i, j, k: (i, k))
        b_spec = pl.BlockSpec((tk, tn), lambda i, j, k: (k, j))
    elif mode == "nt":
        a_spec = pl.BlockSpec((tm, tk), lambda i, j, k: (i, k))
        b_spec = (pl.BlockSpec((1, tn, tk), lambda i, j, k: (k, j, 0)) if b_sharded
                  else pl.BlockSpec((tn, tk), lambda i, j, k: (j, k)))
    else:
        a_spec = pl.BlockSpec((tk, tm), lambda i, j, k: (k, i + m_off))
        b_spec = pl.BlockSpec((tk, tn), lambda i, j, k: (k, j))
    if n_split == 1:
        out_shape = jax.ShapeDtypeStruct((m, n), out_dtype)
        o_spec = pl.BlockSpec((tm, tn), lambda i, j, k: (i, j))
    else:
        per = ns // tn
        out_shape = jax.ShapeDtypeStruct((n_split, m, ns), out_dtype)
        o_spec = pl.BlockSpec((1, tm, tn), lambda i, j, k: (j // per, i, j % per))
    outs = _call(body, name=name, grid=(m // tm, n // tn, nk), in_specs=[a_spec, b_spec], out_specs=[o_spec],
                 out_shape=[out_shape], scratch_shapes=[pltpu.VMEM((tm, tn), F32)], args=(a, b),
                 sem=("parallel", "parallel", "arbitrary"), ride=ride)
    return outs[0] if ride is None else (outs[0], outs[1:])


def _matmul_at(a, b, out_dtype, name, tm=1024, tn=512, n_split=1, ride=None, m_part=None):
    (kk, m), (_, n) = a.shape, b.shape
    m_off = 0
    if m_part is not None:
        first, count, parts = m_part
        tm = _tile(m // parts, tm)
        m_off = first * (m // parts) // tm
        m = count * (m // parts)
    tm = _tile(m, tm)
    ns = n // n_split
    tn = _tile(ns, tn)
    kc = _tile(kk, 512)

    def body(a_ref, b_ref, o_ref, at_ref):
        @pl.when(pl.program_id(1) == 0)
        def _():
            for c in range(kk // kc):
                at_ref[:, c * kc:(c + 1) * kc] = _bf(a_ref[c * kc:(c + 1) * kc, :].astype(F32).T)

        o_ref[...] = _dot(at_ref[...], _bf(b_ref[...])).astype(out_dtype).reshape(o_ref.shape)

    if n_split == 1:
        out_shape = jax.ShapeDtypeStruct((m, n), out_dtype)
        o_spec = pl.BlockSpec((tm, tn), lambda i, j: (i, j))
    else:
        per = ns // tn
        out_shape = jax.ShapeDtypeStruct((n_split, m, ns), out_dtype)
        o_spec = pl.BlockSpec((1, tm, tn), lambda i, j: (j // per, i, j % per))
    outs = _call(body, name=name, grid=(m // tm, n // tn),
                 in_specs=[pl.BlockSpec((kk, tm), lambda i, j: (0, i + m_off)), pl.BlockSpec((kk, tn), lambda i, j: (0, j))],
                 out_specs=[o_spec], out_shape=[out_shape], scratch_shapes=[pltpu.VMEM((tm, kk), BF16)], args=(a, b),
                 sem=("arbitrary", "arbitrary"), ride=ride)
    return outs[0] if ride is None else (outs[0], outs[1:])


def _row_spec(tm, d):
    return pl.BlockSpec((tm, d), lambda i: (i, 0))


def _vec_spec(d):
    return pl.BlockSpec((1, d), lambda i: (0, 0))


def _acc(ref, first, val):
    @pl.when(first)
    def _():
        ref[...] = val

    @pl.when(jnp.logical_not(first))
    def _():
        ref[...] += val


def _colsum(x):
    return jnp.sum(x, axis=0, keepdims=True)


def _rownorm(x):
    r = lax.rsqrt(jnp.mean(x * x, axis=-1, keepdims=True) + EPS)
    return x * r, r


STRIP = 64


def _fold8(x):
    return functools.reduce(lambda a, b: a + b, [x[8 * k:8 * k + 8] for k in range(x.shape[0] // 8)])


def _pre_fwd(x, g, scale, shift, name, ride=None):
    l, d = x.shape
    tm = _tile(l, 256)

    def body(x_ref, g_ref, sc_ref, sh_ref, h_ref):
        n, _ = _rownorm(x_ref[...])
        h_ref[...] = _bf(n * g_ref[...] * (1.0 + sc_ref[...]) + sh_ref[...])

    outs = _call(body, name=name, grid=(l // tm,), in_specs=[_row_spec(tm, d), _vec_spec(d), _vec_spec(d), _vec_spec(d)],
                 out_specs=[_row_spec(tm, d)], out_shape=[jax.ShapeDtypeStruct((l, d), BF16)],
                 args=(x, g, scale, shift), sem=("parallel",), ride=ride)
    return outs[0], outs[1:]


def _post_pre_fwd(x, y, gate, pg, g1, scale1, shift1, name):
    l, d = x.shape
    tm = _tile(l, 256)

    def body(x_ref, y_ref, gate_ref, pg_ref, g1_ref, sc_ref, sh_ref, x1_ref, h1_ref):
        @pl.loop(0, tm // STRIP)
        def _(s):
            rows = pl.ds(pl.multiple_of(s * STRIP, STRIP), STRIP)
            ny, _ = _rownorm(y_ref[rows, :].astype(F32))
            x1 = x_ref[rows, :] + gate_ref[...] * (ny * pg_ref[...])
            x1_ref[rows, :] = x1
            n1, _ = _rownorm(x1)
            h1_ref[rows, :] = _bf(n1 * g1_ref[...] * (1.0 + sc_ref[...]) + sh_ref[...])

    v = _vec_spec(d)
    return pl.pallas_call(
        body, name=name, grid=(l // tm,),
        in_specs=[_row_spec(tm, d), _row_spec(tm, d), v, v, v, v, v],
        out_specs=[_row_spec(tm, d), _row_spec(tm, d)],
        out_shape=[jax.ShapeDtypeStruct((l, d), F32), jax.ShapeDtypeStruct((l, d), BF16)],
        compiler_params=_params(("parallel",)),
    )(x, y, gate, pg, g1, scale1, shift1)


def _post_loss(x1, y1, gate, pg, target, name):
    l, d = x1.shape
    tm = _tile(l, 256)

    def body(x_ref, y_ref, gate_ref, pg_ref, t_ref, loss_ref, dy_ref, dx_ref, dgate_ref, dpg_ref):
        first = pl.program_id(0) == 0

        def strip(s, sums):
            rows = pl.ds(pl.multiple_of(s * STRIP, STRIP), STRIP)
            ny, ry = _rownorm(y_ref[rows, :].astype(F32))
            q = ny * pg_ref[...]
            e = x_ref[rows, :] + gate_ref[...] * q - t_ref[rows, :]
            dx2 = e * (1.0 / d)
            dx_ref[rows, :] = dx2
            dq = dx2 * gate_ref[...]
            dny = dq * pg_ref[...]
            dy_ref[rows, :] = _bf(ry * (dny - ny * jnp.mean(dny * ny, axis=-1, keepdims=True)))
            return sums[0] + _fold8(e * e), sums[1] + _fold8(dx2 * q), sums[2] + _fold8(dq * ny)

        zero = jnp.zeros((8, d), F32)
        sq, dgate, dpg = lax.fori_loop(0, tm // STRIP, strip, (zero, zero, zero))
        _acc(loss_ref, first, jnp.full((1, 128), 0.5 / d, F32) * jnp.sum(sq))
        _acc(dgate_ref, first, _colsum(dgate))
        _acc(dpg_ref, first, _colsum(dpg))

    v = _vec_spec(d)
    return pl.pallas_call(
        body, name=name, grid=(l // tm,),
        in_specs=[_row_spec(tm, d), _row_spec(tm, d), v, v, _row_spec(tm, d)],
        out_specs=[_vec_spec(128), _row_spec(tm, d), _row_spec(tm, d), v, v],
        out_shape=[jax.ShapeDtypeStruct((1, 128), F32), jax.ShapeDtypeStruct((l, d), BF16),
                   jax.ShapeDtypeStruct((l, d), F32), jax.ShapeDtypeStruct((1, d), F32),
                   jax.ShapeDtypeStruct((1, d), F32)],
        compiler_params=_params(("arbitrary",)),
    )(x1, y1, gate, pg, target)


def _pre_bwd(dh, dres, x, g, scale, name, post=None):
    l, d = x.shape
    tm = _tile(l, 256)
    with_post = post is not None

    def body(*refs):
        if with_post:
            (dh_ref, dres_ref, x_ref, g_ref, sc_ref, y_ref, gate_ref, pg_ref,
             dx_ref, dsc_ref, dsh_ref, dg_ref, dy_ref, dgate_ref, dpg_ref) = refs
        else:
            dh_ref, dres_ref, x_ref, g_ref, sc_ref, dx_ref, dsc_ref, dsh_ref, dg_ref = refs
        first = pl.program_id(0) == 0

        def strip(s, sums):
            rows = pl.ds(pl.multiple_of(s * STRIP, STRIP), STRIP)
            dh = dh_ref[rows, :].astype(F32)
            n, r = _rownorm(x_ref[rows, :])
            dyn = dh * (1.0 + sc_ref[...])
            dn = dyn * g_ref[...]
            dx = dres_ref[rows, :] + r * (dn - n * jnp.mean(dn * n, axis=-1, keepdims=True))
            dx_ref[rows, :] = dx
            new = [sums[0] + _fold8(dh * (n * g_ref[...])), sums[1] + _fold8(dh), sums[2] + _fold8(dyn * n)]
            if with_post:
                ny, ry = _rownorm(y_ref[rows, :].astype(F32))
                dq = dx * gate_ref[...]
                dny = dq * pg_ref[...]
                dy_ref[rows, :] = _bf(ry * (dny - ny * jnp.mean(dny * ny, axis=-1, keepdims=True)))
                new += [sums[3] + _fold8(dx * (ny * pg_ref[...])), sums[4] + _fold8(dq * ny)]
            return tuple(new)

        zero = jnp.zeros((8, d), F32)
        sums = lax.fori_loop(0, tm // STRIP, strip, (zero,) * (5 if with_post else 3))
        outs = [dsc_ref, dsh_ref, dg_ref] + ([dgate_ref, dpg_ref] if with_post else [])
        for ref, acc in zip(outs, sums):
            _acc(ref, first, _colsum(acc))

    v = _vec_spec(d)
    row = _row_spec(tm, d)
    vec_out = jax.ShapeDtypeStruct((1, d), F32)
    in_specs = [row, row, row, v, v]
    args = [dh, dres, x, g, scale]
    out_specs = [row, v, v, v]
    out_shape = [jax.ShapeDtypeStruct((l, d), F32), vec_out, vec_out, vec_out]
    if with_post:
        in_specs += [row, v, v]
        args += list(post)
        out_specs += [row, v, v]
        out_shape += [jax.ShapeDtypeStruct((l, d), BF16), vec_out, vec_out]
    return pl.pallas_call(
        body, name=name, grid=(l // tm,), in_specs=in_specs, out_specs=out_specs, out_shape=out_shape,
        compiler_params=_params(("arbitrary",)),
    )(*args)


def _softplus_parts(z):
    e = jnp.exp(-jnp.abs(z))
    den = 1.0 + e
    lb = jnp.minimum(z, 0.0) - jnp.log(den)
    return lb, lb - z, jnp.exp(lb)


def _tri(cmp, n=HEAD):
    row = lax.broadcasted_iota(jnp.int32, (n, n), 0)
    col = lax.broadcasted_iota(jnp.int32, (n, n), 1)
    return cmp(row, col)


ATT_T = 256
ATT_DEAD = 104.0


def _any_alive(runs):
    return functools.reduce(jnp.maximum, [jnp.max(r) for r in runs]) > -ATT_DEAD


def _attn_fwd(qkv, wb, name, hp=4, ride=None):
    l = qkv.shape[0]
    t = ATT_T
    nh, nq = wb // HEAD, l // t
    hp = min(hp, nh)
    ng, wg = nh // hp, hp * HEAD
    scale = 1.0 / math.sqrt(HEAD)

    def body(q_ref, k_ref, v_ref, o_ref):
        i = pl.program_id(1)
        valid = _tri(lambda r, c: c < r, t)
        m_gt = _bf(_tri(lambda r, c: r > c, t).astype(F32))

        def tile(j, carry, diag):
            rows = pl.ds(pl.multiple_of(j * t, t), t)
            cols = [slice(hh * HEAD, (hh + 1) * HEAD) for hh in range(hp)]
            zs = [_dot(q_ref[:, cs], k_ref[rows, cs], NT) * scale for cs in cols]
            lbs, lks = [], []
            for z in zs:
                lb, lk, _ = _softplus_parts(z)
                lbs.append(lb)
                lks.append(jnp.where(valid, lk, 0.0) if diag else lk)
            laters = [_dot(_bf(lk), m_gt) for lk in lks]
            ws = [jnp.exp(lb + later + run) for lb, later, (_, run) in zip(lbs, laters, carry)]
            if diag:
                ws = [jnp.where(valid, w, 0.0) for w in ws]
            return tuple((acc + _dot(_bf(w), v_ref[rows, cs]), run + jnp.sum(lk, axis=1, keepdims=True))
                         for w, lk, cs, (acc, run) in zip(ws, lks, cols, carry))

        zero = (jnp.zeros((t, HEAD), F32), jnp.zeros((t, 1), F32))
        carry = tile(i, (zero,) * hp, True)
        _, carry = lax.while_loop(lambda c: (c[0] < i) & _any_alive([run for _, run in c[1]]),
                                  lambda c: (c[0] + 1, tile(i - 1 - c[0], c[1], False)), (jnp.int32(0), carry))
        for hh, (acc, _) in enumerate(carry):
            o_ref[:, hh * HEAD:(hh + 1) * HEAD] = acc

    blk = lambda off: pl.BlockSpec((t, wg), lambda h, i: (i, off + h))
    full = lambda off: pl.BlockSpec((l, wg), lambda h, i: (0, off + h))
    out = pl.BlockSpec((t, wg), lambda h, i: (i, h))
    outs = _call(body, name=name, grid=(ng, nq), in_specs=[blk(0), full(ng), full(2 * ng)], out_specs=[out],
                 out_shape=[jax.ShapeDtypeStruct((l, wb), F32)],
                 args=(qkv, qkv, qkv), sem=("parallel", "arbitrary"), ride=ride)
    return outs[0], outs[1:]


def _attn_bwd(qkv, proj, dcat, wa, wb, name, hp=2, ride=None):
    l = qkv.shape[0]
    t = ATT_T
    nh, nq = wb // HEAD, l // t
    hp = min(hp, nh)
    ng, wg = nh // hp, hp * HEAD
    scale = 1.0 / math.sqrt(HEAD)

    def body(q_ref, k_ref, v_ref, bz_ref, dc_ref, dq_ref, dkt_out, dvt_out, do_s, qt_s, dot_s,
             dkt_ref, dvt_ref, out_sems):
        i = pl.program_id(1)

        @pl.when(i == 0)
        def _():
            dkt_ref[...] = jnp.zeros_like(dkt_ref)
            dvt_ref[...] = jnp.zeros_like(dvt_ref)

        do = dc_ref[...].astype(F32) * _silu(bz_ref[...].astype(F32))
        do_s[...] = _bf(do)
        for hh in range(hp):
            cs = slice(hh * HEAD, (hh + 1) * HEAD)
            qt_s[hh] = _bf(q_ref[:, cs].astype(F32).T * scale)
            dot_s[hh] = _bf(do[:, cs].T)
        valid = _tri(lambda r, c: c < r, t)
        m_le = _bf(_tri(lambda r, c: r <= c, t).astype(F32))
        m_lt = _bf(_tri(lambda r, c: r < c, t).astype(F32))

        heads = range(hp)
        cols = [slice(hh * HEAD, (hh + 1) * HEAD) for hh in heads]

        def row_sums(j, runs, diag):
            rows = pl.ds(pl.multiple_of(j * t, t), t)
            out = []
            for cs, run in zip(cols, runs):
                _, lk, _ = _softplus_parts(_dot(q_ref[:, cs], k_ref[rows, cs], NT) * scale)
                if diag:
                    lk = jnp.where(valid, lk, 0.0)
                out.append(run + jnp.sum(lk, axis=1, keepdims=True))
            return tuple(out)

        runs = row_sums(i, (jnp.zeros((t, 1), F32),) * hp, True)
        below, lktot = lax.while_loop(lambda c: (c[0] < i) & _any_alive(c[1]),
                                      lambda c: (c[0] + 1, row_sums(i - 1 - c[0], c[1], False)), (jnp.int32(0), runs))

        def tile(j, carry, diag):
            rows = pl.ds(pl.multiple_of(j * t, t), t)
            zs = [_dot(q_ref[:, cs], k_ref[rows, cs], NT) * scale for cs in cols]
            dws = [_dot(do_s[:, cs], v_ref[rows, cs], NT) for cs in cols]
            lbs, lks, sigs = [], [], []
            for z in zs:
                lb, lk, sig = _softplus_parts(z)
                lbs.append(lb)
                lks.append(jnp.where(valid, lk, 0.0) if diag else lk)
                sigs.append(sig)
            pins = [_dot(_bf(lk), m_le) for lk in lks]
            ws = [jnp.exp(lbs[hh] + (lktot[hh] - carry[hh][1]) - pins[hh]) for hh in heads]
            if diag:
                ws = [jnp.where(valid, w, 0.0) for w in ws]
            das = [dw * w for dw, w in zip(dws, ws)]
            pexs = [_dot(_bf(da), m_lt) for da in das]
            dzs = [das[hh] - sigs[hh] * (das[hh] + carry[hh][2] + pexs[hh]) for hh in heads]
            if diag:
                dzs = [jnp.where(valid, dz, 0.0) for dz in dzs]
            dzs = [_bf(dz) for dz in dzs]
            out = []
            for hh in heads:
                dkt, dvt = _dot(qt_s[hh], dzs[hh]), _dot(dot_s[hh], _bf(ws[hh]))
                for half in range(t // HEAD):
                    dkt_ref[hh, sub * j + half] += dkt[:, half * HEAD:(half + 1) * HEAD]
                    dvt_ref[hh, sub * j + half] += dvt[:, half * HEAD:(half + 1) * HEAD]
                dq, cpre, ppre = carry[hh]
                out.append((dq + _dot(dzs[hh], k_ref[rows, cols[hh]]), cpre + jnp.sum(lks[hh], axis=1, keepdims=True),
                            ppre + pexs[hh][:, t - 1:] + das[hh][:, t - 1:]))
            return tuple(out)

        zero = (jnp.zeros((t, HEAD), F32), jnp.zeros((t, 1), F32), jnp.zeros((t, 1), F32))
        carry = lax.fori_loop(i - below, i, lambda j, c: tile(j, c, False), (zero,) * hp)
        carry = tile(i, carry, True)
        for hh in range(hp):
            dq_ref[:, hh * HEAD:(hh + 1) * HEAD] = carry[hh][0] * scale

        @pl.when(i == nq - 1)
        def _():
            heads = pl.ds(pl.program_id(0) * hp, hp)
            copies = [pltpu.make_async_copy(dkt_ref, dkt_out.at[heads], out_sems.at[0]),
                      pltpu.make_async_copy(dvt_ref, dvt_out.at[heads], out_sems.at[1])]
            for cp in copies:
                cp.start()
            for cp in copies:
                cp.wait()

    sub = t // HEAD
    blk = lambda off: pl.BlockSpec((t, wg), lambda h, i: (i, off + h))
    full = lambda off: pl.BlockSpec((l, wg), lambda h, i: (0, off + h))
    acc_shape = jax.ShapeDtypeStruct((nh, l // HEAD, HEAD, HEAD), F32)
    acc_scratch = pltpu.VMEM((hp, l // HEAD, HEAD, HEAD), F32)
    outs = _call(
        body, name=name, grid=(ng, nq),
        in_specs=[blk(0), full(ng), full(2 * ng), blk(3 * wa // wg), blk(wa // wg)],
        out_specs=[blk(0), ANY, ANY], out_shape=[jax.ShapeDtypeStruct((l, wb), F32), acc_shape, acc_shape],
        scratch_shapes=[pltpu.VMEM((t, wg), BF16), pltpu.VMEM((hp, HEAD, t), BF16), pltpu.VMEM((hp, HEAD, t), BF16),
                        acc_scratch, acc_scratch, pltpu.SemaphoreType.DMA((2,))],
        args=(qkv, qkv, qkv, proj, dcat), sem=("parallel", "arbitrary"), ride=ride)
    return outs[0], outs[1], outs[2], outs[3:]


def _sgu_heads(v, g_ref, w_ref, bt_ref, nh):
    keep = _tri(lambda r, c: r >= c)
    out = []
    for h in range(nh):
        cols = slice(h * HEAD, (h + 1) * HEAD)
        nv, r = _rownorm(v[:, cols])
        wm = jnp.where(keep, w_ref[h], 0.0)
        s = _dot(_bf(wm), _bf(nv * g_ref[:, cols])) + bt_ref[:, h:h + 1]
        out.append((nv, r, wm, s))
    return out


def _sgu_fwd(proj, out_b, norm_g, sgu_w, sgu_bt, wa, wb, name):
    l, n = proj.shape
    nh = wa // HEAD

    def body(au_ref, av_ref, az_ref, bz_ref, ob_ref, g_ref, w_ref, bt_ref, cat_ref):
        u, v, sz = _gelu(au_ref[...].astype(F32)), _gelu(av_ref[...].astype(F32)), _silu(az_ref[...].astype(F32))
        for h, (_, _, _, s) in enumerate(_sgu_heads(v, g_ref, w_ref, bt_ref, nh)):
            cols = slice(h * HEAD, (h + 1) * HEAD)
            cat_ref[:, cols] = _bf(u[:, cols] * s * sz[:, cols])
        cat_ref[:, wa:] = _bf(ob_ref[...] * _silu(bz_ref[...].astype(F32)))

    a_blk = lambda j: pl.BlockSpec((HEAD, wa), lambda i: (i, j))
    return pl.pallas_call(
        body, name=name, grid=(l // HEAD,),
        in_specs=[a_blk(0), a_blk(1), a_blk(2), a_blk(3), pl.BlockSpec((HEAD, wb), lambda i: (i, 0)),
                  _vec_spec(wa), pl.BlockSpec((nh, HEAD, HEAD), lambda i: (0, 0, 0)),
                  pl.BlockSpec((HEAD, nh), lambda i: (0, 0))],
        out_specs=pl.BlockSpec((HEAD, wa + wb), lambda i: (i, 0)),
        out_shape=jax.ShapeDtypeStruct((l, wa + wb), BF16),
        compiler_params=_params(("parallel",)),
    )(proj, proj, proj, proj, out_b, norm_g, sgu_w, sgu_bt)


def _sgu_bwd(proj, out_b, dcat, dq, dk, dv, norm_g, sgu_w, sgu_bt, wa, wb, name, ride=None):
    l = proj.shape[0]
    n = 3 * wa + 4 * wb
    nh = wa // HEAD

    def body(au_ref, av_ref, az_ref, bz_ref, ob_ref, dc_ref, dq_ref, dk_ref, dv_ref, g_ref, w_ref, wt_ref, bt_ref,
             dp_ref, dw_ref, dbt_ref, dg_ref):
        first = pl.program_id(0) == 0
        keep = _tri(lambda r, c: r >= c)
        au, av, az = au_ref[...].astype(F32), av_ref[...].astype(F32), az_ref[...].astype(F32)
        u, v, sz = _gelu(au), _gelu(av), _silu(az)
        dgelu_u, dgelu_v, dsilu_z = _gelu_grad(au), _gelu_grad(av), _silu_grad(az)
        heads = _sgu_heads(v, g_ref, w_ref, bt_ref, nh)
        cols = [slice(h * HEAD, (h + 1) * HEAD) for h in range(nh)]
        dss = []
        for h, (nv, r, wm, s) in enumerate(heads):
            dca, uh, szh = dc_ref[:, cols[h]].astype(F32), u[:, cols[h]], sz[:, cols[h]]
            dp_ref[:, cols[h]] = _bf(dca * s * szh * dgelu_u[:, cols[h]])
            dp_ref[:, 2 * wa + h * HEAD:2 * wa + (h + 1) * HEAD] = _bf(dca * uh * s * dsilu_z[:, cols[h]])
            dss.append(dca * uh * szh)
        dws = [_dot(_bf(ds), _bf(nv * g_ref[:, cs]), NT) for ds, cs, (nv, _, _, _) in zip(dss, cols, heads)]
        keep_t = _tri(lambda r, c: r <= c)
        dvhs = [_dot(_bf(jnp.where(keep_t, wt_ref[h], 0.0)), _bf(dss[h])) for h in range(nh)]
        dg_parts = []
        for h, (nv, r, wm, s) in enumerate(heads):
            _acc(dw_ref.at[h], first, jnp.where(keep, dws[h], 0.0))
            _acc(dbt_ref.at[:, h:h + 1], first, jnp.sum(dss[h], axis=1, keepdims=True))
            dg_parts.append(_colsum(dvhs[h] * nv))
            dnv = dvhs[h] * g_ref[:, cols[h]]
            dvv = r * (dnv - nv * jnp.mean(dnv * nv, axis=-1, keepdims=True))
            dp_ref[:, wa + h * HEAD:wa + (h + 1) * HEAD] = _bf(dvv * dgelu_v[:, cols[h]])
        _acc(dg_ref, first, jnp.concatenate(dg_parts, axis=1))
        base = 3 * wa
        dp_ref[:, base:base + wb] = _bf(dq_ref[...])
        for h in range(wb // HEAD):
            dp_ref[:, base + wb + h * HEAD:base + wb + (h + 1) * HEAD] = _bf(dk_ref[h, 0].T)
            dp_ref[:, base + 2 * wb + h * HEAD:base + 2 * wb + (h + 1) * HEAD] = _bf(dv_ref[h, 0].T)
        dp_ref[:, base + 3 * wb:] = _bf(dc_ref[:, wa:].astype(F32) * ob_ref[...]
                                        * _silu_grad(bz_ref[...].astype(F32)))

    a_blk = lambda j: pl.BlockSpec((HEAD, wa), lambda i: (i, j))
    b_blk = pl.BlockSpec((HEAD, wb), lambda i: (i, 0))
    t_blk = pl.BlockSpec((wb // HEAD, 1, HEAD, HEAD), lambda i: (0, i, 0, 0))
    w_spec = pl.BlockSpec((nh, HEAD, HEAD), lambda i: (0, 0, 0))
    bt_spec = pl.BlockSpec((HEAD, nh), lambda i: (0, 0))
    outs = _call(
        body, name=name, grid=(l // HEAD,), ride=ride, sem=("arbitrary",),
        in_specs=[a_blk(0), a_blk(1), a_blk(2), a_blk(3), b_blk, pl.BlockSpec((HEAD, wa + wb), lambda i: (i, 0)),
                  b_blk, t_blk, t_blk, _vec_spec(wa), w_spec, w_spec, bt_spec],
        out_specs=[pl.BlockSpec((HEAD, n), lambda i: (i, 0)), w_spec, bt_spec, _vec_spec(wa)],
        out_shape=[jax.ShapeDtypeStruct((l, n), BF16), jax.ShapeDtypeStruct((nh, HEAD, HEAD), F32),
                   jax.ShapeDtypeStruct((HEAD, nh), F32), jax.ShapeDtypeStruct((1, wa), F32)],
        args=(proj, proj, proj, proj, out_b, dcat, dq, dk, dv, norm_g, sgu_w, sgu_w.transpose(0, 2, 1), sgu_bt))
    return (*outs[:4], outs[4:])


def _ssm_discretise(lr, li, ldt, br, bi):
    dt = jnp.exp(ldt)
    mag = jnp.exp(lr * dt)
    a_re = mag * jnp.cos(li * dt)
    a_im = mag * jnp.sin(li * dt)
    den = lr * lr + li * li
    nr = a_re - 1.0
    coef_re = (nr * lr + a_im * li) / den
    coef_im = (a_im * lr - nr * li) / den
    return a_re, a_im, coef_re * br - coef_im * bi, coef_re * bi + coef_im * br


def _ssm_prep(lr, li, ldt, br, bi, lr_row, li_row, ldt_row, name):
    s, c = br.shape

    def body(lr_ref, li_ref, ldt_ref, br_ref, bi_ref, lrr_ref, lir_ref, ldtr_ref, bbr_ref, bbi_ref, tr_ref, ti_ref):
        _, _, bbr, bbi = _ssm_discretise(lr_ref[...], li_ref[...], ldt_ref[...], br_ref[...], bi_ref[...])
        bbr_ref[...] = bbr
        bbi_ref[...] = bbi
        row = lax.broadcasted_iota(jnp.int32, (SCAN_ROWS, 1), 0)
        blk, r = jnp.right_shift(row, 3), jnp.bitwise_and(row, 7)
        kind, rev = jnp.bitwise_and(blk, 3), blk >= 4
        step = jnp.left_shift(1, kind)
        n = jnp.where(kind < 3, step, jnp.where(rev, 8 - r, r + 1)).astype(F32)
        keep = (kind == 3) | (rev & (r < 8 - step)) | (jnp.logical_not(rev) & (r >= step))
        dt = jnp.exp(ldtr_ref[...])
        mag = jnp.exp(n * (lrr_ref[...] * dt))
        ang = n * (lir_ref[...] * dt)
        tr_ref[...] = jnp.where(keep, mag * jnp.cos(ang), 0.0)
        ti_ref[...] = jnp.where(keep, jnp.where(rev, -1.0, 1.0) * mag * jnp.sin(ang), 0.0)

    col = jax.ShapeDtypeStruct((s, c), F32)
    row = jax.ShapeDtypeStruct((SCAN_ROWS, s), F32)
    return pl.pallas_call(body, name=name, out_shape=[col, col, row, row])(
        lr, li, ldt, br, bi, lr_row, li_row, ldt_row)


def _ssm_prep_bwd(lr, li, ldt, br, bi, da_re, da_im, dbb_re, dbb_im, p, name):
    s, c = br.shape

    def body(lr_ref, li_ref, ldt_ref, br_ref, bi_ref, dar_ref, dai_ref, dbr_ref, dbi_ref,
             dlr_ref, dli_ref, dldt_ref, dbre_ref, dbim_ref):
        args = (lr_ref[...], li_ref[...], ldt_ref[...], br_ref[...], bi_ref[...])
        _, vjp = jax.vjp(_ssm_discretise, *args)
        dlr, dli, dldt, dbr, dbi = vjp((dar_ref[...], dai_ref[...], dbr_ref[...], dbi_ref[...]))
        dlr_ref[...] = dlr
        dli_ref[...] = dli
        dbre_ref[...] = dbr
        dbim_ref[...] = dbi
        idx = lax.broadcasted_iota(jnp.int32, (s, s // p), 0)
        grp = lax.broadcasted_iota(jnp.int32, (s, s // p), 1)
        own = (idx >= grp * p) & (idx < (grp + 1) * p)
        dldt_ref[...] = _colsum(jnp.where(own, dldt, 0.0))

    col1 = jax.ShapeDtypeStruct((s, 1), F32)
    colc = jax.ShapeDtypeStruct((s, c), F32)
    return pl.pallas_call(
        body, name=name, out_shape=[col1, col1, jax.ShapeDtypeStruct((1, s // p), F32), colc, colc],
    )(lr, li, ldt, br, bi, da_re, da_im, dbb_re, dbb_im)


SCAN_ROWS = 64


def _scan_groups(xr, xi, tr_ref, ti_ref, cr, ci, reverse):
    ng = xr.shape[0] // 8
    base = SCAN_ROWS // 2 if reverse else 0
    pr, pi = tr_ref[base + 24:base + 32, :], ti_ref[base + 24:base + 32, :]
    edge = slice(0, 1) if reverse else slice(7, 8)
    out_r, out_i = [None] * ng, [None] * ng
    for g in (range(ng - 1, -1, -1) if reverse else range(ng)):
        sr, si = xr[8 * g:8 * g + 8, :], xi[8 * g:8 * g + 8, :]
        for k in range(3):
            ar, ai = tr_ref[base + 8 * k:base + 8 * k + 8, :], ti_ref[base + 8 * k:base + 8 * k + 8, :]
            shift = 8 - (1 << k) if reverse else 1 << k
            rr, ri = pltpu.roll(sr, shift, 0), pltpu.roll(si, shift, 0)
            sr, si = sr + ar * rr - ai * ri, si + ar * ri + ai * rr
        sr, si = sr + pr * cr - pi * ci, si + pr * ci + pi * cr
        cr, ci = sr[edge, :], si[edge, :]
        out_r[g], out_i[g] = sr, si
    return jnp.concatenate(out_r, axis=0), jnp.concatenate(out_i, axis=0), cr, ci


def _ssm_fwd(proj, bbd, ccd, pw_re, pw_im, d_skip, w, name):
    l = proj.shape[0]
    nb, cw, ns2 = bbd.shape
    ns = ns2 // 2
    nc = l // SSM_T

    def body(u_ref, bbd_ref, ccd_ref, pr_ref, pi_ref, d_ref, y_ref, hsr_ref, hsi_ref, h_ref, hr_s, hi_s):
        @pl.when(pl.program_id(1) == 0)
        def _():
            hr_s[...] = jnp.zeros_like(hr_s)
            hi_s[...] = jnp.zeros_like(hi_s)

        hsr_ref[...] = hr_s[...].reshape(hsr_ref.shape)
        hsi_ref[...] = hi_s[...].reshape(hsi_ref.shape)
        u = u_ref[...]
        bu = _dot(_bf(u), bbd_ref[0])
        hr, hi, cr, ci = _scan_groups(bu[:, :ns], bu[:, ns:], pr_ref, pi_ref, hr_s[...], hi_s[...], False)
        hr_s[...] = cr
        hi_s[...] = ci
        h_bf = _bf(jnp.concatenate([hr, hi], axis=1))
        h_ref[...] = h_bf
        y_ref[...] = _dot(h_bf, ccd_ref[0]) + d_ref[...] * u

    tab = pl.BlockSpec((SCAN_ROWS, ns), lambda b, k: (0, b))
    return pl.pallas_call(
        body, name=name, grid=(nb, nc),
        in_specs=[pl.BlockSpec((SSM_T, cw), lambda b, k: (k, b)),
                  pl.BlockSpec((1, cw, ns2), lambda b, k: (b, 0, 0)),
                  pl.BlockSpec((1, ns2, cw), lambda b, k: (b, 0, 0)),
                  tab, tab, pl.BlockSpec((1, cw), lambda b, k: (0, b))],
        out_specs=[pl.BlockSpec((SSM_T, cw), lambda b, k: (k, b)),
                   pl.BlockSpec((1, 1, ns), lambda b, k: (k, 0, b)), pl.BlockSpec((1, 1, ns), lambda b, k: (k, 0, b)),
                   pl.BlockSpec((SSM_T, ns2), lambda b, k: (k, b))],
        out_shape=[jax.ShapeDtypeStruct((l, w), F32), jax.ShapeDtypeStruct((nc, 1, nb * ns), F32),
                   jax.ShapeDtypeStruct((nc, 1, nb * ns), F32), jax.ShapeDtypeStruct((l, nb * ns2), BF16)],
        scratch_shapes=[pltpu.VMEM((1, ns), F32), pltpu.VMEM((1, ns), F32)],
        compiler_params=_params(("parallel", "arbitrary")),
    )(proj, bbd, ccd, pw_re, pw_im, d_skip)


def _ssm_bwd(proj, dy, hs_re, hs_im, h_all, bbd, ccd, pw_re, pw_im, d_skip, w, name, ride=None):
    l = proj.shape[0]
    nb, cw, ns2 = bbd.shape
    ns = ns2 // 2
    nc = l // SSM_T

    def body(u_ref, dy_ref, hsr_ref, hsi_ref, h_ref, bbd_ref, ccd_ref, pr_ref, pi_ref, d_ref,
             du_ref, dbbd_ref, dccd_ref, dar_ref, dai_ref, dd_ref, gr_s, gi_s):
        first = pl.program_id(1) == 0

        @pl.when(first)
        def _():
            gr_s[...] = jnp.zeros_like(gr_s)
            gi_s[...] = jnp.zeros_like(gi_s)

        u, dy = u_ref[...], dy_ref[...]
        dy_bf = _bf(dy)
        hr0, hi0 = hsr_ref[0], hsi_ref[0]
        h = h_ref[...].astype(F32)
        hr, hi = h[:, :ns], h[:, ns:]
        dh = _dot(dy_bf, ccd_ref[0], NT)
        gr, gi, gcr, gci = _scan_groups(dh[:, :ns], dh[:, ns:], pr_ref, pi_ref, gr_s[...], gi_s[...], True)
        gr_s[...] = gcr
        gi_s[...] = gci
        row0 = lax.broadcasted_iota(jnp.int32, hr.shape, 0) == 0
        pr_h = jnp.where(row0, hr0, pltpu.roll(hr, 1, 0))
        pi_h = jnp.where(row0, hi0, pltpu.roll(hi, 1, 0))
        _acc(dar_ref, first, _colsum(pr_h * gr + pi_h * gi))
        _acc(dai_ref, first, _colsum(pr_h * gi - pi_h * gr))
        g_bf = _bf(jnp.concatenate([gr, gi], axis=1))
        _acc(dbbd_ref.at[0], first, _dot(_bf(u.T), g_bf))
        _acc(dccd_ref.at[0], first, _dot(_bf(h.T), dy_bf))
        du_ref[...] = _bf(_dot(g_bf, bbd_ref[0], NT) + d_ref[...] * dy)
        _acc(dd_ref, first, _colsum(dy * u))

    rev = lambda b, k: (nc - 1 - k, b)
    outs = _call(
        body, name=name, grid=(nb, nc), ride=ride, sem=("parallel", "arbitrary"),
        args=(proj, dy, hs_re, hs_im, h_all, bbd, ccd, pw_re, pw_im, d_skip),
        in_specs=[pl.BlockSpec((SSM_T, cw), rev), pl.BlockSpec((SSM_T, cw), rev),
                  pl.BlockSpec((1, 1, ns), lambda b, k: (nc - 1 - k, 0, b)),
                  pl.BlockSpec((1, 1, ns), lambda b, k: (nc - 1 - k, 0, b)),
                  pl.BlockSpec((SSM_T, ns2), rev),
                  pl.BlockSpec((1, cw, ns2), lambda b, k: (b, 0, 0)),
                  pl.BlockSpec((1, ns2, cw), lambda b, k: (b, 0, 0)),
                  pl.BlockSpec((SCAN_ROWS, ns), lambda b, k: (0, b)), pl.BlockSpec((SCAN_ROWS, ns), lambda b, k: (0, b)),
                  pl.BlockSpec((1, cw), lambda b, k: (0, b))],
        out_specs=[pl.BlockSpec((SSM_T, cw), rev),
                   pl.BlockSpec((1, cw, ns2), lambda b, k: (b, 0, 0)),
                   pl.BlockSpec((1, ns2, cw), lambda b, k: (b, 0, 0)),
                   pl.BlockSpec((1, ns), lambda b, k: (0, b)), pl.BlockSpec((1, ns), lambda b, k: (0, b)),
                   pl.BlockSpec((1, cw), lambda b, k: (0, b))],
        out_shape=[jax.ShapeDtypeStruct((l, w), BF16), jax.ShapeDtypeStruct(bbd.shape, F32),
                   jax.ShapeDtypeStruct(ccd.shape, F32), jax.ShapeDtypeStruct((1, nb * ns), F32),
                   jax.ShapeDtypeStruct((1, nb * ns), F32), jax.ShapeDtypeStruct((1, w), F32)],
        scratch_shapes=[pltpu.VMEM((1, ns), F32), pltpu.VMEM((1, ns), F32)])
    return (*outs[:6], outs[6:])


def _block_diag_b(bb_re, bb_im, g, p, c):
    nb = g // SSM_GB
    keep = _same_group(SSM_GB * c, c, SSM_GB * p, p)

    def one(bb):
        t = bb.reshape(nb, SSM_GB, p, c).transpose(0, 1, 3, 2).reshape(nb, SSM_GB * c, p)
        return jnp.where(keep, jnp.tile(t, (1, 1, SSM_GB)), 0.0)

    return jnp.concatenate([one(bb_re), one(bb_im)], axis=2)


def _same_group(rows, per_row, cols, per_col):
    r = lax.broadcasted_iota(jnp.int32, (rows, cols), 0) // per_row
    q = lax.broadcasted_iota(jnp.int32, (rows, cols), 1) // per_col
    return r == q


def _block_diag_c(c_re, c_im, g, p, c):
    nb = g // SSM_GB
    keep = _same_group(SSM_GB * p, p, SSM_GB * c, c)

    def one(cc):
        t = cc.reshape(nb, SSM_GB, c, p).transpose(0, 1, 3, 2).reshape(nb, SSM_GB * p, c)
        return jnp.where(keep, jnp.tile(t, (1, 1, SSM_GB)), 0.0)

    return jnp.concatenate([one(c_re), one(-c_im)], axis=1)


def _diag_of_b(dbbd, g, p, c):
    nb = g // SSM_GB
    keep = _same_group(SSM_GB * c, c, SSM_GB * p, p)

    def one(blk):
        d = jnp.where(keep, blk, 0.0).reshape(nb, SSM_GB * c, SSM_GB, p).sum(axis=2)
        return d.reshape(nb, SSM_GB, c, p).transpose(0, 1, 3, 2).reshape(g * p, c)

    half = SSM_GB * p
    return one(dbbd[:, :, :half]), one(dbbd[:, :, half:])


def _diag_of_c(dccd, g, p, c):
    nb = g // SSM_GB
    keep = _same_group(SSM_GB * p, p, SSM_GB * c, c)

    def one(blk):
        d = jnp.where(keep, blk, 0.0).reshape(nb, SSM_GB * p, SSM_GB, c).sum(axis=2)
        return d.reshape(nb, SSM_GB, p, c).transpose(0, 1, 3, 2).reshape(g, c, p)

    half = SSM_GB * p
    return one(dccd[:, :half]), -one(dccd[:, half:])


def _glu_fwd(y, proj, w_glu, b_glu, name):
    l, w = y.shape
    tm = _tile(l, 256)

    def body(y_ref, z_ref, w_ref, b_ref, o_ref):
        g = _gelu(y_ref[...])
        t = _dot(_bf(g), w_ref[...]) + b_ref[...]
        o_ref[...] = _bf(g * _sigmoid(t) * _silu(z_ref[...]))

    return pl.pallas_call(
        body, name=name, grid=(l // tm,),
        in_specs=[_row_spec(tm, w), pl.BlockSpec((tm, w), lambda i: (i, 1)),
                  pl.BlockSpec((w, w), lambda i: (0, 0)), _vec_spec(w)],
        out_specs=_row_spec(tm, w), out_shape=jax.ShapeDtypeStruct((l, w), BF16),
        compiler_params=_params(("parallel",)),
    )(y, proj, w_glu, b_glu)


def _glu_bwd(do, y, proj, w_glu, b_glu, name):
    l, w = y.shape
    tm = _tile(l, 512)
    nsteps = l // tm

    def body(do_ref, y_ref, z_ref, w_ref, b_ref, dy_ref, dz_ref, dw_ref, db_ref, dw_acc):
        i = pl.program_id(0)
        first = i == 0
        yv, z, do = y_ref[...], z_ref[...], do_ref[...].astype(F32)
        g = _gelu(yv)
        g_bf = _bf(g)
        sg = _sigmoid(_dot(g_bf, w_ref[...]) + b_ref[...])
        dyy = do * _silu(z)
        dz_ref[...] = _bf(do * g * sg * _silu_grad(z))
        dt = dyy * g * sg * (1.0 - sg)
        dt_bf = _bf(dt)
        dg = dyy * sg + _dot(dt_bf, w_ref[...], NT)
        dy_ref[...] = dg * _gelu_grad(yv)
        _acc(dw_acc, first, _dot(_bf(g.T), dt_bf))
        _acc(db_ref, first, _colsum(dt))

        @pl.when(i == nsteps - 1)
        def _():
            dw_ref[...] = _bf(dw_acc[...])

    return pl.pallas_call(
        body, name=name, grid=(nsteps,),
        in_specs=[_row_spec(tm, w), _row_spec(tm, w), pl.BlockSpec((tm, w), lambda i: (i, 1)),
                  pl.BlockSpec((w, w), lambda i: (0, 0)), _vec_spec(w)],
        out_specs=[_row_spec(tm, w), _row_spec(tm, w), pl.BlockSpec((w, w), lambda i: (0, 0)), _vec_spec(w)],
        out_shape=[jax.ShapeDtypeStruct((l, w), F32), jax.ShapeDtypeStruct((l, w), BF16),
                   jax.ShapeDtypeStruct((w, w), BF16), jax.ShapeDtypeStruct((1, w), F32)],
        scratch_shapes=[pltpu.VMEM((w, w), F32)],
        compiler_params=_params(("arbitrary",)),
    )(do, y, proj, w_glu, b_glu)


MOD_ROWS = 128


def _mod_fwd(cond_pad, w_mod, b_shard, name):
    nl, d, ncol = w_mod.shape
    tn = _tile(ncol, 512)

    def body(c_ref, w_ref, b_ref, o_ref):
        o_ref[0] = _dot(_bf(c_ref[...]), _bf(w_ref[0])) + b_ref[0]

    return pl.pallas_call(
        body, name=name, grid=(nl, ncol // tn),
        in_specs=[pl.BlockSpec((MOD_ROWS, d), lambda a, j: (0, 0)),
                  pl.BlockSpec((1, d, tn), lambda a, j: (a, 0, j)),
                  pl.BlockSpec((1, 1, tn), lambda a, j: (a, 0, j))],
        out_specs=pl.BlockSpec((1, MOD_ROWS, tn), lambda a, j: (a, 0, j)),
        out_shape=jax.ShapeDtypeStruct((nl, MOD_ROWS, ncol), F32),
        compiler_params=_params(("parallel", "parallel")),
    )(cond_pad, w_mod, b_shard)


def _mod_bwd(cond_pad_t, dmod_pad, name):
    nl, _, ncol = dmod_pad.shape
    d = cond_pad_t.shape[0]
    tn = _tile(ncol, 512)

    def body(c_ref, dm_ref, o_ref):
        o_ref[0] = _dot(_bf(c_ref[...]), _bf(dm_ref[0]))

    return pl.pallas_call(
        body, name=name, grid=(nl, ncol // tn),
        in_specs=[pl.BlockSpec((d, MOD_ROWS), lambda a, j: (0, 0)),
                  pl.BlockSpec((1, MOD_ROWS, tn), lambda a, j: (a, 0, j))],
        out_specs=pl.BlockSpec((1, d, tn), lambda a, j: (a, 0, j)),
        out_shape=jax.ShapeDtypeStruct((nl, d, ncol), F32),
        compiler_params=_params(("parallel", "parallel")),
    )(cond_pad_t, dmod_pad)


def _silu_rows(c2d, name):
    def body(c_ref, o_ref):
        o_ref[...] = _silu(c_ref[...])

    return pl.pallas_call(body, name=name, out_shape=jax.ShapeDtypeStruct(c2d.shape, F32))(c2d)


def _sum_leading(x, name):
    n, r, c = x.shape
    tr = _tile(r, max(16, (1 << 20) // (4 * c)), 16 if r % 16 == 0 else 8)

    def body(x_ref, o_ref):
        acc = x_ref[0].astype(F32)
        for k in range(1, n):
            acc = acc + x_ref[k].astype(F32)
        o_ref[...] = acc

    return pl.pallas_call(
        body, name=name, grid=(r // tr,),
        in_specs=[pl.BlockSpec((n, tr, c), lambda i: (0, i, 0))], out_specs=pl.BlockSpec((tr, c), lambda i: (i, 0)),
        out_shape=jax.ShapeDtypeStruct((r, c), F32), compiler_params=_params(("parallel",)),
    )(x)


def _adamw(w, gs, m, v, name, ride=None):
    r, c = w.shape
    tr = _tile(r, max(8, (3 << 19) // (4 * c)), 8)
    ng = len(gs)

    def body(*refs):
        w_ref, g_refs, m_ref, v_ref = refs[0], refs[1:1 + ng], refs[1 + ng], refs[2 + ng]
        g_ref, d_ref, nm_ref, nv_ref = refs[3 + ng:]
        g = g_refs[0][...]
        for extra in g_refs[1:]:
            g = g + extra[...]
        g_ref[...] = g
        d_ref[...], nm_ref[...], nv_ref[...] = _adamw_math(w_ref[...], g, m_ref[...], v_ref[...])

    spec = pl.BlockSpec((tr, c), lambda i: (i, 0))
    shp = jax.ShapeDtypeStruct((r, c), F32)
    outs = _call(body, name=name, grid=(r // tr,), in_specs=[spec] * (3 + ng), out_specs=[spec] * 4,
                 out_shape=[shp] * 4, args=(w, *gs, m, v), sem=("parallel",), ride=ride)
    return outs if ride is None else (outs[:4], outs[4:])


def _adamw_math(w, g, m, v):
    nm = ADAM_B1 * m + (1.0 - ADAM_B1) * g
    nv = ADAM_B2 * v + (1.0 - ADAM_B2) * (g * g)
    m_hat = nm / (1.0 - ADAM_B1 ** ADAM_STEP)
    v_hat = nv / (1.0 - ADAM_B2 ** ADAM_STEP)
    return -ADAM_LR * (m_hat / (jnp.sqrt(v_hat) + ADAM_EPS) + ADAM_WD * w), nm, nv


def _adamw_many(ws, gs, ms, vs, name):
    n = len(ws)

    def body(*refs):
        w_refs, g_refs, m_refs, v_refs = (refs[k * n:(k + 1) * n] for k in range(4))
        outs = refs[4 * n:]
        for i in range(n):
            outs[3 * i][...], outs[3 * i + 1][...], outs[3 * i + 2][...] = _adamw_math(
                w_refs[i][...], g_refs[i][...], m_refs[i][...], v_refs[i][...])

    out_shape = [jax.ShapeDtypeStruct(w.shape, F32) for w in ws for _ in range(3)]
    outs = pl.pallas_call(body, name=name, out_shape=out_shape, compiler_params=_params())(*ws, *gs, *ms, *vs)
    return [tuple(outs[3 * i:3 * i + 3]) for i in range(n)]


ANY = pl.BlockSpec(memory_space=pl.ANY)


def _flip(v, bit):
    return 1 - v if bit else v


def _allgather8_ops(x_ref, o_ref, send_sems, recv_sems, local_sem):
    mx, my, mc = lax.axis_index("x"), lax.axis_index("y"), lax.axis_index("c")
    me = 4 * mx + 2 * my + mc

    def mine():
        return pltpu.make_async_copy(x_ref, o_ref.at[me], local_sem)

    def copy(j, outgoing):
        peer = (_flip(mx, j & 4), _flip(my, j & 2), _flip(mc, j & 1))
        slot = me if outgoing else 4 * peer[0] + 2 * peer[1] + peer[2]
        return pltpu.make_async_remote_copy(
            src_ref=x_ref, dst_ref=o_ref.at[slot], send_sem=send_sems.at[j - 1], recv_sem=recv_sems.at[j - 1],
            device_id=peer, device_id_type=MESH)

    def start():
        mine().start()
        for j in range(1, 8):
            copy(j, True).start()

    def wait():
        for j in range(1, 8):
            copy(j, False).wait()
        mine().wait()

    return start, wait


def _ride_all8(x):
    return dict(xs=[x], shapes=[jax.ShapeDtypeStruct((8,) + x.shape, x.dtype)],
                sems=[pltpu.SemaphoreType.DMA((7,)), pltpu.SemaphoreType.DMA((7,)), pltpu.SemaphoreType.DMA],
                ops=lambda x_refs, o_refs, sems: _allgather8_ops(x_refs[0], o_refs[0], *sems))


def _ride_chip(xs, gather):
    return dict(xs=list(xs), shapes=_chip_exchange_shapes(xs, gather), sems=_chip_exchange_sems(len(xs)),
                ops=lambda x_refs, o_refs, sems: _chip_exchange_ops(x_refs, o_refs, *sems, gather))


def _allgather8(x, name):
    return _exchange([_ride_all8(x)], name)[0]


def _gather_halves_ops(x_ref, o_ref, ici_send, ici_recv, d2d_send, d2d_recv, local_sem):
    half = x_ref.shape[0] // 2
    quarter = half // 2
    mx, my, mc = lax.axis_index("x"), lax.axis_index("y"), lax.axis_index("c")
    k0, kx, ky, kd = 2 * mx + my, 2 * (1 - mx) + my, 2 * mx + (1 - my), 2 * (1 - mx) + (1 - my)
    x_nbr, y_nbr, sib = (1 - mx, my, mc), (mx, 1 - my, mc), (mx, my, 1 - mc)

    def rows(core, part):
        if part is None:
            return pl.ds(pl.multiple_of(core * half, 16), half)
        return pl.ds(pl.multiple_of(core * half + part * quarter, 16), quarter)

    def local():
        return pltpu.make_async_copy(x_ref, o_ref.at[k0], local_sem)

    def ici(n, outgoing):
        to = x_nbr if n in (0, 2) else y_nbr
        if n < 2:
            src = x_ref.at[rows(mc, None)]
            dst = o_ref.at[k0 if outgoing else (kx if n == 0 else ky), rows(mc, None)]
        else:
            part = n - 2
            src = o_ref.at[ky if n == 2 else kx, rows(mc, part)]
            dst = o_ref.at[(ky if n == 2 else kx) if outgoing else kd, rows(mc, part)]
        return pltpu.make_async_remote_copy(src_ref=src, dst_ref=dst, send_sem=ici_send.at[n], recv_sem=ici_recv.at[n],
                                            device_id=to, device_id_type=MESH)

    def d2d(n, outgoing):
        slot, part = ((kx, None), (ky, None), (kd, 0), (kd, 1))[n]
        piece = o_ref.at[slot, rows(mc if outgoing else 1 - mc, part)]
        return pltpu.make_async_remote_copy(src_ref=piece, dst_ref=piece, send_sem=d2d_send.at[n],
                                            recv_sem=d2d_recv.at[n], device_id=sib, device_id_type=MESH)

    def start():
        local().start()
        ici(0, True).start()
        ici(1, True).start()

    def wait():
        ici(1, False).wait_recv()
        ici(2, True).start()
        d2d(1, True).start()
        ici(0, False).wait_recv()
        ici(3, True).start()
        d2d(0, True).start()
        ici(2, False).wait_recv()
        d2d(2, True).start()
        ici(3, False).wait_recv()
        d2d(3, True).start()
        for n in range(4):
            ici(n, True).wait_send()
            d2d(n, True).wait_send()
            d2d(n, False).wait_recv()
        local().wait()

    return start, wait


def _ride_halves(x):
    dma4 = pltpu.SemaphoreType.DMA((4,))
    return dict(xs=[x], shapes=[jax.ShapeDtypeStruct((4,) + x.shape, x.dtype)],
                sems=[dma4, dma4, dma4, dma4, pltpu.SemaphoreType.DMA],
                ops=lambda x_refs, o_refs, sems: _gather_halves_ops(x_refs[0], o_refs[0], *sems))


def _exchange(rides, name):
    xs = [x for r in rides for x in r["xs"]]
    nx = len(xs)

    def body(*refs):
        x_refs, o_refs, sems = refs[:nx], refs[nx:2 * nx], refs[2 * nx:]
        ops, xo, so = [], 0, 0
        for r in rides:
            nr, ns = len(r["xs"]), len(r["sems"])
            ops.append(r["ops"](x_refs[xo:xo + nr], o_refs[xo:xo + nr], sems[so:so + ns]))
            xo, so = xo + nr, so + ns
        for start, _ in ops:
            start()
        for _, wait in ops:
            wait()

    return pl.pallas_call(body, name=name, in_specs=[ANY] * nx, out_specs=[ANY] * nx,
                          out_shape=[s for r in rides for s in r["shapes"]],
                          scratch_shapes=[s for r in rides for s in r["sems"]])(*xs)


def _chip_exchange_shapes(xs, gather):
    return [jax.ShapeDtypeStruct(((4,) + x.shape) if gather else x.shape, x.dtype) for x in xs]


def _chip_exchange_sems(n):
    return [pltpu.SemaphoreType.DMA((3 * n,)), pltpu.SemaphoreType.DMA((3 * n,)), pltpu.SemaphoreType.DMA((n,))]


def _chip_exchange_ops(x_refs, o_refs, send_sems, recv_sems, local_sems, gather):
    n = len(x_refs)
    mx, my, mc = lax.axis_index("x"), lax.axis_index("y"), lax.axis_index("c")
    k0 = 2 * mx + my

    def local(a):
        src = x_refs[a] if gather else x_refs[a].at[k0]
        return pltpu.make_async_copy(src, o_refs[a].at[k0], local_sems.at[a])

    def copy(a, j, outgoing):
        px, py = _flip(mx, j & 2), _flip(my, j & 1)
        kp = 2 * px + py
        if outgoing:
            src = x_refs[a] if gather else x_refs[a].at[kp]
            dst = o_refs[a].at[k0]
        else:
            src = x_refs[a] if gather else x_refs[a].at[k0]
            dst = o_refs[a].at[kp]
        s = a * 3 + j - 1
        return pltpu.make_async_remote_copy(
            src_ref=src, dst_ref=dst, send_sem=send_sems.at[s], recv_sem=recv_sems.at[s],
            device_id=(px, py, mc), device_id_type=MESH)

    def start():
        for a in range(n):
            local(a).start()
            for j in range(1, 4):
                copy(a, j, True).start()

    def wait():
        for a in range(n):
            for j in range(1, 4):
                copy(a, j, False).wait()
            local(a).wait()

    return start, wait


def _call(body, *, name, grid, in_specs, out_specs, out_shape, args, scratch_shapes=(), sem=None, ride=None):
    if not ride:
        return pl.pallas_call(
            body, name=name, grid=grid, in_specs=list(in_specs), out_specs=list(out_specs), out_shape=list(out_shape),
            scratch_shapes=list(scratch_shapes), compiler_params=_params(sem))(*args)
    xs = [x for r in ride for x in r["xs"]]
    shapes = [s for r in ride for s in r["shapes"]]
    sems = [s for r in ride for s in r["sems"]]
    n_in, n_out, n_scr, nx = len(in_specs), len(out_specs), len(scratch_shapes), len(xs)

    def wrapped(*refs):
        ins, x_refs = refs[:n_in], refs[n_in:n_in + nx]
        outs = refs[n_in + nx:n_in + nx + n_out]
        lands = refs[n_in + nx + n_out:n_in + 2 * nx + n_out]
        rest = refs[n_in + 2 * nx + n_out:]
        scr, sem_refs = rest[:n_scr], rest[n_scr:]
        ops, xo, so = [], 0, 0
        for r in ride:
            nr, ns = len(r["xs"]), len(r["sems"])
            ops.append(r["ops"](x_refs[xo:xo + nr], lands[xo:xo + nr], sem_refs[so:so + ns]))
            xo, so = xo + nr, so + ns
        ids = [pl.program_id(a) for a in range(len(grid))]
        first = functools.reduce(jnp.logical_and, [i == 0 for i in ids])
        last = functools.reduce(jnp.logical_and, [i == g - 1 for i, g in zip(ids, grid)])

        @pl.when(first)
        def _():
            for start, _ in ops:
                start()

        body(*ins, *outs, *scr)

        @pl.when(last)
        def _():
            for _, wait in ops:
                wait()

    return pl.pallas_call(
        wrapped, name=name, grid=grid, in_specs=list(in_specs) + [ANY] * nx, out_specs=list(out_specs) + [ANY] * nx,
        out_shape=list(out_shape) + shapes, scratch_shapes=list(scratch_shapes) + sems,
        compiler_params=_params(("arbitrary",) * len(grid)))(*args, *xs)


def _ride_sibling(xs):
    n = len(xs)

    def ops(x_refs, o_refs, sems):
        send_sems, recv_sems = sems
        sib = (lax.axis_index("x"), lax.axis_index("y"), 1 - lax.axis_index("c"))

        def copies():
            return [pltpu.make_async_remote_copy(
                src_ref=x_refs[a], dst_ref=o_refs[a], send_sem=send_sems.at[a], recv_sem=recv_sems.at[a],
                device_id=sib, device_id_type=MESH) for a in range(n)]

        def start():
            for cp in copies():
                cp.start()

        def wait():
            for cp in copies():
                cp.wait()

        return start, wait

    return dict(xs=list(xs), shapes=[jax.ShapeDtypeStruct(x.shape, x.dtype) for x in xs],
                sems=[pltpu.SemaphoreType.DMA((n,)), pltpu.SemaphoreType.DMA((n,))], ops=ops)


PACK = 1024
PACK_ROWS = 512


def _pack(parts, pad_rows=PACK_ROWS):
    flat = []
    for p in parts:
        v = p.reshape(-1).astype(F32)
        flat.append(jnp.pad(v, (0, (-v.shape[0]) % PACK)))
    total = sum(v.shape[0] for v in flat)
    flat.append(jnp.zeros(((-total) % (pad_rows * 128),), F32))
    return jnp.concatenate(flat).reshape(-1, 128)


def _shard_columns(shards, lo, hi):
    width = shards.shape[2]
    out = []
    for k in range(shards.shape[0]):
        a, b = max(lo, k * width), min(hi, (k + 1) * width)
        if a < b:
            out.append(shards[k, :, a - k * width:b - k * width])
    return out


def _unpack_rows(gathered, shapes):
    flat = gathered.reshape(gathered.shape[0], -1)
    out, off = [], 0
    for shp in shapes:
        n = math.prod(shp)
        out.append(flat[:, off:off + n].reshape((flat.shape[0],) + tuple(shp)))
        off += n + (-n) % PACK
    return out


def _unpack(packed, shapes):
    flat = packed.reshape(-1)
    out, off = [], 0
    for shp in shapes:
        n = math.prod(shp)
        out.append(flat[off:off + n].reshape(shp))
        off += n + (-n) % PACK
    return out


def kernel(x, c, ln_pre_g, ln_post_g, w_mod, b_mod, w_in_ab, w_out_ab, sgu_norm_g, sgu_w, sgu_b, w_in_ssm, w_out_ssm, lam_re, lam_im, b_re, b_im, c_re, c_im, d_skip, log_dt, w_glu, b_glu, loss_target, m_ln_pre_g, m_ln_post_g, m_w_mod, m_b_mod, m_w_in_ab, m_w_out_ab, m_sgu_norm_g, m_sgu_w, m_sgu_b, m_w_in_ssm, m_w_out_ssm, m_lam_re, m_lam_im, m_b_re, m_b_im, m_c_re, m_c_im, m_d_skip, m_log_dt, m_w_glu, m_b_glu, v_ln_pre_g, v_ln_post_g, v_w_mod, v_b_mod, v_w_in_ab, v_w_out_ab, v_sgu_norm_g, v_sgu_w, v_sgu_b, v_w_in_ssm, v_w_out_ssm, v_lam_re, v_lam_im, v_b_re, v_b_im, v_c_re, v_c_im, v_d_skip, v_log_dt, v_w_glu, v_b_glu):
    given = dict(locals())
    mx, my, mc = lax.axis_index("x"), lax.axis_index("y"), lax.axis_index("c")
    me = 4 * mx + 2 * my + mc
    chip = 2 * mx + my

    _, l, d = x.shape
    x2, tgt = x[0], loss_target[0]
    n_in = w_in_ab.shape[2] * 4
    wa = wb = n_in // 7
    w = w_out_ssm.shape[1]
    g, p, cch = b_re.shape[1:]
    nmod = w_mod.shape[2]


    cond = _silu_rows(c.reshape(d // 128, 128), "cond_silu")
    cond_all = _allgather8(cond, "gather_cond").reshape(8, d)
    b_shard = lax.dynamic_slice(b_mod, (0, chip * nmod), (2, nmod)).reshape(2, 1, nmod)
    cond_pad = jnp.pad(cond_all, ((0, MOD_ROWS - 8), (0, 0)))
    modp = _mod_fwd(cond_pad, w_mod, b_shard, "mod_fwd")[:, :8]
    modp_all = _allgather8(modp.reshape(16, nmod), "gather_mod").reshape(4, 2, 2, 8, nmod)
    mine = lax.dynamic_index_in_dim(lax.dynamic_index_in_dim(modp_all, mc, 1, False), me, 2, False)
    mod = mine.transpose(1, 0, 2).reshape(2, 3 * d)
    shift = [mod[a:a + 1, :d] for a in range(2)]
    scale = [mod[a:a + 1, d:2 * d] for a in range(2)]
    gate = [mod[a:a + 1, 2 * d:] for a in range(2)]
    pre_g = [ln_pre_g[a:a + 1] for a in range(2)]
    post_g = [ln_post_g[a:a + 1] for a in range(2)]

    sgu_w0, sgu_bt = sgu_w[0], sgu_b[0].T
    h0, (gw_in_ab,) = _pre_fwd(x2, pre_g[0], scale[0], shift[0], "pre0_fwd", ride=[_ride_halves(_bf(w_in_ab[0]))])
    w_gates = jnp.concatenate(_shard_columns(gw_in_ab, 0, 3 * wa) + _shard_columns(gw_in_ab, 3 * wa + 3 * wb, n_in),
                              axis=1)
    w_qkv = jnp.concatenate(_shard_columns(gw_in_ab, 3 * wa, 3 * wa + 3 * wb), axis=1)
    proj0, (gw_in_ssm,) = _matmul(h0, w_gates, "nn", BF16, "proj0", tm=1024, ride=[_ride_chip([_bf(w_in_ssm[0])], True)])
    qkv, (gw_out_ssm, gw_glu, g_dskip, g_bglu) = _matmul(
        h0, w_qkv, "nn", BF16, "proj0_qkv", tm=1024,
        ride=[_ride_chip([_bf(w_out_ssm[0]), _bf(w_glu[0]), d_skip, b_glu], True)])
    out_b, (gw_out_ab,) = _attn_fwd(qkv, wb, "attn_fwd", hp=8, ride=[_ride_chip([_bf(w_out_ab[0])], True)])
    wout_ab = gw_out_ab.reshape(wa + wb, d)
    win_ssm = gw_in_ssm.reshape(d, 2 * w)
    wout_ssm = jnp.concatenate([gw_out_ssm[k] for k in range(4)], axis=1)
    wglu = gw_glu.reshape(w, w)
    dskip_full = g_dskip.reshape(1, w)
    bglu_full = g_bglu.reshape(1, w)
    cat =_sgu_fwd(proj0, out_b, sgu_norm_g, sgu_w0, sgu_bt, wa, wb, "sgu_fwd")
    y0 = _matmul(cat, wout_ab, "nn", BF16, "out0", tm=1024)
    x1, h1 = _post_pre_fwd(x2, y0, gate[0], post_g[0], pre_g[1], scale[1], shift[1], "post0_pre1_fwd")

    s = g * p
    lr_c, li_c = lam_re.reshape(s, 1), lam_im.reshape(s, 1)
    ldt_c = jnp.repeat(log_dt.reshape(g), p).reshape(s, 1)
    br_c, bi_c = b_re.reshape(s, cch), b_im.reshape(s, cch)
    bb_re, bb_im, pw_re, pw_im = _ssm_prep(lr_c, li_c, ldt_c, br_c, bi_c, lr_c.reshape(1, s), li_c.reshape(1, s),
                                           ldt_c.reshape(1, s), "ssm_prep")
    bbd = _bf(_block_diag_b(bb_re, bb_im, g, p, cch))
    ccd = _bf(_block_diag_c(c_re[0], c_im[0], g, p, cch))
    proj1 = _matmul(h1, win_ssm, "nn", F32, "proj1", tm=1024)
    y_ssm, hs_re, hs_im, h_all = _ssm_fwd(proj1, bbd, ccd, pw_re, pw_im, dskip_full, w, "ssm_fwd")
    o1 = _glu_fwd(y_ssm, proj1, wglu, bglu_full, "glu_fwd")
    y1 = _matmul(o1, wout_ssm, "nn", BF16, "out1", tm=1024)
    loss_vec, dy1, dx2, dgate1, dpost1 = _post_loss(x1, y1, gate[1], post_g[1], tgt, "post1_loss")

    do1 = _matmul(dy1, wout_ssm, "nt", BF16, "out1_dx", tm=1024)
    gr_wout_ssm = _matmul_at(o1, dy1, BF16, "out1_dw", n_split=4)
    dy_ssm, dz1, gr_wglu, gr_bglu = _glu_bwd(do1, y_ssm, proj1, wglu, bglu_full, "glu_bwd")
    du1, dbbd, dccd, da_re, da_im, gr_dskip, (ld_wout_ssm, ld_wglu) = _ssm_bwd(
        proj1, dy_ssm, hs_re, hs_im, h_all, bbd, ccd, pw_re, pw_im, dskip_full, w, "ssm_bwd",
        ride=[_ride_chip([gr_wout_ssm, gr_wglu.reshape(4, w // 4, w)], False)])
    dproj1 = jnp.concatenate([du1, dz1], axis=1)
    dh1 = _matmul(dproj1, win_ssm, "nt", BF16, "proj1_dx", tm=1024)
    gr_win_ssm = _matmul_at(h1, dproj1, BF16, "proj1_dw", tn=1024)
    dx1, dscale1, dshift1, dpre1, dy0, dgate0, dpost0 = _pre_bwd(
        dh1, dx2, x1, pre_g[1], scale[1], "pre1_post0_bwd", post=(y0, gate[0], post_g[0]))

    dcat = _matmul(dy0, wout_ab, "nt", BF16, "out0_dx", tm=1024)
    gr_wout_ab = _matmul_at(cat, dy0, BF16, "out0_dw", tn=1024)
    dbb_re, dbb_im = _diag_of_b(dbbd, g, p, cch)
    dc_re, dc_im = _diag_of_c(dccd, g, p, cch)
    part_a = [loss_vec[:, :1], dpre1, dpost0, dpost1, dgate0, dshift1, dscale1, dgate1, da_re, da_im,
              dbb_re, dbb_im, dc_re, dc_im, gr_dskip, gr_bglu]
    shapes_a = [a.shape for a in part_a]
    dq, dk, dv, (ld_win_ssm, ld_wout_ab, gath_a) = _attn_bwd(
        qkv, proj0, dcat, wa, wb, "attn_bwd", hp=4,
        ride=[_ride_chip([gr_win_ssm.reshape(4, d // 4, 2 * w), gr_wout_ab.reshape(4, (wa + wb) // 4, d)], False),
              _ride_all8(_pack(part_a))])
    early_names = ["w_out_ab", "w_in_ssm", "w_out_ssm", "w_glu"]
    early_sums = [_sum_leading(a, "sum_" + nm) for a, nm in zip([ld_wout_ab, ld_win_ssm, ld_wout_ssm, ld_wglu], early_names)]
    dproj0, gr_sgu_w, gr_sgu_bt, gr_sgu_g, early_sib = _sgu_bwd(
        proj0, out_b, dcat, dq, dk, dv, sgu_norm_g, sgu_w0, sgu_bt, wa, wb, "sgu_bwd", ride=[_ride_sibling(early_sums)])
    part_b = [gr_sgu_g, gr_sgu_w, gr_sgu_bt.T]
    shapes_b = [a.shape for a in part_b]
    gr_win_ab_lo, (gath_b,) = _matmul_at(h0, dproj0, BF16, "proj0_dw_lo", tm=512, tn=n_in // 4, n_split=4,
                                         m_part=(0, 1, 2), ride=[_ride_all8(_pack(part_b))])
    gr_win_ab_hi, (ld_win_ab_lo,) = _matmul_at(
        h0, dproj0, BF16, "proj0_dw_hi", tm=512, tn=n_in // 4, n_split=4, m_part=(1, 1, 2),
        ride=[_ride_chip([gr_win_ab_lo], False)])
    dh0, (ld_win_ab_hi,) = _matmul(dproj0, gw_in_ab, "nt", BF16, "proj0_dx", tm=1024, tn=1024,
                                   ride=[_ride_chip([gr_win_ab_hi], False)])
    grad_x, dscale0, dshift0, dpre0 = _pre_bwd(dh0, dx1, x2, pre_g[0], scale[0], "pre0_bwd")
    part_c = [dpre0, dshift0, dscale0]
    shapes_c = [a.shape for a in part_c]

    big_names = ["w_in_ab"] + early_names
    sum_win_ab = jnp.concatenate([_sum_leading(ld_win_ab_lo, "sum_w_in_ab_lo"),
                                  _sum_leading(ld_win_ab_hi, "sum_w_in_ab_hi")], axis=0)
    sums = [sum_win_ab] + early_sums
    gath_c, sib_win_ab = _exchange([_ride_all8(_pack(part_c, pad_rows=8)), _ride_sibling([sum_win_ab])], "tail_exchange")
    sib = [sib_win_ab] + list(early_sib)
    results = {}
    for nm, s_mine, s_sib in zip(big_names, sums, sib):
        shp = given[nm].shape
        two_d = lambda a: a.reshape(-1, shp[-1])
        outs = _adamw(two_d(given[nm]), [s_mine, s_sib], two_d(given["m_" + nm]), two_d(given["v_" + nm]),
                      "adamw_" + nm)
        results[nm] = [o.reshape(shp) for o in outs]

    (loss_s, g_pre1, g_post0, g_post1, g_gate0, g_shift1, g_scale1, g_gate1, s_da_re, s_da_im, s_dbb_re, s_dbb_im,
     g_c_re, g_c_im, g_dskip_full, g_bglu_full) = _unpack(_sum_leading(gath_a, "sum_small_a"), shapes_a)
    g_sgu_g, g_sgu_w, g_sgu_b = _unpack(_sum_leading(gath_b, "sum_small_b"), shapes_b)
    g_pre0, g_shift0, g_scale0 = _unpack(_sum_leading(gath_c, "sum_small_c"), shapes_c)
    loss = loss_s.reshape(())
    g_pre = jnp.concatenate([g_pre0, g_pre1], axis=0)
    g_post = jnp.concatenate([g_post0, g_post1], axis=0)
    g_bmod = jnp.concatenate([jnp.concatenate([g_shift0, g_scale0, g_gate0], axis=1),
                              jnp.concatenate([g_shift1, g_scale1, g_gate1], axis=1)], axis=0)

    g_lr, g_li, g_ldt, g_br, g_bi = _ssm_prep_bwd(lr_c, li_c, ldt_c, br_c, bi_c, s_da_re.reshape(s, 1),
                                                  s_da_im.reshape(s, 1), s_dbb_re, s_dbb_im, p, "ssm_prep_bwd")
    small = {
        "ln_pre_g": g_pre, "ln_post_g": g_post, "b_mod": g_bmod, "sgu_norm_g": g_sgu_g,
        "sgu_w": g_sgu_w.reshape(sgu_w.shape), "sgu_b": g_sgu_b.reshape(sgu_b.shape),
        "lam_re": g_lr.reshape(lam_re.shape), "lam_im": g_li.reshape(lam_im.shape),
        "b_re": g_br.reshape(b_re.shape), "b_im": g_bi.reshape(b_im.shape),
        "c_re": g_c_re.reshape(c_re.shape), "c_im": g_c_im.reshape(c_im.shape),
        "d_skip": lax.dynamic_slice(g_dskip_full, (0, chip * (w // 4)), (1, w // 4)),
        "log_dt": g_ldt.reshape(log_dt.shape),
        "b_glu": lax.dynamic_slice(g_bglu_full, (0, chip * (w // 4)), (1, w // 4)),
    }
    flat2 = lambda a: a.reshape(-1, a.shape[-1])
    wide = ("b_re", "b_im")
    for tag, group in (("adamw_small", [nm for nm in small if nm not in wide]), ("adamw_small_b", list(wide))):
        outs = _adamw_many([flat2(given[nm]) for nm in group], [flat2(small[nm]) for nm in group],
                           [flat2(given["m_" + nm]) for nm in group], [flat2(given["v_" + nm]) for nm in group], tag)
        for nm, trio in zip(group, outs):
            results[nm] = [small[nm]] + [o.reshape(given[nm].shape) for o in trio]

    rows_a = _unpack_rows(gath_a, shapes_a)
    rows_c = _unpack_rows(gath_c, shapes_c)
    dmod_rows = jnp.concatenate([rows_c[1], rows_c[2], rows_a[4], rows_a[5], rows_a[6], rows_a[7]],
                                axis=2).reshape(8, 2, 3 * d)
    dmod_shard = lax.dynamic_slice(dmod_rows, (0, 0, chip * nmod), (8, 2, nmod)).transpose(1, 0, 2)
    dmod_pad = jnp.pad(dmod_shard, ((0, 0), (0, MOD_ROWS - 8), (0, 0)))
    gr_wmod = _mod_bwd(cond_pad.T, dmod_pad, "mod_bwd")
    two_d = lambda a: a.reshape(-1, nmod)
    outs = _adamw(two_d(w_mod), [two_d(gr_wmod)], two_d(m_w_mod), two_d(v_w_mod), "adamw_w_mod")
    results["w_mod"] = [o.reshape(w_mod.shape) for o in outs]

    names = ["ln_pre_g", "ln_post_g", "w_mod", "b_mod", "w_in_ab", "w_out_ab", "sgu_norm_g", "sgu_w", "sgu_b",
             "w_in_ssm", "w_out_ssm", "lam_re", "lam_im", "b_re", "b_im", "c_re", "c_im", "d_skip", "log_dt",
             "w_glu", "b_glu"]
    return (loss, grad_x[None], *[results[nm][0] for nm in names], *[results[nm][1] for nm in names],
            *[results[nm][2] for nm in names], *[results[nm][3] for nm in names])
```

```python
import functools
import math

import jax
import jax.numpy as jnp
from jax import lax
from jax.experimental import pallas as pl
from jax.experimental.pallas import tpu as pltpu

F32 = jnp.float32
BF16 = jnp.bfloat16
MESH = pl.DeviceIdType.MESH

EPS = 1e-6
HEAD = 128
SSM_T = 512
SSM_GB = 16
ADAM_LR, ADAM_B1, ADAM_B2, ADAM_EPS, ADAM_WD, ADAM_STEP = 0.001, 0.9, 0.999, 1e-08, 0.01, 10
VMEM_LIMIT = 56 * 1024 * 1024

NN = (((1,), (0,)), ((), ()))
NT = (((1,), (1,)), ((), ()))
TN = (((0,), (0,)), ((), ()))


def _params(sem=None):
    return pltpu.CompilerParams(dimension_semantics=sem, vmem_limit_bytes=VMEM_LIMIT)


def _dot(a, b, dims=NN):
    return lax.dot_general(a, b, dims, preferred_element_type=F32)


def _bf(x):
    return x.astype(BF16)


def _gelu(x):
    k = math.sqrt(2.0 / math.pi)
    t = jnp.tanh(k * (x + 0.044715 * x * x * x))
    return 0.5 * x * (1.0 + t)


def _gelu_grad(x):
    k = math.sqrt(2.0 / math.pi)
    x2 = x * x
    t = jnp.tanh(k * (x + 0.044715 * x * x2))
    return 0.5 * (1.0 + t) + 0.5 * x * (1.0 - t * t) * k * (1.0 + 3.0 * 0.044715 * x2)


def _sigmoid(x):
    return 1.0 / (1.0 + jnp.exp(-x))


def _silu(x):
    return x * _sigmoid(x)


def _silu_grad(x):
    s = _sigmoid(x)
    return s * (1.0 + x * (1.0 - s))


def _tile(n, t, mult=128):
    if n <= t:
        return n
    for cand in range(t - t % mult, 0, -mult):
        if n % cand == 0:
            return cand
    raise ValueError((n, t, mult))


def _matmul(a, b, mode, out_dtype, name, tm=512, tn=512, tk=2048, n_split=1, ride=None, m_part=None):
    b_sharded = b.ndim == 3
    if mode == "nn":
        (m, kk), (_, n) = a.shape, b.shape
    elif b_sharded:
        assert mode == "nt"
        (m, kk), n, tk = a.shape, b.shape[1], b.shape[2]
    elif mode == "nt":
        (m, kk), (n, _) = a.shape, b.shape
    else:
        (kk, m), (_, n) = a.shape, b.shape
    m_off = 0
    if m_part is not None:
        assert mode == "tn"
        first, count, parts = m_part
        tm = _tile(m // parts, tm)
        m_off = first * (m // parts) // tm
        m = count * (m // parts)
    tm, tk = _tile(m, tm), _tile(kk, tk)
    ns = n // n_split
    tn = _tile(ns, tn)
    nk = kk // tk
    dims = {"nn": NN, "nt": NT, "tn": TN}[mode]

    def body(a_ref, b_ref, o_ref, acc_ref):
        k = pl.program_id(2)
        part = _dot(_bf(a_ref[...]), _bf(b_ref[0] if b_sharded else b_ref[...]), dims)

        @pl.when(k == 0)
        def _():
            acc_ref[...] = part

        @pl.when(k > 0)
        def _():
            acc_ref[...] += part

        @pl.when(k == nk - 1)
        def _():
            o_ref[...] = acc_ref[...].astype(out_dtype).reshape(o_ref.shape)

    if mode == "nn":
        a_spec = pl.BlockSpec((tm, tk), lambda i, j, k: (i, k))
        b_spec = pl.BlockSpec((tk, tn), lambda i, j, k: (k, j))
    elif mode == "nt":
        a_spec = pl.BlockSpec((tm, tk), lambda i, j, k: (i, k))
        b_spec = (pl.BlockSpec((1, tn, tk), lambda i, j, k: (k, j, 0)) if b_sharded
                  else pl.BlockSpec((tn, tk), lambda i, j, k: (j, k)))
    else:
        a_spec = pl.BlockSpec((tk, tm), lambda i, j, k: (k, i + m_off))
        b_spec = pl.BlockSpec((tk, tn), lambda i, j, k: (k, j))
    if n_split == 1:
        out_shape = jax.ShapeDtypeStruct((m, n), out_dtype)
        o_spec = pl.BlockSpec((tm, tn), lambda i, j, k: (i, j))
    else:
        per = ns // tn
        out_shape = jax.ShapeDtypeStruct((n_split, m, ns), out_dtype)
        o_spec = pl.BlockSpec((1, tm, tn), lambda i, j, k: (j // per, i, j % per))
    outs = _call(body, name=name, grid=(m // tm, n // tn, nk), in_specs=[a_spec, b_spec], out_specs=[o_spec],
                 out_shape=[out_shape], scratch_shapes=[pltpu.VMEM((tm, tn), F32)], args=(a, b),
                 sem=("parallel", "parallel", "arbitrary"), ride=ride)
    return outs[0] if ride is None else (outs[0], outs[1:])


def _matmul_at(a, b, out_dtype, name, tm=1024, tn=512, n_split=1, ride=None, m_part=None):
    (kk, m), (_, n) = a.shape, b.shape
    m_off = 0
    if m_part is not None:
        first, count, parts = m_part
        tm = _tile(m // parts, tm)
        m_off = first * (m // parts) // tm
        m = count * (m // parts)
    tm = _tile(m, tm)
    ns = n // n_split
    tn = _tile(ns, tn)
    kc = _tile(kk, 512)

    def body(a_ref, b_ref, o_ref, at_ref):
        @pl.when(pl.program_id(1) == 0)
        def _():
            for c in range(kk // kc):
                at_ref[:, c * kc:(c + 1) * kc] = _bf(a_ref[c * kc:(c + 1) * kc, :].astype(F32).T)

        o_ref[...] = _dot(at_ref[...], _bf(b_ref[...])).astype(out_dtype).reshape(o_ref.shape)

    if n_split == 1:
        out_shape = jax.ShapeDtypeStruct((m, n), out_dtype)
        o_spec = pl.BlockSpec((tm, tn), lambda i, j: (i, j))
    else:
        per = ns // tn
        out_shape = jax.ShapeDtypeStruct((n_split, m, ns), out_dtype)
        o_spec = pl.BlockSpec((1, tm, tn), lambda i, j: (j // per, i, j % per))
    outs = _call(body, name=name, grid=(m // tm, n // tn),
                 in_specs=[pl.BlockSpec((kk, tm), lambda i, j: (0, i + m_off)), pl.BlockSpec((kk, tn), lambda i, j: (0, j))],
                 out_specs=[o_spec], out_shape=[out_shape], scratch_shapes=[pltpu.VMEM((tm, kk), BF16)], args=(a, b),
                 sem=("arbitrary", "arbitrary"), ride=ride)
    return outs[0] if ride is None else (outs[0], outs[1:])


def _row_spec(tm, d):
    return pl.BlockSpec((tm, d), lambda i: (i, 0))


def _vec_spec(d):
    return pl.BlockSpec((1, d), lambda i: (0, 0))


def _acc(ref, first, val):
    @pl.when(first)
    def _():
        ref[...] = val

    @pl.when(jnp.logical_not(first))
    def _():
        ref[...] += val


def _colsum(x):
    return jnp.sum(x, axis=0, keepdims=True)


def _rownorm(x):
    r = lax.rsqrt(jnp.mean(x * x, axis=-1, keepdims=True) + EPS)
    return x * r, r


STRIP = 64


def _fold8(x):
    return functools.reduce(lambda a, b: a + b, [x[8 * k:8 * k + 8] for k in range(x.shape[0] // 8)])


def _pre_fwd(x, g, scale, shift, name, ride=None):
    l, d = x.shape
    tm = _tile(l, 256)

    def body(x_ref, g_ref, sc_ref, sh_ref, h_ref):
        n, _ = _rownorm(x_ref[...])
        h_ref[...] = _bf(n * g_ref[...] * (1.0 + sc_ref[...]) + sh_ref[...])

    outs = _call(body, name=name, grid=(l // tm,), in_specs=[_row_spec(tm, d), _vec_spec(d), _vec_spec(d), _vec_spec(d)],
                 out_specs=[_row_spec(tm, d)], out_shape=[jax.ShapeDtypeStruct((l, d), BF16)],
                 args=(x, g, scale, shift), sem=("parallel",), ride=ride)
    return outs[0], outs[1:]


def _post_pre_fwd(x, y, gate, pg, g1, scale1, shift1, name):
    l, d = x.shape
    tm = _tile(l, 256)

    def body(x_ref, y_ref, gate_ref, pg_ref, g1_ref, sc_ref, sh_ref, x1_ref, h1_ref):
        @pl.loop(0, tm // STRIP)
        def _(s):
            rows = pl.ds(pl.multiple_of(s * STRIP, STRIP), STRIP)
            ny, _ = _rownorm(y_ref[rows, :].astype(F32))
            x1 = x_ref[rows, :] + gate_ref[...] * (ny * pg_ref[...])
            x1_ref[rows, :] = x1
            n1, _ = _rownorm(x1)
            h1_ref[rows, :] = _bf(n1 * g1_ref[...] * (1.0 + sc_ref[...]) + sh_ref[...])

    v = _vec_spec(d)
    return pl.pallas_call(
        body, name=name, grid=(l // tm,),
        in_specs=[_row_spec(tm, d), _row_spec(tm, d), v, v, v, v, v],
        out_specs=[_row_spec(tm, d), _row_spec(tm, d)],
        out_shape=[jax.ShapeDtypeStruct((l, d), F32), jax.ShapeDtypeStruct((l, d), BF16)],
        compiler_params=_params(("parallel",)),
    )(x, y, gate, pg, g1, scale1, shift1)


def _post_loss(x1, y1, gate, pg, target, name):
    l, d = x1.shape
    tm = _tile(l, 256)

    def body(x_ref, y_ref, gate_ref, pg_ref, t_ref, loss_ref, dy_ref, dx_ref, dgate_ref, dpg_ref):
        first = pl.program_id(0) == 0

        def strip(s, sums):
            rows = pl.ds(pl.multiple_of(s * STRIP, STRIP), STRIP)
            ny, ry = _rownorm(y_ref[rows, :].astype(F32))
            q = ny * pg_ref[...]
            e = x_ref[rows, :] + gate_ref[...] * q - t_ref[rows, :]
            dx2 = e * (1.0 / d)
            dx_ref[rows, :] = dx2
            dq = dx2 * gate_ref[...]
            dny = dq * pg_ref[...]
            dy_ref[rows, :] = _bf(ry * (dny - ny * jnp.mean(dny * ny, axis=-1, keepdims=True)))
            return sums[0] + _fold8(e * e), sums[1] + _fold8(dx2 * q), sums[2] + _fold8(dq * ny)

        zero = jnp.zeros((8, d), F32)
        sq, dgate, dpg = lax.fori_loop(0, tm // STRIP, strip, (zero, zero, zero))
        _acc(loss_ref, first, jnp.full((1, 128), 0.5 / d, F32) * jnp.sum(sq))
        _acc(dgate_ref, first, _colsum(dgate))
        _acc(dpg_ref, first, _colsum(dpg))

    v = _vec_spec(d)
    return pl.pallas_call(
        body, name=name, grid=(l // tm,),
        in_specs=[_row_spec(tm, d), _row_spec(tm, d), v, v, _row_spec(tm, d)],
        out_specs=[_vec_spec(128), _row_spec(tm, d), _row_spec(tm, d), v, v],
        out_shape=[jax.ShapeDtypeStruct((1, 128), F32), jax.ShapeDtypeStruct((l, d), BF16),
                   jax.ShapeDtypeStruct((l, d), F32), jax.ShapeDtypeStruct((1, d), F32),
                   jax.ShapeDtypeStruct((1, d), F32)],
        compiler_params=_params(("arbitrary",)),
    )(x1, y1, gate, pg, target)


def _pre_bwd(dh, dres, x, g, scale, name, post=None):
    l, d = x.shape
    tm = _tile(l, 256)
    with_post = post is not None

    def body(*refs):
        if with_post:
            (dh_ref, dres_ref, x_ref, g_ref, sc_ref, y_ref, gate_ref, pg_ref,
             dx_ref, dsc_ref, dsh_ref, dg_ref, dy_ref, dgate_ref, dpg_ref) = refs
        else:
            dh_ref, dres_ref, x_ref, g_ref, sc_ref, dx_ref, dsc_ref, dsh_ref, dg_ref = refs
        first = pl.program_id(0) == 0

        def strip(s, sums):
            rows = pl.ds(pl.multiple_of(s * STRIP, STRIP), STRIP)
            dh = dh_ref[rows, :].astype(F32)
            n, r = _rownorm(x_ref[rows, :])
            dyn = dh * (1.0 + sc_ref[...])
            dn = dyn * g_ref[...]
            dx = dres_ref[rows, :] + r * (dn - n * jnp.mean(dn * n, axis=-1, keepdims=True))
            dx_ref[rows, :] = dx
            new = [sums[0] + _fold8(dh * (n * g_ref[...])), sums[1] + _fold8(dh), sums[2] + _fold8(dyn * n)]
            if with_post:
                ny, ry = _rownorm(y_ref[rows, :].astype(F32))
                dq = dx * gate_ref[...]
                dny = dq * pg_ref[...]
                dy_ref[rows, :] = _bf(ry * (dny - ny * jnp.mean(dny * ny, axis=-1, keepdims=True)))
                new += [sums[3] + _fold8(dx * (ny * pg_ref[...])), sums[4] + _fold8(dq * ny)]
            return tuple(new)

        zero = jnp.zeros((8, d), F32)
        sums = lax.fori_loop(0, tm // STRIP, strip, (zero,) * (5 if with_post else 3))
        outs = [dsc_ref, dsh_ref, dg_ref] + ([dgate_ref, dpg_ref] if with_post else [])
        for ref, acc in zip(outs, sums):
            _acc(ref, first, _colsum(acc))

    v = _vec_spec(d)
    row = _row_spec(tm, d)
    vec_out = jax.ShapeDtypeStruct((1, d), F32)
    in_specs = [row, row, row, v, v]
    args = [dh, dres, x, g, scale]
    out_specs = [row, v, v, v]
    out_shape = [jax.ShapeDtypeStruct((l, d), F32), vec_out, vec_out, vec_out]
    if with_post:
        in_specs += [row, v, v]
        args += list(post)
        out_specs += [row, v, v]
        out_shape += [jax.ShapeDtypeStruct((l, d), BF16), vec_out, vec_out]
    return pl.pallas_call(
        body, name=name, grid=(l // tm,), in_specs=in_specs, out_specs=out_specs, out_shape=out_shape,
        compiler_params=_params(("arbitrary",)),
    )(*args)


def _softplus_parts(z):
    e = jnp.exp(-jnp.abs(z))
    den = 1.0 + e
    lb = jnp.minimum(z, 0.0) - jnp.log(den)
    return lb, lb - z, jnp.exp(lb)


def _tri(cmp, n=HEAD):
    row = lax.broadcasted_iota(jnp.int32, (n, n), 0)
    col = lax.broadcasted_iota(jnp.int32, (n, n), 1)
    return cmp(row, col)


ATT_T = 256
ATT_DEAD = 104.0


def _any_alive(runs):
    return functools.reduce(jnp.maximum, [jnp.max(r) for r in runs]) > -ATT_DEAD


def _attn_fwd(qkv, wb, name, hp=4, ride=None):
    l = qkv.shape[0]
    t = ATT_T
    nh, nq = wb // HEAD, l // t
    hp = min(hp, nh)
    ng, wg = nh // hp, hp * HEAD
    scale = 1.0 / math.sqrt(HEAD)

    def body(q_ref, k_ref, v_ref, o_ref):
        i = pl.program_id(1)
        valid = _tri(lambda r, c: c < r, t)
        m_gt = _bf(_tri(lambda r, c: r > c, t).astype(F32))

        def tile(j, carry, diag):
            rows = pl.ds(pl.multiple_of(j * t, t), t)
            cols = [slice(hh * HEAD, (hh + 1) * HEAD) for hh in range(hp)]
            zs = [_dot(q_ref[:, cs], k_ref[rows, cs], NT) * scale for cs in cols]
            lbs, lks = [], []
            for z in zs:
                lb, lk, _ = _softplus_parts(z)
                lbs.append(lb)
                lks.append(jnp.where(valid, lk, 0.0) if diag else lk)
            laters = [_dot(_bf(lk), m_gt) for lk in lks]
            ws = [jnp.exp(lb + later + run) for lb, later, (_, run) in zip(lbs, laters, carry)]
            if diag:
                ws = [jnp.where(valid, w, 0.0) for w in ws]
            return tuple((acc + _dot(_bf(w), v_ref[rows, cs]), run + jnp.sum(lk, axis=1, keepdims=True))
                         for w, lk, cs, (acc, run) in zip(ws, lks, cols, carry))

        zero = (jnp.zeros((t, HEAD), F32), jnp.zeros((t, 1), F32))
        carry = tile(i, (zero,) * hp, True)
        _, carry = lax.while_loop(lambda c: (c[0] < i) & _any_alive([run for _, run in c[1]]),
                                  lambda c: (c[0] + 1, tile(i - 1 - c[0], c[1], False)), (jnp.int32(0), carry))
        for hh, (acc, _) in enumerate(carry):
            o_ref[:, hh * HEAD:(hh + 1) * HEAD] = _bf(acc)

    blk = lambda off: pl.BlockSpec((t, wg), lambda h, i: (i, off + h))
    full = lambda off: pl.BlockSpec((l, wg), lambda h, i: (0, off + h))
    out = pl.BlockSpec((t, wg), lambda h, i: (i, h))
    outs = _call(body, name=name, grid=(ng, nq), in_specs=[blk(0), full(ng), full(2 * ng)], out_specs=[out],
                 out_shape=[jax.ShapeDtypeStruct((l, wb), BF16)],
                 args=(qkv, qkv, qkv), sem=("parallel", "arbitrary"), ride=ride)
    return outs[0], outs[1:]


def _attn_bwd(qkv, proj, dcat, wa, wb, name, hp=2, ride=None):
    l = qkv.shape[0]
    t = ATT_T
    nh, nq = wb // HEAD, l // t
    hp = min(hp, nh)
    ng, wg = nh // hp, hp * HEAD
    scale = 1.0 / math.sqrt(HEAD)

    def body(q_ref, k_ref, v_ref, bz_ref, dc_ref, dq_ref, dkt_out, dvt_out, do_s, qt_s, dot_s,
             dkt_ref, dvt_ref, out_sems):
        i = pl.program_id(1)

        @pl.when(i == 0)
        def _():
            dkt_ref[...] = jnp.zeros_like(dkt_ref)
            dvt_ref[...] = jnp.zeros_like(dvt_ref)

        do = dc_ref[...].astype(F32) * _silu(bz_ref[...].astype(F32))
        do_s[...] = _bf(do)
        for hh in range(hp):
            cs = slice(hh * HEAD, (hh + 1) * HEAD)
            qt_s[hh] = _bf(q_ref[:, cs].astype(F32).T * scale)
            dot_s[hh] = _bf(do[:, cs].T)
        valid = _tri(lambda r, c: c < r, t)
        m_le = _bf(_tri(lambda r, c: r <= c, t).astype(F32))
        m_lt = _bf(_tri(lambda r, c: r < c, t).astype(F32))

        heads = range(hp)
        cols = [slice(hh * HEAD, (hh + 1) * HEAD) for hh in heads]

        def row_sums(j, runs, diag):
            rows = pl.ds(pl.multiple_of(j * t, t), t)
            out = []
            for cs, run in zip(cols, runs):
                _, lk, _ = _softplus_parts(_dot(q_ref[:, cs], k_ref[rows, cs], NT) * scale)
                if diag:
                    lk = jnp.where(valid, lk, 0.0)
                out.append(run + jnp.sum(lk, axis=1, keepdims=True))
            return tuple(out)

        runs = row_sums(i, (jnp.zeros((t, 1), F32),) * hp, True)
        below, lktot = lax.while_loop(lambda c: (c[0] < i) & _any_alive(c[1]),
                                      lambda c: (c[0] + 1, row_sums(i - 1 - c[0], c[1], False)), (jnp.int32(0), runs))

        def tile(j, carry, diag):
            rows = pl.ds(pl.multiple_of(j * t, t), t)
            zs = [_dot(q_ref[:, cs], k_ref[rows, cs], NT) * scale for cs in cols]
            dws = [_dot(do_s[:, cs], v_ref[rows, cs], NT) for cs in cols]
            lbs, lks, sigs = [], [], []
            for z in zs:
                lb, lk, sig = _softplus_parts(z)
                lbs.append(lb)
                lks.append(jnp.where(valid, lk, 0.0) if diag else lk)
                sigs.append(sig)
            pins = [_dot(_bf(lk), m_le) for lk in lks]
            ws = [jnp.exp(lbs[hh] + (lktot[hh] - carry[hh][1]) - pins[hh]) for hh in heads]
            if diag:
                ws = [jnp.where(valid, w, 0.0) for w in ws]
            das = [dw * w for dw, w in zip(dws, ws)]
            pexs = [_dot(_bf(da), m_lt) for da in das]
            dzs = [das[hh] - sigs[hh] * (das[hh] + carry[hh][2] + pexs[hh]) for hh in heads]
            if diag:
                dzs = [jnp.where(valid, dz, 0.0) for dz in dzs]
            dzs = [_bf(dz) for dz in dzs]
            out = []
            for hh in heads:
                dkt, dvt = _dot(qt_s[hh], dzs[hh]), _dot(dot_s[hh], _bf(ws[hh]))
                for half in range(t // HEAD):
                    dkt_ref[hh, sub * j + half] += dkt[:, half * HEAD:(half + 1) * HEAD]
                    dvt_ref[hh, sub * j + half] += dvt[:, half * HEAD:(half + 1) * HEAD]
                dq, cpre, ppre = carry[hh]
                out.append((dq + _dot(dzs[hh], k_ref[rows, cols[hh]]), cpre + jnp.sum(lks[hh], axis=1, keepdims=True),
                            ppre + pexs[hh][:, t - 1:] + das[hh][:, t - 1:]))
            return tuple(out)

        zero = (jnp.zeros((t, HEAD), F32), jnp.zeros((t, 1), F32), jnp.zeros((t, 1), F32))
        carry = lax.fori_loop(i - below, i, lambda j, c: tile(j, c, False), (zero,) * hp)
        carry = tile(i, carry, True)
        for hh in range(hp):
            dq_ref[:, hh * HEAD:(hh + 1) * HEAD] = carry[hh][0] * scale

        @pl.when(i == nq - 1)
        def _():
            heads = pl.ds(pl.program_id(0) * hp, hp)
            copies = [pltpu.make_async_copy(dkt_ref, dkt_out.at[heads], out_sems.at[0]),
                      pltpu.make_async_copy(dvt_ref, dvt_out.at[heads], out_sems.at[1])]
            for cp in copies:
                cp.start()
            for cp in copies:
                cp.wait()

    sub = t // HEAD
    blk = lambda off: pl.BlockSpec((t, wg), lambda h, i: (i, off + h))
    full = lambda off: pl.BlockSpec((l, wg), lambda h, i: (0, off + h))
    acc_shape = jax.ShapeDtypeStruct((nh, l // HEAD, HEAD, HEAD), F32)
    acc_scratch = pltpu.VMEM((hp, l // HEAD, HEAD, HEAD), F32)
    outs = _call(
        body, name=name, grid=(ng, nq),
        in_specs=[blk(0), full(ng), full(2 * ng), blk(3 * wa // wg), blk(wa // wg)],
        out_specs=[blk(0), ANY, ANY], out_shape=[jax.ShapeDtypeStruct((l, wb), F32), acc_shape, acc_shape],
        scratch_shapes=[pltpu.VMEM((t, wg), BF16), pltpu.VMEM((hp, HEAD, t), BF16), pltpu.VMEM((hp, HEAD, t), BF16),
                        acc_scratch, acc_scratch, pltpu.SemaphoreType.DMA((2,))],
        args=(qkv, qkv, qkv, proj, dcat), sem=("parallel", "arbitrary"), ride=ride)
    return outs[0], outs[1], outs[2], outs[3:]


def _sgu_heads(v, g_ref, w_ref, bt_ref, nh):
    keep = _tri(lambda r, c: r >= c)
    out = []
    for h in range(nh):
        cols = slice(h * HEAD, (h + 1) * HEAD)
        nv, r = _rownorm(v[:, cols])
        wm = jnp.where(keep, w_ref[h], 0.0)
        s = _dot(_bf(wm), _bf(nv * g_ref[:, cols])) + bt_ref[:, h:h + 1]
        out.append((nv, r, wm, s))
    return out


def _sgu_fwd(proj, out_b, norm_g, sgu_w, sgu_bt, wa, wb, name):
    l, n = proj.shape
    nh = wa // HEAD

    def body(au_ref, av_ref, az_ref, bz_ref, ob_ref, g_ref, w_ref, bt_ref, cat_ref):
        u, v, sz = _gelu(au_ref[...].astype(F32)), _gelu(av_ref[...].astype(F32)), _silu(az_ref[...].astype(F32))
        for h, (_, _, _, s) in enumerate(_sgu_heads(v, g_ref, w_ref, bt_ref, nh)):
            cols = slice(h * HEAD, (h + 1) * HEAD)
            cat_ref[:, cols] = _bf(u[:, cols] * s * sz[:, cols])
        cat_ref[:, wa:] = _bf(ob_ref[...] * _silu(bz_ref[...].astype(F32)))

    a_blk = lambda j: pl.BlockSpec((HEAD, wa), lambda i: (i, j))
    return pl.pallas_call(
        body, name=name, grid=(l // HEAD,),
        in_specs=[a_blk(0), a_blk(1), a_blk(2), a_blk(3), pl.BlockSpec((HEAD, wb), lambda i: (i, 0)),
                  _vec_spec(wa), pl.BlockSpec((nh, HEAD, HEAD), lambda i: (0, 0, 0)),
                  pl.BlockSpec((HEAD, nh), lambda i: (0, 0))],
        out_specs=pl.BlockSpec((HEAD, wa + wb), lambda i: (i, 0)),
        out_shape=jax.ShapeDtypeStruct((l, wa + wb), BF16),
        compiler_params=_params(("parallel",)),
    )(proj, proj, proj, proj, out_b, norm_g, sgu_w, sgu_bt)


def _sgu_bwd(proj, out_b, dcat, dq, dk, dv, norm_g, sgu_w, sgu_bt, wa, wb, name, ride=None):
    l = proj.shape[0]
    n = 3 * wa + 4 * wb
    nh = wa // HEAD

    def body(au_ref, av_ref, az_ref, bz_ref, ob_ref, dc_ref, dq_ref, dk_ref, dv_ref, g_ref, w_ref, wt_ref, bt_ref,
             dp_ref, dw_ref, dbt_ref, dg_ref):
        first = pl.program_id(0) == 0
        keep = _tri(lambda r, c: r >= c)
        au, av, az = au_ref[...].astype(F32), av_ref[...].astype(F32), az_ref[...].astype(F32)
        u, v, sz = _gelu(au), _gelu(av), _silu(az)
        dgelu_u, dgelu_v, dsilu_z = _gelu_grad(au), _gelu_grad(av), _silu_grad(az)
        heads = _sgu_heads(v, g_ref, w_ref, bt_ref, nh)
        cols = [slice(h * HEAD, (h + 1) * HEAD) for h in range(nh)]
        dss = []
        for h, (nv, r, wm, s) in enumerate(heads):
            dca, uh, szh = dc_ref[:, cols[h]].astype(F32), u[:, cols[h]], sz[:, cols[h]]
            dp_ref[:, cols[h]] = _bf(dca * s * szh * dgelu_u[:, cols[h]])
            dp_ref[:, 2 * wa + h * HEAD:2 * wa + (h + 1) * HEAD] = _bf(dca * uh * s * dsilu_z[:, cols[h]])
            dss.append(dca * uh * szh)
        dws = [_dot(_bf(ds), _bf(nv * g_ref[:, cs]), NT) for ds, cs, (nv, _, _, _) in zip(dss, cols, heads)]
        keep_t = _tri(lambda r, c: r <= c)
        dvhs = [_dot(_bf(jnp.where(keep_t, wt_ref[h], 0.0)), _bf(dss[h])) for h in range(nh)]
        dg_parts = []
        for h, (nv, r, wm, s) in enumerate(heads):
            _acc(dw_ref.at[h], first, jnp.where(keep, dws[h], 0.0))
            _acc(dbt_ref.at[:, h:h + 1], first, jnp.sum(dss[h], axis=1, keepdims=True))
            dg_parts.append(_colsum(dvhs[h] * nv))
            dnv = dvhs[h] * g_ref[:, cols[h]]
            dvv = r * (dnv - nv * jnp.mean(dnv * nv, axis=-1, keepdims=True))
            dp_ref[:, wa + h * HEAD:wa + (h + 1) * HEAD] = _bf(dvv * dgelu_v[:, cols[h]])
        _acc(dg_ref, first, jnp.concatenate(dg_parts, axis=1))
        base = 3 * wa
        dp_ref[:, base:base + wb] = _bf(dq_ref[...])
        for h in range(wb // HEAD):
            dp_ref[:, base + wb + h * HEAD:base + wb + (h + 1) * HEAD] = _bf(dk_ref[h, 0].T)
            dp_ref[:, base + 2 * wb + h * HEAD:base + 2 * wb + (h + 1) * HEAD] = _bf(dv_ref[h, 0].T)
        dp_ref[:, base + 3 * wb:] = _bf(dc_ref[:, wa:].astype(F32) * ob_ref[...]
                                        * _silu_grad(bz_ref[...].astype(F32)))

    a_blk = lambda j: pl.BlockSpec((HEAD, wa), lambda i: (i, j))
    b_blk = pl.BlockSpec((HEAD, wb), lambda i: (i, 0))
    t_blk = pl.BlockSpec((wb // HEAD, 1, HEAD, HEAD), lambda i: (0, i, 0, 0))
    w_spec = pl.BlockSpec((nh, HEAD, HEAD), lambda i: (0, 0, 0))
    bt_spec = pl.BlockSpec((HEAD, nh), lambda i: (0, 0))
    outs = _call(
        body, name=name, grid=(l // HEAD,), ride=ride, sem=("arbitrary",),
        in_specs=[a_blk(0), a_blk(1), a_blk(2), a_blk(3), b_blk, pl.BlockSpec((HEAD, wa + wb), lambda i: (i, 0)),
                  b_blk, t_blk, t_blk, _vec_spec(wa), w_spec, w_spec, bt_spec],
        out_specs=[pl.BlockSpec((HEAD, n), lambda i: (i, 0)), w_spec, bt_spec, _vec_spec(wa)],
        out_shape=[jax.ShapeDtypeStruct((l, n), BF16), jax.ShapeDtypeStruct((nh, HEAD, HEAD), F32),
                   jax.ShapeDtypeStruct((HEAD, nh), F32), jax.ShapeDtypeStruct((1, wa), F32)],
        args=(proj, proj, proj, proj, out_b, dcat, dq, dk, dv, norm_g, sgu_w, sgu_w.transpose(0, 2, 1), sgu_bt))
    return (*outs[:4], outs[4:])


def _ssm_discretise(lr, li, ldt, br, bi):
    dt = jnp.exp(ldt)
    mag = jnp.exp(lr * dt)
    a_re = mag * jnp.cos(li * dt)
    a_im = mag * jnp.sin(li * dt)
    den = lr * lr + li * li
    nr = a_re - 1.0
    coef_re = (nr * lr + a_im * li) / den
    coef_im = (a_im * lr - nr * li) / den
    return a_re, a_im, coef_re * br - coef_im * bi, coef_re * bi + coef_im * br


def _ssm_prep(lr, li, ldt, br, bi, lr_row, li_row, ldt_row, name):
    s, c = br.shape

    def body(lr_ref, li_ref, ldt_ref, br_ref, bi_ref, lrr_ref, lir_ref, ldtr_ref, bbr_ref, bbi_ref, tr_ref, ti_ref):
        _, _, bbr, bbi = _ssm_discretise(lr_ref[...], li_ref[...], ldt_ref[...], br_ref[...], bi_ref[...])
        bbr_ref[...] = bbr
        bbi_ref[...] = bbi
        row = lax.broadcasted_iota(jnp.int32, (SCAN_ROWS, 1), 0)
        blk, r = jnp.right_shift(row, 3), jnp.bitwise_and(row, 7)
        kind, rev = jnp.bitwise_and(blk, 3), blk >= 4
        step = jnp.left_shift(1, kind)
        n = jnp.where(kind < 3, step, jnp.where(rev, 8 - r, r + 1)).astype(F32)
        keep = (kind == 3) | (rev & (r < 8 - step)) | (jnp.logical_not(rev) & (r >= step))
        dt = jnp.exp(ldtr_ref[...])
        mag = jnp.exp(n * (lrr_ref[...] * dt))
        ang = n * (lir_ref[...] * dt)
        tr_ref[...] = jnp.where(keep, mag * jnp.cos(ang), 0.0)
        ti_ref[...] = jnp.where(keep, jnp.where(rev, -1.0, 1.0) * mag * jnp.sin(ang), 0.0)

    col = jax.ShapeDtypeStruct((s, c), F32)
    row = jax.ShapeDtypeStruct((SCAN_ROWS, s), F32)
    return pl.pallas_call(body, name=name, out_shape=[col, col, row, row])(
        lr, li, ldt, br, bi, lr_row, li_row, ldt_row)


def _ssm_prep_bwd(lr, li, ldt, br, bi, da_re, da_im, dbb_re, dbb_im, p, name):
    s, c = br.shape

    def body(lr_ref, li_ref, ldt_ref, br_ref, bi_ref, dar_ref, dai_ref, dbr_ref, dbi_ref,
             dlr_ref, dli_ref, dldt_ref, dbre_ref, dbim_ref):
        args = (lr_ref[...], li_ref[...], ldt_ref[...], br_ref[...], bi_ref[...])
        _, vjp = jax.vjp(_ssm_discretise, *args)
        dlr, dli, dldt, dbr, dbi = vjp((dar_ref[...], dai_ref[...], dbr_ref[...], dbi_ref[...]))
        dlr_ref[...] = dlr
        dli_ref[...] = dli
        dbre_ref[...] = dbr
        dbim_ref[...] = dbi
        idx = lax.broadcasted_iota(jnp.int32, (s, s // p), 0)
        grp = lax.broadcasted_iota(jnp.int32, (s, s // p), 1)
        own = (idx >= grp * p) & (idx < (grp + 1) * p)
        dldt_ref[...] = _colsum(jnp.where(own, dldt, 0.0))

    col1 = jax.ShapeDtypeStruct((s, 1), F32)
    colc = jax.ShapeDtypeStruct((s, c), F32)
    return pl.pallas_call(
        body, name=name, out_shape=[col1, col1, jax.ShapeDtypeStruct((1, s // p), F32), colc, colc],
    )(lr, li, ldt, br, bi, da_re, da_im, dbb_re, dbb_im)


SCAN_ROWS = 64


def _scan_groups(xr, xi, tr_ref, ti_ref, cr, ci, reverse):
    ng = xr.shape[0] // 8
    base = SCAN_ROWS // 2 if reverse else 0
    pr, pi = tr_ref[base + 24:base + 32, :], ti_ref[base + 24:base + 32, :]
    edge = slice(0, 1) if reverse else slice(7, 8)
    out_r, out_i = [None] * ng, [None] * ng
    for g in (range(ng - 1, -1, -1) if reverse else range(ng)):
        sr, si = xr[8 * g:8 * g + 8, :], xi[8 * g:8 * g + 8, :]
        for k in range(3):
            ar, ai = tr_ref[base + 8 * k:base + 8 * k + 8, :], ti_ref[base + 8 * k:base + 8 * k + 8, :]
            shift = 8 - (1 << k) if reverse else 1 << k
            rr, ri = pltpu.roll(sr, shift, 0), pltpu.roll(si, shift, 0)
            sr, si = sr + ar * rr - ai * ri, si + ar * ri + ai * rr
        sr, si = sr + pr * cr - pi * ci, si + pr * ci + pi * cr
        cr, ci = sr[edge, :], si[edge, :]
        out_r[g], out_i[g] = sr, si
    return jnp.concatenate(out_r, axis=0), jnp.concatenate(out_i, axis=0), cr, ci


def _ssm_fwd(proj, bbd, ccd, pw_re, pw_im, d_skip, w, name):
    l = proj.shape[0]
    nb, cw, ns2 = bbd.shape
    ns = ns2 // 2
    nc = l // SSM_T

    def body(u_ref, bbd_ref, ccd_ref, pr_ref, pi_ref, d_ref, y_ref, hsr_ref, hsi_ref, h_ref, hr_s, hi_s):
        @pl.when(pl.program_id(1) == 0)
        def _():
            hr_s[...] = jnp.zeros_like(hr_s)
            hi_s[...] = jnp.zeros_like(hi_s)

        hsr_ref[...] = hr_s[...].reshape(hsr_ref.shape)
        hsi_ref[...] = hi_s[...].reshape(hsi_ref.shape)
        u = u_ref[...].astype(F32)
        bu = _dot(_bf(u), bbd_ref[0])
        hr, hi, cr, ci = _scan_groups(bu[:, :ns], bu[:, ns:], pr_ref, pi_ref, hr_s[...], hi_s[...], False)
        hr_s[...] = cr
        hi_s[...] = ci
        h_bf = _bf(jnp.concatenate([hr, hi], axis=1))
        h_ref[...] = h_bf
        y_ref[...] = _dot(h_bf, ccd_ref[0]) + d_ref[...] * u

    tab = pl.BlockSpec((SCAN_ROWS, ns), lambda b, k: (0, b))
    return pl.pallas_call(
        body, name=name, grid=(nb, nc),
        in_specs=[pl.BlockSpec((SSM_T, cw), lambda b, k: (k, b)),
                  pl.BlockSpec((1, cw, ns2), lambda b, k: (b, 0, 0)),
                  pl.BlockSpec((1, ns2, cw), lambda b, k: (b, 0, 0)),
                  tab, tab, pl.BlockSpec((1, cw), lambda b, k: (0, b))],
        out_specs=[pl.BlockSpec((SSM_T, cw), lambda b, k: (k, b)),
                   pl.BlockSpec((1, 1, ns), lambda b, k: (k, 0, b)), pl.BlockSpec((1, 1, ns), lambda b, k: (k, 0, b)),
                   pl.BlockSpec((SSM_T, ns2), lambda b, k: (k, b))],
        out_shape=[jax.ShapeDtypeStruct((l, w), F32), jax.ShapeDtypeStruct((nc, 1, nb * ns), F32),
                   jax.ShapeDtypeStruct((nc, 1, nb * ns), F32), jax.ShapeDtypeStruct((l, nb * ns2), BF16)],
        scratch_shapes=[pltpu.VMEM((1, ns), F32), pltpu.VMEM((1, ns), F32)],
        compiler_params=_params(("parallel", "arbitrary")),
    )(proj, bbd, ccd, pw_re, pw_im, d_skip)


def _ssm_bwd(proj, dy, hs_re, hs_im, h_all, bbd, ccd, pw_re, pw_im, d_skip, w, name, ride=None):
    l = proj.shape[0]
    nb, cw, ns2 = bbd.shape
    ns = ns2 // 2
    nc = l // SSM_T

    def body(u_ref, dy_ref, hsr_ref, hsi_ref, h_ref, bbd_ref, ccd_ref, pr_ref, pi_ref, d_ref,
             du_ref, dbbd_ref, dccd_ref, dar_ref, dai_ref, dd_ref, gr_s, gi_s):
        first = pl.program_id(1) == 0

        @pl.when(first)
        def _():
            gr_s[...] = jnp.zeros_like(gr_s)
            gi_s[...] = jnp.zeros_like(gi_s)

        u, dy = u_ref[...].astype(F32), dy_ref[...]
        dy_bf = _bf(dy)
        hr0, hi0 = hsr_ref[0], hsi_ref[0]
        h = h_ref[...].astype(F32)
        hr, hi = h[:, :ns], h[:, ns:]
        dh = _dot(dy_bf, ccd_ref[0], NT)
        gr, gi, gcr, gci = _scan_groups(dh[:, :ns], dh[:, ns:], pr_ref, pi_ref, gr_s[...], gi_s[...], True)
        gr_s[...] = gcr
        gi_s[...] = gci
        row0 = lax.broadcasted_iota(jnp.int32, hr.shape, 0) == 0
        pr_h = jnp.where(row0, hr0, pltpu.roll(hr, 1, 0))
        pi_h = jnp.where(row0, hi0, pltpu.roll(hi, 1, 0))
        _acc(dar_ref, first, _colsum(pr_h * gr + pi_h * gi))
        _acc(dai_ref, first, _colsum(pr_h * gi - pi_h * gr))
        g_bf = _bf(jnp.concatenate([gr, gi], axis=1))
        _acc(dbbd_ref.at[0], first, _dot(_bf(u.T), g_bf))
        _acc(dccd_ref.at[0], first, _dot(_bf(h.T), dy_bf))
        du_ref[...] = _bf(_dot(g_bf, bbd_ref[0], NT) + d_ref[...] * dy)
        _acc(dd_ref, first, _colsum(dy * u))

    rev = lambda b, k: (nc - 1 - k, b)
    outs = _call(
        body, name=name, grid=(nb, nc), ride=ride, sem=("parallel", "arbitrary"),
        args=(proj, dy, hs_re, hs_im, h_all, bbd, ccd, pw_re, pw_im, d_skip),
        in_specs=[pl.BlockSpec((SSM_T, cw), rev), pl.BlockSpec((SSM_T, cw), rev),
                  pl.BlockSpec((1, 1, ns), lambda b, k: (nc - 1 - k, 0, b)),
                  pl.BlockSpec((1, 1, ns), lambda b, k: (nc - 1 - k, 0, b)),
                  pl.BlockSpec((SSM_T, ns2), rev),
                  pl.BlockSpec((1, cw, ns2), lambda b, k: (b, 0, 0)),
                  pl.BlockSpec((1, ns2, cw), lambda b, k: (b, 0, 0)),
                  pl.BlockSpec((SCAN_ROWS, ns), lambda b, k: (0, b)), pl.BlockSpec((SCAN_ROWS, ns), lambda b, k: (0, b)),
                  pl.BlockSpec((1, cw), lambda b, k: (0, b))],
        out_specs=[pl.BlockSpec((SSM_T, cw), rev),
                   pl.BlockSpec((1, cw, ns2), lambda b, k: (b, 0, 0)),
                   pl.BlockSpec((1, ns2, cw), lambda b, k: (b, 0, 0)),
                   pl.BlockSpec((1, ns), lambda b, k: (0, b)), pl.BlockSpec((1, ns), lambda b, k: (0, b)),
                   pl.BlockSpec((1, cw), lambda b, k: (0, b))],
        out_shape=[jax.ShapeDtypeStruct((l, w), BF16), jax.ShapeDtypeStruct(bbd.shape, F32),
                   jax.ShapeDtypeStruct(ccd.shape, F32), jax.ShapeDtypeStruct((1, nb * ns), F32),
                   jax.ShapeDtypeStruct((1, nb * ns), F32), jax.ShapeDtypeStruct((1, w), F32)],
        scratch_shapes=[pltpu.VMEM((1, ns), F32), pltpu.VMEM((1, ns), F32)])
    return (*outs[:6], outs[6:])


def _block_diag_b(bb_re, bb_im, g, p, c):
    nb = g // SSM_GB
    keep = _same_group(SSM_GB * c, c, SSM_GB * p, p)

    def one(bb):
        t = bb.reshape(nb, SSM_GB, p, c).transpose(0, 1, 3, 2).reshape(nb, SSM_GB * c, p)
        return jnp.where(keep, jnp.tile(t, (1, 1, SSM_GB)), 0.0)

    return jnp.concatenate([one(bb_re), one(bb_im)], axis=2)


def _same_group(rows, per_row, cols, per_col):
    r = lax.broadcasted_iota(jnp.int32, (rows, cols), 0) // per_row
    q = lax.broadcasted_iota(jnp.int32, (rows, cols), 1) // per_col
    return r == q


def _block_diag_c(c_re, c_im, g, p, c):
    nb = g // SSM_GB
    keep = _same_group(SSM_GB * p, p, SSM_GB * c, c)

    def one(cc):
        t = cc.reshape(nb, SSM_GB, c, p).transpose(0, 1, 3, 2).reshape(nb, SSM_GB * p, c)
        return jnp.where(keep, jnp.tile(t, (1, 1, SSM_GB)), 0.0)

    return jnp.concatenate([one(c_re), one(-c_im)], axis=1)


def _diag_of_b(dbbd, g, p, c):
    nb = g // SSM_GB
    keep = _same_group(SSM_GB * c, c, SSM_GB * p, p)

    def one(blk):
        d = jnp.where(keep, blk, 0.0).reshape(nb, SSM_GB * c, SSM_GB, p).sum(axis=2)
        return d.reshape(nb, SSM_GB, c, p).transpose(0, 1, 3, 2).reshape(g * p, c)

    half = SSM_GB * p
    return one(dbbd[:, :, :half]), one(dbbd[:, :, half:])


def _diag_of_c(dccd, g, p, c):
    nb = g // SSM_GB
    keep = _same_group(SSM_GB * p, p, SSM_GB * c, c)

    def one(blk):
        d = jnp.where(keep, blk, 0.0).reshape(nb, SSM_GB * p, SSM_GB, c).sum(axis=2)
        return d.reshape(nb, SSM_GB, p, c).transpose(0, 1, 3, 2).reshape(g, c, p)

    half = SSM_GB * p
    return one(dccd[:, :half]), -one(dccd[:, half:])


def _glu_fwd(y, proj, w_glu, b_glu, name):
    l, w = y.shape
    tm = _tile(l, 256)

    def body(y_ref, z_ref, w_ref, b_ref, o_ref):
        g = _gelu(y_ref[...])
        t = _dot(_bf(g), w_ref[...]) + b_ref[...]
        o_ref[...] = _bf(g * _sigmoid(t) * _silu(z_ref[...].astype(F32)))

    return pl.pallas_call(
        body, name=name, grid=(l // tm,),
        in_specs=[_row_spec(tm, w), pl.BlockSpec((tm, w), lambda i: (i, 1)),
                  pl.BlockSpec((w, w), lambda i: (0, 0)), _vec_spec(w)],
        out_specs=_row_spec(tm, w), out_shape=jax.ShapeDtypeStruct((l, w), BF16),
        compiler_params=_params(("parallel",)),
    )(y, proj, w_glu, b_glu)


def _glu_bwd(do, y, proj, w_glu, b_glu, name):
    l, w = y.shape
    tm = _tile(l, 512)
    nsteps = l // tm

    def body(do_ref, y_ref, z_ref, w_ref, b_ref, dy_ref, dz_ref, dw_ref, db_ref, dw_acc):
        i = pl.program_id(0)
        first = i == 0
        yv, z, do = y_ref[...], z_ref[...].astype(F32), do_ref[...].astype(F32)
        g = _gelu(yv)
        g_bf = _bf(g)
        sg = _sigmoid(_dot(g_bf, w_ref[...]) + b_ref[...])
        dyy = do * _silu(z)
        dz_ref[...] = _bf(do * g * sg * _silu_grad(z))
        dt = dyy * g * sg * (1.0 - sg)
        dt_bf = _bf(dt)
        dg = dyy * sg + _dot(dt_bf, w_ref[...], NT)
        dy_ref[...] = dg * _gelu_grad(yv)
        _acc(dw_acc, first, _dot(_bf(g.T), dt_bf))
        _acc(db_ref, first, _colsum(dt))

        @pl.when(i == nsteps - 1)
        def _():
            dw_ref[...] = _bf(dw_acc[...])

    return pl.pallas_call(
        body, name=name, grid=(nsteps,),
        in_specs=[_row_spec(tm, w), _row_spec(tm, w), pl.BlockSpec((tm, w), lambda i: (i, 1)),
                  pl.BlockSpec((w, w), lambda i: (0, 0)), _vec_spec(w)],
        out_specs=[_row_spec(tm, w), _row_spec(tm, w), pl.BlockSpec((w, w), lambda i: (0, 0)), _vec_spec(w)],
        out_shape=[jax.ShapeDtypeStruct((l, w), F32), jax.ShapeDtypeStruct((l, w), BF16),
                   jax.ShapeDtypeStruct((w, w), BF16), jax.ShapeDtypeStruct((1, w), F32)],
        scratch_shapes=[pltpu.VMEM((w, w), F32)],
        compiler_params=_params(("arbitrary",)),
    )(do, y, proj, w_glu, b_glu)


MOD_ROWS = 128


def _mod_fwd(cond_pad, w_mod, b_shard, name):
    nl, d, ncol = w_mod.shape
    tn = _tile(ncol, 512)

    def body(c_ref, w_ref, b_ref, o_ref):
        o_ref[0] = _dot(_bf(c_ref[...]), _bf(w_ref[0])) + b_ref[0]

    return pl.pallas_call(
        body, name=name, grid=(nl, ncol // tn),
        in_specs=[pl.BlockSpec((MOD_ROWS, d), lambda a, j: (0, 0)),
                  pl.BlockSpec((1, d, tn), lambda a, j: (a, 0, j)),
                  pl.BlockSpec((1, 1, tn), lambda a, j: (a, 0, j))],
        out_specs=pl.BlockSpec((1, MOD_ROWS, tn), lambda a, j: (a, 0, j)),
        out_shape=jax.ShapeDtypeStruct((nl, MOD_ROWS, ncol), F32),
        compiler_params=_params(("parallel", "parallel")),
    )(cond_pad, w_mod, b_shard)


def _mod_bwd(cond_pad_t, dmod_pad, name):
    nl, _, ncol = dmod_pad.shape
    d = cond_pad_t.shape[0]
    tn = _tile(ncol, 512)

    def body(c_ref, dm_ref, o_ref):
        o_ref[0] = _dot(_bf(c_ref[...]), _bf(dm_ref[0]))

    return pl.pallas_call(
        body, name=name, grid=(nl, ncol // tn),
        in_specs=[pl.BlockSpec((d, MOD_ROWS), lambda a, j: (0, 0)),
                  pl.BlockSpec((1, MOD_ROWS, tn), lambda a, j: (a, 0, j))],
        out_specs=pl.BlockSpec((1, d, tn), lambda a, j: (a, 0, j)),
        out_shape=jax.ShapeDtypeStruct((nl, d, ncol), F32),
        compiler_params=_params(("parallel", "parallel")),
    )(cond_pad_t, dmod_pad)


def _silu_rows(c2d, name):
    def body(c_ref, o_ref):
        o_ref[...] = _silu(c_ref[...])

    return pl.pallas_call(body, name=name, out_shape=jax.ShapeDtypeStruct(c2d.shape, F32))(c2d)


def _sum_leading(x, name):
    n, r, c = x.shape
    tr = _tile(r, max(16, (1 << 20) // (4 * c)), 16 if r % 16 == 0 else 8)

    def body(x_ref, o_ref):
        acc = x_ref[0].astype(F32)
        for k in range(1, n):
            acc = acc + x_ref[k].astype(F32)
        o_ref[...] = acc

    return pl.pallas_call(
        body, name=name, grid=(r // tr,),
        in_specs=[pl.BlockSpec((n, tr, c), lambda i: (0, i, 0))], out_specs=pl.BlockSpec((tr, c), lambda i: (i, 0)),
        out_shape=jax.ShapeDtypeStruct((r, c), F32), compiler_params=_params(("parallel",)),
    )(x)


def _adamw(w, gs, m, v, name, ride=None):
    r, c = w.shape
    tr = _tile(r, max(8, (3 << 19) // (4 * c)), 8)
    ng = len(gs)

    def body(*refs):
        w_ref, g_refs, m_ref, v_ref = refs[0], refs[1:1 + ng], refs[1 + ng], refs[2 + ng]
        g_ref, d_ref, nm_ref, nv_ref = refs[3 + ng:]
        g = g_refs[0][...]
        for extra in g_refs[1:]:
            g = g + extra[...]
        g_ref[...] = g
        d_ref[...], nm_ref[...], nv_ref[...] = _adamw_math(w_ref[...], g, m_ref[...], v_ref[...])

    spec = pl.BlockSpec((tr, c), lambda i: (i, 0))
    shp = jax.ShapeDtypeStruct((r, c), F32)
    outs = _call(body, name=name, grid=(r // tr,), in_specs=[spec] * (3 + ng), out_specs=[spec] * 4,
                 out_shape=[shp] * 4, args=(w, *gs, m, v), sem=("parallel",), ride=ride)
    return outs if ride is None else (outs[:4], outs[4:])


def _adamw_math(w, g, m, v):
    nm = ADAM_B1 * m + (1.0 - ADAM_B1) * g
    nv = ADAM_B2 * v + (1.0 - ADAM_B2) * (g * g)
    m_hat = nm / (1.0 - ADAM_B1 ** ADAM_STEP)
    v_hat = nv / (1.0 - ADAM_B2 ** ADAM_STEP)
    return -ADAM_LR * (m_hat / (jnp.sqrt(v_hat) + ADAM_EPS) + ADAM_WD * w), nm, nv


def _adamw_many(ws, gs, ms, vs, name):
    n = len(ws)

    def body(*refs):
        w_refs, g_refs, m_refs, v_refs = (refs[k * n:(k + 1) * n] for k in range(4))
        outs = refs[4 * n:]
        for i in range(n):
            outs[3 * i][...], outs[3 * i + 1][...], outs[3 * i + 2][...] = _adamw_math(
                w_refs[i][...], g_refs[i][...], m_refs[i][...], v_refs[i][...])

    out_shape = [jax.ShapeDtypeStruct(w.shape, F32) for w in ws for _ in range(3)]
    outs = pl.pallas_call(body, name=name, out_shape=out_shape, compiler_params=_params())(*ws, *gs, *ms, *vs)
    return [tuple(outs[3 * i:3 * i + 3]) for i in range(n)]


ANY = pl.BlockSpec(memory_space=pl.ANY)


def _flip(v, bit):
    return 1 - v if bit else v


def _allgather8_ops(x_ref, o_ref, send_sems, recv_sems, local_sem):
    mx, my, mc = lax.axis_index("x"), lax.axis_index("y"), lax.axis_index("c")
    me = 4 * mx + 2 * my + mc

    def mine():
        return pltpu.make_async_copy(x_ref, o_ref.at[me], local_sem)

    def copy(j, outgoing):
        peer = (_flip(mx, j & 4), _flip(my, j & 2), _flip(mc, j & 1))
        slot = me if outgoing else 4 * peer[0] + 2 * peer[1] + peer[2]
        return pltpu.make_async_remote_copy(
            src_ref=x_ref, dst_ref=o_ref.at[slot], send_sem=send_sems.at[j - 1], recv_sem=recv_sems.at[j - 1],
            device_id=peer, device_id_type=MESH)

    def start():
        mine().start()
        for j in range(1, 8):
            copy(j, True).start()

    def wait():
        for j in range(1, 8):
            copy(j, False).wait()
        mine().wait()

    return start, wait


def _ride_all8(x):
    return dict(xs=[x], shapes=[jax.ShapeDtypeStruct((8,) + x.shape, x.dtype)],
                sems=[pltpu.SemaphoreType.DMA((7,)), pltpu.SemaphoreType.DMA((7,)), pltpu.SemaphoreType.DMA],
                ops=lambda x_refs, o_refs, sems: _allgather8_ops(x_refs[0], o_refs[0], *sems))


def _ride_chip(xs, gather):
    return dict(xs=list(xs), shapes=_chip_exchange_shapes(xs, gather), sems=_chip_exchange_sems(len(xs)),
                ops=lambda x_refs, o_refs, sems: _chip_exchange_ops(x_refs, o_refs, *sems, gather))


def _allgather8(x, name):
    return _exchange([_ride_all8(x)], name)[0]


def _gather_halves_ops(x_ref, o_ref, ici_send, ici_recv, d2d_send, d2d_recv, local_sem):
    half = x_ref.shape[0] // 2
    quarter = half // 2
    mx, my, mc = lax.axis_index("x"), lax.axis_index("y"), lax.axis_index("c")
    k0, kx, ky, kd = 2 * mx + my, 2 * (1 - mx) + my, 2 * mx + (1 - my), 2 * (1 - mx) + (1 - my)
    x_nbr, y_nbr, sib = (1 - mx, my, mc), (mx, 1 - my, mc), (mx, my, 1 - mc)

    def rows(core, part):
        if part is None:
            return pl.ds(pl.multiple_of(core * half, 16), half)
        return pl.ds(pl.multiple_of(core * half + part * quarter, 16), quarter)

    def local():
        return pltpu.make_async_copy(x_ref, o_ref.at[k0], local_sem)

    def ici(n, outgoing):
        to = x_nbr if n in (0, 2) else y_nbr
        if n < 2:
            src = x_ref.at[rows(mc, None)]
            dst = o_ref.at[k0 if outgoing else (kx if n == 0 else ky), rows(mc, None)]
        else:
            part = n - 2
            src = o_ref.at[ky if n == 2 else kx, rows(mc, part)]
            dst = o_ref.at[(ky if n == 2 else kx) if outgoing else kd, rows(mc, part)]
        return pltpu.make_async_remote_copy(src_ref=src, dst_ref=dst, send_sem=ici_send.at[n], recv_sem=ici_recv.at[n],
                                            device_id=to, device_id_type=MESH)

    def d2d(n, outgoing):
        slot, part = ((kx, None), (ky, None), (kd, 0), (kd, 1))[n]
        piece = o_ref.at[slot, rows(mc if outgoing else 1 - mc, part)]
        return pltpu.make_async_remote_copy(src_ref=piece, dst_ref=piece, send_sem=d2d_send.at[n],
                                            recv_sem=d2d_recv.at[n], device_id=sib, device_id_type=MESH)

    def start():
        local().start()
        ici(0, True).start()
        ici(1, True).start()

    def wait():
        ici(1, False).wait_recv()
        ici(2, True).start()
        d2d(1, True).start()
        ici(0, False).wait_recv()
        ici(3, True).start()
        d2d(0, True).start()
        ici(2, False).wait_recv()
        d2d(2, True).start()
        ici(3, False).wait_recv()
        d2d(3, True).start()
        for n in range(4):
            ici(n, True).wait_send()
            d2d(n, True).wait_send()
            d2d(n, False).wait_recv()
        local().wait()

    return start, wait


def _ride_halves(x):
    dma4 = pltpu.SemaphoreType.DMA((4,))
    return dict(xs=[x], shapes=[jax.ShapeDtypeStruct((4,) + x.shape, x.dtype)],
                sems=[dma4, dma4, dma4, dma4, pltpu.SemaphoreType.DMA],
                ops=lambda x_refs, o_refs, sems: _gather_halves_ops(x_refs[0], o_refs[0], *sems))


def _exchange(rides, name):
    xs = [x for r in rides for x in r["xs"]]
    nx = len(xs)

    def body(*refs):
        x_refs, o_refs, sems = refs[:nx], refs[nx:2 * nx], refs[2 * nx:]
        ops, xo, so = [], 0, 0
        for r in rides:
            nr, ns = len(r["xs"]), len(r["sems"])
            ops.append(r["ops"](x_refs[xo:xo + nr], o_refs[xo:xo + nr], sems[so:so + ns]))
            xo, so = xo + nr, so + ns
        for start, _ in ops:
            start()
        for _, wait in ops:
            wait()

    return pl.pallas_call(body, name=name, in_specs=[ANY] * nx, out_specs=[ANY] * nx,
                          out_shape=[s for r in rides for s in r["shapes"]],
                          scratch_shapes=[s for r in rides for s in r["sems"]])(*xs)


def _chip_exchange_shapes(xs, gather):
    return [jax.ShapeDtypeStruct(((4,) + x.shape) if gather else x.shape, x.dtype) for x in xs]


def _chip_exchange_sems(n):
    return [pltpu.SemaphoreType.DMA((3 * n,)), pltpu.SemaphoreType.DMA((3 * n,)), pltpu.SemaphoreType.DMA((n,))]


def _chip_exchange_ops(x_refs, o_refs, send_sems, recv_sems, local_sems, gather):
    n = len(x_refs)
    mx, my, mc = lax.axis_index("x"), lax.axis_index("y"), lax.axis_index("c")
    k0 = 2 * mx + my

    def local(a):
        src = x_refs[a] if gather else x_refs[a].at[k0]
        return pltpu.make_async_copy(src, o_refs[a].at[k0], local_sems.at[a])

    def copy(a, j, outgoing):
        px, py = _flip(mx, j & 2), _flip(my, j & 1)
        kp = 2 * px + py
        if outgoing:
            src = x_refs[a] if gather else x_refs[a].at[kp]
            dst = o_refs[a].at[k0]
        else:
            src = x_refs[a] if gather else x_refs[a].at[k0]
            dst = o_refs[a].at[kp]
        s = a * 3 + j - 1
        return pltpu.make_async_remote_copy(
            src_ref=src, dst_ref=dst, send_sem=send_sems.at[s], recv_sem=recv_sems.at[s],
            device_id=(px, py, mc), device_id_type=MESH)

    def start():
        for a in range(n):
            local(a).start()
            for j in range(1, 4):
                copy(a, j, True).start()

    def wait():
        for a in range(n):
            for j in range(1, 4):
                copy(a, j, False).wait()
            local(a).wait()

    return start, wait


def _call(body, *, name, grid, in_specs, out_specs, out_shape, args, scratch_shapes=(), sem=None, ride=None):
    if not ride:
        return pl.pallas_call(
            body, name=name, grid=grid, in_specs=list(in_specs), out_specs=list(out_specs), out_shape=list(out_shape),
            scratch_shapes=list(scratch_shapes), compiler_params=_params(sem))(*args)
    xs = [x for r in ride for x in r["xs"]]
    shapes = [s for r in ride for s in r["shapes"]]
    sems = [s for r in ride for s in r["sems"]]
    n_in, n_out, n_scr, nx = len(in_specs), len(out_specs), len(scratch_shapes), len(xs)

    def wrapped(*refs):
        ins, x_refs = refs[:n_in], refs[n_in:n_in + nx]
        outs = refs[n_in + nx:n_in + nx + n_out]
        lands = refs[n_in + nx + n_out:n_in + 2 * nx + n_out]
        rest = refs[n_in + 2 * nx + n_out:]
        scr, sem_refs = rest[:n_scr], rest[n_scr:]
        ops, xo, so = [], 0, 0
        for r in ride:
            nr, ns = len(r["xs"]), len(r["sems"])
            ops.append(r["ops"](x_refs[xo:xo + nr], lands[xo:xo + nr], sem_refs[so:so + ns]))
            xo, so = xo + nr, so + ns
        ids = [pl.program_id(a) for a in range(len(grid))]
        first = functools.reduce(jnp.logical_and, [i == 0 for i in ids])
        last = functools.reduce(jnp.logical_and, [i == g - 1 for i, g in zip(ids, grid)])

        @pl.when(first)
        def _():
            for start, _ in ops:
                start()

        body(*ins, *outs, *scr)

        @pl.when(last)
        def _():
            for _, wait in ops:
                wait()

    return pl.pallas_call(
        wrapped, name=name, grid=grid, in_specs=list(in_specs) + [ANY] * nx, out_specs=list(out_specs) + [ANY] * nx,
        out_shape=list(out_shape) + shapes, scratch_shapes=list(scratch_shapes) + sems,
        compiler_params=_params(("arbitrary",) * len(grid)))(*args, *xs)


def _ride_sibling(xs):
    n = len(xs)

    def ops(x_refs, o_refs, sems):
        send_sems, recv_sems = sems
        sib = (lax.axis_index("x"), lax.axis_index("y"), 1 - lax.axis_index("c"))

        def copies():
            return [pltpu.make_async_remote_copy(
                src_ref=x_refs[a], dst_ref=o_refs[a], send_sem=send_sems.at[a], recv_sem=recv_sems.at[a],
                device_id=sib, device_id_type=MESH) for a in range(n)]

        def start():
            for cp in copies():
                cp.start()

        def wait():
            for cp in copies():
                cp.wait()

        return start, wait

    return dict(xs=list(xs), shapes=[jax.ShapeDtypeStruct(x.shape, x.dtype) for x in xs],
                sems=[pltpu.SemaphoreType.DMA((n,)), pltpu.SemaphoreType.DMA((n,))], ops=ops)


PACK = 1024
PACK_ROWS = 512


def _pack(parts, pad_rows=PACK_ROWS):
    flat = []
    for p in parts:
        v = p.reshape(-1).astype(F32)
        flat.append(jnp.pad(v, (0, (-v.shape[0]) % PACK)))
    total = sum(v.shape[0] for v in flat)
    flat.append(jnp.zeros(((-total) % (pad_rows * 128),), F32))
    return jnp.concatenate(flat).reshape(-1, 128)


def _shard_columns(shards, lo, hi):
    width = shards.shape[2]
    out = []
    for k in range(shards.shape[0]):
        a, b = max(lo, k * width), min(hi, (k + 1) * width)
        if a < b:
            out.append(shards[k, :, a - k * width:b - k * width])
    return out


def _unpack_rows(gathered, shapes):
    flat = gathered.reshape(gathered.shape[0], -1)
    out, off = [], 0
    for shp in shapes:
        n = math.prod(shp)
        out.append(flat[:, off:off + n].reshape((flat.shape[0],) + tuple(shp)))
        off += n + (-n) % PACK
    return out


def _unpack(packed, shapes):
    flat = packed.reshape(-1)
    out, off = [], 0
    for shp in shapes:
        n = math.prod(shp)
        out.append(flat[off:off + n].reshape(shp))
        off += n + (-n) % PACK
    return out


def kernel(x, c, ln_pre_g, ln_post_g, w_mod, b_mod, w_in_ab, w_out_ab, sgu_norm_g, sgu_w, sgu_b, w_in_ssm, w_out_ssm, lam_re, lam_im, b_re, b_im, c_re, c_im, d_skip, log_dt, w_glu, b_glu, loss_target, m_ln_pre_g, m_ln_post_g, m_w_mod, m_b_mod, m_w_in_ab, m_w_out_ab, m_sgu_norm_g, m_sgu_w, m_sgu_b, m_w_in_ssm, m_w_out_ssm, m_lam_re, m_lam_im, m_b_re, m_b_im, m_c_re, m_c_im, m_d_skip, m_log_dt, m_w_glu, m_b_glu, v_ln_pre_g, v_ln_post_g, v_w_mod, v_b_mod, v_w_in_ab, v_w_out_ab, v_sgu_norm_g, v_sgu_w, v_sgu_b, v_w_in_ssm, v_w_out_ssm, v_lam_re, v_lam_im, v_b_re, v_b_im, v_c_re, v_c_im, v_d_skip, v_log_dt, v_w_glu, v_b_glu):
    given = dict(locals())
    mx, my, mc = lax.axis_index("x"), lax.axis_index("y"), lax.axis_index("c")
    me = 4 * mx + 2 * my + mc
    chip = 2 * mx + my

    _, l, d = x.shape
    x2, tgt = x[0], loss_target[0]
    n_in = w_in_ab.shape[2] * 4
    wa = wb = n_in // 7
    w = w_out_ssm.shape[1]
    g, p, cch = b_re.shape[1:]
    nmod = w_mod.shape[2]


    cond = _silu_rows(c.reshape(d // 128, 128), "cond_silu")
    cond_all = _allgather8(cond, "gather_cond").reshape(8, d)
    b_shard = lax.dynamic_slice(b_mod, (0, chip * nmod), (2, nmod)).reshape(2, 1, nmod)
    cond_pad = jnp.pad(cond_all, ((0, MOD_ROWS - 8), (0, 0)))
    modp = _mod_fwd(cond_pad, w_mod, b_shard, "mod_fwd")[:, :8]
    modp_all = _allgather8(modp.reshape(16, nmod), "gather_mod").reshape(4, 2, 2, 8, nmod)
    mine = lax.dynamic_index_in_dim(lax.dynamic_index_in_dim(modp_all, mc, 1, False), me, 2, False)
    mod = mine.transpose(1, 0, 2).reshape(2, 3 * d)
    shift = [mod[a:a + 1, :d] for a in range(2)]
    scale = [mod[a:a + 1, d:2 * d] for a in range(2)]
    gate = [mod[a:a + 1, 2 * d:] for a in range(2)]
    pre_g = [ln_pre_g[a:a + 1] for a in range(2)]
    post_g = [ln_post_g[a:a + 1] for a in range(2)]

    sgu_w0, sgu_bt = sgu_w[0], sgu_b[0].T
    h0, (gw_in_ab,) = _pre_fwd(x2, pre_g[0], scale[0], shift[0], "pre0_fwd", ride=[_ride_halves(_bf(w_in_ab[0]))])
    w_gates = jnp.concatenate(_shard_columns(gw_in_ab, 0, 3 * wa) + _shard_columns(gw_in_ab, 3 * wa + 3 * wb, n_in),
                              axis=1)
    w_qkv = jnp.concatenate(_shard_columns(gw_in_ab, 3 * wa, 3 * wa + 3 * wb), axis=1)
    proj0, (gw_in_ssm,) = _matmul(h0, w_gates, "nn", BF16, "proj0", tm=1024, ride=[_ride_chip([_bf(w_in_ssm[0])], True)])
    qkv, (gw_out_ssm, gw_glu, g_dskip, g_bglu) = _matmul(
        h0, w_qkv, "nn", BF16, "proj0_qkv", tm=1024,
        ride=[_ride_chip([_bf(w_out_ssm[0]), _bf(w_glu[0]), d_skip, b_glu], True)])
    out_b, (gw_out_ab,) = _attn_fwd(qkv, wb, "attn_fwd", hp=8, ride=[_ride_chip([_bf(w_out_ab[0])], True)])
    wout_ab = gw_out_ab.reshape(wa + wb, d)
    win_ssm = gw_in_ssm.reshape(d, 2 * w)
    wout_ssm = jnp.concatenate([gw_out_ssm[k] for k in range(4)], axis=1)
    wglu = gw_glu.reshape(w, w)
    dskip_full = g_dskip.reshape(1, w)
    bglu_full = g_bglu.reshape(1, w)
    cat =_sgu_fwd(proj0, out_b, sgu_norm_g, sgu_w0, sgu_bt, wa, wb, "sgu_fwd")
    y0 = _matmul(cat, wout_ab, "nn", BF16, "out0", tm=1024)
    x1, h1 = _post_pre_fwd(x2, y0, gate[0], post_g[0], pre_g[1], scale[1], shift[1], "post0_pre1_fwd")

    s = g * p
    lr_c, li_c = lam_re.reshape(s, 1), lam_im.reshape(s, 1)
    ldt_c = jnp.repeat(log_dt.reshape(g), p).reshape(s, 1)
    br_c, bi_c = b_re.reshape(s, cch), b_im.reshape(s, cch)
    bb_re, bb_im, pw_re, pw_im = _ssm_prep(lr_c, li_c, ldt_c, br_c, bi_c, lr_c.reshape(1, s), li_c.reshape(1, s),
                                           ldt_c.reshape(1, s), "ssm_prep")
    bbd = _bf(_block_diag_b(bb_re, bb_im, g, p, cch))
    ccd = _bf(_block_diag_c(c_re[0], c_im[0], g, p, cch))
    proj1 = _matmul(h1, win_ssm, "nn", BF16, "proj1", tm=1024)
    y_ssm, hs_re, hs_im, h_all = _ssm_fwd(proj1, bbd, ccd, pw_re, pw_im, dskip_full, w, "ssm_fwd")
    o1 = _glu_fwd(y_ssm, proj1, wglu, bglu_full, "glu_fwd")
    y1 = _matmul(o1, wout_ssm, "nn", BF16, "out1", tm=1024)
    loss_vec, dy1, dx2, dgate1, dpost1 = _post_loss(x1, y1, gate[1], post_g[1], tgt, "post1_loss")

    do1 = _matmul(dy1, wout_ssm, "nt", BF16, "out1_dx", tm=1024)
    gr_wout_ssm = _matmul_at(o1, dy1, BF16, "out1_dw", n_split=4)
    dy_ssm, dz1, gr_wglu, gr_bglu = _glu_bwd(do1, y_ssm, proj1, wglu, bglu_full, "glu_bwd")
    du1, dbbd, dccd, da_re, da_im, gr_dskip, (ld_wout_ssm, ld_wglu) = _ssm_bwd(
        proj1, dy_ssm, hs_re, hs_im, h_all, bbd, ccd, pw_re, pw_im, dskip_full, w, "ssm_bwd",
        ride=[_ride_chip([gr_wout_ssm, gr_wglu.reshape(4, w // 4, w)], False)])
    dproj1 = jnp.concatenate([du1, dz1], axis=1)
    dh1 = _matmul(dproj1, win_ssm, "nt", BF16, "proj1_dx", tm=1024)
    gr_win_ssm = _matmul_at(h1, dproj1, BF16, "proj1_dw", tn=1024)
    dx1, dscale1, dshift1, dpre1, dy0, dgate0, dpost0 = _pre_bwd(
        dh1, dx2, x1, pre_g[1], scale[1], "pre1_post0_bwd", post=(y0, gate[0], post_g[0]))

    dcat = _matmul(dy0, wout_ab, "nt", BF16, "out0_dx", tm=1024)
    gr_wout_ab = _matmul_at(cat, dy0, BF16, "out0_dw", tn=1024)
    dbb_re, dbb_im = _diag_of_b(dbbd, g, p, cch)
    dc_re, dc_im = _diag_of_c(dccd, g, p, cch)
    part_a = [loss_vec[:, :1], dpre1, dpost0, dpost1, dgate0, dshift1, dscale1, dgate1, da_re, da_im,
              dbb_re, dbb_im, dc_re, dc_im, gr_dskip, gr_bglu]
    shapes_a = [a.shape for a in part_a]
    dq, dk, dv, (ld_win_ssm, ld_wout_ab, gath_a) = _attn_bwd(
        qkv, proj0, dcat, wa, wb, "attn_bwd", hp=4,
        ride=[_ride_chip([gr_win_ssm.reshape(4, d // 4, 2 * w), gr_wout_ab.reshape(4, (wa + wb) // 4, d)], False),
              _ride_all8(_pack(part_a))])
    early_names = ["w_out_ab", "w_in_ssm", "w_out_ssm", "w_glu"]
    early_sums = [_sum_leading(a, "sum_" + nm) for a, nm in zip([ld_wout_ab, ld_win_ssm, ld_wout_ssm, ld_wglu], early_names)]
    dproj0, gr_sgu_w, gr_sgu_bt, gr_sgu_g, early_sib = _sgu_bwd(
        proj0, out_b, dcat, dq, dk, dv, sgu_norm_g, sgu_w0, sgu_bt, wa, wb, "sgu_bwd", ride=[_ride_sibling(early_sums)])
    part_b = [gr_sgu_g, gr_sgu_w, gr_sgu_bt.T]
    shapes_b = [a.shape for a in part_b]
    gr_win_ab_lo, (gath_b,) = _matmul_at(h0, dproj0, BF16, "proj0_dw_lo", tm=512, tn=n_in // 4, n_split=4,
                                         m_part=(0, 1, 2), ride=[_ride_all8(_pack(part_b))])
    gr_win_ab_hi, (ld_win_ab_lo,) = _matmul_at(
        h0, dproj0, BF16, "proj0_dw_hi", tm=512, tn=n_in // 4, n_split=4, m_part=(1, 1, 2),
        ride=[_ride_chip([gr_win_ab_lo], False)])
    dh0, (ld_win_ab_hi,) = _matmul(dproj0, gw_in_ab, "nt", BF16, "proj0_dx", tm=1024, tn=1024,
                                   ride=[_ride_chip([gr_win_ab_hi], False)])
    grad_x, dscale0, dshift0, dpre0 = _pre_bwd(dh0, dx1, x2, pre_g[0], scale[0], "pre0_bwd")
    part_c = [dpre0, dshift0, dscale0]
    shapes_c = [a.shape for a in part_c]

    big_names = ["w_in_ab"] + early_names
    sum_win_ab = jnp.concatenate([_sum_leading(ld_win_ab_lo, "sum_w_in_ab_lo"),
                                  _sum_leading(ld_win_ab_hi, "sum_w_in_ab_hi")], axis=0)
    sums = [sum_win_ab] + early_sums
    gath_c, sib_win_ab = _exchange([_ride_all8(_pack(part_c, pad_rows=8)), _ride_sibling([sum_win_ab])], "tail_exchange")
    sib = [sib_win_ab] + list(early_sib)
    results = {}
    for nm, s_mine, s_sib in zip(big_names, sums, sib):
        shp = given[nm].shape
        two_d = lambda a: a.reshape(-1, shp[-1])
        outs = _adamw(two_d(given[nm]), [s_mine, s_sib], two_d(given["m_" + nm]), two_d(given["v_" + nm]),
                      "adamw_" + nm)
        results[nm] = [o.reshape(shp) for o in outs]

    (loss_s, g_pre1, g_post0, g_post1, g_gate0, g_shift1, g_scale1, g_gate1, s_da_re, s_da_im, s_dbb_re, s_dbb_im,
     g_c_re, g_c_im, g_dskip_full, g_bglu_full) = _unpack(_sum_leading(gath_a, "sum_small_a"), shapes_a)
    g_sgu_g, g_sgu_w, g_sgu_b = _unpack(_sum_leading(gath_b, "sum_small_b"), shapes_b)
    g_pre0, g_shift0, g_scale0 = _unpack(_sum_leading(gath_c, "sum_small_c"), shapes_c)
    loss = loss_s.reshape(())
    g_pre = jnp.concatenate([g_pre0, g_pre1], axis=0)
    g_post = jnp.concatenate([g_post0, g_post1], axis=0)
    g_bmod = jnp.concatenate([jnp.concatenate([g_shift0, g_scale0, g_gate0], axis=1),
                              jnp.concatenate([g_shift1, g_scale1, g_gate1], axis=1)], axis=0)

    g_lr, g_li, g_ldt, g_br, g_bi = _ssm_prep_bwd(lr_c, li_c, ldt_c, br_c, bi_c, s_da_re.reshape(s, 1),
                                                  s_da_im.reshape(s, 1), s_dbb_re, s_dbb_im, p, "ssm_prep_bwd")
    small = {
        "ln_pre_g": g_pre, "ln_post_g": g_post, "b_mod": g_bmod, "sgu_norm_g": g_sgu_g,
        "sgu_w": g_sgu_w.reshape(sgu_w.shape), "sgu_b": g_sgu_b.reshape(sgu_b.shape),
        "lam_re": g_lr.reshape(lam_re.shape), "lam_im": g_li.reshape(lam_im.shape),
        "b_re": g_br.reshape(b_re.shape), "b_im": g_bi.reshape(b_im.shape),
        "c_re": g_c_re.reshape(c_re.shape), "c_im": g_c_im.reshape(c_im.shape),
        "d_skip": lax.dynamic_slice(g_dskip_full, (0, chip * (w // 4)), (1, w // 4)),
        "log_dt": g_ldt.reshape(log_dt.shape),
        "b_glu": lax.dynamic_slice(g_bglu_full, (0, chip * (w // 4)), (1, w // 4)),
    }
    flat2 = lambda a: a.reshape(-1, a.shape[-1])
    wide = ("b_re", "b_im")
    for tag, group in (("adamw_small", [nm for nm in small if nm not in wide]), ("adamw_small_b", list(wide))):
        outs = _adamw_many([flat2(given[nm]) for nm in group], [flat2(small[nm]) for nm in group],
                           [flat2(given["m_" + nm]) for nm in group], [flat2(given["v_" + nm]) for nm in group], tag)
        for nm, trio in zip(group, outs):
            results[nm] = [small[nm]] + [o.reshape(given[nm].shape) for o in trio]

    rows_a = _unpack_rows(gath_a, shapes_a)
    rows_c = _unpack_rows(gath_c, shapes_c)
    dmod_rows = jnp.concatenate([rows_c[1], rows_c[2], rows_a[4], rows_a[5], rows_a[6], rows_a[7]],
                                axis=2).reshape(8, 2, 3 * d)
    dmod_shard = lax.dynamic_slice(dmod_rows, (0, 0, chip * nmod), (8, 2, nmod)).transpose(1, 0, 2)
    dmod_pad = jnp.pad(dmod_shard, ((0, 0), (0, MOD_ROWS - 8), (0, 0)))
    gr_wmod = _mod_bwd(cond_pad.T, dmod_pad, "mod_bwd")
    two_d = lambda a: a.reshape(-1, nmod)
    outs = _adamw(two_d(w_mod), [two_d(gr_wmod)], two_d(m_w_mod), two_d(v_w_mod), "adamw_w_mod")
    results["w_mod"] = [o.reshape(w_mod.shape) for o in outs]

    names = ["ln_pre_g", "ln_post_g", "w_mod", "b_mod", "w_in_ab", "w_out_ab", "sgu_norm_g", "sgu_w", "sgu_b",
             "w_in_ssm", "w_out_ssm", "lam_re", "lam_im", "b_re", "b_im", "c_re", "c_im", "d_skip", "log_dt",
             "w_glu", "b_glu"]
    return (loss, grad_x[None], *[results[nm][0] for nm in names], *[results[nm][1] for nm in names],
            *[results[nm][2] for nm in names], *[results[nm][3] for nm in names])
```

```python
import functools
import math

import jax
import jax.numpy as jnp
from jax import lax
from jax.experimental import pallas as pl
from jax.experimental.pallas import tpu as pltpu

F32 = jnp.float32
BF16 = jnp.bfloat16
MESH = pl.DeviceIdType.MESH

EPS = 1e-6
HEAD = 128
SSM_T = 512
SSM_GB = 16
ADAM_LR, ADAM_B1, ADAM_B2, ADAM_EPS, ADAM_WD, ADAM_STEP = 0.001, 0.9, 0.999, 1e-08, 0.01, 10
VMEM_LIMIT = 56 * 1024 * 1024

NN = (((1,), (0,)), ((), ()))
NT = (((1,), (1,)), ((), ()))
TN = (((0,), (0,)), ((), ()))


def _params(sem=None):
    return pltpu.CompilerParams(dimension_semantics=sem, vmem_limit_bytes=VMEM_LIMIT)


def _dot(a, b, dims=NN):
    return lax.dot_general(a, b, dims, preferred_element_type=F32)


def _bf(x):
    return x.astype(BF16)


def _gelu(x):
    k = math.sqrt(2.0 / math.pi)
    t = jnp.tanh(k * (x + 0.044715 * x * x * x))
    return 0.5 * x * (1.0 + t)


def _gelu_grad(x):
    k = math.sqrt(2.0 / math.pi)
    x2 = x * x
    t = jnp.tanh(k * (x + 0.044715 * x * x2))
    return 0.5 * (1.0 + t) + 0.5 * x * (1.0 - t * t) * k * (1.0 + 3.0 * 0.044715 * x2)


def _sigmoid(x):
    return 1.0 / (1.0 + jnp.exp(-x))


def _silu(x):
    return x * _sigmoid(x)


def _silu_grad(x):
    s = _sigmoid(x)
    return s * (1.0 + x * (1.0 - s))


def _tile(n, t, mult=128):
    if n <= t:
        return n
    for cand in range(t - t % mult, 0, -mult):
        if n % cand == 0:
            return cand
    raise ValueError((n, t, mult))


def _matmul(a, b, mode, out_dtype, name, tm=512, tn=512, tk=2048, n_split=1, ride=None, m_part=None):
    b_sharded = b.ndim == 3
    if mode == "nn":
        (m, kk), (_, n) = a.shape, b.shape
    elif b_sharded:
        assert mode == "nt"
        (m, kk), n, tk = a.shape, b.shape[1], b.shape[2]
    elif mode == "nt":
        (m, kk), (n, _) = a.shape, b.shape
    else:
        (kk, m), (_, n) = a.shape, b.shape
    m_off = 0
    if m_part is not None:
        assert mode == "tn"
        first, count, parts = m_part
        tm = _tile(m // parts, tm)
        m_off = first * (m // parts) // tm
        m = count * (m // parts)
    tm, tk = _tile(m, tm), _tile(kk, tk)
    ns = n // n_split
    tn = _tile(ns, tn)
    nk = kk // tk
    dims = {"nn": NN, "nt": NT, "tn": TN}[mode]

    def body(a_ref, b_ref, o_ref, acc_ref):
        k = pl.program_id(2)
        part = _dot(_bf(a_ref[...]), _bf(b_ref[0] if b_sharded else b_ref[...]), dims)

        @pl.when(k == 0)
        def _():
            acc_ref[...] = part

        @pl.when(k > 0)
        def _():
            acc_ref[...] += part

        @pl.when(k == nk - 1)
        def _():
            o_ref[...] = acc_ref[...].astype(out_dtype).reshape(o_ref.shape)

    if mode == "nn":
        a_spec = pl.BlockSpec((tm, tk), lambda i, j, k: (i, k))
        b_spec = pl.BlockSpec((tk, tn), lambda i, j, k: (k, j))
    elif mode == "nt":
        a_spec = pl.BlockSpec((tm, tk), lambda i, j, k: (i, k))
        b_spec = (pl.BlockSpec((1, tn, tk), lambda i, j, k: (k, j, 0)) if b_sharded
                  else pl.BlockSpec((tn, tk), lambda i, j, k: (j, k)))
    else:
        a_spec = pl.BlockSpec((tk, tm), lambda i, j, k: (k, i + m_off))
        b_spec = pl.BlockSpec((tk, tn), lambda i, j, k: (k, j))
    if n_split == 1:
        out_shape = jax.ShapeDtypeStruct((m, n), out_dtype)
        o_spec = pl.BlockSpec((tm, tn), lambda i, j, k: (i, j))
    else:
        per = ns // tn
        out_shape = jax.ShapeDtypeStruct((n_split, m, ns), out_dtype)
        o_spec = pl.BlockSpec((1, tm, tn), lambda i, j, k: (j // per, i, j % per))
    outs = _call(body, name=name, grid=(m // tm, n // tn, nk), in_specs=[a_spec, b_spec], out_specs=[o_spec],
                 out_shape=[out_shape], scratch_shapes=[pltpu.VMEM((tm, tn), F32)], args=(a, b),
                 sem=("parallel", "parallel", "arbitrary"), ride=ride)
    return outs[0] if ride is None else (outs[0], outs[1:])


def _matmul_at(a, b, out_dtype, name, tm=1024, tn=512, n_split=1, ride=None, m_part=None):
    (kk, m), (_, n) = a.shape, b.shape
    m_off = 0
    if m_part is not None:
        first, count, parts = m_part
        tm = _tile(m // parts, tm)
        m_off = first * (m // parts) // tm
        m = count * (m // parts)
    tm = _tile(m, tm)
    ns = n // n_split
    tn = _tile(ns, tn)
    kc = _tile(kk, 512)

    def body(a_ref, b_ref, o_ref, at_ref):
        @pl.when(pl.program_id(1) == 0)
        def _():
            for c in range(kk // kc):
                at_ref[:, c * kc:(c + 1) * kc] = _bf(a_ref[c * kc:(c + 1) * kc, :].astype(F32).T)

        o_ref[...] = _dot(at_ref[...], _bf(b_ref[...])).astype(out_dtype).reshape(o_ref.shape)

    if n_split == 1:
        out_shape = jax.ShapeDtypeStruct((m, n), out_dtype)
        o_spec = pl.BlockSpec((tm, tn), lambda i, j: (i, j))
    else:
        per = ns // tn
        out_shape = jax.ShapeDtypeStruct((n_split, m, ns), out_dtype)
        o_spec = pl.BlockSpec((1, tm, tn), lambda i, j: (j // per, i, j % per))
    outs = _call(body, name=name, grid=(m // tm, n // tn),
                 in_specs=[pl.BlockSpec((kk, tm), lambda i, j: (0, i + m_off)), pl.BlockSpec((kk, tn), lambda i, j: (0, j))],
                 out_specs=[o_spec], out_shape=[out_shape], scratch_shapes=[pltpu.VMEM((tm, kk), BF16)], args=(a, b),
                 sem=("arbitrary", "arbitrary"), ride=ride)
    return outs[0] if ride is None else (outs[0], outs[1:])


def _row_spec(tm, d):
    return pl.BlockSpec((tm, d), lambda i: (i, 0))


def _vec_spec(d):
    return pl.BlockSpec((1, d), lambda i: (0, 0))


def _acc(ref, first, val):
    @pl.when(first)
    def _():
        ref[...] = val

    @pl.when(jnp.logical_not(first))
    def _():
        ref[...] += val


def _colsum(x):
    return jnp.sum(x, axis=0, keepdims=True)


def _rownorm(x):
    r = lax.rsqrt(jnp.mean(x * x, axis=-1, keepdims=True) + EPS)
    return x * r, r


STRIP = 64


def _fold8(x):
    return functools.reduce(lambda a, b: a + b, [x[8 * k:8 * k + 8] for k in range(x.shape[0] // 8)])


def _pre_fwd(x, g, scale, shift, name, ride=None):
    l, d = x.shape
    tm = _tile(l, 256)

    def body(x_ref, g_ref, sc_ref, sh_ref, h_ref):
        n, _ = _rownorm(x_ref[...])
        h_ref[...] = _bf(n * g_ref[...] * (1.0 + sc_ref[...]) + sh_ref[...])

    outs = _call(body, name=name, grid=(l // tm,), in_specs=[_row_spec(tm, d), _vec_spec(d), _vec_spec(d), _vec_spec(d)],
                 out_specs=[_row_spec(tm, d)], out_shape=[jax.ShapeDtypeStruct((l, d), BF16)],
                 args=(x, g, scale, shift), sem=("parallel",), ride=ride)
    return outs[0], outs[1:]


def _post_pre_fwd(x, y, gate, pg, g1, scale1, shift1, name):
    l, d = x.shape
    tm = _tile(l, 256)

    def body(x_ref, y_ref, gate_ref, pg_ref, g1_ref, sc_ref, sh_ref, x1_ref, h1_ref):
        @pl.loop(0, tm // STRIP)
        def _(s):
            rows = pl.ds(pl.multiple_of(s * STRIP, STRIP), STRIP)
            ny, _ = _rownorm(y_ref[rows, :].astype(F32))
            x1 = x_ref[rows, :] + gate_ref[...] * (ny * pg_ref[...])
            x1_ref[rows, :] = x1
            n1, _ = _rownorm(x1)
            h1_ref[rows, :] = _bf(n1 * g1_ref[...] * (1.0 + sc_ref[...]) + sh_ref[...])

    v = _vec_spec(d)
    return pl.pallas_call(
        body, name=name, grid=(l // tm,),
        in_specs=[_row_spec(tm, d), _row_spec(tm, d), v, v, v, v, v],
        out_specs=[_row_spec(tm, d), _row_spec(tm, d)],
        out_shape=[jax.ShapeDtypeStruct((l, d), F32), jax.ShapeDtypeStruct((l, d), BF16)],
        compiler_params=_params(("parallel",)),
    )(x, y, gate, pg, g1, scale1, shift1)


def _post_loss(x1, y1, gate, pg, target, name):
    l, d = x1.shape
    tm = _tile(l, 256)

    def body(x_ref, y_ref, gate_ref, pg_ref, t_ref, loss_ref, dy_ref, dx_ref, dgate_ref, dpg_ref):
        first = pl.program_id(0) == 0

        def strip(s, sums):
            rows = pl.ds(pl.multiple_of(s * STRIP, STRIP), STRIP)
            ny, ry = _rownorm(y_ref[rows, :].astype(F32))
            q = ny * pg_ref[...]
            e = x_ref[rows, :] + gate_ref[...] * q - t_ref[rows, :]
            dx2 = e * (1.0 / d)
            dx_ref[rows, :] = dx2
            dq = dx2 * gate_ref[...]
            dny = dq * pg_ref[...]
            dy_ref[rows, :] = _bf(ry * (dny - ny * jnp.mean(dny * ny, axis=-1, keepdims=True)))
            return sums[0] + _fold8(e * e), sums[1] + _fold8(dx2 * q), sums[2] + _fold8(dq * ny)

        zero = jnp.zeros((8, d), F32)
        sq, dgate, dpg = lax.fori_loop(0, tm // STRIP, strip, (zero, zero, zero))
        _acc(loss_ref, first, jnp.full((1, 128), 0.5 / d, F32) * jnp.sum(sq))
        _acc(dgate_ref, first, _colsum(dgate))
        _acc(dpg_ref, first, _colsum(dpg))

    v = _vec_spec(d)
    return pl.pallas_call(
        body, name=name, grid=(l // tm,),
        in_specs=[_row_spec(tm, d), _row_spec(tm, d), v, v, _row_spec(tm, d)],
        out_specs=[_vec_spec(128), _row_spec(tm, d), _row_spec(tm, d), v, v],
        out_shape=[jax.ShapeDtypeStruct((1, 128), F32), jax.ShapeDtypeStruct((l, d), BF16),
                   jax.ShapeDtypeStruct((l, d), F32), jax.ShapeDtypeStruct((1, d), F32),
                   jax.ShapeDtypeStruct((1, d), F32)],
        compiler_params=_params(("arbitrary",)),
    )(x1, y1, gate, pg, target)


def _pre_bwd(dh, dres, x, g, scale, name, post=None):
    l, d = x.shape
    tm = _tile(l, 256)
    with_post = post is not None

    def body(*refs):
        if with_post:
            (dh_ref, dres_ref, x_ref, g_ref, sc_ref, y_ref, gate_ref, pg_ref,
             dx_ref, dsc_ref, dsh_ref, dg_ref, dy_ref, dgate_ref, dpg_ref) = refs
        else:
            dh_ref, dres_ref, x_ref, g_ref, sc_ref, dx_ref, dsc_ref, dsh_ref, dg_ref = refs
        first = pl.program_id(0) == 0

        def strip(s, sums):
            rows = pl.ds(pl.multiple_of(s * STRIP, STRIP), STRIP)
            dh = dh_ref[rows, :].astype(F32)
            n, r = _rownorm(x_ref[rows, :])
            dyn = dh * (1.0 + sc_ref[...])
            dn = dyn * g_ref[...]
            dx = dres_ref[rows, :] + r * (dn - n * jnp.mean(dn * n, axis=-1, keepdims=True))
            dx_ref[rows, :] = dx
            new = [sums[0] + _fold8(dh * (n * g_ref[...])), sums[1] + _fold8(dh), sums[2] + _fold8(dyn * n)]
            if with_post:
                ny, ry = _rownorm(y_ref[rows, :].astype(F32))
                dq = dx * gate_ref[...]
                dny = dq * pg_ref[...]
                dy_ref[rows, :] = _bf(ry * (dny - ny * jnp.mean(dny * ny, axis=-1, keepdims=True)))
                new += [sums[3] + _fold8(dx * (ny * pg_ref[...])), sums[4] + _fold8(dq * ny)]
            return tuple(new)

        zero = jnp.zeros((8, d), F32)
        sums = lax.fori_loop(0, tm // STRIP, strip, (zero,) * (5 if with_post else 3))
        outs = [dsc_ref, dsh_ref, dg_ref] + ([dgate_ref, dpg_ref] if with_post else [])
        for ref, acc in zip(outs, sums):
            _acc(ref, first, _colsum(acc))

    v = _vec_spec(d)
    row = _row_spec(tm, d)
    vec_out = jax.ShapeDtypeStruct((1, d), F32)
    in_specs = [row, row, row, v, v]
    args = [dh, dres, x, g, scale]
    out_specs = [row, v, v, v]
    out_shape = [jax.ShapeDtypeStruct((l, d), F32), vec_out, vec_out, vec_out]
    if with_post:
        in_specs += [row, v, v]
        args += list(post)
        out_specs += [row, v, v]
        out_shape += [jax.ShapeDtypeStruct((l, d), BF16), vec_out, vec_out]
    return pl.pallas_call(
        body, name=name, grid=(l // tm,), in_specs=in_specs, out_specs=out_specs, out_shape=out_shape,
        compiler_params=_params(("arbitrary",)),
    )(*args)


def _softplus_parts(z):
    e = jnp.exp(-jnp.abs(z))
    den = 1.0 + e
    lb = jnp.minimum(z, 0.0) - jnp.log(den)
    return lb, lb - z, jnp.exp(lb)


def _tri(cmp, n=HEAD):
    row = lax.broadcasted_iota(jnp.int32, (n, n), 0)
    col = lax.broadcasted_iota(jnp.int32, (n, n), 1)
    return cmp(row, col)


ATT_T = 128
ATT_DEAD = 104.0


def _any_alive(runs):
    return functools.reduce(jnp.maximum, [jnp.max(r) for r in runs]) > -ATT_DEAD


def _attn_fwd(qkv, wb, name, hp=4, ride=None):
    l = qkv.shape[0]
    t = ATT_T
    nh, nq = wb // HEAD, l // t
    hp = min(hp, nh)
    ng, wg = nh // hp, hp * HEAD
    scale = 1.0 / math.sqrt(HEAD)

    def body(q_ref, k_ref, v_ref, o_ref):
        i = pl.program_id(1)
        valid = _tri(lambda r, c: c < r, t)
        m_gt = _bf(_tri(lambda r, c: r > c, t).astype(F32))

        def tile(j, carry, diag):
            rows = pl.ds(pl.multiple_of(j * t, t), t)
            cols = [slice(hh * HEAD, (hh + 1) * HEAD) for hh in range(hp)]
            zs = [_dot(q_ref[:, cs], k_ref[rows, cs], NT) * scale for cs in cols]
            lbs, lks = [], []
            for z in zs:
                lb, lk, _ = _softplus_parts(z)
                lbs.append(lb)
                lks.append(jnp.where(valid, lk, 0.0) if diag else lk)
            laters = [_dot(_bf(lk), m_gt) for lk in lks]
            ws = [jnp.exp(lb + later + run) for lb, later, (_, run) in zip(lbs, laters, carry)]
            if diag:
                ws = [jnp.where(valid, w, 0.0) for w in ws]
            return tuple((acc + _dot(_bf(w), v_ref[rows, cs]), run + jnp.sum(lk, axis=1, keepdims=True))
                         for w, lk, cs, (acc, run) in zip(ws, lks, cols, carry))

        zero = (jnp.zeros((t, HEAD), F32), jnp.zeros((t, 1), F32))
        carry = tile(i, (zero,) * hp, True)
        _, carry = lax.while_loop(lambda c: (c[0] < i) & _any_alive([run for _, run in c[1]]),
                                  lambda c: (c[0] + 1, tile(i - 1 - c[0], c[1], False)), (jnp.int32(0), carry))
        for hh, (acc, _) in enumerate(carry):
            o_ref[:, hh * HEAD:(hh + 1) * HEAD] = acc

    blk = lambda off: pl.BlockSpec((t, wg), lambda h, i: (i, off + h))
    full = lambda off: pl.BlockSpec((l, wg), lambda h, i: (0, off + h))
    out = pl.BlockSpec((t, wg), lambda h, i: (i, h))
    outs = _call(body, name=name, grid=(ng, nq), in_specs=[blk(0), full(ng), full(2 * ng)], out_specs=[out],
                 out_shape=[jax.ShapeDtypeStruct((l, wb), F32)],
                 args=(qkv, qkv, qkv), sem=("parallel", "arbitrary"), ride=ride)
    return outs[0], outs[1:]


def _attn_bwd(qkv, proj, dcat, wa, wb, name, hp=2, ride=None):
    l = qkv.shape[0]
    t = ATT_T
    nh, nq = wb // HEAD, l // t
    hp = min(hp, nh)
    ng, wg = nh // hp, hp * HEAD
    scale = 1.0 / math.sqrt(HEAD)

    def body(q_ref, k_ref, v_ref, bz_ref, dc_ref, dq_ref, dkt_out, dvt_out, do_s, qt_s, dot_s,
             dkt_ref, dvt_ref, out_sems):
        i = pl.program_id(1)

        @pl.when(i == 0)
        def _():
            dkt_ref[...] = jnp.zeros_like(dkt_ref)
            dvt_ref[...] = jnp.zeros_like(dvt_ref)

        do = dc_ref[...].astype(F32) * _silu(bz_ref[...].astype(F32))
        do_s[...] = _bf(do)
        for hh in range(hp):
            cs = slice(hh * HEAD, (hh + 1) * HEAD)
            qt_s[hh] = _bf(q_ref[:, cs].astype(F32).T * scale)
            dot_s[hh] = _bf(do[:, cs].T)
        valid = _tri(lambda r, c: c < r, t)
        m_le = _bf(_tri(lambda r, c: r <= c, t).astype(F32))
        m_lt = _bf(_tri(lambda r, c: r < c, t).astype(F32))

        heads = range(hp)
        cols = [slice(hh * HEAD, (hh + 1) * HEAD) for hh in heads]

        def row_sums(j, runs, diag):
            rows = pl.ds(pl.multiple_of(j * t, t), t)
            out = []
            for cs, run in zip(cols, runs):
                _, lk, _ = _softplus_parts(_dot(q_ref[:, cs], k_ref[rows, cs], NT) * scale)
                if diag:
                    lk = jnp.where(valid, lk, 0.0)
                out.append(run + jnp.sum(lk, axis=1, keepdims=True))
            return tuple(out)

        runs = row_sums(i, (jnp.zeros((t, 1), F32),) * hp, True)
        below, lktot = lax.while_loop(lambda c: (c[0] < i) & _any_alive(c[1]),
                                      lambda c: (c[0] + 1, row_sums(i - 1 - c[0], c[1], False)), (jnp.int32(0), runs))

        def tile(j, carry, diag):
            rows = pl.ds(pl.multiple_of(j * t, t), t)
            zs = [_dot(q_ref[:, cs], k_ref[rows, cs], NT) * scale for cs in cols]
            dws = [_dot(do_s[:, cs], v_ref[rows, cs], NT) for cs in cols]
            lbs, lks, sigs = [], [], []
            for z in zs:
                lb, lk, sig = _softplus_parts(z)
                lbs.append(lb)
                lks.append(jnp.where(valid, lk, 0.0) if diag else lk)
                sigs.append(sig)
            pins = [_dot(_bf(lk), m_le) for lk in lks]
            ws = [jnp.exp(lbs[hh] + (lktot[hh] - carry[hh][1]) - pins[hh]) for hh in heads]
            if diag:
                ws = [jnp.where(valid, w, 0.0) for w in ws]
            das = [dw * w for dw, w in zip(dws, ws)]
            pexs = [_dot(_bf(da), m_lt) for da in das]
            dzs = [das[hh] - sigs[hh] * (das[hh] + carry[hh][2] + pexs[hh]) for hh in heads]
            if diag:
                dzs = [jnp.where(valid, dz, 0.0) for dz in dzs]
            dzs = [_bf(dz) for dz in dzs]
            out = []
            for hh in heads:
                dkt, dvt = _dot(qt_s[hh], dzs[hh]), _dot(dot_s[hh], _bf(ws[hh]))
                for half in range(t // HEAD):
                    dkt_ref[hh, sub * j + half] += dkt[:, half * HEAD:(half + 1) * HEAD]
                    dvt_ref[hh, sub * j + half] += dvt[:, half * HEAD:(half + 1) * HEAD]
                dq, cpre, ppre = carry[hh]
                out.append((dq + _dot(dzs[hh], k_ref[rows, cols[hh]]), cpre + jnp.sum(lks[hh], axis=1, keepdims=True),
                            ppre + pexs[hh][:, t - 1:] + das[hh][:, t - 1:]))
            return tuple(out)

        zero = (jnp.zeros((t, HEAD), F32), jnp.zeros((t, 1), F32), jnp.zeros((t, 1), F32))
        carry = lax.fori_loop(i - below, i, lambda j, c: tile(j, c, False), (zero,) * hp)
        carry = tile(i, carry, True)
        for hh in range(hp):
            dq_ref[:, hh * HEAD:(hh + 1) * HEAD] = carry[hh][0] * scale

        @pl.when(i == nq - 1)
        def _():
            heads = pl.ds(pl.program_id(0) * hp, hp)
            copies = [pltpu.make_async_copy(dkt_ref, dkt_out.at[heads], out_sems.at[0]),
                      pltpu.make_async_copy(dvt_ref, dvt_out.at[heads], out_sems.at[1])]
            for cp in copies:
                cp.start()
            for cp in copies:
                cp.wait()

    sub = t // HEAD
    blk = lambda off: pl.BlockSpec((t, wg), lambda h, i: (i, off + h))
    full = lambda off: pl.BlockSpec((l, wg), lambda h, i: (0, off + h))
    acc_shape = jax.ShapeDtypeStruct((nh, l // HEAD, HEAD, HEAD), F32)
    acc_scratch = pltpu.VMEM((hp, l // HEAD, HEAD, HEAD), F32)
    outs = _call(
        body, name=name, grid=(ng, nq),
        in_specs=[blk(0), full(ng), full(2 * ng), blk(3 * wa // wg), blk(wa // wg)],
        out_specs=[blk(0), ANY, ANY], out_shape=[jax.ShapeDtypeStruct((l, wb), F32), acc_shape, acc_shape],
        scratch_shapes=[pltpu.VMEM((t, wg), BF16), pltpu.VMEM((hp, HEAD, t), BF16), pltpu.VMEM((hp, HEAD, t), BF16),
                        acc_scratch, acc_scratch, pltpu.SemaphoreType.DMA((2,))],
        args=(qkv, qkv, qkv, proj, dcat), sem=("parallel", "arbitrary"), ride=ride)
    return outs[0], outs[1], outs[2], outs[3:]


def _sgu_heads(v, g_ref, w_ref, bt_ref, nh):
    keep = _tri(lambda r, c: r >= c)
    out = []
    for h in range(nh):
        cols = slice(h * HEAD, (h + 1) * HEAD)
        nv, r = _rownorm(v[:, cols])
        wm = jnp.where(keep, w_ref[h], 0.0)
        s = _dot(_bf(wm), _bf(nv * g_ref[:, cols])) + bt_ref[:, h:h + 1]
        out.append((nv, r, wm, s))
    return out


def _sgu_fwd(proj, out_b, norm_g, sgu_w, sgu_bt, wa, wb, name):
    l, n = proj.shape
    nh = wa // HEAD

    def body(au_ref, av_ref, az_ref, bz_ref, ob_ref, g_ref, w_ref, bt_ref, cat_ref):
        u, v, sz = _gelu(au_ref[...].astype(F32)), _gelu(av_ref[...].astype(F32)), _silu(az_ref[...].astype(F32))
        for h, (_, _, _, s) in enumerate(_sgu_heads(v, g_ref, w_ref, bt_ref, nh)):
            cols = slice(h * HEAD, (h + 1) * HEAD)
            cat_ref[:, cols] = _bf(u[:, cols] * s * sz[:, cols])
        cat_ref[:, wa:] = _bf(ob_ref[...] * _silu(bz_ref[...].astype(F32)))

    a_blk = lambda j: pl.BlockSpec((HEAD, wa), lambda i: (i, j))
    return pl.pallas_call(
        body, name=name, grid=(l // HEAD,),
        in_specs=[a_blk(0), a_blk(1), a_blk(2), a_blk(3), pl.BlockSpec((HEAD, wb), lambda i: (i, 0)),
                  _vec_spec(wa), pl.BlockSpec((nh, HEAD, HEAD), lambda i: (0, 0, 0)),
                  pl.BlockSpec((HEAD, nh), lambda i: (0, 0))],
        out_specs=pl.BlockSpec((HEAD, wa + wb), lambda i: (i, 0)),
        out_shape=jax.ShapeDtypeStruct((l, wa + wb), BF16),
        compiler_params=_params(("parallel",)),
    )(proj, proj, proj, proj, out_b, norm_g, sgu_w, sgu_bt)


def _sgu_bwd(proj, out_b, dcat, dq, dk, dv, norm_g, sgu_w, sgu_bt, wa, wb, name, ride=None):
    l = proj.shape[0]
    n = 3 * wa + 4 * wb
    nh = wa // HEAD

    def body(au_ref, av_ref, az_ref, bz_ref, ob_ref, dc_ref, dq_ref, dk_ref, dv_ref, g_ref, w_ref, wt_ref, bt_ref,
             dp_ref, dw_ref, dbt_ref, dg_ref):
        first = pl.program_id(0) == 0
        keep = _tri(lambda r, c: r >= c)
        au, av, az = au_ref[...].astype(F32), av_ref[...].astype(F32), az_ref[...].astype(F32)
        u, v, sz = _gelu(au), _gelu(av), _silu(az)
        dgelu_u, dgelu_v, dsilu_z = _gelu_grad(au), _gelu_grad(av), _silu_grad(az)
        heads = _sgu_heads(v, g_ref, w_ref, bt_ref, nh)
        cols = [slice(h * HEAD, (h + 1) * HEAD) for h in range(nh)]
        dss = []
        for h, (nv, r, wm, s) in enumerate(heads):
            dca, uh, szh = dc_ref[:, cols[h]].astype(F32), u[:, cols[h]], sz[:, cols[h]]
            dp_ref[:, cols[h]] = _bf(dca * s * szh * dgelu_u[:, cols[h]])
            dp_ref[:, 2 * wa + h * HEAD:2 * wa + (h + 1) * HEAD] = _bf(dca * uh * s * dsilu_z[:, cols[h]])
            dss.append(dca * uh * szh)
        dws = [_dot(_bf(ds), _bf(nv * g_ref[:, cs]), NT) for ds, cs, (nv, _, _, _) in zip(dss, cols, heads)]
        keep_t = _tri(lambda r, c: r <= c)
        dvhs = [_dot(_bf(jnp.where(keep_t, wt_ref[h], 0.0)), _bf(dss[h])) for h in range(nh)]
        dg_parts = []
        for h, (nv, r, wm, s) in enumerate(heads):
            _acc(dw_ref.at[h], first, jnp.where(keep, dws[h], 0.0))
            _acc(dbt_ref.at[:, h:h + 1], first, jnp.sum(dss[h], axis=1, keepdims=True))
            dg_parts.append(_colsum(dvhs[h] * nv))
            dnv = dvhs[h] * g_ref[:, cols[h]]
            dvv = r * (dnv - nv * jnp.mean(dnv * nv, axis=-1, keepdims=True))
            dp_ref[:, wa + h * HEAD:wa + (h + 1) * HEAD] = _bf(dvv * dgelu_v[:, cols[h]])
        _acc(dg_ref, first, jnp.concatenate(dg_parts, axis=1))
        base = 3 * wa
        dp_ref[:, base:base + wb] = _bf(dq_ref[...])
        for h in range(wb // HEAD):
            dp_ref[:, base + wb + h * HEAD:base + wb + (h + 1) * HEAD] = _bf(dk_ref[h, 0].T)
            dp_ref[:, base + 2 * wb + h * HEAD:base + 2 * wb + (h + 1) * HEAD] = _bf(dv_ref[h, 0].T)
        dp_ref[:, base + 3 * wb:] = _bf(dc_ref[:, wa:].astype(F32) * ob_ref[...]
                                        * _silu_grad(bz_ref[...].astype(F32)))

    a_blk = lambda j: pl.BlockSpec((HEAD, wa), lambda i: (i, j))
    b_blk = pl.BlockSpec((HEAD, wb), lambda i: (i, 0))
    t_blk = pl.BlockSpec((wb // HEAD, 1, HEAD, HEAD), lambda i: (0, i, 0, 0))
    w_spec = pl.BlockSpec((nh, HEAD, HEAD), lambda i: (0, 0, 0))
    bt_spec = pl.BlockSpec((HEAD, nh), lambda i: (0, 0))
    outs = _call(
        body, name=name, grid=(l // HEAD,), ride=ride, sem=("arbitrary",),
        in_specs=[a_blk(0), a_blk(1), a_blk(2), a_blk(3), b_blk, pl.BlockSpec((HEAD, wa + wb), lambda i: (i, 0)),
                  b_blk, t_blk, t_blk, _vec_spec(wa), w_spec, w_spec, bt_spec],
        out_specs=[pl.BlockSpec((HEAD, n), lambda i: (i, 0)), w_spec, bt_spec, _vec_spec(wa)],
        out_shape=[jax.ShapeDtypeStruct((l, n), BF16), jax.ShapeDtypeStruct((nh, HEAD, HEAD), F32),
                   jax.ShapeDtypeStruct((HEAD, nh), F32), jax.ShapeDtypeStruct((1, wa), F32)],
        args=(proj, proj, proj, proj, out_b, dcat, dq, dk, dv, norm_g, sgu_w, sgu_w.transpose(0, 2, 1), sgu_bt))
    return (*outs[:4], outs[4:])


def _ssm_discretise(lr, li, ldt, br, bi):
    dt = jnp.exp(ldt)
    mag = jnp.exp(lr * dt)
    a_re = mag * jnp.cos(li * dt)
    a_im = mag * jnp.sin(li * dt)
    den = lr * lr + li * li
    nr = a_re - 1.0
    coef_re = (nr * lr + a_im * li) / den
    coef_im = (a_im * lr - nr * li) / den
    return a_re, a_im, coef_re * br - coef_im * bi, coef_re * bi + coef_im * br


def _ssm_prep(lr, li, ldt, br, bi, lr_row, li_row, ldt_row, name):
    s, c = br.shape

    def body(lr_ref, li_ref, ldt_ref, br_ref, bi_ref, lrr_ref, lir_ref, ldtr_ref, bbr_ref, bbi_ref, tr_ref, ti_ref):
        _, _, bbr, bbi = _ssm_discretise(lr_ref[...], li_ref[...], ldt_ref[...], br_ref[...], bi_ref[...])
        bbr_ref[...] = bbr
        bbi_ref[...] = bbi
        row = lax.broadcasted_iota(jnp.int32, (SCAN_ROWS, 1), 0)
        blk, r = jnp.right_shift(row, 3), jnp.bitwise_and(row, 7)
        kind, rev = jnp.bitwise_and(blk, 3), blk >= 4
        step = jnp.left_shift(1, kind)
        n = jnp.where(kind < 3, step, jnp.where(rev, 8 - r, r + 1)).astype(F32)
        keep = (kind == 3) | (rev & (r < 8 - step)) | (jnp.logical_not(rev) & (r >= step))
        dt = jnp.exp(ldtr_ref[...])
        mag = jnp.exp(n * (lrr_ref[...] * dt))
        ang = n * (lir_ref[...] * dt)
        tr_ref[...] = jnp.where(keep, mag * jnp.cos(ang), 0.0)
        ti_ref[...] = jnp.where(keep, jnp.where(rev, -1.0, 1.0) * mag * jnp.sin(ang), 0.0)

    col = jax.ShapeDtypeStruct((s, c), F32)
    row = jax.ShapeDtypeStruct((SCAN_ROWS, s), F32)
    return pl.pallas_call(body, name=name, out_shape=[col, col, row, row])(
        lr, li, ldt, br, bi, lr_row, li_row, ldt_row)


def _ssm_prep_bwd(lr, li, ldt, br, bi, da_re, da_im, dbb_re, dbb_im, p, name):
    s, c = br.shape

    def body(lr_ref, li_ref, ldt_ref, br_ref, bi_ref, dar_ref, dai_ref, dbr_ref, dbi_ref,
             dlr_ref, dli_ref, dldt_ref, dbre_ref, dbim_ref):
        args = (lr_ref[...], li_ref[...], ldt_ref[...], br_ref[...], bi_ref[...])
        _, vjp = jax.vjp(_ssm_discretise, *args)
        dlr, dli, dldt, dbr, dbi = vjp((dar_ref[...], dai_ref[...], dbr_ref[...], dbi_ref[...]))
        dlr_ref[...] = dlr
        dli_ref[...] = dli
        dbre_ref[...] = dbr
        dbim_ref[...] = dbi
        idx = lax.broadcasted_iota(jnp.int32, (s, s // p), 0)
        grp = lax.broadcasted_iota(jnp.int32, (s, s // p), 1)
        own = (idx >= grp * p) & (idx < (grp + 1) * p)
        dldt_ref[...] = _colsum(jnp.where(own, dldt, 0.0))

    col1 = jax.ShapeDtypeStruct((s, 1), F32)
    colc = jax.ShapeDtypeStruct((s, c), F32)
    return pl.pallas_call(
        body, name=name, out_shape=[col1, col1, jax.ShapeDtypeStruct((1, s // p), F32), colc, colc],
    )(lr, li, ldt, br, bi, da_re, da_im, dbb_re, dbb_im)


SCAN_ROWS = 64


def _scan_groups(xr, xi, tr_ref, ti_ref, cr, ci, reverse):
    ng = xr.shape[0] // 8
    base = SCAN_ROWS // 2 if reverse else 0
    pr, pi = tr_ref[base + 24:base + 32, :], ti_ref[base + 24:base + 32, :]
    edge = slice(0, 1) if reverse else slice(7, 8)
    out_r, out_i = [None] * ng, [None] * ng
    for g in (range(ng - 1, -1, -1) if reverse else range(ng)):
        sr, si = xr[8 * g:8 * g + 8, :], xi[8 * g:8 * g + 8, :]
        for k in range(3):
            ar, ai = tr_ref[base + 8 * k:base + 8 * k + 8, :], ti_ref[base + 8 * k:base + 8 * k + 8, :]
            shift = 8 - (1 << k) if reverse else 1 << k
            rr, ri = pltpu.roll(sr, shift, 0), pltpu.roll(si, shift, 0)
            sr, si = sr + ar * rr - ai * ri, si + ar * ri + ai * rr
        sr, si = sr + pr * cr - pi * ci, si + pr * ci + pi * cr
        cr, ci = sr[edge, :], si[edge, :]
        out_r[g], out_i[g] = sr, si
    return jnp.concatenate(out_r, axis=0), jnp.concatenate(out_i, axis=0), cr, ci


def _ssm_fwd(proj, bbd, ccd, pw_re, pw_im, d_skip, w, name):
    l = proj.shape[0]
    nb, cw, ns2 = bbd.shape
    ns = ns2 // 2
    nc = l // SSM_T

    def body(u_ref, bbd_ref, ccd_ref, pr_ref, pi_ref, d_ref, y_ref, hsr_ref, hsi_ref, h_ref, hr_s, hi_s):
        @pl.when(pl.program_id(1) == 0)
        def _():
            hr_s[...] = jnp.zeros_like(hr_s)
            hi_s[...] = jnp.zeros_like(hi_s)

        hsr_ref[...] = hr_s[...].reshape(hsr_ref.shape)
        hsi_ref[...] = hi_s[...].reshape(hsi_ref.shape)
        u = u_ref[...]
        bu = _dot(_bf(u), bbd_ref[0])
        hr, hi, cr, ci = _scan_groups(bu[:, :ns], bu[:, ns:], pr_ref, pi_ref, hr_s[...], hi_s[...], False)
        hr_s[...] = cr
        hi_s[...] = ci
        h_bf = _bf(jnp.concatenate([hr, hi], axis=1))
        h_ref[...] = h_bf
        y_ref[...] = _dot(h_bf, ccd_ref[0]) + d_ref[...] * u

    tab = pl.BlockSpec((SCAN_ROWS, ns), lambda b, k: (0, b))
    return pl.pallas_call(
        body, name=name, grid=(nb, nc),
        in_specs=[pl.BlockSpec((SSM_T, cw), lambda b, k: (k, b)),
                  pl.BlockSpec((1, cw, ns2), lambda b, k: (b, 0, 0)),
                  pl.BlockSpec((1, ns2, cw), lambda b, k: (b, 0, 0)),
                  tab, tab, pl.BlockSpec((1, cw), lambda b, k: (0, b))],
        out_specs=[pl.BlockSpec((SSM_T, cw), lambda b, k: (k, b)),
                   pl.BlockSpec((1, 1, ns), lambda b, k: (k, 0, b)), pl.BlockSpec((1, 1, ns), lambda b, k: (k, 0, b)),
                   pl.BlockSpec((SSM_T, ns2), lambda b, k: (k, b))],
        out_shape=[jax.ShapeDtypeStruct((l, w), F32), jax.ShapeDtypeStruct((nc, 1, nb * ns), F32),
                   jax.ShapeDtypeStruct((nc, 1, nb * ns), F32), jax.ShapeDtypeStruct((l, nb * ns2), BF16)],
        scratch_shapes=[pltpu.VMEM((1, ns), F32), pltpu.VMEM((1, ns), F32)],
        compiler_params=_params(("parallel", "arbitrary")),
    )(proj, bbd, ccd, pw_re, pw_im, d_skip)


def _ssm_bwd(proj, dy, hs_re, hs_im, h_all, bbd, ccd, pw_re, pw_im, d_skip, w, name, ride=None):
    l = proj.shape[0]
    nb, cw, ns2 = bbd.shape
    ns = ns2 // 2
    nc = l // SSM_T

    def body(u_ref, dy_ref, hsr_ref, hsi_ref, h_ref, bbd_ref, ccd_ref, pr_ref, pi_ref, d_ref,
             du_ref, dbbd_ref, dccd_ref, dar_ref, dai_ref, dd_ref, gr_s, gi_s):
        first = pl.program_id(1) == 0

        @pl.when(first)
        def _():
            gr_s[...] = jnp.zeros_like(gr_s)
            gi_s[...] = jnp.zeros_like(gi_s)

        u, dy = u_ref[...], dy_ref[...]
        dy_bf = _bf(dy)
        hr0, hi0 = hsr_ref[0], hsi_ref[0]
        h = h_ref[...].astype(F32)
        hr, hi = h[:, :ns], h[:, ns:]
        dh = _dot(dy_bf, ccd_ref[0], NT)
        gr, gi, gcr, gci = _scan_groups(dh[:, :ns], dh[:, ns:], pr_ref, pi_ref, gr_s[...], gi_s[...], True)
        gr_s[...] = gcr
        gi_s[...] = gci
        row0 = lax.broadcasted_iota(jnp.int32, hr.shape, 0) == 0
        pr_h = jnp.where(row0, hr0, pltpu.roll(hr, 1, 0))
        pi_h = jnp.where(row0, hi0, pltpu.roll(hi, 1, 0))
        _acc(dar_ref, first, _colsum(pr_h * gr + pi_h * gi))
        _acc(dai_ref, first, _colsum(pr_h * gi - pi_h * gr))
        g_bf = _bf(jnp.concatenate([gr, gi], axis=1))
        _acc(dbbd_ref.at[0], first, _dot(_bf(u.T), g_bf))
        _acc(dccd_ref.at[0], first, _dot(_bf(h.T), dy_bf))
        du_ref[...] = _bf(_dot(g_bf, bbd_ref[0], NT) + d_ref[...] * dy)
        _acc(dd_ref, first, _colsum(dy * u))

    rev = lambda b, k: (nc - 1 - k, b)
    outs = _call(
        body, name=name, grid=(nb, nc), ride=ride, sem=("parallel", "arbitrary"),
        args=(proj, dy, hs_re, hs_im, h_all, bbd, ccd, pw_re, pw_im, d_skip),
        in_specs=[pl.BlockSpec((SSM_T, cw), rev), pl.BlockSpec((SSM_T, cw), rev),
                  pl.BlockSpec((1, 1, ns), lambda b, k: (nc - 1 - k, 0, b)),
                  pl.BlockSpec((1, 1, ns), lambda b, k: (nc - 1 - k, 0, b)),
                  pl.BlockSpec((SSM_T, ns2), rev),
                  pl.BlockSpec((1, cw, ns2), lambda b, k: (b, 0, 0)),
                  pl.BlockSpec((1, ns2, cw), lambda b, k: (b, 0, 0)),
                  pl.BlockSpec((SCAN_ROWS, ns), lambda b, k: (0, b)), pl.BlockSpec((SCAN_ROWS, ns), lambda b, k: (0, b)),
                  pl.BlockSpec((1, cw), lambda b, k: (0, b))],
        out_specs=[pl.BlockSpec((SSM_T, cw), rev),
                   pl.BlockSpec((1, cw, ns2), lambda b, k: (b, 0, 0)),
                   pl.BlockSpec((1, ns2, cw), lambda b, k: (b, 0, 0)),
                   pl.BlockSpec((1, ns), lambda b, k: (0, b)), pl.BlockSpec((1, ns), lambda b, k: (0, b)),
                   pl.BlockSpec((1, cw), lambda b, k: (0, b))],
        out_shape=[jax.ShapeDtypeStruct((l, w), BF16), jax.ShapeDtypeStruct(bbd.shape, F32),
                   jax.ShapeDtypeStruct(ccd.shape, F32), jax.ShapeDtypeStruct((1, nb * ns), F32),
                   jax.ShapeDtypeStruct((1, nb * ns), F32), jax.ShapeDtypeStruct((1, w), F32)],
        scratch_shapes=[pltpu.VMEM((1, ns), F32), pltpu.VMEM((1, ns), F32)])
    return (*outs[:6], outs[6:])


def _block_diag_b(bb_re, bb_im, g, p, c):
    nb = g // SSM_GB
    keep = _same_group(SSM_GB * c, c, SSM_GB * p, p)

    def one(bb):
        t = bb.reshape(nb, SSM_GB, p, c).transpose(0, 1, 3, 2).reshape(nb, SSM_GB * c, p)
        return jnp.where(keep, jnp.tile(t, (1, 1, SSM_GB)), 0.0)

    return jnp.concatenate([one(bb_re), one(bb_im)], axis=2)


def _same_group(rows, per_row, cols, per_col):
    r = lax.broadcasted_iota(jnp.int32, (rows, cols), 0) // per_row
    q = lax.broadcasted_iota(jnp.int32, (rows, cols), 1) // per_col
    return r == q


def _block_diag_c(c_re, c_im, g, p, c):
    nb = g // SSM_GB
    keep = _same_group(SSM_GB * p, p, SSM_GB * c, c)

    def one(cc):
        t = cc.reshape(nb, SSM_GB, c, p).transpose(0, 1, 3, 2).reshape(nb, SSM_GB * p, c)
        return jnp.where(keep, jnp.tile(t, (1, 1, SSM_GB)), 0.0)

    return jnp.concatenate([one(c_re), one(-c_im)], axis=1)


def _diag_of_b(dbbd, g, p, c):
    nb = g // SSM_GB
    keep = _same_group(SSM_GB * c, c, SSM_GB * p, p)

    def one(blk):
        d = jnp.where(keep, blk, 0.0).reshape(nb, SSM_GB * c, SSM_GB, p).sum(axis=2)
        return d.reshape(nb, SSM_GB, c, p).transpose(0, 1, 3, 2).reshape(g * p, c)

    half = SSM_GB * p
    return one(dbbd[:, :, :half]), one(dbbd[:, :, half:])


def _diag_of_c(dccd, g, p, c):
    nb = g // SSM_GB
    keep = _same_group(SSM_GB * p, p, SSM_GB * c, c)

    def one(blk):
        d = jnp.where(keep, blk, 0.0).reshape(nb, SSM_GB * p, SSM_GB, c).sum(axis=2)
        return d.reshape(nb, SSM_GB, p, c).transpose(0, 1, 3, 2).reshape(g, c, p)

    half = SSM_GB * p
    return one(dccd[:, :half]), -one(dccd[:, half:])


def _glu_fwd(y, proj, w_glu, b_glu, name):
    l, w = y.shape
    tm = _tile(l, 256)

    def body(y_ref, z_ref, w_ref, b_ref, o_ref):
        g = _gelu(y_ref[...])
        t = _dot(_bf(g), w_ref[...]) + b_ref[...]
        o_ref[...] = _bf(g * _sigmoid(t) * _silu(z_ref[...]))

    return pl.pallas_call(
        body, name=name, grid=(l // tm,),
        in_specs=[_row_spec(tm, w), pl.BlockSpec((tm, w), lambda i: (i, 1)),
                  pl.BlockSpec((w, w), lambda i: (0, 0)), _vec_spec(w)],
        out_specs=_row_spec(tm, w), out_shape=jax.ShapeDtypeStruct((l, w), BF16),
        compiler_params=_params(("parallel",)),
    )(y, proj, w_glu, b_glu)


def _glu_bwd(do, y, proj, w_glu, b_glu, name):
    l, w = y.shape
    tm = _tile(l, 512)
    nsteps = l // tm

    def body(do_ref, y_ref, z_ref, w_ref, b_ref, dy_ref, dz_ref, dw_ref, db_ref, dw_acc):
        i = pl.program_id(0)
        first = i == 0
        yv, z, do = y_ref[...], z_ref[...], do_ref[...].astype(F32)
        g = _gelu(yv)
        g_bf = _bf(g)
        sg = _sigmoid(_dot(g_bf, w_ref[...]) + b_ref[...])
        dyy = do * _silu(z)
        dz_ref[...] = _bf(do * g * sg * _silu_grad(z))
        dt = dyy * g * sg * (1.0 - sg)
        dt_bf = _bf(dt)
        dg = dyy * sg + _dot(dt_bf, w_ref[...], NT)
        dy_ref[...] = dg * _gelu_grad(yv)
        _acc(dw_acc, first, _dot(_bf(g.T), dt_bf))
        _acc(db_ref, first, _colsum(dt))

        @pl.when(i == nsteps - 1)
        def _():
            dw_ref[...] = _bf(dw_acc[...])

    return pl.pallas_call(
        body, name=name, grid=(nsteps,),
        in_specs=[_row_spec(tm, w), _row_spec(tm, w), pl.BlockSpec((tm, w), lambda i: (i, 1)),
                  pl.BlockSpec((w, w), lambda i: (0, 0)), _vec_spec(w)],
        out_specs=[_row_spec(tm, w), _row_spec(tm, w), pl.BlockSpec((w, w), lambda i: (0, 0)), _vec_spec(w)],
        out_shape=[jax.ShapeDtypeStruct((l, w), F32), jax.ShapeDtypeStruct((l, w), BF16),
                   jax.ShapeDtypeStruct((w, w), BF16), jax.ShapeDtypeStruct((1, w), F32)],
        scratch_shapes=[pltpu.VMEM((w, w), F32)],
        compiler_params=_params(("arbitrary",)),
    )(do, y, proj, w_glu, b_glu)


MOD_ROWS = 128


def _mod_fwd(cond_pad, w_mod, b_shard, name):
    nl, d, ncol = w_mod.shape
    tn = _tile(ncol, 512)

    def body(c_ref, w_ref, b_ref, o_ref):
        o_ref[0] = _dot(_bf(c_ref[...]), _bf(w_ref[0])) + b_ref[0]

    return pl.pallas_call(
        body, name=name, grid=(nl, ncol // tn),
        in_specs=[pl.BlockSpec((MOD_ROWS, d), lambda a, j: (0, 0)),
                  pl.BlockSpec((1, d, tn), lambda a, j: (a, 0, j)),
                  pl.BlockSpec((1, 1, tn), lambda a, j: (a, 0, j))],
        out_specs=pl.BlockSpec((1, MOD_ROWS, tn), lambda a, j: (a, 0, j)),
        out_shape=jax.ShapeDtypeStruct((nl, MOD_ROWS, ncol), F32),
        compiler_params=_params(("parallel", "parallel")),
    )(cond_pad, w_mod, b_shard)


def _mod_bwd(cond_pad_t, dmod_pad, name):
    nl, _, ncol = dmod_pad.shape
    d = cond_pad_t.shape[0]
    tn = _tile(ncol, 512)

    def body(c_ref, dm_ref, o_ref):
        o_ref[0] = _dot(_bf(c_ref[...]), _bf(dm_ref[0]))

    return pl.pallas_call(
        body, name=name, grid=(nl, ncol // tn),
        in_specs=[pl.BlockSpec((d, MOD_ROWS), lambda a, j: (0, 0)),
                  pl.BlockSpec((1, MOD_ROWS, tn), lambda a, j: (a, 0, j))],
        out_specs=pl.BlockSpec((1, d, tn), lambda a, j: (a, 0, j)),
        out_shape=jax.ShapeDtypeStruct((nl, d, ncol), F32),
        compiler_params=_params(("parallel", "parallel")),
    )(cond_pad_t, dmod_pad)


def _silu_rows(c2d, name):
    def body(c_ref, o_ref):
        o_ref[...] = _silu(c_ref[...])

    return pl.pallas_call(body, name=name, out_shape=jax.ShapeDtypeStruct(c2d.shape, F32))(c2d)


def _sum_leading(x, name):
    n, r, c = x.shape
    tr = _tile(r, max(16, (1 << 20) // (4 * c)), 16 if r % 16 == 0 else 8)

    def body(x_ref, o_ref):
        acc = x_ref[0].astype(F32)
        for k in range(1, n):
            acc = acc + x_ref[k].astype(F32)
        o_ref[...] = acc

    return pl.pallas_call(
        body, name=name, grid=(r // tr,),
        in_specs=[pl.BlockSpec((n, tr, c), lambda i: (0, i, 0))], out_specs=pl.BlockSpec((tr, c), lambda i: (i, 0)),
        out_shape=jax.ShapeDtypeStruct((r, c), F32), compiler_params=_params(("parallel",)),
    )(x)


def _adamw(w, gs, m, v, name, ride=None):
    r, c = w.shape
    tr = _tile(r, max(8, (3 << 19) // (4 * c)), 8)
    ng = len(gs)

    def body(*refs):
        w_ref, g_refs, m_ref, v_ref = refs[0], refs[1:1 + ng], refs[1 + ng], refs[2 + ng]
        g_ref, d_ref, nm_ref, nv_ref = refs[3 + ng:]
        g = g_refs[0][...]
        for extra in g_refs[1:]:
            g = g + extra[...]
        g_ref[...] = g
        d_ref[...], nm_ref[...], nv_ref[...] = _adamw_math(w_ref[...], g, m_ref[...], v_ref[...])

    spec = pl.BlockSpec((tr, c), lambda i: (i, 0))
    shp = jax.ShapeDtypeStruct((r, c), F32)
    outs = _call(body, name=name, grid=(r // tr,), in_specs=[spec] * (3 + ng), out_specs=[spec] * 4,
                 out_shape=[shp] * 4, args=(w, *gs, m, v), sem=("parallel",), ride=ride)
    return outs if ride is None else (outs[:4], outs[4:])


def _adamw_math(w, g, m, v):
    nm = ADAM_B1 * m + (1.0 - ADAM_B1) * g
    nv = ADAM_B2 * v + (1.0 - ADAM_B2) * (g * g)
    m_hat = nm / (1.0 - ADAM_B1 ** ADAM_STEP)
    v_hat = nv / (1.0 - ADAM_B2 ** ADAM_STEP)
    return -ADAM_LR * (m_hat / (jnp.sqrt(v_hat) + ADAM_EPS) + ADAM_WD * w), nm, nv


def _adamw_many(ws, gs, ms, vs, name):
    n = len(ws)

    def body(*refs):
        w_refs, g_refs, m_refs, v_refs = (refs[k * n:(k + 1) * n] for k in range(4))
        outs = refs[4 * n:]
        for i in range(n):
            outs[3 * i][...], outs[3 * i + 1][...], outs[3 * i + 2][...] = _adamw_math(
                w_refs[i][...], g_refs[i][...], m_refs[i][...], v_refs[i][...])

    out_shape = [jax.ShapeDtypeStruct(w.shape, F32) for w in ws for _ in range(3)]
    outs = pl.pallas_call(body, name=name, out_shape=out_shape, compiler_params=_params())(*ws, *gs, *ms, *vs)
    return [tuple(outs[3 * i:3 * i + 3]) for i in range(n)]


ANY = pl.BlockSpec(memory_space=pl.ANY)


def _flip(v, bit):
    return 1 - v if bit else v


def _allgather8_ops(x_ref, o_ref, send_sems, recv_sems, local_sem):
    mx, my, mc = lax.axis_index("x"), lax.axis_index("y"), lax.axis_index("c")
    me = 4 * mx + 2 * my + mc

    def mine():
        return pltpu.make_async_copy(x_ref, o_ref.at[me], local_sem)

    def copy(j, outgoing):
        peer = (_flip(mx, j & 4), _flip(my, j & 2), _flip(mc, j & 1))
        slot = me if outgoing else 4 * peer[0] + 2 * peer[1] + peer[2]
        return pltpu.make_async_remote_copy(
            src_ref=x_ref, dst_ref=o_ref.at[slot], send_sem=send_sems.at[j - 1], recv_sem=recv_sems.at[j - 1],
            device_id=peer, device_id_type=MESH)

    def start():
        mine().start()
        for j in range(1, 8):
            copy(j, True).start()

    def wait():
        for j in range(1, 8):
            copy(j, False).wait()
        mine().wait()

    return start, wait


def _ride_all8(x):
    return dict(xs=[x], shapes=[jax.ShapeDtypeStruct((8,) + x.shape, x.dtype)],
                sems=[pltpu.SemaphoreType.DMA((7,)), pltpu.SemaphoreType.DMA((7,)), pltpu.SemaphoreType.DMA],
                ops=lambda x_refs, o_refs, sems: _allgather8_ops(x_refs[0], o_refs[0], *sems))


def _ride_chip(xs, gather):
    return dict(xs=list(xs), shapes=_chip_exchange_shapes(xs, gather), sems=_chip_exchange_sems(len(xs)),
                ops=lambda x_refs, o_refs, sems: _chip_exchange_ops(x_refs, o_refs, *sems, gather))


def _allgather8(x, name):
    return _exchange([_ride_all8(x)], name)[0]


def _gather_halves_ops(x_ref, o_ref, ici_send, ici_recv, d2d_send, d2d_recv, local_sem):
    half = x_ref.shape[0] // 2
    quarter = half // 2
    mx, my, mc = lax.axis_index("x"), lax.axis_index("y"), lax.axis_index("c")
    k0, kx, ky, kd = 2 * mx + my, 2 * (1 - mx) + my, 2 * mx + (1 - my), 2 * (1 - mx) + (1 - my)
    x_nbr, y_nbr, sib = (1 - mx, my, mc), (mx, 1 - my, mc), (mx, my, 1 - mc)

    def rows(core, part):
        if part is None:
            return pl.ds(pl.multiple_of(core * half, 16), half)
        return pl.ds(pl.multiple_of(core * half + part * quarter, 16), quarter)

    def local():
        return pltpu.make_async_copy(x_ref, o_ref.at[k0], local_sem)

    def ici(n, outgoing):
        to = x_nbr if n in (0, 2) else y_nbr
        if n < 2:
            src = x_ref.at[rows(mc, None)]
            dst = o_ref.at[k0 if outgoing else (kx if n == 0 else ky), rows(mc, None)]
        else:
            part = n - 2
            src = o_ref.at[ky if n == 2 else kx, rows(mc, part)]
            dst = o_ref.at[(ky if n == 2 else kx) if outgoing else kd, rows(mc, part)]
        return pltpu.make_async_remote_copy(src_ref=src, dst_ref=dst, send_sem=ici_send.at[n], recv_sem=ici_recv.at[n],
                                            device_id=to, device_id_type=MESH)

    def d2d(n, outgoing):
        slot, part = ((kx, None), (ky, None), (kd, 0), (kd, 1))[n]
        piece = o_ref.at[slot, rows(mc if outgoing else 1 - mc, part)]
        return pltpu.make_async_remote_copy(src_ref=piece, dst_ref=piece, send_sem=d2d_send.at[n],
                                            recv_sem=d2d_recv.at[n], device_id=sib, device_id_type=MESH)

    def start():
        local().start()
        ici(0, True).start()
        ici(1, True).start()

    def wait():
        ici(1, False).wait_recv()
        ici(2, True).start()
        d2d(1, True).start()
        ici(0, False).wait_recv()
        ici(3, True).start()
        d2d(0, True).start()
        ici(2, False).wait_recv()
        d2d(2, True).start()
        ici(3, False).wait_recv()
        d2d(3, True).start()
        for n in range(4):
            ici(n, True).wait_send()
            d2d(n, True).wait_send()
            d2d(n, False).wait_recv()
        local().wait()

    return start, wait


def _ride_halves(x):
    dma4 = pltpu.SemaphoreType.DMA((4,))
    return dict(xs=[x], shapes=[jax.ShapeDtypeStruct((4,) + x.shape, x.dtype)],
                sems=[dma4, dma4, dma4, dma4, pltpu.SemaphoreType.DMA],
                ops=lambda x_refs, o_refs, sems: _gather_halves_ops(x_refs[0], o_refs[0], *sems))


def _exchange(rides, name):
    xs = [x for r in rides for x in r["xs"]]
    nx = len(xs)

    def body(*refs):
        x_refs, o_refs, sems = refs[:nx], refs[nx:2 * nx], refs[2 * nx:]
        ops, xo, so = [], 0, 0
        for r in rides:
            nr, ns = len(r["xs"]), len(r["sems"])
            ops.append(r["ops"](x_refs[xo:xo + nr], o_refs[xo:xo + nr], sems[so:so + ns]))
            xo, so = xo + nr, so + ns
        for start, _ in ops:
            start()
        for _, wait in ops:
            wait()

    return pl.pallas_call(body, name=name, in_specs=[ANY] * nx, out_specs=[ANY] * nx,
                          out_shape=[s for r in rides for s in r["shapes"]],
                          scratch_shapes=[s for r in rides for s in r["sems"]])(*xs)


def _chip_exchange_shapes(xs, gather):
    return [jax.ShapeDtypeStruct(((4,) + x.shape) if gather else x.shape, x.dtype) for x in xs]


def _chip_exchange_sems(n):
    return [pltpu.SemaphoreType.DMA((3 * n,)), pltpu.SemaphoreType.DMA((3 * n,)), pltpu.SemaphoreType.DMA((n,))]


def _chip_exchange_ops(x_refs, o_refs, send_sems, recv_sems, local_sems, gather):
    n = len(x_refs)
    mx, my, mc = lax.axis_index("x"), lax.axis_index("y"), lax.axis_index("c")
    k0 = 2 * mx + my

    def local(a):
        src = x_refs[a] if gather else x_refs[a].at[k0]
        return pltpu.make_async_copy(src, o_refs[a].at[k0], local_sems.at[a])

    def copy(a, j, outgoing):
        px, py = _flip(mx, j & 2), _flip(my, j & 1)
        kp = 2 * px + py
        if outgoing:
            src = x_refs[a] if gather else x_refs[a].at[kp]
            dst = o_refs[a].at[k0]
        else:
            src = x_refs[a] if gather else x_refs[a].at[k0]
            dst = o_refs[a].at[kp]
        s = a * 3 + j - 1
        return pltpu.make_async_remote_copy(
            src_ref=src, dst_ref=dst, send_sem=send_sems.at[s], recv_sem=recv_sems.at[s],
            device_id=(px, py, mc), device_id_type=MESH)

    def start():
        for a in range(n):
            local(a).start()
            for j in range(1, 4):
                copy(a, j, True).start()

    def wait():
        for a in range(n):
            for j in range(1, 4):
                copy(a, j, False).wait()
            local(a).wait()

    return start, wait


def _call(body, *, name, grid, in_specs, out_specs, out_shape, args, scratch_shapes=(), sem=None, ride=None):
    if not ride:
        return pl.pallas_call(
            body, name=name, grid=grid, in_specs=list(in_specs), out_specs=list(out_specs), out_shape=list(out_shape),
            scratch_shapes=list(scratch_shapes), compiler_params=_params(sem))(*args)
    xs = [x for r in ride for x in r["xs"]]
    shapes = [s for r in ride for s in r["shapes"]]
    sems = [s for r in ride for s in r["sems"]]
    n_in, n_out, n_scr, nx = len(in_specs), len(out_specs), len(scratch_shapes), len(xs)

    def wrapped(*refs):
        ins, x_refs = refs[:n_in], refs[n_in:n_in + nx]
        outs = refs[n_in + nx:n_in + nx + n_out]
        lands = refs[n_in + nx + n_out:n_in + 2 * nx + n_out]
        rest = refs[n_in + 2 * nx + n_out:]
        scr, sem_refs = rest[:n_scr], rest[n_scr:]
        ops, xo, so = [], 0, 0
        for r in ride:
            nr, ns = len(r["xs"]), len(r["sems"])
            ops.append(r["ops"](x_refs[xo:xo + nr], lands[xo:xo + nr], sem_refs[so:so + ns]))
            xo, so = xo + nr, so + ns
        ids = [pl.program_id(a) for a in range(len(grid))]
        first = functools.reduce(jnp.logical_and, [i == 0 for i in ids])
        last = functools.reduce(jnp.logical_and, [i == g - 1 for i, g in zip(ids, grid)])

        @pl.when(first)
        def _():
            for start, _ in ops:
                start()

        body(*ins, *outs, *scr)

        @pl.when(last)
        def _():
            for _, wait in ops:
                wait()

    return pl.pallas_call(
        wrapped, name=name, grid=grid, in_specs=list(in_specs) + [ANY] * nx, out_specs=list(out_specs) + [ANY] * nx,
        out_shape=list(out_shape) + shapes, scratch_shapes=list(scratch_shapes) + sems,
        compiler_params=_params(("arbitrary",) * len(grid)))(*args, *xs)


def _ride_sibling(xs):
    n = len(xs)

    def ops(x_refs, o_refs, sems):
        send_sems, recv_sems = sems
        sib = (lax.axis_index("x"), lax.axis_index("y"), 1 - lax.axis_index("c"))

        def copies():
            return [pltpu.make_async_remote_copy(
                src_ref=x_refs[a], dst_ref=o_refs[a], send_sem=send_sems.at[a], recv_sem=recv_sems.at[a],
                device_id=sib, device_id_type=MESH) for a in range(n)]

        def start():
            for cp in copies():
                cp.start()

        def wait():
            for cp in copies():
                cp.wait()

        return start, wait

    return dict(xs=list(xs), shapes=[jax.ShapeDtypeStruct(x.shape, x.dtype) for x in xs],
                sems=[pltpu.SemaphoreType.DMA((n,)), pltpu.SemaphoreType.DMA((n,))], ops=ops)


PACK = 1024
PACK_ROWS = 512


def _pack(parts, pad_rows=PACK_ROWS):
    flat = []
    for p in parts:
        v = p.reshape(-1).astype(F32)
        flat.append(jnp.pad(v, (0, (-v.shape[0]) % PACK)))
    total = sum(v.shape[0] for v in flat)
    flat.append(jnp.zeros(((-total) % (pad_rows * 128),), F32))
    return jnp.concatenate(flat).reshape(-1, 128)


def _shard_columns(shards, lo, hi):
    width = shards.shape[2]
    out = []
    for k in range(shards.shape[0]):
        a, b = max(lo, k * width), min(hi, (k + 1) * width)
        if a < b:
            out.append(shards[k, :, a - k * width:b - k * width])
    return out


def _unpack_rows(gathered, shapes):
    flat = gathered.reshape(gathered.shape[0], -1)
    out, off = [], 0
    for shp in shapes:
        n = math.prod(shp)
        out.append(flat[:, off:off + n].reshape((flat.shape[0],) + tuple(shp)))
        off += n + (-n) % PACK
    return out


def _unpack(packed, shapes):
    flat = packed.reshape(-1)
    out, off = [], 0
    for shp in shapes:
        n = math.prod(shp)
        out.append(flat[off:off + n].reshape(shp))
        off += n + (-n) % PACK
    return out


def kernel(x, c, ln_pre_g, ln_post_g, w_mod, b_mod, w_in_ab, w_out_ab, sgu_norm_g, sgu_w, sgu_b, w_in_ssm, w_out_ssm, lam_re, lam_im, b_re, b_im, c_re, c_im, d_skip, log_dt, w_glu, b_glu, loss_target, m_ln_pre_g, m_ln_post_g, m_w_mod, m_b_mod, m_w_in_ab, m_w_out_ab, m_sgu_norm_g, m_sgu_w, m_sgu_b, m_w_in_ssm, m_w_out_ssm, m_lam_re, m_lam_im, m_b_re, m_b_im, m_c_re, m_c_im, m_d_skip, m_log_dt, m_w_glu, m_b_glu, v_ln_pre_g, v_ln_post_g, v_w_mod, v_b_mod, v_w_in_ab, v_w_out_ab, v_sgu_norm_g, v_sgu_w, v_sgu_b, v_w_in_ssm, v_w_out_ssm, v_lam_re, v_lam_im, v_b_re, v_b_im, v_c_re, v_c_im, v_d_skip, v_log_dt, v_w_glu, v_b_glu):
    given = dict(locals())
    mx, my, mc = lax.axis_index("x"), lax.axis_index("y"), lax.axis_index("c")
    me = 4 * mx + 2 * my + mc
    chip = 2 * mx + my

    _, l, d = x.shape
    x2, tgt = x[0], loss_target[0]
    n_in = w_in_ab.shape[2] * 4
    wa = wb = n_in // 7
    w = w_out_ssm.shape[1]
    g, p, cch = b_re.shape[1:]
    nmod = w_mod.shape[2]


    cond = _silu_rows(c.reshape(d // 128, 128), "cond_silu")
    cond_all = _allgather8(cond, "gather_cond").reshape(8, d)
    b_shard = lax.dynamic_slice(b_mod, (0, chip * nmod), (2, nmod)).reshape(2, 1, nmod)
    cond_pad = jnp.pad(cond_all, ((0, MOD_ROWS - 8), (0, 0)))
    modp = _mod_fwd(cond_pad, w_mod, b_shard, "mod_fwd")[:, :8]
    modp_all = _allgather8(modp.reshape(16, nmod), "gather_mod").reshape(4, 2, 2, 8, nmod)
    mine = lax.dynamic_index_in_dim(lax.dynamic_index_in_dim(modp_all, mc, 1, False), me, 2, False)
    mod = mine.transpose(1, 0, 2).reshape(2, 3 * d)
    shift = [mod[a:a + 1, :d] for a in range(2)]
    scale = [mod[a:a + 1, d:2 * d] for a in range(2)]
    gate = [mod[a:a + 1, 2 * d:] for a in range(2)]
    pre_g = [ln_pre_g[a:a + 1] for a in range(2)]
    post_g = [ln_post_g[a:a + 1] for a in range(2)]

    sgu_w0, sgu_bt = sgu_w[0], sgu_b[0].T
    h0, (gw_in_ab,) = _pre_fwd(x2, pre_g[0], scale[0], shift[0], "pre0_fwd", ride=[_ride_halves(_bf(w_in_ab[0]))])
    w_gates = jnp.concatenate(_shard_columns(gw_in_ab, 0, 3 * wa) + _shard_columns(gw_in_ab, 3 * wa + 3 * wb, n_in),
                              axis=1)
    w_qkv = jnp.concatenate(_shard_columns(gw_in_ab, 3 * wa, 3 * wa + 3 * wb), axis=1)
    proj0, (gw_in_ssm,) = _matmul(h0, w_gates, "nn", BF16, "proj0", tm=1024, ride=[_ride_chip([_bf(w_in_ssm[0])], True)])
    qkv, (gw_out_ssm, gw_glu, g_dskip, g_bglu) = _matmul(
        h0, w_qkv, "nn", BF16, "proj0_qkv", tm=1024,
        ride=[_ride_chip([_bf(w_out_ssm[0]), _bf(w_glu[0]), d_skip, b_glu], True)])
    out_b, (gw_out_ab,) = _attn_fwd(qkv, wb, "attn_fwd", hp=8, ride=[_ride_chip([_bf(w_out_ab[0])], True)])
    wout_ab = gw_out_ab.reshape(wa + wb, d)
    win_ssm = gw_in_ssm.reshape(d, 2 * w)
    wout_ssm = jnp.concatenate([gw_out_ssm[k] for k in range(4)], axis=1)
    wglu = gw_glu.reshape(w, w)
    dskip_full = g_dskip.reshape(1, w)
    bglu_full = g_bglu.reshape(1, w)
    cat =_sgu_fwd(proj0, out_b, sgu_norm_g, sgu_w0, sgu_bt, wa, wb, "sgu_fwd")
    y0 = _matmul(cat, wout_ab, "nn", BF16, "out0", tm=1024)
    x1, h1 = _post_pre_fwd(x2, y0, gate[0], post_g[0], pre_g[1], scale[1], shift[1], "post0_pre1_fwd")

    s = g * p
    lr_c, li_c = lam_re.reshape(s, 1), lam_im.reshape(s, 1)
    ldt_c = jnp.repeat(log_dt.reshape(g), p).reshape(s, 1)
    br_c, bi_c = b_re.reshape(s, cch), b_im.reshape(s, cch)
    bb_re, bb_im, pw_re, pw_im = _ssm_prep(lr_c, li_c, ldt_c, br_c, bi_c, lr_c.reshape(1, s), li_c.reshape(1, s),
                                           ldt_c.reshape(1, s), "ssm_prep")
    bbd = _bf(_block_diag_b(bb_re, bb_im, g, p, cch))
    ccd = _bf(_block_diag_c(c_re[0], c_im[0], g, p, cch))
    proj1 = _matmul(h1, win_ssm, "nn", F32, "proj1", tm=1024)
    y_ssm, hs_re, hs_im, h_all = _ssm_fwd(proj1, bbd, ccd, pw_re, pw_im, dskip_full, w, "ssm_fwd")
    o1 = _glu_fwd(y_ssm, proj1, wglu, bglu_full, "glu_fwd")
    y1 = _matmul(o1, wout_ssm, "nn", BF16, "out1", tm=1024)
    loss_vec, dy1, dx2, dgate1, dpost1 = _post_loss(x1, y1, gate[1], post_g[1], tgt, "post1_loss")

    do1 = _matmul(dy1, wout_ssm, "nt", BF16, "out1_dx", tm=1024)
    gr_wout_ssm = _matmul_at(o1, dy1, BF16, "out1_dw", n_split=4)
    dy_ssm, dz1, gr_wglu, gr_bglu = _glu_bwd(do1, y_ssm, proj1, wglu, bglu_full, "glu_bwd")
    du1, dbbd, dccd, da_re, da_im, gr_dskip, (ld_wout_ssm, ld_wglu) = _ssm_bwd(
        proj1, dy_ssm, hs_re, hs_im, h_all, bbd, ccd, pw_re, pw_im, dskip_full, w, "ssm_bwd",
        ride=[_ride_chip([gr_wout_ssm, gr_wglu.reshape(4, w // 4, w)], False)])
    dproj1 = jnp.concatenate([du1, dz1], axis=1)
    dh1 = _matmul(dproj1, win_ssm, "nt", BF16, "proj1_dx", tm=1024)
    gr_win_ssm = _matmul_at(h1, dproj1, BF16, "proj1_dw", tn=1024)
    dx1, dscale1, dshift1, dpre1, dy0, dgate0, dpost0 = _pre_bwd(
        dh1, dx2, x1, pre_g[1], scale[1], "pre1_post0_bwd", post=(y0, gate[0], post_g[0]))

    dcat = _matmul(dy0, wout_ab, "nt", BF16, "out0_dx", tm=1024)
    gr_wout_ab = _matmul_at(cat, dy0, BF16, "out0_dw", tn=1024)
    dbb_re, dbb_im = _diag_of_b(dbbd, g, p, cch)
    dc_re, dc_im = _diag_of_c(dccd, g, p, cch)
    part_a = [loss_vec[:, :1], dpre1, dpost0, dpost1, dgate0, dshift1, dscale1, dgate1, da_re, da_im,
              dbb_re, dbb_im, dc_re, dc_im, gr_dskip, gr_bglu]
    shapes_a = [a.shape for a in part_a]
    dq, dk, dv, (ld_win_ssm, ld_wout_ab, gath_a) = _attn_bwd(
        qkv, proj0, dcat, wa, wb, "attn_bwd", hp=4,
        ride=[_ride_chip([gr_win_ssm.reshape(4, d // 4, 2 * w), gr_wout_ab.reshape(4, (wa + wb) // 4, d)], False),
              _ride_all8(_pack(part_a))])
    early_names = ["w_out_ab", "w_in_ssm", "w_out_ssm", "w_glu"]
    early_sums = [_sum_leading(a, "sum_" + nm) for a, nm in zip([ld_wout_ab, ld_win_ssm, ld_wout_ssm, ld_wglu], early_names)]
    dproj0, gr_sgu_w, gr_sgu_bt, gr_sgu_g, early_sib = _sgu_bwd(
        proj0, out_b, dcat, dq, dk, dv, sgu_norm_g, sgu_w0, sgu_bt, wa, wb, "sgu_bwd", ride=[_ride_sibling(early_sums)])
    part_b = [gr_sgu_g, gr_sgu_w, gr_sgu_bt.T]
    shapes_b = [a.shape for a in part_b]
    gr_win_ab_lo, (gath_b,) = _matmul_at(h0, dproj0, BF16, "proj0_dw_lo", tm=512, tn=n_in // 4, n_split=4,
                                         m_part=(0, 1, 2), ride=[_ride_all8(_pack(part_b))])
    gr_win_ab_hi, (ld_win_ab_lo,) = _matmul_at(
        h0, dproj0, BF16, "proj0_dw_hi", tm=512, tn=n_in // 4, n_split=4, m_part=(1, 1, 2),
        ride=[_ride_chip([gr_win_ab_lo], False)])
    dh0, (ld_win_ab_hi,) = _matmul(dproj0, gw_in_ab, "nt", BF16, "proj0_dx", tm=1024, tn=1024,
                                   ride=[_ride_chip([gr_win_ab_hi], False)])
    grad_x, dscale0, dshift0, dpre0 = _pre_bwd(dh0, dx1, x2, pre_g[0], scale[0], "pre0_bwd")
    part_c = [dpre0, dshift0, dscale0]
    shapes_c = [a.shape for a in part_c]

    big_names = ["w_in_ab"] + early_names
    sum_win_ab = jnp.concatenate([_sum_leading(ld_win_ab_lo, "sum_w_in_ab_lo"),
                                  _sum_leading(ld_win_ab_hi, "sum_w_in_ab_hi")], axis=0)
    sums = [sum_win_ab] + early_sums
    gath_c, sib_win_ab = _exchange([_ride_all8(_pack(part_c, pad_rows=8)), _ride_sibling([sum_win_ab])], "tail_exchange")
    sib = [sib_win_ab] + list(early_sib)
    results = {}
    for nm, s_mine, s_sib in zip(big_names, sums, sib):
        shp = given[nm].shape
        two_d = lambda a: a.reshape(-1, shp[-1])
        outs = _adamw(two_d(given[nm]), [s_mine, s_sib], two_d(given["m_" + nm]), two_d(given["v_" + nm]),
                      "adamw_" + nm)
        results[nm] = [o.reshape(shp) for o in outs]

    (loss_s, g_pre1, g_post0, g_post1, g_gate0, g_shift1, g_scale1, g_gate1, s_da_re, s_da_im, s_dbb_re, s_dbb_im,
     g_c_re, g_c_im, g_dskip_full, g_bglu_full) = _unpack(_sum_leading(gath_a, "sum_small_a"), shapes_a)
    g_sgu_g, g_sgu_w, g_sgu_b = _unpack(_sum_leading(gath_b, "sum_small_b"), shapes_b)
    g_pre0, g_shift0, g_scale0 = _unpack(_sum_leading(gath_c, "sum_small_c"), shapes_c)
    loss = loss_s.reshape(())
    g_pre = jnp.concatenate([g_pre0, g_pre1], axis=0)
    g_post = jnp.concatenate([g_post0, g_post1], axis=0)
    g_bmod = jnp.concatenate([jnp.concatenate([g_shift0, g_scale0, g_gate0], axis=1),
                              jnp.concatenate([g_shift1, g_scale1, g_gate1], axis=1)], axis=0)

    g_lr, g_li, g_ldt, g_br, g_bi = _ssm_prep_bwd(lr_c, li_c, ldt_c, br_c, bi_c, s_da_re.reshape(s, 1),
                                                  s_da_im.reshape(s, 1), s_dbb_re, s_dbb_im, p, "ssm_prep_bwd")
    small = {
        "ln_pre_g": g_pre, "ln_post_g": g_post, "b_mod": g_bmod, "sgu_norm_g": g_sgu_g,
        "sgu_w": g_sgu_w.reshape(sgu_w.shape), "sgu_b": g_sgu_b.reshape(sgu_b.shape),
        "lam_re": g_lr.reshape(lam_re.shape), "lam_im": g_li.reshape(lam_im.shape),
        "b_re": g_br.reshape(b_re.shape), "b_im": g_bi.reshape(b_im.shape),
        "c_re": g_c_re.reshape(c_re.shape), "c_im": g_c_im.reshape(c_im.shape),
        "d_skip": lax.dynamic_slice(g_dskip_full, (0, chip * (w // 4)), (1, w // 4)),
        "log_dt": g_ldt.reshape(log_dt.shape),
        "b_glu": lax.dynamic_slice(g_bglu_full, (0, chip * (w // 4)), (1, w // 4)),
    }
    flat2 = lambda a: a.reshape(-1, a.shape[-1])
    wide = ("b_re", "b_im")
    for tag, group in (("adamw_small", [nm for nm in small if nm not in wide]), ("adamw_small_b", list(wide))):
        outs = _adamw_many([flat2(given[nm]) for nm in group], [flat2(small[nm]) for nm in group],
                           [flat2(given["m_" + nm]) for nm in group], [flat2(given["v_" + nm]) for nm in group], tag)
        for nm, trio in zip(group, outs):
            results[nm] = [small[nm]] + [o.reshape(given[nm].shape) for o in trio]

    rows_a = _unpack_rows(gath_a, shapes_a)
    rows_c = _unpack_rows(gath_c, shapes_c)
    dmod_rows = jnp.concatenate([rows_c[1], rows_c[2], rows_a[4], rows_a[5], rows_a[6], rows_a[7]],
                                axis=2).reshape(8, 2, 3 * d)
    dmod_shard = lax.dynamic_slice(dmod_rows, (0, 0, chip * nmod), (8, 2, nmod)).transpose(1, 0, 2)
    dmod_pad = jnp.pad(dmod_shard, ((0, 0), (0, MOD_ROWS - 8), (0, 0)))
    gr_wmod = _mod_bwd(cond_pad.T, dmod_pad, "mod_bwd")
    two_d = lambda a: a.reshape(-1, nmod)
    outs = _adamw(two_d(w_mod), [two_d(gr_wmod)], two_d(m_w_mod), two_d(v_w_mod), "adamw_w_mod")
    results["w_mod"] = [o.reshape(w_mod.shape) for o in outs]

    names = ["ln_pre_g", "ln_post_g", "w_mod", "b_mod", "w_in_ab", "w_out_ab", "sgu_norm_g", "sgu_w", "sgu_b",
             "w_in_ssm", "w_out_ssm", "lam_re", "lam_im", "b_re", "b_im", "c_re", "c_im", "d_skip", "log_dt",
             "w_glu", "b_glu"]
    return (loss, grad_x[None], *[results[nm][0] for nm in names], *[results[nm][1] for nm in names],
            *[results[nm][2] for nm in names], *[results[nm][3] for nm in names])
```

```python
import functools
import math

import jax
import jax.numpy as jnp
from jax import lax
from jax.experimental import pallas as pl
from jax.experimental.pallas import tpu as pltpu

F32 = jnp.float32
BF16 = jnp.bfloat16
MESH = pl.DeviceIdType.MESH

EPS = 1e-6
HEAD = 128
SSM_T = 512
SSM_GB = 16
ADAM_LR, ADAM_B1, ADAM_B2, ADAM_EPS, ADAM_WD, ADAM_STEP = 0.001, 0.9, 0.999, 1e-08, 0.01, 10
VMEM_LIMIT = 56 * 1024 * 1024

NN = (((1,), (0,)), ((), ()))
NT = (((1,), (1,)), ((), ()))
TN = (((0,), (0,)), ((), ()))


def _params(sem=None):
    return pltpu.CompilerParams(dimension_semantics=sem, vmem_limit_bytes=VMEM_LIMIT)


def _dot(a, b, dims=NN):
    return lax.dot_general(a, b, dims, preferred_element_type=F32)


def _bf(x):
    return x.astype(BF16)


def _gelu(x):
    k = math.sqrt(2.0 / math.pi)
    t = jnp.tanh(k * (x + 0.044715 * x * x * x))
    return 0.5 * x * (1.0 + t)


def _gelu_grad(x):
    k = math.sqrt(2.0 / math.pi)
    x2 = x * x
    t = jnp.tanh(k * (x + 0.044715 * x * x2))
    return 0.5 * (1.0 + t) + 0.5 * x * (1.0 - t * t) * k * (1.0 + 3.0 * 0.044715 * x2)


def _sigmoid(x):
    return 1.0 / (1.0 + jnp.exp(-x))


def _silu(x):
    return x * _sigmoid(x)


def _silu_grad(x):
    s = _sigmoid(x)
    return s * (1.0 + x * (1.0 - s))


def _tile(n, t, mult=128):
    if n <= t:
        return n
    for cand in range(t - t % mult, 0, -mult):
        if n % cand == 0:
            return cand
    raise ValueError((n, t, mult))


def _matmul(a, b, mode, out_dtype, name, tm=512, tn=512, tk=2048, n_split=1, ride=None, m_part=None):
    b_sharded = b.ndim == 3
    if mode == "nn":
        (m, kk), (_, n) = a.shape, b.shape
    elif b_sharded:
        assert mode == "nt"
        (m, kk), n, tk = a.shape, b.shape[1], b.shape[2]
    elif mode == "nt":
        (m, kk), (n, _) = a.shape, b.shape
    else:
        (kk, m), (_, n) = a.shape, b.shape
    m_off = 0
    if m_part is not None:
        assert mode == "tn"
        first, count, parts = m_part
        tm = _tile(m // parts, tm)
        m_off = first * (m // parts) // tm
        m = count * (m // parts)
    tm, tk = _tile(m, tm), _tile(kk, tk)
    ns = n // n_split
    tn = _tile(ns, tn)
    nk = kk // tk
    dims = {"nn": NN, "nt": NT, "tn": TN}[mode]

    def body(a_ref, b_ref, o_ref, acc_ref):
        k = pl.program_id(2)
        part = _dot(_bf(a_ref[...]), _bf(b_ref[0] if b_sharded else b_ref[...]), dims)

        @pl.when(k == 0)
        def _():
            acc_ref[...] = part

        @pl.when(k > 0)
        def _():
            acc_ref[...] += part

        @pl.when(k == nk - 1)
        def _():
            o_ref[...] = acc_ref[...].astype(out_dtype).reshape(o_ref.shape)

    if mode == "nn":
        a_spec = pl.BlockSpec((tm, tk), lambda i, j, k: (i, k))
        b_spec = pl.BlockSpec((tk, tn), lambda i, j, k: (k, j))
    elif mode == "nt":
        a_spec = pl.BlockSpec((tm, tk), lambda i, j, k: (i, k))
        b_spec = (pl.BlockSpec((1, tn, tk), lambda i, j, k: (k, j, 0)) if b_sharded
                  else pl.BlockSpec((tn, tk), lambda i, j, k: (j, k)))
    else:
        a_spec = pl.BlockSpec((tk, tm), lambda i, j, k: (k, i + m_off))
        b_spec = pl.BlockSpec((tk, tn), lambda i, j, k: (k, j))
    if n_split == 1:
        out_shape = jax.ShapeDtypeStruct((m, n), out_dtype)
        o_spec = pl.BlockSpec((tm, tn), lambda i, j, k: (i, j))
    else:
        per = ns // tn
        out_shape = jax.ShapeDtypeStruct((n_split, m, ns), out_dtype)
        o_spec = pl.BlockSpec((1, tm, tn), lambda i, j, k: (j // per, i, j % per))
    outs = _call(body, name=name, grid=(m // tm, n // tn, nk), in_specs=[a_spec, b_spec], out_specs=[o_spec],
                 out_shape=[out_shape], scratch_shapes=[pltpu.VMEM((tm, tn), F32)], args=(a, b),
                 sem=("parallel", "parallel", "arbitrary"), ride=ride)
    return outs[0] if ride is None else (outs[0], outs[1:])


def _matmul_at(a, b, out_dtype, name, tm=1024, tn=512, n_split=1, ride=None, m_part=None):
    (kk, m), (_, n) = a.shape, b.shape
    m_off = 0
    if m_part is not None:
        first, count, parts = m_part
        tm = _tile(m // parts, tm)
        m_off = first * (m // parts) // tm
        m = count * (m // parts)
    tm = _tile(m, tm)
    ns = n // n_split
    tn = _tile(ns, tn)
    kc = _tile(kk, 512)

    def body(a_ref, b_ref, o_ref, at_ref):
        @pl.when(pl.program_id(1) == 0)
        def _():
            for c in range(kk // kc):
                at_ref[:, c * kc:(c + 1) * kc] = _bf(a_ref[c * kc:(c + 1) * kc, :].astype(F32).T)

        o_ref[...] = _dot(at_ref[...], _bf(b_ref[...])).astype(out_dtype).reshape(o_ref.shape)

    if n_split == 1:
        out_shape = jax.ShapeDtypeStruct((m, n), out_dtype)
        o_spec = pl.BlockSpec((tm, tn), lambda i, j: (i, j))
    else:
        per = ns // tn
        out_shape = jax.ShapeDtypeStruct((n_split, m, ns), out_dtype)
        o_spec = pl.BlockSpec((1, tm, tn), lambda i, j: (j // per, i, j % per))
    outs = _call(body, name=name, grid=(m // tm, n // tn),
                 in_specs=[pl.BlockSpec((kk, tm), lambda i, j: (0, i + m_off)), pl.BlockSpec((kk, tn), lambda i, j: (0, j))],
                 out_specs=[o_spec], out_shape=[out_shape], scratch_shapes=[pltpu.VMEM((tm, kk), BF16)], args=(a, b),
                 sem=("arbitrary", "arbitrary"), ride=ride)
    return outs[0] if ride is None else (outs[0], outs[1:])


def _row_spec(tm, d):
    return pl.BlockSpec((tm, d), lambda i: (i, 0))


def _vec_spec(d):
    return pl.BlockSpec((1, d), lambda i: (0, 0))


def _acc(ref, first, val):
    @pl.when(first)
    def _():
        ref[...] = val

    @pl.when(jnp.logical_not(first))
    def _():
        ref[...] += val


def _colsum(x):
    return jnp.sum(x, axis=0, keepdims=True)


def _rownorm(x):
    r = lax.rsqrt(jnp.mean(x * x, axis=-1, keepdims=True) + EPS)
    return x * r, r


STRIP = 64


def _fold8(x):
    return functools.reduce(lambda a, b: a + b, [x[8 * k:8 * k + 8] for k in range(x.shape[0] // 8)])


def _pre_fwd(x, g, scale, shift, name, ride=None):
    l, d = x.shape
    tm = _tile(l, 256)

    def body(x_ref, g_ref, sc_ref, sh_ref, h_ref):
        n, _ = _rownorm(x_ref[...])
        h_ref[...] = _bf(n * g_ref[...] * (1.0 + sc_ref[...]) + sh_ref[...])

    outs = _call(body, name=name, grid=(l // tm,), in_specs=[_row_spec(tm, d), _vec_spec(d), _vec_spec(d), _vec_spec(d)],
                 out_specs=[_row_spec(tm, d)], out_shape=[jax.ShapeDtypeStruct((l, d), BF16)],
                 args=(x, g, scale, shift), sem=("parallel",), ride=ride)
    return outs[0], outs[1:]


def _post_pre_fwd(x, y, gate, pg, g1, scale1, shift1, name):
    l, d = x.shape
    tm = _tile(l, 256)

    def body(x_ref, y_ref, gate_ref, pg_ref, g1_ref, sc_ref, sh_ref, x1_ref, h1_ref):
        @pl.loop(0, tm // STRIP)
        def _(s):
            rows = pl.ds(pl.multiple_of(s * STRIP, STRIP), STRIP)
            ny, _ = _rownorm(y_ref[rows, :].astype(F32))
            x1 = x_ref[rows, :] + gate_ref[...] * (ny * pg_ref[...])
            x1_ref[rows, :] = x1
            n1, _ = _rownorm(x1)
            h1_ref[rows, :] = _bf(n1 * g1_ref[...] * (1.0 + sc_ref[...]) + sh_ref[...])

    v = _vec_spec(d)
    return pl.pallas_call(
        body, name=name, grid=(l // tm,),
        in_specs=[_row_spec(tm, d), _row_spec(tm, d), v, v, v, v, v],
        out_specs=[_row_spec(tm, d), _row_spec(tm, d)],
        out_shape=[jax.ShapeDtypeStruct((l, d), F32), jax.ShapeDtypeStruct((l, d), BF16)],
        compiler_params=_params(("parallel",)),
    )(x, y, gate, pg, g1, scale1, shift1)


def _post_loss(x1, y1, gate, pg, target, name):
    l, d = x1.shape
    tm = _tile(l, 256)

    def body(x_ref, y_ref, gate_ref, pg_ref, t_ref, loss_ref, dy_ref, dx_ref, dgate_ref, dpg_ref):
        first = pl.program_id(0) == 0

        def strip(s, sums):
            rows = pl.ds(pl.multiple_of(s * STRIP, STRIP), STRIP)
            ny, ry = _rownorm(y_ref[rows, :].astype(F32))
            q = ny * pg_ref[...]
            e = x_ref[rows, :] + gate_ref[...] * q - t_ref[rows, :]
            dx2 = e * (1.0 / d)
            dx_ref[rows, :] = dx2
            dq = dx2 * gate_ref[...]
            dny = dq * pg_ref[...]
            dy_ref[rows, :] = _bf(ry * (dny - ny * jnp.mean(dny * ny, axis=-1, keepdims=True)))
            return sums[0] + _fold8(e * e), sums[1] + _fold8(dx2 * q), sums[2] + _fold8(dq * ny)

        zero = jnp.zeros((8, d), F32)
        sq, dgate, dpg = lax.fori_loop(0, tm // STRIP, strip, (zero, zero, zero))
        _acc(loss_ref, first, jnp.full((1, 128), 0.5 / d, F32) * jnp.sum(sq))
        _acc(dgate_ref, first, _colsum(dgate))
        _acc(dpg_ref, first, _colsum(dpg))

    v = _vec_spec(d)
    return pl.pallas_call(
        body, name=name, grid=(l // tm,),
        in_specs=[_row_spec(tm, d), _row_spec(tm, d), v, v, _row_spec(tm, d)],
        out_specs=[_vec_spec(128), _row_spec(tm, d), _row_spec(tm, d), v, v],
        out_shape=[jax.ShapeDtypeStruct((1, 128), F32), jax.ShapeDtypeStruct((l, d), BF16),
                   jax.ShapeDtypeStruct((l, d), F32), jax.ShapeDtypeStruct((1, d), F32),
                   jax.ShapeDtypeStruct((1, d), F32)],
        compiler_params=_params(("arbitrary",)),
    )(x1, y1, gate, pg, target)


def _pre_bwd(dh, dres, x, g, scale, name, post=None):
    l, d = x.shape
    tm = _tile(l, 256)
    with_post = post is not None

    def body(*refs):
        if with_post:
            (dh_ref, dres_ref, x_ref, g_ref, sc_ref, y_ref, gate_ref, pg_ref,
             dx_ref, dsc_ref, dsh_ref, dg_ref, dy_ref, dgate_ref, dpg_ref) = refs
        else:
            dh_ref, dres_ref, x_ref, g_ref, sc_ref, dx_ref, dsc_ref, dsh_ref, dg_ref = refs
        first = pl.program_id(0) == 0

        def strip(s, sums):
            rows = pl.ds(pl.multiple_of(s * STRIP, STRIP), STRIP)
            dh = dh_ref[rows, :].astype(F32)
            n, r = _rownorm(x_ref[rows, :])
            dyn = dh * (1.0 + sc_ref[...])
            dn = dyn * g_ref[...]
            dx = dres_ref[rows, :] + r * (dn - n * jnp.mean(dn * n, axis=-1, keepdims=True))
            dx_ref[rows, :] = dx
            new = [sums[0] + _fold8(dh * (n * g_ref[...])), sums[1] + _fold8(dh), sums[2] + _fold8(dyn * n)]
            if with_post:
                ny, ry = _rownorm(y_ref[rows, :].astype(F32))
                dq = dx * gate_ref[...]
                dny = dq * pg_ref[...]
                dy_ref[rows, :] = _bf(ry * (dny - ny * jnp.mean(dny * ny, axis=-1, keepdims=True)))
                new += [sums[3] + _fold8(dx * (ny * pg_ref[...])), sums[4] + _fold8(dq * ny)]
            return tuple(new)

        zero = jnp.zeros((8, d), F32)
        sums = lax.fori_loop(0, tm // STRIP, strip, (zero,) * (5 if with_post else 3))
        outs = [dsc_ref, dsh_ref, dg_ref] + ([dgate_ref, dpg_ref] if with_post else [])
        for ref, acc in zip(outs, sums):
            _acc(ref, first, _colsum(acc))

    v = _vec_spec(d)
    row = _row_spec(tm, d)
    vec_out = jax.ShapeDtypeStruct((1, d), F32)
    in_specs = [row, row, row, v, v]
    args = [dh, dres, x, g, scale]
    out_specs = [row, v, v, v]
    out_shape = [jax.ShapeDtypeStruct((l, d), F32), vec_out, vec_out, vec_out]
    if with_post:
        in_specs += [row, v, v]
        args += list(post)
        out_specs += [row, v, v]
        out_shape += [jax.ShapeDtypeStruct((l, d), BF16), vec_out, vec_out]
    return pl.pallas_call(
        body, name=name, grid=(l // tm,), in_specs=in_specs, out_specs=out_specs, out_shape=out_shape,
        compiler_params=_params(("arbitrary",)),
    )(*args)


def _softplus_parts(z):
    e = jnp.exp(-jnp.abs(z))
    den = 1.0 + e
    lb = jnp.minimum(z, 0.0) - jnp.log(den)
    return lb, lb - z, jnp.exp(lb)


def _tri(cmp, n=HEAD):
    row = lax.broadcasted_iota(jnp.int32, (n, n), 0)
    col = lax.broadcasted_iota(jnp.int32, (n, n), 1)
    return cmp(row, col)


ATT_T = 256
ATT_DEAD = 104.0


def _any_alive(runs):
    return functools.reduce(jnp.maximum, [jnp.max(r) for r in runs]) > -ATT_DEAD


def _attn_fwd(qkv, wb, name, hp=4, ride=None):
    l = qkv.shape[0]
    t = ATT_T
    nh, nq = wb // HEAD, l // t
    hp = min(hp, nh)
    ng, wg = nh // hp, hp * HEAD
    scale = 1.0 / math.sqrt(HEAD)

    def body(q_ref, k_ref, v_ref, o_ref):
        i = pl.program_id(1)
        valid = _tri(lambda r, c: c < r, t)
        m_gt = _bf(_tri(lambda r, c: r > c, t).astype(F32))

        def tile(j, carry, diag):
            rows = pl.ds(pl.multiple_of(j * t, t), t)
            cols = [slice(hh * HEAD, (hh + 1) * HEAD) for hh in range(hp)]
            zs = [_dot(q_ref[:, cs], k_ref[rows, cs], NT) * scale for cs in cols]
            lbs, lks = [], []
            for z in zs:
                lb, lk, _ = _softplus_parts(z)
                lbs.append(lb)
                lks.append(jnp.where(valid, lk, 0.0) if diag else lk)
            laters = [_dot(_bf(lk), m_gt) for lk in lks]
            ws = [jnp.exp(lb + later + run) for lb, later, (_, run) in zip(lbs, laters, carry)]
            if diag:
                ws = [jnp.where(valid, w, 0.0) for w in ws]
            return tuple((acc + _dot(_bf(w), v_ref[rows, cs]), run + jnp.sum(lk, axis=1, keepdims=True))
                         for w, lk, cs, (acc, run) in zip(ws, lks, cols, carry))

        zero = (jnp.zeros((t, HEAD), F32), jnp.zeros((t, 1), F32))
        carry = tile(i, (zero,) * hp, True)
        _, carry = lax.while_loop(lambda c: (c[0] < i) & _any_alive([run for _, run in c[1]]),
                                  lambda c: (c[0] + 1, tile(i - 1 - c[0], c[1], False)), (jnp.int32(0), carry))
        for hh, (acc, _) in enumerate(carry):
            o_ref[:, hh * HEAD:(hh + 1) * HEAD] = acc

    blk = lambda off: pl.BlockSpec((t, wg), lambda h, i: (i, off + h))
    full = lambda off: pl.BlockSpec((l, wg), lambda h, i: (0, off + h))
    out = pl.BlockSpec((t, wg), lambda h, i: (i, h))
    outs = _call(body, name=name, grid=(ng, nq), in_specs=[blk(0), full(ng), full(2 * ng)], out_specs=[out],
                 out_shape=[jax.ShapeDtypeStruct((l, wb), F32)],
                 args=(qkv, qkv, qkv), sem=("parallel", "arbitrary"), ride=ride)
    return outs[0], outs[1:]


def _attn_bwd(qkv, proj, dcat, wa, wb, name, hp=2, ride=None):
    l = qkv.shape[0]
    t = ATT_T
    nh, nq = wb // HEAD, l // t
    hp = min(hp, nh)
    ng, wg = nh // hp, hp * HEAD
    scale = 1.0 / math.sqrt(HEAD)

    def body(q_ref, k_ref, v_ref, bz_ref, dc_ref, dq_ref, dkt_out, dvt_out, do_s, qt_s, dot_s,
             dkt_ref, dvt_ref, out_sems):
        i = pl.program_id(1)

        @pl.when(i == 0)
        def _():
            dkt_ref[...] = jnp.zeros_like(dkt_ref)
            dvt_ref[...] = jnp.zeros_like(dvt_ref)

        do = dc_ref[...].astype(F32) * _silu(bz_ref[...].astype(F32))
        do_s[...] = _bf(do)
        for hh in range(hp):
            cs = slice(hh * HEAD, (hh + 1) * HEAD)
            qt_s[hh] = _bf(q_ref[:, cs].astype(F32).T * scale)
            dot_s[hh] = _bf(do[:, cs].T)
        valid = _tri(lambda r, c: c < r, t)
        m_le = _bf(_tri(lambda r, c: r <= c, t).astype(F32))
        m_lt = _bf(_tri(lambda r, c: r < c, t).astype(F32))

        heads = range(hp)
        cols = [slice(hh * HEAD, (hh + 1) * HEAD) for hh in heads]

        def row_sums(j, runs, diag):
            rows = pl.ds(pl.multiple_of(j * t, t), t)
            out = []
            for cs, run in zip(cols, runs):
                _, lk, _ = _softplus_parts(_dot(q_ref[:, cs], k_ref[rows, cs], NT) * scale)
                if diag:
                    lk = jnp.where(valid, lk, 0.0)
                out.append(run + jnp.sum(lk, axis=1, keepdims=True))
            return tuple(out)

        runs = row_sums(i, (jnp.zeros((t, 1), F32),) * hp, True)
        below, lktot = lax.while_loop(lambda c: (c[0] < i) & _any_alive(c[1]),
                                      lambda c: (c[0] + 1, row_sums(i - 1 - c[0], c[1], False)), (jnp.int32(0), runs))

        def tile(j, carry, diag):
            rows = pl.ds(pl.multiple_of(j * t, t), t)
            zs = [_dot(q_ref[:, cs], k_ref[rows, cs], NT) * scale for cs in cols]
            dws = [_dot(do_s[:, cs], v_ref[rows, cs], NT) for cs in cols]
            lbs, lks, sigs = [], [], []
            for z in zs:
                lb, lk, sig = _softplus_parts(z)
                lbs.append(lb)
                lks.append(jnp.where(valid, lk, 0.0) if diag else lk)
                sigs.append(sig)
            pins = [_dot(_bf(lk), m_le) for lk in lks]
            ws = [jnp.exp(lbs[hh] + (lktot[hh] - carry[hh][1]) - pins[hh]) for hh in heads]
            if diag:
                ws = [jnp.where(valid, w, 0.0) for w in ws]
            das = [dw * w for dw, w in zip(dws, ws)]
            pexs = [_dot(_bf(da), m_lt) for da in das]
            dzs = [das[hh] - sigs[hh] * (das[hh] + carry[hh][2] + pexs[hh]) for hh in heads]
            if diag:
                dzs = [jnp.where(valid, dz, 0.0) for dz in dzs]
            dzs = [_bf(dz) for dz in dzs]
            out = []
            for hh in heads:
                dkt, dvt = _dot(qt_s[hh], dzs[hh]), _dot(dot_s[hh], _bf(ws[hh]))
                for half in range(t // HEAD):
                    dkt_ref[hh, sub * j + half] += dkt[:, half * HEAD:(half + 1) * HEAD]
                    dvt_ref[hh, sub * j + half] += dvt[:, half * HEAD:(half + 1) * HEAD]
                dq, cpre, ppre = carry[hh]
                out.append((dq + _dot(dzs[hh], k_ref[rows, cols[hh]]), cpre + jnp.sum(lks[hh], axis=1, keepdims=True),
                            ppre + pexs[hh][:, t - 1:] + das[hh][:, t - 1:]))
            return tuple(out)

        zero = (jnp.zeros((t, HEAD), F32), jnp.zeros((t, 1), F32), jnp.zeros((t, 1), F32))
        carry = lax.fori_loop(i - below, i, lambda j, c: tile(j, c, False), (zero,) * hp)
        carry = tile(i, carry, True)
        for hh in range(hp):
            dq_ref[:, hh * HEAD:(hh + 1) * HEAD] = carry[hh][0] * scale

        @pl.when(i == nq - 1)
        def _():
            heads = pl.ds(pl.program_id(0) * hp, hp)
            copies = [pltpu.make_async_copy(dkt_ref, dkt_out.at[heads], out_sems.at[0]),
                      pltpu.make_async_copy(dvt_ref, dvt_out.at[heads], out_sems.at[1])]
            for cp in copies:
                cp.start()
            for cp in copies:
                cp.wait()

    sub = t // HEAD
    blk = lambda off: pl.BlockSpec((t, wg), lambda h, i: (i, off + h))
    full = lambda off: pl.BlockSpec((l, wg), lambda h, i: (0, off + h))
    acc_shape = jax.ShapeDtypeStruct((nh, l // HEAD, HEAD, HEAD), F32)
    acc_scratch = pltpu.VMEM((hp, l // HEAD, HEAD, HEAD), F32)
    outs = _call(
        body, name=name, grid=(ng, nq),
        in_specs=[blk(0), full(ng), full(2 * ng), blk(3 * wa // wg), blk(wa // wg)],
        out_specs=[blk(0), ANY, ANY], out_shape=[jax.ShapeDtypeStruct((l, wb), F32), acc_shape, acc_shape],
        scratch_shapes=[pltpu.VMEM((t, wg), BF16), pltpu.VMEM((hp, HEAD, t), BF16), pltpu.VMEM((hp, HEAD, t), BF16),
                        acc_scratch, acc_scratch, pltpu.SemaphoreType.DMA((2,))],
        args=(qkv, qkv, qkv, proj, dcat), sem=("parallel", "arbitrary"), ride=ride)
    return outs[0], outs[1], outs[2], outs[3:]


def _sgu_heads(v, g_ref, w_ref, bt_ref, nh):
    keep = _tri(lambda r, c: r >= c)
    out = []
    for h in range(nh):
        cols = slice(h * HEAD, (h + 1) * HEAD)
        nv, r = _rownorm(v[:, cols])
        wm = jnp.where(keep, w_ref[h], 0.0)
        s = _dot(_bf(wm), _bf(nv * g_ref[:, cols])) + bt_ref[:, h:h + 1]
        out.append((nv, r, wm, s))
    return out


def _sgu_fwd(proj, out_b, norm_g, sgu_w, sgu_bt, wa, wb, name):
    l, n = proj.shape
    nh = wa // HEAD

    def body(au_ref, av_ref, az_ref, bz_ref, ob_ref, g_ref, w_ref, bt_ref, cat_ref):
        u, v, sz = _gelu(au_ref[...].astype(F32)), _gelu(av_ref[...].astype(F32)), _silu(az_ref[...].astype(F32))
        for h, (_, _, _, s) in enumerate(_sgu_heads(v, g_ref, w_ref, bt_ref, nh)):
            cols = slice(h * HEAD, (h + 1) * HEAD)
            cat_ref[:, cols] = _bf(u[:, cols] * s * sz[:, cols])
        cat_ref[:, wa:] = _bf(ob_ref[...] * _silu(bz_ref[...].astype(F32)))

    a_blk = lambda j: pl.BlockSpec((HEAD, wa), lambda i: (i, j))
    return pl.pallas_call(
        body, name=name, grid=(l // HEAD,),
        in_specs=[a_blk(0), a_blk(1), a_blk(2), a_blk(3), pl.BlockSpec((HEAD, wb), lambda i: (i, 0)),
                  _vec_spec(wa), pl.BlockSpec((nh, HEAD, HEAD), lambda i: (0, 0, 0)),
                  pl.BlockSpec((HEAD, nh), lambda i: (0, 0))],
        out_specs=pl.BlockSpec((HEAD, wa + wb), lambda i: (i, 0)),
        out_shape=jax.ShapeDtypeStruct((l, wa + wb), BF16),
        compiler_params=_params(("parallel",)),
    )(proj, proj, proj, proj, out_b, norm_g, sgu_w, sgu_bt)


def _sgu_bwd(proj, out_b, dcat, dq, dk, dv, norm_g, sgu_w, sgu_bt, wa, wb, name, ride=None):
    l = proj.shape[0]
    n = 3 * wa + 4 * wb
    nh = wa // HEAD

    def body(au_ref, av_ref, az_ref, bz_ref, ob_ref, dc_ref, dq_ref, dk_ref, dv_ref, g_ref, w_ref, wt_ref, bt_ref,
             dp_ref, dw_ref, dbt_ref, dg_ref):
        first = pl.program_id(0) == 0
        keep = _tri(lambda r, c: r >= c)
        au, av, az = au_ref[...].astype(F32), av_ref[...].astype(F32), az_ref[...].astype(F32)
        u, v, sz = _gelu(au), _gelu(av), _silu(az)
        dgelu_u, dgelu_v, dsilu_z = _gelu_grad(au), _gelu_grad(av), _silu_grad(az)
        heads = _sgu_heads(v, g_ref, w_ref, bt_ref, nh)
        cols = [slice(h * HEAD, (h + 1) * HEAD) for h in range(nh)]
        dss = []
        for h, (nv, r, wm, s) in enumerate(heads):
            dca, uh, szh = dc_ref[:, cols[h]].astype(F32), u[:, cols[h]], sz[:, cols[h]]
            dp_ref[:, cols[h]] = _bf(dca * s * szh * dgelu_u[:, cols[h]])
            dp_ref[:, 2 * wa + h * HEAD:2 * wa + (h + 1) * HEAD] = _bf(dca * uh * s * dsilu_z[:, cols[h]])
            dss.append(dca * uh * szh)
        dws = [_dot(_bf(ds), _bf(nv * g_ref[:, cs]), NT) for ds, cs, (nv, _, _, _) in zip(dss, cols, heads)]
        keep_t = _tri(lambda r, c: r <= c)
        dvhs = [_dot(_bf(jnp.where(keep_t, wt_ref[h], 0.0)), _bf(dss[h])) for h in range(nh)]
        dg_parts = []
        for h, (nv, r, wm, s) in enumerate(heads):
            _acc(dw_ref.at[h], first, jnp.where(keep, dws[h], 0.0))
            _acc(dbt_ref.at[:, h:h + 1], first, jnp.sum(dss[h], axis=1, keepdims=True))
            dg_parts.append(_colsum(dvhs[h] * nv))
            dnv = dvhs[h] * g_ref[:, cols[h]]
            dvv = r * (dnv - nv * jnp.mean(dnv * nv, axis=-1, keepdims=True))
            dp_ref[:, wa + h * HEAD:wa + (h + 1) * HEAD] = _bf(dvv * dgelu_v[:, cols[h]])
        _acc(dg_ref, first, jnp.concatenate(dg_parts, axis=1))
        base = 3 * wa
        dp_ref[:, base:base + wb] = _bf(dq_ref[...])
        for h in range(wb // HEAD):
            dp_ref[:, base + wb + h * HEAD:base + wb + (h + 1) * HEAD] = _bf(dk_ref[h, 0].T)
            dp_ref[:, base + 2 * wb + h * HEAD:base + 2 * wb + (h + 1) * HEAD] = _bf(dv_ref[h, 0].T)
        dp_ref[:, base + 3 * wb:] = _bf(dc_ref[:, wa:].astype(F32) * ob_ref[...]
                                        * _silu_grad(bz_ref[...].astype(F32)))

    a_blk = lambda j: pl.BlockSpec((HEAD, wa), lambda i: (i, j))
    b_blk = pl.BlockSpec((HEAD, wb), lambda i: (i, 0))
    t_blk = pl.BlockSpec((wb // HEAD, 1, HEAD, HEAD), lambda i: (0, i, 0, 0))
    w_spec = pl.BlockSpec((nh, HEAD, HEAD), lambda i: (0, 0, 0))
    bt_spec = pl.BlockSpec((HEAD, nh), lambda i: (0, 0))
    outs = _call(
        body, name=name, grid=(l // HEAD,), ride=ride, sem=("arbitrary",),
        in_specs=[a_blk(0), a_blk(1), a_blk(2), a_blk(3), b_blk, pl.BlockSpec((HEAD, wa + wb), lambda i: (i, 0)),
                  b_blk, t_blk, t_blk, _vec_spec(wa), w_spec, w_spec, bt_spec],
        out_specs=[pl.BlockSpec((HEAD, n), lambda i: (i, 0)), w_spec, bt_spec, _vec_spec(wa)],
        out_shape=[jax.ShapeDtypeStruct((l, n), BF16), jax.ShapeDtypeStruct((nh, HEAD, HEAD), F32),
                   jax.ShapeDtypeStruct((HEAD, nh), F32), jax.ShapeDtypeStruct((1, wa), F32)],
        args=(proj, proj, proj, proj, out_b, dcat, dq, dk, dv, norm_g, sgu_w, sgu_w.transpose(0, 2, 1), sgu_bt))
    return (*outs[:4], outs[4:])


def _ssm_discretise(lr, li, ldt, br, bi):
    dt = jnp.exp(ldt)
    mag = jnp.exp(lr * dt)
    a_re = mag * jnp.cos(li * dt)
    a_im = mag * jnp.sin(li * dt)
    den = lr * lr + li * li
    nr = a_re - 1.0
    coef_re = (nr * lr + a_im * li) / den
    coef_im = (a_im * lr - nr * li) / den
    return a_re, a_im, coef_re * br - coef_im * bi, coef_re * bi + coef_im * br


def _ssm_prep(lr, li, ldt, br, bi, lr_row, li_row, ldt_row, name):
    s, c = br.shape

    def body(lr_ref, li_ref, ldt_ref, br_ref, bi_ref, lrr_ref, lir_ref, ldtr_ref, bbr_ref, bbi_ref, tr_ref, ti_ref):
        _, _, bbr, bbi = _ssm_discretise(lr_ref[...], li_ref[...], ldt_ref[...], br_ref[...], bi_ref[...])
        bbr_ref[...] = bbr
        bbi_ref[...] = bbi
        row = lax.broadcasted_iota(jnp.int32, (SCAN_ROWS, 1), 0)
        blk, r = jnp.right_shift(row, 3), jnp.bitwise_and(row, 7)
        kind, rev = jnp.bitwise_and(blk, 3), blk >= 4
        step = jnp.left_shift(1, kind)
        n = jnp.where(kind < 3, step, jnp.where(rev, 8 - r, r + 1)).astype(F32)
        keep = (kind == 3) | (rev & (r < 8 - step)) | (jnp.logical_not(rev) & (r >= step))
        dt = jnp.exp(ldtr_ref[...])
        mag = jnp.exp(n * (lrr_ref[...] * dt))
        ang = n * (lir_ref[...] * dt)
        tr_ref[...] = jnp.where(keep, mag * jnp.cos(ang), 0.0)
        ti_ref[...] = jnp.where(keep, jnp.where(rev, -1.0, 1.0) * mag * jnp.sin(ang), 0.0)

    col = jax.ShapeDtypeStruct((s, c), F32)
    row = jax.ShapeDtypeStruct((SCAN_ROWS, s), F32)
    return pl.pallas_call(body, name=name, out_shape=[col, col, row, row])(
        lr, li, ldt, br, bi, lr_row, li_row, ldt_row)


def _ssm_prep_bwd(lr, li, ldt, br, bi, da_re, da_im, dbb_re, dbb_im, p, name):
    s, c = br.shape

    def body(lr_ref, li_ref, ldt_ref, br_ref, bi_ref, dar_ref, dai_ref, dbr_ref, dbi_ref,
             dlr_ref, dli_ref, dldt_ref, dbre_ref, dbim_ref):
        args = (lr_ref[...], li_ref[...], ldt_ref[...], br_ref[...], bi_ref[...])
        _, vjp = jax.vjp(_ssm_discretise, *args)
        dlr, dli, dldt, dbr, dbi = vjp((dar_ref[...], dai_ref[...], dbr_ref[...], dbi_ref[...]))
        dlr_ref[...] = dlr
        dli_ref[...] = dli
        dbre_ref[...] = dbr
        dbim_ref[...] = dbi
        idx = lax.broadcasted_iota(jnp.int32, (s, s // p), 0)
        grp = lax.broadcasted_iota(jnp.int32, (s, s // p), 1)
        own = (idx >= grp * p) & (idx < (grp + 1) * p)
        dldt_ref[...] = _colsum(jnp.where(own, dldt, 0.0))

    col1 = jax.ShapeDtypeStruct((s, 1), F32)
    colc = jax.ShapeDtypeStruct((s, c), F32)
    return pl.pallas_call(
        body, name=name, out_shape=[col1, col1, jax.ShapeDtypeStruct((1, s // p), F32), colc, colc],
    )(lr, li, ldt, br, bi, da_re, da_im, dbb_re, dbb_im)


SCAN_ROWS = 64


def _scan_groups(xr, xi, tr_ref, ti_ref, cr, ci, reverse, per_group=None):
    ng = xr.shape[0] // 8
    base = SCAN_ROWS // 2 if reverse else 0
    pr, pi = tr_ref[base + 24:base + 32, :], ti_ref[base + 24:base + 32, :]
    edge = slice(0, 1) if reverse else slice(7, 8)
    out_r, out_i = [None] * ng, [None] * ng
    for g in (range(ng - 1, -1, -1) if reverse else range(ng)):
        sr, si = xr[8 * g:8 * g + 8, :], xi[8 * g:8 * g + 8, :]
        for k in range(3):
            ar, ai = tr_ref[base + 8 * k:base + 8 * k + 8, :], ti_ref[base + 8 * k:base + 8 * k + 8, :]
            shift = 8 - (1 << k) if reverse else 1 << k
            rr, ri = pltpu.roll(sr, shift, 0), pltpu.roll(si, shift, 0)
            sr, si = sr + ar * rr - ai * ri, si + ar * ri + ai * rr
        sr, si = sr + pr * cr - pi * ci, si + pr * ci + pi * cr
        cr, ci = sr[edge, :], si[edge, :]
        out_r[g], out_i[g] = sr, si
        if per_group is not None:
            per_group(g, sr, si)
    return jnp.concatenate(out_r, axis=0), jnp.concatenate(out_i, axis=0), cr, ci


def _ssm_fwd(proj, bbd, ccd, pw_re, pw_im, d_skip, w, name):
    l = proj.shape[0]
    nb, cw, ns2 = bbd.shape
    ns = ns2 // 2
    nc = l // SSM_T

    def body(u_ref, bbd_ref, ccd_ref, pr_ref, pi_ref, d_ref, y_ref, hsr_ref, hsi_ref, h_ref, hr_s, hi_s):
        @pl.when(pl.program_id(1) == 0)
        def _():
            hr_s[...] = jnp.zeros_like(hr_s)
            hi_s[...] = jnp.zeros_like(hi_s)

        hsr_ref[...] = hr_s[...].reshape(hsr_ref.shape)
        hsi_ref[...] = hi_s[...].reshape(hsi_ref.shape)
        u = u_ref[...]
        bu = _dot(_bf(u), bbd_ref[0])
        hr, hi, cr, ci = _scan_groups(bu[:, :ns], bu[:, ns:], pr_ref, pi_ref, hr_s[...], hi_s[...], False)
        hr_s[...] = cr
        hi_s[...] = ci
        h_bf = _bf(jnp.concatenate([hr, hi], axis=1))
        h_ref[...] = h_bf
        y_ref[...] = _dot(h_bf, ccd_ref[0]) + d_ref[...] * u

    tab = pl.BlockSpec((SCAN_ROWS, ns), lambda b, k: (0, b))
    return pl.pallas_call(
        body, name=name, grid=(nb, nc),
        in_specs=[pl.BlockSpec((SSM_T, cw), lambda b, k: (k, b)),
                  pl.BlockSpec((1, cw, ns2), lambda b, k: (b, 0, 0)),
                  pl.BlockSpec((1, ns2, cw), lambda b, k: (b, 0, 0)),
                  tab, tab, pl.BlockSpec((1, cw), lambda b, k: (0, b))],
        out_specs=[pl.BlockSpec((SSM_T, cw), lambda b, k: (k, b)),
                   pl.BlockSpec((1, 1, ns), lambda b, k: (k, 0, b)), pl.BlockSpec((1, 1, ns), lambda b, k: (k, 0, b)),
                   pl.BlockSpec((SSM_T, ns2), lambda b, k: (k, b))],
        out_shape=[jax.ShapeDtypeStruct((l, w), F32), jax.ShapeDtypeStruct((nc, 1, nb * ns), F32),
                   jax.ShapeDtypeStruct((nc, 1, nb * ns), F32), jax.ShapeDtypeStruct((l, nb * ns2), BF16)],
        scratch_shapes=[pltpu.VMEM((1, ns), F32), pltpu.VMEM((1, ns), F32)],
        compiler_params=_params(("parallel", "arbitrary")),
    )(proj, bbd, ccd, pw_re, pw_im, d_skip)


def _ssm_bwd(proj, dy, hs_re, hs_im, h_all, bbd, ccd, pw_re, pw_im, d_skip, w, name, ride=None):
    l = proj.shape[0]
    nb, cw, ns2 = bbd.shape
    ns = ns2 // 2
    nc = l // SSM_T

    def body(u_ref, dy_ref, hsr_ref, hsi_ref, h_ref, bbd_ref, ccd_ref, pr_ref, pi_ref, d_ref,
             du_ref, dbbd_ref, dccd_ref, dar_ref, dai_ref, dd_ref, gr_s, gi_s):
        first = pl.program_id(1) == 0

        @pl.when(first)
        def _():
            gr_s[...] = jnp.zeros_like(gr_s)
            gi_s[...] = jnp.zeros_like(gi_s)

        u, dy = u_ref[...], dy_ref[...]
        dy_bf = _bf(dy)
        hr0, hi0 = hsr_ref[0], hsi_ref[0]
        h = h_ref[...].astype(F32)
        hr, hi = h[:, :ns], h[:, ns:]
        dh = _dot(dy_bf, ccd_ref[0], NT)
        row0 = lax.broadcasted_iota(jnp.int32, (8, ns), 0) == 0
        da = [jnp.zeros((8, ns), F32), jnp.zeros((8, ns), F32)]

        def fold(g, g_re, g_im):
            before_r = hr0 if g == 0 else hr[8 * g - 1:8 * g, :]
            before_i = hi0 if g == 0 else hi[8 * g - 1:8 * g, :]
            p_re = jnp.where(row0, before_r, pltpu.roll(hr[8 * g:8 * g + 8, :], 1, 0))
            p_im = jnp.where(row0, before_i, pltpu.roll(hi[8 * g:8 * g + 8, :], 1, 0))
            da[0] = da[0] + p_re * g_re + p_im * g_im
            da[1] = da[1] + p_re * g_im - p_im * g_re

        gr, gi, gcr, gci = _scan_groups(dh[:, :ns], dh[:, ns:], pr_ref, pi_ref, gr_s[...], gi_s[...], True, fold)
        gr_s[...] = gcr
        gi_s[...] = gci
        _acc(dar_ref, first, _colsum(da[0]))
        _acc(dai_ref, first, _colsum(da[1]))
        g_bf = _bf(jnp.concatenate([gr, gi], axis=1))
        _acc(dbbd_ref.at[0], first, _dot(_bf(u.T), g_bf))
        _acc(dccd_ref.at[0], first, _dot(_bf(h.T), dy_bf))
        du_ref[...] = _bf(_dot(g_bf, bbd_ref[0], NT) + d_ref[...] * dy)
        _acc(dd_ref, first, _colsum(dy * u))

    rev = lambda b, k: (nc - 1 - k, b)
    outs = _call(
        body, name=name, grid=(nb, nc), ride=ride, sem=("parallel", "arbitrary"),
        args=(proj, dy, hs_re, hs_im, h_all, bbd, ccd, pw_re, pw_im, d_skip),
        in_specs=[pl.BlockSpec((SSM_T, cw), rev), pl.BlockSpec((SSM_T, cw), rev),
                  pl.BlockSpec((1, 1, ns), lambda b, k: (nc - 1 - k, 0, b)),
                  pl.BlockSpec((1, 1, ns), lambda b, k: (nc - 1 - k, 0, b)),
                  pl.BlockSpec((SSM_T, ns2), rev),
                  pl.BlockSpec((1, cw, ns2), lambda b, k: (b, 0, 0)),
                  pl.BlockSpec((1, ns2, cw), lambda b, k: (b, 0, 0)),
                  pl.BlockSpec((SCAN_ROWS, ns), lambda b, k: (0, b)), pl.BlockSpec((SCAN_ROWS, ns), lambda b, k: (0, b)),
                  pl.BlockSpec((1, cw), lambda b, k: (0, b))],
        out_specs=[pl.BlockSpec((SSM_T, cw), rev),
                   pl.BlockSpec((1, cw, ns2), lambda b, k: (b, 0, 0)),
                   pl.BlockSpec((1, ns2, cw), lambda b, k: (b, 0, 0)),
                   pl.BlockSpec((1, ns), lambda b, k: (0, b)), pl.BlockSpec((1, ns), lambda b, k: (0, b)),
                   pl.BlockSpec((1, cw), lambda b, k: (0, b))],
        out_shape=[jax.ShapeDtypeStruct((l, w), BF16), jax.ShapeDtypeStruct(bbd.shape, F32),
                   jax.ShapeDtypeStruct(ccd.shape, F32), jax.ShapeDtypeStruct((1, nb * ns), F32),
                   jax.ShapeDtypeStruct((1, nb * ns), F32), jax.ShapeDtypeStruct((1, w), F32)],
        scratch_shapes=[pltpu.VMEM((1, ns), F32), pltpu.VMEM((1, ns), F32)])
    return (*outs[:6], outs[6:])


def _block_diag_b(bb_re, bb_im, g, p, c):
    nb = g // SSM_GB
    keep = _same_group(SSM_GB * c, c, SSM_GB * p, p)

    def one(bb):
        t = bb.reshape(nb, SSM_GB, p, c).transpose(0, 1, 3, 2).reshape(nb, SSM_GB * c, p)
        return jnp.where(keep, jnp.tile(t, (1, 1, SSM_GB)), 0.0)

    return jnp.concatenate([one(bb_re), one(bb_im)], axis=2)


def _same_group(rows, per_row, cols, per_col):
    r = lax.broadcasted_iota(jnp.int32, (rows, cols), 0) // per_row
    q = lax.broadcasted_iota(jnp.int32, (rows, cols), 1) // per_col
    return r == q


def _block_diag_c(c_re, c_im, g, p, c):
    nb = g // SSM_GB
    keep = _same_group(SSM_GB * p, p, SSM_GB * c, c)

    def one(cc):
        t = cc.reshape(nb, SSM_GB, c, p).transpose(0, 1, 3, 2).reshape(nb, SSM_GB * p, c)
        return jnp.where(keep, jnp.tile(t, (1, 1, SSM_GB)), 0.0)

    return jnp.concatenate([one(c_re), one(-c_im)], axis=1)


def _diag_of_b(dbbd, g, p, c):
    nb = g // SSM_GB
    keep = _same_group(SSM_GB * c, c, SSM_GB * p, p)

    def one(blk):
        d = jnp.where(keep, blk, 0.0).reshape(nb, SSM_GB * c, SSM_GB, p).sum(axis=2)
        return d.reshape(nb, SSM_GB, c, p).transpose(0, 1, 3, 2).reshape(g * p, c)

    half = SSM_GB * p
    return one(dbbd[:, :, :half]), one(dbbd[:, :, half:])


def _diag_of_c(dccd, g, p, c):
    nb = g // SSM_GB
    keep = _same_group(SSM_GB * p, p, SSM_GB * c, c)

    def one(blk):
        d = jnp.where(keep, blk, 0.0).reshape(nb, SSM_GB * p, SSM_GB, c).sum(axis=2)
        return d.reshape(nb, SSM_GB, p, c).transpose(0, 1, 3, 2).reshape(g, c, p)

    half = SSM_GB * p
    return one(dccd[:, :half]), -one(dccd[:, half:])


def _glu_fwd(y, proj, w_glu, b_glu, name):
    l, w = y.shape
    tm = _tile(l, 256)

    def body(y_ref, z_ref, w_ref, b_ref, o_ref):
        g = _gelu(y_ref[...])
        t = _dot(_bf(g), w_ref[...]) + b_ref[...]
        o_ref[...] = _bf(g * _sigmoid(t) * _silu(z_ref[...]))

    return pl.pallas_call(
        body, name=name, grid=(l // tm,),
        in_specs=[_row_spec(tm, w), pl.BlockSpec((tm, w), lambda i: (i, 1)),
                  pl.BlockSpec((w, w), lambda i: (0, 0)), _vec_spec(w)],
        out_specs=_row_spec(tm, w), out_shape=jax.ShapeDtypeStruct((l, w), BF16),
        compiler_params=_params(("parallel",)),
    )(y, proj, w_glu, b_glu)


def _glu_bwd(do, y, proj, w_glu, b_glu, name):
    l, w = y.shape
    tm = _tile(l, 512)
    nsteps = l // tm

    def body(do_ref, y_ref, z_ref, w_ref, b_ref, dy_ref, dz_ref, dw_ref, db_ref, dw_acc):
        i = pl.program_id(0)
        first = i == 0
        yv, z, do = y_ref[...], z_ref[...], do_ref[...].astype(F32)
        g = _gelu(yv)
        g_bf = _bf(g)
        sg = _sigmoid(_dot(g_bf, w_ref[...]) + b_ref[...])
        dyy = do * _silu(z)
        dz_ref[...] = _bf(do * g * sg * _silu_grad(z))
        dt = dyy * g * sg * (1.0 - sg)
        dt_bf = _bf(dt)
        dg = dyy * sg + _dot(dt_bf, w_ref[...], NT)
        dy_ref[...] = dg * _gelu_grad(yv)
        _acc(dw_acc, first, _dot(_bf(g.T), dt_bf))
        _acc(db_ref, first, _colsum(dt))

        @pl.when(i == nsteps - 1)
        def _():
            dw_ref[...] = _bf(dw_acc[...])

    return pl.pallas_call(
        body, name=name, grid=(nsteps,),
        in_specs=[_row_spec(tm, w), _row_spec(tm, w), pl.BlockSpec((tm, w), lambda i: (i, 1)),
                  pl.BlockSpec((w, w), lambda i: (0, 0)), _vec_spec(w)],
        out_specs=[_row_spec(tm, w), _row_spec(tm, w), pl.BlockSpec((w, w), lambda i: (0, 0)), _vec_spec(w)],
        out_shape=[jax.ShapeDtypeStruct((l, w), F32), jax.ShapeDtypeStruct((l, w), BF16),
                   jax.ShapeDtypeStruct((w, w), BF16), jax.ShapeDtypeStruct((1, w), F32)],
        scratch_shapes=[pltpu.VMEM((w, w), F32)],
        compiler_params=_params(("arbitrary",)),
    )(do, y, proj, w_glu, b_glu)


MOD_ROWS = 128


def _mod_fwd(cond_pad, w_mod, b_shard, name):
    nl, d, ncol = w_mod.shape
    tn = _tile(ncol, 512)

    def body(c_ref, w_ref, b_ref, o_ref):
        o_ref[0] = _dot(_bf(c_ref[...]), _bf(w_ref[0])) + b_ref[0]

    return pl.pallas_call(
        body, name=name, grid=(nl, ncol // tn),
        in_specs=[pl.BlockSpec((MOD_ROWS, d), lambda a, j: (0, 0)),
                  pl.BlockSpec((1, d, tn), lambda a, j: (a, 0, j)),
                  pl.BlockSpec((1, 1, tn), lambda a, j: (a, 0, j))],
        out_specs=pl.BlockSpec((1, MOD_ROWS, tn), lambda a, j: (a, 0, j)),
        out_shape=jax.ShapeDtypeStruct((nl, MOD_ROWS, ncol), F32),
        compiler_params=_params(("parallel", "parallel")),
    )(cond_pad, w_mod, b_shard)


def _mod_bwd(cond_pad_t, dmod_pad, name):
    nl, _, ncol = dmod_pad.shape
    d = cond_pad_t.shape[0]
    tn = _tile(ncol, 512)

    def body(c_ref, dm_ref, o_ref):
        o_ref[0] = _dot(_bf(c_ref[...]), _bf(dm_ref[0]))

    return pl.pallas_call(
        body, name=name, grid=(nl, ncol // tn),
        in_specs=[pl.BlockSpec((d, MOD_ROWS), lambda a, j: (0, 0)),
                  pl.BlockSpec((1, MOD_ROWS, tn), lambda a, j: (a, 0, j))],
        out_specs=pl.BlockSpec((1, d, tn), lambda a, j: (a, 0, j)),
        out_shape=jax.ShapeDtypeStruct((nl, d, ncol), F32),
        compiler_params=_params(("parallel", "parallel")),
    )(cond_pad_t, dmod_pad)


def _silu_rows(c2d, name):
    def body(c_ref, o_ref):
        o_ref[...] = _silu(c_ref[...])

    return pl.pallas_call(body, name=name, out_shape=jax.ShapeDtypeStruct(c2d.shape, F32))(c2d)


def _sum_leading(x, name):
    n, r, c = x.shape
    tr = _tile(r, max(16, (1 << 20) // (4 * c)), 16 if r % 16 == 0 else 8)

    def body(x_ref, o_ref):
        acc = x_ref[0].astype(F32)
        for k in range(1, n):
            acc = acc + x_ref[k].astype(F32)
        o_ref[...] = acc

    return pl.pallas_call(
        body, name=name, grid=(r // tr,),
        in_specs=[pl.BlockSpec((n, tr, c), lambda i: (0, i, 0))], out_specs=pl.BlockSpec((tr, c), lambda i: (i, 0)),
        out_shape=jax.ShapeDtypeStruct((r, c), F32), compiler_params=_params(("parallel",)),
    )(x)


def _adamw(w, gs, m, v, name, ride=None):
    r, c = w.shape
    tr = _tile(r, max(8, (3 << 19) // (4 * c)), 8)
    ng = len(gs)

    def body(*refs):
        w_ref, g_refs, m_ref, v_ref = refs[0], refs[1:1 + ng], refs[1 + ng], refs[2 + ng]
        g_ref, d_ref, nm_ref, nv_ref = refs[3 + ng:]
        g = g_refs[0][...]
        for extra in g_refs[1:]:
            g = g + extra[...]
        g_ref[...] = g
        d_ref[...], nm_ref[...], nv_ref[...] = _adamw_math(w_ref[...], g, m_ref[...], v_ref[...])

    spec = pl.BlockSpec((tr, c), lambda i: (i, 0))
    shp = jax.ShapeDtypeStruct((r, c), F32)
    outs = _call(body, name=name, grid=(r // tr,), in_specs=[spec] * (3 + ng), out_specs=[spec] * 4,
                 out_shape=[shp] * 4, args=(w, *gs, m, v), sem=("parallel",), ride=ride)
    return outs if ride is None else (outs[:4], outs[4:])


def _adamw_math(w, g, m, v):
    nm = ADAM_B1 * m + (1.0 - ADAM_B1) * g
    nv = ADAM_B2 * v + (1.0 - ADAM_B2) * (g * g)
    m_hat = nm / (1.0 - ADAM_B1 ** ADAM_STEP)
    v_hat = nv / (1.0 - ADAM_B2 ** ADAM_STEP)
    return -ADAM_LR * (m_hat / (jnp.sqrt(v_hat) + ADAM_EPS) + ADAM_WD * w), nm, nv


def _adamw_many(ws, gs, ms, vs, name):
    n = len(ws)

    def body(*refs):
        w_refs, g_refs, m_refs, v_refs = (refs[k * n:(k + 1) * n] for k in range(4))
        outs = refs[4 * n:]
        for i in range(n):
            outs[3 * i][...], outs[3 * i + 1][...], outs[3 * i + 2][...] = _adamw_math(
                w_refs[i][...], g_refs[i][...], m_refs[i][...], v_refs[i][...])

    out_shape = [jax.ShapeDtypeStruct(w.shape, F32) for w in ws for _ in range(3)]
    outs = pl.pallas_call(body, name=name, out_shape=out_shape, compiler_params=_params())(*ws, *gs, *ms, *vs)
    return [tuple(outs[3 * i:3 * i + 3]) for i in range(n)]


ANY = pl.BlockSpec(memory_space=pl.ANY)


def _flip(v, bit):
    return 1 - v if bit else v


def _allgather8_ops(x_ref, o_ref, send_sems, recv_sems, local_sem):
    mx, my, mc = lax.axis_index("x"), lax.axis_index("y"), lax.axis_index("c")
    me = 4 * mx + 2 * my + mc

    def mine():
        return pltpu.make_async_copy(x_ref, o_ref.at[me], local_sem)

    def copy(j, outgoing):
        peer = (_flip(mx, j & 4), _flip(my, j & 2), _flip(mc, j & 1))
        slot = me if outgoing else 4 * peer[0] + 2 * peer[1] + peer[2]
        return pltpu.make_async_remote_copy(
            src_ref=x_ref, dst_ref=o_ref.at[slot], send_sem=send_sems.at[j - 1], recv_sem=recv_sems.at[j - 1],
            device_id=peer, device_id_type=MESH)

    def start():
        mine().start()
        for j in range(1, 8):
            copy(j, True).start()

    def wait():
        for j in range(1, 8):
            copy(j, False).wait()
        mine().wait()

    return start, wait


def _ride_all8(x):
    return dict(xs=[x], shapes=[jax.ShapeDtypeStruct((8,) + x.shape, x.dtype)],
                sems=[pltpu.SemaphoreType.DMA((7,)), pltpu.SemaphoreType.DMA((7,)), pltpu.SemaphoreType.DMA],
                ops=lambda x_refs, o_refs, sems: _allgather8_ops(x_refs[0], o_refs[0], *sems))


def _ride_chip(xs, gather):
    return dict(xs=list(xs), shapes=_chip_exchange_shapes(xs, gather), sems=_chip_exchange_sems(len(xs)),
                ops=lambda x_refs, o_refs, sems: _chip_exchange_ops(x_refs, o_refs, *sems, gather))


def _allgather8(x, name):
    return _exchange([_ride_all8(x)], name)[0]


def _gather_halves_ops(x_ref, o_ref, ici_send, ici_recv, d2d_send, d2d_recv, local_sem):
    half = x_ref.shape[0] // 2
    quarter = half // 2
    mx, my, mc = lax.axis_index("x"), lax.axis_index("y"), lax.axis_index("c")
    k0, kx, ky, kd = 2 * mx + my, 2 * (1 - mx) + my, 2 * mx + (1 - my), 2 * (1 - mx) + (1 - my)
    x_nbr, y_nbr, sib = (1 - mx, my, mc), (mx, 1 - my, mc), (mx, my, 1 - mc)

    def rows(core, part):
        if part is None:
            return pl.ds(pl.multiple_of(core * half, 16), half)
        return pl.ds(pl.multiple_of(core * half + part * quarter, 16), quarter)

    def local():
        return pltpu.make_async_copy(x_ref, o_ref.at[k0], local_sem)

    def ici(n, outgoing):
        to = x_nbr if n in (0, 2) else y_nbr
        if n < 2:
            src = x_ref.at[rows(mc, None)]
            dst = o_ref.at[k0 if outgoing else (kx if n == 0 else ky), rows(mc, None)]
        else:
            part = n - 2
            src = o_ref.at[ky if n == 2 else kx, rows(mc, part)]
            dst = o_ref.at[(ky if n == 2 else kx) if outgoing else kd, rows(mc, part)]
        return pltpu.make_async_remote_copy(src_ref=src, dst_ref=dst, send_sem=ici_send.at[n], recv_sem=ici_recv.at[n],
                                            device_id=to, device_id_type=MESH)

    def d2d(n, outgoing):
        slot, part = ((kx, None), (ky, None), (kd, 0), (kd, 1))[n]
        piece = o_ref.at[slot, rows(mc if outgoing else 1 - mc, part)]
        return pltpu.make_async_remote_copy(src_ref=piece, dst_ref=piece, send_sem=d2d_send.at[n],
                                            recv_sem=d2d_recv.at[n], device_id=sib, device_id_type=MESH)

    def start():
        local().start()
        ici(0, True).start()
        ici(1, True).start()

    def wait():
        ici(1, False).wait_recv()
        ici(2, True).start()
        d2d(1, True).start()
        ici(0, False).wait_recv()
        ici(3, True).start()
        d2d(0, True).start()
        ici(2, False).wait_recv()
        d2d(2, True).start()
        ici(3, False).wait_recv()
        d2d(3, True).start()
        for n in range(4):
            ici(n, True).wait_send()
            d2d(n, True).wait_send()
            d2d(n, False).wait_recv()
        local().wait()

    return start, wait


def _ride_halves(x):
    dma4 = pltpu.SemaphoreType.DMA((4,))
    return dict(xs=[x], shapes=[jax.ShapeDtypeStruct((4,) + x.shape, x.dtype)],
                sems=[dma4, dma4, dma4, dma4, pltpu.SemaphoreType.DMA],
                ops=lambda x_refs, o_refs, sems: _gather_halves_ops(x_refs[0], o_refs[0], *sems))


def _exchange(rides, name):
    xs = [x for r in rides for x in r["xs"]]
    nx = len(xs)

    def body(*refs):
        x_refs, o_refs, sems = refs[:nx], refs[nx:2 * nx], refs[2 * nx:]
        ops, xo, so = [], 0, 0
        for r in rides:
            nr, ns = len(r["xs"]), len(r["sems"])
            ops.append(r["ops"](x_refs[xo:xo + nr], o_refs[xo:xo + nr], sems[so:so + ns]))
            xo, so = xo + nr, so + ns
        for start, _ in ops:
            start()
        for _, wait in ops:
            wait()

    return pl.pallas_call(body, name=name, in_specs=[ANY] * nx, out_specs=[ANY] * nx,
                          out_shape=[s for r in rides for s in r["shapes"]],
                          scratch_shapes=[s for r in rides for s in r["sems"]])(*xs)


def _chip_exchange_shapes(xs, gather):
    return [jax.ShapeDtypeStruct(((4,) + x.shape) if gather else x.shape, x.dtype) for x in xs]


def _chip_exchange_sems(n):
    return [pltpu.SemaphoreType.DMA((3 * n,)), pltpu.SemaphoreType.DMA((3 * n,)), pltpu.SemaphoreType.DMA((n,))]


def _chip_exchange_ops(x_refs, o_refs, send_sems, recv_sems, local_sems, gather):
    n = len(x_refs)
    mx, my, mc = lax.axis_index("x"), lax.axis_index("y"), lax.axis_index("c")
    k0 = 2 * mx + my

    def local(a):
        src = x_refs[a] if gather else x_refs[a].at[k0]
        return pltpu.make_async_copy(src, o_refs[a].at[k0], local_sems.at[a])

    def copy(a, j, outgoing):
        px, py = _flip(mx, j & 2), _flip(my, j & 1)
        kp = 2 * px + py
        if outgoing:
            src = x_refs[a] if gather else x_refs[a].at[kp]
            dst = o_refs[a].at[k0]
        else:
            src = x_refs[a] if gather else x_refs[a].at[k0]
            dst = o_refs[a].at[kp]
        s = a * 3 + j - 1
        return pltpu.make_async_remote_copy(
            src_ref=src, dst_ref=dst, send_sem=send_sems.at[s], recv_sem=recv_sems.at[s],
            device_id=(px, py, mc), device_id_type=MESH)

    def start():
        for a in range(n):
            local(a).start()
            for j in range(1, 4):
                copy(a, j, True).start()

    def wait():
        for a in range(n):
            for j in range(1, 4):
                copy(a, j, False).wait()
            local(a).wait()

    return start, wait


def _call(body, *, name, grid, in_specs, out_specs, out_shape, args, scratch_shapes=(), sem=None, ride=None):
    if not ride:
        return pl.pallas_call(
            body, name=name, grid=grid, in_specs=list(in_specs), out_specs=list(out_specs), out_shape=list(out_shape),
            scratch_shapes=list(scratch_shapes), compiler_params=_params(sem))(*args)
    xs = [x for r in ride for x in r["xs"]]
    shapes = [s for r in ride for s in r["shapes"]]
    sems = [s for r in ride for s in r["sems"]]
    n_in, n_out, n_scr, nx = len(in_specs), len(out_specs), len(scratch_shapes), len(xs)

    def wrapped(*refs):
        ins, x_refs = refs[:n_in], refs[n_in:n_in + nx]
        outs = refs[n_in + nx:n_in + nx + n_out]
        lands = refs[n_in + nx + n_out:n_in + 2 * nx + n_out]
        rest = refs[n_in + 2 * nx + n_out:]
        scr, sem_refs = rest[:n_scr], rest[n_scr:]
        ops, xo, so = [], 0, 0
        for r in ride:
            nr, ns = len(r["xs"]), len(r["sems"])
            ops.append(r["ops"](x_refs[xo:xo + nr], lands[xo:xo + nr], sem_refs[so:so + ns]))
            xo, so = xo + nr, so + ns
        ids = [pl.program_id(a) for a in range(len(grid))]
        first = functools.reduce(jnp.logical_and, [i == 0 for i in ids])
        last = functools.reduce(jnp.logical_and, [i == g - 1 for i, g in zip(ids, grid)])

        @pl.when(first)
        def _():
            for start, _ in ops:
                start()

        body(*ins, *outs, *scr)

        @pl.when(last)
        def _():
            for _, wait in ops:
                wait()

    return pl.pallas_call(
        wrapped, name=name, grid=grid, in_specs=list(in_specs) + [ANY] * nx, out_specs=list(out_specs) + [ANY] * nx,
        out_shape=list(out_shape) + shapes, scratch_shapes=list(scratch_shapes) + sems,
        compiler_params=_params(("arbitrary",) * len(grid)))(*args, *xs)


def _ride_sibling(xs):
    n = len(xs)

    def ops(x_refs, o_refs, sems):
        send_sems, recv_sems = sems
        sib = (lax.axis_index("x"), lax.axis_index("y"), 1 - lax.axis_index("c"))

        def copies():
            return [pltpu.make_async_remote_copy(
                src_ref=x_refs[a], dst_ref=o_refs[a], send_sem=send_sems.at[a], recv_sem=recv_sems.at[a],
                device_id=sib, device_id_type=MESH) for a in range(n)]

        def start():
            for cp in copies():
                cp.start()

        def wait():
            for cp in copies():
                cp.wait()

        return start, wait

    return dict(xs=list(xs), shapes=[jax.ShapeDtypeStruct(x.shape, x.dtype) for x in xs],
                sems=[pltpu.SemaphoreType.DMA((n,)), pltpu.SemaphoreType.DMA((n,))], ops=ops)


PACK = 1024
PACK_ROWS = 512


def _pack(parts, pad_rows=PACK_ROWS):
    flat = []
    for p in parts:
        v = p.reshape(-1).astype(F32)
        flat.append(jnp.pad(v, (0, (-v.shape[0]) % PACK)))
    total = sum(v.shape[0] for v in flat)
    flat.append(jnp.zeros(((-total) % (pad_rows * 128),), F32))
    return jnp.concatenate(flat).reshape(-1, 128)


def _shard_columns(shards, lo, hi):
    width = shards.shape[2]
    out = []
    for k in range(shards.shape[0]):
        a, b = max(lo, k * width), min(hi, (k + 1) * width)
        if a < b:
            out.append(shards[k, :, a - k * width:b - k * width])
    return out


def _unpack_rows(gathered, shapes):
    flat = gathered.reshape(gathered.shape[0], -1)
    out, off = [], 0
    for shp in shapes:
        n = math.prod(shp)
        out.append(flat[:, off:off + n].reshape((flat.shape[0],) + tuple(shp)))
        off += n + (-n) % PACK
    return out


def _unpack(packed, shapes):
    flat = packed.reshape(-1)
    out, off = [], 0
    for shp in shapes:
        n = math.prod(shp)
        out.append(flat[off:off + n].reshape(shp))
        off += n + (-n) % PACK
    return out


def kernel(x, c, ln_pre_g, ln_post_g, w_mod, b_mod, w_in_ab, w_out_ab, sgu_norm_g, sgu_w, sgu_b, w_in_ssm, w_out_ssm, lam_re, lam_im, b_re, b_im, c_re, c_im, d_skip, log_dt, w_glu, b_glu, loss_target, m_ln_pre_g, m_ln_post_g, m_w_mod, m_b_mod, m_w_in_ab, m_w_out_ab, m_sgu_norm_g, m_sgu_w, m_sgu_b, m_w_in_ssm, m_w_out_ssm, m_lam_re, m_lam_im, m_b_re, m_b_im, m_c_re, m_c_im, m_d_skip, m_log_dt, m_w_glu, m_b_glu, v_ln_pre_g, v_ln_post_g, v_w_mod, v_b_mod, v_w_in_ab, v_w_out_ab, v_sgu_norm_g, v_sgu_w, v_sgu_b, v_w_in_ssm, v_w_out_ssm, v_lam_re, v_lam_im, v_b_re, v_b_im, v_c_re, v_c_im, v_d_skip, v_log_dt, v_w_glu, v_b_glu):
    given = dict(locals())
    mx, my, mc = lax.axis_index("x"), lax.axis_index("y"), lax.axis_index("c")
    me = 4 * mx + 2 * my + mc
    chip = 2 * mx + my

    _, l, d = x.shape
    x2, tgt = x[0], loss_target[0]
    n_in = w_in_ab.shape[2] * 4
    wa = wb = n_in // 7
    w = w_out_ssm.shape[1]
    g, p, cch = b_re.shape[1:]
    nmod = w_mod.shape[2]


    cond = _silu_rows(c.reshape(d // 128, 128), "cond_silu")
    cond_all = _allgather8(cond, "gather_cond").reshape(8, d)
    b_shard = lax.dynamic_slice(b_mod, (0, chip * nmod), (2, nmod)).reshape(2, 1, nmod)
    cond_pad = jnp.pad(cond_all, ((0, MOD_ROWS - 8), (0, 0)))
    modp = _mod_fwd(cond_pad, w_mod, b_shard, "mod_fwd")[:, :8]
    modp_all = _allgather8(modp.reshape(16, nmod), "gather_mod").reshape(4, 2, 2, 8, nmod)
    mine = lax.dynamic_index_in_dim(lax.dynamic_index_in_dim(modp_all, mc, 1, False), me, 2, False)
    mod = mine.transpose(1, 0, 2).reshape(2, 3 * d)
    shift = [mod[a:a + 1, :d] for a in range(2)]
    scale = [mod[a:a + 1, d:2 * d] for a in range(2)]
    gate = [mod[a:a + 1, 2 * d:] for a in range(2)]
    pre_g = [ln_pre_g[a:a + 1] for a in range(2)]
    post_g = [ln_post_g[a:a + 1] for a in range(2)]

    sgu_w0, sgu_bt = sgu_w[0], sgu_b[0].T
    h0, (gw_in_ab,) = _pre_fwd(x2, pre_g[0], scale[0], shift[0], "pre0_fwd", ride=[_ride_halves(_bf(w_in_ab[0]))])
    w_gates = jnp.concatenate(_shard_columns(gw_in_ab, 0, 3 * wa) + _shard_columns(gw_in_ab, 3 * wa + 3 * wb, n_in),
                              axis=1)
    w_qkv = jnp.concatenate(_shard_columns(gw_in_ab, 3 * wa, 3 * wa + 3 * wb), axis=1)
    proj0, (gw_in_ssm,) = _matmul(h0, w_gates, "nn", BF16, "proj0", tm=1024, ride=[_ride_chip([_bf(w_in_ssm[0])], True)])
    qkv, (gw_out_ssm, gw_glu, g_dskip, g_bglu) = _matmul(
        h0, w_qkv, "nn", BF16, "proj0_qkv", tm=1024,
        ride=[_ride_chip([_bf(w_out_ssm[0]), _bf(w_glu[0]), d_skip, b_glu], True)])
    out_b, (gw_out_ab,) = _attn_fwd(qkv, wb, "attn_fwd", hp=8, ride=[_ride_chip([_bf(w_out_ab[0])], True)])
    wout_ab = gw_out_ab.reshape(wa + wb, d)
    win_ssm = gw_in_ssm.reshape(d, 2 * w)
    wout_ssm = jnp.concatenate([gw_out_ssm[k] for k in range(4)], axis=1)
    wglu = gw_glu.reshape(w, w)
    dskip_full = g_dskip.reshape(1, w)
    bglu_full = g_bglu.reshape(1, w)
    cat =_sgu_fwd(proj0, out_b, sgu_norm_g, sgu_w0, sgu_bt, wa, wb, "sgu_fwd")
    y0 = _matmul(cat, wout_ab, "nn", BF16, "out0", tm=1024)
    x1, h1 = _post_pre_fwd(x2, y0, gate[0], post_g[0], pre_g[1], scale[1], shift[1], "post0_pre1_fwd")

    s = g * p
    lr_c, li_c = lam_re.reshape(s, 1), lam_im.reshape(s, 1)
    ldt_c = jnp.repeat(log_dt.reshape(g), p).reshape(s, 1)
    br_c, bi_c = b_re.reshape(s, cch), b_im.reshape(s, cch)
    bb_re, bb_im, pw_re, pw_im = _ssm_prep(lr_c, li_c, ldt_c, br_c, bi_c, lr_c.reshape(1, s), li_c.reshape(1, s),
                                           ldt_c.reshape(1, s), "ssm_prep")
    bbd = _bf(_block_diag_b(bb_re, bb_im, g, p, cch))
    ccd = _bf(_block_diag_c(c_re[0], c_im[0], g, p, cch))
    proj1 = _matmul(h1, win_ssm, "nn", F32, "proj1", tm=1024)
    y_ssm, hs_re, hs_im, h_all = _ssm_fwd(proj1, bbd, ccd, pw_re, pw_im, dskip_full, w, "ssm_fwd")
    o1 = _glu_fwd(y_ssm, proj1, wglu, bglu_full, "glu_fwd")
    y1 = _matmul(o1, wout_ssm, "nn", BF16, "out1", tm=1024)
    loss_vec, dy1, dx2, dgate1, dpost1 = _post_loss(x1, y1, gate[1], post_g[1], tgt, "post1_loss")

    do1 = _matmul(dy1, wout_ssm, "nt", BF16, "out1_dx", tm=1024)
    gr_wout_ssm = _matmul_at(o1, dy1, BF16, "out1_dw", n_split=4)
    dy_ssm, dz1, gr_wglu, gr_bglu = _glu_bwd(do1, y_ssm, proj1, wglu, bglu_full, "glu_bwd")
    du1, dbbd, dccd, da_re, da_im, gr_dskip, (ld_wout_ssm, ld_wglu) = _ssm_bwd(
        proj1, dy_ssm, hs_re, hs_im, h_all, bbd, ccd, pw_re, pw_im, dskip_full, w, "ssm_bwd",
        ride=[_ride_chip([gr_wout_ssm, gr_wglu.reshape(4, w // 4, w)], False)])
    dproj1 = jnp.concatenate([du1, dz1], axis=1)
    dh1 = _matmul(dproj1, win_ssm, "nt", BF16, "proj1_dx", tm=1024)
    gr_win_ssm = _matmul_at(h1, dproj1, BF16, "proj1_dw", tn=1024)
    dx1, dscale1, dshift1, dpre1, dy0, dgate0, dpost0 = _pre_bwd(
        dh1, dx2, x1, pre_g[1], scale[1], "pre1_post0_bwd", post=(y0, gate[0], post_g[0]))

    dcat = _matmul(dy0, wout_ab, "nt", BF16, "out0_dx", tm=1024)
    gr_wout_ab = _matmul_at(cat, dy0, BF16, "out0_dw", tn=1024)
    dbb_re, dbb_im = _diag_of_b(dbbd, g, p, cch)
    dc_re, dc_im = _diag_of_c(dccd, g, p, cch)
    part_a = [loss_vec[:, :1], dpre1, dpost0, dpost1, dgate0, dshift1, dscale1, dgate1, da_re, da_im,
              dbb_re, dbb_im, dc_re, dc_im, gr_dskip, gr_bglu]
    shapes_a = [a.shape for a in part_a]
    dq, dk, dv, (ld_win_ssm, ld_wout_ab, gath_a) = _attn_bwd(
        qkv, proj0, dcat, wa, wb, "attn_bwd", hp=4,
        ride=[_ride_chip([gr_win_ssm.reshape(4, d // 4, 2 * w), gr_wout_ab.reshape(4, (wa + wb) // 4, d)], False),
              _ride_all8(_pack(part_a))])
    early_names = ["w_out_ab", "w_in_ssm", "w_out_ssm", "w_glu"]
    early_sums = [_sum_leading(a, "sum_" + nm) for a, nm in zip([ld_wout_ab, ld_win_ssm, ld_wout_ssm, ld_wglu], early_names)]
    dproj0, gr_sgu_w, gr_sgu_bt, gr_sgu_g, early_sib = _sgu_bwd(
        proj0, out_b, dcat, dq, dk, dv, sgu_norm_g, sgu_w0, sgu_bt, wa, wb, "sgu_bwd", ride=[_ride_sibling(early_sums)])
    part_b = [gr_sgu_g, gr_sgu_w, gr_sgu_bt.T]
    shapes_b = [a.shape for a in part_b]
    gr_win_ab_lo, (gath_b,) = _matmul_at(h0, dproj0, BF16, "proj0_dw_lo", tm=512, tn=n_in // 4, n_split=4,
                                         m_part=(0, 1, 2), ride=[_ride_all8(_pack(part_b))])
    gr_win_ab_hi, (ld_win_ab_lo,) = _matmul_at(
        h0, dproj0, BF16, "proj0_dw_hi", tm=512, tn=n_in // 4, n_split=4, m_part=(1, 1, 2),
        ride=[_ride_chip([gr_win_ab_lo], False)])
    dh0, (ld_win_ab_hi,) = _matmul(dproj0, gw_in_ab, "nt", BF16, "proj0_dx", tm=1024, tn=1024,
                                   ride=[_ride_chip([gr_win_ab_hi], False)])
    grad_x, dscale0, dshift0, dpre0 = _pre_bwd(dh0, dx1, x2, pre_g[0], scale[0], "pre0_bwd")
    part_c = [dpre0, dshift0, dscale0]
    shapes_c = [a.shape for a in part_c]

    big_names = ["w_in_ab"] + early_names
    sum_win_ab = jnp.concatenate([_sum_leading(ld_win_ab_lo, "sum_w_in_ab_lo"),
                                  _sum_leading(ld_win_ab_hi, "sum_w_in_ab_hi")], axis=0)
    sums = [sum_win_ab] + early_sums
    gath_c, sib_win_ab = _exchange([_ride_all8(_pack(part_c, pad_rows=8)), _ride_sibling([sum_win_ab])], "tail_exchange")
    sib = [sib_win_ab] + list(early_sib)
    results = {}
    for nm, s_mine, s_sib in zip(big_names, sums, sib):
        shp = given[nm].shape
        two_d = lambda a: a.reshape(-1, shp[-1])
        outs = _adamw(two_d(given[nm]), [s_mine, s_sib], two_d(given["m_" + nm]), two_d(given["v_" + nm]),
                      "adamw_" + nm)
        results[nm] = [o.reshape(shp) for o in outs]

    (loss_s, g_pre1, g_post0, g_post1, g_gate0, g_shift1, g_scale1, g_gate1, s_da_re, s_da_im, s_dbb_re, s_dbb_im,
     g_c_re, g_c_im, g_dskip_full, g_bglu_full) = _unpack(_sum_leading(gath_a, "sum_small_a"), shapes_a)
    g_sgu_g, g_sgu_w, g_sgu_b = _unpack(_sum_leading(gath_b, "sum_small_b"), shapes_b)
    g_pre0, g_shift0, g_scale0 = _unpack(_sum_leading(gath_c, "sum_small_c"), shapes_c)
    loss = loss_s.reshape(())
    g_pre = jnp.concatenate([g_pre0, g_pre1], axis=0)
    g_post = jnp.concatenate([g_post0, g_post1], axis=0)
    g_bmod = jnp.concatenate([jnp.concatenate([g_shift0, g_scale0, g_gate0], axis=1),
                              jnp.concatenate([g_shift1, g_scale1, g_gate1], axis=1)], axis=0)

    g_lr, g_li, g_ldt, g_br, g_bi = _ssm_prep_bwd(lr_c, li_c, ldt_c, br_c, bi_c, s_da_re.reshape(s, 1),
                                                  s_da_im.reshape(s, 1), s_dbb_re, s_dbb_im, p, "ssm_prep_bwd")
    small = {
        "ln_pre_g": g_pre, "ln_post_g": g_post, "b_mod": g_bmod, "sgu_norm_g": g_sgu_g,
        "sgu_w": g_sgu_w.reshape(sgu_w.shape), "sgu_b": g_sgu_b.reshape(sgu_b.shape),
        "lam_re": g_lr.reshape(lam_re.shape), "lam_im": g_li.reshape(lam_im.shape),
        "b_re": g_br.reshape(b_re.shape), "b_im": g_bi.reshape(b_im.shape),
        "c_re": g_c_re.reshape(c_re.shape), "c_im": g_c_im.reshape(c_im.shape),
        "d_skip": lax.dynamic_slice(g_dskip_full, (0, chip * (w // 4)), (1, w // 4)),
        "log_dt": g_ldt.reshape(log_dt.shape),
        "b_glu": lax.dynamic_slice(g_bglu_full, (0, chip * (w // 4)), (1, w // 4)),
    }
    flat2 = lambda a: a.reshape(-1, a.shape[-1])
    wide = ("b_re", "b_im")
    for tag, group in (("adamw_small", [nm for nm in small if nm not in wide]), ("adamw_small_b", list(wide))):
        outs = _adamw_many([flat2(given[nm]) for nm in group], [flat2(small[nm]) for nm in group],
                           [flat2(given["m_" + nm]) for nm in group], [flat2(given["v_" + nm]) for nm in group], tag)
        for nm, trio in zip(group, outs):
            results[nm] = [small[nm]] + [o.reshape(given[nm].shape) for o in trio]

    rows_a = _unpack_rows(gath_a, shapes_a)
    rows_c = _unpack_rows(gath_c, shapes_c)
    dmod_rows = jnp.concatenate([rows_c[1], rows_c[2], rows_a[4], rows_a[5], rows_a[6], rows_a[7]],
                                axis=2).reshape(8, 2, 3 * d)
    dmod_shard = lax.dynamic_slice(dmod_rows, (0, 0, chip * nmod), (8, 2, nmod)).transpose(1, 0, 2)
    dmod_pad = jnp.pad(dmod_shard, ((0, 0), (0, MOD_ROWS - 8), (0, 0)))
    gr_wmod = _mod_bwd(cond_pad.T, dmod_pad, "mod_bwd")
    two_d = lambda a: a.reshape(-1, nmod)
    outs = _adamw(two_d(w_mod), [two_d(gr_wmod)], two_d(m_w_mod), two_d(v_w_mod), "adamw_w_mod")
    results["w_mod"] = [o.reshape(w_mod.shape) for o in outs]

    names = ["ln_pre_g", "ln_post_g", "w_mod", "b_mod", "w_in_ab", "w_out_ab", "sgu_norm_g", "sgu_w", "sgu_b",
             "w_in_ssm", "w_out_ssm", "lam_re", "lam_im", "b_re", "b_im", "c_re", "c_im", "d_skip", "log_dt",
             "w_glu", "b_glu"]
    return (loss, grad_x[None], *[results[nm][0] for nm in names], *[results[nm][1] for nm in names],
            *[results[nm][2] for nm in names], *[results[nm][3] for nm in names])
```

```python
import functools
import math

import jax
import jax.numpy as jnp
from jax import lax
from jax.experimental import pallas as pl
from jax.experimental.pallas import tpu as pltpu

F32 = jnp.float32
BF16 = jnp.bfloat16
MESH = pl.DeviceIdType.MESH

EPS = 1e-6
HEAD = 128
SSM_T = 512
SSM_GB = 16
ADAM_LR, ADAM_B1, ADAM_B2, ADAM_EPS, ADAM_WD, ADAM_STEP = 0.001, 0.9, 0.999, 1e-08, 0.01, 10
VMEM_LIMIT = 56 * 1024 * 1024

NN = (((1,), (0,)), ((), ()))
NT = (((1,), (1,)), ((), ()))
TN = (((0,), (0,)), ((), ()))


def _params(sem=None):
    return pltpu.CompilerParams(dimension_semantics=sem, vmem_limit_bytes=VMEM_LIMIT)


def _dot(a, b, dims=NN):
    return lax.dot_general(a, b, dims, preferred_element_type=F32)


def _bf(x):
    return x.astype(BF16)


def _gelu(x):
    k = math.sqrt(2.0 / math.pi)
    t = jnp.tanh(k * (x + 0.044715 * x * x * x))
    return 0.5 * x * (1.0 + t)


def _gelu_grad(x):
    k = math.sqrt(2.0 / math.pi)
    x2 = x * x
    t = jnp.tanh(k * (x + 0.044715 * x * x2))
    return 0.5 * (1.0 + t) + 0.5 * x * (1.0 - t * t) * k * (1.0 + 3.0 * 0.044715 * x2)


def _sigmoid(x):
    return 1.0 / (1.0 + jnp.exp(-x))


def _silu(x):
    return x * _sigmoid(x)


def _silu_grad(x):
    s = _sigmoid(x)
    return s * (1.0 + x * (1.0 - s))


def _tile(n, t, mult=128):
    if n <= t:
        return n
    for cand in range(t - t % mult, 0, -mult):
        if n % cand == 0:
            return cand
    raise ValueError((n, t, mult))


def _matmul(a, b, mode, out_dtype, name, tm=512, tn=512, tk=2048, n_split=1, ride=None, m_part=None):
    b_sharded = b.ndim == 3
    if mode == "nn":
        (m, kk), (_, n) = a.shape, b.shape
    elif b_sharded:
        assert mode == "nt"
        (m, kk), n, tk = a.shape, b.shape[1], b.shape[2]
    elif mode == "nt":
        (m, kk), (n, _) = a.shape, b.shape
    else:
        (kk, m), (_, n) = a.shape, b.shape
    m_off = 0
    if m_part is not None:
        assert mode == "tn"
        first, count, parts = m_part
        tm = _tile(m // parts, tm)
        m_off = first * (m // parts) // tm
        m = count * (m // parts)
    tm, tk = _tile(m, tm), _tile(kk, tk)
    ns = n // n_split
    tn = _tile(ns, tn)
    nk = kk // tk
    dims = {"nn": NN, "nt": NT, "tn": TN}[mode]

    def body(a_ref, b_ref, o_ref, acc_ref):
        k = pl.program_id(2)
        part = _dot(_bf(a_ref[...]), _bf(b_ref[0] if b_sharded else b_ref[...]), dims)

        @pl.when(k == 0)
        def _():
            acc_ref[...] = part

        @pl.when(k > 0)
        def _():
            acc_ref[...] += part

        @pl.when(k == nk - 1)
        def _():
            o_ref[...] = acc_ref[...].astype(out_dtype).reshape(o_ref.shape)

    if mode == "nn":
        a_spec = pl.BlockSpec((tm, tk), lambda i, j, k: (i, k))
        b_spec = pl.BlockSpec((tk, tn), lambda i, j, k: (k, j))
    elif mode == "nt":
        a_spec = pl.BlockSpec((tm, tk), lambda i, j, k: (i, k))
        b_spec = (pl.BlockSpec((1, tn, tk), lambda i, j, k: (k, j, 0)) if b_sharded
                  else pl.BlockSpec((tn, tk), lambda i, j, k: (j, k)))
    else:
        a_spec = pl.BlockSpec((tk, tm), lambda i, j, k: (k, i + m_off))
        b_spec = pl.BlockSpec((tk, tn), lambda i, j, k: (k, j))
    if n_split == 1:
        out_shape = jax.ShapeDtypeStruct((m, n), out_dtype)
        o_spec = pl.BlockSpec((tm, tn), lambda i, j, k: (i, j))
    else:
        per = ns // tn
        out_shape = jax.ShapeDtypeStruct((n_split, m, ns), out_dtype)
        o_spec = pl.BlockSpec((1, tm, tn), lambda i, j, k: (j // per, i, j % per))
    outs = _call(body, name=name, grid=(m // tm, n // tn, nk), in_specs=[a_spec, b_spec], out_specs=[o_spec],
                 out_shape=[out_shape], scratch_shapes=[pltpu.VMEM((tm, tn), F32)], args=(a, b),
                 sem=("parallel", "parallel", "arbitrary"), ride=ride)
    return outs[0] if ride is None else (outs[0], outs[1:])


def _matmul_at(a, b, out_dtype, name, tm=1024, tn=512, n_split=1, ride=None, m_part=None):
    (kk, m), (_, n) = a.shape, b.shape
    m_off = 0
    if m_part is not None:
        first, count, parts = m_part
        tm = _tile(m // parts, tm)
        m_off = first * (m // parts) // tm
        m = count * (m // parts)
    tm = _tile(m, tm)
    ns = n // n_split
    tn = _tile(ns, tn)
    kc = _tile(kk, 512)

    def body(a_ref, b_ref, o_ref, at_ref):
        @pl.when(pl.program_id(1) == 0)
        def _():
            for c in range(kk // kc):
                at_ref[:, c * kc:(c + 1) * kc] = _bf(a_ref[c * kc:(c + 1) * kc, :].astype(F32).T)

        o_ref[...] = _dot(at_ref[...], _bf(b_ref[...])).astype(out_dtype).reshape(o_ref.shape)

    if n_split == 1:
        out_shape = jax.ShapeDtypeStruct((m, n), out_dtype)
        o_spec = pl.BlockSpec((tm, tn), lambda i, j: (i, j))
    else:
        per = ns // tn
        out_shape = jax.ShapeDtypeStruct((n_split, m, ns), out_dtype)
        o_spec = pl.BlockSpec((1, tm, tn), lambda i, j: (j // per, i, j % per))
    outs = _call(body, name=name, grid=(m // tm, n // tn),
                 in_specs=[pl.BlockSpec((kk, tm), lambda i, j: (0, i + m_off)), pl.BlockSpec((kk, tn), lambda i, j: (0, j))],
                 out_specs=[o_spec], out_shape=[out_shape], scratch_shapes=[pltpu.VMEM((tm, kk), BF16)], args=(a, b),
                 sem=("arbitrary", "arbitrary"), ride=ride)
    return outs[0] if ride is None else (outs[0], outs[1:])


def _row_spec(tm, d):
    return pl.BlockSpec((tm, d), lambda i: (i, 0))


def _vec_spec(d):
    return pl.BlockSpec((1, d), lambda i: (0, 0))


def _acc(ref, first, val):
    @pl.when(first)
    def _():
        ref[...] = val

    @pl.when(jnp.logical_not(first))
    def _():
        ref[...] += val


def _colsum(x):
    return jnp.sum(x, axis=0, keepdims=True)


def _rownorm(x):
    r = lax.rsqrt(jnp.mean(x * x, axis=-1, keepdims=True) + EPS)
    return x * r, r


STRIP = 64


def _fold8(x):
    return functools.reduce(lambda a, b: a + b, [x[8 * k:8 * k + 8] for k in range(x.shape[0] // 8)])


def _pre_fwd(x, g, scale, shift, name, ride=None):
    l, d = x.shape
    tm = _tile(l, 256)

    def body(x_ref, g_ref, sc_ref, sh_ref, h_ref):
        n, _ = _rownorm(x_ref[...])
        h_ref[...] = _bf(n * g_ref[...] * (1.0 + sc_ref[...]) + sh_ref[...])

    outs = _call(body, name=name, grid=(l // tm,), in_specs=[_row_spec(tm, d), _vec_spec(d), _vec_spec(d), _vec_spec(d)],
                 out_specs=[_row_spec(tm, d)], out_shape=[jax.ShapeDtypeStruct((l, d), BF16)],
                 args=(x, g, scale, shift), sem=("parallel",), ride=ride)
    return outs[0], outs[1:]


def _post_pre_fwd(x, y, gate, pg, g1, scale1, shift1, name):
    l, d = x.shape
    tm = _tile(l, 256)

    def body(x_ref, y_ref, gate_ref, pg_ref, g1_ref, sc_ref, sh_ref, x1_ref, h1_ref):
        @pl.loop(0, tm // STRIP)
        def _(s):
            rows = pl.ds(pl.multiple_of(s * STRIP, STRIP), STRIP)
            ny, _ = _rownorm(y_ref[rows, :].astype(F32))
            x1 = x_ref[rows, :] + gate_ref[...] * (ny * pg_ref[...])
            x1_ref[rows, :] = x1
            n1, _ = _rownorm(x1)
            h1_ref[rows, :] = _bf(n1 * g1_ref[...] * (1.0 + sc_ref[...]) + sh_ref[...])

    v = _vec_spec(d)
    return pl.pallas_call(
        body, name=name, grid=(l // tm,),
        in_specs=[_row_spec(tm, d), _row_spec(tm, d), v, v, v, v, v],
        out_specs=[_row_spec(tm, d), _row_spec(tm, d)],
        out_shape=[jax.ShapeDtypeStruct((l, d), F32), jax.ShapeDtypeStruct((l, d), BF16)],
        compiler_params=_params(("parallel",)),
    )(x, y, gate, pg, g1, scale1, shift1)


def _post_loss(x1, y1, gate, pg, target, name):
    l, d = x1.shape
    tm = _tile(l, 256)

    def body(x_ref, y_ref, gate_ref, pg_ref, t_ref, loss_ref, dy_ref, dx_ref, dgate_ref, dpg_ref):
        first = pl.program_id(0) == 0

        def strip(s, sums):
            rows = pl.ds(pl.multiple_of(s * STRIP, STRIP), STRIP)
            ny, ry = _rownorm(y_ref[rows, :].astype(F32))
            q = ny * pg_ref[...]
            e = x_ref[rows, :] + gate_ref[...] * q - t_ref[rows, :]
            dx2 = e * (1.0 / d)
            dx_ref[rows, :] = dx2
            dq = dx2 * gate_ref[...]
            dny = dq * pg_ref[...]
            dy_ref[rows, :] = _bf(ry * (dny - ny * jnp.mean(dny * ny, axis=-1, keepdims=True)))
            return sums[0] + _fold8(e * e), sums[1] + _fold8(dx2 * q), sums[2] + _fold8(dq * ny)

        zero = jnp.zeros((8, d), F32)
        sq, dgate, dpg = lax.fori_loop(0, tm // STRIP, strip, (zero, zero, zero))
        _acc(loss_ref, first, jnp.full((1, 128), 0.5 / d, F32) * jnp.sum(sq))
        _acc(dgate_ref, first, _colsum(dgate))
        _acc(dpg_ref, first, _colsum(dpg))

    v = _vec_spec(d)
    return pl.pallas_call(
        body, name=name, grid=(l // tm,),
        in_specs=[_row_spec(tm, d), _row_spec(tm, d), v, v, _row_spec(tm, d)],
        out_specs=[_vec_spec(128), _row_spec(tm, d), _row_spec(tm, d), v, v],
        out_shape=[jax.ShapeDtypeStruct((1, 128), F32), jax.ShapeDtypeStruct((l, d), BF16),
                   jax.ShapeDtypeStruct((l, d), F32), jax.ShapeDtypeStruct((1, d), F32),
                   jax.ShapeDtypeStruct((1, d), F32)],
        compiler_params=_params(("arbitrary",)),
    )(x1, y1, gate, pg, target)


def _pre_bwd(dh, dres, x, g, scale, name, post=None):
    l, d = x.shape
    tm = _tile(l, 256)
    with_post = post is not None

    def body(*refs):
        if with_post:
            (dh_ref, dres_ref, x_ref, g_ref, sc_ref, y_ref, gate_ref, pg_ref,
             dx_ref, dsc_ref, dsh_ref, dg_ref, dy_ref, dgate_ref, dpg_ref) = refs
        else:
            dh_ref, dres_ref, x_ref, g_ref, sc_ref, dx_ref, dsc_ref, dsh_ref, dg_ref = refs
        first = pl.program_id(0) == 0

        def strip(s, sums):
            rows = pl.ds(pl.multiple_of(s * STRIP, STRIP), STRIP)
            dh = dh_ref[rows, :].astype(F32)
            n, r = _rownorm(x_ref[rows, :])
            dyn = dh * (1.0 + sc_ref[...])
            dn = dyn * g_ref[...]
            dx = dres_ref[rows, :] + r * (dn - n * jnp.mean(dn * n, axis=-1, keepdims=True))
            dx_ref[rows, :] = dx
            new = [sums[0] + _fold8(dh * (n * g_ref[...])), sums[1] + _fold8(dh), sums[2] + _fold8(dyn * n)]
            if with_post:
                ny, ry = _rownorm(y_ref[rows, :].astype(F32))
                dq = dx * gate_ref[...]
                dny = dq * pg_ref[...]
                dy_ref[rows, :] = _bf(ry * (dny - ny * jnp.mean(dny * ny, axis=-1, keepdims=True)))
                new += [sums[3] + _fold8(dx * (ny * pg_ref[...])), sums[4] + _fold8(dq * ny)]
            return tuple(new)

        zero = jnp.zeros((8, d), F32)
        sums = lax.fori_loop(0, tm // STRIP, strip, (zero,) * (5 if with_post else 3))
        outs = [dsc_ref, dsh_ref, dg_ref] + ([dgate_ref, dpg_ref] if with_post else [])
        for ref, acc in zip(outs, sums):
            _acc(ref, first, _colsum(acc))

    v = _vec_spec(d)
    row = _row_spec(tm, d)
    vec_out = jax.ShapeDtypeStruct((1, d), F32)
    in_specs = [row, row, row, v, v]
    args = [dh, dres, x, g, scale]
    out_specs = [row, v, v, v]
    out_shape = [jax.ShapeDtypeStruct((l, d), F32), vec_out, vec_out, vec_out]
    if with_post:
        in_specs += [row, v, v]
        args += list(post)
        out_specs += [row, v, v]
        out_shape += [jax.ShapeDtypeStruct((l, d), BF16), vec_out, vec_out]
    return pl.pallas_call(
        body, name=name, grid=(l // tm,), in_specs=in_specs, out_specs=out_specs, out_shape=out_shape,
        compiler_params=_params(("arbitrary",)),
    )(*args)


def _softplus_parts(z):
    e = jnp.exp(-jnp.abs(z))
    den = 1.0 + e
    lb = jnp.minimum(z, 0.0) - jnp.log(den)
    return lb, lb - z, jnp.exp(lb)


def _tri(cmp, n=HEAD):
    row = lax.broadcasted_iota(jnp.int32, (n, n), 0)
    col = lax.broadcasted_iota(jnp.int32, (n, n), 1)
    return cmp(row, col)


ATT_T = 256
ATT_DEAD = 104.0


def _any_alive(runs):
    return functools.reduce(jnp.maximum, [jnp.max(r) for r in runs]) > -ATT_DEAD


def _attn_fwd(qkv, wb, name, hp=4, ride=None):
    l = qkv.shape[0]
    t = ATT_T
    nh, nq = wb // HEAD, l // t
    hp = min(hp, nh)
    ng, wg = nh // hp, hp * HEAD
    scale = 1.0 / math.sqrt(HEAD)

    def body(q_ref, k_ref, v_ref, o_ref):
        i = pl.program_id(1)
        valid = _tri(lambda r, c: c < r, t)
        m_gt = _bf(_tri(lambda r, c: r > c, t).astype(F32))

        def tile(j, carry, diag):
            rows = pl.ds(pl.multiple_of(j * t, t), t)
            cols = [slice(hh * HEAD, (hh + 1) * HEAD) for hh in range(hp)]
            zs = [_dot(q_ref[:, cs], k_ref[rows, cs], NT) * scale for cs in cols]
            lbs, lks = [], []
            for z in zs:
                lb, lk, _ = _softplus_parts(z)
                lbs.append(lb)
                lks.append(jnp.where(valid, lk, 0.0) if diag else lk)
            laters = [_dot(_bf(lk), m_gt) for lk in lks]
            ws = [jnp.exp(lb + later + run) for lb, later, (_, run) in zip(lbs, laters, carry)]
            if diag:
                ws = [jnp.where(valid, w, 0.0) for w in ws]
            return tuple((acc + _dot(_bf(w), v_ref[rows, cs]), run + jnp.sum(lk, axis=1, keepdims=True))
                         for w, lk, cs, (acc, run) in zip(ws, lks, cols, carry))

        zero = (jnp.zeros((t, HEAD), F32), jnp.zeros((t, 1), F32))
        carry = tile(i, (zero,) * hp, True)
        _, carry = lax.while_loop(lambda c: (c[0] < i) & _any_alive([run for _, run in c[1]]),
                                  lambda c: (c[0] + 1, tile(i - 1 - c[0], c[1], False)), (jnp.int32(0), carry))
        for hh, (acc, _) in enumerate(carry):
            o_ref[:, hh * HEAD:(hh + 1) * HEAD] = acc

    blk = lambda off: pl.BlockSpec((t, wg), lambda h, i: (i, off + h))
    full = lambda off: pl.BlockSpec((l, wg), lambda h, i: (0, off + h))
    out = pl.BlockSpec((t, wg), lambda h, i: (i, h))
    outs = _call(body, name=name, grid=(ng, nq), in_specs=[blk(0), full(ng), full(2 * ng)], out_specs=[out],
                 out_shape=[jax.ShapeDtypeStruct((l, wb), F32)],
                 args=(qkv, qkv, qkv), sem=("parallel", "arbitrary"), ride=ride)
    return outs[0], outs[1:]


def _attn_bwd(qkv, proj, dcat, wa, wb, name, hp=2, ride=None):
    l = qkv.shape[0]
    t = ATT_T
    nh, nq = wb // HEAD, l // t
    hp = min(hp, nh)
    ng, wg = nh // hp, hp * HEAD
    scale = 1.0 / math.sqrt(HEAD)

    def body(q_ref, k_ref, v_ref, bz_ref, dc_ref, dq_ref, dkt_out, dvt_out, do_s, qt_s, dot_s,
             dkt_ref, dvt_ref, out_sems):
        i = pl.program_id(1)

        @pl.when(i == 0)
        def _():
            dkt_ref[...] = jnp.zeros_like(dkt_ref)
            dvt_ref[...] = jnp.zeros_like(dvt_ref)

        do = dc_ref[...].astype(F32) * _silu(bz_ref[...].astype(F32))
        do_s[...] = _bf(do)
        for hh in range(hp):
            cs = slice(hh * HEAD, (hh + 1) * HEAD)
            qt_s[hh] = _bf(q_ref[:, cs].astype(F32).T * scale)
            dot_s[hh] = _bf(do[:, cs].T)
        valid = _tri(lambda r, c: c < r, t)
        m_le = _bf(_tri(lambda r, c: r <= c, t).astype(F32))
        m_lt = _bf(_tri(lambda r, c: r < c, t).astype(F32))

        heads = range(hp)
        cols = [slice(hh * HEAD, (hh + 1) * HEAD) for hh in heads]

        def row_sums(j, runs, diag):
            rows = pl.ds(pl.multiple_of(j * t, t), t)
            out = []
            for cs, run in zip(cols, runs):
                _, lk, _ = _softplus_parts(_dot(q_ref[:, cs], k_ref[rows, cs], NT) * scale)
                if diag:
                    lk = jnp.where(valid, lk, 0.0)
                out.append(run + jnp.sum(lk, axis=1, keepdims=True))
            return tuple(out)

        runs = row_sums(i, (jnp.zeros((t, 1), F32),) * hp, True)
        below, lktot = lax.while_loop(lambda c: (c[0] < i) & _any_alive(c[1]),
                                      lambda c: (c[0] + 1, row_sums(i - 1 - c[0], c[1], False)), (jnp.int32(0), runs))

        def tile(j, carry, diag):
            rows = pl.ds(pl.multiple_of(j * t, t), t)
            zs = [_dot(q_ref[:, cs], k_ref[rows, cs], NT) * scale for cs in cols]
            dws = [_dot(do_s[:, cs], v_ref[rows, cs], NT) for cs in cols]
            lbs, lks, sigs = [], [], []
            for z in zs:
                lb, lk, sig = _softplus_parts(z)
                lbs.append(lb)
                lks.append(jnp.where(valid, lk, 0.0) if diag else lk)
                sigs.append(sig)
            pins = [_dot(_bf(lk), m_le) for lk in lks]
            ws = [jnp.exp(lbs[hh] + (lktot[hh] - carry[hh][1]) - pins[hh]) for hh in heads]
            if diag:
                ws = [jnp.where(valid, w, 0.0) for w in ws]
            das = [dw * w for dw, w in zip(dws, ws)]
            pexs = [_dot(_bf(da), m_lt) for da in das]
            dzs = [das[hh] - sigs[hh] * (das[hh] + carry[hh][2] + pexs[hh]) for hh in heads]
            if diag:
                dzs = [jnp.where(valid, dz, 0.0) for dz in dzs]
            dzs = [_bf(dz) for dz in dzs]
            out = []
            for hh in heads:
                dkt, dvt = _dot(qt_s[hh], dzs[hh]), _dot(dot_s[hh], _bf(ws[hh]))
                for half in range(t // HEAD):
                    dkt_ref[hh, sub * j + half] += dkt[:, half * HEAD:(half + 1) * HEAD]
                    dvt_ref[hh, sub * j + half] += dvt[:, half * HEAD:(half + 1) * HEAD]
                dq, cpre, ppre = carry[hh]
                out.append((dq + _dot(dzs[hh], k_ref[rows, cols[hh]]), cpre + jnp.sum(lks[hh], axis=1, keepdims=True),
                            ppre + pexs[hh][:, t - 1:] + das[hh][:, t - 1:]))
            return tuple(out)

        zero = (jnp.zeros((t, HEAD), F32), jnp.zeros((t, 1), F32), jnp.zeros((t, 1), F32))
        carry = lax.fori_loop(i - below, i, lambda j, c: tile(j, c, False), (zero,) * hp)
        carry = tile(i, carry, True)
        for hh in range(hp):
            dq_ref[:, hh * HEAD:(hh + 1) * HEAD] = carry[hh][0] * scale

        @pl.when(i == nq - 1)
        def _():
            heads = pl.ds(pl.program_id(0) * hp, hp)
            copies = [pltpu.make_async_copy(dkt_ref, dkt_out.at[heads], out_sems.at[0]),
                      pltpu.make_async_copy(dvt_ref, dvt_out.at[heads], out_sems.at[1])]
            for cp in copies:
                cp.start()
            for cp in copies:
                cp.wait()

    sub = t // HEAD
    blk = lambda off: pl.BlockSpec((t, wg), lambda h, i: (i, off + h))
    full = lambda off: pl.BlockSpec((l, wg), lambda h, i: (0, off + h))
    acc_shape = jax.ShapeDtypeStruct((nh, l // HEAD, HEAD, HEAD), F32)
    acc_scratch = pltpu.VMEM((hp, l // HEAD, HEAD, HEAD), F32)
    outs = _call(
        body, name=name, grid=(ng, nq),
        in_specs=[blk(0), full(ng), full(2 * ng), blk(3 * wa // wg), blk(wa // wg)],
        out_specs=[blk(0), ANY, ANY], out_shape=[jax.ShapeDtypeStruct((l, wb), F32), acc_shape, acc_shape],
        scratch_shapes=[pltpu.VMEM((t, wg), BF16), pltpu.VMEM((hp, HEAD, t), BF16), pltpu.VMEM((hp, HEAD, t), BF16),
                        acc_scratch, acc_scratch, pltpu.SemaphoreType.DMA((2,))],
        args=(qkv, qkv, qkv, proj, dcat), sem=("parallel", "arbitrary"), ride=ride)
    return outs[0], outs[1], outs[2], outs[3:]


def _sgu_heads(v, g_ref, w_ref, bt_ref, nh):
    keep = _tri(lambda r, c: r >= c)
    out = []
    for h in range(nh):
        cols = slice(h * HEAD, (h + 1) * HEAD)
        nv, r = _rownorm(v[:, cols])
        wm = jnp.where(keep, w_ref[h], 0.0)
        s = _dot(_bf(wm), _bf(nv * g_ref[:, cols])) + bt_ref[:, h:h + 1]
        out.append((nv, r, wm, s))
    return out


def _sgu_fwd(proj, out_b, norm_g, sgu_w, sgu_bt, wa, wb, name):
    l, n = proj.shape
    nh = wa // HEAD

    def body(au_ref, av_ref, az_ref, bz_ref, ob_ref, g_ref, w_ref, bt_ref, cat_ref):
        u, v, sz = _gelu(au_ref[...].astype(F32)), _gelu(av_ref[...].astype(F32)), _silu(az_ref[...].astype(F32))
        for h, (_, _, _, s) in enumerate(_sgu_heads(v, g_ref, w_ref, bt_ref, nh)):
            cols = slice(h * HEAD, (h + 1) * HEAD)
            cat_ref[:, cols] = _bf(u[:, cols] * s * sz[:, cols])
        cat_ref[:, wa:] = _bf(ob_ref[...] * _silu(bz_ref[...].astype(F32)))

    a_blk = lambda j: pl.BlockSpec((HEAD, wa), lambda i: (i, j))
    return pl.pallas_call(
        body, name=name, grid=(l // HEAD,),
        in_specs=[a_blk(0), a_blk(1), a_blk(2), a_blk(3), pl.BlockSpec((HEAD, wb), lambda i: (i, 0)),
                  _vec_spec(wa), pl.BlockSpec((nh, HEAD, HEAD), lambda i: (0, 0, 0)),
                  pl.BlockSpec((HEAD, nh), lambda i: (0, 0))],
        out_specs=pl.BlockSpec((HEAD, wa + wb), lambda i: (i, 0)),
        out_shape=jax.ShapeDtypeStruct((l, wa + wb), BF16),
        compiler_params=_params(("parallel",)),
    )(proj, proj, proj, proj, out_b, norm_g, sgu_w, sgu_bt)


def _sgu_bwd(proj, out_b, dcat, dq, dk, dv, norm_g, sgu_w, sgu_bt, wa, wb, name, ride=None):
    l = proj.shape[0]
    n = 3 * wa + 4 * wb
    nh = wa // HEAD

    def body(au_ref, av_ref, az_ref, bz_ref, ob_ref, dc_ref, dq_ref, dk_ref, dv_ref, g_ref, w_ref, wt_ref, bt_ref,
             dp_ref, dw_ref, dbt_ref, dg_ref):
        first = pl.program_id(0) == 0
        keep = _tri(lambda r, c: r >= c)
        au, av, az = au_ref[...].astype(F32), av_ref[...].astype(F32), az_ref[...].astype(F32)
        u, v, sz = _gelu(au), _gelu(av), _silu(az)
        dgelu_u, dgelu_v, dsilu_z = _gelu_grad(au), _gelu_grad(av), _silu_grad(az)
        heads = _sgu_heads(v, g_ref, w_ref, bt_ref, nh)
        cols = [slice(h * HEAD, (h + 1) * HEAD) for h in range(nh)]
        dss = []
        for h, (nv, r, wm, s) in enumerate(heads):
            dca, uh, szh = dc_ref[:, cols[h]].astype(F32), u[:, cols[h]], sz[:, cols[h]]
            dp_ref[:, cols[h]] = _bf(dca * s * szh * dgelu_u[:, cols[h]])
            dp_ref[:, 2 * wa + h * HEAD:2 * wa + (h + 1) * HEAD] = _bf(dca * uh * s * dsilu_z[:, cols[h]])
            dss.append(dca * uh * szh)
        dws = [_dot(_bf(ds), _bf(nv * g_ref[:, cs]), NT) for ds, cs, (nv, _, _, _) in zip(dss, cols, heads)]
        keep_t = _tri(lambda r, c: r <= c)
        dvhs = [_dot(_bf(jnp.where(keep_t, wt_ref[h], 0.0)), _bf(dss[h])) for h in range(nh)]
        dg_parts = []
        for h, (nv, r, wm, s) in enumerate(heads):
            _acc(dw_ref.at[h], first, jnp.where(keep, dws[h], 0.0))
            _acc(dbt_ref.at[:, h:h + 1], first, jnp.sum(dss[h], axis=1, keepdims=True))
            dg_parts.append(_colsum(dvhs[h] * nv))
            dnv = dvhs[h] * g_ref[:, cols[h]]
            dvv = r * (dnv - nv * jnp.mean(dnv * nv, axis=-1, keepdims=True))
            dp_ref[:, wa + h * HEAD:wa + (h + 1) * HEAD] = _bf(dvv * dgelu_v[:, cols[h]])
        _acc(dg_ref, first, jnp.concatenate(dg_parts, axis=1))
        base = 3 * wa
        dp_ref[:, base:base + wb] = _bf(dq_ref[...])
        for h in range(wb // HEAD):
            dp_ref[:, base + wb + h * HEAD:base + wb + (h + 1) * HEAD] = _bf(dk_ref[h, 0].T)
            dp_ref[:, base + 2 * wb + h * HEAD:base + 2 * wb + (h + 1) * HEAD] = _bf(dv_ref[h, 0].T)
        dp_ref[:, base + 3 * wb:] = _bf(dc_ref[:, wa:].astype(F32) * ob_ref[...]
                                        * _silu_grad(bz_ref[...].astype(F32)))

    a_blk = lambda j: pl.BlockSpec((HEAD, wa), lambda i: (i, j))
    b_blk = pl.BlockSpec((HEAD, wb), lambda i: (i, 0))
    t_blk = pl.BlockSpec((wb // HEAD, 1, HEAD, HEAD), lambda i: (0, i, 0, 0))
    w_spec = pl.BlockSpec((nh, HEAD, HEAD), lambda i: (0, 0, 0))
    bt_spec = pl.BlockSpec((HEAD, nh), lambda i: (0, 0))
    outs = _call(
        body, name=name, grid=(l // HEAD,), ride=ride, sem=("arbitrary",),
        in_specs=[a_blk(0), a_blk(1), a_blk(2), a_blk(3), b_blk, pl.BlockSpec((HEAD, wa + wb), lambda i: (i, 0)),
                  b_blk, t_blk, t_blk, _vec_spec(wa), w_spec, w_spec, bt_spec],
        out_specs=[pl.BlockSpec((HEAD, n), lambda i: (i, 0)), w_spec, bt_spec, _vec_spec(wa)],
        out_shape=[jax.ShapeDtypeStruct((l, n), BF16), jax.ShapeDtypeStruct((nh, HEAD, HEAD), F32),
                   jax.ShapeDtypeStruct((HEAD, nh), F32), jax.ShapeDtypeStruct((1, wa), F32)],
        args=(proj, proj, proj, proj, out_b, dcat, dq, dk, dv, norm_g, sgu_w, sgu_w.transpose(0, 2, 1), sgu_bt))
    return (*outs[:4], outs[4:])


def _ssm_discretise(lr, li, ldt, br, bi):
    dt = jnp.exp(ldt)
    mag = jnp.exp(lr * dt)
    a_re = mag * jnp.cos(li * dt)
    a_im = mag * jnp.sin(li * dt)
    den = lr * lr + li * li
    nr = a_re - 1.0
    coef_re = (nr * lr + a_im * li) / den
    coef_im = (a_im * lr - nr * li) / den
    return a_re, a_im, coef_re * br - coef_im * bi, coef_re * bi + coef_im * br


def _ssm_prep(lr, li, ldt, br, bi, lr_row, li_row, ldt_row, name):
    s, c = br.shape

    def body(lr_ref, li_ref, ldt_ref, br_ref, bi_ref, lrr_ref, lir_ref, ldtr_ref, bbr_ref, bbi_ref, tr_ref, ti_ref):
        _, _, bbr, bbi = _ssm_discretise(lr_ref[...], li_ref[...], ldt_ref[...], br_ref[...], bi_ref[...])
        bbr_ref[...] = bbr
        bbi_ref[...] = bbi
        row = lax.broadcasted_iota(jnp.int32, (SCAN_ROWS, 1), 0)
        blk, r = jnp.right_shift(row, 3), jnp.bitwise_and(row, 7)
        kind, rev = jnp.bitwise_and(blk, 3), blk >= 4
        step = jnp.left_shift(1, kind)
        n = jnp.where(kind < 3, step, jnp.where(rev, 8 - r, r + 1)).astype(F32)
        keep = (kind == 3) | (rev & (r < 8 - step)) | (jnp.logical_not(rev) & (r >= step))
        dt = jnp.exp(ldtr_ref[...])
        mag = jnp.exp(n * (lrr_ref[...] * dt))
        ang = n * (lir_ref[...] * dt)
        tr_ref[...] = jnp.where(keep, mag * jnp.cos(ang), 0.0)
        ti_ref[...] = jnp.where(keep, jnp.where(rev, -1.0, 1.0) * mag * jnp.sin(ang), 0.0)

    col = jax.ShapeDtypeStruct((s, c), F32)
    row = jax.ShapeDtypeStruct((SCAN_ROWS, s), F32)
    return pl.pallas_call(body, name=name, out_shape=[col, col, row, row])(
        lr, li, ldt, br, bi, lr_row, li_row, ldt_row)


def _ssm_prep_bwd(lr, li, ldt, br, bi, da_re, da_im, dbb_re, dbb_im, p, name):
    s, c = br.shape

    def body(lr_ref, li_ref, ldt_ref, br_ref, bi_ref, dar_ref, dai_ref, dbr_ref, dbi_ref,
             dlr_ref, dli_ref, dldt_ref, dbre_ref, dbim_ref):
        args = (lr_ref[...], li_ref[...], ldt_ref[...], br_ref[...], bi_ref[...])
        _, vjp = jax.vjp(_ssm_discretise, *args)
        dlr, dli, dldt, dbr, dbi = vjp((dar_ref[...], dai_ref[...], dbr_ref[...], dbi_ref[...]))
        dlr_ref[...] = dlr
        dli_ref[...] = dli
        dbre_ref[...] = dbr
        dbim_ref[...] = dbi
        idx = lax.broadcasted_iota(jnp.int32, (s, s // p), 0)
        grp = lax.broadcasted_iota(jnp.int32, (s, s // p), 1)
        own = (idx >= grp * p) & (idx < (grp + 1) * p)
        dldt_ref[...] = _colsum(jnp.where(own, dldt, 0.0))

    col1 = jax.ShapeDtypeStruct((s, 1), F32)
    colc = jax.ShapeDtypeStruct((s, c), F32)
    return pl.pallas_call(
        body, name=name, out_shape=[col1, col1, jax.ShapeDtypeStruct((1, s // p), F32), colc, colc],
    )(lr, li, ldt, br, bi, da_re, da_im, dbb_re, dbb_im)


SCAN_ROWS = 64


def _scan_groups(xr, xi, tr_ref, ti_ref, cr, ci, reverse, per_group=None):
    ng = xr.shape[0] // 8
    base = SCAN_ROWS // 2 if reverse else 0
    pr, pi = tr_ref[base + 24:base + 32, :], ti_ref[base + 24:base + 32, :]
    edge = slice(0, 1) if reverse else slice(7, 8)
    out_r, out_i = [None] * ng, [None] * ng
    for g in (range(ng - 1, -1, -1) if reverse else range(ng)):
        sr, si = xr[8 * g:8 * g + 8, :], xi[8 * g:8 * g + 8, :]
        for k in range(3):
            ar, ai = tr_ref[base + 8 * k:base + 8 * k + 8, :], ti_ref[base + 8 * k:base + 8 * k + 8, :]
            shift = 8 - (1 << k) if reverse else 1 << k
            rr, ri = pltpu.roll(sr, shift, 0), pltpu.roll(si, shift, 0)
            sr, si = sr + ar * rr - ai * ri, si + ar * ri + ai * rr
        sr, si = sr + pr * cr - pi * ci, si + pr * ci + pi * cr
        cr, ci = sr[edge, :], si[edge, :]
        out_r[g], out_i[g] = sr, si
        if per_group is not None:
            per_group(g, sr, si)
    return jnp.concatenate(out_r, axis=0), jnp.concatenate(out_i, axis=0), cr, ci


def _ssm_fwd(proj, bbd, ccd, pw_re, pw_im, d_skip, w, name):
    l = proj.shape[0]
    nb, cw, ns2 = bbd.shape
    ns = ns2 // 2
    nc = l // SSM_T

    def body(u_ref, bbd_ref, ccd_ref, pr_ref, pi_ref, d_ref, y_ref, hsr_ref, hsi_ref, h_ref, hr_s, hi_s):
        @pl.when(pl.program_id(1) == 0)
        def _():
            hr_s[...] = jnp.zeros_like(hr_s)
            hi_s[...] = jnp.zeros_like(hi_s)

        hsr_ref[...] = hr_s[...].reshape(hsr_ref.shape)
        hsi_ref[...] = hi_s[...].reshape(hsi_ref.shape)
        u = u_ref[...]
        bu = _dot(_bf(u), bbd_ref[0])
        hr, hi, cr, ci = _scan_groups(bu[:, :ns], bu[:, ns:], pr_ref, pi_ref, hr_s[...], hi_s[...], False)
        hr_s[...] = cr
        hi_s[...] = ci
        h_bf = _bf(jnp.concatenate([hr, hi], axis=1))
        h_ref[...] = h_bf
        y_ref[...] = _dot(h_bf, ccd_ref[0]) + d_ref[...] * u

    tab = pl.BlockSpec((SCAN_ROWS, ns), lambda b, k: (0, b))
    return pl.pallas_call(
        body, name=name, grid=(nb, nc),
        in_specs=[pl.BlockSpec((SSM_T, cw), lambda b, k: (k, b)),
                  pl.BlockSpec((1, cw, ns2), lambda b, k: (b, 0, 0)),
                  pl.BlockSpec((1, ns2, cw), lambda b, k: (b, 0, 0)),
                  tab, tab, pl.BlockSpec((1, cw), lambda b, k: (0, b))],
        out_specs=[pl.BlockSpec((SSM_T, cw), lambda b, k: (k, b)),
                   pl.BlockSpec((1, 1, ns), lambda b, k: (k, 0, b)), pl.BlockSpec((1, 1, ns), lambda b, k: (k, 0, b)),
                   pl.BlockSpec((SSM_T, ns2), lambda b, k: (k, b))],
        out_shape=[jax.ShapeDtypeStruct((l, w), F32), jax.ShapeDtypeStruct((nc, 1, nb * ns), F32),
                   jax.ShapeDtypeStruct((nc, 1, nb * ns), F32), jax.ShapeDtypeStruct((l, nb * ns2), BF16)],
        scratch_shapes=[pltpu.VMEM((1, ns), F32), pltpu.VMEM((1, ns), F32)],
        compiler_params=_params(("parallel", "arbitrary")),
    )(proj, bbd, ccd, pw_re, pw_im, d_skip)


def _ssm_bwd(proj, dy, hs_re, hs_im, h_all, bbd, ccd, pw_re, pw_im, d_skip, w, name, ride=None):
    l = proj.shape[0]
    nb, cw, ns2 = bbd.shape
    ns = ns2 // 2
    nc = l // SSM_T

    def body(u_ref, dy_ref, hsr_ref, hsi_ref, h_ref, bbd_ref, ccd_ref, pr_ref, pi_ref, d_ref,
             du_ref, dbbd_ref, dccd_ref, dar_ref, dai_ref, dd_ref, gr_s, gi_s):
        first = pl.program_id(1) == 0

        @pl.when(first)
        def _():
            gr_s[...] = jnp.zeros_like(gr_s)
            gi_s[...] = jnp.zeros_like(gi_s)

        u, dy = u_ref[...], dy_ref[...]
        dy_bf = _bf(dy)
        hr0, hi0 = hsr_ref[0], hsi_ref[0]
        h = h_ref[...].astype(F32)
        hr, hi = h[:, :ns], h[:, ns:]
        dh = _dot(dy_bf, ccd_ref[0], NT)
        row0 = lax.broadcasted_iota(jnp.int32, (8, ns), 0) == 0
        da = [jnp.zeros((8, ns), F32), jnp.zeros((8, ns), F32)]

        def fold(g, g_re, g_im):
            before_r = hr0 if g == 0 else hr[8 * g - 1:8 * g, :]
            before_i = hi0 if g == 0 else hi[8 * g - 1:8 * g, :]
            p_re = jnp.where(row0, before_r, pltpu.roll(hr[8 * g:8 * g + 8, :], 1, 0))
            p_im = jnp.where(row0, before_i, pltpu.roll(hi[8 * g:8 * g + 8, :], 1, 0))
            da[0] = da[0] + p_re * g_re + p_im * g_im
            da[1] = da[1] + p_re * g_im - p_im * g_re

        gr, gi, gcr, gci = _scan_groups(dh[:, :ns], dh[:, ns:], pr_ref, pi_ref, gr_s[...], gi_s[...], True, fold)
        gr_s[...] = gcr
        gi_s[...] = gci
        _acc(dar_ref, first, _colsum(da[0]))
        _acc(dai_ref, first, _colsum(da[1]))
        g_bf = _bf(jnp.concatenate([gr, gi], axis=1))
        _acc(dbbd_ref.at[0], first, _dot(_bf(u.T), g_bf))
        _acc(dccd_ref.at[0], first, _dot(_bf(h.T), dy_bf))
        du_ref[...] = _bf(_dot(g_bf, bbd_ref[0], NT) + d_ref[...] * dy)
        _acc(dd_ref, first, _colsum(dy * u))

    rev = lambda b, k: (nc - 1 - k, b)
    outs = _call(
        body, name=name, grid=(nb, nc), ride=ride, sem=("parallel", "arbitrary"),
        args=(proj, dy, hs_re, hs_im, h_all, bbd, ccd, pw_re, pw_im, d_skip),
        in_specs=[pl.BlockSpec((SSM_T, cw), rev), pl.BlockSpec((SSM_T, cw), rev),
                  pl.BlockSpec((1, 1, ns), lambda b, k: (nc - 1 - k, 0, b)),
                  pl.BlockSpec((1, 1, ns), lambda b, k: (nc - 1 - k, 0, b)),
                  pl.BlockSpec((SSM_T, ns2), rev),
                  pl.BlockSpec((1, cw, ns2), lambda b, k: (b, 0, 0)),
                  pl.BlockSpec((1, ns2, cw), lambda b, k: (b, 0, 0)),
                  pl.BlockSpec((SCAN_ROWS, ns), lambda b, k: (0, b)), pl.BlockSpec((SCAN_ROWS, ns), lambda b, k: (0, b)),
                  pl.BlockSpec((1, cw), lambda b, k: (0, b))],
        out_specs=[pl.BlockSpec((SSM_T, cw), rev),
                   pl.BlockSpec((1, cw, ns2), lambda b, k: (b, 0, 0)),
                   pl.BlockSpec((1, ns2, cw), lambda b, k: (b, 0, 0)),
                   pl.BlockSpec((1, ns), lambda b, k: (0, b)), pl.BlockSpec((1, ns), lambda b, k: (0, b)),
                   pl.BlockSpec((1, cw), lambda b, k: (0, b))],
        out_shape=[jax.ShapeDtypeStruct((l, w), BF16), jax.ShapeDtypeStruct(bbd.shape, F32),
                   jax.ShapeDtypeStruct(ccd.shape, F32), jax.ShapeDtypeStruct((1, nb * ns), F32),
                   jax.ShapeDtypeStruct((1, nb * ns), F32), jax.ShapeDtypeStruct((1, w), F32)],
        scratch_shapes=[pltpu.VMEM((1, ns), F32), pltpu.VMEM((1, ns), F32)])
    return (*outs[:6], outs[6:])


def _block_diag_b(bb_re, bb_im, g, p, c):
    nb = g // SSM_GB
    keep = _same_group(SSM_GB * c, c, SSM_GB * p, p)

    def one(bb):
        t = bb.reshape(nb, SSM_GB, p, c).transpose(0, 1, 3, 2).reshape(nb, SSM_GB * c, p)
        return jnp.where(keep, jnp.tile(t, (1, 1, SSM_GB)), 0.0)

    return jnp.concatenate([one(bb_re), one(bb_im)], axis=2)


def _same_group(rows, per_row, cols, per_col):
    r = lax.broadcasted_iota(jnp.int32, (rows, cols), 0) // per_row
    q = lax.broadcasted_iota(jnp.int32, (rows, cols), 1) // per_col
    return r == q


def _block_diag_c(c_re, c_im, g, p, c):
    nb = g // SSM_GB
    keep = _same_group(SSM_GB * p, p, SSM_GB * c, c)

    def one(cc):
        t = cc.reshape(nb, SSM_GB, c, p).transpose(0, 1, 3, 2).reshape(nb, SSM_GB * p, c)
        return jnp.where(keep, jnp.tile(t, (1, 1, SSM_GB)), 0.0)

    return jnp.concatenate([one(c_re), one(-c_im)], axis=1)


def _diag_of_b(dbbd, g, p, c):
    nb = g // SSM_GB
    keep = _same_group(SSM_GB * c, c, SSM_GB * p, p)

    def one(blk):
        d = jnp.where(keep, blk, 0.0).reshape(nb, SSM_GB * c, SSM_GB, p).sum(axis=2)
        return d.reshape(nb, SSM_GB, c, p).transpose(0, 1, 3, 2).reshape(g * p, c)

    half = SSM_GB * p
    return one(dbbd[:, :, :half]), one(dbbd[:, :, half:])


def _diag_of_c(dccd, g, p, c):
    nb = g // SSM_GB
    keep = _same_group(SSM_GB * p, p, SSM_GB * c, c)

    def one(blk):
        d = jnp.where(keep, blk, 0.0).reshape(nb, SSM_GB * p, SSM_GB, c).sum(axis=2)
        return d.reshape(nb, SSM_GB, p, c).transpose(0, 1, 3, 2).reshape(g, c, p)

    half = SSM_GB * p
    return one(dccd[:, :half]), -one(dccd[:, half:])


def _glu_fwd(y, proj, w_glu, b_glu, name):
    l, w = y.shape
    tm = _tile(l, 256)

    def body(y_ref, z_ref, w_ref, b_ref, o_ref):
        g = _gelu(y_ref[...])
        t = _dot(_bf(g), w_ref[...]) + b_ref[...]
        o_ref[...] = _bf(g * _sigmoid(t) * _silu(z_ref[...]))

    return pl.pallas_call(
        body, name=name, grid=(l // tm,),
        in_specs=[_row_spec(tm, w), pl.BlockSpec((tm, w), lambda i: (i, 1)),
                  pl.BlockSpec((w, w), lambda i: (0, 0)), _vec_spec(w)],
        out_specs=_row_spec(tm, w), out_shape=jax.ShapeDtypeStruct((l, w), BF16),
        compiler_params=_params(("parallel",)),
    )(y, proj, w_glu, b_glu)


def _glu_bwd(do, y, proj, w_glu, b_glu, name):
    l, w = y.shape
    tm = _tile(l, 512)
    nsteps = l // tm

    def body(do_ref, y_ref, z_ref, w_ref, b_ref, dy_ref, dz_ref, dw_ref, db_ref, dw_acc):
        i = pl.program_id(0)
        first = i == 0
        halves = [slice(0, tm // 2), slice(tm // 2, tm)]
        gs = [_gelu(y_ref[rows, :]) for rows in halves]
        ts = [_dot(_bf(g), w_ref[...]) for g in gs]
        dts, dyys, sgs = [], [], []
        for rows, g, t in zip(halves, gs, ts):
            z, do = z_ref[rows, :], do_ref[rows, :].astype(F32)
            sg = _sigmoid(t + b_ref[...])
            dyy = do * _silu(z)
            dz_ref[rows, :] = _bf(do * g * sg * _silu_grad(z))
            dts.append(dyy * g * sg * (1.0 - sg))
            dyys.append(dyy)
            sgs.append(sg)
        backs = [_dot(_bf(dt), w_ref[...], NT) for dt in dts]
        for rows, dyy, sg, back in zip(halves, dyys, sgs, backs):
            dy_ref[rows, :] = (dyy * sg + back) * _gelu_grad(y_ref[rows, :])
        dt = jnp.concatenate(dts, axis=0)
        _acc(dw_acc, first, _dot(_bf(jnp.concatenate(gs, axis=0).T), _bf(dt)))
        _acc(db_ref, first, _colsum(dt))

        @pl.when(i == nsteps - 1)
        def _():
            dw_ref[...] = _bf(dw_acc[...])

    return pl.pallas_call(
        body, name=name, grid=(nsteps,),
        in_specs=[_row_spec(tm, w), _row_spec(tm, w), pl.BlockSpec((tm, w), lambda i: (i, 1)),
                  pl.BlockSpec((w, w), lambda i: (0, 0)), _vec_spec(w)],
        out_specs=[_row_spec(tm, w), _row_spec(tm, w), pl.BlockSpec((w, w), lambda i: (0, 0)), _vec_spec(w)],
        out_shape=[jax.ShapeDtypeStruct((l, w), F32), jax.ShapeDtypeStruct((l, w), BF16),
                   jax.ShapeDtypeStruct((w, w), BF16), jax.ShapeDtypeStruct((1, w), F32)],
        scratch_shapes=[pltpu.VMEM((w, w), F32)],
        compiler_params=_params(("arbitrary",)),
    )(do, y, proj, w_glu, b_glu)


MOD_ROWS = 128


def _mod_fwd(cond_pad, w_mod, b_shard, name):
    nl, d, ncol = w_mod.shape
    tn = _tile(ncol, 512)

    def body(c_ref, w_ref, b_ref, o_ref):
        o_ref[0] = _dot(_bf(c_ref[...]), _bf(w_ref[0])) + b_ref[0]

    return pl.pallas_call(
        body, name=name, grid=(nl, ncol // tn),
        in_specs=[pl.BlockSpec((MOD_ROWS, d), lambda a, j: (0, 0)),
                  pl.BlockSpec((1, d, tn), lambda a, j: (a, 0, j)),
                  pl.BlockSpec((1, 1, tn), lambda a, j: (a, 0, j))],
        out_specs=pl.BlockSpec((1, MOD_ROWS, tn), lambda a, j: (a, 0, j)),
        out_shape=jax.ShapeDtypeStruct((nl, MOD_ROWS, ncol), F32),
        compiler_params=_params(("parallel", "parallel")),
    )(cond_pad, w_mod, b_shard)


def _mod_bwd(cond_pad_t, dmod_pad, name):
    nl, _, ncol = dmod_pad.shape
    d = cond_pad_t.shape[0]
    tn = _tile(ncol, 512)

    def body(c_ref, dm_ref, o_ref):
        o_ref[0] = _dot(_bf(c_ref[...]), _bf(dm_ref[0]))

    return pl.pallas_call(
        body, name=name, grid=(nl, ncol // tn),
        in_specs=[pl.BlockSpec((d, MOD_ROWS), lambda a, j: (0, 0)),
                  pl.BlockSpec((1, MOD_ROWS, tn), lambda a, j: (a, 0, j))],
        out_specs=pl.BlockSpec((1, d, tn), lambda a, j: (a, 0, j)),
        out_shape=jax.ShapeDtypeStruct((nl, d, ncol), F32),
        compiler_params=_params(("parallel", "parallel")),
    )(cond_pad_t, dmod_pad)


def _silu_rows(c2d, name):
    def body(c_ref, o_ref):
        o_ref[...] = _silu(c_ref[...])

    return pl.pallas_call(body, name=name, out_shape=jax.ShapeDtypeStruct(c2d.shape, F32))(c2d)


def _sum_leading(x, name):
    n, r, c = x.shape
    tr = _tile(r, max(16, (1 << 20) // (4 * c)), 16 if r % 16 == 0 else 8)

    def body(x_ref, o_ref):
        acc = x_ref[0].astype(F32)
        for k in range(1, n):
            acc = acc + x_ref[k].astype(F32)
        o_ref[...] = acc

    return pl.pallas_call(
        body, name=name, grid=(r // tr,),
        in_specs=[pl.BlockSpec((n, tr, c), lambda i: (0, i, 0))], out_specs=pl.BlockSpec((tr, c), lambda i: (i, 0)),
        out_shape=jax.ShapeDtypeStruct((r, c), F32), compiler_params=_params(("parallel",)),
    )(x)


def _adamw(w, gs, m, v, name, ride=None):
    r, c = w.shape
    tr = _tile(r, max(8, (3 << 19) // (4 * c)), 8)
    ng = len(gs)

    def body(*refs):
        w_ref, g_refs, m_ref, v_ref = refs[0], refs[1:1 + ng], refs[1 + ng], refs[2 + ng]
        g_ref, d_ref, nm_ref, nv_ref = refs[3 + ng:]
        g = g_refs[0][...]
        for extra in g_refs[1:]:
            g = g + extra[...]
        g_ref[...] = g
        d_ref[...], nm_ref[...], nv_ref[...] = _adamw_math(w_ref[...], g, m_ref[...], v_ref[...])

    spec = pl.BlockSpec((tr, c), lambda i: (i, 0))
    shp = jax.ShapeDtypeStruct((r, c), F32)
    outs = _call(body, name=name, grid=(r // tr,), in_specs=[spec] * (3 + ng), out_specs=[spec] * 4,
                 out_shape=[shp] * 4, args=(w, *gs, m, v), sem=("parallel",), ride=ride)
    return outs if ride is None else (outs[:4], outs[4:])


def _adamw_math(w, g, m, v):
    nm = ADAM_B1 * m + (1.0 - ADAM_B1) * g
    nv = ADAM_B2 * v + (1.0 - ADAM_B2) * (g * g)
    m_hat = nm / (1.0 - ADAM_B1 ** ADAM_STEP)
    v_hat = nv / (1.0 - ADAM_B2 ** ADAM_STEP)
    return -ADAM_LR * (m_hat / (jnp.sqrt(v_hat) + ADAM_EPS) + ADAM_WD * w), nm, nv


def _adamw_many(ws, gs, ms, vs, name):
    n = len(ws)

    def body(*refs):
        w_refs, g_refs, m_refs, v_refs = (refs[k * n:(k + 1) * n] for k in range(4))
        outs = refs[4 * n:]
        for i in range(n):
            outs[3 * i][...], outs[3 * i + 1][...], outs[3 * i + 2][...] = _adamw_math(
                w_refs[i][...], g_refs[i][...], m_refs[i][...], v_refs[i][...])

    out_shape = [jax.ShapeDtypeStruct(w.shape, F32) for w in ws for _ in range(3)]
    outs = pl.pallas_call(body, name=name, out_shape=out_shape, compiler_params=_params())(*ws, *gs, *ms, *vs)
    return [tuple(outs[3 * i:3 * i + 3]) for i in range(n)]


ANY = pl.BlockSpec(memory_space=pl.ANY)


def _flip(v, bit):
    return 1 - v if bit else v


def _allgather8_ops(x_ref, o_ref, send_sems, recv_sems, local_sem):
    mx, my, mc = lax.axis_index("x"), lax.axis_index("y"), lax.axis_index("c")
    me = 4 * mx + 2 * my + mc

    def mine():
        return pltpu.make_async_copy(x_ref, o_ref.at[me], local_sem)

    def copy(j, outgoing):
        peer = (_flip(mx, j & 4), _flip(my, j & 2), _flip(mc, j & 1))
        slot = me if outgoing else 4 * peer[0] + 2 * peer[1] + peer[2]
        return pltpu.make_async_remote_copy(
            src_ref=x_ref, dst_ref=o_ref.at[slot], send_sem=send_sems.at[j - 1], recv_sem=recv_sems.at[j - 1],
            device_id=peer, device_id_type=MESH)

    def start():
        mine().start()
        for j in range(1, 8):
            copy(j, True).start()

    def wait():
        for j in range(1, 8):
            copy(j, False).wait()
        mine().wait()

    return start, wait


def _ride_all8(x):
    return dict(xs=[x], shapes=[jax.ShapeDtypeStruct((8,) + x.shape, x.dtype)],
                sems=[pltpu.SemaphoreType.DMA((7,)), pltpu.SemaphoreType.DMA((7,)), pltpu.SemaphoreType.DMA],
                ops=lambda x_refs, o_refs, sems: _allgather8_ops(x_refs[0], o_refs[0], *sems))


def _ride_chip(xs, gather):
    return dict(xs=list(xs), shapes=_chip_exchange_shapes(xs, gather), sems=_chip_exchange_sems(len(xs)),
                ops=lambda x_refs, o_refs, sems: _chip_exchange_ops(x_refs, o_refs, *sems, gather))


def _allgather8(x, name):
    return _exchange([_ride_all8(x)], name)[0]


def _gather_halves_ops(x_ref, o_ref, ici_send, ici_recv, d2d_send, d2d_recv, local_sem):
    half = x_ref.shape[0] // 2
    quarter = half // 2
    mx, my, mc = lax.axis_index("x"), lax.axis_index("y"), lax.axis_index("c")
    k0, kx, ky, kd = 2 * mx + my, 2 * (1 - mx) + my, 2 * mx + (1 - my), 2 * (1 - mx) + (1 - my)
    x_nbr, y_nbr, sib = (1 - mx, my, mc), (mx, 1 - my, mc), (mx, my, 1 - mc)

    def rows(core, part):
        if part is None:
            return pl.ds(pl.multiple_of(core * half, 16), half)
        return pl.ds(pl.multiple_of(core * half + part * quarter, 16), quarter)

    def local():
        return pltpu.make_async_copy(x_ref, o_ref.at[k0], local_sem)

    def ici(n, outgoing):
        to = x_nbr if n in (0, 2) else y_nbr
        if n < 2:
            src = x_ref.at[rows(mc, None)]
            dst = o_ref.at[k0 if outgoing else (kx if n == 0 else ky), rows(mc, None)]
        else:
            part = n - 2
            src = o_ref.at[ky if n == 2 else kx, rows(mc, part)]
            dst = o_ref.at[(ky if n == 2 else kx) if outgoing else kd, rows(mc, part)]
        return pltpu.make_async_remote_copy(src_ref=src, dst_ref=dst, send_sem=ici_send.at[n], recv_sem=ici_recv.at[n],
                                            device_id=to, device_id_type=MESH)

    def d2d(n, outgoing):
        slot, part = ((kx, None), (ky, None), (kd, 0), (kd, 1))[n]
        piece = o_ref.at[slot, rows(mc if outgoing else 1 - mc, part)]
        return pltpu.make_async_remote_copy(src_ref=piece, dst_ref=piece, send_sem=d2d_send.at[n],
                                            recv_sem=d2d_recv.at[n], device_id=sib, device_id_type=MESH)

    def start():
        local().start()
        ici(0, True).start()
        ici(1, True).start()

    def wait():
        ici(1, False).wait_recv()
        ici(2, True).start()
        d2d(1, True).start()
        ici(0, False).wait_recv()
        ici(3, True).start()
        d2d(0, True).start()
        ici(2, False).wait_recv()
        d2d(2, True).start()
        ici(3, False).wait_recv()
        d2d(3, True).start()
        for n in range(4):
            ici(n, True).wait_send()
            d2d(n, True).wait_send()
            d2d(n, False).wait_recv()
        local().wait()

    return start, wait


def _ride_halves(x):
    dma4 = pltpu.SemaphoreType.DMA((4,))
    return dict(xs=[x], shapes=[jax.ShapeDtypeStruct((4,) + x.shape, x.dtype)],
                sems=[dma4, dma4, dma4, dma4, pltpu.SemaphoreType.DMA],
                ops=lambda x_refs, o_refs, sems: _gather_halves_ops(x_refs[0], o_refs[0], *sems))


def _exchange(rides, name):
    xs = [x for r in rides for x in r["xs"]]
    nx = len(xs)

    def body(*refs):
        x_refs, o_refs, sems = refs[:nx], refs[nx:2 * nx], refs[2 * nx:]
        ops, xo, so = [], 0, 0
        for r in rides:
            nr, ns = len(r["xs"]), len(r["sems"])
            ops.append(r["ops"](x_refs[xo:xo + nr], o_refs[xo:xo + nr], sems[so:so + ns]))
            xo, so = xo + nr, so + ns
        for start, _ in ops:
            start()
        for _, wait in ops:
            wait()

    return pl.pallas_call(body, name=name, in_specs=[ANY] * nx, out_specs=[ANY] * nx,
                          out_shape=[s for r in rides for s in r["shapes"]],
                          scratch_shapes=[s for r in rides for s in r["sems"]])(*xs)


def _chip_exchange_shapes(xs, gather):
    return [jax.ShapeDtypeStruct(((4,) + x.shape) if gather else x.shape, x.dtype) for x in xs]


def _chip_exchange_sems(n):
    return [pltpu.SemaphoreType.DMA((3 * n,)), pltpu.SemaphoreType.DMA((3 * n,)), pltpu.SemaphoreType.DMA((n,))]


def _chip_exchange_ops(x_refs, o_refs, send_sems, recv_sems, local_sems, gather):
    n = len(x_refs)
    mx, my, mc = lax.axis_index("x"), lax.axis_index("y"), lax.axis_index("c")
    k0 = 2 * mx + my

    def local(a):
        src = x_refs[a] if gather else x_refs[a].at[k0]
        return pltpu.make_async_copy(src, o_refs[a].at[k0], local_sems.at[a])

    def copy(a, j, outgoing):
        px, py = _flip(mx, j & 2), _flip(my, j & 1)
        kp = 2 * px + py
        if outgoing:
            src = x_refs[a] if gather else x_refs[a].at[kp]
            dst = o_refs[a].at[k0]
        else:
            src = x_refs[a] if gather else x_refs[a].at[k0]
            dst = o_refs[a].at[kp]
        s = a * 3 + j - 1
        return pltpu.make_async_remote_copy(
            src_ref=src, dst_ref=dst, send_sem=send_sems.at[s], recv_sem=recv_sems.at[s],
            device_id=(px, py, mc), device_id_type=MESH)

    def start():
        for a in range(n):
            local(a).start()
            for j in range(1, 4):
                copy(a, j, True).start()

    def wait():
        for a in range(n):
            for j in range(1, 4):
                copy(a, j, False).wait()
            local(a).wait()

    return start, wait


def _call(body, *, name, grid, in_specs, out_specs, out_shape, args, scratch_shapes=(), sem=None, ride=None):
    if not ride:
        return pl.pallas_call(
            body, name=name, grid=grid, in_specs=list(in_specs), out_specs=list(out_specs), out_shape=list(out_shape),
            scratch_shapes=list(scratch_shapes), compiler_params=_params(sem))(*args)
    xs = [x for r in ride for x in r["xs"]]
    shapes = [s for r in ride for s in r["shapes"]]
    sems = [s for r in ride for s in r["sems"]]
    n_in, n_out, n_scr, nx = len(in_specs), len(out_specs), len(scratch_shapes), len(xs)

    def wrapped(*refs):
        ins, x_refs = refs[:n_in], refs[n_in:n_in + nx]
        outs = refs[n_in + nx:n_in + nx + n_out]
        lands = refs[n_in + nx + n_out:n_in + 2 * nx + n_out]
        rest = refs[n_in + 2 * nx + n_out:]
        scr, sem_refs = rest[:n_scr], rest[n_scr:]
        ops, xo, so = [], 0, 0
        for r in ride:
            nr, ns = len(r["xs"]), len(r["sems"])
            ops.append(r["ops"](x_refs[xo:xo + nr], lands[xo:xo + nr], sem_refs[so:so + ns]))
            xo, so = xo + nr, so + ns
        ids = [pl.program_id(a) for a in range(len(grid))]
        first = functools.reduce(jnp.logical_and, [i == 0 for i in ids])
        last = functools.reduce(jnp.logical_and, [i == g - 1 for i, g in zip(ids, grid)])

        @pl.when(first)
        def _():
            for start, _ in ops:
                start()

        body(*ins, *outs, *scr)

        @pl.when(last)
        def _():
            for _, wait in ops:
                wait()

    return pl.pallas_call(
        wrapped, name=name, grid=grid, in_specs=list(in_specs) + [ANY] * nx, out_specs=list(out_specs) + [ANY] * nx,
        out_shape=list(out_shape) + shapes, scratch_shapes=list(scratch_shapes) + sems,
        compiler_params=_params(("arbitrary",) * len(grid)))(*args, *xs)


def _ride_sibling(xs):
    n = len(xs)

    def ops(x_refs, o_refs, sems):
        send_sems, recv_sems = sems
        sib = (lax.axis_index("x"), lax.axis_index("y"), 1 - lax.axis_index("c"))

        def copies():
            return [pltpu.make_async_remote_copy(
                src_ref=x_refs[a], dst_ref=o_refs[a], send_sem=send_sems.at[a], recv_sem=recv_sems.at[a],
                device_id=sib, device_id_type=MESH) for a in range(n)]

        def start():
            for cp in copies():
                cp.start()

        def wait():
            for cp in copies():
                cp.wait()

        return start, wait

    return dict(xs=list(xs), shapes=[jax.ShapeDtypeStruct(x.shape, x.dtype) for x in xs],
                sems=[pltpu.SemaphoreType.DMA((n,)), pltpu.SemaphoreType.DMA((n,))], ops=ops)


PACK = 1024
PACK_ROWS = 512


def _pack(parts, pad_rows=PACK_ROWS):
    flat = []
    for p in parts:
        v = p.reshape(-1).astype(F32)
        flat.append(jnp.pad(v, (0, (-v.shape[0]) % PACK)))
    total = sum(v.shape[0] for v in flat)
    flat.append(jnp.zeros(((-total) % (pad_rows * 128),), F32))
    return jnp.concatenate(flat).reshape(-1, 128)


def _shard_columns(shards, lo, hi):
    width = shards.shape[2]
    out = []
    for k in range(shards.shape[0]):
        a, b = max(lo, k * width), min(hi, (k + 1) * width)
        if a < b:
            out.append(shards[k, :, a - k * width:b - k * width])
    return out


def _unpack_rows(gathered, shapes):
    flat = gathered.reshape(gathered.shape[0], -1)
    out, off = [], 0
    for shp in shapes:
        n = math.prod(shp)
        out.append(flat[:, off:off + n].reshape((flat.shape[0],) + tuple(shp)))
        off += n + (-n) % PACK
    return out


def _unpack(packed, shapes):
    flat = packed.reshape(-1)
    out, off = [], 0
    for shp in shapes:
        n = math.prod(shp)
        out.append(flat[off:off + n].reshape(shp))
        off += n + (-n) % PACK
    return out


def kernel(x, c, ln_pre_g, ln_post_g, w_mod, b_mod, w_in_ab, w_out_ab, sgu_norm_g, sgu_w, sgu_b, w_in_ssm, w_out_ssm, lam_re, lam_im, b_re, b_im, c_re, c_im, d_skip, log_dt, w_glu, b_glu, loss_target, m_ln_pre_g, m_ln_post_g, m_w_mod, m_b_mod, m_w_in_ab, m_w_out_ab, m_sgu_norm_g, m_sgu_w, m_sgu_b, m_w_in_ssm, m_w_out_ssm, m_lam_re, m_lam_im, m_b_re, m_b_im, m_c_re, m_c_im, m_d_skip, m_log_dt, m_w_glu, m_b_glu, v_ln_pre_g, v_ln_post_g, v_w_mod, v_b_mod, v_w_in_ab, v_w_out_ab, v_sgu_norm_g, v_sgu_w, v_sgu_b, v_w_in_ssm, v_w_out_ssm, v_lam_re, v_lam_im, v_b_re, v_b_im, v_c_re, v_c_im, v_d_skip, v_log_dt, v_w_glu, v_b_glu):
    given = dict(locals())
    mx, my, mc = lax.axis_index("x"), lax.axis_index("y"), lax.axis_index("c")
    me = 4 * mx + 2 * my + mc
    chip = 2 * mx + my

    _, l, d = x.shape
    x2, tgt = x[0], loss_target[0]
    n_in = w_in_ab.shape[2] * 4
    wa = wb = n_in // 7
    w = w_out_ssm.shape[1]
    g, p, cch = b_re.shape[1:]
    nmod = w_mod.shape[2]


    cond = _silu_rows(c.reshape(d // 128, 128), "cond_silu")
    cond_all = _allgather8(cond, "gather_cond").reshape(8, d)
    b_shard = lax.dynamic_slice(b_mod, (0, chip * nmod), (2, nmod)).reshape(2, 1, nmod)
    cond_pad = jnp.pad(cond_all, ((0, MOD_ROWS - 8), (0, 0)))
    modp = _mod_fwd(cond_pad, w_mod, b_shard, "mod_fwd")[:, :8]
    modp_all = _allgather8(modp.reshape(16, nmod), "gather_mod").reshape(4, 2, 2, 8, nmod)
    mine = lax.dynamic_index_in_dim(lax.dynamic_index_in_dim(modp_all, mc, 1, False), me, 2, False)
    mod = mine.transpose(1, 0, 2).reshape(2, 3 * d)
    shift = [mod[a:a + 1, :d] for a in range(2)]
    scale = [mod[a:a + 1, d:2 * d] for a in range(2)]
    gate = [mod[a:a + 1, 2 * d:] for a in range(2)]
    pre_g = [ln_pre_g[a:a + 1] for a in range(2)]
    post_g = [ln_post_g[a:a + 1] for a in range(2)]

    sgu_w0, sgu_bt = sgu_w[0], sgu_b[0].T
    h0, (gw_in_ab,) = _pre_fwd(x2, pre_g[0], scale[0], shift[0], "pre0_fwd", ride=[_ride_halves(_bf(w_in_ab[0]))])
    w_gates = jnp.concatenate(_shard_columns(gw_in_ab, 0, 3 * wa) + _shard_columns(gw_in_ab, 3 * wa + 3 * wb, n_in),
                              axis=1)
    w_qkv = jnp.concatenate(_shard_columns(gw_in_ab, 3 * wa, 3 * wa + 3 * wb), axis=1)
    proj0, (gw_in_ssm,) = _matmul(h0, w_gates, "nn", BF16, "proj0", tm=1024, ride=[_ride_chip([_bf(w_in_ssm[0])], True)])
    qkv, (gw_out_ssm, gw_glu, g_dskip, g_bglu) = _matmul(
        h0, w_qkv, "nn", BF16, "proj0_qkv", tm=1024,
        ride=[_ride_chip([_bf(w_out_ssm[0]), _bf(w_glu[0]), d_skip, b_glu], True)])
    out_b, (gw_out_ab,) = _attn_fwd(qkv, wb, "attn_fwd", hp=8, ride=[_ride_chip([_bf(w_out_ab[0])], True)])
    wout_ab = gw_out_ab.reshape(wa + wb, d)
    win_ssm = gw_in_ssm.reshape(d, 2 * w)
    wout_ssm = jnp.concatenate([gw_out_ssm[k] for k in range(4)], axis=1)
    wglu = gw_glu.reshape(w, w)
    dskip_full = g_dskip.reshape(1, w)
    bglu_full = g_bglu.reshape(1, w)
    cat =_sgu_fwd(proj0, out_b, sgu_norm_g, sgu_w0, sgu_bt, wa, wb, "sgu_fwd")
    y0 = _matmul(cat, wout_ab, "nn", BF16, "out0", tm=1024)
    x1, h1 = _post_pre_fwd(x2, y0, gate[0], post_g[0], pre_g[1], scale[1], shift[1], "post0_pre1_fwd")

    s = g * p
    lr_c, li_c = lam_re.reshape(s, 1), lam_im.reshape(s, 1)
    ldt_c = jnp.repeat(log_dt.reshape(g), p).reshape(s, 1)
    br_c, bi_c = b_re.reshape(s, cch), b_im.reshape(s, cch)
    bb_re, bb_im, pw_re, pw_im = _ssm_prep(lr_c, li_c, ldt_c, br_c, bi_c, lr_c.reshape(1, s), li_c.reshape(1, s),
                                           ldt_c.reshape(1, s), "ssm_prep")
    bbd = _bf(_block_diag_b(bb_re, bb_im, g, p, cch))
    ccd = _bf(_block_diag_c(c_re[0], c_im[0], g, p, cch))
    proj1 = _matmul(h1, win_ssm, "nn", F32, "proj1", tm=1024)
    y_ssm, hs_re, hs_im, h_all = _ssm_fwd(proj1, bbd, ccd, pw_re, pw_im, dskip_full, w, "ssm_fwd")
    o1 = _glu_fwd(y_ssm, proj1, wglu, bglu_full, "glu_fwd")
    y1 = _matmul(o1, wout_ssm, "nn", BF16, "out1", tm=1024)
    loss_vec, dy1, dx2, dgate1, dpost1 = _post_loss(x1, y1, gate[1], post_g[1], tgt, "post1_loss")

    do1 = _matmul(dy1, wout_ssm, "nt", BF16, "out1_dx", tm=1024)
    gr_wout_ssm = _matmul_at(o1, dy1, BF16, "out1_dw", n_split=4)
    dy_ssm, dz1, gr_wglu, gr_bglu = _glu_bwd(do1, y_ssm, proj1, wglu, bglu_full, "glu_bwd")
    du1, dbbd, dccd, da_re, da_im, gr_dskip, (ld_wout_ssm, ld_wglu) = _ssm_bwd(
        proj1, dy_ssm, hs_re, hs_im, h_all, bbd, ccd, pw_re, pw_im, dskip_full, w, "ssm_bwd",
        ride=[_ride_chip([gr_wout_ssm, gr_wglu.reshape(4, w // 4, w)], False)])
    dproj1 = jnp.concatenate([du1, dz1], axis=1)
    dh1 = _matmul(dproj1, win_ssm, "nt", BF16, "proj1_dx", tm=1024)
    gr_win_ssm = _matmul_at(h1, dproj1, BF16, "proj1_dw", tn=1024)
    dx1, dscale1, dshift1, dpre1, dy0, dgate0, dpost0 = _pre_bwd(
        dh1, dx2, x1, pre_g[1], scale[1], "pre1_post0_bwd", post=(y0, gate[0], post_g[0]))

    dcat = _matmul(dy0, wout_ab, "nt", BF16, "out0_dx", tm=1024)
    gr_wout_ab = _matmul_at(cat, dy0, BF16, "out0_dw", tn=1024)
    dbb_re, dbb_im = _diag_of_b(dbbd, g, p, cch)
    dc_re, dc_im = _diag_of_c(dccd, g, p, cch)
    part_a = [loss_vec[:, :1], dpre1, dpost0, dpost1, dgate0, dshift1, dscale1, dgate1, da_re, da_im,
              dbb_re, dbb_im, dc_re, dc_im, gr_dskip, gr_bglu]
    shapes_a = [a.shape for a in part_a]
    dq, dk, dv, (ld_win_ssm, ld_wout_ab, gath_a) = _attn_bwd(
        qkv, proj0, dcat, wa, wb, "attn_bwd", hp=4,
        ride=[_ride_chip([gr_win_ssm.reshape(4, d // 4, 2 * w), gr_wout_ab.reshape(4, (wa + wb) // 4, d)], False),
              _ride_all8(_pack(part_a))])
    early_names = ["w_out_ab", "w_in_ssm", "w_out_ssm", "w_glu"]
    early_sums = [_sum_leading(a, "sum_" + nm) for a, nm in zip([ld_wout_ab, ld_win_ssm, ld_wout_ssm, ld_wglu], early_names)]
    dproj0, gr_sgu_w, gr_sgu_bt, gr_sgu_g, early_sib = _sgu_bwd(
        proj0, out_b, dcat, dq, dk, dv, sgu_norm_g, sgu_w0, sgu_bt, wa, wb, "sgu_bwd", ride=[_ride_sibling(early_sums)])
    part_b = [gr_sgu_g, gr_sgu_w, gr_sgu_bt.T]
    shapes_b = [a.shape for a in part_b]
    gr_win_ab_lo, (gath_b,) = _matmul_at(h0, dproj0, BF16, "proj0_dw_lo", tm=512, tn=n_in // 4, n_split=4,
                                         m_part=(0, 1, 2), ride=[_ride_all8(_pack(part_b))])
    gr_win_ab_hi, (ld_win_ab_lo,) = _matmul_at(
        h0, dproj0, BF16, "proj0_dw_hi", tm=512, tn=n_in // 4, n_split=4, m_part=(1, 1, 2),
        ride=[_ride_chip([gr_win_ab_lo], False)])
    dh0, (ld_win_ab_hi,) = _matmul(dproj0, gw_in_ab, "nt", BF16, "proj0_dx", tm=1024, tn=1024,
                                   ride=[_ride_chip([gr_win_ab_hi], False)])
    grad_x, dscale0, dshift0, dpre0 = _pre_bwd(dh0, dx1, x2, pre_g[0], scale[0], "pre0_bwd")
    part_c = [dpre0, dshift0, dscale0]
    shapes_c = [a.shape for a in part_c]

    big_names = ["w_in_ab"] + early_names
    sum_win_ab = jnp.concatenate([_sum_leading(ld_win_ab_lo, "sum_w_in_ab_lo"),
                                  _sum_leading(ld_win_ab_hi, "sum_w_in_ab_hi")], axis=0)
    sums = [sum_win_ab] + early_sums
    gath_c, sib_win_ab = _exchange([_ride_all8(_pack(part_c, pad_rows=8)), _ride_sibling([sum_win_ab])], "tail_exchange")
    sib = [sib_win_ab] + list(early_sib)
    results = {}
    for nm, s_mine, s_sib in zip(big_names, sums, sib):
        shp = given[nm].shape
        two_d = lambda a: a.reshape(-1, shp[-1])
        outs = _adamw(two_d(given[nm]), [s_mine, s_sib], two_d(given["m_" + nm]), two_d(given["v_" + nm]),
                      "adamw_" + nm)
        results[nm] = [o.reshape(shp) for o in outs]

    (loss_s, g_pre1, g_post0, g_post1, g_gate0, g_shift1, g_scale1, g_gate1, s_da_re, s_da_im, s_dbb_re, s_dbb_im,
     g_c_re, g_c_im, g_dskip_full, g_bglu_full) = _unpack(_sum_leading(gath_a, "sum_small_a"), shapes_a)
    g_sgu_g, g_sgu_w, g_sgu_b = _unpack(_sum_leading(gath_b, "sum_small_b"), shapes_b)
    g_pre0, g_shift0, g_scale0 = _unpack(_sum_leading(gath_c, "sum_small_c"), shapes_c)
    loss = loss_s.reshape(())
    g_pre = jnp.concatenate([g_pre0, g_pre1], axis=0)
    g_post = jnp.concatenate([g_post0, g_post1], axis=0)
    g_bmod = jnp.concatenate([jnp.concatenate([g_shift0, g_scale0, g_gate0], axis=1),
                              jnp.concatenate([g_shift1, g_scale1, g_gate1], axis=1)], axis=0)

    g_lr, g_li, g_ldt, g_br, g_bi = _ssm_prep_bwd(lr_c, li_c, ldt_c, br_c, bi_c, s_da_re.reshape(s, 1),
                                                  s_da_im.reshape(s, 1), s_dbb_re, s_dbb_im, p, "ssm_prep_bwd")
    small = {
        "ln_pre_g": g_pre, "ln_post_g": g_post, "b_mod": g_bmod, "sgu_norm_g": g_sgu_g,
        "sgu_w": g_sgu_w.reshape(sgu_w.shape), "sgu_b": g_sgu_b.reshape(sgu_b.shape),
        "lam_re": g_lr.reshape(lam_re.shape), "lam_im": g_li.reshape(lam_im.shape),
        "b_re": g_br.reshape(b_re.shape), "b_im": g_bi.reshape(b_im.shape),
        "c_re": g_c_re.reshape(c_re.shape), "c_im": g_c_im.reshape(c_im.shape),
        "d_skip": lax.dynamic_slice(g_dskip_full, (0, chip * (w // 4)), (1, w // 4)),
        "log_dt": g_ldt.reshape(log_dt.shape),
        "b_glu": lax.dynamic_slice(g_bglu_full, (0, chip * (w // 4)), (1, w // 4)),
    }
    flat2 = lambda a: a.reshape(-1, a.shape[-1])
    wide = ("b_re", "b_im")
    for tag, group in (("adamw_small", [nm for nm in small if nm not in wide]), ("adamw_small_b", list(wide))):
        outs = _adamw_many([flat2(given[nm]) for nm in group], [flat2(small[nm]) for nm in group],
                           [flat2(given["m_" + nm]) for nm in group], [flat2(given["v_" + nm]) for nm in group], tag)
        for nm, trio in zip(group, outs):
            results[nm] = [small[nm]] + [o.reshape(given[nm].shape) for o in trio]

    rows_a = _unpack_rows(gath_a, shapes_a)
    rows_c = _unpack_rows(gath_c, shapes_c)
    dmod_rows = jnp.concatenate([rows_c[1], rows_c[2], rows_a[4], rows_a[5], rows_a[6], rows_a[7]],
                                axis=2).reshape(8, 2, 3 * d)
    dmod_shard = lax.dynamic_slice(dmod_rows, (0, 0, chip * nmod), (8, 2, nmod)).transpose(1, 0, 2)
    dmod_pad = jnp.pad(dmod_shard, ((0, 0), (0, MOD_ROWS - 8), (0, 0)))
    gr_wmod = _mod_bwd(cond_pad.T, dmod_pad, "mod_bwd")
    two_d = lambda a: a.reshape(-1, nmod)
    outs = _adamw(two_d(w_mod), [two_d(gr_wmod)], two_d(m_w_mod), two_d(v_w_mod), "adamw_w_mod")
    results["w_mod"] = [o.reshape(w_mod.shape) for o in outs]

    names = ["ln_pre_g", "ln_post_g", "w_mod", "b_mod", "w_in_ab", "w_out_ab", "sgu_norm_g", "sgu_w", "sgu_b",
             "w_in_ssm", "w_out_ssm", "lam_re", "lam_im", "b_re", "b_im", "c_re", "c_im", "d_skip", "log_dt",
             "w_glu", "b_glu"]
    return (loss, grad_x[None], *[results[nm][0] for nm in names], *[results[nm][1] for nm in names],
            *[results[nm][2] for nm in names], *[results[nm][3] for nm in names])
```
